```python
import math
import jax, jax.numpy as jnp
from jax import lax
import numpy as np

D_MODEL = 1024
BATCH = 16
SEQ = 2048
DEPTH = 2

N_EVEN = (DEPTH + 1) // 2
N_ODD = DEPTH // 2

MLA_HEADS = 8
Q_LORA = 256
KV_LORA = 128
QK_NOPE = 64
QK_ROPE = 32
V_HEAD = 64
MLA_WIDTH = MLA_HEADS * V_HEAD
ROPE_BASE = 10000.0
Q_BLOCK = 128

LRU_HEADS = 8
LRU_WIDTH = 512
LRU_BLOCK = LRU_WIDTH // LRU_HEADS
LRU_CONV = 4
LRU_C = 8.0

AB_IN = Q_LORA + KV_LORA + QK_ROPE + 2 * LRU_WIDTH
AB_MIX = MLA_WIDTH + LRU_WIDTH

CHUNK = 128
SGU_GROUPS = 8
SGU_WIDTH = D_MODEL
SGU_GROUP_DIM = SGU_WIDTH // SGU_GROUPS

D_FF = 2816
FFN_CONV = 3

NORM_EPS = 1e-6

kernel_name = "hybrid_mla_rglru_chunksgu_convffn"


def rms_norm(x, g):
    xf = x.astype(jnp.float32)
    y = xf * lax.rsqrt(jnp.mean(xf * xf, axis=-1, keepdims=True) + NORM_EPS)
    return (y * g.astype(jnp.float32)).astype(x.dtype)


def layer_norm(x, g, b):
    xf = x.astype(jnp.float32)
    mu = jnp.mean(xf, axis=-1, keepdims=True)
    xc = xf - mu
    y = xc * lax.rsqrt(jnp.mean(xc * xc, axis=-1, keepdims=True) + NORM_EPS)
    return (y * g.astype(jnp.float32) + b.astype(jnp.float32)).astype(x.dtype)


def causal_dwconv(x, w, b):
    K = w.shape[0]
    S = x.shape[1]
    xp = jnp.pad(x, ((0, 0), (K - 1, 0), (0, 0)))
    y = xp[:, 0:S] * w[0]
    for k in range(1, K):
        y = y + xp[:, k:k + S] * w[k]
    return y + b


def rope(x, positions):
    half = x.shape[-1] // 2
    inv_freq = jnp.exp(-math.log(ROPE_BASE) * jnp.arange(half, dtype=jnp.float32) / half)
    ang = positions.astype(jnp.float32)[..., None] * inv_freq
    cos = jnp.cos(ang)[:, :, None, :]
    sin = jnp.sin(ang)[:, :, None, :]
    xf = x.astype(jnp.float32)
    x1, x2 = xf[..., :half], xf[..., half:]
    return jnp.concatenate([x1 * cos - x2 * sin, x2 * cos + x1 * sin], axis=-1).astype(x.dtype)


def causal_block_attention(q, k, v):
    B, S, H, Dk = q.shape
    Dv = v.shape[-1]
    nb = S // Q_BLOCK
    scale = Dk ** -0.5
    qb = q.reshape(B, nb, Q_BLOCK, H, Dk).transpose(1, 0, 2, 3, 4)
    kpos = jnp.arange(S)

    def one_block(args):
        q_blk, blk = args
        s = jnp.einsum('bqhd,bkhd->bhqk', q_blk, k,
                       preferred_element_type=jnp.float32) * scale
        qpos = blk * Q_BLOCK + jnp.arange(Q_BLOCK)
        s = jnp.where(kpos[None, :] <= qpos[:, None], s, -jnp.inf)
        p = jax.nn.softmax(s, axis=-1)
        return jnp.einsum('bhqk,bkhd->bqhd', p.astype(v.dtype), v)

    o = lax.map(one_block, (qb, jnp.arange(nb)))
    return o.transpose(1, 0, 2, 3, 4).reshape(B, S, H * Dv)


def rg_lru(x, w_a, b_a, w_x, b_x, lam):
    B, S, C = x.shape
    xg = x.reshape(B, S, LRU_HEADS, LRU_BLOCK)
    r = jax.nn.sigmoid(jnp.einsum('bsgi,gij->bsgj', xg, w_a).reshape(B, S, C) + b_a).astype(jnp.float32)
    i = jax.nn.sigmoid(jnp.einsum('bsgi,gij->bsgj', xg, w_x).reshape(B, S, C) + b_x).astype(jnp.float32)
    log_a = -LRU_C * r * jax.nn.softplus(-lam.astype(jnp.float32))
    a = jnp.exp(log_a)
    bx = jnp.sqrt(-jnp.expm1(2.0 * log_a)) * (i * x.astype(jnp.float32))

    def combine(left, right):
        a_l, b_l = left
        a_r, b_r = right
        return a_l * a_r, a_r * b_l + b_r

    _, h = lax.associative_scan(combine, (a, bx), axis=1)
    return h.astype(x.dtype)


def mla_lru_mixer(h, positions, w_in, q_norm, w_q_b, kv_norm, w_kv_b, conv_w, conv_b,
                  w_rg_a, b_rg_a, w_rg_x, b_rg_x, lam, w_out):
    B, S, _ = h.shape
    z = h @ w_in
    o1 = Q_LORA
    o2 = o1 + KV_LORA
    o3 = o2 + QK_ROPE
    o4 = o3 + LRU_WIDTH
    c_q, c_kv, k_pe, x_lru, gate_lru = jnp.split(z, [o1, o2, o3, o4], axis=-1)

    q = (rms_norm(c_q, q_norm) @ w_q_b).reshape(B, S, MLA_HEADS, QK_NOPE + QK_ROPE)
    q = jnp.concatenate([q[..., :QK_NOPE], rope(q[..., QK_NOPE:], positions)], axis=-1)
    kv = (rms_norm(c_kv, kv_norm) @ w_kv_b).reshape(B, S, MLA_HEADS, QK_NOPE + V_HEAD)
    k_pe = jnp.broadcast_to(rope(k_pe[:, :, None, :], positions), (B, S, MLA_HEADS, QK_ROPE))
    k = jnp.concatenate([kv[..., :QK_NOPE], k_pe], axis=-1)
    v = kv[..., QK_NOPE:]
    y_mla = causal_block_attention(q, k, v)

    xc = causal_dwconv(x_lru, conv_w, conv_b)
    y_lru = rg_lru(xc, w_rg_a, b_rg_a, w_rg_x, b_rg_x, lam) * jax.nn.gelu(gate_lru)

    return jnp.concatenate([y_mla, y_lru], axis=-1) @ w_out


def chunk_sgu_mixer(h, w_in, ln_g, ln_b, w_s, b_s, w_out):
    B, S, _ = h.shape
    z = jax.nn.gelu(h @ w_in)
    u, v = jnp.split(z, 2, axis=-1)
    v = layer_norm(v, ln_g, ln_b).reshape(B, S // CHUNK, CHUNK, SGU_GROUPS, SGU_GROUP_DIM)
    causal = jnp.tril(jnp.ones((CHUNK, CHUNK), dtype=w_s.dtype))
    s = jnp.einsum('gts,bnsgc->bntgc', w_s * causal, v) + b_s.T[:, :, None]
    return (u * s.reshape(B, S, SGU_WIDTH)) @ w_out


def conv_ffn(h, w_gate, w_up, conv_w, conv_b, w_down):
    g = causal_dwconv(h @ w_gate, conv_w, conv_b)
    return (jax.nn.gelu(g) * (h @ w_up)) @ w_down


def _fwd_setup_inputs(seed: int = 0) -> dict:
    key = jax.random.key(seed)
    ks = jax.random.split(key, 32)
    f32 = jnp.float32

    def nrm(k, shape, fan_in):
        return jax.random.normal(k, shape, f32) * (fan_in ** -0.5)

    def gain(k, shape, s=0.02):
        return 1.0 + s * jax.random.normal(k, shape, f32)

    def bias(k, shape):
        return 0.02 * jax.random.normal(k, shape, f32)

    x = jax.random.normal(ks[0], (BATCH, SEQ, D_MODEL), f32)
    positions = jnp.broadcast_to(jnp.arange(SEQ, dtype=jnp.int32), (BATCH, SEQ))

    a0 = jax.random.uniform(ks[13], (N_EVEN, LRU_WIDTH), f32, minval=0.9, maxval=0.999)
    lam = jnp.log(a0) - jnp.log1p(-a0)

    return {
        "x": x,
        "positions": positions,
        "ab_norm": gain(ks[1], (N_EVEN, D_MODEL)),
        "ab_w_in": nrm(ks[2], (N_EVEN, D_MODEL, AB_IN), D_MODEL),
        "ab_q_norm": gain(ks[3], (N_EVEN, Q_LORA)),
        "ab_w_q_b": nrm(ks[4], (N_EVEN, Q_LORA, MLA_HEADS * (QK_NOPE + QK_ROPE)), Q_LORA),
        "ab_kv_norm": gain(ks[5], (N_EVEN, KV_LORA)),
        "ab_w_kv_b": nrm(ks[6], (N_EVEN, KV_LORA, MLA_HEADS * (QK_NOPE + V_HEAD)), KV_LORA),
        "ab_conv_w": nrm(ks[7], (N_EVEN, LRU_CONV, LRU_WIDTH), LRU_CONV),
        "ab_conv_b": bias(ks[8], (N_EVEN, LRU_WIDTH)),
        "ab_w_rg_a": nrm(ks[9], (N_EVEN, LRU_HEADS, LRU_BLOCK, LRU_BLOCK), LRU_BLOCK),
        "ab_b_rg_a": bias(ks[10], (N_EVEN, LRU_WIDTH)),
        "ab_w_rg_x": nrm(ks[11], (N_EVEN, LRU_HEADS, LRU_BLOCK, LRU_BLOCK), LRU_BLOCK),
        "ab_b_rg_x": bias(ks[12], (N_EVEN, LRU_WIDTH)),
        "ab_lambda": lam,
        "ab_w_out": nrm(ks[14], (N_EVEN, AB_MIX, D_MODEL), AB_MIX),
        "c_norm": gain(ks[15], (N_ODD, D_MODEL)),
        "c_w_in": nrm(ks[16], (N_ODD, D_MODEL, 2 * SGU_WIDTH), D_MODEL),
        "c_ln_g": gain(ks[17], (N_ODD, SGU_WIDTH)),
        "c_ln_b": bias(ks[18], (N_ODD, SGU_WIDTH)),
        "c_w_s": nrm(ks[19], (N_ODD, SGU_GROUPS, CHUNK, CHUNK), CHUNK),
        "c_b_s": gain(ks[20], (N_ODD, SGU_GROUPS, CHUNK), 0.1),
        "c_w_out": nrm(ks[21], (N_ODD, SGU_WIDTH, D_MODEL), SGU_WIDTH),
        "ffn_norm": gain(ks[22], (DEPTH, D_MODEL)),
        "ffn_w_gate": nrm(ks[23], (DEPTH, D_MODEL, D_FF), D_MODEL),
        "ffn_w_up": nrm(ks[24], (DEPTH, D_MODEL, D_FF), D_MODEL),
        "ffn_conv_w": nrm(ks[25], (DEPTH, FFN_CONV, D_FF), FFN_CONV),
        "ffn_conv_b": bias(ks[26], (DEPTH, D_FF)),
        "ffn_w_down": nrm(ks[27], (DEPTH, D_FF, D_MODEL), D_FF),
        "final_norm": gain(ks[28], (D_MODEL,)),
    }


def _fwd_reference(x, positions, ab_norm, ab_w_in, ab_q_norm, ab_w_q_b, ab_kv_norm, ab_w_kv_b,
              ab_conv_w, ab_conv_b, ab_w_rg_a, ab_b_rg_a, ab_w_rg_x, ab_b_rg_x, ab_lambda,
              ab_w_out, c_norm, c_w_in, c_ln_g, c_ln_b, c_w_s, c_b_s, c_w_out,
              ffn_norm, ffn_w_gate, ffn_w_up, ffn_conv_w, ffn_conv_b, ffn_w_down, final_norm):
    h = x
    for layer in range(DEPTH):
        if layer % 2 == 0:
            i = layer // 2
            h = h + mla_lru_mixer(rms_norm(h, ab_norm[i]), positions, ab_w_in[i],
                                  ab_q_norm[i], ab_w_q_b[i], ab_kv_norm[i], ab_w_kv_b[i],
                                  ab_conv_w[i], ab_conv_b[i], ab_w_rg_a[i], ab_b_rg_a[i],
                                  ab_w_rg_x[i], ab_b_rg_x[i], ab_lambda[i], ab_w_out[i])
        else:
            i = layer // 2
            h = h + chunk_sgu_mixer(rms_norm(h, c_norm[i]), c_w_in[i], c_ln_g[i], c_ln_b[i],
                                    c_w_s[i], c_b_s[i], c_w_out[i])
        h = h + conv_ffn(rms_norm(h, ffn_norm[layer]), ffn_w_gate[layer], ffn_w_up[layer],
                         ffn_conv_w[layer], ffn_conv_b[layer], ffn_w_down[layer])
    return rms_norm(h, final_norm)


import jax as _jax
import jax.numpy as _jnp

TWIN_FORMAT = 'train_step'
FWD_PARAMS = ['x', 'positions', 'ab_norm', 'ab_w_in', 'ab_q_norm', 'ab_w_q_b', 'ab_kv_norm', 'ab_w_kv_b', 'ab_conv_w', 'ab_conv_b', 'ab_w_rg_a', 'ab_b_rg_a', 'ab_w_rg_x', 'ab_b_rg_x', 'ab_lambda', 'ab_w_out', 'c_norm', 'c_w_in', 'c_ln_g', 'c_ln_b', 'c_w_s', 'c_b_s', 'c_w_out', 'ffn_norm', 'ffn_w_gate', 'ffn_w_up', 'ffn_conv_w', 'ffn_conv_b', 'ffn_w_down', 'final_norm']
TWIN_WEIGHTS = ['ab_norm', 'ab_w_in', 'ab_q_norm', 'ab_w_q_b', 'ab_kv_norm', 'ab_w_kv_b', 'ab_conv_w', 'ab_conv_b', 'ab_w_rg_a', 'ab_b_rg_a', 'ab_w_rg_x', 'ab_b_rg_x', 'ab_lambda', 'ab_w_out', 'c_norm', 'c_w_in', 'c_ln_g', 'c_ln_b', 'c_w_s', 'c_b_s', 'c_w_out', 'ffn_norm', 'ffn_w_gate', 'ffn_w_up', 'ffn_conv_w', 'ffn_conv_b', 'ffn_w_down', 'final_norm']
TWIN_DIFF_INPUT = 'x'
TWIN_INPUTS = ['x', 'positions', 'ab_norm', 'ab_w_in', 'ab_q_norm', 'ab_w_q_b', 'ab_kv_norm', 'ab_w_kv_b', 'ab_conv_w', 'ab_conv_b', 'ab_w_rg_a', 'ab_b_rg_a', 'ab_w_rg_x', 'ab_b_rg_x', 'ab_lambda', 'ab_w_out', 'c_norm', 'c_w_in', 'c_ln_g', 'c_ln_b', 'c_w_s', 'c_b_s', 'c_w_out', 'ffn_norm', 'ffn_w_gate', 'ffn_w_up', 'ffn_conv_w', 'ffn_conv_b', 'ffn_w_down', 'final_norm', 'loss_target', 'm_ab_norm', 'm_ab_w_in', 'm_ab_q_norm', 'm_ab_w_q_b', 'm_ab_kv_norm', 'm_ab_w_kv_b', 'm_ab_conv_w', 'm_ab_conv_b', 'm_ab_w_rg_a', 'm_ab_b_rg_a', 'm_ab_w_rg_x', 'm_ab_b_rg_x', 'm_ab_lambda', 'm_ab_w_out', 'm_c_norm', 'm_c_w_in', 'm_c_ln_g', 'm_c_ln_b', 'm_c_w_s', 'm_c_b_s', 'm_c_w_out', 'm_ffn_norm', 'm_ffn_w_gate', 'm_ffn_w_up', 'm_ffn_conv_w', 'm_ffn_conv_b', 'm_ffn_w_down', 'm_final_norm', 'v_ab_norm', 'v_ab_w_in', 'v_ab_q_norm', 'v_ab_w_q_b', 'v_ab_kv_norm', 'v_ab_w_kv_b', 'v_ab_conv_w', 'v_ab_conv_b', 'v_ab_w_rg_a', 'v_ab_b_rg_a', 'v_ab_w_rg_x', 'v_ab_b_rg_x', 'v_ab_lambda', 'v_ab_w_out', 'v_c_norm', 'v_c_w_in', 'v_c_ln_g', 'v_c_ln_b', 'v_c_w_s', 'v_c_b_s', 'v_c_w_out', 'v_ffn_norm', 'v_ffn_w_gate', 'v_ffn_w_up', 'v_ffn_conv_w', 'v_ffn_conv_b', 'v_ffn_w_down', 'v_final_norm']
TWIN_OUTPUTS = ['loss', 'grad_x', 'grad_ab_norm', 'grad_ab_w_in', 'grad_ab_q_norm', 'grad_ab_w_q_b', 'grad_ab_kv_norm', 'grad_ab_w_kv_b', 'grad_ab_conv_w', 'grad_ab_conv_b', 'grad_ab_w_rg_a', 'grad_ab_b_rg_a', 'grad_ab_w_rg_x', 'grad_ab_b_rg_x', 'grad_ab_lambda', 'grad_ab_w_out', 'grad_c_norm', 'grad_c_w_in', 'grad_c_ln_g', 'grad_c_ln_b', 'grad_c_w_s', 'grad_c_b_s', 'grad_c_w_out', 'grad_ffn_norm', 'grad_ffn_w_gate', 'grad_ffn_w_up', 'grad_ffn_conv_w', 'grad_ffn_conv_b', 'grad_ffn_w_down', 'grad_final_norm', 'delta_ab_norm', 'delta_ab_w_in', 'delta_ab_q_norm', 'delta_ab_w_q_b', 'delta_ab_kv_norm', 'delta_ab_w_kv_b', 'delta_ab_conv_w', 'delta_ab_conv_b', 'delta_ab_w_rg_a', 'delta_ab_b_rg_a', 'delta_ab_w_rg_x', 'delta_ab_b_rg_x', 'delta_ab_lambda', 'delta_ab_w_out', 'delta_c_norm', 'delta_c_w_in', 'delta_c_ln_g', 'delta_c_ln_b', 'delta_c_w_s', 'delta_c_b_s', 'delta_c_w_out', 'delta_ffn_norm', 'delta_ffn_w_gate', 'delta_ffn_w_up', 'delta_ffn_conv_w', 'delta_ffn_conv_b', 'delta_ffn_w_down', 'delta_final_norm', 'new_m_ab_norm', 'new_m_ab_w_in', 'new_m_ab_q_norm', 'new_m_ab_w_q_b', 'new_m_ab_kv_norm', 'new_m_ab_w_kv_b', 'new_m_ab_conv_w', 'new_m_ab_conv_b', 'new_m_ab_w_rg_a', 'new_m_ab_b_rg_a', 'new_m_ab_w_rg_x', 'new_m_ab_b_rg_x', 'new_m_ab_lambda', 'new_m_ab_w_out', 'new_m_c_norm', 'new_m_c_w_in', 'new_m_c_ln_g', 'new_m_c_ln_b', 'new_m_c_w_s', 'new_m_c_b_s', 'new_m_c_w_out', 'new_m_ffn_norm', 'new_m_ffn_w_gate', 'new_m_ffn_w_up', 'new_m_ffn_conv_w', 'new_m_ffn_conv_b', 'new_m_ffn_w_down', 'new_m_final_norm', 'new_v_ab_norm', 'new_v_ab_w_in', 'new_v_ab_q_norm', 'new_v_ab_w_q_b', 'new_v_ab_kv_norm', 'new_v_ab_w_kv_b', 'new_v_ab_conv_w', 'new_v_ab_conv_b', 'new_v_ab_w_rg_a', 'new_v_ab_b_rg_a', 'new_v_ab_w_rg_x', 'new_v_ab_b_rg_x', 'new_v_ab_lambda', 'new_v_ab_w_out', 'new_v_c_norm', 'new_v_c_w_in', 'new_v_c_ln_g', 'new_v_c_ln_b', 'new_v_c_w_s', 'new_v_c_b_s', 'new_v_c_w_out', 'new_v_ffn_norm', 'new_v_ffn_w_gate', 'new_v_ffn_w_up', 'new_v_ffn_conv_w', 'new_v_ffn_conv_b', 'new_v_ffn_w_down', 'new_v_final_norm']
TWIN_LEAF_KINDS = {'loss': 'loss', 'grad_x': 'grad_x', 'grad_ab_norm': 'grad_w', 'grad_ab_w_in': 'grad_w', 'grad_ab_q_norm': 'grad_w', 'grad_ab_w_q_b': 'grad_w', 'grad_ab_kv_norm': 'grad_w', 'grad_ab_w_kv_b': 'grad_w', 'grad_ab_conv_w': 'grad_w', 'grad_ab_conv_b': 'grad_w', 'grad_ab_w_rg_a': 'grad_w', 'grad_ab_b_rg_a': 'grad_w', 'grad_ab_w_rg_x': 'grad_w', 'grad_ab_b_rg_x': 'grad_w', 'grad_ab_lambda': 'grad_w', 'grad_ab_w_out': 'grad_w', 'grad_c_norm': 'grad_w', 'grad_c_w_in': 'grad_w', 'grad_c_ln_g': 'grad_w', 'grad_c_ln_b': 'grad_w', 'grad_c_w_s': 'grad_w', 'grad_c_b_s': 'grad_w', 'grad_c_w_out': 'grad_w', 'grad_ffn_norm': 'grad_w', 'grad_ffn_w_gate': 'grad_w', 'grad_ffn_w_up': 'grad_w', 'grad_ffn_conv_w': 'grad_w', 'grad_ffn_conv_b': 'grad_w', 'grad_ffn_w_down': 'grad_w', 'grad_final_norm': 'grad_w', 'delta_ab_norm': 'delta_w', 'delta_ab_w_in': 'delta_w', 'delta_ab_q_norm': 'delta_w', 'delta_ab_w_q_b': 'delta_w', 'delta_ab_kv_norm': 'delta_w', 'delta_ab_w_kv_b': 'delta_w', 'delta_ab_conv_w': 'delta_w', 'delta_ab_conv_b': 'delta_w', 'delta_ab_w_rg_a': 'delta_w', 'delta_ab_b_rg_a': 'delta_w', 'delta_ab_w_rg_x': 'delta_w', 'delta_ab_b_rg_x': 'delta_w', 'delta_ab_lambda': 'delta_w', 'delta_ab_w_out': 'delta_w', 'delta_c_norm': 'delta_w', 'delta_c_w_in': 'delta_w', 'delta_c_ln_g': 'delta_w', 'delta_c_ln_b': 'delta_w', 'delta_c_w_s': 'delta_w', 'delta_c_b_s': 'delta_w', 'delta_c_w_out': 'delta_w', 'delta_ffn_norm': 'delta_w', 'delta_ffn_w_gate': 'delta_w', 'delta_ffn_w_up': 'delta_w', 'delta_ffn_conv_w': 'delta_w', 'delta_ffn_conv_b': 'delta_w', 'delta_ffn_w_down': 'delta_w', 'delta_final_norm': 'delta_w', 'new_m_ab_norm': 'new_m', 'new_m_ab_w_in': 'new_m', 'new_m_ab_q_norm': 'new_m', 'new_m_ab_w_q_b': 'new_m', 'new_m_ab_kv_norm': 'new_m', 'new_m_ab_w_kv_b': 'new_m', 'new_m_ab_conv_w': 'new_m', 'new_m_ab_conv_b': 'new_m', 'new_m_ab_w_rg_a': 'new_m', 'new_m_ab_b_rg_a': 'new_m', 'new_m_ab_w_rg_x': 'new_m', 'new_m_ab_b_rg_x': 'new_m', 'new_m_ab_lambda': 'new_m', 'new_m_ab_w_out': 'new_m', 'new_m_c_norm': 'new_m', 'new_m_c_w_in': 'new_m', 'new_m_c_ln_g': 'new_m', 'new_m_c_ln_b': 'new_m', 'new_m_c_w_s': 'new_m', 'new_m_c_b_s': 'new_m', 'new_m_c_w_out': 'new_m', 'new_m_ffn_norm': 'new_m', 'new_m_ffn_w_gate': 'new_m', 'new_m_ffn_w_up': 'new_m', 'new_m_ffn_conv_w': 'new_m', 'new_m_ffn_conv_b': 'new_m', 'new_m_ffn_w_down': 'new_m', 'new_m_final_norm': 'new_m', 'new_v_ab_norm': 'new_v', 'new_v_ab_w_in': 'new_v', 'new_v_ab_q_norm': 'new_v', 'new_v_ab_w_q_b': 'new_v', 'new_v_ab_kv_norm': 'new_v', 'new_v_ab_w_kv_b': 'new_v', 'new_v_ab_conv_w': 'new_v', 'new_v_ab_conv_b': 'new_v', 'new_v_ab_w_rg_a': 'new_v', 'new_v_ab_b_rg_a': 'new_v', 'new_v_ab_w_rg_x': 'new_v', 'new_v_ab_b_rg_x': 'new_v', 'new_v_ab_lambda': 'new_v', 'new_v_ab_w_out': 'new_v', 'new_v_c_norm': 'new_v', 'new_v_c_w_in': 'new_v', 'new_v_c_ln_g': 'new_v', 'new_v_c_ln_b': 'new_v', 'new_v_c_w_s': 'new_v', 'new_v_c_b_s': 'new_v', 'new_v_c_w_out': 'new_v', 'new_v_ffn_norm': 'new_v', 'new_v_ffn_w_gate': 'new_v', 'new_v_ffn_w_up': 'new_v', 'new_v_ffn_conv_w': 'new_v', 'new_v_ffn_conv_b': 'new_v', 'new_v_ffn_w_down': 'new_v', 'new_v_final_norm': 'new_v'}


def _forward(args):
    return _fwd_reference(*[args[k] for k in FWD_PARAMS])


def _output_shape():
    out = _jax.eval_shape(lambda: _forward(_fwd_setup_inputs(0)))
    return out.shape, out.dtype

N_MICROBATCH = 1
ADAM_LR = 0.001
ADAM_B1 = 0.9
ADAM_B2 = 0.999
ADAM_EPS = 1e-08
ADAM_WD = 0.01
ADAM_STEP = 10
PER_EXAMPLE_BATCH_AXIS = {'x': 0, 'positions': 0, 'loss_target': 0}
SHARED_INPUTS = []
_WEIGHT_DTYPES = {'ab_norm': _jnp.float32, 'ab_w_in': _jnp.float32, 'ab_q_norm': _jnp.float32, 'ab_w_q_b': _jnp.float32, 'ab_kv_norm': _jnp.float32, 'ab_w_kv_b': _jnp.float32, 'ab_conv_w': _jnp.float32, 'ab_conv_b': _jnp.float32, 'ab_w_rg_a': _jnp.float32, 'ab_b_rg_a': _jnp.float32, 'ab_w_rg_x': _jnp.float32, 'ab_b_rg_x': _jnp.float32, 'ab_lambda': _jnp.float32, 'ab_w_out': _jnp.float32, 'c_norm': _jnp.float32, 'c_w_in': _jnp.float32, 'c_ln_g': _jnp.float32, 'c_ln_b': _jnp.float32, 'c_w_s': _jnp.float32, 'c_b_s': _jnp.float32, 'c_w_out': _jnp.float32, 'ffn_norm': _jnp.float32, 'ffn_w_gate': _jnp.float32, 'ffn_w_up': _jnp.float32, 'ffn_conv_w': _jnp.float32, 'ffn_conv_b': _jnp.float32, 'ffn_w_down': _jnp.float32, 'final_norm': _jnp.float32}
MOMENT_SCALE = {'ab_norm': 9.797813e-02, 'ab_w_in': 8.428778e-02, 'ab_q_norm': 6.028690e-02, 'ab_w_q_b': 3.474157e-02, 'ab_kv_norm': 1.214350e-01, 'ab_w_kv_b': 4.371492e-02, 'ab_conv_w': 1.008263e-01, 'ab_conv_b': 4.013944e-01, 'ab_w_rg_a': 2.474211e-02, 'ab_b_rg_a': 2.421452e-02, 'ab_w_rg_x': 4.388937e-02, 'ab_b_rg_x': 3.368052e-02, 'ab_lambda': 4.488746e-02, 'ab_w_out': 6.819967e-02, 'c_norm': 1.256300e-01, 'c_w_in': 9.069280e-02, 'c_ln_g': 6.558131e-02, 'c_ln_b': 6.036042e-02, 'c_w_s': 5.963829e-02, 'c_b_s': 8.738651e-02, 'c_w_out': 1.065442e-01, 'ffn_norm': 1.393724e-01, 'ffn_w_gate': 5.864868e-02, 'ffn_w_up': 5.706016e-02, 'ffn_conv_w': 5.912499e-02, 'ffn_conv_b': 5.662007e-02, 'ffn_w_down': 9.438073e-02, 'final_norm': 3.206818e+01}


def _to_microbatches(a, axis):
    t = _jnp.moveaxis(a, axis, 0)
    t = t.reshape((N_MICROBATCH, t.shape[0] // N_MICROBATCH) + t.shape[1:])
    return _jnp.moveaxis(t, 1, axis + 1)


def setup_inputs(seed: int = 0) -> dict:
    inp = _fwd_setup_inputs(seed)
    key = _jax.random.fold_in(_jax.random.key(seed), 7919)
    shape, _ = _output_shape()
    out = dict(inp)
    out["loss_target"] = _jax.random.normal(_jax.random.fold_in(key, 0), shape, _jnp.float32)
    for i, name in enumerate(TWIN_WEIGHTS):
        w = inp[name].astype(_jnp.float32)
        if MOMENT_SCALE is None:
            s = _jnp.sqrt(_jnp.mean(_jnp.square(w)) + 1e-30)
        else:
            s = MOMENT_SCALE[name]
        km, kv = _jax.random.split(_jax.random.fold_in(key, i + 1))
        out[name] = w
        out["m_" + name] = s * _jax.random.normal(km, w.shape, _jnp.float32)
        out["v_" + name] = (s * s) * _jax.random.uniform(kv, w.shape, _jnp.float32, 0.5, 1.5)
    if N_MICROBATCH > 1:
        for name, axis in PER_EXAMPLE_BATCH_AXIS.items():
            out[name] = _to_microbatches(out[name], axis)
    return {'x': out['x'], 'positions': out['positions'], 'ab_norm': out['ab_norm'], 'ab_w_in': out['ab_w_in'], 'ab_q_norm': out['ab_q_norm'], 'ab_w_q_b': out['ab_w_q_b'], 'ab_kv_norm': out['ab_kv_norm'], 'ab_w_kv_b': out['ab_w_kv_b'], 'ab_conv_w': out['ab_conv_w'], 'ab_conv_b': out['ab_conv_b'], 'ab_w_rg_a': out['ab_w_rg_a'], 'ab_b_rg_a': out['ab_b_rg_a'], 'ab_w_rg_x': out['ab_w_rg_x'], 'ab_b_rg_x': out['ab_b_rg_x'], 'ab_lambda': out['ab_lambda'], 'ab_w_out': out['ab_w_out'], 'c_norm': out['c_norm'], 'c_w_in': out['c_w_in'], 'c_ln_g': out['c_ln_g'], 'c_ln_b': out['c_ln_b'], 'c_w_s': out['c_w_s'], 'c_b_s': out['c_b_s'], 'c_w_out': out['c_w_out'], 'ffn_norm': out['ffn_norm'], 'ffn_w_gate': out['ffn_w_gate'], 'ffn_w_up': out['ffn_w_up'], 'ffn_conv_w': out['ffn_conv_w'], 'ffn_conv_b': out['ffn_conv_b'], 'ffn_w_down': out['ffn_w_down'], 'final_norm': out['final_norm'], 'loss_target': out['loss_target'], 'm_ab_norm': out['m_ab_norm'], 'm_ab_w_in': out['m_ab_w_in'], 'm_ab_q_norm': out['m_ab_q_norm'], 'm_ab_w_q_b': out['m_ab_w_q_b'], 'm_ab_kv_norm': out['m_ab_kv_norm'], 'm_ab_w_kv_b': out['m_ab_w_kv_b'], 'm_ab_conv_w': out['m_ab_conv_w'], 'm_ab_conv_b': out['m_ab_conv_b'], 'm_ab_w_rg_a': out['m_ab_w_rg_a'], 'm_ab_b_rg_a': out['m_ab_b_rg_a'], 'm_ab_w_rg_x': out['m_ab_w_rg_x'], 'm_ab_b_rg_x': out['m_ab_b_rg_x'], 'm_ab_lambda': out['m_ab_lambda'], 'm_ab_w_out': out['m_ab_w_out'], 'm_c_norm': out['m_c_norm'], 'm_c_w_in': out['m_c_w_in'], 'm_c_ln_g': out['m_c_ln_g'], 'm_c_ln_b': out['m_c_ln_b'], 'm_c_w_s': out['m_c_w_s'], 'm_c_b_s': out['m_c_b_s'], 'm_c_w_out': out['m_c_w_out'], 'm_ffn_norm': out['m_ffn_norm'], 'm_ffn_w_gate': out['m_ffn_w_gate'], 'm_ffn_w_up': out['m_ffn_w_up'], 'm_ffn_conv_w': out['m_ffn_conv_w'], 'm_ffn_conv_b': out['m_ffn_conv_b'], 'm_ffn_w_down': out['m_ffn_w_down'], 'm_final_norm': out['m_final_norm'], 'v_ab_norm': out['v_ab_norm'], 'v_ab_w_in': out['v_ab_w_in'], 'v_ab_q_norm': out['v_ab_q_norm'], 'v_ab_w_q_b': out['v_ab_w_q_b'], 'v_ab_kv_norm': out['v_ab_kv_norm'], 'v_ab_w_kv_b': out['v_ab_w_kv_b'], 'v_ab_conv_w': out['v_ab_conv_w'], 'v_ab_conv_b': out['v_ab_conv_b'], 'v_ab_w_rg_a': out['v_ab_w_rg_a'], 'v_ab_b_rg_a': out['v_ab_b_rg_a'], 'v_ab_w_rg_x': out['v_ab_w_rg_x'], 'v_ab_b_rg_x': out['v_ab_b_rg_x'], 'v_ab_lambda': out['v_ab_lambda'], 'v_ab_w_out': out['v_ab_w_out'], 'v_c_norm': out['v_c_norm'], 'v_c_w_in': out['v_c_w_in'], 'v_c_ln_g': out['v_c_ln_g'], 'v_c_ln_b': out['v_c_ln_b'], 'v_c_w_s': out['v_c_w_s'], 'v_c_b_s': out['v_c_b_s'], 'v_c_w_out': out['v_c_w_out'], 'v_ffn_norm': out['v_ffn_norm'], 'v_ffn_w_gate': out['v_ffn_w_gate'], 'v_ffn_w_up': out['v_ffn_w_up'], 'v_ffn_conv_w': out['v_ffn_conv_w'], 'v_ffn_conv_b': out['v_ffn_conv_b'], 'v_ffn_w_down': out['v_ffn_w_down'], 'v_final_norm': out['v_final_norm']}


def _loss(weights, diff, rest, loss_target):
    with _jax.named_scope("forward"):
        args = {**rest, TWIN_DIFF_INPUT: diff, **{k: w.astype(_WEIGHT_DTYPES[k]) for k, w in weights.items()}}
        y = _forward(args)
    with _jax.named_scope("loss_head"):
        err = _jnp.square(y.astype(_jnp.float32) - loss_target)
        return 0.5 * _jnp.sum(_jnp.mean(err, axis=-1)) if err.ndim else 0.5 * err


def _adamw(w, g, m, v):
    m = ADAM_B1 * m + (1.0 - ADAM_B1) * g
    v = ADAM_B2 * v + (1.0 - ADAM_B2) * _jnp.square(g)
    m_hat = m / (1.0 - ADAM_B1 ** ADAM_STEP)
    v_hat = v / (1.0 - ADAM_B2 ** ADAM_STEP)
    delta = -ADAM_LR * (m_hat / (_jnp.sqrt(v_hat) + ADAM_EPS) + ADAM_WD * w)
    return delta, m, v


def reference(x, positions, ab_norm, ab_w_in, ab_q_norm, ab_w_q_b, ab_kv_norm, ab_w_kv_b, ab_conv_w, ab_conv_b, ab_w_rg_a, ab_b_rg_a, ab_w_rg_x, ab_b_rg_x, ab_lambda, ab_w_out, c_norm, c_w_in, c_ln_g, c_ln_b, c_w_s, c_b_s, c_w_out, ffn_norm, ffn_w_gate, ffn_w_up, ffn_conv_w, ffn_conv_b, ffn_w_down, final_norm, loss_target, m_ab_norm, m_ab_w_in, m_ab_q_norm, m_ab_w_q_b, m_ab_kv_norm, m_ab_w_kv_b, m_ab_conv_w, m_ab_conv_b, m_ab_w_rg_a, m_ab_b_rg_a, m_ab_w_rg_x, m_ab_b_rg_x, m_ab_lambda, m_ab_w_out, m_c_norm, m_c_w_in, m_c_ln_g, m_c_ln_b, m_c_w_s, m_c_b_s, m_c_w_out, m_ffn_norm, m_ffn_w_gate, m_ffn_w_up, m_ffn_conv_w, m_ffn_conv_b, m_ffn_w_down, m_final_norm, v_ab_norm, v_ab_w_in, v_ab_q_norm, v_ab_w_q_b, v_ab_kv_norm, v_ab_w_kv_b, v_ab_conv_w, v_ab_conv_b, v_ab_w_rg_a, v_ab_b_rg_a, v_ab_w_rg_x, v_ab_b_rg_x, v_ab_lambda, v_ab_w_out, v_c_norm, v_c_w_in, v_c_ln_g, v_c_ln_b, v_c_w_s, v_c_b_s, v_c_w_out, v_ffn_norm, v_ffn_w_gate, v_ffn_w_up, v_ffn_conv_w, v_ffn_conv_b, v_ffn_w_down, v_final_norm):
    given = dict(x=x, positions=positions, ab_norm=ab_norm, ab_w_in=ab_w_in, ab_q_norm=ab_q_norm, ab_w_q_b=ab_w_q_b, ab_kv_norm=ab_kv_norm, ab_w_kv_b=ab_w_kv_b, ab_conv_w=ab_conv_w, ab_conv_b=ab_conv_b, ab_w_rg_a=ab_w_rg_a, ab_b_rg_a=ab_b_rg_a, ab_w_rg_x=ab_w_rg_x, ab_b_rg_x=ab_b_rg_x, ab_lambda=ab_lambda, ab_w_out=ab_w_out, c_norm=c_norm, c_w_in=c_w_in, c_ln_g=c_ln_g, c_ln_b=c_ln_b, c_w_s=c_w_s, c_b_s=c_b_s, c_w_out=c_w_out, ffn_norm=ffn_norm, ffn_w_gate=ffn_w_gate, ffn_w_up=ffn_w_up, ffn_conv_w=ffn_conv_w, ffn_conv_b=ffn_conv_b, ffn_w_down=ffn_w_down, final_norm=final_norm, loss_target=loss_target, m_ab_norm=m_ab_norm, m_ab_w_in=m_ab_w_in, m_ab_q_norm=m_ab_q_norm, m_ab_w_q_b=m_ab_w_q_b, m_ab_kv_norm=m_ab_kv_norm, m_ab_w_kv_b=m_ab_w_kv_b, m_ab_conv_w=m_ab_conv_w, m_ab_conv_b=m_ab_conv_b, m_ab_w_rg_a=m_ab_w_rg_a, m_ab_b_rg_a=m_ab_b_rg_a, m_ab_w_rg_x=m_ab_w_rg_x, m_ab_b_rg_x=m_ab_b_rg_x, m_ab_lambda=m_ab_lambda, m_ab_w_out=m_ab_w_out, m_c_norm=m_c_norm, m_c_w_in=m_c_w_in, m_c_ln_g=m_c_ln_g, m_c_ln_b=m_c_ln_b, m_c_w_s=m_c_w_s, m_c_b_s=m_c_b_s, m_c_w_out=m_c_w_out, m_ffn_norm=m_ffn_norm, m_ffn_w_gate=m_ffn_w_gate, m_ffn_w_up=m_ffn_w_up, m_ffn_conv_w=m_ffn_conv_w, m_ffn_conv_b=m_ffn_conv_b, m_ffn_w_down=m_ffn_w_down, m_final_norm=m_final_norm, v_ab_norm=v_ab_norm, v_ab_w_in=v_ab_w_in, v_ab_q_norm=v_ab_q_norm, v_ab_w_q_b=v_ab_w_q_b, v_ab_kv_norm=v_ab_kv_norm, v_ab_w_kv_b=v_ab_w_kv_b, v_ab_conv_w=v_ab_conv_w, v_ab_conv_b=v_ab_conv_b, v_ab_w_rg_a=v_ab_w_rg_a, v_ab_b_rg_a=v_ab_b_rg_a, v_ab_w_rg_x=v_ab_w_rg_x, v_ab_b_rg_x=v_ab_b_rg_x, v_ab_lambda=v_ab_lambda, v_ab_w_out=v_ab_w_out, v_c_norm=v_c_norm, v_c_w_in=v_c_w_in, v_c_ln_g=v_c_ln_g, v_c_ln_b=v_c_ln_b, v_c_w_s=v_c_w_s, v_c_b_s=v_c_b_s, v_c_w_out=v_c_w_out, v_ffn_norm=v_ffn_norm, v_ffn_w_gate=v_ffn_w_gate, v_ffn_w_up=v_ffn_w_up, v_ffn_conv_w=v_ffn_conv_w, v_ffn_conv_b=v_ffn_conv_b, v_ffn_w_down=v_ffn_w_down, v_final_norm=v_final_norm)
    weights = {n: given[n] for n in TWIN_WEIGHTS}
    shared = {n: given[n] for n in SHARED_INPUTS}
    per_example = {n: given[n] for n in ['x', 'positions']}
    grad_fn = _jax.value_and_grad(_loss, argnums=(0, 1))

    def one_microbatch(ex, loss_target):
        ex = dict(ex)
        diff = ex.pop(TWIN_DIFF_INPUT)
        return grad_fn(weights, diff, {**shared, **ex}, loss_target)

    if N_MICROBATCH == 1:
        loss, (grad_w, grad_x) = one_microbatch(per_example, given["loss_target"])
    else:
        def body(carry, xs):
            loss_sum, grad_sum = carry
            l_k, (gw_k, gx_k) = one_microbatch(xs[0], xs[1])
            with _jax.named_scope("update"):
                return (loss_sum + l_k, _jax.tree.map(_jnp.add, grad_sum, gw_k)), gx_k

        init = (_jnp.zeros((), _jnp.float32), _jax.tree.map(_jnp.zeros_like, weights))
        (loss, grad_w), grad_x = _jax.lax.scan(body, init, (per_example, given["loss_target"]))
    with _jax.named_scope("update"):
        delta_w, new_m, new_v = {}, {}, {}
        for n in TWIN_WEIGHTS:
            delta_w[n], new_m[n], new_v[n] = _adamw(weights[n], grad_w[n], given["m_" + n], given["v_" + n])
    return (loss, grad_x, *[grad_w[n] for n in TWIN_WEIGHTS], *[delta_w[n] for n in TWIN_WEIGHTS],
            *[new_m[n] for n in TWIN_WEIGHTS], *[new_v[n] for n in TWIN_WEIGHTS])
```

```python
import functools
import math

import jax
import jax.numpy as jnp
from jax import lax
from jax.experimental import pallas as pl
from jax.experimental.pallas import tpu as pltpu

F32 = jnp.float32
BF16 = jnp.bfloat16
MESH = pl.DeviceIdType.MESH

D_MODEL = 1024
MLA_HEADS = 8
Q_LORA = 256
KV_LORA = 128
QK_NOPE = 64
QK_ROPE = 32
V_HEAD = 64
LRU_WIDTH = 512
LRU_HEADS = 8
LRU_BLOCK = 64
LRU_CONV = 4
LRU_C = 8.0
CHUNK = 128
SGU_GROUPS = 8
SGU_WIDTH = 1024
D_FF = 2816
FFN_CONV = 3
NORM_EPS = 1e-6
ROPE_BASE = 10000.0
AB_IN_PAD = 1536
ADAM_LR = 0.001
ADAM_B1 = 0.9
ADAM_B2 = 0.999
ADAM_EPS = 1e-08
ADAM_WD = 0.01
ADAM_STEP = 10

N_CHIPS = 4
LANES = 128
VMEM_LIMIT = 56 * 1024 * 1024
ROW_TILE = 256
GELU_C = math.sqrt(2.0 / math.pi)


def _cparams(sem):
    return pltpu.CompilerParams(dimension_semantics=sem, vmem_limit_bytes=VMEM_LIMIT)


def _tile(n, target, mult=LANES):
    t = (min(n, target) // mult) * mult
    while t >= mult:
        if n % t == 0:
            return t
        t -= mult
    return n


def _gelu(x):
    t = jnp.tanh(GELU_C * (x + 0.044715 * x * x * x))
    return 0.5 * x * (1.0 + t)


def _gelu_and_grad(x):
    x2 = x * x
    t = jnp.tanh(GELU_C * (x + 0.044715 * x * x2))
    g = 0.5 * x * (1.0 + t)
    dg = 0.5 * (1.0 + t) + 0.5 * x * (1.0 - t * t) * GELU_C * (1.0 + 3.0 * 0.044715 * x2)
    return g, dg


def _sigmoid(x):
    return 1.0 / (1.0 + jnp.exp(-x))


def _shift_rows(x, d, fill_rows):
    ext = jnp.concatenate([fill_rows, x], axis=0)
    return pltpu.roll(ext, d, 0)[8:]


def _shift_rows_up(x, d, fill_rows):
    n = x.shape[0]
    ext = jnp.concatenate([x, fill_rows], axis=0)
    return pltpu.roll(ext, n + 8 - d, 0)[:n]


def _dot(a, b, dims):
    return lax.dot_general(a.astype(BF16), b.astype(BF16), (dims, ((), ())), preferred_element_type=F32)


def _dot_nn(a, b):
    return _dot(a, b, ((1,), (0,)))


def _dot_nt(a, b):
    return _dot(a, b, ((1,), (1,)))


def _dot_tn(a, b):
    return _dot(a, b, ((0,), (0,)))


def _mm(a, b, *, name, ta=False, tb=False, res=None, out_dtype=F32, tm=512, tn=1024, tk=1024):
    if ta:
        K, M = a.shape
    else:
        M, K = a.shape
    N = b.shape[0] if tb else b.shape[1]
    tm = _tile(M, tm, LANES if ta else 8)
    tn = _tile(N, tn, LANES)
    tk = _tile(K, tk, LANES)
    nk = K // tk
    a_spec = pl.BlockSpec((tk, tm), lambda i, j, k: (k, i)) if ta else pl.BlockSpec((tm, tk), lambda i, j, k: (i, k))
    b_spec = pl.BlockSpec((tn, tk), lambda i, j, k: (j, k)) if tb else pl.BlockSpec((tk, tn), lambda i, j, k: (k, j))
    o_spec = pl.BlockSpec((tm, tn), lambda i, j, k: (i, j))
    dims = ((0,) if ta else (1,), (1,) if tb else (0,))
    has_res = res is not None

    def body(*refs):
        if has_res:
            a_ref, b_ref, r_ref, o_ref, acc_ref = refs
        else:
            a_ref, b_ref, o_ref, acc_ref = refs
            r_ref = None
        k = pl.program_id(2)
        p = _dot(a_ref[...], b_ref[...], dims)

        @pl.when(k == 0)
        def _():
            acc_ref[...] = p

        @pl.when(k > 0)
        def _():
            acc_ref[...] += p

        @pl.when(k == nk - 1)
        def _():
            r = acc_ref[...]
            if has_res:
                r = r + r_ref[...].astype(F32)
            o_ref[...] = r.astype(out_dtype)

    in_specs = [a_spec, b_spec] + ([o_spec] if has_res else [])
    args = (a, b) + ((res,) if has_res else ())
    return pl.pallas_call(
        body, name=name, grid=(M // tm, N // tn, nk), in_specs=in_specs, out_specs=o_spec,
        out_shape=jax.ShapeDtypeStruct((M, N), out_dtype),
        scratch_shapes=[pltpu.VMEM((tm, tn), F32)],
        compiler_params=_cparams(("parallel", "parallel", "arbitrary")),
    )(*args)


def _rms_fwd(x, g, *, name, cb=0, out_dtype=BF16):
    T = x.shape[0]
    W = g.shape[-1]
    g = g.reshape(1, W)
    tt = ROW_TILE

    def body(x_ref, g_ref, o_ref):
        xf = x_ref[...].astype(F32)
        rstd = lax.rsqrt(jnp.mean(xf * xf, axis=-1, keepdims=True) + NORM_EPS)
        o_ref[...] = (xf * rstd * g_ref[...]).astype(out_dtype)

    return pl.pallas_call(
        body, name=name, grid=(T // tt,),
        in_specs=[pl.BlockSpec((tt, W), lambda i: (i, cb)), pl.BlockSpec((1, W), lambda i: (0, 0))],
        out_specs=pl.BlockSpec((tt, W), lambda i: (i, 0)),
        out_shape=jax.ShapeDtypeStruct((T, W), out_dtype),
        compiler_params=_cparams(("parallel",)),
    )(x, g)


def _rms_bwd(x, g, dy, *, name, cb=0, res=None, out_dtype=F32):
    T = x.shape[0]
    W = g.shape[-1]
    g = g.reshape(1, W)
    tt = ROW_TILE
    has_res = res is not None

    def body(*refs):
        if has_res:
            x_ref, g_ref, dy_ref, r_ref, dx_ref, dg_ref = refs
        else:
            x_ref, g_ref, dy_ref, dx_ref, dg_ref = refs
        xf = x_ref[...].astype(F32)
        dyf = dy_ref[...].astype(F32)
        rstd = lax.rsqrt(jnp.mean(xf * xf, axis=-1, keepdims=True) + NORM_EPS)
        xhat = xf * rstd
        dxhat = dyf * g_ref[...]
        dx = rstd * (dxhat - xhat * jnp.mean(dxhat * xhat, axis=-1, keepdims=True))
        if has_res:
            dx = dx + r_ref[...].astype(F32)
        dx_ref[...] = dx.astype(out_dtype)
        part = jnp.sum(dyf * xhat, axis=0, keepdims=True)

        @pl.when(pl.program_id(0) == 0)
        def _():
            dg_ref[...] = part

        @pl.when(pl.program_id(0) > 0)
        def _():
            dg_ref[...] += part

    row = pl.BlockSpec((tt, W), lambda i: (i, 0))
    in_specs = [pl.BlockSpec((tt, W), lambda i: (i, cb)), pl.BlockSpec((1, W), lambda i: (0, 0)), row]
    args = (x, g, dy)
    if has_res:
        in_specs.append(row)
        args = args + (res,)
    return pl.pallas_call(
        body, name=name, grid=(T // tt,), in_specs=in_specs,
        out_specs=[row, pl.BlockSpec((1, W), lambda i: (0, 0))],
        out_shape=[jax.ShapeDtypeStruct((T, W), out_dtype), jax.ShapeDtypeStruct((1, W), F32)],
        compiler_params=_cparams(("arbitrary",)),
    )(*args)


def _final_fwd_bwd(h, g, target, *, name):
    T, W = h.shape
    g = g.reshape(1, W)
    tt = ROW_TILE

    def body(x_ref, g_ref, t_ref, loss_ref, dx_ref, dg_ref):
        xf = x_ref[...]
        rstd = lax.rsqrt(jnp.mean(xf * xf, axis=-1, keepdims=True) + NORM_EPS)
        xhat = xf * rstd
        err = xhat * g_ref[...] - t_ref[...]
        lpart = jnp.zeros((1, LANES), F32) + (0.5 / W) * jnp.sum(err * err)
        dyf = err * (1.0 / W)
        dxhat = dyf * g_ref[...]
        dx_ref[...] = rstd * (dxhat - xhat * jnp.mean(dxhat * xhat, axis=-1, keepdims=True))
        part = jnp.sum(dyf * xhat, axis=0, keepdims=True)

        @pl.when(pl.program_id(0) == 0)
        def _():
            dg_ref[...] = part
            loss_ref[...] = lpart

        @pl.when(pl.program_id(0) > 0)
        def _():
            dg_ref[...] += part
            loss_ref[...] += lpart

    row = pl.BlockSpec((tt, W), lambda i: (i, 0))
    return pl.pallas_call(
        body, name=name, grid=(T // tt,),
        in_specs=[row, pl.BlockSpec((1, W), lambda i: (0, 0)), row],
        out_specs=[pl.BlockSpec((1, LANES), lambda i: (0, 0)), row, pl.BlockSpec((1, W), lambda i: (0, 0))],
        out_shape=[jax.ShapeDtypeStruct((1, LANES), F32), jax.ShapeDtypeStruct((T, W), F32),
                   jax.ShapeDtypeStruct((1, W), F32)],
        compiler_params=_cparams(("arbitrary",)),
    )(h, g, target)


def _swap16(x):
    lane = lax.broadcasted_iota(jnp.int32, x.shape, 1)
    return jnp.where((lane % 32) < 16, pltpu.roll(x, LANES - 16, 1), pltpu.roll(x, 16, 1))


def _rope_fwd(x, cos, sin, *, name, cb0, ncb):
    T = x.shape[0]
    tt = ROW_TILE

    def body(x_ref, c_ref, s_ref, o_ref):
        xf = x_ref[...]
        o_ref[...] = xf * c_ref[...] + _swap16(xf) * s_ref[...]

    tab = pl.BlockSpec((tt, LANES), lambda i, j: (i, 0))
    return pl.pallas_call(
        body, name=name, grid=(T // tt, ncb),
        in_specs=[pl.BlockSpec((tt, LANES), lambda i, j: (i, cb0 + j)), tab, tab],
        out_specs=pl.BlockSpec((tt, LANES), lambda i, j: (i, j)),
        out_shape=jax.ShapeDtypeStruct((T, ncb * LANES), F32),
        compiler_params=_cparams(("parallel", "parallel")),
    )(x, cos, sin)


def _rope_bwd(dy, cos, sin, *, name, head_sum):
    T = dy.shape[0]
    tt = ROW_TILE
    ncb = 1 if head_sum else dy.shape[1] // LANES

    def body(d_ref, c_ref, s_ref, o_ref):
        d = d_ref[...]
        if head_sum:
            d = d[:, :LANES] + d[:, LANES:]
            d = d + pltpu.roll(d, 64, 1)
            d = d + pltpu.roll(d, 96, 1)
            lane = lax.broadcasted_iota(jnp.int32, d.shape, 1)
            d = jnp.where(lane < QK_ROPE, d, 0.0)
        o_ref[...] = d * c_ref[...] + _swap16(d * s_ref[...])

    tab = pl.BlockSpec((tt, LANES), lambda i, j: (i, 0))
    d_spec = pl.BlockSpec((tt, 2 * LANES), lambda i, j: (i, 0)) if head_sum else pl.BlockSpec((tt, LANES), lambda i, j: (i, j))
    return pl.pallas_call(
        body, name=name, grid=(T // tt, ncb),
        in_specs=[d_spec, tab, tab],
        out_specs=pl.BlockSpec((tt, LANES), lambda i, j: (i, j)),
        out_shape=jax.ShapeDtypeStruct((T, ncb * LANES), F32),
        compiler_params=_cparams(("parallel", "parallel")),
    )(dy, cos, sin)


ATT_BLOCK = 256


def _attn_scale():
    return float((QK_NOPE + QK_ROPE) ** -0.5)


def _causal_mask(qi, kj, tq, tk):
    row = qi * tq + lax.broadcasted_iota(jnp.int32, (tq, tk), 0)
    col = kj * tk + lax.broadcasted_iota(jnp.int32, (tq, tk), 1)
    return col <= row


def _flash_fwd(q, k, v, *, name):
    BH, S, DK = q.shape
    DV = v.shape[-1]
    tq = tk = min(ATT_BLOCK, S)
    scale = _attn_scale()

    def body(q_ref, k_ref, v_ref, o_ref, lse_ref):
        qi = pl.program_id(1)
        qb = q_ref[0]

        def step(j, carry):
            m, l, acc = carry
            kb = k_ref[0, pl.ds(pl.multiple_of(j * tk, tk), tk), :]
            vb = v_ref[0, pl.ds(pl.multiple_of(j * tk, tk), tk), :]
            s = _dot_nt(qb, kb) * scale
            s = jnp.where(_causal_mask(qi, j, tq, tk), s, -jnp.inf)
            m_new = jnp.maximum(m, jnp.max(s, axis=-1, keepdims=True))
            alpha = jnp.exp(m - m_new)
            p = jnp.exp(s - m_new)
            l = alpha * l + jnp.sum(p, axis=-1, keepdims=True)
            acc = alpha * acc + _dot_nn(p, vb)
            return m_new, l, acc

        init = (jnp.full((tq, 1), -jnp.inf, F32), jnp.zeros((tq, 1), F32), jnp.zeros((tq, DV), F32))
        m, l, acc = lax.fori_loop(0, qi + 1, step, init)
        o_ref[0] = acc / l
        lse_ref[0] = m + jnp.log(l)

    return pl.pallas_call(
        body, name=name, grid=(BH, S // tq),
        in_specs=[pl.BlockSpec((1, tq, DK), lambda b, i: (b, i, 0)),
                  pl.BlockSpec((1, S, DK), lambda b, i: (b, 0, 0)),
                  pl.BlockSpec((1, S, DV), lambda b, i: (b, 0, 0))],
        out_specs=[pl.BlockSpec((1, tq, DV), lambda b, i: (b, i, 0)),
                   pl.BlockSpec((1, tq, 1), lambda b, i: (b, i, 0))],
        out_shape=[jax.ShapeDtypeStruct((BH, S, DV), F32), jax.ShapeDtypeStruct((BH, S, 1), F32)],
        compiler_params=_cparams(("parallel", "parallel")),
    )(q, k, v)


def _flash_bwd_dq(q, k, v, o, lse, do, *, name):
    BH, S, DK = q.shape
    DV = v.shape[-1]
    tq = tk = min(ATT_BLOCK, S)
    scale = _attn_scale()

    def body(q_ref, k_ref, v_ref, o_ref, lse_ref, do_ref, dq_ref):
        qi = pl.program_id(1)
        qb = q_ref[0]
        dob = do_ref[0]
        lse_b = lse_ref[0]
        delta = jnp.sum(dob * o_ref[0], axis=-1, keepdims=True)

        def step(j, dq):
            kb = k_ref[0, pl.ds(pl.multiple_of(j * tk, tk), tk), :]
            vb = v_ref[0, pl.ds(pl.multiple_of(j * tk, tk), tk), :]
            s = _dot_nt(qb, kb) * scale
            p = jnp.where(_causal_mask(qi, j, tq, tk), jnp.exp(s - lse_b), 0.0)
            dp = _dot_nt(dob, vb)
            ds = p * (dp - delta) * scale
            return dq + _dot_nn(ds, kb)

        dq_ref[0] = lax.fori_loop(0, qi + 1, step, jnp.zeros((tq, DK), F32))

    qspec = lambda w: pl.BlockSpec((1, tq, w), lambda b, i: (b, i, 0))
    full = lambda w: pl.BlockSpec((1, S, w), lambda b, i: (b, 0, 0))
    return pl.pallas_call(
        body, name=name, grid=(BH, S // tq),
        in_specs=[qspec(DK), full(DK), full(DV), qspec(DV), qspec(1), qspec(DV)],
        out_specs=qspec(DK),
        out_shape=jax.ShapeDtypeStruct((BH, S, DK), F32),
        compiler_params=_cparams(("parallel", "parallel")),
    )(q, k, v, o, lse, do)


def _flash_bwd_dkv(q, k, v, o, lse, do, *, name):
    BH, S, DK = q.shape
    DV = v.shape[-1]
    tq = tk = min(ATT_BLOCK, S)
    nq = S // tq
    scale = _attn_scale()

    def body(q_ref, k_ref, v_ref, o_ref, lse_ref, do_ref, dk_ref, dv_ref):
        kj = pl.program_id(1)
        kb = k_ref[0]
        vb = v_ref[0]

        def step(i, carry):
            dk, dv = carry
            rows = pl.ds(pl.multiple_of(i * tq, tq), tq)
            qb = q_ref[0, rows, :]
            dob = do_ref[0, rows, :]
            delta = jnp.sum(dob * o_ref[0, rows, :], axis=-1, keepdims=True)
            s = _dot_nt(qb, kb) * scale
            p = jnp.where(_causal_mask(i, kj, tq, tk), jnp.exp(s - lse_ref[0, rows, :]), 0.0)
            dp = _dot_nt(dob, vb)
            ds = p * (dp - delta) * scale
            return dk + _dot_tn(ds, qb), dv + _dot_tn(p, dob)

        dk, dv = lax.fori_loop(kj, nq, step, (jnp.zeros((tk, DK), F32), jnp.zeros((tk, DV), F32)))
        dk_ref[0] = dk
        dv_ref[0] = dv

    kspec = lambda w: pl.BlockSpec((1, tk, w), lambda b, j: (b, j, 0))
    full = lambda w: pl.BlockSpec((1, S, w), lambda b, j: (b, 0, 0))
    return pl.pallas_call(
        body, name=name, grid=(BH, S // tk),
        in_specs=[full(DK), kspec(DK), kspec(DV), full(DV), full(1), full(DV)],
        out_specs=[kspec(DK), kspec(DV)],
        out_shape=[jax.ShapeDtypeStruct((BH, S, DK), F32), jax.ShapeDtypeStruct((BH, S, DV), F32)],
        compiler_params=_cparams(("parallel", "parallel")),
    )(q, k, v, o, lse, do)


def _lru_gates(xl, halo, cw_ref, cb_ref, wa_ref, ba_ref, wx_ref, bx_ref, lam_ref):
    xc = cb_ref[...] + cw_ref[3:4, :] * xl
    for kk in range(LRU_CONV - 1):
        xc = xc + cw_ref[kk:kk + 1, :] * _shift_rows(xl, LRU_CONV - 1 - kk, halo)
    r = _sigmoid(_dot_nn(xc, wa_ref[...]) + ba_ref[...])
    i = _sigmoid(_dot_nn(xc, wx_ref[...]) + bx_ref[...])
    lam = lam_ref[...]
    sp = jnp.maximum(-lam, 0.0) + jnp.log(1.0 + jnp.exp(-jnp.abs(lam)))
    a = jnp.exp(-LRU_C * r * sp)
    mult = jnp.sqrt(1.0 - a * a)
    return xc, r, i, sp, a, mult


def _lru_specs(tt, nt, S):
    def make(rev):
        tmap = (lambda t: nt - 1 - t) if rev else (lambda t: t)
        tile = lambda cb: pl.BlockSpec((tt, LRU_WIDTH), lambda b, t: (b * nt + tmap(t), cb))
        prev8 = lambda cb: pl.BlockSpec(
            (8, LRU_WIDTH), lambda b, t: (jnp.maximum((b * nt + tmap(t)) * (tt // 8) - 1, 0), cb))
        return tile, prev8, tmap
    return make


def _lru_fwd(z, cw, cb, wa, ba, wx, bx, lam, *, S, name):
    T = z.shape[0]
    tt = min(ROW_TILE, S)
    nt = S // tt
    tile, prev8, _ = _lru_specs(tt, nt, S)(False)
    vec = lambda r: pl.BlockSpec((r, LRU_WIDTH), lambda b, t: (0, 0))
    mat = pl.BlockSpec((LRU_WIDTH, LRU_WIDTH), lambda b, t: (0, 0))

    def body(xl_ref, halo_ref, gate_ref, cw_ref, cb_ref, wa_ref, ba_ref, wx_ref, bx_ref, lam_ref,
             y_ref, h_ref, carry_ref):
        t = pl.program_id(1)
        first = t == 0
        halo = jnp.where(first, 0.0, halo_ref[...])
        xl_t = xl_ref[...]
        xc, r, i, sp, a, mult = _lru_gates(xl_t, halo, cw_ref, cb_ref, wa_ref, ba_ref, wx_ref, bx_ref, lam_ref)
        bv = mult * (i * xc)
        ones = jnp.ones((8, LRU_WIDTH), F32)
        zeros = jnp.zeros((8, LRU_WIDTH), F32)
        row = lax.broadcasted_iota(jnp.int32, (tt, LRU_WIDTH), 0)
        A = a
        d = 1
        while d < tt:
            if d < 8:
                a_sh = _shift_rows(A, d, ones)
                b_sh = _shift_rows(bv, d, zeros)
            else:
                a_sh = jnp.where(row < d, 1.0, pltpu.roll(A, d, 0))
                b_sh = jnp.where(row < d, 0.0, pltpu.roll(bv, d, 0))
            bv = A * b_sh + bv
            A = A * a_sh
            d *= 2
        h0 = jnp.where(first, 0.0, carry_ref[0:1, :])
        h = A * h0 + bv
        carry_ref[...] = jnp.broadcast_to(h[tt - 1:tt, :], (8, LRU_WIDTH))
        h_ref[...] = h
        y_ref[...] = (h * _gelu(gate_ref[...])).astype(BF16)

    return pl.pallas_call(
        body, name=name, grid=(T // S, nt),
        in_specs=[tile(0), prev8(0), tile(1), vec(LRU_CONV), vec(1), mat, vec(1), mat, vec(1), vec(1)],
        out_specs=[tile(0), tile(0)],
        out_shape=[jax.ShapeDtypeStruct((T, LRU_WIDTH), BF16), jax.ShapeDtypeStruct((T, LRU_WIDTH), F32)],
        scratch_shapes=[pltpu.VMEM((8, LRU_WIDTH), F32)],
        compiler_params=_cparams(("arbitrary", "arbitrary")),
    )(z, z, z, cw, cb, wa, ba, wx, bx, lam)


def _lru_bwd(z, h, dy, cw, cb, wa, ba, wx, bx, lam, *, S, name):
    T = z.shape[0]
    tt = min(ROW_TILE, S)
    nt = S // tt
    tile, prev8, tmap = _lru_specs(tt, nt, S)(True)
    vec = lambda r: pl.BlockSpec((r, LRU_WIDTH), lambda b, t: (0, 0))
    mat = pl.BlockSpec((LRU_WIDTH, LRU_WIDTH), lambda b, t: (0, 0))

    def body(xl_ref, halo_ref, gate_ref, h_ref, hprev_ref, dy_ref, cw_ref, cb_ref, wa_ref, ba_ref, wx_ref,
             bx_ref, lam_ref, dxl_ref, dgate_ref, dcw_ref, dcb_ref, dwa_ref, dba_ref, dwx_ref, dbx_ref,
             dlam_ref, lamc_ref, ac_ref, dxc_ref):
        b = pl.program_id(0)
        t = pl.program_id(1)
        tr = nt - 1 - t
        seq_first = tr == 0
        seq_last = t == 0
        halo = jnp.where(seq_first, 0.0, halo_ref[...])
        xl_t = xl_ref[...]
        xc, r, i, sp, a, mult = _lru_gates(xl_t, halo, cw_ref, cb_ref, wa_ref, ba_ref, wx_ref, bx_ref, lam_ref)
        hh = h_ref[...]
        dyf = dy_ref[...].astype(F32)
        gl, dgl = _gelu_and_grad(gate_ref[...])
        dgate_ref[...] = (dyf * hh * dgl).astype(BF16)
        dh = dyf * gl

        a_first_later = jnp.where(seq_last, 0.0, ac_ref[...])
        lam_later = jnp.where(seq_last, 0.0, lamc_ref[...])
        row = lax.broadcasted_iota(jnp.int32, (tt, LRU_WIDTH), 0)
        A = _shift_rows_up(a, 1, a_first_later)
        lm = dh
        ones = jnp.ones((8, LRU_WIDTH), F32)
        zeros = jnp.zeros((8, LRU_WIDTH), F32)
        d = 1
        while d < tt:
            if d < 8:
                a_sh = _shift_rows_up(A, d, ones)
                l_sh = _shift_rows_up(lm, d, zeros)
            else:
                a_sh = jnp.where(row >= tt - d, 1.0, pltpu.roll(A, tt - d, 0))
                l_sh = jnp.where(row >= tt - d, 0.0, pltpu.roll(lm, tt - d, 0))
            lm = lm + A * l_sh
            A = A * a_sh
            d *= 2
        lm = lm + A * lam_later[0:1, :]
        lamc_ref[...] = jnp.broadcast_to(lm[0:1, :], (8, LRU_WIDTH))
        ac_ref[...] = jnp.broadcast_to(a[0:1, :], (8, LRU_WIDTH))

        hprev_halo = jnp.where(seq_first, 0.0, hprev_ref[...])
        h_prev = _shift_rows(hh, 1, hprev_halo)
        da = lm * h_prev
        ixc = i * xc
        dmult = lm * ixc
        di = lm * mult * xc
        dxc = lm * mult * i
        da = da - dmult * a / mult
        dlog = da * a
        dr = dlog * (-LRU_C) * sp
        dsp_part = jnp.sum(dlog * (-LRU_C) * r, axis=0, keepdims=True)
        dpa = dr * r * (1.0 - r)
        dpx = di * i * (1.0 - i)
        dxc = dxc + _dot_nt(dpa, wa_ref[...]) + _dot_nt(dpx, wx_ref[...])
        dwa_part = _dot_tn(xc, dpa)
        dwx_part = _dot_tn(xc, dpx)

        later = jnp.where(seq_last, 0.0, dxc_ref[...])
        dxl = cw_ref[3:4, :] * dxc
        for kk in range(LRU_CONV - 1):
            dxl = dxl + cw_ref[kk:kk + 1, :] * _shift_rows_up(dxc, LRU_CONV - 1 - kk, later)
        dxl_ref[...] = dxl.astype(BF16)
        dxc_ref[...] = dxc[0:8, :]
        dcw_rows = [jnp.sum(dxc * _shift_rows(xl_t, LRU_CONV - 1 - kk, halo), axis=0, keepdims=True)
                    for kk in range(LRU_CONV - 1)]
        dcw_rows.append(jnp.sum(dxc * xl_t, axis=0, keepdims=True))
        dcw_part = jnp.concatenate(dcw_rows + [jnp.zeros((8 - LRU_CONV, LRU_WIDTH), F32)], axis=0)
        lamv = lam_ref[...]
        dlam_part = dsp_part * (-_sigmoid(-lamv))
        parts = ((dcw_ref, dcw_part), (dcb_ref, jnp.sum(dxc, axis=0, keepdims=True)),
                 (dwa_ref, dwa_part), (dba_ref, jnp.sum(dpa, axis=0, keepdims=True)),
                 (dwx_ref, dwx_part), (dbx_ref, jnp.sum(dpx, axis=0, keepdims=True)),
                 (dlam_ref, dlam_part))
        start = jnp.logical_and(b == 0, t == 0)

        @pl.when(start)
        def _():
            for ref, val in parts:
                ref[...] = val

        @pl.when(jnp.logical_not(start))
        def _():
            for ref, val in parts:
                ref[...] += val

    acc = lambda r: pl.BlockSpec((r, LRU_WIDTH), lambda b, t: (0, 0))
    return pl.pallas_call(
        body, name=name, grid=(T // S, nt),
        in_specs=[tile(0), prev8(0), tile(1), tile(0), prev8(0), tile(1),
                  vec(LRU_CONV), vec(1), mat, vec(1), mat, vec(1), vec(1)],
        out_specs=[tile(0), tile(0), acc(8), acc(1), mat, acc(1), mat, acc(1), acc(1)],
        out_shape=[jax.ShapeDtypeStruct((T, LRU_WIDTH), BF16), jax.ShapeDtypeStruct((T, LRU_WIDTH), BF16),
                   jax.ShapeDtypeStruct((8, LRU_WIDTH), F32), jax.ShapeDtypeStruct((1, LRU_WIDTH), F32),
                   jax.ShapeDtypeStruct((LRU_WIDTH, LRU_WIDTH), F32), jax.ShapeDtypeStruct((1, LRU_WIDTH), F32),
                   jax.ShapeDtypeStruct((LRU_WIDTH, LRU_WIDTH), F32), jax.ShapeDtypeStruct((1, LRU_WIDTH), F32),
                   jax.ShapeDtypeStruct((1, LRU_WIDTH), F32)],
        scratch_shapes=[pltpu.VMEM((8, LRU_WIDTH), F32), pltpu.VMEM((8, LRU_WIDTH), F32),
                        pltpu.VMEM((8, LRU_WIDTH), F32)],
        compiler_params=_cparams(("arbitrary", "arbitrary")),
    )(z, z, z, h, h, dy, cw, cb, wa, ba, wx, bx, lam)


FFN_CT = 1408


def _ffn_conv(g, halo, cw_ref, cb_ref):
    gc = cb_ref[...] + cw_ref[2:3, :] * g
    for kk in range(FFN_CONV - 1):
        gc = gc + cw_ref[kk:kk + 1, :] * _shift_rows(g, FFN_CONV - 1 - kk, halo)
    return gc


def _ffn_act_fwd(g, u, cw, cb, *, S, name):
    T, F = g.shape
    tt = min(ROW_TILE, S)
    nt = S // tt
    tc = _tile(F, FFN_CT)

    def body(g_ref, halo_ref, u_ref, cw_ref, cb_ref, o_ref):
        first = (pl.program_id(0) % nt) == 0
        halo = jnp.where(first, 0.0, halo_ref[...])
        gc = _ffn_conv(g_ref[...], halo, cw_ref, cb_ref)
        o_ref[...] = (_gelu(gc) * u_ref[...]).astype(BF16)

    tile = pl.BlockSpec((tt, tc), lambda i, j: (i, j))
    prev8 = pl.BlockSpec((8, tc), lambda i, j: (jnp.maximum(i * (tt // 8) - 1, 0), j))
    return pl.pallas_call(
        body, name=name, grid=(T // tt, F // tc),
        in_specs=[tile, prev8, tile, pl.BlockSpec((FFN_CONV, tc), lambda i, j: (0, j)),
                  pl.BlockSpec((1, tc), lambda i, j: (0, j))],
        out_specs=tile,
        out_shape=jax.ShapeDtypeStruct((T, F), BF16),
        compiler_params=_cparams(("parallel", "parallel")),
    )(g, g, u, cw, cb)


def _ffn_act_bwd(g, u, dact, cw, cb, *, S, name):
    T, F = g.shape
    tt = min(ROW_TILE, S)
    nt = S // tt
    ntt = T // tt
    tc = _tile(F, FFN_CT)

    def body(g_ref, halo_ref, u_ref, da_ref, cw_ref, cb_ref, dg_ref, du_ref, dcw_ref, dcb_ref, later_ref):
        step = pl.program_id(1)
        ti = (ntt - 1 - step) % nt
        halo = jnp.where(ti == 0, 0.0, halo_ref[...])
        gt = g_ref[...]
        gc = _ffn_conv(gt, halo, cw_ref, cb_ref)
        gl, dgl = _gelu_and_grad(gc)
        da = da_ref[...].astype(F32)
        du_ref[...] = (da * gl).astype(BF16)
        dgc = da * u_ref[...] * dgl
        later = jnp.where(ti == nt - 1, 0.0, later_ref[...])
        dg = cw_ref[2:3, :] * dgc
        for kk in range(FFN_CONV - 1):
            dg = dg + cw_ref[kk:kk + 1, :] * _shift_rows_up(dgc, FFN_CONV - 1 - kk, later)
        dg_ref[...] = dg.astype(BF16)
        later_ref[...] = dgc[0:8, :]
        rows = [jnp.sum(dgc * _shift_rows(gt, FFN_CONV - 1 - kk, halo), axis=0, keepdims=True)
                for kk in range(FFN_CONV - 1)]
        rows.append(jnp.sum(dgc * gt, axis=0, keepdims=True))
        dcw_part = jnp.concatenate(rows + [jnp.zeros((8 - FFN_CONV, tc), F32)], axis=0)
        dcb_part = jnp.sum(dgc, axis=0, keepdims=True)

        @pl.when(step == 0)
        def _():
            dcw_ref[...] = dcw_part
            dcb_ref[...] = dcb_part

        @pl.when(step > 0)
        def _():
            dcw_ref[...] += dcw_part
            dcb_ref[...] += dcb_part

    tile = pl.BlockSpec((tt, tc), lambda j, s: (ntt - 1 - s, j))
    prev8 = pl.BlockSpec((8, tc), lambda j, s: (jnp.maximum((ntt - 1 - s) * (tt // 8) - 1, 0), j))
    return pl.pallas_call(
        body, name=name, grid=(F // tc, ntt),
        in_specs=[tile, prev8, tile, tile, pl.BlockSpec((FFN_CONV, tc), lambda j, s: (0, j)),
                  pl.BlockSpec((1, tc), lambda j, s: (0, j))],
        out_specs=[tile, tile, pl.BlockSpec((8, tc), lambda j, s: (0, j)), pl.BlockSpec((1, tc), lambda j, s: (0, j))],
        out_shape=[jax.ShapeDtypeStruct((T, F), BF16), jax.ShapeDtypeStruct((T, F), BF16),
                   jax.ShapeDtypeStruct((8, F), F32), jax.ShapeDtypeStruct((1, F), F32)],
        scratch_shapes=[pltpu.VMEM((8, tc), F32)],
        compiler_params=_cparams(("arbitrary", "arbitrary")),
    )(g, g, u, dact, cw, cb)


def _sgu_norm(zv, g_ref, b_ref):
    v = _gelu(zv)
    mu = jnp.mean(v, axis=-1, keepdims=True)
    xc = v - mu
    rstd = lax.rsqrt(jnp.mean(xc * xc, axis=-1, keepdims=True) + NORM_EPS)
    xhat = xc * rstd
    return xhat, rstd, xhat * g_ref[...] + b_ref[...]


def _sgu_fwd(zc, ln_g, ln_b, wm, bmap, *, name):
    T = zc.shape[0]
    W = SGU_WIDTH
    tt = ROW_TILE
    nch = tt // CHUNK

    def body(z_ref, g_ref, b_ref, wm_ref, bm_ref, p_ref):
        u = _gelu(z_ref[:, :W])
        _, _, vn = _sgu_norm(z_ref[:, W:], g_ref, b_ref)
        vn = vn.astype(BF16)
        for n in range(nch):
            rows = slice(n * CHUNK, (n + 1) * CHUNK)
            for gi in range(SGU_GROUPS):
                cols = slice(gi * LANES, (gi + 1) * LANES)
                s = _dot_nn(wm_ref[gi], vn[rows, cols]) + bm_ref[:, cols]
                p_ref[rows, cols] = (u[rows, cols] * s).astype(BF16)

    const2 = lambda r, c: pl.BlockSpec((r, c), lambda i: (0, 0))
    return pl.pallas_call(
        body, name=name, grid=(T // tt,),
        in_specs=[pl.BlockSpec((tt, 2 * W), lambda i: (i, 0)), const2(1, W), const2(1, W),
                  pl.BlockSpec((SGU_GROUPS, CHUNK, CHUNK), lambda i: (0, 0, 0)), const2(CHUNK, W)],
        out_specs=pl.BlockSpec((tt, W), lambda i: (i, 0)),
        out_shape=jax.ShapeDtypeStruct((T, W), BF16),
        compiler_params=_cparams(("parallel",)),
    )(zc, ln_g, ln_b, wm, bmap)


def _sgu_bwd(zc, dp, ln_g, ln_b, wm, bmap, *, name):
    T = zc.shape[0]
    W = SGU_WIDTH
    tt = ROW_TILE
    nch = tt // CHUNK
    nsteps = T // tt

    def body(z_ref, dp_ref, g_ref, b_ref, wm_ref, bm_ref, dz_ref, dg_ref, db_ref, dwm_ref, dbm_ref,
             s_scr, dvn_scr):
        step = pl.program_id(0)
        zu = z_ref[:, :W]
        zv = z_ref[:, W:]
        u, dgu = _gelu_and_grad(zu)
        xhat, rstd, vn = _sgu_norm(zv, g_ref, b_ref)
        vnb = vn.astype(BF16)
        dpf = dp_ref[...].astype(F32)
        ds = dpf * u

        @pl.when(step == 0)
        def _():
            dwm_ref[...] = jnp.zeros_like(dwm_ref)
            dbm_ref[...] = jnp.zeros_like(dbm_ref)

        for n in range(nch):
            rows = slice(n * CHUNK, (n + 1) * CHUNK)
            for gi in range(SGU_GROUPS):
                cols = slice(gi * LANES, (gi + 1) * LANES)
                s_scr[rows, cols] = _dot_nn(wm_ref[gi], vnb[rows, cols]) + bm_ref[:, cols]
                dsb = ds[rows, cols]
                dvn_scr[rows, cols] = _dot_tn(wm_ref[gi], dsb)
                dwm_ref[gi] += _dot_nt(dsb, vnb[rows, cols])
                dbm_ref[:, cols] += dsb
        dz_ref[:, :W] = (dpf * s_scr[...] * dgu).astype(BF16)
        dvn = dvn_scr[...]
        dxhat = dvn * g_ref[...]
        dv = rstd * (dxhat - jnp.mean(dxhat, axis=-1, keepdims=True)
                     - xhat * jnp.mean(dxhat * xhat, axis=-1, keepdims=True))
        _, dgv = _gelu_and_grad(zv)
        dz_ref[:, W:] = (dv * dgv).astype(BF16)
        dg_part = jnp.sum(dvn * xhat, axis=0, keepdims=True)
        db_part = jnp.sum(dvn, axis=0, keepdims=True)

        @pl.when(step == 0)
        def _():
            dg_ref[...] = dg_part
            db_ref[...] = db_part

        @pl.when(step > 0)
        def _():
            dg_ref[...] += dg_part
            db_ref[...] += db_part

        @pl.when(step == nsteps - 1)
        def _():
            for gi in range(SGU_GROUPS):
                cols = slice(gi * LANES, (gi + 1) * LANES)
                tot = jnp.sum(dbm_ref[:, cols], axis=1, keepdims=True)
                dbm_ref[:, cols] = jnp.broadcast_to(tot, (CHUNK, LANES))

    const2 = lambda r, c: pl.BlockSpec((r, c), lambda i: (0, 0))
    wspec = pl.BlockSpec((SGU_GROUPS, CHUNK, CHUNK), lambda i: (0, 0, 0))
    return pl.pallas_call(
        body, name=name, grid=(nsteps,),
        in_specs=[pl.BlockSpec((tt, 2 * W), lambda i: (i, 0)), pl.BlockSpec((tt, W), lambda i: (i, 0)),
                  const2(1, W), const2(1, W), wspec, const2(CHUNK, W)],
        out_specs=[pl.BlockSpec((tt, 2 * W), lambda i: (i, 0)), const2(1, W), const2(1, W), wspec, const2(CHUNK, W)],
        out_shape=[jax.ShapeDtypeStruct((T, 2 * W), BF16), jax.ShapeDtypeStruct((1, W), F32),
                   jax.ShapeDtypeStruct((1, W), F32), jax.ShapeDtypeStruct((SGU_GROUPS, CHUNK, CHUNK), F32),
                   jax.ShapeDtypeStruct((CHUNK, W), F32)],
        scratch_shapes=[pltpu.VMEM((tt, W), F32), pltpu.VMEM((tt, W), F32)],
        compiler_params=_cparams(("arbitrary",)),
    )(zc, dp, ln_g, ln_b, wm, bmap)


def _rope_tables(positions):
    half = QK_ROPE // 2
    inv_freq = jnp.exp(-math.log(ROPE_BASE) * jnp.arange(half, dtype=F32) / half)
    ang = positions.reshape(-1).astype(F32)[:, None] * inv_freq
    cos = jnp.cos(ang)
    sin = jnp.sin(ang)
    reps = LANES // QK_ROPE
    return jnp.tile(jnp.concatenate([cos, cos], axis=1), (1, reps)), jnp.tile(jnp.concatenate([-sin, sin], axis=1), (1, reps))


def _to_heads(a, B, S, w):
    return a.reshape(B, S, MLA_HEADS, w).transpose(0, 2, 1, 3)


def _from_heads(a, B, S):
    w = a.shape[-1]
    return a.reshape(B, MLA_HEADS, S, w).transpose(0, 2, 1, 3).reshape(B * S, MLA_HEADS * w)


def _prep_weights(w):
    p = dict(w)
    w_in = w["ab_w_in"][0]
    o1, o2, o3, o4 = Q_LORA, Q_LORA + KV_LORA, Q_LORA + KV_LORA + QK_ROPE, Q_LORA + KV_LORA + QK_ROPE + LRU_WIDTH
    p["w_in_p"] = jnp.concatenate(
        [w_in[:, o3:o4], w_in[:, o4:], w_in[:, :o1], w_in[:, o1:o2], w_in[:, o2:o3],
         jnp.zeros((D_MODEL, LANES - QK_ROPE), w_in.dtype)], axis=1).astype(BF16)
    wq = w["ab_w_q_b"][0].reshape(Q_LORA, MLA_HEADS, QK_NOPE + QK_ROPE)
    p["w_q_p"] = jnp.concatenate([wq[:, :, :QK_NOPE].reshape(Q_LORA, -1), wq[:, :, QK_NOPE:].reshape(Q_LORA, -1)],
                                 axis=1).astype(BF16)
    wkv = w["ab_w_kv_b"][0].reshape(KV_LORA, MLA_HEADS, QK_NOPE + V_HEAD)
    p["w_kv_p"] = jnp.concatenate([wkv[:, :, :QK_NOPE].reshape(KV_LORA, -1), wkv[:, :, QK_NOPE:].reshape(KV_LORA, -1)],
                                  axis=1).astype(BF16)
    eye = jnp.eye(LRU_HEADS, dtype=F32)
    dense = lambda wg: (wg[:, :, None, :] * eye[:, None, :, None]).reshape(LRU_WIDTH, LRU_WIDTH).astype(BF16)
    p["wa_d"] = dense(w["ab_w_rg_a"][0])
    p["wx_d"] = dense(w["ab_w_rg_x"][0])
    causal = jnp.tril(jnp.ones((CHUNK, CHUNK), F32))
    p["wm"] = (w["c_w_s"][0] * causal).astype(BF16)
    p["bmap"] = jnp.repeat(w["c_b_s"][0].T, SGU_GROUP_DIM_, axis=1)
    return p


SGU_GROUP_DIM_ = SGU_WIDTH // SGU_GROUPS


def _ffn_fwd(h, l, p, S):
    hn = _rms_fwd(h, p["ffn_norm"][l], name=f"ffn{l}_norm")
    g = _mm(hn, p["ffn_w_gate"][l], name=f"ffn{l}_gate")
    u = _mm(hn, p["ffn_w_up"][l], name=f"ffn{l}_up")
    act = _ffn_act_fwd(g, u, p["ffn_conv_w"][l], p["ffn_conv_b"][l][None], S=S, name=f"ffn{l}_act")
    out = _mm(act, p["ffn_w_down"][l], res=h, name=f"ffn{l}_down")
    return out, (hn, g, u, act)


def _ffn_bwd(dh, h_in, l, p, saved, S):
    hn, g, u, act = saved
    dact = _mm(dh, p["ffn_w_down"][l], tb=True, out_dtype=BF16, name=f"ffn{l}_dact")
    dw_down = _mm(act, dh, ta=True, name=f"ffn{l}_dwdown")
    dg, du, dcw, dcb = _ffn_act_bwd(g, u, dact, p["ffn_conv_w"][l], p["ffn_conv_b"][l][None], S=S, name=f"ffn{l}_dactbwd")
    dhn = _mm(dg, p["ffn_w_gate"][l], tb=True, name=f"ffn{l}_dhn_g")
    dhn = _mm(du, p["ffn_w_up"][l], tb=True, res=dhn, name=f"ffn{l}_dhn_u")
    dw_gate = _mm(hn, dg, ta=True, name=f"ffn{l}_dwgate")
    dw_up = _mm(hn, du, ta=True, name=f"ffn{l}_dwup")
    dh_in, dnorm = _rms_bwd(h_in, p["ffn_norm"][l], dhn, res=dh, name=f"ffn{l}_dnorm")
    grads = dict(ffn_norm=dnorm[0], ffn_w_gate=dw_gate, ffn_w_up=dw_up, ffn_conv_w=dcw[:FFN_CONV],
                 ffn_conv_b=dcb[0], ffn_w_down=dw_down)
    return dh_in, grads


def _local_step(x, positions, target, w):
    B, S, D = x.shape
    T = B * S
    H = MLA_HEADS
    p = _prep_weights(w)
    xf = x.reshape(T, D)
    tgt = target.reshape(T, D)
    cos, sin = _rope_tables(positions)

    hn0 = _rms_fwd(xf, p["ab_norm"][0], name="ab_norm")
    z = _mm(hn0, p["w_in_p"], name="ab_in")
    cqn = _rms_fwd(z, p["ab_q_norm"][0], cb=4, name="q_norm")
    ckvn = _rms_fwd(z, p["ab_kv_norm"][0], cb=10, name="kv_norm")
    q = _mm(cqn, p["w_q_p"], name="q_up")
    kv = _mm(ckvn, p["w_kv_p"], name="kv_up")
    qr = _rope_fwd(q, cos, sin, cb0=4, ncb=2, name="q_rope")
    kr = _rope_fwd(z, cos, sin, cb0=11, ncb=1, name="k_rope")
    nope = H * QK_NOPE
    zpad = jnp.zeros((B, H, S, LANES - QK_NOPE - QK_ROPE), F32)
    q_h = jnp.concatenate([_to_heads(q[:, :nope], B, S, QK_NOPE), _to_heads(qr, B, S, QK_ROPE), zpad], axis=-1)
    k_h = jnp.concatenate([_to_heads(kv[:, :nope], B, S, QK_NOPE),
                           jnp.broadcast_to(kr[:, :QK_ROPE].reshape(B, 1, S, QK_ROPE), (B, H, S, QK_ROPE)), zpad], axis=-1)
    q_h = q_h.reshape(B * H, S, LANES).astype(BF16)
    k_h = k_h.reshape(B * H, S, LANES).astype(BF16)
    v_h = _to_heads(kv[:, nope:], B, S, V_HEAD).reshape(B * H, S, V_HEAD).astype(BF16)
    o_h, lse = _flash_fwd(q_h, k_h, v_h, name="attn_fwd")
    y_mla = _from_heads(o_h, B, S).astype(BF16)
    lru_par = (p["ab_conv_w"][0], p["ab_conv_b"], p["wa_d"], p["ab_b_rg_a"], p["wx_d"], p["ab_b_rg_x"], p["ab_lambda"])
    y_lru, hs = _lru_fwd(z, *lru_par, S=S, name="lru_fwd")
    mix = jnp.concatenate([y_mla, y_lru], axis=1)
    h1 = _mm(mix, p["ab_w_out"][0], res=xf, name="ab_out")
    h2, ffn0 = _ffn_fwd(h1, 0, p, S)

    hn2 = _rms_fwd(h2, p["c_norm"][0], name="c_norm")
    zc = _mm(hn2, p["c_w_in"][0], name="c_in")
    pg = _sgu_fwd(zc, p["c_ln_g"], p["c_ln_b"], p["wm"], p["bmap"], name="sgu_fwd")
    h3 = _mm(pg, p["c_w_out"][0], res=h2, name="c_out")
    h4, ffn1 = _ffn_fwd(h3, 1, p, S)

    loss_row, dh4, dfinal = _final_fwd_bwd(h4, p["final_norm"], tgt, name="final")

    dh3, g_ffn1 = _ffn_bwd(dh4, h3, 1, p, ffn1, S)
    dpg = _mm(dh3, p["c_w_out"][0], tb=True, out_dtype=BF16, name="c_dp")
    dw_c_out = _mm(pg, dh3, ta=True, name="c_dwout")
    dzc, dlng, dlnb, dwm, dbm = _sgu_bwd(zc, dpg, p["c_ln_g"], p["c_ln_b"], p["wm"], p["bmap"], name="sgu_bwd")
    dhn2 = _mm(dzc, p["c_w_in"][0], tb=True, name="c_dhn")
    dw_c_in = _mm(hn2, dzc, ta=True, name="c_dwin")
    dh2, dcnorm = _rms_bwd(h2, p["c_norm"][0], dhn2, res=dh3, name="c_dnorm")
    dh1, g_ffn0 = _ffn_bwd(dh2, h1, 0, p, ffn0, S)

    dmix = _mm(dh1, p["ab_w_out"][0], tb=True, name="ab_dmix")
    dw_out = _mm(mix, dh1, ta=True, name="ab_dwout")
    do_h = _to_heads(dmix[:, :H * V_HEAD], B, S, V_HEAD).reshape(B * H, S, V_HEAD)
    dq_h = _flash_bwd_dq(q_h, k_h, v_h, o_h, lse, do_h, name="attn_dq")
    dk_h, dv_h = _flash_bwd_dkv(q_h, k_h, v_h, o_h, lse, do_h, name="attn_dkv")
    dqr = _rope_bwd(_from_heads(dq_h[..., QK_NOPE:QK_NOPE + QK_ROPE], B, S), cos, sin, head_sum=False, name="q_rope_bwd")
    dkr = _rope_bwd(_from_heads(dk_h[..., QK_NOPE:QK_NOPE + QK_ROPE], B, S), cos, sin, head_sum=True, name="k_rope_bwd")
    dq_full = jnp.concatenate([_from_heads(dq_h[..., :QK_NOPE], B, S), dqr], axis=1).astype(BF16)
    dkv = jnp.concatenate([_from_heads(dk_h[..., :QK_NOPE], B, S), _from_heads(dv_h, B, S)], axis=1).astype(BF16)
    dcqn = _mm(dq_full, p["w_q_p"], tb=True, name="q_dlat")
    dw_q_p = _mm(cqn, dq_full, ta=True, name="q_dw")
    dckvn = _mm(dkv, p["w_kv_p"], tb=True, name="kv_dlat")
    dw_kv_p = _mm(ckvn, dkv, ta=True, name="kv_dw")
    dcq, dqnorm = _rms_bwd(z, p["ab_q_norm"][0], dcqn, cb=4, out_dtype=BF16, name="q_dnorm")
    dckv, dkvnorm = _rms_bwd(z, p["ab_kv_norm"][0], dckvn, cb=10, out_dtype=BF16, name="kv_dnorm")
    dxl, dgate, dcw, dcb, dwa, dba, dwx, dbx, dlam = _lru_bwd(z, hs, dmix, *lru_par, S=S, name="lru_bwd")
    dz = jnp.concatenate([dxl, dgate, dcq, dckv, dkr.astype(BF16)], axis=1)
    dhn0 = _mm(dz, p["w_in_p"], tb=True, name="ab_dhn")
    dw_in_p = _mm(hn0, dz, ta=True, name="ab_dwin")
    dx, dabnorm = _rms_bwd(xf, p["ab_norm"][0], dhn0, res=dh1, name="ab_dnorm")

    a0, a1, a2, a3 = 2 * LRU_WIDTH, 2 * LRU_WIDTH + Q_LORA, 2 * LRU_WIDTH + Q_LORA + KV_LORA, 2 * LRU_WIDTH + Q_LORA + KV_LORA + QK_ROPE
    dw_in = jnp.concatenate([dw_in_p[:, a0:a1], dw_in_p[:, a1:a2], dw_in_p[:, a2:a3],
                             dw_in_p[:, :LRU_WIDTH], dw_in_p[:, LRU_WIDTH:a0]], axis=1)
    dw_q = jnp.concatenate([dw_q_p[:, :nope].reshape(Q_LORA, H, QK_NOPE), dw_q_p[:, nope:].reshape(Q_LORA, H, QK_ROPE)],
                           axis=2).reshape(Q_LORA, -1)
    dw_kv = jnp.concatenate([dw_kv_p[:, :nope].reshape(KV_LORA, H, QK_NOPE), dw_kv_p[:, nope:].reshape(KV_LORA, H, V_HEAD)],
                            axis=2).reshape(KV_LORA, -1)
    blocks = lambda dd: jnp.stack([dd[i * LRU_BLOCK:(i + 1) * LRU_BLOCK, i * LRU_BLOCK:(i + 1) * LRU_BLOCK]
                                   for i in range(LRU_HEADS)])
    causal = jnp.tril(jnp.ones((CHUNK, CHUNK), F32))
    grads = {
        "ab_norm": dabnorm, "ab_w_in": dw_in[None], "ab_q_norm": dqnorm, "ab_w_q_b": dw_q[None],
        "ab_kv_norm": dkvnorm, "ab_w_kv_b": dw_kv[None], "ab_conv_w": dcw[:LRU_CONV][None], "ab_conv_b": dcb,
        "ab_w_rg_a": blocks(dwa)[None], "ab_b_rg_a": dba, "ab_w_rg_x": blocks(dwx)[None], "ab_b_rg_x": dbx,
        "ab_lambda": dlam, "ab_w_out": dw_out[None],
        "c_norm": dcnorm, "c_w_in": dw_c_in[None], "c_ln_g": dlng, "c_ln_b": dlnb,
        "c_w_s": (dwm * causal)[None], "c_b_s": dbm[:, ::SGU_GROUP_DIM_].T[None], "c_w_out": dw_c_out[None],
        "final_norm": dfinal[0],
    }
    for name in ("ffn_norm", "ffn_w_gate", "ffn_w_up", "ffn_conv_w", "ffn_conv_b", "ffn_w_down"):
        grads[name] = jnp.stack([g_ffn0[name], g_ffn1[name]])
    return loss_row, dx.reshape(B, S, D), grads


ANY = pl.BlockSpec(memory_space=pl.ANY)


def _place():
    x, y, c = lax.axis_index("x"), lax.axis_index("y"), lax.axis_index("c")
    chips = [(1 - x, y), (x, 1 - y), (1 - x, 1 - y)]
    return x, y, c, 2 * x + y, (x, y, 1 - c), chips


def _remote(src, dst, send_sems, recv_sems, k, to):
    return pltpu.make_async_remote_copy(src_ref=src, dst_ref=dst, send_sem=send_sems.at[k], recv_sem=recv_sems.at[k],
                                        device_id=to, device_id_type=MESH)


def _all_gather_shards(ws, *, name):
    _, R, L = ws.shape

    def body(ws_ref, out_ref, send_sems, recv_sems, local_sem):
        x, y, c, j, sib, chips = _place()
        mine = pltpu.make_async_copy(ws_ref, out_ref.at[j], local_sem)
        mine.start()
        first = [_remote(ws_ref.at[c], out_ref.at[j, c], send_sems, recv_sems, k, (cx, cy, c))
                 for k, (cx, cy) in enumerate(chips)]
        for cp in first:
            cp.start()
        passed = []
        for k, (cx, cy) in enumerate(chips):
            jk = 2 * cx + cy
            _remote(ws_ref.at[c], out_ref.at[jk, c], send_sems, recv_sems, k, (cx, cy, c)).wait_recv()
            cp = _remote(out_ref.at[jk, c], out_ref.at[jk, c], send_sems, recv_sems, 3 + k, sib)
            cp.start()
            passed.append(cp)
        for k, (cx, cy) in enumerate(chips):
            jk = 2 * cx + cy
            _remote(out_ref.at[jk, 1 - c], out_ref.at[jk, 1 - c], send_sems, recv_sems, 3 + k, sib).wait_recv()
        for cp in first + passed:
            cp.wait_send()
        mine.wait()

    return pl.pallas_call(
        body, name=name, in_specs=[ANY], out_specs=ANY,
        out_shape=jax.ShapeDtypeStruct((N_CHIPS, 2, R, L), ws.dtype),
        scratch_shapes=[pltpu.SemaphoreType.DMA((6,)), pltpu.SemaphoreType.DMA((6,)), pltpu.SemaphoreType.DMA],
    )(ws)


def _pair_swap(a, *, name):
    def body(a_ref, land_ref, send_sems, recv_sems):
        x, y, c, j, sib, chips = _place()
        cp = _remote(a_ref, land_ref, send_sems, recv_sems, 0, sib)
        cp.start()
        cp.wait()

    return pl.pallas_call(
        body, name=name, in_specs=[ANY], out_specs=ANY, out_shape=jax.ShapeDtypeStruct(a.shape, a.dtype),
        scratch_shapes=[pltpu.SemaphoreType.DMA((1,)), pltpu.SemaphoreType.DMA((1,))],
    )(a)


def _pair_share(a, *, name):
    def body(a_ref, out_ref, send_sems, recv_sems, local_sem):
        x, y, c, j, sib, chips = _place()
        mine = pltpu.make_async_copy(a_ref, out_ref.at[c], local_sem)
        mine.start()
        cp = _remote(a_ref, out_ref.at[c], send_sems, recv_sems, 0, sib)
        cp.start()
        _remote(a_ref, out_ref.at[1 - c], send_sems, recv_sems, 0, sib).wait_recv()
        cp.wait_send()
        mine.wait()

    return pl.pallas_call(
        body, name=name, in_specs=[ANY], out_specs=ANY, out_shape=jax.ShapeDtypeStruct((2,) + a.shape, a.dtype),
        scratch_shapes=[pltpu.SemaphoreType.DMA((1,)), pltpu.SemaphoreType.DMA((1,)), pltpu.SemaphoreType.DMA],
    )(a)


def _chip_exchange(p, *, scatter, name):
    R, L = p.shape[-2:]

    def body(p_ref, out_ref, send_sems, recv_sems, local_sem):
        x, y, c, j, sib, chips = _place()
        own = p_ref.at[j] if scatter else p_ref
        mine = pltpu.make_async_copy(own, out_ref.at[j], local_sem)
        mine.start()
        sends = []
        for k, (cx, cy) in enumerate(chips):
            src = p_ref.at[2 * cx + cy] if scatter else p_ref
            cp = _remote(src, out_ref.at[j], send_sems, recv_sems, k, (cx, cy, c))
            cp.start()
            sends.append(cp)
        for k, (cx, cy) in enumerate(chips):
            _remote(own, out_ref.at[2 * cx + cy], send_sems, recv_sems, k, (cx, cy, c)).wait_recv()
        for cp in sends:
            cp.wait_send()
        mine.wait()

    return pl.pallas_call(
        body, name=name, in_specs=[ANY], out_specs=ANY, out_shape=jax.ShapeDtypeStruct((N_CHIPS, R, L), p.dtype),
        scratch_shapes=[pltpu.SemaphoreType.DMA((3,)), pltpu.SemaphoreType.DMA((3,)), pltpu.SemaphoreType.DMA],
    )(p)


FLAT_ROWS = 512


def _add2(a, b, *, out_dtype, name):
    n, R, L = a.shape
    tr = _tile(R, FLAT_ROWS, 8)

    def body(a_ref, b_ref, o_ref):
        o_ref[...] = (a_ref[...].astype(F32) + b_ref[...].astype(F32)).astype(out_dtype)

    spec = pl.BlockSpec((n, tr, L), lambda i: (0, i, 0))
    return pl.pallas_call(
        body, name=name, grid=(R // tr,), in_specs=[spec, spec], out_specs=spec,
        out_shape=jax.ShapeDtypeStruct(a.shape, out_dtype), compiler_params=_cparams(("parallel",)),
    )(a, b)


def _sum_slots(buf, *, name):
    n, R, L = buf.shape
    tr = _tile(R, FLAT_ROWS, 8)

    def body(b_ref, o_ref):
        acc = b_ref[0].astype(F32)
        for k in range(1, n):
            acc = acc + b_ref[k].astype(F32)
        o_ref[...] = acc

    return pl.pallas_call(
        body, name=name, grid=(R // tr,), in_specs=[pl.BlockSpec((n, tr, L), lambda i: (0, i, 0))],
        out_specs=pl.BlockSpec((tr, L), lambda i: (i, 0)),
        out_shape=jax.ShapeDtypeStruct((R, L), F32), compiler_params=_cparams(("parallel",)),
    )(buf)


def _adamw(w, g, m, v, *, name):
    R, L = w.shape
    tr = _tile(R, FLAT_ROWS, 8)
    c1 = 1.0 - ADAM_B1 ** ADAM_STEP
    c2 = 1.0 - ADAM_B2 ** ADAM_STEP

    def body(w_ref, g_ref, m_ref, v_ref, d_ref, nm_ref, nv_ref):
        gg = g_ref[...]
        mm = ADAM_B1 * m_ref[...] + (1.0 - ADAM_B1) * gg
        vv = ADAM_B2 * v_ref[...] + (1.0 - ADAM_B2) * (gg * gg)
        nm_ref[...] = mm
        nv_ref[...] = vv
        d_ref[...] = -ADAM_LR * ((mm / c1) / (jnp.sqrt(vv / c2) + ADAM_EPS) + ADAM_WD * w_ref[...])

    spec = pl.BlockSpec((tr, L), lambda i: (i, 0))
    sh = jax.ShapeDtypeStruct((R, L), F32)
    return pl.pallas_call(
        body, name=name, grid=(R // tr,), in_specs=[spec] * 4, out_specs=[spec] * 3, out_shape=[sh] * 3,
        compiler_params=_cparams(("parallel",)),
    )(w, g, m, v)


WEIGHT_NAMES = ["ab_norm", "ab_w_in", "ab_q_norm", "ab_w_q_b", "ab_kv_norm", "ab_w_kv_b", "ab_conv_w", "ab_conv_b",
                "ab_w_rg_a", "ab_b_rg_a", "ab_w_rg_x", "ab_b_rg_x", "ab_lambda", "ab_w_out", "c_norm", "c_w_in",
                "c_ln_g", "c_ln_b", "c_w_s", "c_b_s", "c_w_out", "ffn_norm", "ffn_w_gate", "ffn_w_up", "ffn_conv_w",
                "ffn_conv_b", "ffn_w_down", "final_norm"]
BIG = {"ab_w_in": 2, "ab_w_q_b": 2, "ab_w_kv_b": 2, "ab_w_out": 1, "c_w_in": 2, "c_w_out": 1,
       "ffn_w_gate": 2, "ffn_w_up": 2, "ffn_w_down": 1}
SMALL_SHARDED = {"ab_conv_w": 2, "c_norm": 1, "c_ln_g": 1, "c_ln_b": 1, "ffn_conv_w": 2}
SMALL_REPLICATED = [n for n in WEIGHT_NAMES if n not in BIG and n not in SMALL_SHARDED]


def _rows(n_elems, mult):
    r = -(-n_elems // LANES)
    return -(-r // mult) * mult


def _flat(parts, rows):
    flat = jnp.concatenate([a.reshape(-1) for a in parts])
    return jnp.pad(flat, (0, rows * LANES - flat.shape[0])).reshape(rows, LANES)


def _unflat(flat, shapes):
    flat = flat.reshape(-1)
    out, off = [], 0
    for s in shapes:
        n = math.prod(s)
        out.append(flat[off:off + n].reshape(s))
        off += n
    return out


def _join_shards(a, axis):
    a = jnp.moveaxis(a, 0, axis)
    return a.reshape(a.shape[:axis] + (a.shape[axis] * a.shape[axis + 1],) + a.shape[axis + 2:])


def _split_shards(a, axis):
    a = a.reshape(a.shape[:axis] + (N_CHIPS, a.shape[axis] // N_CHIPS) + a.shape[axis + 1:])
    return jnp.moveaxis(a, axis, 0)


def kernel(x, positions, ab_norm, ab_w_in, ab_q_norm, ab_w_q_b, ab_kv_norm, ab_w_kv_b, ab_conv_w, ab_conv_b, ab_w_rg_a, ab_b_rg_a, ab_w_rg_x, ab_b_rg_x, ab_lambda, ab_w_out, c_norm, c_w_in, c_ln_g, c_ln_b, c_w_s, c_b_s, c_w_out, ffn_norm, ffn_w_gate, ffn_w_up, ffn_conv_w, ffn_conv_b, ffn_w_down, final_norm, loss_target, m_ab_norm, m_ab_w_in, m_ab_q_norm, m_ab_w_q_b, m_ab_kv_norm, m_ab_w_kv_b, m_ab_conv_w, m_ab_conv_b, m_ab_w_rg_a, m_ab_b_rg_a, m_ab_w_rg_x, m_ab_b_rg_x, m_ab_lambda, m_ab_w_out, m_c_norm, m_c_w_in, m_c_ln_g, m_c_ln_b, m_c_w_s, m_c_b_s, m_c_w_out, m_ffn_norm, m_ffn_w_gate, m_ffn_w_up, m_ffn_conv_w, m_ffn_conv_b, m_ffn_w_down, m_final_norm, v_ab_norm, v_ab_w_in, v_ab_q_norm, v_ab_w_q_b, v_ab_kv_norm, v_ab_w_kv_b, v_ab_conv_w, v_ab_conv_b, v_ab_w_rg_a, v_ab_b_rg_a, v_ab_w_rg_x, v_ab_b_rg_x, v_ab_lambda, v_ab_w_out, v_c_norm, v_c_w_in, v_c_ln_g, v_c_ln_b, v_c_w_s, v_c_b_s, v_c_w_out, v_ffn_norm, v_ffn_w_gate, v_ffn_w_up, v_ffn_conv_w, v_ffn_conv_b, v_ffn_w_down, v_final_norm):
    given = dict(locals())
    w = {n: given[n] for n in WEIGHT_NAMES}
    m = {n: given["m_" + n] for n in WEIGHT_NAMES}
    v = {n: given["v_" + n] for n in WEIGHT_NAMES}
    c = lax.axis_index("c")
    chip = 2 * lax.axis_index("x") + lax.axis_index("y")

    small_bits = [lax.bitcast_convert_type(w[n], BF16) for n in SMALL_SHARDED]
    n_big = sum(w[n].size for n in BIG)
    n_gather = n_big + sum(a.size for a in small_bits)
    R = _rows(-(-n_gather // 2), FLAT_ROWS)
    ws = _flat([w[n].astype(BF16) for n in BIG] + small_bits, 2 * R).reshape(2, R, LANES)
    gathered = _all_gather_shards(ws, name="gather_weights").reshape(N_CHIPS, -1)
    full, off = {}, 0
    for n in BIG:
        seg = gathered[:, off:off + w[n].size].reshape((N_CHIPS,) + w[n].shape)
        full[n] = _join_shards(seg, BIG[n])
        off += w[n].size
    for n in SMALL_SHARDED:
        seg = gathered[:, off:off + 2 * w[n].size].reshape((N_CHIPS,) + w[n].shape + (2,))
        full[n] = _join_shards(lax.bitcast_convert_type(seg, F32), SMALL_SHARDED[n])
        off += 2 * w[n].size
    for n in SMALL_REPLICATED:
        full[n] = w[n]

    loss_row, grad_x, g_full = _local_step(x, positions, loss_target, full)

    gp = jnp.concatenate([_split_shards(g_full[n], BIG[n]).reshape(N_CHIPS, -1) for n in BIG], axis=1)
    gp = jnp.pad(gp.astype(BF16), ((0, 0), (0, 2 * R * LANES - n_big))).reshape(N_CHIPS, 2, R, LANES)
    own = lax.dynamic_index_in_dim(gp, c, axis=1, keepdims=False)
    other = lax.dynamic_index_in_dim(gp, 1 - c, axis=1, keepdims=False)
    from_sib = _pair_swap(other, name="grad_pair_swap")
    pair = _add2(own, from_sib, out_dtype=BF16, name="grad_pair_add")
    from_chips = _chip_exchange(pair, scatter=True, name="grad_chip_exchange")
    half = _sum_slots(from_chips, name="grad_chip_sum")
    g_big = _pair_share(half, name="grad_pair_share")

    small_names = SMALL_REPLICATED + list(SMALL_SHARDED)
    rs = _rows(sum(g_full[n].size for n in small_names) + LANES, FLAT_ROWS)
    small = _flat([loss_row] + [g_full[n] for n in small_names], rs)
    both = _pair_share(small, name="small_pair_share")
    pair_small = _sum_slots(both, name="small_pair_sum")
    all_small = _chip_exchange(pair_small, scatter=False, name="small_chip_exchange")
    small_sum = _sum_slots(all_small, name="small_chip_sum")
    small_parts = _unflat(small_sum, [(1, LANES)] + [g_full[n].shape for n in small_names])
    loss = small_parts[0][0, 0]
    grads = {}
    for n, a in zip(small_names, small_parts[1:]):
        if n in SMALL_SHARDED:
            ax = SMALL_SHARDED[n]
            a = lax.dynamic_slice_in_dim(a, chip * w[n].shape[ax], w[n].shape[ax], axis=ax)
        grads[n] = a
    for n, a in zip(BIG, _unflat(g_big, [w[n].shape for n in BIG])):
        grads[n] = a

    ra = _rows(sum(w[n].size for n in WEIGHT_NAMES), FLAT_ROWS)
    pack = lambda d: _flat([d[n] for n in WEIGHT_NAMES], ra)
    delta_f, new_m_f, new_v_f = _adamw(pack(w), pack(grads), pack(m), pack(v), name="adamw")
    shapes = [w[n].shape for n in WEIGHT_NAMES]
    delta, new_m, new_v = _unflat(delta_f, shapes), _unflat(new_m_f, shapes), _unflat(new_v_f, shapes)
    return (loss, grad_x, *[grads[n].reshape(w[n].shape) for n in WEIGHT_NAMES], *delta, *new_m, *new_v)
```

```python
import functools
import math

import jax
import jax.numpy as jnp
from jax import lax
from jax.experimental import pallas as pl
from jax.experimental.pallas import tpu as pltpu

F32 = jnp.float32
BF16 = jnp.bfloat16
MESH = pl.DeviceIdType.MESH

D_MODEL = 1024
MLA_HEADS = 8
Q_LORA = 256
KV_LORA = 128
QK_NOPE = 64
QK_ROPE = 32
V_HEAD = 64
LRU_WIDTH = 512
LRU_HEADS = 8
LRU_BLOCK = 64
LRU_CONV = 4
LRU_C = 8.0
CHUNK = 128
SGU_GROUPS = 8
SGU_WIDTH = 1024
D_FF = 2816
FFN_CONV = 3
NORM_EPS = 1e-6
ROPE_BASE = 10000.0
AB_IN_PAD = 1536
ADAM_LR = 0.001
ADAM_B1 = 0.9
ADAM_B2 = 0.999
ADAM_EPS = 1e-08
ADAM_WD = 0.01
ADAM_STEP = 10

N_CHIPS = 4
LANES = 128
VMEM_LIMIT = 56 * 1024 * 1024
ROW_TILE = 256
MM_TM, MM_TN, MM_TK = 512, 1536, 2816
MM_TM_T, MM_TK_T = 1408, 1024
GELU_C = math.sqrt(2.0 / math.pi)


def _cparams(sem):
    return pltpu.CompilerParams(dimension_semantics=sem, vmem_limit_bytes=VMEM_LIMIT)


def _tile(n, target, mult=LANES):
    t = (min(n, target) // mult) * mult
    while t >= mult:
        if n % t == 0:
            return t
        t -= mult
    return n


def _gelu(x):
    t = jnp.tanh(GELU_C * (x + 0.044715 * x * x * x))
    return 0.5 * x * (1.0 + t)


def _gelu_and_grad(x):
    x2 = x * x
    t = jnp.tanh(GELU_C * (x + 0.044715 * x * x2))
    g = 0.5 * x * (1.0 + t)
    dg = 0.5 * (1.0 + t) + 0.5 * x * (1.0 - t * t) * GELU_C * (1.0 + 3.0 * 0.044715 * x2)
    return g, dg


def _sigmoid(x):
    return 1.0 / (1.0 + jnp.exp(-x))


def _shift_rows(x, d, fill_rows):
    ext = jnp.concatenate([fill_rows, x], axis=0)
    return pltpu.roll(ext, d, 0)[8:]


def _shift_rows_up(x, d, fill_rows):
    n = x.shape[0]
    ext = jnp.concatenate([x, fill_rows], axis=0)
    return pltpu.roll(ext, n + 8 - d, 0)[:n]


def _dot(a, b, dims):
    return lax.dot_general(a.astype(BF16), b.astype(BF16), (dims, ((), ())), preferred_element_type=F32)


def _dot_nn(a, b):
    return _dot(a, b, ((1,), (0,)))


def _dot_nt(a, b):
    return _dot(a, b, ((1,), (1,)))


def _dot_tn(a, b):
    return _dot(a, b, ((0,), (0,)))


def _mm(a, b, *, name, ta=False, tb=False, res=None, out_dtype=F32):
    if ta:
        K, M = a.shape
    else:
        M, K = a.shape
    N = b.shape[0] if tb else b.shape[1]
    tm = _tile(M, MM_TM_T if ta else MM_TM, LANES if ta else 8)
    tn = _tile(N, MM_TN, LANES)
    tk = _tile(K, MM_TK_T if ta else MM_TK, LANES)
    nk = K // tk
    a_spec = pl.BlockSpec((tk, tm), lambda i, j, k: (k, i)) if ta else pl.BlockSpec((tm, tk), lambda i, j, k: (i, k))
    b_spec = pl.BlockSpec((tn, tk), lambda i, j, k: (j, k)) if tb else pl.BlockSpec((tk, tn), lambda i, j, k: (k, j))
    o_spec = pl.BlockSpec((tm, tn), lambda i, j, k: (i, j))
    dims = ((0,) if ta else (1,), (1,) if tb else (0,))
    has_res = res is not None

    def body(*refs):
        a_ref, b_ref = refs[:2]
        r_ref = refs[2] if has_res else None
        o_ref = refs[3] if has_res else refs[2]
        p = _dot(a_ref[...], b_ref[...], dims)

        def finish(r):
            if has_res:
                r = r + r_ref[...].astype(F32)
            o_ref[...] = r.astype(out_dtype)

        if nk == 1:
            finish(p)
            return
        acc_ref = refs[-1]
        k = pl.program_id(2)

        @pl.when(k == 0)
        def _():
            acc_ref[...] = p

        @pl.when(jnp.logical_and(k > 0, k < nk - 1))
        def _():
            acc_ref[...] += p

        @pl.when(k == nk - 1)
        def _():
            finish(acc_ref[...] + p)

    in_specs = [a_spec, b_spec] + ([o_spec] if has_res else [])
    args = (a, b) + ((res,) if has_res else ())
    return pl.pallas_call(
        body, name=name, grid=(M // tm, N // tn, nk), in_specs=in_specs, out_specs=o_spec,
        out_shape=jax.ShapeDtypeStruct((M, N), out_dtype),
        scratch_shapes=[pltpu.VMEM((tm, tn), F32)] if nk > 1 else [],
        compiler_params=_cparams(("parallel", "parallel", "arbitrary")),
    )(*args)


def _rms_fwd(x, g, *, name, cb=0, out_dtype=BF16):
    T = x.shape[0]
    W = g.shape[-1]
    g = g.reshape(1, W)
    tt = ROW_TILE

    def body(x_ref, g_ref, o_ref):
        xf = x_ref[...].astype(F32)
        rstd = lax.rsqrt(jnp.mean(xf * xf, axis=-1, keepdims=True) + NORM_EPS)
        o_ref[...] = (xf * rstd * g_ref[...]).astype(out_dtype)

    return pl.pallas_call(
        body, name=name, grid=(T // tt,),
        in_specs=[pl.BlockSpec((tt, W), lambda i: (i, cb)), pl.BlockSpec((1, W), lambda i: (0, 0))],
        out_specs=pl.BlockSpec((tt, W), lambda i: (i, 0)),
        out_shape=jax.ShapeDtypeStruct((T, W), out_dtype),
        compiler_params=_cparams(("parallel",)),
    )(x, g)


def _rms_bwd(x, g, dy, *, name, cb=0, res=None, out_dtype=F32):
    T = x.shape[0]
    W = g.shape[-1]
    g = g.reshape(1, W)
    tt = ROW_TILE
    has_res = res is not None

    def body(*refs):
        if has_res:
            x_ref, g_ref, dy_ref, r_ref, dx_ref, dg_ref = refs
        else:
            x_ref, g_ref, dy_ref, dx_ref, dg_ref = refs
        xf = x_ref[...].astype(F32)
        dyf = dy_ref[...].astype(F32)
        rstd = lax.rsqrt(jnp.mean(xf * xf, axis=-1, keepdims=True) + NORM_EPS)
        xhat = xf * rstd
        dxhat = dyf * g_ref[...]
        dx = rstd * (dxhat - xhat * jnp.mean(dxhat * xhat, axis=-1, keepdims=True))
        if has_res:
            dx = dx + r_ref[...].astype(F32)
        dx_ref[...] = dx.astype(out_dtype)
        part = jnp.sum(dyf * xhat, axis=0, keepdims=True)

        @pl.when(pl.program_id(0) == 0)
        def _():
            dg_ref[...] = part

        @pl.when(pl.program_id(0) > 0)
        def _():
            dg_ref[...] += part

    row = pl.BlockSpec((tt, W), lambda i: (i, 0))
    in_specs = [pl.BlockSpec((tt, W), lambda i: (i, cb)), pl.BlockSpec((1, W), lambda i: (0, 0)), row]
    args = (x, g, dy)
    if has_res:
        in_specs.append(row)
        args = args + (res,)
    return pl.pallas_call(
        body, name=name, grid=(T // tt,), in_specs=in_specs,
        out_specs=[row, pl.BlockSpec((1, W), lambda i: (0, 0))],
        out_shape=[jax.ShapeDtypeStruct((T, W), out_dtype), jax.ShapeDtypeStruct((1, W), F32)],
        compiler_params=_cparams(("arbitrary",)),
    )(*args)


def _final_fwd_bwd(h, g, target, *, name):
    T, W = h.shape
    g = g.reshape(1, W)
    tt = ROW_TILE

    def body(x_ref, g_ref, t_ref, loss_ref, dx_ref, dg_ref):
        xf = x_ref[...]
        rstd = lax.rsqrt(jnp.mean(xf * xf, axis=-1, keepdims=True) + NORM_EPS)
        xhat = xf * rstd
        err = xhat * g_ref[...] - t_ref[...]
        lpart = jnp.zeros((1, LANES), F32) + (0.5 / W) * jnp.sum(err * err)
        dyf = err * (1.0 / W)
        dxhat = dyf * g_ref[...]
        dx_ref[...] = rstd * (dxhat - xhat * jnp.mean(dxhat * xhat, axis=-1, keepdims=True))
        part = jnp.sum(dyf * xhat, axis=0, keepdims=True)

        @pl.when(pl.program_id(0) == 0)
        def _():
            dg_ref[...] = part
            loss_ref[...] = lpart

        @pl.when(pl.program_id(0) > 0)
        def _():
            dg_ref[...] += part
            loss_ref[...] += lpart

    row = pl.BlockSpec((tt, W), lambda i: (i, 0))
    return pl.pallas_call(
        body, name=name, grid=(T // tt,),
        in_specs=[row, pl.BlockSpec((1, W), lambda i: (0, 0)), row],
        out_specs=[pl.BlockSpec((1, LANES), lambda i: (0, 0)), row, pl.BlockSpec((1, W), lambda i: (0, 0))],
        out_shape=[jax.ShapeDtypeStruct((1, LANES), F32), jax.ShapeDtypeStruct((T, W), F32),
                   jax.ShapeDtypeStruct((1, W), F32)],
        compiler_params=_cparams(("arbitrary",)),
    )(h, g, target)


def _swap16(x):
    lane = lax.broadcasted_iota(jnp.int32, x.shape, 1)
    return jnp.where((lane % 32) < 16, pltpu.roll(x, LANES - 16, 1), pltpu.roll(x, 16, 1))


def _rope_fwd(x, cos, sin, *, name, cb0, ncb):
    T = x.shape[0]
    tt = ROW_TILE

    def body(x_ref, c_ref, s_ref, o_ref):
        xf = x_ref[...]
        o_ref[...] = xf * c_ref[...] + _swap16(xf) * s_ref[...]

    tab = pl.BlockSpec((tt, LANES), lambda i, j: (i, 0))
    return pl.pallas_call(
        body, name=name, grid=(T // tt, ncb),
        in_specs=[pl.BlockSpec((tt, LANES), lambda i, j: (i, cb0 + j)), tab, tab],
        out_specs=pl.BlockSpec((tt, LANES), lambda i, j: (i, j)),
        out_shape=jax.ShapeDtypeStruct((T, ncb * LANES), F32),
        compiler_params=_cparams(("parallel", "parallel")),
    )(x, cos, sin)


def _rope_bwd(dy, cos, sin, *, name, head_sum):
    T = dy.shape[0]
    tt = ROW_TILE
    ncb = 1 if head_sum else dy.shape[1] // LANES

    def body(d_ref, c_ref, s_ref, o_ref):
        d = d_ref[...]
        if head_sum:
            d = d[:, :LANES] + d[:, LANES:]
            d = d + pltpu.roll(d, 64, 1)
            d = d + pltpu.roll(d, 96, 1)
            lane = lax.broadcasted_iota(jnp.int32, d.shape, 1)
            d = jnp.where(lane < QK_ROPE, d, 0.0)
        o_ref[...] = d * c_ref[...] + _swap16(d * s_ref[...])

    tab = pl.BlockSpec((tt, LANES), lambda i, j: (i, 0))
    d_spec = pl.BlockSpec((tt, 2 * LANES), lambda i, j: (i, 0)) if head_sum else pl.BlockSpec((tt, LANES), lambda i, j: (i, j))
    return pl.pallas_call(
        body, name=name, grid=(T // tt, ncb),
        in_specs=[d_spec, tab, tab],
        out_specs=pl.BlockSpec((tt, LANES), lambda i, j: (i, j)),
        out_shape=jax.ShapeDtypeStruct((T, ncb * LANES), F32),
        compiler_params=_cparams(("parallel", "parallel")),
    )(dy, cos, sin)


ATT_BLOCK = 256


def _attn_scale():
    return float((QK_NOPE + QK_ROPE) ** -0.5)


def _causal_mask(qi, kj, tq, tk):
    row = qi * tq + lax.broadcasted_iota(jnp.int32, (tq, tk), 0)
    col = kj * tk + lax.broadcasted_iota(jnp.int32, (tq, tk), 1)
    return col <= row


def _flash_fwd(q, k, v, *, name):
    BH, S, DK = q.shape
    DV = v.shape[-1]
    tq = tk = min(ATT_BLOCK, S)
    scale = _attn_scale()

    def body(q_ref, k_ref, v_ref, o_ref, lse_ref):
        qi = pl.program_id(1)
        qb = q_ref[0]

        def step(j, carry):
            m, l, acc = carry
            kb = k_ref[0, pl.ds(pl.multiple_of(j * tk, tk), tk), :]
            vb = v_ref[0, pl.ds(pl.multiple_of(j * tk, tk), tk), :]
            s = _dot_nt(qb, kb) * scale
            s = jnp.where(_causal_mask(qi, j, tq, tk), s, -jnp.inf)
            m_new = jnp.maximum(m, jnp.max(s, axis=-1, keepdims=True))
            alpha = jnp.exp(m - m_new)
            p = jnp.exp(s - m_new)
            l = alpha * l + jnp.sum(p, axis=-1, keepdims=True)
            acc = alpha * acc + _dot_nn(p, vb)
            return m_new, l, acc

        init = (jnp.full((tq, 1), -jnp.inf, F32), jnp.zeros((tq, 1), F32), jnp.zeros((tq, DV), F32))
        m, l, acc = lax.fori_loop(0, qi + 1, step, init)
        o_ref[0] = acc / l
        lse_ref[0] = m + jnp.log(l)

    return pl.pallas_call(
        body, name=name, grid=(BH, S // tq),
        in_specs=[pl.BlockSpec((1, tq, DK), lambda b, i: (b, i, 0)),
                  pl.BlockSpec((1, S, DK), lambda b, i: (b, 0, 0)),
                  pl.BlockSpec((1, S, DV), lambda b, i: (b, 0, 0))],
        out_specs=[pl.BlockSpec((1, tq, DV), lambda b, i: (b, i, 0)),
                   pl.BlockSpec((1, tq, 1), lambda b, i: (b, i, 0))],
        out_shape=[jax.ShapeDtypeStruct((BH, S, DV), F32), jax.ShapeDtypeStruct((BH, S, 1), F32)],
        compiler_params=_cparams(("parallel", "parallel")),
    )(q, k, v)


def _flash_bwd_dq(q, k, v, o, lse, do, *, name):
    BH, S, DK = q.shape
    DV = v.shape[-1]
    tq = tk = min(ATT_BLOCK, S)
    scale = _attn_scale()

    def body(q_ref, k_ref, v_ref, o_ref, lse_ref, do_ref, dq_ref):
        qi = pl.program_id(1)
        qb = q_ref[0]
        dob = do_ref[0]
        lse_b = lse_ref[0]
        delta = jnp.sum(dob * o_ref[0], axis=-1, keepdims=True)

        def step(j, dq):
            kb = k_ref[0, pl.ds(pl.multiple_of(j * tk, tk), tk), :]
            vb = v_ref[0, pl.ds(pl.multiple_of(j * tk, tk), tk), :]
            s = _dot_nt(qb, kb) * scale
            p = jnp.where(_causal_mask(qi, j, tq, tk), jnp.exp(s - lse_b), 0.0)
            dp = _dot_nt(dob, vb)
            ds = p * (dp - delta) * scale
            return dq + _dot_nn(ds, kb)

        dq_ref[0] = lax.fori_loop(0, qi + 1, step, jnp.zeros((tq, DK), F32))

    qspec = lambda w: pl.BlockSpec((1, tq, w), lambda b, i: (b, i, 0))
    full = lambda w: pl.BlockSpec((1, S, w), lambda b, i: (b, 0, 0))
    return pl.pallas_call(
        body, name=name, grid=(BH, S // tq),
        in_specs=[qspec(DK), full(DK), full(DV), qspec(DV), qspec(1), qspec(DV)],
        out_specs=qspec(DK),
        out_shape=jax.ShapeDtypeStruct((BH, S, DK), F32),
        compiler_params=_cparams(("parallel", "parallel")),
    )(q, k, v, o, lse, do)


def _flash_bwd_dkv(q, k, v, o, lse, do, *, name):
    BH, S, DK = q.shape
    DV = v.shape[-1]
    tq = tk = min(ATT_BLOCK, S)
    nq = S // tq
    scale = _attn_scale()

    def body(q_ref, k_ref, v_ref, o_ref, lse_ref, do_ref, dk_ref, dv_ref):
        kj = pl.program_id(1)
        kb = k_ref[0]
        vb = v_ref[0]

        def step(i, carry):
            dk, dv = carry
            rows = pl.ds(pl.multiple_of(i * tq, tq), tq)
            qb = q_ref[0, rows, :]
            dob = do_ref[0, rows, :]
            delta = jnp.sum(dob * o_ref[0, rows, :], axis=-1, keepdims=True)
            s = _dot_nt(qb, kb) * scale
            p = jnp.where(_causal_mask(i, kj, tq, tk), jnp.exp(s - lse_ref[0, rows, :]), 0.0)
            dp = _dot_nt(dob, vb)
            ds = p * (dp - delta) * scale
            return dk + _dot_tn(ds, qb), dv + _dot_tn(p, dob)

        dk, dv = lax.fori_loop(kj, nq, step, (jnp.zeros((tk, DK), F32), jnp.zeros((tk, DV), F32)))
        dk_ref[0] = dk
        dv_ref[0] = dv

    kspec = lambda w: pl.BlockSpec((1, tk, w), lambda b, j: (b, j, 0))
    full = lambda w: pl.BlockSpec((1, S, w), lambda b, j: (b, 0, 0))
    return pl.pallas_call(
        body, name=name, grid=(BH, S // tk),
        in_specs=[full(DK), kspec(DK), kspec(DV), full(DV), full(1), full(DV)],
        out_specs=[kspec(DK), kspec(DV)],
        out_shape=[jax.ShapeDtypeStruct((BH, S, DK), F32), jax.ShapeDtypeStruct((BH, S, DV), F32)],
        compiler_params=_cparams(("parallel", "parallel")),
    )(q, k, v, o, lse, do)


def _lru_gates(xl, halo, cw_ref, cb_ref, wa_ref, ba_ref, wx_ref, bx_ref, lam_ref):
    xc = cb_ref[...] + cw_ref[3:4, :] * xl
    for kk in range(LRU_CONV - 1):
        xc = xc + cw_ref[kk:kk + 1, :] * _shift_rows(xl, LRU_CONV - 1 - kk, halo)
    r = _sigmoid(_dot_nn(xc, wa_ref[...]) + ba_ref[...])
    i = _sigmoid(_dot_nn(xc, wx_ref[...]) + bx_ref[...])
    lam = lam_ref[...]
    sp = jnp.maximum(-lam, 0.0) + jnp.log(1.0 + jnp.exp(-jnp.abs(lam)))
    a = jnp.exp(-LRU_C * r * sp)
    mult = jnp.sqrt(1.0 - a * a)
    return xc, r, i, sp, a, mult


def _lru_specs(tt, nt, S):
    def make(rev):
        tmap = (lambda t: nt - 1 - t) if rev else (lambda t: t)
        tile = lambda cb: pl.BlockSpec((tt, LRU_WIDTH), lambda b, t: (b * nt + tmap(t), cb))
        prev8 = lambda cb: pl.BlockSpec(
            (8, LRU_WIDTH), lambda b, t: (jnp.maximum((b * nt + tmap(t)) * (tt // 8) - 1, 0), cb))
        return tile, prev8, tmap
    return make


def _lru_fwd(z, cw, cb, wa, ba, wx, bx, lam, *, S, name):
    T = z.shape[0]
    tt = min(ROW_TILE, S)
    nt = S // tt
    tile, prev8, _ = _lru_specs(tt, nt, S)(False)
    vec = lambda r: pl.BlockSpec((r, LRU_WIDTH), lambda b, t: (0, 0))
    mat = pl.BlockSpec((LRU_WIDTH, LRU_WIDTH), lambda b, t: (0, 0))

    def body(xl_ref, halo_ref, gate_ref, cw_ref, cb_ref, wa_ref, ba_ref, wx_ref, bx_ref, lam_ref,
             y_ref, h_ref, carry_ref):
        t = pl.program_id(1)
        first = t == 0
        halo = jnp.where(first, 0.0, halo_ref[...])
        xl_t = xl_ref[...]
        xc, r, i, sp, a, mult = _lru_gates(xl_t, halo, cw_ref, cb_ref, wa_ref, ba_ref, wx_ref, bx_ref, lam_ref)
        bv = mult * (i * xc)
        ones = jnp.ones((8, LRU_WIDTH), F32)
        zeros = jnp.zeros((8, LRU_WIDTH), F32)
        row = lax.broadcasted_iota(jnp.int32, (tt, LRU_WIDTH), 0)
        A = a
        d = 1
        while d < tt:
            if d < 8:
                a_sh = _shift_rows(A, d, ones)
                b_sh = _shift_rows(bv, d, zeros)
            else:
                a_sh = jnp.where(row < d, 1.0, pltpu.roll(A, d, 0))
                b_sh = jnp.where(row < d, 0.0, pltpu.roll(bv, d, 0))
            bv = A * b_sh + bv
            A = A * a_sh
            d *= 2
        h0 = jnp.where(first, 0.0, carry_ref[0:1, :])
        h = A * h0 + bv
        carry_ref[...] = jnp.broadcast_to(h[tt - 1:tt, :], (8, LRU_WIDTH))
        h_ref[...] = h
        y_ref[...] = (h * _gelu(gate_ref[...])).astype(BF16)

    return pl.pallas_call(
        body, name=name, grid=(T // S, nt),
        in_specs=[tile(0), prev8(0), tile(1), vec(LRU_CONV), vec(1), mat, vec(1), mat, vec(1), vec(1)],
        out_specs=[tile(0), tile(0)],
        out_shape=[jax.ShapeDtypeStruct((T, LRU_WIDTH), BF16), jax.ShapeDtypeStruct((T, LRU_WIDTH), F32)],
        scratch_shapes=[pltpu.VMEM((8, LRU_WIDTH), F32)],
        compiler_params=_cparams(("arbitrary", "arbitrary")),
    )(z, z, z, cw, cb, wa, ba, wx, bx, lam)


def _lru_bwd(z, h, dy, cw, cb, wa, ba, wx, bx, lam, *, S, name):
    T = z.shape[0]
    tt = min(ROW_TILE, S)
    nt = S // tt
    tile, prev8, tmap = _lru_specs(tt, nt, S)(True)
    vec = lambda r: pl.BlockSpec((r, LRU_WIDTH), lambda b, t: (0, 0))
    mat = pl.BlockSpec((LRU_WIDTH, LRU_WIDTH), lambda b, t: (0, 0))

    def body(xl_ref, halo_ref, gate_ref, h_ref, hprev_ref, dy_ref, cw_ref, cb_ref, wa_ref, ba_ref, wx_ref,
             bx_ref, lam_ref, dxl_ref, dgate_ref, dcw_ref, dcb_ref, dwa_ref, dba_ref, dwx_ref, dbx_ref,
             dlam_ref, lamc_ref, ac_ref, dxc_ref):
        b = pl.program_id(0)
        t = pl.program_id(1)
        tr = nt - 1 - t
        seq_first = tr == 0
        seq_last = t == 0
        halo = jnp.where(seq_first, 0.0, halo_ref[...])
        xl_t = xl_ref[...]
        xc, r, i, sp, a, mult = _lru_gates(xl_t, halo, cw_ref, cb_ref, wa_ref, ba_ref, wx_ref, bx_ref, lam_ref)
        hh = h_ref[...]
        dyf = dy_ref[...].astype(F32)
        gl, dgl = _gelu_and_grad(gate_ref[...])
        dgate_ref[...] = (dyf * hh * dgl).astype(BF16)
        dh = dyf * gl

        a_first_later = jnp.where(seq_last, 0.0, ac_ref[...])
        lam_later = jnp.where(seq_last, 0.0, lamc_ref[...])
        row = lax.broadcasted_iota(jnp.int32, (tt, LRU_WIDTH), 0)
        A = _shift_rows_up(a, 1, a_first_later)
        lm = dh
        ones = jnp.ones((8, LRU_WIDTH), F32)
        zeros = jnp.zeros((8, LRU_WIDTH), F32)
        d = 1
        while d < tt:
            if d < 8:
                a_sh = _shift_rows_up(A, d, ones)
                l_sh = _shift_rows_up(lm, d, zeros)
            else:
                a_sh = jnp.where(row >= tt - d, 1.0, pltpu.roll(A, tt - d, 0))
                l_sh = jnp.where(row >= tt - d, 0.0, pltpu.roll(lm, tt - d, 0))
            lm = lm + A * l_sh
            A = A * a_sh
            d *= 2
        lm = lm + A * lam_later[0:1, :]
        lamc_ref[...] = jnp.broadcast_to(lm[0:1, :], (8, LRU_WIDTH))
        ac_ref[...] = jnp.broadcast_to(a[0:1, :], (8, LRU_WIDTH))

        hprev_halo = jnp.where(seq_first, 0.0, hprev_ref[...])
        h_prev = _shift_rows(hh, 1, hprev_halo)
        da = lm * h_prev
        ixc = i * xc
        dmult = lm * ixc
        di = lm * mult * xc
        dxc = lm * mult * i
        da = da - dmult * a / mult
        dlog = da * a
        dr = dlog * (-LRU_C) * sp
        dsp_part = jnp.sum(dlog * (-LRU_C) * r, axis=0, keepdims=True)
        dpa = dr * r * (1.0 - r)
        dpx = di * i * (1.0 - i)
        dxc = dxc + _dot_nt(dpa, wa_ref[...]) + _dot_nt(dpx, wx_ref[...])
        dwa_part = _dot_tn(xc, dpa)
        dwx_part = _dot_tn(xc, dpx)

        later = jnp.where(seq_last, 0.0, dxc_ref[...])
        dxl = cw_ref[3:4, :] * dxc
        for kk in range(LRU_CONV - 1):
            dxl = dxl + cw_ref[kk:kk + 1, :] * _shift_rows_up(dxc, LRU_CONV - 1 - kk, later)
        dxl_ref[...] = dxl.astype(BF16)
        dxc_ref[...] = dxc[0:8, :]
        dcw_rows = [jnp.sum(dxc * _shift_rows(xl_t, LRU_CONV - 1 - kk, halo), axis=0, keepdims=True)
                    for kk in range(LRU_CONV - 1)]
        dcw_rows.append(jnp.sum(dxc * xl_t, axis=0, keepdims=True))
        dcw_part = jnp.concatenate(dcw_rows + [jnp.zeros((8 - LRU_CONV, LRU_WIDTH), F32)], axis=0)
        lamv = lam_ref[...]
        dlam_part = dsp_part * (-_sigmoid(-lamv))
        parts = ((dcw_ref, dcw_part), (dcb_ref, jnp.sum(dxc, axis=0, keepdims=True)),
                 (dwa_ref, dwa_part), (dba_ref, jnp.sum(dpa, axis=0, keepdims=True)),
                 (dwx_ref, dwx_part), (dbx_ref, jnp.sum(dpx, axis=0, keepdims=True)),
                 (dlam_ref, dlam_part))
        start = jnp.logical_and(b == 0, t == 0)

        @pl.when(start)
        def _():
            for ref, val in parts:
                ref[...] = val

        @pl.when(jnp.logical_not(start))
        def _():
            for ref, val in parts:
                ref[...] += val

    acc = lambda r: pl.BlockSpec((r, LRU_WIDTH), lambda b, t: (0, 0))
    return pl.pallas_call(
        body, name=name, grid=(T // S, nt),
        in_specs=[tile(0), prev8(0), tile(1), tile(0), prev8(0), tile(1),
                  vec(LRU_CONV), vec(1), mat, vec(1), mat, vec(1), vec(1)],
        out_specs=[tile(0), tile(0), acc(8), acc(1), mat, acc(1), mat, acc(1), acc(1)],
        out_shape=[jax.ShapeDtypeStruct((T, LRU_WIDTH), BF16), jax.ShapeDtypeStruct((T, LRU_WIDTH), BF16),
                   jax.ShapeDtypeStruct((8, LRU_WIDTH), F32), jax.ShapeDtypeStruct((1, LRU_WIDTH), F32),
                   jax.ShapeDtypeStruct((LRU_WIDTH, LRU_WIDTH), F32), jax.ShapeDtypeStruct((1, LRU_WIDTH), F32),
                   jax.ShapeDtypeStruct((LRU_WIDTH, LRU_WIDTH), F32), jax.ShapeDtypeStruct((1, LRU_WIDTH), F32),
                   jax.ShapeDtypeStruct((1, LRU_WIDTH), F32)],
        scratch_shapes=[pltpu.VMEM((8, LRU_WIDTH), F32), pltpu.VMEM((8, LRU_WIDTH), F32),
                        pltpu.VMEM((8, LRU_WIDTH), F32)],
        compiler_params=_cparams(("arbitrary", "arbitrary")),
    )(z, z, z, h, h, dy, cw, cb, wa, ba, wx, bx, lam)


FFN_CT = 1408


def _ffn_conv(g, halo, cw_ref, cb_ref):
    gc = cb_ref[...] + cw_ref[2:3, :] * g
    for kk in range(FFN_CONV - 1):
        gc = gc + cw_ref[kk:kk + 1, :] * _shift_rows(g, FFN_CONV - 1 - kk, halo)
    return gc


def _ffn_act_fwd(g, u, cw, cb, *, S, name):
    T, F = g.shape
    tt = min(ROW_TILE, S)
    nt = S // tt
    tc = _tile(F, FFN_CT)

    def body(g_ref, halo_ref, u_ref, cw_ref, cb_ref, o_ref):
        first = (pl.program_id(0) % nt) == 0
        halo = jnp.where(first, 0.0, halo_ref[...])
        gc = _ffn_conv(g_ref[...], halo, cw_ref, cb_ref)
        o_ref[...] = (_gelu(gc) * u_ref[...]).astype(BF16)

    tile = pl.BlockSpec((tt, tc), lambda i, j: (i, j))
    prev8 = pl.BlockSpec((8, tc), lambda i, j: (jnp.maximum(i * (tt // 8) - 1, 0), j))
    return pl.pallas_call(
        body, name=name, grid=(T // tt, F // tc),
        in_specs=[tile, prev8, tile, pl.BlockSpec((FFN_CONV, tc), lambda i, j: (0, j)),
                  pl.BlockSpec((1, tc), lambda i, j: (0, j))],
        out_specs=tile,
        out_shape=jax.ShapeDtypeStruct((T, F), BF16),
        compiler_params=_cparams(("parallel", "parallel")),
    )(g, g, u, cw, cb)


def _ffn_act_bwd(g, u, dact, cw, cb, *, S, name):
    T, F = g.shape
    tt = min(ROW_TILE, S)
    nt = S // tt
    ntt = T // tt
    tc = _tile(F, FFN_CT)

    def body(g_ref, halo_ref, u_ref, da_ref, cw_ref, cb_ref, dg_ref, du_ref, dcw_ref, dcb_ref, later_ref):
        step = pl.program_id(1)
        ti = (ntt - 1 - step) % nt
        halo = jnp.where(ti == 0, 0.0, halo_ref[...])
        gt = g_ref[...]
        gc = _ffn_conv(gt, halo, cw_ref, cb_ref)
        gl, dgl = _gelu_and_grad(gc)
        da = da_ref[...].astype(F32)
        du_ref[...] = (da * gl).astype(BF16)
        dgc = da * u_ref[...] * dgl
        later = jnp.where(ti == nt - 1, 0.0, later_ref[...])
        dg = cw_ref[2:3, :] * dgc
        for kk in range(FFN_CONV - 1):
            dg = dg + cw_ref[kk:kk + 1, :] * _shift_rows_up(dgc, FFN_CONV - 1 - kk, later)
        dg_ref[...] = dg.astype(BF16)
        later_ref[...] = dgc[0:8, :]
        rows = [jnp.sum(dgc * _shift_rows(gt, FFN_CONV - 1 - kk, halo), axis=0, keepdims=True)
                for kk in range(FFN_CONV - 1)]
        rows.append(jnp.sum(dgc * gt, axis=0, keepdims=True))
        dcw_part = jnp.concatenate(rows + [jnp.zeros((8 - FFN_CONV, tc), F32)], axis=0)
        dcb_part = jnp.sum(dgc, axis=0, keepdims=True)

        @pl.when(step == 0)
        def _():
            dcw_ref[...] = dcw_part
            dcb_ref[...] = dcb_part

        @pl.when(step > 0)
        def _():
            dcw_ref[...] += dcw_part
            dcb_ref[...] += dcb_part

    tile = pl.BlockSpec((tt, tc), lambda j, s: (ntt - 1 - s, j))
    prev8 = pl.BlockSpec((8, tc), lambda j, s: (jnp.maximum((ntt - 1 - s) * (tt // 8) - 1, 0), j))
    return pl.pallas_call(
        body, name=name, grid=(F // tc, ntt),
        in_specs=[tile, prev8, tile, tile, pl.BlockSpec((FFN_CONV, tc), lambda j, s: (0, j)),
                  pl.BlockSpec((1, tc), lambda j, s: (0, j))],
        out_specs=[tile, tile, pl.BlockSpec((8, tc), lambda j, s: (0, j)), pl.BlockSpec((1, tc), lambda j, s: (0, j))],
        out_shape=[jax.ShapeDtypeStruct((T, F), BF16), jax.ShapeDtypeStruct((T, F), BF16),
                   jax.ShapeDtypeStruct((8, F), F32), jax.ShapeDtypeStruct((1, F), F32)],
        scratch_shapes=[pltpu.VMEM((8, tc), F32)],
        compiler_params=_cparams(("arbitrary", "arbitrary")),
    )(g, g, u, dact, cw, cb)


def _sgu_norm(zv, g_ref, b_ref):
    v = _gelu(zv)
    mu = jnp.mean(v, axis=-1, keepdims=True)
    xc = v - mu
    rstd = lax.rsqrt(jnp.mean(xc * xc, axis=-1, keepdims=True) + NORM_EPS)
    xhat = xc * rstd
    return xhat, rstd, xhat * g_ref[...] + b_ref[...]


def _sgu_fwd(zc, ln_g, ln_b, wm, bmap, *, name):
    T = zc.shape[0]
    W = SGU_WIDTH
    tt = ROW_TILE
    nch = tt // CHUNK

    def body(z_ref, g_ref, b_ref, wm_ref, bm_ref, p_ref):
        u = _gelu(z_ref[:, :W])
        _, _, vn = _sgu_norm(z_ref[:, W:], g_ref, b_ref)
        vn = vn.astype(BF16)
        for n in range(nch):
            rows = slice(n * CHUNK, (n + 1) * CHUNK)
            for gi in range(SGU_GROUPS):
                cols = slice(gi * LANES, (gi + 1) * LANES)
                s = _dot_nn(wm_ref[gi], vn[rows, cols]) + bm_ref[:, cols]
                p_ref[rows, cols] = (u[rows, cols] * s).astype(BF16)

    const2 = lambda r, c: pl.BlockSpec((r, c), lambda i: (0, 0))
    return pl.pallas_call(
        body, name=name, grid=(T // tt,),
        in_specs=[pl.BlockSpec((tt, 2 * W), lambda i: (i, 0)), const2(1, W), const2(1, W),
                  pl.BlockSpec((SGU_GROUPS, CHUNK, CHUNK), lambda i: (0, 0, 0)), const2(CHUNK, W)],
        out_specs=pl.BlockSpec((tt, W), lambda i: (i, 0)),
        out_shape=jax.ShapeDtypeStruct((T, W), BF16),
        compiler_params=_cparams(("parallel",)),
    )(zc, ln_g, ln_b, wm, bmap)


def _sgu_bwd(zc, dp, ln_g, ln_b, wm, bmap, *, name):
    T = zc.shape[0]
    W = SGU_WIDTH
    tt = ROW_TILE
    nch = tt // CHUNK
    nsteps = T // tt

    def body(z_ref, dp_ref, g_ref, b_ref, wm_ref, bm_ref, dz_ref, dg_ref, db_ref, dwm_ref, dbm_ref,
             s_scr, dvn_scr):
        step = pl.program_id(0)
        zu = z_ref[:, :W]
        zv = z_ref[:, W:]
        u, dgu = _gelu_and_grad(zu)
        xhat, rstd, vn = _sgu_norm(zv, g_ref, b_ref)
        vnb = vn.astype(BF16)
        dpf = dp_ref[...].astype(F32)
        ds = dpf * u

        @pl.when(step == 0)
        def _():
            dwm_ref[...] = jnp.zeros_like(dwm_ref)
            dbm_ref[...] = jnp.zeros_like(dbm_ref)

        for n in range(nch):
            rows = slice(n * CHUNK, (n + 1) * CHUNK)
            for gi in range(SGU_GROUPS):
                cols = slice(gi * LANES, (gi + 1) * LANES)
                s_scr[rows, cols] = _dot_nn(wm_ref[gi], vnb[rows, cols]) + bm_ref[:, cols]
                dsb = ds[rows, cols]
                dvn_scr[rows, cols] = _dot_tn(wm_ref[gi], dsb)
                dwm_ref[gi] += _dot_nt(dsb, vnb[rows, cols])
                dbm_ref[:, cols] += dsb
        dz_ref[:, :W] = (dpf * s_scr[...] * dgu).astype(BF16)
        dvn = dvn_scr[...]
        dxhat = dvn * g_ref[...]
        dv = rstd * (dxhat - jnp.mean(dxhat, axis=-1, keepdims=True)
                     - xhat * jnp.mean(dxhat * xhat, axis=-1, keepdims=True))
        _, dgv = _gelu_and_grad(zv)
        dz_ref[:, W:] = (dv * dgv).astype(BF16)
        dg_part = jnp.sum(dvn * xhat, axis=0, keepdims=True)
        db_part = jnp.sum(dvn, axis=0, keepdims=True)

        @pl.when(step == 0)
        def _():
            dg_ref[...] = dg_part
            db_ref[...] = db_part

        @pl.when(step > 0)
        def _():
            dg_ref[...] += dg_part
            db_ref[...] += db_part

        @pl.when(step == nsteps - 1)
        def _():
            for gi in range(SGU_GROUPS):
                cols = slice(gi * LANES, (gi + 1) * LANES)
                tot = jnp.sum(dbm_ref[:, cols], axis=1, keepdims=True)
                dbm_ref[:, cols] = jnp.broadcast_to(tot, (CHUNK, LANES))

    const2 = lambda r, c: pl.BlockSpec((r, c), lambda i: (0, 0))
    wspec = pl.BlockSpec((SGU_GROUPS, CHUNK, CHUNK), lambda i: (0, 0, 0))
    return pl.pallas_call(
        body, name=name, grid=(nsteps,),
        in_specs=[pl.BlockSpec((tt, 2 * W), lambda i: (i, 0)), pl.BlockSpec((tt, W), lambda i: (i, 0)),
                  const2(1, W), const2(1, W), wspec, const2(CHUNK, W)],
        out_specs=[pl.BlockSpec((tt, 2 * W), lambda i: (i, 0)), const2(1, W), const2(1, W), wspec, const2(CHUNK, W)],
        out_shape=[jax.ShapeDtypeStruct((T, 2 * W), BF16), jax.ShapeDtypeStruct((1, W), F32),
                   jax.ShapeDtypeStruct((1, W), F32), jax.ShapeDtypeStruct((SGU_GROUPS, CHUNK, CHUNK), F32),
                   jax.ShapeDtypeStruct((CHUNK, W), F32)],
        scratch_shapes=[pltpu.VMEM((tt, W), F32), pltpu.VMEM((tt, W), F32)],
        compiler_params=_cparams(("arbitrary",)),
    )(zc, dp, ln_g, ln_b, wm, bmap)


def _rope_tables(positions):
    half = QK_ROPE // 2
    inv_freq = jnp.exp(-math.log(ROPE_BASE) * jnp.arange(half, dtype=F32) / half)
    ang = positions.reshape(-1).astype(F32)[:, None] * inv_freq
    cos = jnp.cos(ang)
    sin = jnp.sin(ang)
    reps = LANES // QK_ROPE
    return jnp.tile(jnp.concatenate([cos, cos], axis=1), (1, reps)), jnp.tile(jnp.concatenate([-sin, sin], axis=1), (1, reps))


def _to_heads(a, B, S, w):
    return a.reshape(B, S, MLA_HEADS, w).transpose(0, 2, 1, 3)


def _from_heads(a, B, S):
    w = a.shape[-1]
    return a.reshape(B, MLA_HEADS, S, w).transpose(0, 2, 1, 3).reshape(B * S, MLA_HEADS * w)


SGU_GROUP_DIM = SGU_WIDTH // SGU_GROUPS
_O1, _O2, _O3, _O4 = Q_LORA, Q_LORA + KV_LORA, Q_LORA + KV_LORA + QK_ROPE, Q_LORA + KV_LORA + QK_ROPE + LRU_WIDTH
_A0, _A1, _A2, _A3 = 2 * LRU_WIDTH, 2 * LRU_WIDTH + Q_LORA, 2 * LRU_WIDTH + Q_LORA + KV_LORA, 2 * LRU_WIDTH + Q_LORA + KV_LORA + QK_ROPE


def _perm_w_in(w_in):
    return jnp.concatenate([w_in[:, _O3:_O4], w_in[:, _O4:], w_in[:, :_O1], w_in[:, _O1:_O2], w_in[:, _O2:_O3],
                            jnp.zeros((w_in.shape[0], LANES - QK_ROPE), w_in.dtype)], axis=1)


def _unperm_w_in(w):
    return jnp.concatenate([w[:, _A0:_A1], w[:, _A1:_A2], w[:, _A2:_A3], w[:, :LRU_WIDTH], w[:, LRU_WIDTH:_A0]], axis=1)


def _perm_heads(w, d1, d2):
    r = w.shape[0]
    w3 = w.reshape(r, MLA_HEADS, d1 + d2)
    return jnp.concatenate([w3[:, :, :d1].reshape(r, -1), w3[:, :, d1:].reshape(r, -1)], axis=1)


def _unperm_heads(w, d1, d2):
    r = w.shape[0]
    n1 = MLA_HEADS * d1
    return jnp.concatenate([w[:, :n1].reshape(r, MLA_HEADS, d1), w[:, n1:].reshape(r, MLA_HEADS, d2)], axis=2).reshape(r, -1)


def _prep_small(w):
    p = {n: w[n] for n in w if n not in BIG}
    eye = jnp.eye(LRU_HEADS, dtype=F32)
    dense = lambda wg: (wg[:, :, None, :] * eye[:, None, :, None]).reshape(LRU_WIDTH, LRU_WIDTH).astype(BF16)
    p["wa_d"] = dense(w["ab_w_rg_a"][0])
    p["wx_d"] = dense(w["ab_w_rg_x"][0])
    causal = jnp.tril(jnp.ones((CHUNK, CHUNK), F32))
    p["wm"] = (w["c_w_s"][0] * causal).astype(BF16)
    p["bmap"] = jnp.repeat(w["c_b_s"][0].T, SGU_GROUP_DIM, axis=1)
    return p


def _prep_big(ab_w_in, ab_w_q_b, ab_w_kv_b):
    return {"w_in_p": _perm_w_in(ab_w_in).astype(BF16),
            "w_q_p": _perm_heads(ab_w_q_b, QK_NOPE, QK_ROPE).astype(BF16),
            "w_kv_p": _perm_heads(ab_w_kv_b, QK_NOPE, V_HEAD).astype(BF16)}


def _ffn_fwd(h, l, p, S):
    hn = _rms_fwd(h, p["ffn_norm"][l], name=f"ffn{l}_norm")
    g = _mm(hn, p["ffn_gate_t"][l], tb=True, name=f"ffn{l}_gate")
    u = _mm(hn, p["ffn_up_t"][l], tb=True, name=f"ffn{l}_up")
    act = _ffn_act_fwd(g, u, p["ffn_conv_w"][l], p["ffn_conv_b"][l][None], S=S, name=f"ffn{l}_act")
    out = _mm(act, p["ffn_down"][l], res=h, name=f"ffn{l}_down")
    return out, (hn, g, u, act)


def _ffn_bwd(dh, h_in, l, p, saved, S):
    hn, g, u, act = saved
    dact = _mm(dh, p["ffn_down"][l], tb=True, out_dtype=BF16, name=f"ffn{l}_dact")
    dw_down = _mm(act, dh, ta=True, out_dtype=BF16, name=f"ffn{l}_dwdown")
    dg, du, dcw, dcb = _ffn_act_bwd(g, u, dact, p["ffn_conv_w"][l], p["ffn_conv_b"][l][None], S=S, name=f"ffn{l}_dactbwd")
    dhn = _mm(dg, p["ffn_gate_t"][l], name=f"ffn{l}_dhn_g")
    dhn = _mm(du, p["ffn_up_t"][l], res=dhn, name=f"ffn{l}_dhn_u")
    dw_gate_t = _mm(dg, hn, ta=True, out_dtype=BF16, name=f"ffn{l}_dwgate")
    dw_up_t = _mm(du, hn, ta=True, out_dtype=BF16, name=f"ffn{l}_dwup")
    dh_in, dnorm = _rms_bwd(h_in, p["ffn_norm"][l], dhn, res=dh, name=f"ffn{l}_dnorm")
    grads = dict(ffn_norm=dnorm[0], ffn_gate_t=dw_gate_t, ffn_up_t=dw_up_t, ffn_conv_w=dcw[:FFN_CONV],
                 ffn_conv_b=dcb[0], ffn_down=dw_down)
    return dh_in, grads


def _local_step(x, positions, target, p):
    B, S, D = x.shape
    T = B * S
    H = MLA_HEADS
    xf = x.reshape(T, D)
    tgt = target.reshape(T, D)
    cos, sin = _rope_tables(positions)

    hn0 = _rms_fwd(xf, p["ab_norm"][0], name="ab_norm")
    z = _mm(hn0, p["w_in_p"], name="ab_in")
    cqn = _rms_fwd(z, p["ab_q_norm"][0], cb=4, name="q_norm")
    ckvn = _rms_fwd(z, p["ab_kv_norm"][0], cb=10, name="kv_norm")
    q = _mm(cqn, p["w_q_p"], name="q_up")
    kv = _mm(ckvn, p["w_kv_p"], name="kv_up")
    qr = _rope_fwd(q, cos, sin, cb0=4, ncb=2, name="q_rope")
    kr = _rope_fwd(z, cos, sin, cb0=11, ncb=1, name="k_rope")
    nope = H * QK_NOPE
    zpad = jnp.zeros((B, H, S, LANES - QK_NOPE - QK_ROPE), F32)
    q_h = jnp.concatenate([_to_heads(q[:, :nope], B, S, QK_NOPE), _to_heads(qr, B, S, QK_ROPE), zpad], axis=-1)
    k_h = jnp.concatenate([_to_heads(kv[:, :nope], B, S, QK_NOPE),
                           jnp.broadcast_to(kr[:, :QK_ROPE].reshape(B, 1, S, QK_ROPE), (B, H, S, QK_ROPE)), zpad], axis=-1)
    q_h = q_h.reshape(B * H, S, LANES).astype(BF16)
    k_h = k_h.reshape(B * H, S, LANES).astype(BF16)
    v_h = _to_heads(kv[:, nope:], B, S, V_HEAD).reshape(B * H, S, V_HEAD).astype(BF16)
    o_h, lse = _flash_fwd(q_h, k_h, v_h, name="attn_fwd")
    y_mla = _from_heads(o_h, B, S).astype(BF16)
    lru_par = (p["ab_conv_w"][0], p["ab_conv_b"], p["wa_d"], p["ab_b_rg_a"], p["wx_d"], p["ab_b_rg_x"], p["ab_lambda"])
    y_lru, hs = _lru_fwd(z, *lru_par, S=S, name="lru_fwd")
    mix = jnp.concatenate([y_mla, y_lru], axis=1)
    h1 = _mm(mix, p["ab_w_out"], res=xf, name="ab_out")
    h2, ffn0 = _ffn_fwd(h1, 0, p, S)

    hn2 = _rms_fwd(h2, p["c_norm"][0], name="c_norm")
    zc = _mm(hn2, p["c_w_in_t"], tb=True, name="c_in")
    pg = _sgu_fwd(zc, p["c_ln_g"], p["c_ln_b"], p["wm"], p["bmap"], name="sgu_fwd")
    h3 = _mm(pg, p["c_w_out"], res=h2, name="c_out")
    h4, ffn1 = _ffn_fwd(h3, 1, p, S)

    loss_row, dh4, dfinal = _final_fwd_bwd(h4, p["final_norm"], tgt, name="final")

    dh3, g_ffn1 = _ffn_bwd(dh4, h3, 1, p, ffn1, S)
    dpg = _mm(dh3, p["c_w_out"], tb=True, out_dtype=BF16, name="c_dp")
    dw_c_out = _mm(pg, dh3, ta=True, out_dtype=BF16, name="c_dwout")
    dzc, dlng, dlnb, dwm, dbm = _sgu_bwd(zc, dpg, p["c_ln_g"], p["c_ln_b"], p["wm"], p["bmap"], name="sgu_bwd")
    dhn2 = _mm(dzc, p["c_w_in_t"], name="c_dhn")
    dw_c_in_t = _mm(dzc, hn2, ta=True, out_dtype=BF16, name="c_dwin")
    dh2, dcnorm = _rms_bwd(h2, p["c_norm"][0], dhn2, res=dh3, name="c_dnorm")
    dh1, g_ffn0 = _ffn_bwd(dh2, h1, 0, p, ffn0, S)

    dmix = _mm(dh1, p["ab_w_out"], tb=True, name="ab_dmix")
    dw_out = _mm(mix, dh1, ta=True, out_dtype=BF16, name="ab_dwout")
    do_h = _to_heads(dmix[:, :H * V_HEAD], B, S, V_HEAD).reshape(B * H, S, V_HEAD)
    dq_h = _flash_bwd_dq(q_h, k_h, v_h, o_h, lse, do_h, name="attn_dq")
    dk_h, dv_h = _flash_bwd_dkv(q_h, k_h, v_h, o_h, lse, do_h, name="attn_dkv")
    dqr = _rope_bwd(_from_heads(dq_h[..., QK_NOPE:QK_NOPE + QK_ROPE], B, S), cos, sin, head_sum=False, name="q_rope_bwd")
    dkr = _rope_bwd(_from_heads(dk_h[..., QK_NOPE:QK_NOPE + QK_ROPE], B, S), cos, sin, head_sum=True, name="k_rope_bwd")
    dq_full = jnp.concatenate([_from_heads(dq_h[..., :QK_NOPE], B, S), dqr], axis=1).astype(BF16)
    dkv = jnp.concatenate([_from_heads(dk_h[..., :QK_NOPE], B, S), _from_heads(dv_h, B, S)], axis=1).astype(BF16)
    dcqn = _mm(dq_full, p["w_q_p"], tb=True, name="q_dlat")
    dw_q_p = _mm(cqn, dq_full, ta=True, out_dtype=BF16, name="q_dw")
    dckvn = _mm(dkv, p["w_kv_p"], tb=True, name="kv_dlat")
    dw_kv_p = _mm(ckvn, dkv, ta=True, out_dtype=BF16, name="kv_dw")
    dcq, dqnorm = _rms_bwd(z, p["ab_q_norm"][0], dcqn, cb=4, out_dtype=BF16, name="q_dnorm")
    dckv, dkvnorm = _rms_bwd(z, p["ab_kv_norm"][0], dckvn, cb=10, out_dtype=BF16, name="kv_dnorm")
    dxl, dgate, dcw, dcb, dwa, dba, dwx, dbx, dlam = _lru_bwd(z, hs, dmix, *lru_par, S=S, name="lru_bwd")
    dz = jnp.concatenate([dxl, dgate, dcq, dckv, dkr.astype(BF16)], axis=1)
    dhn0 = _mm(dz, p["w_in_p"], tb=True, name="ab_dhn")
    dw_in_p = _mm(hn0, dz, ta=True, out_dtype=BF16, name="ab_dwin")
    dx, dabnorm = _rms_bwd(xf, p["ab_norm"][0], dhn0, res=dh1, name="ab_dnorm")

    blocks = lambda dd: jnp.stack([dd[i * LRU_BLOCK:(i + 1) * LRU_BLOCK, i * LRU_BLOCK:(i + 1) * LRU_BLOCK]
                                   for i in range(LRU_HEADS)])
    causal = jnp.tril(jnp.ones((CHUNK, CHUNK), F32))
    grads = {
        "ab_norm": dabnorm, "w_in_p": dw_in_p, "ab_q_norm": dqnorm, "w_q_p": dw_q_p,
        "ab_kv_norm": dkvnorm, "w_kv_p": dw_kv_p, "ab_conv_w": dcw[:LRU_CONV][None], "ab_conv_b": dcb,
        "ab_w_rg_a": blocks(dwa)[None], "ab_b_rg_a": dba, "ab_w_rg_x": blocks(dwx)[None], "ab_b_rg_x": dbx,
        "ab_lambda": dlam, "ab_w_out": dw_out,
        "c_norm": dcnorm, "c_w_in_t": dw_c_in_t, "c_ln_g": dlng, "c_ln_b": dlnb,
        "c_w_s": (dwm * causal)[None], "c_b_s": dbm[:, ::SGU_GROUP_DIM].T[None], "c_w_out": dw_c_out,
        "final_norm": dfinal[0],
    }
    for name in ("ffn_norm", "ffn_conv_w", "ffn_conv_b"):
        grads[name] = jnp.stack([g_ffn0[name], g_ffn1[name]])
    for name in ("ffn_gate_t", "ffn_up_t", "ffn_down"):
        grads[name] = [g_ffn0[name], g_ffn1[name]]
    return loss_row, dx.reshape(B, S, D), grads


ANY = pl.BlockSpec(memory_space=pl.ANY)


def _place():
    x, y, c = lax.axis_index("x"), lax.axis_index("y"), lax.axis_index("c")
    chips = [(1 - x, y), (x, 1 - y), (1 - x, 1 - y)]
    return x, y, c, 2 * x + y, (x, y, 1 - c), chips


def _remote(src, dst, send_sems, recv_sems, k, to):
    return pltpu.make_async_remote_copy(src_ref=src, dst_ref=dst, send_sem=send_sems.at[k], recv_sem=recv_sems.at[k],
                                        device_id=to, device_id_type=MESH)


def _comm_call(body, arrs, out_shapes, n_remote, n_local, name):
    return pl.pallas_call(
        body, name=name, in_specs=[ANY] * len(arrs), out_specs=[ANY] * len(out_shapes), out_shape=out_shapes,
        scratch_shapes=[pltpu.SemaphoreType.DMA((n_remote,)), pltpu.SemaphoreType.DMA((n_remote,)),
                        pltpu.SemaphoreType.DMA((n_local,))],
    )(*arrs)


def _all_gather(arrs, *, name):
    n = len(arrs)

    def body(*refs):
        ins, outs = refs[:n], refs[n:2 * n]
        send_sems, recv_sems, local_sems = refs[2 * n:]
        x, y, c, j, sib, chips = _place()
        mine = [pltpu.make_async_copy(ins[i], outs[i].at[:, j], local_sems.at[i]) for i in range(n)]
        for cp in mine:
            cp.start()
        first = [_remote(ins[i].at[:, c], outs[i].at[:, j, c], send_sems, recv_sems, 6 * i + k, (cx, cy, c))
                 for i in range(n) for k, (cx, cy) in enumerate(chips)]
        for cp in first:
            cp.start()
        passed = []
        for i in range(n):
            for k, (cx, cy) in enumerate(chips):
                got = outs[i].at[:, 2 * cx + cy, c]
                _remote(got, got, send_sems, recv_sems, 6 * i + k, (cx, cy, c)).wait_recv()
                cp = _remote(got, got, send_sems, recv_sems, 6 * i + 3 + k, sib)
                cp.start()
                passed.append(cp)
        for i in range(n):
            for k, (cx, cy) in enumerate(chips):
                got = outs[i].at[:, 2 * cx + cy, 1 - c]
                _remote(got, got, send_sems, recv_sems, 6 * i + 3 + k, sib).wait_recv()
        for cp in first + passed:
            cp.wait_send()
        for cp in mine:
            cp.wait()

    shapes = [jax.ShapeDtypeStruct((a.shape[0], N_CHIPS) + a.shape[1:], a.dtype) for a in arrs]
    return _comm_call(body, arrs, shapes, 6 * n, n, name)


def _pair_swap(arrs, *, name):
    n = len(arrs)

    def body(*refs):
        ins, owns, lands = refs[:n], refs[n:2 * n], refs[2 * n:3 * n]
        send_sems, recv_sems, local_sems = refs[3 * n:]
        x, y, c, j, sib, chips = _place()
        mine = [pltpu.make_async_copy(ins[i].at[:, c], owns[i], local_sems.at[i]) for i in range(n)]
        sends = [_remote(ins[i].at[:, 1 - c], lands[i], send_sems, recv_sems, i, sib) for i in range(n)]
        for cp in mine + sends:
            cp.start()
        for cp in sends:
            cp.wait()
        for cp in mine:
            cp.wait()

    shapes = [jax.ShapeDtypeStruct((a.shape[0],) + a.shape[2:], a.dtype) for a in arrs]
    res = _comm_call(body, arrs, shapes + shapes, n, n, name)
    return res[:n], res[n:]


def _pair_share(arrs, *, name):
    n = len(arrs)

    def body(*refs):
        ins, outs = refs[:n], refs[n:2 * n]
        send_sems, recv_sems, local_sems = refs[2 * n:]
        x, y, c, j, sib, chips = _place()
        mine = [pltpu.make_async_copy(ins[i], outs[i].at[c], local_sems.at[i]) for i in range(n)]
        sends = [_remote(ins[i], outs[i].at[c], send_sems, recv_sems, i, sib) for i in range(n)]
        for cp in mine + sends:
            cp.start()
        for i in range(n):
            _remote(ins[i], outs[i].at[1 - c], send_sems, recv_sems, i, sib).wait_recv()
        for cp in sends:
            cp.wait_send()
        for cp in mine:
            cp.wait()

    shapes = [jax.ShapeDtypeStruct((2,) + a.shape, a.dtype) for a in arrs]
    return _comm_call(body, arrs, shapes, n, n, name)


def _chip_exchange(arrs, *, scatter, name):
    n = len(arrs)

    def body(*refs):
        ins, outs = refs[:n], refs[n:2 * n]
        send_sems, recv_sems, local_sems = refs[2 * n:]
        x, y, c, j, sib, chips = _place()
        part = lambda i, k: ins[i].at[k] if scatter else ins[i]
        mine = [pltpu.make_async_copy(part(i, j), outs[i].at[j], local_sems.at[i]) for i in range(n)]
        sends = [_remote(part(i, 2 * cx + cy), outs[i].at[j], send_sems, recv_sems, 3 * i + k, (cx, cy, c))
                 for i in range(n) for k, (cx, cy) in enumerate(chips)]
        for cp in mine + sends:
            cp.start()
        for i in range(n):
            for k, (cx, cy) in enumerate(chips):
                got = outs[i].at[2 * cx + cy]
                _remote(got, got, send_sems, recv_sems, 3 * i + k, (cx, cy, c)).wait_recv()
        for cp in sends:
            cp.wait_send()
        for cp in mine:
            cp.wait()

    shapes = [jax.ShapeDtypeStruct((N_CHIPS,) + a.shape[-2:], a.dtype) for a in arrs]
    return _comm_call(body, arrs, shapes, 3 * n, n, name)


FLAT_ROWS = 512


def _add2(a, b, *, out_dtype, name):
    n, R, L = a.shape
    tr = _tile(R, FLAT_ROWS, 16)

    def body(a_ref, b_ref, o_ref):
        o_ref[...] = (a_ref[...].astype(F32) + b_ref[...].astype(F32)).astype(out_dtype)

    spec = pl.BlockSpec((n, tr, L), lambda i: (0, i, 0))
    return pl.pallas_call(
        body, name=name, grid=(R // tr,), in_specs=[spec, spec], out_specs=spec,
        out_shape=jax.ShapeDtypeStruct(a.shape, out_dtype), compiler_params=_cparams(("parallel",)),
    )(a, b)


def _sum_slots(buf, *, name):
    n, R, L = buf.shape
    tr = _tile(R, FLAT_ROWS, 16)

    def body(b_ref, o_ref):
        acc = b_ref[0].astype(F32)
        for k in range(1, n):
            acc = acc + b_ref[k].astype(F32)
        o_ref[...] = acc

    return pl.pallas_call(
        body, name=name, grid=(R // tr,), in_specs=[pl.BlockSpec((n, tr, L), lambda i: (0, i, 0))],
        out_specs=pl.BlockSpec((tr, L), lambda i: (i, 0)),
        out_shape=jax.ShapeDtypeStruct((R, L), F32), compiler_params=_cparams(("parallel",)),
    )(buf)


def _adamw(w, g, m, v, *, name):
    R, L = w.shape
    tr = _tile(R, FLAT_ROWS, 16)
    c1 = 1.0 - ADAM_B1 ** ADAM_STEP
    c2 = 1.0 - ADAM_B2 ** ADAM_STEP

    def body(w_ref, g_ref, m_ref, v_ref, d_ref, nm_ref, nv_ref):
        gg = g_ref[...]
        mm = ADAM_B1 * m_ref[...] + (1.0 - ADAM_B1) * gg
        vv = ADAM_B2 * v_ref[...] + (1.0 - ADAM_B2) * (gg * gg)
        nm_ref[...] = mm
        nv_ref[...] = vv
        d_ref[...] = -ADAM_LR * ((mm / c1) / (jnp.sqrt(vv / c2) + ADAM_EPS) + ADAM_WD * w_ref[...])

    spec = pl.BlockSpec((tr, L), lambda i: (i, 0))
    sh = jax.ShapeDtypeStruct((R, L), F32)
    return pl.pallas_call(
        body, name=name, grid=(R // tr,), in_specs=[spec] * 4, out_specs=[spec] * 3, out_shape=[sh] * 3,
        compiler_params=_cparams(("parallel",)),
    )(w, g, m, v)


WEIGHT_NAMES = ["ab_norm", "ab_w_in", "ab_q_norm", "ab_w_q_b", "ab_kv_norm", "ab_w_kv_b", "ab_conv_w", "ab_conv_b",
                "ab_w_rg_a", "ab_b_rg_a", "ab_w_rg_x", "ab_b_rg_x", "ab_lambda", "ab_w_out", "c_norm", "c_w_in",
                "c_ln_g", "c_ln_b", "c_w_s", "c_b_s", "c_w_out", "ffn_norm", "ffn_w_gate", "ffn_w_up", "ffn_conv_w",
                "ffn_conv_b", "ffn_w_down", "final_norm"]
BIG = {"ab_w_in": 2, "ab_w_q_b": 2, "ab_w_kv_b": 2, "ab_w_out": 1, "c_w_in": 2, "c_w_out": 1,
       "ffn_w_gate": 2, "ffn_w_up": 2, "ffn_w_down": 1}
SMALL_SHARDED = {"ab_conv_w": 2, "c_norm": 1, "c_ln_g": 1, "c_ln_b": 1, "ffn_conv_w": 2}
SMALL_REPLICATED = [n for n in WEIGHT_NAMES if n not in BIG and n not in SMALL_SHARDED]


def _rows(n_elems, mult):
    r = -(-n_elems // LANES)
    return -(-r // mult) * mult


def _flat(parts, rows):
    flat = jnp.concatenate([a.reshape(-1) for a in parts])
    return jnp.pad(flat, (0, rows * LANES - flat.shape[0])).reshape(rows, LANES)


def _unflat(flat, shapes):
    flat = flat.reshape(-1)
    out, off = [], 0
    for s in shapes:
        n = math.prod(s)
        out.append(flat[off:off + n].reshape(s))
        off += n
    return out


def _join_shards(a, axis):
    a = jnp.moveaxis(a, 0, axis)
    return a.reshape(a.shape[:axis] + (a.shape[axis] * a.shape[axis + 1],) + a.shape[axis + 2:])


def kernel(x, positions, ab_norm, ab_w_in, ab_q_norm, ab_w_q_b, ab_kv_norm, ab_w_kv_b, ab_conv_w, ab_conv_b, ab_w_rg_a, ab_b_rg_a, ab_w_rg_x, ab_b_rg_x, ab_lambda, ab_w_out, c_norm, c_w_in, c_ln_g, c_ln_b, c_w_s, c_b_s, c_w_out, ffn_norm, ffn_w_gate, ffn_w_up, ffn_conv_w, ffn_conv_b, ffn_w_down, final_norm, loss_target, m_ab_norm, m_ab_w_in, m_ab_q_norm, m_ab_w_q_b, m_ab_kv_norm, m_ab_w_kv_b, m_ab_conv_w, m_ab_conv_b, m_ab_w_rg_a, m_ab_b_rg_a, m_ab_w_rg_x, m_ab_b_rg_x, m_ab_lambda, m_ab_w_out, m_c_norm, m_c_w_in, m_c_ln_g, m_c_ln_b, m_c_w_s, m_c_b_s, m_c_w_out, m_ffn_norm, m_ffn_w_gate, m_ffn_w_up, m_ffn_conv_w, m_ffn_conv_b, m_ffn_w_down, m_final_norm, v_ab_norm, v_ab_w_in, v_ab_q_norm, v_ab_w_q_b, v_ab_kv_norm, v_ab_w_kv_b, v_ab_conv_w, v_ab_conv_b, v_ab_w_rg_a, v_ab_b_rg_a, v_ab_w_rg_x, v_ab_b_rg_x, v_ab_lambda, v_ab_w_out, v_c_norm, v_c_w_in, v_c_ln_g, v_c_ln_b, v_c_w_s, v_c_b_s, v_c_w_out, v_ffn_norm, v_ffn_w_gate, v_ffn_w_up, v_ffn_conv_w, v_ffn_conv_b, v_ffn_w_down, v_final_norm):
    given = dict(locals())
    w = {n: given[n] for n in WEIGHT_NAMES}
    m = {n: given["m_" + n] for n in WEIGHT_NAMES}
    v = {n: given["v_" + n] for n in WEIGHT_NAMES}
    c = lax.axis_index("c")
    chip = 2 * lax.axis_index("x") + lax.axis_index("y")

    halves = lambda a: a.reshape(a.shape[0], 2, a.shape[1] // 2, a.shape[2])
    tr = lambda a: jnp.swapaxes(a, 1, 2)
    send = {"ab_w_in": w["ab_w_in"], "ab_w_q_b": w["ab_w_q_b"], "ab_w_kv_b": w["ab_w_kv_b"], "ab_w_out": w["ab_w_out"],
            "c_w_in": tr(w["c_w_in"]), "c_w_out": w["c_w_out"], "ffn_w_gate": tr(w["ffn_w_gate"]),
            "ffn_w_up": tr(w["ffn_w_up"]), "ffn_w_down": w["ffn_w_down"]}
    small_rows = _rows(sum(w[n].size for n in SMALL_SHARDED), 16)
    small_sh = _flat([w[n] for n in SMALL_SHARDED], small_rows).reshape(1, 2, small_rows // 2, LANES)
    got = _all_gather([halves(send[n].astype(BF16)) for n in BIG] + [small_sh], name="gather_weights")
    full = {n: a.reshape(a.shape[0], -1, a.shape[-1]) for n, a in zip(BIG, got)}
    unshard = lambda a: jnp.swapaxes(a.reshape(N_CHIPS, -1, a.shape[-1]), 0, 1).reshape(-1, N_CHIPS * a.shape[-1])
    p = _prep_big(unshard(full["ab_w_in"][0]), unshard(full["ab_w_q_b"][0]), unshard(full["ab_w_kv_b"][0]))
    p.update(ab_w_out=full["ab_w_out"][0], c_w_in_t=full["c_w_in"][0], c_w_out=full["c_w_out"][0],
             ffn_gate_t=full["ffn_w_gate"], ffn_up_t=full["ffn_w_up"], ffn_down=full["ffn_w_down"])
    small_full = dict(w)
    off = 0
    small_got = got[-1].reshape(N_CHIPS, -1)
    for n, ax in SMALL_SHARDED.items():
        seg = small_got[:, off:off + w[n].size].reshape((N_CHIPS,) + w[n].shape)
        small_full[n] = _join_shards(seg, ax)
        off += w[n].size
    p.update(_prep_small(small_full))

    loss_row, grad_x, g = _local_step(x, positions, loss_target, p)

    cols = lambda a, n: jnp.swapaxes(a.reshape(a.shape[0], N_CHIPS, n), 0, 1)
    n_in, n_q, n_kv = w["ab_w_in"].shape[2], w["ab_w_q_b"].shape[2], w["ab_w_kv_b"].shape[2]
    sharded = [cols(_unperm_w_in(g["w_in_p"]), n_in), cols(_unperm_heads(g["w_q_p"], QK_NOPE, QK_ROPE), n_q),
               cols(_unperm_heads(g["w_kv_p"], QK_NOPE, V_HEAD), n_kv), g["ab_w_out"], g["c_w_in_t"], g["c_w_out"],
               *g["ffn_gate_t"], *g["ffn_up_t"], *g["ffn_down"]]
    sharded = [a.reshape(N_CHIPS, 2, -1, a.shape[-1]) for a in sharded]
    own, from_sib = _pair_swap(sharded, name="grad_pair_swap")
    pair = [_add2(a, b, out_dtype=BF16, name=f"grad_pair_add{i}") for i, (a, b) in enumerate(zip(own, from_sib))]
    from_chips = _chip_exchange(pair, scatter=True, name="grad_chip_exchange")
    half = [_sum_slots(a, name=f"grad_chip_sum{i}") for i, a in enumerate(from_chips)]
    summed = [a.reshape(-1, a.shape[-1]) for a in _pair_share(half, name="grad_pair_share")]
    s_in, s_q, s_kv, s_out, s_cin, s_cout, g0, g1, u0, u1, d0, d1 = summed
    grads = {"ab_w_in": s_in[None], "ab_w_q_b": s_q[None], "ab_w_kv_b": s_kv[None], "ab_w_out": s_out[None],
             "c_w_in": s_cin.T[None], "c_w_out": s_cout[None], "ffn_w_gate": jnp.stack([g0.T, g1.T]),
             "ffn_w_up": jnp.stack([u0.T, u1.T]), "ffn_w_down": jnp.stack([d0, d1])}

    small_names = SMALL_REPLICATED + list(SMALL_SHARDED)
    rs = _rows(sum(g[n].size for n in small_names) + LANES, FLAT_ROWS)
    small = _flat([loss_row] + [g[n] for n in small_names], rs)
    both, = _pair_share([small], name="small_pair_share")
    pair_small = _sum_slots(both, name="small_pair_sum")
    all_small, = _chip_exchange([pair_small], scatter=False, name="small_chip_exchange")
    small_sum = _sum_slots(all_small, name="small_chip_sum")
    small_parts = _unflat(small_sum, [(1, LANES)] + [g[n].shape for n in small_names])
    loss = small_parts[0][0, 0]
    for n, a in zip(small_names, small_parts[1:]):
        if n in SMALL_SHARDED:
            ax = SMALL_SHARDED[n]
            a = lax.dynamic_slice_in_dim(a, chip * w[n].shape[ax], w[n].shape[ax], axis=ax)
        grads[n] = a.reshape(w[n].shape)

    delta, new_m, new_v = {}, {}, {}
    two_d = lambda a: a.reshape(-1, a.shape[-1])
    for n in BIG:
        out = _adamw(two_d(w[n]), two_d(grads[n]), two_d(m[n]), two_d(v[n]), name=f"adamw_{n}")
        delta[n], new_m[n], new_v[n] = (a.reshape(w[n].shape) for a in out)
    small_all = [n for n in WEIGHT_NAMES if n not in BIG]
    ra = _rows(sum(w[n].size for n in small_all), FLAT_ROWS)
    pack = lambda d: _flat([d[n] for n in small_all], ra)
    out = _adamw(pack(w), pack(grads), pack(m), pack(v), name="adamw_small")
    shapes = [w[n].shape for n in small_all]
    for d, flat in zip((delta, new_m, new_v), out):
        d.update(zip(small_all, _unflat(flat, shapes)))
    return (loss, grad_x, *[grads[n] for n in WEIGHT_NAMES], *[delta[n] for n in WEIGHT_NAMES],
            *[new_m[n] for n in WEIGHT_NAMES], *[new_v[n] for n in WEIGHT_NAMES])
```

```python
import functools
import math

import jax
import jax.numpy as jnp
from jax import lax
from jax.experimental import pallas as pl
from jax.experimental.pallas import tpu as pltpu

F32 = jnp.float32
BF16 = jnp.bfloat16
MESH = pl.DeviceIdType.MESH

D_MODEL = 1024
MLA_HEADS = 8
Q_LORA = 256
KV_LORA = 128
QK_NOPE = 64
QK_ROPE = 32
V_HEAD = 64
LRU_WIDTH = 512
LRU_HEADS = 8
LRU_BLOCK = 64
LRU_CONV = 4
LRU_C = 8.0
CHUNK = 128
SGU_GROUPS = 8
SGU_WIDTH = 1024
D_FF = 2816
FFN_CONV = 3
NORM_EPS = 1e-6
ROPE_BASE = 10000.0
AB_IN_PAD = 1536
ADAM_LR = 0.001
ADAM_B1 = 0.9
ADAM_B2 = 0.999
ADAM_EPS = 1e-08
ADAM_WD = 0.01
ADAM_STEP = 10

N_CHIPS = 4
LANES = 128
VMEM_LIMIT = 56 * 1024 * 1024
ROW_TILE = 256
MM_TM, MM_TN, MM_TK = 512, 1536, 2816
MM_TM_T, MM_TK_T = 1408, 1024
GELU_C = math.sqrt(2.0 / math.pi)


def _cparams(sem):
    return pltpu.CompilerParams(dimension_semantics=sem, vmem_limit_bytes=VMEM_LIMIT)


def _tile(n, target, mult=LANES):
    t = (min(n, target) // mult) * mult
    while t >= mult:
        if n % t == 0:
            return t
        t -= mult
    return n


def _gelu(x):
    t = jnp.tanh(GELU_C * (x + 0.044715 * x * x * x))
    return 0.5 * x * (1.0 + t)


def _gelu_and_grad(x):
    x2 = x * x
    t = jnp.tanh(GELU_C * (x + 0.044715 * x * x2))
    g = 0.5 * x * (1.0 + t)
    dg = 0.5 * (1.0 + t) + 0.5 * x * (1.0 - t * t) * GELU_C * (1.0 + 3.0 * 0.044715 * x2)
    return g, dg


def _sigmoid(x):
    return 1.0 / (1.0 + jnp.exp(-x))


def _shift_rows(x, d, fill_rows):
    ext = jnp.concatenate([fill_rows, x], axis=0)
    return pltpu.roll(ext, d, 0)[8:]


def _shift_rows_up(x, d, fill_rows):
    n = x.shape[0]
    ext = jnp.concatenate([x, fill_rows], axis=0)
    return pltpu.roll(ext, n + 8 - d, 0)[:n]


def _dot(a, b, dims):
    return lax.dot_general(a.astype(BF16), b.astype(BF16), (dims, ((), ())), preferred_element_type=F32)


def _dot_nn(a, b):
    return _dot(a, b, ((1,), (0,)))


def _dot_nt(a, b):
    return _dot(a, b, ((1,), (1,)))


def _dot_tn(a, b):
    return _dot(a, b, ((0,), (0,)))


def _mm(a, b, *, name, ta=False, tb=False, res=None, out_dtype=F32):
    if ta:
        K, M = a.shape
    else:
        M, K = a.shape
    N = b.shape[0] if tb else b.shape[1]
    tm = _tile(M, MM_TM_T if ta else MM_TM, LANES if ta else 8)
    tn = _tile(N, MM_TN, LANES)
    tk = _tile(K, MM_TK_T if ta else MM_TK, LANES)
    nk = K // tk
    a_spec = pl.BlockSpec((tk, tm), lambda i, j, k: (k, i)) if ta else pl.BlockSpec((tm, tk), lambda i, j, k: (i, k))
    b_spec = pl.BlockSpec((tn, tk), lambda i, j, k: (j, k)) if tb else pl.BlockSpec((tk, tn), lambda i, j, k: (k, j))
    o_spec = pl.BlockSpec((tm, tn), lambda i, j, k: (i, j))
    dims = ((0,) if ta else (1,), (1,) if tb else (0,))
    has_res = res is not None

    def body(*refs):
        a_ref, b_ref = refs[:2]
        r_ref = refs[2] if has_res else None
        o_ref = refs[3] if has_res else refs[2]
        p = _dot(a_ref[...], b_ref[...], dims)

        def finish(r):
            if has_res:
                r = r + r_ref[...].astype(F32)
            o_ref[...] = r.astype(out_dtype)

        if nk == 1:
            finish(p)
            return
        acc_ref = refs[-1]
        k = pl.program_id(2)

        @pl.when(k == 0)
        def _():
            acc_ref[...] = p

        @pl.when(jnp.logical_and(k > 0, k < nk - 1))
        def _():
            acc_ref[...] += p

        @pl.when(k == nk - 1)
        def _():
            finish(acc_ref[...] + p)

    in_specs = [a_spec, b_spec] + ([o_spec] if has_res else [])
    args = (a, b) + ((res,) if has_res else ())
    return pl.pallas_call(
        body, name=name, grid=(M // tm, N // tn, nk), in_specs=in_specs, out_specs=o_spec,
        out_shape=jax.ShapeDtypeStruct((M, N), out_dtype),
        scratch_shapes=[pltpu.VMEM((tm, tn), F32)] if nk > 1 else [],
        compiler_params=_cparams(("parallel", "parallel", "arbitrary")),
    )(*args)


def _rms_fwd(x, g, *, name, cb=0, out_dtype=BF16):
    T = x.shape[0]
    W = g.shape[-1]
    g = g.reshape(1, W)
    tt = ROW_TILE

    def body(x_ref, g_ref, o_ref):
        xf = x_ref[...].astype(F32)
        rstd = lax.rsqrt(jnp.mean(xf * xf, axis=-1, keepdims=True) + NORM_EPS)
        o_ref[...] = (xf * rstd * g_ref[...]).astype(out_dtype)

    return pl.pallas_call(
        body, name=name, grid=(T // tt,),
        in_specs=[pl.BlockSpec((tt, W), lambda i: (i, cb)), pl.BlockSpec((1, W), lambda i: (0, 0))],
        out_specs=pl.BlockSpec((tt, W), lambda i: (i, 0)),
        out_shape=jax.ShapeDtypeStruct((T, W), out_dtype),
        compiler_params=_cparams(("parallel",)),
    )(x, g)


def _rms_bwd(x, g, dy, *, name, cb=0, res=None, out_dtype=F32):
    T = x.shape[0]
    W = g.shape[-1]
    g = g.reshape(1, W)
    tt = ROW_TILE
    has_res = res is not None

    def body(*refs):
        if has_res:
            x_ref, g_ref, dy_ref, r_ref, dx_ref, dg_ref = refs
        else:
            x_ref, g_ref, dy_ref, dx_ref, dg_ref = refs
        xf = x_ref[...].astype(F32)
        dyf = dy_ref[...].astype(F32)
        rstd = lax.rsqrt(jnp.mean(xf * xf, axis=-1, keepdims=True) + NORM_EPS)
        xhat = xf * rstd
        dxhat = dyf * g_ref[...]
        dx = rstd * (dxhat - xhat * jnp.mean(dxhat * xhat, axis=-1, keepdims=True))
        if has_res:
            dx = dx + r_ref[...].astype(F32)
        dx_ref[...] = dx.astype(out_dtype)
        part = jnp.sum(dyf * xhat, axis=0, keepdims=True)

        @pl.when(pl.program_id(0) == 0)
        def _():
            dg_ref[...] = part

        @pl.when(pl.program_id(0) > 0)
        def _():
            dg_ref[...] += part

    row = pl.BlockSpec((tt, W), lambda i: (i, 0))
    in_specs = [pl.BlockSpec((tt, W), lambda i: (i, cb)), pl.BlockSpec((1, W), lambda i: (0, 0)), row]
    args = (x, g, dy)
    if has_res:
        in_specs.append(row)
        args = args + (res,)
    return pl.pallas_call(
        body, name=name, grid=(T // tt,), in_specs=in_specs,
        out_specs=[row, pl.BlockSpec((1, W), lambda i: (0, 0))],
        out_shape=[jax.ShapeDtypeStruct((T, W), out_dtype), jax.ShapeDtypeStruct((1, W), F32)],
        compiler_params=_cparams(("arbitrary",)),
    )(*args)


def _final_fwd_bwd(h, g, target, *, name):
    T, W = h.shape
    g = g.reshape(1, W)
    tt = ROW_TILE

    def body(x_ref, g_ref, t_ref, loss_ref, dx_ref, dg_ref):
        xf = x_ref[...]
        rstd = lax.rsqrt(jnp.mean(xf * xf, axis=-1, keepdims=True) + NORM_EPS)
        xhat = xf * rstd
        err = xhat * g_ref[...] - t_ref[...]
        lpart = jnp.zeros((1, LANES), F32) + (0.5 / W) * jnp.sum(err * err)
        dyf = err * (1.0 / W)
        dxhat = dyf * g_ref[...]
        dx_ref[...] = rstd * (dxhat - xhat * jnp.mean(dxhat * xhat, axis=-1, keepdims=True))
        part = jnp.sum(dyf * xhat, axis=0, keepdims=True)

        @pl.when(pl.program_id(0) == 0)
        def _():
            dg_ref[...] = part
            loss_ref[...] = lpart

        @pl.when(pl.program_id(0) > 0)
        def _():
            dg_ref[...] += part
            loss_ref[...] += lpart

    row = pl.BlockSpec((tt, W), lambda i: (i, 0))
    return pl.pallas_call(
        body, name=name, grid=(T // tt,),
        in_specs=[row, pl.BlockSpec((1, W), lambda i: (0, 0)), row],
        out_specs=[pl.BlockSpec((1, LANES), lambda i: (0, 0)), row, pl.BlockSpec((1, W), lambda i: (0, 0))],
        out_shape=[jax.ShapeDtypeStruct((1, LANES), F32), jax.ShapeDtypeStruct((T, W), F32),
                   jax.ShapeDtypeStruct((1, W), F32)],
        compiler_params=_cparams(("arbitrary",)),
    )(h, g, target)


def _swap16(x):
    lane = lax.broadcasted_iota(jnp.int32, x.shape, 1)
    return jnp.where((lane % 32) < 16, pltpu.roll(x, LANES - 16, 1), pltpu.roll(x, 16, 1))


def _rope_fwd(x, cos, sin, *, name, cb0, ncb):
    T = x.shape[0]
    tt = ROW_TILE

    def body(x_ref, c_ref, s_ref, o_ref):
        xf = x_ref[...]
        o_ref[...] = xf * c_ref[...] + _swap16(xf) * s_ref[...]

    tab = pl.BlockSpec((tt, LANES), lambda i, j: (i, 0))
    return pl.pallas_call(
        body, name=name, grid=(T // tt, ncb),
        in_specs=[pl.BlockSpec((tt, LANES), lambda i, j: (i, cb0 + j)), tab, tab],
        out_specs=pl.BlockSpec((tt, LANES), lambda i, j: (i, j)),
        out_shape=jax.ShapeDtypeStruct((T, ncb * LANES), F32),
        compiler_params=_cparams(("parallel", "parallel")),
    )(x, cos, sin)


def _rope_bwd(dy, cos, sin, *, name, head_sum):
    T = dy.shape[0]
    tt = ROW_TILE
    ncb = 1 if head_sum else dy.shape[1] // LANES

    def body(d_ref, c_ref, s_ref, o_ref):
        d = d_ref[...]
        if head_sum:
            d = d[:, :LANES] + d[:, LANES:]
            d = d + pltpu.roll(d, 64, 1)
            d = d + pltpu.roll(d, 96, 1)
            lane = lax.broadcasted_iota(jnp.int32, d.shape, 1)
            d = jnp.where(lane < QK_ROPE, d, 0.0)
        o_ref[...] = d * c_ref[...] + _swap16(d * s_ref[...])

    tab = pl.BlockSpec((tt, LANES), lambda i, j: (i, 0))
    d_spec = pl.BlockSpec((tt, 2 * LANES), lambda i, j: (i, 0)) if head_sum else pl.BlockSpec((tt, LANES), lambda i, j: (i, j))
    return pl.pallas_call(
        body, name=name, grid=(T // tt, ncb),
        in_specs=[d_spec, tab, tab],
        out_specs=pl.BlockSpec((tt, LANES), lambda i, j: (i, j)),
        out_shape=jax.ShapeDtypeStruct((T, ncb * LANES), F32),
        compiler_params=_cparams(("parallel", "parallel")),
    )(dy, cos, sin)


ATT_BLOCK = 256


def _attn_scale():
    return float((QK_NOPE + QK_ROPE) ** -0.5)


def _causal_mask(qi, kj, tq, tk):
    row = qi * tq + lax.broadcasted_iota(jnp.int32, (tq, tk), 0)
    col = kj * tk + lax.broadcasted_iota(jnp.int32, (tq, tk), 1)
    return col <= row


def _host_refs(refs, n_in, n_out, ex):
    if ex is None:
        return refs[:n_in], refs[n_in:n_in + n_out], None
    e_in, e_out = len(ex.arrs), len(ex.out_shapes)
    o0 = n_in + e_in
    return refs[:n_in], refs[o0:o0 + n_out], (refs[n_in:o0], refs[o0 + n_out:o0 + n_out + e_out], refs[-2], refs[-1])


def _host_call(body, name, grid, in_specs, out_specs, out_shape, args, ex):
    if ex is not None:
        in_specs, out_specs = in_specs + ex.in_specs, out_specs + ex.out_specs
        out_shape, args = out_shape + ex.out_shapes, tuple(args) + tuple(ex.arrs)
    return pl.pallas_call(
        body, name=name, grid=grid, in_specs=in_specs, out_specs=out_specs, out_shape=out_shape,
        scratch_shapes=ex.scratch if ex is not None else [],
        compiler_params=_cparams(("arbitrary",) * len(grid)),
    )(*args)


def _flash_fwd(q, k, v, *, name, exchange=None):
    BH, S, DK = q.shape
    DV = v.shape[-1]
    tq = tk = min(ATT_BLOCK, S)
    nq = S // tq

    def body(*refs):
        (q_ref, k_ref, v_ref), (o_ref, lse_ref), ex_parts = _host_refs(refs, 3, 2, exchange)
        b, qi = pl.program_id(0), pl.program_id(1)
        if exchange is not None:
            pl.when(jnp.logical_and(b == 0, qi == 0))(lambda: exchange.start(*ex_parts))
        qb = q_ref[0]

        def step(masked):
            def f(j, carry):
                m, l, acc = carry
                kb = k_ref[0, pl.ds(pl.multiple_of(j * tk, tk), tk), :]
                vb = v_ref[0, pl.ds(pl.multiple_of(j * tk, tk), tk), :]
                s = _dot_nt(qb, kb)
                if masked:
                    s = jnp.where(_causal_mask(qi, j, tq, tk), s, -jnp.inf)
                m_new = jnp.maximum(m, jnp.max(s, axis=-1, keepdims=True))
                alpha = jnp.exp(m - m_new)
                p = jnp.exp(s - m_new)
                l = alpha * l + jnp.sum(p, axis=-1, keepdims=True)
                acc = alpha * acc + _dot_nn(p, vb)
                return m_new, l, acc
            return f

        init = (jnp.full((tq, 1), -1e30, F32), jnp.zeros((tq, 1), F32), jnp.zeros((tq, DV), F32))
        m, l, acc = step(True)(qi, lax.fori_loop(0, qi, step(False), init))
        o_ref[0] = acc / l
        lse_ref[0] = m + jnp.log(l)
        if exchange is not None:
            pl.when(jnp.logical_and(b == BH - 1, qi == nq - 1))(lambda: exchange.finish(*ex_parts))

    return _host_call(
        body, name, (BH, nq),
        [pl.BlockSpec((1, tq, DK), lambda b, i: (b, i, 0)), pl.BlockSpec((1, S, DK), lambda b, i: (b, 0, 0)),
         pl.BlockSpec((1, S, DV), lambda b, i: (b, 0, 0))],
        [pl.BlockSpec((1, tq, DV), lambda b, i: (b, i, 0)), pl.BlockSpec((1, tq, 1), lambda b, i: (b, i, 0))],
        [jax.ShapeDtypeStruct((BH, S, DV), F32), jax.ShapeDtypeStruct((BH, S, 1), F32)], (q, k, v), exchange)


def _flash_bwd_dq(q, k, v, o, lse, do, *, name):
    BH, S, DK = q.shape
    DV = v.shape[-1]
    tq = tk = min(ATT_BLOCK, S)
    scale = _attn_scale()

    def body(q_ref, k_ref, v_ref, o_ref, lse_ref, do_ref, dq_ref, delta_ref):
        qi = pl.program_id(1)
        qb = q_ref[0]
        dob = do_ref[0]
        lse_b = lse_ref[0]
        delta = jnp.sum(dob * o_ref[0], axis=-1, keepdims=True)
        delta_ref[0] = delta

        def step(masked):
            def f(j, dq):
                kb = k_ref[0, pl.ds(pl.multiple_of(j * tk, tk), tk), :]
                vb = v_ref[0, pl.ds(pl.multiple_of(j * tk, tk), tk), :]
                p = jnp.exp(_dot_nt(qb, kb) - lse_b)
                if masked:
                    p = jnp.where(_causal_mask(qi, j, tq, tk), p, 0.0)
                ds = p * (_dot_nt(dob, vb) - delta)
                return dq + _dot_nn(ds, kb)
            return f

        dq = step(True)(qi, lax.fori_loop(0, qi, step(False), jnp.zeros((tq, DK), F32)))
        dq_ref[0] = dq * scale

    qspec = lambda w: pl.BlockSpec((1, tq, w), lambda b, i: (b, i, 0))
    full = lambda w: pl.BlockSpec((1, S, w), lambda b, i: (b, 0, 0))
    return pl.pallas_call(
        body, name=name, grid=(BH, S // tq),
        in_specs=[qspec(DK), full(DK), full(DV), qspec(DV), qspec(1), qspec(DV)],
        out_specs=[qspec(DK), qspec(1)],
        out_shape=[jax.ShapeDtypeStruct((BH, S, DK), F32), jax.ShapeDtypeStruct((BH, S, 1), F32)],
        compiler_params=_cparams(("parallel", "parallel")),
    )(q, k, v, o, lse, do)


def _flash_bwd_dkv(q, k, v, lse, delta, do, *, name, exchange=None):
    BH, S, DK = q.shape
    DV = v.shape[-1]
    tq = tk = min(ATT_BLOCK, S)
    nq = S // tq

    def body(*refs):
        (q_ref, k_ref, v_ref, lse_ref, delta_ref, do_ref), (dk_ref, dv_ref), ex_parts = _host_refs(refs, 6, 2, exchange)
        b, kj = pl.program_id(0), pl.program_id(1)
        if exchange is not None:
            pl.when(jnp.logical_and(b == 0, kj == 0))(lambda: exchange.start(*ex_parts))
        kb = k_ref[0]
        vb = v_ref[0]

        def step(masked):
            def f(i, carry):
                dk, dv = carry
                rows = pl.ds(pl.multiple_of(i * tq, tq), tq)
                qb = q_ref[0, rows, :]
                dob = do_ref[0, rows, :]
                p = jnp.exp(_dot_nt(qb, kb) - lse_ref[0, rows, :])
                if masked:
                    p = jnp.where(_causal_mask(i, kj, tq, tk), p, 0.0)
                ds = p * (_dot_nt(dob, vb) - delta_ref[0, rows, :])
                return dk + _dot_tn(ds, qb), dv + _dot_tn(p, dob)
            return f

        first = step(True)(kj, (jnp.zeros((tk, DK), F32), jnp.zeros((tk, DV), F32)))
        dk, dv = lax.fori_loop(kj + 1, nq, step(False), first)
        dk_ref[0] = dk
        dv_ref[0] = dv
        if exchange is not None:
            pl.when(jnp.logical_and(b == BH - 1, kj == nq - 1))(lambda: exchange.finish(*ex_parts))

    kspec = lambda w: pl.BlockSpec((1, tk, w), lambda b, j: (b, j, 0))
    full = lambda w: pl.BlockSpec((1, S, w), lambda b, j: (b, 0, 0))
    return _host_call(
        body, name, (BH, S // tk), [full(DK), kspec(DK), kspec(DV), full(1), full(1), full(DV)],
        [kspec(DK), kspec(DV)],
        [jax.ShapeDtypeStruct((BH, S, DK), F32), jax.ShapeDtypeStruct((BH, S, DV), F32)],
        (q, k, v, lse, delta, do), exchange)


def _lru_gates(xl, halo, cw_ref, cb_ref, wa_ref, ba_ref, wx_ref, bx_ref, lam_ref):
    xc = cb_ref[...] + cw_ref[3:4, :] * xl
    for kk in range(LRU_CONV - 1):
        xc = xc + cw_ref[kk:kk + 1, :] * _shift_rows(xl, LRU_CONV - 1 - kk, halo)
    r = _sigmoid(_dot_nn(xc, wa_ref[...]) + ba_ref[...])
    i = _sigmoid(_dot_nn(xc, wx_ref[...]) + bx_ref[...])
    lam = lam_ref[...]
    sp = jnp.maximum(-lam, 0.0) + jnp.log(1.0 + jnp.exp(-jnp.abs(lam)))
    a = jnp.exp(-LRU_C * r * sp)
    mult = jnp.sqrt(1.0 - a * a)
    return xc, r, i, sp, a, mult


def _lru_specs(tt, nt, S):
    def make(rev):
        tmap = (lambda t: nt - 1 - t) if rev else (lambda t: t)
        tile = lambda cb: pl.BlockSpec((tt, LRU_WIDTH), lambda b, t: (b * nt + tmap(t), cb))
        prev8 = lambda cb: pl.BlockSpec(
            (8, LRU_WIDTH), lambda b, t: (jnp.maximum((b * nt + tmap(t)) * (tt // 8) - 1, 0), cb))
        return tile, prev8, tmap
    return make


def _lru_fwd(z, cw, cb, wa, ba, wx, bx, lam, *, S, name):
    T = z.shape[0]
    tt = min(ROW_TILE, S)
    nt = S // tt
    tile, prev8, _ = _lru_specs(tt, nt, S)(False)
    vec = lambda r: pl.BlockSpec((r, LRU_WIDTH), lambda b, t: (0, 0))
    mat = pl.BlockSpec((LRU_WIDTH, LRU_WIDTH), lambda b, t: (0, 0))

    def body(xl_ref, halo_ref, gate_ref, cw_ref, cb_ref, wa_ref, ba_ref, wx_ref, bx_ref, lam_ref,
             y_ref, h_ref, carry_ref):
        t = pl.program_id(1)
        first = t == 0
        halo = jnp.where(first, 0.0, halo_ref[...])
        xl_t = xl_ref[...]
        xc, r, i, sp, a, mult = _lru_gates(xl_t, halo, cw_ref, cb_ref, wa_ref, ba_ref, wx_ref, bx_ref, lam_ref)
        bv = mult * (i * xc)
        ones = jnp.ones((8, LRU_WIDTH), F32)
        zeros = jnp.zeros((8, LRU_WIDTH), F32)
        row = lax.broadcasted_iota(jnp.int32, (tt, LRU_WIDTH), 0)
        A = a
        d = 1
        while d < tt:
            if d < 8:
                a_sh = _shift_rows(A, d, ones)
                b_sh = _shift_rows(bv, d, zeros)
            else:
                a_sh = jnp.where(row < d, 1.0, pltpu.roll(A, d, 0))
                b_sh = jnp.where(row < d, 0.0, pltpu.roll(bv, d, 0))
            bv = A * b_sh + bv
            A = A * a_sh
            d *= 2
        h0 = jnp.where(first, 0.0, carry_ref[0:1, :])
        h = A * h0 + bv
        carry_ref[...] = jnp.broadcast_to(h[tt - 1:tt, :], (8, LRU_WIDTH))
        h_ref[...] = h
        y_ref[...] = (h * _gelu(gate_ref[...])).astype(BF16)

    return pl.pallas_call(
        body, name=name, grid=(T // S, nt),
        in_specs=[tile(0), prev8(0), tile(1), vec(LRU_CONV), vec(1), mat, vec(1), mat, vec(1), vec(1)],
        out_specs=[tile(0), tile(0)],
        out_shape=[jax.ShapeDtypeStruct((T, LRU_WIDTH), BF16), jax.ShapeDtypeStruct((T, LRU_WIDTH), F32)],
        scratch_shapes=[pltpu.VMEM((8, LRU_WIDTH), F32)],
        compiler_params=_cparams(("arbitrary", "arbitrary")),
    )(z, z, z, cw, cb, wa, ba, wx, bx, lam)


def _lru_bwd(z, h, dy, cw, cb, wa, ba, wx, bx, lam, *, S, name):
    T = z.shape[0]
    tt = min(ROW_TILE, S)
    nt = S // tt
    tile, prev8, tmap = _lru_specs(tt, nt, S)(True)
    vec = lambda r: pl.BlockSpec((r, LRU_WIDTH), lambda b, t: (0, 0))
    mat = pl.BlockSpec((LRU_WIDTH, LRU_WIDTH), lambda b, t: (0, 0))

    def body(xl_ref, halo_ref, gate_ref, h_ref, hprev_ref, dy_ref, cw_ref, cb_ref, wa_ref, ba_ref, wx_ref,
             bx_ref, lam_ref, dxl_ref, dgate_ref, dcw_ref, dcb_ref, dwa_ref, dba_ref, dwx_ref, dbx_ref,
             dlam_ref, lamc_ref, ac_ref, dxc_ref):
        b = pl.program_id(0)
        t = pl.program_id(1)
        tr = nt - 1 - t
        seq_first = tr == 0
        seq_last = t == 0
        halo = jnp.where(seq_first, 0.0, halo_ref[...])
        xl_t = xl_ref[...]
        xc, r, i, sp, a, mult = _lru_gates(xl_t, halo, cw_ref, cb_ref, wa_ref, ba_ref, wx_ref, bx_ref, lam_ref)
        hh = h_ref[...]
        dyf = dy_ref[...].astype(F32)
        gl, dgl = _gelu_and_grad(gate_ref[...])
        dgate_ref[...] = (dyf * hh * dgl).astype(BF16)
        dh = dyf * gl

        a_first_later = jnp.where(seq_last, 0.0, ac_ref[...])
        lam_later = jnp.where(seq_last, 0.0, lamc_ref[...])
        row = lax.broadcasted_iota(jnp.int32, (tt, LRU_WIDTH), 0)
        A = _shift_rows_up(a, 1, a_first_later)
        lm = dh
        ones = jnp.ones((8, LRU_WIDTH), F32)
        zeros = jnp.zeros((8, LRU_WIDTH), F32)
        d = 1
        while d < tt:
            if d < 8:
                a_sh = _shift_rows_up(A, d, ones)
                l_sh = _shift_rows_up(lm, d, zeros)
            else:
                a_sh = jnp.where(row >= tt - d, 1.0, pltpu.roll(A, tt - d, 0))
                l_sh = jnp.where(row >= tt - d, 0.0, pltpu.roll(lm, tt - d, 0))
            lm = lm + A * l_sh
            A = A * a_sh
            d *= 2
        lm = lm + A * lam_later[0:1, :]
        lamc_ref[...] = jnp.broadcast_to(lm[0:1, :], (8, LRU_WIDTH))
        ac_ref[...] = jnp.broadcast_to(a[0:1, :], (8, LRU_WIDTH))

        hprev_halo = jnp.where(seq_first, 0.0, hprev_ref[...])
        h_prev = _shift_rows(hh, 1, hprev_halo)
        da = lm * h_prev
        ixc = i * xc
        dmult = lm * ixc
        di = lm * mult * xc
        dxc = lm * mult * i
        da = da - dmult * a / mult
        dlog = da * a
        dr = dlog * (-LRU_C) * sp
        dsp_part = jnp.sum(dlog * (-LRU_C) * r, axis=0, keepdims=True)
        dpa = dr * r * (1.0 - r)
        dpx = di * i * (1.0 - i)
        dxc = dxc + _dot_nt(dpa, wa_ref[...]) + _dot_nt(dpx, wx_ref[...])
        dwa_part = _dot_tn(xc, dpa)
        dwx_part = _dot_tn(xc, dpx)

        later = jnp.where(seq_last, 0.0, dxc_ref[...])
        dxl = cw_ref[3:4, :] * dxc
        for kk in range(LRU_CONV - 1):
            dxl = dxl + cw_ref[kk:kk + 1, :] * _shift_rows_up(dxc, LRU_CONV - 1 - kk, later)
        dxl_ref[...] = dxl.astype(BF16)
        dxc_ref[...] = dxc[0:8, :]
        dcw_rows = [jnp.sum(dxc * _shift_rows(xl_t, LRU_CONV - 1 - kk, halo), axis=0, keepdims=True)
                    for kk in range(LRU_CONV - 1)]
        dcw_rows.append(jnp.sum(dxc * xl_t, axis=0, keepdims=True))
        dcw_part = jnp.concatenate(dcw_rows + [jnp.zeros((8 - LRU_CONV, LRU_WIDTH), F32)], axis=0)
        lamv = lam_ref[...]
        dlam_part = dsp_part * (-_sigmoid(-lamv))
        parts = ((dcw_ref, dcw_part), (dcb_ref, jnp.sum(dxc, axis=0, keepdims=True)),
                 (dwa_ref, dwa_part), (dba_ref, jnp.sum(dpa, axis=0, keepdims=True)),
                 (dwx_ref, dwx_part), (dbx_ref, jnp.sum(dpx, axis=0, keepdims=True)),
                 (dlam_ref, dlam_part))
        start = jnp.logical_and(b == 0, t == 0)

        @pl.when(start)
        def _():
            for ref, val in parts:
                ref[...] = val

        @pl.when(jnp.logical_not(start))
        def _():
            for ref, val in parts:
                ref[...] += val

    acc = lambda r: pl.BlockSpec((r, LRU_WIDTH), lambda b, t: (0, 0))
    return pl.pallas_call(
        body, name=name, grid=(T // S, nt),
        in_specs=[tile(0), prev8(0), tile(1), tile(0), prev8(0), tile(1),
                  vec(LRU_CONV), vec(1), mat, vec(1), mat, vec(1), vec(1)],
        out_specs=[tile(0), tile(0), acc(8), acc(1), mat, acc(1), mat, acc(1), acc(1)],
        out_shape=[jax.ShapeDtypeStruct((T, LRU_WIDTH), BF16), jax.ShapeDtypeStruct((T, LRU_WIDTH), BF16),
                   jax.ShapeDtypeStruct((8, LRU_WIDTH), F32), jax.ShapeDtypeStruct((1, LRU_WIDTH), F32),
                   jax.ShapeDtypeStruct((LRU_WIDTH, LRU_WIDTH), F32), jax.ShapeDtypeStruct((1, LRU_WIDTH), F32),
                   jax.ShapeDtypeStruct((LRU_WIDTH, LRU_WIDTH), F32), jax.ShapeDtypeStruct((1, LRU_WIDTH), F32),
                   jax.ShapeDtypeStruct((1, LRU_WIDTH), F32)],
        scratch_shapes=[pltpu.VMEM((8, LRU_WIDTH), F32), pltpu.VMEM((8, LRU_WIDTH), F32),
                        pltpu.VMEM((8, LRU_WIDTH), F32)],
        compiler_params=_cparams(("arbitrary", "arbitrary")),
    )(z, z, z, h, h, dy, cw, cb, wa, ba, wx, bx, lam)


FFN_CT = 1408


def _ffn_conv(g, halo, cw_ref, cb_ref):
    gc = cb_ref[...] + cw_ref[2:3, :] * g
    for kk in range(FFN_CONV - 1):
        gc = gc + cw_ref[kk:kk + 1, :] * _shift_rows(g, FFN_CONV - 1 - kk, halo)
    return gc


def _ffn_act_fwd(g, u, cw, cb, *, S, name):
    T, F = g.shape
    tt = min(ROW_TILE, S)
    nt = S // tt
    tc = _tile(F, FFN_CT)

    def body(g_ref, halo_ref, u_ref, cw_ref, cb_ref, o_ref):
        first = (pl.program_id(0) % nt) == 0
        halo = jnp.where(first, 0.0, halo_ref[...])
        gc = _ffn_conv(g_ref[...], halo, cw_ref, cb_ref)
        o_ref[...] = (_gelu(gc) * u_ref[...]).astype(BF16)

    tile = pl.BlockSpec((tt, tc), lambda i, j: (i, j))
    prev8 = pl.BlockSpec((8, tc), lambda i, j: (jnp.maximum(i * (tt // 8) - 1, 0), j))
    return pl.pallas_call(
        body, name=name, grid=(T // tt, F // tc),
        in_specs=[tile, prev8, tile, pl.BlockSpec((FFN_CONV, tc), lambda i, j: (0, j)),
                  pl.BlockSpec((1, tc), lambda i, j: (0, j))],
        out_specs=tile,
        out_shape=jax.ShapeDtypeStruct((T, F), BF16),
        compiler_params=_cparams(("parallel", "parallel")),
    )(g, g, u, cw, cb)


def _ffn_act_bwd(g, u, dact, cw, cb, *, S, name):
    T, F = g.shape
    tt = min(ROW_TILE, S)
    nt = S // tt
    ntt = T // tt
    tc = _tile(F, FFN_CT)

    def body(g_ref, halo_ref, u_ref, da_ref, cw_ref, cb_ref, dg_ref, du_ref, dcw_ref, dcb_ref, later_ref):
        step = pl.program_id(1)
        ti = (ntt - 1 - step) % nt
        halo = jnp.where(ti == 0, 0.0, halo_ref[...])
        gt = g_ref[...]
        gc = _ffn_conv(gt, halo, cw_ref, cb_ref)
        gl, dgl = _gelu_and_grad(gc)
        da = da_ref[...].astype(F32)
        du_ref[...] = (da * gl).astype(BF16)
        dgc = da * u_ref[...] * dgl
        later = jnp.where(ti == nt - 1, 0.0, later_ref[...])
        dg = cw_ref[2:3, :] * dgc
        for kk in range(FFN_CONV - 1):
            dg = dg + cw_ref[kk:kk + 1, :] * _shift_rows_up(dgc, FFN_CONV - 1 - kk, later)
        dg_ref[...] = dg.astype(BF16)
        later_ref[...] = dgc[0:8, :]
        rows = [jnp.sum(dgc * _shift_rows(gt, FFN_CONV - 1 - kk, halo), axis=0, keepdims=True)
                for kk in range(FFN_CONV - 1)]
        rows.append(jnp.sum(dgc * gt, axis=0, keepdims=True))
        dcw_part = jnp.concatenate(rows + [jnp.zeros((8 - FFN_CONV, tc), F32)], axis=0)
        dcb_part = jnp.sum(dgc, axis=0, keepdims=True)

        @pl.when(step == 0)
        def _():
            dcw_ref[...] = dcw_part
            dcb_ref[...] = dcb_part

        @pl.when(step > 0)
        def _():
            dcw_ref[...] += dcw_part
            dcb_ref[...] += dcb_part

    tile = pl.BlockSpec((tt, tc), lambda j, s: (ntt - 1 - s, j))
    prev8 = pl.BlockSpec((8, tc), lambda j, s: (jnp.maximum((ntt - 1 - s) * (tt // 8) - 1, 0), j))
    return pl.pallas_call(
        body, name=name, grid=(F // tc, ntt),
        in_specs=[tile, prev8, tile, tile, pl.BlockSpec((FFN_CONV, tc), lambda j, s: (0, j)),
                  pl.BlockSpec((1, tc), lambda j, s: (0, j))],
        out_specs=[tile, tile, pl.BlockSpec((8, tc), lambda j, s: (0, j)), pl.BlockSpec((1, tc), lambda j, s: (0, j))],
        out_shape=[jax.ShapeDtypeStruct((T, F), BF16), jax.ShapeDtypeStruct((T, F), BF16),
                   jax.ShapeDtypeStruct((8, F), F32), jax.ShapeDtypeStruct((1, F), F32)],
        scratch_shapes=[pltpu.VMEM((8, tc), F32)],
        compiler_params=_cparams(("arbitrary", "arbitrary")),
    )(g, g, u, dact, cw, cb)


def _sgu_norm(zv, g_ref, b_ref):
    v = _gelu(zv)
    mu = jnp.mean(v, axis=-1, keepdims=True)
    xc = v - mu
    rstd = lax.rsqrt(jnp.mean(xc * xc, axis=-1, keepdims=True) + NORM_EPS)
    xhat = xc * rstd
    return xhat, rstd, xhat * g_ref[...] + b_ref[...]


def _sgu_fwd(zc, ln_g, ln_b, wm, bmap, *, name):
    T = zc.shape[0]
    W = SGU_WIDTH
    tt = ROW_TILE
    nch = tt // CHUNK

    def body(z_ref, g_ref, b_ref, wm_ref, bm_ref, p_ref):
        u = _gelu(z_ref[:, :W])
        _, _, vn = _sgu_norm(z_ref[:, W:], g_ref, b_ref)
        vn = vn.astype(BF16)
        for n in range(nch):
            rows = slice(n * CHUNK, (n + 1) * CHUNK)
            for gi in range(SGU_GROUPS):
                cols = slice(gi * LANES, (gi + 1) * LANES)
                s = _dot_nn(wm_ref[gi], vn[rows, cols]) + bm_ref[:, cols]
                p_ref[rows, cols] = (u[rows, cols] * s).astype(BF16)

    const2 = lambda r, c: pl.BlockSpec((r, c), lambda i: (0, 0))
    return pl.pallas_call(
        body, name=name, grid=(T // tt,),
        in_specs=[pl.BlockSpec((tt, 2 * W), lambda i: (i, 0)), const2(1, W), const2(1, W),
                  pl.BlockSpec((SGU_GROUPS, CHUNK, CHUNK), lambda i: (0, 0, 0)), const2(CHUNK, W)],
        out_specs=pl.BlockSpec((tt, W), lambda i: (i, 0)),
        out_shape=jax.ShapeDtypeStruct((T, W), BF16),
        compiler_params=_cparams(("parallel",)),
    )(zc, ln_g, ln_b, wm, bmap)


def _sgu_bwd(zc, dp, ln_g, ln_b, wm, bmap, *, name):
    T = zc.shape[0]
    W = SGU_WIDTH
    tt = ROW_TILE
    nch = tt // CHUNK
    nsteps = T // tt

    def body(z_ref, dp_ref, g_ref, b_ref, wm_ref, bm_ref, dz_ref, dg_ref, db_ref, dwm_ref, dbm_ref,
             s_scr, dvn_scr):
        step = pl.program_id(0)
        zu = z_ref[:, :W]
        zv = z_ref[:, W:]
        u, dgu = _gelu_and_grad(zu)
        xhat, rstd, vn = _sgu_norm(zv, g_ref, b_ref)
        vnb = vn.astype(BF16)
        dpf = dp_ref[...].astype(F32)
        ds = dpf * u

        @pl.when(step == 0)
        def _():
            dwm_ref[...] = jnp.zeros_like(dwm_ref)
            dbm_ref[...] = jnp.zeros_like(dbm_ref)

        for n in range(nch):
            rows = slice(n * CHUNK, (n + 1) * CHUNK)
            for gi in range(SGU_GROUPS):
                cols = slice(gi * LANES, (gi + 1) * LANES)
                s_scr[rows, cols] = _dot_nn(wm_ref[gi], vnb[rows, cols]) + bm_ref[:, cols]
                dsb = ds[rows, cols]
                dvn_scr[rows, cols] = _dot_tn(wm_ref[gi], dsb)
                dwm_ref[gi] += _dot_nt(dsb, vnb[rows, cols])
                dbm_ref[:, cols] += dsb
        dz_ref[:, :W] = (dpf * s_scr[...] * dgu).astype(BF16)
        dvn = dvn_scr[...]
        dxhat = dvn * g_ref[...]
        dv = rstd * (dxhat - jnp.mean(dxhat, axis=-1, keepdims=True)
                     - xhat * jnp.mean(dxhat * xhat, axis=-1, keepdims=True))
        _, dgv = _gelu_and_grad(zv)
        dz_ref[:, W:] = (dv * dgv).astype(BF16)
        dg_part = jnp.sum(dvn * xhat, axis=0, keepdims=True)
        db_part = jnp.sum(dvn, axis=0, keepdims=True)

        @pl.when(step == 0)
        def _():
            dg_ref[...] = dg_part
            db_ref[...] = db_part

        @pl.when(step > 0)
        def _():
            dg_ref[...] += dg_part
            db_ref[...] += db_part

        @pl.when(step == nsteps - 1)
        def _():
            for gi in range(SGU_GROUPS):
                cols = slice(gi * LANES, (gi + 1) * LANES)
                tot = jnp.sum(dbm_ref[:, cols], axis=1, keepdims=True)
                dbm_ref[:, cols] = jnp.broadcast_to(tot, (CHUNK, LANES))

    const2 = lambda r, c: pl.BlockSpec((r, c), lambda i: (0, 0))
    wspec = pl.BlockSpec((SGU_GROUPS, CHUNK, CHUNK), lambda i: (0, 0, 0))
    return pl.pallas_call(
        body, name=name, grid=(nsteps,),
        in_specs=[pl.BlockSpec((tt, 2 * W), lambda i: (i, 0)), pl.BlockSpec((tt, W), lambda i: (i, 0)),
                  const2(1, W), const2(1, W), wspec, const2(CHUNK, W)],
        out_specs=[pl.BlockSpec((tt, 2 * W), lambda i: (i, 0)), const2(1, W), const2(1, W), wspec, const2(CHUNK, W)],
        out_shape=[jax.ShapeDtypeStruct((T, 2 * W), BF16), jax.ShapeDtypeStruct((1, W), F32),
                   jax.ShapeDtypeStruct((1, W), F32), jax.ShapeDtypeStruct((SGU_GROUPS, CHUNK, CHUNK), F32),
                   jax.ShapeDtypeStruct((CHUNK, W), F32)],
        scratch_shapes=[pltpu.VMEM((tt, W), F32), pltpu.VMEM((tt, W), F32)],
        compiler_params=_cparams(("arbitrary",)),
    )(zc, dp, ln_g, ln_b, wm, bmap)


def _rope_tables(positions):
    half = QK_ROPE // 2
    inv_freq = jnp.exp(-math.log(ROPE_BASE) * jnp.arange(half, dtype=F32) / half)
    ang = positions.reshape(-1).astype(F32)[:, None] * inv_freq
    cos = jnp.cos(ang)
    sin = jnp.sin(ang)
    reps = LANES // QK_ROPE
    return jnp.tile(jnp.concatenate([cos, cos], axis=1), (1, reps)), jnp.tile(jnp.concatenate([-sin, sin], axis=1), (1, reps))


def _to_heads(a, B, S, w):
    return a.reshape(B, S, MLA_HEADS, w).transpose(0, 2, 1, 3)


def _from_heads(a, B, S):
    w = a.shape[-1]
    return a.reshape(B, MLA_HEADS, S, w).transpose(0, 2, 1, 3).reshape(B * S, MLA_HEADS * w)


SGU_GROUP_DIM = SGU_WIDTH // SGU_GROUPS
_O1, _O2, _O3, _O4 = Q_LORA, Q_LORA + KV_LORA, Q_LORA + KV_LORA + QK_ROPE, Q_LORA + KV_LORA + QK_ROPE + LRU_WIDTH
_A0, _A1, _A2, _A3 = 2 * LRU_WIDTH, 2 * LRU_WIDTH + Q_LORA, 2 * LRU_WIDTH + Q_LORA + KV_LORA, 2 * LRU_WIDTH + Q_LORA + KV_LORA + QK_ROPE


def _perm_w_in(w_in):
    return jnp.concatenate([w_in[:, _O3:_O4], w_in[:, _O4:], w_in[:, :_O1], w_in[:, _O1:_O2], w_in[:, _O2:_O3],
                            jnp.zeros((w_in.shape[0], LANES - QK_ROPE), w_in.dtype)], axis=1)


def _unperm_w_in(w):
    return jnp.concatenate([w[:, _A0:_A1], w[:, _A1:_A2], w[:, _A2:_A3], w[:, :LRU_WIDTH], w[:, LRU_WIDTH:_A0]], axis=1)


def _perm_heads(w, d1, d2):
    r = w.shape[0]
    w3 = w.reshape(r, MLA_HEADS, d1 + d2)
    return jnp.concatenate([w3[:, :, :d1].reshape(r, -1), w3[:, :, d1:].reshape(r, -1)], axis=1)


def _unperm_heads(w, d1, d2):
    r = w.shape[0]
    n1 = MLA_HEADS * d1
    return jnp.concatenate([w[:, :n1].reshape(r, MLA_HEADS, d1), w[:, n1:].reshape(r, MLA_HEADS, d2)], axis=2).reshape(r, -1)


def _prep_small(w):
    p = {n: w[n] for n in w if n not in BIG}
    eye = jnp.eye(LRU_HEADS, dtype=F32)
    dense = lambda wg: (wg[:, :, None, :] * eye[:, None, :, None]).reshape(LRU_WIDTH, LRU_WIDTH).astype(BF16)
    p["wa_d"] = dense(w["ab_w_rg_a"][0])
    p["wx_d"] = dense(w["ab_w_rg_x"][0])
    causal = jnp.tril(jnp.ones((CHUNK, CHUNK), F32))
    p["wm"] = (w["c_w_s"][0] * causal).astype(BF16)
    p["bmap"] = jnp.repeat(w["c_b_s"][0].T, SGU_GROUP_DIM, axis=1)
    return p


def _prep_big(ab_w_in, ab_w_q_b, ab_w_kv_b):
    return {"w_in_p": _perm_w_in(ab_w_in).astype(BF16),
            "w_q_p": _perm_heads(ab_w_q_b, QK_NOPE, QK_ROPE).astype(BF16),
            "w_kv_p": _perm_heads(ab_w_kv_b, QK_NOPE, V_HEAD).astype(BF16)}


def _ffn_fwd(h, l, p, S):
    hn = _rms_fwd(h, p["ffn_norm"][l], name=f"ffn{l}_norm")
    g = _mm(hn, p["ffn_gate_t"][l], tb=True, name=f"ffn{l}_gate")
    u = _mm(hn, p["ffn_up_t"][l], tb=True, name=f"ffn{l}_up")
    act = _ffn_act_fwd(g, u, p["ffn_conv_w"][l], p["ffn_conv_b"][l][None], S=S, name=f"ffn{l}_act")
    out = _mm(act, p["ffn_down"][l], res=h, name=f"ffn{l}_down")
    return out, (hn, g, u, act)


def _ffn_bwd(dh, h_in, l, p, saved, S):
    hn, g, u, act = saved
    dact = _mm(dh, p["ffn_down"][l], tb=True, out_dtype=BF16, name=f"ffn{l}_dact")
    dw_down = _mm(act, dh, ta=True, out_dtype=BF16, name=f"ffn{l}_dwdown")
    dg, du, dcw, dcb = _ffn_act_bwd(g, u, dact, p["ffn_conv_w"][l], p["ffn_conv_b"][l][None], S=S, name=f"ffn{l}_dactbwd")
    dhn = _mm(dg, p["ffn_gate_t"][l], name=f"ffn{l}_dhn_g")
    dhn = _mm(du, p["ffn_up_t"][l], res=dhn, name=f"ffn{l}_dhn_u")
    dw_gate_t = _mm(dg, hn, ta=True, out_dtype=BF16, name=f"ffn{l}_dwgate")
    dw_up_t = _mm(du, hn, ta=True, out_dtype=BF16, name=f"ffn{l}_dwup")
    dh_in, dnorm = _rms_bwd(h_in, p["ffn_norm"][l], dhn, res=dh, name=f"ffn{l}_dnorm")
    grads = dict(ffn_norm=dnorm[0], ffn_gate_t=dw_gate_t, ffn_up_t=dw_up_t, ffn_conv_w=dcw[:FFN_CONV],
                 ffn_conv_b=dcb[0], ffn_down=dw_down)
    return dh_in, grads


def _local_step(x, positions, target, p, late_weights=None, early_grads=None):
    B, S, D = x.shape
    T = B * S
    H = MLA_HEADS
    xf = x.reshape(T, D)
    tgt = target.reshape(T, D)
    cos, sin = _rope_tables(positions)

    hn0 = _rms_fwd(xf, p["ab_norm"][0], name="ab_norm")
    z = _mm(hn0, p["w_in_p"], name="ab_in")
    cqn = _rms_fwd(z, p["ab_q_norm"][0], cb=4, name="q_norm")
    ckvn = _rms_fwd(z, p["ab_kv_norm"][0], cb=10, name="kv_norm")
    q = _mm(cqn, p["w_q_p"], name="q_up")
    kv = _mm(ckvn, p["w_kv_p"], name="kv_up")
    qr = _rope_fwd(q, cos, sin, cb0=4, ncb=2, name="q_rope")
    kr = _rope_fwd(z, cos, sin, cb0=11, ncb=1, name="k_rope")
    nope = H * QK_NOPE
    zpad = jnp.zeros((B, H, S, LANES - QK_NOPE - QK_ROPE), F32)
    q_h = jnp.concatenate([_to_heads(q[:, :nope], B, S, QK_NOPE), _to_heads(qr, B, S, QK_ROPE), zpad], axis=-1)
    k_h = jnp.concatenate([_to_heads(kv[:, :nope], B, S, QK_NOPE),
                           jnp.broadcast_to(kr[:, :QK_ROPE].reshape(B, 1, S, QK_ROPE), (B, H, S, QK_ROPE)), zpad], axis=-1)
    q_h = (q_h * _attn_scale()).reshape(B * H, S, LANES).astype(BF16)
    k_h = k_h.reshape(B * H, S, LANES).astype(BF16)
    v_h = _to_heads(kv[:, nope:], B, S, V_HEAD).reshape(B * H, S, V_HEAD).astype(BF16)
    if late_weights is None:
        o_h, lse = _flash_fwd(q_h, k_h, v_h, name="attn_fwd")
    else:
        o_h, lse, *arrived = _flash_fwd(q_h, k_h, v_h, name="attn_fwd", exchange=late_weights[0])
        p = {**p, **late_weights[1](arrived)}
    y_mla = _from_heads(o_h, B, S).astype(BF16)
    lru_par = (p["ab_conv_w"][0], p["ab_conv_b"], p["wa_d"], p["ab_b_rg_a"], p["wx_d"], p["ab_b_rg_x"], p["ab_lambda"])
    y_lru, hs = _lru_fwd(z, *lru_par, S=S, name="lru_fwd")
    mix = jnp.concatenate([y_mla, y_lru], axis=1)
    h1 = _mm(mix, p["ab_w_out"], res=xf, name="ab_out")
    h2, ffn0 = _ffn_fwd(h1, 0, p, S)

    hn2 = _rms_fwd(h2, p["c_norm"][0], name="c_norm")
    zc = _mm(hn2, p["c_w_in_t"], tb=True, name="c_in")
    pg = _sgu_fwd(zc, p["c_ln_g"], p["c_ln_b"], p["wm"], p["bmap"], name="sgu_fwd")
    h3 = _mm(pg, p["c_w_out"], res=h2, name="c_out")
    h4, ffn1 = _ffn_fwd(h3, 1, p, S)

    loss_row, dh4, dfinal = _final_fwd_bwd(h4, p["final_norm"], tgt, name="final")

    dh3, g_ffn1 = _ffn_bwd(dh4, h3, 1, p, ffn1, S)
    dpg = _mm(dh3, p["c_w_out"], tb=True, out_dtype=BF16, name="c_dp")
    dw_c_out = _mm(pg, dh3, ta=True, out_dtype=BF16, name="c_dwout")
    dzc, dlng, dlnb, dwm, dbm = _sgu_bwd(zc, dpg, p["c_ln_g"], p["c_ln_b"], p["wm"], p["bmap"], name="sgu_bwd")
    dhn2 = _mm(dzc, p["c_w_in_t"], name="c_dhn")
    dw_c_in_t = _mm(dzc, hn2, ta=True, out_dtype=BF16, name="c_dwin")
    dh2, dcnorm = _rms_bwd(h2, p["c_norm"][0], dhn2, res=dh3, name="c_dnorm")
    dh1, g_ffn0 = _ffn_bwd(dh2, h1, 0, p, ffn0, S)

    dmix = _mm(dh1, p["ab_w_out"], tb=True, name="ab_dmix")
    dw_out = _mm(mix, dh1, ta=True, out_dtype=BF16, name="ab_dwout")
    do_h = _to_heads(dmix[:, :H * V_HEAD], B, S, V_HEAD).reshape(B * H, S, V_HEAD)
    dq_h, delta = _flash_bwd_dq(q_h, k_h, v_h, o_h, lse, do_h, name="attn_dq")
    early = None
    if early_grads is None:
        dk_h, dv_h = _flash_bwd_dkv(q_h, k_h, v_h, lse, delta, do_h, name="attn_dkv")
    else:
        ready = {"c_w_in_t": dw_c_in_t, "c_w_out": dw_c_out}
        for name in ("ffn_gate_t", "ffn_up_t", "ffn_down"):
            ready[name] = [g_ffn0[name], g_ffn1[name]]
        exchange, finish = early_grads(ready)
        dk_h, dv_h, *arrived = _flash_bwd_dkv(q_h, k_h, v_h, lse, delta, do_h, name="attn_dkv", exchange=exchange)
        early = finish(arrived)
    dqr = _rope_bwd(_from_heads(dq_h[..., QK_NOPE:QK_NOPE + QK_ROPE], B, S), cos, sin, head_sum=False, name="q_rope_bwd")
    dkr = _rope_bwd(_from_heads(dk_h[..., QK_NOPE:QK_NOPE + QK_ROPE], B, S), cos, sin, head_sum=True, name="k_rope_bwd")
    dq_full = jnp.concatenate([_from_heads(dq_h[..., :QK_NOPE], B, S), dqr], axis=1).astype(BF16)
    dkv = jnp.concatenate([_from_heads(dk_h[..., :QK_NOPE], B, S), _from_heads(dv_h, B, S)], axis=1).astype(BF16)
    dcqn = _mm(dq_full, p["w_q_p"], tb=True, name="q_dlat")
    dw_q_p = _mm(cqn, dq_full, ta=True, out_dtype=BF16, name="q_dw")
    dckvn = _mm(dkv, p["w_kv_p"], tb=True, name="kv_dlat")
    dw_kv_p = _mm(ckvn, dkv, ta=True, out_dtype=BF16, name="kv_dw")
    dcq, dqnorm = _rms_bwd(z, p["ab_q_norm"][0], dcqn, cb=4, out_dtype=BF16, name="q_dnorm")
    dckv, dkvnorm = _rms_bwd(z, p["ab_kv_norm"][0], dckvn, cb=10, out_dtype=BF16, name="kv_dnorm")
    dxl, dgate, dcw, dcb, dwa, dba, dwx, dbx, dlam = _lru_bwd(z, hs, dmix, *lru_par, S=S, name="lru_bwd")
    dz = jnp.concatenate([dxl, dgate, dcq, dckv, dkr.astype(BF16)], axis=1)
    dhn0 = _mm(dz, p["w_in_p"], tb=True, name="ab_dhn")
    dw_in_p = _mm(hn0, dz, ta=True, out_dtype=BF16, name="ab_dwin")
    dx, dabnorm = _rms_bwd(xf, p["ab_norm"][0], dhn0, res=dh1, name="ab_dnorm")

    blocks = lambda dd: jnp.stack([dd[i * LRU_BLOCK:(i + 1) * LRU_BLOCK, i * LRU_BLOCK:(i + 1) * LRU_BLOCK]
                                   for i in range(LRU_HEADS)])
    causal = jnp.tril(jnp.ones((CHUNK, CHUNK), F32))
    grads = {
        "ab_norm": dabnorm, "w_in_p": dw_in_p, "ab_q_norm": dqnorm, "w_q_p": dw_q_p,
        "ab_kv_norm": dkvnorm, "w_kv_p": dw_kv_p, "ab_conv_w": dcw[:LRU_CONV][None], "ab_conv_b": dcb,
        "ab_w_rg_a": blocks(dwa)[None], "ab_b_rg_a": dba, "ab_w_rg_x": blocks(dwx)[None], "ab_b_rg_x": dbx,
        "ab_lambda": dlam, "ab_w_out": dw_out,
        "c_norm": dcnorm, "c_w_in_t": dw_c_in_t, "c_ln_g": dlng, "c_ln_b": dlnb,
        "c_w_s": (dwm * causal)[None], "c_b_s": dbm[:, ::SGU_GROUP_DIM].T[None], "c_w_out": dw_c_out,
        "final_norm": dfinal[0],
    }
    for name in ("ffn_norm", "ffn_conv_w", "ffn_conv_b"):
        grads[name] = jnp.stack([g_ffn0[name], g_ffn1[name]])
    for name in ("ffn_gate_t", "ffn_up_t", "ffn_down"):
        grads[name] = [g_ffn0[name], g_ffn1[name]]
    return loss_row, dx.reshape(B, S, D), grads, early


ANY = pl.BlockSpec(memory_space=pl.ANY)


def _place():
    x, y, c = lax.axis_index("x"), lax.axis_index("y"), lax.axis_index("c")
    chips = [(1 - x, y), (x, 1 - y), (1 - x, 1 - y)]
    return x, y, c, 2 * x + y, (x, y, 1 - c), chips


def _remote(src, dst, send_sems, recv_sems, k, to):
    return pltpu.make_async_remote_copy(src_ref=src, dst_ref=dst, send_sem=send_sems.at[k], recv_sem=recv_sems.at[k],
                                        device_id=to, device_id_type=MESH)


class _Exchange:
    def __init__(self, arrs, out_shapes, n_sems, start, finish):
        self.arrs, self.out_shapes, self.n_sems, self.start, self.finish = list(arrs), out_shapes, n_sems, start, finish

    @property
    def in_specs(self):
        return [ANY] * len(self.arrs)

    @property
    def out_specs(self):
        return [ANY] * len(self.out_shapes)

    @property
    def scratch(self):
        return [pltpu.SemaphoreType.DMA((self.n_sems,)), pltpu.SemaphoreType.DMA((self.n_sems,))]

    def split(self, refs):
        n = len(self.arrs)
        return refs[:n], refs[n:n + len(self.out_shapes)], refs[-2], refs[-1]

    def run(self, name):
        def body(*refs):
            parts = self.split(refs)
            self.start(*parts)
            self.finish(*parts)

        return pl.pallas_call(body, name=name, in_specs=self.in_specs, out_specs=self.out_specs,
                              out_shape=self.out_shapes, scratch_shapes=self.scratch)(*self.arrs)


def _put(buf, piece, idx, axis):
    return lax.dynamic_update_slice_in_dim(buf, jnp.expand_dims(piece, axis).astype(buf.dtype), idx, axis)


def _all_gather(arrs):
    n = len(arrs)

    def start(ins, outs, send_sems, recv_sems):
        x, y, c, j, sib, chips = _place()
        for i in range(n):
            for k, (cx, cy) in enumerate(chips):
                _remote(ins[i].at[:, c], outs[i].at[:, j, c], send_sems, recv_sems, 6 * i + k, (cx, cy, c)).start()

    def finish(ins, outs, send_sems, recv_sems):
        x, y, c, j, sib, chips = _place()
        passed = []
        for i in range(n):
            for k, (cx, cy) in enumerate(chips):
                got = outs[i].at[:, 2 * cx + cy, c]
                _remote(got, got, send_sems, recv_sems, 6 * i + k, (cx, cy, c)).wait_recv()
                cp = _remote(got, got, send_sems, recv_sems, 6 * i + 3 + k, sib)
                cp.start()
                passed.append(cp)
        for i in range(n):
            for k, (cx, cy) in enumerate(chips):
                got = outs[i].at[:, 2 * cx + cy, 1 - c]
                _remote(got, got, send_sems, recv_sems, 6 * i + 3 + k, sib).wait_recv()
                _remote(ins[i].at[:, c], ins[i].at[:, c], send_sems, recv_sems, 6 * i + k, sib).wait_send()
        for cp in passed:
            cp.wait_send()

    shapes = [jax.ShapeDtypeStruct((a.shape[0], N_CHIPS) + a.shape[1:], a.dtype) for a in arrs]
    return _Exchange(arrs, shapes, 6 * n, start, finish)


def _pair_swap(arrs):
    n = len(arrs)

    def start(ins, outs, send_sems, recv_sems):
        x, y, c, j, sib, chips = _place()
        for i in range(n):
            _remote(ins[i].at[:, 1 - c], outs[i], send_sems, recv_sems, i, sib).start()

    def finish(ins, outs, send_sems, recv_sems):
        x, y, c, j, sib, chips = _place()
        for i in range(n):
            _remote(ins[i].at[:, 1 - c], outs[i], send_sems, recv_sems, i, sib).wait()

    shapes = [jax.ShapeDtypeStruct((a.shape[0],) + a.shape[2:], a.dtype) for a in arrs]
    return _Exchange(arrs, shapes, n, start, finish)


def _pair_send(arrs):
    n = len(arrs)

    def start(ins, outs, send_sems, recv_sems):
        x, y, c, j, sib, chips = _place()
        for i in range(n):
            _remote(ins[i], outs[i], send_sems, recv_sems, i, sib).start()

    def finish(ins, outs, send_sems, recv_sems):
        x, y, c, j, sib, chips = _place()
        for i in range(n):
            _remote(ins[i], outs[i], send_sems, recv_sems, i, sib).wait()

    shapes = [jax.ShapeDtypeStruct(a.shape, a.dtype) for a in arrs]
    return _Exchange(arrs, shapes, n, start, finish)


def _chip_exchange(arrs, *, scatter):
    n = len(arrs)

    def copies(ins, outs, send_sems, recv_sems):
        x, y, c, j, sib, chips = _place()
        return [(_remote(ins[i].at[2 * cx + cy] if scatter else ins[i], outs[i].at[j], send_sems, recv_sems,
                         3 * i + k, (cx, cy, c)),
                 _remote(outs[i].at[2 * cx + cy], outs[i].at[2 * cx + cy], send_sems, recv_sems, 3 * i + k, (cx, cy, c)))
                for i in range(n) for k, (cx, cy) in enumerate(chips)]

    def start(*refs):
        for out, _ in copies(*refs):
            out.start()

    def finish(*refs):
        for out, back in copies(*refs):
            back.wait_recv()
            out.wait_send()

    shapes = [jax.ShapeDtypeStruct((N_CHIPS,) + a.shape[-2:], a.dtype) for a in arrs]
    return _Exchange(arrs, shapes, 3 * n, start, finish)


FLAT_ROWS = 512


def _add2(a, b, *, out_dtype, name):
    n, R, L = a.shape
    tr = _tile(R, FLAT_ROWS, 16)

    def body(a_ref, b_ref, o_ref):
        o_ref[...] = (a_ref[...].astype(F32) + b_ref[...].astype(F32)).astype(out_dtype)

    spec = pl.BlockSpec((n, tr, L), lambda i: (0, i, 0))
    return pl.pallas_call(
        body, name=name, grid=(R // tr,), in_specs=[spec, spec], out_specs=spec,
        out_shape=jax.ShapeDtypeStruct(a.shape, out_dtype), compiler_params=_cparams(("parallel",)),
    )(a, b)


def _sum_slots(buf, *, name):
    n, R, L = buf.shape
    tr = _tile(R, FLAT_ROWS, 16)

    def body(b_ref, o_ref):
        acc = b_ref[0].astype(F32)
        for k in range(1, n):
            acc = acc + b_ref[k].astype(F32)
        o_ref[...] = acc

    return pl.pallas_call(
        body, name=name, grid=(R // tr,), in_specs=[pl.BlockSpec((n, tr, L), lambda i: (0, i, 0))],
        out_specs=pl.BlockSpec((tr, L), lambda i: (i, 0)),
        out_shape=jax.ShapeDtypeStruct((R, L), F32), compiler_params=_cparams(("parallel",)),
    )(buf)


def _adamw(w, g, m, v, *, name):
    R, L = w.shape
    tr = _tile(R, FLAT_ROWS, 16)
    c1 = 1.0 - ADAM_B1 ** ADAM_STEP
    c2 = 1.0 - ADAM_B2 ** ADAM_STEP

    def body(w_ref, g_ref, m_ref, v_ref, d_ref, nm_ref, nv_ref):
        gg = g_ref[...]
        mm = ADAM_B1 * m_ref[...] + (1.0 - ADAM_B1) * gg
        vv = ADAM_B2 * v_ref[...] + (1.0 - ADAM_B2) * (gg * gg)
        nm_ref[...] = mm
        nv_ref[...] = vv
        d_ref[...] = -ADAM_LR * ((mm / c1) / (jnp.sqrt(vv / c2) + ADAM_EPS) + ADAM_WD * w_ref[...])

    spec = pl.BlockSpec((tr, L), lambda i: (i, 0))
    sh = jax.ShapeDtypeStruct((R, L), F32)
    return pl.pallas_call(
        body, name=name, grid=(R // tr,), in_specs=[spec] * 4, out_specs=[spec] * 3, out_shape=[sh] * 3,
        compiler_params=_cparams(("parallel",)),
    )(w, g, m, v)


WEIGHT_NAMES = ["ab_norm", "ab_w_in", "ab_q_norm", "ab_w_q_b", "ab_kv_norm", "ab_w_kv_b", "ab_conv_w", "ab_conv_b",
                "ab_w_rg_a", "ab_b_rg_a", "ab_w_rg_x", "ab_b_rg_x", "ab_lambda", "ab_w_out", "c_norm", "c_w_in",
                "c_ln_g", "c_ln_b", "c_w_s", "c_b_s", "c_w_out", "ffn_norm", "ffn_w_gate", "ffn_w_up", "ffn_conv_w",
                "ffn_conv_b", "ffn_w_down", "final_norm"]
BIG = {"ab_w_in": 2, "ab_w_q_b": 2, "ab_w_kv_b": 2, "ab_w_out": 1, "c_w_in": 2, "c_w_out": 1,
       "ffn_w_gate": 2, "ffn_w_up": 2, "ffn_w_down": 1}
SMALL_SHARDED = {"ab_conv_w": 2, "c_norm": 1, "c_ln_g": 1, "c_ln_b": 1, "ffn_conv_w": 2}
SMALL_REPLICATED = [n for n in WEIGHT_NAMES if n not in BIG and n not in SMALL_SHARDED]


def _rows(n_elems, mult):
    r = -(-n_elems // LANES)
    return -(-r // mult) * mult


def _flat(parts, rows):
    flat = jnp.concatenate([a.reshape(-1) for a in parts])
    return jnp.pad(flat, (0, rows * LANES - flat.shape[0])).reshape(rows, LANES)


def _unflat(flat, shapes):
    flat = flat.reshape(-1)
    out, off = [], 0
    for s in shapes:
        n = math.prod(s)
        out.append(flat[off:off + n].reshape(s))
        off += n
    return out


def _join_shards(a, axis):
    a = jnp.moveaxis(a, 0, axis)
    return a.reshape(a.shape[:axis] + (a.shape[axis] * a.shape[axis + 1],) + a.shape[axis + 2:])


def kernel(x, positions, ab_norm, ab_w_in, ab_q_norm, ab_w_q_b, ab_kv_norm, ab_w_kv_b, ab_conv_w, ab_conv_b, ab_w_rg_a, ab_b_rg_a, ab_w_rg_x, ab_b_rg_x, ab_lambda, ab_w_out, c_norm, c_w_in, c_ln_g, c_ln_b, c_w_s, c_b_s, c_w_out, ffn_norm, ffn_w_gate, ffn_w_up, ffn_conv_w, ffn_conv_b, ffn_w_down, final_norm, loss_target, m_ab_norm, m_ab_w_in, m_ab_q_norm, m_ab_w_q_b, m_ab_kv_norm, m_ab_w_kv_b, m_ab_conv_w, m_ab_conv_b, m_ab_w_rg_a, m_ab_b_rg_a, m_ab_w_rg_x, m_ab_b_rg_x, m_ab_lambda, m_ab_w_out, m_c_norm, m_c_w_in, m_c_ln_g, m_c_ln_b, m_c_w_s, m_c_b_s, m_c_w_out, m_ffn_norm, m_ffn_w_gate, m_ffn_w_up, m_ffn_conv_w, m_ffn_conv_b, m_ffn_w_down, m_final_norm, v_ab_norm, v_ab_w_in, v_ab_q_norm, v_ab_w_q_b, v_ab_kv_norm, v_ab_w_kv_b, v_ab_conv_w, v_ab_conv_b, v_ab_w_rg_a, v_ab_b_rg_a, v_ab_w_rg_x, v_ab_b_rg_x, v_ab_lambda, v_ab_w_out, v_c_norm, v_c_w_in, v_c_ln_g, v_c_ln_b, v_c_w_s, v_c_b_s, v_c_w_out, v_ffn_norm, v_ffn_w_gate, v_ffn_w_up, v_ffn_conv_w, v_ffn_conv_b, v_ffn_w_down, v_final_norm):
    given = dict(locals())
    w = {n: given[n] for n in WEIGHT_NAMES}
    m = {n: given["m_" + n] for n in WEIGHT_NAMES}
    v = {n: given["v_" + n] for n in WEIGHT_NAMES}
    c = lax.axis_index("c")
    chip = 2 * lax.axis_index("x") + lax.axis_index("y")

    halves = lambda a: a.reshape(a.shape[0], 2, a.shape[1] // 2, a.shape[2])
    tr = lambda a: jnp.swapaxes(a, 1, 2)
    send = {"ab_w_in": w["ab_w_in"], "ab_w_q_b": w["ab_w_q_b"], "ab_w_kv_b": w["ab_w_kv_b"], "ab_w_out": w["ab_w_out"],
            "c_w_in": tr(w["c_w_in"]), "c_w_out": w["c_w_out"], "ffn_w_gate": tr(w["ffn_w_gate"]),
            "ffn_w_up": tr(w["ffn_w_up"]), "ffn_w_down": w["ffn_w_down"]}
    small_rows = _rows(sum(w[n].size for n in SMALL_SHARDED), 16)
    small_sh = _flat([w[n] for n in SMALL_SHARDED], small_rows).reshape(1, 2, small_rows // 2, LANES)
    first_names = ["ab_w_in", "ab_w_q_b", "ab_w_kv_b", "ab_w_out"]
    late_names = [n for n in BIG if n not in first_names]
    mine = {n: halves(send[n].astype(BF16)) for n in BIG}

    def gathered(names, arrived):
        full = {}
        for n, a in zip(names, arrived):
            a = _put(a, mine[n] if n in mine else small_sh, chip, 1)
            full[n] = a.reshape(a.shape[0], -1, a.shape[-1])
        return full

    got = _all_gather([mine[n] for n in first_names] + [small_sh]).run("gather_first")
    full = gathered(first_names + ["small"], got)
    unshard = lambda a: jnp.swapaxes(a.reshape(N_CHIPS, -1, a.shape[-1]), 0, 1).reshape(-1, N_CHIPS * a.shape[-1])
    p = _prep_big(unshard(full["ab_w_in"][0]), unshard(full["ab_w_q_b"][0]), unshard(full["ab_w_kv_b"][0]))
    p["ab_w_out"] = full["ab_w_out"][0]
    small_full = dict(w)
    off = 0
    small_got = full["small"].reshape(N_CHIPS, -1)
    for n, ax in SMALL_SHARDED.items():
        seg = small_got[:, off:off + w[n].size].reshape((N_CHIPS,) + w[n].shape)
        small_full[n] = _join_shards(seg, ax)
        off += w[n].size
    p.update(_prep_small(small_full))

    def late_weights(arrived):
        full = gathered(late_names, arrived)
        return dict(c_w_in_t=full["c_w_in"][0], c_w_out=full["c_w_out"][0], ffn_gate_t=full["ffn_w_gate"],
                    ffn_up_t=full["ffn_w_up"], ffn_down=full["ffn_w_down"])

    def pair_sums(sharded, tag):
        sharded = [a.reshape(N_CHIPS, 2, -1, a.shape[-1]) for a in sharded]
        from_sib = _pair_swap(sharded).run(f"grad_pair_swap_{tag}")
        own = [lax.dynamic_index_in_dim(a, c, axis=1, keepdims=False) for a in sharded]
        return [_add2(a, b, out_dtype=BF16, name=f"grad_pair_add_{tag}{i}") for i, (a, b) in enumerate(zip(own, from_sib))]

    def chip_sums(pair, arrived, tag):
        own = [lax.dynamic_index_in_dim(a, chip, axis=0, keepdims=False) for a in pair]
        return [_sum_slots(_put(a, o, chip, 0), name=f"grad_chip_sum_{tag}{i}") for i, (a, o) in enumerate(zip(arrived, own))]

    def early_grads(ready):
        pair = pair_sums([ready["c_w_in_t"], ready["c_w_out"], *ready["ffn_gate_t"], *ready["ffn_up_t"],
                          *ready["ffn_down"]], "a")
        return _chip_exchange(pair, scatter=True), lambda arrived: chip_sums(pair, arrived, "a")

    loss_row, grad_x, g, half_late = _local_step(x, positions, loss_target, p, (_all_gather([mine[n] for n in late_names]),
                                                                                 late_weights), early_grads)

    cols = lambda a, n: jnp.swapaxes(a.reshape(a.shape[0], N_CHIPS, n), 0, 1)
    n_in, n_q, n_kv = w["ab_w_in"].shape[2], w["ab_w_q_b"].shape[2], w["ab_w_kv_b"].shape[2]
    pair = pair_sums([cols(_unperm_w_in(g["w_in_p"]), n_in), cols(_unperm_heads(g["w_q_p"], QK_NOPE, QK_ROPE), n_q),
                      cols(_unperm_heads(g["w_kv_p"], QK_NOPE, V_HEAD), n_kv), g["ab_w_out"]], "b")
    half_first = chip_sums(pair, _chip_exchange(pair, scatter=True).run("grad_chip_exchange_b"), "b")
    half = half_first + half_late
    slot = (jnp.arange(2) == c)[:, None, None]
    summed = [jnp.where(slot, a[None], b[None]).reshape(-1, a.shape[-1])
              for a, b in zip(half, _pair_send(half).run("grad_pair_share"))]
    s_in, s_q, s_kv, s_out, s_cin, s_cout, g0, g1, u0, u1, d0, d1 = summed
    grads = {"ab_w_in": s_in[None], "ab_w_q_b": s_q[None], "ab_w_kv_b": s_kv[None], "ab_w_out": s_out[None],
             "c_w_in": s_cin.T[None], "c_w_out": s_cout[None], "ffn_w_gate": jnp.stack([g0.T, g1.T]),
             "ffn_w_up": jnp.stack([u0.T, u1.T]), "ffn_w_down": jnp.stack([d0, d1])}

    small_names = SMALL_REPLICATED + list(SMALL_SHARDED)
    rs = _rows(sum(g[n].size for n in small_names) + LANES, FLAT_ROWS)
    small = _flat([loss_row] + [g[n] for n in small_names], rs)
    from_sib, = _pair_send([small]).run("small_pair_share")
    pair_small = _sum_slots(jnp.where(slot, small[None], from_sib[None]), name="small_pair_sum")
    all_small, = _chip_exchange([pair_small], scatter=False).run("small_chip_exchange")
    small_sum = _sum_slots(_put(all_small, pair_small, chip, 0), name="small_chip_sum")
    small_parts = _unflat(small_sum, [(1, LANES)] + [g[n].shape for n in small_names])
    loss = small_parts[0][0, 0]
    for n, a in zip(small_names, small_parts[1:]):
        if n in SMALL_SHARDED:
            ax = SMALL_SHARDED[n]
            a = lax.dynamic_slice_in_dim(a, chip * w[n].shape[ax], w[n].shape[ax], axis=ax)
        grads[n] = a.reshape(w[n].shape)

    delta, new_m, new_v = {}, {}, {}
    two_d = lambda a: a.reshape(-1, a.shape[-1])
    for n in BIG:
        out = _adamw(two_d(w[n]), two_d(grads[n]), two_d(m[n]), two_d(v[n]), name=f"adamw_{n}")
        delta[n], new_m[n], new_v[n] = (a.reshape(w[n].shape) for a in out)
    small_all = [n for n in WEIGHT_NAMES if n not in BIG]
    ra = _rows(sum(w[n].size for n in small_all), FLAT_ROWS)
    pack = lambda d: _flat([d[n] for n in small_all], ra)
    out = _adamw(pack(w), pack(grads), pack(m), pack(v), name="adamw_small")
    shapes = [w[n].shape for n in small_all]
    for d, flat in zip((delta, new_m, new_v), out):
        d.update(zip(small_all, _unflat(flat, shapes)))
    return (loss, grad_x, *[grads[n] for n in WEIGHT_NAMES], *[delta[n] for n in WEIGHT_NAMES],
            *[new_m[n] for n in WEIGHT_NAMES], *[new_v[n] for n in WEIGHT_NAMES])
```

```python
import functools
import math

import jax
import jax.numpy as jnp
from jax import lax
from jax.experimental import pallas as pl
from jax.experimental.pallas import tpu as pltpu

F32 = jnp.float32
BF16 = jnp.bfloat16
MESH = pl.DeviceIdType.MESH

D_MODEL = 1024
MLA_HEADS = 8
Q_LORA = 256
KV_LORA = 128
QK_NOPE = 64
QK_ROPE = 32
V_HEAD = 64
LRU_WIDTH = 512
LRU_HEADS = 8
LRU_BLOCK = 64
LRU_CONV = 4
LRU_C = 8.0
CHUNK = 128
SGU_GROUPS = 8
SGU_WIDTH = 1024
D_FF = 2816
FFN_CONV = 3
NORM_EPS = 1e-6
ROPE_BASE = 10000.0
AB_IN_PAD = 1536
ADAM_LR = 0.001
ADAM_B1 = 0.9
ADAM_B2 = 0.999
ADAM_EPS = 1e-08
ADAM_WD = 0.01
ADAM_STEP = 10

N_CHIPS = 4
LANES = 128
VMEM_LIMIT = 56 * 1024 * 1024
ROW_TILE = 256
MM_TM, MM_TN, MM_TK = 512, 1536, 2816
MM_TM_T, MM_TK_T = 1408, 1024
GELU_C = math.sqrt(2.0 / math.pi)


def _cparams(sem):
    return pltpu.CompilerParams(dimension_semantics=sem, vmem_limit_bytes=VMEM_LIMIT)


def _tile(n, target, mult=LANES):
    t = (min(n, target) // mult) * mult
    while t >= mult:
        if n % t == 0:
            return t
        t -= mult
    return n


def _gelu(x):
    t = jnp.tanh(GELU_C * (x + 0.044715 * x * x * x))
    return 0.5 * x * (1.0 + t)


def _gelu_and_grad(x):
    x2 = x * x
    t = jnp.tanh(GELU_C * (x + 0.044715 * x * x2))
    g = 0.5 * x * (1.0 + t)
    dg = 0.5 * (1.0 + t) + 0.5 * x * (1.0 - t * t) * GELU_C * (1.0 + 3.0 * 0.044715 * x2)
    return g, dg


def _sigmoid(x):
    return 1.0 / (1.0 + jnp.exp(-x))


def _shift_rows(x, d, fill_rows):
    ext = jnp.concatenate([fill_rows, x], axis=0)
    return pltpu.roll(ext, d, 0)[8:]


def _shift_rows_up(x, d, fill_rows):
    n = x.shape[0]
    ext = jnp.concatenate([x, fill_rows], axis=0)
    return pltpu.roll(ext, n + 8 - d, 0)[:n]


def _dot(a, b, dims):
    return lax.dot_general(a.astype(BF16), b.astype(BF16), (dims, ((), ())), preferred_element_type=F32)


def _dot_nn(a, b):
    return _dot(a, b, ((1,), (0,)))


def _dot_nt(a, b):
    return _dot(a, b, ((1,), (1,)))


def _dot_tn(a, b):
    return _dot(a, b, ((0,), (0,)))


def _mm(a, b, *, name, ta=False, tb=False, res=None, out_dtype=F32):
    if ta:
        K, M = a.shape
    else:
        M, K = a.shape
    N = b.shape[0] if tb else b.shape[1]
    tm = _tile(M, MM_TM_T if ta else MM_TM, LANES if ta else 8)
    tn = _tile(N, MM_TN, LANES)
    tk = _tile(K, MM_TK_T if ta else MM_TK, LANES)
    nk = K // tk
    a_spec = pl.BlockSpec((tk, tm), lambda i, j, k: (k, i)) if ta else pl.BlockSpec((tm, tk), lambda i, j, k: (i, k))
    b_spec = pl.BlockSpec((tn, tk), lambda i, j, k: (j, k)) if tb else pl.BlockSpec((tk, tn), lambda i, j, k: (k, j))
    o_spec = pl.BlockSpec((tm, tn), lambda i, j, k: (i, j))
    dims = ((0,) if ta else (1,), (1,) if tb else (0,))
    has_res = res is not None

    def body(*refs):
        a_ref, b_ref = refs[:2]
        r_ref = refs[2] if has_res else None
        o_ref = refs[3] if has_res else refs[2]
        p = _dot(a_ref[...], b_ref[...], dims)

        def finish(r):
            if has_res:
                r = r + r_ref[...].astype(F32)
            o_ref[...] = r.astype(out_dtype)

        if nk == 1:
            finish(p)
            return
        acc_ref = refs[-1]
        k = pl.program_id(2)

        @pl.when(k == 0)
        def _():
            acc_ref[...] = p

        @pl.when(jnp.logical_and(k > 0, k < nk - 1))
        def _():
            acc_ref[...] += p

        @pl.when(k == nk - 1)
        def _():
            finish(acc_ref[...] + p)

    in_specs = [a_spec, b_spec] + ([o_spec] if has_res else [])
    args = (a, b) + ((res,) if has_res else ())
    return pl.pallas_call(
        body, name=name, grid=(M // tm, N // tn, nk), in_specs=in_specs, out_specs=o_spec,
        out_shape=jax.ShapeDtypeStruct((M, N), out_dtype),
        scratch_shapes=[pltpu.VMEM((tm, tn), F32)] if nk > 1 else [],
        compiler_params=_cparams(("parallel", "parallel", "arbitrary")),
    )(*args)


def _rms_fwd(x, g, *, name, cb=0, out_dtype=BF16):
    T = x.shape[0]
    W = g.shape[-1]
    g = g.reshape(1, W)
    tt = ROW_TILE

    def body(x_ref, g_ref, o_ref):
        xf = x_ref[...].astype(F32)
        rstd = lax.rsqrt(jnp.mean(xf * xf, axis=-1, keepdims=True) + NORM_EPS)
        o_ref[...] = (xf * rstd * g_ref[...]).astype(out_dtype)

    return pl.pallas_call(
        body, name=name, grid=(T // tt,),
        in_specs=[pl.BlockSpec((tt, W), lambda i: (i, cb)), pl.BlockSpec((1, W), lambda i: (0, 0))],
        out_specs=pl.BlockSpec((tt, W), lambda i: (i, 0)),
        out_shape=jax.ShapeDtypeStruct((T, W), out_dtype),
        compiler_params=_cparams(("parallel",)),
    )(x, g)


def _rms_bwd(x, g, dy, *, name, cb=0, res=None, out_dtype=F32):
    T = x.shape[0]
    W = g.shape[-1]
    g = g.reshape(1, W)
    tt = ROW_TILE
    has_res = res is not None

    def body(*refs):
        if has_res:
            x_ref, g_ref, dy_ref, r_ref, dx_ref, dg_ref = refs
        else:
            x_ref, g_ref, dy_ref, dx_ref, dg_ref = refs
        xf = x_ref[...].astype(F32)
        dyf = dy_ref[...].astype(F32)
        rstd = lax.rsqrt(jnp.mean(xf * xf, axis=-1, keepdims=True) + NORM_EPS)
        xhat = xf * rstd
        dxhat = dyf * g_ref[...]
        dx = rstd * (dxhat - xhat * jnp.mean(dxhat * xhat, axis=-1, keepdims=True))
        if has_res:
            dx = dx + r_ref[...].astype(F32)
        dx_ref[...] = dx.astype(out_dtype)
        part = jnp.sum(dyf * xhat, axis=0, keepdims=True)

        @pl.when(pl.program_id(0) == 0)
        def _():
            dg_ref[...] = part

        @pl.when(pl.program_id(0) > 0)
        def _():
            dg_ref[...] += part

    row = pl.BlockSpec((tt, W), lambda i: (i, 0))
    in_specs = [pl.BlockSpec((tt, W), lambda i: (i, cb)), pl.BlockSpec((1, W), lambda i: (0, 0)), row]
    args = (x, g, dy)
    if has_res:
        in_specs.append(row)
        args = args + (res,)
    return pl.pallas_call(
        body, name=name, grid=(T // tt,), in_specs=in_specs,
        out_specs=[row, pl.BlockSpec((1, W), lambda i: (0, 0))],
        out_shape=[jax.ShapeDtypeStruct((T, W), out_dtype), jax.ShapeDtypeStruct((1, W), F32)],
        compiler_params=_cparams(("arbitrary",)),
    )(*args)


def _final_fwd_bwd(h, g, target, *, name):
    T, W = h.shape
    g = g.reshape(1, W)
    tt = ROW_TILE

    def body(x_ref, g_ref, t_ref, loss_ref, dx_ref, dg_ref):
        xf = x_ref[...]
        rstd = lax.rsqrt(jnp.mean(xf * xf, axis=-1, keepdims=True) + NORM_EPS)
        xhat = xf * rstd
        err = xhat * g_ref[...] - t_ref[...]
        lpart = jnp.zeros((1, LANES), F32) + (0.5 / W) * jnp.sum(err * err)
        dyf = err * (1.0 / W)
        dxhat = dyf * g_ref[...]
        dx_ref[...] = rstd * (dxhat - xhat * jnp.mean(dxhat * xhat, axis=-1, keepdims=True))
        part = jnp.sum(dyf * xhat, axis=0, keepdims=True)

        @pl.when(pl.program_id(0) == 0)
        def _():
            dg_ref[...] = part
            loss_ref[...] = lpart

        @pl.when(pl.program_id(0) > 0)
        def _():
            dg_ref[...] += part
            loss_ref[...] += lpart

    row = pl.BlockSpec((tt, W), lambda i: (i, 0))
    return pl.pallas_call(
        body, name=name, grid=(T // tt,),
        in_specs=[row, pl.BlockSpec((1, W), lambda i: (0, 0)), row],
        out_specs=[pl.BlockSpec((1, LANES), lambda i: (0, 0)), row, pl.BlockSpec((1, W), lambda i: (0, 0))],
        out_shape=[jax.ShapeDtypeStruct((1, LANES), F32), jax.ShapeDtypeStruct((T, W), F32),
                   jax.ShapeDtypeStruct((1, W), F32)],
        compiler_params=_cparams(("arbitrary",)),
    )(h, g, target)


def _swap16(x):
    lane = lax.broadcasted_iota(jnp.int32, x.shape, 1)
    return jnp.where((lane % 32) < 16, pltpu.roll(x, LANES - 16, 1), pltpu.roll(x, 16, 1))


def _rope(x, c, s):
    return x * c + _swap16(x) * s


def _rope_t(d, c, s):
    return d * c + _swap16(d * s)


def _rope_q(q, cos, sin, *, name):
    T, W = q.shape
    tt = ROW_TILE
    scale = _attn_scale()

    def body(x_ref, c_ref, s_ref, o_ref):
        o_ref[...] = (_rope(x_ref[...], c_ref[...], s_ref[...]) * scale).astype(BF16)

    tab = pl.BlockSpec((tt, LANES), lambda i, j: (i, 0))
    blk = pl.BlockSpec((tt, LANES), lambda i, j: (i, j))
    return pl.pallas_call(
        body, name=name, grid=(T // tt, W // LANES), in_specs=[blk, tab, tab], out_specs=blk,
        out_shape=jax.ShapeDtypeStruct((T, W), BF16), compiler_params=_cparams(("parallel", "parallel")),
    )(q, cos, sin)


def _rope_q_bwd(dq, cos, sin, *, name):
    T, W = dq.shape
    tt = ROW_TILE

    def body(d_ref, c_ref, s_ref, o_ref):
        o_ref[...] = _rope_t(d_ref[...], c_ref[...], s_ref[...]).astype(BF16)

    tab = pl.BlockSpec((tt, LANES), lambda i, j: (i, 0))
    blk = pl.BlockSpec((tt, LANES), lambda i, j: (i, j))
    return pl.pallas_call(
        body, name=name, grid=(T // tt, W // LANES), in_specs=[blk, tab, tab], out_specs=blk,
        out_shape=jax.ShapeDtypeStruct((T, W), BF16), compiler_params=_cparams(("parallel", "parallel")),
    )(dq, cos, sin)


def _key_blocks(kv, z, cos, sin, *, kpe_block, name):
    T = kv.shape[0]
    tt = ROW_TILE

    def body(kv_ref, z_ref, c_ref, s_ref, o_ref):
        o_ref[...] = (kv_ref[...].astype(F32) + _rope(z_ref[...], c_ref[...], s_ref[...])).astype(BF16)

    tab = pl.BlockSpec((tt, LANES), lambda i, j: (i, 0))
    blk = pl.BlockSpec((tt, LANES), lambda i, j: (i, j))
    return pl.pallas_call(
        body, name=name, grid=(T // tt, MLA_HEADS),
        in_specs=[blk, pl.BlockSpec((tt, LANES), lambda i, j: (i, kpe_block)), tab, tab], out_specs=blk,
        out_shape=jax.ShapeDtypeStruct((T, MLA_HEADS * LANES), BF16), compiler_params=_cparams(("parallel", "parallel")),
    )(kv, z, cos, sin)


def _key_rope_bwd(dk, cos, sin, *, name):
    T = dk.shape[0]
    tt = ROW_TILE

    def body(d_ref, c_ref, s_ref, o_ref):
        d = d_ref[:, :LANES]
        for h in range(1, MLA_HEADS):
            d = d + d_ref[:, h * LANES:(h + 1) * LANES]
        lane = lax.broadcasted_iota(jnp.int32, d.shape, 1)
        d = jnp.where(jnp.logical_and(lane >= QK_NOPE, lane < QK_NOPE + QK_ROPE), d, 0.0)
        o_ref[...] = _rope_t(d, c_ref[...], s_ref[...]).astype(BF16)

    tab = pl.BlockSpec((tt, LANES), lambda i: (i, 0))
    return pl.pallas_call(
        body, name=name, grid=(T // tt,),
        in_specs=[pl.BlockSpec((tt, MLA_HEADS * LANES), lambda i: (i, 0)), tab, tab], out_specs=tab,
        out_shape=jax.ShapeDtypeStruct((T, LANES), BF16), compiler_params=_cparams(("parallel",)),
    )(dk, cos, sin)


ATT_BLOCK = 512


def _attn_scale():
    return float((QK_NOPE + QK_ROPE) ** -0.5)


def _causal_mask(qi, kj, tq, tk):
    row = qi * tq + lax.broadcasted_iota(jnp.int32, (tq, tk), 0)
    col = kj * tk + lax.broadcasted_iota(jnp.int32, (tq, tk), 1)
    return col <= row


def _host_refs(refs, n_in, n_out, ex):
    if ex is None:
        return refs[:n_in], refs[n_in:n_in + n_out], None
    e_in, e_out = len(ex.arrs), len(ex.out_shapes)
    o0 = n_in + e_in
    return refs[:n_in], refs[o0:o0 + n_out], (refs[n_in:o0], refs[o0 + n_out:o0 + n_out + e_out], refs[-2], refs[-1])


def _host_call(body, name, grid, in_specs, out_specs, out_shape, args, ex):
    if ex is not None:
        in_specs, out_specs = in_specs + ex.in_specs, out_specs + ex.out_specs
        out_shape, args = out_shape + ex.out_shapes, tuple(args) + tuple(ex.arrs)
    return pl.pallas_call(
        body, name=name, grid=grid, in_specs=in_specs, out_specs=out_specs, out_shape=out_shape,
        scratch_shapes=ex.scratch if ex is not None else [],
        compiler_params=_cparams(("arbitrary",) * len(grid)),
    )(*args)


PAIRS = MLA_HEADS // 2


def _own_lanes(x, first):
    lane = lax.broadcasted_iota(jnp.int32, x.shape, 1)
    return jnp.where((lane < V_HEAD) if first else (lane >= V_HEAD), x, 0.0)


def _attn_fwd(q, k, kv, *, B, S, v_block0, name, exchange=None):
    tq = tk = min(ATT_BLOCK, S)
    nq = S // tq
    T = B * S

    def body(*refs):
        (q_ref, k_ref, v_ref), (o_ref, lse_ref), ex_parts = _host_refs(refs, 3, 2, exchange)
        b, g, qi = pl.program_id(0), pl.program_id(1), pl.program_id(2)
        if exchange is not None:
            pl.when(b + g + qi == 0)(lambda: exchange.start(*ex_parts))
        qs = (q_ref[:, :LANES], q_ref[:, LANES:])

        def step(masked):
            def f(j, carry):
                rows = pl.ds(pl.multiple_of(j * tk, tk), tk)
                vb = v_ref[rows, :]
                out = []
                for h in range(2):
                    m, l, acc = carry[h]
                    s = _dot_nt(qs[h], k_ref[rows, h * LANES:(h + 1) * LANES])
                    if masked:
                        s = jnp.where(_causal_mask(qi, j, tq, tk), s, -jnp.inf)
                    m_new = jnp.maximum(m, jnp.max(s, axis=-1, keepdims=True))
                    alpha = jnp.exp(m - m_new)
                    p = jnp.exp(s - m_new)
                    out.append((m_new, alpha * l + jnp.sum(p, axis=-1, keepdims=True), alpha * acc + _dot_nn(p, vb)))
                return tuple(out)
            return f

        one = (jnp.full((tq, 1), -1e30, F32), jnp.zeros((tq, 1), F32), jnp.zeros((tq, LANES), F32))
        (ma, la, acca), (mb, lb, accb) = step(True)(qi, lax.fori_loop(0, qi, step(False), (one, one)))
        o_ref[...] = _own_lanes(acca / la, True) + _own_lanes(accb / lb, False)
        lse_ref[0, 0] = ma + jnp.log(la)
        lse_ref[0, 1] = mb + jnp.log(lb)
        if exchange is not None:
            pl.when(jnp.logical_and(b == B - 1, jnp.logical_and(g == PAIRS - 1, qi == nq - 1)))(
                lambda: exchange.finish(*ex_parts))

    return _host_call(
        body, name, (B, PAIRS, nq),
        [pl.BlockSpec((tq, 2 * LANES), lambda b, g, i: (b * nq + i, g)),
         pl.BlockSpec((S, 2 * LANES), lambda b, g, i: (b, g)),
         pl.BlockSpec((S, LANES), lambda b, g, i: (b, v_block0 + g))],
        [pl.BlockSpec((tq, LANES), lambda b, g, i: (b * nq + i, g)),
         pl.BlockSpec((1, 2, tq, 1), lambda b, g, i: (b, g, i, 0))],
        [jax.ShapeDtypeStruct((T, PAIRS * LANES), F32), jax.ShapeDtypeStruct((B, MLA_HEADS, S, 1), F32)],
        (q, k, kv), exchange)


def _attn_dq(q, k, kv, o, lse, do, *, B, S, v_block0, name):
    tq = tk = min(ATT_BLOCK, S)
    nq = S // tq
    T = B * S
    scale = _attn_scale()

    def body(q_ref, k_ref, v_ref, o_ref, lse_ref, do_ref, dq_ref, delta_ref):
        qi = pl.program_id(2)
        qs = (q_ref[:, :LANES], q_ref[:, LANES:])
        dos = (_own_lanes(do_ref[...], True), _own_lanes(do_ref[...], False))
        deltas = tuple(jnp.sum(d * o_ref[...], axis=-1, keepdims=True) for d in dos)
        lses = (lse_ref[0, 0], lse_ref[0, 1])

        def step(masked):
            def f(j, carry):
                rows = pl.ds(pl.multiple_of(j * tk, tk), tk)
                vb = v_ref[rows, :]
                out = []
                for h in range(2):
                    kb = k_ref[rows, h * LANES:(h + 1) * LANES]
                    p = jnp.exp(_dot_nt(qs[h], kb) - lses[h])
                    if masked:
                        p = jnp.where(_causal_mask(qi, j, tq, tk), p, 0.0)
                    ds = p * (_dot_nt(dos[h], vb) - deltas[h])
                    out.append(carry[h] + _dot_nn(ds, kb))
                return tuple(out)
            return f

        zero = jnp.zeros((tq, LANES), F32)
        dqa, dqb = step(True)(qi, lax.fori_loop(0, qi, step(False), (zero, zero)))
        dq_ref[:, :LANES] = dqa * scale
        dq_ref[:, LANES:] = dqb * scale
        delta_ref[0, 0] = deltas[0]
        delta_ref[0, 1] = deltas[1]

    qrow = lambda w: pl.BlockSpec((tq, w), lambda b, g, i: (b * nq + i, g))
    stat = pl.BlockSpec((1, 2, tq, 1), lambda b, g, i: (b, g, i, 0))
    return pl.pallas_call(
        body, name=name, grid=(B, PAIRS, nq),
        in_specs=[qrow(2 * LANES), pl.BlockSpec((S, 2 * LANES), lambda b, g, i: (b, g)),
                  pl.BlockSpec((S, LANES), lambda b, g, i: (b, v_block0 + g)), qrow(LANES), stat, qrow(LANES)],
        out_specs=[qrow(2 * LANES), stat],
        out_shape=[jax.ShapeDtypeStruct((T, MLA_HEADS * LANES), F32), jax.ShapeDtypeStruct((B, MLA_HEADS, S, 1), F32)],
        compiler_params=_cparams(("parallel", "parallel", "parallel")),
    )(q, k, kv, o, lse, do)


def _attn_dkv(q, k, kv, lse_rows, delta_rows, do, *, B, S, v_block0, name, exchange=None):
    tq = tk = min(ATT_BLOCK, S)
    nq = S // tq
    T = B * S

    def body(*refs):
        (q_ref, k_ref, v_ref, lse_ref, delta_ref, do_ref), (dk_ref, dv_ref), ex_parts = _host_refs(refs, 6, 2, exchange)
        b, g, kj = pl.program_id(0), pl.program_id(1), pl.program_id(2)
        if exchange is not None:
            pl.when(b + g + kj == 0)(lambda: exchange.start(*ex_parts))
        ks = (k_ref[:, :LANES], k_ref[:, LANES:])
        vb = v_ref[...]

        def step(masked):
            def f(i, carry):
                rows = pl.ds(pl.multiple_of(i * tq, tq), tq)
                do_b = do_ref[rows, :]
                dks, dv = list(carry[:2]), carry[2]
                for h in range(2):
                    qb = q_ref[rows, h * LANES:(h + 1) * LANES]
                    doh = _own_lanes(do_b, h == 0)
                    pt = jnp.exp(_dot_nt(ks[h], qb) - lse_ref[0, h, pl.ds(i, 1), :])
                    if masked:
                        krow = kj * tk + lax.broadcasted_iota(jnp.int32, (tk, tq), 0)
                        qcol = i * tq + lax.broadcasted_iota(jnp.int32, (tk, tq), 1)
                        pt = jnp.where(krow <= qcol, pt, 0.0)
                    dst = pt * (_dot_nt(vb, doh) - delta_ref[0, h, pl.ds(i, 1), :])
                    dks[h] = dks[h] + _dot_nn(dst, qb)
                    dv = dv + _dot_nn(pt, doh)
                return dks[0], dks[1], dv
            return f

        zero = jnp.zeros((tk, LANES), F32)
        dka, dkb, dv = lax.fori_loop(kj + 1, nq, step(False), step(True)(kj, (zero, zero, zero)))
        dk_ref[:, :LANES] = dka
        dk_ref[:, LANES:] = dkb
        dv_ref[...] = dv
        if exchange is not None:
            pl.when(jnp.logical_and(b == B - 1, jnp.logical_and(g == PAIRS - 1, kj == nq - 1)))(
                lambda: exchange.finish(*ex_parts))

    krow = lambda w, c0: pl.BlockSpec((tk, w), lambda b, g, j: (b * nq + j, c0 + g))
    seq = lambda w: pl.BlockSpec((S, w), lambda b, g, j: (b, g))
    stat = pl.BlockSpec((1, 2, nq, tq), lambda b, g, j: (b, g, 0, 0))
    return _host_call(
        body, name, (B, PAIRS, nq), [seq(2 * LANES), krow(2 * LANES, 0), krow(LANES, v_block0), stat, stat, seq(LANES)],
        [krow(2 * LANES, 0), krow(LANES, 0)],
        [jax.ShapeDtypeStruct((T, MLA_HEADS * LANES), F32), jax.ShapeDtypeStruct((T, PAIRS * LANES), F32)],
        (q, k, kv, lse_rows, delta_rows, do), exchange)


def _lru_gates(xl, halo, cw_ref, cb_ref, wa_ref, ba_ref, wx_ref, bx_ref, lam_ref):
    xc = cb_ref[...] + cw_ref[3:4, :] * xl
    for kk in range(LRU_CONV - 1):
        xc = xc + cw_ref[kk:kk + 1, :] * _shift_rows(xl, LRU_CONV - 1 - kk, halo)
    r = _sigmoid(_dot_nn(xc, wa_ref[...]) + ba_ref[...])
    i = _sigmoid(_dot_nn(xc, wx_ref[...]) + bx_ref[...])
    lam = lam_ref[...]
    sp = jnp.maximum(-lam, 0.0) + jnp.log(1.0 + jnp.exp(-jnp.abs(lam)))
    a = jnp.exp(-LRU_C * r * sp)
    mult = jnp.sqrt(1.0 - a * a)
    return xc, r, i, sp, a, mult


def _lru_specs(tt, nt, S):
    def make(rev):
        tmap = (lambda t: nt - 1 - t) if rev else (lambda t: t)
        tile = lambda cb: pl.BlockSpec((tt, LRU_WIDTH), lambda b, t: (b * nt + tmap(t), cb))
        prev8 = lambda cb: pl.BlockSpec(
            (8, LRU_WIDTH), lambda b, t: (jnp.maximum((b * nt + tmap(t)) * (tt // 8) - 1, 0), cb))
        return tile, prev8, tmap
    return make


def _lru_fwd(z, cw, cb, wa, ba, wx, bx, lam, *, S, name):
    T = z.shape[0]
    tt = min(ROW_TILE, S)
    nt = S // tt
    tile, prev8, _ = _lru_specs(tt, nt, S)(False)
    vec = lambda r: pl.BlockSpec((r, LRU_WIDTH), lambda b, t: (0, 0))
    mat = pl.BlockSpec((LRU_WIDTH, LRU_WIDTH), lambda b, t: (0, 0))

    def body(xl_ref, halo_ref, gate_ref, cw_ref, cb_ref, wa_ref, ba_ref, wx_ref, bx_ref, lam_ref,
             y_ref, h_ref, carry_ref):
        t = pl.program_id(1)
        first = t == 0
        halo = jnp.where(first, 0.0, halo_ref[...])
        xl_t = xl_ref[...]
        xc, r, i, sp, a, mult = _lru_gates(xl_t, halo, cw_ref, cb_ref, wa_ref, ba_ref, wx_ref, bx_ref, lam_ref)
        bv = mult * (i * xc)
        ones = jnp.ones((8, LRU_WIDTH), F32)
        zeros = jnp.zeros((8, LRU_WIDTH), F32)
        row = lax.broadcasted_iota(jnp.int32, (tt, LRU_WIDTH), 0)
        A = a
        d = 1
        while d < tt:
            if d < 8:
                a_sh = _shift_rows(A, d, ones)
                b_sh = _shift_rows(bv, d, zeros)
            else:
                a_sh = jnp.where(row < d, 1.0, pltpu.roll(A, d, 0))
                b_sh = jnp.where(row < d, 0.0, pltpu.roll(bv, d, 0))
            bv = A * b_sh + bv
            A = A * a_sh
            d *= 2
        h0 = jnp.where(first, 0.0, carry_ref[0:1, :])
        h = A * h0 + bv
        carry_ref[...] = jnp.broadcast_to(h[tt - 1:tt, :], (8, LRU_WIDTH))
        h_ref[...] = h
        y_ref[...] = (h * _gelu(gate_ref[...])).astype(BF16)

    return pl.pallas_call(
        body, name=name, grid=(T // S, nt),
        in_specs=[tile(0), prev8(0), tile(1), vec(LRU_CONV), vec(1), mat, vec(1), mat, vec(1), vec(1)],
        out_specs=[tile(0), tile(0)],
        out_shape=[jax.ShapeDtypeStruct((T, LRU_WIDTH), BF16), jax.ShapeDtypeStruct((T, LRU_WIDTH), F32)],
        scratch_shapes=[pltpu.VMEM((8, LRU_WIDTH), F32)],
        compiler_params=_cparams(("arbitrary", "arbitrary")),
    )(z, z, z, cw, cb, wa, ba, wx, bx, lam)


def _lru_bwd(z, h, dy, cw, cb, wa, ba, wx, bx, lam, *, S, name):
    T = z.shape[0]
    tt = min(ROW_TILE, S)
    nt = S // tt
    tile, prev8, tmap = _lru_specs(tt, nt, S)(True)
    vec = lambda r: pl.BlockSpec((r, LRU_WIDTH), lambda b, t: (0, 0))
    mat = pl.BlockSpec((LRU_WIDTH, LRU_WIDTH), lambda b, t: (0, 0))

    def body(xl_ref, halo_ref, gate_ref, h_ref, hprev_ref, dy_ref, cw_ref, cb_ref, wa_ref, ba_ref, wx_ref,
             bx_ref, lam_ref, dxl_ref, dgate_ref, dcw_ref, dcb_ref, dwa_ref, dba_ref, dwx_ref, dbx_ref,
             dlam_ref, lamc_ref, ac_ref, dxc_ref):
        b = pl.program_id(0)
        t = pl.program_id(1)
        tr = nt - 1 - t
        seq_first = tr == 0
        seq_last = t == 0
        halo = jnp.where(seq_first, 0.0, halo_ref[...])
        xl_t = xl_ref[...]
        xc, r, i, sp, a, mult = _lru_gates(xl_t, halo, cw_ref, cb_ref, wa_ref, ba_ref, wx_ref, bx_ref, lam_ref)
        hh = h_ref[...]
        dyf = dy_ref[...].astype(F32)
        gl, dgl = _gelu_and_grad(gate_ref[...])
        dgate_ref[...] = (dyf * hh * dgl).astype(BF16)
        dh = dyf * gl

        a_first_later = jnp.where(seq_last, 0.0, ac_ref[...])
        lam_later = jnp.where(seq_last, 0.0, lamc_ref[...])
        row = lax.broadcasted_iota(jnp.int32, (tt, LRU_WIDTH), 0)
        A = _shift_rows_up(a, 1, a_first_later)
        lm = dh
        ones = jnp.ones((8, LRU_WIDTH), F32)
        zeros = jnp.zeros((8, LRU_WIDTH), F32)
        d = 1
        while d < tt:
            if d < 8:
                a_sh = _shift_rows_up(A, d, ones)
                l_sh = _shift_rows_up(lm, d, zeros)
            else:
                a_sh = jnp.where(row >= tt - d, 1.0, pltpu.roll(A, tt - d, 0))
                l_sh = jnp.where(row >= tt - d, 0.0, pltpu.roll(lm, tt - d, 0))
            lm = lm + A * l_sh
            A = A * a_sh
            d *= 2
        lm = lm + A * lam_later[0:1, :]
        lamc_ref[...] = jnp.broadcast_to(lm[0:1, :], (8, LRU_WIDTH))
        ac_ref[...] = jnp.broadcast_to(a[0:1, :], (8, LRU_WIDTH))

        hprev_halo = jnp.where(seq_first, 0.0, hprev_ref[...])
        h_prev = _shift_rows(hh, 1, hprev_halo)
        da = lm * h_prev
        ixc = i * xc
        dmult = lm * ixc
        di = lm * mult * xc
        dxc = lm * mult * i
        da = da - dmult * a / mult
        dlog = da * a
        dr = dlog * (-LRU_C) * sp
        dsp_part = jnp.sum(dlog * (-LRU_C) * r, axis=0, keepdims=True)
        dpa = dr * r * (1.0 - r)
        dpx = di * i * (1.0 - i)
        dxc = dxc + _dot_nt(dpa, wa_ref[...]) + _dot_nt(dpx, wx_ref[...])
        dwa_part = _dot_tn(xc, dpa)
        dwx_part = _dot_tn(xc, dpx)

        later = jnp.where(seq_last, 0.0, dxc_ref[...])
        dxl = cw_ref[3:4, :] * dxc
        for kk in range(LRU_CONV - 1):
            dxl = dxl + cw_ref[kk:kk + 1, :] * _shift_rows_up(dxc, LRU_CONV - 1 - kk, later)
        dxl_ref[...] = dxl.astype(BF16)
        dxc_ref[...] = dxc[0:8, :]
        dcw_rows = [jnp.sum(dxc * _shift_rows(xl_t, LRU_CONV - 1 - kk, halo), axis=0, keepdims=True)
                    for kk in range(LRU_CONV - 1)]
        dcw_rows.append(jnp.sum(dxc * xl_t, axis=0, keepdims=True))
        dcw_part = jnp.concatenate(dcw_rows + [jnp.zeros((8 - LRU_CONV, LRU_WIDTH), F32)], axis=0)
        lamv = lam_ref[...]
        dlam_part = dsp_part * (-_sigmoid(-lamv))
        parts = ((dcw_ref, dcw_part), (dcb_ref, jnp.sum(dxc, axis=0, keepdims=True)),
                 (dwa_ref, dwa_part), (dba_ref, jnp.sum(dpa, axis=0, keepdims=True)),
                 (dwx_ref, dwx_part), (dbx_ref, jnp.sum(dpx, axis=0, keepdims=True)),
                 (dlam_ref, dlam_part))
        start = jnp.logical_and(b == 0, t == 0)

        @pl.when(start)
        def _():
            for ref, val in parts:
                ref[...] = val

        @pl.when(jnp.logical_not(start))
        def _():
            for ref, val in parts:
                ref[...] += val

    acc = lambda r: pl.BlockSpec((r, LRU_WIDTH), lambda b, t: (0, 0))
    return pl.pallas_call(
        body, name=name, grid=(T // S, nt),
        in_specs=[tile(0), prev8(0), tile(1), tile(0), prev8(0), tile(0),
                  vec(LRU_CONV), vec(1), mat, vec(1), mat, vec(1), vec(1)],
        out_specs=[tile(0), tile(0), acc(8), acc(1), mat, acc(1), mat, acc(1), acc(1)],
        out_shape=[jax.ShapeDtypeStruct((T, LRU_WIDTH), BF16), jax.ShapeDtypeStruct((T, LRU_WIDTH), BF16),
                   jax.ShapeDtypeStruct((8, LRU_WIDTH), F32), jax.ShapeDtypeStruct((1, LRU_WIDTH), F32),
                   jax.ShapeDtypeStruct((LRU_WIDTH, LRU_WIDTH), F32), jax.ShapeDtypeStruct((1, LRU_WIDTH), F32),
                   jax.ShapeDtypeStruct((LRU_WIDTH, LRU_WIDTH), F32), jax.ShapeDtypeStruct((1, LRU_WIDTH), F32),
                   jax.ShapeDtypeStruct((1, LRU_WIDTH), F32)],
        scratch_shapes=[pltpu.VMEM((8, LRU_WIDTH), F32), pltpu.VMEM((8, LRU_WIDTH), F32),
                        pltpu.VMEM((8, LRU_WIDTH), F32)],
        compiler_params=_cparams(("arbitrary", "arbitrary")),
    )(z, z, z, h, h, dy, cw, cb, wa, ba, wx, bx, lam)


FFN_CT = 1408


def _ffn_conv(g, halo, cw_ref, cb_ref):
    gc = cb_ref[...] + cw_ref[2:3, :] * g
    for kk in range(FFN_CONV - 1):
        gc = gc + cw_ref[kk:kk + 1, :] * _shift_rows(g, FFN_CONV - 1 - kk, halo)
    return gc


def _ffn_act_fwd(g, u, cw, cb, *, S, name):
    T, F = g.shape
    tt = min(ROW_TILE, S)
    nt = S // tt
    tc = _tile(F, FFN_CT)

    def body(g_ref, halo_ref, u_ref, cw_ref, cb_ref, o_ref):
        first = (pl.program_id(0) % nt) == 0
        halo = jnp.where(first, 0.0, halo_ref[...])
        gc = _ffn_conv(g_ref[...], halo, cw_ref, cb_ref)
        o_ref[...] = (_gelu(gc) * u_ref[...]).astype(BF16)

    tile = pl.BlockSpec((tt, tc), lambda i, j: (i, j))
    prev8 = pl.BlockSpec((8, tc), lambda i, j: (jnp.maximum(i * (tt // 8) - 1, 0), j))
    return pl.pallas_call(
        body, name=name, grid=(T // tt, F // tc),
        in_specs=[tile, prev8, tile, pl.BlockSpec((FFN_CONV, tc), lambda i, j: (0, j)),
                  pl.BlockSpec((1, tc), lambda i, j: (0, j))],
        out_specs=tile,
        out_shape=jax.ShapeDtypeStruct((T, F), BF16),
        compiler_params=_cparams(("parallel", "parallel")),
    )(g, g, u, cw, cb)


def _ffn_act_bwd(g, u, dact, cw, cb, *, S, name):
    T, F = g.shape
    tt = min(ROW_TILE, S)
    nt = S // tt
    ntt = T // tt
    tc = _tile(F, FFN_CT)

    def body(g_ref, halo_ref, u_ref, da_ref, cw_ref, cb_ref, dg_ref, du_ref, dcw_ref, dcb_ref, later_ref):
        step = pl.program_id(1)
        ti = (ntt - 1 - step) % nt
        halo = jnp.where(ti == 0, 0.0, halo_ref[...])
        gt = g_ref[...]
        gc = _ffn_conv(gt, halo, cw_ref, cb_ref)
        gl, dgl = _gelu_and_grad(gc)
        da = da_ref[...].astype(F32)
        du_ref[...] = (da * gl).astype(BF16)
        dgc = da * u_ref[...] * dgl
        later = jnp.where(ti == nt - 1, 0.0, later_ref[...])
        dg = cw_ref[2:3, :] * dgc
        for kk in range(FFN_CONV - 1):
            dg = dg + cw_ref[kk:kk + 1, :] * _shift_rows_up(dgc, FFN_CONV - 1 - kk, later)
        dg_ref[...] = dg.astype(BF16)
        later_ref[...] = dgc[0:8, :]
        rows = [jnp.sum(dgc * _shift_rows(gt, FFN_CONV - 1 - kk, halo), axis=0, keepdims=True)
                for kk in range(FFN_CONV - 1)]
        rows.append(jnp.sum(dgc * gt, axis=0, keepdims=True))
        dcw_part = jnp.concatenate(rows + [jnp.zeros((8 - FFN_CONV, tc), F32)], axis=0)
        dcb_part = jnp.sum(dgc, axis=0, keepdims=True)

        @pl.when(step == 0)
        def _():
            dcw_ref[...] = dcw_part
            dcb_ref[...] = dcb_part

        @pl.when(step > 0)
        def _():
            dcw_ref[...] += dcw_part
            dcb_ref[...] += dcb_part

    tile = pl.BlockSpec((tt, tc), lambda j, s: (ntt - 1 - s, j))
    prev8 = pl.BlockSpec((8, tc), lambda j, s: (jnp.maximum((ntt - 1 - s) * (tt // 8) - 1, 0), j))
    return pl.pallas_call(
        body, name=name, grid=(F // tc, ntt),
        in_specs=[tile, prev8, tile, tile, pl.BlockSpec((FFN_CONV, tc), lambda j, s: (0, j)),
                  pl.BlockSpec((1, tc), lambda j, s: (0, j))],
        out_specs=[tile, tile, pl.BlockSpec((8, tc), lambda j, s: (0, j)), pl.BlockSpec((1, tc), lambda j, s: (0, j))],
        out_shape=[jax.ShapeDtypeStruct((T, F), BF16), jax.ShapeDtypeStruct((T, F), BF16),
                   jax.ShapeDtypeStruct((8, F), F32), jax.ShapeDtypeStruct((1, F), F32)],
        scratch_shapes=[pltpu.VMEM((8, tc), F32)],
        compiler_params=_cparams(("arbitrary", "arbitrary")),
    )(g, g, u, dact, cw, cb)


def _sgu_norm(zv, g_ref, b_ref):
    v = _gelu(zv)
    mu = jnp.mean(v, axis=-1, keepdims=True)
    xc = v - mu
    rstd = lax.rsqrt(jnp.mean(xc * xc, axis=-1, keepdims=True) + NORM_EPS)
    xhat = xc * rstd
    return xhat, rstd, xhat * g_ref[...] + b_ref[...]


def _sgu_fwd(zc, ln_g, ln_b, wm, bmap, *, name):
    T = zc.shape[0]
    W = SGU_WIDTH
    tt = ROW_TILE
    nch = tt // CHUNK

    def body(z_ref, g_ref, b_ref, wm_ref, bm_ref, p_ref):
        u = _gelu(z_ref[:, :W])
        _, _, vn = _sgu_norm(z_ref[:, W:], g_ref, b_ref)
        vn = vn.astype(BF16)
        for n in range(nch):
            rows = slice(n * CHUNK, (n + 1) * CHUNK)
            for gi in range(SGU_GROUPS):
                cols = slice(gi * LANES, (gi + 1) * LANES)
                s = _dot_nn(wm_ref[gi], vn[rows, cols]) + bm_ref[:, cols]
                p_ref[rows, cols] = (u[rows, cols] * s).astype(BF16)

    const2 = lambda r, c: pl.BlockSpec((r, c), lambda i: (0, 0))
    return pl.pallas_call(
        body, name=name, grid=(T // tt,),
        in_specs=[pl.BlockSpec((tt, 2 * W), lambda i: (i, 0)), const2(1, W), const2(1, W),
                  pl.BlockSpec((SGU_GROUPS, CHUNK, CHUNK), lambda i: (0, 0, 0)), const2(CHUNK, W)],
        out_specs=pl.BlockSpec((tt, W), lambda i: (i, 0)),
        out_shape=jax.ShapeDtypeStruct((T, W), BF16),
        compiler_params=_cparams(("parallel",)),
    )(zc, ln_g, ln_b, wm, bmap)


def _sgu_bwd(zc, dp, ln_g, ln_b, wm, bmap, *, name):
    T = zc.shape[0]
    W = SGU_WIDTH
    tt = ROW_TILE
    nch = tt // CHUNK
    nsteps = T // tt

    def body(z_ref, dp_ref, g_ref, b_ref, wm_ref, bm_ref, dz_ref, dg_ref, db_ref, dwm_ref, dbm_ref,
             s_scr, dvn_scr):
        step = pl.program_id(0)
        zu = z_ref[:, :W]
        zv = z_ref[:, W:]
        u, dgu = _gelu_and_grad(zu)
        xhat, rstd, vn = _sgu_norm(zv, g_ref, b_ref)
        vnb = vn.astype(BF16)
        dpf = dp_ref[...].astype(F32)
        ds = dpf * u

        @pl.when(step == 0)
        def _():
            dwm_ref[...] = jnp.zeros_like(dwm_ref)
            dbm_ref[...] = jnp.zeros_like(dbm_ref)

        for n in range(nch):
            rows = slice(n * CHUNK, (n + 1) * CHUNK)
            for gi in range(SGU_GROUPS):
                cols = slice(gi * LANES, (gi + 1) * LANES)
                s_scr[rows, cols] = _dot_nn(wm_ref[gi], vnb[rows, cols]) + bm_ref[:, cols]
                dsb = ds[rows, cols]
                dvn_scr[rows, cols] = _dot_tn(wm_ref[gi], dsb)
                dwm_ref[gi] += _dot_nt(dsb, vnb[rows, cols])
                dbm_ref[:, cols] += dsb
        dz_ref[:, :W] = (dpf * s_scr[...] * dgu).astype(BF16)
        dvn = dvn_scr[...]
        dxhat = dvn * g_ref[...]
        dv = rstd * (dxhat - jnp.mean(dxhat, axis=-1, keepdims=True)
                     - xhat * jnp.mean(dxhat * xhat, axis=-1, keepdims=True))
        _, dgv = _gelu_and_grad(zv)
        dz_ref[:, W:] = (dv * dgv).astype(BF16)
        dg_part = jnp.sum(dvn * xhat, axis=0, keepdims=True)
        db_part = jnp.sum(dvn, axis=0, keepdims=True)

        @pl.when(step == 0)
        def _():
            dg_ref[...] = dg_part
            db_ref[...] = db_part

        @pl.when(step > 0)
        def _():
            dg_ref[...] += dg_part
            db_ref[...] += db_part

        @pl.when(step == nsteps - 1)
        def _():
            for gi in range(SGU_GROUPS):
                cols = slice(gi * LANES, (gi + 1) * LANES)
                tot = jnp.sum(dbm_ref[:, cols], axis=1, keepdims=True)
                dbm_ref[:, cols] = jnp.broadcast_to(tot, (CHUNK, LANES))

    const2 = lambda r, c: pl.BlockSpec((r, c), lambda i: (0, 0))
    wspec = pl.BlockSpec((SGU_GROUPS, CHUNK, CHUNK), lambda i: (0, 0, 0))
    return pl.pallas_call(
        body, name=name, grid=(nsteps,),
        in_specs=[pl.BlockSpec((tt, 2 * W), lambda i: (i, 0)), pl.BlockSpec((tt, W), lambda i: (i, 0)),
                  const2(1, W), const2(1, W), wspec, const2(CHUNK, W)],
        out_specs=[pl.BlockSpec((tt, 2 * W), lambda i: (i, 0)), const2(1, W), const2(1, W), wspec, const2(CHUNK, W)],
        out_shape=[jax.ShapeDtypeStruct((T, 2 * W), BF16), jax.ShapeDtypeStruct((1, W), F32),
                   jax.ShapeDtypeStruct((1, W), F32), jax.ShapeDtypeStruct((SGU_GROUPS, CHUNK, CHUNK), F32),
                   jax.ShapeDtypeStruct((CHUNK, W), F32)],
        scratch_shapes=[pltpu.VMEM((tt, W), F32), pltpu.VMEM((tt, W), F32)],
        compiler_params=_cparams(("arbitrary",)),
    )(zc, dp, ln_g, ln_b, wm, bmap)


def _rope_tables(positions):
    half = QK_ROPE // 2
    inv_freq = jnp.exp(-math.log(ROPE_BASE) * jnp.arange(half, dtype=F32) / half)
    ang = positions.reshape(-1).astype(F32)[:, None] * inv_freq
    cos = jnp.cos(ang)
    sin = jnp.sin(ang)
    n = ang.shape[0]
    tail = LANES - QK_NOPE - QK_ROPE
    cos_t = jnp.concatenate([jnp.ones((n, QK_NOPE), F32), cos, cos, jnp.ones((n, tail), F32)], axis=1)
    sin_t = jnp.concatenate([jnp.zeros((n, QK_NOPE), F32), -sin, sin, jnp.zeros((n, tail), F32)], axis=1)
    return cos_t, sin_t


SGU_GROUP_DIM = SGU_WIDTH // SGU_GROUPS
_O1, _O2, _O3, _O4 = Q_LORA, Q_LORA + KV_LORA, Q_LORA + KV_LORA + QK_ROPE, Q_LORA + KV_LORA + QK_ROPE + LRU_WIDTH
_A0, _A1, _A2 = 2 * LRU_WIDTH, 2 * LRU_WIDTH + Q_LORA, 2 * LRU_WIDTH + Q_LORA + KV_LORA
_A3 = _A2 + QK_NOPE
Z_Q_BLOCK, Z_KV_BLOCK, Z_KPE_BLOCK = _A0 // Q_LORA, _A1 // KV_LORA, _A2 // LANES


def _perm_w_in(w_in):
    zeros = lambda n: jnp.zeros((w_in.shape[0], n), w_in.dtype)
    return jnp.concatenate([w_in[:, _O3:_O4], w_in[:, _O4:], w_in[:, :_O1], w_in[:, _O1:_O2], zeros(QK_NOPE),
                            w_in[:, _O2:_O3], zeros(LANES - QK_NOPE - QK_ROPE)], axis=1)


def _unperm_w_in(w):
    return jnp.concatenate([w[:, _A0:_A1], w[:, _A1:_A2], w[:, _A3:_A3 + QK_ROPE], w[:, :LRU_WIDTH],
                            w[:, LRU_WIDTH:_A0]], axis=1)


def _head_blocks(w, d):
    r = w.shape[0]
    return jnp.pad(w.reshape(r, MLA_HEADS, d), ((0, 0), (0, 0), (0, LANES - d))).reshape(r, MLA_HEADS * LANES)


def _from_head_blocks(w, d):
    r = w.shape[0]
    return w.reshape(r, MLA_HEADS, LANES)[:, :, :d].reshape(r, MLA_HEADS * d)


def _split_kv(w_kv):
    r = w_kv.shape[0]
    w3 = w_kv.reshape(r, MLA_HEADS, QK_NOPE + V_HEAD)
    return _head_blocks(w3[:, :, :QK_NOPE].reshape(r, -1), QK_NOPE), w3[:, :, QK_NOPE:].reshape(r, -1)


def _join_kv(w_k, w_v):
    r = w_k.shape[0]
    return jnp.concatenate([_from_head_blocks(w_k, QK_NOPE).reshape(r, MLA_HEADS, QK_NOPE),
                            w_v.reshape(r, MLA_HEADS, V_HEAD)], axis=2).reshape(r, -1)


def _prep_small(w):
    p = {n: w[n] for n in w if n not in BIG}
    eye = jnp.eye(LRU_HEADS, dtype=F32)
    dense = lambda wg: (wg[:, :, None, :] * eye[:, None, :, None]).reshape(LRU_WIDTH, LRU_WIDTH).astype(BF16)
    p["wa_d"] = dense(w["ab_w_rg_a"][0])
    p["wx_d"] = dense(w["ab_w_rg_x"][0])
    causal = jnp.tril(jnp.ones((CHUNK, CHUNK), F32))
    p["wm"] = (w["c_w_s"][0] * causal).astype(BF16)
    p["bmap"] = jnp.repeat(w["c_b_s"][0].T, SGU_GROUP_DIM, axis=1)
    return p


def _prep_big(ab_w_in, ab_w_q_b, ab_w_kv_b):
    return {"w_in_p": _perm_w_in(ab_w_in).astype(BF16),
            "w_q_p": _head_blocks(ab_w_q_b, QK_NOPE + QK_ROPE).astype(BF16),
            "w_kv_p": jnp.concatenate(_split_kv(ab_w_kv_b), axis=1).astype(BF16)}


def _ffn_fwd(h, l, p, S):
    hn = _rms_fwd(h, p["ffn_norm"][l], name=f"ffn{l}_norm")
    g = _mm(hn, p["ffn_gate_t"][l], tb=True, name=f"ffn{l}_gate")
    u = _mm(hn, p["ffn_up_t"][l], tb=True, name=f"ffn{l}_up")
    act = _ffn_act_fwd(g, u, p["ffn_conv_w"][l], p["ffn_conv_b"][l][None], S=S, name=f"ffn{l}_act")
    out = _mm(act, p["ffn_down"][l], res=h, name=f"ffn{l}_down")
    return out, (hn, g, u, act)


def _ffn_bwd(dh, h_in, l, p, saved, S):
    hn, g, u, act = saved
    dact = _mm(dh, p["ffn_down"][l], tb=True, out_dtype=BF16, name=f"ffn{l}_dact")
    dw_down = _mm(act, dh, ta=True, out_dtype=BF16, name=f"ffn{l}_dwdown")
    dg, du, dcw, dcb = _ffn_act_bwd(g, u, dact, p["ffn_conv_w"][l], p["ffn_conv_b"][l][None], S=S, name=f"ffn{l}_dactbwd")
    dhn = _mm(dg, p["ffn_gate_t"][l], name=f"ffn{l}_dhn_g")
    dhn = _mm(du, p["ffn_up_t"][l], res=dhn, name=f"ffn{l}_dhn_u")
    dw_gate_t = _mm(dg, hn, ta=True, out_dtype=BF16, name=f"ffn{l}_dwgate")
    dw_up_t = _mm(du, hn, ta=True, out_dtype=BF16, name=f"ffn{l}_dwup")
    dh_in, dnorm = _rms_bwd(h_in, p["ffn_norm"][l], dhn, res=dh, name=f"ffn{l}_dnorm")
    grads = dict(ffn_norm=dnorm[0], ffn_gate_t=dw_gate_t, ffn_up_t=dw_up_t, ffn_conv_w=dcw[:FFN_CONV],
                 ffn_conv_b=dcb[0], ffn_down=dw_down)
    return dh_in, grads


def _local_step(x, positions, target, p, late_weights=None, early_grads=None):
    B, S, D = x.shape
    T = B * S
    H = MLA_HEADS
    xf = x.reshape(T, D)
    tgt = target.reshape(T, D)
    cos, sin = _rope_tables(positions)

    hn0 = _rms_fwd(xf, p["ab_norm"][0], name="ab_norm")
    z = _mm(hn0, p["w_in_p"], name="ab_in")
    cqn = _rms_fwd(z, p["ab_q_norm"][0], cb=Z_Q_BLOCK, name="q_norm")
    ckvn = _rms_fwd(z, p["ab_kv_norm"][0], cb=Z_KV_BLOCK, name="kv_norm")
    q = _mm(cqn, p["w_q_p"], name="q_up")
    kv = _mm(ckvn, p["w_kv_p"], out_dtype=BF16, name="kv_up")
    qs = _rope_q(q, cos, sin, name="q_rope")
    kk = _key_blocks(kv, z, cos, sin, kpe_block=Z_KPE_BLOCK, name="k_rope")
    att = dict(B=B, S=S, v_block0=H)
    if late_weights is None:
        o, lse = _attn_fwd(qs, kk, kv, name="attn_fwd", **att)
    else:
        o, lse, *arrived = _attn_fwd(qs, kk, kv, name="attn_fwd", exchange=late_weights[0], **att)
        p = {**p, **late_weights[1](arrived)}
    lru_par = (p["ab_conv_w"][0], p["ab_conv_b"], p["wa_d"], p["ab_b_rg_a"], p["wx_d"], p["ab_b_rg_x"], p["ab_lambda"])
    y_lru, hs = _lru_fwd(z, *lru_par, S=S, name="lru_fwd")
    n_att = H * V_HEAD
    w_out_a, w_out_b = p["ab_w_out"][:n_att], p["ab_w_out"][n_att:]
    h1 = _mm(y_lru, w_out_b, res=_mm(o, w_out_a, res=xf, name="ab_out_a"), name="ab_out_b")
    h2, ffn0 = _ffn_fwd(h1, 0, p, S)

    hn2 = _rms_fwd(h2, p["c_norm"][0], name="c_norm")
    zc = _mm(hn2, p["c_w_in_t"], tb=True, name="c_in")
    pg = _sgu_fwd(zc, p["c_ln_g"], p["c_ln_b"], p["wm"], p["bmap"], name="sgu_fwd")
    h3 = _mm(pg, p["c_w_out"], res=h2, name="c_out")
    h4, ffn1 = _ffn_fwd(h3, 1, p, S)

    loss_row, dh4, dfinal = _final_fwd_bwd(h4, p["final_norm"], tgt, name="final")

    dh3, g_ffn1 = _ffn_bwd(dh4, h3, 1, p, ffn1, S)
    dpg = _mm(dh3, p["c_w_out"], tb=True, out_dtype=BF16, name="c_dp")
    dw_c_out = _mm(pg, dh3, ta=True, out_dtype=BF16, name="c_dwout")
    dzc, dlng, dlnb, dwm, dbm = _sgu_bwd(zc, dpg, p["c_ln_g"], p["c_ln_b"], p["wm"], p["bmap"], name="sgu_bwd")
    dhn2 = _mm(dzc, p["c_w_in_t"], name="c_dhn")
    dw_c_in_t = _mm(dzc, hn2, ta=True, out_dtype=BF16, name="c_dwin")
    dh2, dcnorm = _rms_bwd(h2, p["c_norm"][0], dhn2, res=dh3, name="c_dnorm")
    dh1, g_ffn0 = _ffn_bwd(dh2, h1, 0, p, ffn0, S)

    do = _mm(dh1, w_out_a, tb=True, name="ab_do")
    dy_lru = _mm(dh1, w_out_b, tb=True, out_dtype=BF16, name="ab_dylru")
    dw_out = jnp.concatenate([_mm(o, dh1, ta=True, out_dtype=BF16, name="ab_dwout_a"),
                              _mm(y_lru, dh1, ta=True, out_dtype=BF16, name="ab_dwout_b")], axis=0)
    dq, delta = _attn_dq(qs, kk, kv, o, lse, do, name="attn_dq", **att)
    nq = S // min(ATT_BLOCK, S)
    rows = lambda a: a.reshape(B, H, nq, S // nq)
    early = None
    if early_grads is None:
        dk, dv = _attn_dkv(qs, kk, kv, rows(lse), rows(delta), do, name="attn_dkv", **att)
    else:
        ready = {"c_w_in_t": dw_c_in_t, "c_w_out": dw_c_out}
        for name in ("ffn_gate_t", "ffn_up_t", "ffn_down"):
            ready[name] = [g_ffn0[name], g_ffn1[name]]
        exchange, finish = early_grads(ready)
        dk, dv, *arrived = _attn_dkv(qs, kk, kv, rows(lse), rows(delta), do, name="attn_dkv", exchange=exchange, **att)
        early = finish(arrived)
    dq_full = _rope_q_bwd(dq, cos, sin, name="q_rope_bwd")
    dkr = _key_rope_bwd(dk, cos, sin, name="k_rope_bwd")
    n_key = H * LANES
    w_k_p, w_v_p = p["w_kv_p"][:, :n_key], p["w_kv_p"][:, n_key:]
    dcqn = _mm(dq_full, p["w_q_p"], tb=True, name="q_dlat")
    dw_q_p = _mm(cqn, dq_full, ta=True, out_dtype=BF16, name="q_dw")
    dckvn = _mm(dv, w_v_p, tb=True, res=_mm(dk, w_k_p, tb=True, name="k_dlat"), name="v_dlat")
    dw_k_p = _mm(ckvn, dk, ta=True, out_dtype=BF16, name="k_dw")
    dw_v_p = _mm(ckvn, dv, ta=True, out_dtype=BF16, name="v_dw")
    dcq, dqnorm = _rms_bwd(z, p["ab_q_norm"][0], dcqn, cb=Z_Q_BLOCK, out_dtype=BF16, name="q_dnorm")
    dckv, dkvnorm = _rms_bwd(z, p["ab_kv_norm"][0], dckvn, cb=Z_KV_BLOCK, out_dtype=BF16, name="kv_dnorm")
    dxl, dgate, dcw, dcb, dwa, dba, dwx, dbx, dlam = _lru_bwd(z, hs, dy_lru, *lru_par, S=S, name="lru_bwd")
    dz = jnp.concatenate([dxl, dgate, dcq, dckv, dkr], axis=1)
    dhn0 = _mm(dz, p["w_in_p"], tb=True, name="ab_dhn")
    dw_in_p = _mm(hn0, dz, ta=True, out_dtype=BF16, name="ab_dwin")
    dx, dabnorm = _rms_bwd(xf, p["ab_norm"][0], dhn0, res=dh1, name="ab_dnorm")

    blocks = lambda dd: jnp.stack([dd[i * LRU_BLOCK:(i + 1) * LRU_BLOCK, i * LRU_BLOCK:(i + 1) * LRU_BLOCK]
                                   for i in range(LRU_HEADS)])
    causal = jnp.tril(jnp.ones((CHUNK, CHUNK), F32))
    grads = {
        "ab_norm": dabnorm, "w_in_p": dw_in_p, "ab_q_norm": dqnorm, "w_q_p": dw_q_p,
        "ab_kv_norm": dkvnorm, "w_k_p": dw_k_p, "w_v_p": dw_v_p, "ab_conv_w": dcw[:LRU_CONV][None], "ab_conv_b": dcb,
        "ab_w_rg_a": blocks(dwa)[None], "ab_b_rg_a": dba, "ab_w_rg_x": blocks(dwx)[None], "ab_b_rg_x": dbx,
        "ab_lambda": dlam, "ab_w_out": dw_out,
        "c_norm": dcnorm, "c_w_in_t": dw_c_in_t, "c_ln_g": dlng, "c_ln_b": dlnb,
        "c_w_s": (dwm * causal)[None], "c_b_s": dbm[:, ::SGU_GROUP_DIM].T[None], "c_w_out": dw_c_out,
        "final_norm": dfinal[0],
    }
    for name in ("ffn_norm", "ffn_conv_w", "ffn_conv_b"):
        grads[name] = jnp.stack([g_ffn0[name], g_ffn1[name]])
    for name in ("ffn_gate_t", "ffn_up_t", "ffn_down"):
        grads[name] = [g_ffn0[name], g_ffn1[name]]
    return loss_row, dx.reshape(B, S, D), grads, early


ANY = pl.BlockSpec(memory_space=pl.ANY)


def _place():
    x, y, c = lax.axis_index("x"), lax.axis_index("y"), lax.axis_index("c")
    chips = [(1 - x, y), (x, 1 - y), (1 - x, 1 - y)]
    return x, y, c, 2 * x + y, (x, y, 1 - c), chips


def _remote(src, dst, send_sems, recv_sems, k, to):
    return pltpu.make_async_remote_copy(src_ref=src, dst_ref=dst, send_sem=send_sems.at[k], recv_sem=recv_sems.at[k],
                                        device_id=to, device_id_type=MESH)


class _Exchange:
    def __init__(self, arrs, out_shapes, n_sems, start, finish):
        self.arrs, self.out_shapes, self.n_sems, self.start, self.finish = list(arrs), out_shapes, n_sems, start, finish

    @property
    def in_specs(self):
        return [ANY] * len(self.arrs)

    @property
    def out_specs(self):
        return [ANY] * len(self.out_shapes)

    @property
    def scratch(self):
        return [pltpu.SemaphoreType.DMA((self.n_sems,)), pltpu.SemaphoreType.DMA((self.n_sems,))]

    def split(self, refs):
        n = len(self.arrs)
        return refs[:n], refs[n:n + len(self.out_shapes)], refs[-2], refs[-1]

    def run(self, name):
        def body(*refs):
            parts = self.split(refs)
            self.start(*parts)
            self.finish(*parts)

        return pl.pallas_call(body, name=name, in_specs=self.in_specs, out_specs=self.out_specs,
                              out_shape=self.out_shapes, scratch_shapes=self.scratch)(*self.arrs)


def _put(buf, piece, idx, axis):
    return lax.dynamic_update_slice_in_dim(buf, jnp.expand_dims(piece, axis).astype(buf.dtype), idx, axis)


def _all_gather(arrs):
    n = len(arrs)

    def start(ins, outs, send_sems, recv_sems):
        x, y, c, j, sib, chips = _place()
        for i in range(n):
            for k, (cx, cy) in enumerate(chips):
                _remote(ins[i].at[:, c], outs[i].at[:, j, c], send_sems, recv_sems, 6 * i + k, (cx, cy, c)).start()

    def finish(ins, outs, send_sems, recv_sems):
        x, y, c, j, sib, chips = _place()
        passed = []
        for i in range(n):
            for k, (cx, cy) in enumerate(chips):
                got = outs[i].at[:, 2 * cx + cy, c]
                _remote(got, got, send_sems, recv_sems, 6 * i + k, (cx, cy, c)).wait_recv()
                cp = _remote(got, got, send_sems, recv_sems, 6 * i + 3 + k, sib)
                cp.start()
                passed.append(cp)
        for i in range(n):
            for k, (cx, cy) in enumerate(chips):
                got = outs[i].at[:, 2 * cx + cy, 1 - c]
                _remote(got, got, send_sems, recv_sems, 6 * i + 3 + k, sib).wait_recv()
                _remote(ins[i].at[:, c], ins[i].at[:, c], send_sems, recv_sems, 6 * i + k, sib).wait_send()
        for cp in passed:
            cp.wait_send()

    shapes = [jax.ShapeDtypeStruct((a.shape[0], N_CHIPS) + a.shape[1:], a.dtype) for a in arrs]
    return _Exchange(arrs, shapes, 6 * n, start, finish)


def _pair_swap(arrs):
    n = len(arrs)

    def start(ins, outs, send_sems, recv_sems):
        x, y, c, j, sib, chips = _place()
        for i in range(n):
            _remote(ins[i].at[:, 1 - c], outs[i], send_sems, recv_sems, i, sib).start()

    def finish(ins, outs, send_sems, recv_sems):
        x, y, c, j, sib, chips = _place()
        for i in range(n):
            _remote(ins[i].at[:, 1 - c], outs[i], send_sems, recv_sems, i, sib).wait()

    shapes = [jax.ShapeDtypeStruct((a.shape[0],) + a.shape[2:], a.dtype) for a in arrs]
    return _Exchange(arrs, shapes, n, start, finish)


def _pair_send(arrs):
    n = len(arrs)

    def start(ins, outs, send_sems, recv_sems):
        x, y, c, j, sib, chips = _place()
        for i in range(n):
            _remote(ins[i], outs[i], send_sems, recv_sems, i, sib).start()

    def finish(ins, outs, send_sems, recv_sems):
        x, y, c, j, sib, chips = _place()
        for i in range(n):
            _remote(ins[i], outs[i], send_sems, recv_sems, i, sib).wait()

    shapes = [jax.ShapeDtypeStruct(a.shape, a.dtype) for a in arrs]
    return _Exchange(arrs, shapes, n, start, finish)


def _chip_exchange(arrs, *, scatter):
    n = len(arrs)

    def copies(ins, outs, send_sems, recv_sems):
        x, y, c, j, sib, chips = _place()
        return [(_remote(ins[i].at[2 * cx + cy] if scatter else ins[i], outs[i].at[j], send_sems, recv_sems,
                         3 * i + k, (cx, cy, c)),
                 _remote(outs[i].at[2 * cx + cy], outs[i].at[2 * cx + cy], send_sems, recv_sems, 3 * i + k, (cx, cy, c)))
                for i in range(n) for k, (cx, cy) in enumerate(chips)]

    def start(*refs):
        for out, _ in copies(*refs):
            out.start()

    def finish(*refs):
        for out, back in copies(*refs):
            back.wait_recv()
            out.wait_send()

    shapes = [jax.ShapeDtypeStruct((N_CHIPS,) + a.shape[-2:], a.dtype) for a in arrs]
    return _Exchange(arrs, shapes, 3 * n, start, finish)


FLAT_ROWS = 512


def _add2(a, b, *, out_dtype, name):
    n, R, L = a.shape
    tr = _tile(R, FLAT_ROWS, 16)

    def body(a_ref, b_ref, o_ref):
        o_ref[...] = (a_ref[...].astype(F32) + b_ref[...].astype(F32)).astype(out_dtype)

    spec = pl.BlockSpec((n, tr, L), lambda i: (0, i, 0))
    return pl.pallas_call(
        body, name=name, grid=(R // tr,), in_specs=[spec, spec], out_specs=spec,
        out_shape=jax.ShapeDtypeStruct(a.shape, out_dtype), compiler_params=_cparams(("parallel",)),
    )(a, b)


def _sum_slots(buf, *, name):
    n, R, L = buf.shape
    tr = _tile(R, FLAT_ROWS, 16)

    def body(b_ref, o_ref):
        acc = b_ref[0].astype(F32)
        for k in range(1, n):
            acc = acc + b_ref[k].astype(F32)
        o_ref[...] = acc

    return pl.pallas_call(
        body, name=name, grid=(R // tr,), in_specs=[pl.BlockSpec((n, tr, L), lambda i: (0, i, 0))],
        out_specs=pl.BlockSpec((tr, L), lambda i: (i, 0)),
        out_shape=jax.ShapeDtypeStruct((R, L), F32), compiler_params=_cparams(("parallel",)),
    )(buf)


def _adamw(w, g, m, v, *, name):
    R, L = w.shape
    tr = _tile(R, FLAT_ROWS, 16)
    c1 = 1.0 - ADAM_B1 ** ADAM_STEP
    c2 = 1.0 - ADAM_B2 ** ADAM_STEP

    def body(w_ref, g_ref, m_ref, v_ref, d_ref, nm_ref, nv_ref):
        gg = g_ref[...]
        mm = ADAM_B1 * m_ref[...] + (1.0 - ADAM_B1) * gg
        vv = ADAM_B2 * v_ref[...] + (1.0 - ADAM_B2) * (gg * gg)
        nm_ref[...] = mm
        nv_ref[...] = vv
        d_ref[...] = -ADAM_LR * ((mm / c1) / (jnp.sqrt(vv / c2) + ADAM_EPS) + ADAM_WD * w_ref[...])

    spec = pl.BlockSpec((tr, L), lambda i: (i, 0))
    sh = jax.ShapeDtypeStruct((R, L), F32)
    return pl.pallas_call(
        body, name=name, grid=(R // tr,), in_specs=[spec] * 4, out_specs=[spec] * 3, out_shape=[sh] * 3,
        compiler_params=_cparams(("parallel",)),
    )(w, g, m, v)


WEIGHT_NAMES = ["ab_norm", "ab_w_in", "ab_q_norm", "ab_w_q_b", "ab_kv_norm", "ab_w_kv_b", "ab_conv_w", "ab_conv_b",
                "ab_w_rg_a", "ab_b_rg_a", "ab_w_rg_x", "ab_b_rg_x", "ab_lambda", "ab_w_out", "c_norm", "c_w_in",
                "c_ln_g", "c_ln_b", "c_w_s", "c_b_s", "c_w_out", "ffn_norm", "ffn_w_gate", "ffn_w_up", "ffn_conv_w",
                "ffn_conv_b", "ffn_w_down", "final_norm"]
BIG = {"ab_w_in": 2, "ab_w_q_b": 2, "ab_w_kv_b": 2, "ab_w_out": 1, "c_w_in": 2, "c_w_out": 1,
       "ffn_w_gate": 2, "ffn_w_up": 2, "ffn_w_down": 1}
SMALL_SHARDED = {"ab_conv_w": 2, "c_norm": 1, "c_ln_g": 1, "c_ln_b": 1, "ffn_conv_w": 2}
SMALL_REPLICATED = [n for n in WEIGHT_NAMES if n not in BIG and n not in SMALL_SHARDED]


def _rows(n_elems, mult):
    r = -(-n_elems // LANES)
    return -(-r // mult) * mult


def _flat(parts, rows):
    flat = jnp.concatenate([a.reshape(-1) for a in parts])
    return jnp.pad(flat, (0, rows * LANES - flat.shape[0])).reshape(rows, LANES)


def _unflat(flat, shapes):
    flat = flat.reshape(-1)
    out, off = [], 0
    for s in shapes:
        n = math.prod(s)
        out.append(flat[off:off + n].reshape(s))
        off += n
    return out


def _join_shards(a, axis):
    a = jnp.moveaxis(a, 0, axis)
    return a.reshape(a.shape[:axis] + (a.shape[axis] * a.shape[axis + 1],) + a.shape[axis + 2:])


def kernel(x, positions, ab_norm, ab_w_in, ab_q_norm, ab_w_q_b, ab_kv_norm, ab_w_kv_b, ab_conv_w, ab_conv_b, ab_w_rg_a, ab_b_rg_a, ab_w_rg_x, ab_b_rg_x, ab_lambda, ab_w_out, c_norm, c_w_in, c_ln_g, c_ln_b, c_w_s, c_b_s, c_w_out, ffn_norm, ffn_w_gate, ffn_w_up, ffn_conv_w, ffn_conv_b, ffn_w_down, final_norm, loss_target, m_ab_norm, m_ab_w_in, m_ab_q_norm, m_ab_w_q_b, m_ab_kv_norm, m_ab_w_kv_b, m_ab_conv_w, m_ab_conv_b, m_ab_w_rg_a, m_ab_b_rg_a, m_ab_w_rg_x, m_ab_b_rg_x, m_ab_lambda, m_ab_w_out, m_c_norm, m_c_w_in, m_c_ln_g, m_c_ln_b, m_c_w_s, m_c_b_s, m_c_w_out, m_ffn_norm, m_ffn_w_gate, m_ffn_w_up, m_ffn_conv_w, m_ffn_conv_b, m_ffn_w_down, m_final_norm, v_ab_norm, v_ab_w_in, v_ab_q_norm, v_ab_w_q_b, v_ab_kv_norm, v_ab_w_kv_b, v_ab_conv_w, v_ab_conv_b, v_ab_w_rg_a, v_ab_b_rg_a, v_ab_w_rg_x, v_ab_b_rg_x, v_ab_lambda, v_ab_w_out, v_c_norm, v_c_w_in, v_c_ln_g, v_c_ln_b, v_c_w_s, v_c_b_s, v_c_w_out, v_ffn_norm, v_ffn_w_gate, v_ffn_w_up, v_ffn_conv_w, v_ffn_conv_b, v_ffn_w_down, v_final_norm):
    given = dict(locals())
    w = {n: given[n] for n in WEIGHT_NAMES}
    m = {n: given["m_" + n] for n in WEIGHT_NAMES}
    v = {n: given["v_" + n] for n in WEIGHT_NAMES}
    c = lax.axis_index("c")
    chip = 2 * lax.axis_index("x") + lax.axis_index("y")

    halves = lambda a: a.reshape(a.shape[0], 2, a.shape[1] // 2, a.shape[2])
    tr = lambda a: jnp.swapaxes(a, 1, 2)
    send = {"ab_w_in": w["ab_w_in"], "ab_w_q_b": w["ab_w_q_b"], "ab_w_kv_b": w["ab_w_kv_b"], "ab_w_out": w["ab_w_out"],
            "c_w_in": tr(w["c_w_in"]), "c_w_out": w["c_w_out"], "ffn_w_gate": tr(w["ffn_w_gate"]),
            "ffn_w_up": tr(w["ffn_w_up"]), "ffn_w_down": w["ffn_w_down"]}
    small_rows = _rows(sum(w[n].size for n in SMALL_SHARDED), 16)
    small_sh = _flat([w[n] for n in SMALL_SHARDED], small_rows).reshape(1, 2, small_rows // 2, LANES)
    first_names = ["ab_w_in", "ab_w_q_b", "ab_w_kv_b", "ab_w_out"]
    late_names = [n for n in BIG if n not in first_names]
    mine = {n: halves(send[n].astype(BF16)) for n in BIG}

    def gathered(names, arrived):
        full = {}
        for n, a in zip(names, arrived):
            a = _put(a, mine[n] if n in mine else small_sh, chip, 1)
            full[n] = a.reshape(a.shape[0], -1, a.shape[-1])
        return full

    got = _all_gather([mine[n] for n in first_names] + [small_sh]).run("gather_first")
    full = gathered(first_names + ["small"], got)
    unshard = lambda a: jnp.swapaxes(a.reshape(N_CHIPS, -1, a.shape[-1]), 0, 1).reshape(-1, N_CHIPS * a.shape[-1])
    p = _prep_big(unshard(full["ab_w_in"][0]), unshard(full["ab_w_q_b"][0]), unshard(full["ab_w_kv_b"][0]))
    p["ab_w_out"] = full["ab_w_out"][0]
    small_full = dict(w)
    off = 0
    small_got = full["small"].reshape(N_CHIPS, -1)
    for n, ax in SMALL_SHARDED.items():
        seg = small_got[:, off:off + w[n].size].reshape((N_CHIPS,) + w[n].shape)
        small_full[n] = _join_shards(seg, ax)
        off += w[n].size
    p.update(_prep_small(small_full))

    def late_weights(arrived):
        full = gathered(late_names, arrived)
        return dict(c_w_in_t=full["c_w_in"][0], c_w_out=full["c_w_out"][0], ffn_gate_t=full["ffn_w_gate"],
                    ffn_up_t=full["ffn_w_up"], ffn_down=full["ffn_w_down"])

    def pair_sums(sharded, tag):
        sharded = [a.reshape(N_CHIPS, 2, -1, a.shape[-1]) for a in sharded]
        from_sib = _pair_swap(sharded).run(f"grad_pair_swap_{tag}")
        own = [lax.dynamic_index_in_dim(a, c, axis=1, keepdims=False) for a in sharded]
        return [_add2(a, b, out_dtype=BF16, name=f"grad_pair_add_{tag}{i}") for i, (a, b) in enumerate(zip(own, from_sib))]

    def chip_sums(pair, arrived, tag):
        own = [lax.dynamic_index_in_dim(a, chip, axis=0, keepdims=False) for a in pair]
        return [_sum_slots(_put(a, o, chip, 0), name=f"grad_chip_sum_{tag}{i}") for i, (a, o) in enumerate(zip(arrived, own))]

    def early_grads(ready):
        pair = pair_sums([ready["c_w_in_t"], ready["c_w_out"], *ready["ffn_gate_t"], *ready["ffn_up_t"],
                          *ready["ffn_down"]], "a")
        return _chip_exchange(pair, scatter=True), lambda arrived: chip_sums(pair, arrived, "a")

    loss_row, grad_x, g, half_late = _local_step(x, positions, loss_target, p, (_all_gather([mine[n] for n in late_names]),
                                                                                 late_weights), early_grads)

    cols = lambda a, n: jnp.swapaxes(a.reshape(a.shape[0], N_CHIPS, n), 0, 1)
    n_in, n_q, n_kv = w["ab_w_in"].shape[2], w["ab_w_q_b"].shape[2], w["ab_w_kv_b"].shape[2]
    pair = pair_sums([cols(_unperm_w_in(g["w_in_p"]), n_in), cols(_from_head_blocks(g["w_q_p"], QK_NOPE + QK_ROPE), n_q),
                      cols(_join_kv(g["w_k_p"], g["w_v_p"]), n_kv), g["ab_w_out"]], "b")
    half_first = chip_sums(pair, _chip_exchange(pair, scatter=True).run("grad_chip_exchange_b"), "b")
    half = half_first + half_late
    slot = (jnp.arange(2) == c)[:, None, None]
    summed = [jnp.where(slot, a[None], b[None]).reshape(-1, a.shape[-1])
              for a, b in zip(half, _pair_send(half).run("grad_pair_share"))]
    s_in, s_q, s_kv, s_out, s_cin, s_cout, g0, g1, u0, u1, d0, d1 = summed
    grads = {"ab_w_in": s_in[None], "ab_w_q_b": s_q[None], "ab_w_kv_b": s_kv[None], "ab_w_out": s_out[None],
             "c_w_in": s_cin.T[None], "c_w_out": s_cout[None], "ffn_w_gate": jnp.stack([g0.T, g1.T]),
             "ffn_w_up": jnp.stack([u0.T, u1.T]), "ffn_w_down": jnp.stack([d0, d1])}

    small_names = SMALL_REPLICATED + list(SMALL_SHARDED)
    rs = _rows(sum(g[n].size for n in small_names) + LANES, FLAT_ROWS)
    small = _flat([loss_row] + [g[n] for n in small_names], rs)
    from_sib, = _pair_send([small]).run("small_pair_share")
    pair_small = _sum_slots(jnp.where(slot, small[None], from_sib[None]), name="small_pair_sum")
    all_small, = _chip_exchange([pair_small], scatter=False).run("small_chip_exchange")
    small_sum = _sum_slots(_put(all_small, pair_small, chip, 0), name="small_chip_sum")
    small_parts = _unflat(small_sum, [(1, LANES)] + [g[n].shape for n in small_names])
    loss = small_parts[0][0, 0]
    for n, a in zip(small_names, small_parts[1:]):
        if n in SMALL_SHARDED:
            ax = SMALL_SHARDED[n]
            a = lax.dynamic_slice_in_dim(a, chip * w[n].shape[ax], w[n].shape[ax], axis=ax)
        grads[n] = a.reshape(w[n].shape)

    delta, new_m, new_v = {}, {}, {}
    two_d = lambda a: a.reshape(-1, a.shape[-1])
    for n in BIG:
        out = _adamw(two_d(w[n]), two_d(grads[n]), two_d(m[n]), two_d(v[n]), name=f"adamw_{n}")
        delta[n], new_m[n], new_v[n] = (a.reshape(w[n].shape) for a in out)
    small_all = [n for n in WEIGHT_NAMES if n not in BIG]
    ra = _rows(sum(w[n].size for n in small_all), FLAT_ROWS)
    pack = lambda d: _flat([d[n] for n in small_all], ra)
    out = _adamw(pack(w), pack(grads), pack(m), pack(v), name="adamw_small")
    shapes = [w[n].shape for n in small_all]
    for d, flat in zip((delta, new_m, new_v), out):
        d.update(zip(small_all, _unflat(flat, shapes)))
    return (loss, grad_x, *[grads[n] for n in WEIGHT_NAMES], *[delta[n] for n in WEIGHT_NAMES],
            *[new_m[n] for n in WEIGHT_NAMES], *[new_v[n] for n in WEIGHT_NAMES])
```

```python
import functools
import math

import jax
import jax.numpy as jnp
from jax import lax
from jax.experimental import pallas as pl
from jax.experimental.pallas import tpu as pltpu

F32 = jnp.float32
BF16 = jnp.bfloat16
MESH = pl.DeviceIdType.MESH

D_MODEL = 1024
MLA_HEADS = 8
Q_LORA = 256
KV_LORA = 128
QK_NOPE = 64
QK_ROPE = 32
V_HEAD = 64
LRU_WIDTH = 512
LRU_HEADS = 8
LRU_BLOCK = 64
LRU_CONV = 4
LRU_C = 8.0
CHUNK = 128
SGU_GROUPS = 8
SGU_WIDTH = 1024
D_FF = 2816
FFN_CONV = 3
NORM_EPS = 1e-6
ROPE_BASE = 10000.0
AB_IN_PAD = 1536
ADAM_LR = 0.001
ADAM_B1 = 0.9
ADAM_B2 = 0.999
ADAM_EPS = 1e-08
ADAM_WD = 0.01
ADAM_STEP = 10

N_CHIPS = 4
LANES = 128
VMEM_LIMIT = 56 * 1024 * 1024
ROW_TILE = 256
MM_TM, MM_TN, MM_TK = 512, 1536, 2816
MM_TM_T, MM_TK_T = 1408, 1024
GELU_C = math.sqrt(2.0 / math.pi)


def _cparams(sem):
    return pltpu.CompilerParams(dimension_semantics=sem, vmem_limit_bytes=VMEM_LIMIT)


def _tile(n, target, mult=LANES):
    t = (min(n, target) // mult) * mult
    while t >= mult:
        if n % t == 0:
            return t
        t -= mult
    return n


def _gelu(x):
    t = jnp.tanh(GELU_C * (x + 0.044715 * x * x * x))
    return 0.5 * x * (1.0 + t)


def _gelu_and_grad(x):
    x2 = x * x
    t = jnp.tanh(GELU_C * (x + 0.044715 * x * x2))
    g = 0.5 * x * (1.0 + t)
    dg = 0.5 * (1.0 + t) + 0.5 * x * (1.0 - t * t) * GELU_C * (1.0 + 3.0 * 0.044715 * x2)
    return g, dg


def _sigmoid(x):
    return 1.0 / (1.0 + jnp.exp(-x))


def _shift_rows(x, d, fill_rows):
    ext = jnp.concatenate([fill_rows, x], axis=0)
    return pltpu.roll(ext, d, 0)[8:]


def _shift_rows_up(x, d, fill_rows):
    n = x.shape[0]
    ext = jnp.concatenate([x, fill_rows], axis=0)
    return pltpu.roll(ext, n + 8 - d, 0)[:n]


def _dot(a, b, dims):
    return lax.dot_general(a.astype(BF16), b.astype(BF16), (dims, ((), ())), preferred_element_type=F32)


def _dot_nn(a, b):
    return _dot(a, b, ((1,), (0,)))


def _dot_nt(a, b):
    return _dot(a, b, ((1,), (1,)))


def _dot_tn(a, b):
    return _dot(a, b, ((0,), (0,)))


def _mm(a, b, *, name, ta=False, tb=False, res=None, out_dtype=F32, ride=None):
    if ta:
        K, M = a.shape
    else:
        M, K = a.shape
    N = b.shape[0] if tb else b.shape[1]
    tm = _tile(M, MM_TM_T if ta else MM_TM, LANES if ta else 8)
    tn = _tile(N, MM_TN, LANES)
    tk = _tile(K, MM_TK_T if ta else MM_TK, LANES)
    nk = K // tk
    a_spec = pl.BlockSpec((tk, tm), lambda i, j, k: (k, i)) if ta else pl.BlockSpec((tm, tk), lambda i, j, k: (i, k))
    b_spec = pl.BlockSpec((tn, tk), lambda i, j, k: (j, k)) if tb else pl.BlockSpec((tk, tn), lambda i, j, k: (k, j))
    o_spec = pl.BlockSpec((tm, tn), lambda i, j, k: (i, j))
    dims = ((0,) if ta else (1,), (1,) if tb else (0,))
    has_res = res is not None

    def body(*refs):
        a_ref, b_ref = refs[:2]
        r_ref = refs[2] if has_res else None
        o_ref = refs[3] if has_res else refs[2]
        p = _dot(a_ref[...], b_ref[...], dims)

        def finish(r):
            if has_res:
                r = r + r_ref[...].astype(F32)
            o_ref[...] = r.astype(out_dtype)

        if nk == 1:
            finish(p)
            return
        acc_ref = refs[-1]
        k = pl.program_id(2)

        @pl.when(k == 0)
        def _():
            acc_ref[...] = p

        @pl.when(jnp.logical_and(k > 0, k < nk - 1))
        def _():
            acc_ref[...] += p

        @pl.when(k == nk - 1)
        def _():
            finish(acc_ref[...] + p)

    in_specs = [a_spec, b_spec] + ([o_spec] if has_res else [])
    args = (a, b) + ((res,) if has_res else ())
    return _pcall(
        body, name=name, grid=(M // tm, N // tn, nk), in_specs=in_specs, out_specs=[o_spec],
        out_shape=[jax.ShapeDtypeStruct((M, N), out_dtype)], args=args,
        scratch=[pltpu.VMEM((tm, tn), F32)] if nk > 1 else [], sem=("parallel", "parallel", "arbitrary"), ride=ride)[0]


def _rms_fwd(x, g, *, name, cb=0, out_dtype=BF16):
    T = x.shape[0]
    W = g.shape[-1]
    g = g.reshape(1, W)
    tt = ROW_TILE

    def body(x_ref, g_ref, o_ref):
        xf = x_ref[...].astype(F32)
        rstd = lax.rsqrt(jnp.mean(xf * xf, axis=-1, keepdims=True) + NORM_EPS)
        o_ref[...] = (xf * rstd * g_ref[...]).astype(out_dtype)

    return pl.pallas_call(
        body, name=name, grid=(T // tt,),
        in_specs=[pl.BlockSpec((tt, W), lambda i: (i, cb)), pl.BlockSpec((1, W), lambda i: (0, 0))],
        out_specs=pl.BlockSpec((tt, W), lambda i: (i, 0)),
        out_shape=jax.ShapeDtypeStruct((T, W), out_dtype),
        compiler_params=_cparams(("parallel",)),
    )(x, g)


def _rms_bwd(x, g, dy, *, name, cb=0, res=None, out_dtype=F32):
    T = x.shape[0]
    W = g.shape[-1]
    g = g.reshape(1, W)
    tt = ROW_TILE
    has_res = res is not None

    def body(*refs):
        if has_res:
            x_ref, g_ref, dy_ref, r_ref, dx_ref, dg_ref = refs
        else:
            x_ref, g_ref, dy_ref, dx_ref, dg_ref = refs
        xf = x_ref[...].astype(F32)
        dyf = dy_ref[...].astype(F32)
        rstd = lax.rsqrt(jnp.mean(xf * xf, axis=-1, keepdims=True) + NORM_EPS)
        xhat = xf * rstd
        dxhat = dyf * g_ref[...]
        dx = rstd * (dxhat - xhat * jnp.mean(dxhat * xhat, axis=-1, keepdims=True))
        if has_res:
            dx = dx + r_ref[...].astype(F32)
        dx_ref[...] = dx.astype(out_dtype)
        part = jnp.sum(dyf * xhat, axis=0, keepdims=True)

        @pl.when(pl.program_id(0) == 0)
        def _():
            dg_ref[...] = part

        @pl.when(pl.program_id(0) > 0)
        def _():
            dg_ref[...] += part

    row = pl.BlockSpec((tt, W), lambda i: (i, 0))
    in_specs = [pl.BlockSpec((tt, W), lambda i: (i, cb)), pl.BlockSpec((1, W), lambda i: (0, 0)), row]
    args = (x, g, dy)
    if has_res:
        in_specs.append(row)
        args = args + (res,)
    return pl.pallas_call(
        body, name=name, grid=(T // tt,), in_specs=in_specs,
        out_specs=[row, pl.BlockSpec((1, W), lambda i: (0, 0))],
        out_shape=[jax.ShapeDtypeStruct((T, W), out_dtype), jax.ShapeDtypeStruct((1, W), F32)],
        compiler_params=_cparams(("arbitrary",)),
    )(*args)


def _final_fwd_bwd(h, g, target, *, name):
    T, W = h.shape
    g = g.reshape(1, W)
    tt = ROW_TILE

    def body(x_ref, g_ref, t_ref, loss_ref, dx_ref, dg_ref):
        xf = x_ref[...]
        rstd = lax.rsqrt(jnp.mean(xf * xf, axis=-1, keepdims=True) + NORM_EPS)
        xhat = xf * rstd
        err = xhat * g_ref[...] - t_ref[...]
        lpart = jnp.zeros((1, LANES), F32) + (0.5 / W) * jnp.sum(err * err)
        dyf = err * (1.0 / W)
        dxhat = dyf * g_ref[...]
        dx_ref[...] = rstd * (dxhat - xhat * jnp.mean(dxhat * xhat, axis=-1, keepdims=True))
        part = jnp.sum(dyf * xhat, axis=0, keepdims=True)

        @pl.when(pl.program_id(0) == 0)
        def _():
            dg_ref[...] = part
            loss_ref[...] = lpart

        @pl.when(pl.program_id(0) > 0)
        def _():
            dg_ref[...] += part
            loss_ref[...] += lpart

    row = pl.BlockSpec((tt, W), lambda i: (i, 0))
    return pl.pallas_call(
        body, name=name, grid=(T // tt,),
        in_specs=[row, pl.BlockSpec((1, W), lambda i: (0, 0)), row],
        out_specs=[pl.BlockSpec((1, LANES), lambda i: (0, 0)), row, pl.BlockSpec((1, W), lambda i: (0, 0))],
        out_shape=[jax.ShapeDtypeStruct((1, LANES), F32), jax.ShapeDtypeStruct((T, W), F32),
                   jax.ShapeDtypeStruct((1, W), F32)],
        compiler_params=_cparams(("arbitrary",)),
    )(h, g, target)


def _swap16(x):
    lane = lax.broadcasted_iota(jnp.int32, x.shape, 1)
    return jnp.where((lane % 32) < 16, pltpu.roll(x, LANES - 16, 1), pltpu.roll(x, 16, 1))


def _rope(x, c, s):
    return x * c + _swap16(x) * s


def _rope_t(d, c, s):
    return d * c + _swap16(d * s)


def _head_block_map(fn, x, cos, sin, *, name):
    T, W = x.shape
    tt = ROW_TILE

    def body(x_ref, c_ref, s_ref, o_ref):
        c, s = c_ref[...], s_ref[...]
        for h in range(W // LANES):
            lanes = slice(h * LANES, (h + 1) * LANES)
            o_ref[:, lanes] = fn(x_ref[:, lanes], c, s).astype(BF16)

    tab = pl.BlockSpec((tt, LANES), lambda i: (i, 0))
    blk = pl.BlockSpec((tt, W), lambda i: (i, 0))
    return pl.pallas_call(
        body, name=name, grid=(T // tt,), in_specs=[blk, tab, tab], out_specs=blk,
        out_shape=jax.ShapeDtypeStruct((T, W), BF16), compiler_params=_cparams(("parallel",)),
    )(x, cos, sin)


def _rope_q(q, cos, sin, *, name):
    scale = _attn_scale()
    return _head_block_map(lambda x, c, s: _rope(x, c, s) * scale, q, cos, sin, name=name)


def _rope_q_bwd(dq, cos, sin, *, name):
    return _head_block_map(_rope_t, dq, cos, sin, name=name)


def _key_blocks(kv, z, cos, sin, *, kpe_block, name):
    T = kv.shape[0]
    tt = ROW_TILE
    W = MLA_HEADS * LANES

    def body(kv_ref, z_ref, c_ref, s_ref, o_ref):
        kr = _rope(z_ref[...], c_ref[...], s_ref[...])
        for h in range(MLA_HEADS):
            lanes = slice(h * LANES, (h + 1) * LANES)
            o_ref[:, lanes] = (kv_ref[:, lanes].astype(F32) + kr).astype(BF16)

    tab = pl.BlockSpec((tt, LANES), lambda i: (i, 0))
    blk = pl.BlockSpec((tt, W), lambda i: (i, 0))
    return pl.pallas_call(
        body, name=name, grid=(T // tt,),
        in_specs=[blk, pl.BlockSpec((tt, LANES), lambda i: (i, kpe_block)), tab, tab], out_specs=blk,
        out_shape=jax.ShapeDtypeStruct((T, W), BF16), compiler_params=_cparams(("parallel",)),
    )(kv, z, cos, sin)


def _key_rope_bwd(dk, cos, sin, *, name):
    T = dk.shape[0]
    tt = ROW_TILE

    def body(d_ref, c_ref, s_ref, o_ref):
        d = d_ref[:, :LANES]
        for h in range(1, MLA_HEADS):
            d = d + d_ref[:, h * LANES:(h + 1) * LANES]
        lane = lax.broadcasted_iota(jnp.int32, d.shape, 1)
        d = jnp.where(jnp.logical_and(lane >= QK_NOPE, lane < QK_NOPE + QK_ROPE), d, 0.0)
        o_ref[...] = _rope_t(d, c_ref[...], s_ref[...]).astype(BF16)

    tab = pl.BlockSpec((tt, LANES), lambda i: (i, 0))
    return pl.pallas_call(
        body, name=name, grid=(T // tt,),
        in_specs=[pl.BlockSpec((tt, MLA_HEADS * LANES), lambda i: (i, 0)), tab, tab], out_specs=tab,
        out_shape=jax.ShapeDtypeStruct((T, LANES), BF16), compiler_params=_cparams(("parallel",)),
    )(dk, cos, sin)


ATT_BLOCK = 512


def _attn_scale():
    return float((QK_NOPE + QK_ROPE) ** -0.5)


def _causal_mask(qi, kj, tq, tk):
    row = qi * tq + lax.broadcasted_iota(jnp.int32, (tq, tk), 0)
    col = kj * tk + lax.broadcasted_iota(jnp.int32, (tq, tk), 1)
    return col <= row


def _pcall(body, *, name, grid, in_specs, out_specs, out_shape, args, scratch=(), sem=None, ride=None):
    n_in, n_out, n_scr = len(args), len(out_shape), len(scratch)
    if ride is None:
        return pl.pallas_call(
            body, name=name, grid=grid, in_specs=list(in_specs), out_specs=list(out_specs), out_shape=list(out_shape),
            scratch_shapes=list(scratch), compiler_params=_cparams(sem or ("arbitrary",) * len(grid)))(*args)
    ex, sink = ride
    o0 = n_in + len(ex.arrs)
    s0 = o0 + n_out + len(ex.out_shapes)

    def hosted(*refs):
        parts = (refs[n_in:o0], refs[o0 + n_out:s0], refs[-2], refs[-1])
        ids = [pl.program_id(i) for i in range(len(grid))]
        pl.when(functools.reduce(jnp.logical_and, [i == 0 for i in ids]))(lambda: ex.start(*parts))
        body(*refs[:n_in], *refs[o0:o0 + n_out], *refs[s0:s0 + n_scr])
        pl.when(functools.reduce(jnp.logical_and, [i == n - 1 for i, n in zip(ids, grid)]))(lambda: ex.finish(*parts))

    outs = pl.pallas_call(
        hosted, name=name, grid=grid, in_specs=list(in_specs) + ex.in_specs, out_specs=list(out_specs) + ex.out_specs,
        out_shape=list(out_shape) + ex.out_shapes, scratch_shapes=list(scratch) + ex.scratch,
        compiler_params=_cparams(("arbitrary",) * len(grid)))(*args, *ex.arrs)
    sink(outs[n_out:])
    return outs[:n_out]


PAIRS = MLA_HEADS // 2


def _own_lanes(x, first):
    lane = lax.broadcasted_iota(jnp.int32, x.shape, 1)
    return jnp.where((lane < V_HEAD) if first else (lane >= V_HEAD), x, 0.0)


def _attn_fwd(q, k, kv, *, B, S, v_block0, name, ride=None):
    tq = tk = min(ATT_BLOCK, S)
    nq = S // tq
    T = B * S

    def body(q_ref, k_ref, v_ref, o_ref, lse_ref):
        qi = pl.program_id(2)
        qs = (q_ref[:, :LANES], q_ref[:, LANES:])

        def step(masked):
            def f(j, carry):
                rows = pl.ds(pl.multiple_of(j * tk, tk), tk)
                vb = v_ref[rows, :]
                out = []
                for h in range(2):
                    m, l, acc = carry[h]
                    s = _dot_nt(qs[h], k_ref[rows, h * LANES:(h + 1) * LANES])
                    if masked:
                        s = jnp.where(_causal_mask(qi, j, tq, tk), s, -jnp.inf)
                    m_new = jnp.maximum(m, jnp.max(s, axis=-1, keepdims=True))
                    alpha = jnp.exp(m - m_new)
                    p = jnp.exp(s - m_new)
                    out.append((m_new, alpha * l + jnp.sum(p, axis=-1, keepdims=True), alpha * acc + _dot_nn(p, vb)))
                return tuple(out)
            return f

        one = (jnp.full((tq, 1), -1e30, F32), jnp.zeros((tq, 1), F32), jnp.zeros((tq, LANES), F32))
        (ma, la, acca), (mb, lb, accb) = step(True)(qi, lax.fori_loop(0, qi, step(False), (one, one)))
        o_ref[...] = _own_lanes(acca / la, True) + _own_lanes(accb / lb, False)
        lse_ref[0, 0] = ma + jnp.log(la)
        lse_ref[0, 1] = mb + jnp.log(lb)

    return _pcall(
        body, name=name, grid=(B, PAIRS, nq),
        in_specs=[pl.BlockSpec((tq, 2 * LANES), lambda b, g, i: (b * nq + i, g)),
                  pl.BlockSpec((S, 2 * LANES), lambda b, g, i: (b, g)),
                  pl.BlockSpec((S, LANES), lambda b, g, i: (b, v_block0 + g))],
        out_specs=[pl.BlockSpec((tq, LANES), lambda b, g, i: (b * nq + i, g)),
                   pl.BlockSpec((1, 2, tq, 1), lambda b, g, i: (b, g, i, 0))],
        out_shape=[jax.ShapeDtypeStruct((T, PAIRS * LANES), F32), jax.ShapeDtypeStruct((B, MLA_HEADS, S, 1), F32)],
        args=(q, k, kv), ride=ride)


def _attn_dq(q, k, kv, o, lse, do, *, B, S, v_block0, name, ride=None):
    tq = tk = min(ATT_BLOCK, S)
    nq = S // tq
    T = B * S
    scale = _attn_scale()

    def body(q_ref, k_ref, v_ref, o_ref, lse_ref, do_ref, dq_ref, delta_ref):
        qi = pl.program_id(2)
        qs = (q_ref[:, :LANES], q_ref[:, LANES:])
        dos = (_own_lanes(do_ref[...], True), _own_lanes(do_ref[...], False))
        deltas = tuple(jnp.sum(d * o_ref[...], axis=-1, keepdims=True) for d in dos)
        lses = (lse_ref[0, 0], lse_ref[0, 1])

        def step(masked):
            def f(j, carry):
                rows = pl.ds(pl.multiple_of(j * tk, tk), tk)
                vb = v_ref[rows, :]
                out = []
                for h in range(2):
                    kb = k_ref[rows, h * LANES:(h + 1) * LANES]
                    p = jnp.exp(_dot_nt(qs[h], kb) - lses[h])
                    if masked:
                        p = jnp.where(_causal_mask(qi, j, tq, tk), p, 0.0)
                    ds = p * (_dot_nt(dos[h], vb) - deltas[h])
                    out.append(carry[h] + _dot_nn(ds, kb))
                return tuple(out)
            return f

        zero = jnp.zeros((tq, LANES), F32)
        dqa, dqb = step(True)(qi, lax.fori_loop(0, qi, step(False), (zero, zero)))
        dq_ref[:, :LANES] = dqa * scale
        dq_ref[:, LANES:] = dqb * scale
        delta_ref[0, 0] = deltas[0]
        delta_ref[0, 1] = deltas[1]

    qrow = lambda w: pl.BlockSpec((tq, w), lambda b, g, i: (b * nq + i, g))
    stat = pl.BlockSpec((1, 2, tq, 1), lambda b, g, i: (b, g, i, 0))
    return _pcall(
        body, name=name, grid=(B, PAIRS, nq),
        in_specs=[qrow(2 * LANES), pl.BlockSpec((S, 2 * LANES), lambda b, g, i: (b, g)),
                  pl.BlockSpec((S, LANES), lambda b, g, i: (b, v_block0 + g)), qrow(LANES), stat, qrow(LANES)],
        out_specs=[qrow(2 * LANES), stat],
        out_shape=[jax.ShapeDtypeStruct((T, MLA_HEADS * LANES), F32), jax.ShapeDtypeStruct((B, MLA_HEADS, S, 1), F32)],
        args=(q, k, kv, o, lse, do), sem=("parallel", "parallel", "parallel"), ride=ride)


def _attn_dkv(q, k, kv, lse_rows, delta_rows, do, *, B, S, v_block0, name, ride=None):
    tq = tk = min(ATT_BLOCK, S)
    nq = S // tq
    T = B * S

    def body(q_ref, k_ref, v_ref, lse_ref, delta_ref, do_ref, dk_ref, dv_ref):
        kj = pl.program_id(2)
        ks = (k_ref[:, :LANES], k_ref[:, LANES:])
        vb = v_ref[...]

        def step(masked):
            def f(i, carry):
                rows = pl.ds(pl.multiple_of(i * tq, tq), tq)
                do_b = do_ref[rows, :]
                dks, dv = list(carry[:2]), carry[2]
                for h in range(2):
                    qb = q_ref[rows, h * LANES:(h + 1) * LANES]
                    doh = _own_lanes(do_b, h == 0)
                    pt = jnp.exp(_dot_nt(ks[h], qb) - lse_ref[0, h, pl.ds(i, 1), :])
                    if masked:
                        krow = kj * tk + lax.broadcasted_iota(jnp.int32, (tk, tq), 0)
                        qcol = i * tq + lax.broadcasted_iota(jnp.int32, (tk, tq), 1)
                        pt = jnp.where(krow <= qcol, pt, 0.0)
                    dst = pt * (_dot_nt(vb, doh) - delta_ref[0, h, pl.ds(i, 1), :])
                    dks[h] = dks[h] + _dot_nn(dst, qb)
                    dv = dv + _dot_nn(pt, doh)
                return dks[0], dks[1], dv
            return f

        zero = jnp.zeros((tk, LANES), F32)
        dka, dkb, dv = lax.fori_loop(kj + 1, nq, step(False), step(True)(kj, (zero, zero, zero)))
        dk_ref[:, :LANES] = dka
        dk_ref[:, LANES:] = dkb
        dv_ref[...] = dv

    krow = lambda w, c0: pl.BlockSpec((tk, w), lambda b, g, j: (b * nq + j, c0 + g))
    seq = lambda w: pl.BlockSpec((S, w), lambda b, g, j: (b, g))
    stat = pl.BlockSpec((1, 2, nq, tq), lambda b, g, j: (b, g, 0, 0))
    return _pcall(
        body, name=name, grid=(B, PAIRS, nq),
        in_specs=[seq(2 * LANES), krow(2 * LANES, 0), krow(LANES, v_block0), stat, stat, seq(LANES)],
        out_specs=[krow(2 * LANES, 0), krow(LANES, 0)],
        out_shape=[jax.ShapeDtypeStruct((T, MLA_HEADS * LANES), F32), jax.ShapeDtypeStruct((T, PAIRS * LANES), F32)],
        args=(q, k, kv, lse_rows, delta_rows, do), ride=ride)


def _lru_gates(xl, halo, cw_ref, cb_ref, wa_ref, ba_ref, wx_ref, bx_ref, lam_ref):
    xc = cb_ref[...] + cw_ref[3:4, :] * xl
    for kk in range(LRU_CONV - 1):
        xc = xc + cw_ref[kk:kk + 1, :] * _shift_rows(xl, LRU_CONV - 1 - kk, halo)
    r = _sigmoid(_dot_nn(xc, wa_ref[...]) + ba_ref[...])
    i = _sigmoid(_dot_nn(xc, wx_ref[...]) + bx_ref[...])
    lam = lam_ref[...]
    sp = jnp.maximum(-lam, 0.0) + jnp.log(1.0 + jnp.exp(-jnp.abs(lam)))
    a = jnp.exp(-LRU_C * r * sp)
    mult = jnp.sqrt(1.0 - a * a)
    return xc, r, i, sp, a, mult


def _lru_specs(tt, nt, S):
    def make(rev):
        tmap = (lambda t: nt - 1 - t) if rev else (lambda t: t)
        tile = lambda cb: pl.BlockSpec((tt, LRU_WIDTH), lambda b, t: (b * nt + tmap(t), cb))
        prev8 = lambda cb: pl.BlockSpec(
            (8, LRU_WIDTH), lambda b, t: (jnp.maximum((b * nt + tmap(t)) * (tt // 8) - 1, 0), cb))
        return tile, prev8, tmap
    return make


def _lru_fwd(z, cw, cb, wa, ba, wx, bx, lam, *, S, name):
    T = z.shape[0]
    tt = min(ROW_TILE, S)
    nt = S // tt
    tile, prev8, _ = _lru_specs(tt, nt, S)(False)
    vec = lambda r: pl.BlockSpec((r, LRU_WIDTH), lambda b, t: (0, 0))
    mat = pl.BlockSpec((LRU_WIDTH, LRU_WIDTH), lambda b, t: (0, 0))

    def body(xl_ref, halo_ref, gate_ref, cw_ref, cb_ref, wa_ref, ba_ref, wx_ref, bx_ref, lam_ref,
             y_ref, h_ref, carry_ref):
        t = pl.program_id(1)
        first = t == 0
        halo = jnp.where(first, 0.0, halo_ref[...])
        xl_t = xl_ref[...]
        xc, r, i, sp, a, mult = _lru_gates(xl_t, halo, cw_ref, cb_ref, wa_ref, ba_ref, wx_ref, bx_ref, lam_ref)
        bv = mult * (i * xc)
        ones = jnp.ones((8, LRU_WIDTH), F32)
        zeros = jnp.zeros((8, LRU_WIDTH), F32)
        row = lax.broadcasted_iota(jnp.int32, (tt, LRU_WIDTH), 0)
        A = a
        d = 1
        while d < tt:
            if d < 8:
                a_sh = _shift_rows(A, d, ones)
                b_sh = _shift_rows(bv, d, zeros)
            else:
                a_sh = jnp.where(row < d, 1.0, pltpu.roll(A, d, 0))
                b_sh = jnp.where(row < d, 0.0, pltpu.roll(bv, d, 0))
            bv = A * b_sh + bv
            A = A * a_sh
            d *= 2
        h0 = jnp.where(first, 0.0, carry_ref[0:1, :])
        h = A * h0 + bv
        carry_ref[...] = jnp.broadcast_to(h[tt - 1:tt, :], (8, LRU_WIDTH))
        h_ref[...] = h
        y_ref[...] = (h * _gelu(gate_ref[...])).astype(BF16)

    return pl.pallas_call(
        body, name=name, grid=(T // S, nt),
        in_specs=[tile(0), prev8(0), tile(1), vec(LRU_CONV), vec(1), mat, vec(1), mat, vec(1), vec(1)],
        out_specs=[tile(0), tile(0)],
        out_shape=[jax.ShapeDtypeStruct((T, LRU_WIDTH), BF16), jax.ShapeDtypeStruct((T, LRU_WIDTH), F32)],
        scratch_shapes=[pltpu.VMEM((8, LRU_WIDTH), F32)],
        compiler_params=_cparams(("arbitrary", "arbitrary")),
    )(z, z, z, cw, cb, wa, ba, wx, bx, lam)


def _lru_bwd(z, h, dy, cw, cb, wa, ba, wx, bx, lam, *, S, name):
    T = z.shape[0]
    tt = min(ROW_TILE, S)
    nt = S // tt
    tile, prev8, tmap = _lru_specs(tt, nt, S)(True)
    vec = lambda r: pl.BlockSpec((r, LRU_WIDTH), lambda b, t: (0, 0))
    mat = pl.BlockSpec((LRU_WIDTH, LRU_WIDTH), lambda b, t: (0, 0))

    def body(xl_ref, halo_ref, gate_ref, h_ref, hprev_ref, dy_ref, cw_ref, cb_ref, wa_ref, ba_ref, wx_ref,
             bx_ref, lam_ref, dxl_ref, dgate_ref, dcw_ref, dcb_ref, dwa_ref, dba_ref, dwx_ref, dbx_ref,
             dlam_ref, lamc_ref, ac_ref, dxc_ref):
        b = pl.program_id(0)
        t = pl.program_id(1)
        tr = nt - 1 - t
        seq_first = tr == 0
        seq_last = t == 0
        halo = jnp.where(seq_first, 0.0, halo_ref[...])
        xl_t = xl_ref[...]
        xc, r, i, sp, a, mult = _lru_gates(xl_t, halo, cw_ref, cb_ref, wa_ref, ba_ref, wx_ref, bx_ref, lam_ref)
        hh = h_ref[...]
        dyf = dy_ref[...].astype(F32)
        gl, dgl = _gelu_and_grad(gate_ref[...])
        dgate_ref[...] = (dyf * hh * dgl).astype(BF16)
        dh = dyf * gl

        a_first_later = jnp.where(seq_last, 0.0, ac_ref[...])
        lam_later = jnp.where(seq_last, 0.0, lamc_ref[...])
        row = lax.broadcasted_iota(jnp.int32, (tt, LRU_WIDTH), 0)
        A = _shift_rows_up(a, 1, a_first_later)
        lm = dh
        ones = jnp.ones((8, LRU_WIDTH), F32)
        zeros = jnp.zeros((8, LRU_WIDTH), F32)
        d = 1
        while d < tt:
            if d < 8:
                a_sh = _shift_rows_up(A, d, ones)
                l_sh = _shift_rows_up(lm, d, zeros)
            else:
                a_sh = jnp.where(row >= tt - d, 1.0, pltpu.roll(A, tt - d, 0))
                l_sh = jnp.where(row >= tt - d, 0.0, pltpu.roll(lm, tt - d, 0))
            lm = lm + A * l_sh
            A = A * a_sh
            d *= 2
        lm = lm + A * lam_later[0:1, :]
        lamc_ref[...] = jnp.broadcast_to(lm[0:1, :], (8, LRU_WIDTH))
        ac_ref[...] = jnp.broadcast_to(a[0:1, :], (8, LRU_WIDTH))

        hprev_halo = jnp.where(seq_first, 0.0, hprev_ref[...])
        h_prev = _shift_rows(hh, 1, hprev_halo)
        da = lm * h_prev
        ixc = i * xc
        dmult = lm * ixc
        di = lm * mult * xc
        dxc = lm * mult * i
        da = da - dmult * a / mult
        dlog = da * a
        dr = dlog * (-LRU_C) * sp
        dsp_part = jnp.sum(dlog * (-LRU_C) * r, axis=0, keepdims=True)
        dpa = dr * r * (1.0 - r)
        dpx = di * i * (1.0 - i)
        dxc = dxc + _dot_nt(dpa, wa_ref[...]) + _dot_nt(dpx, wx_ref[...])
        dwa_part = _dot_tn(xc, dpa)
        dwx_part = _dot_tn(xc, dpx)

        later = jnp.where(seq_last, 0.0, dxc_ref[...])
        dxl = cw_ref[3:4, :] * dxc
        for kk in range(LRU_CONV - 1):
            dxl = dxl + cw_ref[kk:kk + 1, :] * _shift_rows_up(dxc, LRU_CONV - 1 - kk, later)
        dxl_ref[...] = dxl.astype(BF16)
        dxc_ref[...] = dxc[0:8, :]
        dcw_rows = [jnp.sum(dxc * _shift_rows(xl_t, LRU_CONV - 1 - kk, halo), axis=0, keepdims=True)
                    for kk in range(LRU_CONV - 1)]
        dcw_rows.append(jnp.sum(dxc * xl_t, axis=0, keepdims=True))
        dcw_part = jnp.concatenate(dcw_rows + [jnp.zeros((8 - LRU_CONV, LRU_WIDTH), F32)], axis=0)
        lamv = lam_ref[...]
        dlam_part = dsp_part * (-_sigmoid(-lamv))
        parts = ((dcw_ref, dcw_part), (dcb_ref, jnp.sum(dxc, axis=0, keepdims=True)),
                 (dwa_ref, dwa_part), (dba_ref, jnp.sum(dpa, axis=0, keepdims=True)),
                 (dwx_ref, dwx_part), (dbx_ref, jnp.sum(dpx, axis=0, keepdims=True)),
                 (dlam_ref, dlam_part))
        start = jnp.logical_and(b == 0, t == 0)

        @pl.when(start)
        def _():
            for ref, val in parts:
                ref[...] = val

        @pl.when(jnp.logical_not(start))
        def _():
            for ref, val in parts:
                ref[...] += val

    acc = lambda r: pl.BlockSpec((r, LRU_WIDTH), lambda b, t: (0, 0))
    return pl.pallas_call(
        body, name=name, grid=(T // S, nt),
        in_specs=[tile(0), prev8(0), tile(1), tile(0), prev8(0), tile(0),
                  vec(LRU_CONV), vec(1), mat, vec(1), mat, vec(1), vec(1)],
        out_specs=[tile(0), tile(0), acc(8), acc(1), mat, acc(1), mat, acc(1), acc(1)],
        out_shape=[jax.ShapeDtypeStruct((T, LRU_WIDTH), BF16), jax.ShapeDtypeStruct((T, LRU_WIDTH), BF16),
                   jax.ShapeDtypeStruct((8, LRU_WIDTH), F32), jax.ShapeDtypeStruct((1, LRU_WIDTH), F32),
                   jax.ShapeDtypeStruct((LRU_WIDTH, LRU_WIDTH), F32), jax.ShapeDtypeStruct((1, LRU_WIDTH), F32),
                   jax.ShapeDtypeStruct((LRU_WIDTH, LRU_WIDTH), F32), jax.ShapeDtypeStruct((1, LRU_WIDTH), F32),
                   jax.ShapeDtypeStruct((1, LRU_WIDTH), F32)],
        scratch_shapes=[pltpu.VMEM((8, LRU_WIDTH), F32), pltpu.VMEM((8, LRU_WIDTH), F32),
                        pltpu.VMEM((8, LRU_WIDTH), F32)],
        compiler_params=_cparams(("arbitrary", "arbitrary")),
    )(z, z, z, h, h, dy, cw, cb, wa, ba, wx, bx, lam)


FFN_CT = 1408


def _ffn_conv(g, halo, cw_ref, cb_ref):
    gc = cb_ref[...] + cw_ref[2:3, :] * g
    for kk in range(FFN_CONV - 1):
        gc = gc + cw_ref[kk:kk + 1, :] * _shift_rows(g, FFN_CONV - 1 - kk, halo)
    return gc


def _ffn_act_fwd(g, u, cw, cb, *, S, name, ride=None):
    T, F = g.shape
    tt = min(ROW_TILE, S)
    nt = S // tt
    tc = _tile(F, FFN_CT)

    def body(g_ref, halo_ref, u_ref, cw_ref, cb_ref, o_ref):
        first = (pl.program_id(0) % nt) == 0
        halo = jnp.where(first, 0.0, halo_ref[...])
        gc = _ffn_conv(g_ref[...], halo, cw_ref, cb_ref)
        o_ref[...] = (_gelu(gc) * u_ref[...]).astype(BF16)

    tile = pl.BlockSpec((tt, tc), lambda i, j: (i, j))
    prev8 = pl.BlockSpec((8, tc), lambda i, j: (jnp.maximum(i * (tt // 8) - 1, 0), j))
    return _pcall(
        body, name=name, grid=(T // tt, F // tc),
        in_specs=[tile, prev8, tile, pl.BlockSpec((FFN_CONV, tc), lambda i, j: (0, j)),
                  pl.BlockSpec((1, tc), lambda i, j: (0, j))],
        out_specs=[tile], out_shape=[jax.ShapeDtypeStruct((T, F), BF16)], args=(g, g, u, cw, cb),
        sem=("parallel", "parallel"), ride=ride)[0]


def _ffn_act_bwd(g, u, dact, cw, cb, *, S, name, ride=None):
    T, F = g.shape
    tt = min(ROW_TILE, S)
    nt = S // tt
    ntt = T // tt
    tc = _tile(F, FFN_CT)

    def body(g_ref, halo_ref, u_ref, da_ref, cw_ref, cb_ref, dg_ref, du_ref, dcw_ref, dcb_ref, later_ref):
        step = pl.program_id(1)
        ti = (ntt - 1 - step) % nt
        halo = jnp.where(ti == 0, 0.0, halo_ref[...])
        gt = g_ref[...]
        gc = _ffn_conv(gt, halo, cw_ref, cb_ref)
        gl, dgl = _gelu_and_grad(gc)
        da = da_ref[...].astype(F32)
        du_ref[...] = (da * gl).astype(BF16)
        dgc = da * u_ref[...] * dgl
        later = jnp.where(ti == nt - 1, 0.0, later_ref[...])
        dg = cw_ref[2:3, :] * dgc
        for kk in range(FFN_CONV - 1):
            dg = dg + cw_ref[kk:kk + 1, :] * _shift_rows_up(dgc, FFN_CONV - 1 - kk, later)
        dg_ref[...] = dg.astype(BF16)
        later_ref[...] = dgc[0:8, :]
        rows = [jnp.sum(dgc * _shift_rows(gt, FFN_CONV - 1 - kk, halo), axis=0, keepdims=True)
                for kk in range(FFN_CONV - 1)]
        rows.append(jnp.sum(dgc * gt, axis=0, keepdims=True))
        dcw_part = jnp.concatenate(rows + [jnp.zeros((8 - FFN_CONV, tc), F32)], axis=0)
        dcb_part = jnp.sum(dgc, axis=0, keepdims=True)

        @pl.when(step == 0)
        def _():
            dcw_ref[...] = dcw_part
            dcb_ref[...] = dcb_part

        @pl.when(step > 0)
        def _():
            dcw_ref[...] += dcw_part
            dcb_ref[...] += dcb_part

    tile = pl.BlockSpec((tt, tc), lambda j, s: (ntt - 1 - s, j))
    prev8 = pl.BlockSpec((8, tc), lambda j, s: (jnp.maximum((ntt - 1 - s) * (tt // 8) - 1, 0), j))
    return _pcall(
        body, name=name, grid=(F // tc, ntt),
        in_specs=[tile, prev8, tile, tile, pl.BlockSpec((FFN_CONV, tc), lambda j, s: (0, j)),
                  pl.BlockSpec((1, tc), lambda j, s: (0, j))],
        out_specs=[tile, tile, pl.BlockSpec((8, tc), lambda j, s: (0, j)), pl.BlockSpec((1, tc), lambda j, s: (0, j))],
        out_shape=[jax.ShapeDtypeStruct((T, F), BF16), jax.ShapeDtypeStruct((T, F), BF16),
                   jax.ShapeDtypeStruct((8, F), F32), jax.ShapeDtypeStruct((1, F), F32)],
        args=(g, g, u, dact, cw, cb), scratch=[pltpu.VMEM((8, tc), F32)], ride=ride)


def _sgu_norm(zv, g_ref, b_ref):
    v = _gelu(zv)
    mu = jnp.mean(v, axis=-1, keepdims=True)
    xc = v - mu
    rstd = lax.rsqrt(jnp.mean(xc * xc, axis=-1, keepdims=True) + NORM_EPS)
    xhat = xc * rstd
    return xhat, rstd, xhat * g_ref[...] + b_ref[...]


def _sgu_fwd(zc, ln_g, ln_b, wm, bmap, *, name):
    T = zc.shape[0]
    W = SGU_WIDTH
    tt = ROW_TILE
    nch = tt // CHUNK

    def body(z_ref, g_ref, b_ref, wm_ref, bm_ref, p_ref):
        u = _gelu(z_ref[:, :W])
        _, _, vn = _sgu_norm(z_ref[:, W:], g_ref, b_ref)
        vn = vn.astype(BF16)
        for n in range(nch):
            rows = slice(n * CHUNK, (n + 1) * CHUNK)
            for gi in range(SGU_GROUPS):
                cols = slice(gi * LANES, (gi + 1) * LANES)
                s = _dot_nn(wm_ref[gi], vn[rows, cols]) + bm_ref[:, cols]
                p_ref[rows, cols] = (u[rows, cols] * s).astype(BF16)

    const2 = lambda r, c: pl.BlockSpec((r, c), lambda i: (0, 0))
    return pl.pallas_call(
        body, name=name, grid=(T // tt,),
        in_specs=[pl.BlockSpec((tt, 2 * W), lambda i: (i, 0)), const2(1, W), const2(1, W),
                  pl.BlockSpec((SGU_GROUPS, CHUNK, CHUNK), lambda i: (0, 0, 0)), const2(CHUNK, W)],
        out_specs=pl.BlockSpec((tt, W), lambda i: (i, 0)),
        out_shape=jax.ShapeDtypeStruct((T, W), BF16),
        compiler_params=_cparams(("parallel",)),
    )(zc, ln_g, ln_b, wm, bmap)


def _sgu_bwd(zc, dp, ln_g, ln_b, wm, bmap, *, name, ride=None):
    T = zc.shape[0]
    W = SGU_WIDTH
    tt = ROW_TILE
    nch = tt // CHUNK
    nsteps = T // tt

    def body(z_ref, dp_ref, g_ref, b_ref, wm_ref, bm_ref, dz_ref, dg_ref, db_ref, dwm_ref, dbm_ref,
             s_scr, dvn_scr):
        step = pl.program_id(0)
        zu = z_ref[:, :W]
        zv = z_ref[:, W:]
        u, dgu = _gelu_and_grad(zu)
        xhat, rstd, vn = _sgu_norm(zv, g_ref, b_ref)
        vnb = vn.astype(BF16)
        dpf = dp_ref[...].astype(F32)
        ds = dpf * u

        @pl.when(step == 0)
        def _():
            dwm_ref[...] = jnp.zeros_like(dwm_ref)
            dbm_ref[...] = jnp.zeros_like(dbm_ref)

        for n in range(nch):
            rows = slice(n * CHUNK, (n + 1) * CHUNK)
            for gi in range(SGU_GROUPS):
                cols = slice(gi * LANES, (gi + 1) * LANES)
                s_scr[rows, cols] = _dot_nn(wm_ref[gi], vnb[rows, cols]) + bm_ref[:, cols]
                dsb = ds[rows, cols]
                dvn_scr[rows, cols] = _dot_tn(wm_ref[gi], dsb)
                dwm_ref[gi] += _dot_nt(dsb, vnb[rows, cols])
                dbm_ref[:, cols] += dsb
        dz_ref[:, :W] = (dpf * s_scr[...] * dgu).astype(BF16)
        dvn = dvn_scr[...]
        dxhat = dvn * g_ref[...]
        dv = rstd * (dxhat - jnp.mean(dxhat, axis=-1, keepdims=True)
                     - xhat * jnp.mean(dxhat * xhat, axis=-1, keepdims=True))
        _, dgv = _gelu_and_grad(zv)
        dz_ref[:, W:] = (dv * dgv).astype(BF16)
        dg_part = jnp.sum(dvn * xhat, axis=0, keepdims=True)
        db_part = jnp.sum(dvn, axis=0, keepdims=True)

        @pl.when(step == 0)
        def _():
            dg_ref[...] = dg_part
            db_ref[...] = db_part

        @pl.when(step > 0)
        def _():
            dg_ref[...] += dg_part
            db_ref[...] += db_part

        @pl.when(step == nsteps - 1)
        def _():
            for gi in range(SGU_GROUPS):
                cols = slice(gi * LANES, (gi + 1) * LANES)
                tot = jnp.sum(dbm_ref[:, cols], axis=1, keepdims=True)
                dbm_ref[:, cols] = jnp.broadcast_to(tot, (CHUNK, LANES))

    const2 = lambda r, c: pl.BlockSpec((r, c), lambda i: (0, 0))
    wspec = pl.BlockSpec((SGU_GROUPS, CHUNK, CHUNK), lambda i: (0, 0, 0))
    return _pcall(
        body, name=name, grid=(nsteps,),
        in_specs=[pl.BlockSpec((tt, 2 * W), lambda i: (i, 0)), pl.BlockSpec((tt, W), lambda i: (i, 0)),
                  const2(1, W), const2(1, W), wspec, const2(CHUNK, W)],
        out_specs=[pl.BlockSpec((tt, 2 * W), lambda i: (i, 0)), const2(1, W), const2(1, W), wspec, const2(CHUNK, W)],
        out_shape=[jax.ShapeDtypeStruct((T, 2 * W), BF16), jax.ShapeDtypeStruct((1, W), F32),
                   jax.ShapeDtypeStruct((1, W), F32), jax.ShapeDtypeStruct((SGU_GROUPS, CHUNK, CHUNK), F32),
                   jax.ShapeDtypeStruct((CHUNK, W), F32)],
        args=(zc, dp, ln_g, ln_b, wm, bmap), scratch=[pltpu.VMEM((tt, W), F32), pltpu.VMEM((tt, W), F32)], ride=ride)


def _rope_tables(positions):
    half = QK_ROPE // 2
    inv_freq = jnp.exp(-math.log(ROPE_BASE) * jnp.arange(half, dtype=F32) / half)
    ang = positions.reshape(-1).astype(F32)[:, None] * inv_freq
    cos = jnp.cos(ang)
    sin = jnp.sin(ang)
    n = ang.shape[0]
    tail = LANES - QK_NOPE - QK_ROPE
    cos_t = jnp.concatenate([jnp.ones((n, QK_NOPE), F32), cos, cos, jnp.ones((n, tail), F32)], axis=1)
    sin_t = jnp.concatenate([jnp.zeros((n, QK_NOPE), F32), -sin, sin, jnp.zeros((n, tail), F32)], axis=1)
    return cos_t, sin_t


SGU_GROUP_DIM = SGU_WIDTH // SGU_GROUPS
_O1, _O2, _O3, _O4 = Q_LORA, Q_LORA + KV_LORA, Q_LORA + KV_LORA + QK_ROPE, Q_LORA + KV_LORA + QK_ROPE + LRU_WIDTH
_A0, _A1, _A2 = 2 * LRU_WIDTH, 2 * LRU_WIDTH + Q_LORA, 2 * LRU_WIDTH + Q_LORA + KV_LORA
_A3 = _A2 + QK_NOPE
Z_Q_BLOCK, Z_KV_BLOCK, Z_KPE_BLOCK = _A0 // Q_LORA, _A1 // KV_LORA, _A2 // LANES


def _perm_w_in(w_in):
    zeros = lambda n: jnp.zeros((w_in.shape[0], n), w_in.dtype)
    return jnp.concatenate([w_in[:, _O3:_O4], w_in[:, _O4:], w_in[:, :_O1], w_in[:, _O1:_O2], zeros(QK_NOPE),
                            w_in[:, _O2:_O3], zeros(LANES - QK_NOPE - QK_ROPE)], axis=1)


def _unperm_w_in(w):
    return jnp.concatenate([w[:, _A0:_A1], w[:, _A1:_A2], w[:, _A3:_A3 + QK_ROPE], w[:, :LRU_WIDTH],
                            w[:, LRU_WIDTH:_A0]], axis=1)


def _head_blocks(w, d):
    r = w.shape[0]
    return jnp.pad(w.reshape(r, MLA_HEADS, d), ((0, 0), (0, 0), (0, LANES - d))).reshape(r, MLA_HEADS * LANES)


def _from_head_blocks(w, d):
    r = w.shape[0]
    return w.reshape(r, MLA_HEADS, LANES)[:, :, :d].reshape(r, MLA_HEADS * d)


def _split_kv(w_kv):
    r = w_kv.shape[0]
    w3 = w_kv.reshape(r, MLA_HEADS, QK_NOPE + V_HEAD)
    return _head_blocks(w3[:, :, :QK_NOPE].reshape(r, -1), QK_NOPE), w3[:, :, QK_NOPE:].reshape(r, -1)


def _join_kv(w_k, w_v):
    r = w_k.shape[0]
    return jnp.concatenate([_from_head_blocks(w_k, QK_NOPE).reshape(r, MLA_HEADS, QK_NOPE),
                            w_v.reshape(r, MLA_HEADS, V_HEAD)], axis=2).reshape(r, -1)


def _prep_small(w):
    p = {n: w[n] for n in w if n not in BIG}
    eye = jnp.eye(LRU_HEADS, dtype=F32)
    dense = lambda wg: (wg[:, :, None, :] * eye[:, None, :, None]).reshape(LRU_WIDTH, LRU_WIDTH).astype(BF16)
    p["wa_d"] = dense(w["ab_w_rg_a"][0])
    p["wx_d"] = dense(w["ab_w_rg_x"][0])
    causal = jnp.tril(jnp.ones((CHUNK, CHUNK), F32))
    p["wm"] = (w["c_w_s"][0] * causal).astype(BF16)
    p["bmap"] = jnp.repeat(w["c_b_s"][0].T, SGU_GROUP_DIM, axis=1)
    return p


def _prep_big(ab_w_in, ab_w_q_b, ab_w_kv_b):
    return {"w_in_p": _perm_w_in(ab_w_in).astype(BF16),
            "w_q_p": _head_blocks(ab_w_q_b, QK_NOPE + QK_ROPE).astype(BF16),
            "w_kv_p": jnp.concatenate(_split_kv(ab_w_kv_b), axis=1).astype(BF16)}


def _ffn_fwd(h, l, p, S, rides):
    hn = _rms_fwd(h, p["ffn_norm"][l], name=f"ffn{l}_norm")
    g = _mm(hn, p["ffn_gate_t"][l], tb=True, name=f"ffn{l}_gate", ride=rides.get(f"ffn{l}_gate"))
    u = _mm(hn, p["ffn_up_t"][l], tb=True, name=f"ffn{l}_up", ride=rides.get(f"ffn{l}_up"))
    act = _ffn_act_fwd(g, u, p["ffn_conv_w"][l], p["ffn_conv_b"][l][None], S=S, name=f"ffn{l}_act",
                       ride=rides.get(f"ffn{l}_act"))
    out = _mm(act, p["ffn_down"][l], res=h, name=f"ffn{l}_down", ride=rides.get(f"ffn{l}_down"))
    return out, (hn, g, u, act)


def _ffn_bwd(dh, h_in, l, p, saved, S, rides):
    hn, g, u, act = saved
    dact = _mm(dh, p["ffn_down"][l], tb=True, out_dtype=BF16, name=f"ffn{l}_dact", ride=rides.get(f"ffn{l}_dact"))
    dw_down = _mm(act, dh, ta=True, out_dtype=BF16, name=f"ffn{l}_dwdown")
    dg, du, dcw, dcb = _ffn_act_bwd(g, u, dact, p["ffn_conv_w"][l], p["ffn_conv_b"][l][None], S=S,
                                    name=f"ffn{l}_dactbwd", ride=rides.get(f"ffn{l}_dactbwd"))
    dhn = _mm(dg, p["ffn_gate_t"][l], name=f"ffn{l}_dhn_g")
    dhn = _mm(du, p["ffn_up_t"][l], res=dhn, name=f"ffn{l}_dhn_u")
    dw_gate_t = _mm(dg, hn, ta=True, out_dtype=BF16, name=f"ffn{l}_dwgate")
    dw_up_t = _mm(du, hn, ta=True, out_dtype=BF16, name=f"ffn{l}_dwup")
    dh_in, dnorm = _rms_bwd(h_in, p["ffn_norm"][l], dhn, res=dh, name=f"ffn{l}_dnorm")
    grads = dict(ffn_norm=dnorm[0], ffn_gate_t=dw_gate_t, ffn_up_t=dw_up_t, ffn_conv_w=dcw[:FFN_CONV],
                 ffn_conv_b=dcb[0], ffn_down=dw_down)
    return dh_in, grads


def _local_step(x, positions, target, p, rides=None, grads_ready=None):
    rides = dict(rides or {})
    more_rides = grads_ready or (lambda layer, ready: {})
    B, S, D = x.shape
    T = B * S
    H = MLA_HEADS
    xf = x.reshape(T, D)
    tgt = target.reshape(T, D)
    cos, sin = _rope_tables(positions)

    hn0 = _rms_fwd(xf, p["ab_norm"][0], name="ab_norm")
    z = _mm(hn0, p["w_in_p"], name="ab_in")
    cqn = _rms_fwd(z, p["ab_q_norm"][0], cb=Z_Q_BLOCK, name="q_norm")
    ckvn = _rms_fwd(z, p["ab_kv_norm"][0], cb=Z_KV_BLOCK, name="kv_norm")
    q = _mm(cqn, p["w_q_p"], name="q_up")
    kv = _mm(ckvn, p["w_kv_p"], out_dtype=BF16, name="kv_up")
    qs = _rope_q(q, cos, sin, name="q_rope")
    kk = _key_blocks(kv, z, cos, sin, kpe_block=Z_KPE_BLOCK, name="k_rope")
    att = dict(B=B, S=S, v_block0=H)
    o, lse = _attn_fwd(qs, kk, kv, name="attn_fwd", ride=rides.get("attn_fwd"), **att)
    lru_par = (p["ab_conv_w"][0], p["ab_conv_b"], p["wa_d"], p["ab_b_rg_a"], p["wx_d"], p["ab_b_rg_x"], p["ab_lambda"])
    y_lru, hs = _lru_fwd(z, *lru_par, S=S, name="lru_fwd")
    n_att = H * V_HEAD
    w_out_a, w_out_b = p["ab_w_out"][:n_att], p["ab_w_out"][n_att:]
    h1 = _mm(y_lru, w_out_b, res=_mm(o, w_out_a, res=xf, name="ab_out_a"), name="ab_out_b")
    h2, ffn0 = _ffn_fwd(h1, 0, p, S, rides)

    hn2 = _rms_fwd(h2, p["c_norm"][0], name="c_norm")
    zc = _mm(hn2, p["c_w_in_t"], tb=True, name="c_in")
    pg = _sgu_fwd(zc, p["c_ln_g"], p["c_ln_b"], p["wm"], p["bmap"], name="sgu_fwd")
    h3 = _mm(pg, p["c_w_out"], res=h2, name="c_out")
    h4, ffn1 = _ffn_fwd(h3, 1, p, S, rides)

    loss_row, dh4, dfinal = _final_fwd_bwd(h4, p["final_norm"], tgt, name="final")

    big = ("ffn_gate_t", "ffn_up_t", "ffn_down")
    dh3, g_ffn1 = _ffn_bwd(dh4, h3, 1, p, ffn1, S, rides)
    rides.update(more_rides(1, {n: g_ffn1[n] for n in big}))
    dpg = _mm(dh3, p["c_w_out"], tb=True, out_dtype=BF16, name="c_dp")
    dw_c_out = _mm(pg, dh3, ta=True, out_dtype=BF16, name="c_dwout")
    dzc, dlng, dlnb, dwm, dbm = _sgu_bwd(zc, dpg, p["c_ln_g"], p["c_ln_b"], p["wm"], p["bmap"], name="sgu_bwd",
                                         ride=rides.get("sgu_bwd"))
    dhn2 = _mm(dzc, p["c_w_in_t"], name="c_dhn")
    dw_c_in_t = _mm(dzc, hn2, ta=True, out_dtype=BF16, name="c_dwin")
    dh2, dcnorm = _rms_bwd(h2, p["c_norm"][0], dhn2, res=dh3, name="c_dnorm")
    dh1, g_ffn0 = _ffn_bwd(dh2, h1, 0, p, ffn0, S, rides)
    rides.update(more_rides(0, {"c_w_in_t": dw_c_in_t, "c_w_out": dw_c_out, **{n: g_ffn0[n] for n in big}}))

    do = _mm(dh1, w_out_a, tb=True, name="ab_do")
    dy_lru = _mm(dh1, w_out_b, tb=True, out_dtype=BF16, name="ab_dylru")
    dw_out = jnp.concatenate([_mm(o, dh1, ta=True, out_dtype=BF16, name="ab_dwout_a"),
                              _mm(y_lru, dh1, ta=True, out_dtype=BF16, name="ab_dwout_b")], axis=0)
    dq, delta = _attn_dq(qs, kk, kv, o, lse, do, name="attn_dq", ride=rides.get("attn_dq"), **att)
    nq = S // min(ATT_BLOCK, S)
    rows = lambda a: a.reshape(B, H, nq, S // nq)
    dk, dv = _attn_dkv(qs, kk, kv, rows(lse), rows(delta), do, name="attn_dkv", ride=rides.get("attn_dkv"), **att)
    dq_full = _rope_q_bwd(dq, cos, sin, name="q_rope_bwd")
    dkr = _key_rope_bwd(dk, cos, sin, name="k_rope_bwd")
    n_key = H * LANES
    w_k_p, w_v_p = p["w_kv_p"][:, :n_key], p["w_kv_p"][:, n_key:]
    dcqn = _mm(dq_full, p["w_q_p"], tb=True, name="q_dlat")
    dw_q_p = _mm(cqn, dq_full, ta=True, out_dtype=BF16, name="q_dw")
    dckvn = _mm(dv, w_v_p, tb=True, res=_mm(dk, w_k_p, tb=True, name="k_dlat"), name="v_dlat")
    dw_k_p = _mm(ckvn, dk, ta=True, out_dtype=BF16, name="k_dw")
    dw_v_p = _mm(ckvn, dv, ta=True, out_dtype=BF16, name="v_dw")
    dcq, dqnorm = _rms_bwd(z, p["ab_q_norm"][0], dcqn, cb=Z_Q_BLOCK, out_dtype=BF16, name="q_dnorm")
    dckv, dkvnorm = _rms_bwd(z, p["ab_kv_norm"][0], dckvn, cb=Z_KV_BLOCK, out_dtype=BF16, name="kv_dnorm")
    dxl, dgate, dcw, dcb, dwa, dba, dwx, dbx, dlam = _lru_bwd(z, hs, dy_lru, *lru_par, S=S, name="lru_bwd")
    dz = jnp.concatenate([dxl, dgate, dcq, dckv, dkr], axis=1)
    dhn0 = _mm(dz, p["w_in_p"], tb=True, name="ab_dhn")
    dw_in_p = _mm(hn0, dz, ta=True, out_dtype=BF16, name="ab_dwin")
    dx, dabnorm = _rms_bwd(xf, p["ab_norm"][0], dhn0, res=dh1, name="ab_dnorm")

    blocks = lambda dd: jnp.stack([dd[i * LRU_BLOCK:(i + 1) * LRU_BLOCK, i * LRU_BLOCK:(i + 1) * LRU_BLOCK]
                                   for i in range(LRU_HEADS)])
    causal = jnp.tril(jnp.ones((CHUNK, CHUNK), F32))
    grads = {
        "ab_norm": dabnorm, "w_in_p": dw_in_p, "ab_q_norm": dqnorm, "w_q_p": dw_q_p,
        "ab_kv_norm": dkvnorm, "w_k_p": dw_k_p, "w_v_p": dw_v_p, "ab_conv_w": dcw[:LRU_CONV][None], "ab_conv_b": dcb,
        "ab_w_rg_a": blocks(dwa)[None], "ab_b_rg_a": dba, "ab_w_rg_x": blocks(dwx)[None], "ab_b_rg_x": dbx,
        "ab_lambda": dlam, "ab_w_out": dw_out,
        "c_norm": dcnorm, "c_w_in_t": dw_c_in_t, "c_ln_g": dlng, "c_ln_b": dlnb,
        "c_w_s": (dwm * causal)[None], "c_b_s": dbm[:, ::SGU_GROUP_DIM].T[None], "c_w_out": dw_c_out,
        "final_norm": dfinal[0],
    }
    for name in ("ffn_norm", "ffn_conv_w", "ffn_conv_b"):
        grads[name] = jnp.stack([g_ffn0[name], g_ffn1[name]])
    for name in ("ffn_gate_t", "ffn_up_t", "ffn_down"):
        grads[name] = [g_ffn0[name], g_ffn1[name]]
    return loss_row, dx.reshape(B, S, D), grads


ANY = pl.BlockSpec(memory_space=pl.ANY)


def _place():
    x, y, c = lax.axis_index("x"), lax.axis_index("y"), lax.axis_index("c")
    chips = [(1 - x, y), (x, 1 - y), (1 - x, 1 - y)]
    return x, y, c, 2 * x + y, (x, y, 1 - c), chips


def _remote(src, dst, send_sems, recv_sems, k, to):
    return pltpu.make_async_remote_copy(src_ref=src, dst_ref=dst, send_sem=send_sems.at[k], recv_sem=recv_sems.at[k],
                                        device_id=to, device_id_type=MESH)


class _Exchange:
    def __init__(self, arrs, out_shapes, n_sems, start, finish):
        self.arrs, self.out_shapes, self.n_sems, self.start, self.finish = list(arrs), out_shapes, n_sems, start, finish

    @property
    def in_specs(self):
        return [ANY] * len(self.arrs)

    @property
    def out_specs(self):
        return [ANY] * len(self.out_shapes)

    @property
    def scratch(self):
        return [pltpu.SemaphoreType.DMA((self.n_sems,)), pltpu.SemaphoreType.DMA((self.n_sems,))]

    def split(self, refs):
        n = len(self.arrs)
        return refs[:n], refs[n:n + len(self.out_shapes)], refs[-2], refs[-1]

    def run(self, name):
        def body(*refs):
            parts = self.split(refs)
            self.start(*parts)
            self.finish(*parts)

        return pl.pallas_call(body, name=name, in_specs=self.in_specs, out_specs=self.out_specs,
                              out_shape=self.out_shapes, scratch_shapes=self.scratch)(*self.arrs)


def _put(buf, piece, idx, axis):
    return lax.dynamic_update_slice_in_dim(buf, jnp.expand_dims(piece, axis).astype(buf.dtype), idx, axis)


def _all_gather(arrs):
    n = len(arrs)

    def start(ins, outs, send_sems, recv_sems):
        x, y, c, j, sib, chips = _place()
        for i in range(n):
            for k, (cx, cy) in enumerate(chips):
                _remote(ins[i].at[:, c], outs[i].at[:, j, c], send_sems, recv_sems, 6 * i + k, (cx, cy, c)).start()

    def finish(ins, outs, send_sems, recv_sems):
        x, y, c, j, sib, chips = _place()
        passed = []
        for i in range(n):
            for k, (cx, cy) in enumerate(chips):
                got = outs[i].at[:, 2 * cx + cy, c]
                _remote(got, got, send_sems, recv_sems, 6 * i + k, (cx, cy, c)).wait_recv()
                cp = _remote(got, got, send_sems, recv_sems, 6 * i + 3 + k, sib)
                cp.start()
                passed.append(cp)
        for i in range(n):
            for k, (cx, cy) in enumerate(chips):
                got = outs[i].at[:, 2 * cx + cy, 1 - c]
                _remote(got, got, send_sems, recv_sems, 6 * i + 3 + k, sib).wait_recv()
                _remote(ins[i].at[:, c], ins[i].at[:, c], send_sems, recv_sems, 6 * i + k, sib).wait_send()
        for cp in passed:
            cp.wait_send()

    shapes = [jax.ShapeDtypeStruct((a.shape[0], N_CHIPS) + a.shape[1:], a.dtype) for a in arrs]
    return _Exchange(arrs, shapes, 6 * n, start, finish)


def _pair_swap(arrs):
    n = len(arrs)

    def start(ins, outs, send_sems, recv_sems):
        x, y, c, j, sib, chips = _place()
        for i in range(n):
            _remote(ins[i].at[:, 1 - c], outs[i], send_sems, recv_sems, i, sib).start()

    def finish(ins, outs, send_sems, recv_sems):
        x, y, c, j, sib, chips = _place()
        for i in range(n):
            _remote(ins[i].at[:, 1 - c], outs[i], send_sems, recv_sems, i, sib).wait()

    shapes = [jax.ShapeDtypeStruct((a.shape[0],) + a.shape[2:], a.dtype) for a in arrs]
    return _Exchange(arrs, shapes, n, start, finish)


def _pair_send(arrs):
    n = len(arrs)

    def start(ins, outs, send_sems, recv_sems):
        x, y, c, j, sib, chips = _place()
        for i in range(n):
            _remote(ins[i], outs[i], send_sems, recv_sems, i, sib).start()

    def finish(ins, outs, send_sems, recv_sems):
        x, y, c, j, sib, chips = _place()
        for i in range(n):
            _remote(ins[i], outs[i], send_sems, recv_sems, i, sib).wait()

    shapes = [jax.ShapeDtypeStruct(a.shape, a.dtype) for a in arrs]
    return _Exchange(arrs, shapes, n, start, finish)


def _chip_exchange(arrs, *, scatter):
    n = len(arrs)

    def copies(ins, outs, send_sems, recv_sems):
        x, y, c, j, sib, chips = _place()
        return [(_remote(ins[i].at[2 * cx + cy] if scatter else ins[i], outs[i].at[j], send_sems, recv_sems,
                         3 * i + k, (cx, cy, c)),
                 _remote(outs[i].at[2 * cx + cy], outs[i].at[2 * cx + cy], send_sems, recv_sems, 3 * i + k, (cx, cy, c)))
                for i in range(n) for k, (cx, cy) in enumerate(chips)]

    def start(*refs):
        for out, _ in copies(*refs):
            out.start()

    def finish(*refs):
        for out, back in copies(*refs):
            back.wait_recv()
            out.wait_send()

    shapes = [jax.ShapeDtypeStruct((N_CHIPS,) + a.shape[-2:], a.dtype) for a in arrs]
    return _Exchange(arrs, shapes, 3 * n, start, finish)


FLAT_ROWS = 512


def _add2(a, b, *, out_dtype, name):
    n, R, L = a.shape
    tr = _tile(R, FLAT_ROWS, 16)

    def body(a_ref, b_ref, o_ref):
        o_ref[...] = (a_ref[...].astype(F32) + b_ref[...].astype(F32)).astype(out_dtype)

    spec = pl.BlockSpec((n, tr, L), lambda i: (0, i, 0))
    return pl.pallas_call(
        body, name=name, grid=(R // tr,), in_specs=[spec, spec], out_specs=spec,
        out_shape=jax.ShapeDtypeStruct(a.shape, out_dtype), compiler_params=_cparams(("parallel",)),
    )(a, b)


def _sum_slots(buf, *, name):
    n, R, L = buf.shape
    tr = _tile(R, FLAT_ROWS, 16)

    def body(b_ref, o_ref):
        acc = b_ref[0].astype(F32)
        for k in range(1, n):
            acc = acc + b_ref[k].astype(F32)
        o_ref[...] = acc

    return pl.pallas_call(
        body, name=name, grid=(R // tr,), in_specs=[pl.BlockSpec((n, tr, L), lambda i: (0, i, 0))],
        out_specs=pl.BlockSpec((tr, L), lambda i: (i, 0)),
        out_shape=jax.ShapeDtypeStruct((R, L), F32), compiler_params=_cparams(("parallel",)),
    )(buf)


def _adamw(w, g, m, v, *, name):
    NL, R, L = w.shape
    tr = _tile(R, FLAT_ROWS, 16)
    c1 = 1.0 - ADAM_B1 ** ADAM_STEP
    c2 = 1.0 - ADAM_B2 ** ADAM_STEP

    def body(w_ref, g_ref, m_ref, v_ref, d_ref, nm_ref, nv_ref):
        gg = g_ref[...]
        mm = ADAM_B1 * m_ref[...] + (1.0 - ADAM_B1) * gg
        vv = ADAM_B2 * v_ref[...] + (1.0 - ADAM_B2) * (gg * gg)
        nm_ref[...] = mm
        nv_ref[...] = vv
        d_ref[...] = -ADAM_LR * ((mm / c1) / (jnp.sqrt(vv / c2) + ADAM_EPS) + ADAM_WD * w_ref[...])

    spec = pl.BlockSpec((1, tr, L), lambda l, i: (l, i, 0))
    sh = jax.ShapeDtypeStruct((NL, R, L), F32)
    return pl.pallas_call(
        body, name=name, grid=(NL, R // tr), in_specs=[spec] * 4, out_specs=[spec] * 3, out_shape=[sh] * 3,
        compiler_params=_cparams(("parallel", "parallel")),
    )(w, g, m, v)


WEIGHT_NAMES = ["ab_norm", "ab_w_in", "ab_q_norm", "ab_w_q_b", "ab_kv_norm", "ab_w_kv_b", "ab_conv_w", "ab_conv_b",
                "ab_w_rg_a", "ab_b_rg_a", "ab_w_rg_x", "ab_b_rg_x", "ab_lambda", "ab_w_out", "c_norm", "c_w_in",
                "c_ln_g", "c_ln_b", "c_w_s", "c_b_s", "c_w_out", "ffn_norm", "ffn_w_gate", "ffn_w_up", "ffn_conv_w",
                "ffn_conv_b", "ffn_w_down", "final_norm"]
BIG = {"ab_w_in": 2, "ab_w_q_b": 2, "ab_w_kv_b": 2, "ab_w_out": 1, "c_w_in": 2, "c_w_out": 1,
       "ffn_w_gate": 2, "ffn_w_up": 2, "ffn_w_down": 1}
SMALL_SHARDED = {"ab_conv_w": 2, "c_norm": 1, "c_ln_g": 1, "c_ln_b": 1, "ffn_conv_w": 2}
SMALL_REPLICATED = [n for n in WEIGHT_NAMES if n not in BIG and n not in SMALL_SHARDED]


def _rows(n_elems, mult):
    r = -(-n_elems // LANES)
    return -(-r // mult) * mult


def _flat(parts, rows):
    flat = jnp.concatenate([a.reshape(-1) for a in parts])
    return jnp.pad(flat, (0, rows * LANES - flat.shape[0])).reshape(rows, LANES)


def _unflat(flat, shapes):
    flat = flat.reshape(-1)
    out, off = [], 0
    for s in shapes:
        n = math.prod(s)
        out.append(flat[off:off + n].reshape(s))
        off += n
    return out


def _join_shards(a, axis):
    a = jnp.moveaxis(a, 0, axis)
    return a.reshape(a.shape[:axis] + (a.shape[axis] * a.shape[axis + 1],) + a.shape[axis + 2:])


def kernel(x, positions, ab_norm, ab_w_in, ab_q_norm, ab_w_q_b, ab_kv_norm, ab_w_kv_b, ab_conv_w, ab_conv_b, ab_w_rg_a, ab_b_rg_a, ab_w_rg_x, ab_b_rg_x, ab_lambda, ab_w_out, c_norm, c_w_in, c_ln_g, c_ln_b, c_w_s, c_b_s, c_w_out, ffn_norm, ffn_w_gate, ffn_w_up, ffn_conv_w, ffn_conv_b, ffn_w_down, final_norm, loss_target, m_ab_norm, m_ab_w_in, m_ab_q_norm, m_ab_w_q_b, m_ab_kv_norm, m_ab_w_kv_b, m_ab_conv_w, m_ab_conv_b, m_ab_w_rg_a, m_ab_b_rg_a, m_ab_w_rg_x, m_ab_b_rg_x, m_ab_lambda, m_ab_w_out, m_c_norm, m_c_w_in, m_c_ln_g, m_c_ln_b, m_c_w_s, m_c_b_s, m_c_w_out, m_ffn_norm, m_ffn_w_gate, m_ffn_w_up, m_ffn_conv_w, m_ffn_conv_b, m_ffn_w_down, m_final_norm, v_ab_norm, v_ab_w_in, v_ab_q_norm, v_ab_w_q_b, v_ab_kv_norm, v_ab_w_kv_b, v_ab_conv_w, v_ab_conv_b, v_ab_w_rg_a, v_ab_b_rg_a, v_ab_w_rg_x, v_ab_b_rg_x, v_ab_lambda, v_ab_w_out, v_c_norm, v_c_w_in, v_c_ln_g, v_c_ln_b, v_c_w_s, v_c_b_s, v_c_w_out, v_ffn_norm, v_ffn_w_gate, v_ffn_w_up, v_ffn_conv_w, v_ffn_conv_b, v_ffn_w_down, v_final_norm):
    given = dict(locals())
    w = {n: given[n] for n in WEIGHT_NAMES}
    m = {n: given["m_" + n] for n in WEIGHT_NAMES}
    v = {n: given["v_" + n] for n in WEIGHT_NAMES}
    c = lax.axis_index("c")
    chip = 2 * lax.axis_index("x") + lax.axis_index("y")

    halves = lambda a: a.reshape(a.shape[0], 2, a.shape[1] // 2, a.shape[2])
    tr = lambda a: jnp.swapaxes(a, 1, 2)
    send = {"ab_w_in": w["ab_w_in"], "ab_w_q_b": w["ab_w_q_b"], "ab_w_kv_b": w["ab_w_kv_b"], "ab_w_out": w["ab_w_out"],
            "c_w_in": tr(w["c_w_in"]), "c_w_out": w["c_w_out"], "ffn_w_gate": tr(w["ffn_w_gate"]),
            "ffn_w_up": tr(w["ffn_w_up"]), "ffn_w_down": w["ffn_w_down"]}
    small_rows = _rows(sum(w[n].size for n in SMALL_SHARDED), 16)
    small_sh = _flat([w[n] for n in SMALL_SHARDED], small_rows).reshape(1, 2, small_rows // 2, LANES)
    first_names = ["ab_w_in", "ab_w_q_b", "ab_w_kv_b", "ab_w_out"]
    mine = {n: halves(send[n].astype(BF16)) for n in BIG}

    def put_own(own, arrived):
        a = _put(arrived, own, chip, 1)
        return a.reshape(a.shape[0], -1, a.shape[-1])

    first = [mine[n] for n in first_names] + [small_sh]
    got = _all_gather(first).run("gather_first")
    full = {n: put_own(o, a) for n, o, a in zip(first_names + ["small"], first, got)}
    unshard = lambda a: jnp.swapaxes(a.reshape(N_CHIPS, -1, a.shape[-1]), 0, 1).reshape(-1, N_CHIPS * a.shape[-1])
    p = _prep_big(unshard(full["ab_w_in"][0]), unshard(full["ab_w_q_b"][0]), unshard(full["ab_w_kv_b"][0]))
    p["ab_w_out"] = full["ab_w_out"][0]
    small_full = dict(w)
    off = 0
    small_got = full["small"].reshape(N_CHIPS, -1)
    for n, ax in SMALL_SHARDED.items():
        seg = small_got[:, off:off + w[n].size].reshape((N_CHIPS,) + w[n].shape)
        small_full[n] = _join_shards(seg, ax)
        off += w[n].size
    p.update(_prep_small(small_full))

    for n in ("ffn_gate_t", "ffn_up_t", "ffn_down"):
        p[n] = {}

    def weights_ride(parts):
        def sink(arrived):
            for (own, setter), a in zip(parts, arrived):
                setter(put_own(own, a)[0])
        return _all_gather([own for own, _ in parts]), sink

    ffn_keys = {"ffn_gate_t": "ffn_w_gate", "ffn_up_t": "ffn_w_up", "ffn_down": "ffn_w_down"}
    ffn_part = lambda key, l: (mine[ffn_keys[key]][l:l + 1], functools.partial(p[key].__setitem__, l))
    rides = {
        "attn_fwd": weights_ride([ffn_part(key, 0) for key in ffn_keys]),
        "ffn0_gate": weights_ride([ffn_part("ffn_gate_t", 1)]),
        "ffn0_up": weights_ride([ffn_part("ffn_up_t", 1)]),
        "ffn0_act": weights_ride([ffn_part("ffn_down", 1)]),
        "ffn0_down": weights_ride([(mine["c_w_in"], functools.partial(p.__setitem__, "c_w_in_t")),
                                   (mine["c_w_out"], functools.partial(p.__setitem__, "c_w_out"))]),
    }

    def pair_sums(sharded, tag):
        sharded = [a.reshape(N_CHIPS, 2, -1, a.shape[-1]) for a in sharded]
        from_sib = _pair_swap(sharded).run(f"grad_pair_swap_{tag}")
        own = [lax.dynamic_index_in_dim(a, c, axis=1, keepdims=False) for a in sharded]
        return [_add2(a, b, out_dtype=BF16, name=f"grad_pair_add_{tag}{i}") for i, (a, b) in enumerate(zip(own, from_sib))]

    def chip_sums(pair, arrived, tag):
        own = [lax.dynamic_index_in_dim(a, chip, axis=0, keepdims=False) for a in pair]
        return [_sum_slots(_put(a, o, chip, 0), name=f"grad_chip_sum_{tag}{i}") for i, (a, o) in enumerate(zip(arrived, own))]

    half_of = {}

    def grad_rides(tag, named, hosts):
        pair = dict(zip(named, pair_sums(list(named.values()), tag)))
        out = {}
        for kernel_name, keys in hosts.items():
            def sink(arrived, keys=keys, kernel_name=kernel_name):
                half_of.update(zip(keys, chip_sums([pair[k] for k in keys], arrived, f"{tag}_{kernel_name}")))
            out[kernel_name] = (_chip_exchange([pair[k] for k in keys], scatter=True), sink)
        return out

    def grads_ready(layer, ready):
        if layer == 1:
            return grad_rides("f1", {"gate1": ready["ffn_gate_t"], "up1": ready["ffn_up_t"], "down1": ready["ffn_down"]},
                              {"sgu_bwd": ["down1"], "ffn0_dact": ["up1"], "ffn0_dactbwd": ["gate1"]})
        return grad_rides("f0", {"c_in": ready["c_w_in_t"], "c_out": ready["c_w_out"], "gate0": ready["ffn_gate_t"],
                                 "up0": ready["ffn_up_t"], "down0": ready["ffn_down"]},
                          {"attn_dq": ["c_in", "c_out", "down0"], "attn_dkv": ["gate0", "up0"]})

    loss_row, grad_x, g = _local_step(x, positions, loss_target, p, rides, grads_ready)

    cols = lambda a, n: jnp.swapaxes(a.reshape(a.shape[0], N_CHIPS, n), 0, 1)
    n_in, n_q, n_kv = w["ab_w_in"].shape[2], w["ab_w_q_b"].shape[2], w["ab_w_kv_b"].shape[2]
    pair = pair_sums([cols(_unperm_w_in(g["w_in_p"]), n_in), cols(_from_head_blocks(g["w_q_p"], QK_NOPE + QK_ROPE), n_q),
                      cols(_join_kv(g["w_k_p"], g["w_v_p"]), n_kv), g["ab_w_out"]], "b")
    half_of.update(zip(["in", "q", "kv", "out"],
                       chip_sums(pair, _chip_exchange(pair, scatter=True).run("grad_chip_exchange_b"), "b")))
    half = [half_of[k] for k in ("in", "q", "kv", "out", "c_in", "c_out", "gate0", "gate1", "up0", "up1", "down0", "down1")]
    slot = (jnp.arange(2) == c)[:, None, None]
    summed = [jnp.where(slot, a[None], b[None]).reshape(-1, a.shape[-1])
              for a, b in zip(half, _pair_send(half).run("grad_pair_share"))]
    s_in, s_q, s_kv, s_out, s_cin, s_cout, g0, g1, u0, u1, d0, d1 = summed
    grads = {"ab_w_in": s_in[None], "ab_w_q_b": s_q[None], "ab_w_kv_b": s_kv[None], "ab_w_out": s_out[None],
             "c_w_in": s_cin.T[None], "c_w_out": s_cout[None], "ffn_w_gate": jnp.stack([g0.T, g1.T]),
             "ffn_w_up": jnp.stack([u0.T, u1.T]), "ffn_w_down": jnp.stack([d0, d1])}

    small_names = SMALL_REPLICATED + list(SMALL_SHARDED)
    rs = _rows(sum(g[n].size for n in small_names) + LANES, FLAT_ROWS)
    small = _flat([loss_row] + [g[n] for n in small_names], rs)
    from_sib, = _pair_send([small]).run("small_pair_share")
    pair_small = _sum_slots(jnp.where(slot, small[None], from_sib[None]), name="small_pair_sum")
    all_small, = _chip_exchange([pair_small], scatter=False).run("small_chip_exchange")
    small_sum = _sum_slots(_put(all_small, pair_small, chip, 0), name="small_chip_sum")
    small_parts = _unflat(small_sum, [(1, LANES)] + [g[n].shape for n in small_names])
    loss = small_parts[0][0, 0]
    for n, a in zip(small_names, small_parts[1:]):
        if n in SMALL_SHARDED:
            ax = SMALL_SHARDED[n]
            a = lax.dynamic_slice_in_dim(a, chip * w[n].shape[ax], w[n].shape[ax], axis=ax)
        grads[n] = a.reshape(w[n].shape)

    delta, new_m, new_v = {}, {}, {}
    for n in BIG:
        delta[n], new_m[n], new_v[n] = _adamw(w[n], grads[n], m[n], v[n], name=f"adamw_{n}")
    small_all = [n for n in WEIGHT_NAMES if n not in BIG]
    ra = _rows(sum(w[n].size for n in small_all), FLAT_ROWS)
    pack = lambda d: _flat([d[n] for n in small_all], ra)[None]
    out = _adamw(pack(w), pack(grads), pack(m), pack(v), name="adamw_small")
    shapes = [w[n].shape for n in small_all]
    for d, flat in zip((delta, new_m, new_v), out):
        d.update(zip(small_all, _unflat(flat, shapes)))
    return (loss, grad_x, *[grads[n] for n in WEIGHT_NAMES], *[delta[n] for n in WEIGHT_NAMES],
            *[new_m[n] for n in WEIGHT_NAMES], *[new_v[n] for n in WEIGHT_NAMES])
```

```python
import functools
import math

import jax
import jax.numpy as jnp
from jax import lax
from jax.experimental import pallas as pl
from jax.experimental.pallas import tpu as pltpu

F32 = jnp.float32
BF16 = jnp.bfloat16
MESH = pl.DeviceIdType.MESH

D_MODEL = 1024
MLA_HEADS = 8
Q_LORA = 256
KV_LORA = 128
QK_NOPE = 64
QK_ROPE = 32
V_HEAD = 64
LRU_WIDTH = 512
LRU_HEADS = 8
LRU_BLOCK = 64
LRU_CONV = 4
LRU_C = 8.0
CHUNK = 128
SGU_GROUPS = 8
SGU_WIDTH = 1024
D_FF = 2816
FFN_CONV = 3
NORM_EPS = 1e-6
ROPE_BASE = 10000.0
AB_IN_PAD = 1536
ADAM_LR = 0.001
ADAM_B1 = 0.9
ADAM_B2 = 0.999
ADAM_EPS = 1e-08
ADAM_WD = 0.01
ADAM_STEP = 10

N_CHIPS = 4
LANES = 128
VMEM_LIMIT = 56 * 1024 * 1024
ROW_TILE = 256
MM_TM, MM_TN, MM_TK = 512, 1536, 2816
MM_TM_T, MM_TK_T = 1408, 1024
GELU_C = math.sqrt(2.0 / math.pi)


def _cparams(sem):
    return pltpu.CompilerParams(dimension_semantics=sem, vmem_limit_bytes=VMEM_LIMIT)


def _tile(n, target, mult=LANES):
    t = (min(n, target) // mult) * mult
    while t >= mult:
        if n % t == 0:
            return t
        t -= mult
    return n


def _gelu(x):
    t = jnp.tanh(GELU_C * (x + 0.044715 * x * x * x))
    return 0.5 * x * (1.0 + t)


def _gelu_and_grad(x):
    x2 = x * x
    t = jnp.tanh(GELU_C * (x + 0.044715 * x * x2))
    g = 0.5 * x * (1.0 + t)
    dg = 0.5 * (1.0 + t) + 0.5 * x * (1.0 - t * t) * GELU_C * (1.0 + 3.0 * 0.044715 * x2)
    return g, dg


def _sigmoid(x):
    return 1.0 / (1.0 + jnp.exp(-x))


def _shift_rows(x, d, fill_rows):
    ext = jnp.concatenate([fill_rows, x], axis=0)
    return pltpu.roll(ext, d, 0)[8:]


def _shift_rows_up(x, d, fill_rows):
    n = x.shape[0]
    ext = jnp.concatenate([x, fill_rows], axis=0)
    return pltpu.roll(ext, n + 8 - d, 0)[:n]


def _dot(a, b, dims):
    return lax.dot_general(a.astype(BF16), b.astype(BF16), (dims, ((), ())), preferred_element_type=F32)


def _dot_nn(a, b):
    return _dot(a, b, ((1,), (0,)))


def _dot_nt(a, b):
    return _dot(a, b, ((1,), (1,)))


def _dot_tn(a, b):
    return _dot(a, b, ((0,), (0,)))


def _mm(a, b, *, name, ta=False, tb=False, res=None, out_dtype=F32, ride=None):
    if ta:
        K, M = a.shape
    else:
        M, K = a.shape
    N = b.shape[0] if tb else b.shape[1]
    tm = _tile(M, MM_TM_T if ta else MM_TM, LANES if ta else 8)
    tn = _tile(N, MM_TN, LANES)
    tk = _tile(K, MM_TK_T if ta else MM_TK, LANES)
    nk = K // tk
    a_spec = pl.BlockSpec((tk, tm), lambda j, i, k: (k, i)) if ta else pl.BlockSpec((tm, tk), lambda j, i, k: (i, k))
    b_spec = pl.BlockSpec((tn, tk), lambda j, i, k: (j, k)) if tb else pl.BlockSpec((tk, tn), lambda j, i, k: (k, j))
    o_spec = pl.BlockSpec((tm, tn), lambda j, i, k: (i, j))
    dims = ((0,) if ta else (1,), (1,) if tb else (0,))
    has_res = res is not None

    def body(*refs):
        a_ref, b_ref = refs[:2]
        r_ref = refs[2] if has_res else None
        o_ref = refs[3] if has_res else refs[2]
        p = _dot(a_ref[...], b_ref[...], dims)

        def finish(r):
            if has_res:
                r = r + r_ref[...].astype(F32)
            o_ref[...] = r.astype(out_dtype)

        if nk == 1:
            finish(p)
            return
        acc_ref = refs[-1]
        k = pl.program_id(2)

        @pl.when(k == 0)
        def _():
            acc_ref[...] = p

        @pl.when(jnp.logical_and(k > 0, k < nk - 1))
        def _():
            acc_ref[...] += p

        @pl.when(k == nk - 1)
        def _():
            finish(acc_ref[...] + p)

    in_specs = [a_spec, b_spec] + ([o_spec] if has_res else [])
    args = (a, b) + ((res,) if has_res else ())
    return _pcall(
        body, name=name, grid=(N // tn, M // tm, nk), in_specs=in_specs, out_specs=[o_spec],
        out_shape=[jax.ShapeDtypeStruct((M, N), out_dtype)], args=args,
        scratch=[pltpu.VMEM((tm, tn), F32)] if nk > 1 else [], sem=("parallel", "parallel", "arbitrary"), ride=ride)[0]


def _rms_fwd(x, g, *, name, cb=0, out_dtype=BF16):
    T = x.shape[0]
    W = g.shape[-1]
    g = g.reshape(1, W)
    tt = ROW_TILE

    def body(x_ref, g_ref, o_ref):
        xf = x_ref[...].astype(F32)
        rstd = lax.rsqrt(jnp.mean(xf * xf, axis=-1, keepdims=True) + NORM_EPS)
        o_ref[...] = (xf * rstd * g_ref[...]).astype(out_dtype)

    return pl.pallas_call(
        body, name=name, grid=(T // tt,),
        in_specs=[pl.BlockSpec((tt, W), lambda i: (i, cb)), pl.BlockSpec((1, W), lambda i: (0, 0))],
        out_specs=pl.BlockSpec((tt, W), lambda i: (i, 0)),
        out_shape=jax.ShapeDtypeStruct((T, W), out_dtype),
        compiler_params=_cparams(("parallel",)),
    )(x, g)


def _rms_bwd(x, g, dy, *, name, cb=0, res=None, out_dtype=F32):
    T = x.shape[0]
    W = g.shape[-1]
    g = g.reshape(1, W)
    tt = ROW_TILE
    has_res = res is not None

    def body(*refs):
        if has_res:
            x_ref, g_ref, dy_ref, r_ref, dx_ref, dg_ref = refs
        else:
            x_ref, g_ref, dy_ref, dx_ref, dg_ref = refs
        xf = x_ref[...].astype(F32)
        dyf = dy_ref[...].astype(F32)
        rstd = lax.rsqrt(jnp.mean(xf * xf, axis=-1, keepdims=True) + NORM_EPS)
        xhat = xf * rstd
        dxhat = dyf * g_ref[...]
        dx = rstd * (dxhat - xhat * jnp.mean(dxhat * xhat, axis=-1, keepdims=True))
        if has_res:
            dx = dx + r_ref[...].astype(F32)
        dx_ref[...] = dx.astype(out_dtype)
        part = jnp.sum(dyf * xhat, axis=0, keepdims=True)

        @pl.when(pl.program_id(0) == 0)
        def _():
            dg_ref[...] = part

        @pl.when(pl.program_id(0) > 0)
        def _():
            dg_ref[...] += part

    row = pl.BlockSpec((tt, W), lambda i: (i, 0))
    in_specs = [pl.BlockSpec((tt, W), lambda i: (i, cb)), pl.BlockSpec((1, W), lambda i: (0, 0)), row]
    args = (x, g, dy)
    if has_res:
        in_specs.append(row)
        args = args + (res,)
    return pl.pallas_call(
        body, name=name, grid=(T // tt,), in_specs=in_specs,
        out_specs=[row, pl.BlockSpec((1, W), lambda i: (0, 0))],
        out_shape=[jax.ShapeDtypeStruct((T, W), out_dtype), jax.ShapeDtypeStruct((1, W), F32)],
        compiler_params=_cparams(("arbitrary",)),
    )(*args)


def _final_fwd_bwd(h, g, target, *, name):
    T, W = h.shape
    g = g.reshape(1, W)
    tt = ROW_TILE

    def body(x_ref, g_ref, t_ref, loss_ref, dx_ref, dg_ref):
        xf = x_ref[...]
        rstd = lax.rsqrt(jnp.mean(xf * xf, axis=-1, keepdims=True) + NORM_EPS)
        xhat = xf * rstd
        err = xhat * g_ref[...] - t_ref[...]
        lpart = jnp.zeros((1, LANES), F32) + (0.5 / W) * jnp.sum(err * err)
        dyf = err * (1.0 / W)
        dxhat = dyf * g_ref[...]
        dx_ref[...] = rstd * (dxhat - xhat * jnp.mean(dxhat * xhat, axis=-1, keepdims=True))
        part = jnp.sum(dyf * xhat, axis=0, keepdims=True)

        @pl.when(pl.program_id(0) == 0)
        def _():
            dg_ref[...] = part
            loss_ref[...] = lpart

        @pl.when(pl.program_id(0) > 0)
        def _():
            dg_ref[...] += part
            loss_ref[...] += lpart

    row = pl.BlockSpec((tt, W), lambda i: (i, 0))
    return pl.pallas_call(
        body, name=name, grid=(T // tt,),
        in_specs=[row, pl.BlockSpec((1, W), lambda i: (0, 0)), row],
        out_specs=[pl.BlockSpec((1, LANES), lambda i: (0, 0)), row, pl.BlockSpec((1, W), lambda i: (0, 0))],
        out_shape=[jax.ShapeDtypeStruct((1, LANES), F32), jax.ShapeDtypeStruct((T, W), F32),
                   jax.ShapeDtypeStruct((1, W), F32)],
        compiler_params=_cparams(("arbitrary",)),
    )(h, g, target)


def _swap16(x):
    lane = lax.broadcasted_iota(jnp.int32, x.shape, 1)
    return jnp.where((lane % 32) < 16, pltpu.roll(x, LANES - 16, 1), pltpu.roll(x, 16, 1))


def _rope(x, c, s):
    return x * c + _swap16(x) * s


def _rope_t(d, c, s):
    return d * c + _swap16(d * s)


def _head_block_map(fn, x, cos, sin, *, name):
    T, W = x.shape
    tt = ROW_TILE

    def body(x_ref, c_ref, s_ref, o_ref):
        c, s = c_ref[...], s_ref[...]
        for h in range(W // LANES):
            lanes = slice(h * LANES, (h + 1) * LANES)
            o_ref[:, lanes] = fn(x_ref[:, lanes], c, s).astype(BF16)

    tab = pl.BlockSpec((tt, LANES), lambda i: (i, 0))
    blk = pl.BlockSpec((tt, W), lambda i: (i, 0))
    return pl.pallas_call(
        body, name=name, grid=(T // tt,), in_specs=[blk, tab, tab], out_specs=blk,
        out_shape=jax.ShapeDtypeStruct((T, W), BF16), compiler_params=_cparams(("parallel",)),
    )(x, cos, sin)


def _rope_q(q, cos, sin, *, name):
    scale = _attn_scale()
    return _head_block_map(lambda x, c, s: _rope(x, c, s) * scale, q, cos, sin, name=name)


def _rope_q_bwd(dq, cos, sin, *, name):
    return _head_block_map(_rope_t, dq, cos, sin, name=name)


def _key_blocks(kv, z, cos, sin, *, kpe_block, name):
    T = kv.shape[0]
    tt = ROW_TILE
    W = MLA_HEADS * LANES

    def body(kv_ref, z_ref, c_ref, s_ref, o_ref):
        kr = _rope(z_ref[...], c_ref[...], s_ref[...])
        for h in range(MLA_HEADS):
            lanes = slice(h * LANES, (h + 1) * LANES)
            o_ref[:, lanes] = (kv_ref[:, lanes].astype(F32) + kr).astype(BF16)

    tab = pl.BlockSpec((tt, LANES), lambda i: (i, 0))
    blk = pl.BlockSpec((tt, W), lambda i: (i, 0))
    return pl.pallas_call(
        body, name=name, grid=(T // tt,),
        in_specs=[blk, pl.BlockSpec((tt, LANES), lambda i: (i, kpe_block)), tab, tab], out_specs=blk,
        out_shape=jax.ShapeDtypeStruct((T, W), BF16), compiler_params=_cparams(("parallel",)),
    )(kv, z, cos, sin)


def _key_rope_bwd(dk, cos, sin, *, name):
    T = dk.shape[0]
    tt = ROW_TILE

    def body(d_ref, c_ref, s_ref, o_ref):
        d = d_ref[:, :LANES]
        for h in range(1, MLA_HEADS):
            d = d + d_ref[:, h * LANES:(h + 1) * LANES]
        lane = lax.broadcasted_iota(jnp.int32, d.shape, 1)
        d = jnp.where(jnp.logical_and(lane >= QK_NOPE, lane < QK_NOPE + QK_ROPE), d, 0.0)
        o_ref[...] = _rope_t(d, c_ref[...], s_ref[...]).astype(BF16)

    tab = pl.BlockSpec((tt, LANES), lambda i: (i, 0))
    return pl.pallas_call(
        body, name=name, grid=(T // tt,),
        in_specs=[pl.BlockSpec((tt, MLA_HEADS * LANES), lambda i: (i, 0)), tab, tab], out_specs=tab,
        out_shape=jax.ShapeDtypeStruct((T, LANES), BF16), compiler_params=_cparams(("parallel",)),
    )(dk, cos, sin)


ATT_BLOCK = 512


def _attn_scale():
    return float((QK_NOPE + QK_ROPE) ** -0.5)


def _causal_mask(qi, kj, tq, tk):
    row = qi * tq + lax.broadcasted_iota(jnp.int32, (tq, tk), 0)
    col = kj * tk + lax.broadcasted_iota(jnp.int32, (tq, tk), 1)
    return col <= row


def _pcall(body, *, name, grid, in_specs, out_specs, out_shape, args, scratch=(), sem=None, ride=None):
    n_in, n_out, n_scr = len(args), len(out_shape), len(scratch)
    if ride is None:
        return pl.pallas_call(
            body, name=name, grid=grid, in_specs=list(in_specs), out_specs=list(out_specs), out_shape=list(out_shape),
            scratch_shapes=list(scratch), compiler_params=_cparams(sem or ("arbitrary",) * len(grid)))(*args)
    ex, sink = ride
    o0 = n_in + len(ex.arrs)
    s0 = o0 + n_out + len(ex.out_shapes)

    def hosted(*refs):
        parts = (refs[n_in:o0], refs[o0 + n_out:s0], refs[-2], refs[-1])
        ids = [pl.program_id(i) for i in range(len(grid))]
        pl.when(functools.reduce(jnp.logical_and, [i == 0 for i in ids]))(lambda: ex.start(*parts))
        body(*refs[:n_in], *refs[o0:o0 + n_out], *refs[s0:s0 + n_scr])
        pl.when(functools.reduce(jnp.logical_and, [i == n - 1 for i, n in zip(ids, grid)]))(lambda: ex.finish(*parts))

    outs = pl.pallas_call(
        hosted, name=name, grid=grid, in_specs=list(in_specs) + ex.in_specs, out_specs=list(out_specs) + ex.out_specs,
        out_shape=list(out_shape) + ex.out_shapes, scratch_shapes=list(scratch) + ex.scratch,
        compiler_params=_cparams(("arbitrary",) * len(grid)))(*args, *ex.arrs)
    sink(outs[n_out:])
    return outs[:n_out]


PAIRS = MLA_HEADS // 2


def _own_lanes(x, first):
    lane = lax.broadcasted_iota(jnp.int32, x.shape, 1)
    return jnp.where((lane < V_HEAD) if first else (lane >= V_HEAD), x, 0.0)


def _attn_fwd(q, k, kv, *, B, S, v_block0, name, ride=None):
    tq = tk = min(ATT_BLOCK, S)
    nq = S // tq
    T = B * S

    def body(q_ref, k_ref, v_ref, o_ref, lse_ref):
        qi = pl.program_id(2)
        qs = (q_ref[:, :LANES], q_ref[:, LANES:])

        def step(masked):
            def f(j, carry):
                rows = pl.ds(pl.multiple_of(j * tk, tk), tk)
                vb = v_ref[rows, :]
                out = []
                for h in range(2):
                    m, l, acc = carry[h]
                    s = _dot_nt(qs[h], k_ref[rows, h * LANES:(h + 1) * LANES])
                    if masked:
                        s = jnp.where(_causal_mask(qi, j, tq, tk), s, -jnp.inf)
                    m_new = jnp.maximum(m, jnp.max(s, axis=-1, keepdims=True))
                    alpha = jnp.exp(m - m_new)
                    p = jnp.exp(s - m_new)
                    out.append((m_new, alpha * l + jnp.sum(p, axis=-1, keepdims=True), alpha * acc + _dot_nn(p, vb)))
                return tuple(out)
            return f

        one = (jnp.full((tq, 1), -1e30, F32), jnp.zeros((tq, 1), F32), jnp.zeros((tq, LANES), F32))
        (ma, la, acca), (mb, lb, accb) = step(True)(qi, lax.fori_loop(0, qi, step(False), (one, one)))
        o_ref[...] = _own_lanes(acca / la, True) + _own_lanes(accb / lb, False)
        lse_ref[0, 0] = ma + jnp.log(la)
        lse_ref[0, 1] = mb + jnp.log(lb)

    return _pcall(
        body, name=name, grid=(B, PAIRS, nq),
        in_specs=[pl.BlockSpec((tq, 2 * LANES), lambda b, g, i: (b * nq + i, g)),
                  pl.BlockSpec((S, 2 * LANES), lambda b, g, i: (b, g)),
                  pl.BlockSpec((S, LANES), lambda b, g, i: (b, v_block0 + g))],
        out_specs=[pl.BlockSpec((tq, LANES), lambda b, g, i: (b * nq + i, g)),
                   pl.BlockSpec((1, 2, tq, 1), lambda b, g, i: (b, g, i, 0))],
        out_shape=[jax.ShapeDtypeStruct((T, PAIRS * LANES), F32), jax.ShapeDtypeStruct((B, MLA_HEADS, S, 1), F32)],
        args=(q, k, kv), ride=ride)


def _attn_dq(q, k, kv, o, lse, do, *, B, S, v_block0, name, ride=None):
    tq = tk = min(ATT_BLOCK, S)
    nq = S // tq
    T = B * S
    scale = _attn_scale()

    def body(q_ref, k_ref, v_ref, o_ref, lse_ref, do_ref, dq_ref, delta_ref):
        qi = pl.program_id(2)
        qs = (q_ref[:, :LANES], q_ref[:, LANES:])
        dos = (_own_lanes(do_ref[...], True), _own_lanes(do_ref[...], False))
        deltas = tuple(jnp.sum(d * o_ref[...], axis=-1, keepdims=True) for d in dos)
        lses = (lse_ref[0, 0], lse_ref[0, 1])

        def step(masked):
            def f(j, carry):
                rows = pl.ds(pl.multiple_of(j * tk, tk), tk)
                vb = v_ref[rows, :]
                out = []
                for h in range(2):
                    kb = k_ref[rows, h * LANES:(h + 1) * LANES]
                    p = jnp.exp(_dot_nt(qs[h], kb) - lses[h])
                    if masked:
                        p = jnp.where(_causal_mask(qi, j, tq, tk), p, 0.0)
                    ds = p * (_dot_nt(dos[h], vb) - deltas[h])
                    out.append(carry[h] + _dot_nn(ds, kb))
                return tuple(out)
            return f

        zero = jnp.zeros((tq, LANES), F32)
        dqa, dqb = step(True)(qi, lax.fori_loop(0, qi, step(False), (zero, zero)))
        dq_ref[:, :LANES] = dqa * scale
        dq_ref[:, LANES:] = dqb * scale
        delta_ref[0, 0] = deltas[0]
        delta_ref[0, 1] = deltas[1]

    qrow = lambda w: pl.BlockSpec((tq, w), lambda b, g, i: (b * nq + i, g))
    stat = pl.BlockSpec((1, 2, tq, 1), lambda b, g, i: (b, g, i, 0))
    return _pcall(
        body, name=name, grid=(B, PAIRS, nq),
        in_specs=[qrow(2 * LANES), pl.BlockSpec((S, 2 * LANES), lambda b, g, i: (b, g)),
                  pl.BlockSpec((S, LANES), lambda b, g, i: (b, v_block0 + g)), qrow(LANES), stat, qrow(LANES)],
        out_specs=[qrow(2 * LANES), stat],
        out_shape=[jax.ShapeDtypeStruct((T, MLA_HEADS * LANES), F32), jax.ShapeDtypeStruct((B, MLA_HEADS, S, 1), F32)],
        args=(q, k, kv, o, lse, do), sem=("parallel", "parallel", "parallel"), ride=ride)


def _attn_dkv(q, k, kv, lse_rows, delta_rows, do, *, B, S, v_block0, name, ride=None):
    tq = tk = min(ATT_BLOCK, S)
    nq = S // tq
    T = B * S

    def body(q_ref, k_ref, v_ref, lse_ref, delta_ref, do_ref, dk_ref, dv_ref):
        kj = pl.program_id(2)
        ks = (k_ref[:, :LANES], k_ref[:, LANES:])
        vb = v_ref[...]

        def step(masked):
            def f(i, carry):
                rows = pl.ds(pl.multiple_of(i * tq, tq), tq)
                do_b = do_ref[rows, :]
                dks, dv = list(carry[:2]), carry[2]
                for h in range(2):
                    qb = q_ref[rows, h * LANES:(h + 1) * LANES]
                    doh = _own_lanes(do_b, h == 0)
                    pt = jnp.exp(_dot_nt(ks[h], qb) - lse_ref[0, h, pl.ds(i, 1), :])
                    if masked:
                        krow = kj * tk + lax.broadcasted_iota(jnp.int32, (tk, tq), 0)
                        qcol = i * tq + lax.broadcasted_iota(jnp.int32, (tk, tq), 1)
                        pt = jnp.where(krow <= qcol, pt, 0.0)
                    dst = pt * (_dot_nt(vb, doh) - delta_ref[0, h, pl.ds(i, 1), :])
                    dks[h] = dks[h] + _dot_nn(dst, qb)
                    dv = dv + _dot_nn(pt, doh)
                return dks[0], dks[1], dv
            return f

        zero = jnp.zeros((tk, LANES), F32)
        dka, dkb, dv = lax.fori_loop(kj + 1, nq, step(False), step(True)(kj, (zero, zero, zero)))
        dk_ref[:, :LANES] = dka
        dk_ref[:, LANES:] = dkb
        dv_ref[...] = dv

    krow = lambda w, c0: pl.BlockSpec((tk, w), lambda b, g, j: (b * nq + j, c0 + g))
    seq = lambda w: pl.BlockSpec((S, w), lambda b, g, j: (b, g))
    stat = pl.BlockSpec((1, 2, nq, tq), lambda b, g, j: (b, g, 0, 0))
    return _pcall(
        body, name=name, grid=(B, PAIRS, nq),
        in_specs=[seq(2 * LANES), krow(2 * LANES, 0), krow(LANES, v_block0), stat, stat, seq(LANES)],
        out_specs=[krow(2 * LANES, 0), krow(LANES, 0)],
        out_shape=[jax.ShapeDtypeStruct((T, MLA_HEADS * LANES), F32), jax.ShapeDtypeStruct((T, PAIRS * LANES), F32)],
        args=(q, k, kv, lse_rows, delta_rows, do), ride=ride)


def _lru_gates(xl, halo, cw_ref, cb_ref, wa_ref, ba_ref, wx_ref, bx_ref, lam_ref):
    xc = cb_ref[...] + cw_ref[3:4, :] * xl
    for kk in range(LRU_CONV - 1):
        xc = xc + cw_ref[kk:kk + 1, :] * _shift_rows(xl, LRU_CONV - 1 - kk, halo)
    r = _sigmoid(_dot_nn(xc, wa_ref[...]) + ba_ref[...])
    i = _sigmoid(_dot_nn(xc, wx_ref[...]) + bx_ref[...])
    lam = lam_ref[...]
    sp = jnp.maximum(-lam, 0.0) + jnp.log(1.0 + jnp.exp(-jnp.abs(lam)))
    a = jnp.exp(-LRU_C * r * sp)
    mult = jnp.sqrt(1.0 - a * a)
    return xc, r, i, sp, a, mult


def _lru_specs(tt, nt, S):
    def make(rev):
        tmap = (lambda t: nt - 1 - t) if rev else (lambda t: t)
        tile = lambda cb: pl.BlockSpec((tt, LRU_WIDTH), lambda b, t: (b * nt + tmap(t), cb))
        prev8 = lambda cb: pl.BlockSpec(
            (8, LRU_WIDTH), lambda b, t: (jnp.maximum((b * nt + tmap(t)) * (tt // 8) - 1, 0), cb))
        return tile, prev8, tmap
    return make


def _lru_fwd(z, cw, cb, wa, ba, wx, bx, lam, *, S, name):
    T = z.shape[0]
    tt = min(ROW_TILE, S)
    nt = S // tt
    tile, prev8, _ = _lru_specs(tt, nt, S)(False)
    vec = lambda r: pl.BlockSpec((r, LRU_WIDTH), lambda b, t: (0, 0))
    mat = pl.BlockSpec((LRU_WIDTH, LRU_WIDTH), lambda b, t: (0, 0))

    def body(xl_ref, halo_ref, gate_ref, cw_ref, cb_ref, wa_ref, ba_ref, wx_ref, bx_ref, lam_ref,
             y_ref, h_ref, carry_ref):
        t = pl.program_id(1)
        first = t == 0
        halo = jnp.where(first, 0.0, halo_ref[...])
        xl_t = xl_ref[...]
        xc, r, i, sp, a, mult = _lru_gates(xl_t, halo, cw_ref, cb_ref, wa_ref, ba_ref, wx_ref, bx_ref, lam_ref)
        bv = mult * (i * xc)
        ones = jnp.ones((8, LRU_WIDTH), F32)
        zeros = jnp.zeros((8, LRU_WIDTH), F32)
        row = lax.broadcasted_iota(jnp.int32, (tt, LRU_WIDTH), 0)
        A = a
        d = 1
        while d < tt:
            if d < 8:
                a_sh = _shift_rows(A, d, ones)
                b_sh = _shift_rows(bv, d, zeros)
            else:
                a_sh = jnp.where(row < d, 1.0, pltpu.roll(A, d, 0))
                b_sh = jnp.where(row < d, 0.0, pltpu.roll(bv, d, 0))
            bv = A * b_sh + bv
            A = A * a_sh
            d *= 2
        h0 = jnp.where(first, 0.0, carry_ref[0:1, :])
        h = A * h0 + bv
        carry_ref[...] = jnp.broadcast_to(h[tt - 1:tt, :], (8, LRU_WIDTH))
        h_ref[...] = h
        y_ref[...] = (h * _gelu(gate_ref[...])).astype(BF16)

    return pl.pallas_call(
        body, name=name, grid=(T // S, nt),
        in_specs=[tile(0), prev8(0), tile(1), vec(LRU_CONV), vec(1), mat, vec(1), mat, vec(1), vec(1)],
        out_specs=[tile(0), tile(0)],
        out_shape=[jax.ShapeDtypeStruct((T, LRU_WIDTH), BF16), jax.ShapeDtypeStruct((T, LRU_WIDTH), F32)],
        scratch_shapes=[pltpu.VMEM((8, LRU_WIDTH), F32)],
        compiler_params=_cparams(("arbitrary", "arbitrary")),
    )(z, z, z, cw, cb, wa, ba, wx, bx, lam)


def _lru_bwd(z, h, dy, cw, cb, wa, ba, wx, bx, lam, *, S, name):
    T = z.shape[0]
    tt = min(ROW_TILE, S)
    nt = S // tt
    tile, prev8, tmap = _lru_specs(tt, nt, S)(True)
    vec = lambda r: pl.BlockSpec((r, LRU_WIDTH), lambda b, t: (0, 0))
    mat = pl.BlockSpec((LRU_WIDTH, LRU_WIDTH), lambda b, t: (0, 0))

    def body(xl_ref, halo_ref, gate_ref, h_ref, hprev_ref, dy_ref, cw_ref, cb_ref, wa_ref, ba_ref, wx_ref,
             bx_ref, lam_ref, dxl_ref, dgate_ref, dcw_ref, dcb_ref, dwa_ref, dba_ref, dwx_ref, dbx_ref,
             dlam_ref, lamc_ref, ac_ref, dxc_ref):
        b = pl.program_id(0)
        t = pl.program_id(1)
        tr = nt - 1 - t
        seq_first = tr == 0
        seq_last = t == 0
        halo = jnp.where(seq_first, 0.0, halo_ref[...])
        xl_t = xl_ref[...]
        xc, r, i, sp, a, mult = _lru_gates(xl_t, halo, cw_ref, cb_ref, wa_ref, ba_ref, wx_ref, bx_ref, lam_ref)
        hh = h_ref[...]
        dyf = dy_ref[...].astype(F32)
        gl, dgl = _gelu_and_grad(gate_ref[...])
        dgate_ref[...] = (dyf * hh * dgl).astype(BF16)
        dh = dyf * gl

        a_first_later = jnp.where(seq_last, 0.0, ac_ref[...])
        lam_later = jnp.where(seq_last, 0.0, lamc_ref[...])
        row = lax.broadcasted_iota(jnp.int32, (tt, LRU_WIDTH), 0)
        A = _shift_rows_up(a, 1, a_first_later)
        lm = dh
        ones = jnp.ones((8, LRU_WIDTH), F32)
        zeros = jnp.zeros((8, LRU_WIDTH), F32)
        d = 1
        while d < tt:
            if d < 8:
                a_sh = _shift_rows_up(A, d, ones)
                l_sh = _shift_rows_up(lm, d, zeros)
            else:
                a_sh = jnp.where(row >= tt - d, 1.0, pltpu.roll(A, tt - d, 0))
                l_sh = jnp.where(row >= tt - d, 0.0, pltpu.roll(lm, tt - d, 0))
            lm = lm + A * l_sh
            A = A * a_sh
            d *= 2
        lm = lm + A * lam_later[0:1, :]
        lamc_ref[...] = jnp.broadcast_to(lm[0:1, :], (8, LRU_WIDTH))
        ac_ref[...] = jnp.broadcast_to(a[0:1, :], (8, LRU_WIDTH))

        hprev_halo = jnp.where(seq_first, 0.0, hprev_ref[...])
        h_prev = _shift_rows(hh, 1, hprev_halo)
        da = lm * h_prev
        ixc = i * xc
        dmult = lm * ixc
        di = lm * mult * xc
        dxc = lm * mult * i
        da = da - dmult * a / mult
        dlog = da * a
        dr = dlog * (-LRU_C) * sp
        dsp_part = jnp.sum(dlog * (-LRU_C) * r, axis=0, keepdims=True)
        dpa = dr * r * (1.0 - r)
        dpx = di * i * (1.0 - i)
        dxc = dxc + _dot_nt(dpa, wa_ref[...]) + _dot_nt(dpx, wx_ref[...])
        dwa_part = _dot_tn(xc, dpa)
        dwx_part = _dot_tn(xc, dpx)

        later = jnp.where(seq_last, 0.0, dxc_ref[...])
        dxl = cw_ref[3:4, :] * dxc
        for kk in range(LRU_CONV - 1):
            dxl = dxl + cw_ref[kk:kk + 1, :] * _shift_rows_up(dxc, LRU_CONV - 1 - kk, later)
        dxl_ref[...] = dxl.astype(BF16)
        dxc_ref[...] = dxc[0:8, :]
        dcw_rows = [jnp.sum(dxc * _shift_rows(xl_t, LRU_CONV - 1 - kk, halo), axis=0, keepdims=True)
                    for kk in range(LRU_CONV - 1)]
        dcw_rows.append(jnp.sum(dxc * xl_t, axis=0, keepdims=True))
        dcw_part = jnp.concatenate(dcw_rows + [jnp.zeros((8 - LRU_CONV, LRU_WIDTH), F32)], axis=0)
        lamv = lam_ref[...]
        dlam_part = dsp_part * (-_sigmoid(-lamv))
        parts = ((dcw_ref, dcw_part), (dcb_ref, jnp.sum(dxc, axis=0, keepdims=True)),
                 (dwa_ref, dwa_part), (dba_ref, jnp.sum(dpa, axis=0, keepdims=True)),
                 (dwx_ref, dwx_part), (dbx_ref, jnp.sum(dpx, axis=0, keepdims=True)),
                 (dlam_ref, dlam_part))
        start = jnp.logical_and(b == 0, t == 0)

        @pl.when(start)
        def _():
            for ref, val in parts:
                ref[...] = val

        @pl.when(jnp.logical_not(start))
        def _():
            for ref, val in parts:
                ref[...] += val

    acc = lambda r: pl.BlockSpec((r, LRU_WIDTH), lambda b, t: (0, 0))
    return pl.pallas_call(
        body, name=name, grid=(T // S, nt),
        in_specs=[tile(0), prev8(0), tile(1), tile(0), prev8(0), tile(0),
                  vec(LRU_CONV), vec(1), mat, vec(1), mat, vec(1), vec(1)],
        out_specs=[tile(0), tile(0), acc(8), acc(1), mat, acc(1), mat, acc(1), acc(1)],
        out_shape=[jax.ShapeDtypeStruct((T, LRU_WIDTH), BF16), jax.ShapeDtypeStruct((T, LRU_WIDTH), BF16),
                   jax.ShapeDtypeStruct((8, LRU_WIDTH), F32), jax.ShapeDtypeStruct((1, LRU_WIDTH), F32),
                   jax.ShapeDtypeStruct((LRU_WIDTH, LRU_WIDTH), F32), jax.ShapeDtypeStruct((1, LRU_WIDTH), F32),
                   jax.ShapeDtypeStruct((LRU_WIDTH, LRU_WIDTH), F32), jax.ShapeDtypeStruct((1, LRU_WIDTH), F32),
                   jax.ShapeDtypeStruct((1, LRU_WIDTH), F32)],
        scratch_shapes=[pltpu.VMEM((8, LRU_WIDTH), F32), pltpu.VMEM((8, LRU_WIDTH), F32),
                        pltpu.VMEM((8, LRU_WIDTH), F32)],
        compiler_params=_cparams(("arbitrary", "arbitrary")),
    )(z, z, z, h, h, dy, cw, cb, wa, ba, wx, bx, lam)


FFN_CT = 1408


def _ffn_conv(g, halo, cw_ref, cb_ref):
    gc = cb_ref[...] + cw_ref[2:3, :] * g
    for kk in range(FFN_CONV - 1):
        gc = gc + cw_ref[kk:kk + 1, :] * _shift_rows(g, FFN_CONV - 1 - kk, halo)
    return gc


def _ffn_act_fwd(g, u, cw, cb, *, S, name, ride=None):
    T, F = g.shape
    tt = min(ROW_TILE, S)
    nt = S // tt
    tc = _tile(F, FFN_CT)

    def body(g_ref, halo_ref, u_ref, cw_ref, cb_ref, o_ref):
        first = (pl.program_id(0) % nt) == 0
        halo = jnp.where(first, 0.0, halo_ref[...])
        gc = _ffn_conv(g_ref[...], halo, cw_ref, cb_ref)
        o_ref[...] = (_gelu(gc) * u_ref[...]).astype(BF16)

    tile = pl.BlockSpec((tt, tc), lambda i, j: (i, j))
    prev8 = pl.BlockSpec((8, tc), lambda i, j: (jnp.maximum(i * (tt // 8) - 1, 0), j))
    return _pcall(
        body, name=name, grid=(T // tt, F // tc),
        in_specs=[tile, prev8, tile, pl.BlockSpec((FFN_CONV, tc), lambda i, j: (0, j)),
                  pl.BlockSpec((1, tc), lambda i, j: (0, j))],
        out_specs=[tile], out_shape=[jax.ShapeDtypeStruct((T, F), BF16)], args=(g, g, u, cw, cb),
        sem=("parallel", "parallel"), ride=ride)[0]


def _ffn_act_bwd(g, u, dact, cw, cb, *, S, name, ride=None):
    T, F = g.shape
    tt = min(ROW_TILE, S)
    nt = S // tt
    ntt = T // tt
    tc = _tile(F, FFN_CT)

    def body(g_ref, halo_ref, u_ref, da_ref, cw_ref, cb_ref, dg_ref, du_ref, dcw_ref, dcb_ref, later_ref):
        step = pl.program_id(1)
        ti = (ntt - 1 - step) % nt
        halo = jnp.where(ti == 0, 0.0, halo_ref[...])
        gt = g_ref[...]
        gc = _ffn_conv(gt, halo, cw_ref, cb_ref)
        gl, dgl = _gelu_and_grad(gc)
        da = da_ref[...].astype(F32)
        du_ref[...] = (da * gl).astype(BF16)
        dgc = da * u_ref[...] * dgl
        later = jnp.where(ti == nt - 1, 0.0, later_ref[...])
        dg = cw_ref[2:3, :] * dgc
        for kk in range(FFN_CONV - 1):
            dg = dg + cw_ref[kk:kk + 1, :] * _shift_rows_up(dgc, FFN_CONV - 1 - kk, later)
        dg_ref[...] = dg.astype(BF16)
        later_ref[...] = dgc[0:8, :]
        rows = [jnp.sum(dgc * _shift_rows(gt, FFN_CONV - 1 - kk, halo), axis=0, keepdims=True)
                for kk in range(FFN_CONV - 1)]
        rows.append(jnp.sum(dgc * gt, axis=0, keepdims=True))
        dcw_part = jnp.concatenate(rows + [jnp.zeros((8 - FFN_CONV, tc), F32)], axis=0)
        dcb_part = jnp.sum(dgc, axis=0, keepdims=True)

        @pl.when(step == 0)
        def _():
            dcw_ref[...] = dcw_part
            dcb_ref[...] = dcb_part

        @pl.when(step > 0)
        def _():
            dcw_ref[...] += dcw_part
            dcb_ref[...] += dcb_part

    tile = pl.BlockSpec((tt, tc), lambda j, s: (ntt - 1 - s, j))
    prev8 = pl.BlockSpec((8, tc), lambda j, s: (jnp.maximum((ntt - 1 - s) * (tt // 8) - 1, 0), j))
    return _pcall(
        body, name=name, grid=(F // tc, ntt),
        in_specs=[tile, prev8, tile, tile, pl.BlockSpec((FFN_CONV, tc), lambda j, s: (0, j)),
                  pl.BlockSpec((1, tc), lambda j, s: (0, j))],
        out_specs=[tile, tile, pl.BlockSpec((8, tc), lambda j, s: (0, j)), pl.BlockSpec((1, tc), lambda j, s: (0, j))],
        out_shape=[jax.ShapeDtypeStruct((T, F), BF16), jax.ShapeDtypeStruct((T, F), BF16),
                   jax.ShapeDtypeStruct((8, F), F32), jax.ShapeDtypeStruct((1, F), F32)],
        args=(g, g, u, dact, cw, cb), scratch=[pltpu.VMEM((8, tc), F32)], ride=ride)


def _sgu_norm(zv, g_ref, b_ref):
    v = _gelu(zv)
    mu = jnp.mean(v, axis=-1, keepdims=True)
    xc = v - mu
    rstd = lax.rsqrt(jnp.mean(xc * xc, axis=-1, keepdims=True) + NORM_EPS)
    xhat = xc * rstd
    return xhat, rstd, xhat * g_ref[...] + b_ref[...]


def _sgu_fwd(zc, ln_g, ln_b, wm, bmap, *, name):
    T = zc.shape[0]
    W = SGU_WIDTH
    tt = ROW_TILE
    nch = tt // CHUNK

    def body(z_ref, g_ref, b_ref, wm_ref, bm_ref, p_ref):
        u = _gelu(z_ref[:, :W])
        _, _, vn = _sgu_norm(z_ref[:, W:], g_ref, b_ref)
        vn = vn.astype(BF16)
        for n in range(nch):
            rows = slice(n * CHUNK, (n + 1) * CHUNK)
            for gi in range(SGU_GROUPS):
                cols = slice(gi * LANES, (gi + 1) * LANES)
                s = _dot_nn(wm_ref[gi], vn[rows, cols]) + bm_ref[:, cols]
                p_ref[rows, cols] = (u[rows, cols] * s).astype(BF16)

    const2 = lambda r, c: pl.BlockSpec((r, c), lambda i: (0, 0))
    return pl.pallas_call(
        body, name=name, grid=(T // tt,),
        in_specs=[pl.BlockSpec((tt, 2 * W), lambda i: (i, 0)), const2(1, W), const2(1, W),
                  pl.BlockSpec((SGU_GROUPS, CHUNK, CHUNK), lambda i: (0, 0, 0)), const2(CHUNK, W)],
        out_specs=pl.BlockSpec((tt, W), lambda i: (i, 0)),
        out_shape=jax.ShapeDtypeStruct((T, W), BF16),
        compiler_params=_cparams(("parallel",)),
    )(zc, ln_g, ln_b, wm, bmap)


def _sgu_bwd(zc, dp, ln_g, ln_b, wm, bmap, *, name, ride=None):
    T = zc.shape[0]
    W = SGU_WIDTH
    tt = ROW_TILE
    nch = tt // CHUNK
    nsteps = T // tt

    def body(z_ref, dp_ref, g_ref, b_ref, wm_ref, bm_ref, dz_ref, dg_ref, db_ref, dwm_ref, dbm_ref,
             s_scr, dvn_scr):
        step = pl.program_id(0)
        zu = z_ref[:, :W]
        zv = z_ref[:, W:]
        u, dgu = _gelu_and_grad(zu)
        xhat, rstd, vn = _sgu_norm(zv, g_ref, b_ref)
        vnb = vn.astype(BF16)
        dpf = dp_ref[...].astype(F32)
        ds = dpf * u

        @pl.when(step == 0)
        def _():
            dwm_ref[...] = jnp.zeros_like(dwm_ref)
            dbm_ref[...] = jnp.zeros_like(dbm_ref)

        for n in range(nch):
            rows = slice(n * CHUNK, (n + 1) * CHUNK)
            for gi in range(SGU_GROUPS):
                cols = slice(gi * LANES, (gi + 1) * LANES)
                s_scr[rows, cols] = _dot_nn(wm_ref[gi], vnb[rows, cols]) + bm_ref[:, cols]
                dsb = ds[rows, cols]
                dvn_scr[rows, cols] = _dot_tn(wm_ref[gi], dsb)
                dwm_ref[gi] += _dot_nt(dsb, vnb[rows, cols])
                dbm_ref[:, cols] += dsb
        dz_ref[:, :W] = (dpf * s_scr[...] * dgu).astype(BF16)
        dvn = dvn_scr[...]
        dxhat = dvn * g_ref[...]
        dv = rstd * (dxhat - jnp.mean(dxhat, axis=-1, keepdims=True)
                     - xhat * jnp.mean(dxhat * xhat, axis=-1, keepdims=True))
        _, dgv = _gelu_and_grad(zv)
        dz_ref[:, W:] = (dv * dgv).astype(BF16)
        dg_part = jnp.sum(dvn * xhat, axis=0, keepdims=True)
        db_part = jnp.sum(dvn, axis=0, keepdims=True)

        @pl.when(step == 0)
        def _():
            dg_ref[...] = dg_part
            db_ref[...] = db_part

        @pl.when(step > 0)
        def _():
            dg_ref[...] += dg_part
            db_ref[...] += db_part

        @pl.when(step == nsteps - 1)
        def _():
            for gi in range(SGU_GROUPS):
                cols = slice(gi * LANES, (gi + 1) * LANES)
                tot = jnp.sum(dbm_ref[:, cols], axis=1, keepdims=True)
                dbm_ref[:, cols] = jnp.broadcast_to(tot, (CHUNK, LANES))

    const2 = lambda r, c: pl.BlockSpec((r, c), lambda i: (0, 0))
    wspec = pl.BlockSpec((SGU_GROUPS, CHUNK, CHUNK), lambda i: (0, 0, 0))
    return _pcall(
        body, name=name, grid=(nsteps,),
        in_specs=[pl.BlockSpec((tt, 2 * W), lambda i: (i, 0)), pl.BlockSpec((tt, W), lambda i: (i, 0)),
                  const2(1, W), const2(1, W), wspec, const2(CHUNK, W)],
        out_specs=[pl.BlockSpec((tt, 2 * W), lambda i: (i, 0)), const2(1, W), const2(1, W), wspec, const2(CHUNK, W)],
        out_shape=[jax.ShapeDtypeStruct((T, 2 * W), BF16), jax.ShapeDtypeStruct((1, W), F32),
                   jax.ShapeDtypeStruct((1, W), F32), jax.ShapeDtypeStruct((SGU_GROUPS, CHUNK, CHUNK), F32),
                   jax.ShapeDtypeStruct((CHUNK, W), F32)],
        args=(zc, dp, ln_g, ln_b, wm, bmap), scratch=[pltpu.VMEM((tt, W), F32), pltpu.VMEM((tt, W), F32)], ride=ride)


def _rope_tables(positions):
    half = QK_ROPE // 2
    inv_freq = jnp.exp(-math.log(ROPE_BASE) * jnp.arange(half, dtype=F32) / half)
    ang = positions.reshape(-1).astype(F32)[:, None] * inv_freq
    cos = jnp.cos(ang)
    sin = jnp.sin(ang)
    n = ang.shape[0]
    tail = LANES - QK_NOPE - QK_ROPE
    cos_t = jnp.concatenate([jnp.ones((n, QK_NOPE), F32), cos, cos, jnp.ones((n, tail), F32)], axis=1)
    sin_t = jnp.concatenate([jnp.zeros((n, QK_NOPE), F32), -sin, sin, jnp.zeros((n, tail), F32)], axis=1)
    return cos_t, sin_t


SGU_GROUP_DIM = SGU_WIDTH // SGU_GROUPS
_O1, _O2, _O3, _O4 = Q_LORA, Q_LORA + KV_LORA, Q_LORA + KV_LORA + QK_ROPE, Q_LORA + KV_LORA + QK_ROPE + LRU_WIDTH
_A0, _A1, _A2 = 2 * LRU_WIDTH, 2 * LRU_WIDTH + Q_LORA, 2 * LRU_WIDTH + Q_LORA + KV_LORA
_A3 = _A2 + QK_NOPE
Z_Q_BLOCK, Z_KV_BLOCK, Z_KPE_BLOCK = _A0 // Q_LORA, _A1 // KV_LORA, _A2 // LANES


def _perm_w_in(w_in):
    zeros = lambda n: jnp.zeros((w_in.shape[0], n), w_in.dtype)
    return jnp.concatenate([w_in[:, _O3:_O4], w_in[:, _O4:], w_in[:, :_O1], w_in[:, _O1:_O2], zeros(QK_NOPE),
                            w_in[:, _O2:_O3], zeros(LANES - QK_NOPE - QK_ROPE)], axis=1)


def _unperm_w_in(w):
    return jnp.concatenate([w[:, _A0:_A1], w[:, _A1:_A2], w[:, _A3:_A3 + QK_ROPE], w[:, :LRU_WIDTH],
                            w[:, LRU_WIDTH:_A0]], axis=1)


def _head_blocks(w, d):
    r = w.shape[0]
    return jnp.pad(w.reshape(r, MLA_HEADS, d), ((0, 0), (0, 0), (0, LANES - d))).reshape(r, MLA_HEADS * LANES)


def _from_head_blocks(w, d):
    r = w.shape[0]
    return w.reshape(r, MLA_HEADS, LANES)[:, :, :d].reshape(r, MLA_HEADS * d)


def _split_kv(w_kv):
    r = w_kv.shape[0]
    w3 = w_kv.reshape(r, MLA_HEADS, QK_NOPE + V_HEAD)
    return _head_blocks(w3[:, :, :QK_NOPE].reshape(r, -1), QK_NOPE), w3[:, :, QK_NOPE:].reshape(r, -1)


def _join_kv(w_k, w_v):
    r = w_k.shape[0]
    return jnp.concatenate([_from_head_blocks(w_k, QK_NOPE).reshape(r, MLA_HEADS, QK_NOPE),
                            w_v.reshape(r, MLA_HEADS, V_HEAD)], axis=2).reshape(r, -1)


def _prep_small(w):
    p = {n: w[n] for n in w if n not in BIG}
    eye = jnp.eye(LRU_HEADS, dtype=F32)
    dense = lambda wg: (wg[:, :, None, :] * eye[:, None, :, None]).reshape(LRU_WIDTH, LRU_WIDTH).astype(BF16)
    p["wa_d"] = dense(w["ab_w_rg_a"][0])
    p["wx_d"] = dense(w["ab_w_rg_x"][0])
    causal = jnp.tril(jnp.ones((CHUNK, CHUNK), F32))
    p["wm"] = (w["c_w_s"][0] * causal).astype(BF16)
    p["bmap"] = jnp.repeat(w["c_b_s"][0].T, SGU_GROUP_DIM, axis=1)
    return p


def _prep_big(ab_w_in, ab_w_q_b, ab_w_kv_b):
    return {"w_in_p": _perm_w_in(ab_w_in).astype(BF16),
            "w_q_p": _head_blocks(ab_w_q_b, QK_NOPE + QK_ROPE).astype(BF16),
            "w_kv_p": jnp.concatenate(_split_kv(ab_w_kv_b), axis=1).astype(BF16)}


def _ffn_fwd(h, l, p, S, rides):
    hn = _rms_fwd(h, p["ffn_norm"][l], name=f"ffn{l}_norm")
    g = _mm(hn, p["ffn_gate_t"][l], tb=True, name=f"ffn{l}_gate", ride=rides.get(f"ffn{l}_gate"))
    u = _mm(hn, p["ffn_up_t"][l], tb=True, name=f"ffn{l}_up", ride=rides.get(f"ffn{l}_up"))
    act = _ffn_act_fwd(g, u, p["ffn_conv_w"][l], p["ffn_conv_b"][l][None], S=S, name=f"ffn{l}_act",
                       ride=rides.get(f"ffn{l}_act"))
    out = _mm(act, p["ffn_down"][l], res=h, name=f"ffn{l}_down", ride=rides.get(f"ffn{l}_down"))
    return out, (hn, g, u, act)


def _ffn_bwd(dh, h_in, l, p, saved, S, rides):
    hn, g, u, act = saved
    dact = _mm(dh, p["ffn_down"][l], tb=True, out_dtype=BF16, name=f"ffn{l}_dact", ride=rides.get(f"ffn{l}_dact"))
    dw_down = _mm(act, dh, ta=True, out_dtype=BF16, name=f"ffn{l}_dwdown")
    dg, du, dcw, dcb = _ffn_act_bwd(g, u, dact, p["ffn_conv_w"][l], p["ffn_conv_b"][l][None], S=S,
                                    name=f"ffn{l}_dactbwd", ride=rides.get(f"ffn{l}_dactbwd"))
    dhn = _mm(dg, p["ffn_gate_t"][l], name=f"ffn{l}_dhn_g")
    dhn = _mm(du, p["ffn_up_t"][l], res=dhn, name=f"ffn{l}_dhn_u")
    dw_gate_t = _mm(dg, hn, ta=True, out_dtype=BF16, name=f"ffn{l}_dwgate")
    dw_up_t = _mm(du, hn, ta=True, out_dtype=BF16, name=f"ffn{l}_dwup")
    dh_in, dnorm = _rms_bwd(h_in, p["ffn_norm"][l], dhn, res=dh, name=f"ffn{l}_dnorm")
    grads = dict(ffn_norm=dnorm[0], ffn_gate_t=dw_gate_t, ffn_up_t=dw_up_t, ffn_conv_w=dcw[:FFN_CONV],
                 ffn_conv_b=dcb[0], ffn_down=dw_down)
    return dh_in, grads


def _local_step(x, positions, target, p, rides=None, grads_ready=None):
    rides = dict(rides or {})
    more_rides = grads_ready or (lambda layer, ready: {})
    B, S, D = x.shape
    T = B * S
    H = MLA_HEADS
    xf = x.reshape(T, D)
    tgt = target.reshape(T, D)
    cos, sin = _rope_tables(positions)

    hn0 = _rms_fwd(xf, p["ab_norm"][0], name="ab_norm")
    z = _mm(hn0, p["w_in_p"], name="ab_in")
    cqn = _rms_fwd(z, p["ab_q_norm"][0], cb=Z_Q_BLOCK, name="q_norm")
    ckvn = _rms_fwd(z, p["ab_kv_norm"][0], cb=Z_KV_BLOCK, name="kv_norm")
    q = _mm(cqn, p["w_q_p"], name="q_up")
    kv = _mm(ckvn, p["w_kv_p"], out_dtype=BF16, name="kv_up")
    qs = _rope_q(q, cos, sin, name="q_rope")
    kk = _key_blocks(kv, z, cos, sin, kpe_block=Z_KPE_BLOCK, name="k_rope")
    att = dict(B=B, S=S, v_block0=H)
    o, lse = _attn_fwd(qs, kk, kv, name="attn_fwd", ride=rides.get("attn_fwd"), **att)
    lru_par = (p["ab_conv_w"][0], p["ab_conv_b"], p["wa_d"], p["ab_b_rg_a"], p["wx_d"], p["ab_b_rg_x"], p["ab_lambda"])
    y_lru, hs = _lru_fwd(z, *lru_par, S=S, name="lru_fwd")
    n_att = H * V_HEAD
    w_out_a, w_out_b = p["ab_w_out"][:n_att], p["ab_w_out"][n_att:]
    h1 = _mm(y_lru, w_out_b, res=_mm(o, w_out_a, res=xf, name="ab_out_a"), name="ab_out_b")
    h2, ffn0 = _ffn_fwd(h1, 0, p, S, rides)

    hn2 = _rms_fwd(h2, p["c_norm"][0], name="c_norm")
    zc = _mm(hn2, p["c_w_in_t"], tb=True, name="c_in")
    pg = _sgu_fwd(zc, p["c_ln_g"], p["c_ln_b"], p["wm"], p["bmap"], name="sgu_fwd")
    h3 = _mm(pg, p["c_w_out"], res=h2, name="c_out")
    h4, ffn1 = _ffn_fwd(h3, 1, p, S, rides)

    loss_row, dh4, dfinal = _final_fwd_bwd(h4, p["final_norm"], tgt, name="final")

    big = ("ffn_gate_t", "ffn_up_t", "ffn_down")
    dh3, g_ffn1 = _ffn_bwd(dh4, h3, 1, p, ffn1, S, rides)
    rides.update(more_rides(1, {n: g_ffn1[n] for n in big}))
    dpg = _mm(dh3, p["c_w_out"], tb=True, out_dtype=BF16, name="c_dp")
    dw_c_out = _mm(pg, dh3, ta=True, out_dtype=BF16, name="c_dwout")
    dzc, dlng, dlnb, dwm, dbm = _sgu_bwd(zc, dpg, p["c_ln_g"], p["c_ln_b"], p["wm"], p["bmap"], name="sgu_bwd",
                                         ride=rides.get("sgu_bwd"))
    dhn2 = _mm(dzc, p["c_w_in_t"], name="c_dhn")
    dw_c_in_t = _mm(dzc, hn2, ta=True, out_dtype=BF16, name="c_dwin")
    dh2, dcnorm = _rms_bwd(h2, p["c_norm"][0], dhn2, res=dh3, name="c_dnorm")
    dh1, g_ffn0 = _ffn_bwd(dh2, h1, 0, p, ffn0, S, rides)
    rides.update(more_rides(0, {"c_w_in_t": dw_c_in_t, "c_w_out": dw_c_out, **{n: g_ffn0[n] for n in big}}))

    do = _mm(dh1, w_out_a, tb=True, name="ab_do")
    dy_lru = _mm(dh1, w_out_b, tb=True, out_dtype=BF16, name="ab_dylru")
    dw_out = jnp.concatenate([_mm(o, dh1, ta=True, out_dtype=BF16, name="ab_dwout_a"),
                              _mm(y_lru, dh1, ta=True, out_dtype=BF16, name="ab_dwout_b")], axis=0)
    dq, delta = _attn_dq(qs, kk, kv, o, lse, do, name="attn_dq", ride=rides.get("attn_dq"), **att)
    nq = S // min(ATT_BLOCK, S)
    rows = lambda a: a.reshape(B, H, nq, S // nq)
    dk, dv = _attn_dkv(qs, kk, kv, rows(lse), rows(delta), do, name="attn_dkv", ride=rides.get("attn_dkv"), **att)
    dq_full = _rope_q_bwd(dq, cos, sin, name="q_rope_bwd")
    dkr = _key_rope_bwd(dk, cos, sin, name="k_rope_bwd")
    n_key = H * LANES
    w_k_p, w_v_p = p["w_kv_p"][:, :n_key], p["w_kv_p"][:, n_key:]
    dcqn = _mm(dq_full, p["w_q_p"], tb=True, name="q_dlat")
    dw_q_p = _mm(cqn, dq_full, ta=True, out_dtype=BF16, name="q_dw")
    dckvn = _mm(dv, w_v_p, tb=True, res=_mm(dk, w_k_p, tb=True, name="k_dlat"), name="v_dlat")
    dw_k_p = _mm(ckvn, dk, ta=True, out_dtype=BF16, name="k_dw")
    dw_v_p = _mm(ckvn, dv, ta=True, out_dtype=BF16, name="v_dw")
    dcq, dqnorm = _rms_bwd(z, p["ab_q_norm"][0], dcqn, cb=Z_Q_BLOCK, out_dtype=BF16, name="q_dnorm")
    dckv, dkvnorm = _rms_bwd(z, p["ab_kv_norm"][0], dckvn, cb=Z_KV_BLOCK, out_dtype=BF16, name="kv_dnorm")
    dxl, dgate, dcw, dcb, dwa, dba, dwx, dbx, dlam = _lru_bwd(z, hs, dy_lru, *lru_par, S=S, name="lru_bwd")
    dz = jnp.concatenate([dxl, dgate, dcq, dckv, dkr], axis=1)
    dhn0 = _mm(dz, p["w_in_p"], tb=True, name="ab_dhn")
    dw_in_p = _mm(hn0, dz, ta=True, out_dtype=BF16, name="ab_dwin")
    dx, dabnorm = _rms_bwd(xf, p["ab_norm"][0], dhn0, res=dh1, name="ab_dnorm")

    blocks = lambda dd: jnp.stack([dd[i * LRU_BLOCK:(i + 1) * LRU_BLOCK, i * LRU_BLOCK:(i + 1) * LRU_BLOCK]
                                   for i in range(LRU_HEADS)])
    causal = jnp.tril(jnp.ones((CHUNK, CHUNK), F32))
    grads = {
        "ab_norm": dabnorm, "w_in_p": dw_in_p, "ab_q_norm": dqnorm, "w_q_p": dw_q_p,
        "ab_kv_norm": dkvnorm, "w_k_p": dw_k_p, "w_v_p": dw_v_p, "ab_conv_w": dcw[:LRU_CONV][None], "ab_conv_b": dcb,
        "ab_w_rg_a": blocks(dwa)[None], "ab_b_rg_a": dba, "ab_w_rg_x": blocks(dwx)[None], "ab_b_rg_x": dbx,
        "ab_lambda": dlam, "ab_w_out": dw_out,
        "c_norm": dcnorm, "c_w_in_t": dw_c_in_t, "c_ln_g": dlng, "c_ln_b": dlnb,
        "c_w_s": (dwm * causal)[None], "c_b_s": dbm[:, ::SGU_GROUP_DIM].T[None], "c_w_out": dw_c_out,
        "final_norm": dfinal[0],
    }
    for name in ("ffn_norm", "ffn_conv_w", "ffn_conv_b"):
        grads[name] = jnp.stack([g_ffn0[name], g_ffn1[name]])
    for name in ("ffn_gate_t", "ffn_up_t", "ffn_down"):
        grads[name] = [g_ffn0[name], g_ffn1[name]]
    return loss_row, dx.reshape(B, S, D), grads


ANY = pl.BlockSpec(memory_space=pl.ANY)


def _place():
    x, y, c = lax.axis_index("x"), lax.axis_index("y"), lax.axis_index("c")
    chips = [(1 - x, y), (x, 1 - y), (1 - x, 1 - y)]
    return x, y, c, 2 * x + y, (x, y, 1 - c), chips


def _remote(src, dst, send_sems, recv_sems, k, to):
    return pltpu.make_async_remote_copy(src_ref=src, dst_ref=dst, send_sem=send_sems.at[k], recv_sem=recv_sems.at[k],
                                        device_id=to, device_id_type=MESH)


class _Exchange:
    def __init__(self, arrs, out_shapes, n_sems, start, finish):
        self.arrs, self.out_shapes, self.n_sems, self.start, self.finish = list(arrs), out_shapes, n_sems, start, finish

    @property
    def in_specs(self):
        return [ANY] * len(self.arrs)

    @property
    def out_specs(self):
        return [ANY] * len(self.out_shapes)

    @property
    def scratch(self):
        return [pltpu.SemaphoreType.DMA((self.n_sems,)), pltpu.SemaphoreType.DMA((self.n_sems,))]

    def split(self, refs):
        n = len(self.arrs)
        return refs[:n], refs[n:n + len(self.out_shapes)], refs[-2], refs[-1]

    def run(self, name):
        def body(*refs):
            parts = self.split(refs)
            self.start(*parts)
            self.finish(*parts)

        return pl.pallas_call(body, name=name, in_specs=self.in_specs, out_specs=self.out_specs,
                              out_shape=self.out_shapes, scratch_shapes=self.scratch)(*self.arrs)


def _put(buf, piece, idx, axis):
    return lax.dynamic_update_slice_in_dim(buf, jnp.expand_dims(piece, axis).astype(buf.dtype), idx, axis)


def _all_gather(arrs):
    n = len(arrs)

    def start(ins, outs, send_sems, recv_sems):
        x, y, c, j, sib, chips = _place()
        for i in range(n):
            for k, (cx, cy) in enumerate(chips):
                _remote(ins[i].at[:, c], outs[i].at[:, j, c], send_sems, recv_sems, 6 * i + k, (cx, cy, c)).start()

    def finish(ins, outs, send_sems, recv_sems):
        x, y, c, j, sib, chips = _place()
        passed = []
        for i in range(n):
            for k, (cx, cy) in enumerate(chips):
                got = outs[i].at[:, 2 * cx + cy, c]
                _remote(got, got, send_sems, recv_sems, 6 * i + k, (cx, cy, c)).wait_recv()
                cp = _remote(got, got, send_sems, recv_sems, 6 * i + 3 + k, sib)
                cp.start()
                passed.append(cp)
        for i in range(n):
            for k, (cx, cy) in enumerate(chips):
                got = outs[i].at[:, 2 * cx + cy, 1 - c]
                _remote(got, got, send_sems, recv_sems, 6 * i + 3 + k, sib).wait_recv()
                _remote(ins[i].at[:, c], ins[i].at[:, c], send_sems, recv_sems, 6 * i + k, sib).wait_send()
        for cp in passed:
            cp.wait_send()

    shapes = [jax.ShapeDtypeStruct((a.shape[0], N_CHIPS) + a.shape[1:], a.dtype) for a in arrs]
    return _Exchange(arrs, shapes, 6 * n, start, finish)


class _Offset:
    def __init__(self, sems, k0):
        self.sems, self.k0 = sems, k0

    @property
    def at(self):
        return self

    def __getitem__(self, k):
        return self.sems.at[self.k0 + k]


def _merge(a, b):
    n_in, n_out = len(a.arrs), len(a.out_shapes)

    def both(fa, fb):
        def f(ins, outs, send_sems, recv_sems):
            fa(ins[:n_in], outs[:n_out], send_sems, recv_sems)
            fb(ins[n_in:], outs[n_out:], _Offset(send_sems, a.n_sems), _Offset(recv_sems, a.n_sems))
        return f

    return _Exchange(a.arrs + b.arrs, a.out_shapes + b.out_shapes, a.n_sems + b.n_sems,
                     both(a.start, b.start), both(a.finish, b.finish))


def _pair_swap(arrs):
    n = len(arrs)

    def start(ins, outs, send_sems, recv_sems):
        x, y, c, j, sib, chips = _place()
        for i in range(n):
            _remote(ins[i].at[:, 1 - c], outs[i], send_sems, recv_sems, i, sib).start()

    def finish(ins, outs, send_sems, recv_sems):
        x, y, c, j, sib, chips = _place()
        for i in range(n):
            _remote(ins[i].at[:, 1 - c], outs[i], send_sems, recv_sems, i, sib).wait()

    shapes = [jax.ShapeDtypeStruct((a.shape[0],) + a.shape[2:], a.dtype) for a in arrs]
    return _Exchange(arrs, shapes, n, start, finish)


def _pair_send(arrs):
    n = len(arrs)

    def start(ins, outs, send_sems, recv_sems):
        x, y, c, j, sib, chips = _place()
        for i in range(n):
            _remote(ins[i], outs[i], send_sems, recv_sems, i, sib).start()

    def finish(ins, outs, send_sems, recv_sems):
        x, y, c, j, sib, chips = _place()
        for i in range(n):
            _remote(ins[i], outs[i], send_sems, recv_sems, i, sib).wait()

    shapes = [jax.ShapeDtypeStruct(a.shape, a.dtype) for a in arrs]
    return _Exchange(arrs, shapes, n, start, finish)


def _chip_exchange(arrs, *, scatter):
    n = len(arrs)

    def copies(ins, outs, send_sems, recv_sems):
        x, y, c, j, sib, chips = _place()
        return [(_remote(ins[i].at[2 * cx + cy] if scatter else ins[i], outs[i].at[j], send_sems, recv_sems,
                         3 * i + k, (cx, cy, c)),
                 _remote(outs[i].at[2 * cx + cy], outs[i].at[2 * cx + cy], send_sems, recv_sems, 3 * i + k, (cx, cy, c)))
                for i in range(n) for k, (cx, cy) in enumerate(chips)]

    def start(*refs):
        for out, _ in copies(*refs):
            out.start()

    def finish(*refs):
        for out, back in copies(*refs):
            back.wait_recv()
            out.wait_send()

    shapes = [jax.ShapeDtypeStruct((N_CHIPS,) + a.shape[-2:], a.dtype) for a in arrs]
    return _Exchange(arrs, shapes, 3 * n, start, finish)


FLAT_ROWS = 512


def _add2(a, b, *, out_dtype, name):
    n, R, L = a.shape
    tr = _tile(R, FLAT_ROWS, 16)

    def body(a_ref, b_ref, o_ref):
        o_ref[...] = (a_ref[...].astype(F32) + b_ref[...].astype(F32)).astype(out_dtype)

    spec = pl.BlockSpec((n, tr, L), lambda i: (0, i, 0))
    return pl.pallas_call(
        body, name=name, grid=(R // tr,), in_specs=[spec, spec], out_specs=spec,
        out_shape=jax.ShapeDtypeStruct(a.shape, out_dtype), compiler_params=_cparams(("parallel",)),
    )(a, b)


def _sum_slots(buf, *, name):
    n, R, L = buf.shape
    tr = _tile(R, FLAT_ROWS, 16)

    def body(b_ref, o_ref):
        acc = b_ref[0].astype(F32)
        for k in range(1, n):
            acc = acc + b_ref[k].astype(F32)
        o_ref[...] = acc

    return pl.pallas_call(
        body, name=name, grid=(R // tr,), in_specs=[pl.BlockSpec((n, tr, L), lambda i: (0, i, 0))],
        out_specs=pl.BlockSpec((tr, L), lambda i: (i, 0)),
        out_shape=jax.ShapeDtypeStruct((R, L), F32), compiler_params=_cparams(("parallel",)),
    )(buf)


def _adamw(w, g, m, v, *, name):
    NL, R, L = w.shape
    tr = _tile(R, FLAT_ROWS, 16)
    c1 = 1.0 - ADAM_B1 ** ADAM_STEP
    c2 = 1.0 - ADAM_B2 ** ADAM_STEP

    def body(w_ref, g_ref, m_ref, v_ref, d_ref, nm_ref, nv_ref):
        gg = g_ref[...]
        mm = ADAM_B1 * m_ref[...] + (1.0 - ADAM_B1) * gg
        vv = ADAM_B2 * v_ref[...] + (1.0 - ADAM_B2) * (gg * gg)
        nm_ref[...] = mm
        nv_ref[...] = vv
        d_ref[...] = -ADAM_LR * ((mm / c1) / (jnp.sqrt(vv / c2) + ADAM_EPS) + ADAM_WD * w_ref[...])

    spec = pl.BlockSpec((1, tr, L), lambda l, i: (l, i, 0))
    sh = jax.ShapeDtypeStruct((NL, R, L), F32)
    return pl.pallas_call(
        body, name=name, grid=(NL, R // tr), in_specs=[spec] * 4, out_specs=[spec] * 3, out_shape=[sh] * 3,
        compiler_params=_cparams(("parallel", "parallel")),
    )(w, g, m, v)


WEIGHT_NAMES = ["ab_norm", "ab_w_in", "ab_q_norm", "ab_w_q_b", "ab_kv_norm", "ab_w_kv_b", "ab_conv_w", "ab_conv_b",
                "ab_w_rg_a", "ab_b_rg_a", "ab_w_rg_x", "ab_b_rg_x", "ab_lambda", "ab_w_out", "c_norm", "c_w_in",
                "c_ln_g", "c_ln_b", "c_w_s", "c_b_s", "c_w_out", "ffn_norm", "ffn_w_gate", "ffn_w_up", "ffn_conv_w",
                "ffn_conv_b", "ffn_w_down", "final_norm"]
BIG = {"ab_w_in": 2, "ab_w_q_b": 2, "ab_w_kv_b": 2, "ab_w_out": 1, "c_w_in": 2, "c_w_out": 1,
       "ffn_w_gate": 2, "ffn_w_up": 2, "ffn_w_down": 1}
SMALL_SHARDED = {"ab_conv_w": 2, "c_norm": 1, "c_ln_g": 1, "c_ln_b": 1, "ffn_conv_w": 2}
SMALL_REPLICATED = [n for n in WEIGHT_NAMES if n not in BIG and n not in SMALL_SHARDED]


def _rows(n_elems, mult):
    r = -(-n_elems // LANES)
    return -(-r // mult) * mult


def _flat(parts, rows):
    flat = jnp.concatenate([a.reshape(-1) for a in parts])
    return jnp.pad(flat, (0, rows * LANES - flat.shape[0])).reshape(rows, LANES)


def _unflat(flat, shapes):
    flat = flat.reshape(-1)
    out, off = [], 0
    for s in shapes:
        n = math.prod(s)
        out.append(flat[off:off + n].reshape(s))
        off += n
    return out


def _join_shards(a, axis):
    a = jnp.moveaxis(a, 0, axis)
    return a.reshape(a.shape[:axis] + (a.shape[axis] * a.shape[axis + 1],) + a.shape[axis + 2:])


def kernel(x, positions, ab_norm, ab_w_in, ab_q_norm, ab_w_q_b, ab_kv_norm, ab_w_kv_b, ab_conv_w, ab_conv_b, ab_w_rg_a, ab_b_rg_a, ab_w_rg_x, ab_b_rg_x, ab_lambda, ab_w_out, c_norm, c_w_in, c_ln_g, c_ln_b, c_w_s, c_b_s, c_w_out, ffn_norm, ffn_w_gate, ffn_w_up, ffn_conv_w, ffn_conv_b, ffn_w_down, final_norm, loss_target, m_ab_norm, m_ab_w_in, m_ab_q_norm, m_ab_w_q_b, m_ab_kv_norm, m_ab_w_kv_b, m_ab_conv_w, m_ab_conv_b, m_ab_w_rg_a, m_ab_b_rg_a, m_ab_w_rg_x, m_ab_b_rg_x, m_ab_lambda, m_ab_w_out, m_c_norm, m_c_w_in, m_c_ln_g, m_c_ln_b, m_c_w_s, m_c_b_s, m_c_w_out, m_ffn_norm, m_ffn_w_gate, m_ffn_w_up, m_ffn_conv_w, m_ffn_conv_b, m_ffn_w_down, m_final_norm, v_ab_norm, v_ab_w_in, v_ab_q_norm, v_ab_w_q_b, v_ab_kv_norm, v_ab_w_kv_b, v_ab_conv_w, v_ab_conv_b, v_ab_w_rg_a, v_ab_b_rg_a, v_ab_w_rg_x, v_ab_b_rg_x, v_ab_lambda, v_ab_w_out, v_c_norm, v_c_w_in, v_c_ln_g, v_c_ln_b, v_c_w_s, v_c_b_s, v_c_w_out, v_ffn_norm, v_ffn_w_gate, v_ffn_w_up, v_ffn_conv_w, v_ffn_conv_b, v_ffn_w_down, v_final_norm):
    given = dict(locals())
    w = {n: given[n] for n in WEIGHT_NAMES}
    m = {n: given["m_" + n] for n in WEIGHT_NAMES}
    v = {n: given["v_" + n] for n in WEIGHT_NAMES}
    c = lax.axis_index("c")
    chip = 2 * lax.axis_index("x") + lax.axis_index("y")

    halves = lambda a: a.reshape(a.shape[0], 2, a.shape[1] // 2, a.shape[2])
    tr = lambda a: jnp.swapaxes(a, 1, 2)
    send = {"ab_w_in": w["ab_w_in"], "ab_w_q_b": w["ab_w_q_b"], "ab_w_kv_b": w["ab_w_kv_b"], "ab_w_out": w["ab_w_out"],
            "c_w_in": tr(w["c_w_in"]), "c_w_out": w["c_w_out"], "ffn_w_gate": tr(w["ffn_w_gate"]),
            "ffn_w_up": tr(w["ffn_w_up"]), "ffn_w_down": w["ffn_w_down"]}
    small_rows = _rows(sum(w[n].size for n in SMALL_SHARDED), 16)
    small_sh = _flat([w[n] for n in SMALL_SHARDED], small_rows).reshape(1, 2, small_rows // 2, LANES)
    first_names = ["ab_w_in", "ab_w_q_b", "ab_w_kv_b", "ab_w_out"]
    mine = {n: halves(send[n].astype(BF16)) for n in BIG}

    def put_own(own, arrived):
        a = _put(arrived, own, chip, 1)
        return a.reshape(a.shape[0], -1, a.shape[-1])

    first = [mine[n] for n in first_names] + [small_sh]
    got = _all_gather(first).run("gather_first")
    full = {n: put_own(o, a) for n, o, a in zip(first_names + ["small"], first, got)}
    unshard = lambda a: jnp.swapaxes(a.reshape(N_CHIPS, -1, a.shape[-1]), 0, 1).reshape(-1, N_CHIPS * a.shape[-1])
    p = _prep_big(unshard(full["ab_w_in"][0]), unshard(full["ab_w_q_b"][0]), unshard(full["ab_w_kv_b"][0]))
    p["ab_w_out"] = full["ab_w_out"][0]
    small_full = dict(w)
    off = 0
    small_got = full["small"].reshape(N_CHIPS, -1)
    for n, ax in SMALL_SHARDED.items():
        seg = small_got[:, off:off + w[n].size].reshape((N_CHIPS,) + w[n].shape)
        small_full[n] = _join_shards(seg, ax)
        off += w[n].size
    p.update(_prep_small(small_full))

    for n in ("ffn_gate_t", "ffn_up_t", "ffn_down"):
        p[n] = {}

    def weights_ride(parts):
        def sink(arrived):
            for (own, setter), a in zip(parts, arrived):
                setter(put_own(own, a)[0])
        return _all_gather([own for own, _ in parts]), sink

    ffn_keys = {"ffn_gate_t": "ffn_w_gate", "ffn_up_t": "ffn_w_up", "ffn_down": "ffn_w_down"}
    ffn_part = lambda key, l: (mine[ffn_keys[key]][l:l + 1], functools.partial(p[key].__setitem__, l))
    rides = {
        "attn_fwd": weights_ride([ffn_part(key, 0) for key in ffn_keys]),
        "ffn0_gate": weights_ride([ffn_part("ffn_gate_t", 1)]),
        "ffn0_up": weights_ride([ffn_part("ffn_up_t", 1)]),
        "ffn0_act": weights_ride([ffn_part("ffn_down", 1)]),
        "ffn0_down": weights_ride([(mine["c_w_in"], functools.partial(p.__setitem__, "c_w_in_t")),
                                   (mine["c_w_out"], functools.partial(p.__setitem__, "c_w_out"))]),
    }

    def pair_sums(sharded, tag):
        sharded = [a.reshape(N_CHIPS, 2, -1, a.shape[-1]) for a in sharded]
        from_sib = _pair_swap(sharded).run(f"grad_pair_swap_{tag}")
        own = [lax.dynamic_index_in_dim(a, c, axis=1, keepdims=False) for a in sharded]
        return [_add2(a, b, out_dtype=BF16, name=f"grad_pair_add_{tag}{i}") for i, (a, b) in enumerate(zip(own, from_sib))]

    def chip_sums(pair, arrived, tag):
        own = [lax.dynamic_index_in_dim(a, chip, axis=0, keepdims=False) for a in pair]
        return [_sum_slots(_put(a, o, chip, 0), name=f"grad_chip_sum_{tag}{i}") for i, (a, o) in enumerate(zip(arrived, own))]

    half_of = {}

    def grad_rides(tag, named, hosts):
        pair = dict(zip(named, pair_sums(list(named.values()), tag)))
        out = {}
        for kernel_name, keys in hosts.items():
            def sink(arrived, keys=keys, kernel_name=kernel_name):
                half_of.update(zip(keys, chip_sums([pair[k] for k in keys], arrived, f"{tag}_{kernel_name}")))
            out[kernel_name] = (_chip_exchange([pair[k] for k in keys], scatter=True), sink)
        return out

    def grads_ready(layer, ready):
        if layer == 1:
            return grad_rides("f1", {"gate1": ready["ffn_gate_t"], "up1": ready["ffn_up_t"], "down1": ready["ffn_down"]},
                              {"sgu_bwd": ["down1"], "ffn0_dact": ["up1"], "ffn0_dactbwd": ["gate1"]})
        return grad_rides("f0", {"c_in": ready["c_w_in_t"], "c_out": ready["c_w_out"], "gate0": ready["ffn_gate_t"],
                                 "up0": ready["ffn_up_t"], "down0": ready["ffn_down"]},
                          {"attn_dq": ["c_in", "c_out", "down0"], "attn_dkv": ["gate0", "up0"]})

    loss_row, grad_x, g = _local_step(x, positions, loss_target, p, rides, grads_ready)

    cols = lambda a, n: jnp.swapaxes(a.reshape(a.shape[0], N_CHIPS, n), 0, 1)
    n_in, n_q, n_kv = w["ab_w_in"].shape[2], w["ab_w_q_b"].shape[2], w["ab_w_kv_b"].shape[2]
    small_names = SMALL_REPLICATED + list(SMALL_SHARDED)
    rs = _rows(sum(g[n].size for n in small_names) + LANES, FLAT_ROWS)
    small = _flat([loss_row] + [g[n] for n in small_names], rs)
    slot = (jnp.arange(2) == c)[:, None, None]
    last = [cols(_unperm_w_in(g["w_in_p"]), n_in), cols(_from_head_blocks(g["w_q_p"], QK_NOPE + QK_ROPE), n_q),
            cols(_join_kv(g["w_k_p"], g["w_v_p"]), n_kv), g["ab_w_out"]]
    last = [a.reshape(N_CHIPS, 2, -1, a.shape[-1]) for a in last]
    *from_sib, small_sib = _merge(_pair_swap(last), _pair_send([small])).run("tail_pair")
    own = [lax.dynamic_index_in_dim(a, c, axis=1, keepdims=False) for a in last]
    pair = [_add2(a, b, out_dtype=BF16, name=f"grad_pair_add_b{i}") for i, (a, b) in enumerate(zip(own, from_sib))]
    pair_small = _sum_slots(jnp.where(slot, small[None], small_sib[None]), name="small_pair_sum")
    *arrived, all_small = _merge(_chip_exchange(pair, scatter=True), _chip_exchange([pair_small], scatter=False)).run("tail_chip")
    half_of.update(zip(["in", "q", "kv", "out"], chip_sums(pair, arrived, "b")))
    small_sum = _sum_slots(_put(all_small, pair_small, chip, 0), name="small_chip_sum")
    half = [half_of[k] for k in ("in", "q", "kv", "out", "c_in", "c_out", "gate0", "gate1", "up0", "up1", "down0", "down1")]
    summed = [jnp.where(slot, a[None], b[None]).reshape(-1, a.shape[-1])
              for a, b in zip(half, _pair_send(half).run("grad_pair_share"))]
    s_in, s_q, s_kv, s_out, s_cin, s_cout, g0, g1, u0, u1, d0, d1 = summed
    grads_t = {"ab_w_in": s_in.T[None], "ab_w_q_b": s_q.T[None], "ffn_w_gate": jnp.stack([g0, g1]),
               "ffn_w_up": jnp.stack([u0, u1])}
    grads = {"ab_w_kv_b": s_kv[None], "ab_w_out": s_out[None], "c_w_in": s_cin.T[None], "c_w_out": s_cout[None],
             "ffn_w_down": jnp.stack([d0, d1]), **{n: tr(a) for n, a in grads_t.items()}}

    small_parts = _unflat(small_sum, [(1, LANES)] + [g[n].shape for n in small_names])
    loss = small_parts[0][0, 0]
    for n, a in zip(small_names, small_parts[1:]):
        if n in SMALL_SHARDED:
            ax = SMALL_SHARDED[n]
            a = lax.dynamic_slice_in_dim(a, chip * w[n].shape[ax], w[n].shape[ax], axis=ax)
        grads[n] = a.reshape(w[n].shape)

    delta, new_m, new_v = {}, {}, {}
    for n in BIG:
        if n in grads_t:
            out = _adamw(tr(w[n]), grads_t[n], tr(m[n]), tr(v[n]), name=f"adamw_{n}")
            delta[n], new_m[n], new_v[n] = (tr(a) for a in out)
        else:
            delta[n], new_m[n], new_v[n] = _adamw(w[n], grads[n], m[n], v[n], name=f"adamw_{n}")
    small_all = [n for n in WEIGHT_NAMES if n not in BIG]
    ra = _rows(sum(w[n].size for n in small_all), FLAT_ROWS)
    pack = lambda d: _flat([d[n] for n in small_all], ra)[None]
    out = _adamw(pack(w), pack(grads), pack(m), pack(v), name="adamw_small")
    shapes = [w[n].shape for n in small_all]
    for d, flat in zip((delta, new_m, new_v), out):
        d.update(zip(small_all, _unflat(flat, shapes)))
    return (loss, grad_x, *[grads[n] for n in WEIGHT_NAMES], *[delta[n] for n in WEIGHT_NAMES],
            *[new_m[n] for n in WEIGHT_NAMES], *[new_v[n] for n in WEIGHT_NAMES])
```

```python
import functools
import math

import jax
import jax.numpy as jnp
from jax import lax
from jax.experimental import pallas as pl
from jax.experimental.pallas import tpu as pltpu

F32 = jnp.float32
BF16 = jnp.bfloat16
MESH = pl.DeviceIdType.MESH

D_MODEL = 1024
MLA_HEADS = 8
Q_LORA = 256
KV_LORA = 128
QK_NOPE = 64
QK_ROPE = 32
V_HEAD = 64
LRU_WIDTH = 512
LRU_HEADS = 8
LRU_BLOCK = 64
LRU_CONV = 4
LRU_C = 8.0
CHUNK = 128
SGU_GROUPS = 8
SGU_WIDTH = 1024
D_FF = 2816
FFN_CONV = 3
NORM_EPS = 1e-6
ROPE_BASE = 10000.0
AB_IN_PAD = 1536
ADAM_LR = 0.001
ADAM_B1 = 0.9
ADAM_B2 = 0.999
ADAM_EPS = 1e-08
ADAM_WD = 0.01
ADAM_STEP = 10

N_CHIPS = 4
LANES = 128
VMEM_LIMIT = 56 * 1024 * 1024
ROW_TILE = 256
MM_TM, MM_TN, MM_TK = 512, 1536, 2816
MM_TM_T, MM_TK_T = 1408, 1024
GELU_C = math.sqrt(2.0 / math.pi)


def _cparams(sem):
    return pltpu.CompilerParams(dimension_semantics=sem, vmem_limit_bytes=VMEM_LIMIT)


def _tile(n, target, mult=LANES):
    t = (min(n, target) // mult) * mult
    while t >= mult:
        if n % t == 0:
            return t
        t -= mult
    return n


GELU_K = GELU_C * 0.044715


def _gelu(x):
    t = jnp.tanh(x * (GELU_C + GELU_K * (x * x)))
    hx = 0.5 * x
    return hx + hx * t


def _gelu_and_grad(x):
    x2 = x * x
    t = jnp.tanh(x * (GELU_C + GELU_K * x2))
    hx = 0.5 * x
    dg = (0.5 + 0.5 * t) + (hx * (1.0 - t * t)) * (GELU_C + (3.0 * GELU_K) * x2)
    return hx + hx * t, dg


def _sigmoid(x):
    return 1.0 / (1.0 + jnp.exp(-x))


def _shift_rows(x, d, fill_rows):
    ext = jnp.concatenate([fill_rows, x], axis=0)
    return pltpu.roll(ext, d, 0)[8:]


def _shift_rows_up(x, d, fill_rows):
    n = x.shape[0]
    ext = jnp.concatenate([x, fill_rows], axis=0)
    return pltpu.roll(ext, n + 8 - d, 0)[:n]


def _dot(a, b, dims):
    return lax.dot_general(a.astype(BF16), b.astype(BF16), (dims, ((), ())), preferred_element_type=F32)


def _dot_nn(a, b):
    return _dot(a, b, ((1,), (0,)))


def _dot_nt(a, b):
    return _dot(a, b, ((1,), (1,)))


def _dot_tn(a, b):
    return _dot(a, b, ((0,), (0,)))


def _mm(a, b, *, name, ta=False, tb=False, res=None, out_dtype=F32, ride=None):
    if ta:
        K, M = a.shape
    else:
        M, K = a.shape
    N = b.shape[0] if tb else b.shape[1]
    tm = _tile(M, MM_TM_T if ta else MM_TM, LANES if ta else 8)
    tn = _tile(N, MM_TN, LANES)
    tk = _tile(K, MM_TK_T if ta else MM_TK, LANES)
    nk = K // tk
    a_spec = pl.BlockSpec((tk, tm), lambda j, i, k: (k, i)) if ta else pl.BlockSpec((tm, tk), lambda j, i, k: (i, k))
    b_spec = pl.BlockSpec((tn, tk), lambda j, i, k: (j, k)) if tb else pl.BlockSpec((tk, tn), lambda j, i, k: (k, j))
    o_spec = pl.BlockSpec((tm, tn), lambda j, i, k: (i, j))
    dims = ((0,) if ta else (1,), (1,) if tb else (0,))
    has_res = res is not None

    def body(*refs):
        a_ref, b_ref = refs[:2]
        r_ref = refs[2] if has_res else None
        o_ref = refs[3] if has_res else refs[2]
        p = _dot(a_ref[...], b_ref[...], dims)

        def finish(r):
            if has_res:
                r = r + r_ref[...].astype(F32)
            o_ref[...] = r.astype(out_dtype)

        if nk == 1:
            finish(p)
            return
        acc_ref = refs[-1]
        k = pl.program_id(2)

        @pl.when(k == 0)
        def _():
            acc_ref[...] = p

        @pl.when(jnp.logical_and(k > 0, k < nk - 1))
        def _():
            acc_ref[...] += p

        @pl.when(k == nk - 1)
        def _():
            finish(acc_ref[...] + p)

    in_specs = [a_spec, b_spec] + ([o_spec] if has_res else [])
    args = (a, b) + ((res,) if has_res else ())
    return _pcall(
        body, name=name, grid=(N // tn, M // tm, nk), in_specs=in_specs, out_specs=[o_spec],
        out_shape=[jax.ShapeDtypeStruct((M, N), out_dtype)], args=args,
        scratch=[pltpu.VMEM((tm, tn), F32)] if nk > 1 else [], sem=("parallel", "parallel", "arbitrary"), ride=ride)[0]


def _rms_fwd(x, g, *, name, cb=0, out_dtype=BF16):
    T = x.shape[0]
    W = g.shape[-1]
    g = g.reshape(1, W)
    tt = ROW_TILE

    def body(x_ref, g_ref, o_ref):
        xf = x_ref[...].astype(F32)
        rstd = lax.rsqrt(jnp.mean(xf * xf, axis=-1, keepdims=True) + NORM_EPS)
        o_ref[...] = (xf * rstd * g_ref[...]).astype(out_dtype)

    return pl.pallas_call(
        body, name=name, grid=(T // tt,),
        in_specs=[pl.BlockSpec((tt, W), lambda i: (i, cb)), pl.BlockSpec((1, W), lambda i: (0, 0))],
        out_specs=pl.BlockSpec((tt, W), lambda i: (i, 0)),
        out_shape=jax.ShapeDtypeStruct((T, W), out_dtype),
        compiler_params=_cparams(("parallel",)),
    )(x, g)


def _rms_bwd(x, g, dy, *, name, cb=0, res=None, out_dtype=F32, ride=None):
    T = x.shape[0]
    W = g.shape[-1]
    g = g.reshape(1, W)
    tt = ROW_TILE
    has_res = res is not None

    def body(*refs):
        if has_res:
            x_ref, g_ref, dy_ref, r_ref, dx_ref, dg_ref = refs
        else:
            x_ref, g_ref, dy_ref, dx_ref, dg_ref = refs
        xf = x_ref[...].astype(F32)
        dyf = dy_ref[...].astype(F32)
        rstd = lax.rsqrt(jnp.mean(xf * xf, axis=-1, keepdims=True) + NORM_EPS)
        xhat = xf * rstd
        dxhat = dyf * g_ref[...]
        dx = rstd * (dxhat - xhat * jnp.mean(dxhat * xhat, axis=-1, keepdims=True))
        if has_res:
            dx = dx + r_ref[...].astype(F32)
        dx_ref[...] = dx.astype(out_dtype)
        part = jnp.sum(dyf * xhat, axis=0, keepdims=True)

        @pl.when(pl.program_id(0) == 0)
        def _():
            dg_ref[...] = part

        @pl.when(pl.program_id(0) > 0)
        def _():
            dg_ref[...] += part

    row = pl.BlockSpec((tt, W), lambda i: (i, 0))
    in_specs = [pl.BlockSpec((tt, W), lambda i: (i, cb)), pl.BlockSpec((1, W), lambda i: (0, 0)), row]
    args = (x, g, dy)
    if has_res:
        in_specs.append(row)
        args = args + (res,)
    return _pcall(
        body, name=name, grid=(T // tt,), in_specs=in_specs,
        out_specs=[row, pl.BlockSpec((1, W), lambda i: (0, 0))],
        out_shape=[jax.ShapeDtypeStruct((T, W), out_dtype), jax.ShapeDtypeStruct((1, W), F32)], args=args, ride=ride)


def _final_fwd_bwd(h, g, target, *, name):
    T, W = h.shape
    g = g.reshape(1, W)
    tt = ROW_TILE

    def body(x_ref, g_ref, t_ref, loss_ref, dx_ref, dg_ref):
        xf = x_ref[...]
        rstd = lax.rsqrt(jnp.mean(xf * xf, axis=-1, keepdims=True) + NORM_EPS)
        xhat = xf * rstd
        err = xhat * g_ref[...] - t_ref[...]
        lpart = jnp.zeros((1, LANES), F32) + (0.5 / W) * jnp.sum(err * err)
        dyf = err * (1.0 / W)
        dxhat = dyf * g_ref[...]
        dx_ref[...] = rstd * (dxhat - xhat * jnp.mean(dxhat * xhat, axis=-1, keepdims=True))
        part = jnp.sum(dyf * xhat, axis=0, keepdims=True)

        @pl.when(pl.program_id(0) == 0)
        def _():
            dg_ref[...] = part
            loss_ref[...] = lpart

        @pl.when(pl.program_id(0) > 0)
        def _():
            dg_ref[...] += part
            loss_ref[...] += lpart

    row = pl.BlockSpec((tt, W), lambda i: (i, 0))
    return pl.pallas_call(
        body, name=name, grid=(T // tt,),
        in_specs=[row, pl.BlockSpec((1, W), lambda i: (0, 0)), row],
        out_specs=[pl.BlockSpec((1, LANES), lambda i: (0, 0)), row, pl.BlockSpec((1, W), lambda i: (0, 0))],
        out_shape=[jax.ShapeDtypeStruct((1, LANES), F32), jax.ShapeDtypeStruct((T, W), F32),
                   jax.ShapeDtypeStruct((1, W), F32)],
        compiler_params=_cparams(("arbitrary",)),
    )(h, g, target)


def _swap16(x):
    lane = lax.broadcasted_iota(jnp.int32, x.shape, 1)
    return jnp.where((lane % 32) < 16, pltpu.roll(x, LANES - 16, 1), pltpu.roll(x, 16, 1))


def _rope(x, c, s):
    return x * c + _swap16(x) * s


def _rope_t(d, c, s):
    return d * c + _swap16(d * s)


def _head_block_map(fn, x, cos, sin, *, name):
    T, W = x.shape
    tt = ROW_TILE

    def body(x_ref, c_ref, s_ref, o_ref):
        c, s = c_ref[...], s_ref[...]
        for h in range(W // LANES):
            lanes = slice(h * LANES, (h + 1) * LANES)
            o_ref[:, lanes] = fn(x_ref[:, lanes], c, s).astype(BF16)

    tab = pl.BlockSpec((tt, LANES), lambda i: (i, 0))
    blk = pl.BlockSpec((tt, W), lambda i: (i, 0))
    return pl.pallas_call(
        body, name=name, grid=(T // tt,), in_specs=[blk, tab, tab], out_specs=blk,
        out_shape=jax.ShapeDtypeStruct((T, W), BF16), compiler_params=_cparams(("parallel",)),
    )(x, cos, sin)


def _rope_q(q, cos, sin, *, name):
    scale = _attn_scale()
    return _head_block_map(lambda x, c, s: _rope(x, c, s) * scale, q, cos, sin, name=name)


def _rope_q_bwd(dq, cos, sin, *, name):
    return _head_block_map(_rope_t, dq, cos, sin, name=name)


def _key_blocks(kv, z, cos, sin, *, kpe_block, name):
    T = kv.shape[0]
    tt = ROW_TILE
    W = MLA_HEADS * LANES

    def body(kv_ref, z_ref, c_ref, s_ref, o_ref):
        kr = _rope(z_ref[...], c_ref[...], s_ref[...])
        for h in range(MLA_HEADS):
            lanes = slice(h * LANES, (h + 1) * LANES)
            o_ref[:, lanes] = (kv_ref[:, lanes].astype(F32) + kr).astype(BF16)

    tab = pl.BlockSpec((tt, LANES), lambda i: (i, 0))
    blk = pl.BlockSpec((tt, W), lambda i: (i, 0))
    return pl.pallas_call(
        body, name=name, grid=(T // tt,),
        in_specs=[blk, pl.BlockSpec((tt, LANES), lambda i: (i, kpe_block)), tab, tab], out_specs=blk,
        out_shape=jax.ShapeDtypeStruct((T, W), BF16), compiler_params=_cparams(("parallel",)),
    )(kv, z, cos, sin)


def _key_rope_bwd(dk, cos, sin, *, name):
    T = dk.shape[0]
    tt = ROW_TILE

    def body(d_ref, c_ref, s_ref, o_ref):
        d = d_ref[:, :LANES]
        for h in range(1, MLA_HEADS):
            d = d + d_ref[:, h * LANES:(h + 1) * LANES]
        lane = lax.broadcasted_iota(jnp.int32, d.shape, 1)
        d = jnp.where(jnp.logical_and(lane >= QK_NOPE, lane < QK_NOPE + QK_ROPE), d, 0.0)
        o_ref[...] = _rope_t(d, c_ref[...], s_ref[...]).astype(BF16)

    tab = pl.BlockSpec((tt, LANES), lambda i: (i, 0))
    return pl.pallas_call(
        body, name=name, grid=(T // tt,),
        in_specs=[pl.BlockSpec((tt, MLA_HEADS * LANES), lambda i: (i, 0)), tab, tab], out_specs=tab,
        out_shape=jax.ShapeDtypeStruct((T, LANES), BF16), compiler_params=_cparams(("parallel",)),
    )(dk, cos, sin)


ATT_BLOCK = 512


def _attn_scale():
    return float((QK_NOPE + QK_ROPE) ** -0.5)


def _causal_mask(qi, kj, tq, tk):
    row = qi * tq + lax.broadcasted_iota(jnp.int32, (tq, tk), 0)
    col = kj * tk + lax.broadcasted_iota(jnp.int32, (tq, tk), 1)
    return col <= row


def _pcall(body, *, name, grid, in_specs, out_specs, out_shape, args, scratch=(), sem=None, ride=None):
    n_in, n_out, n_scr = len(args), len(out_shape), len(scratch)
    if ride is None:
        return pl.pallas_call(
            body, name=name, grid=grid, in_specs=list(in_specs), out_specs=list(out_specs), out_shape=list(out_shape),
            scratch_shapes=list(scratch), compiler_params=_cparams(sem or ("arbitrary",) * len(grid)))(*args)
    ex, sink = ride
    o0 = n_in + len(ex.arrs)
    s0 = o0 + n_out + len(ex.out_shapes)

    def hosted(*refs):
        parts = (refs[n_in:o0], refs[o0 + n_out:s0], refs[-2], refs[-1])
        ids = [pl.program_id(i) for i in range(len(grid))]
        pl.when(functools.reduce(jnp.logical_and, [i == 0 for i in ids]))(lambda: ex.start(*parts))
        body(*refs[:n_in], *refs[o0:o0 + n_out], *refs[s0:s0 + n_scr])
        pl.when(functools.reduce(jnp.logical_and, [i == n - 1 for i, n in zip(ids, grid)]))(lambda: ex.finish(*parts))

    outs = pl.pallas_call(
        hosted, name=name, grid=grid, in_specs=list(in_specs) + ex.in_specs, out_specs=list(out_specs) + ex.out_specs,
        out_shape=list(out_shape) + ex.out_shapes, scratch_shapes=list(scratch) + ex.scratch,
        compiler_params=_cparams(("arbitrary",) * len(grid)))(*args, *ex.arrs)
    sink(outs[n_out:])
    return outs[:n_out]


PAIRS = MLA_HEADS // 2


def _own_lanes(x, first):
    lane = lax.broadcasted_iota(jnp.int32, x.shape, 1)
    return jnp.where((lane < V_HEAD) if first else (lane >= V_HEAD), x, 0.0)


def _attn_fwd(q, k, kv, *, B, S, v_block0, name, ride=None):
    tq = tk = min(ATT_BLOCK, S)
    nq = S // tq
    T = B * S

    def body(q_ref, k_ref, v_ref, o_ref, lse_ref):
        qi = pl.program_id(2)
        qs = (q_ref[:, :LANES], q_ref[:, LANES:])

        def step(masked):
            def f(j, carry):
                rows = pl.ds(pl.multiple_of(j * tk, tk), tk)
                vb = v_ref[rows, :]
                out = []
                for h in range(2):
                    m, l, acc = carry[h]
                    s = _dot_nt(qs[h], k_ref[rows, h * LANES:(h + 1) * LANES])
                    if masked:
                        s = jnp.where(_causal_mask(qi, j, tq, tk), s, -jnp.inf)
                    m_new = jnp.maximum(m, jnp.max(s, axis=-1, keepdims=True))
                    alpha = jnp.exp(m - m_new)
                    p = jnp.exp(s - m_new)
                    out.append((m_new, alpha * l + jnp.sum(p, axis=-1, keepdims=True), alpha * acc + _dot_nn(p, vb)))
                return tuple(out)
            return f

        one = (jnp.full((tq, 1), -1e30, F32), jnp.zeros((tq, 1), F32), jnp.zeros((tq, LANES), F32))
        (ma, la, acca), (mb, lb, accb) = step(True)(qi, lax.fori_loop(0, qi, step(False), (one, one)))
        o_ref[...] = _own_lanes(acca / la, True) + _own_lanes(accb / lb, False)
        lse_ref[0, 0] = ma + jnp.log(la)
        lse_ref[0, 1] = mb + jnp.log(lb)

    return _pcall(
        body, name=name, grid=(B, PAIRS, nq),
        in_specs=[pl.BlockSpec((tq, 2 * LANES), lambda b, g, i: (b * nq + i, g)),
                  pl.BlockSpec((S, 2 * LANES), lambda b, g, i: (b, g)),
                  pl.BlockSpec((S, LANES), lambda b, g, i: (b, v_block0 + g))],
        out_specs=[pl.BlockSpec((tq, LANES), lambda b, g, i: (b * nq + i, g)),
                   pl.BlockSpec((1, 2, tq, 1), lambda b, g, i: (b, g, i, 0))],
        out_shape=[jax.ShapeDtypeStruct((T, PAIRS * LANES), F32), jax.ShapeDtypeStruct((B, MLA_HEADS, S, 1), F32)],
        args=(q, k, kv), ride=ride)


def _attn_dq(q, k, kv, o, lse, do, *, B, S, v_block0, name, ride=None):
    tq = tk = min(ATT_BLOCK, S)
    nq = S // tq
    T = B * S
    scale = _attn_scale()

    def body(q_ref, k_ref, v_ref, o_ref, lse_ref, do_ref, dq_ref, delta_ref):
        qi = pl.program_id(2)
        qs = (q_ref[:, :LANES], q_ref[:, LANES:])
        dos = (_own_lanes(do_ref[...], True), _own_lanes(do_ref[...], False))
        deltas = tuple(jnp.sum(d * o_ref[...], axis=-1, keepdims=True) for d in dos)
        lses = (lse_ref[0, 0], lse_ref[0, 1])

        def step(masked):
            def f(j, carry):
                rows = pl.ds(pl.multiple_of(j * tk, tk), tk)
                vb = v_ref[rows, :]
                out = []
                for h in range(2):
                    kb = k_ref[rows, h * LANES:(h + 1) * LANES]
                    p = jnp.exp(_dot_nt(qs[h], kb) - lses[h])
                    if masked:
                        p = jnp.where(_causal_mask(qi, j, tq, tk), p, 0.0)
                    ds = p * (_dot_nt(dos[h], vb) - deltas[h])
                    out.append(carry[h] + _dot_nn(ds, kb))
                return tuple(out)
            return f

        zero = jnp.zeros((tq, LANES), F32)
        dqa, dqb = step(True)(qi, lax.fori_loop(0, qi, step(False), (zero, zero)))
        dq_ref[:, :LANES] = dqa * scale
        dq_ref[:, LANES:] = dqb * scale
        delta_ref[0, 0] = deltas[0]
        delta_ref[0, 1] = deltas[1]

    qrow = lambda w: pl.BlockSpec((tq, w), lambda b, g, i: (b * nq + i, g))
    stat = pl.BlockSpec((1, 2, tq, 1), lambda b, g, i: (b, g, i, 0))
    return _pcall(
        body, name=name, grid=(B, PAIRS, nq),
        in_specs=[qrow(2 * LANES), pl.BlockSpec((S, 2 * LANES), lambda b, g, i: (b, g)),
                  pl.BlockSpec((S, LANES), lambda b, g, i: (b, v_block0 + g)), qrow(LANES), stat, qrow(LANES)],
        out_specs=[qrow(2 * LANES), stat],
        out_shape=[jax.ShapeDtypeStruct((T, MLA_HEADS * LANES), F32), jax.ShapeDtypeStruct((B, MLA_HEADS, S, 1), F32)],
        args=(q, k, kv, o, lse, do), sem=("parallel", "parallel", "parallel"), ride=ride)


def _attn_dkv(q, k, kv, lse_rows, delta_rows, do, *, B, S, v_block0, name, ride=None):
    tq = tk = min(ATT_BLOCK, S)
    nq = S // tq
    T = B * S

    def body(q_ref, k_ref, v_ref, lse_ref, delta_ref, do_ref, dk_ref, dv_ref):
        kj = pl.program_id(2)
        ks = (k_ref[:, :LANES], k_ref[:, LANES:])
        vb = v_ref[...]

        def step(masked):
            def f(i, carry):
                rows = pl.ds(pl.multiple_of(i * tq, tq), tq)
                do_b = do_ref[rows, :]
                dks, dv = list(carry[:2]), carry[2]
                for h in range(2):
                    qb = q_ref[rows, h * LANES:(h + 1) * LANES]
                    doh = _own_lanes(do_b, h == 0)
                    pt = jnp.exp(_dot_nt(ks[h], qb) - lse_ref[0, h, pl.ds(i, 1), :])
                    if masked:
                        krow = kj * tk + lax.broadcasted_iota(jnp.int32, (tk, tq), 0)
                        qcol = i * tq + lax.broadcasted_iota(jnp.int32, (tk, tq), 1)
                        pt = jnp.where(krow <= qcol, pt, 0.0)
                    dst = pt * (_dot_nt(vb, doh) - delta_ref[0, h, pl.ds(i, 1), :])
                    dks[h] = dks[h] + _dot_nn(dst, qb)
                    dv = dv + _dot_nn(pt, doh)
                return dks[0], dks[1], dv
            return f

        zero = jnp.zeros((tk, LANES), F32)
        dka, dkb, dv = lax.fori_loop(kj + 1, nq, step(False), step(True)(kj, (zero, zero, zero)))
        dk_ref[:, :LANES] = dka
        dk_ref[:, LANES:] = dkb
        dv_ref[...] = dv

    krow = lambda w, c0: pl.BlockSpec((tk, w), lambda b, g, j: (b * nq + j, c0 + g))
    seq = lambda w: pl.BlockSpec((S, w), lambda b, g, j: (b, g))
    stat = pl.BlockSpec((1, 2, nq, tq), lambda b, g, j: (b, g, 0, 0))
    return _pcall(
        body, name=name, grid=(B, PAIRS, nq),
        in_specs=[seq(2 * LANES), krow(2 * LANES, 0), krow(LANES, v_block0), stat, stat, seq(LANES)],
        out_specs=[krow(2 * LANES, 0), krow(LANES, 0)],
        out_shape=[jax.ShapeDtypeStruct((T, MLA_HEADS * LANES), F32), jax.ShapeDtypeStruct((T, PAIRS * LANES), F32)],
        args=(q, k, kv, lse_rows, delta_rows, do), ride=ride)


def _lru_gates(xl, halo, cw_ref, cb_ref, wa_ref, ba_ref, wx_ref, bx_ref, lam_ref):
    xc = cb_ref[...] + cw_ref[3:4, :] * xl
    for kk in range(LRU_CONV - 1):
        xc = xc + cw_ref[kk:kk + 1, :] * _shift_rows(xl, LRU_CONV - 1 - kk, halo)
    r = _sigmoid(_dot_nn(xc, wa_ref[...]) + ba_ref[...])
    i = _sigmoid(_dot_nn(xc, wx_ref[...]) + bx_ref[...])
    lam = lam_ref[...]
    sp = jnp.maximum(-lam, 0.0) + jnp.log(1.0 + jnp.exp(-jnp.abs(lam)))
    a = jnp.exp(-LRU_C * r * sp)
    mult = jnp.sqrt(1.0 - a * a)
    return xc, r, i, sp, a, mult


def _lru_specs(tt, nt, S):
    def make(rev):
        tmap = (lambda t: nt - 1 - t) if rev else (lambda t: t)
        tile = lambda cb: pl.BlockSpec((tt, LRU_WIDTH), lambda b, t: (b * nt + tmap(t), cb))
        prev8 = lambda cb: pl.BlockSpec(
            (8, LRU_WIDTH), lambda b, t: (jnp.maximum((b * nt + tmap(t)) * (tt // 8) - 1, 0), cb))
        return tile, prev8, tmap
    return make


def _lru_fwd(z, cw, cb, wa, ba, wx, bx, lam, *, S, name, ride=None):
    T = z.shape[0]
    tt = min(ROW_TILE, S)
    nt = S // tt
    tile, prev8, _ = _lru_specs(tt, nt, S)(False)
    vec = lambda r: pl.BlockSpec((r, LRU_WIDTH), lambda b, t: (0, 0))
    mat = pl.BlockSpec((LRU_WIDTH, LRU_WIDTH), lambda b, t: (0, 0))

    def body(xl_ref, halo_ref, gate_ref, cw_ref, cb_ref, wa_ref, ba_ref, wx_ref, bx_ref, lam_ref,
             y_ref, h_ref, carry_ref):
        t = pl.program_id(1)
        first = t == 0
        halo = jnp.where(first, 0.0, halo_ref[...])
        xl_t = xl_ref[...]
        xc, r, i, sp, a, mult = _lru_gates(xl_t, halo, cw_ref, cb_ref, wa_ref, ba_ref, wx_ref, bx_ref, lam_ref)
        bv = mult * (i * xc)
        ones = jnp.ones((8, LRU_WIDTH), F32)
        zeros = jnp.zeros((8, LRU_WIDTH), F32)
        row = lax.broadcasted_iota(jnp.int32, (tt, LRU_WIDTH), 0)
        A = a
        d = 1
        while d < tt:
            if d < 8:
                a_sh = _shift_rows(A, d, ones)
                b_sh = _shift_rows(bv, d, zeros)
            else:
                a_sh = jnp.where(row < d, 1.0, pltpu.roll(A, d, 0))
                b_sh = jnp.where(row < d, 0.0, pltpu.roll(bv, d, 0))
            bv = A * b_sh + bv
            A = A * a_sh
            d *= 2
        h0 = jnp.where(first, 0.0, carry_ref[0:1, :])
        h = A * h0 + bv
        carry_ref[...] = jnp.broadcast_to(h[tt - 1:tt, :], (8, LRU_WIDTH))
        h_ref[...] = h
        y_ref[...] = (h * _gelu(gate_ref[...])).astype(BF16)

    return _pcall(
        body, name=name, grid=(T // S, nt),
        in_specs=[tile(0), prev8(0), tile(1), vec(LRU_CONV), vec(1), mat, vec(1), mat, vec(1), vec(1)],
        out_specs=[tile(0), tile(0)],
        out_shape=[jax.ShapeDtypeStruct((T, LRU_WIDTH), BF16), jax.ShapeDtypeStruct((T, LRU_WIDTH), F32)],
        args=(z, z, z, cw, cb, wa, ba, wx, bx, lam), scratch=[pltpu.VMEM((8, LRU_WIDTH), F32)], ride=ride)


def _lru_bwd(z, h, dy, cw, cb, wa, ba, wx, bx, lam, *, S, name):
    T = z.shape[0]
    tt = min(ROW_TILE, S)
    nt = S // tt
    tile, prev8, tmap = _lru_specs(tt, nt, S)(True)
    vec = lambda r: pl.BlockSpec((r, LRU_WIDTH), lambda b, t: (0, 0))
    mat = pl.BlockSpec((LRU_WIDTH, LRU_WIDTH), lambda b, t: (0, 0))

    def body(xl_ref, halo_ref, gate_ref, h_ref, hprev_ref, dy_ref, cw_ref, cb_ref, wa_ref, ba_ref, wx_ref,
             bx_ref, lam_ref, dxl_ref, dgate_ref, dcw_ref, dcb_ref, dwa_ref, dba_ref, dwx_ref, dbx_ref,
             dlam_ref, lamc_ref, ac_ref, dxc_ref):
        b = pl.program_id(0)
        t = pl.program_id(1)
        tr = nt - 1 - t
        seq_first = tr == 0
        seq_last = t == 0
        halo = jnp.where(seq_first, 0.0, halo_ref[...])
        xl_t = xl_ref[...]
        xc, r, i, sp, a, mult = _lru_gates(xl_t, halo, cw_ref, cb_ref, wa_ref, ba_ref, wx_ref, bx_ref, lam_ref)
        hh = h_ref[...]
        dyf = dy_ref[...].astype(F32)
        gl, dgl = _gelu_and_grad(gate_ref[...])
        dgate_ref[...] = (dyf * hh * dgl).astype(BF16)
        dh = dyf * gl

        a_first_later = jnp.where(seq_last, 0.0, ac_ref[...])
        lam_later = jnp.where(seq_last, 0.0, lamc_ref[...])
        row = lax.broadcasted_iota(jnp.int32, (tt, LRU_WIDTH), 0)
        A = _shift_rows_up(a, 1, a_first_later)
        lm = dh
        ones = jnp.ones((8, LRU_WIDTH), F32)
        zeros = jnp.zeros((8, LRU_WIDTH), F32)
        d = 1
        while d < tt:
            if d < 8:
                a_sh = _shift_rows_up(A, d, ones)
                l_sh = _shift_rows_up(lm, d, zeros)
            else:
                a_sh = jnp.where(row >= tt - d, 1.0, pltpu.roll(A, tt - d, 0))
                l_sh = jnp.where(row >= tt - d, 0.0, pltpu.roll(lm, tt - d, 0))
            lm = lm + A * l_sh
            A = A * a_sh
            d *= 2
        lm = lm + A * lam_later[0:1, :]
        lamc_ref[...] = jnp.broadcast_to(lm[0:1, :], (8, LRU_WIDTH))
        ac_ref[...] = jnp.broadcast_to(a[0:1, :], (8, LRU_WIDTH))

        hprev_halo = jnp.where(seq_first, 0.0, hprev_ref[...])
        h_prev = _shift_rows(hh, 1, hprev_halo)
        da = lm * h_prev
        ixc = i * xc
        dmult = lm * ixc
        di = lm * mult * xc
        dxc = lm * mult * i
        da = da - dmult * a / mult
        dlog = da * a
        dr = dlog * (-LRU_C) * sp
        dsp_part = jnp.sum(dlog * (-LRU_C) * r, axis=0, keepdims=True)
        dpa = dr * r * (1.0 - r)
        dpx = di * i * (1.0 - i)
        dxc = dxc + _dot_nt(dpa, wa_ref[...]) + _dot_nt(dpx, wx_ref[...])
        dwa_part = _dot_tn(xc, dpa)
        dwx_part = _dot_tn(xc, dpx)

        later = jnp.where(seq_last, 0.0, dxc_ref[...])
        dxl = cw_ref[3:4, :] * dxc
        for kk in range(LRU_CONV - 1):
            dxl = dxl + cw_ref[kk:kk + 1, :] * _shift_rows_up(dxc, LRU_CONV - 1 - kk, later)
        dxl_ref[...] = dxl.astype(BF16)
        dxc_ref[...] = dxc[0:8, :]
        dcw_rows = [jnp.sum(dxc * _shift_rows(xl_t, LRU_CONV - 1 - kk, halo), axis=0, keepdims=True)
                    for kk in range(LRU_CONV - 1)]
        dcw_rows.append(jnp.sum(dxc * xl_t, axis=0, keepdims=True))
        dcw_part = jnp.concatenate(dcw_rows + [jnp.zeros((8 - LRU_CONV, LRU_WIDTH), F32)], axis=0)
        lamv = lam_ref[...]
        dlam_part = dsp_part * (-_sigmoid(-lamv))
        parts = ((dcw_ref, dcw_part), (dcb_ref, jnp.sum(dxc, axis=0, keepdims=True)),
                 (dwa_ref, dwa_part), (dba_ref, jnp.sum(dpa, axis=0, keepdims=True)),
                 (dwx_ref, dwx_part), (dbx_ref, jnp.sum(dpx, axis=0, keepdims=True)),
                 (dlam_ref, dlam_part))
        start = jnp.logical_and(b == 0, t == 0)

        @pl.when(start)
        def _():
            for ref, val in parts:
                ref[...] = val

        @pl.when(jnp.logical_not(start))
        def _():
            for ref, val in parts:
                ref[...] += val

    acc = lambda r: pl.BlockSpec((r, LRU_WIDTH), lambda b, t: (0, 0))
    return pl.pallas_call(
        body, name=name, grid=(T // S, nt),
        in_specs=[tile(0), prev8(0), tile(1), tile(0), prev8(0), tile(0),
                  vec(LRU_CONV), vec(1), mat, vec(1), mat, vec(1), vec(1)],
        out_specs=[tile(0), tile(0), acc(8), acc(1), mat, acc(1), mat, acc(1), acc(1)],
        out_shape=[jax.ShapeDtypeStruct((T, LRU_WIDTH), BF16), jax.ShapeDtypeStruct((T, LRU_WIDTH), BF16),
                   jax.ShapeDtypeStruct((8, LRU_WIDTH), F32), jax.ShapeDtypeStruct((1, LRU_WIDTH), F32),
                   jax.ShapeDtypeStruct((LRU_WIDTH, LRU_WIDTH), F32), jax.ShapeDtypeStruct((1, LRU_WIDTH), F32),
                   jax.ShapeDtypeStruct((LRU_WIDTH, LRU_WIDTH), F32), jax.ShapeDtypeStruct((1, LRU_WIDTH), F32),
                   jax.ShapeDtypeStruct((1, LRU_WIDTH), F32)],
        scratch_shapes=[pltpu.VMEM((8, LRU_WIDTH), F32), pltpu.VMEM((8, LRU_WIDTH), F32),
                        pltpu.VMEM((8, LRU_WIDTH), F32)],
        compiler_params=_cparams(("arbitrary", "arbitrary")),
    )(z, z, z, h, h, dy, cw, cb, wa, ba, wx, bx, lam)


FFN_CT = 1408


def _ffn_conv(g, halo, cw_ref, cb_ref):
    gc = cb_ref[...] + cw_ref[2:3, :] * g
    for kk in range(FFN_CONV - 1):
        gc = gc + cw_ref[kk:kk + 1, :] * _shift_rows(g, FFN_CONV - 1 - kk, halo)
    return gc


def _ffn_act_fwd(g, u, cw, cb, *, S, name, ride=None):
    T, F = g.shape
    tt = min(ROW_TILE, S)
    nt = S // tt
    tc = _tile(F, FFN_CT)

    def body(g_ref, halo_ref, u_ref, cw_ref, cb_ref, o_ref):
        first = (pl.program_id(0) % nt) == 0
        halo = jnp.where(first, 0.0, halo_ref[...])
        gc = _ffn_conv(g_ref[...], halo, cw_ref, cb_ref)
        o_ref[...] = (_gelu(gc) * u_ref[...]).astype(BF16)

    tile = pl.BlockSpec((tt, tc), lambda i, j: (i, j))
    prev8 = pl.BlockSpec((8, tc), lambda i, j: (jnp.maximum(i * (tt // 8) - 1, 0), j))
    return _pcall(
        body, name=name, grid=(T // tt, F // tc),
        in_specs=[tile, prev8, tile, pl.BlockSpec((FFN_CONV, tc), lambda i, j: (0, j)),
                  pl.BlockSpec((1, tc), lambda i, j: (0, j))],
        out_specs=[tile], out_shape=[jax.ShapeDtypeStruct((T, F), BF16)], args=(g, g, u, cw, cb),
        sem=("parallel", "parallel"), ride=ride)[0]


def _ffn_act_bwd(g, u, dact, cw, cb, *, S, name, ride=None):
    T, F = g.shape
    tt = min(ROW_TILE, S)
    nt = S // tt
    ntt = T // tt
    tc = _tile(F, FFN_CT)

    def body(g_ref, halo_ref, u_ref, da_ref, cw_ref, cb_ref, dg_ref, du_ref, dcw_ref, dcb_ref, later_ref):
        step = pl.program_id(1)
        ti = (ntt - 1 - step) % nt
        halo = jnp.where(ti == 0, 0.0, halo_ref[...])
        gt = g_ref[...]
        gc = _ffn_conv(gt, halo, cw_ref, cb_ref)
        gl, dgl = _gelu_and_grad(gc)
        da = da_ref[...].astype(F32)
        du_ref[...] = (da * gl).astype(BF16)
        dgc = da * u_ref[...] * dgl
        later = jnp.where(ti == nt - 1, 0.0, later_ref[...])
        dg = cw_ref[2:3, :] * dgc
        for kk in range(FFN_CONV - 1):
            dg = dg + cw_ref[kk:kk + 1, :] * _shift_rows_up(dgc, FFN_CONV - 1 - kk, later)
        dg_ref[...] = dg.astype(BF16)
        later_ref[...] = dgc[0:8, :]
        rows = [jnp.sum(dgc * _shift_rows(gt, FFN_CONV - 1 - kk, halo), axis=0, keepdims=True)
                for kk in range(FFN_CONV - 1)]
        rows.append(jnp.sum(dgc * gt, axis=0, keepdims=True))
        dcw_part = jnp.concatenate(rows + [jnp.zeros((8 - FFN_CONV, tc), F32)], axis=0)
        dcb_part = jnp.sum(dgc, axis=0, keepdims=True)

        @pl.when(step == 0)
        def _():
            dcw_ref[...] = dcw_part
            dcb_ref[...] = dcb_part

        @pl.when(step > 0)
        def _():
            dcw_ref[...] += dcw_part
            dcb_ref[...] += dcb_part

    tile = pl.BlockSpec((tt, tc), lambda j, s: (ntt - 1 - s, j))
    prev8 = pl.BlockSpec((8, tc), lambda j, s: (jnp.maximum((ntt - 1 - s) * (tt // 8) - 1, 0), j))
    return _pcall(
        body, name=name, grid=(F // tc, ntt),
        in_specs=[tile, prev8, tile, tile, pl.BlockSpec((FFN_CONV, tc), lambda j, s: (0, j)),
                  pl.BlockSpec((1, tc), lambda j, s: (0, j))],
        out_specs=[tile, tile, pl.BlockSpec((8, tc), lambda j, s: (0, j)), pl.BlockSpec((1, tc), lambda j, s: (0, j))],
        out_shape=[jax.ShapeDtypeStruct((T, F), BF16), jax.ShapeDtypeStruct((T, F), BF16),
                   jax.ShapeDtypeStruct((8, F), F32), jax.ShapeDtypeStruct((1, F), F32)],
        args=(g, g, u, dact, cw, cb), scratch=[pltpu.VMEM((8, tc), F32)], ride=ride)


def _sgu_norm(zv, g_ref, b_ref):
    v = _gelu(zv)
    mu = jnp.mean(v, axis=-1, keepdims=True)
    xc = v - mu
    rstd = lax.rsqrt(jnp.mean(xc * xc, axis=-1, keepdims=True) + NORM_EPS)
    xhat = xc * rstd
    return xhat, rstd, xhat * g_ref[...] + b_ref[...]


def _sgu_fwd(zc, ln_g, ln_b, wm, bmap, *, name):
    T = zc.shape[0]
    W = SGU_WIDTH
    tt = ROW_TILE
    nch = tt // CHUNK

    def body(z_ref, g_ref, b_ref, wm_ref, bm_ref, p_ref):
        u = _gelu(z_ref[:, :W])
        _, _, vn = _sgu_norm(z_ref[:, W:], g_ref, b_ref)
        vn = vn.astype(BF16)
        for n in range(nch):
            rows = slice(n * CHUNK, (n + 1) * CHUNK)
            for gi in range(SGU_GROUPS):
                cols = slice(gi * LANES, (gi + 1) * LANES)
                s = _dot_nn(wm_ref[gi], vn[rows, cols]) + bm_ref[:, cols]
                p_ref[rows, cols] = (u[rows, cols] * s).astype(BF16)

    const2 = lambda r, c: pl.BlockSpec((r, c), lambda i: (0, 0))
    return pl.pallas_call(
        body, name=name, grid=(T // tt,),
        in_specs=[pl.BlockSpec((tt, 2 * W), lambda i: (i, 0)), const2(1, W), const2(1, W),
                  pl.BlockSpec((SGU_GROUPS, CHUNK, CHUNK), lambda i: (0, 0, 0)), const2(CHUNK, W)],
        out_specs=pl.BlockSpec((tt, W), lambda i: (i, 0)),
        out_shape=jax.ShapeDtypeStruct((T, W), BF16),
        compiler_params=_cparams(("parallel",)),
    )(zc, ln_g, ln_b, wm, bmap)


def _sgu_bwd(zc, dp, ln_g, ln_b, wm, bmap, *, name, ride=None):
    T = zc.shape[0]
    W = SGU_WIDTH
    tt = ROW_TILE
    nch = tt // CHUNK
    nsteps = T // tt

    def body(z_ref, dp_ref, g_ref, b_ref, wm_ref, bm_ref, dz_ref, dg_ref, db_ref, dwm_ref, dbm_ref,
             s_scr, dvn_scr):
        step = pl.program_id(0)
        zu = z_ref[:, :W]
        zv = z_ref[:, W:]
        u, dgu = _gelu_and_grad(zu)
        xhat, rstd, vn = _sgu_norm(zv, g_ref, b_ref)
        vnb = vn.astype(BF16)
        dpf = dp_ref[...].astype(F32)
        ds = dpf * u

        @pl.when(step == 0)
        def _():
            dwm_ref[...] = jnp.zeros_like(dwm_ref)
            dbm_ref[...] = jnp.zeros_like(dbm_ref)

        for n in range(nch):
            rows = slice(n * CHUNK, (n + 1) * CHUNK)
            for gi in range(SGU_GROUPS):
                cols = slice(gi * LANES, (gi + 1) * LANES)
                s_scr[rows, cols] = _dot_nn(wm_ref[gi], vnb[rows, cols]) + bm_ref[:, cols]
                dsb = ds[rows, cols]
                dvn_scr[rows, cols] = _dot_tn(wm_ref[gi], dsb)
                dwm_ref[gi] += _dot_nt(dsb, vnb[rows, cols])
                dbm_ref[:, cols] += dsb
        dz_ref[:, :W] = (dpf * s_scr[...] * dgu).astype(BF16)
        dvn = dvn_scr[...]
        dxhat = dvn * g_ref[...]
        dv = rstd * (dxhat - jnp.mean(dxhat, axis=-1, keepdims=True)
                     - xhat * jnp.mean(dxhat * xhat, axis=-1, keepdims=True))
        _, dgv = _gelu_and_grad(zv)
        dz_ref[:, W:] = (dv * dgv).astype(BF16)
        dg_part = jnp.sum(dvn * xhat, axis=0, keepdims=True)
        db_part = jnp.sum(dvn, axis=0, keepdims=True)

        @pl.when(step == 0)
        def _():
            dg_ref[...] = dg_part
            db_ref[...] = db_part

        @pl.when(step > 0)
        def _():
            dg_ref[...] += dg_part
            db_ref[...] += db_part

        @pl.when(step == nsteps - 1)
        def _():
            for gi in range(SGU_GROUPS):
                cols = slice(gi * LANES, (gi + 1) * LANES)
                tot = jnp.sum(dbm_ref[:, cols], axis=1, keepdims=True)
                dbm_ref[:, cols] = jnp.broadcast_to(tot, (CHUNK, LANES))

    const2 = lambda r, c: pl.BlockSpec((r, c), lambda i: (0, 0))
    wspec = pl.BlockSpec((SGU_GROUPS, CHUNK, CHUNK), lambda i: (0, 0, 0))
    return _pcall(
        body, name=name, grid=(nsteps,),
        in_specs=[pl.BlockSpec((tt, 2 * W), lambda i: (i, 0)), pl.BlockSpec((tt, W), lambda i: (i, 0)),
                  const2(1, W), const2(1, W), wspec, const2(CHUNK, W)],
        out_specs=[pl.BlockSpec((tt, 2 * W), lambda i: (i, 0)), const2(1, W), const2(1, W), wspec, const2(CHUNK, W)],
        out_shape=[jax.ShapeDtypeStruct((T, 2 * W), BF16), jax.ShapeDtypeStruct((1, W), F32),
                   jax.ShapeDtypeStruct((1, W), F32), jax.ShapeDtypeStruct((SGU_GROUPS, CHUNK, CHUNK), F32),
                   jax.ShapeDtypeStruct((CHUNK, W), F32)],
        args=(zc, dp, ln_g, ln_b, wm, bmap), scratch=[pltpu.VMEM((tt, W), F32), pltpu.VMEM((tt, W), F32)], ride=ride)


def _rope_tables(positions):
    half = QK_ROPE // 2
    inv_freq = jnp.exp(-math.log(ROPE_BASE) * jnp.arange(half, dtype=F32) / half)
    ang = positions.reshape(-1).astype(F32)[:, None] * inv_freq
    cos = jnp.cos(ang)
    sin = jnp.sin(ang)
    n = ang.shape[0]
    tail = LANES - QK_NOPE - QK_ROPE
    cos_t = jnp.concatenate([jnp.ones((n, QK_NOPE), F32), cos, cos, jnp.ones((n, tail), F32)], axis=1)
    sin_t = jnp.concatenate([jnp.zeros((n, QK_NOPE), F32), -sin, sin, jnp.zeros((n, tail), F32)], axis=1)
    return cos_t, sin_t


SGU_GROUP_DIM = SGU_WIDTH // SGU_GROUPS
_O1, _O2, _O3, _O4 = Q_LORA, Q_LORA + KV_LORA, Q_LORA + KV_LORA + QK_ROPE, Q_LORA + KV_LORA + QK_ROPE + LRU_WIDTH
_A0, _A1, _A2 = 2 * LRU_WIDTH, 2 * LRU_WIDTH + Q_LORA, 2 * LRU_WIDTH + Q_LORA + KV_LORA
_A3 = _A2 + QK_NOPE
Z_Q_BLOCK, Z_KV_BLOCK, Z_KPE_BLOCK = _A0 // Q_LORA, _A1 // KV_LORA, _A2 // LANES


def _perm_w_in(w_in):
    zeros = lambda n: jnp.zeros((w_in.shape[0], n), w_in.dtype)
    return jnp.concatenate([w_in[:, _O3:_O4], w_in[:, _O4:], w_in[:, :_O1], w_in[:, _O1:_O2], zeros(QK_NOPE),
                            w_in[:, _O2:_O3], zeros(LANES - QK_NOPE - QK_ROPE)], axis=1)


def _unperm_w_in(w):
    return jnp.concatenate([w[:, _A0:_A1], w[:, _A1:_A2], w[:, _A3:_A3 + QK_ROPE], w[:, :LRU_WIDTH],
                            w[:, LRU_WIDTH:_A0]], axis=1)


def _head_blocks(w, d):
    r = w.shape[0]
    return jnp.pad(w.reshape(r, MLA_HEADS, d), ((0, 0), (0, 0), (0, LANES - d))).reshape(r, MLA_HEADS * LANES)


def _from_head_blocks(w, d):
    r = w.shape[0]
    return w.reshape(r, MLA_HEADS, LANES)[:, :, :d].reshape(r, MLA_HEADS * d)


def _split_kv(w_kv):
    r = w_kv.shape[0]
    w3 = w_kv.reshape(r, MLA_HEADS, QK_NOPE + V_HEAD)
    return _head_blocks(w3[:, :, :QK_NOPE].reshape(r, -1), QK_NOPE), w3[:, :, QK_NOPE:].reshape(r, -1)


def _join_kv(w_k, w_v):
    r = w_k.shape[0]
    return jnp.concatenate([_from_head_blocks(w_k, QK_NOPE).reshape(r, MLA_HEADS, QK_NOPE),
                            w_v.reshape(r, MLA_HEADS, V_HEAD)], axis=2).reshape(r, -1)


def _prep_small(w):
    p = {n: w[n] for n in w if n not in BIG}
    eye = jnp.eye(LRU_HEADS, dtype=F32)
    dense = lambda wg: (wg[:, :, None, :] * eye[:, None, :, None]).reshape(LRU_WIDTH, LRU_WIDTH).astype(BF16)
    p["wa_d"] = dense(w["ab_w_rg_a"][0])
    p["wx_d"] = dense(w["ab_w_rg_x"][0])
    causal = jnp.tril(jnp.ones((CHUNK, CHUNK), F32))
    p["wm"] = (w["c_w_s"][0] * causal).astype(BF16)
    p["bmap"] = jnp.repeat(w["c_b_s"][0].T, SGU_GROUP_DIM, axis=1)
    return p


def _prep_big(ab_w_in, ab_w_q_b, ab_w_kv_b):
    return {"w_in_p": _perm_w_in(ab_w_in).astype(BF16),
            "w_q_p": _head_blocks(ab_w_q_b, QK_NOPE + QK_ROPE).astype(BF16),
            "w_kv_p": jnp.concatenate(_split_kv(ab_w_kv_b), axis=1).astype(BF16)}


def _ffn_fwd(h, l, p, S, rides):
    hn = _rms_fwd(h, p["ffn_norm"][l], name=f"ffn{l}_norm")
    g = _mm(hn, p["ffn_gate_t"][l], tb=True, name=f"ffn{l}_gate", ride=rides.get(f"ffn{l}_gate"))
    u = _mm(hn, p["ffn_up_t"][l], tb=True, name=f"ffn{l}_up", ride=rides.get(f"ffn{l}_up"))
    act = _ffn_act_fwd(g, u, p["ffn_conv_w"][l], p["ffn_conv_b"][l][None], S=S, name=f"ffn{l}_act",
                       ride=rides.get(f"ffn{l}_act"))
    out = _mm(act, p["ffn_down"][l], res=h, name=f"ffn{l}_down", ride=rides.get(f"ffn{l}_down"))
    return out, (hn, g, u, act)


def _ffn_bwd(dh, h_in, l, p, saved, S, rides, grads_ready, also_ready=None):
    hn, g, u, act = saved
    dact = _mm(dh, p["ffn_down"][l], tb=True, out_dtype=BF16, name=f"ffn{l}_dact", ride=rides.get(f"ffn{l}_dact"))
    dw_down = _mm(act, dh, ta=True, out_dtype=BF16, name=f"ffn{l}_dwdown")
    dg, du, dcw, dcb = _ffn_act_bwd(g, u, dact, p["ffn_conv_w"][l], p["ffn_conv_b"][l][None], S=S,
                                    name=f"ffn{l}_dactbwd", ride=rides.get(f"ffn{l}_dactbwd"))
    dhn = _mm(dg, p["ffn_gate_t"][l], name=f"ffn{l}_dhn_g")
    dhn = _mm(du, p["ffn_up_t"][l], res=dhn, name=f"ffn{l}_dhn_u")
    dw_gate_t = _mm(dg, hn, ta=True, out_dtype=BF16, name=f"ffn{l}_dwgate")
    dw_up_t = _mm(du, hn, ta=True, out_dtype=BF16, name=f"ffn{l}_dwup")
    grads_ready(l, {**(also_ready or {}), "ffn_gate_t": dw_gate_t, "ffn_up_t": dw_up_t, "ffn_down": dw_down})
    dh_in, dnorm = _rms_bwd(h_in, p["ffn_norm"][l], dhn, res=dh, name=f"ffn{l}_dnorm", ride=rides.get(f"ffn{l}_dnorm"))
    grads = dict(ffn_norm=dnorm[0], ffn_gate_t=dw_gate_t, ffn_up_t=dw_up_t, ffn_conv_w=dcw[:FFN_CONV],
                 ffn_conv_b=dcb[0], ffn_down=dw_down)
    return dh_in, grads


def _local_step(x, positions, target, p, rides=None, grads_ready=None):
    rides = {} if rides is None else rides
    grads_ready = grads_ready or (lambda layer, ready: None)
    B, S, D = x.shape
    T = B * S
    H = MLA_HEADS
    xf = x.reshape(T, D)
    tgt = target.reshape(T, D)
    cos, sin = _rope_tables(positions)

    hn0 = _rms_fwd(xf, p["ab_norm"][0], name="ab_norm")
    z = _mm(hn0, p["w_in_p"], name="ab_in")
    cqn = _rms_fwd(z, p["ab_q_norm"][0], cb=Z_Q_BLOCK, name="q_norm")
    ckvn = _rms_fwd(z, p["ab_kv_norm"][0], cb=Z_KV_BLOCK, name="kv_norm")
    q = _mm(cqn, p["w_q_p"], name="q_up")
    kv = _mm(ckvn, p["w_kv_p"], out_dtype=BF16, name="kv_up")
    qs = _rope_q(q, cos, sin, name="q_rope")
    kk = _key_blocks(kv, z, cos, sin, kpe_block=Z_KPE_BLOCK, name="k_rope")
    att = dict(B=B, S=S, v_block0=H)
    o, lse = _attn_fwd(qs, kk, kv, name="attn_fwd", ride=rides.get("attn_fwd"), **att)
    lru_par = (p["ab_conv_w"][0], p["ab_conv_b"], p["wa_d"], p["ab_b_rg_a"], p["wx_d"], p["ab_b_rg_x"], p["ab_lambda"])
    y_lru, hs = _lru_fwd(z, *lru_par, S=S, name="lru_fwd", ride=rides.get("lru_fwd"))
    n_att = H * V_HEAD
    w_out_a, w_out_b = p["ab_w_out"][:n_att], p["ab_w_out"][n_att:]
    h1 = _mm(y_lru, w_out_b, res=_mm(o, w_out_a, res=xf, name="ab_out_a"), name="ab_out_b")
    h2, ffn0 = _ffn_fwd(h1, 0, p, S, rides)

    hn2 = _rms_fwd(h2, p["c_norm"][0], name="c_norm")
    zc = _mm(hn2, p["c_w_in_t"], tb=True, name="c_in")
    pg = _sgu_fwd(zc, p["c_ln_g"], p["c_ln_b"], p["wm"], p["bmap"], name="sgu_fwd")
    h3 = _mm(pg, p["c_w_out"], res=h2, name="c_out")
    h4, ffn1 = _ffn_fwd(h3, 1, p, S, rides)

    loss_row, dh4, dfinal = _final_fwd_bwd(h4, p["final_norm"], tgt, name="final")

    dh3, g_ffn1 = _ffn_bwd(dh4, h3, 1, p, ffn1, S, rides, grads_ready)
    dpg = _mm(dh3, p["c_w_out"], tb=True, out_dtype=BF16, name="c_dp")
    dw_c_out = _mm(pg, dh3, ta=True, out_dtype=BF16, name="c_dwout")
    dzc, dlng, dlnb, dwm, dbm = _sgu_bwd(zc, dpg, p["c_ln_g"], p["c_ln_b"], p["wm"], p["bmap"], name="sgu_bwd",
                                         ride=rides.get("sgu_bwd"))
    dhn2 = _mm(dzc, p["c_w_in_t"], name="c_dhn")
    dw_c_in_t = _mm(dzc, hn2, ta=True, out_dtype=BF16, name="c_dwin")
    dh2, dcnorm = _rms_bwd(h2, p["c_norm"][0], dhn2, res=dh3, name="c_dnorm")
    dh1, g_ffn0 = _ffn_bwd(dh2, h1, 0, p, ffn0, S, rides, grads_ready, {"c_w_in_t": dw_c_in_t, "c_w_out": dw_c_out})

    do = _mm(dh1, w_out_a, tb=True, name="ab_do")
    dy_lru = _mm(dh1, w_out_b, tb=True, out_dtype=BF16, name="ab_dylru")
    dw_out = jnp.concatenate([_mm(o, dh1, ta=True, out_dtype=BF16, name="ab_dwout_a"),
                              _mm(y_lru, dh1, ta=True, out_dtype=BF16, name="ab_dwout_b")], axis=0)
    dq, delta = _attn_dq(qs, kk, kv, o, lse, do, name="attn_dq", ride=rides.get("attn_dq"), **att)
    nq = S // min(ATT_BLOCK, S)
    rows = lambda a: a.reshape(B, H, nq, S // nq)
    dk, dv = _attn_dkv(qs, kk, kv, rows(lse), rows(delta), do, name="attn_dkv", ride=rides.get("attn_dkv"), **att)
    dq_full = _rope_q_bwd(dq, cos, sin, name="q_rope_bwd")
    dkr = _key_rope_bwd(dk, cos, sin, name="k_rope_bwd")
    n_key = H * LANES
    w_k_p, w_v_p = p["w_kv_p"][:, :n_key], p["w_kv_p"][:, n_key:]
    dcqn = _mm(dq_full, p["w_q_p"], tb=True, name="q_dlat")
    dw_q_p = _mm(cqn, dq_full, ta=True, out_dtype=BF16, name="q_dw")
    dckvn = _mm(dv, w_v_p, tb=True, res=_mm(dk, w_k_p, tb=True, name="k_dlat"), name="v_dlat")
    dw_k_p = _mm(ckvn, dk, ta=True, out_dtype=BF16, name="k_dw")
    dw_v_p = _mm(ckvn, dv, ta=True, out_dtype=BF16, name="v_dw")
    dcq, dqnorm = _rms_bwd(z, p["ab_q_norm"][0], dcqn, cb=Z_Q_BLOCK, out_dtype=BF16, name="q_dnorm")
    dckv, dkvnorm = _rms_bwd(z, p["ab_kv_norm"][0], dckvn, cb=Z_KV_BLOCK, out_dtype=BF16, name="kv_dnorm")
    dxl, dgate, dcw, dcb, dwa, dba, dwx, dbx, dlam = _lru_bwd(z, hs, dy_lru, *lru_par, S=S, name="lru_bwd")
    dz = jnp.concatenate([dxl, dgate, dcq, dckv, dkr], axis=1)
    dhn0 = _mm(dz, p["w_in_p"], tb=True, name="ab_dhn")
    dw_in_p = _mm(hn0, dz, ta=True, out_dtype=BF16, name="ab_dwin")
    dx, dabnorm = _rms_bwd(xf, p["ab_norm"][0], dhn0, res=dh1, name="ab_dnorm")

    blocks = lambda dd: jnp.stack([dd[i * LRU_BLOCK:(i + 1) * LRU_BLOCK, i * LRU_BLOCK:(i + 1) * LRU_BLOCK]
                                   for i in range(LRU_HEADS)])
    causal = jnp.tril(jnp.ones((CHUNK, CHUNK), F32))
    grads = {
        "ab_norm": dabnorm, "w_in_p": dw_in_p, "ab_q_norm": dqnorm, "w_q_p": dw_q_p,
        "ab_kv_norm": dkvnorm, "w_k_p": dw_k_p, "w_v_p": dw_v_p, "ab_conv_w": dcw[:LRU_CONV][None], "ab_conv_b": dcb,
        "ab_w_rg_a": blocks(dwa)[None], "ab_b_rg_a": dba, "ab_w_rg_x": blocks(dwx)[None], "ab_b_rg_x": dbx,
        "ab_lambda": dlam, "ab_w_out": dw_out,
        "c_norm": dcnorm, "c_w_in_t": dw_c_in_t, "c_ln_g": dlng, "c_ln_b": dlnb,
        "c_w_s": (dwm * causal)[None], "c_b_s": dbm[:, ::SGU_GROUP_DIM].T[None], "c_w_out": dw_c_out,
        "final_norm": dfinal[0],
    }
    for name in ("ffn_norm", "ffn_conv_w", "ffn_conv_b"):
        grads[name] = jnp.stack([g_ffn0[name], g_ffn1[name]])
    for name in ("ffn_gate_t", "ffn_up_t", "ffn_down"):
        grads[name] = [g_ffn0[name], g_ffn1[name]]
    return loss_row, dx.reshape(B, S, D), grads


ANY = pl.BlockSpec(memory_space=pl.ANY)


def _place():
    x, y, c = lax.axis_index("x"), lax.axis_index("y"), lax.axis_index("c")
    chips = [(1 - x, y), (x, 1 - y), (1 - x, 1 - y)]
    return x, y, c, 2 * x + y, (x, y, 1 - c), chips


def _remote(src, dst, send_sems, recv_sems, k, to):
    return pltpu.make_async_remote_copy(src_ref=src, dst_ref=dst, send_sem=send_sems.at[k], recv_sem=recv_sems.at[k],
                                        device_id=to, device_id_type=MESH)


class _Exchange:
    def __init__(self, arrs, out_shapes, n_sems, start, finish):
        self.arrs, self.out_shapes, self.n_sems, self.start, self.finish = list(arrs), out_shapes, n_sems, start, finish

    @property
    def in_specs(self):
        return [ANY] * len(self.arrs)

    @property
    def out_specs(self):
        return [ANY] * len(self.out_shapes)

    @property
    def scratch(self):
        return [pltpu.SemaphoreType.DMA((self.n_sems,)), pltpu.SemaphoreType.DMA((self.n_sems,))]

    def split(self, refs):
        n = len(self.arrs)
        return refs[:n], refs[n:n + len(self.out_shapes)], refs[-2], refs[-1]

    def run(self, name):
        def body(*refs):
            parts = self.split(refs)
            self.start(*parts)
            self.finish(*parts)

        return pl.pallas_call(body, name=name, in_specs=self.in_specs, out_specs=self.out_specs,
                              out_shape=self.out_shapes, scratch_shapes=self.scratch)(*self.arrs)


def _put(buf, piece, idx, axis):
    return lax.dynamic_update_slice_in_dim(buf, jnp.expand_dims(piece, axis).astype(buf.dtype), idx, axis)


def _all_gather(arrs):
    n = len(arrs)

    def start(ins, outs, send_sems, recv_sems):
        x, y, c, j, sib, chips = _place()
        for i in range(n):
            for k, (cx, cy) in enumerate(chips):
                _remote(ins[i].at[:, c], outs[i].at[:, j, c], send_sems, recv_sems, 6 * i + k, (cx, cy, c)).start()

    def finish(ins, outs, send_sems, recv_sems):
        x, y, c, j, sib, chips = _place()
        passed = []
        for i in range(n):
            for k, (cx, cy) in enumerate(chips):
                got = outs[i].at[:, 2 * cx + cy, c]
                _remote(got, got, send_sems, recv_sems, 6 * i + k, (cx, cy, c)).wait_recv()
                cp = _remote(got, got, send_sems, recv_sems, 6 * i + 3 + k, sib)
                cp.start()
                passed.append(cp)
        for i in range(n):
            for k, (cx, cy) in enumerate(chips):
                got = outs[i].at[:, 2 * cx + cy, 1 - c]
                _remote(got, got, send_sems, recv_sems, 6 * i + 3 + k, sib).wait_recv()
                _remote(ins[i].at[:, c], ins[i].at[:, c], send_sems, recv_sems, 6 * i + k, sib).wait_send()
        for cp in passed:
            cp.wait_send()

    shapes = [jax.ShapeDtypeStruct((a.shape[0], N_CHIPS) + a.shape[1:], a.dtype) for a in arrs]
    return _Exchange(arrs, shapes, 6 * n, start, finish)


class _Offset:
    def __init__(self, sems, k0):
        self.sems, self.k0 = sems, k0

    @property
    def at(self):
        return self

    def __getitem__(self, k):
        return self.sems.at[self.k0 + k]


def _merge(a, b):
    n_in, n_out = len(a.arrs), len(a.out_shapes)

    def both(fa, fb):
        def f(ins, outs, send_sems, recv_sems):
            fa(ins[:n_in], outs[:n_out], send_sems, recv_sems)
            fb(ins[n_in:], outs[n_out:], _Offset(send_sems, a.n_sems), _Offset(recv_sems, a.n_sems))
        return f

    return _Exchange(a.arrs + b.arrs, a.out_shapes + b.out_shapes, a.n_sems + b.n_sems,
                     both(a.start, b.start), both(a.finish, b.finish))


def _pair_swap(arrs):
    n = len(arrs)

    def start(ins, outs, send_sems, recv_sems):
        x, y, c, j, sib, chips = _place()
        for i in range(n):
            _remote(ins[i].at[:, 1 - c], outs[i], send_sems, recv_sems, i, sib).start()

    def finish(ins, outs, send_sems, recv_sems):
        x, y, c, j, sib, chips = _place()
        for i in range(n):
            _remote(ins[i].at[:, 1 - c], outs[i], send_sems, recv_sems, i, sib).wait()

    shapes = [jax.ShapeDtypeStruct((a.shape[0],) + a.shape[2:], a.dtype) for a in arrs]
    return _Exchange(arrs, shapes, n, start, finish)


def _pair_send(arrs):
    n = len(arrs)

    def start(ins, outs, send_sems, recv_sems):
        x, y, c, j, sib, chips = _place()
        for i in range(n):
            _remote(ins[i], outs[i], send_sems, recv_sems, i, sib).start()

    def finish(ins, outs, send_sems, recv_sems):
        x, y, c, j, sib, chips = _place()
        for i in range(n):
            _remote(ins[i], outs[i], send_sems, recv_sems, i, sib).wait()

    shapes = [jax.ShapeDtypeStruct(a.shape, a.dtype) for a in arrs]
    return _Exchange(arrs, shapes, n, start, finish)


def _chip_exchange(arrs, *, scatter):
    n = len(arrs)

    def copies(ins, outs, send_sems, recv_sems):
        x, y, c, j, sib, chips = _place()
        return [(_remote(ins[i].at[2 * cx + cy] if scatter else ins[i], outs[i].at[j], send_sems, recv_sems,
                         3 * i + k, (cx, cy, c)),
                 _remote(outs[i].at[2 * cx + cy], outs[i].at[2 * cx + cy], send_sems, recv_sems, 3 * i + k, (cx, cy, c)))
                for i in range(n) for k, (cx, cy) in enumerate(chips)]

    def start(*refs):
        for out, _ in copies(*refs):
            out.start()

    def finish(*refs):
        for out, back in copies(*refs):
            back.wait_recv()
            out.wait_send()

    shapes = [jax.ShapeDtypeStruct((N_CHIPS,) + a.shape[-2:], a.dtype) for a in arrs]
    return _Exchange(arrs, shapes, 3 * n, start, finish)


FLAT_ROWS = 512


def _add2(a, b, *, out_dtype, name):
    n, R, L = a.shape
    tr = _tile(R, FLAT_ROWS, 16)

    def body(a_ref, b_ref, o_ref):
        o_ref[...] = (a_ref[...].astype(F32) + b_ref[...].astype(F32)).astype(out_dtype)

    spec = pl.BlockSpec((n, tr, L), lambda i: (0, i, 0))
    return pl.pallas_call(
        body, name=name, grid=(R // tr,), in_specs=[spec, spec], out_specs=spec,
        out_shape=jax.ShapeDtypeStruct(a.shape, out_dtype), compiler_params=_cparams(("parallel",)),
    )(a, b)


def _sum_slots(buf, *, name):
    n, R, L = buf.shape
    tr = _tile(R, FLAT_ROWS, 16)

    def body(b_ref, o_ref):
        acc = b_ref[0].astype(F32)
        for k in range(1, n):
            acc = acc + b_ref[k].astype(F32)
        o_ref[...] = acc

    return pl.pallas_call(
        body, name=name, grid=(R // tr,), in_specs=[pl.BlockSpec((n, tr, L), lambda i: (0, i, 0))],
        out_specs=pl.BlockSpec((tr, L), lambda i: (i, 0)),
        out_shape=jax.ShapeDtypeStruct((R, L), F32), compiler_params=_cparams(("parallel",)),
    )(buf)


def _adamw_update(w, g, m, v):
    c1 = 1.0 - ADAM_B1 ** ADAM_STEP
    c2 = 1.0 - ADAM_B2 ** ADAM_STEP
    m = ADAM_B1 * m + (1.0 - ADAM_B1) * g
    v = ADAM_B2 * v + (1.0 - ADAM_B2) * (g * g)
    return -ADAM_LR * ((m / c1) / (jnp.sqrt(v / c2) + ADAM_EPS) + ADAM_WD * w), m, v


def _adamw_halves(w, m, v, own, other, *, name):
    NL, R, L = w.shape
    h = R // 2
    tr = _tile(h, FLAT_ROWS, 16)
    nt = h // tr

    def body(*refs):
        w_ref, m_ref, v_ref = refs[:3]
        own_refs, other_refs = refs[3:3 + NL], refs[3 + NL:3 + 2 * NL]
        d_ref, nm_ref, nv_ref, g_ref = refs[3 + 2 * NL:]
        layer, half = pl.program_id(0), pl.program_id(1)
        mine = half == lax.axis_index("c")
        g = jnp.where(mine, own_refs[0][...], other_refs[0][...])
        for l in range(1, NL):
            g = jnp.where(layer == l, jnp.where(mine, own_refs[l][...], other_refs[l][...]), g)
        d, mm, vv = _adamw_update(w_ref[0], g, m_ref[0], v_ref[0])
        d_ref[0], nm_ref[0], nv_ref[0], g_ref[0] = d, mm, vv, g

    spec = pl.BlockSpec((1, tr, L), lambda l, hh, i: (l, hh * nt + i, 0))
    part = pl.BlockSpec((tr, L), lambda l, hh, i: (i, 0))
    sh = jax.ShapeDtypeStruct((NL, R, L), F32)
    return pl.pallas_call(
        body, name=name, grid=(NL, 2, nt), in_specs=[spec] * 3 + [part] * (2 * NL), out_specs=[spec] * 4,
        out_shape=[sh] * 4, compiler_params=_cparams(("parallel", "parallel", "parallel")),
    )(w, m, v, *own, *other)


def _adamw(w, g, m, v, *, name):
    NL, R, L = w.shape
    tr = _tile(R, FLAT_ROWS, 16)

    def body(w_ref, g_ref, m_ref, v_ref, d_ref, nm_ref, nv_ref):
        d_ref[...], nm_ref[...], nv_ref[...] = _adamw_update(w_ref[...], g_ref[...], m_ref[...], v_ref[...])

    spec = pl.BlockSpec((1, tr, L), lambda l, i: (l, i, 0))
    sh = jax.ShapeDtypeStruct((NL, R, L), F32)
    return pl.pallas_call(
        body, name=name, grid=(NL, R // tr), in_specs=[spec] * 4, out_specs=[spec] * 3, out_shape=[sh] * 3,
        compiler_params=_cparams(("parallel", "parallel")),
    )(w, g, m, v)


WEIGHT_NAMES = ["ab_norm", "ab_w_in", "ab_q_norm", "ab_w_q_b", "ab_kv_norm", "ab_w_kv_b", "ab_conv_w", "ab_conv_b",
                "ab_w_rg_a", "ab_b_rg_a", "ab_w_rg_x", "ab_b_rg_x", "ab_lambda", "ab_w_out", "c_norm", "c_w_in",
                "c_ln_g", "c_ln_b", "c_w_s", "c_b_s", "c_w_out", "ffn_norm", "ffn_w_gate", "ffn_w_up", "ffn_conv_w",
                "ffn_conv_b", "ffn_w_down", "final_norm"]
BIG = {"ab_w_in": 2, "ab_w_q_b": 2, "ab_w_kv_b": 2, "ab_w_out": 1, "c_w_in": 2, "c_w_out": 1,
       "ffn_w_gate": 2, "ffn_w_up": 2, "ffn_w_down": 1}
SMALL_SHARDED = {"ab_conv_w": 2, "c_norm": 1, "c_ln_g": 1, "c_ln_b": 1, "ffn_conv_w": 2}
SMALL_REPLICATED = [n for n in WEIGHT_NAMES if n not in BIG and n not in SMALL_SHARDED]


def _rows(n_elems, mult):
    r = -(-n_elems // LANES)
    return -(-r // mult) * mult


def _flat(parts, rows):
    flat = jnp.concatenate([a.reshape(-1) for a in parts])
    return jnp.pad(flat, (0, rows * LANES - flat.shape[0])).reshape(rows, LANES)


def _unflat(flat, shapes):
    flat = flat.reshape(-1)
    out, off = [], 0
    for s in shapes:
        n = math.prod(s)
        out.append(flat[off:off + n].reshape(s))
        off += n
    return out


def _join_shards(a, axis):
    a = jnp.moveaxis(a, 0, axis)
    return a.reshape(a.shape[:axis] + (a.shape[axis] * a.shape[axis + 1],) + a.shape[axis + 2:])


def kernel(x, positions, ab_norm, ab_w_in, ab_q_norm, ab_w_q_b, ab_kv_norm, ab_w_kv_b, ab_conv_w, ab_conv_b, ab_w_rg_a, ab_b_rg_a, ab_w_rg_x, ab_b_rg_x, ab_lambda, ab_w_out, c_norm, c_w_in, c_ln_g, c_ln_b, c_w_s, c_b_s, c_w_out, ffn_norm, ffn_w_gate, ffn_w_up, ffn_conv_w, ffn_conv_b, ffn_w_down, final_norm, loss_target, m_ab_norm, m_ab_w_in, m_ab_q_norm, m_ab_w_q_b, m_ab_kv_norm, m_ab_w_kv_b, m_ab_conv_w, m_ab_conv_b, m_ab_w_rg_a, m_ab_b_rg_a, m_ab_w_rg_x, m_ab_b_rg_x, m_ab_lambda, m_ab_w_out, m_c_norm, m_c_w_in, m_c_ln_g, m_c_ln_b, m_c_w_s, m_c_b_s, m_c_w_out, m_ffn_norm, m_ffn_w_gate, m_ffn_w_up, m_ffn_conv_w, m_ffn_conv_b, m_ffn_w_down, m_final_norm, v_ab_norm, v_ab_w_in, v_ab_q_norm, v_ab_w_q_b, v_ab_kv_norm, v_ab_w_kv_b, v_ab_conv_w, v_ab_conv_b, v_ab_w_rg_a, v_ab_b_rg_a, v_ab_w_rg_x, v_ab_b_rg_x, v_ab_lambda, v_ab_w_out, v_c_norm, v_c_w_in, v_c_ln_g, v_c_ln_b, v_c_w_s, v_c_b_s, v_c_w_out, v_ffn_norm, v_ffn_w_gate, v_ffn_w_up, v_ffn_conv_w, v_ffn_conv_b, v_ffn_w_down, v_final_norm):
    given = dict(locals())
    w = {n: given[n] for n in WEIGHT_NAMES}
    m = {n: given["m_" + n] for n in WEIGHT_NAMES}
    v = {n: given["v_" + n] for n in WEIGHT_NAMES}
    c = lax.axis_index("c")
    chip = 2 * lax.axis_index("x") + lax.axis_index("y")

    halves = lambda a: a.reshape(a.shape[0], 2, a.shape[1] // 2, a.shape[2])
    tr = lambda a: jnp.swapaxes(a, 1, 2)
    send = {"ab_w_in": w["ab_w_in"], "ab_w_q_b": w["ab_w_q_b"], "ab_w_kv_b": w["ab_w_kv_b"], "ab_w_out": w["ab_w_out"],
            "c_w_in": tr(w["c_w_in"]), "c_w_out": w["c_w_out"], "ffn_w_gate": tr(w["ffn_w_gate"]),
            "ffn_w_up": tr(w["ffn_w_up"]), "ffn_w_down": w["ffn_w_down"]}
    small_rows = _rows(sum(w[n].size for n in SMALL_SHARDED), 16)
    small_sh = _flat([w[n] for n in SMALL_SHARDED], small_rows).reshape(1, 2, small_rows // 2, LANES)
    first_names = ["ab_w_in", "ab_w_q_b", "ab_w_kv_b", "ab_w_out"]
    mine = {n: halves(send[n].astype(BF16)) for n in BIG}

    def put_own(own, arrived):
        a = _put(arrived, own, chip, 1)
        return a.reshape(a.shape[0], -1, a.shape[-1])

    first = [mine[n] for n in first_names] + [small_sh]
    got = _all_gather(first).run("gather_first")
    full = {n: put_own(o, a) for n, o, a in zip(first_names + ["small"], first, got)}
    unshard = lambda a: jnp.swapaxes(a.reshape(N_CHIPS, -1, a.shape[-1]), 0, 1).reshape(-1, N_CHIPS * a.shape[-1])
    p = _prep_big(unshard(full["ab_w_in"][0]), unshard(full["ab_w_q_b"][0]), unshard(full["ab_w_kv_b"][0]))
    p["ab_w_out"] = full["ab_w_out"][0]
    small_full = dict(w)
    off = 0
    small_got = full["small"].reshape(N_CHIPS, -1)
    for n, ax in SMALL_SHARDED.items():
        seg = small_got[:, off:off + w[n].size].reshape((N_CHIPS,) + w[n].shape)
        small_full[n] = _join_shards(seg, ax)
        off += w[n].size
    p.update(_prep_small(small_full))

    for n in ("ffn_gate_t", "ffn_up_t", "ffn_down"):
        p[n] = {}

    def weights_ride(parts):
        def sink(arrived):
            for (own, setter), a in zip(parts, arrived):
                setter(put_own(own, a)[0])
        return _all_gather([own for own, _ in parts]), sink

    ffn_keys = {"ffn_gate_t": "ffn_w_gate", "ffn_up_t": "ffn_w_up", "ffn_down": "ffn_w_down"}
    ffn_part = lambda key, l: (mine[ffn_keys[key]][l:l + 1], functools.partial(p[key].__setitem__, l))
    rides = {
        "attn_fwd": weights_ride([ffn_part("ffn_gate_t", 0), ffn_part("ffn_up_t", 0)]),
        "lru_fwd": weights_ride([ffn_part("ffn_down", 0)]),
        "ffn0_gate": weights_ride([ffn_part("ffn_gate_t", 1)]),
        "ffn0_up": weights_ride([ffn_part("ffn_up_t", 1)]),
        "ffn0_act": weights_ride([ffn_part("ffn_down", 1)]),
        "ffn0_down": weights_ride([(mine["c_w_in"], functools.partial(p.__setitem__, "c_w_in_t")),
                                   (mine["c_w_out"], functools.partial(p.__setitem__, "c_w_out"))]),
    }

    def chip_sums(pair, arrived, tag):
        own = [lax.dynamic_index_in_dim(a, chip, axis=0, keepdims=False) for a in pair]
        return [_sum_slots(_put(a, o, chip, 0), name=f"grad_chip_sum_{tag}{i}") for i, (a, o) in enumerate(zip(arrived, own))]

    half_of = {}

    def grads_ready(layer, ready):
        if layer == 1:
            named = {"gate1": ready["ffn_gate_t"], "up1": ready["ffn_up_t"], "down1": ready["ffn_down"]}
            hosts = {"sgu_bwd": ["down1"], "ffn0_dact": ["up1"], "ffn0_dactbwd": ["gate1"]}
        else:
            named = {"c_in": ready["c_w_in_t"], "c_out": ready["c_w_out"], "gate0": ready["ffn_gate_t"],
                     "up0": ready["ffn_up_t"], "down0": ready["ffn_down"]}
            hosts = {"attn_dq": ["c_in", "c_out", "down0"], "attn_dkv": ["gate0", "up0"]}
        tag = f"f{layer}"
        sharded = [a.reshape(N_CHIPS, 2, -1, a.shape[-1]) for a in named.values()]

        def paired(from_sib):
            own = [lax.dynamic_index_in_dim(a, c, axis=1, keepdims=False) for a in sharded]
            pair = {k: _add2(a, b, out_dtype=BF16, name=f"grad_pair_add_{tag}{i}")
                    for i, (k, a, b) in enumerate(zip(named, own, from_sib))}
            for kernel_name, keys in hosts.items():
                def sink(arrived, keys=keys, kernel_name=kernel_name):
                    half_of.update(zip(keys, chip_sums([pair[k] for k in keys], arrived, f"{tag}_{kernel_name}")))
                rides[kernel_name] = (_chip_exchange([pair[k] for k in keys], scatter=True), sink)

        rides[f"ffn{layer}_dnorm"] = (_pair_swap(sharded), paired)

    loss_row, grad_x, g = _local_step(x, positions, loss_target, p, rides, grads_ready)

    cols = lambda a, n: jnp.swapaxes(a.reshape(a.shape[0], N_CHIPS, n), 0, 1)
    n_in, n_q, n_kv = w["ab_w_in"].shape[2], w["ab_w_q_b"].shape[2], w["ab_w_kv_b"].shape[2]
    small_names = SMALL_REPLICATED + list(SMALL_SHARDED)
    rs = _rows(sum(g[n].size for n in small_names) + LANES, FLAT_ROWS)
    small = _flat([loss_row] + [g[n] for n in small_names], rs)
    slot = (jnp.arange(2) == c)[:, None, None]
    last = [cols(_unperm_w_in(g["w_in_p"]), n_in), cols(_from_head_blocks(g["w_q_p"], QK_NOPE + QK_ROPE), n_q),
            cols(_join_kv(g["w_k_p"], g["w_v_p"]), n_kv), g["ab_w_out"]]
    last = [a.reshape(N_CHIPS, 2, -1, a.shape[-1]) for a in last]
    *from_sib, small_sib = _merge(_pair_swap(last), _pair_send([small])).run("tail_pair")
    own = [lax.dynamic_index_in_dim(a, c, axis=1, keepdims=False) for a in last]
    pair = [_add2(a, b, out_dtype=BF16, name=f"grad_pair_add_b{i}") for i, (a, b) in enumerate(zip(own, from_sib))]
    pair_small = _sum_slots(jnp.where(slot, small[None], small_sib[None]), name="small_pair_sum")
    *arrived, all_small = _merge(_chip_exchange(pair, scatter=True), _chip_exchange([pair_small], scatter=False)).run("tail_chip")
    half_of.update(zip(["in", "q", "kv", "out"], chip_sums(pair, arrived, "b")))
    small_sum = _sum_slots(_put(all_small, pair_small, chip, 0), name="small_chip_sum")
    keys = ("in", "q", "kv", "out", "c_in", "c_out", "gate0", "gate1", "up0", "up1", "down0", "down1")
    other_half = dict(zip(keys, _pair_send([half_of[k] for k in keys]).run("grad_pair_share")))
    whole = lambda k: jnp.where(slot, half_of[k][None], other_half[k][None]).reshape(-1, half_of[k].shape[-1])
    grads_t = {"ab_w_in": whole("in").T[None], "ab_w_q_b": whole("q").T[None]}
    grads = {"ab_w_kv_b": whole("kv")[None], "c_w_in": whole("c_in").T[None], **{n: tr(a) for n, a in grads_t.items()}}
    by_halves = {"ab_w_out": (("out",), False), "c_w_out": (("c_out",), False), "ffn_w_down": (("down0", "down1"), False),
                 "ffn_w_gate": (("gate0", "gate1"), True), "ffn_w_up": (("up0", "up1"), True)}

    small_parts = _unflat(small_sum, [(1, LANES)] + [g[n].shape for n in small_names])
    loss = small_parts[0][0, 0]
    for n, a in zip(small_names, small_parts[1:]):
        if n in SMALL_SHARDED:
            ax = SMALL_SHARDED[n]
            a = lax.dynamic_slice_in_dim(a, chip * w[n].shape[ax], w[n].shape[ax], axis=ax)
        grads[n] = a.reshape(w[n].shape)

    delta, new_m, new_v = {}, {}, {}
    for n in BIG:
        if n in by_halves:
            ks, transposed = by_halves[n]
            view = tr if transposed else (lambda a: a)
            out = _adamw_halves(view(w[n]), view(m[n]), view(v[n]), [half_of[k] for k in ks], [other_half[k] for k in ks],
                                name=f"adamw_{n}")
            delta[n], new_m[n], new_v[n], grads[n] = (view(a) for a in out)
        elif n in grads_t:
            out = _adamw(tr(w[n]), grads_t[n], tr(m[n]), tr(v[n]), name=f"adamw_{n}")
            delta[n], new_m[n], new_v[n] = (tr(a) for a in out)
        else:
            delta[n], new_m[n], new_v[n] = _adamw(w[n], grads[n], m[n], v[n], name=f"adamw_{n}")
    small_all = [n for n in WEIGHT_NAMES if n not in BIG]
    ra = _rows(sum(w[n].size for n in small_all), FLAT_ROWS)
    pack = lambda d: _flat([d[n] for n in small_all], ra)[None]
    out = _adamw(pack(w), pack(grads), pack(m), pack(v), name="adamw_small")
    shapes = [w[n].shape for n in small_all]
    for d, flat in zip((delta, new_m, new_v), out):
        d.update(zip(small_all, _unflat(flat, shapes)))
    return (loss, grad_x, *[grads[n] for n in WEIGHT_NAMES], *[delta[n] for n in WEIGHT_NAMES],
            *[new_m[n] for n in WEIGHT_NAMES], *[new_v[n] for n in WEIGHT_NAMES])
```

```python
import functools
import math

import jax
import jax.numpy as jnp
from jax import lax
from jax.experimental import pallas as pl
from jax.experimental.pallas import tpu as pltpu

F32 = jnp.float32
BF16 = jnp.bfloat16
MESH = pl.DeviceIdType.MESH

D_MODEL = 1024
MLA_HEADS = 8
Q_LORA = 256
KV_LORA = 128
QK_NOPE = 64
QK_ROPE = 32
V_HEAD = 64
LRU_WIDTH = 512
LRU_HEADS = 8
LRU_BLOCK = 64
LRU_CONV = 4
LRU_C = 8.0
CHUNK = 128
SGU_GROUPS = 8
SGU_WIDTH = 1024
D_FF = 2816
FFN_CONV = 3
NORM_EPS = 1e-6
ROPE_BASE = 10000.0
AB_IN_PAD = 1536
ADAM_LR = 0.001
ADAM_B1 = 0.9
ADAM_B2 = 0.999
ADAM_EPS = 1e-08
ADAM_WD = 0.01
ADAM_STEP = 10

N_CHIPS = 4
LANES = 128
VMEM_LIMIT = 56 * 1024 * 1024
ROW_TILE = 256
NORM_TILE = 512
MM_TM, MM_TN, MM_TK = 512, 1536, 2816
MM_TM_T, MM_TK_T = 1408, 1024
GELU_C = math.sqrt(2.0 / math.pi)


def _cparams(sem):
    return pltpu.CompilerParams(dimension_semantics=sem, vmem_limit_bytes=VMEM_LIMIT)


def _tile(n, target, mult=LANES):
    t = (min(n, target) // mult) * mult
    while t >= mult:
        if n % t == 0:
            return t
        t -= mult
    return n


GELU_K = GELU_C * 0.044715


def _gelu(x):
    t = jnp.tanh(x * (GELU_C + GELU_K * (x * x)))
    hx = 0.5 * x
    return hx + hx * t


def _gelu_and_grad(x):
    x2 = x * x
    t = jnp.tanh(x * (GELU_C + GELU_K * x2))
    hx = 0.5 * x
    dg = (0.5 + 0.5 * t) + (hx * (1.0 - t * t)) * (GELU_C + (3.0 * GELU_K) * x2)
    return hx + hx * t, dg


def _sigmoid(x):
    return 1.0 / (1.0 + jnp.exp(-x))


def _shift_rows(x, d, fill_rows):
    ext = jnp.concatenate([fill_rows, x], axis=0)
    return pltpu.roll(ext, d, 0)[8:]


def _shift_rows_up(x, d, fill_rows):
    n = x.shape[0]
    ext = jnp.concatenate([x, fill_rows], axis=0)
    return pltpu.roll(ext, n + 8 - d, 0)[:n]


def _dot(a, b, dims):
    return lax.dot_general(a.astype(BF16), b.astype(BF16), (dims, ((), ())), preferred_element_type=F32)


def _dot_nn(a, b):
    return _dot(a, b, ((1,), (0,)))


def _dot_nt(a, b):
    return _dot(a, b, ((1,), (1,)))


def _dot_tn(a, b):
    return _dot(a, b, ((0,), (0,)))


def _mm(a, b, *, name, ta=False, tb=False, res=None, out_dtype=F32, ride=None):
    if ta:
        K, M = a.shape
    else:
        M, K = a.shape
    N = b.shape[0] if tb else b.shape[1]
    tm = _tile(M, MM_TM_T if ta else MM_TM, LANES if ta else 8)
    tn = _tile(N, MM_TN, LANES)
    tk = _tile(K, MM_TK_T if ta else MM_TK, LANES)
    nk = K // tk
    a_spec = pl.BlockSpec((tk, tm), lambda j, i, k: (k, i)) if ta else pl.BlockSpec((tm, tk), lambda j, i, k: (i, k))
    b_spec = pl.BlockSpec((tn, tk), lambda j, i, k: (j, k)) if tb else pl.BlockSpec((tk, tn), lambda j, i, k: (k, j))
    o_spec = pl.BlockSpec((tm, tn), lambda j, i, k: (i, j))
    dims = ((0,) if ta else (1,), (1,) if tb else (0,))
    has_res = res is not None

    def body(*refs):
        a_ref, b_ref = refs[:2]
        r_ref = refs[2] if has_res else None
        o_ref = refs[3] if has_res else refs[2]
        p = _dot(a_ref[...], b_ref[...], dims)

        def finish(r):
            if has_res:
                r = r + r_ref[...].astype(F32)
            o_ref[...] = r.astype(out_dtype)

        if nk == 1:
            finish(p)
            return
        acc_ref = refs[-1]
        k = pl.program_id(2)

        @pl.when(k == 0)
        def _():
            acc_ref[...] = p

        @pl.when(jnp.logical_and(k > 0, k < nk - 1))
        def _():
            acc_ref[...] += p

        @pl.when(k == nk - 1)
        def _():
            finish(acc_ref[...] + p)

    in_specs = [a_spec, b_spec] + ([o_spec] if has_res else [])
    args = (a, b) + ((res,) if has_res else ())
    return _pcall(
        body, name=name, grid=(N // tn, M // tm, nk), in_specs=in_specs, out_specs=[o_spec],
        out_shape=[jax.ShapeDtypeStruct((M, N), out_dtype)], args=args,
        scratch=[pltpu.VMEM((tm, tn), F32)] if nk > 1 else [], sem=("parallel", "parallel", "arbitrary"), ride=ride)[0]


def _rms_fwd(x, g, *, name, cb=0, out_dtype=BF16, ride=None):
    T = x.shape[0]
    W = g.shape[-1]
    g = g.reshape(1, W)
    tt = _tile(T, NORM_TILE, 16)

    def body(x_ref, g_ref, o_ref):
        xf = x_ref[...].astype(F32)
        rstd = lax.rsqrt(jnp.mean(xf * xf, axis=-1, keepdims=True) + NORM_EPS)
        o_ref[...] = (xf * rstd * g_ref[...]).astype(out_dtype)

    return _pcall(
        body, name=name, grid=(T // tt,),
        in_specs=[pl.BlockSpec((tt, W), lambda i: (i, cb)), pl.BlockSpec((1, W), lambda i: (0, 0))],
        out_specs=[pl.BlockSpec((tt, W), lambda i: (i, 0))], out_shape=[jax.ShapeDtypeStruct((T, W), out_dtype)],
        args=(x, g), sem=("parallel",), ride=ride)[0]


def _rms_bwd(x, g, dy, *, name, cb=0, res=None, out_dtype=F32, ride=None):
    T = x.shape[0]
    W = g.shape[-1]
    g = g.reshape(1, W)
    tt = _tile(T, NORM_TILE, 16)
    has_res = res is not None

    def body(*refs):
        if has_res:
            x_ref, g_ref, dy_ref, r_ref, dx_ref, dg_ref = refs
        else:
            x_ref, g_ref, dy_ref, dx_ref, dg_ref = refs
        xf = x_ref[...].astype(F32)
        dyf = dy_ref[...].astype(F32)
        rstd = lax.rsqrt(jnp.mean(xf * xf, axis=-1, keepdims=True) + NORM_EPS)
        xhat = xf * rstd
        dxhat = dyf * g_ref[...]
        dx = rstd * (dxhat - xhat * jnp.mean(dxhat * xhat, axis=-1, keepdims=True))
        if has_res:
            dx = dx + r_ref[...].astype(F32)
        dx_ref[...] = dx.astype(out_dtype)
        part = jnp.sum(dyf * xhat, axis=0, keepdims=True)

        @pl.when(pl.program_id(0) == 0)
        def _():
            dg_ref[...] = part

        @pl.when(pl.program_id(0) > 0)
        def _():
            dg_ref[...] += part

    row = pl.BlockSpec((tt, W), lambda i: (i, 0))
    in_specs = [pl.BlockSpec((tt, W), lambda i: (i, cb)), pl.BlockSpec((1, W), lambda i: (0, 0)), row]
    args = (x, g, dy)
    if has_res:
        in_specs.append(row)
        args = args + (res,)
    return _pcall(
        body, name=name, grid=(T // tt,), in_specs=in_specs,
        out_specs=[row, pl.BlockSpec((1, W), lambda i: (0, 0))],
        out_shape=[jax.ShapeDtypeStruct((T, W), out_dtype), jax.ShapeDtypeStruct((1, W), F32)], args=args, ride=ride)


def _final_fwd_bwd(h, g, target, *, name):
    T, W = h.shape
    g = g.reshape(1, W)
    tt = _tile(T, NORM_TILE, 16)

    def body(x_ref, g_ref, t_ref, loss_ref, dx_ref, dg_ref):
        xf = x_ref[...]
        rstd = lax.rsqrt(jnp.mean(xf * xf, axis=-1, keepdims=True) + NORM_EPS)
        xhat = xf * rstd
        err = xhat * g_ref[...] - t_ref[...]
        lpart = jnp.zeros((1, LANES), F32) + (0.5 / W) * jnp.sum(err * err)
        dyf = err * (1.0 / W)
        dxhat = dyf * g_ref[...]
        dx_ref[...] = rstd * (dxhat - xhat * jnp.mean(dxhat * xhat, axis=-1, keepdims=True))
        part = jnp.sum(dyf * xhat, axis=0, keepdims=True)

        @pl.when(pl.program_id(0) == 0)
        def _():
            dg_ref[...] = part
            loss_ref[...] = lpart

        @pl.when(pl.program_id(0) > 0)
        def _():
            dg_ref[...] += part
            loss_ref[...] += lpart

    row = pl.BlockSpec((tt, W), lambda i: (i, 0))
    return pl.pallas_call(
        body, name=name, grid=(T // tt,),
        in_specs=[row, pl.BlockSpec((1, W), lambda i: (0, 0)), row],
        out_specs=[pl.BlockSpec((1, LANES), lambda i: (0, 0)), row, pl.BlockSpec((1, W), lambda i: (0, 0))],
        out_shape=[jax.ShapeDtypeStruct((1, LANES), F32), jax.ShapeDtypeStruct((T, W), F32),
                   jax.ShapeDtypeStruct((1, W), F32)],
        compiler_params=_cparams(("arbitrary",)),
    )(h, g, target)


def _swap16(x):
    lane = lax.broadcasted_iota(jnp.int32, x.shape, 1)
    return jnp.where((lane % 32) < 16, pltpu.roll(x, LANES - 16, 1), pltpu.roll(x, 16, 1))


def _rope(x, c, s):
    return x * c + _swap16(x) * s


def _rope_t(d, c, s):
    return d * c + _swap16(d * s)


def _head_block_map(fn, x, cos, sin, *, name):
    T, W = x.shape
    tt = _tile(T, NORM_TILE, 16)

    def body(x_ref, c_ref, s_ref, o_ref):
        c, s = c_ref[...], s_ref[...]
        for h in range(W // LANES):
            lanes = slice(h * LANES, (h + 1) * LANES)
            o_ref[:, lanes] = fn(x_ref[:, lanes], c, s).astype(BF16)

    tab = pl.BlockSpec((tt, LANES), lambda i: (i, 0))
    blk = pl.BlockSpec((tt, W), lambda i: (i, 0))
    return pl.pallas_call(
        body, name=name, grid=(T // tt,), in_specs=[blk, tab, tab], out_specs=blk,
        out_shape=jax.ShapeDtypeStruct((T, W), BF16), compiler_params=_cparams(("parallel",)),
    )(x, cos, sin)


def _rope_q(q, cos, sin, *, name):
    scale = _attn_scale()
    return _head_block_map(lambda x, c, s: _rope(x, c, s) * scale, q, cos, sin, name=name)


def _rope_q_bwd(dq, cos, sin, *, name):
    return _head_block_map(_rope_t, dq, cos, sin, name=name)


def _key_blocks(kv, z, cos, sin, *, kpe_block, name):
    T = kv.shape[0]
    tt = _tile(T, NORM_TILE, 16)
    W = MLA_HEADS * LANES

    def body(kv_ref, z_ref, c_ref, s_ref, o_ref):
        kr = _rope(z_ref[...], c_ref[...], s_ref[...])
        for h in range(MLA_HEADS):
            lanes = slice(h * LANES, (h + 1) * LANES)
            o_ref[:, lanes] = (kv_ref[:, lanes].astype(F32) + kr).astype(BF16)

    tab = pl.BlockSpec((tt, LANES), lambda i: (i, 0))
    blk = pl.BlockSpec((tt, W), lambda i: (i, 0))
    return pl.pallas_call(
        body, name=name, grid=(T // tt,),
        in_specs=[blk, pl.BlockSpec((tt, LANES), lambda i: (i, kpe_block)), tab, tab], out_specs=blk,
        out_shape=jax.ShapeDtypeStruct((T, W), BF16), compiler_params=_cparams(("parallel",)),
    )(kv, z, cos, sin)


def _key_rope_bwd(dk, cos, sin, *, name):
    T = dk.shape[0]
    tt = _tile(T, NORM_TILE, 16)

    def body(d_ref, c_ref, s_ref, o_ref):
        d = d_ref[:, :LANES]
        for h in range(1, MLA_HEADS):
            d = d + d_ref[:, h * LANES:(h + 1) * LANES]
        lane = lax.broadcasted_iota(jnp.int32, d.shape, 1)
        d = jnp.where(jnp.logical_and(lane >= QK_NOPE, lane < QK_NOPE + QK_ROPE), d, 0.0)
        o_ref[...] = _rope_t(d, c_ref[...], s_ref[...]).astype(BF16)

    tab = pl.BlockSpec((tt, LANES), lambda i: (i, 0))
    return pl.pallas_call(
        body, name=name, grid=(T // tt,),
        in_specs=[pl.BlockSpec((tt, MLA_HEADS * LANES), lambda i: (i, 0)), tab, tab], out_specs=tab,
        out_shape=jax.ShapeDtypeStruct((T, LANES), BF16), compiler_params=_cparams(("parallel",)),
    )(dk, cos, sin)


ATT_BLOCK = 512


def _attn_scale():
    return float((QK_NOPE + QK_ROPE) ** -0.5)


def _causal_mask(qi, kj, tq, tk):
    row = qi * tq + lax.broadcasted_iota(jnp.int32, (tq, tk), 0)
    col = kj * tk + lax.broadcasted_iota(jnp.int32, (tq, tk), 1)
    return col <= row


def _pcall(body, *, name, grid, in_specs, out_specs, out_shape, args, scratch=(), sem=None, ride=None):
    n_in, n_out, n_scr = len(args), len(out_shape), len(scratch)
    if ride is None:
        return pl.pallas_call(
            body, name=name, grid=grid, in_specs=list(in_specs), out_specs=list(out_specs), out_shape=list(out_shape),
            scratch_shapes=list(scratch), compiler_params=_cparams(sem or ("arbitrary",) * len(grid)))(*args)
    ex, sink = ride
    o0 = n_in + len(ex.arrs)
    s0 = o0 + n_out + len(ex.out_shapes)

    def hosted(*refs):
        parts = (refs[n_in:o0], refs[o0 + n_out:s0], refs[-2], refs[-1])
        ids = [pl.program_id(i) for i in range(len(grid))]
        pl.when(functools.reduce(jnp.logical_and, [i == 0 for i in ids]))(lambda: ex.start(*parts))
        body(*refs[:n_in], *refs[o0:o0 + n_out], *refs[s0:s0 + n_scr])
        pl.when(functools.reduce(jnp.logical_and, [i == n - 1 for i, n in zip(ids, grid)]))(lambda: ex.finish(*parts))

    outs = pl.pallas_call(
        hosted, name=name, grid=grid, in_specs=list(in_specs) + ex.in_specs, out_specs=list(out_specs) + ex.out_specs,
        out_shape=list(out_shape) + ex.out_shapes, scratch_shapes=list(scratch) + ex.scratch,
        compiler_params=_cparams(("arbitrary",) * len(grid)))(*args, *ex.arrs)
    sink(outs[n_out:])
    return outs[:n_out]


PAIRS = MLA_HEADS // 2


def _own_lanes(x, first):
    lane = lax.broadcasted_iota(jnp.int32, x.shape, 1)
    return jnp.where((lane < V_HEAD) if first else (lane >= V_HEAD), x, 0.0)


def _attn_fwd(q, k, kv, *, B, S, v_block0, name, ride=None):
    tq = tk = min(ATT_BLOCK, S)
    nq = S // tq
    T = B * S

    def body(q_ref, k_ref, v_ref, o_ref, lse_ref):
        qi = pl.program_id(2)
        qs = (q_ref[:, :LANES], q_ref[:, LANES:])

        def step(masked):
            def f(j, carry):
                rows = pl.ds(pl.multiple_of(j * tk, tk), tk)
                vb = v_ref[rows, :]
                out = []
                for h in range(2):
                    m, l, acc = carry[h]
                    s = _dot_nt(qs[h], k_ref[rows, h * LANES:(h + 1) * LANES])
                    if masked:
                        s = jnp.where(_causal_mask(qi, j, tq, tk), s, -jnp.inf)
                    m_new = jnp.maximum(m, jnp.max(s, axis=-1, keepdims=True))
                    alpha = jnp.exp(m - m_new)
                    p = jnp.exp(s - m_new)
                    out.append((m_new, alpha * l + jnp.sum(p, axis=-1, keepdims=True), alpha * acc + _dot_nn(p, vb)))
                return tuple(out)
            return f

        one = (jnp.full((tq, 1), -1e30, F32), jnp.zeros((tq, 1), F32), jnp.zeros((tq, LANES), F32))
        (ma, la, acca), (mb, lb, accb) = step(True)(qi, lax.fori_loop(0, qi, step(False), (one, one)))
        o_ref[...] = _own_lanes(acca / la, True) + _own_lanes(accb / lb, False)
        lse_ref[0, 0] = ma + jnp.log(la)
        lse_ref[0, 1] = mb + jnp.log(lb)

    return _pcall(
        body, name=name, grid=(B, PAIRS, nq),
        in_specs=[pl.BlockSpec((tq, 2 * LANES), lambda b, g, i: (b * nq + i, g)),
                  pl.BlockSpec((S, 2 * LANES), lambda b, g, i: (b, g)),
                  pl.BlockSpec((S, LANES), lambda b, g, i: (b, v_block0 + g))],
        out_specs=[pl.BlockSpec((tq, LANES), lambda b, g, i: (b * nq + i, g)),
                   pl.BlockSpec((1, 2, tq, 1), lambda b, g, i: (b, g, i, 0))],
        out_shape=[jax.ShapeDtypeStruct((T, PAIRS * LANES), F32), jax.ShapeDtypeStruct((B, MLA_HEADS, S, 1), F32)],
        args=(q, k, kv), ride=ride)


def _attn_dq(q, k, kv, o, lse, do, *, B, S, v_block0, name, ride=None):
    tq = tk = min(ATT_BLOCK, S)
    nq = S // tq
    T = B * S
    scale = _attn_scale()

    def body(q_ref, k_ref, v_ref, o_ref, lse_ref, do_ref, dq_ref, delta_ref):
        qi = pl.program_id(2)
        qs = (q_ref[:, :LANES], q_ref[:, LANES:])
        dos = (_own_lanes(do_ref[...], True), _own_lanes(do_ref[...], False))
        deltas = tuple(jnp.sum(d * o_ref[...], axis=-1, keepdims=True) for d in dos)
        lses = (lse_ref[0, 0], lse_ref[0, 1])

        def step(masked):
            def f(j, carry):
                rows = pl.ds(pl.multiple_of(j * tk, tk), tk)
                vb = v_ref[rows, :]
                out = []
                for h in range(2):
                    kb = k_ref[rows, h * LANES:(h + 1) * LANES]
                    p = jnp.exp(_dot_nt(qs[h], kb) - lses[h])
                    if masked:
                        p = jnp.where(_causal_mask(qi, j, tq, tk), p, 0.0)
                    ds = p * (_dot_nt(dos[h], vb) - deltas[h])
                    out.append(carry[h] + _dot_nn(ds, kb))
                return tuple(out)
            return f

        zero = jnp.zeros((tq, LANES), F32)
        dqa, dqb = step(True)(qi, lax.fori_loop(0, qi, step(False), (zero, zero)))
        dq_ref[:, :LANES] = dqa * scale
        dq_ref[:, LANES:] = dqb * scale
        delta_ref[0, 0] = deltas[0]
        delta_ref[0, 1] = deltas[1]

    qrow = lambda w: pl.BlockSpec((tq, w), lambda b, g, i: (b * nq + i, g))
    stat = pl.BlockSpec((1, 2, tq, 1), lambda b, g, i: (b, g, i, 0))
    return _pcall(
        body, name=name, grid=(B, PAIRS, nq),
        in_specs=[qrow(2 * LANES), pl.BlockSpec((S, 2 * LANES), lambda b, g, i: (b, g)),
                  pl.BlockSpec((S, LANES), lambda b, g, i: (b, v_block0 + g)), qrow(LANES), stat, qrow(LANES)],
        out_specs=[qrow(2 * LANES), stat],
        out_shape=[jax.ShapeDtypeStruct((T, MLA_HEADS * LANES), F32), jax.ShapeDtypeStruct((B, MLA_HEADS, S, 1), F32)],
        args=(q, k, kv, o, lse, do), sem=("parallel", "parallel", "parallel"), ride=ride)


def _attn_dkv(q, k, kv, lse_rows, delta_rows, do, *, B, S, v_block0, name, ride=None):
    tq = tk = min(ATT_BLOCK, S)
    nq = S // tq
    T = B * S

    def body(q_ref, k_ref, v_ref, lse_ref, delta_ref, do_ref, dk_ref, dv_ref):
        kj = pl.program_id(2)
        ks = (k_ref[:, :LANES], k_ref[:, LANES:])
        vb = v_ref[...]

        def step(masked):
            def f(i, carry):
                rows = pl.ds(pl.multiple_of(i * tq, tq), tq)
                do_b = do_ref[rows, :]
                dks, dv = list(carry[:2]), carry[2]
                for h in range(2):
                    qb = q_ref[rows, h * LANES:(h + 1) * LANES]
                    doh = _own_lanes(do_b, h == 0)
                    pt = jnp.exp(_dot_nt(ks[h], qb) - lse_ref[0, h, pl.ds(i, 1), :])
                    if masked:
                        krow = kj * tk + lax.broadcasted_iota(jnp.int32, (tk, tq), 0)
                        qcol = i * tq + lax.broadcasted_iota(jnp.int32, (tk, tq), 1)
                        pt = jnp.where(krow <= qcol, pt, 0.0)
                    dst = pt * (_dot_nt(vb, doh) - delta_ref[0, h, pl.ds(i, 1), :])
                    dks[h] = dks[h] + _dot_nn(dst, qb)
                    dv = dv + _dot_nn(pt, doh)
                return dks[0], dks[1], dv
            return f

        zero = jnp.zeros((tk, LANES), F32)
        dka, dkb, dv = lax.fori_loop(kj + 1, nq, step(False), step(True)(kj, (zero, zero, zero)))
        dk_ref[:, :LANES] = dka
        dk_ref[:, LANES:] = dkb
        dv_ref[...] = dv

    krow = lambda w, c0: pl.BlockSpec((tk, w), lambda b, g, j: (b * nq + j, c0 + g))
    seq = lambda w: pl.BlockSpec((S, w), lambda b, g, j: (b, g))
    stat = pl.BlockSpec((1, 2, nq, tq), lambda b, g, j: (b, g, 0, 0))
    return _pcall(
        body, name=name, grid=(B, PAIRS, nq),
        in_specs=[seq(2 * LANES), krow(2 * LANES, 0), krow(LANES, v_block0), stat, stat, seq(LANES)],
        out_specs=[krow(2 * LANES, 0), krow(LANES, 0)],
        out_shape=[jax.ShapeDtypeStruct((T, MLA_HEADS * LANES), F32), jax.ShapeDtypeStruct((T, PAIRS * LANES), F32)],
        args=(q, k, kv, lse_rows, delta_rows, do), ride=ride)


def _lru_gates(xl, halo, cw_ref, cb_ref, wa_ref, ba_ref, wx_ref, bx_ref, lam_ref):
    xc = cb_ref[...] + cw_ref[3:4, :] * xl
    for kk in range(LRU_CONV - 1):
        xc = xc + cw_ref[kk:kk + 1, :] * _shift_rows(xl, LRU_CONV - 1 - kk, halo)
    r = _sigmoid(_dot_nn(xc, wa_ref[...]) + ba_ref[...])
    i = _sigmoid(_dot_nn(xc, wx_ref[...]) + bx_ref[...])
    lam = lam_ref[...]
    sp = jnp.maximum(-lam, 0.0) + jnp.log(1.0 + jnp.exp(-jnp.abs(lam)))
    a = jnp.exp(-LRU_C * r * sp)
    mult = jnp.sqrt(1.0 - a * a)
    return xc, r, i, sp, a, mult


def _lru_specs(tt, nt, S):
    def make(rev):
        tmap = (lambda t: nt - 1 - t) if rev else (lambda t: t)
        tile = lambda cb: pl.BlockSpec((tt, LRU_WIDTH), lambda b, t: (b * nt + tmap(t), cb))
        prev8 = lambda cb: pl.BlockSpec(
            (8, LRU_WIDTH), lambda b, t: (jnp.maximum((b * nt + tmap(t)) * (tt // 8) - 1, 0), cb))
        return tile, prev8, tmap
    return make


def _lru_fwd(z, cw, cb, wa, ba, wx, bx, lam, *, S, name, ride=None):
    T = z.shape[0]
    tt = min(ROW_TILE, S)
    nt = S // tt
    tile, prev8, _ = _lru_specs(tt, nt, S)(False)
    vec = lambda r: pl.BlockSpec((r, LRU_WIDTH), lambda b, t: (0, 0))
    mat = pl.BlockSpec((LRU_WIDTH, LRU_WIDTH), lambda b, t: (0, 0))

    def body(xl_ref, halo_ref, gate_ref, cw_ref, cb_ref, wa_ref, ba_ref, wx_ref, bx_ref, lam_ref,
             y_ref, h_ref, carry_ref):
        t = pl.program_id(1)
        first = t == 0
        halo = jnp.where(first, 0.0, halo_ref[...])
        xl_t = xl_ref[...]
        xc, r, i, sp, a, mult = _lru_gates(xl_t, halo, cw_ref, cb_ref, wa_ref, ba_ref, wx_ref, bx_ref, lam_ref)
        bv = mult * (i * xc)
        ones = jnp.ones((8, LRU_WIDTH), F32)
        zeros = jnp.zeros((8, LRU_WIDTH), F32)
        row = lax.broadcasted_iota(jnp.int32, (tt, LRU_WIDTH), 0)
        A = a
        d = 1
        while d < tt:
            if d < 8:
                a_sh = _shift_rows(A, d, ones)
                b_sh = _shift_rows(bv, d, zeros)
            else:
                a_sh = jnp.where(row < d, 1.0, pltpu.roll(A, d, 0))
                b_sh = jnp.where(row < d, 0.0, pltpu.roll(bv, d, 0))
            bv = A * b_sh + bv
            A = A * a_sh
            d *= 2
        h0 = jnp.where(first, 0.0, carry_ref[0:1, :])
        h = A * h0 + bv
        carry_ref[...] = jnp.broadcast_to(h[tt - 1:tt, :], (8, LRU_WIDTH))
        h_ref[...] = h
        y_ref[...] = (h * _gelu(gate_ref[...])).astype(BF16)

    return _pcall(
        body, name=name, grid=(T // S, nt),
        in_specs=[tile(0), prev8(0), tile(1), vec(LRU_CONV), vec(1), mat, vec(1), mat, vec(1), vec(1)],
        out_specs=[tile(0), tile(0)],
        out_shape=[jax.ShapeDtypeStruct((T, LRU_WIDTH), BF16), jax.ShapeDtypeStruct((T, LRU_WIDTH), F32)],
        args=(z, z, z, cw, cb, wa, ba, wx, bx, lam), scratch=[pltpu.VMEM((8, LRU_WIDTH), F32)], ride=ride)


def _lru_bwd(z, h, dy, cw, cb, wa, ba, wx, bx, lam, *, S, name):
    T = z.shape[0]
    tt = min(ROW_TILE, S)
    nt = S // tt
    tile, prev8, tmap = _lru_specs(tt, nt, S)(True)
    vec = lambda r: pl.BlockSpec((r, LRU_WIDTH), lambda b, t: (0, 0))
    mat = pl.BlockSpec((LRU_WIDTH, LRU_WIDTH), lambda b, t: (0, 0))

    def body(xl_ref, halo_ref, gate_ref, h_ref, hprev_ref, dy_ref, cw_ref, cb_ref, wa_ref, ba_ref, wx_ref,
             bx_ref, lam_ref, dxl_ref, dgate_ref, dcw_ref, dcb_ref, dwa_ref, dba_ref, dwx_ref, dbx_ref,
             dlam_ref, lamc_ref, ac_ref, dxc_ref):
        b = pl.program_id(0)
        t = pl.program_id(1)
        tr = nt - 1 - t
        seq_first = tr == 0
        seq_last = t == 0
        halo = jnp.where(seq_first, 0.0, halo_ref[...])
        xl_t = xl_ref[...]
        xc, r, i, sp, a, mult = _lru_gates(xl_t, halo, cw_ref, cb_ref, wa_ref, ba_ref, wx_ref, bx_ref, lam_ref)
        hh = h_ref[...]
        dyf = dy_ref[...].astype(F32)
        gl, dgl = _gelu_and_grad(gate_ref[...])
        dgate_ref[...] = (dyf * hh * dgl).astype(BF16)
        dh = dyf * gl

        a_first_later = jnp.where(seq_last, 0.0, ac_ref[...])
        lam_later = jnp.where(seq_last, 0.0, lamc_ref[...])
        row = lax.broadcasted_iota(jnp.int32, (tt, LRU_WIDTH), 0)
        A = _shift_rows_up(a, 1, a_first_later)
        lm = dh
        ones = jnp.ones((8, LRU_WIDTH), F32)
        zeros = jnp.zeros((8, LRU_WIDTH), F32)
        d = 1
        while d < tt:
            if d < 8:
                a_sh = _shift_rows_up(A, d, ones)
                l_sh = _shift_rows_up(lm, d, zeros)
            else:
                a_sh = jnp.where(row >= tt - d, 1.0, pltpu.roll(A, tt - d, 0))
                l_sh = jnp.where(row >= tt - d, 0.0, pltpu.roll(lm, tt - d, 0))
            lm = lm + A * l_sh
            A = A * a_sh
            d *= 2
        lm = lm + A * lam_later[0:1, :]
        lamc_ref[...] = jnp.broadcast_to(lm[0:1, :], (8, LRU_WIDTH))
        ac_ref[...] = jnp.broadcast_to(a[0:1, :], (8, LRU_WIDTH))

        hprev_halo = jnp.where(seq_first, 0.0, hprev_ref[...])
        h_prev = _shift_rows(hh, 1, hprev_halo)
        da = lm * h_prev
        ixc = i * xc
        dmult = lm * ixc
        di = lm * mult * xc
        dxc = lm * mult * i
        da = da - dmult * a / mult
        dlog = da * a
        dr = dlog * (-LRU_C) * sp
        dsp_part = jnp.sum(dlog * (-LRU_C) * r, axis=0, keepdims=True)
        dpa = dr * r * (1.0 - r)
        dpx = di * i * (1.0 - i)
        dxc = dxc + _dot_nt(dpa, wa_ref[...]) + _dot_nt(dpx, wx_ref[...])
        dwa_part = _dot_tn(xc, dpa)
        dwx_part = _dot_tn(xc, dpx)

        later = jnp.where(seq_last, 0.0, dxc_ref[...])
        dxl = cw_ref[3:4, :] * dxc
        for kk in range(LRU_CONV - 1):
            dxl = dxl + cw_ref[kk:kk + 1, :] * _shift_rows_up(dxc, LRU_CONV - 1 - kk, later)
        dxl_ref[...] = dxl.astype(BF16)
        dxc_ref[...] = dxc[0:8, :]
        dcw_rows = [jnp.sum(dxc * _shift_rows(xl_t, LRU_CONV - 1 - kk, halo), axis=0, keepdims=True)
                    for kk in range(LRU_CONV - 1)]
        dcw_rows.append(jnp.sum(dxc * xl_t, axis=0, keepdims=True))
        dcw_part = jnp.concatenate(dcw_rows + [jnp.zeros((8 - LRU_CONV, LRU_WIDTH), F32)], axis=0)
        lamv = lam_ref[...]
        dlam_part = dsp_part * (-_sigmoid(-lamv))
        parts = ((dcw_ref, dcw_part), (dcb_ref, jnp.sum(dxc, axis=0, keepdims=True)),
                 (dwa_ref, dwa_part), (dba_ref, jnp.sum(dpa, axis=0, keepdims=True)),
                 (dwx_ref, dwx_part), (dbx_ref, jnp.sum(dpx, axis=0, keepdims=True)),
                 (dlam_ref, dlam_part))
        start = jnp.logical_and(b == 0, t == 0)

        @pl.when(start)
        def _():
            for ref, val in parts:
                ref[...] = val

        @pl.when(jnp.logical_not(start))
        def _():
            for ref, val in parts:
                ref[...] += val

    acc = lambda r: pl.BlockSpec((r, LRU_WIDTH), lambda b, t: (0, 0))
    return pl.pallas_call(
        body, name=name, grid=(T // S, nt),
        in_specs=[tile(0), prev8(0), tile(1), tile(0), prev8(0), tile(0),
                  vec(LRU_CONV), vec(1), mat, vec(1), mat, vec(1), vec(1)],
        out_specs=[tile(0), tile(0), acc(8), acc(1), mat, acc(1), mat, acc(1), acc(1)],
        out_shape=[jax.ShapeDtypeStruct((T, LRU_WIDTH), BF16), jax.ShapeDtypeStruct((T, LRU_WIDTH), BF16),
                   jax.ShapeDtypeStruct((8, LRU_WIDTH), F32), jax.ShapeDtypeStruct((1, LRU_WIDTH), F32),
                   jax.ShapeDtypeStruct((LRU_WIDTH, LRU_WIDTH), F32), jax.ShapeDtypeStruct((1, LRU_WIDTH), F32),
                   jax.ShapeDtypeStruct((LRU_WIDTH, LRU_WIDTH), F32), jax.ShapeDtypeStruct((1, LRU_WIDTH), F32),
                   jax.ShapeDtypeStruct((1, LRU_WIDTH), F32)],
        scratch_shapes=[pltpu.VMEM((8, LRU_WIDTH), F32), pltpu.VMEM((8, LRU_WIDTH), F32),
                        pltpu.VMEM((8, LRU_WIDTH), F32)],
        compiler_params=_cparams(("arbitrary", "arbitrary")),
    )(z, z, z, h, h, dy, cw, cb, wa, ba, wx, bx, lam)


FFN_CT = 1408


def _ffn_conv(g, halo, cw_ref, cb_ref):
    gc = cb_ref[...] + cw_ref[2:3, :] * g
    for kk in range(FFN_CONV - 1):
        gc = gc + cw_ref[kk:kk + 1, :] * _shift_rows(g, FFN_CONV - 1 - kk, halo)
    return gc


def _ffn_act_fwd(g, u, cw, cb, *, S, name, ride=None):
    T, F = g.shape
    tt = min(ROW_TILE, S)
    nt = S // tt
    tc = _tile(F, FFN_CT)

    def body(g_ref, halo_ref, u_ref, cw_ref, cb_ref, o_ref):
        first = (pl.program_id(0) % nt) == 0
        halo = jnp.where(first, 0.0, halo_ref[...])
        gc = _ffn_conv(g_ref[...], halo, cw_ref, cb_ref)
        o_ref[...] = (_gelu(gc) * u_ref[...]).astype(BF16)

    tile = pl.BlockSpec((tt, tc), lambda i, j: (i, j))
    prev8 = pl.BlockSpec((8, tc), lambda i, j: (jnp.maximum(i * (tt // 8) - 1, 0), j))
    return _pcall(
        body, name=name, grid=(T // tt, F // tc),
        in_specs=[tile, prev8, tile, pl.BlockSpec((FFN_CONV, tc), lambda i, j: (0, j)),
                  pl.BlockSpec((1, tc), lambda i, j: (0, j))],
        out_specs=[tile], out_shape=[jax.ShapeDtypeStruct((T, F), BF16)], args=(g, g, u, cw, cb),
        sem=("parallel", "parallel"), ride=ride)[0]


def _ffn_act_bwd(g, u, dact, cw, cb, *, S, name, ride=None):
    T, F = g.shape
    tt = min(ROW_TILE, S)
    nt = S // tt
    ntt = T // tt
    tc = _tile(F, FFN_CT)

    def body(g_ref, halo_ref, u_ref, da_ref, cw_ref, cb_ref, dg_ref, du_ref, dcw_ref, dcb_ref, later_ref):
        step = pl.program_id(1)
        ti = (ntt - 1 - step) % nt
        halo = jnp.where(ti == 0, 0.0, halo_ref[...])
        gt = g_ref[...]
        gc = _ffn_conv(gt, halo, cw_ref, cb_ref)
        gl, dgl = _gelu_and_grad(gc)
        da = da_ref[...].astype(F32)
        du_ref[...] = (da * gl).astype(BF16)
        dgc = da * u_ref[...] * dgl
        later = jnp.where(ti == nt - 1, 0.0, later_ref[...])
        dg = cw_ref[2:3, :] * dgc
        for kk in range(FFN_CONV - 1):
            dg = dg + cw_ref[kk:kk + 1, :] * _shift_rows_up(dgc, FFN_CONV - 1 - kk, later)
        dg_ref[...] = dg.astype(BF16)
        later_ref[...] = dgc[0:8, :]
        rows = [jnp.sum(dgc * _shift_rows(gt, FFN_CONV - 1 - kk, halo), axis=0, keepdims=True)
                for kk in range(FFN_CONV - 1)]
        rows.append(jnp.sum(dgc * gt, axis=0, keepdims=True))
        dcw_part = jnp.concatenate(rows + [jnp.zeros((8 - FFN_CONV, tc), F32)], axis=0)
        dcb_part = jnp.sum(dgc, axis=0, keepdims=True)

        @pl.when(step == 0)
        def _():
            dcw_ref[...] = dcw_part
            dcb_ref[...] = dcb_part

        @pl.when(step > 0)
        def _():
            dcw_ref[...] += dcw_part
            dcb_ref[...] += dcb_part

    tile = pl.BlockSpec((tt, tc), lambda j, s: (ntt - 1 - s, j))
    prev8 = pl.BlockSpec((8, tc), lambda j, s: (jnp.maximum((ntt - 1 - s) * (tt // 8) - 1, 0), j))
    return _pcall(
        body, name=name, grid=(F // tc, ntt),
        in_specs=[tile, prev8, tile, tile, pl.BlockSpec((FFN_CONV, tc), lambda j, s: (0, j)),
                  pl.BlockSpec((1, tc), lambda j, s: (0, j))],
        out_specs=[tile, tile, pl.BlockSpec((8, tc), lambda j, s: (0, j)), pl.BlockSpec((1, tc), lambda j, s: (0, j))],
        out_shape=[jax.ShapeDtypeStruct((T, F), BF16), jax.ShapeDtypeStruct((T, F), BF16),
                   jax.ShapeDtypeStruct((8, F), F32), jax.ShapeDtypeStruct((1, F), F32)],
        args=(g, g, u, dact, cw, cb), scratch=[pltpu.VMEM((8, tc), F32)], ride=ride)


def _sgu_norm(zv, g_ref, b_ref):
    v = _gelu(zv)
    mu = jnp.mean(v, axis=-1, keepdims=True)
    xc = v - mu
    rstd = lax.rsqrt(jnp.mean(xc * xc, axis=-1, keepdims=True) + NORM_EPS)
    xhat = xc * rstd
    return xhat, rstd, xhat * g_ref[...] + b_ref[...]


def _sgu_fwd(zc, ln_g, ln_b, wm, bmap, *, name):
    T = zc.shape[0]
    W = SGU_WIDTH
    tt = ROW_TILE
    nch = tt // CHUNK

    def body(z_ref, g_ref, b_ref, wm_ref, bm_ref, p_ref):
        u = _gelu(z_ref[:, :W])
        _, _, vn = _sgu_norm(z_ref[:, W:], g_ref, b_ref)
        vn = vn.astype(BF16)
        for n in range(nch):
            rows = slice(n * CHUNK, (n + 1) * CHUNK)
            for gi in range(SGU_GROUPS):
                cols = slice(gi * LANES, (gi + 1) * LANES)
                s = _dot_nn(wm_ref[gi], vn[rows, cols]) + bm_ref[:, cols]
                p_ref[rows, cols] = (u[rows, cols] * s).astype(BF16)

    const2 = lambda r, c: pl.BlockSpec((r, c), lambda i: (0, 0))
    return pl.pallas_call(
        body, name=name, grid=(T // tt,),
        in_specs=[pl.BlockSpec((tt, 2 * W), lambda i: (i, 0)), const2(1, W), const2(1, W),
                  pl.BlockSpec((SGU_GROUPS, CHUNK, CHUNK), lambda i: (0, 0, 0)), const2(CHUNK, W)],
        out_specs=pl.BlockSpec((tt, W), lambda i: (i, 0)),
        out_shape=jax.ShapeDtypeStruct((T, W), BF16),
        compiler_params=_cparams(("parallel",)),
    )(zc, ln_g, ln_b, wm, bmap)


def _sgu_bwd(zc, dp, ln_g, ln_b, wm, bmap, *, name, ride=None):
    T = zc.shape[0]
    W = SGU_WIDTH
    tt = ROW_TILE
    nch = tt // CHUNK
    nsteps = T // tt

    def body(z_ref, dp_ref, g_ref, b_ref, wm_ref, bm_ref, dz_ref, dg_ref, db_ref, dwm_ref, dbm_ref,
             s_scr, dvn_scr):
        step = pl.program_id(0)
        zu = z_ref[:, :W]
        zv = z_ref[:, W:]
        u, dgu = _gelu_and_grad(zu)
        xhat, rstd, vn = _sgu_norm(zv, g_ref, b_ref)
        vnb = vn.astype(BF16)
        dpf = dp_ref[...].astype(F32)
        ds = dpf * u

        @pl.when(step == 0)
        def _():
            dwm_ref[...] = jnp.zeros_like(dwm_ref)
            dbm_ref[...] = jnp.zeros_like(dbm_ref)

        for n in range(nch):
            rows = slice(n * CHUNK, (n + 1) * CHUNK)
            for gi in range(SGU_GROUPS):
                cols = slice(gi * LANES, (gi + 1) * LANES)
                s_scr[rows, cols] = _dot_nn(wm_ref[gi], vnb[rows, cols]) + bm_ref[:, cols]
                dsb = ds[rows, cols]
                dvn_scr[rows, cols] = _dot_tn(wm_ref[gi], dsb)
                dwm_ref[gi] += _dot_nt(dsb, vnb[rows, cols])
                dbm_ref[:, cols] += dsb
        dz_ref[:, :W] = (dpf * s_scr[...] * dgu).astype(BF16)
        dvn = dvn_scr[...]
        dxhat = dvn * g_ref[...]
        dv = rstd * (dxhat - jnp.mean(dxhat, axis=-1, keepdims=True)
                     - xhat * jnp.mean(dxhat * xhat, axis=-1, keepdims=True))
        _, dgv = _gelu_and_grad(zv)
        dz_ref[:, W:] = (dv * dgv).astype(BF16)
        dg_part = jnp.sum(dvn * xhat, axis=0, keepdims=True)
        db_part = jnp.sum(dvn, axis=0, keepdims=True)

        @pl.when(step == 0)
        def _():
            dg_ref[...] = dg_part
            db_ref[...] = db_part

        @pl.when(step > 0)
        def _():
            dg_ref[...] += dg_part
            db_ref[...] += db_part

        @pl.when(step == nsteps - 1)
        def _():
            for gi in range(SGU_GROUPS):
                cols = slice(gi * LANES, (gi + 1) * LANES)
                tot = jnp.sum(dbm_ref[:, cols], axis=1, keepdims=True)
                dbm_ref[:, cols] = jnp.broadcast_to(tot, (CHUNK, LANES))

    const2 = lambda r, c: pl.BlockSpec((r, c), lambda i: (0, 0))
    wspec = pl.BlockSpec((SGU_GROUPS, CHUNK, CHUNK), lambda i: (0, 0, 0))
    return _pcall(
        body, name=name, grid=(nsteps,),
        in_specs=[pl.BlockSpec((tt, 2 * W), lambda i: (i, 0)), pl.BlockSpec((tt, W), lambda i: (i, 0)),
                  const2(1, W), const2(1, W), wspec, const2(CHUNK, W)],
        out_specs=[pl.BlockSpec((tt, 2 * W), lambda i: (i, 0)), const2(1, W), const2(1, W), wspec, const2(CHUNK, W)],
        out_shape=[jax.ShapeDtypeStruct((T, 2 * W), BF16), jax.ShapeDtypeStruct((1, W), F32),
                   jax.ShapeDtypeStruct((1, W), F32), jax.ShapeDtypeStruct((SGU_GROUPS, CHUNK, CHUNK), F32),
                   jax.ShapeDtypeStruct((CHUNK, W), F32)],
        args=(zc, dp, ln_g, ln_b, wm, bmap), scratch=[pltpu.VMEM((tt, W), F32), pltpu.VMEM((tt, W), F32)], ride=ride)


def _rope_tables(positions):
    half = QK_ROPE // 2
    inv_freq = jnp.exp(-math.log(ROPE_BASE) * jnp.arange(half, dtype=F32) / half)
    ang = positions.reshape(-1).astype(F32)[:, None] * inv_freq
    cos = jnp.cos(ang)
    sin = jnp.sin(ang)
    n = ang.shape[0]
    tail = LANES - QK_NOPE - QK_ROPE
    cos_t = jnp.concatenate([jnp.ones((n, QK_NOPE), F32), cos, cos, jnp.ones((n, tail), F32)], axis=1)
    sin_t = jnp.concatenate([jnp.zeros((n, QK_NOPE), F32), -sin, sin, jnp.zeros((n, tail), F32)], axis=1)
    return cos_t, sin_t


SGU_GROUP_DIM = SGU_WIDTH // SGU_GROUPS
_O1, _O2, _O3, _O4 = Q_LORA, Q_LORA + KV_LORA, Q_LORA + KV_LORA + QK_ROPE, Q_LORA + KV_LORA + QK_ROPE + LRU_WIDTH
_A0, _A1, _A2 = 2 * LRU_WIDTH, 2 * LRU_WIDTH + Q_LORA, 2 * LRU_WIDTH + Q_LORA + KV_LORA
_A3 = _A2 + QK_NOPE
Z_Q_BLOCK, Z_KV_BLOCK, Z_KPE_BLOCK = _A0 // Q_LORA, _A1 // KV_LORA, _A2 // LANES


def _perm_w_in(w_in):
    zeros = lambda n: jnp.zeros((w_in.shape[0], n), w_in.dtype)
    return jnp.concatenate([w_in[:, _O3:_O4], w_in[:, _O4:], w_in[:, :_O1], w_in[:, _O1:_O2], zeros(QK_NOPE),
                            w_in[:, _O2:_O3], zeros(LANES - QK_NOPE - QK_ROPE)], axis=1)


def _unperm_w_in(w):
    return jnp.concatenate([w[:, _A0:_A1], w[:, _A1:_A2], w[:, _A3:_A3 + QK_ROPE], w[:, :LRU_WIDTH],
                            w[:, LRU_WIDTH:_A0]], axis=1)


def _head_blocks(w, d):
    r = w.shape[0]
    return jnp.pad(w.reshape(r, MLA_HEADS, d), ((0, 0), (0, 0), (0, LANES - d))).reshape(r, MLA_HEADS * LANES)


def _from_head_blocks(w, d):
    r = w.shape[0]
    return w.reshape(r, MLA_HEADS, LANES)[:, :, :d].reshape(r, MLA_HEADS * d)


def _split_kv(w_kv):
    r = w_kv.shape[0]
    w3 = w_kv.reshape(r, MLA_HEADS, QK_NOPE + V_HEAD)
    return _head_blocks(w3[:, :, :QK_NOPE].reshape(r, -1), QK_NOPE), w3[:, :, QK_NOPE:].reshape(r, -1)


def _join_kv(w_k, w_v):
    r = w_k.shape[0]
    return jnp.concatenate([_from_head_blocks(w_k, QK_NOPE).reshape(r, MLA_HEADS, QK_NOPE),
                            w_v.reshape(r, MLA_HEADS, V_HEAD)], axis=2).reshape(r, -1)


def _prep_small(w):
    p = {n: w[n] for n in w if n not in BIG}
    eye = jnp.eye(LRU_HEADS, dtype=F32)
    dense = lambda wg: (wg[:, :, None, :] * eye[:, None, :, None]).reshape(LRU_WIDTH, LRU_WIDTH).astype(BF16)
    p["wa_d"] = dense(w["ab_w_rg_a"][0])
    p["wx_d"] = dense(w["ab_w_rg_x"][0])
    causal = jnp.tril(jnp.ones((CHUNK, CHUNK), F32))
    p["wm"] = (w["c_w_s"][0] * causal).astype(BF16)
    p["bmap"] = jnp.repeat(w["c_b_s"][0].T, SGU_GROUP_DIM, axis=1)
    return p


def _prep_big(ab_w_in, ab_w_q_b, ab_w_kv_b):
    return {"w_in_p": _perm_w_in(ab_w_in).astype(BF16),
            "w_q_p": _head_blocks(ab_w_q_b, QK_NOPE + QK_ROPE).astype(BF16),
            "w_kv_p": jnp.concatenate(_split_kv(ab_w_kv_b), axis=1).astype(BF16)}


def _ffn_fwd(h, l, p, S, rides):
    hn = _rms_fwd(h, p["ffn_norm"][l], name=f"ffn{l}_norm")
    g = _mm(hn, p["ffn_gate_t"][l], tb=True, name=f"ffn{l}_gate", ride=rides.get(f"ffn{l}_gate"))
    u = _mm(hn, p["ffn_up_t"][l], tb=True, name=f"ffn{l}_up", ride=rides.get(f"ffn{l}_up"))
    act = _ffn_act_fwd(g, u, p["ffn_conv_w"][l], p["ffn_conv_b"][l][None], S=S, name=f"ffn{l}_act",
                       ride=rides.get(f"ffn{l}_act"))
    out = _mm(act, p["ffn_down"][l], res=h, name=f"ffn{l}_down", ride=rides.get(f"ffn{l}_down"))
    return out, (hn, g, u, act)


def _ffn_bwd(dh, h_in, l, p, saved, S, rides, grads_ready, also_ready=None):
    hn, g, u, act = saved
    dact = _mm(dh, p["ffn_down"][l], tb=True, out_dtype=BF16, name=f"ffn{l}_dact", ride=rides.get(f"ffn{l}_dact"))
    dw_down = _mm(act, dh, ta=True, out_dtype=BF16, name=f"ffn{l}_dwdown")
    dg, du, dcw, dcb = _ffn_act_bwd(g, u, dact, p["ffn_conv_w"][l], p["ffn_conv_b"][l][None], S=S,
                                    name=f"ffn{l}_dactbwd", ride=rides.get(f"ffn{l}_dactbwd"))
    dhn = _mm(dg, p["ffn_gate_t"][l], name=f"ffn{l}_dhn_g")
    dhn = _mm(du, p["ffn_up_t"][l], res=dhn, name=f"ffn{l}_dhn_u")
    dw_gate_t = _mm(dg, hn, ta=True, out_dtype=BF16, name=f"ffn{l}_dwgate")
    dw_up_t = _mm(du, hn, ta=True, out_dtype=BF16, name=f"ffn{l}_dwup")
    grads_ready(l, {**(also_ready or {}), "ffn_gate_t": dw_gate_t, "ffn_up_t": dw_up_t, "ffn_down": dw_down})
    dh_in, dnorm = _rms_bwd(h_in, p["ffn_norm"][l], dhn, res=dh, name=f"ffn{l}_dnorm", ride=rides.get(f"ffn{l}_dnorm"))
    grads = dict(ffn_norm=dnorm[0], ffn_gate_t=dw_gate_t, ffn_up_t=dw_up_t, ffn_conv_w=dcw[:FFN_CONV],
                 ffn_conv_b=dcb[0], ffn_down=dw_down)
    return dh_in, grads


def _local_step(x, positions, target, p, rides=None, grads_ready=None):
    rides = {} if rides is None else rides
    grads_ready = grads_ready or (lambda layer, ready: None)
    B, S, D = x.shape
    T = B * S
    H = MLA_HEADS
    xf = x.reshape(T, D)
    tgt = target.reshape(T, D)
    cos, sin = _rope_tables(positions)

    hn0 = _rms_fwd(xf, p["ab_norm"][0], name="ab_norm", ride=rides.get("ab_norm"))
    z = _mm(hn0, p["w_in_p"], name="ab_in")
    cqn = _rms_fwd(z, p["ab_q_norm"][0], cb=Z_Q_BLOCK, name="q_norm")
    ckvn = _rms_fwd(z, p["ab_kv_norm"][0], cb=Z_KV_BLOCK, name="kv_norm")
    q = _mm(cqn, p["w_q_p"], name="q_up")
    kv = _mm(ckvn, p["w_kv_p"], out_dtype=BF16, name="kv_up")
    qs = _rope_q(q, cos, sin, name="q_rope")
    kk = _key_blocks(kv, z, cos, sin, kpe_block=Z_KPE_BLOCK, name="k_rope")
    att = dict(B=B, S=S, v_block0=H)
    o, lse = _attn_fwd(qs, kk, kv, name="attn_fwd", ride=rides.get("attn_fwd"), **att)
    lru_par = (p["ab_conv_w"][0], p["ab_conv_b"], p["wa_d"], p["ab_b_rg_a"], p["wx_d"], p["ab_b_rg_x"], p["ab_lambda"])
    y_lru, hs = _lru_fwd(z, *lru_par, S=S, name="lru_fwd", ride=rides.get("lru_fwd"))
    n_att = H * V_HEAD
    w_out_a, w_out_b = p["ab_w_out"][:n_att], p["ab_w_out"][n_att:]
    h1 = _mm(y_lru, w_out_b, res=_mm(o, w_out_a, res=xf, name="ab_out_a"), name="ab_out_b")
    h2, ffn0 = _ffn_fwd(h1, 0, p, S, rides)

    hn2 = _rms_fwd(h2, p["c_norm"][0], name="c_norm")
    zc = _mm(hn2, p["c_w_in_t"], tb=True, name="c_in")
    pg = _sgu_fwd(zc, p["c_ln_g"], p["c_ln_b"], p["wm"], p["bmap"], name="sgu_fwd")
    h3 = _mm(pg, p["c_w_out"], res=h2, name="c_out")
    h4, ffn1 = _ffn_fwd(h3, 1, p, S, rides)

    loss_row, dh4, dfinal = _final_fwd_bwd(h4, p["final_norm"], tgt, name="final")

    dh3, g_ffn1 = _ffn_bwd(dh4, h3, 1, p, ffn1, S, rides, grads_ready)
    dpg = _mm(dh3, p["c_w_out"], tb=True, out_dtype=BF16, name="c_dp")
    dw_c_out = _mm(pg, dh3, ta=True, out_dtype=BF16, name="c_dwout")
    dzc, dlng, dlnb, dwm, dbm = _sgu_bwd(zc, dpg, p["c_ln_g"], p["c_ln_b"], p["wm"], p["bmap"], name="sgu_bwd",
                                         ride=rides.get("sgu_bwd"))
    dhn2 = _mm(dzc, p["c_w_in_t"], name="c_dhn")
    dw_c_in_t = _mm(dzc, hn2, ta=True, out_dtype=BF16, name="c_dwin")
    dh2, dcnorm = _rms_bwd(h2, p["c_norm"][0], dhn2, res=dh3, name="c_dnorm")
    dh1, g_ffn0 = _ffn_bwd(dh2, h1, 0, p, ffn0, S, rides, grads_ready, {"c_w_in_t": dw_c_in_t, "c_w_out": dw_c_out})

    do = _mm(dh1, w_out_a, tb=True, name="ab_do")
    dy_lru = _mm(dh1, w_out_b, tb=True, out_dtype=BF16, name="ab_dylru")
    dw_out = jnp.concatenate([_mm(o, dh1, ta=True, out_dtype=BF16, name="ab_dwout_a"),
                              _mm(y_lru, dh1, ta=True, out_dtype=BF16, name="ab_dwout_b")], axis=0)
    dq, delta = _attn_dq(qs, kk, kv, o, lse, do, name="attn_dq", ride=rides.get("attn_dq"), **att)
    nq = S // min(ATT_BLOCK, S)
    rows = lambda a: a.reshape(B, H, nq, S // nq)
    dk, dv = _attn_dkv(qs, kk, kv, rows(lse), rows(delta), do, name="attn_dkv", ride=rides.get("attn_dkv"), **att)
    dq_full = _rope_q_bwd(dq, cos, sin, name="q_rope_bwd")
    dkr = _key_rope_bwd(dk, cos, sin, name="k_rope_bwd")
    n_key = H * LANES
    w_k_p, w_v_p = p["w_kv_p"][:, :n_key], p["w_kv_p"][:, n_key:]
    dcqn = _mm(dq_full, p["w_q_p"], tb=True, name="q_dlat")
    dw_q_p = _mm(cqn, dq_full, ta=True, out_dtype=BF16, name="q_dw")
    dckvn = _mm(dv, w_v_p, tb=True, res=_mm(dk, w_k_p, tb=True, name="k_dlat"), name="v_dlat")
    dw_k_p = _mm(ckvn, dk, ta=True, out_dtype=BF16, name="k_dw")
    dw_v_p = _mm(ckvn, dv, ta=True, out_dtype=BF16, name="v_dw")
    dcq, dqnorm = _rms_bwd(z, p["ab_q_norm"][0], dcqn, cb=Z_Q_BLOCK, out_dtype=BF16, name="q_dnorm")
    dckv, dkvnorm = _rms_bwd(z, p["ab_kv_norm"][0], dckvn, cb=Z_KV_BLOCK, out_dtype=BF16, name="kv_dnorm")
    dxl, dgate, dcw, dcb, dwa, dba, dwx, dbx, dlam = _lru_bwd(z, hs, dy_lru, *lru_par, S=S, name="lru_bwd")
    dz = jnp.concatenate([dxl, dgate, dcq, dckv, dkr], axis=1)
    dhn0 = _mm(dz, p["w_in_p"], tb=True, name="ab_dhn")
    dw_in_p = _mm(hn0, dz, ta=True, out_dtype=BF16, name="ab_dwin")
    dx, dabnorm = _rms_bwd(xf, p["ab_norm"][0], dhn0, res=dh1, name="ab_dnorm")

    blocks = lambda dd: jnp.stack([dd[i * LRU_BLOCK:(i + 1) * LRU_BLOCK, i * LRU_BLOCK:(i + 1) * LRU_BLOCK]
                                   for i in range(LRU_HEADS)])
    causal = jnp.tril(jnp.ones((CHUNK, CHUNK), F32))
    grads = {
        "ab_norm": dabnorm, "w_in_p": dw_in_p, "ab_q_norm": dqnorm, "w_q_p": dw_q_p,
        "ab_kv_norm": dkvnorm, "w_k_p": dw_k_p, "w_v_p": dw_v_p, "ab_conv_w": dcw[:LRU_CONV][None], "ab_conv_b": dcb,
        "ab_w_rg_a": blocks(dwa)[None], "ab_b_rg_a": dba, "ab_w_rg_x": blocks(dwx)[None], "ab_b_rg_x": dbx,
        "ab_lambda": dlam, "ab_w_out": dw_out,
        "c_norm": dcnorm, "c_w_in_t": dw_c_in_t, "c_ln_g": dlng, "c_ln_b": dlnb,
        "c_w_s": (dwm * causal)[None], "c_b_s": dbm[:, ::SGU_GROUP_DIM].T[None], "c_w_out": dw_c_out,
        "final_norm": dfinal[0],
    }
    for name in ("ffn_norm", "ffn_conv_w", "ffn_conv_b"):
        grads[name] = jnp.stack([g_ffn0[name], g_ffn1[name]])
    for name in ("ffn_gate_t", "ffn_up_t", "ffn_down"):
        grads[name] = [g_ffn0[name], g_ffn1[name]]
    return loss_row, dx.reshape(B, S, D), grads


ANY = pl.BlockSpec(memory_space=pl.ANY)


def _place():
    x, y, c = lax.axis_index("x"), lax.axis_index("y"), lax.axis_index("c")
    chips = [(1 - x, y), (x, 1 - y), (1 - x, 1 - y)]
    return x, y, c, 2 * x + y, (x, y, 1 - c), chips


def _remote(src, dst, send_sems, recv_sems, k, to):
    return pltpu.make_async_remote_copy(src_ref=src, dst_ref=dst, send_sem=send_sems.at[k], recv_sem=recv_sems.at[k],
                                        device_id=to, device_id_type=MESH)


class _Exchange:
    def __init__(self, arrs, out_shapes, n_sems, start, finish):
        self.arrs, self.out_shapes, self.n_sems, self.start, self.finish = list(arrs), out_shapes, n_sems, start, finish

    @property
    def in_specs(self):
        return [ANY] * len(self.arrs)

    @property
    def out_specs(self):
        return [ANY] * len(self.out_shapes)

    @property
    def scratch(self):
        return [pltpu.SemaphoreType.DMA((self.n_sems,)), pltpu.SemaphoreType.DMA((self.n_sems,))]

    def split(self, refs):
        n = len(self.arrs)
        return refs[:n], refs[n:n + len(self.out_shapes)], refs[-2], refs[-1]

    def run(self, name):
        def body(*refs):
            parts = self.split(refs)
            self.start(*parts)
            self.finish(*parts)

        return pl.pallas_call(body, name=name, in_specs=self.in_specs, out_specs=self.out_specs,
                              out_shape=self.out_shapes, scratch_shapes=self.scratch)(*self.arrs)


def _put(buf, piece, idx, axis):
    return lax.dynamic_update_slice_in_dim(buf, jnp.expand_dims(piece, axis).astype(buf.dtype), idx, axis)


def _all_gather(arrs):
    n = len(arrs)

    def start(ins, outs, send_sems, recv_sems):
        x, y, c, j, sib, chips = _place()
        for i in range(n):
            for k, (cx, cy) in enumerate(chips):
                _remote(ins[i].at[:, c], outs[i].at[:, j, c], send_sems, recv_sems, 6 * i + k, (cx, cy, c)).start()

    def finish(ins, outs, send_sems, recv_sems):
        x, y, c, j, sib, chips = _place()
        passed = []
        for i in range(n):
            for k, (cx, cy) in enumerate(chips):
                got = outs[i].at[:, 2 * cx + cy, c]
                _remote(got, got, send_sems, recv_sems, 6 * i + k, (cx, cy, c)).wait_recv()
                cp = _remote(got, got, send_sems, recv_sems, 6 * i + 3 + k, sib)
                cp.start()
                passed.append(cp)
        for i in range(n):
            for k, (cx, cy) in enumerate(chips):
                got = outs[i].at[:, 2 * cx + cy, 1 - c]
                _remote(got, got, send_sems, recv_sems, 6 * i + 3 + k, sib).wait_recv()
                _remote(ins[i].at[:, c], ins[i].at[:, c], send_sems, recv_sems, 6 * i + k, sib).wait_send()
        for cp in passed:
            cp.wait_send()

    shapes = [jax.ShapeDtypeStruct((a.shape[0], N_CHIPS) + a.shape[1:], a.dtype) for a in arrs]
    return _Exchange(arrs, shapes, 6 * n, start, finish)


class _Offset:
    def __init__(self, sems, k0):
        self.sems, self.k0 = sems, k0

    @property
    def at(self):
        return self

    def __getitem__(self, k):
        return self.sems.at[self.k0 + k]


def _merge(a, b):
    n_in, n_out = len(a.arrs), len(a.out_shapes)

    def both(fa, fb):
        def f(ins, outs, send_sems, recv_sems):
            fa(ins[:n_in], outs[:n_out], send_sems, recv_sems)
            fb(ins[n_in:], outs[n_out:], _Offset(send_sems, a.n_sems), _Offset(recv_sems, a.n_sems))
        return f

    return _Exchange(a.arrs + b.arrs, a.out_shapes + b.out_shapes, a.n_sems + b.n_sems,
                     both(a.start, b.start), both(a.finish, b.finish))


def _pair_swap(arrs):
    n = len(arrs)

    def start(ins, outs, send_sems, recv_sems):
        x, y, c, j, sib, chips = _place()
        for i in range(n):
            _remote(ins[i].at[:, 1 - c], outs[i], send_sems, recv_sems, i, sib).start()

    def finish(ins, outs, send_sems, recv_sems):
        x, y, c, j, sib, chips = _place()
        for i in range(n):
            _remote(ins[i].at[:, 1 - c], outs[i], send_sems, recv_sems, i, sib).wait()

    shapes = [jax.ShapeDtypeStruct((a.shape[0],) + a.shape[2:], a.dtype) for a in arrs]
    return _Exchange(arrs, shapes, n, start, finish)


def _pair_send(arrs):
    n = len(arrs)

    def start(ins, outs, send_sems, recv_sems):
        x, y, c, j, sib, chips = _place()
        for i in range(n):
            _remote(ins[i], outs[i], send_sems, recv_sems, i, sib).start()

    def finish(ins, outs, send_sems, recv_sems):
        x, y, c, j, sib, chips = _place()
        for i in range(n):
            _remote(ins[i], outs[i], send_sems, recv_sems, i, sib).wait()

    shapes = [jax.ShapeDtypeStruct(a.shape, a.dtype) for a in arrs]
    return _Exchange(arrs, shapes, n, start, finish)


def _chip_exchange(arrs, *, scatter):
    n = len(arrs)

    def copies(ins, outs, send_sems, recv_sems):
        x, y, c, j, sib, chips = _place()
        return [(_remote(ins[i].at[2 * cx + cy] if scatter else ins[i], outs[i].at[j], send_sems, recv_sems,
                         3 * i + k, (cx, cy, c)),
                 _remote(outs[i].at[2 * cx + cy], outs[i].at[2 * cx + cy], send_sems, recv_sems, 3 * i + k, (cx, cy, c)))
                for i in range(n) for k, (cx, cy) in enumerate(chips)]

    def start(*refs):
        for out, _ in copies(*refs):
            out.start()

    def finish(*refs):
        for out, back in copies(*refs):
            back.wait_recv()
            out.wait_send()

    shapes = [jax.ShapeDtypeStruct((N_CHIPS,) + a.shape[-2:], a.dtype) for a in arrs]
    return _Exchange(arrs, shapes, 3 * n, start, finish)


FLAT_ROWS = 512


def _add2(a, b, *, out_dtype, name):
    n, R, L = a.shape
    tr = _tile(R, FLAT_ROWS, 16)

    def body(a_ref, b_ref, o_ref):
        o_ref[...] = (a_ref[...].astype(F32) + b_ref[...].astype(F32)).astype(out_dtype)

    spec = pl.BlockSpec((n, tr, L), lambda i: (0, i, 0))
    return pl.pallas_call(
        body, name=name, grid=(R // tr,), in_specs=[spec, spec], out_specs=spec,
        out_shape=jax.ShapeDtypeStruct(a.shape, out_dtype), compiler_params=_cparams(("parallel",)),
    )(a, b)


def _sum_slots(buf, *, name):
    n, R, L = buf.shape
    tr = _tile(R, FLAT_ROWS, 16)

    def body(b_ref, o_ref):
        acc = b_ref[0].astype(F32)
        for k in range(1, n):
            acc = acc + b_ref[k].astype(F32)
        o_ref[...] = acc

    return pl.pallas_call(
        body, name=name, grid=(R // tr,), in_specs=[pl.BlockSpec((n, tr, L), lambda i: (0, i, 0))],
        out_specs=pl.BlockSpec((tr, L), lambda i: (i, 0)),
        out_shape=jax.ShapeDtypeStruct((R, L), F32), compiler_params=_cparams(("parallel",)),
    )(buf)


def _adamw_update(w, g, m, v):
    c1 = 1.0 - ADAM_B1 ** ADAM_STEP
    c2 = 1.0 - ADAM_B2 ** ADAM_STEP
    m = ADAM_B1 * m + (1.0 - ADAM_B1) * g
    v = ADAM_B2 * v + (1.0 - ADAM_B2) * (g * g)
    return -ADAM_LR * ((m / c1) / (jnp.sqrt(v / c2) + ADAM_EPS) + ADAM_WD * w), m, v


def _adamw_halves(w, m, v, own, other, *, name):
    NL, R, L = w.shape
    h = R // 2
    tr = _tile(h, FLAT_ROWS, 16)
    nt = h // tr

    def body(*refs):
        w_ref, m_ref, v_ref = refs[:3]
        own_refs, other_refs = refs[3:3 + NL], refs[3 + NL:3 + 2 * NL]
        d_ref, nm_ref, nv_ref, g_ref = refs[3 + 2 * NL:]
        layer, half = pl.program_id(0), pl.program_id(1)
        mine = half == lax.axis_index("c")
        g = jnp.where(mine, own_refs[0][...], other_refs[0][...])
        for l in range(1, NL):
            g = jnp.where(layer == l, jnp.where(mine, own_refs[l][...], other_refs[l][...]), g)
        d, mm, vv = _adamw_update(w_ref[0], g, m_ref[0], v_ref[0])
        d_ref[0], nm_ref[0], nv_ref[0], g_ref[0] = d, mm, vv, g

    spec = pl.BlockSpec((1, tr, L), lambda l, hh, i: (l, hh * nt + i, 0))
    part = pl.BlockSpec((tr, L), lambda l, hh, i: (i, 0))
    sh = jax.ShapeDtypeStruct((NL, R, L), F32)
    return pl.pallas_call(
        body, name=name, grid=(NL, 2, nt), in_specs=[spec] * 3 + [part] * (2 * NL), out_specs=[spec] * 4,
        out_shape=[sh] * 4, compiler_params=_cparams(("parallel", "parallel", "parallel")),
    )(w, m, v, *own, *other)


def _adamw(w, g, m, v, *, name):
    NL, R, L = w.shape
    tr = _tile(R, FLAT_ROWS, 16)

    def body(w_ref, g_ref, m_ref, v_ref, d_ref, nm_ref, nv_ref):
        d_ref[...], nm_ref[...], nv_ref[...] = _adamw_update(w_ref[...], g_ref[...], m_ref[...], v_ref[...])

    spec = pl.BlockSpec((1, tr, L), lambda l, i: (l, i, 0))
    sh = jax.ShapeDtypeStruct((NL, R, L), F32)
    return pl.pallas_call(
        body, name=name, grid=(NL, R // tr), in_specs=[spec] * 4, out_specs=[spec] * 3, out_shape=[sh] * 3,
        compiler_params=_cparams(("parallel", "parallel")),
    )(w, g, m, v)


WEIGHT_NAMES = ["ab_norm", "ab_w_in", "ab_q_norm", "ab_w_q_b", "ab_kv_norm", "ab_w_kv_b", "ab_conv_w", "ab_conv_b",
                "ab_w_rg_a", "ab_b_rg_a", "ab_w_rg_x", "ab_b_rg_x", "ab_lambda", "ab_w_out", "c_norm", "c_w_in",
                "c_ln_g", "c_ln_b", "c_w_s", "c_b_s", "c_w_out", "ffn_norm", "ffn_w_gate", "ffn_w_up", "ffn_conv_w",
                "ffn_conv_b", "ffn_w_down", "final_norm"]
BIG = {"ab_w_in": 2, "ab_w_q_b": 2, "ab_w_kv_b": 2, "ab_w_out": 1, "c_w_in": 2, "c_w_out": 1,
       "ffn_w_gate": 2, "ffn_w_up": 2, "ffn_w_down": 1}
SMALL_SHARDED = {"ab_conv_w": 2, "c_norm": 1, "c_ln_g": 1, "c_ln_b": 1, "ffn_conv_w": 2}
SMALL_REPLICATED = [n for n in WEIGHT_NAMES if n not in BIG and n not in SMALL_SHARDED]


def _rows(n_elems, mult):
    r = -(-n_elems // LANES)
    return -(-r // mult) * mult


def _flat(parts, rows):
    flat = jnp.concatenate([a.reshape(-1) for a in parts])
    return jnp.pad(flat, (0, rows * LANES - flat.shape[0])).reshape(rows, LANES)


def _unflat(flat, shapes):
    flat = flat.reshape(-1)
    out, off = [], 0
    for s in shapes:
        n = math.prod(s)
        out.append(flat[off:off + n].reshape(s))
        off += n
    return out


def _join_shards(a, axis):
    a = jnp.moveaxis(a, 0, axis)
    return a.reshape(a.shape[:axis] + (a.shape[axis] * a.shape[axis + 1],) + a.shape[axis + 2:])


def kernel(x, positions, ab_norm, ab_w_in, ab_q_norm, ab_w_q_b, ab_kv_norm, ab_w_kv_b, ab_conv_w, ab_conv_b, ab_w_rg_a, ab_b_rg_a, ab_w_rg_x, ab_b_rg_x, ab_lambda, ab_w_out, c_norm, c_w_in, c_ln_g, c_ln_b, c_w_s, c_b_s, c_w_out, ffn_norm, ffn_w_gate, ffn_w_up, ffn_conv_w, ffn_conv_b, ffn_w_down, final_norm, loss_target, m_ab_norm, m_ab_w_in, m_ab_q_norm, m_ab_w_q_b, m_ab_kv_norm, m_ab_w_kv_b, m_ab_conv_w, m_ab_conv_b, m_ab_w_rg_a, m_ab_b_rg_a, m_ab_w_rg_x, m_ab_b_rg_x, m_ab_lambda, m_ab_w_out, m_c_norm, m_c_w_in, m_c_ln_g, m_c_ln_b, m_c_w_s, m_c_b_s, m_c_w_out, m_ffn_norm, m_ffn_w_gate, m_ffn_w_up, m_ffn_conv_w, m_ffn_conv_b, m_ffn_w_down, m_final_norm, v_ab_norm, v_ab_w_in, v_ab_q_norm, v_ab_w_q_b, v_ab_kv_norm, v_ab_w_kv_b, v_ab_conv_w, v_ab_conv_b, v_ab_w_rg_a, v_ab_b_rg_a, v_ab_w_rg_x, v_ab_b_rg_x, v_ab_lambda, v_ab_w_out, v_c_norm, v_c_w_in, v_c_ln_g, v_c_ln_b, v_c_w_s, v_c_b_s, v_c_w_out, v_ffn_norm, v_ffn_w_gate, v_ffn_w_up, v_ffn_conv_w, v_ffn_conv_b, v_ffn_w_down, v_final_norm):
    given = dict(locals())
    w = {n: given[n] for n in WEIGHT_NAMES}
    m = {n: given["m_" + n] for n in WEIGHT_NAMES}
    v = {n: given["v_" + n] for n in WEIGHT_NAMES}
    c = lax.axis_index("c")
    chip = 2 * lax.axis_index("x") + lax.axis_index("y")

    halves = lambda a: a.reshape(a.shape[0], 2, a.shape[1] // 2, a.shape[2])
    tr = lambda a: jnp.swapaxes(a, 1, 2)
    send = {"ab_w_in": w["ab_w_in"], "ab_w_q_b": w["ab_w_q_b"], "ab_w_kv_b": w["ab_w_kv_b"], "ab_w_out": w["ab_w_out"],
            "c_w_in": tr(w["c_w_in"]), "c_w_out": w["c_w_out"], "ffn_w_gate": tr(w["ffn_w_gate"]),
            "ffn_w_up": tr(w["ffn_w_up"]), "ffn_w_down": w["ffn_w_down"]}
    small_rows = _rows(sum(w[n].size for n in SMALL_SHARDED), 16)
    small_sh = _flat([w[n] for n in SMALL_SHARDED], small_rows).reshape(1, 2, small_rows // 2, LANES)
    first_names = ["ab_w_in", "ab_w_q_b", "ab_w_kv_b", "ab_w_out"]
    mine = {n: halves(send[n].astype(BF16)) for n in BIG}

    def put_own(own, arrived):
        a = _put(arrived, own, chip, 1)
        return a.reshape(a.shape[0], -1, a.shape[-1])

    p = {"ab_norm": w["ab_norm"], "ffn_gate_t": {}, "ffn_up_t": {}, "ffn_down": {}}
    first = [mine[n] for n in first_names] + [small_sh]

    def first_arrived(got):
        full = {n: put_own(o, a) for n, o, a in zip(first_names + ["small"], first, got)}
        unshard = lambda a: jnp.swapaxes(a.reshape(N_CHIPS, -1, a.shape[-1]), 0, 1).reshape(-1, N_CHIPS * a.shape[-1])
        p.update(_prep_big(unshard(full["ab_w_in"][0]), unshard(full["ab_w_q_b"][0]), unshard(full["ab_w_kv_b"][0])))
        p["ab_w_out"] = full["ab_w_out"][0]
        small_full = dict(w)
        off = 0
        small_got = full["small"].reshape(N_CHIPS, -1)
        for n, ax in SMALL_SHARDED.items():
            seg = small_got[:, off:off + w[n].size].reshape((N_CHIPS,) + w[n].shape)
            small_full[n] = _join_shards(seg, ax)
            off += w[n].size
        p.update(_prep_small(small_full))

    def weights_ride(parts):
        def sink(arrived):
            for (own, setter), a in zip(parts, arrived):
                setter(put_own(own, a)[0])
        return _all_gather([own for own, _ in parts]), sink

    ffn_keys = {"ffn_gate_t": "ffn_w_gate", "ffn_up_t": "ffn_w_up", "ffn_down": "ffn_w_down"}
    ffn_part = lambda key, l: (mine[ffn_keys[key]][l:l + 1], functools.partial(p[key].__setitem__, l))
    rides = {
        "ab_norm": (_all_gather(first), first_arrived),
        "attn_fwd": weights_ride([ffn_part("ffn_gate_t", 0), ffn_part("ffn_up_t", 0)]),
        "lru_fwd": weights_ride([ffn_part("ffn_down", 0)]),
        "ffn0_gate": weights_ride([ffn_part("ffn_gate_t", 1)]),
        "ffn0_up": weights_ride([ffn_part("ffn_up_t", 1)]),
        "ffn0_act": weights_ride([ffn_part("ffn_down", 1)]),
        "ffn0_down": weights_ride([(mine["c_w_in"], functools.partial(p.__setitem__, "c_w_in_t")),
                                   (mine["c_w_out"], functools.partial(p.__setitem__, "c_w_out"))]),
    }

    def chip_sums(pair, arrived, tag):
        own = [lax.dynamic_index_in_dim(a, chip, axis=0, keepdims=False) for a in pair]
        return [_sum_slots(_put(a, o, chip, 0), name=f"grad_chip_sum_{tag}{i}") for i, (a, o) in enumerate(zip(arrived, own))]

    half_of = {}

    def grads_ready(layer, ready):
        if layer == 1:
            named = {"gate1": ready["ffn_gate_t"], "up1": ready["ffn_up_t"], "down1": ready["ffn_down"]}
            hosts = {"sgu_bwd": ["down1"], "ffn0_dact": ["up1"], "ffn0_dactbwd": ["gate1"]}
        else:
            named = {"c_in": ready["c_w_in_t"], "c_out": ready["c_w_out"], "gate0": ready["ffn_gate_t"],
                     "up0": ready["ffn_up_t"], "down0": ready["ffn_down"]}
            hosts = {"attn_dq": ["c_in", "c_out", "down0"], "attn_dkv": ["gate0", "up0"]}
        tag = f"f{layer}"
        sharded = [a.reshape(N_CHIPS, 2, -1, a.shape[-1]) for a in named.values()]

        def paired(from_sib):
            own = [lax.dynamic_index_in_dim(a, c, axis=1, keepdims=False) for a in sharded]
            pair = {k: _add2(a, b, out_dtype=BF16, name=f"grad_pair_add_{tag}{i}")
                    for i, (k, a, b) in enumerate(zip(named, own, from_sib))}
            for kernel_name, keys in hosts.items():
                def sink(arrived, keys=keys, kernel_name=kernel_name):
                    half_of.update(zip(keys, chip_sums([pair[k] for k in keys], arrived, f"{tag}_{kernel_name}")))
                rides[kernel_name] = (_chip_exchange([pair[k] for k in keys], scatter=True), sink)

        rides[f"ffn{layer}_dnorm"] = (_pair_swap(sharded), paired)

    loss_row, grad_x, g = _local_step(x, positions, loss_target, p, rides, grads_ready)

    cols = lambda a, n: jnp.swapaxes(a.reshape(a.shape[0], N_CHIPS, n), 0, 1)
    n_in, n_q, n_kv = w["ab_w_in"].shape[2], w["ab_w_q_b"].shape[2], w["ab_w_kv_b"].shape[2]
    small_names = SMALL_REPLICATED + list(SMALL_SHARDED)
    rs = _rows(sum(g[n].size for n in small_names) + LANES, FLAT_ROWS)
    small = _flat([loss_row] + [g[n] for n in small_names], rs)
    slot = (jnp.arange(2) == c)[:, None, None]
    last = [cols(_unperm_w_in(g["w_in_p"]), n_in), cols(_from_head_blocks(g["w_q_p"], QK_NOPE + QK_ROPE), n_q),
            cols(_join_kv(g["w_k_p"], g["w_v_p"]), n_kv), g["ab_w_out"]]
    last = [a.reshape(N_CHIPS, 2, -1, a.shape[-1]) for a in last]
    *from_sib, small_sib = _merge(_pair_swap(last), _pair_send([small])).run("tail_pair")
    own = [lax.dynamic_index_in_dim(a, c, axis=1, keepdims=False) for a in last]
    pair = [_add2(a, b, out_dtype=BF16, name=f"grad_pair_add_b{i}") for i, (a, b) in enumerate(zip(own, from_sib))]
    pair_small = _sum_slots(jnp.where(slot, small[None], small_sib[None]), name="small_pair_sum")
    *arrived, all_small = _merge(_chip_exchange(pair, scatter=True), _chip_exchange([pair_small], scatter=False)).run("tail_chip")
    half_of.update(zip(["in", "q", "kv", "out"], chip_sums(pair, arrived, "b")))
    small_sum = _sum_slots(_put(all_small, pair_small, chip, 0), name="small_chip_sum")
    keys = ("in", "q", "kv", "out", "c_in", "c_out", "gate0", "gate1", "up0", "up1", "down0", "down1")
    other_half = dict(zip(keys, _pair_send([half_of[k] for k in keys]).run("grad_pair_share")))
    whole = lambda k: jnp.where(slot, half_of[k][None], other_half[k][None]).reshape(-1, half_of[k].shape[-1])
    grads_t = {"ab_w_in": whole("in").T[None], "ab_w_q_b": whole("q").T[None]}
    grads = {"ab_w_kv_b": whole("kv")[None], "c_w_in": whole("c_in").T[None], **{n: tr(a) for n, a in grads_t.items()}}
    by_halves = {"ab_w_out": (("out",), False), "c_w_out": (("c_out",), False), "ffn_w_down": (("down0", "down1"), False),
                 "ffn_w_gate": (("gate0", "gate1"), True), "ffn_w_up": (("up0", "up1"), True)}

    small_parts = _unflat(small_sum, [(1, LANES)] + [g[n].shape for n in small_names])
    loss = small_parts[0][0, 0]
    for n, a in zip(small_names, small_parts[1:]):
        if n in SMALL_SHARDED:
            ax = SMALL_SHARDED[n]
            a = lax.dynamic_slice_in_dim(a, chip * w[n].shape[ax], w[n].shape[ax], axis=ax)
        grads[n] = a.reshape(w[n].shape)

    delta, new_m, new_v = {}, {}, {}
    for n in BIG:
        if n in by_halves:
            ks, transposed = by_halves[n]
            view = tr if transposed else (lambda a: a)
            out = _adamw_halves(view(w[n]), view(m[n]), view(v[n]), [half_of[k] for k in ks], [other_half[k] for k in ks],
                                name=f"adamw_{n}")
            delta[n], new_m[n], new_v[n], grads[n] = (view(a) for a in out)
        elif n in grads_t:
            out = _adamw(tr(w[n]), grads_t[n], tr(m[n]), tr(v[n]), name=f"adamw_{n}")
            delta[n], new_m[n], new_v[n] = (tr(a) for a in out)
        else:
            delta[n], new_m[n], new_v[n] = _adamw(w[n], grads[n], m[n], v[n], name=f"adamw_{n}")
    small_all = [n for n in WEIGHT_NAMES if n not in BIG]
    ra = _rows(sum(w[n].size for n in small_all), FLAT_ROWS)
    pack = lambda d: _flat([d[n] for n in small_all], ra)[None]
    out = _adamw(pack(w), pack(grads), pack(m), pack(v), name="adamw_small")
    shapes = [w[n].shape for n in small_all]
    for d, flat in zip((delta, new_m, new_v), out):
        d.update(zip(small_all, _unflat(flat, shapes)))
    return (loss, grad_x, *[grads[n] for n in WEIGHT_NAMES], *[delta[n] for n in WEIGHT_NAMES],
            *[new_m[n] for n in WEIGHT_NAMES], *[new_v[n] for n in WEIGHT_NAMES])
```

```python
import functools
import math

import jax
import jax.numpy as jnp
from jax import lax
from jax.experimental import pallas as pl
from jax.experimental.pallas import tpu as pltpu

F32 = jnp.float32
BF16 = jnp.bfloat16
MESH = pl.DeviceIdType.MESH

D_MODEL = 1024
MLA_HEADS = 8
Q_LORA = 256
KV_LORA = 128
QK_NOPE = 64
QK_ROPE = 32
V_HEAD = 64
LRU_WIDTH = 512
LRU_HEADS = 8
LRU_BLOCK = 64
LRU_CONV = 4
LRU_C = 8.0
CHUNK = 128
SGU_GROUPS = 8
SGU_WIDTH = 1024
D_FF = 2816
FFN_CONV = 3
NORM_EPS = 1e-6
ROPE_BASE = 10000.0
AB_IN_PAD = 1536
ADAM_LR = 0.001
ADAM_B1 = 0.9
ADAM_B2 = 0.999
ADAM_EPS = 1e-08
ADAM_WD = 0.01
ADAM_STEP = 10

N_CHIPS = 4
LANES = 128
VMEM_LIMIT = 56 * 1024 * 1024
ROW_TILE = 256
NORM_TILE = 1024
MM_TM, MM_TN, MM_TK = 1024, 1536, 2816
MM_TM_T, MM_TK_T = 1408, 1024
GELU_C = math.sqrt(2.0 / math.pi)


def _cparams(sem):
    return pltpu.CompilerParams(dimension_semantics=sem, vmem_limit_bytes=VMEM_LIMIT)


def _tile(n, target, mult=LANES):
    t = (min(n, target) // mult) * mult
    while t >= mult:
        if n % t == 0:
            return t
        t -= mult
    return n


GELU_K = GELU_C * 0.044715


def _gelu(x):
    t = jnp.tanh(x * (GELU_C + GELU_K * (x * x)))
    hx = 0.5 * x
    return hx + hx * t


def _gelu_and_grad(x):
    x2 = x * x
    t = jnp.tanh(x * (GELU_C + GELU_K * x2))
    hx = 0.5 * x
    dg = (0.5 + 0.5 * t) + (hx * (1.0 - t * t)) * (GELU_C + (3.0 * GELU_K) * x2)
    return hx + hx * t, dg


def _sigmoid(x):
    return 1.0 / (1.0 + jnp.exp(-x))


def _shift_rows(x, d, fill_rows):
    ext = jnp.concatenate([fill_rows, x], axis=0)
    return pltpu.roll(ext, d, 0)[8:]


def _shift_rows_up(x, d, fill_rows):
    n = x.shape[0]
    ext = jnp.concatenate([x, fill_rows], axis=0)
    return pltpu.roll(ext, n + 8 - d, 0)[:n]


def _dot(a, b, dims):
    return lax.dot_general(a.astype(BF16), b.astype(BF16), (dims, ((), ())), preferred_element_type=F32)


def _dot_nn(a, b):
    return _dot(a, b, ((1,), (0,)))


def _dot_nt(a, b):
    return _dot(a, b, ((1,), (1,)))


def _dot_tn(a, b):
    return _dot(a, b, ((0,), (0,)))


def _mm(a, b, *, name, ta=False, tb=False, res=None, out_dtype=F32, ride=None):
    if ta:
        K, M = a.shape
    else:
        M, K = a.shape
    N = b.shape[0] if tb else b.shape[1]
    tm = _tile(M, MM_TM_T if ta else MM_TM, LANES if ta else 8)
    tn = _tile(N, MM_TN, LANES)
    tk = _tile(K, MM_TK_T if ta else MM_TK, LANES)
    nk = K // tk
    a_spec = pl.BlockSpec((tk, tm), lambda j, i, k: (k, i)) if ta else pl.BlockSpec((tm, tk), lambda j, i, k: (i, k))
    b_spec = pl.BlockSpec((tn, tk), lambda j, i, k: (j, k)) if tb else pl.BlockSpec((tk, tn), lambda j, i, k: (k, j))
    o_spec = pl.BlockSpec((tm, tn), lambda j, i, k: (i, j))
    dims = ((0,) if ta else (1,), (1,) if tb else (0,))
    has_res = res is not None

    def body(*refs):
        a_ref, b_ref = refs[:2]
        r_ref = refs[2] if has_res else None
        o_ref = refs[3] if has_res else refs[2]
        p = _dot(a_ref[...], b_ref[...], dims)

        def finish(r):
            if has_res:
                r = r + r_ref[...].astype(F32)
            o_ref[...] = r.astype(out_dtype)

        if nk == 1:
            finish(p)
            return
        acc_ref = refs[-1]
        k = pl.program_id(2)

        @pl.when(k == 0)
        def _():
            acc_ref[...] = p

        @pl.when(jnp.logical_and(k > 0, k < nk - 1))
        def _():
            acc_ref[...] += p

        @pl.when(k == nk - 1)
        def _():
            finish(acc_ref[...] + p)

    in_specs = [a_spec, b_spec] + ([o_spec] if has_res else [])
    args = (a, b) + ((res,) if has_res else ())
    return _pcall(
        body, name=name, grid=(N // tn, M // tm, nk), in_specs=in_specs, out_specs=[o_spec],
        out_shape=[jax.ShapeDtypeStruct((M, N), out_dtype)], args=args,
        scratch=[pltpu.VMEM((tm, tn), F32)] if nk > 1 else [], sem=("parallel", "parallel", "arbitrary"), ride=ride)[0]


def _rms_fwd(x, g, *, name, cb=0, out_dtype=BF16, ride=None):
    T = x.shape[0]
    W = g.shape[-1]
    g = g.reshape(1, W)
    tt = _tile(T, NORM_TILE, 16)

    def body(x_ref, g_ref, o_ref):
        xf = x_ref[...].astype(F32)
        rstd = lax.rsqrt(jnp.mean(xf * xf, axis=-1, keepdims=True) + NORM_EPS)
        o_ref[...] = (xf * rstd * g_ref[...]).astype(out_dtype)

    return _pcall(
        body, name=name, grid=(T // tt,),
        in_specs=[pl.BlockSpec((tt, W), lambda i: (i, cb)), pl.BlockSpec((1, W), lambda i: (0, 0))],
        out_specs=[pl.BlockSpec((tt, W), lambda i: (i, 0))], out_shape=[jax.ShapeDtypeStruct((T, W), out_dtype)],
        args=(x, g), sem=("parallel",), ride=ride)[0]


def _rms_bwd(x, g, dy, *, name, cb=0, res=None, out_dtype=F32, ride=None):
    T = x.shape[0]
    W = g.shape[-1]
    g = g.reshape(1, W)
    tt = _tile(T, NORM_TILE, 16)
    has_res = res is not None

    def body(*refs):
        if has_res:
            x_ref, g_ref, dy_ref, r_ref, dx_ref, dg_ref = refs
        else:
            x_ref, g_ref, dy_ref, dx_ref, dg_ref = refs
        xf = x_ref[...].astype(F32)
        dyf = dy_ref[...].astype(F32)
        rstd = lax.rsqrt(jnp.mean(xf * xf, axis=-1, keepdims=True) + NORM_EPS)
        xhat = xf * rstd
        dxhat = dyf * g_ref[...]
        dx = rstd * (dxhat - xhat * jnp.mean(dxhat * xhat, axis=-1, keepdims=True))
        if has_res:
            dx = dx + r_ref[...].astype(F32)
        dx_ref[...] = dx.astype(out_dtype)
        part = jnp.sum(dyf * xhat, axis=0, keepdims=True)

        @pl.when(pl.program_id(0) == 0)
        def _():
            dg_ref[...] = part

        @pl.when(pl.program_id(0) > 0)
        def _():
            dg_ref[...] += part

    row = pl.BlockSpec((tt, W), lambda i: (i, 0))
    in_specs = [pl.BlockSpec((tt, W), lambda i: (i, cb)), pl.BlockSpec((1, W), lambda i: (0, 0)), row]
    args = (x, g, dy)
    if has_res:
        in_specs.append(row)
        args = args + (res,)
    return _pcall(
        body, name=name, grid=(T // tt,), in_specs=in_specs,
        out_specs=[row, pl.BlockSpec((1, W), lambda i: (0, 0))],
        out_shape=[jax.ShapeDtypeStruct((T, W), out_dtype), jax.ShapeDtypeStruct((1, W), F32)], args=args, ride=ride)


def _final_fwd_bwd(h, g, target, *, name):
    T, W = h.shape
    g = g.reshape(1, W)
    tt = _tile(T, NORM_TILE, 16)

    def body(x_ref, g_ref, t_ref, loss_ref, dx_ref, dg_ref):
        xf = x_ref[...]
        rstd = lax.rsqrt(jnp.mean(xf * xf, axis=-1, keepdims=True) + NORM_EPS)
        xhat = xf * rstd
        err = xhat * g_ref[...] - t_ref[...]
        lpart = jnp.zeros((1, LANES), F32) + (0.5 / W) * jnp.sum(err * err)
        dyf = err * (1.0 / W)
        dxhat = dyf * g_ref[...]
        dx_ref[...] = rstd * (dxhat - xhat * jnp.mean(dxhat * xhat, axis=-1, keepdims=True))
        part = jnp.sum(dyf * xhat, axis=0, keepdims=True)

        @pl.when(pl.program_id(0) == 0)
        def _():
            dg_ref[...] = part
            loss_ref[...] = lpart

        @pl.when(pl.program_id(0) > 0)
        def _():
            dg_ref[...] += part
            loss_ref[...] += lpart

    row = pl.BlockSpec((tt, W), lambda i: (i, 0))
    return pl.pallas_call(
        body, name=name, grid=(T // tt,),
        in_specs=[row, pl.BlockSpec((1, W), lambda i: (0, 0)), row],
        out_specs=[pl.BlockSpec((1, LANES), lambda i: (0, 0)), row, pl.BlockSpec((1, W), lambda i: (0, 0))],
        out_shape=[jax.ShapeDtypeStruct((1, LANES), F32), jax.ShapeDtypeStruct((T, W), F32),
                   jax.ShapeDtypeStruct((1, W), F32)],
        compiler_params=_cparams(("arbitrary",)),
    )(h, g, target)


def _swap16(x):
    lane = lax.broadcasted_iota(jnp.int32, x.shape, 1)
    return jnp.where((lane % 32) < 16, pltpu.roll(x, LANES - 16, 1), pltpu.roll(x, 16, 1))


def _rope(x, c, s):
    return x * c + _swap16(x) * s


def _rope_t(d, c, s):
    return d * c + _swap16(d * s)


def _head_block_map(fn, x, cos, sin, *, name):
    T, W = x.shape
    tt = _tile(T, NORM_TILE, 16)

    def body(x_ref, c_ref, s_ref, o_ref):
        c, s = c_ref[...], s_ref[...]
        for h in range(W // LANES):
            lanes = slice(h * LANES, (h + 1) * LANES)
            o_ref[:, lanes] = fn(x_ref[:, lanes], c, s).astype(BF16)

    tab = pl.BlockSpec((tt, LANES), lambda i: (i, 0))
    blk = pl.BlockSpec((tt, W), lambda i: (i, 0))
    return pl.pallas_call(
        body, name=name, grid=(T // tt,), in_specs=[blk, tab, tab], out_specs=blk,
        out_shape=jax.ShapeDtypeStruct((T, W), BF16), compiler_params=_cparams(("parallel",)),
    )(x, cos, sin)


def _rope_q(q, cos, sin, *, name):
    scale = _attn_scale()
    return _head_block_map(lambda x, c, s: _rope(x, c, s) * scale, q, cos, sin, name=name)


def _rope_q_bwd(dq, cos, sin, *, name):
    return _head_block_map(_rope_t, dq, cos, sin, name=name)


def _key_blocks(kv, z, cos, sin, *, kpe_block, name):
    T = kv.shape[0]
    tt = _tile(T, NORM_TILE, 16)
    W = MLA_HEADS * LANES

    def body(kv_ref, z_ref, c_ref, s_ref, o_ref):
        kr = _rope(z_ref[...], c_ref[...], s_ref[...])
        for h in range(MLA_HEADS):
            lanes = slice(h * LANES, (h + 1) * LANES)
            o_ref[:, lanes] = (kv_ref[:, lanes].astype(F32) + kr).astype(BF16)

    tab = pl.BlockSpec((tt, LANES), lambda i: (i, 0))
    blk = pl.BlockSpec((tt, W), lambda i: (i, 0))
    return pl.pallas_call(
        body, name=name, grid=(T // tt,),
        in_specs=[blk, pl.BlockSpec((tt, LANES), lambda i: (i, kpe_block)), tab, tab], out_specs=blk,
        out_shape=jax.ShapeDtypeStruct((T, W), BF16), compiler_params=_cparams(("parallel",)),
    )(kv, z, cos, sin)


def _key_rope_bwd(dk, cos, sin, *, name):
    T = dk.shape[0]
    tt = _tile(T, NORM_TILE, 16)

    def body(d_ref, c_ref, s_ref, o_ref):
        d = d_ref[:, :LANES]
        for h in range(1, MLA_HEADS):
            d = d + d_ref[:, h * LANES:(h + 1) * LANES]
        lane = lax.broadcasted_iota(jnp.int32, d.shape, 1)
        d = jnp.where(jnp.logical_and(lane >= QK_NOPE, lane < QK_NOPE + QK_ROPE), d, 0.0)
        o_ref[...] = _rope_t(d, c_ref[...], s_ref[...]).astype(BF16)

    tab = pl.BlockSpec((tt, LANES), lambda i: (i, 0))
    return pl.pallas_call(
        body, name=name, grid=(T // tt,),
        in_specs=[pl.BlockSpec((tt, MLA_HEADS * LANES), lambda i: (i, 0)), tab, tab], out_specs=tab,
        out_shape=jax.ShapeDtypeStruct((T, LANES), BF16), compiler_params=_cparams(("parallel",)),
    )(dk, cos, sin)


ATT_BLOCK = 512


def _attn_scale():
    return float((QK_NOPE + QK_ROPE) ** -0.5)


def _causal_mask(qi, kj, tq, tk):
    row = qi * tq + lax.broadcasted_iota(jnp.int32, (tq, tk), 0)
    col = kj * tk + lax.broadcasted_iota(jnp.int32, (tq, tk), 1)
    return col <= row


def _pcall(body, *, name, grid, in_specs, out_specs, out_shape, args, scratch=(), sem=None, ride=None):
    n_in, n_out, n_scr = len(args), len(out_shape), len(scratch)
    if ride is None:
        return pl.pallas_call(
            body, name=name, grid=grid, in_specs=list(in_specs), out_specs=list(out_specs), out_shape=list(out_shape),
            scratch_shapes=list(scratch), compiler_params=_cparams(sem or ("arbitrary",) * len(grid)))(*args)
    ex, sink = ride
    o0 = n_in + len(ex.arrs)
    s0 = o0 + n_out + len(ex.out_shapes)

    def hosted(*refs):
        parts = (refs[n_in:o0], refs[o0 + n_out:s0], refs[-2], refs[-1])
        ids = [pl.program_id(i) for i in range(len(grid))]
        pl.when(functools.reduce(jnp.logical_and, [i == 0 for i in ids]))(lambda: ex.start(*parts))
        body(*refs[:n_in], *refs[o0:o0 + n_out], *refs[s0:s0 + n_scr])
        pl.when(functools.reduce(jnp.logical_and, [i == n - 1 for i, n in zip(ids, grid)]))(lambda: ex.finish(*parts))

    outs = pl.pallas_call(
        hosted, name=name, grid=grid, in_specs=list(in_specs) + ex.in_specs, out_specs=list(out_specs) + ex.out_specs,
        out_shape=list(out_shape) + ex.out_shapes, scratch_shapes=list(scratch) + ex.scratch,
        compiler_params=_cparams(("arbitrary",) * len(grid)))(*args, *ex.arrs)
    sink(outs[n_out:])
    return outs[:n_out]


PAIRS = MLA_HEADS // 2


def _own_lanes(x, first):
    lane = lax.broadcasted_iota(jnp.int32, x.shape, 1)
    return jnp.where((lane < V_HEAD) if first else (lane >= V_HEAD), x, 0.0)


def _attn_fwd(q, k, kv, *, B, S, v_block0, name, ride=None):
    tq = tk = min(ATT_BLOCK, S)
    nq = S // tq
    T = B * S

    def body(q_ref, k_ref, v_ref, o_ref, lse_ref):
        qi = pl.program_id(2)
        qs = (q_ref[:, :LANES], q_ref[:, LANES:])

        def step(masked):
            def f(j, carry):
                rows = pl.ds(pl.multiple_of(j * tk, tk), tk)
                vb = v_ref[rows, :]
                out = []
                for h in range(2):
                    m, l, acc = carry[h]
                    s = _dot_nt(qs[h], k_ref[rows, h * LANES:(h + 1) * LANES])
                    if masked:
                        s = jnp.where(_causal_mask(qi, j, tq, tk), s, -jnp.inf)
                    m_new = jnp.maximum(m, jnp.max(s, axis=-1, keepdims=True))
                    alpha = jnp.exp(m - m_new)
                    p = jnp.exp(s - m_new)
                    out.append((m_new, alpha * l + jnp.sum(p, axis=-1, keepdims=True), alpha * acc + _dot_nn(p, vb)))
                return tuple(out)
            return f

        one = (jnp.full((tq, 1), -1e30, F32), jnp.zeros((tq, 1), F32), jnp.zeros((tq, LANES), F32))
        (ma, la, acca), (mb, lb, accb) = step(True)(qi, lax.fori_loop(0, qi, step(False), (one, one)))
        o_ref[...] = _own_lanes(acca / la, True) + _own_lanes(accb / lb, False)
        lse_ref[0, 0] = ma + jnp.log(la)
        lse_ref[0, 1] = mb + jnp.log(lb)

    return _pcall(
        body, name=name, grid=(B, PAIRS, nq),
        in_specs=[pl.BlockSpec((tq, 2 * LANES), lambda b, g, i: (b * nq + i, g)),
                  pl.BlockSpec((S, 2 * LANES), lambda b, g, i: (b, g)),
                  pl.BlockSpec((S, LANES), lambda b, g, i: (b, v_block0 + g))],
        out_specs=[pl.BlockSpec((tq, LANES), lambda b, g, i: (b * nq + i, g)),
                   pl.BlockSpec((1, 2, tq, 1), lambda b, g, i: (b, g, i, 0))],
        out_shape=[jax.ShapeDtypeStruct((T, PAIRS * LANES), F32), jax.ShapeDtypeStruct((B, MLA_HEADS, S, 1), F32)],
        args=(q, k, kv), ride=ride)


def _attn_dq(q, k, kv, o, lse, do, *, B, S, v_block0, name, ride=None):
    tq = tk = min(ATT_BLOCK, S)
    nq = S // tq
    T = B * S
    scale = _attn_scale()

    def body(q_ref, k_ref, v_ref, o_ref, lse_ref, do_ref, dq_ref, delta_ref):
        qi = pl.program_id(2)
        qs = (q_ref[:, :LANES], q_ref[:, LANES:])
        dos = (_own_lanes(do_ref[...], True), _own_lanes(do_ref[...], False))
        deltas = tuple(jnp.sum(d * o_ref[...], axis=-1, keepdims=True) for d in dos)
        lses = (lse_ref[0, 0], lse_ref[0, 1])

        def step(masked):
            def f(j, carry):
                rows = pl.ds(pl.multiple_of(j * tk, tk), tk)
                vb = v_ref[rows, :]
                out = []
                for h in range(2):
                    kb = k_ref[rows, h * LANES:(h + 1) * LANES]
                    p = jnp.exp(_dot_nt(qs[h], kb) - lses[h])
                    if masked:
                        p = jnp.where(_causal_mask(qi, j, tq, tk), p, 0.0)
                    ds = p * (_dot_nt(dos[h], vb) - deltas[h])
                    out.append(carry[h] + _dot_nn(ds, kb))
                return tuple(out)
            return f

        zero = jnp.zeros((tq, LANES), F32)
        dqa, dqb = step(True)(qi, lax.fori_loop(0, qi, step(False), (zero, zero)))
        dq_ref[:, :LANES] = dqa * scale
        dq_ref[:, LANES:] = dqb * scale
        delta_ref[0, 0] = deltas[0]
        delta_ref[0, 1] = deltas[1]

    qrow = lambda w: pl.BlockSpec((tq, w), lambda b, g, i: (b * nq + i, g))
    stat = pl.BlockSpec((1, 2, tq, 1), lambda b, g, i: (b, g, i, 0))
    return _pcall(
        body, name=name, grid=(B, PAIRS, nq),
        in_specs=[qrow(2 * LANES), pl.BlockSpec((S, 2 * LANES), lambda b, g, i: (b, g)),
                  pl.BlockSpec((S, LANES), lambda b, g, i: (b, v_block0 + g)), qrow(LANES), stat, qrow(LANES)],
        out_specs=[qrow(2 * LANES), stat],
        out_shape=[jax.ShapeDtypeStruct((T, MLA_HEADS * LANES), F32), jax.ShapeDtypeStruct((B, MLA_HEADS, S, 1), F32)],
        args=(q, k, kv, o, lse, do), sem=("parallel", "parallel", "parallel"), ride=ride)


def _attn_dkv(q, k, kv, lse_rows, delta_rows, do, *, B, S, v_block0, name, ride=None):
    tq = tk = min(ATT_BLOCK, S)
    nq = S // tq
    T = B * S

    def body(q_ref, k_ref, v_ref, lse_ref, delta_ref, do_ref, dk_ref, dv_ref):
        kj = pl.program_id(2)
        ks = (k_ref[:, :LANES], k_ref[:, LANES:])
        vb = v_ref[...]

        def step(masked):
            def f(i, carry):
                rows = pl.ds(pl.multiple_of(i * tq, tq), tq)
                do_b = do_ref[rows, :]
                dks, dv = list(carry[:2]), carry[2]
                for h in range(2):
                    qb = q_ref[rows, h * LANES:(h + 1) * LANES]
                    doh = _own_lanes(do_b, h == 0)
                    pt = jnp.exp(_dot_nt(ks[h], qb) - lse_ref[0, h, pl.ds(i, 1), :])
                    if masked:
                        krow = kj * tk + lax.broadcasted_iota(jnp.int32, (tk, tq), 0)
                        qcol = i * tq + lax.broadcasted_iota(jnp.int32, (tk, tq), 1)
                        pt = jnp.where(krow <= qcol, pt, 0.0)
                    dst = pt * (_dot_nt(vb, doh) - delta_ref[0, h, pl.ds(i, 1), :])
                    dks[h] = dks[h] + _dot_nn(dst, qb)
                    dv = dv + _dot_nn(pt, doh)
                return dks[0], dks[1], dv
            return f

        zero = jnp.zeros((tk, LANES), F32)
        dka, dkb, dv = lax.fori_loop(kj + 1, nq, step(False), step(True)(kj, (zero, zero, zero)))
        dk_ref[:, :LANES] = dka
        dk_ref[:, LANES:] = dkb
        dv_ref[...] = dv

    krow = lambda w, c0: pl.BlockSpec((tk, w), lambda b, g, j: (b * nq + j, c0 + g))
    seq = lambda w: pl.BlockSpec((S, w), lambda b, g, j: (b, g))
    stat = pl.BlockSpec((1, 2, nq, tq), lambda b, g, j: (b, g, 0, 0))
    return _pcall(
        body, name=name, grid=(B, PAIRS, nq),
        in_specs=[seq(2 * LANES), krow(2 * LANES, 0), krow(LANES, v_block0), stat, stat, seq(LANES)],
        out_specs=[krow(2 * LANES, 0), krow(LANES, 0)],
        out_shape=[jax.ShapeDtypeStruct((T, MLA_HEADS * LANES), F32), jax.ShapeDtypeStruct((T, PAIRS * LANES), F32)],
        args=(q, k, kv, lse_rows, delta_rows, do), ride=ride)


def _lru_gates(xl, halo, cw_ref, cb_ref, wa_ref, ba_ref, wx_ref, bx_ref, lam_ref):
    xc = cb_ref[...] + cw_ref[3:4, :] * xl
    for kk in range(LRU_CONV - 1):
        xc = xc + cw_ref[kk:kk + 1, :] * _shift_rows(xl, LRU_CONV - 1 - kk, halo)
    r = _sigmoid(_dot_nn(xc, wa_ref[...]) + ba_ref[...])
    i = _sigmoid(_dot_nn(xc, wx_ref[...]) + bx_ref[...])
    lam = lam_ref[...]
    sp = jnp.maximum(-lam, 0.0) + jnp.log(1.0 + jnp.exp(-jnp.abs(lam)))
    a = jnp.exp(-LRU_C * r * sp)
    mult = jnp.sqrt(1.0 - a * a)
    return xc, r, i, sp, a, mult


def _lru_specs(tt, nt, S):
    def make(rev):
        tmap = (lambda t: nt - 1 - t) if rev else (lambda t: t)
        tile = lambda cb: pl.BlockSpec((tt, LRU_WIDTH), lambda b, t: (b * nt + tmap(t), cb))
        prev8 = lambda cb: pl.BlockSpec(
            (8, LRU_WIDTH), lambda b, t: (jnp.maximum((b * nt + tmap(t)) * (tt // 8) - 1, 0), cb))
        return tile, prev8, tmap
    return make


def _lru_fwd(z, cw, cb, wa, ba, wx, bx, lam, *, S, name, ride=None):
    T = z.shape[0]
    tt = min(ROW_TILE, S)
    nt = S // tt
    tile, prev8, _ = _lru_specs(tt, nt, S)(False)
    vec = lambda r: pl.BlockSpec((r, LRU_WIDTH), lambda b, t: (0, 0))
    mat = pl.BlockSpec((LRU_WIDTH, LRU_WIDTH), lambda b, t: (0, 0))

    def body(xl_ref, halo_ref, gate_ref, cw_ref, cb_ref, wa_ref, ba_ref, wx_ref, bx_ref, lam_ref,
             y_ref, h_ref, carry_ref):
        t = pl.program_id(1)
        first = t == 0
        halo = jnp.where(first, 0.0, halo_ref[...])
        xl_t = xl_ref[...]
        xc, r, i, sp, a, mult = _lru_gates(xl_t, halo, cw_ref, cb_ref, wa_ref, ba_ref, wx_ref, bx_ref, lam_ref)
        bv = mult * (i * xc)
        ones = jnp.ones((8, LRU_WIDTH), F32)
        zeros = jnp.zeros((8, LRU_WIDTH), F32)
        row = lax.broadcasted_iota(jnp.int32, (tt, LRU_WIDTH), 0)
        A = a
        d = 1
        while d < tt:
            if d < 8:
                a_sh = _shift_rows(A, d, ones)
                b_sh = _shift_rows(bv, d, zeros)
            else:
                a_sh = jnp.where(row < d, 1.0, pltpu.roll(A, d, 0))
                b_sh = jnp.where(row < d, 0.0, pltpu.roll(bv, d, 0))
            bv = A * b_sh + bv
            A = A * a_sh
            d *= 2
        h0 = jnp.where(first, 0.0, carry_ref[0:1, :])
        h = A * h0 + bv
        carry_ref[...] = jnp.broadcast_to(h[tt - 1:tt, :], (8, LRU_WIDTH))
        h_ref[...] = h
        y_ref[...] = (h * _gelu(gate_ref[...])).astype(BF16)

    return _pcall(
        body, name=name, grid=(T // S, nt),
        in_specs=[tile(0), prev8(0), tile(1), vec(LRU_CONV), vec(1), mat, vec(1), mat, vec(1), vec(1)],
        out_specs=[tile(0), tile(0)],
        out_shape=[jax.ShapeDtypeStruct((T, LRU_WIDTH), BF16), jax.ShapeDtypeStruct((T, LRU_WIDTH), F32)],
        args=(z, z, z, cw, cb, wa, ba, wx, bx, lam), scratch=[pltpu.VMEM((8, LRU_WIDTH), F32)], ride=ride)


def _lru_bwd(z, h, dy, cw, cb, wa, ba, wx, bx, lam, *, S, name):
    T = z.shape[0]
    tt = min(ROW_TILE, S)
    nt = S // tt
    tile, prev8, tmap = _lru_specs(tt, nt, S)(True)
    vec = lambda r: pl.BlockSpec((r, LRU_WIDTH), lambda b, t: (0, 0))
    mat = pl.BlockSpec((LRU_WIDTH, LRU_WIDTH), lambda b, t: (0, 0))

    def body(xl_ref, halo_ref, gate_ref, h_ref, hprev_ref, dy_ref, cw_ref, cb_ref, wa_ref, ba_ref, wx_ref,
             bx_ref, lam_ref, dxl_ref, dgate_ref, dcw_ref, dcb_ref, dwa_ref, dba_ref, dwx_ref, dbx_ref,
             dlam_ref, lamc_ref, ac_ref, dxc_ref):
        b = pl.program_id(0)
        t = pl.program_id(1)
        tr = nt - 1 - t
        seq_first = tr == 0
        seq_last = t == 0
        halo = jnp.where(seq_first, 0.0, halo_ref[...])
        xl_t = xl_ref[...]
        xc, r, i, sp, a, mult = _lru_gates(xl_t, halo, cw_ref, cb_ref, wa_ref, ba_ref, wx_ref, bx_ref, lam_ref)
        hh = h_ref[...]
        dyf = dy_ref[...].astype(F32)
        gl, dgl = _gelu_and_grad(gate_ref[...])
        dgate_ref[...] = (dyf * hh * dgl).astype(BF16)
        dh = dyf * gl

        a_first_later = jnp.where(seq_last, 0.0, ac_ref[...])
        lam_later = jnp.where(seq_last, 0.0, lamc_ref[...])
        row = lax.broadcasted_iota(jnp.int32, (tt, LRU_WIDTH), 0)
        A = _shift_rows_up(a, 1, a_first_later)
        lm = dh
        ones = jnp.ones((8, LRU_WIDTH), F32)
        zeros = jnp.zeros((8, LRU_WIDTH), F32)
        d = 1
        while d < tt:
            if d < 8:
                a_sh = _shift_rows_up(A, d, ones)
                l_sh = _shift_rows_up(lm, d, zeros)
            else:
                a_sh = jnp.where(row >= tt - d, 1.0, pltpu.roll(A, tt - d, 0))
                l_sh = jnp.where(row >= tt - d, 0.0, pltpu.roll(lm, tt - d, 0))
            lm = lm + A * l_sh
            A = A * a_sh
            d *= 2
        lm = lm + A * lam_later[0:1, :]
        lamc_ref[...] = jnp.broadcast_to(lm[0:1, :], (8, LRU_WIDTH))
        ac_ref[...] = jnp.broadcast_to(a[0:1, :], (8, LRU_WIDTH))

        hprev_halo = jnp.where(seq_first, 0.0, hprev_ref[...])
        h_prev = _shift_rows(hh, 1, hprev_halo)
        da = lm * h_prev
        ixc = i * xc
        dmult = lm * ixc
        di = lm * mult * xc
        dxc = lm * mult * i
        da = da - dmult * a / mult
        dlog = da * a
        dr = dlog * (-LRU_C) * sp
        dsp_part = jnp.sum(dlog * (-LRU_C) * r, axis=0, keepdims=True)
        dpa = dr * r * (1.0 - r)
        dpx = di * i * (1.0 - i)
        dxc = dxc + _dot_nt(dpa, wa_ref[...]) + _dot_nt(dpx, wx_ref[...])
        dwa_part = _dot_tn(xc, dpa)
        dwx_part = _dot_tn(xc, dpx)

        later = jnp.where(seq_last, 0.0, dxc_ref[...])
        dxl = cw_ref[3:4, :] * dxc
        for kk in range(LRU_CONV - 1):
            dxl = dxl + cw_ref[kk:kk + 1, :] * _shift_rows_up(dxc, LRU_CONV - 1 - kk, later)
        dxl_ref[...] = dxl.astype(BF16)
        dxc_ref[...] = dxc[0:8, :]
        dcw_rows = [jnp.sum(dxc * _shift_rows(xl_t, LRU_CONV - 1 - kk, halo), axis=0, keepdims=True)
                    for kk in range(LRU_CONV - 1)]
        dcw_rows.append(jnp.sum(dxc * xl_t, axis=0, keepdims=True))
        dcw_part = jnp.concatenate(dcw_rows + [jnp.zeros((8 - LRU_CONV, LRU_WIDTH), F32)], axis=0)
        lamv = lam_ref[...]
        dlam_part = dsp_part * (-_sigmoid(-lamv))
        parts = ((dcw_ref, dcw_part), (dcb_ref, jnp.sum(dxc, axis=0, keepdims=True)),
                 (dwa_ref, dwa_part), (dba_ref, jnp.sum(dpa, axis=0, keepdims=True)),
                 (dwx_ref, dwx_part), (dbx_ref, jnp.sum(dpx, axis=0, keepdims=True)),
                 (dlam_ref, dlam_part))
        start = jnp.logical_and(b == 0, t == 0)

        @pl.when(start)
        def _():
            for ref, val in parts:
                ref[...] = val

        @pl.when(jnp.logical_not(start))
        def _():
            for ref, val in parts:
                ref[...] += val

    acc = lambda r: pl.BlockSpec((r, LRU_WIDTH), lambda b, t: (0, 0))
    return pl.pallas_call(
        body, name=name, grid=(T // S, nt),
        in_specs=[tile(0), prev8(0), tile(1), tile(0), prev8(0), tile(0),
                  vec(LRU_CONV), vec(1), mat, vec(1), mat, vec(1), vec(1)],
        out_specs=[tile(0), tile(0), acc(8), acc(1), mat, acc(1), mat, acc(1), acc(1)],
        out_shape=[jax.ShapeDtypeStruct((T, LRU_WIDTH), BF16), jax.ShapeDtypeStruct((T, LRU_WIDTH), BF16),
                   jax.ShapeDtypeStruct((8, LRU_WIDTH), F32), jax.ShapeDtypeStruct((1, LRU_WIDTH), F32),
                   jax.ShapeDtypeStruct((LRU_WIDTH, LRU_WIDTH), F32), jax.ShapeDtypeStruct((1, LRU_WIDTH), F32),
                   jax.ShapeDtypeStruct((LRU_WIDTH, LRU_WIDTH), F32), jax.ShapeDtypeStruct((1, LRU_WIDTH), F32),
                   jax.ShapeDtypeStruct((1, LRU_WIDTH), F32)],
        scratch_shapes=[pltpu.VMEM((8, LRU_WIDTH), F32), pltpu.VMEM((8, LRU_WIDTH), F32),
                        pltpu.VMEM((8, LRU_WIDTH), F32)],
        compiler_params=_cparams(("arbitrary", "arbitrary")),
    )(z, z, z, h, h, dy, cw, cb, wa, ba, wx, bx, lam)


FFN_CT = 1408
FFN_TILE = 512


def _ffn_conv(g, halo, cw_ref, cb_ref):
    gc = cb_ref[...] + cw_ref[2:3, :] * g
    for kk in range(FFN_CONV - 1):
        gc = gc + cw_ref[kk:kk + 1, :] * _shift_rows(g, FFN_CONV - 1 - kk, halo)
    return gc


def _ffn_act_fwd(g, u, cw, cb, *, S, name, ride=None):
    T, F = g.shape
    tt = min(FFN_TILE, S)
    nt = S // tt
    tc = _tile(F, FFN_CT)

    def body(g_ref, halo_ref, u_ref, cw_ref, cb_ref, o_ref):
        first = (pl.program_id(0) % nt) == 0
        halo = jnp.where(first, 0.0, halo_ref[...])
        gc = _ffn_conv(g_ref[...], halo, cw_ref, cb_ref)
        o_ref[...] = (_gelu(gc) * u_ref[...]).astype(BF16)

    tile = pl.BlockSpec((tt, tc), lambda i, j: (i, j))
    prev8 = pl.BlockSpec((8, tc), lambda i, j: (jnp.maximum(i * (tt // 8) - 1, 0), j))
    return _pcall(
        body, name=name, grid=(T // tt, F // tc),
        in_specs=[tile, prev8, tile, pl.BlockSpec((FFN_CONV, tc), lambda i, j: (0, j)),
                  pl.BlockSpec((1, tc), lambda i, j: (0, j))],
        out_specs=[tile], out_shape=[jax.ShapeDtypeStruct((T, F), BF16)], args=(g, g, u, cw, cb),
        sem=("parallel", "parallel"), ride=ride)[0]


def _ffn_act_bwd(g, u, dact, cw, cb, *, S, name, ride=None):
    T, F = g.shape
    tt = min(FFN_TILE, S)
    nt = S // tt
    ntt = T // tt
    tc = _tile(F, FFN_CT)

    def body(g_ref, halo_ref, u_ref, da_ref, cw_ref, cb_ref, dg_ref, du_ref, dcw_ref, dcb_ref, later_ref):
        step = pl.program_id(1)
        ti = (ntt - 1 - step) % nt
        halo = jnp.where(ti == 0, 0.0, halo_ref[...])
        gt = g_ref[...]
        gc = _ffn_conv(gt, halo, cw_ref, cb_ref)
        gl, dgl = _gelu_and_grad(gc)
        da = da_ref[...].astype(F32)
        du_ref[...] = (da * gl).astype(BF16)
        dgc = da * u_ref[...] * dgl
        later = jnp.where(ti == nt - 1, 0.0, later_ref[...])
        dg = cw_ref[2:3, :] * dgc
        for kk in range(FFN_CONV - 1):
            dg = dg + cw_ref[kk:kk + 1, :] * _shift_rows_up(dgc, FFN_CONV - 1 - kk, later)
        dg_ref[...] = dg.astype(BF16)
        later_ref[...] = dgc[0:8, :]
        rows = [jnp.sum(dgc * _shift_rows(gt, FFN_CONV - 1 - kk, halo), axis=0, keepdims=True)
                for kk in range(FFN_CONV - 1)]
        rows.append(jnp.sum(dgc * gt, axis=0, keepdims=True))
        dcw_part = jnp.concatenate(rows + [jnp.zeros((8 - FFN_CONV, tc), F32)], axis=0)
        dcb_part = jnp.sum(dgc, axis=0, keepdims=True)

        @pl.when(step == 0)
        def _():
            dcw_ref[...] = dcw_part
            dcb_ref[...] = dcb_part

        @pl.when(step > 0)
        def _():
            dcw_ref[...] += dcw_part
            dcb_ref[...] += dcb_part

    tile = pl.BlockSpec((tt, tc), lambda j, s: (ntt - 1 - s, j))
    prev8 = pl.BlockSpec((8, tc), lambda j, s: (jnp.maximum((ntt - 1 - s) * (tt // 8) - 1, 0), j))
    return _pcall(
        body, name=name, grid=(F // tc, ntt),
        in_specs=[tile, prev8, tile, tile, pl.BlockSpec((FFN_CONV, tc), lambda j, s: (0, j)),
                  pl.BlockSpec((1, tc), lambda j, s: (0, j))],
        out_specs=[tile, tile, pl.BlockSpec((8, tc), lambda j, s: (0, j)), pl.BlockSpec((1, tc), lambda j, s: (0, j))],
        out_shape=[jax.ShapeDtypeStruct((T, F), BF16), jax.ShapeDtypeStruct((T, F), BF16),
                   jax.ShapeDtypeStruct((8, F), F32), jax.ShapeDtypeStruct((1, F), F32)],
        args=(g, g, u, dact, cw, cb), scratch=[pltpu.VMEM((8, tc), F32)], ride=ride)


def _sgu_norm(zv, g_ref, b_ref):
    v = _gelu(zv)
    mu = jnp.mean(v, axis=-1, keepdims=True)
    xc = v - mu
    rstd = lax.rsqrt(jnp.mean(xc * xc, axis=-1, keepdims=True) + NORM_EPS)
    xhat = xc * rstd
    return xhat, rstd, xhat * g_ref[...] + b_ref[...]


def _sgu_fwd(zc, ln_g, ln_b, wm, bmap, *, name):
    T = zc.shape[0]
    W = SGU_WIDTH
    tt = ROW_TILE
    nch = tt // CHUNK

    def body(z_ref, g_ref, b_ref, wm_ref, bm_ref, p_ref):
        u = _gelu(z_ref[:, :W])
        _, _, vn = _sgu_norm(z_ref[:, W:], g_ref, b_ref)
        vn = vn.astype(BF16)
        for n in range(nch):
            rows = slice(n * CHUNK, (n + 1) * CHUNK)
            for gi in range(SGU_GROUPS):
                cols = slice(gi * LANES, (gi + 1) * LANES)
                s = _dot_nn(wm_ref[gi], vn[rows, cols]) + bm_ref[:, cols]
                p_ref[rows, cols] = (u[rows, cols] * s).astype(BF16)

    const2 = lambda r, c: pl.BlockSpec((r, c), lambda i: (0, 0))
    return pl.pallas_call(
        body, name=name, grid=(T // tt,),
        in_specs=[pl.BlockSpec((tt, 2 * W), lambda i: (i, 0)), const2(1, W), const2(1, W),
                  pl.BlockSpec((SGU_GROUPS, CHUNK, CHUNK), lambda i: (0, 0, 0)), const2(CHUNK, W)],
        out_specs=pl.BlockSpec((tt, W), lambda i: (i, 0)),
        out_shape=jax.ShapeDtypeStruct((T, W), BF16),
        compiler_params=_cparams(("parallel",)),
    )(zc, ln_g, ln_b, wm, bmap)


def _sgu_bwd(zc, dp, ln_g, ln_b, wm, bmap, *, name, ride=None):
    T = zc.shape[0]
    W = SGU_WIDTH
    tt = ROW_TILE
    nch = tt // CHUNK
    nsteps = T // tt

    def body(z_ref, dp_ref, g_ref, b_ref, wm_ref, bm_ref, dz_ref, dg_ref, db_ref, dwm_ref, dbm_ref,
             s_scr, dvn_scr):
        step = pl.program_id(0)
        zu = z_ref[:, :W]
        zv = z_ref[:, W:]
        u, dgu = _gelu_and_grad(zu)
        xhat, rstd, vn = _sgu_norm(zv, g_ref, b_ref)
        vnb = vn.astype(BF16)
        dpf = dp_ref[...].astype(F32)
        ds = dpf * u

        @pl.when(step == 0)
        def _():
            dwm_ref[...] = jnp.zeros_like(dwm_ref)
            dbm_ref[...] = jnp.zeros_like(dbm_ref)

        for n in range(nch):
            rows = slice(n * CHUNK, (n + 1) * CHUNK)
            for gi in range(SGU_GROUPS):
                cols = slice(gi * LANES, (gi + 1) * LANES)
                s_scr[rows, cols] = _dot_nn(wm_ref[gi], vnb[rows, cols]) + bm_ref[:, cols]
                dsb = ds[rows, cols]
                dvn_scr[rows, cols] = _dot_tn(wm_ref[gi], dsb)
                dwm_ref[gi] += _dot_nt(dsb, vnb[rows, cols])
                dbm_ref[:, cols] += dsb
        dz_ref[:, :W] = (dpf * s_scr[...] * dgu).astype(BF16)
        dvn = dvn_scr[...]
        dxhat = dvn * g_ref[...]
        dv = rstd * (dxhat - jnp.mean(dxhat, axis=-1, keepdims=True)
                     - xhat * jnp.mean(dxhat * xhat, axis=-1, keepdims=True))
        _, dgv = _gelu_and_grad(zv)
        dz_ref[:, W:] = (dv * dgv).astype(BF16)
        dg_part = jnp.sum(dvn * xhat, axis=0, keepdims=True)
        db_part = jnp.sum(dvn, axis=0, keepdims=True)

        @pl.when(step == 0)
        def _():
            dg_ref[...] = dg_part
            db_ref[...] = db_part

        @pl.when(step > 0)
        def _():
            dg_ref[...] += dg_part
            db_ref[...] += db_part

        @pl.when(step == nsteps - 1)
        def _():
            for gi in range(SGU_GROUPS):
                cols = slice(gi * LANES, (gi + 1) * LANES)
                tot = jnp.sum(dbm_ref[:, cols], axis=1, keepdims=True)
                dbm_ref[:, cols] = jnp.broadcast_to(tot, (CHUNK, LANES))

    const2 = lambda r, c: pl.BlockSpec((r, c), lambda i: (0, 0))
    wspec = pl.BlockSpec((SGU_GROUPS, CHUNK, CHUNK), lambda i: (0, 0, 0))
    return _pcall(
        body, name=name, grid=(nsteps,),
        in_specs=[pl.BlockSpec((tt, 2 * W), lambda i: (i, 0)), pl.BlockSpec((tt, W), lambda i: (i, 0)),
                  const2(1, W), const2(1, W), wspec, const2(CHUNK, W)],
        out_specs=[pl.BlockSpec((tt, 2 * W), lambda i: (i, 0)), const2(1, W), const2(1, W), wspec, const2(CHUNK, W)],
        out_shape=[jax.ShapeDtypeStruct((T, 2 * W), BF16), jax.ShapeDtypeStruct((1, W), F32),
                   jax.ShapeDtypeStruct((1, W), F32), jax.ShapeDtypeStruct((SGU_GROUPS, CHUNK, CHUNK), F32),
                   jax.ShapeDtypeStruct((CHUNK, W), F32)],
        args=(zc, dp, ln_g, ln_b, wm, bmap), scratch=[pltpu.VMEM((tt, W), F32), pltpu.VMEM((tt, W), F32)], ride=ride)


def _rope_tables(positions):
    half = QK_ROPE // 2
    inv_freq = jnp.exp(-math.log(ROPE_BASE) * jnp.arange(half, dtype=F32) / half)
    ang = positions.reshape(-1).astype(F32)[:, None] * inv_freq
    cos = jnp.cos(ang)
    sin = jnp.sin(ang)
    n = ang.shape[0]
    tail = LANES - QK_NOPE - QK_ROPE
    cos_t = jnp.concatenate([jnp.ones((n, QK_NOPE), F32), cos, cos, jnp.ones((n, tail), F32)], axis=1)
    sin_t = jnp.concatenate([jnp.zeros((n, QK_NOPE), F32), -sin, sin, jnp.zeros((n, tail), F32)], axis=1)
    return cos_t, sin_t


SGU_GROUP_DIM = SGU_WIDTH // SGU_GROUPS
_O1, _O2, _O3, _O4 = Q_LORA, Q_LORA + KV_LORA, Q_LORA + KV_LORA + QK_ROPE, Q_LORA + KV_LORA + QK_ROPE + LRU_WIDTH
_A0, _A1, _A2 = 2 * LRU_WIDTH, 2 * LRU_WIDTH + Q_LORA, 2 * LRU_WIDTH + Q_LORA + KV_LORA
_A3 = _A2 + QK_NOPE
Z_Q_BLOCK, Z_KV_BLOCK, Z_KPE_BLOCK = _A0 // Q_LORA, _A1 // KV_LORA, _A2 // LANES


def _perm_w_in(w_in):
    zeros = lambda n: jnp.zeros((w_in.shape[0], n), w_in.dtype)
    return jnp.concatenate([w_in[:, _O3:_O4], w_in[:, _O4:], w_in[:, :_O1], w_in[:, _O1:_O2], zeros(QK_NOPE),
                            w_in[:, _O2:_O3], zeros(LANES - QK_NOPE - QK_ROPE)], axis=1)


def _unperm_w_in(w):
    return jnp.concatenate([w[:, _A0:_A1], w[:, _A1:_A2], w[:, _A3:_A3 + QK_ROPE], w[:, :LRU_WIDTH],
                            w[:, LRU_WIDTH:_A0]], axis=1)


def _head_blocks(w, d):
    r = w.shape[0]
    return jnp.pad(w.reshape(r, MLA_HEADS, d), ((0, 0), (0, 0), (0, LANES - d))).reshape(r, MLA_HEADS * LANES)


def _from_head_blocks(w, d):
    r = w.shape[0]
    return w.reshape(r, MLA_HEADS, LANES)[:, :, :d].reshape(r, MLA_HEADS * d)


def _split_kv(w_kv):
    r = w_kv.shape[0]
    w3 = w_kv.reshape(r, MLA_HEADS, QK_NOPE + V_HEAD)
    return _head_blocks(w3[:, :, :QK_NOPE].reshape(r, -1), QK_NOPE), w3[:, :, QK_NOPE:].reshape(r, -1)


def _join_kv(w_k, w_v):
    r = w_k.shape[0]
    return jnp.concatenate([_from_head_blocks(w_k, QK_NOPE).reshape(r, MLA_HEADS, QK_NOPE),
                            w_v.reshape(r, MLA_HEADS, V_HEAD)], axis=2).reshape(r, -1)


def _prep_small(w):
    p = {n: w[n] for n in w if n not in BIG}
    eye = jnp.eye(LRU_HEADS, dtype=F32)
    dense = lambda wg: (wg[:, :, None, :] * eye[:, None, :, None]).reshape(LRU_WIDTH, LRU_WIDTH).astype(BF16)
    p["wa_d"] = dense(w["ab_w_rg_a"][0])
    p["wx_d"] = dense(w["ab_w_rg_x"][0])
    causal = jnp.tril(jnp.ones((CHUNK, CHUNK), F32))
    p["wm"] = (w["c_w_s"][0] * causal).astype(BF16)
    p["bmap"] = jnp.repeat(w["c_b_s"][0].T, SGU_GROUP_DIM, axis=1)
    return p


def _prep_big(ab_w_in, ab_w_q_b, ab_w_kv_b):
    return {"w_in_p": _perm_w_in(ab_w_in).astype(BF16),
            "w_q_p": _head_blocks(ab_w_q_b, QK_NOPE + QK_ROPE).astype(BF16),
            "w_kv_p": jnp.concatenate(_split_kv(ab_w_kv_b), axis=1).astype(BF16)}


def _ffn_fwd(h, l, p, S, rides):
    hn = _rms_fwd(h, p["ffn_norm"][l], name=f"ffn{l}_norm")
    g = _mm(hn, p["ffn_gate_t"][l], tb=True, name=f"ffn{l}_gate", ride=rides.get(f"ffn{l}_gate"))
    u = _mm(hn, p["ffn_up_t"][l], tb=True, name=f"ffn{l}_up", ride=rides.get(f"ffn{l}_up"))
    act = _ffn_act_fwd(g, u, p["ffn_conv_w"][l], p["ffn_conv_b"][l][None], S=S, name=f"ffn{l}_act",
                       ride=rides.get(f"ffn{l}_act"))
    out = _mm(act, p["ffn_down"][l], res=h, name=f"ffn{l}_down", ride=rides.get(f"ffn{l}_down"))
    return out, (hn, g, u, act)


def _ffn_bwd(dh, h_in, l, p, saved, S, rides, grads_ready, also_ready=None):
    hn, g, u, act = saved
    dact = _mm(dh, p["ffn_down"][l], tb=True, out_dtype=BF16, name=f"ffn{l}_dact", ride=rides.get(f"ffn{l}_dact"))
    dw_down = _mm(act, dh, ta=True, out_dtype=BF16, name=f"ffn{l}_dwdown")
    dg, du, dcw, dcb = _ffn_act_bwd(g, u, dact, p["ffn_conv_w"][l], p["ffn_conv_b"][l][None], S=S,
                                    name=f"ffn{l}_dactbwd", ride=rides.get(f"ffn{l}_dactbwd"))
    dhn = _mm(dg, p["ffn_gate_t"][l], name=f"ffn{l}_dhn_g")
    dhn = _mm(du, p["ffn_up_t"][l], res=dhn, name=f"ffn{l}_dhn_u")
    dw_gate_t = _mm(dg, hn, ta=True, out_dtype=BF16, name=f"ffn{l}_dwgate")
    dw_up_t = _mm(du, hn, ta=True, out_dtype=BF16, name=f"ffn{l}_dwup")
    grads_ready(l, {**(also_ready or {}), "ffn_gate_t": dw_gate_t, "ffn_up_t": dw_up_t, "ffn_down": dw_down})
    dh_in, dnorm = _rms_bwd(h_in, p["ffn_norm"][l], dhn, res=dh, name=f"ffn{l}_dnorm", ride=rides.get(f"ffn{l}_dnorm"))
    grads = dict(ffn_norm=dnorm[0], ffn_gate_t=dw_gate_t, ffn_up_t=dw_up_t, ffn_conv_w=dcw[:FFN_CONV],
                 ffn_conv_b=dcb[0], ffn_down=dw_down)
    return dh_in, grads


def _local_step(x, positions, target, p, rides=None, grads_ready=None):
    rides = {} if rides is None else rides
    grads_ready = grads_ready or (lambda layer, ready: None)
    B, S, D = x.shape
    T = B * S
    H = MLA_HEADS
    xf = x.reshape(T, D)
    tgt = target.reshape(T, D)
    cos, sin = _rope_tables(positions)

    hn0 = _rms_fwd(xf, p["ab_norm"][0], name="ab_norm", ride=rides.get("ab_norm"))
    z = _mm(hn0, p["w_in_p"], name="ab_in")
    cqn = _rms_fwd(z, p["ab_q_norm"][0], cb=Z_Q_BLOCK, name="q_norm")
    ckvn = _rms_fwd(z, p["ab_kv_norm"][0], cb=Z_KV_BLOCK, name="kv_norm")
    q = _mm(cqn, p["w_q_p"], name="q_up")
    kv = _mm(ckvn, p["w_kv_p"], out_dtype=BF16, name="kv_up")
    qs = _rope_q(q, cos, sin, name="q_rope")
    kk = _key_blocks(kv, z, cos, sin, kpe_block=Z_KPE_BLOCK, name="k_rope")
    att = dict(B=B, S=S, v_block0=H)
    o, lse = _attn_fwd(qs, kk, kv, name="attn_fwd", ride=rides.get("attn_fwd"), **att)
    lru_par = (p["ab_conv_w"][0], p["ab_conv_b"], p["wa_d"], p["ab_b_rg_a"], p["wx_d"], p["ab_b_rg_x"], p["ab_lambda"])
    y_lru, hs = _lru_fwd(z, *lru_par, S=S, name="lru_fwd", ride=rides.get("lru_fwd"))
    n_att = H * V_HEAD
    w_out_a, w_out_b = p["ab_w_out"][:n_att], p["ab_w_out"][n_att:]
    h1 = _mm(y_lru, w_out_b, res=_mm(o, w_out_a, res=xf, name="ab_out_a"), name="ab_out_b")
    h2, ffn0 = _ffn_fwd(h1, 0, p, S, rides)

    hn2 = _rms_fwd(h2, p["c_norm"][0], name="c_norm")
    zc = _mm(hn2, p["c_w_in_t"], tb=True, name="c_in")
    pg = _sgu_fwd(zc, p["c_ln_g"], p["c_ln_b"], p["wm"], p["bmap"], name="sgu_fwd")
    h3 = _mm(pg, p["c_w_out"], res=h2, name="c_out")
    h4, ffn1 = _ffn_fwd(h3, 1, p, S, rides)

    loss_row, dh4, dfinal = _final_fwd_bwd(h4, p["final_norm"], tgt, name="final")

    dh3, g_ffn1 = _ffn_bwd(dh4, h3, 1, p, ffn1, S, rides, grads_ready)
    dpg = _mm(dh3, p["c_w_out"], tb=True, out_dtype=BF16, name="c_dp")
    dw_c_out = _mm(pg, dh3, ta=True, out_dtype=BF16, name="c_dwout")
    dzc, dlng, dlnb, dwm, dbm = _sgu_bwd(zc, dpg, p["c_ln_g"], p["c_ln_b"], p["wm"], p["bmap"], name="sgu_bwd",
                                         ride=rides.get("sgu_bwd"))
    dhn2 = _mm(dzc, p["c_w_in_t"], name="c_dhn")
    dw_c_in_t = _mm(dzc, hn2, ta=True, out_dtype=BF16, name="c_dwin")
    dh2, dcnorm = _rms_bwd(h2, p["c_norm"][0], dhn2, res=dh3, name="c_dnorm")
    dh1, g_ffn0 = _ffn_bwd(dh2, h1, 0, p, ffn0, S, rides, grads_ready, {"c_w_in_t": dw_c_in_t, "c_w_out": dw_c_out})

    do = _mm(dh1, w_out_a, tb=True, name="ab_do")
    dy_lru = _mm(dh1, w_out_b, tb=True, out_dtype=BF16, name="ab_dylru")
    dw_out = jnp.concatenate([_mm(o, dh1, ta=True, out_dtype=BF16, name="ab_dwout_a"),
                              _mm(y_lru, dh1, ta=True, out_dtype=BF16, name="ab_dwout_b")], axis=0)
    dq, delta = _attn_dq(qs, kk, kv, o, lse, do, name="attn_dq", ride=rides.get("attn_dq"), **att)
    nq = S // min(ATT_BLOCK, S)
    rows = lambda a: a.reshape(B, H, nq, S // nq)
    dk, dv = _attn_dkv(qs, kk, kv, rows(lse), rows(delta), do, name="attn_dkv", ride=rides.get("attn_dkv"), **att)
    dq_full = _rope_q_bwd(dq, cos, sin, name="q_rope_bwd")
    dkr = _key_rope_bwd(dk, cos, sin, name="k_rope_bwd")
    n_key = H * LANES
    w_k_p, w_v_p = p["w_kv_p"][:, :n_key], p["w_kv_p"][:, n_key:]
    dcqn = _mm(dq_full, p["w_q_p"], tb=True, name="q_dlat")
    dw_q_p = _mm(cqn, dq_full, ta=True, out_dtype=BF16, name="q_dw")
    dckvn = _mm(dv, w_v_p, tb=True, res=_mm(dk, w_k_p, tb=True, name="k_dlat"), name="v_dlat")
    dw_k_p = _mm(ckvn, dk, ta=True, out_dtype=BF16, name="k_dw")
    dw_v_p = _mm(ckvn, dv, ta=True, out_dtype=BF16, name="v_dw")
    dcq, dqnorm = _rms_bwd(z, p["ab_q_norm"][0], dcqn, cb=Z_Q_BLOCK, out_dtype=BF16, name="q_dnorm")
    dckv, dkvnorm = _rms_bwd(z, p["ab_kv_norm"][0], dckvn, cb=Z_KV_BLOCK, out_dtype=BF16, name="kv_dnorm")
    dxl, dgate, dcw, dcb, dwa, dba, dwx, dbx, dlam = _lru_bwd(z, hs, dy_lru, *lru_par, S=S, name="lru_bwd")
    dz = jnp.concatenate([dxl, dgate, dcq, dckv, dkr], axis=1)
    dhn0 = _mm(dz, p["w_in_p"], tb=True, name="ab_dhn")
    dw_in_p = _mm(hn0, dz, ta=True, out_dtype=BF16, name="ab_dwin")
    dx, dabnorm = _rms_bwd(xf, p["ab_norm"][0], dhn0, res=dh1, name="ab_dnorm")

    blocks = lambda dd: jnp.stack([dd[i * LRU_BLOCK:(i + 1) * LRU_BLOCK, i * LRU_BLOCK:(i + 1) * LRU_BLOCK]
                                   for i in range(LRU_HEADS)])
    causal = jnp.tril(jnp.ones((CHUNK, CHUNK), F32))
    grads = {
        "ab_norm": dabnorm, "w_in_p": dw_in_p, "ab_q_norm": dqnorm, "w_q_p": dw_q_p,
        "ab_kv_norm": dkvnorm, "w_k_p": dw_k_p, "w_v_p": dw_v_p, "ab_conv_w": dcw[:LRU_CONV][None], "ab_conv_b": dcb,
        "ab_w_rg_a": blocks(dwa)[None], "ab_b_rg_a": dba, "ab_w_rg_x": blocks(dwx)[None], "ab_b_rg_x": dbx,
        "ab_lambda": dlam, "ab_w_out": dw_out,
        "c_norm": dcnorm, "c_w_in_t": dw_c_in_t, "c_ln_g": dlng, "c_ln_b": dlnb,
        "c_w_s": (dwm * causal)[None], "c_b_s": dbm[:, ::SGU_GROUP_DIM].T[None], "c_w_out": dw_c_out,
        "final_norm": dfinal[0],
    }
    for name in ("ffn_norm", "ffn_conv_w", "ffn_conv_b"):
        grads[name] = jnp.stack([g_ffn0[name], g_ffn1[name]])
    for name in ("ffn_gate_t", "ffn_up_t", "ffn_down"):
        grads[name] = [g_ffn0[name], g_ffn1[name]]
    return loss_row, dx.reshape(B, S, D), grads


ANY = pl.BlockSpec(memory_space=pl.ANY)


def _place():
    x, y, c = lax.axis_index("x"), lax.axis_index("y"), lax.axis_index("c")
    chips = [(1 - x, y), (x, 1 - y), (1 - x, 1 - y)]
    return x, y, c, 2 * x + y, (x, y, 1 - c), chips


def _remote(src, dst, send_sems, recv_sems, k, to):
    return pltpu.make_async_remote_copy(src_ref=src, dst_ref=dst, send_sem=send_sems.at[k], recv_sem=recv_sems.at[k],
                                        device_id=to, device_id_type=MESH)


class _Exchange:
    def __init__(self, arrs, out_shapes, n_sems, start, finish):
        self.arrs, self.out_shapes, self.n_sems, self.start, self.finish = list(arrs), out_shapes, n_sems, start, finish

    @property
    def in_specs(self):
        return [ANY] * len(self.arrs)

    @property
    def out_specs(self):
        return [ANY] * len(self.out_shapes)

    @property
    def scratch(self):
        return [pltpu.SemaphoreType.DMA((self.n_sems,)), pltpu.SemaphoreType.DMA((self.n_sems,))]

    def split(self, refs):
        n = len(self.arrs)
        return refs[:n], refs[n:n + len(self.out_shapes)], refs[-2], refs[-1]

    def run(self, name):
        def body(*refs):
            parts = self.split(refs)
            self.start(*parts)
            self.finish(*parts)

        return pl.pallas_call(body, name=name, in_specs=self.in_specs, out_specs=self.out_specs,
                              out_shape=self.out_shapes, scratch_shapes=self.scratch)(*self.arrs)


def _put(buf, piece, idx, axis):
    return lax.dynamic_update_slice_in_dim(buf, jnp.expand_dims(piece, axis).astype(buf.dtype), idx, axis)


def _all_gather(arrs):
    n = len(arrs)

    def start(ins, outs, send_sems, recv_sems):
        x, y, c, j, sib, chips = _place()
        for i in range(n):
            for k, (cx, cy) in enumerate(chips):
                _remote(ins[i].at[:, c], outs[i].at[:, j, c], send_sems, recv_sems, 6 * i + k, (cx, cy, c)).start()

    def finish(ins, outs, send_sems, recv_sems):
        x, y, c, j, sib, chips = _place()
        passed = []
        for i in range(n):
            for k, (cx, cy) in enumerate(chips):
                got = outs[i].at[:, 2 * cx + cy, c]
                _remote(got, got, send_sems, recv_sems, 6 * i + k, (cx, cy, c)).wait_recv()
                cp = _remote(got, got, send_sems, recv_sems, 6 * i + 3 + k, sib)
                cp.start()
                passed.append(cp)
        for i in range(n):
            for k, (cx, cy) in enumerate(chips):
                got = outs[i].at[:, 2 * cx + cy, 1 - c]
                _remote(got, got, send_sems, recv_sems, 6 * i + 3 + k, sib).wait_recv()
                _remote(ins[i].at[:, c], ins[i].at[:, c], send_sems, recv_sems, 6 * i + k, sib).wait_send()
        for cp in passed:
            cp.wait_send()

    shapes = [jax.ShapeDtypeStruct((a.shape[0], N_CHIPS) + a.shape[1:], a.dtype) for a in arrs]
    return _Exchange(arrs, shapes, 6 * n, start, finish)


class _Offset:
    def __init__(self, sems, k0):
        self.sems, self.k0 = sems, k0

    @property
    def at(self):
        return self

    def __getitem__(self, k):
        return self.sems.at[self.k0 + k]


def _merge(a, b):
    n_in, n_out = len(a.arrs), len(a.out_shapes)

    def both(fa, fb):
        def f(ins, outs, send_sems, recv_sems):
            fa(ins[:n_in], outs[:n_out], send_sems, recv_sems)
            fb(ins[n_in:], outs[n_out:], _Offset(send_sems, a.n_sems), _Offset(recv_sems, a.n_sems))
        return f

    return _Exchange(a.arrs + b.arrs, a.out_shapes + b.out_shapes, a.n_sems + b.n_sems,
                     both(a.start, b.start), both(a.finish, b.finish))


def _pair_swap(arrs):
    n = len(arrs)

    def start(ins, outs, send_sems, recv_sems):
        x, y, c, j, sib, chips = _place()
        for i in range(n):
            _remote(ins[i].at[:, 1 - c], outs[i], send_sems, recv_sems, i, sib).start()

    def finish(ins, outs, send_sems, recv_sems):
        x, y, c, j, sib, chips = _place()
        for i in range(n):
            _remote(ins[i].at[:, 1 - c], outs[i], send_sems, recv_sems, i, sib).wait()

    shapes = [jax.ShapeDtypeStruct((a.shape[0],) + a.shape[2:], a.dtype) for a in arrs]
    return _Exchange(arrs, shapes, n, start, finish)


def _pair_send(arrs):
    n = len(arrs)

    def start(ins, outs, send_sems, recv_sems):
        x, y, c, j, sib, chips = _place()
        for i in range(n):
            _remote(ins[i], outs[i], send_sems, recv_sems, i, sib).start()

    def finish(ins, outs, send_sems, recv_sems):
        x, y, c, j, sib, chips = _place()
        for i in range(n):
            _remote(ins[i], outs[i], send_sems, recv_sems, i, sib).wait()

    shapes = [jax.ShapeDtypeStruct(a.shape, a.dtype) for a in arrs]
    return _Exchange(arrs, shapes, n, start, finish)


def _chip_exchange(arrs, *, scatter):
    n = len(arrs)

    def copies(ins, outs, send_sems, recv_sems):
        x, y, c, j, sib, chips = _place()
        return [(_remote(ins[i].at[2 * cx + cy] if scatter else ins[i], outs[i].at[j], send_sems, recv_sems,
                         3 * i + k, (cx, cy, c)),
                 _remote(outs[i].at[2 * cx + cy], outs[i].at[2 * cx + cy], send_sems, recv_sems, 3 * i + k, (cx, cy, c)))
                for i in range(n) for k, (cx, cy) in enumerate(chips)]

    def start(*refs):
        for out, _ in copies(*refs):
            out.start()

    def finish(*refs):
        for out, back in copies(*refs):
            back.wait_recv()
            out.wait_send()

    shapes = [jax.ShapeDtypeStruct((N_CHIPS,) + a.shape[-2:], a.dtype) for a in arrs]
    return _Exchange(arrs, shapes, 3 * n, start, finish)


FLAT_ROWS = 512


def _add2(a, b, *, out_dtype, name):
    n, R, L = a.shape
    tr = _tile(R, FLAT_ROWS, 16)

    def body(a_ref, b_ref, o_ref):
        o_ref[...] = (a_ref[...].astype(F32) + b_ref[...].astype(F32)).astype(out_dtype)

    spec = pl.BlockSpec((n, tr, L), lambda i: (0, i, 0))
    return pl.pallas_call(
        body, name=name, grid=(R // tr,), in_specs=[spec, spec], out_specs=spec,
        out_shape=jax.ShapeDtypeStruct(a.shape, out_dtype), compiler_params=_cparams(("parallel",)),
    )(a, b)


def _sum_slots(buf, *, name):
    n, R, L = buf.shape
    tr = _tile(R, FLAT_ROWS, 16)

    def body(b_ref, o_ref):
        acc = b_ref[0].astype(F32)
        for k in range(1, n):
            acc = acc + b_ref[k].astype(F32)
        o_ref[...] = acc

    return pl.pallas_call(
        body, name=name, grid=(R // tr,), in_specs=[pl.BlockSpec((n, tr, L), lambda i: (0, i, 0))],
        out_specs=pl.BlockSpec((tr, L), lambda i: (i, 0)),
        out_shape=jax.ShapeDtypeStruct((R, L), F32), compiler_params=_cparams(("parallel",)),
    )(buf)


def _adamw_update(w, g, m, v):
    c1 = 1.0 - ADAM_B1 ** ADAM_STEP
    c2 = 1.0 - ADAM_B2 ** ADAM_STEP
    m = ADAM_B1 * m + (1.0 - ADAM_B1) * g
    v = ADAM_B2 * v + (1.0 - ADAM_B2) * (g * g)
    return -ADAM_LR * ((m / c1) / (jnp.sqrt(v / c2) + ADAM_EPS) + ADAM_WD * w), m, v


def _adamw_halves(w, m, v, own, other, *, name):
    NL, R, L = w.shape
    h = R // 2
    tr = _tile(h, FLAT_ROWS, 16)
    nt = h // tr

    def body(*refs):
        w_ref, m_ref, v_ref = refs[:3]
        own_refs, other_refs = refs[3:3 + NL], refs[3 + NL:3 + 2 * NL]
        d_ref, nm_ref, nv_ref, g_ref = refs[3 + 2 * NL:]
        layer, half = pl.program_id(0), pl.program_id(1)
        mine = half == lax.axis_index("c")
        g = jnp.where(mine, own_refs[0][...], other_refs[0][...])
        for l in range(1, NL):
            g = jnp.where(layer == l, jnp.where(mine, own_refs[l][...], other_refs[l][...]), g)
        d, mm, vv = _adamw_update(w_ref[0], g, m_ref[0], v_ref[0])
        d_ref[0], nm_ref[0], nv_ref[0], g_ref[0] = d, mm, vv, g

    spec = pl.BlockSpec((1, tr, L), lambda l, hh, i: (l, hh * nt + i, 0))
    part = pl.BlockSpec((tr, L), lambda l, hh, i: (i, 0))
    sh = jax.ShapeDtypeStruct((NL, R, L), F32)
    return pl.pallas_call(
        body, name=name, grid=(NL, 2, nt), in_specs=[spec] * 3 + [part] * (2 * NL), out_specs=[spec] * 4,
        out_shape=[sh] * 4, compiler_params=_cparams(("parallel", "parallel", "parallel")),
    )(w, m, v, *own, *other)


def _adamw(w, g, m, v, *, name):
    NL, R, L = w.shape
    tr = _tile(R, FLAT_ROWS, 16)

    def body(w_ref, g_ref, m_ref, v_ref, d_ref, nm_ref, nv_ref):
        d_ref[...], nm_ref[...], nv_ref[...] = _adamw_update(w_ref[...], g_ref[...], m_ref[...], v_ref[...])

    spec = pl.BlockSpec((1, tr, L), lambda l, i: (l, i, 0))
    sh = jax.ShapeDtypeStruct((NL, R, L), F32)
    return pl.pallas_call(
        body, name=name, grid=(NL, R // tr), in_specs=[spec] * 4, out_specs=[spec] * 3, out_shape=[sh] * 3,
        compiler_params=_cparams(("parallel", "parallel")),
    )(w, g, m, v)


WEIGHT_NAMES = ["ab_norm", "ab_w_in", "ab_q_norm", "ab_w_q_b", "ab_kv_norm", "ab_w_kv_b", "ab_conv_w", "ab_conv_b",
                "ab_w_rg_a", "ab_b_rg_a", "ab_w_rg_x", "ab_b_rg_x", "ab_lambda", "ab_w_out", "c_norm", "c_w_in",
                "c_ln_g", "c_ln_b", "c_w_s", "c_b_s", "c_w_out", "ffn_norm", "ffn_w_gate", "ffn_w_up", "ffn_conv_w",
                "ffn_conv_b", "ffn_w_down", "final_norm"]
BIG = {"ab_w_in": 2, "ab_w_q_b": 2, "ab_w_kv_b": 2, "ab_w_out": 1, "c_w_in": 2, "c_w_out": 1,
       "ffn_w_gate": 2, "ffn_w_up": 2, "ffn_w_down": 1}
SMALL_SHARDED = {"ab_conv_w": 2, "c_norm": 1, "c_ln_g": 1, "c_ln_b": 1, "ffn_conv_w": 2}
SMALL_REPLICATED = [n for n in WEIGHT_NAMES if n not in BIG and n not in SMALL_SHARDED]


def _rows(n_elems, mult):
    r = -(-n_elems // LANES)
    return -(-r // mult) * mult


def _flat(parts, rows):
    flat = jnp.concatenate([a.reshape(-1) for a in parts])
    return jnp.pad(flat, (0, rows * LANES - flat.shape[0])).reshape(rows, LANES)


def _unflat(flat, shapes):
    flat = flat.reshape(-1)
    out, off = [], 0
    for s in shapes:
        n = math.prod(s)
        out.append(flat[off:off + n].reshape(s))
        off += n
    return out


def _join_shards(a, axis):
    a = jnp.moveaxis(a, 0, axis)
    return a.reshape(a.shape[:axis] + (a.shape[axis] * a.shape[axis + 1],) + a.shape[axis + 2:])


def kernel(x, positions, ab_norm, ab_w_in, ab_q_norm, ab_w_q_b, ab_kv_norm, ab_w_kv_b, ab_conv_w, ab_conv_b, ab_w_rg_a, ab_b_rg_a, ab_w_rg_x, ab_b_rg_x, ab_lambda, ab_w_out, c_norm, c_w_in, c_ln_g, c_ln_b, c_w_s, c_b_s, c_w_out, ffn_norm, ffn_w_gate, ffn_w_up, ffn_conv_w, ffn_conv_b, ffn_w_down, final_norm, loss_target, m_ab_norm, m_ab_w_in, m_ab_q_norm, m_ab_w_q_b, m_ab_kv_norm, m_ab_w_kv_b, m_ab_conv_w, m_ab_conv_b, m_ab_w_rg_a, m_ab_b_rg_a, m_ab_w_rg_x, m_ab_b_rg_x, m_ab_lambda, m_ab_w_out, m_c_norm, m_c_w_in, m_c_ln_g, m_c_ln_b, m_c_w_s, m_c_b_s, m_c_w_out, m_ffn_norm, m_ffn_w_gate, m_ffn_w_up, m_ffn_conv_w, m_ffn_conv_b, m_ffn_w_down, m_final_norm, v_ab_norm, v_ab_w_in, v_ab_q_norm, v_ab_w_q_b, v_ab_kv_norm, v_ab_w_kv_b, v_ab_conv_w, v_ab_conv_b, v_ab_w_rg_a, v_ab_b_rg_a, v_ab_w_rg_x, v_ab_b_rg_x, v_ab_lambda, v_ab_w_out, v_c_norm, v_c_w_in, v_c_ln_g, v_c_ln_b, v_c_w_s, v_c_b_s, v_c_w_out, v_ffn_norm, v_ffn_w_gate, v_ffn_w_up, v_ffn_conv_w, v_ffn_conv_b, v_ffn_w_down, v_final_norm):
    given = dict(locals())
    w = {n: given[n] for n in WEIGHT_NAMES}
    m = {n: given["m_" + n] for n in WEIGHT_NAMES}
    v = {n: given["v_" + n] for n in WEIGHT_NAMES}
    c = lax.axis_index("c")
    chip = 2 * lax.axis_index("x") + lax.axis_index("y")

    halves = lambda a: a.reshape(a.shape[0], 2, a.shape[1] // 2, a.shape[2])
    tr = lambda a: jnp.swapaxes(a, 1, 2)
    send = {"ab_w_in": w["ab_w_in"], "ab_w_q_b": w["ab_w_q_b"], "ab_w_kv_b": w["ab_w_kv_b"], "ab_w_out": w["ab_w_out"],
            "c_w_in": tr(w["c_w_in"]), "c_w_out": w["c_w_out"], "ffn_w_gate": tr(w["ffn_w_gate"]),
            "ffn_w_up": tr(w["ffn_w_up"]), "ffn_w_down": w["ffn_w_down"]}
    small_rows = _rows(sum(w[n].size for n in SMALL_SHARDED), 16)
    small_sh = _flat([w[n] for n in SMALL_SHARDED], small_rows).reshape(1, 2, small_rows // 2, LANES)
    first_names = ["ab_w_in", "ab_w_q_b", "ab_w_kv_b", "ab_w_out"]
    mine = {n: halves(send[n].astype(BF16)) for n in BIG}

    def put_own(own, arrived):
        a = _put(arrived, own, chip, 1)
        return a.reshape(a.shape[0], -1, a.shape[-1])

    p = {"ab_norm": w["ab_norm"], "ffn_gate_t": {}, "ffn_up_t": {}, "ffn_down": {}}
    first = [mine[n] for n in first_names] + [small_sh]

    def first_arrived(got):
        full = {n: put_own(o, a) for n, o, a in zip(first_names + ["small"], first, got)}
        unshard = lambda a: jnp.swapaxes(a.reshape(N_CHIPS, -1, a.shape[-1]), 0, 1).reshape(-1, N_CHIPS * a.shape[-1])
        p.update(_prep_big(unshard(full["ab_w_in"][0]), unshard(full["ab_w_q_b"][0]), unshard(full["ab_w_kv_b"][0])))
        p["ab_w_out"] = full["ab_w_out"][0]
        small_full = dict(w)
        off = 0
        small_got = full["small"].reshape(N_CHIPS, -1)
        for n, ax in SMALL_SHARDED.items():
            seg = small_got[:, off:off + w[n].size].reshape((N_CHIPS,) + w[n].shape)
            small_full[n] = _join_shards(seg, ax)
            off += w[n].size
        p.update(_prep_small(small_full))

    def weights_ride(parts):
        def sink(arrived):
            for (own, setter), a in zip(parts, arrived):
                setter(put_own(own, a)[0])
        return _all_gather([own for own, _ in parts]), sink

    ffn_keys = {"ffn_gate_t": "ffn_w_gate", "ffn_up_t": "ffn_w_up", "ffn_down": "ffn_w_down"}
    ffn_part = lambda key, l: (mine[ffn_keys[key]][l:l + 1], functools.partial(p[key].__setitem__, l))
    rides = {
        "ab_norm": (_all_gather(first), first_arrived),
        "attn_fwd": weights_ride([ffn_part("ffn_gate_t", 0), ffn_part("ffn_up_t", 0)]),
        "lru_fwd": weights_ride([ffn_part("ffn_down", 0)]),
        "ffn0_gate": weights_ride([ffn_part("ffn_gate_t", 1)]),
        "ffn0_up": weights_ride([ffn_part("ffn_up_t", 1)]),
        "ffn0_act": weights_ride([ffn_part("ffn_down", 1)]),
        "ffn0_down": weights_ride([(mine["c_w_in"], functools.partial(p.__setitem__, "c_w_in_t")),
                                   (mine["c_w_out"], functools.partial(p.__setitem__, "c_w_out"))]),
    }

    def chip_sums(pair, arrived, tag):
        own = [lax.dynamic_index_in_dim(a, chip, axis=0, keepdims=False) for a in pair]
        return [_sum_slots(_put(a, o, chip, 0), name=f"grad_chip_sum_{tag}{i}") for i, (a, o) in enumerate(zip(arrived, own))]

    half_of = {}

    def grads_ready(layer, ready):
        if layer == 1:
            named = {"gate1": ready["ffn_gate_t"], "up1": ready["ffn_up_t"], "down1": ready["ffn_down"]}
            hosts = {"sgu_bwd": ["down1"], "ffn0_dact": ["up1"], "ffn0_dactbwd": ["gate1"]}
        else:
            named = {"c_in": ready["c_w_in_t"], "c_out": ready["c_w_out"], "gate0": ready["ffn_gate_t"],
                     "up0": ready["ffn_up_t"], "down0": ready["ffn_down"]}
            hosts = {"attn_dq": ["c_in", "c_out", "down0"], "attn_dkv": ["gate0", "up0"]}
        tag = f"f{layer}"
        sharded = [a.reshape(N_CHIPS, 2, -1, a.shape[-1]) for a in named.values()]

        def paired(from_sib):
            own = [lax.dynamic_index_in_dim(a, c, axis=1, keepdims=False) for a in sharded]
            pair = {k: _add2(a, b, out_dtype=BF16, name=f"grad_pair_add_{tag}{i}")
                    for i, (k, a, b) in enumerate(zip(named, own, from_sib))}
            for kernel_name, keys in hosts.items():
                def sink(arrived, keys=keys, kernel_name=kernel_name):
                    half_of.update(zip(keys, chip_sums([pair[k] for k in keys], arrived, f"{tag}_{kernel_name}")))
                rides[kernel_name] = (_chip_exchange([pair[k] for k in keys], scatter=True), sink)

        rides[f"ffn{layer}_dnorm"] = (_pair_swap(sharded), paired)

    loss_row, grad_x, g = _local_step(x, positions, loss_target, p, rides, grads_ready)

    cols = lambda a, n: jnp.swapaxes(a.reshape(a.shape[0], N_CHIPS, n), 0, 1)
    n_in, n_q, n_kv = w["ab_w_in"].shape[2], w["ab_w_q_b"].shape[2], w["ab_w_kv_b"].shape[2]
    small_names = SMALL_REPLICATED + list(SMALL_SHARDED)
    rs = _rows(sum(g[n].size for n in small_names) + LANES, FLAT_ROWS)
    small = _flat([loss_row] + [g[n] for n in small_names], rs)
    slot = (jnp.arange(2) == c)[:, None, None]
    last = [cols(_unperm_w_in(g["w_in_p"]), n_in), cols(_from_head_blocks(g["w_q_p"], QK_NOPE + QK_ROPE), n_q),
            cols(_join_kv(g["w_k_p"], g["w_v_p"]), n_kv), g["ab_w_out"]]
    last = [a.reshape(N_CHIPS, 2, -1, a.shape[-1]) for a in last]
    *from_sib, small_sib = _merge(_pair_swap(last), _pair_send([small])).run("tail_pair")
    own = [lax.dynamic_index_in_dim(a, c, axis=1, keepdims=False) for a in last]
    pair = [_add2(a, b, out_dtype=BF16, name=f"grad_pair_add_b{i}") for i, (a, b) in enumerate(zip(own, from_sib))]
    pair_small = _sum_slots(jnp.where(slot, small[None], small_sib[None]), name="small_pair_sum")
    *arrived, all_small = _merge(_chip_exchange(pair, scatter=True), _chip_exchange([pair_small], scatter=False)).run("tail_chip")
    half_of.update(zip(["in", "q", "kv", "out"], chip_sums(pair, arrived, "b")))
    small_sum = _sum_slots(_put(all_small, pair_small, chip, 0), name="small_chip_sum")
    keys = ("in", "q", "kv", "out", "c_in", "c_out", "gate0", "gate1", "up0", "up1", "down0", "down1")
    other_half = dict(zip(keys, _pair_send([half_of[k] for k in keys]).run("grad_pair_share")))
    whole = lambda k: jnp.where(slot, half_of[k][None], other_half[k][None]).reshape(-1, half_of[k].shape[-1])
    grads_t = {"ab_w_in": whole("in").T[None], "ab_w_q_b": whole("q").T[None]}
    grads = {"ab_w_kv_b": whole("kv")[None], "c_w_in": whole("c_in").T[None], **{n: tr(a) for n, a in grads_t.items()}}
    by_halves = {"ab_w_out": (("out",), False), "c_w_out": (("c_out",), False), "ffn_w_down": (("down0", "down1"), False),
                 "ffn_w_gate": (("gate0", "gate1"), True), "ffn_w_up": (("up0", "up1"), True)}

    small_parts = _unflat(small_sum, [(1, LANES)] + [g[n].shape for n in small_names])
    loss = small_parts[0][0, 0]
    for n, a in zip(small_names, small_parts[1:]):
        if n in SMALL_SHARDED:
            ax = SMALL_SHARDED[n]
            a = lax.dynamic_slice_in_dim(a, chip * w[n].shape[ax], w[n].shape[ax], axis=ax)
        grads[n] = a.reshape(w[n].shape)

    delta, new_m, new_v = {}, {}, {}
    for n in BIG:
        if n in by_halves:
            ks, transposed = by_halves[n]
            view = tr if transposed else (lambda a: a)
            out = _adamw_halves(view(w[n]), view(m[n]), view(v[n]), [half_of[k] for k in ks], [other_half[k] for k in ks],
                                name=f"adamw_{n}")
            delta[n], new_m[n], new_v[n], grads[n] = (view(a) for a in out)
        elif n in grads_t:
            out = _adamw(tr(w[n]), grads_t[n], tr(m[n]), tr(v[n]), name=f"adamw_{n}")
            delta[n], new_m[n], new_v[n] = (tr(a) for a in out)
        else:
            delta[n], new_m[n], new_v[n] = _adamw(w[n], grads[n], m[n], v[n], name=f"adamw_{n}")
    small_all = [n for n in WEIGHT_NAMES if n not in BIG]
    ra = _rows(sum(w[n].size for n in small_all), FLAT_ROWS)
    pack = lambda d: _flat([d[n] for n in small_all], ra)[None]
    out = _adamw(pack(w), pack(grads), pack(m), pack(v), name="adamw_small")
    shapes = [w[n].shape for n in small_all]
    for d, flat in zip((delta, new_m, new_v), out):
        d.update(zip(small_all, _unflat(flat, shapes)))
    return (loss, grad_x, *[grads[n] for n in WEIGHT_NAMES], *[delta[n] for n in WEIGHT_NAMES],
            *[new_m[n] for n in WEIGHT_NAMES], *[new_v[n] for n in WEIGHT_NAMES])
```

```python
import functools
import math

import jax
import jax.numpy as jnp
from jax import lax
from jax.experimental import pallas as pl
from jax.experimental.pallas import tpu as pltpu

F32 = jnp.float32
BF16 = jnp.bfloat16
MESH = pl.DeviceIdType.MESH

D_MODEL = 1024
MLA_HEADS = 8
Q_LORA = 256
KV_LORA = 128
QK_NOPE = 64
QK_ROPE = 32
V_HEAD = 64
LRU_WIDTH = 512
LRU_HEADS = 8
LRU_BLOCK = 64
LRU_CONV = 4
LRU_C = 8.0
CHUNK = 128
SGU_GROUPS = 8
SGU_WIDTH = 1024
D_FF = 2816
FFN_CONV = 3
NORM_EPS = 1e-6
ROPE_BASE = 10000.0
AB_IN_PAD = 1536
ADAM_LR = 0.001
ADAM_B1 = 0.9
ADAM_B2 = 0.999
ADAM_EPS = 1e-08
ADAM_WD = 0.01
ADAM_STEP = 10

N_CHIPS = 4
LANES = 128
VMEM_LIMIT = 56 * 1024 * 1024
ROW_TILE = 256
NORM_TILE = 1024
MM_TM, MM_TN, MM_TK = 1024, 1536, 2816
MM_TM_T, MM_TK_T = 1408, 1024
GELU_C = math.sqrt(2.0 / math.pi)


def _cparams(sem):
    return pltpu.CompilerParams(dimension_semantics=sem, vmem_limit_bytes=VMEM_LIMIT)


def _tile(n, target, mult=LANES):
    t = (min(n, target) // mult) * mult
    while t >= mult:
        if n % t == 0:
            return t
        t -= mult
    return n


GELU_K = GELU_C * 0.044715


def _gelu(x):
    t = jnp.tanh(x * (GELU_C + GELU_K * (x * x)))
    hx = 0.5 * x
    return hx + hx * t


def _gelu_and_grad(x):
    x2 = x * x
    t = jnp.tanh(x * (GELU_C + GELU_K * x2))
    hx = 0.5 * x
    dg = (0.5 + 0.5 * t) + (hx * (1.0 - t * t)) * (GELU_C + (3.0 * GELU_K) * x2)
    return hx + hx * t, dg


def _sigmoid(x):
    return 1.0 / (1.0 + jnp.exp(-x))


def _shift_rows(x, d, fill_rows):
    ext = jnp.concatenate([fill_rows, x], axis=0)
    return pltpu.roll(ext, d, 0)[8:]


def _shift_rows_up(x, d, fill_rows):
    n = x.shape[0]
    ext = jnp.concatenate([x, fill_rows], axis=0)
    return pltpu.roll(ext, n + 8 - d, 0)[:n]


def _dot(a, b, dims):
    return lax.dot_general(a.astype(BF16), b.astype(BF16), (dims, ((), ())), preferred_element_type=F32)


def _dot_nn(a, b):
    return _dot(a, b, ((1,), (0,)))


def _dot_nt(a, b):
    return _dot(a, b, ((1,), (1,)))


def _dot_tn(a, b):
    return _dot(a, b, ((0,), (0,)))


def _mm(a, b, *, name, ta=False, tb=False, res=None, out_dtype=F32, ride=None):
    if ta:
        K, M = a.shape
    else:
        M, K = a.shape
    N = b.shape[0] if tb else b.shape[1]
    tm = _tile(M, MM_TM_T if ta else (MM_TM if K <= MM_TM else MM_TM // 2), LANES if ta else 8)
    tn = _tile(N, MM_TN, LANES)
    tk = _tile(K, MM_TK_T if ta else MM_TK, LANES)
    nk = K // tk
    a_spec = pl.BlockSpec((tk, tm), lambda j, i, k: (k, i)) if ta else pl.BlockSpec((tm, tk), lambda j, i, k: (i, k))
    b_spec = pl.BlockSpec((tn, tk), lambda j, i, k: (j, k)) if tb else pl.BlockSpec((tk, tn), lambda j, i, k: (k, j))
    o_spec = pl.BlockSpec((tm, tn), lambda j, i, k: (i, j))
    dims = ((0,) if ta else (1,), (1,) if tb else (0,))
    has_res = res is not None

    def body(*refs):
        a_ref, b_ref = refs[:2]
        r_ref = refs[2] if has_res else None
        o_ref = refs[3] if has_res else refs[2]
        p = _dot(a_ref[...], b_ref[...], dims)

        def finish(r):
            if has_res:
                r = r + r_ref[...].astype(F32)
            o_ref[...] = r.astype(out_dtype)

        if nk == 1:
            finish(p)
            return
        acc_ref = refs[-1]
        k = pl.program_id(2)

        @pl.when(k == 0)
        def _():
            acc_ref[...] = p

        @pl.when(jnp.logical_and(k > 0, k < nk - 1))
        def _():
            acc_ref[...] += p

        @pl.when(k == nk - 1)
        def _():
            finish(acc_ref[...] + p)

    in_specs = [a_spec, b_spec] + ([o_spec] if has_res else [])
    args = (a, b) + ((res,) if has_res else ())
    return _pcall(
        body, name=name, grid=(N // tn, M // tm, nk), in_specs=in_specs, out_specs=[o_spec],
        out_shape=[jax.ShapeDtypeStruct((M, N), out_dtype)], args=args,
        scratch=[pltpu.VMEM((tm, tn), F32)] if nk > 1 else [], sem=("parallel", "parallel", "arbitrary"), ride=ride)[0]


def _rms_fwd(x, g, *, name, cb=0, out_dtype=BF16, ride=None):
    T = x.shape[0]
    W = g.shape[-1]
    g = g.reshape(1, W)
    tt = _tile(T, NORM_TILE, 16)

    def body(x_ref, g_ref, o_ref):
        xf = x_ref[...].astype(F32)
        rstd = lax.rsqrt(jnp.mean(xf * xf, axis=-1, keepdims=True) + NORM_EPS)
        o_ref[...] = (xf * rstd * g_ref[...]).astype(out_dtype)

    return _pcall(
        body, name=name, grid=(T // tt,),
        in_specs=[pl.BlockSpec((tt, W), lambda i: (i, cb)), pl.BlockSpec((1, W), lambda i: (0, 0))],
        out_specs=[pl.BlockSpec((tt, W), lambda i: (i, 0))], out_shape=[jax.ShapeDtypeStruct((T, W), out_dtype)],
        args=(x, g), sem=("parallel",), ride=ride)[0]


def _rms_bwd(x, g, dy, *, name, cb=0, res=None, out_dtype=F32, ride=None):
    T = x.shape[0]
    W = g.shape[-1]
    g = g.reshape(1, W)
    tt = _tile(T, NORM_TILE // 2, 16)
    has_res = res is not None

    def body(*refs):
        if has_res:
            x_ref, g_ref, dy_ref, r_ref, dx_ref, dg_ref = refs
        else:
            x_ref, g_ref, dy_ref, dx_ref, dg_ref = refs
        xf = x_ref[...].astype(F32)
        dyf = dy_ref[...].astype(F32)
        rstd = lax.rsqrt(jnp.mean(xf * xf, axis=-1, keepdims=True) + NORM_EPS)
        xhat = xf * rstd
        dxhat = dyf * g_ref[...]
        dx = rstd * (dxhat - xhat * jnp.mean(dxhat * xhat, axis=-1, keepdims=True))
        if has_res:
            dx = dx + r_ref[...].astype(F32)
        dx_ref[...] = dx.astype(out_dtype)
        part = jnp.sum(dyf * xhat, axis=0, keepdims=True)

        @pl.when(pl.program_id(0) == 0)
        def _():
            dg_ref[...] = part

        @pl.when(pl.program_id(0) > 0)
        def _():
            dg_ref[...] += part

    row = pl.BlockSpec((tt, W), lambda i: (i, 0))
    in_specs = [pl.BlockSpec((tt, W), lambda i: (i, cb)), pl.BlockSpec((1, W), lambda i: (0, 0)), row]
    args = (x, g, dy)
    if has_res:
        in_specs.append(row)
        args = args + (res,)
    return _pcall(
        body, name=name, grid=(T // tt,), in_specs=in_specs,
        out_specs=[row, pl.BlockSpec((1, W), lambda i: (0, 0))],
        out_shape=[jax.ShapeDtypeStruct((T, W), out_dtype), jax.ShapeDtypeStruct((1, W), F32)], args=args, ride=ride)


def _final_fwd_bwd(h, g, target, *, name):
    T, W = h.shape
    g = g.reshape(1, W)
    tt = _tile(T, NORM_TILE, 16)

    def body(x_ref, g_ref, t_ref, loss_ref, dx_ref, dg_ref):
        xf = x_ref[...]
        rstd = lax.rsqrt(jnp.mean(xf * xf, axis=-1, keepdims=True) + NORM_EPS)
        xhat = xf * rstd
        err = xhat * g_ref[...] - t_ref[...]
        lpart = jnp.zeros((1, LANES), F32) + (0.5 / W) * jnp.sum(err * err)
        dyf = err * (1.0 / W)
        dxhat = dyf * g_ref[...]
        dx_ref[...] = rstd * (dxhat - xhat * jnp.mean(dxhat * xhat, axis=-1, keepdims=True))
        part = jnp.sum(dyf * xhat, axis=0, keepdims=True)

        @pl.when(pl.program_id(0) == 0)
        def _():
            dg_ref[...] = part
            loss_ref[...] = lpart

        @pl.when(pl.program_id(0) > 0)
        def _():
            dg_ref[...] += part
            loss_ref[...] += lpart

    row = pl.BlockSpec((tt, W), lambda i: (i, 0))
    return pl.pallas_call(
        body, name=name, grid=(T // tt,),
        in_specs=[row, pl.BlockSpec((1, W), lambda i: (0, 0)), row],
        out_specs=[pl.BlockSpec((1, LANES), lambda i: (0, 0)), row, pl.BlockSpec((1, W), lambda i: (0, 0))],
        out_shape=[jax.ShapeDtypeStruct((1, LANES), F32), jax.ShapeDtypeStruct((T, W), F32),
                   jax.ShapeDtypeStruct((1, W), F32)],
        compiler_params=_cparams(("arbitrary",)),
    )(h, g, target)


def _swap16(x):
    lane = lax.broadcasted_iota(jnp.int32, x.shape, 1)
    return jnp.where((lane % 32) < 16, pltpu.roll(x, LANES - 16, 1), pltpu.roll(x, 16, 1))


def _rope(x, c, s):
    return x * c + _swap16(x) * s


def _rope_t(d, c, s):
    return d * c + _swap16(d * s)


def _head_block_map(fn, x, cos, sin, *, name):
    T, W = x.shape
    tt = _tile(T, NORM_TILE, 16)

    def body(x_ref, c_ref, s_ref, o_ref):
        c, s = c_ref[...], s_ref[...]
        for h in range(W // LANES):
            lanes = slice(h * LANES, (h + 1) * LANES)
            o_ref[:, lanes] = fn(x_ref[:, lanes], c, s).astype(BF16)

    tab = pl.BlockSpec((tt, LANES), lambda i: (i, 0))
    blk = pl.BlockSpec((tt, W), lambda i: (i, 0))
    return pl.pallas_call(
        body, name=name, grid=(T // tt,), in_specs=[blk, tab, tab], out_specs=blk,
        out_shape=jax.ShapeDtypeStruct((T, W), BF16), compiler_params=_cparams(("parallel",)),
    )(x, cos, sin)


def _rope_q(q, cos, sin, *, name):
    scale = _attn_scale()
    return _head_block_map(lambda x, c, s: _rope(x, c, s) * scale, q, cos, sin, name=name)


def _rope_q_bwd(dq, cos, sin, *, name):
    return _head_block_map(_rope_t, dq, cos, sin, name=name)


def _key_blocks(kv, z, cos, sin, *, kpe_block, name):
    T = kv.shape[0]
    tt = _tile(T, NORM_TILE, 16)
    W = MLA_HEADS * LANES

    def body(kv_ref, z_ref, c_ref, s_ref, o_ref):
        kr = _rope(z_ref[...], c_ref[...], s_ref[...])
        for h in range(MLA_HEADS):
            lanes = slice(h * LANES, (h + 1) * LANES)
            o_ref[:, lanes] = (kv_ref[:, lanes].astype(F32) + kr).astype(BF16)

    tab = pl.BlockSpec((tt, LANES), lambda i: (i, 0))
    blk = pl.BlockSpec((tt, W), lambda i: (i, 0))
    return pl.pallas_call(
        body, name=name, grid=(T // tt,),
        in_specs=[blk, pl.BlockSpec((tt, LANES), lambda i: (i, kpe_block)), tab, tab], out_specs=blk,
        out_shape=jax.ShapeDtypeStruct((T, W), BF16), compiler_params=_cparams(("parallel",)),
    )(kv, z, cos, sin)


def _key_rope_bwd(dk, cos, sin, *, name):
    T = dk.shape[0]
    tt = _tile(T, NORM_TILE, 16)

    def body(d_ref, c_ref, s_ref, o_ref):
        d = d_ref[:, :LANES]
        for h in range(1, MLA_HEADS):
            d = d + d_ref[:, h * LANES:(h + 1) * LANES]
        lane = lax.broadcasted_iota(jnp.int32, d.shape, 1)
        d = jnp.where(jnp.logical_and(lane >= QK_NOPE, lane < QK_NOPE + QK_ROPE), d, 0.0)
        o_ref[...] = _rope_t(d, c_ref[...], s_ref[...]).astype(BF16)

    tab = pl.BlockSpec((tt, LANES), lambda i: (i, 0))
    return pl.pallas_call(
        body, name=name, grid=(T // tt,),
        in_specs=[pl.BlockSpec((tt, MLA_HEADS * LANES), lambda i: (i, 0)), tab, tab], out_specs=tab,
        out_shape=jax.ShapeDtypeStruct((T, LANES), BF16), compiler_params=_cparams(("parallel",)),
    )(dk, cos, sin)


ATT_BLOCK = 512


def _attn_scale():
    return float((QK_NOPE + QK_ROPE) ** -0.5)


def _causal_mask(qi, kj, tq, tk):
    row = qi * tq + lax.broadcasted_iota(jnp.int32, (tq, tk), 0)
    col = kj * tk + lax.broadcasted_iota(jnp.int32, (tq, tk), 1)
    return col <= row


def _pcall(body, *, name, grid, in_specs, out_specs, out_shape, args, scratch=(), sem=None, ride=None):
    n_in, n_out, n_scr = len(args), len(out_shape), len(scratch)
    if ride is None:
        return pl.pallas_call(
            body, name=name, grid=grid, in_specs=list(in_specs), out_specs=list(out_specs), out_shape=list(out_shape),
            scratch_shapes=list(scratch), compiler_params=_cparams(sem or ("arbitrary",) * len(grid)))(*args)
    ex, sink = ride
    o0 = n_in + len(ex.arrs)
    s0 = o0 + n_out + len(ex.out_shapes)

    def hosted(*refs):
        parts = (refs[n_in:o0], refs[o0 + n_out:s0], refs[-2], refs[-1])
        ids = [pl.program_id(i) for i in range(len(grid))]
        pl.when(functools.reduce(jnp.logical_and, [i == 0 for i in ids]))(lambda: ex.start(*parts))
        body(*refs[:n_in], *refs[o0:o0 + n_out], *refs[s0:s0 + n_scr])
        pl.when(functools.reduce(jnp.logical_and, [i == n - 1 for i, n in zip(ids, grid)]))(lambda: ex.finish(*parts))

    outs = pl.pallas_call(
        hosted, name=name, grid=grid, in_specs=list(in_specs) + ex.in_specs, out_specs=list(out_specs) + ex.out_specs,
        out_shape=list(out_shape) + ex.out_shapes, scratch_shapes=list(scratch) + ex.scratch,
        compiler_params=_cparams(("arbitrary",) * len(grid)))(*args, *ex.arrs)
    sink(outs[n_out:])
    return outs[:n_out]


PAIRS = MLA_HEADS // 2


def _own_lanes(x, first):
    lane = lax.broadcasted_iota(jnp.int32, x.shape, 1)
    return jnp.where((lane < V_HEAD) if first else (lane >= V_HEAD), x, 0.0)


def _attn_fwd(q, k, kv, *, B, S, v_block0, name, ride=None):
    tq = tk = min(ATT_BLOCK, S)
    nq = S // tq
    T = B * S

    def body(q_ref, k_ref, v_ref, o_ref, lse_ref):
        qi = pl.program_id(2)
        qs = (q_ref[:, :LANES], q_ref[:, LANES:])

        def step(masked):
            def f(j, carry):
                rows = pl.ds(pl.multiple_of(j * tk, tk), tk)
                vb = v_ref[rows, :]
                out = []
                for h in range(2):
                    m, l, acc = carry[h]
                    s = _dot_nt(qs[h], k_ref[rows, h * LANES:(h + 1) * LANES])
                    if masked:
                        s = jnp.where(_causal_mask(qi, j, tq, tk), s, -jnp.inf)
                    m_new = jnp.maximum(m, jnp.max(s, axis=-1, keepdims=True))
                    alpha = jnp.exp(m - m_new)
                    p = jnp.exp(s - m_new)
                    out.append((m_new, alpha * l + jnp.sum(p, axis=-1, keepdims=True), alpha * acc + _dot_nn(p, vb)))
                return tuple(out)
            return f

        one = (jnp.full((tq, 1), -1e30, F32), jnp.zeros((tq, 1), F32), jnp.zeros((tq, LANES), F32))
        (ma, la, acca), (mb, lb, accb) = step(True)(qi, lax.fori_loop(0, qi, step(False), (one, one)))
        o_ref[...] = _own_lanes(acca / la, True) + _own_lanes(accb / lb, False)
        lse_ref[0, 0] = ma + jnp.log(la)
        lse_ref[0, 1] = mb + jnp.log(lb)

    return _pcall(
        body, name=name, grid=(B, PAIRS, nq),
        in_specs=[pl.BlockSpec((tq, 2 * LANES), lambda b, g, i: (b * nq + i, g)),
                  pl.BlockSpec((S, 2 * LANES), lambda b, g, i: (b, g)),
                  pl.BlockSpec((S, LANES), lambda b, g, i: (b, v_block0 + g))],
        out_specs=[pl.BlockSpec((tq, LANES), lambda b, g, i: (b * nq + i, g)),
                   pl.BlockSpec((1, 2, tq, 1), lambda b, g, i: (b, g, i, 0))],
        out_shape=[jax.ShapeDtypeStruct((T, PAIRS * LANES), F32), jax.ShapeDtypeStruct((B, MLA_HEADS, S, 1), F32)],
        args=(q, k, kv), ride=ride)


def _attn_dq(q, k, kv, o, lse, do, *, B, S, v_block0, name, ride=None):
    tq = tk = min(ATT_BLOCK, S)
    nq = S // tq
    T = B * S
    scale = _attn_scale()

    def body(q_ref, k_ref, v_ref, o_ref, lse_ref, do_ref, dq_ref, delta_ref):
        qi = pl.program_id(2)
        qs = (q_ref[:, :LANES], q_ref[:, LANES:])
        dos = (_own_lanes(do_ref[...], True), _own_lanes(do_ref[...], False))
        deltas = tuple(jnp.sum(d * o_ref[...], axis=-1, keepdims=True) for d in dos)
        lses = (lse_ref[0, 0], lse_ref[0, 1])

        def step(masked):
            def f(j, carry):
                rows = pl.ds(pl.multiple_of(j * tk, tk), tk)
                vb = v_ref[rows, :]
                out = []
                for h in range(2):
                    kb = k_ref[rows, h * LANES:(h + 1) * LANES]
                    p = jnp.exp(_dot_nt(qs[h], kb) - lses[h])
                    if masked:
                        p = jnp.where(_causal_mask(qi, j, tq, tk), p, 0.0)
                    ds = p * (_dot_nt(dos[h], vb) - deltas[h])
                    out.append(carry[h] + _dot_nn(ds, kb))
                return tuple(out)
            return f

        zero = jnp.zeros((tq, LANES), F32)
        dqa, dqb = step(True)(qi, lax.fori_loop(0, qi, step(False), (zero, zero)))
        dq_ref[:, :LANES] = dqa * scale
        dq_ref[:, LANES:] = dqb * scale
        delta_ref[0, 0] = deltas[0]
        delta_ref[0, 1] = deltas[1]

    qrow = lambda w: pl.BlockSpec((tq, w), lambda b, g, i: (b * nq + i, g))
    stat = pl.BlockSpec((1, 2, tq, 1), lambda b, g, i: (b, g, i, 0))
    return _pcall(
        body, name=name, grid=(B, PAIRS, nq),
        in_specs=[qrow(2 * LANES), pl.BlockSpec((S, 2 * LANES), lambda b, g, i: (b, g)),
                  pl.BlockSpec((S, LANES), lambda b, g, i: (b, v_block0 + g)), qrow(LANES), stat, qrow(LANES)],
        out_specs=[qrow(2 * LANES), stat],
        out_shape=[jax.ShapeDtypeStruct((T, MLA_HEADS * LANES), F32), jax.ShapeDtypeStruct((B, MLA_HEADS, S, 1), F32)],
        args=(q, k, kv, o, lse, do), sem=("parallel", "parallel", "parallel"), ride=ride)


def _attn_dkv(q, k, kv, lse_rows, delta_rows, do, *, B, S, v_block0, name, ride=None):
    tq = tk = min(ATT_BLOCK, S)
    nq = S // tq
    T = B * S

    def body(q_ref, k_ref, v_ref, lse_ref, delta_ref, do_ref, dk_ref, dv_ref):
        kj = pl.program_id(2)
        ks = (k_ref[:, :LANES], k_ref[:, LANES:])
        vb = v_ref[...]

        def step(masked):
            def f(i, carry):
                rows = pl.ds(pl.multiple_of(i * tq, tq), tq)
                do_b = do_ref[rows, :]
                dks, dv = list(carry[:2]), carry[2]
                for h in range(2):
                    qb = q_ref[rows, h * LANES:(h + 1) * LANES]
                    doh = _own_lanes(do_b, h == 0)
                    pt = jnp.exp(_dot_nt(ks[h], qb) - lse_ref[0, h, pl.ds(i, 1), :])
                    if masked:
                        krow = kj * tk + lax.broadcasted_iota(jnp.int32, (tk, tq), 0)
                        qcol = i * tq + lax.broadcasted_iota(jnp.int32, (tk, tq), 1)
                        pt = jnp.where(krow <= qcol, pt, 0.0)
                    dst = pt * (_dot_nt(vb, doh) - delta_ref[0, h, pl.ds(i, 1), :])
                    dks[h] = dks[h] + _dot_nn(dst, qb)
                    dv = dv + _dot_nn(pt, doh)
                return dks[0], dks[1], dv
            return f

        zero = jnp.zeros((tk, LANES), F32)
        dka, dkb, dv = lax.fori_loop(kj + 1, nq, step(False), step(True)(kj, (zero, zero, zero)))
        dk_ref[:, :LANES] = dka
        dk_ref[:, LANES:] = dkb
        dv_ref[...] = dv

    krow = lambda w, c0: pl.BlockSpec((tk, w), lambda b, g, j: (b * nq + j, c0 + g))
    seq = lambda w: pl.BlockSpec((S, w), lambda b, g, j: (b, g))
    stat = pl.BlockSpec((1, 2, nq, tq), lambda b, g, j: (b, g, 0, 0))
    return _pcall(
        body, name=name, grid=(B, PAIRS, nq),
        in_specs=[seq(2 * LANES), krow(2 * LANES, 0), krow(LANES, v_block0), stat, stat, seq(LANES)],
        out_specs=[krow(2 * LANES, 0), krow(LANES, 0)],
        out_shape=[jax.ShapeDtypeStruct((T, MLA_HEADS * LANES), F32), jax.ShapeDtypeStruct((T, PAIRS * LANES), F32)],
        args=(q, k, kv, lse_rows, delta_rows, do), ride=ride)


def _lru_gates(xl, halo, cw_ref, cb_ref, wa_ref, ba_ref, wx_ref, bx_ref, lam_ref):
    xc = cb_ref[...] + cw_ref[3:4, :] * xl
    for kk in range(LRU_CONV - 1):
        xc = xc + cw_ref[kk:kk + 1, :] * _shift_rows(xl, LRU_CONV - 1 - kk, halo)
    r = _sigmoid(_dot_nn(xc, wa_ref[...]) + ba_ref[...])
    i = _sigmoid(_dot_nn(xc, wx_ref[...]) + bx_ref[...])
    lam = lam_ref[...]
    sp = jnp.maximum(-lam, 0.0) + jnp.log(1.0 + jnp.exp(-jnp.abs(lam)))
    a = jnp.exp(-LRU_C * r * sp)
    mult = jnp.sqrt(1.0 - a * a)
    return xc, r, i, sp, a, mult


def _lru_specs(tt, nt, S):
    def make(rev):
        tmap = (lambda t: nt - 1 - t) if rev else (lambda t: t)
        tile = lambda cb: pl.BlockSpec((tt, LRU_WIDTH), lambda b, t: (b * nt + tmap(t), cb))
        prev8 = lambda cb: pl.BlockSpec(
            (8, LRU_WIDTH), lambda b, t: (jnp.maximum((b * nt + tmap(t)) * (tt // 8) - 1, 0), cb))
        return tile, prev8, tmap
    return make


def _lru_fwd(z, cw, cb, wa, ba, wx, bx, lam, *, S, name, ride=None):
    T = z.shape[0]
    tt = min(ROW_TILE, S)
    nt = S // tt
    tile, prev8, _ = _lru_specs(tt, nt, S)(False)
    vec = lambda r: pl.BlockSpec((r, LRU_WIDTH), lambda b, t: (0, 0))
    mat = pl.BlockSpec((LRU_WIDTH, LRU_WIDTH), lambda b, t: (0, 0))

    def body(xl_ref, halo_ref, gate_ref, cw_ref, cb_ref, wa_ref, ba_ref, wx_ref, bx_ref, lam_ref,
             y_ref, h_ref, carry_ref):
        t = pl.program_id(1)
        first = t == 0
        halo = jnp.where(first, 0.0, halo_ref[...])
        xl_t = xl_ref[...]
        xc, r, i, sp, a, mult = _lru_gates(xl_t, halo, cw_ref, cb_ref, wa_ref, ba_ref, wx_ref, bx_ref, lam_ref)
        bv = mult * (i * xc)
        ones = jnp.ones((8, LRU_WIDTH), F32)
        zeros = jnp.zeros((8, LRU_WIDTH), F32)
        row = lax.broadcasted_iota(jnp.int32, (tt, LRU_WIDTH), 0)
        A = a
        d = 1
        while d < tt:
            if d < 8:
                a_sh = _shift_rows(A, d, ones)
                b_sh = _shift_rows(bv, d, zeros)
            else:
                a_sh = jnp.where(row < d, 1.0, pltpu.roll(A, d, 0))
                b_sh = jnp.where(row < d, 0.0, pltpu.roll(bv, d, 0))
            bv = A * b_sh + bv
            A = A * a_sh
            d *= 2
        h0 = jnp.where(first, 0.0, carry_ref[0:1, :])
        h = A * h0 + bv
        carry_ref[...] = jnp.broadcast_to(h[tt - 1:tt, :], (8, LRU_WIDTH))
        h_ref[...] = h
        y_ref[...] = (h * _gelu(gate_ref[...])).astype(BF16)

    return _pcall(
        body, name=name, grid=(T // S, nt),
        in_specs=[tile(0), prev8(0), tile(1), vec(LRU_CONV), vec(1), mat, vec(1), mat, vec(1), vec(1)],
        out_specs=[tile(0), tile(0)],
        out_shape=[jax.ShapeDtypeStruct((T, LRU_WIDTH), BF16), jax.ShapeDtypeStruct((T, LRU_WIDTH), F32)],
        args=(z, z, z, cw, cb, wa, ba, wx, bx, lam), scratch=[pltpu.VMEM((8, LRU_WIDTH), F32)], ride=ride)


def _lru_bwd(z, h, dy, cw, cb, wa, ba, wx, bx, lam, *, S, name):
    T = z.shape[0]
    tt = min(ROW_TILE, S)
    nt = S // tt
    tile, prev8, tmap = _lru_specs(tt, nt, S)(True)
    vec = lambda r: pl.BlockSpec((r, LRU_WIDTH), lambda b, t: (0, 0))
    mat = pl.BlockSpec((LRU_WIDTH, LRU_WIDTH), lambda b, t: (0, 0))

    def body(xl_ref, halo_ref, gate_ref, h_ref, hprev_ref, dy_ref, cw_ref, cb_ref, wa_ref, ba_ref, wx_ref,
             bx_ref, lam_ref, dxl_ref, dgate_ref, dcw_ref, dcb_ref, dwa_ref, dba_ref, dwx_ref, dbx_ref,
             dlam_ref, lamc_ref, ac_ref, dxc_ref):
        b = pl.program_id(0)
        t = pl.program_id(1)
        tr = nt - 1 - t
        seq_first = tr == 0
        seq_last = t == 0
        halo = jnp.where(seq_first, 0.0, halo_ref[...])
        xl_t = xl_ref[...]
        xc, r, i, sp, a, mult = _lru_gates(xl_t, halo, cw_ref, cb_ref, wa_ref, ba_ref, wx_ref, bx_ref, lam_ref)
        hh = h_ref[...]
        dyf = dy_ref[...].astype(F32)
        gl, dgl = _gelu_and_grad(gate_ref[...])
        dgate_ref[...] = (dyf * hh * dgl).astype(BF16)
        dh = dyf * gl

        a_first_later = jnp.where(seq_last, 0.0, ac_ref[...])
        lam_later = jnp.where(seq_last, 0.0, lamc_ref[...])
        row = lax.broadcasted_iota(jnp.int32, (tt, LRU_WIDTH), 0)
        A = _shift_rows_up(a, 1, a_first_later)
        lm = dh
        ones = jnp.ones((8, LRU_WIDTH), F32)
        zeros = jnp.zeros((8, LRU_WIDTH), F32)
        d = 1
        while d < tt:
            if d < 8:
                a_sh = _shift_rows_up(A, d, ones)
                l_sh = _shift_rows_up(lm, d, zeros)
            else:
                a_sh = jnp.where(row >= tt - d, 1.0, pltpu.roll(A, tt - d, 0))
                l_sh = jnp.where(row >= tt - d, 0.0, pltpu.roll(lm, tt - d, 0))
            lm = lm + A * l_sh
            A = A * a_sh
            d *= 2
        lm = lm + A * lam_later[0:1, :]
        lamc_ref[...] = jnp.broadcast_to(lm[0:1, :], (8, LRU_WIDTH))
        ac_ref[...] = jnp.broadcast_to(a[0:1, :], (8, LRU_WIDTH))

        hprev_halo = jnp.where(seq_first, 0.0, hprev_ref[...])
        h_prev = _shift_rows(hh, 1, hprev_halo)
        da = lm * h_prev
        ixc = i * xc
        dmult = lm * ixc
        di = lm * mult * xc
        dxc = lm * mult * i
        da = da - dmult * a / mult
        dlog = da * a
        dr = dlog * (-LRU_C) * sp
        dsp_part = jnp.sum(dlog * (-LRU_C) * r, axis=0, keepdims=True)
        dpa = dr * r * (1.0 - r)
        dpx = di * i * (1.0 - i)
        dxc = dxc + _dot_nt(dpa, wa_ref[...]) + _dot_nt(dpx, wx_ref[...])
        dwa_part = _dot_tn(xc, dpa)
        dwx_part = _dot_tn(xc, dpx)

        later = jnp.where(seq_last, 0.0, dxc_ref[...])
        dxl = cw_ref[3:4, :] * dxc
        for kk in range(LRU_CONV - 1):
            dxl = dxl + cw_ref[kk:kk + 1, :] * _shift_rows_up(dxc, LRU_CONV - 1 - kk, later)
        dxl_ref[...] = dxl.astype(BF16)
        dxc_ref[...] = dxc[0:8, :]
        dcw_rows = [jnp.sum(dxc * _shift_rows(xl_t, LRU_CONV - 1 - kk, halo), axis=0, keepdims=True)
                    for kk in range(LRU_CONV - 1)]
        dcw_rows.append(jnp.sum(dxc * xl_t, axis=0, keepdims=True))
        dcw_part = jnp.concatenate(dcw_rows + [jnp.zeros((8 - LRU_CONV, LRU_WIDTH), F32)], axis=0)
        lamv = lam_ref[...]
        dlam_part = dsp_part * (-_sigmoid(-lamv))
        parts = ((dcw_ref, dcw_part), (dcb_ref, jnp.sum(dxc, axis=0, keepdims=True)),
                 (dwa_ref, dwa_part), (dba_ref, jnp.sum(dpa, axis=0, keepdims=True)),
                 (dwx_ref, dwx_part), (dbx_ref, jnp.sum(dpx, axis=0, keepdims=True)),
                 (dlam_ref, dlam_part))
        start = jnp.logical_and(b == 0, t == 0)

        @pl.when(start)
        def _():
            for ref, val in parts:
                ref[...] = val

        @pl.when(jnp.logical_not(start))
        def _():
            for ref, val in parts:
                ref[...] += val

    acc = lambda r: pl.BlockSpec((r, LRU_WIDTH), lambda b, t: (0, 0))
    return pl.pallas_call(
        body, name=name, grid=(T // S, nt),
        in_specs=[tile(0), prev8(0), tile(1), tile(0), prev8(0), tile(0),
                  vec(LRU_CONV), vec(1), mat, vec(1), mat, vec(1), vec(1)],
        out_specs=[tile(0), tile(0), acc(8), acc(1), mat, acc(1), mat, acc(1), acc(1)],
        out_shape=[jax.ShapeDtypeStruct((T, LRU_WIDTH), BF16), jax.ShapeDtypeStruct((T, LRU_WIDTH), BF16),
                   jax.ShapeDtypeStruct((8, LRU_WIDTH), F32), jax.ShapeDtypeStruct((1, LRU_WIDTH), F32),
                   jax.ShapeDtypeStruct((LRU_WIDTH, LRU_WIDTH), F32), jax.ShapeDtypeStruct((1, LRU_WIDTH), F32),
                   jax.ShapeDtypeStruct((LRU_WIDTH, LRU_WIDTH), F32), jax.ShapeDtypeStruct((1, LRU_WIDTH), F32),
                   jax.ShapeDtypeStruct((1, LRU_WIDTH), F32)],
        scratch_shapes=[pltpu.VMEM((8, LRU_WIDTH), F32), pltpu.VMEM((8, LRU_WIDTH), F32),
                        pltpu.VMEM((8, LRU_WIDTH), F32)],
        compiler_params=_cparams(("arbitrary", "arbitrary")),
    )(z, z, z, h, h, dy, cw, cb, wa, ba, wx, bx, lam)


FFN_CT = 1408
FFN_TILE = 512


def _ffn_conv(g, halo, cw_ref, cb_ref):
    gc = cb_ref[...] + cw_ref[2:3, :] * g
    for kk in range(FFN_CONV - 1):
        gc = gc + cw_ref[kk:kk + 1, :] * _shift_rows(g, FFN_CONV - 1 - kk, halo)
    return gc


def _ffn_act_down(g, u, cw, cb, w_down, res, *, S, name, ride=None):
    T, F = g.shape
    D = w_down.shape[1]
    tt = min(FFN_TILE, S)
    nt = S // tt
    tc = _tile(F, FFN_CT)
    nj = F // tc
    assert nj > 1

    def body(g_ref, halo_ref, u_ref, cw_ref, cb_ref, w_ref, r_ref, o_ref, act_ref, acc_ref):
        j = pl.program_id(1)
        first = (pl.program_id(0) % nt) == 0
        halo = jnp.where(first, 0.0, halo_ref[...])
        gc = _ffn_conv(g_ref[...], halo, cw_ref, cb_ref)
        act = (_gelu(gc) * u_ref[...]).astype(BF16)
        act_ref[...] = act
        part = _dot_nn(act, w_ref[...])

        @pl.when(j == 0)
        def _():
            acc_ref[...] = part + r_ref[...]

        @pl.when(jnp.logical_and(j > 0, j < nj - 1))
        def _():
            acc_ref[...] += part

        @pl.when(j == nj - 1)
        def _():
            o_ref[...] = acc_ref[...] + part

    tile = pl.BlockSpec((tt, tc), lambda i, j: (i, j))
    prev8 = pl.BlockSpec((8, tc), lambda i, j: (jnp.maximum(i * (tt // 8) - 1, 0), j))
    rows = pl.BlockSpec((tt, D), lambda i, j: (i, 0))
    return _pcall(
        body, name=name, grid=(T // tt, nj),
        in_specs=[tile, prev8, tile, pl.BlockSpec((FFN_CONV, tc), lambda i, j: (0, j)),
                  pl.BlockSpec((1, tc), lambda i, j: (0, j)), pl.BlockSpec((tc, D), lambda i, j: (j, 0)), rows],
        out_specs=[rows, tile], out_shape=[jax.ShapeDtypeStruct((T, D), F32), jax.ShapeDtypeStruct((T, F), BF16)],
        args=(g, g, u, cw, cb, w_down, res), scratch=[pltpu.VMEM((tt, D), F32)], sem=("parallel", "arbitrary"), ride=ride)


def _ffn_act_bwd(g, u, dh, w_down, cw, cb, *, S, name, ride=None):
    T, F = g.shape
    D = w_down.shape[1]
    tt = min(FFN_TILE, S)
    nt = S // tt
    ntt = T // tt
    tc = _tile(F, FFN_CT)

    def body(g_ref, halo_ref, u_ref, dh_ref, w_ref, cw_ref, cb_ref, dg_ref, du_ref, dcw_ref, dcb_ref, later_ref):
        step = pl.program_id(1)
        ti = (ntt - 1 - step) % nt
        halo = jnp.where(ti == 0, 0.0, halo_ref[...])
        gt = g_ref[...]
        gc = _ffn_conv(gt, halo, cw_ref, cb_ref)
        gl, dgl = _gelu_and_grad(gc)
        da = _dot_nt(dh_ref[...], w_ref[...])
        du_ref[...] = (da * gl).astype(BF16)
        dgc = da * u_ref[...] * dgl
        later = jnp.where(ti == nt - 1, 0.0, later_ref[...])
        dg = cw_ref[2:3, :] * dgc
        for kk in range(FFN_CONV - 1):
            dg = dg + cw_ref[kk:kk + 1, :] * _shift_rows_up(dgc, FFN_CONV - 1 - kk, later)
        dg_ref[...] = dg.astype(BF16)
        later_ref[...] = dgc[0:8, :]
        rows = [jnp.sum(dgc * _shift_rows(gt, FFN_CONV - 1 - kk, halo), axis=0, keepdims=True)
                for kk in range(FFN_CONV - 1)]
        rows.append(jnp.sum(dgc * gt, axis=0, keepdims=True))
        dcw_part = jnp.concatenate(rows + [jnp.zeros((8 - FFN_CONV, tc), F32)], axis=0)
        dcb_part = jnp.sum(dgc, axis=0, keepdims=True)

        @pl.when(step == 0)
        def _():
            dcw_ref[...] = dcw_part
            dcb_ref[...] = dcb_part

        @pl.when(step > 0)
        def _():
            dcw_ref[...] += dcw_part
            dcb_ref[...] += dcb_part

    tile = pl.BlockSpec((tt, tc), lambda j, s: (ntt - 1 - s, j))
    prev8 = pl.BlockSpec((8, tc), lambda j, s: (jnp.maximum((ntt - 1 - s) * (tt // 8) - 1, 0), j))
    return _pcall(
        body, name=name, grid=(F // tc, ntt),
        in_specs=[tile, prev8, tile, pl.BlockSpec((tt, D), lambda j, s: (ntt - 1 - s, 0)),
                  pl.BlockSpec((tc, D), lambda j, s: (j, 0)), pl.BlockSpec((FFN_CONV, tc), lambda j, s: (0, j)),
                  pl.BlockSpec((1, tc), lambda j, s: (0, j))],
        out_specs=[tile, tile, pl.BlockSpec((8, tc), lambda j, s: (0, j)), pl.BlockSpec((1, tc), lambda j, s: (0, j))],
        out_shape=[jax.ShapeDtypeStruct((T, F), BF16), jax.ShapeDtypeStruct((T, F), BF16),
                   jax.ShapeDtypeStruct((8, F), F32), jax.ShapeDtypeStruct((1, F), F32)],
        args=(g, g, u, dh, w_down, cw, cb), scratch=[pltpu.VMEM((8, tc), F32)], ride=ride)


def _sgu_norm(zv, g_ref, b_ref):
    v = _gelu(zv)
    mu = jnp.mean(v, axis=-1, keepdims=True)
    xc = v - mu
    rstd = lax.rsqrt(jnp.mean(xc * xc, axis=-1, keepdims=True) + NORM_EPS)
    xhat = xc * rstd
    return xhat, rstd, xhat * g_ref[...] + b_ref[...]


def _sgu_fwd(zc, ln_g, ln_b, wm, bmap, *, name):
    T = zc.shape[0]
    W = SGU_WIDTH
    tt = ROW_TILE
    nch = tt // CHUNK

    def body(z_ref, g_ref, b_ref, wm_ref, bm_ref, p_ref):
        u = _gelu(z_ref[:, :W])
        _, _, vn = _sgu_norm(z_ref[:, W:], g_ref, b_ref)
        vn = vn.astype(BF16)
        for n in range(nch):
            rows = slice(n * CHUNK, (n + 1) * CHUNK)
            for gi in range(SGU_GROUPS):
                cols = slice(gi * LANES, (gi + 1) * LANES)
                s = _dot_nn(wm_ref[gi], vn[rows, cols]) + bm_ref[:, cols]
                p_ref[rows, cols] = (u[rows, cols] * s).astype(BF16)

    const2 = lambda r, c: pl.BlockSpec((r, c), lambda i: (0, 0))
    return pl.pallas_call(
        body, name=name, grid=(T // tt,),
        in_specs=[pl.BlockSpec((tt, 2 * W), lambda i: (i, 0)), const2(1, W), const2(1, W),
                  pl.BlockSpec((SGU_GROUPS, CHUNK, CHUNK), lambda i: (0, 0, 0)), const2(CHUNK, W)],
        out_specs=pl.BlockSpec((tt, W), lambda i: (i, 0)),
        out_shape=jax.ShapeDtypeStruct((T, W), BF16),
        compiler_params=_cparams(("parallel",)),
    )(zc, ln_g, ln_b, wm, bmap)


def _sgu_bwd(zc, dp, ln_g, ln_b, wm, bmap, *, name, ride=None):
    T = zc.shape[0]
    W = SGU_WIDTH
    tt = ROW_TILE
    nch = tt // CHUNK
    nsteps = T // tt

    def body(z_ref, dp_ref, g_ref, b_ref, wm_ref, bm_ref, dz_ref, dg_ref, db_ref, dwm_ref, dbm_ref,
             s_scr, dvn_scr):
        step = pl.program_id(0)
        zu = z_ref[:, :W]
        zv = z_ref[:, W:]
        u, dgu = _gelu_and_grad(zu)
        xhat, rstd, vn = _sgu_norm(zv, g_ref, b_ref)
        vnb = vn.astype(BF16)
        dpf = dp_ref[...].astype(F32)
        ds = dpf * u

        @pl.when(step == 0)
        def _():
            dwm_ref[...] = jnp.zeros_like(dwm_ref)
            dbm_ref[...] = jnp.zeros_like(dbm_ref)

        for n in range(nch):
            rows = slice(n * CHUNK, (n + 1) * CHUNK)
            for gi in range(SGU_GROUPS):
                cols = slice(gi * LANES, (gi + 1) * LANES)
                s_scr[rows, cols] = _dot_nn(wm_ref[gi], vnb[rows, cols]) + bm_ref[:, cols]
                dsb = ds[rows, cols]
                dvn_scr[rows, cols] = _dot_tn(wm_ref[gi], dsb)
                dwm_ref[gi] += _dot_nt(dsb, vnb[rows, cols])
                dbm_ref[:, cols] += dsb
        dz_ref[:, :W] = (dpf * s_scr[...] * dgu).astype(BF16)
        dvn = dvn_scr[...]
        dxhat = dvn * g_ref[...]
        dv = rstd * (dxhat - jnp.mean(dxhat, axis=-1, keepdims=True)
                     - xhat * jnp.mean(dxhat * xhat, axis=-1, keepdims=True))
        _, dgv = _gelu_and_grad(zv)
        dz_ref[:, W:] = (dv * dgv).astype(BF16)
        dg_part = jnp.sum(dvn * xhat, axis=0, keepdims=True)
        db_part = jnp.sum(dvn, axis=0, keepdims=True)

        @pl.when(step == 0)
        def _():
            dg_ref[...] = dg_part
            db_ref[...] = db_part

        @pl.when(step > 0)
        def _():
            dg_ref[...] += dg_part
            db_ref[...] += db_part

        @pl.when(step == nsteps - 1)
        def _():
            for gi in range(SGU_GROUPS):
                cols = slice(gi * LANES, (gi + 1) * LANES)
                tot = jnp.sum(dbm_ref[:, cols], axis=1, keepdims=True)
                dbm_ref[:, cols] = jnp.broadcast_to(tot, (CHUNK, LANES))

    const2 = lambda r, c: pl.BlockSpec((r, c), lambda i: (0, 0))
    wspec = pl.BlockSpec((SGU_GROUPS, CHUNK, CHUNK), lambda i: (0, 0, 0))
    return _pcall(
        body, name=name, grid=(nsteps,),
        in_specs=[pl.BlockSpec((tt, 2 * W), lambda i: (i, 0)), pl.BlockSpec((tt, W), lambda i: (i, 0)),
                  const2(1, W), const2(1, W), wspec, const2(CHUNK, W)],
        out_specs=[pl.BlockSpec((tt, 2 * W), lambda i: (i, 0)), const2(1, W), const2(1, W), wspec, const2(CHUNK, W)],
        out_shape=[jax.ShapeDtypeStruct((T, 2 * W), BF16), jax.ShapeDtypeStruct((1, W), F32),
                   jax.ShapeDtypeStruct((1, W), F32), jax.ShapeDtypeStruct((SGU_GROUPS, CHUNK, CHUNK), F32),
                   jax.ShapeDtypeStruct((CHUNK, W), F32)],
        args=(zc, dp, ln_g, ln_b, wm, bmap), scratch=[pltpu.VMEM((tt, W), F32), pltpu.VMEM((tt, W), F32)], ride=ride)


def _rope_tables(positions):
    half = QK_ROPE // 2
    inv_freq = jnp.exp(-math.log(ROPE_BASE) * jnp.arange(half, dtype=F32) / half)
    ang = positions.reshape(-1).astype(F32)[:, None] * inv_freq
    cos = jnp.cos(ang)
    sin = jnp.sin(ang)
    n = ang.shape[0]
    tail = LANES - QK_NOPE - QK_ROPE
    cos_t = jnp.concatenate([jnp.ones((n, QK_NOPE), F32), cos, cos, jnp.ones((n, tail), F32)], axis=1)
    sin_t = jnp.concatenate([jnp.zeros((n, QK_NOPE), F32), -sin, sin, jnp.zeros((n, tail), F32)], axis=1)
    return cos_t, sin_t


SGU_GROUP_DIM = SGU_WIDTH // SGU_GROUPS
_O1, _O2, _O3, _O4 = Q_LORA, Q_LORA + KV_LORA, Q_LORA + KV_LORA + QK_ROPE, Q_LORA + KV_LORA + QK_ROPE + LRU_WIDTH
_A0, _A1, _A2 = 2 * LRU_WIDTH, 2 * LRU_WIDTH + Q_LORA, 2 * LRU_WIDTH + Q_LORA + KV_LORA
_A3 = _A2 + QK_NOPE
Z_Q_BLOCK, Z_KV_BLOCK, Z_KPE_BLOCK = _A0 // Q_LORA, _A1 // KV_LORA, _A2 // LANES


def _perm_w_in(w_in):
    zeros = lambda n: jnp.zeros((w_in.shape[0], n), w_in.dtype)
    return jnp.concatenate([w_in[:, _O3:_O4], w_in[:, _O4:], w_in[:, :_O1], w_in[:, _O1:_O2], zeros(QK_NOPE),
                            w_in[:, _O2:_O3], zeros(LANES - QK_NOPE - QK_ROPE)], axis=1)


def _unperm_w_in(w):
    return jnp.concatenate([w[:, _A0:_A1], w[:, _A1:_A2], w[:, _A3:_A3 + QK_ROPE], w[:, :LRU_WIDTH],
                            w[:, LRU_WIDTH:_A0]], axis=1)


def _head_blocks(w, d):
    r = w.shape[0]
    return jnp.pad(w.reshape(r, MLA_HEADS, d), ((0, 0), (0, 0), (0, LANES - d))).reshape(r, MLA_HEADS * LANES)


def _from_head_blocks(w, d):
    r = w.shape[0]
    return w.reshape(r, MLA_HEADS, LANES)[:, :, :d].reshape(r, MLA_HEADS * d)


def _split_kv(w_kv):
    r = w_kv.shape[0]
    w3 = w_kv.reshape(r, MLA_HEADS, QK_NOPE + V_HEAD)
    return _head_blocks(w3[:, :, :QK_NOPE].reshape(r, -1), QK_NOPE), w3[:, :, QK_NOPE:].reshape(r, -1)


def _join_kv(w_k, w_v):
    r = w_k.shape[0]
    return jnp.concatenate([_from_head_blocks(w_k, QK_NOPE).reshape(r, MLA_HEADS, QK_NOPE),
                            w_v.reshape(r, MLA_HEADS, V_HEAD)], axis=2).reshape(r, -1)


def _prep_small(w):
    p = {n: w[n] for n in w if n not in BIG}
    eye = jnp.eye(LRU_HEADS, dtype=F32)
    dense = lambda wg: (wg[:, :, None, :] * eye[:, None, :, None]).reshape(LRU_WIDTH, LRU_WIDTH).astype(BF16)
    p["wa_d"] = dense(w["ab_w_rg_a"][0])
    p["wx_d"] = dense(w["ab_w_rg_x"][0])
    causal = jnp.tril(jnp.ones((CHUNK, CHUNK), F32))
    p["wm"] = (w["c_w_s"][0] * causal).astype(BF16)
    p["bmap"] = jnp.repeat(w["c_b_s"][0].T, SGU_GROUP_DIM, axis=1)
    return p


def _prep_big(ab_w_in, ab_w_q_b, ab_w_kv_b):
    return {"w_in_p": _perm_w_in(ab_w_in).astype(BF16),
            "w_q_p": _head_blocks(ab_w_q_b, QK_NOPE + QK_ROPE).astype(BF16),
            "w_kv_p": jnp.concatenate(_split_kv(ab_w_kv_b), axis=1).astype(BF16)}


def _ffn_fwd(h, l, p, S, rides):
    hn = _rms_fwd(h, p["ffn_norm"][l], name=f"ffn{l}_norm")
    g = _mm(hn, p["ffn_gate_t"][l], tb=True, name=f"ffn{l}_gate", ride=rides.get(f"ffn{l}_gate"))
    u = _mm(hn, p["ffn_up_t"][l], tb=True, name=f"ffn{l}_up", ride=rides.get(f"ffn{l}_up"))
    out, act = _ffn_act_down(g, u, p["ffn_conv_w"][l], p["ffn_conv_b"][l][None], p["ffn_down"][l], h, S=S,
                             name=f"ffn{l}_down", ride=rides.get(f"ffn{l}_down"))
    return out, (hn, g, u, act)


def _ffn_bwd(dh, h_in, l, p, saved, S, rides, grads_ready, also_ready=None):
    hn, g, u, act = saved
    dw_down = _mm(act, dh, ta=True, out_dtype=BF16, name=f"ffn{l}_dwdown")
    dg, du, dcw, dcb = _ffn_act_bwd(g, u, dh, p["ffn_down"][l], p["ffn_conv_w"][l], p["ffn_conv_b"][l][None], S=S,
                                    name=f"ffn{l}_dactbwd", ride=rides.get(f"ffn{l}_dactbwd"))
    dhn = _mm(dg, p["ffn_gate_t"][l], name=f"ffn{l}_dhn_g")
    dhn = _mm(du, p["ffn_up_t"][l], res=dhn, name=f"ffn{l}_dhn_u")
    dw_gate_t = _mm(dg, hn, ta=True, out_dtype=BF16, name=f"ffn{l}_dwgate")
    dw_up_t = _mm(du, hn, ta=True, out_dtype=BF16, name=f"ffn{l}_dwup")
    grads_ready(l, {**(also_ready or {}), "ffn_gate_t": dw_gate_t, "ffn_up_t": dw_up_t, "ffn_down": dw_down})
    dh_in, dnorm = _rms_bwd(h_in, p["ffn_norm"][l], dhn, res=dh, name=f"ffn{l}_dnorm", ride=rides.get(f"ffn{l}_dnorm"))
    grads = dict(ffn_norm=dnorm[0], ffn_gate_t=dw_gate_t, ffn_up_t=dw_up_t, ffn_conv_w=dcw[:FFN_CONV],
                 ffn_conv_b=dcb[0], ffn_down=dw_down)
    return dh_in, grads


def _local_step(x, positions, target, p, rides=None, grads_ready=None):
    rides = {} if rides is None else rides
    grads_ready = grads_ready or (lambda layer, ready: None)
    B, S, D = x.shape
    T = B * S
    H = MLA_HEADS
    xf = x.reshape(T, D)
    tgt = target.reshape(T, D)
    cos, sin = _rope_tables(positions)

    hn0 = _rms_fwd(xf, p["ab_norm"][0], name="ab_norm", ride=rides.get("ab_norm"))
    z = _mm(hn0, p["w_in_p"], name="ab_in")
    cqn = _rms_fwd(z, p["ab_q_norm"][0], cb=Z_Q_BLOCK, name="q_norm")
    ckvn = _rms_fwd(z, p["ab_kv_norm"][0], cb=Z_KV_BLOCK, name="kv_norm")
    q = _mm(cqn, p["w_q_p"], name="q_up")
    kv = _mm(ckvn, p["w_kv_p"], out_dtype=BF16, name="kv_up")
    qs = _rope_q(q, cos, sin, name="q_rope")
    kk = _key_blocks(kv, z, cos, sin, kpe_block=Z_KPE_BLOCK, name="k_rope")
    att = dict(B=B, S=S, v_block0=H)
    o, lse = _attn_fwd(qs, kk, kv, name="attn_fwd", ride=rides.get("attn_fwd"), **att)
    lru_par = (p["ab_conv_w"][0], p["ab_conv_b"], p["wa_d"], p["ab_b_rg_a"], p["wx_d"], p["ab_b_rg_x"], p["ab_lambda"])
    y_lru, hs = _lru_fwd(z, *lru_par, S=S, name="lru_fwd", ride=rides.get("lru_fwd"))
    n_att = H * V_HEAD
    w_out_a, w_out_b = p["ab_w_out"][:n_att], p["ab_w_out"][n_att:]
    h1 = _mm(y_lru, w_out_b, res=_mm(o, w_out_a, res=xf, name="ab_out_a"), name="ab_out_b")
    h2, ffn0 = _ffn_fwd(h1, 0, p, S, rides)

    hn2 = _rms_fwd(h2, p["c_norm"][0], name="c_norm")
    zc = _mm(hn2, p["c_w_in_t"], tb=True, name="c_in")
    pg = _sgu_fwd(zc, p["c_ln_g"], p["c_ln_b"], p["wm"], p["bmap"], name="sgu_fwd")
    h3 = _mm(pg, p["c_w_out"], res=h2, name="c_out")
    h4, ffn1 = _ffn_fwd(h3, 1, p, S, rides)

    loss_row, dh4, dfinal = _final_fwd_bwd(h4, p["final_norm"], tgt, name="final")

    dh3, g_ffn1 = _ffn_bwd(dh4, h3, 1, p, ffn1, S, rides, grads_ready)
    dpg = _mm(dh3, p["c_w_out"], tb=True, out_dtype=BF16, name="c_dp")
    dw_c_out = _mm(pg, dh3, ta=True, out_dtype=BF16, name="c_dwout")
    dzc, dlng, dlnb, dwm, dbm = _sgu_bwd(zc, dpg, p["c_ln_g"], p["c_ln_b"], p["wm"], p["bmap"], name="sgu_bwd",
                                         ride=rides.get("sgu_bwd"))
    dhn2 = _mm(dzc, p["c_w_in_t"], name="c_dhn")
    dw_c_in_t = _mm(dzc, hn2, ta=True, out_dtype=BF16, name="c_dwin")
    dh2, dcnorm = _rms_bwd(h2, p["c_norm"][0], dhn2, res=dh3, name="c_dnorm")
    dh1, g_ffn0 = _ffn_bwd(dh2, h1, 0, p, ffn0, S, rides, grads_ready, {"c_w_in_t": dw_c_in_t, "c_w_out": dw_c_out})

    do = _mm(dh1, w_out_a, tb=True, name="ab_do")
    dy_lru = _mm(dh1, w_out_b, tb=True, out_dtype=BF16, name="ab_dylru")
    dw_out = jnp.concatenate([_mm(o, dh1, ta=True, out_dtype=BF16, name="ab_dwout_a"),
                              _mm(y_lru, dh1, ta=True, out_dtype=BF16, name="ab_dwout_b")], axis=0)
    dq, delta = _attn_dq(qs, kk, kv, o, lse, do, name="attn_dq", ride=rides.get("attn_dq"), **att)
    nq = S // min(ATT_BLOCK, S)
    rows = lambda a: a.reshape(B, H, nq, S // nq)
    dk, dv = _attn_dkv(qs, kk, kv, rows(lse), rows(delta), do, name="attn_dkv", ride=rides.get("attn_dkv"), **att)
    dq_full = _rope_q_bwd(dq, cos, sin, name="q_rope_bwd")
    dkr = _key_rope_bwd(dk, cos, sin, name="k_rope_bwd")
    n_key = H * LANES
    w_k_p, w_v_p = p["w_kv_p"][:, :n_key], p["w_kv_p"][:, n_key:]
    dcqn = _mm(dq_full, p["w_q_p"], tb=True, name="q_dlat")
    dw_q_p = _mm(cqn, dq_full, ta=True, out_dtype=BF16, name="q_dw")
    dckvn = _mm(dv, w_v_p, tb=True, res=_mm(dk, w_k_p, tb=True, name="k_dlat"), name="v_dlat")
    dw_k_p = _mm(ckvn, dk, ta=True, out_dtype=BF16, name="k_dw")
    dw_v_p = _mm(ckvn, dv, ta=True, out_dtype=BF16, name="v_dw")
    dcq, dqnorm = _rms_bwd(z, p["ab_q_norm"][0], dcqn, cb=Z_Q_BLOCK, out_dtype=BF16, name="q_dnorm")
    dckv, dkvnorm = _rms_bwd(z, p["ab_kv_norm"][0], dckvn, cb=Z_KV_BLOCK, out_dtype=BF16, name="kv_dnorm")
    dxl, dgate, dcw, dcb, dwa, dba, dwx, dbx, dlam = _lru_bwd(z, hs, dy_lru, *lru_par, S=S, name="lru_bwd")
    dz = jnp.concatenate([dxl, dgate, dcq, dckv, dkr], axis=1)
    dhn0 = _mm(dz, p["w_in_p"], tb=True, name="ab_dhn")
    dw_in_p = _mm(hn0, dz, ta=True, out_dtype=BF16, name="ab_dwin")
    dx, dabnorm = _rms_bwd(xf, p["ab_norm"][0], dhn0, res=dh1, name="ab_dnorm")

    blocks = lambda dd: jnp.stack([dd[i * LRU_BLOCK:(i + 1) * LRU_BLOCK, i * LRU_BLOCK:(i + 1) * LRU_BLOCK]
                                   for i in range(LRU_HEADS)])
    causal = jnp.tril(jnp.ones((CHUNK, CHUNK), F32))
    grads = {
        "ab_norm": dabnorm, "w_in_p": dw_in_p, "ab_q_norm": dqnorm, "w_q_p": dw_q_p,
        "ab_kv_norm": dkvnorm, "w_k_p": dw_k_p, "w_v_p": dw_v_p, "ab_conv_w": dcw[:LRU_CONV][None], "ab_conv_b": dcb,
        "ab_w_rg_a": blocks(dwa)[None], "ab_b_rg_a": dba, "ab_w_rg_x": blocks(dwx)[None], "ab_b_rg_x": dbx,
        "ab_lambda": dlam, "ab_w_out": dw_out,
        "c_norm": dcnorm, "c_w_in_t": dw_c_in_t, "c_ln_g": dlng, "c_ln_b": dlnb,
        "c_w_s": (dwm * causal)[None], "c_b_s": dbm[:, ::SGU_GROUP_DIM].T[None], "c_w_out": dw_c_out,
        "final_norm": dfinal[0],
    }
    for name in ("ffn_norm", "ffn_conv_w", "ffn_conv_b"):
        grads[name] = jnp.stack([g_ffn0[name], g_ffn1[name]])
    for name in ("ffn_gate_t", "ffn_up_t", "ffn_down"):
        grads[name] = [g_ffn0[name], g_ffn1[name]]
    return loss_row, dx.reshape(B, S, D), grads


ANY = pl.BlockSpec(memory_space=pl.ANY)


def _place():
    x, y, c = lax.axis_index("x"), lax.axis_index("y"), lax.axis_index("c")
    chips = [(1 - x, y), (x, 1 - y), (1 - x, 1 - y)]
    return x, y, c, 2 * x + y, (x, y, 1 - c), chips


def _remote(src, dst, send_sems, recv_sems, k, to):
    return pltpu.make_async_remote_copy(src_ref=src, dst_ref=dst, send_sem=send_sems.at[k], recv_sem=recv_sems.at[k],
                                        device_id=to, device_id_type=MESH)


class _Exchange:
    def __init__(self, arrs, out_shapes, n_sems, start, finish):
        self.arrs, self.out_shapes, self.n_sems, self.start, self.finish = list(arrs), out_shapes, n_sems, start, finish

    @property
    def in_specs(self):
        return [ANY] * len(self.arrs)

    @property
    def out_specs(self):
        return [ANY] * len(self.out_shapes)

    @property
    def scratch(self):
        return [pltpu.SemaphoreType.DMA((self.n_sems,)), pltpu.SemaphoreType.DMA((self.n_sems,))]

    def split(self, refs):
        n = len(self.arrs)
        return refs[:n], refs[n:n + len(self.out_shapes)], refs[-2], refs[-1]

    def run(self, name):
        def body(*refs):
            parts = self.split(refs)
            self.start(*parts)
            self.finish(*parts)

        return pl.pallas_call(body, name=name, in_specs=self.in_specs, out_specs=self.out_specs,
                              out_shape=self.out_shapes, scratch_shapes=self.scratch)(*self.arrs)


def _put(buf, piece, idx, axis):
    return lax.dynamic_update_slice_in_dim(buf, jnp.expand_dims(piece, axis).astype(buf.dtype), idx, axis)


def _all_gather(arrs):
    n = len(arrs)

    def start(ins, outs, send_sems, recv_sems):
        x, y, c, j, sib, chips = _place()
        for i in range(n):
            for k, (cx, cy) in enumerate(chips):
                _remote(ins[i].at[:, c], outs[i].at[:, j, c], send_sems, recv_sems, 6 * i + k, (cx, cy, c)).start()

    def finish(ins, outs, send_sems, recv_sems):
        x, y, c, j, sib, chips = _place()
        passed = []
        for i in range(n):
            for k, (cx, cy) in enumerate(chips):
                got = outs[i].at[:, 2 * cx + cy, c]
                _remote(got, got, send_sems, recv_sems, 6 * i + k, (cx, cy, c)).wait_recv()
                cp = _remote(got, got, send_sems, recv_sems, 6 * i + 3 + k, sib)
                cp.start()
                passed.append(cp)
        for i in range(n):
            for k, (cx, cy) in enumerate(chips):
                got = outs[i].at[:, 2 * cx + cy, 1 - c]
                _remote(got, got, send_sems, recv_sems, 6 * i + 3 + k, sib).wait_recv()
                _remote(ins[i].at[:, c], ins[i].at[:, c], send_sems, recv_sems, 6 * i + k, sib).wait_send()
        for cp in passed:
            cp.wait_send()

    shapes = [jax.ShapeDtypeStruct((a.shape[0], N_CHIPS) + a.shape[1:], a.dtype) for a in arrs]
    return _Exchange(arrs, shapes, 6 * n, start, finish)


class _Offset:
    def __init__(self, sems, k0):
        self.sems, self.k0 = sems, k0

    @property
    def at(self):
        return self

    def __getitem__(self, k):
        return self.sems.at[self.k0 + k]


def _merge(a, b):
    n_in, n_out = len(a.arrs), len(a.out_shapes)

    def both(fa, fb):
        def f(ins, outs, send_sems, recv_sems):
            fa(ins[:n_in], outs[:n_out], send_sems, recv_sems)
            fb(ins[n_in:], outs[n_out:], _Offset(send_sems, a.n_sems), _Offset(recv_sems, a.n_sems))
        return f

    return _Exchange(a.arrs + b.arrs, a.out_shapes + b.out_shapes, a.n_sems + b.n_sems,
                     both(a.start, b.start), both(a.finish, b.finish))


def _pair_swap(arrs):
    n = len(arrs)

    def start(ins, outs, send_sems, recv_sems):
        x, y, c, j, sib, chips = _place()
        for i in range(n):
            _remote(ins[i].at[:, 1 - c], outs[i], send_sems, recv_sems, i, sib).start()

    def finish(ins, outs, send_sems, recv_sems):
        x, y, c, j, sib, chips = _place()
        for i in range(n):
            _remote(ins[i].at[:, 1 - c], outs[i], send_sems, recv_sems, i, sib).wait()

    shapes = [jax.ShapeDtypeStruct((a.shape[0],) + a.shape[2:], a.dtype) for a in arrs]
    return _Exchange(arrs, shapes, n, start, finish)


def _pair_send(arrs):
    n = len(arrs)

    def start(ins, outs, send_sems, recv_sems):
        x, y, c, j, sib, chips = _place()
        for i in range(n):
            _remote(ins[i], outs[i], send_sems, recv_sems, i, sib).start()

    def finish(ins, outs, send_sems, recv_sems):
        x, y, c, j, sib, chips = _place()
        for i in range(n):
            _remote(ins[i], outs[i], send_sems, recv_sems, i, sib).wait()

    shapes = [jax.ShapeDtypeStruct(a.shape, a.dtype) for a in arrs]
    return _Exchange(arrs, shapes, n, start, finish)


def _chip_exchange(arrs, *, scatter):
    n = len(arrs)

    def copies(ins, outs, send_sems, recv_sems):
        x, y, c, j, sib, chips = _place()
        return [(_remote(ins[i].at[2 * cx + cy] if scatter else ins[i], outs[i].at[j], send_sems, recv_sems,
                         3 * i + k, (cx, cy, c)),
                 _remote(outs[i].at[2 * cx + cy], outs[i].at[2 * cx + cy], send_sems, recv_sems, 3 * i + k, (cx, cy, c)))
                for i in range(n) for k, (cx, cy) in enumerate(chips)]

    def start(*refs):
        for out, _ in copies(*refs):
            out.start()

    def finish(*refs):
        for out, back in copies(*refs):
            back.wait_recv()
            out.wait_send()

    shapes = [jax.ShapeDtypeStruct((N_CHIPS,) + a.shape[-2:], a.dtype) for a in arrs]
    return _Exchange(arrs, shapes, 3 * n, start, finish)


FLAT_ROWS = 512


def _add2(a, b, *, out_dtype, name):
    n, R, L = a.shape
    tr = _tile(R, FLAT_ROWS, 16)

    def body(a_ref, b_ref, o_ref):
        o_ref[...] = (a_ref[...].astype(F32) + b_ref[...].astype(F32)).astype(out_dtype)

    spec = pl.BlockSpec((n, tr, L), lambda i: (0, i, 0))
    return pl.pallas_call(
        body, name=name, grid=(R // tr,), in_specs=[spec, spec], out_specs=spec,
        out_shape=jax.ShapeDtypeStruct(a.shape, out_dtype), compiler_params=_cparams(("parallel",)),
    )(a, b)


def _sum_slots(buf, *, name):
    n, R, L = buf.shape
    tr = _tile(R, FLAT_ROWS, 16)

    def body(b_ref, o_ref):
        acc = b_ref[0].astype(F32)
        for k in range(1, n):
            acc = acc + b_ref[k].astype(F32)
        o_ref[...] = acc

    return pl.pallas_call(
        body, name=name, grid=(R // tr,), in_specs=[pl.BlockSpec((n, tr, L), lambda i: (0, i, 0))],
        out_specs=pl.BlockSpec((tr, L), lambda i: (i, 0)),
        out_shape=jax.ShapeDtypeStruct((R, L), F32), compiler_params=_cparams(("parallel",)),
    )(buf)


def _adamw_update(w, g, m, v):
    c1 = 1.0 - ADAM_B1 ** ADAM_STEP
    c2 = 1.0 - ADAM_B2 ** ADAM_STEP
    m = ADAM_B1 * m + (1.0 - ADAM_B1) * g
    v = ADAM_B2 * v + (1.0 - ADAM_B2) * (g * g)
    return -ADAM_LR * ((m / c1) / (jnp.sqrt(v / c2) + ADAM_EPS) + ADAM_WD * w), m, v


def _adamw_halves(w, m, v, own, other, *, name):
    NL, R, L = w.shape
    h = R // 2
    tr = _tile(h, FLAT_ROWS, 16)
    nt = h // tr

    def body(*refs):
        w_ref, m_ref, v_ref = refs[:3]
        own_refs, other_refs = refs[3:3 + NL], refs[3 + NL:3 + 2 * NL]
        d_ref, nm_ref, nv_ref, g_ref = refs[3 + 2 * NL:]
        layer, half = pl.program_id(0), pl.program_id(1)
        mine = half == lax.axis_index("c")
        g = jnp.where(mine, own_refs[0][...], other_refs[0][...])
        for l in range(1, NL):
            g = jnp.where(layer == l, jnp.where(mine, own_refs[l][...], other_refs[l][...]), g)
        d, mm, vv = _adamw_update(w_ref[0], g, m_ref[0], v_ref[0])
        d_ref[0], nm_ref[0], nv_ref[0], g_ref[0] = d, mm, vv, g

    spec = pl.BlockSpec((1, tr, L), lambda l, hh, i: (l, hh * nt + i, 0))
    part = pl.BlockSpec((tr, L), lambda l, hh, i: (i, 0))
    sh = jax.ShapeDtypeStruct((NL, R, L), F32)
    return pl.pallas_call(
        body, name=name, grid=(NL, 2, nt), in_specs=[spec] * 3 + [part] * (2 * NL), out_specs=[spec] * 4,
        out_shape=[sh] * 4, compiler_params=_cparams(("parallel", "parallel", "parallel")),
    )(w, m, v, *own, *other)


def _adamw(w, g, m, v, *, name):
    NL, R, L = w.shape
    tr = _tile(R, FLAT_ROWS, 16)

    def body(w_ref, g_ref, m_ref, v_ref, d_ref, nm_ref, nv_ref):
        d_ref[...], nm_ref[...], nv_ref[...] = _adamw_update(w_ref[...], g_ref[...], m_ref[...], v_ref[...])

    spec = pl.BlockSpec((1, tr, L), lambda l, i: (l, i, 0))
    sh = jax.ShapeDtypeStruct((NL, R, L), F32)
    return pl.pallas_call(
        body, name=name, grid=(NL, R // tr), in_specs=[spec] * 4, out_specs=[spec] * 3, out_shape=[sh] * 3,
        compiler_params=_cparams(("parallel", "parallel")),
    )(w, g, m, v)


WEIGHT_NAMES = ["ab_norm", "ab_w_in", "ab_q_norm", "ab_w_q_b", "ab_kv_norm", "ab_w_kv_b", "ab_conv_w", "ab_conv_b",
                "ab_w_rg_a", "ab_b_rg_a", "ab_w_rg_x", "ab_b_rg_x", "ab_lambda", "ab_w_out", "c_norm", "c_w_in",
                "c_ln_g", "c_ln_b", "c_w_s", "c_b_s", "c_w_out", "ffn_norm", "ffn_w_gate", "ffn_w_up", "ffn_conv_w",
                "ffn_conv_b", "ffn_w_down", "final_norm"]
BIG = {"ab_w_in": 2, "ab_w_q_b": 2, "ab_w_kv_b": 2, "ab_w_out": 1, "c_w_in": 2, "c_w_out": 1,
       "ffn_w_gate": 2, "ffn_w_up": 2, "ffn_w_down": 1}
SMALL_SHARDED = {"ab_conv_w": 2, "c_norm": 1, "c_ln_g": 1, "c_ln_b": 1, "ffn_conv_w": 2}
SMALL_REPLICATED = [n for n in WEIGHT_NAMES if n not in BIG and n not in SMALL_SHARDED]


def _rows(n_elems, mult):
    r = -(-n_elems // LANES)
    return -(-r // mult) * mult


def _flat(parts, rows):
    flat = jnp.concatenate([a.reshape(-1) for a in parts])
    return jnp.pad(flat, (0, rows * LANES - flat.shape[0])).reshape(rows, LANES)


def _unflat(flat, shapes):
    flat = flat.reshape(-1)
    out, off = [], 0
    for s in shapes:
        n = math.prod(s)
        out.append(flat[off:off + n].reshape(s))
        off += n
    return out


def _join_shards(a, axis):
    a = jnp.moveaxis(a, 0, axis)
    return a.reshape(a.shape[:axis] + (a.shape[axis] * a.shape[axis + 1],) + a.shape[axis + 2:])


def kernel(x, positions, ab_norm, ab_w_in, ab_q_norm, ab_w_q_b, ab_kv_norm, ab_w_kv_b, ab_conv_w, ab_conv_b, ab_w_rg_a, ab_b_rg_a, ab_w_rg_x, ab_b_rg_x, ab_lambda, ab_w_out, c_norm, c_w_in, c_ln_g, c_ln_b, c_w_s, c_b_s, c_w_out, ffn_norm, ffn_w_gate, ffn_w_up, ffn_conv_w, ffn_conv_b, ffn_w_down, final_norm, loss_target, m_ab_norm, m_ab_w_in, m_ab_q_norm, m_ab_w_q_b, m_ab_kv_norm, m_ab_w_kv_b, m_ab_conv_w, m_ab_conv_b, m_ab_w_rg_a, m_ab_b_rg_a, m_ab_w_rg_x, m_ab_b_rg_x, m_ab_lambda, m_ab_w_out, m_c_norm, m_c_w_in, m_c_ln_g, m_c_ln_b, m_c_w_s, m_c_b_s, m_c_w_out, m_ffn_norm, m_ffn_w_gate, m_ffn_w_up, m_ffn_conv_w, m_ffn_conv_b, m_ffn_w_down, m_final_norm, v_ab_norm, v_ab_w_in, v_ab_q_norm, v_ab_w_q_b, v_ab_kv_norm, v_ab_w_kv_b, v_ab_conv_w, v_ab_conv_b, v_ab_w_rg_a, v_ab_b_rg_a, v_ab_w_rg_x, v_ab_b_rg_x, v_ab_lambda, v_ab_w_out, v_c_norm, v_c_w_in, v_c_ln_g, v_c_ln_b, v_c_w_s, v_c_b_s, v_c_w_out, v_ffn_norm, v_ffn_w_gate, v_ffn_w_up, v_ffn_conv_w, v_ffn_conv_b, v_ffn_w_down, v_final_norm):
    given = dict(locals())
    w = {n: given[n] for n in WEIGHT_NAMES}
    m = {n: given["m_" + n] for n in WEIGHT_NAMES}
    v = {n: given["v_" + n] for n in WEIGHT_NAMES}
    c = lax.axis_index("c")
    chip = 2 * lax.axis_index("x") + lax.axis_index("y")

    halves = lambda a: a.reshape(a.shape[0], 2, a.shape[1] // 2, a.shape[2])
    tr = lambda a: jnp.swapaxes(a, 1, 2)
    send = {"ab_w_in": w["ab_w_in"], "ab_w_q_b": w["ab_w_q_b"], "ab_w_kv_b": w["ab_w_kv_b"], "ab_w_out": w["ab_w_out"],
            "c_w_in": tr(w["c_w_in"]), "c_w_out": w["c_w_out"], "ffn_w_gate": tr(w["ffn_w_gate"]),
            "ffn_w_up": tr(w["ffn_w_up"]), "ffn_w_down": w["ffn_w_down"]}
    small_rows = _rows(sum(w[n].size for n in SMALL_SHARDED), 16)
    small_sh = _flat([w[n] for n in SMALL_SHARDED], small_rows).reshape(1, 2, small_rows // 2, LANES)
    first_names = ["ab_w_in", "ab_w_q_b", "ab_w_kv_b", "ab_w_out"]
    mine = {n: halves(send[n].astype(BF16)) for n in BIG}

    def put_own(own, arrived):
        a = _put(arrived, own, chip, 1)
        return a.reshape(a.shape[0], -1, a.shape[-1])

    p = {"ab_norm": w["ab_norm"], "ffn_gate_t": {}, "ffn_up_t": {}, "ffn_down": {}}
    first = [mine[n] for n in first_names] + [small_sh]

    def first_arrived(got):
        full = {n: put_own(o, a) for n, o, a in zip(first_names + ["small"], first, got)}
        unshard = lambda a: jnp.swapaxes(a.reshape(N_CHIPS, -1, a.shape[-1]), 0, 1).reshape(-1, N_CHIPS * a.shape[-1])
        p.update(_prep_big(unshard(full["ab_w_in"][0]), unshard(full["ab_w_q_b"][0]), unshard(full["ab_w_kv_b"][0])))
        p["ab_w_out"] = full["ab_w_out"][0]
        small_full = dict(w)
        off = 0
        small_got = full["small"].reshape(N_CHIPS, -1)
        for n, ax in SMALL_SHARDED.items():
            seg = small_got[:, off:off + w[n].size].reshape((N_CHIPS,) + w[n].shape)
            small_full[n] = _join_shards(seg, ax)
            off += w[n].size
        p.update(_prep_small(small_full))

    def weights_ride(parts):
        def sink(arrived):
            for (own, setter), a in zip(parts, arrived):
                setter(put_own(own, a)[0])
        return _all_gather([own for own, _ in parts]), sink

    ffn_keys = {"ffn_gate_t": "ffn_w_gate", "ffn_up_t": "ffn_w_up", "ffn_down": "ffn_w_down"}
    ffn_part = lambda key, l: (mine[ffn_keys[key]][l:l + 1], functools.partial(p[key].__setitem__, l))
    rides = {
        "ab_norm": (_all_gather(first), first_arrived),
        "attn_fwd": weights_ride([ffn_part("ffn_gate_t", 0), ffn_part("ffn_up_t", 0)]),
        "lru_fwd": weights_ride([ffn_part("ffn_down", 0)]),
        "ffn0_gate": weights_ride([ffn_part("ffn_gate_t", 1)]),
        "ffn0_up": weights_ride([ffn_part("ffn_up_t", 1)]),
        "ffn0_down": weights_ride([ffn_part("ffn_down", 1),
                                   (mine["c_w_in"], functools.partial(p.__setitem__, "c_w_in_t")),
                                   (mine["c_w_out"], functools.partial(p.__setitem__, "c_w_out"))]),
    }

    def chip_sums(pair, arrived, tag):
        own = [lax.dynamic_index_in_dim(a, chip, axis=0, keepdims=False) for a in pair]
        return [_sum_slots(_put(a, o, chip, 0), name=f"grad_chip_sum_{tag}{i}") for i, (a, o) in enumerate(zip(arrived, own))]

    half_of = {}

    def grads_ready(layer, ready):
        if layer == 1:
            named = {"gate1": ready["ffn_gate_t"], "up1": ready["ffn_up_t"], "down1": ready["ffn_down"]}
            hosts = {"sgu_bwd": ["down1"], "ffn0_dactbwd": ["gate1", "up1"]}
        else:
            named = {"c_in": ready["c_w_in_t"], "c_out": ready["c_w_out"], "gate0": ready["ffn_gate_t"],
                     "up0": ready["ffn_up_t"], "down0": ready["ffn_down"]}
            hosts = {"attn_dq": ["c_in", "c_out", "down0"], "attn_dkv": ["gate0", "up0"]}
        tag = f"f{layer}"
        sharded = [a.reshape(N_CHIPS, 2, -1, a.shape[-1]) for a in named.values()]

        def paired(from_sib):
            own = [lax.dynamic_index_in_dim(a, c, axis=1, keepdims=False) for a in sharded]
            pair = {k: _add2(a, b, out_dtype=BF16, name=f"grad_pair_add_{tag}{i}")
                    for i, (k, a, b) in enumerate(zip(named, own, from_sib))}
            for kernel_name, keys in hosts.items():
                def sink(arrived, keys=keys, kernel_name=kernel_name):
                    half_of.update(zip(keys, chip_sums([pair[k] for k in keys], arrived, f"{tag}_{kernel_name}")))
                rides[kernel_name] = (_chip_exchange([pair[k] for k in keys], scatter=True), sink)

        rides[f"ffn{layer}_dnorm"] = (_pair_swap(sharded), paired)

    loss_row, grad_x, g = _local_step(x, positions, loss_target, p, rides, grads_ready)

    cols = lambda a, n: jnp.swapaxes(a.reshape(a.shape[0], N_CHIPS, n), 0, 1)
    n_in, n_q, n_kv = w["ab_w_in"].shape[2], w["ab_w_q_b"].shape[2], w["ab_w_kv_b"].shape[2]
    small_names = SMALL_REPLICATED + list(SMALL_SHARDED)
    rs = _rows(sum(g[n].size for n in small_names) + LANES, FLAT_ROWS)
    small = _flat([loss_row] + [g[n] for n in small_names], rs)
    slot = (jnp.arange(2) == c)[:, None, None]
    last = [cols(_unperm_w_in(g["w_in_p"]), n_in), cols(_from_head_blocks(g["w_q_p"], QK_NOPE + QK_ROPE), n_q),
            cols(_join_kv(g["w_k_p"], g["w_v_p"]), n_kv), g["ab_w_out"]]
    last = [a.reshape(N_CHIPS, 2, -1, a.shape[-1]) for a in last]
    *from_sib, small_sib = _merge(_pair_swap(last), _pair_send([small])).run("tail_pair")
    own = [lax.dynamic_index_in_dim(a, c, axis=1, keepdims=False) for a in last]
    pair = [_add2(a, b, out_dtype=BF16, name=f"grad_pair_add_b{i}") for i, (a, b) in enumerate(zip(own, from_sib))]
    pair_small = _sum_slots(jnp.where(slot, small[None], small_sib[None]), name="small_pair_sum")
    *arrived, all_small = _merge(_chip_exchange(pair, scatter=True), _chip_exchange([pair_small], scatter=False)).run("tail_chip")
    half_of.update(zip(["in", "q", "kv", "out"], chip_sums(pair, arrived, "b")))
    small_sum = _sum_slots(_put(all_small, pair_small, chip, 0), name="small_chip_sum")
    keys = ("in", "q", "kv", "out", "c_in", "c_out", "gate0", "gate1", "up0", "up1", "down0", "down1")
    other_half = dict(zip(keys, _pair_send([half_of[k] for k in keys]).run("grad_pair_share")))
    whole = lambda k: jnp.where(slot, half_of[k][None], other_half[k][None]).reshape(-1, half_of[k].shape[-1])
    grads_t = {"ab_w_in": whole("in").T[None], "ab_w_q_b": whole("q").T[None]}
    grads = {"ab_w_kv_b": whole("kv")[None], "c_w_in": whole("c_in").T[None], **{n: tr(a) for n, a in grads_t.items()}}
    by_halves = {"ab_w_out": (("out",), False), "c_w_out": (("c_out",), False), "ffn_w_down": (("down0", "down1"), False),
                 "ffn_w_gate": (("gate0", "gate1"), True), "ffn_w_up": (("up0", "up1"), True)}

    small_parts = _unflat(small_sum, [(1, LANES)] + [g[n].shape for n in small_names])
    loss = small_parts[0][0, 0]
    for n, a in zip(small_names, small_parts[1:]):
        if n in SMALL_SHARDED:
            ax = SMALL_SHARDED[n]
            a = lax.dynamic_slice_in_dim(a, chip * w[n].shape[ax], w[n].shape[ax], axis=ax)
        grads[n] = a.reshape(w[n].shape)

    delta, new_m, new_v = {}, {}, {}
    for n in BIG:
        if n in by_halves:
            ks, transposed = by_halves[n]
            view = tr if transposed else (lambda a: a)
            out = _adamw_halves(view(w[n]), view(m[n]), view(v[n]), [half_of[k] for k in ks], [other_half[k] for k in ks],
                                name=f"adamw_{n}")
            delta[n], new_m[n], new_v[n], grads[n] = (view(a) for a in out)
        elif n in grads_t:
            out = _adamw(tr(w[n]), grads_t[n], tr(m[n]), tr(v[n]), name=f"adamw_{n}")
            delta[n], new_m[n], new_v[n] = (tr(a) for a in out)
        else:
            delta[n], new_m[n], new_v[n] = _adamw(w[n], grads[n], m[n], v[n], name=f"adamw_{n}")
    small_all = [n for n in WEIGHT_NAMES if n not in BIG]
    ra = _rows(sum(w[n].size for n in small_all), FLAT_ROWS)
    pack = lambda d: _flat([d[n] for n in small_all], ra)[None]
    out = _adamw(pack(w), pack(grads), pack(m), pack(v), name="adamw_small")
    shapes = [w[n].shape for n in small_all]
    for d, flat in zip((delta, new_m, new_v), out):
        d.update(zip(small_all, _unflat(flat, shapes)))
    return (loss, grad_x, *[grads[n] for n in WEIGHT_NAMES], *[delta[n] for n in WEIGHT_NAMES],
            *[new_m[n] for n in WEIGHT_NAMES], *[new_v[n] for n in WEIGHT_NAMES])
```

```python
import functools
import math

import jax
import jax.numpy as jnp
from jax import lax
from jax.experimental import pallas as pl
from jax.experimental.pallas import tpu as pltpu

F32 = jnp.float32
BF16 = jnp.bfloat16
MESH = pl.DeviceIdType.MESH

D_MODEL = 1024
MLA_HEADS = 8
Q_LORA = 256
KV_LORA = 128
QK_NOPE = 64
QK_ROPE = 32
V_HEAD = 64
LRU_WIDTH = 512
LRU_HEADS = 8
LRU_BLOCK = 64
LRU_CONV = 4
LRU_C = 8.0
CHUNK = 128
SGU_GROUPS = 8
SGU_WIDTH = 1024
D_FF = 2816
FFN_CONV = 3
NORM_EPS = 1e-6
ROPE_BASE = 10000.0
AB_IN_PAD = 1536
ADAM_LR = 0.001
ADAM_B1 = 0.9
ADAM_B2 = 0.999
ADAM_EPS = 1e-08
ADAM_WD = 0.01
ADAM_STEP = 10

N_CHIPS = 4
LANES = 128
VMEM_LIMIT = 56 * 1024 * 1024
ROW_TILE = 256
NORM_TILE = 1024
MM_TM, MM_TN, MM_TK = 1024, 1536, 2816
MM_TM_T, MM_TK_T = 1408, 1024
GELU_C = math.sqrt(2.0 / math.pi)


def _cparams(sem):
    return pltpu.CompilerParams(dimension_semantics=sem, vmem_limit_bytes=VMEM_LIMIT)


def _tile(n, target, mult=LANES):
    t = (min(n, target) // mult) * mult
    while t >= mult:
        if n % t == 0:
            return t
        t -= mult
    return n


GELU_K = GELU_C * 0.044715


def _gelu(x):
    t = jnp.tanh(x * (GELU_C + GELU_K * (x * x)))
    hx = 0.5 * x
    return hx + hx * t


def _gelu_and_grad(x):
    x2 = x * x
    t = jnp.tanh(x * (GELU_C + GELU_K * x2))
    hx = 0.5 * x
    dg = (0.5 + 0.5 * t) + (hx * (1.0 - t * t)) * (GELU_C + (3.0 * GELU_K) * x2)
    return hx + hx * t, dg


def _sigmoid(x):
    return 1.0 / (1.0 + jnp.exp(-x))


def _shift_rows(x, d, fill_rows):
    ext = jnp.concatenate([fill_rows, x], axis=0)
    return pltpu.roll(ext, d, 0)[8:]


def _shift_rows_up(x, d, fill_rows):
    n = x.shape[0]
    ext = jnp.concatenate([x, fill_rows], axis=0)
    return pltpu.roll(ext, n + 8 - d, 0)[:n]


def _dot(a, b, dims):
    return lax.dot_general(a.astype(BF16), b.astype(BF16), (dims, ((), ())), preferred_element_type=F32)


def _dot_nn(a, b):
    return _dot(a, b, ((1,), (0,)))


def _dot_nt(a, b):
    return _dot(a, b, ((1,), (1,)))


def _dot_tn(a, b):
    return _dot(a, b, ((0,), (0,)))


def _mm(a, b, *, name, ta=False, tb=False, res=None, out_dtype=F32, ride=None):
    if ta:
        K, M = a.shape
    else:
        M, K = a.shape
    N = b.shape[0] if tb else b.shape[1]
    tm = _tile(M, MM_TM_T if ta else (MM_TM if K <= MM_TM else MM_TM // 2), LANES if ta else 8)
    tn = _tile(N, MM_TN, LANES)
    tk = _tile(K, MM_TK_T if ta else MM_TK, LANES)
    nk = K // tk
    a_spec = pl.BlockSpec((tk, tm), lambda j, i, k: (k, i)) if ta else pl.BlockSpec((tm, tk), lambda j, i, k: (i, k))
    b_spec = pl.BlockSpec((tn, tk), lambda j, i, k: (j, k)) if tb else pl.BlockSpec((tk, tn), lambda j, i, k: (k, j))
    o_spec = pl.BlockSpec((tm, tn), lambda j, i, k: (i, j))
    dims = ((0,) if ta else (1,), (1,) if tb else (0,))
    has_res = res is not None

    def body(*refs):
        a_ref, b_ref = refs[:2]
        r_ref = refs[2] if has_res else None
        o_ref = refs[3] if has_res else refs[2]
        p = _dot(a_ref[...], b_ref[...], dims)

        def finish(r):
            if has_res:
                r = r + r_ref[...].astype(F32)
            o_ref[...] = r.astype(out_dtype)

        if nk == 1:
            finish(p)
            return
        acc_ref = refs[-1]
        k = pl.program_id(2)

        @pl.when(k == 0)
        def _():
            acc_ref[...] = p

        @pl.when(jnp.logical_and(k > 0, k < nk - 1))
        def _():
            acc_ref[...] += p

        @pl.when(k == nk - 1)
        def _():
            finish(acc_ref[...] + p)

    in_specs = [a_spec, b_spec] + ([o_spec] if has_res else [])
    args = (a, b) + ((res,) if has_res else ())
    return _pcall(
        body, name=name, grid=(N // tn, M // tm, nk), in_specs=in_specs, out_specs=[o_spec],
        out_shape=[jax.ShapeDtypeStruct((M, N), out_dtype)], args=args,
        scratch=[pltpu.VMEM((tm, tn), F32)] if nk > 1 else [], sem=("parallel", "parallel", "arbitrary"), ride=ride)[0]


def _rms_fwd(x, g, *, name, cb=0, out_dtype=BF16, ride=None):
    T = x.shape[0]
    W = g.shape[-1]
    g = g.reshape(1, W)
    tt = _tile(T, NORM_TILE, 16)

    def body(x_ref, g_ref, o_ref):
        xf = x_ref[...].astype(F32)
        rstd = lax.rsqrt(jnp.mean(xf * xf, axis=-1, keepdims=True) + NORM_EPS)
        o_ref[...] = (xf * rstd * g_ref[...]).astype(out_dtype)

    return _pcall(
        body, name=name, grid=(T // tt,),
        in_specs=[pl.BlockSpec((tt, W), lambda i: (i, cb)), pl.BlockSpec((1, W), lambda i: (0, 0))],
        out_specs=[pl.BlockSpec((tt, W), lambda i: (i, 0))], out_shape=[jax.ShapeDtypeStruct((T, W), out_dtype)],
        args=(x, g), sem=("parallel",), ride=ride)[0]


def _rms_bwd(x, g, dy, *, name, cb=0, res=None, out_dtype=F32, ride=None):
    T = x.shape[0]
    W = g.shape[-1]
    g = g.reshape(1, W)
    tt = _tile(T, NORM_TILE // 2, 16)
    has_res = res is not None

    def body(*refs):
        if has_res:
            x_ref, g_ref, dy_ref, r_ref, dx_ref, dg_ref = refs
        else:
            x_ref, g_ref, dy_ref, dx_ref, dg_ref = refs
        xf = x_ref[...].astype(F32)
        dyf = dy_ref[...].astype(F32)
        rstd = lax.rsqrt(jnp.mean(xf * xf, axis=-1, keepdims=True) + NORM_EPS)
        xhat = xf * rstd
        dxhat = dyf * g_ref[...]
        dx = rstd * (dxhat - xhat * jnp.mean(dxhat * xhat, axis=-1, keepdims=True))
        if has_res:
            dx = dx + r_ref[...].astype(F32)
        dx_ref[...] = dx.astype(out_dtype)
        part = jnp.sum(dyf * xhat, axis=0, keepdims=True)

        @pl.when(pl.program_id(0) == 0)
        def _():
            dg_ref[...] = part

        @pl.when(pl.program_id(0) > 0)
        def _():
            dg_ref[...] += part

    row = pl.BlockSpec((tt, W), lambda i: (i, 0))
    in_specs = [pl.BlockSpec((tt, W), lambda i: (i, cb)), pl.BlockSpec((1, W), lambda i: (0, 0)), row]
    args = (x, g, dy)
    if has_res:
        in_specs.append(row)
        args = args + (res,)
    return _pcall(
        body, name=name, grid=(T // tt,), in_specs=in_specs,
        out_specs=[row, pl.BlockSpec((1, W), lambda i: (0, 0))],
        out_shape=[jax.ShapeDtypeStruct((T, W), out_dtype), jax.ShapeDtypeStruct((1, W), F32)], args=args, ride=ride)


def _final_fwd_bwd(h, g, target, *, name):
    T, W = h.shape
    g = g.reshape(1, W)
    tt = _tile(T, NORM_TILE, 16)

    def body(x_ref, g_ref, t_ref, loss_ref, dx_ref, dg_ref):
        xf = x_ref[...]
        rstd = lax.rsqrt(jnp.mean(xf * xf, axis=-1, keepdims=True) + NORM_EPS)
        xhat = xf * rstd
        err = xhat * g_ref[...] - t_ref[...]
        lpart = jnp.zeros((1, LANES), F32) + (0.5 / W) * jnp.sum(err * err)
        dyf = err * (1.0 / W)
        dxhat = dyf * g_ref[...]
        dx_ref[...] = rstd * (dxhat - xhat * jnp.mean(dxhat * xhat, axis=-1, keepdims=True))
        part = jnp.sum(dyf * xhat, axis=0, keepdims=True)

        @pl.when(pl.program_id(0) == 0)
        def _():
            dg_ref[...] = part
            loss_ref[...] = lpart

        @pl.when(pl.program_id(0) > 0)
        def _():
            dg_ref[...] += part
            loss_ref[...] += lpart

    row = pl.BlockSpec((tt, W), lambda i: (i, 0))
    return pl.pallas_call(
        body, name=name, grid=(T // tt,),
        in_specs=[row, pl.BlockSpec((1, W), lambda i: (0, 0)), row],
        out_specs=[pl.BlockSpec((1, LANES), lambda i: (0, 0)), row, pl.BlockSpec((1, W), lambda i: (0, 0))],
        out_shape=[jax.ShapeDtypeStruct((1, LANES), F32), jax.ShapeDtypeStruct((T, W), F32),
                   jax.ShapeDtypeStruct((1, W), F32)],
        compiler_params=_cparams(("arbitrary",)),
    )(h, g, target)


def _swap16(x):
    lane = lax.broadcasted_iota(jnp.int32, x.shape, 1)
    return jnp.where((lane % 32) < 16, pltpu.roll(x, LANES - 16, 1), pltpu.roll(x, 16, 1))


def _rope(x, c, s):
    return x * c + _swap16(x) * s


def _rope_t(d, c, s):
    return d * c + _swap16(d * s)


def _head_block_map(fn, x, cos, sin, *, name):
    T, W = x.shape
    tt = _tile(T, NORM_TILE, 16)

    def body(x_ref, c_ref, s_ref, o_ref):
        c, s = c_ref[...], s_ref[...]
        for h in range(W // LANES):
            lanes = slice(h * LANES, (h + 1) * LANES)
            o_ref[:, lanes] = fn(x_ref[:, lanes], c, s).astype(BF16)

    tab = pl.BlockSpec((tt, LANES), lambda i: (i, 0))
    blk = pl.BlockSpec((tt, W), lambda i: (i, 0))
    return pl.pallas_call(
        body, name=name, grid=(T // tt,), in_specs=[blk, tab, tab], out_specs=blk,
        out_shape=jax.ShapeDtypeStruct((T, W), BF16), compiler_params=_cparams(("parallel",)),
    )(x, cos, sin)


def _rope_q(q, cos, sin, *, name):
    scale = _attn_scale()
    return _head_block_map(lambda x, c, s: _rope(x, c, s) * scale, q, cos, sin, name=name)


def _rope_q_bwd(dq, cos, sin, *, name):
    return _head_block_map(_rope_t, dq, cos, sin, name=name)


def _key_blocks(kv, z, cos, sin, *, kpe_block, name):
    T = kv.shape[0]
    tt = _tile(T, NORM_TILE, 16)
    W = MLA_HEADS * LANES

    def body(kv_ref, z_ref, c_ref, s_ref, o_ref):
        kr = _rope(z_ref[...], c_ref[...], s_ref[...])
        for h in range(MLA_HEADS):
            lanes = slice(h * LANES, (h + 1) * LANES)
            o_ref[:, lanes] = (kv_ref[:, lanes].astype(F32) + kr).astype(BF16)

    tab = pl.BlockSpec((tt, LANES), lambda i: (i, 0))
    blk = pl.BlockSpec((tt, W), lambda i: (i, 0))
    return pl.pallas_call(
        body, name=name, grid=(T // tt,),
        in_specs=[blk, pl.BlockSpec((tt, LANES), lambda i: (i, kpe_block)), tab, tab], out_specs=blk,
        out_shape=jax.ShapeDtypeStruct((T, W), BF16), compiler_params=_cparams(("parallel",)),
    )(kv, z, cos, sin)


def _key_rope_bwd(dk, cos, sin, *, name):
    T = dk.shape[0]
    tt = _tile(T, NORM_TILE, 16)

    def body(d_ref, c_ref, s_ref, o_ref):
        d = d_ref[:, :LANES]
        for h in range(1, MLA_HEADS):
            d = d + d_ref[:, h * LANES:(h + 1) * LANES]
        lane = lax.broadcasted_iota(jnp.int32, d.shape, 1)
        d = jnp.where(jnp.logical_and(lane >= QK_NOPE, lane < QK_NOPE + QK_ROPE), d, 0.0)
        o_ref[...] = _rope_t(d, c_ref[...], s_ref[...]).astype(BF16)

    tab = pl.BlockSpec((tt, LANES), lambda i: (i, 0))
    return pl.pallas_call(
        body, name=name, grid=(T // tt,),
        in_specs=[pl.BlockSpec((tt, MLA_HEADS * LANES), lambda i: (i, 0)), tab, tab], out_specs=tab,
        out_shape=jax.ShapeDtypeStruct((T, LANES), BF16), compiler_params=_cparams(("parallel",)),
    )(dk, cos, sin)


ATT_BLOCK = 512


def _attn_scale():
    return float((QK_NOPE + QK_ROPE) ** -0.5)


def _causal_mask(qi, kj, tq, tk):
    row = qi * tq + lax.broadcasted_iota(jnp.int32, (tq, tk), 0)
    col = kj * tk + lax.broadcasted_iota(jnp.int32, (tq, tk), 1)
    return col <= row


def _pcall(body, *, name, grid, in_specs, out_specs, out_shape, args, scratch=(), sem=None, ride=None):
    n_in, n_out, n_scr = len(args), len(out_shape), len(scratch)
    if ride is None:
        return pl.pallas_call(
            body, name=name, grid=grid, in_specs=list(in_specs), out_specs=list(out_specs), out_shape=list(out_shape),
            scratch_shapes=list(scratch), compiler_params=_cparams(sem or ("arbitrary",) * len(grid)))(*args)
    ex, sink = ride
    o0 = n_in + len(ex.arrs)
    s0 = o0 + n_out + len(ex.out_shapes)

    def hosted(*refs):
        parts = (refs[n_in:o0], refs[o0 + n_out:s0], refs[-2], refs[-1])
        ids = [pl.program_id(i) for i in range(len(grid))]
        pl.when(functools.reduce(jnp.logical_and, [i == 0 for i in ids]))(lambda: ex.start(*parts))
        body(*refs[:n_in], *refs[o0:o0 + n_out], *refs[s0:s0 + n_scr])
        pl.when(functools.reduce(jnp.logical_and, [i == n - 1 for i, n in zip(ids, grid)]))(lambda: ex.finish(*parts))

    outs = pl.pallas_call(
        hosted, name=name, grid=grid, in_specs=list(in_specs) + ex.in_specs, out_specs=list(out_specs) + ex.out_specs,
        out_shape=list(out_shape) + ex.out_shapes, scratch_shapes=list(scratch) + ex.scratch,
        compiler_params=_cparams(("arbitrary",) * len(grid)))(*args, *ex.arrs)
    sink(outs[n_out:])
    return outs[:n_out]


PAIRS = MLA_HEADS // 2


def _own_lanes(x, first):
    lane = lax.broadcasted_iota(jnp.int32, x.shape, 1)
    return jnp.where((lane < V_HEAD) if first else (lane >= V_HEAD), x, 0.0)


def _attn_fwd(q, k, kv, *, B, S, v_block0, name, ride=None):
    tq = tk = min(ATT_BLOCK, S)
    nq = S // tq
    T = B * S

    def body(q_ref, k_ref, v_ref, o_ref, lse_ref):
        qi = pl.program_id(2)
        qs = (q_ref[:, :LANES], q_ref[:, LANES:])

        def step(masked):
            def f(j, carry):
                rows = pl.ds(pl.multiple_of(j * tk, tk), tk)
                vb = v_ref[rows, :]
                out = []
                for h in range(2):
                    m, l, acc = carry[h]
                    s = _dot_nt(qs[h], k_ref[rows, h * LANES:(h + 1) * LANES])
                    if masked:
                        s = jnp.where(_causal_mask(qi, j, tq, tk), s, -jnp.inf)
                    m_new = jnp.maximum(m, jnp.max(s, axis=-1, keepdims=True))
                    alpha = jnp.exp(m - m_new)
                    p = jnp.exp(s - m_new)
                    out.append((m_new, alpha * l + jnp.sum(p, axis=-1, keepdims=True), alpha * acc + _dot_nn(p, vb)))
                return tuple(out)
            return f

        one = (jnp.full((tq, 1), -1e30, F32), jnp.zeros((tq, 1), F32), jnp.zeros((tq, LANES), F32))
        (ma, la, acca), (mb, lb, accb) = step(True)(qi, lax.fori_loop(0, qi, step(False), (one, one)))
        o_ref[...] = _own_lanes(acca / la, True) + _own_lanes(accb / lb, False)
        lse_ref[0, 0] = ma + jnp.log(la)
        lse_ref[0, 1] = mb + jnp.log(lb)

    return _pcall(
        body, name=name, grid=(B, PAIRS, nq),
        in_specs=[pl.BlockSpec((tq, 2 * LANES), lambda b, g, i: (b * nq + i, g)),
                  pl.BlockSpec((S, 2 * LANES), lambda b, g, i: (b, g)),
                  pl.BlockSpec((S, LANES), lambda b, g, i: (b, v_block0 + g))],
        out_specs=[pl.BlockSpec((tq, LANES), lambda b, g, i: (b * nq + i, g)),
                   pl.BlockSpec((1, 2, tq, 1), lambda b, g, i: (b, g, i, 0))],
        out_shape=[jax.ShapeDtypeStruct((T, PAIRS * LANES), F32), jax.ShapeDtypeStruct((B, MLA_HEADS, S, 1), F32)],
        args=(q, k, kv), ride=ride)


def _attn_dq(q, k, kv, o, lse, do, *, B, S, v_block0, name, ride=None):
    tq = tk = min(ATT_BLOCK, S)
    nq = S // tq
    T = B * S
    scale = _attn_scale()

    def body(q_ref, k_ref, v_ref, o_ref, lse_ref, do_ref, dq_ref, delta_ref):
        qi = pl.program_id(2)
        qs = (q_ref[:, :LANES], q_ref[:, LANES:])
        dos = (_own_lanes(do_ref[...], True), _own_lanes(do_ref[...], False))
        deltas = tuple(jnp.sum(d * o_ref[...], axis=-1, keepdims=True) for d in dos)
        lses = (lse_ref[0, 0], lse_ref[0, 1])

        def step(masked):
            def f(j, carry):
                rows = pl.ds(pl.multiple_of(j * tk, tk), tk)
                vb = v_ref[rows, :]
                out = []
                for h in range(2):
                    kb = k_ref[rows, h * LANES:(h + 1) * LANES]
                    p = jnp.exp(_dot_nt(qs[h], kb) - lses[h])
                    if masked:
                        p = jnp.where(_causal_mask(qi, j, tq, tk), p, 0.0)
                    ds = p * (_dot_nt(dos[h], vb) - deltas[h])
                    out.append(carry[h] + _dot_nn(ds, kb))
                return tuple(out)
            return f

        zero = jnp.zeros((tq, LANES), F32)
        dqa, dqb = step(True)(qi, lax.fori_loop(0, qi, step(False), (zero, zero)))
        dq_ref[:, :LANES] = dqa * scale
        dq_ref[:, LANES:] = dqb * scale
        delta_ref[0, 0] = deltas[0]
        delta_ref[0, 1] = deltas[1]

    qrow = lambda w: pl.BlockSpec((tq, w), lambda b, g, i: (b * nq + i, g))
    stat = pl.BlockSpec((1, 2, tq, 1), lambda b, g, i: (b, g, i, 0))
    return _pcall(
        body, name=name, grid=(B, PAIRS, nq),
        in_specs=[qrow(2 * LANES), pl.BlockSpec((S, 2 * LANES), lambda b, g, i: (b, g)),
                  pl.BlockSpec((S, LANES), lambda b, g, i: (b, v_block0 + g)), qrow(LANES), stat, qrow(LANES)],
        out_specs=[qrow(2 * LANES), stat],
        out_shape=[jax.ShapeDtypeStruct((T, MLA_HEADS * LANES), F32), jax.ShapeDtypeStruct((B, MLA_HEADS, S, 1), F32)],
        args=(q, k, kv, o, lse, do), sem=("parallel", "parallel", "parallel"), ride=ride)


def _attn_dkv(q, k, kv, lse_rows, delta_rows, do, *, B, S, v_block0, name, ride=None):
    tq = tk = min(ATT_BLOCK, S)
    nq = S // tq
    T = B * S

    def body(q_ref, k_ref, v_ref, lse_ref, delta_ref, do_ref, dk_ref, dv_ref):
        kj = pl.program_id(2)
        ks = (k_ref[:, :LANES], k_ref[:, LANES:])
        vb = v_ref[...]

        def step(masked):
            def f(i, carry):
                rows = pl.ds(pl.multiple_of(i * tq, tq), tq)
                do_b = do_ref[rows, :]
                dks, dv = list(carry[:2]), carry[2]
                for h in range(2):
                    qb = q_ref[rows, h * LANES:(h + 1) * LANES]
                    doh = _own_lanes(do_b, h == 0)
                    pt = jnp.exp(_dot_nt(ks[h], qb) - lse_ref[0, h, pl.ds(i, 1), :])
                    if masked:
                        krow = kj * tk + lax.broadcasted_iota(jnp.int32, (tk, tq), 0)
                        qcol = i * tq + lax.broadcasted_iota(jnp.int32, (tk, tq), 1)
                        pt = jnp.where(krow <= qcol, pt, 0.0)
                    dst = pt * (_dot_nt(vb, doh) - delta_ref[0, h, pl.ds(i, 1), :])
                    dks[h] = dks[h] + _dot_nn(dst, qb)
                    dv = dv + _dot_nn(pt, doh)
                return dks[0], dks[1], dv
            return f

        zero = jnp.zeros((tk, LANES), F32)
        dka, dkb, dv = lax.fori_loop(kj + 1, nq, step(False), step(True)(kj, (zero, zero, zero)))
        dk_ref[:, :LANES] = dka
        dk_ref[:, LANES:] = dkb
        dv_ref[...] = dv

    krow = lambda w, c0: pl.BlockSpec((tk, w), lambda b, g, j: (b * nq + j, c0 + g))
    seq = lambda w: pl.BlockSpec((S, w), lambda b, g, j: (b, g))
    stat = pl.BlockSpec((1, 2, nq, tq), lambda b, g, j: (b, g, 0, 0))
    return _pcall(
        body, name=name, grid=(B, PAIRS, nq),
        in_specs=[seq(2 * LANES), krow(2 * LANES, 0), krow(LANES, v_block0), stat, stat, seq(LANES)],
        out_specs=[krow(2 * LANES, 0), krow(LANES, 0)],
        out_shape=[jax.ShapeDtypeStruct((T, MLA_HEADS * LANES), F32), jax.ShapeDtypeStruct((T, PAIRS * LANES), F32)],
        args=(q, k, kv, lse_rows, delta_rows, do), ride=ride)


def _lru_gates(xl, halo, cw_ref, cb_ref, wa_ref, ba_ref, wx_ref, bx_ref, lam_ref):
    xc = cb_ref[...] + cw_ref[3:4, :] * xl
    for kk in range(LRU_CONV - 1):
        xc = xc + cw_ref[kk:kk + 1, :] * _shift_rows(xl, LRU_CONV - 1 - kk, halo)
    r = _sigmoid(_dot_nn(xc, wa_ref[...]) + ba_ref[...])
    i = _sigmoid(_dot_nn(xc, wx_ref[...]) + bx_ref[...])
    lam = lam_ref[...]
    sp = jnp.maximum(-lam, 0.0) + jnp.log(1.0 + jnp.exp(-jnp.abs(lam)))
    a = jnp.exp(-LRU_C * r * sp)
    mult = jnp.sqrt(1.0 - a * a)
    return xc, r, i, sp, a, mult


def _lru_specs(tt, nt, S):
    def make(rev):
        tmap = (lambda t: nt - 1 - t) if rev else (lambda t: t)
        tile = lambda cb: pl.BlockSpec((tt, LRU_WIDTH), lambda b, t: (b * nt + tmap(t), cb))
        prev8 = lambda cb: pl.BlockSpec(
            (8, LRU_WIDTH), lambda b, t: (jnp.maximum((b * nt + tmap(t)) * (tt // 8) - 1, 0), cb))
        return tile, prev8, tmap
    return make


def _lru_fwd(z, cw, cb, wa, ba, wx, bx, lam, *, S, name, ride=None):
    T = z.shape[0]
    tt = min(ROW_TILE, S)
    nt = S // tt
    tile, prev8, _ = _lru_specs(tt, nt, S)(False)
    vec = lambda r: pl.BlockSpec((r, LRU_WIDTH), lambda b, t: (0, 0))
    mat = pl.BlockSpec((LRU_WIDTH, LRU_WIDTH), lambda b, t: (0, 0))

    def body(xl_ref, halo_ref, gate_ref, cw_ref, cb_ref, wa_ref, ba_ref, wx_ref, bx_ref, lam_ref,
             y_ref, h_ref, carry_ref):
        t = pl.program_id(1)
        first = t == 0
        halo = jnp.where(first, 0.0, halo_ref[...])
        xl_t = xl_ref[...]
        xc, r, i, sp, a, mult = _lru_gates(xl_t, halo, cw_ref, cb_ref, wa_ref, ba_ref, wx_ref, bx_ref, lam_ref)
        bv = mult * (i * xc)
        ones = jnp.ones((8, LRU_WIDTH), F32)
        zeros = jnp.zeros((8, LRU_WIDTH), F32)
        row = lax.broadcasted_iota(jnp.int32, (tt, LRU_WIDTH), 0)
        A = a
        d = 1
        while d < tt:
            if d < 8:
                a_sh = _shift_rows(A, d, ones)
                b_sh = _shift_rows(bv, d, zeros)
            else:
                a_sh = jnp.where(row < d, 1.0, pltpu.roll(A, d, 0))
                b_sh = jnp.where(row < d, 0.0, pltpu.roll(bv, d, 0))
            bv = A * b_sh + bv
            A = A * a_sh
            d *= 2
        h0 = jnp.where(first, 0.0, carry_ref[0:1, :])
        h = A * h0 + bv
        carry_ref[...] = jnp.broadcast_to(h[tt - 1:tt, :], (8, LRU_WIDTH))
        h_ref[...] = h
        y_ref[...] = (h * _gelu(gate_ref[...])).astype(BF16)

    return _pcall(
        body, name=name, grid=(T // S, nt),
        in_specs=[tile(0), prev8(0), tile(1), vec(LRU_CONV), vec(1), mat, vec(1), mat, vec(1), vec(1)],
        out_specs=[tile(0), tile(0)],
        out_shape=[jax.ShapeDtypeStruct((T, LRU_WIDTH), BF16), jax.ShapeDtypeStruct((T, LRU_WIDTH), F32)],
        args=(z, z, z, cw, cb, wa, ba, wx, bx, lam), scratch=[pltpu.VMEM((8, LRU_WIDTH), F32)], ride=ride)


def _lru_bwd(z, h, dy, cw, cb, wa, ba, wx, bx, lam, *, S, name):
    T = z.shape[0]
    tt = min(ROW_TILE, S)
    nt = S // tt
    tile, prev8, tmap = _lru_specs(tt, nt, S)(True)
    vec = lambda r: pl.BlockSpec((r, LRU_WIDTH), lambda b, t: (0, 0))
    mat = pl.BlockSpec((LRU_WIDTH, LRU_WIDTH), lambda b, t: (0, 0))

    def body(xl_ref, halo_ref, gate_ref, h_ref, hprev_ref, dy_ref, cw_ref, cb_ref, wa_ref, ba_ref, wx_ref,
             bx_ref, lam_ref, dxl_ref, dgate_ref, dcw_ref, dcb_ref, dwa_ref, dba_ref, dwx_ref, dbx_ref,
             dlam_ref, lamc_ref, ac_ref, dxc_ref):
        b = pl.program_id(0)
        t = pl.program_id(1)
        tr = nt - 1 - t
        seq_first = tr == 0
        seq_last = t == 0
        halo = jnp.where(seq_first, 0.0, halo_ref[...])
        xl_t = xl_ref[...]
        xc, r, i, sp, a, mult = _lru_gates(xl_t, halo, cw_ref, cb_ref, wa_ref, ba_ref, wx_ref, bx_ref, lam_ref)
        hh = h_ref[...]
        dyf = dy_ref[...].astype(F32)
        gl, dgl = _gelu_and_grad(gate_ref[...])
        dgate_ref[...] = (dyf * hh * dgl).astype(BF16)
        dh = dyf * gl

        a_first_later = jnp.where(seq_last, 0.0, ac_ref[...])
        lam_later = jnp.where(seq_last, 0.0, lamc_ref[...])
        row = lax.broadcasted_iota(jnp.int32, (tt, LRU_WIDTH), 0)
        A = _shift_rows_up(a, 1, a_first_later)
        lm = dh
        ones = jnp.ones((8, LRU_WIDTH), F32)
        zeros = jnp.zeros((8, LRU_WIDTH), F32)
        d = 1
        while d < tt:
            if d < 8:
                a_sh = _shift_rows_up(A, d, ones)
                l_sh = _shift_rows_up(lm, d, zeros)
            else:
                a_sh = jnp.where(row >= tt - d, 1.0, pltpu.roll(A, tt - d, 0))
                l_sh = jnp.where(row >= tt - d, 0.0, pltpu.roll(lm, tt - d, 0))
            lm = lm + A * l_sh
            A = A * a_sh
            d *= 2
        lm = lm + A * lam_later[0:1, :]
        lamc_ref[...] = jnp.broadcast_to(lm[0:1, :], (8, LRU_WIDTH))
        ac_ref[...] = jnp.broadcast_to(a[0:1, :], (8, LRU_WIDTH))

        hprev_halo = jnp.where(seq_first, 0.0, hprev_ref[...])
        h_prev = _shift_rows(hh, 1, hprev_halo)
        da = lm * h_prev
        ixc = i * xc
        dmult = lm * ixc
        di = lm * mult * xc
        dxc = lm * mult * i
        da = da - dmult * a / mult
        dlog = da * a
        dr = dlog * (-LRU_C) * sp
        dsp_part = jnp.sum(dlog * (-LRU_C) * r, axis=0, keepdims=True)
        dpa = dr * r * (1.0 - r)
        dpx = di * i * (1.0 - i)
        dxc = dxc + _dot_nt(dpa, wa_ref[...]) + _dot_nt(dpx, wx_ref[...])
        dwa_part = _dot_tn(xc, dpa)
        dwx_part = _dot_tn(xc, dpx)

        later = jnp.where(seq_last, 0.0, dxc_ref[...])
        dxl = cw_ref[3:4, :] * dxc
        for kk in range(LRU_CONV - 1):
            dxl = dxl + cw_ref[kk:kk + 1, :] * _shift_rows_up(dxc, LRU_CONV - 1 - kk, later)
        dxl_ref[...] = dxl.astype(BF16)
        dxc_ref[...] = dxc[0:8, :]
        dcw_rows = [jnp.sum(dxc * _shift_rows(xl_t, LRU_CONV - 1 - kk, halo), axis=0, keepdims=True)
                    for kk in range(LRU_CONV - 1)]
        dcw_rows.append(jnp.sum(dxc * xl_t, axis=0, keepdims=True))
        dcw_part = jnp.concatenate(dcw_rows + [jnp.zeros((8 - LRU_CONV, LRU_WIDTH), F32)], axis=0)
        lamv = lam_ref[...]
        dlam_part = dsp_part * (-_sigmoid(-lamv))
        parts = ((dcw_ref, dcw_part), (dcb_ref, jnp.sum(dxc, axis=0, keepdims=True)),
                 (dwa_ref, dwa_part), (dba_ref, jnp.sum(dpa, axis=0, keepdims=True)),
                 (dwx_ref, dwx_part), (dbx_ref, jnp.sum(dpx, axis=0, keepdims=True)),
                 (dlam_ref, dlam_part))
        start = jnp.logical_and(b == 0, t == 0)

        @pl.when(start)
        def _():
            for ref, val in parts:
                ref[...] = val

        @pl.when(jnp.logical_not(start))
        def _():
            for ref, val in parts:
                ref[...] += val

    acc = lambda r: pl.BlockSpec((r, LRU_WIDTH), lambda b, t: (0, 0))
    return pl.pallas_call(
        body, name=name, grid=(T // S, nt),
        in_specs=[tile(0), prev8(0), tile(1), tile(0), prev8(0), tile(0),
                  vec(LRU_CONV), vec(1), mat, vec(1), mat, vec(1), vec(1)],
        out_specs=[tile(0), tile(0), acc(8), acc(1), mat, acc(1), mat, acc(1), acc(1)],
        out_shape=[jax.ShapeDtypeStruct((T, LRU_WIDTH), BF16), jax.ShapeDtypeStruct((T, LRU_WIDTH), BF16),
                   jax.ShapeDtypeStruct((8, LRU_WIDTH), F32), jax.ShapeDtypeStruct((1, LRU_WIDTH), F32),
                   jax.ShapeDtypeStruct((LRU_WIDTH, LRU_WIDTH), F32), jax.ShapeDtypeStruct((1, LRU_WIDTH), F32),
                   jax.ShapeDtypeStruct((LRU_WIDTH, LRU_WIDTH), F32), jax.ShapeDtypeStruct((1, LRU_WIDTH), F32),
                   jax.ShapeDtypeStruct((1, LRU_WIDTH), F32)],
        scratch_shapes=[pltpu.VMEM((8, LRU_WIDTH), F32), pltpu.VMEM((8, LRU_WIDTH), F32),
                        pltpu.VMEM((8, LRU_WIDTH), F32)],
        compiler_params=_cparams(("arbitrary", "arbitrary")),
    )(z, z, z, h, h, dy, cw, cb, wa, ba, wx, bx, lam)


FFN_CT = 1408
FFN_TILE = 512


def _ffn_conv(g, halo, cw, cb):
    gc = cb + cw[2:3, :] * g
    for kk in range(FFN_CONV - 1):
        gc = gc + cw[kk:kk + 1, :] * _shift_rows(g, FFN_CONV - 1 - kk, halo)
    return gc


def _row_chunks(rows, chunk):
    return [slice(r0, min(r0 + chunk, rows)) for r0 in range(0, rows, chunk)]


FFN_CHUNK = 128


def _ffn_act_down(g, u, cw, cb, w_down, res, *, S, name, ride=None):
    T, F = g.shape
    D = w_down.shape[1]
    tt = min(FFN_TILE, S)
    nt = S // tt
    tc = _tile(F, FFN_CT)
    nj = F // tc

    def body(g_ref, halo_ref, u_ref, cw_ref, cb_ref, w_ref, r_ref, o_ref, act_ref):
        j = pl.program_id(1)
        first = (pl.program_id(0) % nt) == 0
        cw, cb = cw_ref[...], cb_ref[...]

        @pl.when(j == 0)
        def _():
            o_ref[...] = r_ref[...]

        for r in _row_chunks(tt, FFN_CHUNK):
            halo = jnp.where(first, 0.0, halo_ref[...]) if r.start == 0 else g_ref[r.start - 8:r.start, :]
            act = (_gelu(_ffn_conv(g_ref[r, :], halo, cw, cb)) * u_ref[r, :]).astype(BF16)
            act_ref[r, :] = act
            o_ref[r, :] += _dot_nn(act, w_ref[...])

    tile = pl.BlockSpec((tt, tc), lambda i, j: (i, j))
    prev8 = pl.BlockSpec((8, tc), lambda i, j: (jnp.maximum(i * (tt // 8) - 1, 0), j))
    rows = pl.BlockSpec((tt, D), lambda i, j: (i, 0))
    return _pcall(
        body, name=name, grid=(T // tt, nj),
        in_specs=[tile, prev8, tile, pl.BlockSpec((FFN_CONV, tc), lambda i, j: (0, j)),
                  pl.BlockSpec((1, tc), lambda i, j: (0, j)), pl.BlockSpec((tc, D), lambda i, j: (j, 0)), rows],
        out_specs=[rows, tile], out_shape=[jax.ShapeDtypeStruct((T, D), F32), jax.ShapeDtypeStruct((T, F), BF16)],
        args=(g, g, u, cw, cb, w_down, res), sem=("parallel", "arbitrary"), ride=ride)


def _ffn_act_bwd(g, u, dh, w_down, cw, cb, *, S, name, ride=None):
    T, F = g.shape
    D = w_down.shape[1]
    tt = min(FFN_TILE, S)
    nt = S // tt
    ntt = T // tt
    tc = _tile(F, FFN_CT)

    def body(g_ref, halo_ref, u_ref, dh_ref, w_ref, cw_ref, cb_ref, dg_ref, du_ref, dcw_ref, dcb_ref, later_ref):
        step = pl.program_id(1)
        ti = (ntt - 1 - step) % nt
        cw, cb = cw_ref[...], cb_ref[...]

        @pl.when(step == 0)
        def _():
            dcw_ref[...] = jnp.zeros_like(dcw_ref)
            dcb_ref[...] = jnp.zeros_like(dcb_ref)

        halo = jnp.where(ti == 0, 0.0, halo_ref[...])
        gt = g_ref[...]
        gl, dgl = _gelu_and_grad(_ffn_conv(gt, halo, cw, cb))
        da = _dot_nt(dh_ref[...], w_ref[...])
        du_ref[...] = (da * gl).astype(BF16)
        dgc = da * u_ref[...] * dgl
        later = jnp.where(ti == nt - 1, 0.0, later_ref[...])
        dg = cw[2:3, :] * dgc
        for kk in range(FFN_CONV - 1):
            dg = dg + cw[kk:kk + 1, :] * _shift_rows_up(dgc, FFN_CONV - 1 - kk, later)
        dg_ref[...] = dg.astype(BF16)
        later_ref[...] = dgc[0:8, :]
        rows = [jnp.sum(dgc * _shift_rows(gt, FFN_CONV - 1 - kk, halo), axis=0, keepdims=True)
                for kk in range(FFN_CONV - 1)]
        rows.append(jnp.sum(dgc * gt, axis=0, keepdims=True))
        dcw_ref[...] += jnp.concatenate(rows + [jnp.zeros((8 - FFN_CONV, tc), F32)], axis=0)
        dcb_ref[...] += jnp.sum(dgc, axis=0, keepdims=True)

    tile = pl.BlockSpec((tt, tc), lambda j, s: (ntt - 1 - s, j))
    prev8 = pl.BlockSpec((8, tc), lambda j, s: (jnp.maximum((ntt - 1 - s) * (tt // 8) - 1, 0), j))
    return _pcall(
        body, name=name, grid=(F // tc, ntt),
        in_specs=[tile, prev8, tile, pl.BlockSpec((tt, D), lambda j, s: (ntt - 1 - s, 0)),
                  pl.BlockSpec((tc, D), lambda j, s: (j, 0)), pl.BlockSpec((FFN_CONV, tc), lambda j, s: (0, j)),
                  pl.BlockSpec((1, tc), lambda j, s: (0, j))],
        out_specs=[tile, tile, pl.BlockSpec((8, tc), lambda j, s: (0, j)), pl.BlockSpec((1, tc), lambda j, s: (0, j))],
        out_shape=[jax.ShapeDtypeStruct((T, F), BF16), jax.ShapeDtypeStruct((T, F), BF16),
                   jax.ShapeDtypeStruct((8, F), F32), jax.ShapeDtypeStruct((1, F), F32)],
        args=(g, g, u, dh, w_down, cw, cb), scratch=[pltpu.VMEM((8, tc), F32)], ride=ride)


def _sgu_norm(zv, g_ref, b_ref):
    v = _gelu(zv)
    mu = jnp.mean(v, axis=-1, keepdims=True)
    xc = v - mu
    rstd = lax.rsqrt(jnp.mean(xc * xc, axis=-1, keepdims=True) + NORM_EPS)
    xhat = xc * rstd
    return xhat, rstd, xhat * g_ref[...] + b_ref[...]


def _sgu_fwd(zc, ln_g, ln_b, wm, bmap, *, name):
    T = zc.shape[0]
    W = SGU_WIDTH
    tt = ROW_TILE
    nch = tt // CHUNK

    def body(z_ref, g_ref, b_ref, wm_ref, bm_ref, p_ref):
        u = _gelu(z_ref[:, :W])
        _, _, vn = _sgu_norm(z_ref[:, W:], g_ref, b_ref)
        vn = vn.astype(BF16)
        for n in range(nch):
            rows = slice(n * CHUNK, (n + 1) * CHUNK)
            for gi in range(SGU_GROUPS):
                cols = slice(gi * LANES, (gi + 1) * LANES)
                s = _dot_nn(wm_ref[gi], vn[rows, cols]) + bm_ref[:, cols]
                p_ref[rows, cols] = (u[rows, cols] * s).astype(BF16)

    const2 = lambda r, c: pl.BlockSpec((r, c), lambda i: (0, 0))
    return pl.pallas_call(
        body, name=name, grid=(T // tt,),
        in_specs=[pl.BlockSpec((tt, 2 * W), lambda i: (i, 0)), const2(1, W), const2(1, W),
                  pl.BlockSpec((SGU_GROUPS, CHUNK, CHUNK), lambda i: (0, 0, 0)), const2(CHUNK, W)],
        out_specs=pl.BlockSpec((tt, W), lambda i: (i, 0)),
        out_shape=jax.ShapeDtypeStruct((T, W), BF16),
        compiler_params=_cparams(("parallel",)),
    )(zc, ln_g, ln_b, wm, bmap)


def _sgu_bwd(zc, dp, ln_g, ln_b, wm, bmap, *, name, ride=None):
    T = zc.shape[0]
    W = SGU_WIDTH
    tt = ROW_TILE
    nch = tt // CHUNK
    nsteps = T // tt

    def body(z_ref, dp_ref, g_ref, b_ref, wm_ref, bm_ref, dz_ref, dg_ref, db_ref, dwm_ref, dbm_ref,
             s_scr, dvn_scr):
        step = pl.program_id(0)
        zu = z_ref[:, :W]
        zv = z_ref[:, W:]
        u, dgu = _gelu_and_grad(zu)
        xhat, rstd, vn = _sgu_norm(zv, g_ref, b_ref)
        vnb = vn.astype(BF16)
        dpf = dp_ref[...].astype(F32)
        ds = dpf * u

        @pl.when(step == 0)
        def _():
            dwm_ref[...] = jnp.zeros_like(dwm_ref)
            dbm_ref[...] = jnp.zeros_like(dbm_ref)

        for n in range(nch):
            rows = slice(n * CHUNK, (n + 1) * CHUNK)
            for gi in range(SGU_GROUPS):
                cols = slice(gi * LANES, (gi + 1) * LANES)
                s_scr[rows, cols] = _dot_nn(wm_ref[gi], vnb[rows, cols]) + bm_ref[:, cols]
                dsb = ds[rows, cols]
                dvn_scr[rows, cols] = _dot_tn(wm_ref[gi], dsb)
                dwm_ref[gi] += _dot_nt(dsb, vnb[rows, cols])
                dbm_ref[:, cols] += dsb
        dz_ref[:, :W] = (dpf * s_scr[...] * dgu).astype(BF16)
        dvn = dvn_scr[...]
        dxhat = dvn * g_ref[...]
        dv = rstd * (dxhat - jnp.mean(dxhat, axis=-1, keepdims=True)
                     - xhat * jnp.mean(dxhat * xhat, axis=-1, keepdims=True))
        _, dgv = _gelu_and_grad(zv)
        dz_ref[:, W:] = (dv * dgv).astype(BF16)
        dg_part = jnp.sum(dvn * xhat, axis=0, keepdims=True)
        db_part = jnp.sum(dvn, axis=0, keepdims=True)

        @pl.when(step == 0)
        def _():
            dg_ref[...] = dg_part
            db_ref[...] = db_part

        @pl.when(step > 0)
        def _():
            dg_ref[...] += dg_part
            db_ref[...] += db_part

        @pl.when(step == nsteps - 1)
        def _():
            for gi in range(SGU_GROUPS):
                cols = slice(gi * LANES, (gi + 1) * LANES)
                tot = jnp.sum(dbm_ref[:, cols], axis=1, keepdims=True)
                dbm_ref[:, cols] = jnp.broadcast_to(tot, (CHUNK, LANES))

    const2 = lambda r, c: pl.BlockSpec((r, c), lambda i: (0, 0))
    wspec = pl.BlockSpec((SGU_GROUPS, CHUNK, CHUNK), lambda i: (0, 0, 0))
    return _pcall(
        body, name=name, grid=(nsteps,),
        in_specs=[pl.BlockSpec((tt, 2 * W), lambda i: (i, 0)), pl.BlockSpec((tt, W), lambda i: (i, 0)),
                  const2(1, W), const2(1, W), wspec, const2(CHUNK, W)],
        out_specs=[pl.BlockSpec((tt, 2 * W), lambda i: (i, 0)), const2(1, W), const2(1, W), wspec, const2(CHUNK, W)],
        out_shape=[jax.ShapeDtypeStruct((T, 2 * W), BF16), jax.ShapeDtypeStruct((1, W), F32),
                   jax.ShapeDtypeStruct((1, W), F32), jax.ShapeDtypeStruct((SGU_GROUPS, CHUNK, CHUNK), F32),
                   jax.ShapeDtypeStruct((CHUNK, W), F32)],
        args=(zc, dp, ln_g, ln_b, wm, bmap), scratch=[pltpu.VMEM((tt, W), F32), pltpu.VMEM((tt, W), F32)], ride=ride)


def _rope_tables(positions):
    half = QK_ROPE // 2
    inv_freq = jnp.exp(-math.log(ROPE_BASE) * jnp.arange(half, dtype=F32) / half)
    ang = positions.reshape(-1).astype(F32)[:, None] * inv_freq
    cos = jnp.cos(ang)
    sin = jnp.sin(ang)
    n = ang.shape[0]
    tail = LANES - QK_NOPE - QK_ROPE
    cos_t = jnp.concatenate([jnp.ones((n, QK_NOPE), F32), cos, cos, jnp.ones((n, tail), F32)], axis=1)
    sin_t = jnp.concatenate([jnp.zeros((n, QK_NOPE), F32), -sin, sin, jnp.zeros((n, tail), F32)], axis=1)
    return cos_t, sin_t


SGU_GROUP_DIM = SGU_WIDTH // SGU_GROUPS
_O1, _O2, _O3, _O4 = Q_LORA, Q_LORA + KV_LORA, Q_LORA + KV_LORA + QK_ROPE, Q_LORA + KV_LORA + QK_ROPE + LRU_WIDTH
_A0, _A1, _A2 = 2 * LRU_WIDTH, 2 * LRU_WIDTH + Q_LORA, 2 * LRU_WIDTH + Q_LORA + KV_LORA
_A3 = _A2 + QK_NOPE
Z_Q_BLOCK, Z_KV_BLOCK, Z_KPE_BLOCK = _A0 // Q_LORA, _A1 // KV_LORA, _A2 // LANES


def _perm_w_in(w_in):
    zeros = lambda n: jnp.zeros((w_in.shape[0], n), w_in.dtype)
    return jnp.concatenate([w_in[:, _O3:_O4], w_in[:, _O4:], w_in[:, :_O1], w_in[:, _O1:_O2], zeros(QK_NOPE),
                            w_in[:, _O2:_O3], zeros(LANES - QK_NOPE - QK_ROPE)], axis=1)


def _unperm_w_in(w):
    return jnp.concatenate([w[:, _A0:_A1], w[:, _A1:_A2], w[:, _A3:_A3 + QK_ROPE], w[:, :LRU_WIDTH],
                            w[:, LRU_WIDTH:_A0]], axis=1)


def _head_blocks(w, d):
    r = w.shape[0]
    return jnp.pad(w.reshape(r, MLA_HEADS, d), ((0, 0), (0, 0), (0, LANES - d))).reshape(r, MLA_HEADS * LANES)


def _from_head_blocks(w, d):
    r = w.shape[0]
    return w.reshape(r, MLA_HEADS, LANES)[:, :, :d].reshape(r, MLA_HEADS * d)


def _split_kv(w_kv):
    r = w_kv.shape[0]
    w3 = w_kv.reshape(r, MLA_HEADS, QK_NOPE + V_HEAD)
    return _head_blocks(w3[:, :, :QK_NOPE].reshape(r, -1), QK_NOPE), w3[:, :, QK_NOPE:].reshape(r, -1)


def _join_kv(w_k, w_v):
    r = w_k.shape[0]
    return jnp.concatenate([_from_head_blocks(w_k, QK_NOPE).reshape(r, MLA_HEADS, QK_NOPE),
                            w_v.reshape(r, MLA_HEADS, V_HEAD)], axis=2).reshape(r, -1)


def _prep_small(w):
    p = {n: w[n] for n in w if n not in BIG}
    eye = jnp.eye(LRU_HEADS, dtype=F32)
    dense = lambda wg: (wg[:, :, None, :] * eye[:, None, :, None]).reshape(LRU_WIDTH, LRU_WIDTH).astype(BF16)
    p["wa_d"] = dense(w["ab_w_rg_a"][0])
    p["wx_d"] = dense(w["ab_w_rg_x"][0])
    causal = jnp.tril(jnp.ones((CHUNK, CHUNK), F32))
    p["wm"] = (w["c_w_s"][0] * causal).astype(BF16)
    p["bmap"] = jnp.repeat(w["c_b_s"][0].T, SGU_GROUP_DIM, axis=1)
    return p


def _prep_big(ab_w_in, ab_w_q_b, ab_w_kv_b):
    return {"w_in_p": _perm_w_in(ab_w_in).astype(BF16),
            "w_q_p": _head_blocks(ab_w_q_b, QK_NOPE + QK_ROPE).astype(BF16),
            "w_kv_p": jnp.concatenate(_split_kv(ab_w_kv_b), axis=1).astype(BF16)}


def _ffn_fwd(h, l, p, S, rides):
    hn = _rms_fwd(h, p["ffn_norm"][l], name=f"ffn{l}_norm")
    g = _mm(hn, p["ffn_gate_t"][l], tb=True, name=f"ffn{l}_gate", ride=rides.get(f"ffn{l}_gate"))
    u = _mm(hn, p["ffn_up_t"][l], tb=True, name=f"ffn{l}_up", ride=rides.get(f"ffn{l}_up"))
    out, act = _ffn_act_down(g, u, p["ffn_conv_w"][l], p["ffn_conv_b"][l][None], p["ffn_down"][l], h, S=S,
                             name=f"ffn{l}_down", ride=rides.get(f"ffn{l}_down"))
    return out, (hn, g, u, act)


def _ffn_bwd(dh, h_in, l, p, saved, S, rides, grads_ready, also_ready=None):
    hn, g, u, act = saved
    dw_down = _mm(act, dh, ta=True, out_dtype=BF16, name=f"ffn{l}_dwdown")
    dg, du, dcw, dcb = _ffn_act_bwd(g, u, dh, p["ffn_down"][l], p["ffn_conv_w"][l], p["ffn_conv_b"][l][None], S=S,
                                    name=f"ffn{l}_dactbwd", ride=rides.get(f"ffn{l}_dactbwd"))
    dhn = _mm(dg, p["ffn_gate_t"][l], name=f"ffn{l}_dhn_g")
    dhn = _mm(du, p["ffn_up_t"][l], res=dhn, out_dtype=BF16, name=f"ffn{l}_dhn_u")
    dw_gate_t = _mm(dg, hn, ta=True, out_dtype=BF16, name=f"ffn{l}_dwgate")
    dw_up_t = _mm(du, hn, ta=True, out_dtype=BF16, name=f"ffn{l}_dwup")
    grads_ready(l, {**(also_ready or {}), "ffn_gate_t": dw_gate_t, "ffn_up_t": dw_up_t, "ffn_down": dw_down})
    dh_in, dnorm = _rms_bwd(h_in, p["ffn_norm"][l], dhn, res=dh, name=f"ffn{l}_dnorm", ride=rides.get(f"ffn{l}_dnorm"))
    grads = dict(ffn_norm=dnorm[0], ffn_gate_t=dw_gate_t, ffn_up_t=dw_up_t, ffn_conv_w=dcw[:FFN_CONV],
                 ffn_conv_b=dcb[0], ffn_down=dw_down)
    return dh_in, grads


def _local_step(x, positions, target, p, rides=None, grads_ready=None):
    rides = {} if rides is None else rides
    grads_ready = grads_ready or (lambda layer, ready: None)
    B, S, D = x.shape
    T = B * S
    H = MLA_HEADS
    xf = x.reshape(T, D)
    tgt = target.reshape(T, D)
    cos, sin = _rope_tables(positions)

    hn0 = _rms_fwd(xf, p["ab_norm"][0], name="ab_norm", ride=rides.get("ab_norm"))
    z = _mm(hn0, p["w_in_p"], name="ab_in")
    cqn = _rms_fwd(z, p["ab_q_norm"][0], cb=Z_Q_BLOCK, name="q_norm")
    ckvn = _rms_fwd(z, p["ab_kv_norm"][0], cb=Z_KV_BLOCK, name="kv_norm")
    q = _mm(cqn, p["w_q_p"], name="q_up")
    kv = _mm(ckvn, p["w_kv_p"], out_dtype=BF16, name="kv_up")
    qs = _rope_q(q, cos, sin, name="q_rope")
    kk = _key_blocks(kv, z, cos, sin, kpe_block=Z_KPE_BLOCK, name="k_rope")
    att = dict(B=B, S=S, v_block0=H)
    o, lse = _attn_fwd(qs, kk, kv, name="attn_fwd", ride=rides.get("attn_fwd"), **att)
    lru_par = (p["ab_conv_w"][0], p["ab_conv_b"], p["wa_d"], p["ab_b_rg_a"], p["wx_d"], p["ab_b_rg_x"], p["ab_lambda"])
    y_lru, hs = _lru_fwd(z, *lru_par, S=S, name="lru_fwd", ride=rides.get("lru_fwd"))
    n_att = H * V_HEAD
    w_out_a, w_out_b = p["ab_w_out"][:n_att], p["ab_w_out"][n_att:]
    h1 = _mm(y_lru, w_out_b, res=_mm(o, w_out_a, res=xf, name="ab_out_a"), name="ab_out_b")
    h2, ffn0 = _ffn_fwd(h1, 0, p, S, rides)

    hn2 = _rms_fwd(h2, p["c_norm"][0], name="c_norm")
    zc = _mm(hn2, p["c_w_in_t"], tb=True, name="c_in")
    pg = _sgu_fwd(zc, p["c_ln_g"], p["c_ln_b"], p["wm"], p["bmap"], name="sgu_fwd")
    h3 = _mm(pg, p["c_w_out"], res=h2, name="c_out")
    h4, ffn1 = _ffn_fwd(h3, 1, p, S, rides)

    loss_row, dh4, dfinal = _final_fwd_bwd(h4, p["final_norm"], tgt, name="final")

    dh3, g_ffn1 = _ffn_bwd(dh4, h3, 1, p, ffn1, S, rides, grads_ready)
    dpg = _mm(dh3, p["c_w_out"], tb=True, out_dtype=BF16, name="c_dp")
    dw_c_out = _mm(pg, dh3, ta=True, out_dtype=BF16, name="c_dwout")
    dzc, dlng, dlnb, dwm, dbm = _sgu_bwd(zc, dpg, p["c_ln_g"], p["c_ln_b"], p["wm"], p["bmap"], name="sgu_bwd",
                                         ride=rides.get("sgu_bwd"))
    dhn2 = _mm(dzc, p["c_w_in_t"], out_dtype=BF16, name="c_dhn")
    dw_c_in_t = _mm(dzc, hn2, ta=True, out_dtype=BF16, name="c_dwin")
    dh2, dcnorm = _rms_bwd(h2, p["c_norm"][0], dhn2, res=dh3, name="c_dnorm")
    dh1, g_ffn0 = _ffn_bwd(dh2, h1, 0, p, ffn0, S, rides, grads_ready, {"c_w_in_t": dw_c_in_t, "c_w_out": dw_c_out})

    do = _mm(dh1, w_out_a, tb=True, name="ab_do")
    dy_lru = _mm(dh1, w_out_b, tb=True, out_dtype=BF16, name="ab_dylru")
    dw_out = jnp.concatenate([_mm(o, dh1, ta=True, out_dtype=BF16, name="ab_dwout_a"),
                              _mm(y_lru, dh1, ta=True, out_dtype=BF16, name="ab_dwout_b")], axis=0)
    dq, delta = _attn_dq(qs, kk, kv, o, lse, do, name="attn_dq", ride=rides.get("attn_dq"), **att)
    nq = S // min(ATT_BLOCK, S)
    rows = lambda a: a.reshape(B, H, nq, S // nq)
    dk, dv = _attn_dkv(qs, kk, kv, rows(lse), rows(delta), do, name="attn_dkv", ride=rides.get("attn_dkv"), **att)
    dq_full = _rope_q_bwd(dq, cos, sin, name="q_rope_bwd")
    dkr = _key_rope_bwd(dk, cos, sin, name="k_rope_bwd")
    n_key = H * LANES
    w_k_p, w_v_p = p["w_kv_p"][:, :n_key], p["w_kv_p"][:, n_key:]
    dcqn = _mm(dq_full, p["w_q_p"], tb=True, name="q_dlat")
    dw_q_p = _mm(cqn, dq_full, ta=True, out_dtype=BF16, name="q_dw")
    dckvn = _mm(dv, w_v_p, tb=True, res=_mm(dk, w_k_p, tb=True, name="k_dlat"), name="v_dlat")
    dw_k_p = _mm(ckvn, dk, ta=True, out_dtype=BF16, name="k_dw")
    dw_v_p = _mm(ckvn, dv, ta=True, out_dtype=BF16, name="v_dw")
    dcq, dqnorm = _rms_bwd(z, p["ab_q_norm"][0], dcqn, cb=Z_Q_BLOCK, out_dtype=BF16, name="q_dnorm")
    dckv, dkvnorm = _rms_bwd(z, p["ab_kv_norm"][0], dckvn, cb=Z_KV_BLOCK, out_dtype=BF16, name="kv_dnorm")
    dxl, dgate, dcw, dcb, dwa, dba, dwx, dbx, dlam = _lru_bwd(z, hs, dy_lru, *lru_par, S=S, name="lru_bwd")
    dz = jnp.concatenate([dxl, dgate, dcq, dckv, dkr], axis=1)
    dhn0 = _mm(dz, p["w_in_p"], tb=True, out_dtype=BF16, name="ab_dhn")
    dw_in_p = _mm(hn0, dz, ta=True, out_dtype=BF16, name="ab_dwin")
    dx, dabnorm = _rms_bwd(xf, p["ab_norm"][0], dhn0, res=dh1, name="ab_dnorm")

    blocks = lambda dd: jnp.stack([dd[i * LRU_BLOCK:(i + 1) * LRU_BLOCK, i * LRU_BLOCK:(i + 1) * LRU_BLOCK]
                                   for i in range(LRU_HEADS)])
    causal = jnp.tril(jnp.ones((CHUNK, CHUNK), F32))
    grads = {
        "ab_norm": dabnorm, "w_in_p": dw_in_p, "ab_q_norm": dqnorm, "w_q_p": dw_q_p,
        "ab_kv_norm": dkvnorm, "w_k_p": dw_k_p, "w_v_p": dw_v_p, "ab_conv_w": dcw[:LRU_CONV][None], "ab_conv_b": dcb,
        "ab_w_rg_a": blocks(dwa)[None], "ab_b_rg_a": dba, "ab_w_rg_x": blocks(dwx)[None], "ab_b_rg_x": dbx,
        "ab_lambda": dlam, "ab_w_out": dw_out,
        "c_norm": dcnorm, "c_w_in_t": dw_c_in_t, "c_ln_g": dlng, "c_ln_b": dlnb,
        "c_w_s": (dwm * causal)[None], "c_b_s": dbm[:, ::SGU_GROUP_DIM].T[None], "c_w_out": dw_c_out,
        "final_norm": dfinal[0],
    }
    for name in ("ffn_norm", "ffn_conv_w", "ffn_conv_b"):
        grads[name] = jnp.stack([g_ffn0[name], g_ffn1[name]])
    for name in ("ffn_gate_t", "ffn_up_t", "ffn_down"):
        grads[name] = [g_ffn0[name], g_ffn1[name]]
    return loss_row, dx.reshape(B, S, D), grads


ANY = pl.BlockSpec(memory_space=pl.ANY)


def _place():
    x, y, c = lax.axis_index("x"), lax.axis_index("y"), lax.axis_index("c")
    chips = [(1 - x, y), (x, 1 - y), (1 - x, 1 - y)]
    return x, y, c, 2 * x + y, (x, y, 1 - c), chips


def _remote(src, dst, send_sems, recv_sems, k, to):
    return pltpu.make_async_remote_copy(src_ref=src, dst_ref=dst, send_sem=send_sems.at[k], recv_sem=recv_sems.at[k],
                                        device_id=to, device_id_type=MESH)


class _Exchange:
    def __init__(self, arrs, out_shapes, n_sems, start, finish):
        self.arrs, self.out_shapes, self.n_sems, self.start, self.finish = list(arrs), out_shapes, n_sems, start, finish

    @property
    def in_specs(self):
        return [ANY] * len(self.arrs)

    @property
    def out_specs(self):
        return [ANY] * len(self.out_shapes)

    @property
    def scratch(self):
        return [pltpu.SemaphoreType.DMA((self.n_sems,)), pltpu.SemaphoreType.DMA((self.n_sems,))]

    def split(self, refs):
        n = len(self.arrs)
        return refs[:n], refs[n:n + len(self.out_shapes)], refs[-2], refs[-1]

    def run(self, name):
        def body(*refs):
            parts = self.split(refs)
            self.start(*parts)
            self.finish(*parts)

        return pl.pallas_call(body, name=name, in_specs=self.in_specs, out_specs=self.out_specs,
                              out_shape=self.out_shapes, scratch_shapes=self.scratch)(*self.arrs)


def _put(buf, piece, idx, axis):
    return lax.dynamic_update_slice_in_dim(buf, jnp.expand_dims(piece, axis).astype(buf.dtype), idx, axis)


def _all_gather(arrs):
    n = len(arrs)

    def start(ins, outs, send_sems, recv_sems):
        x, y, c, j, sib, chips = _place()
        for i in range(n):
            for k, (cx, cy) in enumerate(chips):
                _remote(ins[i].at[:, c], outs[i].at[:, j, c], send_sems, recv_sems, 6 * i + k, (cx, cy, c)).start()

    def finish(ins, outs, send_sems, recv_sems):
        x, y, c, j, sib, chips = _place()
        passed = []
        for i in range(n):
            for k, (cx, cy) in enumerate(chips):
                got = outs[i].at[:, 2 * cx + cy, c]
                _remote(got, got, send_sems, recv_sems, 6 * i + k, (cx, cy, c)).wait_recv()
                cp = _remote(got, got, send_sems, recv_sems, 6 * i + 3 + k, sib)
                cp.start()
                passed.append(cp)
        for i in range(n):
            for k, (cx, cy) in enumerate(chips):
                got = outs[i].at[:, 2 * cx + cy, 1 - c]
                _remote(got, got, send_sems, recv_sems, 6 * i + 3 + k, sib).wait_recv()
                _remote(ins[i].at[:, c], ins[i].at[:, c], send_sems, recv_sems, 6 * i + k, sib).wait_send()
        for cp in passed:
            cp.wait_send()

    shapes = [jax.ShapeDtypeStruct((a.shape[0], N_CHIPS) + a.shape[1:], a.dtype) for a in arrs]
    return _Exchange(arrs, shapes, 6 * n, start, finish)


class _Offset:
    def __init__(self, sems, k0):
        self.sems, self.k0 = sems, k0

    @property
    def at(self):
        return self

    def __getitem__(self, k):
        return self.sems.at[self.k0 + k]


def _merge(a, b):
    n_in, n_out = len(a.arrs), len(a.out_shapes)

    def both(fa, fb):
        def f(ins, outs, send_sems, recv_sems):
            fa(ins[:n_in], outs[:n_out], send_sems, recv_sems)
            fb(ins[n_in:], outs[n_out:], _Offset(send_sems, a.n_sems), _Offset(recv_sems, a.n_sems))
        return f

    return _Exchange(a.arrs + b.arrs, a.out_shapes + b.out_shapes, a.n_sems + b.n_sems,
                     both(a.start, b.start), both(a.finish, b.finish))


def _pair_swap(arrs):
    n = len(arrs)

    def start(ins, outs, send_sems, recv_sems):
        x, y, c, j, sib, chips = _place()
        for i in range(n):
            _remote(ins[i].at[:, 1 - c], outs[i], send_sems, recv_sems, i, sib).start()

    def finish(ins, outs, send_sems, recv_sems):
        x, y, c, j, sib, chips = _place()
        for i in range(n):
            _remote(ins[i].at[:, 1 - c], outs[i], send_sems, recv_sems, i, sib).wait()

    shapes = [jax.ShapeDtypeStruct((a.shape[0],) + a.shape[2:], a.dtype) for a in arrs]
    return _Exchange(arrs, shapes, n, start, finish)


def _pair_send(arrs):
    n = len(arrs)

    def start(ins, outs, send_sems, recv_sems):
        x, y, c, j, sib, chips = _place()
        for i in range(n):
            _remote(ins[i], outs[i], send_sems, recv_sems, i, sib).start()

    def finish(ins, outs, send_sems, recv_sems):
        x, y, c, j, sib, chips = _place()
        for i in range(n):
            _remote(ins[i], outs[i], send_sems, recv_sems, i, sib).wait()

    shapes = [jax.ShapeDtypeStruct(a.shape, a.dtype) for a in arrs]
    return _Exchange(arrs, shapes, n, start, finish)


def _chip_exchange(arrs, *, scatter):
    n = len(arrs)

    def copies(ins, outs, send_sems, recv_sems):
        x, y, c, j, sib, chips = _place()
        return [(_remote(ins[i].at[2 * cx + cy] if scatter else ins[i], outs[i].at[j], send_sems, recv_sems,
                         3 * i + k, (cx, cy, c)),
                 _remote(outs[i].at[2 * cx + cy], outs[i].at[2 * cx + cy], send_sems, recv_sems, 3 * i + k, (cx, cy, c)))
                for i in range(n) for k, (cx, cy) in enumerate(chips)]

    def start(*refs):
        for out, _ in copies(*refs):
            out.start()

    def finish(*refs):
        for out, back in copies(*refs):
            back.wait_recv()
            out.wait_send()

    shapes = [jax.ShapeDtypeStruct((N_CHIPS,) + a.shape[-2:], a.dtype) for a in arrs]
    return _Exchange(arrs, shapes, 3 * n, start, finish)


FLAT_ROWS = 512


def _add2(a, b, *, out_dtype, name):
    n, R, L = a.shape
    tr = _tile(R, FLAT_ROWS, 16)

    def body(a_ref, b_ref, o_ref):
        o_ref[...] = (a_ref[...].astype(F32) + b_ref[...].astype(F32)).astype(out_dtype)

    spec = pl.BlockSpec((n, tr, L), lambda i: (0, i, 0))
    return pl.pallas_call(
        body, name=name, grid=(R // tr,), in_specs=[spec, spec], out_specs=spec,
        out_shape=jax.ShapeDtypeStruct(a.shape, out_dtype), compiler_params=_cparams(("parallel",)),
    )(a, b)


def _sum_slots(buf, *, name):
    n, R, L = buf.shape
    tr = _tile(R, FLAT_ROWS, 16)

    def body(b_ref, o_ref):
        acc = b_ref[0].astype(F32)
        for k in range(1, n):
            acc = acc + b_ref[k].astype(F32)
        o_ref[...] = acc

    return pl.pallas_call(
        body, name=name, grid=(R // tr,), in_specs=[pl.BlockSpec((n, tr, L), lambda i: (0, i, 0))],
        out_specs=pl.BlockSpec((tr, L), lambda i: (i, 0)),
        out_shape=jax.ShapeDtypeStruct((R, L), F32), compiler_params=_cparams(("parallel",)),
    )(buf)


def _adamw_update(w, g, m, v):
    c1 = 1.0 - ADAM_B1 ** ADAM_STEP
    c2 = 1.0 - ADAM_B2 ** ADAM_STEP
    m = ADAM_B1 * m + (1.0 - ADAM_B1) * g
    v = ADAM_B2 * v + (1.0 - ADAM_B2) * (g * g)
    return -ADAM_LR * ((m / c1) / (jnp.sqrt(v / c2) + ADAM_EPS) + ADAM_WD * w), m, v


def _adamw_halves(w, m, v, own, other, *, name):
    NL, R, L = w.shape
    h = R // 2
    tr = _tile(h, FLAT_ROWS, 16)
    nt = h // tr

    def body(*refs):
        w_ref, m_ref, v_ref = refs[:3]
        own_refs, other_refs = refs[3:3 + NL], refs[3 + NL:3 + 2 * NL]
        d_ref, nm_ref, nv_ref, g_ref = refs[3 + 2 * NL:]
        layer, half = pl.program_id(0), pl.program_id(1)
        mine = half == lax.axis_index("c")
        g = jnp.where(mine, own_refs[0][...], other_refs[0][...])
        for l in range(1, NL):
            g = jnp.where(layer == l, jnp.where(mine, own_refs[l][...], other_refs[l][...]), g)
        d, mm, vv = _adamw_update(w_ref[0], g, m_ref[0], v_ref[0])
        d_ref[0], nm_ref[0], nv_ref[0], g_ref[0] = d, mm, vv, g

    spec = pl.BlockSpec((1, tr, L), lambda l, hh, i: (l, hh * nt + i, 0))
    part = pl.BlockSpec((tr, L), lambda l, hh, i: (i, 0))
    sh = jax.ShapeDtypeStruct((NL, R, L), F32)
    return pl.pallas_call(
        body, name=name, grid=(NL, 2, nt), in_specs=[spec] * 3 + [part] * (2 * NL), out_specs=[spec] * 4,
        out_shape=[sh] * 4, compiler_params=_cparams(("parallel", "parallel", "parallel")),
    )(w, m, v, *own, *other)


def _adamw(w, g, m, v, *, name):
    NL, R, L = w.shape
    tr = _tile(R, FLAT_ROWS, 16)

    def body(w_ref, g_ref, m_ref, v_ref, d_ref, nm_ref, nv_ref):
        d_ref[...], nm_ref[...], nv_ref[...] = _adamw_update(w_ref[...], g_ref[...], m_ref[...], v_ref[...])

    spec = pl.BlockSpec((1, tr, L), lambda l, i: (l, i, 0))
    sh = jax.ShapeDtypeStruct((NL, R, L), F32)
    return pl.pallas_call(
        body, name=name, grid=(NL, R // tr), in_specs=[spec] * 4, out_specs=[spec] * 3, out_shape=[sh] * 3,
        compiler_params=_cparams(("parallel", "parallel")),
    )(w, g, m, v)


WEIGHT_NAMES = ["ab_norm", "ab_w_in", "ab_q_norm", "ab_w_q_b", "ab_kv_norm", "ab_w_kv_b", "ab_conv_w", "ab_conv_b",
                "ab_w_rg_a", "ab_b_rg_a", "ab_w_rg_x", "ab_b_rg_x", "ab_lambda", "ab_w_out", "c_norm", "c_w_in",
                "c_ln_g", "c_ln_b", "c_w_s", "c_b_s", "c_w_out", "ffn_norm", "ffn_w_gate", "ffn_w_up", "ffn_conv_w",
                "ffn_conv_b", "ffn_w_down", "final_norm"]
BIG = {"ab_w_in": 2, "ab_w_q_b": 2, "ab_w_kv_b": 2, "ab_w_out": 1, "c_w_in": 2, "c_w_out": 1,
       "ffn_w_gate": 2, "ffn_w_up": 2, "ffn_w_down": 1}
SMALL_SHARDED = {"ab_conv_w": 2, "c_norm": 1, "c_ln_g": 1, "c_ln_b": 1, "ffn_conv_w": 2}
SMALL_REPLICATED = [n for n in WEIGHT_NAMES if n not in BIG and n not in SMALL_SHARDED]


def _rows(n_elems, mult):
    r = -(-n_elems // LANES)
    return -(-r // mult) * mult


def _flat(parts, rows):
    flat = jnp.concatenate([a.reshape(-1) for a in parts])
    return jnp.pad(flat, (0, rows * LANES - flat.shape[0])).reshape(rows, LANES)


def _unflat(flat, shapes):
    flat = flat.reshape(-1)
    out, off = [], 0
    for s in shapes:
        n = math.prod(s)
        out.append(flat[off:off + n].reshape(s))
        off += n
    return out


def _join_shards(a, axis):
    a = jnp.moveaxis(a, 0, axis)
    return a.reshape(a.shape[:axis] + (a.shape[axis] * a.shape[axis + 1],) + a.shape[axis + 2:])


def kernel(x, positions, ab_norm, ab_w_in, ab_q_norm, ab_w_q_b, ab_kv_norm, ab_w_kv_b, ab_conv_w, ab_conv_b, ab_w_rg_a, ab_b_rg_a, ab_w_rg_x, ab_b_rg_x, ab_lambda, ab_w_out, c_norm, c_w_in, c_ln_g, c_ln_b, c_w_s, c_b_s, c_w_out, ffn_norm, ffn_w_gate, ffn_w_up, ffn_conv_w, ffn_conv_b, ffn_w_down, final_norm, loss_target, m_ab_norm, m_ab_w_in, m_ab_q_norm, m_ab_w_q_b, m_ab_kv_norm, m_ab_w_kv_b, m_ab_conv_w, m_ab_conv_b, m_ab_w_rg_a, m_ab_b_rg_a, m_ab_w_rg_x, m_ab_b_rg_x, m_ab_lambda, m_ab_w_out, m_c_norm, m_c_w_in, m_c_ln_g, m_c_ln_b, m_c_w_s, m_c_b_s, m_c_w_out, m_ffn_norm, m_ffn_w_gate, m_ffn_w_up, m_ffn_conv_w, m_ffn_conv_b, m_ffn_w_down, m_final_norm, v_ab_norm, v_ab_w_in, v_ab_q_norm, v_ab_w_q_b, v_ab_kv_norm, v_ab_w_kv_b, v_ab_conv_w, v_ab_conv_b, v_ab_w_rg_a, v_ab_b_rg_a, v_ab_w_rg_x, v_ab_b_rg_x, v_ab_lambda, v_ab_w_out, v_c_norm, v_c_w_in, v_c_ln_g, v_c_ln_b, v_c_w_s, v_c_b_s, v_c_w_out, v_ffn_norm, v_ffn_w_gate, v_ffn_w_up, v_ffn_conv_w, v_ffn_conv_b, v_ffn_w_down, v_final_norm):
    given = dict(locals())
    w = {n: given[n] for n in WEIGHT_NAMES}
    m = {n: given["m_" + n] for n in WEIGHT_NAMES}
    v = {n: given["v_" + n] for n in WEIGHT_NAMES}
    c = lax.axis_index("c")
    chip = 2 * lax.axis_index("x") + lax.axis_index("y")

    halves = lambda a: a.reshape(a.shape[0], 2, a.shape[1] // 2, a.shape[2])
    tr = lambda a: jnp.swapaxes(a, 1, 2)
    send = {"ab_w_in": w["ab_w_in"], "ab_w_q_b": w["ab_w_q_b"], "ab_w_kv_b": w["ab_w_kv_b"], "ab_w_out": w["ab_w_out"],
            "c_w_in": tr(w["c_w_in"]), "c_w_out": w["c_w_out"], "ffn_w_gate": tr(w["ffn_w_gate"]),
            "ffn_w_up": tr(w["ffn_w_up"]), "ffn_w_down": w["ffn_w_down"]}
    small_rows = _rows(sum(w[n].size for n in SMALL_SHARDED), 16)
    small_sh = _flat([w[n] for n in SMALL_SHARDED], small_rows).reshape(1, 2, small_rows // 2, LANES)
    first_names = ["ab_w_in", "ab_w_q_b", "ab_w_kv_b", "ab_w_out"]
    mine = {n: halves(send[n].astype(BF16)) for n in BIG}

    def put_own(own, arrived):
        a = _put(arrived, own, chip, 1)
        return a.reshape(a.shape[0], -1, a.shape[-1])

    p = {"ab_norm": w["ab_norm"], "ffn_gate_t": {}, "ffn_up_t": {}, "ffn_down": {}}
    first = [mine[n] for n in first_names] + [small_sh]

    def first_arrived(got):
        full = {n: put_own(o, a) for n, o, a in zip(first_names + ["small"], first, got)}
        unshard = lambda a: jnp.swapaxes(a.reshape(N_CHIPS, -1, a.shape[-1]), 0, 1).reshape(-1, N_CHIPS * a.shape[-1])
        p.update(_prep_big(unshard(full["ab_w_in"][0]), unshard(full["ab_w_q_b"][0]), unshard(full["ab_w_kv_b"][0])))
        p["ab_w_out"] = full["ab_w_out"][0]
        small_full = dict(w)
        off = 0
        small_got = full["small"].reshape(N_CHIPS, -1)
        for n, ax in SMALL_SHARDED.items():
            seg = small_got[:, off:off + w[n].size].reshape((N_CHIPS,) + w[n].shape)
            small_full[n] = _join_shards(seg, ax)
            off += w[n].size
        p.update(_prep_small(small_full))

    def weights_ride(parts):
        def sink(arrived):
            for (own, setter), a in zip(parts, arrived):
                setter(put_own(own, a)[0])
        return _all_gather([own for own, _ in parts]), sink

    ffn_keys = {"ffn_gate_t": "ffn_w_gate", "ffn_up_t": "ffn_w_up", "ffn_down": "ffn_w_down"}
    ffn_part = lambda key, l: (mine[ffn_keys[key]][l:l + 1], functools.partial(p[key].__setitem__, l))
    rides = {
        "ab_norm": (_all_gather(first), first_arrived),
        "attn_fwd": weights_ride([ffn_part("ffn_gate_t", 0), ffn_part("ffn_up_t", 0)]),
        "lru_fwd": weights_ride([ffn_part("ffn_down", 0)]),
        "ffn0_gate": weights_ride([ffn_part("ffn_gate_t", 1)]),
        "ffn0_up": weights_ride([ffn_part("ffn_up_t", 1)]),
        "ffn0_down": weights_ride([ffn_part("ffn_down", 1),
                                   (mine["c_w_in"], functools.partial(p.__setitem__, "c_w_in_t")),
                                   (mine["c_w_out"], functools.partial(p.__setitem__, "c_w_out"))]),
    }

    def chip_sums(pair, arrived, tag):
        own = [lax.dynamic_index_in_dim(a, chip, axis=0, keepdims=False) for a in pair]
        return [_sum_slots(_put(a, o, chip, 0), name=f"grad_chip_sum_{tag}{i}") for i, (a, o) in enumerate(zip(arrived, own))]

    half_of = {}

    def grads_ready(layer, ready):
        if layer == 1:
            named = {"gate1": ready["ffn_gate_t"], "up1": ready["ffn_up_t"], "down1": ready["ffn_down"]}
            hosts = {"sgu_bwd": ["down1"], "ffn0_dactbwd": ["gate1", "up1"]}
        else:
            named = {"c_in": ready["c_w_in_t"], "c_out": ready["c_w_out"], "gate0": ready["ffn_gate_t"],
                     "up0": ready["ffn_up_t"], "down0": ready["ffn_down"]}
            hosts = {"attn_dq": ["c_in", "c_out", "down0"], "attn_dkv": ["gate0", "up0"]}
        tag = f"f{layer}"
        sharded = [a.reshape(N_CHIPS, 2, -1, a.shape[-1]) for a in named.values()]

        def paired(from_sib):
            own = [lax.dynamic_index_in_dim(a, c, axis=1, keepdims=False) for a in sharded]
            pair = {k: _add2(a, b, out_dtype=BF16, name=f"grad_pair_add_{tag}{i}")
                    for i, (k, a, b) in enumerate(zip(named, own, from_sib))}
            for kernel_name, keys in hosts.items():
                def sink(arrived, keys=keys, kernel_name=kernel_name):
                    half_of.update(zip(keys, chip_sums([pair[k] for k in keys], arrived, f"{tag}_{kernel_name}")))
                rides[kernel_name] = (_chip_exchange([pair[k] for k in keys], scatter=True), sink)

        rides[f"ffn{layer}_dnorm"] = (_pair_swap(sharded), paired)

    loss_row, grad_x, g = _local_step(x, positions, loss_target, p, rides, grads_ready)

    cols = lambda a, n: jnp.swapaxes(a.reshape(a.shape[0], N_CHIPS, n), 0, 1)
    n_in, n_q, n_kv = w["ab_w_in"].shape[2], w["ab_w_q_b"].shape[2], w["ab_w_kv_b"].shape[2]
    small_names = SMALL_REPLICATED + list(SMALL_SHARDED)
    rs = _rows(sum(g[n].size for n in small_names) + LANES, FLAT_ROWS)
    small = _flat([loss_row] + [g[n] for n in small_names], rs)
    slot = (jnp.arange(2) == c)[:, None, None]
    last = [cols(_unperm_w_in(g["w_in_p"]), n_in), cols(_from_head_blocks(g["w_q_p"], QK_NOPE + QK_ROPE), n_q),
            cols(_join_kv(g["w_k_p"], g["w_v_p"]), n_kv), g["ab_w_out"]]
    last = [a.reshape(N_CHIPS, 2, -1, a.shape[-1]) for a in last]
    *from_sib, small_sib = _merge(_pair_swap(last), _pair_send([small])).run("tail_pair")
    own = [lax.dynamic_index_in_dim(a, c, axis=1, keepdims=False) for a in last]
    pair = [_add2(a, b, out_dtype=BF16, name=f"grad_pair_add_b{i}") for i, (a, b) in enumerate(zip(own, from_sib))]
    pair_small = _sum_slots(jnp.where(slot, small[None], small_sib[None]), name="small_pair_sum")
    my_small = lax.dynamic_index_in_dim(pair_small.reshape(2, rs // 2, LANES), c, axis=0, keepdims=False)
    *arrived, all_small = _merge(_chip_exchange(pair, scatter=True), _chip_exchange([my_small], scatter=False)).run("tail_chip")
    half_of.update(zip(["in", "q", "kv", "out"], chip_sums(pair, arrived, "b")))
    half_of["small"] = _sum_slots(_put(all_small, my_small, chip, 0), name="small_chip_sum")
    keys = ("in", "q", "kv", "out", "c_in", "c_out", "gate0", "gate1", "up0", "up1", "down0", "down1", "small")
    other_half = dict(zip(keys, _pair_send([half_of[k] for k in keys]).run("grad_pair_share")))
    small_sum = jnp.where(slot, half_of["small"][None], other_half["small"][None]).reshape(rs, LANES)
    whole = lambda k: jnp.where(slot, half_of[k][None], other_half[k][None]).reshape(-1, half_of[k].shape[-1])
    grads_t = {"ab_w_in": whole("in").T[None], "ab_w_q_b": whole("q").T[None]}
    grads = {"ab_w_kv_b": whole("kv")[None], "c_w_in": whole("c_in").T[None], **{n: tr(a) for n, a in grads_t.items()}}
    by_halves = {"ab_w_out": (("out",), False), "c_w_out": (("c_out",), False), "ffn_w_down": (("down0", "down1"), False),
                 "ffn_w_gate": (("gate0", "gate1"), True), "ffn_w_up": (("up0", "up1"), True)}

    small_parts = _unflat(small_sum, [(1, LANES)] + [g[n].shape for n in small_names])
    loss = small_parts[0][0, 0]
    for n, a in zip(small_names, small_parts[1:]):
        if n in SMALL_SHARDED:
            ax = SMALL_SHARDED[n]
            a = lax.dynamic_slice_in_dim(a, chip * w[n].shape[ax], w[n].shape[ax], axis=ax)
        grads[n] = a.reshape(w[n].shape)

    delta, new_m, new_v = {}, {}, {}
    for n in BIG:
        if n in by_halves:
            ks, transposed = by_halves[n]
            view = tr if transposed else (lambda a: a)
            out = _adamw_halves(view(w[n]), view(m[n]), view(v[n]), [half_of[k] for k in ks], [other_half[k] for k in ks],
                                name=f"adamw_{n}")
            delta[n], new_m[n], new_v[n], grads[n] = (view(a) for a in out)
        elif n in grads_t:
            out = _adamw(tr(w[n]), grads_t[n], tr(m[n]), tr(v[n]), name=f"adamw_{n}")
            delta[n], new_m[n], new_v[n] = (tr(a) for a in out)
        else:
            delta[n], new_m[n], new_v[n] = _adamw(w[n], grads[n], m[n], v[n], name=f"adamw_{n}")
    small_all = [n for n in WEIGHT_NAMES if n not in BIG]
    ra = _rows(sum(w[n].size for n in small_all), FLAT_ROWS)
    pack = lambda d: _flat([d[n] for n in small_all], ra)[None]
    out = _adamw(pack(w), pack(grads), pack(m), pack(v), name="adamw_small")
    shapes = [w[n].shape for n in small_all]
    for d, flat in zip((delta, new_m, new_v), out):
        d.update(zip(small_all, _unflat(flat, shapes)))
    return (loss, grad_x, *[grads[n] for n in WEIGHT_NAMES], *[delta[n] for n in WEIGHT_NAMES],
            *[new_m[n] for n in WEIGHT_NAMES], *[new_v[n] for n in WEIGHT_NAMES])
```

```python
import functools
import math

import jax
import jax.numpy as jnp
from jax import lax
from jax.experimental import pallas as pl
from jax.experimental.pallas import tpu as pltpu

F32 = jnp.float32
BF16 = jnp.bfloat16
MESH = pl.DeviceIdType.MESH

D_MODEL = 1024
MLA_HEADS = 8
Q_LORA = 256
KV_LORA = 128
QK_NOPE = 64
QK_ROPE = 32
V_HEAD = 64
LRU_WIDTH = 512
LRU_HEADS = 8
LRU_BLOCK = 64
LRU_CONV = 4
LRU_C = 8.0
CHUNK = 128
SGU_GROUPS = 8
SGU_WIDTH = 1024
D_FF = 2816
FFN_CONV = 3
NORM_EPS = 1e-6
ROPE_BASE = 10000.0
AB_IN_PAD = 1536
ADAM_LR = 0.001
ADAM_B1 = 0.9
ADAM_B2 = 0.999
ADAM_EPS = 1e-08
ADAM_WD = 0.01
ADAM_STEP = 10

N_CHIPS = 4
LANES = 128
VMEM_LIMIT = 56 * 1024 * 1024
ROW_TILE = 256
NORM_TILE = 1024
MM_TM, MM_TN, MM_TK = 1024, 1536, 2816
MM_TM_T, MM_TK_T = 1408, 1024
GELU_C = math.sqrt(2.0 / math.pi)


def _cparams(sem):
    return pltpu.CompilerParams(dimension_semantics=sem, vmem_limit_bytes=VMEM_LIMIT)


def _tile(n, target, mult=LANES):
    t = (min(n, target) // mult) * mult
    while t >= mult:
        if n % t == 0:
            return t
        t -= mult
    return n


GELU_K = GELU_C * 0.044715


def _gelu(x):
    t = jnp.tanh(x * (GELU_C + GELU_K * (x * x)))
    hx = 0.5 * x
    return hx + hx * t


def _gelu_and_grad(x):
    x2 = x * x
    t = jnp.tanh(x * (GELU_C + GELU_K * x2))
    hx = 0.5 * x
    dg = (0.5 + 0.5 * t) + (hx * (1.0 - t * t)) * (GELU_C + (3.0 * GELU_K) * x2)
    return hx + hx * t, dg


def _sigmoid(x):
    return 1.0 / (1.0 + jnp.exp(-x))


def _shift_rows(x, d, fill_rows):
    ext = jnp.concatenate([fill_rows, x], axis=0)
    return pltpu.roll(ext, d, 0)[8:]


def _shift_rows_up(x, d, fill_rows):
    n = x.shape[0]
    ext = jnp.concatenate([x, fill_rows], axis=0)
    return pltpu.roll(ext, n + 8 - d, 0)[:n]


def _dot(a, b, dims):
    return lax.dot_general(a.astype(BF16), b.astype(BF16), (dims, ((), ())), preferred_element_type=F32)


def _dot_nn(a, b):
    return _dot(a, b, ((1,), (0,)))


def _dot_nt(a, b):
    return _dot(a, b, ((1,), (1,)))


def _dot_tn(a, b):
    return _dot(a, b, ((0,), (0,)))


def _mm(a, b, *, name, ta=False, tb=False, res=None, out_dtype=F32, ride=None):
    if ta:
        K, M = a.shape
    else:
        M, K = a.shape
    N = b.shape[0] if tb else b.shape[1]
    tm = _tile(M, MM_TM_T if ta else (MM_TM if K <= MM_TM else MM_TM // 2), LANES if ta else 8)
    tn = _tile(N, MM_TN, LANES)
    tk = _tile(K, MM_TK_T if ta else MM_TK, LANES)
    nk = K // tk
    a_spec = pl.BlockSpec((tk, tm), lambda j, i, k: (k, i)) if ta else pl.BlockSpec((tm, tk), lambda j, i, k: (i, k))
    b_spec = pl.BlockSpec((tn, tk), lambda j, i, k: (j, k)) if tb else pl.BlockSpec((tk, tn), lambda j, i, k: (k, j))
    o_spec = pl.BlockSpec((tm, tn), lambda j, i, k: (i, j))
    dims = ((0,) if ta else (1,), (1,) if tb else (0,))
    has_res = res is not None

    def body(*refs):
        a_ref, b_ref = refs[:2]
        r_ref = refs[2] if has_res else None
        o_ref = refs[3] if has_res else refs[2]
        p = _dot(a_ref[...], b_ref[...], dims)

        def finish(r):
            if has_res:
                r = r + r_ref[...].astype(F32)
            o_ref[...] = r.astype(out_dtype)

        if nk == 1:
            finish(p)
            return
        acc_ref = refs[-1]
        k = pl.program_id(2)

        @pl.when(k == 0)
        def _():
            acc_ref[...] = p

        @pl.when(jnp.logical_and(k > 0, k < nk - 1))
        def _():
            acc_ref[...] += p

        @pl.when(k == nk - 1)
        def _():
            finish(acc_ref[...] + p)

    in_specs = [a_spec, b_spec] + ([o_spec] if has_res else [])
    args = (a, b) + ((res,) if has_res else ())
    return _pcall(
        body, name=name, grid=(N // tn, M // tm, nk), in_specs=in_specs, out_specs=[o_spec],
        out_shape=[jax.ShapeDtypeStruct((M, N), out_dtype)], args=args,
        scratch=[pltpu.VMEM((tm, tn), F32)] if nk > 1 else [], sem=("parallel", "parallel", "arbitrary"), ride=ride)[0]


def _rms_fwd(x, g, *, name, cb=0, out_dtype=BF16, ride=None):
    T = x.shape[0]
    W = g.shape[-1]
    g = g.reshape(1, W)
    tt = _tile(T, NORM_TILE, 16)

    def body(x_ref, g_ref, o_ref):
        xf = x_ref[...].astype(F32)
        rstd = lax.rsqrt(jnp.mean(xf * xf, axis=-1, keepdims=True) + NORM_EPS)
        o_ref[...] = (xf * rstd * g_ref[...]).astype(out_dtype)

    return _pcall(
        body, name=name, grid=(T // tt,),
        in_specs=[pl.BlockSpec((tt, W), lambda i: (i, cb)), pl.BlockSpec((1, W), lambda i: (0, 0))],
        out_specs=[pl.BlockSpec((tt, W), lambda i: (i, 0))], out_shape=[jax.ShapeDtypeStruct((T, W), out_dtype)],
        args=(x, g), sem=("parallel",), ride=ride)[0]


def _rms_bwd(x, g, dy, *, name, cb=0, res=None, out_dtype=F32, ride=None):
    T = x.shape[0]
    W = g.shape[-1]
    g = g.reshape(1, W)
    tt = _tile(T, NORM_TILE // 2, 16)
    has_res = res is not None

    def body(*refs):
        if has_res:
            x_ref, g_ref, dy_ref, r_ref, dx_ref, dg_ref = refs
        else:
            x_ref, g_ref, dy_ref, dx_ref, dg_ref = refs
        xf = x_ref[...].astype(F32)
        dyf = dy_ref[...].astype(F32)
        rstd = lax.rsqrt(jnp.mean(xf * xf, axis=-1, keepdims=True) + NORM_EPS)
        xhat = xf * rstd
        dxhat = dyf * g_ref[...]
        dx = rstd * (dxhat - xhat * jnp.mean(dxhat * xhat, axis=-1, keepdims=True))
        if has_res:
            dx = dx + r_ref[...].astype(F32)
        dx_ref[...] = dx.astype(out_dtype)
        part = jnp.sum(dyf * xhat, axis=0, keepdims=True)

        @pl.when(pl.program_id(0) == 0)
        def _():
            dg_ref[...] = part

        @pl.when(pl.program_id(0) > 0)
        def _():
            dg_ref[...] += part

    row = pl.BlockSpec((tt, W), lambda i: (i, 0))
    in_specs = [pl.BlockSpec((tt, W), lambda i: (i, cb)), pl.BlockSpec((1, W), lambda i: (0, 0)), row]
    args = (x, g, dy)
    if has_res:
        in_specs.append(row)
        args = args + (res,)
    return _pcall(
        body, name=name, grid=(T // tt,), in_specs=in_specs,
        out_specs=[row, pl.BlockSpec((1, W), lambda i: (0, 0))],
        out_shape=[jax.ShapeDtypeStruct((T, W), out_dtype), jax.ShapeDtypeStruct((1, W), F32)], args=args, ride=ride)


def _final_fwd_bwd(h, g, target, *, name):
    T, W = h.shape
    g = g.reshape(1, W)
    tt = _tile(T, NORM_TILE, 16)

    def body(x_ref, g_ref, t_ref, loss_ref, dx_ref, dg_ref):
        xf = x_ref[...]
        rstd = lax.rsqrt(jnp.mean(xf * xf, axis=-1, keepdims=True) + NORM_EPS)
        xhat = xf * rstd
        err = xhat * g_ref[...] - t_ref[...]
        lpart = jnp.zeros((1, LANES), F32) + (0.5 / W) * jnp.sum(err * err)
        dyf = err * (1.0 / W)
        dxhat = dyf * g_ref[...]
        dx_ref[...] = rstd * (dxhat - xhat * jnp.mean(dxhat * xhat, axis=-1, keepdims=True))
        part = jnp.sum(dyf * xhat, axis=0, keepdims=True)

        @pl.when(pl.program_id(0) == 0)
        def _():
            dg_ref[...] = part
            loss_ref[...] = lpart

        @pl.when(pl.program_id(0) > 0)
        def _():
            dg_ref[...] += part
            loss_ref[...] += lpart

    row = pl.BlockSpec((tt, W), lambda i: (i, 0))
    return pl.pallas_call(
        body, name=name, grid=(T // tt,),
        in_specs=[row, pl.BlockSpec((1, W), lambda i: (0, 0)), row],
        out_specs=[pl.BlockSpec((1, LANES), lambda i: (0, 0)), row, pl.BlockSpec((1, W), lambda i: (0, 0))],
        out_shape=[jax.ShapeDtypeStruct((1, LANES), F32), jax.ShapeDtypeStruct((T, W), F32),
                   jax.ShapeDtypeStruct((1, W), F32)],
        compiler_params=_cparams(("arbitrary",)),
    )(h, g, target)


def _swap16(x):
    lane = lax.broadcasted_iota(jnp.int32, x.shape, 1)
    return jnp.where((lane % 32) < 16, pltpu.roll(x, LANES - 16, 1), pltpu.roll(x, 16, 1))


def _rope(x, c, s):
    return x * c + _swap16(x) * s


def _rope_t(d, c, s):
    return d * c + _swap16(d * s)


def _head_block_map(fn, x, cos, sin, *, name):
    T, W = x.shape
    tt = _tile(T, NORM_TILE, 16)

    def body(x_ref, c_ref, s_ref, o_ref):
        c, s = c_ref[...], s_ref[...]
        for h in range(W // LANES):
            lanes = slice(h * LANES, (h + 1) * LANES)
            o_ref[:, lanes] = fn(x_ref[:, lanes], c, s).astype(BF16)

    tab = pl.BlockSpec((tt, LANES), lambda i: (i, 0))
    blk = pl.BlockSpec((tt, W), lambda i: (i, 0))
    return pl.pallas_call(
        body, name=name, grid=(T // tt,), in_specs=[blk, tab, tab], out_specs=blk,
        out_shape=jax.ShapeDtypeStruct((T, W), BF16), compiler_params=_cparams(("parallel",)),
    )(x, cos, sin)


def _rope_q(q, cos, sin, *, name):
    scale = _attn_scale()
    return _head_block_map(lambda x, c, s: _rope(x, c, s) * scale, q, cos, sin, name=name)


def _rope_q_bwd(dq, cos, sin, *, name):
    return _head_block_map(_rope_t, dq, cos, sin, name=name)


def _key_blocks(kv, z, cos, sin, *, kpe_block, name):
    T = kv.shape[0]
    tt = _tile(T, NORM_TILE, 16)
    W = MLA_HEADS * LANES

    def body(kv_ref, z_ref, c_ref, s_ref, o_ref):
        kr = _rope(z_ref[...], c_ref[...], s_ref[...])
        for h in range(MLA_HEADS):
            lanes = slice(h * LANES, (h + 1) * LANES)
            o_ref[:, lanes] = (kv_ref[:, lanes].astype(F32) + kr).astype(BF16)

    tab = pl.BlockSpec((tt, LANES), lambda i: (i, 0))
    blk = pl.BlockSpec((tt, W), lambda i: (i, 0))
    return pl.pallas_call(
        body, name=name, grid=(T // tt,),
        in_specs=[blk, pl.BlockSpec((tt, LANES), lambda i: (i, kpe_block)), tab, tab], out_specs=blk,
        out_shape=jax.ShapeDtypeStruct((T, W), BF16), compiler_params=_cparams(("parallel",)),
    )(kv, z, cos, sin)


def _key_rope_bwd(dk, cos, sin, *, name):
    T = dk.shape[0]
    tt = _tile(T, NORM_TILE, 16)

    def body(d_ref, c_ref, s_ref, o_ref):
        d = d_ref[:, :LANES]
        for h in range(1, MLA_HEADS):
            d = d + d_ref[:, h * LANES:(h + 1) * LANES]
        lane = lax.broadcasted_iota(jnp.int32, d.shape, 1)
        d = jnp.where(jnp.logical_and(lane >= QK_NOPE, lane < QK_NOPE + QK_ROPE), d, 0.0)
        o_ref[...] = _rope_t(d, c_ref[...], s_ref[...]).astype(BF16)

    tab = pl.BlockSpec((tt, LANES), lambda i: (i, 0))
    return pl.pallas_call(
        body, name=name, grid=(T // tt,),
        in_specs=[pl.BlockSpec((tt, MLA_HEADS * LANES), lambda i: (i, 0)), tab, tab], out_specs=tab,
        out_shape=jax.ShapeDtypeStruct((T, LANES), BF16), compiler_params=_cparams(("parallel",)),
    )(dk, cos, sin)


ATT_BLOCK = 512


def _attn_scale():
    return float((QK_NOPE + QK_ROPE) ** -0.5)


def _causal_mask(qi, kj, tq, tk):
    row = qi * tq + lax.broadcasted_iota(jnp.int32, (tq, tk), 0)
    col = kj * tk + lax.broadcasted_iota(jnp.int32, (tq, tk), 1)
    return col <= row


def _pcall(body, *, name, grid, in_specs, out_specs, out_shape, args, scratch=(), sem=None, ride=None):
    n_in, n_out, n_scr = len(args), len(out_shape), len(scratch)
    if ride is None:
        return pl.pallas_call(
            body, name=name, grid=grid, in_specs=list(in_specs), out_specs=list(out_specs), out_shape=list(out_shape),
            scratch_shapes=list(scratch), compiler_params=_cparams(sem or ("arbitrary",) * len(grid)))(*args)
    ex, sink = ride
    o0 = n_in + len(ex.arrs)
    s0 = o0 + n_out + len(ex.out_shapes)

    def hosted(*refs):
        parts = (refs[n_in:o0], refs[o0 + n_out:s0], refs[-2], refs[-1])
        ids = [pl.program_id(i) for i in range(len(grid))]
        pl.when(functools.reduce(jnp.logical_and, [i == 0 for i in ids]))(lambda: ex.start(*parts))
        body(*refs[:n_in], *refs[o0:o0 + n_out], *refs[s0:s0 + n_scr])
        pl.when(functools.reduce(jnp.logical_and, [i == n - 1 for i, n in zip(ids, grid)]))(lambda: ex.finish(*parts))

    outs = pl.pallas_call(
        hosted, name=name, grid=grid, in_specs=list(in_specs) + ex.in_specs, out_specs=list(out_specs) + ex.out_specs,
        out_shape=list(out_shape) + ex.out_shapes, scratch_shapes=list(scratch) + ex.scratch,
        compiler_params=_cparams(("arbitrary",) * len(grid)))(*args, *ex.arrs)
    sink(outs[n_out:])
    return outs[:n_out]


PAIRS = MLA_HEADS // 2


def _own_lanes(x, first):
    lane = lax.broadcasted_iota(jnp.int32, x.shape, 1)
    return jnp.where((lane < V_HEAD) if first else (lane >= V_HEAD), x, 0.0)


def _attn_fwd(q, k, kv, *, B, S, v_block0, name, ride=None):
    tq = tk = min(ATT_BLOCK, S)
    nq = S // tq
    T = B * S

    def body(q_ref, k_ref, v_ref, o_ref, lse_ref):
        qi = pl.program_id(2)
        qs = (q_ref[:, :LANES], q_ref[:, LANES:])

        def step(masked):
            def f(j, carry):
                rows = pl.ds(pl.multiple_of(j * tk, tk), tk)
                vb = v_ref[rows, :]
                out = []
                for h in range(2):
                    m, l, acc = carry[h]
                    s = _dot_nt(qs[h], k_ref[rows, h * LANES:(h + 1) * LANES])
                    if masked:
                        s = jnp.where(_causal_mask(qi, j, tq, tk), s, -jnp.inf)
                    m_new = jnp.maximum(m, jnp.max(s, axis=-1, keepdims=True))
                    alpha = jnp.exp(m - m_new)
                    p = jnp.exp(s - m_new)
                    out.append((m_new, alpha * l + jnp.sum(p, axis=-1, keepdims=True), alpha * acc + _dot_nn(p, vb)))
                return tuple(out)
            return f

        one = (jnp.full((tq, 1), -1e30, F32), jnp.zeros((tq, 1), F32), jnp.zeros((tq, LANES), F32))
        (ma, la, acca), (mb, lb, accb) = step(True)(qi, lax.fori_loop(0, qi, step(False), (one, one)))
        o_ref[...] = _own_lanes(acca / la, True) + _own_lanes(accb / lb, False)
        lse_ref[0, 0] = ma + jnp.log(la)
        lse_ref[0, 1] = mb + jnp.log(lb)

    return _pcall(
        body, name=name, grid=(B, PAIRS, nq),
        in_specs=[pl.BlockSpec((tq, 2 * LANES), lambda b, g, i: (b * nq + i, g)),
                  pl.BlockSpec((S, 2 * LANES), lambda b, g, i: (b, g)),
                  pl.BlockSpec((S, LANES), lambda b, g, i: (b, v_block0 + g))],
        out_specs=[pl.BlockSpec((tq, LANES), lambda b, g, i: (b * nq + i, g)),
                   pl.BlockSpec((1, 2, tq, 1), lambda b, g, i: (b, g, i, 0))],
        out_shape=[jax.ShapeDtypeStruct((T, PAIRS * LANES), F32), jax.ShapeDtypeStruct((B, MLA_HEADS, S, 1), F32)],
        args=(q, k, kv), ride=ride)


def _attn_dq(q, k, kv, o, lse, do, *, B, S, v_block0, name, ride=None):
    tq = tk = min(ATT_BLOCK, S)
    nq = S // tq
    T = B * S
    scale = _attn_scale()

    def body(q_ref, k_ref, v_ref, o_ref, lse_ref, do_ref, dq_ref, delta_ref):
        qi = pl.program_id(2)
        qs = (q_ref[:, :LANES], q_ref[:, LANES:])
        dos = (_own_lanes(do_ref[...], True), _own_lanes(do_ref[...], False))
        deltas = tuple(jnp.sum(d * o_ref[...], axis=-1, keepdims=True) for d in dos)
        lses = (lse_ref[0, 0], lse_ref[0, 1])

        def step(masked):
            def f(j, carry):
                rows = pl.ds(pl.multiple_of(j * tk, tk), tk)
                vb = v_ref[rows, :]
                out = []
                for h in range(2):
                    kb = k_ref[rows, h * LANES:(h + 1) * LANES]
                    p = jnp.exp(_dot_nt(qs[h], kb) - lses[h])
                    if masked:
                        p = jnp.where(_causal_mask(qi, j, tq, tk), p, 0.0)
                    ds = p * (_dot_nt(dos[h], vb) - deltas[h])
                    out.append(carry[h] + _dot_nn(ds, kb))
                return tuple(out)
            return f

        zero = jnp.zeros((tq, LANES), F32)
        dqa, dqb = step(True)(qi, lax.fori_loop(0, qi, step(False), (zero, zero)))
        dq_ref[:, :LANES] = dqa * scale
        dq_ref[:, LANES:] = dqb * scale
        delta_ref[0, 0] = deltas[0]
        delta_ref[0, 1] = deltas[1]

    qrow = lambda w: pl.BlockSpec((tq, w), lambda b, g, i: (b * nq + i, g))
    stat = pl.BlockSpec((1, 2, tq, 1), lambda b, g, i: (b, g, i, 0))
    return _pcall(
        body, name=name, grid=(B, PAIRS, nq),
        in_specs=[qrow(2 * LANES), pl.BlockSpec((S, 2 * LANES), lambda b, g, i: (b, g)),
                  pl.BlockSpec((S, LANES), lambda b, g, i: (b, v_block0 + g)), qrow(LANES), stat, qrow(LANES)],
        out_specs=[qrow(2 * LANES), stat],
        out_shape=[jax.ShapeDtypeStruct((T, MLA_HEADS * LANES), F32), jax.ShapeDtypeStruct((B, MLA_HEADS, S, 1), F32)],
        args=(q, k, kv, o, lse, do), sem=("parallel", "parallel", "parallel"), ride=ride)


def _attn_dkv(q, k, kv, lse_rows, delta_rows, do, *, B, S, v_block0, name, ride=None):
    tq = tk = min(ATT_BLOCK, S)
    nq = S // tq
    T = B * S

    def body(q_ref, k_ref, v_ref, lse_ref, delta_ref, do_ref, dk_ref, dv_ref):
        kj = pl.program_id(2)
        ks = (k_ref[:, :LANES], k_ref[:, LANES:])
        vb = v_ref[...]

        def step(masked):
            def f(i, carry):
                rows = pl.ds(pl.multiple_of(i * tq, tq), tq)
                do_b = do_ref[rows, :]
                dks, dv = list(carry[:2]), carry[2]
                for h in range(2):
                    qb = q_ref[rows, h * LANES:(h + 1) * LANES]
                    doh = _own_lanes(do_b, h == 0)
                    pt = jnp.exp(_dot_nt(ks[h], qb) - lse_ref[0, h, pl.ds(i, 1), :])
                    if masked:
                        krow = kj * tk + lax.broadcasted_iota(jnp.int32, (tk, tq), 0)
                        qcol = i * tq + lax.broadcasted_iota(jnp.int32, (tk, tq), 1)
                        pt = jnp.where(krow <= qcol, pt, 0.0)
                    dst = pt * (_dot_nt(vb, doh) - delta_ref[0, h, pl.ds(i, 1), :])
                    dks[h] = dks[h] + _dot_nn(dst, qb)
                    dv = dv + _dot_nn(pt, doh)
                return dks[0], dks[1], dv
            return f

        zero = jnp.zeros((tk, LANES), F32)
        dka, dkb, dv = lax.fori_loop(kj + 1, nq, step(False), step(True)(kj, (zero, zero, zero)))
        dk_ref[:, :LANES] = dka
        dk_ref[:, LANES:] = dkb
        dv_ref[...] = dv

    krow = lambda w, c0: pl.BlockSpec((tk, w), lambda b, g, j: (b * nq + j, c0 + g))
    seq = lambda w: pl.BlockSpec((S, w), lambda b, g, j: (b, g))
    stat = pl.BlockSpec((1, 2, nq, tq), lambda b, g, j: (b, g, 0, 0))
    return _pcall(
        body, name=name, grid=(B, PAIRS, nq),
        in_specs=[seq(2 * LANES), krow(2 * LANES, 0), krow(LANES, v_block0), stat, stat, seq(LANES)],
        out_specs=[krow(2 * LANES, 0), krow(LANES, 0)],
        out_shape=[jax.ShapeDtypeStruct((T, MLA_HEADS * LANES), F32), jax.ShapeDtypeStruct((T, PAIRS * LANES), F32)],
        args=(q, k, kv, lse_rows, delta_rows, do), ride=ride)


def _lru_gates(xl, halo, cw_ref, cb_ref, wa_ref, ba_ref, wx_ref, bx_ref, lam_ref):
    xc = cb_ref[...] + cw_ref[3:4, :] * xl
    for kk in range(LRU_CONV - 1):
        xc = xc + cw_ref[kk:kk + 1, :] * _shift_rows(xl, LRU_CONV - 1 - kk, halo)
    r = _sigmoid(_dot_nn(xc, wa_ref[...]) + ba_ref[...])
    i = _sigmoid(_dot_nn(xc, wx_ref[...]) + bx_ref[...])
    lam = lam_ref[...]
    sp = jnp.maximum(-lam, 0.0) + jnp.log(1.0 + jnp.exp(-jnp.abs(lam)))
    a = jnp.exp(-LRU_C * r * sp)
    mult = jnp.sqrt(1.0 - a * a)
    return xc, r, i, sp, a, mult


def _lru_specs(tt, nt, S):
    def make(rev):
        tmap = (lambda t: nt - 1 - t) if rev else (lambda t: t)
        tile = lambda cb: pl.BlockSpec((tt, LRU_WIDTH), lambda b, t: (b * nt + tmap(t), cb))
        prev8 = lambda cb: pl.BlockSpec(
            (8, LRU_WIDTH), lambda b, t: (jnp.maximum((b * nt + tmap(t)) * (tt // 8) - 1, 0), cb))
        return tile, prev8, tmap
    return make


def _lru_fwd(z, cw, cb, wa, ba, wx, bx, lam, *, S, name, ride=None):
    T = z.shape[0]
    tt = min(ROW_TILE, S)
    nt = S // tt
    tile, prev8, _ = _lru_specs(tt, nt, S)(False)
    vec = lambda r: pl.BlockSpec((r, LRU_WIDTH), lambda b, t: (0, 0))
    mat = pl.BlockSpec((LRU_WIDTH, LRU_WIDTH), lambda b, t: (0, 0))

    def body(xl_ref, halo_ref, gate_ref, cw_ref, cb_ref, wa_ref, ba_ref, wx_ref, bx_ref, lam_ref,
             y_ref, h_ref, carry_ref):
        t = pl.program_id(1)
        first = t == 0
        halo = jnp.where(first, 0.0, halo_ref[...])
        xl_t = xl_ref[...]
        xc, r, i, sp, a, mult = _lru_gates(xl_t, halo, cw_ref, cb_ref, wa_ref, ba_ref, wx_ref, bx_ref, lam_ref)
        bv = mult * (i * xc)
        ones = jnp.ones((8, LRU_WIDTH), F32)
        zeros = jnp.zeros((8, LRU_WIDTH), F32)
        row = lax.broadcasted_iota(jnp.int32, (tt, LRU_WIDTH), 0)
        A = a
        d = 1
        while d < tt:
            if d < 8:
                a_sh = _shift_rows(A, d, ones)
                b_sh = _shift_rows(bv, d, zeros)
            else:
                a_sh = jnp.where(row < d, 1.0, pltpu.roll(A, d, 0))
                b_sh = jnp.where(row < d, 0.0, pltpu.roll(bv, d, 0))
            bv = A * b_sh + bv
            A = A * a_sh
            d *= 2
        h0 = jnp.where(first, 0.0, carry_ref[0:1, :])
        h = A * h0 + bv
        carry_ref[...] = jnp.broadcast_to(h[tt - 1:tt, :], (8, LRU_WIDTH))
        h_ref[...] = h
        y_ref[...] = (h * _gelu(gate_ref[...])).astype(BF16)

    return _pcall(
        body, name=name, grid=(T // S, nt),
        in_specs=[tile(0), prev8(0), tile(1), vec(LRU_CONV), vec(1), mat, vec(1), mat, vec(1), vec(1)],
        out_specs=[tile(0), tile(0)],
        out_shape=[jax.ShapeDtypeStruct((T, LRU_WIDTH), BF16), jax.ShapeDtypeStruct((T, LRU_WIDTH), F32)],
        args=(z, z, z, cw, cb, wa, ba, wx, bx, lam), scratch=[pltpu.VMEM((8, LRU_WIDTH), F32)], ride=ride)


def _lru_bwd(z, h, dy, cw, cb, wa, ba, wx, bx, lam, *, S, name):
    T = z.shape[0]
    tt = min(ROW_TILE, S)
    nt = S // tt
    tile, prev8, tmap = _lru_specs(tt, nt, S)(True)
    vec = lambda r: pl.BlockSpec((r, LRU_WIDTH), lambda b, t: (0, 0))
    mat = pl.BlockSpec((LRU_WIDTH, LRU_WIDTH), lambda b, t: (0, 0))

    def body(xl_ref, halo_ref, gate_ref, h_ref, hprev_ref, dy_ref, cw_ref, cb_ref, wa_ref, ba_ref, wx_ref,
             bx_ref, lam_ref, dxl_ref, dgate_ref, dcw_ref, dcb_ref, dwa_ref, dba_ref, dwx_ref, dbx_ref,
             dlam_ref, lamc_ref, ac_ref, dxc_ref):
        b = pl.program_id(0)
        t = pl.program_id(1)
        tr = nt - 1 - t
        seq_first = tr == 0
        seq_last = t == 0
        halo = jnp.where(seq_first, 0.0, halo_ref[...])
        xl_t = xl_ref[...]
        xc, r, i, sp, a, mult = _lru_gates(xl_t, halo, cw_ref, cb_ref, wa_ref, ba_ref, wx_ref, bx_ref, lam_ref)
        hh = h_ref[...]
        dyf = dy_ref[...].astype(F32)
        gl, dgl = _gelu_and_grad(gate_ref[...])
        dgate_ref[...] = (dyf * hh * dgl).astype(BF16)
        dh = dyf * gl

        a_first_later = jnp.where(seq_last, 0.0, ac_ref[...])
        lam_later = jnp.where(seq_last, 0.0, lamc_ref[...])
        row = lax.broadcasted_iota(jnp.int32, (tt, LRU_WIDTH), 0)
        A = _shift_rows_up(a, 1, a_first_later)
        lm = dh
        ones = jnp.ones((8, LRU_WIDTH), F32)
        zeros = jnp.zeros((8, LRU_WIDTH), F32)
        d = 1
        while d < tt:
            if d < 8:
                a_sh = _shift_rows_up(A, d, ones)
                l_sh = _shift_rows_up(lm, d, zeros)
            else:
                a_sh = jnp.where(row >= tt - d, 1.0, pltpu.roll(A, tt - d, 0))
                l_sh = jnp.where(row >= tt - d, 0.0, pltpu.roll(lm, tt - d, 0))
            lm = lm + A * l_sh
            A = A * a_sh
            d *= 2
        lm = lm + A * lam_later[0:1, :]
        lamc_ref[...] = jnp.broadcast_to(lm[0:1, :], (8, LRU_WIDTH))
        ac_ref[...] = jnp.broadcast_to(a[0:1, :], (8, LRU_WIDTH))

        hprev_halo = jnp.where(seq_first, 0.0, hprev_ref[...])
        h_prev = _shift_rows(hh, 1, hprev_halo)
        da = lm * h_prev
        ixc = i * xc
        dmult = lm * ixc
        di = lm * mult * xc
        dxc = lm * mult * i
        da = da - dmult * a / mult
        dlog = da * a
        dr = dlog * (-LRU_C) * sp
        dsp_part = jnp.sum(dlog * (-LRU_C) * r, axis=0, keepdims=True)
        dpa = dr * r * (1.0 - r)
        dpx = di * i * (1.0 - i)
        dxc = dxc + _dot_nt(dpa, wa_ref[...]) + _dot_nt(dpx, wx_ref[...])
        dwa_part = _dot_tn(xc, dpa)
        dwx_part = _dot_tn(xc, dpx)

        later = jnp.where(seq_last, 0.0, dxc_ref[...])
        dxl = cw_ref[3:4, :] * dxc
        for kk in range(LRU_CONV - 1):
            dxl = dxl + cw_ref[kk:kk + 1, :] * _shift_rows_up(dxc, LRU_CONV - 1 - kk, later)
        dxl_ref[...] = dxl.astype(BF16)
        dxc_ref[...] = dxc[0:8, :]
        dcw_rows = [jnp.sum(dxc * _shift_rows(xl_t, LRU_CONV - 1 - kk, halo), axis=0, keepdims=True)
                    for kk in range(LRU_CONV - 1)]
        dcw_rows.append(jnp.sum(dxc * xl_t, axis=0, keepdims=True))
        dcw_part = jnp.concatenate(dcw_rows + [jnp.zeros((8 - LRU_CONV, LRU_WIDTH), F32)], axis=0)
        lamv = lam_ref[...]
        dlam_part = dsp_part * (-_sigmoid(-lamv))
        parts = ((dcw_ref, dcw_part), (dcb_ref, jnp.sum(dxc, axis=0, keepdims=True)),
                 (dwa_ref, dwa_part), (dba_ref, jnp.sum(dpa, axis=0, keepdims=True)),
                 (dwx_ref, dwx_part), (dbx_ref, jnp.sum(dpx, axis=0, keepdims=True)),
                 (dlam_ref, dlam_part))
        start = jnp.logical_and(b == 0, t == 0)

        @pl.when(start)
        def _():
            for ref, val in parts:
                ref[...] = val

        @pl.when(jnp.logical_not(start))
        def _():
            for ref, val in parts:
                ref[...] += val

    acc = lambda r: pl.BlockSpec((r, LRU_WIDTH), lambda b, t: (0, 0))
    return pl.pallas_call(
        body, name=name, grid=(T // S, nt),
        in_specs=[tile(0), prev8(0), tile(1), tile(0), prev8(0), tile(0),
                  vec(LRU_CONV), vec(1), mat, vec(1), mat, vec(1), vec(1)],
        out_specs=[tile(0), tile(0), acc(8), acc(1), mat, acc(1), mat, acc(1), acc(1)],
        out_shape=[jax.ShapeDtypeStruct((T, LRU_WIDTH), BF16), jax.ShapeDtypeStruct((T, LRU_WIDTH), BF16),
                   jax.ShapeDtypeStruct((8, LRU_WIDTH), F32), jax.ShapeDtypeStruct((1, LRU_WIDTH), F32),
                   jax.ShapeDtypeStruct((LRU_WIDTH, LRU_WIDTH), F32), jax.ShapeDtypeStruct((1, LRU_WIDTH), F32),
                   jax.ShapeDtypeStruct((LRU_WIDTH, LRU_WIDTH), F32), jax.ShapeDtypeStruct((1, LRU_WIDTH), F32),
                   jax.ShapeDtypeStruct((1, LRU_WIDTH), F32)],
        scratch_shapes=[pltpu.VMEM((8, LRU_WIDTH), F32), pltpu.VMEM((8, LRU_WIDTH), F32),
                        pltpu.VMEM((8, LRU_WIDTH), F32)],
        compiler_params=_cparams(("arbitrary", "arbitrary")),
    )(z, z, z, h, h, dy, cw, cb, wa, ba, wx, bx, lam)


FFN_CT = 1408
FFN_TILE = 512


def _ffn_conv(g, halo, cw, cb):
    gc = cb + cw[2:3, :] * g
    for kk in range(FFN_CONV - 1):
        gc = gc + cw[kk:kk + 1, :] * _shift_rows(g, FFN_CONV - 1 - kk, halo)
    return gc


def _row_chunks(rows, chunk):
    return [slice(r0, min(r0 + chunk, rows)) for r0 in range(0, rows, chunk)]


FFN_CHUNK = 128
HALO = 16


def _ffn_act_down(g, u, cw, cb, w_down, res, *, S, name, ride=None):
    T, F = g.shape
    D = w_down.shape[1]
    tt = min(FFN_TILE, S)
    nt = S // tt
    tc = _tile(F, FFN_CT)
    nj = F // tc

    def body(g_ref, halo_ref, u_ref, cw_ref, cb_ref, w_ref, r_ref, o_ref, act_ref):
        j = pl.program_id(1)
        first = (pl.program_id(0) % nt) == 0
        cw, cb = cw_ref[...], cb_ref[...]

        @pl.when(j == 0)
        def _():
            o_ref[...] = r_ref[...]

        for r in _row_chunks(tt, FFN_CHUNK):
            before = halo_ref[...] if r.start == 0 else g_ref[r.start - HALO:r.start, :]
            halo = before.astype(F32)[HALO - 8:]
            if r.start == 0:
                halo = jnp.where(first, 0.0, halo)
            gc = _ffn_conv(g_ref[r, :].astype(F32), halo, cw, cb)
            act = (_gelu(gc) * u_ref[r, :].astype(F32)).astype(BF16)
            act_ref[r, :] = act
            o_ref[r, :] += _dot_nn(act, w_ref[...])

    tile = pl.BlockSpec((tt, tc), lambda i, j: (i, j))
    prev = pl.BlockSpec((HALO, tc), lambda i, j: (jnp.maximum(i * (tt // HALO) - 1, 0), j))
    rows = pl.BlockSpec((tt, D), lambda i, j: (i, 0))
    return _pcall(
        body, name=name, grid=(T // tt, nj),
        in_specs=[tile, prev, tile, pl.BlockSpec((FFN_CONV, tc), lambda i, j: (0, j)),
                  pl.BlockSpec((1, tc), lambda i, j: (0, j)), pl.BlockSpec((tc, D), lambda i, j: (j, 0)), rows],
        out_specs=[rows, tile], out_shape=[jax.ShapeDtypeStruct((T, D), F32), jax.ShapeDtypeStruct((T, F), BF16)],
        args=(g, g, u, cw, cb, w_down, res), sem=("parallel", "arbitrary"), ride=ride)


def _ffn_act_bwd(g, u, dh, w_down, cw, cb, *, S, name, ride=None):
    T, F = g.shape
    D = w_down.shape[1]
    tt = min(FFN_TILE, S)
    nt = S // tt
    ntt = T // tt
    tc = _tile(F, FFN_CT)

    def body(g_ref, halo_ref, u_ref, dh_ref, w_ref, cw_ref, cb_ref, dg_ref, du_ref, dcw_ref, dcb_ref, later_ref):
        step = pl.program_id(1)
        ti = (ntt - 1 - step) % nt
        cw, cb = cw_ref[...], cb_ref[...]

        @pl.when(step == 0)
        def _():
            dcw_ref[...] = jnp.zeros_like(dcw_ref)
            dcb_ref[...] = jnp.zeros_like(dcb_ref)

        halo = jnp.where(ti == 0, 0.0, halo_ref[...].astype(F32)[HALO - 8:])
        gt = g_ref[...].astype(F32)
        gl, dgl = _gelu_and_grad(_ffn_conv(gt, halo, cw, cb))
        da = _dot_nt(dh_ref[...], w_ref[...])
        du_ref[...] = (da * gl).astype(BF16)
        dgc = da * u_ref[...].astype(F32) * dgl
        later = jnp.where(ti == nt - 1, 0.0, later_ref[...])
        dg = cw[2:3, :] * dgc
        for kk in range(FFN_CONV - 1):
            dg = dg + cw[kk:kk + 1, :] * _shift_rows_up(dgc, FFN_CONV - 1 - kk, later)
        dg_ref[...] = dg.astype(BF16)
        later_ref[...] = dgc[0:8, :]
        rows = [jnp.sum(dgc * _shift_rows(gt, FFN_CONV - 1 - kk, halo), axis=0, keepdims=True)
                for kk in range(FFN_CONV - 1)]
        rows.append(jnp.sum(dgc * gt, axis=0, keepdims=True))
        dcw_ref[...] += jnp.concatenate(rows + [jnp.zeros((8 - FFN_CONV, tc), F32)], axis=0)
        dcb_ref[...] += jnp.sum(dgc, axis=0, keepdims=True)

    tile = pl.BlockSpec((tt, tc), lambda j, s: (ntt - 1 - s, j))
    prev = pl.BlockSpec((HALO, tc), lambda j, s: (jnp.maximum((ntt - 1 - s) * (tt // HALO) - 1, 0), j))
    return _pcall(
        body, name=name, grid=(F // tc, ntt),
        in_specs=[tile, prev, tile, pl.BlockSpec((tt, D), lambda j, s: (ntt - 1 - s, 0)),
                  pl.BlockSpec((tc, D), lambda j, s: (j, 0)), pl.BlockSpec((FFN_CONV, tc), lambda j, s: (0, j)),
                  pl.BlockSpec((1, tc), lambda j, s: (0, j))],
        out_specs=[tile, tile, pl.BlockSpec((8, tc), lambda j, s: (0, j)), pl.BlockSpec((1, tc), lambda j, s: (0, j))],
        out_shape=[jax.ShapeDtypeStruct((T, F), BF16), jax.ShapeDtypeStruct((T, F), BF16),
                   jax.ShapeDtypeStruct((8, F), F32), jax.ShapeDtypeStruct((1, F), F32)],
        args=(g, g, u, dh, w_down, cw, cb), scratch=[pltpu.VMEM((8, tc), F32)], ride=ride)


def _sgu_norm(zv, g_ref, b_ref):
    v = _gelu(zv)
    mu = jnp.mean(v, axis=-1, keepdims=True)
    xc = v - mu
    rstd = lax.rsqrt(jnp.mean(xc * xc, axis=-1, keepdims=True) + NORM_EPS)
    xhat = xc * rstd
    return xhat, rstd, xhat * g_ref[...] + b_ref[...]


def _sgu_fwd(zc, ln_g, ln_b, wm, bmap, *, name):
    T = zc.shape[0]
    W = SGU_WIDTH
    tt = ROW_TILE
    nch = tt // CHUNK

    def body(z_ref, g_ref, b_ref, wm_ref, bm_ref, p_ref):
        u = _gelu(z_ref[:, :W])
        _, _, vn = _sgu_norm(z_ref[:, W:], g_ref, b_ref)
        vn = vn.astype(BF16)
        for n in range(nch):
            rows = slice(n * CHUNK, (n + 1) * CHUNK)
            for gi in range(SGU_GROUPS):
                cols = slice(gi * LANES, (gi + 1) * LANES)
                s = _dot_nn(wm_ref[gi], vn[rows, cols]) + bm_ref[:, cols]
                p_ref[rows, cols] = (u[rows, cols] * s).astype(BF16)

    const2 = lambda r, c: pl.BlockSpec((r, c), lambda i: (0, 0))
    return pl.pallas_call(
        body, name=name, grid=(T // tt,),
        in_specs=[pl.BlockSpec((tt, 2 * W), lambda i: (i, 0)), const2(1, W), const2(1, W),
                  pl.BlockSpec((SGU_GROUPS, CHUNK, CHUNK), lambda i: (0, 0, 0)), const2(CHUNK, W)],
        out_specs=pl.BlockSpec((tt, W), lambda i: (i, 0)),
        out_shape=jax.ShapeDtypeStruct((T, W), BF16),
        compiler_params=_cparams(("parallel",)),
    )(zc, ln_g, ln_b, wm, bmap)


def _sgu_bwd(zc, dp, ln_g, ln_b, wm, bmap, *, name, ride=None):
    T = zc.shape[0]
    W = SGU_WIDTH
    tt = ROW_TILE
    nch = tt // CHUNK
    nsteps = T // tt

    def body(z_ref, dp_ref, g_ref, b_ref, wm_ref, bm_ref, dz_ref, dg_ref, db_ref, dwm_ref, dbm_ref,
             s_scr, dvn_scr):
        step = pl.program_id(0)
        zu = z_ref[:, :W]
        zv = z_ref[:, W:]
        u, dgu = _gelu_and_grad(zu)
        xhat, rstd, vn = _sgu_norm(zv, g_ref, b_ref)
        vnb = vn.astype(BF16)
        dpf = dp_ref[...].astype(F32)
        ds = dpf * u

        @pl.when(step == 0)
        def _():
            dwm_ref[...] = jnp.zeros_like(dwm_ref)
            dbm_ref[...] = jnp.zeros_like(dbm_ref)

        for n in range(nch):
            rows = slice(n * CHUNK, (n + 1) * CHUNK)
            for gi in range(SGU_GROUPS):
                cols = slice(gi * LANES, (gi + 1) * LANES)
                s_scr[rows, cols] = _dot_nn(wm_ref[gi], vnb[rows, cols]) + bm_ref[:, cols]
                dsb = ds[rows, cols]
                dvn_scr[rows, cols] = _dot_tn(wm_ref[gi], dsb)
                dwm_ref[gi] += _dot_nt(dsb, vnb[rows, cols])
                dbm_ref[:, cols] += dsb
        dz_ref[:, :W] = (dpf * s_scr[...] * dgu).astype(BF16)
        dvn = dvn_scr[...]
        dxhat = dvn * g_ref[...]
        dv = rstd * (dxhat - jnp.mean(dxhat, axis=-1, keepdims=True)
                     - xhat * jnp.mean(dxhat * xhat, axis=-1, keepdims=True))
        _, dgv = _gelu_and_grad(zv)
        dz_ref[:, W:] = (dv * dgv).astype(BF16)
        dg_part = jnp.sum(dvn * xhat, axis=0, keepdims=True)
        db_part = jnp.sum(dvn, axis=0, keepdims=True)

        @pl.when(step == 0)
        def _():
            dg_ref[...] = dg_part
            db_ref[...] = db_part

        @pl.when(step > 0)
        def _():
            dg_ref[...] += dg_part
            db_ref[...] += db_part

        @pl.when(step == nsteps - 1)
        def _():
            for gi in range(SGU_GROUPS):
                cols = slice(gi * LANES, (gi + 1) * LANES)
                tot = jnp.sum(dbm_ref[:, cols], axis=1, keepdims=True)
                dbm_ref[:, cols] = jnp.broadcast_to(tot, (CHUNK, LANES))

    const2 = lambda r, c: pl.BlockSpec((r, c), lambda i: (0, 0))
    wspec = pl.BlockSpec((SGU_GROUPS, CHUNK, CHUNK), lambda i: (0, 0, 0))
    return _pcall(
        body, name=name, grid=(nsteps,),
        in_specs=[pl.BlockSpec((tt, 2 * W), lambda i: (i, 0)), pl.BlockSpec((tt, W), lambda i: (i, 0)),
                  const2(1, W), const2(1, W), wspec, const2(CHUNK, W)],
        out_specs=[pl.BlockSpec((tt, 2 * W), lambda i: (i, 0)), const2(1, W), const2(1, W), wspec, const2(CHUNK, W)],
        out_shape=[jax.ShapeDtypeStruct((T, 2 * W), BF16), jax.ShapeDtypeStruct((1, W), F32),
                   jax.ShapeDtypeStruct((1, W), F32), jax.ShapeDtypeStruct((SGU_GROUPS, CHUNK, CHUNK), F32),
                   jax.ShapeDtypeStruct((CHUNK, W), F32)],
        args=(zc, dp, ln_g, ln_b, wm, bmap), scratch=[pltpu.VMEM((tt, W), F32), pltpu.VMEM((tt, W), F32)], ride=ride)


def _rope_tables(positions):
    half = QK_ROPE // 2
    inv_freq = jnp.exp(-math.log(ROPE_BASE) * jnp.arange(half, dtype=F32) / half)
    ang = positions.reshape(-1).astype(F32)[:, None] * inv_freq
    cos = jnp.cos(ang)
    sin = jnp.sin(ang)
    n = ang.shape[0]
    tail = LANES - QK_NOPE - QK_ROPE
    cos_t = jnp.concatenate([jnp.ones((n, QK_NOPE), F32), cos, cos, jnp.ones((n, tail), F32)], axis=1)
    sin_t = jnp.concatenate([jnp.zeros((n, QK_NOPE), F32), -sin, sin, jnp.zeros((n, tail), F32)], axis=1)
    return cos_t, sin_t


SGU_GROUP_DIM = SGU_WIDTH // SGU_GROUPS
_O1, _O2, _O3, _O4 = Q_LORA, Q_LORA + KV_LORA, Q_LORA + KV_LORA + QK_ROPE, Q_LORA + KV_LORA + QK_ROPE + LRU_WIDTH
_A0, _A1, _A2 = 2 * LRU_WIDTH, 2 * LRU_WIDTH + Q_LORA, 2 * LRU_WIDTH + Q_LORA + KV_LORA
_A3 = _A2 + QK_NOPE
Z_Q_BLOCK, Z_KV_BLOCK, Z_KPE_BLOCK = _A0 // Q_LORA, _A1 // KV_LORA, _A2 // LANES


def _perm_w_in(w_in):
    zeros = lambda n: jnp.zeros((w_in.shape[0], n), w_in.dtype)
    return jnp.concatenate([w_in[:, _O3:_O4], w_in[:, _O4:], w_in[:, :_O1], w_in[:, _O1:_O2], zeros(QK_NOPE),
                            w_in[:, _O2:_O3], zeros(LANES - QK_NOPE - QK_ROPE)], axis=1)


def _unperm_w_in(w):
    return jnp.concatenate([w[:, _A0:_A1], w[:, _A1:_A2], w[:, _A3:_A3 + QK_ROPE], w[:, :LRU_WIDTH],
                            w[:, LRU_WIDTH:_A0]], axis=1)


def _head_blocks(w, d):
    r = w.shape[0]
    return jnp.pad(w.reshape(r, MLA_HEADS, d), ((0, 0), (0, 0), (0, LANES - d))).reshape(r, MLA_HEADS * LANES)


def _from_head_blocks(w, d):
    r = w.shape[0]
    return w.reshape(r, MLA_HEADS, LANES)[:, :, :d].reshape(r, MLA_HEADS * d)


def _split_kv(w_kv):
    r = w_kv.shape[0]
    w3 = w_kv.reshape(r, MLA_HEADS, QK_NOPE + V_HEAD)
    return _head_blocks(w3[:, :, :QK_NOPE].reshape(r, -1), QK_NOPE), w3[:, :, QK_NOPE:].reshape(r, -1)


def _join_kv(w_k, w_v):
    r = w_k.shape[0]
    return jnp.concatenate([_from_head_blocks(w_k, QK_NOPE).reshape(r, MLA_HEADS, QK_NOPE),
                            w_v.reshape(r, MLA_HEADS, V_HEAD)], axis=2).reshape(r, -1)


def _prep_small(w):
    p = {n: w[n] for n in w if n not in BIG}
    eye = jnp.eye(LRU_HEADS, dtype=F32)
    dense = lambda wg: (wg[:, :, None, :] * eye[:, None, :, None]).reshape(LRU_WIDTH, LRU_WIDTH).astype(BF16)
    p["wa_d"] = dense(w["ab_w_rg_a"][0])
    p["wx_d"] = dense(w["ab_w_rg_x"][0])
    causal = jnp.tril(jnp.ones((CHUNK, CHUNK), F32))
    p["wm"] = (w["c_w_s"][0] * causal).astype(BF16)
    p["bmap"] = jnp.repeat(w["c_b_s"][0].T, SGU_GROUP_DIM, axis=1)
    return p


def _prep_big(ab_w_in, ab_w_q_b, ab_w_kv_b):
    return {"w_in_p": _perm_w_in(ab_w_in).astype(BF16),
            "w_q_p": _head_blocks(ab_w_q_b, QK_NOPE + QK_ROPE).astype(BF16),
            "w_kv_p": jnp.concatenate(_split_kv(ab_w_kv_b), axis=1).astype(BF16)}


def _ffn_fwd(h, l, p, S, rides):
    hn = _rms_fwd(h, p["ffn_norm"][l], name=f"ffn{l}_norm")
    g = _mm(hn, p["ffn_gate_t"][l], tb=True, out_dtype=BF16, name=f"ffn{l}_gate", ride=rides.get(f"ffn{l}_gate"))
    u = _mm(hn, p["ffn_up_t"][l], tb=True, out_dtype=BF16, name=f"ffn{l}_up", ride=rides.get(f"ffn{l}_up"))
    out, act = _ffn_act_down(g, u, p["ffn_conv_w"][l], p["ffn_conv_b"][l][None], p["ffn_down"][l], h, S=S,
                             name=f"ffn{l}_down", ride=rides.get(f"ffn{l}_down"))
    return out, (hn, g, u, act)


def _ffn_bwd(dh, h_in, l, p, saved, S, rides, grads_ready, also_ready=None):
    hn, g, u, act = saved
    dw_down = _mm(act, dh, ta=True, out_dtype=BF16, name=f"ffn{l}_dwdown")
    dg, du, dcw, dcb = _ffn_act_bwd(g, u, dh, p["ffn_down"][l], p["ffn_conv_w"][l], p["ffn_conv_b"][l][None], S=S,
                                    name=f"ffn{l}_dactbwd", ride=rides.get(f"ffn{l}_dactbwd"))
    dhn = _mm(dg, p["ffn_gate_t"][l], name=f"ffn{l}_dhn_g")
    dhn = _mm(du, p["ffn_up_t"][l], res=dhn, out_dtype=BF16, name=f"ffn{l}_dhn_u")
    dw_gate_t = _mm(dg, hn, ta=True, out_dtype=BF16, name=f"ffn{l}_dwgate")
    dw_up_t = _mm(du, hn, ta=True, out_dtype=BF16, name=f"ffn{l}_dwup")
    grads_ready(l, {**(also_ready or {}), "ffn_gate_t": dw_gate_t, "ffn_up_t": dw_up_t, "ffn_down": dw_down})
    dh_in, dnorm = _rms_bwd(h_in, p["ffn_norm"][l], dhn, res=dh, name=f"ffn{l}_dnorm", ride=rides.get(f"ffn{l}_dnorm"))
    grads = dict(ffn_norm=dnorm[0], ffn_gate_t=dw_gate_t, ffn_up_t=dw_up_t, ffn_conv_w=dcw[:FFN_CONV],
                 ffn_conv_b=dcb[0], ffn_down=dw_down)
    return dh_in, grads


def _local_step(x, positions, target, p, rides=None, grads_ready=None):
    rides = {} if rides is None else rides
    grads_ready = grads_ready or (lambda layer, ready: None)
    B, S, D = x.shape
    T = B * S
    H = MLA_HEADS
    xf = x.reshape(T, D)
    tgt = target.reshape(T, D)
    cos, sin = _rope_tables(positions)

    hn0 = _rms_fwd(xf, p["ab_norm"][0], name="ab_norm", ride=rides.get("ab_norm"))
    z = _mm(hn0, p["w_in_p"], name="ab_in")
    cqn = _rms_fwd(z, p["ab_q_norm"][0], cb=Z_Q_BLOCK, name="q_norm")
    ckvn = _rms_fwd(z, p["ab_kv_norm"][0], cb=Z_KV_BLOCK, name="kv_norm")
    q = _mm(cqn, p["w_q_p"], name="q_up")
    kv = _mm(ckvn, p["w_kv_p"], out_dtype=BF16, name="kv_up")
    qs = _rope_q(q, cos, sin, name="q_rope")
    kk = _key_blocks(kv, z, cos, sin, kpe_block=Z_KPE_BLOCK, name="k_rope")
    att = dict(B=B, S=S, v_block0=H)
    o, lse = _attn_fwd(qs, kk, kv, name="attn_fwd", ride=rides.get("attn_fwd"), **att)
    lru_par = (p["ab_conv_w"][0], p["ab_conv_b"], p["wa_d"], p["ab_b_rg_a"], p["wx_d"], p["ab_b_rg_x"], p["ab_lambda"])
    y_lru, hs = _lru_fwd(z, *lru_par, S=S, name="lru_fwd", ride=rides.get("lru_fwd"))
    n_att = H * V_HEAD
    w_out_a, w_out_b = p["ab_w_out"][:n_att], p["ab_w_out"][n_att:]
    h1 = _mm(y_lru, w_out_b, res=_mm(o, w_out_a, res=xf, name="ab_out_a"), name="ab_out_b")
    h2, ffn0 = _ffn_fwd(h1, 0, p, S, rides)

    hn2 = _rms_fwd(h2, p["c_norm"][0], name="c_norm")
    zc = _mm(hn2, p["c_w_in_t"], tb=True, name="c_in")
    pg = _sgu_fwd(zc, p["c_ln_g"], p["c_ln_b"], p["wm"], p["bmap"], name="sgu_fwd")
    h3 = _mm(pg, p["c_w_out"], res=h2, name="c_out")
    h4, ffn1 = _ffn_fwd(h3, 1, p, S, rides)

    loss_row, dh4, dfinal = _final_fwd_bwd(h4, p["final_norm"], tgt, name="final")

    dh3, g_ffn1 = _ffn_bwd(dh4, h3, 1, p, ffn1, S, rides, grads_ready)
    dpg = _mm(dh3, p["c_w_out"], tb=True, out_dtype=BF16, name="c_dp")
    dw_c_out = _mm(pg, dh3, ta=True, out_dtype=BF16, name="c_dwout")
    dzc, dlng, dlnb, dwm, dbm = _sgu_bwd(zc, dpg, p["c_ln_g"], p["c_ln_b"], p["wm"], p["bmap"], name="sgu_bwd",
                                         ride=rides.get("sgu_bwd"))
    dhn2 = _mm(dzc, p["c_w_in_t"], out_dtype=BF16, name="c_dhn")
    dw_c_in_t = _mm(dzc, hn2, ta=True, out_dtype=BF16, name="c_dwin")
    dh2, dcnorm = _rms_bwd(h2, p["c_norm"][0], dhn2, res=dh3, name="c_dnorm")
    dh1, g_ffn0 = _ffn_bwd(dh2, h1, 0, p, ffn0, S, rides, grads_ready, {"c_w_in_t": dw_c_in_t, "c_w_out": dw_c_out})

    do = _mm(dh1, w_out_a, tb=True, name="ab_do")
    dy_lru = _mm(dh1, w_out_b, tb=True, out_dtype=BF16, name="ab_dylru")
    dw_out = jnp.concatenate([_mm(o, dh1, ta=True, out_dtype=BF16, name="ab_dwout_a"),
                              _mm(y_lru, dh1, ta=True, out_dtype=BF16, name="ab_dwout_b")], axis=0)
    dq, delta = _attn_dq(qs, kk, kv, o, lse, do, name="attn_dq", ride=rides.get("attn_dq"), **att)
    nq = S // min(ATT_BLOCK, S)
    rows = lambda a: a.reshape(B, H, nq, S // nq)
    dk, dv = _attn_dkv(qs, kk, kv, rows(lse), rows(delta), do, name="attn_dkv", ride=rides.get("attn_dkv"), **att)
    dq_full = _rope_q_bwd(dq, cos, sin, name="q_rope_bwd")
    dkr = _key_rope_bwd(dk, cos, sin, name="k_rope_bwd")
    n_key = H * LANES
    w_k_p, w_v_p = p["w_kv_p"][:, :n_key], p["w_kv_p"][:, n_key:]
    dcqn = _mm(dq_full, p["w_q_p"], tb=True, name="q_dlat")
    dw_q_p = _mm(cqn, dq_full, ta=True, out_dtype=BF16, name="q_dw")
    dckvn = _mm(dv, w_v_p, tb=True, res=_mm(dk, w_k_p, tb=True, name="k_dlat"), name="v_dlat")
    dw_k_p = _mm(ckvn, dk, ta=True, out_dtype=BF16, name="k_dw")
    dw_v_p = _mm(ckvn, dv, ta=True, out_dtype=BF16, name="v_dw")
    dcq, dqnorm = _rms_bwd(z, p["ab_q_norm"][0], dcqn, cb=Z_Q_BLOCK, out_dtype=BF16, name="q_dnorm")
    dckv, dkvnorm = _rms_bwd(z, p["ab_kv_norm"][0], dckvn, cb=Z_KV_BLOCK, out_dtype=BF16, name="kv_dnorm")
    dxl, dgate, dcw, dcb, dwa, dba, dwx, dbx, dlam = _lru_bwd(z, hs, dy_lru, *lru_par, S=S, name="lru_bwd")
    dz = jnp.concatenate([dxl, dgate, dcq, dckv, dkr], axis=1)
    dhn0 = _mm(dz, p["w_in_p"], tb=True, out_dtype=BF16, name="ab_dhn")
    dw_in_p = _mm(hn0, dz, ta=True, out_dtype=BF16, name="ab_dwin")
    dx, dabnorm = _rms_bwd(xf, p["ab_norm"][0], dhn0, res=dh1, name="ab_dnorm")

    blocks = lambda dd: jnp.stack([dd[i * LRU_BLOCK:(i + 1) * LRU_BLOCK, i * LRU_BLOCK:(i + 1) * LRU_BLOCK]
                                   for i in range(LRU_HEADS)])
    causal = jnp.tril(jnp.ones((CHUNK, CHUNK), F32))
    grads = {
        "ab_norm": dabnorm, "w_in_p": dw_in_p, "ab_q_norm": dqnorm, "w_q_p": dw_q_p,
        "ab_kv_norm": dkvnorm, "w_k_p": dw_k_p, "w_v_p": dw_v_p, "ab_conv_w": dcw[:LRU_CONV][None], "ab_conv_b": dcb,
        "ab_w_rg_a": blocks(dwa)[None], "ab_b_rg_a": dba, "ab_w_rg_x": blocks(dwx)[None], "ab_b_rg_x": dbx,
        "ab_lambda": dlam, "ab_w_out": dw_out,
        "c_norm": dcnorm, "c_w_in_t": dw_c_in_t, "c_ln_g": dlng, "c_ln_b": dlnb,
        "c_w_s": (dwm * causal)[None], "c_b_s": dbm[:, ::SGU_GROUP_DIM].T[None], "c_w_out": dw_c_out,
        "final_norm": dfinal[0],
    }
    for name in ("ffn_norm", "ffn_conv_w", "ffn_conv_b"):
        grads[name] = jnp.stack([g_ffn0[name], g_ffn1[name]])
    for name in ("ffn_gate_t", "ffn_up_t", "ffn_down"):
        grads[name] = [g_ffn0[name], g_ffn1[name]]
    return loss_row, dx.reshape(B, S, D), grads


ANY = pl.BlockSpec(memory_space=pl.ANY)


def _place():
    x, y, c = lax.axis_index("x"), lax.axis_index("y"), lax.axis_index("c")
    chips = [(1 - x, y), (x, 1 - y), (1 - x, 1 - y)]
    return x, y, c, 2 * x + y, (x, y, 1 - c), chips


def _remote(src, dst, send_sems, recv_sems, k, to):
    return pltpu.make_async_remote_copy(src_ref=src, dst_ref=dst, send_sem=send_sems.at[k], recv_sem=recv_sems.at[k],
                                        device_id=to, device_id_type=MESH)


class _Exchange:
    def __init__(self, arrs, out_shapes, n_sems, start, finish):
        self.arrs, self.out_shapes, self.n_sems, self.start, self.finish = list(arrs), out_shapes, n_sems, start, finish

    @property
    def in_specs(self):
        return [ANY] * len(self.arrs)

    @property
    def out_specs(self):
        return [ANY] * len(self.out_shapes)

    @property
    def scratch(self):
        return [pltpu.SemaphoreType.DMA((self.n_sems,)), pltpu.SemaphoreType.DMA((self.n_sems,))]

    def split(self, refs):
        n = len(self.arrs)
        return refs[:n], refs[n:n + len(self.out_shapes)], refs[-2], refs[-1]

    def run(self, name):
        def body(*refs):
            parts = self.split(refs)
            self.start(*parts)
            self.finish(*parts)

        return pl.pallas_call(body, name=name, in_specs=self.in_specs, out_specs=self.out_specs,
                              out_shape=self.out_shapes, scratch_shapes=self.scratch)(*self.arrs)


def _put(buf, piece, idx, axis):
    return lax.dynamic_update_slice_in_dim(buf, jnp.expand_dims(piece, axis).astype(buf.dtype), idx, axis)


def _all_gather(arrs):
    n = len(arrs)

    def start(ins, outs, send_sems, recv_sems):
        x, y, c, j, sib, chips = _place()
        for i in range(n):
            for k, (cx, cy) in enumerate(chips):
                _remote(ins[i].at[:, c], outs[i].at[:, j, c], send_sems, recv_sems, 6 * i + k, (cx, cy, c)).start()

    def finish(ins, outs, send_sems, recv_sems):
        x, y, c, j, sib, chips = _place()
        passed = []
        for i in range(n):
            for k, (cx, cy) in enumerate(chips):
                got = outs[i].at[:, 2 * cx + cy, c]
                _remote(got, got, send_sems, recv_sems, 6 * i + k, (cx, cy, c)).wait_recv()
                cp = _remote(got, got, send_sems, recv_sems, 6 * i + 3 + k, sib)
                cp.start()
                passed.append(cp)
        for i in range(n):
            for k, (cx, cy) in enumerate(chips):
                got = outs[i].at[:, 2 * cx + cy, 1 - c]
                _remote(got, got, send_sems, recv_sems, 6 * i + 3 + k, sib).wait_recv()
                _remote(ins[i].at[:, c], ins[i].at[:, c], send_sems, recv_sems, 6 * i + k, sib).wait_send()
        for cp in passed:
            cp.wait_send()

    shapes = [jax.ShapeDtypeStruct((a.shape[0], N_CHIPS) + a.shape[1:], a.dtype) for a in arrs]
    return _Exchange(arrs, shapes, 6 * n, start, finish)


class _Offset:
    def __init__(self, sems, k0):
        self.sems, self.k0 = sems, k0

    @property
    def at(self):
        return self

    def __getitem__(self, k):
        return self.sems.at[self.k0 + k]


def _merge(a, b):
    n_in, n_out = len(a.arrs), len(a.out_shapes)

    def both(fa, fb):
        def f(ins, outs, send_sems, recv_sems):
            fa(ins[:n_in], outs[:n_out], send_sems, recv_sems)
            fb(ins[n_in:], outs[n_out:], _Offset(send_sems, a.n_sems), _Offset(recv_sems, a.n_sems))
        return f

    return _Exchange(a.arrs + b.arrs, a.out_shapes + b.out_shapes, a.n_sems + b.n_sems,
                     both(a.start, b.start), both(a.finish, b.finish))


def _pair_swap(arrs):
    n = len(arrs)

    def start(ins, outs, send_sems, recv_sems):
        x, y, c, j, sib, chips = _place()
        for i in range(n):
            _remote(ins[i].at[:, 1 - c], outs[i], send_sems, recv_sems, i, sib).start()

    def finish(ins, outs, send_sems, recv_sems):
        x, y, c, j, sib, chips = _place()
        for i in range(n):
            _remote(ins[i].at[:, 1 - c], outs[i], send_sems, recv_sems, i, sib).wait()

    shapes = [jax.ShapeDtypeStruct((a.shape[0],) + a.shape[2:], a.dtype) for a in arrs]
    return _Exchange(arrs, shapes, n, start, finish)


def _pair_send(arrs):
    n = len(arrs)

    def start(ins, outs, send_sems, recv_sems):
        x, y, c, j, sib, chips = _place()
        for i in range(n):
            _remote(ins[i], outs[i], send_sems, recv_sems, i, sib).start()

    def finish(ins, outs, send_sems, recv_sems):
        x, y, c, j, sib, chips = _place()
        for i in range(n):
            _remote(ins[i], outs[i], send_sems, recv_sems, i, sib).wait()

    shapes = [jax.ShapeDtypeStruct(a.shape, a.dtype) for a in arrs]
    return _Exchange(arrs, shapes, n, start, finish)


def _chip_exchange(arrs, *, scatter):
    n = len(arrs)

    def copies(ins, outs, send_sems, recv_sems):
        x, y, c, j, sib, chips = _place()
        return [(_remote(ins[i].at[2 * cx + cy] if scatter else ins[i], outs[i].at[j], send_sems, recv_sems,
                         3 * i + k, (cx, cy, c)),
                 _remote(outs[i].at[2 * cx + cy], outs[i].at[2 * cx + cy], send_sems, recv_sems, 3 * i + k, (cx, cy, c)))
                for i in range(n) for k, (cx, cy) in enumerate(chips)]

    def start(*refs):
        for out, _ in copies(*refs):
            out.start()

    def finish(*refs):
        for out, back in copies(*refs):
            back.wait_recv()
            out.wait_send()

    shapes = [jax.ShapeDtypeStruct((N_CHIPS,) + a.shape[-2:], a.dtype) for a in arrs]
    return _Exchange(arrs, shapes, 3 * n, start, finish)


FLAT_ROWS = 512


def _add2(a, b, *, out_dtype, name):
    n, R, L = a.shape
    tr = _tile(R, FLAT_ROWS, 16)

    def body(a_ref, b_ref, o_ref):
        o_ref[...] = (a_ref[...].astype(F32) + b_ref[...].astype(F32)).astype(out_dtype)

    spec = pl.BlockSpec((n, tr, L), lambda i: (0, i, 0))
    return pl.pallas_call(
        body, name=name, grid=(R // tr,), in_specs=[spec, spec], out_specs=spec,
        out_shape=jax.ShapeDtypeStruct(a.shape, out_dtype), compiler_params=_cparams(("parallel",)),
    )(a, b)


def _sum_slots(buf, *, name):
    n, R, L = buf.shape
    tr = _tile(R, FLAT_ROWS, 16)

    def body(b_ref, o_ref):
        acc = b_ref[0].astype(F32)
        for k in range(1, n):
            acc = acc + b_ref[k].astype(F32)
        o_ref[...] = acc

    return pl.pallas_call(
        body, name=name, grid=(R // tr,), in_specs=[pl.BlockSpec((n, tr, L), lambda i: (0, i, 0))],
        out_specs=pl.BlockSpec((tr, L), lambda i: (i, 0)),
        out_shape=jax.ShapeDtypeStruct((R, L), F32), compiler_params=_cparams(("parallel",)),
    )(buf)


def _adamw_update(w, g, m, v):
    c1 = 1.0 - ADAM_B1 ** ADAM_STEP
    c2 = 1.0 - ADAM_B2 ** ADAM_STEP
    m = ADAM_B1 * m + (1.0 - ADAM_B1) * g
    v = ADAM_B2 * v + (1.0 - ADAM_B2) * (g * g)
    return -ADAM_LR * ((m / c1) / (jnp.sqrt(v / c2) + ADAM_EPS) + ADAM_WD * w), m, v


def _adamw_halves(w, m, v, own, other, *, name):
    NL, R, L = w.shape
    h = R // 2
    tr = _tile(h, FLAT_ROWS, 16)
    nt = h // tr

    def body(*refs):
        w_ref, m_ref, v_ref = refs[:3]
        own_refs, other_refs = refs[3:3 + NL], refs[3 + NL:3 + 2 * NL]
        d_ref, nm_ref, nv_ref, g_ref = refs[3 + 2 * NL:]
        layer, half = pl.program_id(0), pl.program_id(1)
        mine = half == lax.axis_index("c")
        g = jnp.where(mine, own_refs[0][...], other_refs[0][...])
        for l in range(1, NL):
            g = jnp.where(layer == l, jnp.where(mine, own_refs[l][...], other_refs[l][...]), g)
        d, mm, vv = _adamw_update(w_ref[0], g, m_ref[0], v_ref[0])
        d_ref[0], nm_ref[0], nv_ref[0], g_ref[0] = d, mm, vv, g

    spec = pl.BlockSpec((1, tr, L), lambda l, hh, i: (l, hh * nt + i, 0))
    part = pl.BlockSpec((tr, L), lambda l, hh, i: (i, 0))
    sh = jax.ShapeDtypeStruct((NL, R, L), F32)
    return pl.pallas_call(
        body, name=name, grid=(NL, 2, nt), in_specs=[spec] * 3 + [part] * (2 * NL), out_specs=[spec] * 4,
        out_shape=[sh] * 4, compiler_params=_cparams(("parallel", "parallel", "parallel")),
    )(w, m, v, *own, *other)


def _adamw(w, g, m, v, *, name):
    NL, R, L = w.shape
    tr = _tile(R, FLAT_ROWS, 16)

    def body(w_ref, g_ref, m_ref, v_ref, d_ref, nm_ref, nv_ref):
        d_ref[...], nm_ref[...], nv_ref[...] = _adamw_update(w_ref[...], g_ref[...], m_ref[...], v_ref[...])

    spec = pl.BlockSpec((1, tr, L), lambda l, i: (l, i, 0))
    sh = jax.ShapeDtypeStruct((NL, R, L), F32)
    return pl.pallas_call(
        body, name=name, grid=(NL, R // tr), in_specs=[spec] * 4, out_specs=[spec] * 3, out_shape=[sh] * 3,
        compiler_params=_cparams(("parallel", "parallel")),
    )(w, g, m, v)


WEIGHT_NAMES = ["ab_norm", "ab_w_in", "ab_q_norm", "ab_w_q_b", "ab_kv_norm", "ab_w_kv_b", "ab_conv_w", "ab_conv_b",
                "ab_w_rg_a", "ab_b_rg_a", "ab_w_rg_x", "ab_b_rg_x", "ab_lambda", "ab_w_out", "c_norm", "c_w_in",
                "c_ln_g", "c_ln_b", "c_w_s", "c_b_s", "c_w_out", "ffn_norm", "ffn_w_gate", "ffn_w_up", "ffn_conv_w",
                "ffn_conv_b", "ffn_w_down", "final_norm"]
BIG = {"ab_w_in": 2, "ab_w_q_b": 2, "ab_w_kv_b": 2, "ab_w_out": 1, "c_w_in": 2, "c_w_out": 1,
       "ffn_w_gate": 2, "ffn_w_up": 2, "ffn_w_down": 1}
SMALL_SHARDED = {"ab_conv_w": 2, "c_norm": 1, "c_ln_g": 1, "c_ln_b": 1, "ffn_conv_w": 2}
SMALL_REPLICATED = [n for n in WEIGHT_NAMES if n not in BIG and n not in SMALL_SHARDED]


def _rows(n_elems, mult):
    r = -(-n_elems // LANES)
    return -(-r // mult) * mult


def _flat(parts, rows):
    flat = jnp.concatenate([a.reshape(-1) for a in parts])
    return jnp.pad(flat, (0, rows * LANES - flat.shape[0])).reshape(rows, LANES)


def _unflat(flat, shapes):
    flat = flat.reshape(-1)
    out, off = [], 0
    for s in shapes:
        n = math.prod(s)
        out.append(flat[off:off + n].reshape(s))
        off += n
    return out


def _join_shards(a, axis):
    a = jnp.moveaxis(a, 0, axis)
    return a.reshape(a.shape[:axis] + (a.shape[axis] * a.shape[axis + 1],) + a.shape[axis + 2:])


def kernel(x, positions, ab_norm, ab_w_in, ab_q_norm, ab_w_q_b, ab_kv_norm, ab_w_kv_b, ab_conv_w, ab_conv_b, ab_w_rg_a, ab_b_rg_a, ab_w_rg_x, ab_b_rg_x, ab_lambda, ab_w_out, c_norm, c_w_in, c_ln_g, c_ln_b, c_w_s, c_b_s, c_w_out, ffn_norm, ffn_w_gate, ffn_w_up, ffn_conv_w, ffn_conv_b, ffn_w_down, final_norm, loss_target, m_ab_norm, m_ab_w_in, m_ab_q_norm, m_ab_w_q_b, m_ab_kv_norm, m_ab_w_kv_b, m_ab_conv_w, m_ab_conv_b, m_ab_w_rg_a, m_ab_b_rg_a, m_ab_w_rg_x, m_ab_b_rg_x, m_ab_lambda, m_ab_w_out, m_c_norm, m_c_w_in, m_c_ln_g, m_c_ln_b, m_c_w_s, m_c_b_s, m_c_w_out, m_ffn_norm, m_ffn_w_gate, m_ffn_w_up, m_ffn_conv_w, m_ffn_conv_b, m_ffn_w_down, m_final_norm, v_ab_norm, v_ab_w_in, v_ab_q_norm, v_ab_w_q_b, v_ab_kv_norm, v_ab_w_kv_b, v_ab_conv_w, v_ab_conv_b, v_ab_w_rg_a, v_ab_b_rg_a, v_ab_w_rg_x, v_ab_b_rg_x, v_ab_lambda, v_ab_w_out, v_c_norm, v_c_w_in, v_c_ln_g, v_c_ln_b, v_c_w_s, v_c_b_s, v_c_w_out, v_ffn_norm, v_ffn_w_gate, v_ffn_w_up, v_ffn_conv_w, v_ffn_conv_b, v_ffn_w_down, v_final_norm):
    given = dict(locals())
    w = {n: given[n] for n in WEIGHT_NAMES}
    m = {n: given["m_" + n] for n in WEIGHT_NAMES}
    v = {n: given["v_" + n] for n in WEIGHT_NAMES}
    c = lax.axis_index("c")
    chip = 2 * lax.axis_index("x") + lax.axis_index("y")

    halves = lambda a: a.reshape(a.shape[0], 2, a.shape[1] // 2, a.shape[2])
    tr = lambda a: jnp.swapaxes(a, 1, 2)
    send = {"ab_w_in": w["ab_w_in"], "ab_w_q_b": w["ab_w_q_b"], "ab_w_kv_b": w["ab_w_kv_b"], "ab_w_out": w["ab_w_out"],
            "c_w_in": tr(w["c_w_in"]), "c_w_out": w["c_w_out"], "ffn_w_gate": tr(w["ffn_w_gate"]),
            "ffn_w_up": tr(w["ffn_w_up"]), "ffn_w_down": w["ffn_w_down"]}
    small_rows = _rows(sum(w[n].size for n in SMALL_SHARDED), 16)
    small_sh = _flat([w[n] for n in SMALL_SHARDED], small_rows).reshape(1, 2, small_rows // 2, LANES)
    first_names = ["ab_w_in", "ab_w_q_b", "ab_w_kv_b", "ab_w_out"]
    mine = {n: halves(send[n].astype(BF16)) for n in BIG}

    def put_own(own, arrived):
        a = _put(arrived, own, chip, 1)
        return a.reshape(a.shape[0], -1, a.shape[-1])

    p = {"ab_norm": w["ab_norm"], "ffn_gate_t": {}, "ffn_up_t": {}, "ffn_down": {}}
    first = [mine[n] for n in first_names] + [small_sh]

    def first_arrived(got):
        full = {n: put_own(o, a) for n, o, a in zip(first_names + ["small"], first, got)}
        unshard = lambda a: jnp.swapaxes(a.reshape(N_CHIPS, -1, a.shape[-1]), 0, 1).reshape(-1, N_CHIPS * a.shape[-1])
        p.update(_prep_big(unshard(full["ab_w_in"][0]), unshard(full["ab_w_q_b"][0]), unshard(full["ab_w_kv_b"][0])))
        p["ab_w_out"] = full["ab_w_out"][0]
        small_full = dict(w)
        off = 0
        small_got = full["small"].reshape(N_CHIPS, -1)
        for n, ax in SMALL_SHARDED.items():
            seg = small_got[:, off:off + w[n].size].reshape((N_CHIPS,) + w[n].shape)
            small_full[n] = _join_shards(seg, ax)
            off += w[n].size
        p.update(_prep_small(small_full))

    def weights_ride(parts):
        def sink(arrived):
            for (own, setter), a in zip(parts, arrived):
                setter(put_own(own, a)[0])
        return _all_gather([own for own, _ in parts]), sink

    ffn_keys = {"ffn_gate_t": "ffn_w_gate", "ffn_up_t": "ffn_w_up", "ffn_down": "ffn_w_down"}
    ffn_part = lambda key, l: (mine[ffn_keys[key]][l:l + 1], functools.partial(p[key].__setitem__, l))
    rides = {
        "ab_norm": (_all_gather(first), first_arrived),
        "attn_fwd": weights_ride([ffn_part("ffn_gate_t", 0), ffn_part("ffn_up_t", 0)]),
        "lru_fwd": weights_ride([ffn_part("ffn_down", 0)]),
        "ffn0_gate": weights_ride([ffn_part("ffn_gate_t", 1)]),
        "ffn0_up": weights_ride([ffn_part("ffn_up_t", 1)]),
        "ffn0_down": weights_ride([ffn_part("ffn_down", 1),
                                   (mine["c_w_in"], functools.partial(p.__setitem__, "c_w_in_t")),
                                   (mine["c_w_out"], functools.partial(p.__setitem__, "c_w_out"))]),
    }

    def chip_sums(pair, arrived, tag):
        own = [lax.dynamic_index_in_dim(a, chip, axis=0, keepdims=False) for a in pair]
        return [_sum_slots(_put(a, o, chip, 0), name=f"grad_chip_sum_{tag}{i}") for i, (a, o) in enumerate(zip(arrived, own))]

    half_of = {}

    def grads_ready(layer, ready):
        if layer == 1:
            named = {"gate1": ready["ffn_gate_t"], "up1": ready["ffn_up_t"], "down1": ready["ffn_down"]}
            hosts = {"sgu_bwd": ["down1"], "ffn0_dactbwd": ["gate1", "up1"]}
        else:
            named = {"c_in": ready["c_w_in_t"], "c_out": ready["c_w_out"], "gate0": ready["ffn_gate_t"],
                     "up0": ready["ffn_up_t"], "down0": ready["ffn_down"]}
            hosts = {"attn_dq": ["c_in", "c_out", "down0"], "attn_dkv": ["gate0", "up0"]}
        tag = f"f{layer}"
        sharded = [a.reshape(N_CHIPS, 2, -1, a.shape[-1]) for a in named.values()]

        def paired(from_sib):
            own = [lax.dynamic_index_in_dim(a, c, axis=1, keepdims=False) for a in sharded]
            pair = {k: _add2(a, b, out_dtype=BF16, name=f"grad_pair_add_{tag}{i}")
                    for i, (k, a, b) in enumerate(zip(named, own, from_sib))}
            for kernel_name, keys in hosts.items():
                def sink(arrived, keys=keys, kernel_name=kernel_name):
                    half_of.update(zip(keys, chip_sums([pair[k] for k in keys], arrived, f"{tag}_{kernel_name}")))
                rides[kernel_name] = (_chip_exchange([pair[k] for k in keys], scatter=True), sink)

        rides[f"ffn{layer}_dnorm"] = (_pair_swap(sharded), paired)

    loss_row, grad_x, g = _local_step(x, positions, loss_target, p, rides, grads_ready)

    cols = lambda a, n: jnp.swapaxes(a.reshape(a.shape[0], N_CHIPS, n), 0, 1)
    n_in, n_q, n_kv = w["ab_w_in"].shape[2], w["ab_w_q_b"].shape[2], w["ab_w_kv_b"].shape[2]
    small_names = SMALL_REPLICATED + list(SMALL_SHARDED)
    rs = _rows(sum(g[n].size for n in small_names) + LANES, FLAT_ROWS)
    small = _flat([loss_row] + [g[n] for n in small_names], rs)
    slot = (jnp.arange(2) == c)[:, None, None]
    last = [cols(_unperm_w_in(g["w_in_p"]), n_in), cols(_from_head_blocks(g["w_q_p"], QK_NOPE + QK_ROPE), n_q),
            cols(_join_kv(g["w_k_p"], g["w_v_p"]), n_kv), g["ab_w_out"]]
    last = [a.reshape(N_CHIPS, 2, -1, a.shape[-1]) for a in last]
    *from_sib, small_sib = _merge(_pair_swap(last), _pair_send([small])).run("tail_pair")
    own = [lax.dynamic_index_in_dim(a, c, axis=1, keepdims=False) for a in last]
    pair = [_add2(a, b, out_dtype=BF16, name=f"grad_pair_add_b{i}") for i, (a, b) in enumerate(zip(own, from_sib))]
    pair_small = _sum_slots(jnp.where(slot, small[None], small_sib[None]), name="small_pair_sum")
    my_small = lax.dynamic_index_in_dim(pair_small.reshape(2, rs // 2, LANES), c, axis=0, keepdims=False)
    *arrived, all_small = _merge(_chip_exchange(pair, scatter=True), _chip_exchange([my_small], scatter=False)).run("tail_chip")
    half_of.update(zip(["in", "q", "kv", "out"], chip_sums(pair, arrived, "b")))
    half_of["small"] = _sum_slots(_put(all_small, my_small, chip, 0), name="small_chip_sum")
    keys = ("in", "q", "kv", "out", "c_in", "c_out", "gate0", "gate1", "up0", "up1", "down0", "down1", "small")
    other_half = dict(zip(keys, _pair_send([half_of[k] for k in keys]).run("grad_pair_share")))
    small_sum = jnp.where(slot, half_of["small"][None], other_half["small"][None]).reshape(rs, LANES)
    whole = lambda k: jnp.where(slot, half_of[k][None], other_half[k][None]).reshape(-1, half_of[k].shape[-1])
    grads_t = {"ab_w_in": whole("in").T[None], "ab_w_q_b": whole("q").T[None]}
    grads = {"ab_w_kv_b": whole("kv")[None], "c_w_in": whole("c_in").T[None], **{n: tr(a) for n, a in grads_t.items()}}
    by_halves = {"ab_w_out": (("out",), False), "c_w_out": (("c_out",), False), "ffn_w_down": (("down0", "down1"), False),
                 "ffn_w_gate": (("gate0", "gate1"), True), "ffn_w_up": (("up0", "up1"), True)}

    small_parts = _unflat(small_sum, [(1, LANES)] + [g[n].shape for n in small_names])
    loss = small_parts[0][0, 0]
    for n, a in zip(small_names, small_parts[1:]):
        if n in SMALL_SHARDED:
            ax = SMALL_SHARDED[n]
            a = lax.dynamic_slice_in_dim(a, chip * w[n].shape[ax], w[n].shape[ax], axis=ax)
        grads[n] = a.reshape(w[n].shape)

    delta, new_m, new_v = {}, {}, {}
    for n in BIG:
        if n in by_halves:
            ks, transposed = by_halves[n]
            view = tr if transposed else (lambda a: a)
            out = _adamw_halves(view(w[n]), view(m[n]), view(v[n]), [half_of[k] for k in ks], [other_half[k] for k in ks],
                                name=f"adamw_{n}")
            delta[n], new_m[n], new_v[n], grads[n] = (view(a) for a in out)
        elif n in grads_t:
            out = _adamw(tr(w[n]), grads_t[n], tr(m[n]), tr(v[n]), name=f"adamw_{n}")
            delta[n], new_m[n], new_v[n] = (tr(a) for a in out)
        else:
            delta[n], new_m[n], new_v[n] = _adamw(w[n], grads[n], m[n], v[n], name=f"adamw_{n}")
    small_all = [n for n in WEIGHT_NAMES if n not in BIG]
    ra = _rows(sum(w[n].size for n in small_all), FLAT_ROWS)
    pack = lambda d: _flat([d[n] for n in small_all], ra)[None]
    out = _adamw(pack(w), pack(grads), pack(m), pack(v), name="adamw_small")
    shapes = [w[n].shape for n in small_all]
    for d, flat in zip((delta, new_m, new_v), out):
        d.update(zip(small_all, _unflat(flat, shapes)))
    return (loss, grad_x, *[grads[n] for n in WEIGHT_NAMES], *[delta[n] for n in WEIGHT_NAMES],
            *[new_m[n] for n in WEIGHT_NAMES], *[new_v[n] for n in WEIGHT_NAMES])
```

```python
import functools
import math

import jax
import jax.numpy as jnp
from jax import lax
from jax.experimental import pallas as pl
from jax.experimental.pallas import tpu as pltpu

F32 = jnp.float32
BF16 = jnp.bfloat16
MESH = pl.DeviceIdType.MESH

D_MODEL = 1024
MLA_HEADS = 8
Q_LORA = 256
KV_LORA = 128
QK_NOPE = 64
QK_ROPE = 32
V_HEAD = 64
LRU_WIDTH = 512
LRU_HEADS = 8
LRU_BLOCK = 64
LRU_CONV = 4
LRU_C = 8.0
CHUNK = 128
SGU_GROUPS = 8
SGU_WIDTH = 1024
D_FF = 2816
FFN_CONV = 3
NORM_EPS = 1e-6
ROPE_BASE = 10000.0
AB_IN_PAD = 1536
ADAM_LR = 0.001
ADAM_B1 = 0.9
ADAM_B2 = 0.999
ADAM_EPS = 1e-08
ADAM_WD = 0.01
ADAM_STEP = 10

N_CHIPS = 4
LANES = 128
VMEM_LIMIT = 56 * 1024 * 1024
ROW_TILE = 256
NORM_TILE = 1024
MM_TM, MM_TN, MM_TK = 1024, 1536, 2816
MM_TM_T, MM_TK_T = 1408, 1024
GELU_C = math.sqrt(2.0 / math.pi)


def _cparams(sem):
    return pltpu.CompilerParams(dimension_semantics=sem, vmem_limit_bytes=VMEM_LIMIT)


def _tile(n, target, mult=LANES):
    t = (min(n, target) // mult) * mult
    while t >= mult:
        if n % t == 0:
            return t
        t -= mult
    return n


GELU_K = GELU_C * 0.044715


def _gelu(x):
    t = jnp.tanh(x * (GELU_C + GELU_K * (x * x)))
    hx = 0.5 * x
    return hx + hx * t


def _gelu_and_grad(x):
    x2 = x * x
    t = jnp.tanh(x * (GELU_C + GELU_K * x2))
    hx = 0.5 * x
    dg = (0.5 + 0.5 * t) + (hx * (1.0 - t * t)) * (GELU_C + (3.0 * GELU_K) * x2)
    return hx + hx * t, dg


def _sigmoid(x):
    return 1.0 / (1.0 + jnp.exp(-x))


def _shift_rows(x, d, fill_rows):
    ext = jnp.concatenate([fill_rows, x], axis=0)
    return pltpu.roll(ext, d, 0)[8:]


def _shift_rows_up(x, d, fill_rows):
    n = x.shape[0]
    ext = jnp.concatenate([x, fill_rows], axis=0)
    return pltpu.roll(ext, n + 8 - d, 0)[:n]


def _dot(a, b, dims):
    return lax.dot_general(a.astype(BF16), b.astype(BF16), (dims, ((), ())), preferred_element_type=F32)


def _dot_nn(a, b):
    return _dot(a, b, ((1,), (0,)))


def _dot_nt(a, b):
    return _dot(a, b, ((1,), (1,)))


def _dot_tn(a, b):
    return _dot(a, b, ((0,), (0,)))


def _mm(a, b, *, name, ta=False, tb=False, res=None, out_dtype=F32, ride=None):
    if ta:
        K, M = a.shape
    else:
        M, K = a.shape
    N = b.shape[0] if tb else b.shape[1]
    tm = _tile(M, MM_TM_T if ta else (MM_TM if K <= MM_TM else MM_TM // 2), LANES if ta else 8)
    tn = _tile(N, MM_TN, LANES)
    tk = _tile(K, MM_TK_T if ta else MM_TK, LANES)
    nk = K // tk
    a_spec = pl.BlockSpec((tk, tm), lambda j, i, k: (k, i)) if ta else pl.BlockSpec((tm, tk), lambda j, i, k: (i, k))
    b_spec = pl.BlockSpec((tn, tk), lambda j, i, k: (j, k)) if tb else pl.BlockSpec((tk, tn), lambda j, i, k: (k, j))
    o_spec = pl.BlockSpec((tm, tn), lambda j, i, k: (i, j))
    dims = ((0,) if ta else (1,), (1,) if tb else (0,))
    has_res = res is not None

    def body(*refs):
        a_ref, b_ref = refs[:2]
        r_ref = refs[2] if has_res else None
        o_ref = refs[3] if has_res else refs[2]
        p = _dot(a_ref[...], b_ref[...], dims)

        def finish(r):
            if has_res:
                r = r + r_ref[...].astype(F32)
            o_ref[...] = r.astype(out_dtype)

        if nk == 1:
            finish(p)
            return
        acc_ref = refs[-1]
        k = pl.program_id(2)

        @pl.when(k == 0)
        def _():
            acc_ref[...] = p

        @pl.when(jnp.logical_and(k > 0, k < nk - 1))
        def _():
            acc_ref[...] += p

        @pl.when(k == nk - 1)
        def _():
            finish(acc_ref[...] + p)

    in_specs = [a_spec, b_spec] + ([o_spec] if has_res else [])
    args = (a, b) + ((res,) if has_res else ())
    return _pcall(
        body, name=name, grid=(N // tn, M // tm, nk), in_specs=in_specs, out_specs=[o_spec],
        out_shape=[jax.ShapeDtypeStruct((M, N), out_dtype)], args=args,
        scratch=[pltpu.VMEM((tm, tn), F32)] if nk > 1 else [], sem=("parallel", "parallel", "arbitrary"), ride=ride)[0]


def _rms_fwd(x, g, *, name, cb=0, out_dtype=BF16, ride=None):
    T = x.shape[0]
    W = g.shape[-1]
    g = g.reshape(1, W)
    tt = _tile(T, NORM_TILE, 16)

    def body(x_ref, g_ref, o_ref):
        xf = x_ref[...].astype(F32)
        rstd = lax.rsqrt(jnp.mean(xf * xf, axis=-1, keepdims=True) + NORM_EPS)
        o_ref[...] = (xf * rstd * g_ref[...]).astype(out_dtype)

    return _pcall(
        body, name=name, grid=(T // tt,),
        in_specs=[pl.BlockSpec((tt, W), lambda i: (i, cb)), pl.BlockSpec((1, W), lambda i: (0, 0))],
        out_specs=[pl.BlockSpec((tt, W), lambda i: (i, 0))], out_shape=[jax.ShapeDtypeStruct((T, W), out_dtype)],
        args=(x, g), sem=("parallel",), ride=ride)[0]


def _rms_bwd(x, g, dy, *, name, cb=0, res=None, out_dtype=F32, ride=None):
    T = x.shape[0]
    W = g.shape[-1]
    g = g.reshape(1, W)
    tt = _tile(T, NORM_TILE // 2, 16)
    has_res = res is not None

    def body(*refs):
        if has_res:
            x_ref, g_ref, dy_ref, r_ref, dx_ref, dg_ref = refs
        else:
            x_ref, g_ref, dy_ref, dx_ref, dg_ref = refs
        xf = x_ref[...].astype(F32)
        dyf = dy_ref[...].astype(F32)
        rstd = lax.rsqrt(jnp.mean(xf * xf, axis=-1, keepdims=True) + NORM_EPS)
        xhat = xf * rstd
        dxhat = dyf * g_ref[...]
        dx = rstd * (dxhat - xhat * jnp.mean(dxhat * xhat, axis=-1, keepdims=True))
        if has_res:
            dx = dx + r_ref[...].astype(F32)
        dx_ref[...] = dx.astype(out_dtype)
        part = jnp.sum(dyf * xhat, axis=0, keepdims=True)

        @pl.when(pl.program_id(0) == 0)
        def _():
            dg_ref[...] = part

        @pl.when(pl.program_id(0) > 0)
        def _():
            dg_ref[...] += part

    row = pl.BlockSpec((tt, W), lambda i: (i, 0))
    in_specs = [pl.BlockSpec((tt, W), lambda i: (i, cb)), pl.BlockSpec((1, W), lambda i: (0, 0)), row]
    args = (x, g, dy)
    if has_res:
        in_specs.append(row)
        args = args + (res,)
    return _pcall(
        body, name=name, grid=(T // tt,), in_specs=in_specs,
        out_specs=[row, pl.BlockSpec((1, W), lambda i: (0, 0))],
        out_shape=[jax.ShapeDtypeStruct((T, W), out_dtype), jax.ShapeDtypeStruct((1, W), F32)], args=args, ride=ride)


def _final_fwd_bwd(h, g, target, *, name):
    T, W = h.shape
    g = g.reshape(1, W)
    tt = _tile(T, NORM_TILE, 16)

    def body(x_ref, g_ref, t_ref, loss_ref, dx_ref, dg_ref):
        xf = x_ref[...]
        rstd = lax.rsqrt(jnp.mean(xf * xf, axis=-1, keepdims=True) + NORM_EPS)
        xhat = xf * rstd
        err = xhat * g_ref[...] - t_ref[...]
        lpart = jnp.zeros((1, LANES), F32) + (0.5 / W) * jnp.sum(err * err)
        dyf = err * (1.0 / W)
        dxhat = dyf * g_ref[...]
        dx_ref[...] = rstd * (dxhat - xhat * jnp.mean(dxhat * xhat, axis=-1, keepdims=True))
        part = jnp.sum(dyf * xhat, axis=0, keepdims=True)

        @pl.when(pl.program_id(0) == 0)
        def _():
            dg_ref[...] = part
            loss_ref[...] = lpart

        @pl.when(pl.program_id(0) > 0)
        def _():
            dg_ref[...] += part
            loss_ref[...] += lpart

    row = pl.BlockSpec((tt, W), lambda i: (i, 0))
    return pl.pallas_call(
        body, name=name, grid=(T // tt,),
        in_specs=[row, pl.BlockSpec((1, W), lambda i: (0, 0)), row],
        out_specs=[pl.BlockSpec((1, LANES), lambda i: (0, 0)), row, pl.BlockSpec((1, W), lambda i: (0, 0))],
        out_shape=[jax.ShapeDtypeStruct((1, LANES), F32), jax.ShapeDtypeStruct((T, W), F32),
                   jax.ShapeDtypeStruct((1, W), F32)],
        compiler_params=_cparams(("arbitrary",)),
    )(h, g, target)


def _swap16(x):
    lane = lax.broadcasted_iota(jnp.int32, x.shape, 1)
    return jnp.where((lane % 32) < 16, pltpu.roll(x, LANES - 16, 1), pltpu.roll(x, 16, 1))


def _rope(x, c, s):
    return x * c + _swap16(x) * s


def _rope_t(d, c, s):
    return d * c + _swap16(d * s)


def _head_block_map(fn, x, cos, sin, *, name):
    T, W = x.shape
    tt = _tile(T, NORM_TILE, 16)

    def body(x_ref, c_ref, s_ref, o_ref):
        c, s = c_ref[...], s_ref[...]
        for h in range(W // LANES):
            lanes = slice(h * LANES, (h + 1) * LANES)
            o_ref[:, lanes] = fn(x_ref[:, lanes], c, s).astype(BF16)

    tab = pl.BlockSpec((tt, LANES), lambda i: (i, 0))
    blk = pl.BlockSpec((tt, W), lambda i: (i, 0))
    return pl.pallas_call(
        body, name=name, grid=(T // tt,), in_specs=[blk, tab, tab], out_specs=blk,
        out_shape=jax.ShapeDtypeStruct((T, W), BF16), compiler_params=_cparams(("parallel",)),
    )(x, cos, sin)


def _rope_q(q, cos, sin, *, name):
    scale = _attn_scale()
    return _head_block_map(lambda x, c, s: _rope(x, c, s) * scale, q, cos, sin, name=name)


def _rope_q_bwd(dq, cos, sin, *, name):
    return _head_block_map(_rope_t, dq, cos, sin, name=name)


def _key_blocks(kv, z, cos, sin, *, kpe_block, name):
    T = kv.shape[0]
    tt = _tile(T, NORM_TILE, 16)
    W = MLA_HEADS * LANES

    def body(kv_ref, z_ref, c_ref, s_ref, o_ref):
        kr = _rope(z_ref[...], c_ref[...], s_ref[...])
        for h in range(MLA_HEADS):
            lanes = slice(h * LANES, (h + 1) * LANES)
            o_ref[:, lanes] = (kv_ref[:, lanes].astype(F32) + kr).astype(BF16)

    tab = pl.BlockSpec((tt, LANES), lambda i: (i, 0))
    blk = pl.BlockSpec((tt, W), lambda i: (i, 0))
    return pl.pallas_call(
        body, name=name, grid=(T // tt,),
        in_specs=[blk, pl.BlockSpec((tt, LANES), lambda i: (i, kpe_block)), tab, tab], out_specs=blk,
        out_shape=jax.ShapeDtypeStruct((T, W), BF16), compiler_params=_cparams(("parallel",)),
    )(kv, z, cos, sin)


def _key_rope_bwd(dk, cos, sin, *, name):
    T = dk.shape[0]
    tt = _tile(T, NORM_TILE, 16)

    def body(d_ref, c_ref, s_ref, o_ref):
        d = d_ref[:, :LANES]
        for h in range(1, MLA_HEADS):
            d = d + d_ref[:, h * LANES:(h + 1) * LANES]
        lane = lax.broadcasted_iota(jnp.int32, d.shape, 1)
        d = jnp.where(jnp.logical_and(lane >= QK_NOPE, lane < QK_NOPE + QK_ROPE), d, 0.0)
        o_ref[...] = _rope_t(d, c_ref[...], s_ref[...]).astype(BF16)

    tab = pl.BlockSpec((tt, LANES), lambda i: (i, 0))
    return pl.pallas_call(
        body, name=name, grid=(T // tt,),
        in_specs=[pl.BlockSpec((tt, MLA_HEADS * LANES), lambda i: (i, 0)), tab, tab], out_specs=tab,
        out_shape=jax.ShapeDtypeStruct((T, LANES), BF16), compiler_params=_cparams(("parallel",)),
    )(dk, cos, sin)


ATT_BLOCK = 512


def _attn_scale():
    return float((QK_NOPE + QK_ROPE) ** -0.5)


def _causal_mask(qi, kj, tq, tk):
    row = qi * tq + lax.broadcasted_iota(jnp.int32, (tq, tk), 0)
    col = kj * tk + lax.broadcasted_iota(jnp.int32, (tq, tk), 1)
    return col <= row


def _pcall(body, *, name, grid, in_specs, out_specs, out_shape, args, scratch=(), sem=None, ride=None):
    n_in, n_out, n_scr = len(args), len(out_shape), len(scratch)
    if ride is None:
        return pl.pallas_call(
            body, name=name, grid=grid, in_specs=list(in_specs), out_specs=list(out_specs), out_shape=list(out_shape),
            scratch_shapes=list(scratch), compiler_params=_cparams(sem or ("arbitrary",) * len(grid)))(*args)
    ex, sink = ride
    o0 = n_in + len(ex.arrs)
    s0 = o0 + n_out + len(ex.out_shapes)

    def hosted(*refs):
        parts = (refs[n_in:o0], refs[o0 + n_out:s0], refs[-2], refs[-1])
        ids = [pl.program_id(i) for i in range(len(grid))]
        pl.when(functools.reduce(jnp.logical_and, [i == 0 for i in ids]))(lambda: ex.start(*parts))
        body(*refs[:n_in], *refs[o0:o0 + n_out], *refs[s0:s0 + n_scr])
        pl.when(functools.reduce(jnp.logical_and, [i == n - 1 for i, n in zip(ids, grid)]))(lambda: ex.finish(*parts))

    outs = pl.pallas_call(
        hosted, name=name, grid=grid, in_specs=list(in_specs) + ex.in_specs, out_specs=list(out_specs) + ex.out_specs,
        out_shape=list(out_shape) + ex.out_shapes, scratch_shapes=list(scratch) + ex.scratch,
        compiler_params=_cparams(("arbitrary",) * len(grid)))(*args, *ex.arrs)
    sink(outs[n_out:])
    return outs[:n_out]


PAIRS = MLA_HEADS // 2


def _own_lanes(x, first):
    lane = lax.broadcasted_iota(jnp.int32, x.shape, 1)
    return jnp.where((lane < V_HEAD) if first else (lane >= V_HEAD), x, 0.0)


def _attn_fwd(q, k, kv, *, B, S, v_block0, name, ride=None):
    tq = tk = min(ATT_BLOCK, S)
    nq = S // tq
    T = B * S

    def body(q_ref, k_ref, v_ref, o_ref, lse_ref):
        qi = pl.program_id(2)
        qs = (q_ref[:, :LANES], q_ref[:, LANES:])

        def step(masked):
            def f(j, carry):
                rows = pl.ds(pl.multiple_of(j * tk, tk), tk)
                vb = v_ref[rows, :]
                out = []
                for h in range(2):
                    m, l, acc = carry[h]
                    s = _dot_nt(qs[h], k_ref[rows, h * LANES:(h + 1) * LANES])
                    if masked:
                        s = jnp.where(_causal_mask(qi, j, tq, tk), s, -jnp.inf)
                    m_new = jnp.maximum(m, jnp.max(s, axis=-1, keepdims=True))
                    alpha = jnp.exp(m - m_new)
                    p = jnp.exp(s - m_new)
                    out.append((m_new, alpha * l + jnp.sum(p, axis=-1, keepdims=True), alpha * acc + _dot_nn(p, vb)))
                return tuple(out)
            return f

        one = (jnp.full((tq, 1), -1e30, F32), jnp.zeros((tq, 1), F32), jnp.zeros((tq, LANES), F32))
        (ma, la, acca), (mb, lb, accb) = step(True)(qi, lax.fori_loop(0, qi, step(False), (one, one)))
        o_ref[...] = _own_lanes(acca / la, True) + _own_lanes(accb / lb, False)
        lse_ref[0, 0] = ma + jnp.log(la)
        lse_ref[0, 1] = mb + jnp.log(lb)

    return _pcall(
        body, name=name, grid=(B, PAIRS, nq),
        in_specs=[pl.BlockSpec((tq, 2 * LANES), lambda b, g, i: (b * nq + i, g)),
                  pl.BlockSpec((S, 2 * LANES), lambda b, g, i: (b, g)),
                  pl.BlockSpec((S, LANES), lambda b, g, i: (b, v_block0 + g))],
        out_specs=[pl.BlockSpec((tq, LANES), lambda b, g, i: (b * nq + i, g)),
                   pl.BlockSpec((1, 2, tq, 1), lambda b, g, i: (b, g, i, 0))],
        out_shape=[jax.ShapeDtypeStruct((T, PAIRS * LANES), F32), jax.ShapeDtypeStruct((B, MLA_HEADS, S, 1), F32)],
        args=(q, k, kv), ride=ride)


def _attn_delta(o, do, *, B, S, name):
    T = B * S
    tt = _tile(S, NORM_TILE, 16)
    nt = S // tt

    def body(o_ref, do_ref, delta_ref):
        prod = o_ref[...] * do_ref[...]
        delta_ref[0, 0] = jnp.sum(_own_lanes(prod, True), axis=-1, keepdims=True)
        delta_ref[0, 1] = jnp.sum(_own_lanes(prod, False), axis=-1, keepdims=True)

    rows = pl.BlockSpec((tt, LANES), lambda b, g, i: (b * nt + i, g))
    return pl.pallas_call(
        body, name=name, grid=(B, PAIRS, nt), in_specs=[rows, rows],
        out_specs=pl.BlockSpec((1, 2, tt, 1), lambda b, g, i: (b, g, i, 0)),
        out_shape=jax.ShapeDtypeStruct((B, MLA_HEADS, S, 1), F32),
        compiler_params=_cparams(("parallel", "parallel", "parallel")),
    )(o, do)


def _attn_bwd(q, k, kv, lse_rows, delta_rows, do, *, B, S, v_block0, name, ride=None):
    tq = tk = min(ATT_BLOCK, S)
    nq = S // tq
    T = B * S
    scale = _attn_scale()

    def body(q_ref, k_ref, v_ref, lse_ref, delta_ref, do_ref, dk_ref, dv_ref, dq_ref):
        kj = pl.program_id(2)
        ks = (k_ref[:, :LANES], k_ref[:, LANES:])
        vb = v_ref[...]

        @pl.when(kj == 0)
        def _():
            dq_ref[...] = jnp.zeros_like(dq_ref)

        def step(masked):
            def f(i, carry):
                rows = pl.ds(pl.multiple_of(i * tq, tq), tq)
                do_b = do_ref[rows, :]
                dks, dv = list(carry[:2]), carry[2]
                for h in range(2):
                    qb = q_ref[rows, h * LANES:(h + 1) * LANES]
                    doh = _own_lanes(do_b, h == 0)
                    pt = jnp.exp(_dot_nt(ks[h], qb) - lse_ref[0, h, pl.ds(i, 1), :])
                    if masked:
                        krow = kj * tk + lax.broadcasted_iota(jnp.int32, (tk, tq), 0)
                        qcol = i * tq + lax.broadcasted_iota(jnp.int32, (tk, tq), 1)
                        pt = jnp.where(krow <= qcol, pt, 0.0)
                    dst = pt * (_dot_nt(vb, doh) - delta_ref[0, h, pl.ds(i, 1), :])
                    dks[h] = dks[h] + _dot_nn(dst, qb)
                    dv = dv + _dot_nn(pt, doh)
                    dq_ref[rows, h * LANES:(h + 1) * LANES] += _dot_tn(dst, ks[h]) * scale
                return dks[0], dks[1], dv
            return f

        zero = jnp.zeros((tk, LANES), F32)
        dka, dkb, dv = lax.fori_loop(kj + 1, nq, step(False), step(True)(kj, (zero, zero, zero)))
        dk_ref[:, :LANES] = dka
        dk_ref[:, LANES:] = dkb
        dv_ref[...] = dv

    krow = lambda w, c0: pl.BlockSpec((tk, w), lambda b, g, j: (b * nq + j, c0 + g))
    seq = lambda w: pl.BlockSpec((S, w), lambda b, g, j: (b, g))
    stat = pl.BlockSpec((1, 2, nq, tq), lambda b, g, j: (b, g, 0, 0))
    dk, dv, dq = _pcall(
        body, name=name, grid=(B, PAIRS, nq),
        in_specs=[seq(2 * LANES), krow(2 * LANES, 0), krow(LANES, v_block0), stat, stat, seq(LANES)],
        out_specs=[krow(2 * LANES, 0), krow(LANES, 0), seq(2 * LANES)],
        out_shape=[jax.ShapeDtypeStruct((T, MLA_HEADS * LANES), F32), jax.ShapeDtypeStruct((T, PAIRS * LANES), F32),
                   jax.ShapeDtypeStruct((T, MLA_HEADS * LANES), F32)],
        args=(q, k, kv, lse_rows, delta_rows, do), ride=ride)
    return dq, dk, dv


def _lru_gates(xl, halo, cw_ref, cb_ref, wa_ref, ba_ref, wx_ref, bx_ref, lam_ref):
    xc = cb_ref[...] + cw_ref[3:4, :] * xl
    for kk in range(LRU_CONV - 1):
        xc = xc + cw_ref[kk:kk + 1, :] * _shift_rows(xl, LRU_CONV - 1 - kk, halo)
    r = _sigmoid(_dot_nn(xc, wa_ref[...]) + ba_ref[...])
    i = _sigmoid(_dot_nn(xc, wx_ref[...]) + bx_ref[...])
    lam = lam_ref[...]
    sp = jnp.maximum(-lam, 0.0) + jnp.log(1.0 + jnp.exp(-jnp.abs(lam)))
    a = jnp.exp(-LRU_C * r * sp)
    mult = jnp.sqrt(1.0 - a * a)
    return xc, r, i, sp, a, mult


def _lru_specs(tt, nt, S):
    def make(rev):
        tmap = (lambda t: nt - 1 - t) if rev else (lambda t: t)
        tile = lambda cb: pl.BlockSpec((tt, LRU_WIDTH), lambda b, t: (b * nt + tmap(t), cb))
        prev8 = lambda cb: pl.BlockSpec(
            (8, LRU_WIDTH), lambda b, t: (jnp.maximum((b * nt + tmap(t)) * (tt // 8) - 1, 0), cb))
        return tile, prev8, tmap
    return make


def _lru_fwd(z, cw, cb, wa, ba, wx, bx, lam, *, S, name, ride=None):
    T = z.shape[0]
    tt = min(ROW_TILE, S)
    nt = S // tt
    tile, prev8, _ = _lru_specs(tt, nt, S)(False)
    vec = lambda r: pl.BlockSpec((r, LRU_WIDTH), lambda b, t: (0, 0))
    mat = pl.BlockSpec((LRU_WIDTH, LRU_WIDTH), lambda b, t: (0, 0))

    def body(xl_ref, halo_ref, gate_ref, cw_ref, cb_ref, wa_ref, ba_ref, wx_ref, bx_ref, lam_ref,
             y_ref, h_ref, carry_ref):
        t = pl.program_id(1)
        first = t == 0
        halo = jnp.where(first, 0.0, halo_ref[...])
        xl_t = xl_ref[...]
        xc, r, i, sp, a, mult = _lru_gates(xl_t, halo, cw_ref, cb_ref, wa_ref, ba_ref, wx_ref, bx_ref, lam_ref)
        bv = mult * (i * xc)
        ones = jnp.ones((8, LRU_WIDTH), F32)
        zeros = jnp.zeros((8, LRU_WIDTH), F32)
        row = lax.broadcasted_iota(jnp.int32, (tt, LRU_WIDTH), 0)
        A = a
        d = 1
        while d < tt:
            if d < 8:
                a_sh = _shift_rows(A, d, ones)
                b_sh = _shift_rows(bv, d, zeros)
            else:
                a_sh = jnp.where(row < d, 1.0, pltpu.roll(A, d, 0))
                b_sh = jnp.where(row < d, 0.0, pltpu.roll(bv, d, 0))
            bv = A * b_sh + bv
            A = A * a_sh
            d *= 2
        h0 = jnp.where(first, 0.0, carry_ref[0:1, :])
        h = A * h0 + bv
        carry_ref[...] = jnp.broadcast_to(h[tt - 1:tt, :], (8, LRU_WIDTH))
        h_ref[...] = h
        y_ref[...] = (h * _gelu(gate_ref[...])).astype(BF16)

    return _pcall(
        body, name=name, grid=(T // S, nt),
        in_specs=[tile(0), prev8(0), tile(1), vec(LRU_CONV), vec(1), mat, vec(1), mat, vec(1), vec(1)],
        out_specs=[tile(0), tile(0)],
        out_shape=[jax.ShapeDtypeStruct((T, LRU_WIDTH), BF16), jax.ShapeDtypeStruct((T, LRU_WIDTH), F32)],
        args=(z, z, z, cw, cb, wa, ba, wx, bx, lam), scratch=[pltpu.VMEM((8, LRU_WIDTH), F32)], ride=ride)


def _lru_bwd(z, h, dy, cw, cb, wa, ba, wx, bx, lam, *, S, name):
    T = z.shape[0]
    tt = min(ROW_TILE, S)
    nt = S // tt
    tile, prev8, tmap = _lru_specs(tt, nt, S)(True)
    vec = lambda r: pl.BlockSpec((r, LRU_WIDTH), lambda b, t: (0, 0))
    mat = pl.BlockSpec((LRU_WIDTH, LRU_WIDTH), lambda b, t: (0, 0))

    def body(xl_ref, halo_ref, gate_ref, h_ref, hprev_ref, dy_ref, cw_ref, cb_ref, wa_ref, ba_ref, wx_ref,
             bx_ref, lam_ref, dxl_ref, dgate_ref, dcw_ref, dcb_ref, dwa_ref, dba_ref, dwx_ref, dbx_ref,
             dlam_ref, lamc_ref, ac_ref, dxc_ref):
        b = pl.program_id(0)
        t = pl.program_id(1)
        tr = nt - 1 - t
        seq_first = tr == 0
        seq_last = t == 0
        halo = jnp.where(seq_first, 0.0, halo_ref[...])
        xl_t = xl_ref[...]
        xc, r, i, sp, a, mult = _lru_gates(xl_t, halo, cw_ref, cb_ref, wa_ref, ba_ref, wx_ref, bx_ref, lam_ref)
        hh = h_ref[...]
        dyf = dy_ref[...].astype(F32)
        gl, dgl = _gelu_and_grad(gate_ref[...])
        dgate_ref[...] = (dyf * hh * dgl).astype(BF16)
        dh = dyf * gl

        a_first_later = jnp.where(seq_last, 0.0, ac_ref[...])
        lam_later = jnp.where(seq_last, 0.0, lamc_ref[...])
        row = lax.broadcasted_iota(jnp.int32, (tt, LRU_WIDTH), 0)
        A = _shift_rows_up(a, 1, a_first_later)
        lm = dh
        ones = jnp.ones((8, LRU_WIDTH), F32)
        zeros = jnp.zeros((8, LRU_WIDTH), F32)
        d = 1
        while d < tt:
            if d < 8:
                a_sh = _shift_rows_up(A, d, ones)
                l_sh = _shift_rows_up(lm, d, zeros)
            else:
                a_sh = jnp.where(row >= tt - d, 1.0, pltpu.roll(A, tt - d, 0))
                l_sh = jnp.where(row >= tt - d, 0.0, pltpu.roll(lm, tt - d, 0))
            lm = lm + A * l_sh
            A = A * a_sh
            d *= 2
        lm = lm + A * lam_later[0:1, :]
        lamc_ref[...] = jnp.broadcast_to(lm[0:1, :], (8, LRU_WIDTH))
        ac_ref[...] = jnp.broadcast_to(a[0:1, :], (8, LRU_WIDTH))

        hprev_halo = jnp.where(seq_first, 0.0, hprev_ref[...])
        h_prev = _shift_rows(hh, 1, hprev_halo)
        da = lm * h_prev
        ixc = i * xc
        dmult = lm * ixc
        di = lm * mult * xc
        dxc = lm * mult * i
        da = da - dmult * a / mult
        dlog = da * a
        dr = dlog * (-LRU_C) * sp
        dsp_part = jnp.sum(dlog * (-LRU_C) * r, axis=0, keepdims=True)
        dpa = dr * r * (1.0 - r)
        dpx = di * i * (1.0 - i)
        dxc = dxc + _dot_nt(dpa, wa_ref[...]) + _dot_nt(dpx, wx_ref[...])
        dwa_part = _dot_tn(xc, dpa)
        dwx_part = _dot_tn(xc, dpx)

        later = jnp.where(seq_last, 0.0, dxc_ref[...])
        dxl = cw_ref[3:4, :] * dxc
        for kk in range(LRU_CONV - 1):
            dxl = dxl + cw_ref[kk:kk + 1, :] * _shift_rows_up(dxc, LRU_CONV - 1 - kk, later)
        dxl_ref[...] = dxl.astype(BF16)
        dxc_ref[...] = dxc[0:8, :]
        dcw_rows = [jnp.sum(dxc * _shift_rows(xl_t, LRU_CONV - 1 - kk, halo), axis=0, keepdims=True)
                    for kk in range(LRU_CONV - 1)]
        dcw_rows.append(jnp.sum(dxc * xl_t, axis=0, keepdims=True))
        dcw_part = jnp.concatenate(dcw_rows + [jnp.zeros((8 - LRU_CONV, LRU_WIDTH), F32)], axis=0)
        lamv = lam_ref[...]
        dlam_part = dsp_part * (-_sigmoid(-lamv))
        parts = ((dcw_ref, dcw_part), (dcb_ref, jnp.sum(dxc, axis=0, keepdims=True)),
                 (dwa_ref, dwa_part), (dba_ref, jnp.sum(dpa, axis=0, keepdims=True)),
                 (dwx_ref, dwx_part), (dbx_ref, jnp.sum(dpx, axis=0, keepdims=True)),
                 (dlam_ref, dlam_part))
        start = jnp.logical_and(b == 0, t == 0)

        @pl.when(start)
        def _():
            for ref, val in parts:
                ref[...] = val

        @pl.when(jnp.logical_not(start))
        def _():
            for ref, val in parts:
                ref[...] += val

    acc = lambda r: pl.BlockSpec((r, LRU_WIDTH), lambda b, t: (0, 0))
    return pl.pallas_call(
        body, name=name, grid=(T // S, nt),
        in_specs=[tile(0), prev8(0), tile(1), tile(0), prev8(0), tile(0),
                  vec(LRU_CONV), vec(1), mat, vec(1), mat, vec(1), vec(1)],
        out_specs=[tile(0), tile(0), acc(8), acc(1), mat, acc(1), mat, acc(1), acc(1)],
        out_shape=[jax.ShapeDtypeStruct((T, LRU_WIDTH), BF16), jax.ShapeDtypeStruct((T, LRU_WIDTH), BF16),
                   jax.ShapeDtypeStruct((8, LRU_WIDTH), F32), jax.ShapeDtypeStruct((1, LRU_WIDTH), F32),
                   jax.ShapeDtypeStruct((LRU_WIDTH, LRU_WIDTH), F32), jax.ShapeDtypeStruct((1, LRU_WIDTH), F32),
                   jax.ShapeDtypeStruct((LRU_WIDTH, LRU_WIDTH), F32), jax.ShapeDtypeStruct((1, LRU_WIDTH), F32),
                   jax.ShapeDtypeStruct((1, LRU_WIDTH), F32)],
        scratch_shapes=[pltpu.VMEM((8, LRU_WIDTH), F32), pltpu.VMEM((8, LRU_WIDTH), F32),
                        pltpu.VMEM((8, LRU_WIDTH), F32)],
        compiler_params=_cparams(("arbitrary", "arbitrary")),
    )(z, z, z, h, h, dy, cw, cb, wa, ba, wx, bx, lam)


FFN_CT = 1408
FFN_TILE = 512


def _ffn_conv(g, halo, cw, cb):
    gc = cb + cw[2:3, :] * g
    for kk in range(FFN_CONV - 1):
        gc = gc + cw[kk:kk + 1, :] * _shift_rows(g, FFN_CONV - 1 - kk, halo)
    return gc


def _row_chunks(rows, chunk):
    return [slice(r0, min(r0 + chunk, rows)) for r0 in range(0, rows, chunk)]


FFN_CHUNK = 128
HALO = 16


def _ffn_act_down(g, u, cw, cb, w_down, res, *, S, name, ride=None):
    T, F = g.shape
    D = w_down.shape[1]
    tt = min(FFN_TILE, S)
    nt = S // tt
    tc = _tile(F, FFN_CT)
    nj = F // tc

    def body(g_ref, halo_ref, u_ref, cw_ref, cb_ref, w_ref, r_ref, o_ref, act_ref):
        j = pl.program_id(1)
        first = (pl.program_id(0) % nt) == 0
        cw, cb = cw_ref[...], cb_ref[...]

        @pl.when(j == 0)
        def _():
            o_ref[...] = r_ref[...]

        for r in _row_chunks(tt, FFN_CHUNK):
            before = halo_ref[...] if r.start == 0 else g_ref[r.start - HALO:r.start, :]
            halo = before.astype(F32)[HALO - 8:]
            if r.start == 0:
                halo = jnp.where(first, 0.0, halo)
            gc = _ffn_conv(g_ref[r, :].astype(F32), halo, cw, cb)
            act = (_gelu(gc) * u_ref[r, :].astype(F32)).astype(BF16)
            act_ref[r, :] = act
            o_ref[r, :] += _dot_nn(act, w_ref[...])

    tile = pl.BlockSpec((tt, tc), lambda i, j: (i, j))
    prev = pl.BlockSpec((HALO, tc), lambda i, j: (jnp.maximum(i * (tt // HALO) - 1, 0), j))
    rows = pl.BlockSpec((tt, D), lambda i, j: (i, 0))
    return _pcall(
        body, name=name, grid=(T // tt, nj),
        in_specs=[tile, prev, tile, pl.BlockSpec((FFN_CONV, tc), lambda i, j: (0, j)),
                  pl.BlockSpec((1, tc), lambda i, j: (0, j)), pl.BlockSpec((tc, D), lambda i, j: (j, 0)), rows],
        out_specs=[rows, tile], out_shape=[jax.ShapeDtypeStruct((T, D), F32), jax.ShapeDtypeStruct((T, F), BF16)],
        args=(g, g, u, cw, cb, w_down, res), sem=("parallel", "arbitrary"), ride=ride)


def _ffn_act_bwd(g, u, dh, w_down, cw, cb, *, S, name, ride=None):
    T, F = g.shape
    D = w_down.shape[1]
    tt = min(FFN_TILE, S)
    nt = S // tt
    ntt = T // tt
    tc = _tile(F, FFN_CT)

    def body(g_ref, halo_ref, u_ref, dh_ref, w_ref, cw_ref, cb_ref, dg_ref, du_ref, dcw_ref, dcb_ref, later_ref):
        step = pl.program_id(1)
        ti = (ntt - 1 - step) % nt
        cw, cb = cw_ref[...], cb_ref[...]

        @pl.when(step == 0)
        def _():
            dcw_ref[...] = jnp.zeros_like(dcw_ref)
            dcb_ref[...] = jnp.zeros_like(dcb_ref)

        halo = jnp.where(ti == 0, 0.0, halo_ref[...].astype(F32)[HALO - 8:])
        gt = g_ref[...].astype(F32)
        gl, dgl = _gelu_and_grad(_ffn_conv(gt, halo, cw, cb))
        da = _dot_nt(dh_ref[...], w_ref[...])
        du_ref[...] = (da * gl).astype(BF16)
        dgc = da * u_ref[...].astype(F32) * dgl
        later = jnp.where(ti == nt - 1, 0.0, later_ref[...])
        dg = cw[2:3, :] * dgc
        for kk in range(FFN_CONV - 1):
            dg = dg + cw[kk:kk + 1, :] * _shift_rows_up(dgc, FFN_CONV - 1 - kk, later)
        dg_ref[...] = dg.astype(BF16)
        later_ref[...] = dgc[0:8, :]
        rows = [jnp.sum(dgc * _shift_rows(gt, FFN_CONV - 1 - kk, halo), axis=0, keepdims=True)
                for kk in range(FFN_CONV - 1)]
        rows.append(jnp.sum(dgc * gt, axis=0, keepdims=True))
        dcw_ref[...] += jnp.concatenate(rows + [jnp.zeros((8 - FFN_CONV, tc), F32)], axis=0)
        dcb_ref[...] += jnp.sum(dgc, axis=0, keepdims=True)

    tile = pl.BlockSpec((tt, tc), lambda j, s: (ntt - 1 - s, j))
    prev = pl.BlockSpec((HALO, tc), lambda j, s: (jnp.maximum((ntt - 1 - s) * (tt // HALO) - 1, 0), j))
    return _pcall(
        body, name=name, grid=(F // tc, ntt),
        in_specs=[tile, prev, tile, pl.BlockSpec((tt, D), lambda j, s: (ntt - 1 - s, 0)),
                  pl.BlockSpec((tc, D), lambda j, s: (j, 0)), pl.BlockSpec((FFN_CONV, tc), lambda j, s: (0, j)),
                  pl.BlockSpec((1, tc), lambda j, s: (0, j))],
        out_specs=[tile, tile, pl.BlockSpec((8, tc), lambda j, s: (0, j)), pl.BlockSpec((1, tc), lambda j, s: (0, j))],
        out_shape=[jax.ShapeDtypeStruct((T, F), BF16), jax.ShapeDtypeStruct((T, F), BF16),
                   jax.ShapeDtypeStruct((8, F), F32), jax.ShapeDtypeStruct((1, F), F32)],
        args=(g, g, u, dh, w_down, cw, cb), scratch=[pltpu.VMEM((8, tc), F32)], ride=ride)


def _sgu_norm(zv, g_ref, b_ref):
    v = _gelu(zv)
    mu = jnp.mean(v, axis=-1, keepdims=True)
    xc = v - mu
    rstd = lax.rsqrt(jnp.mean(xc * xc, axis=-1, keepdims=True) + NORM_EPS)
    xhat = xc * rstd
    return xhat, rstd, xhat * g_ref[...] + b_ref[...]


def _sgu_fwd(zc, ln_g, ln_b, wm, bmap, *, name):
    T = zc.shape[0]
    W = SGU_WIDTH
    tt = ROW_TILE
    nch = tt // CHUNK

    def body(z_ref, g_ref, b_ref, wm_ref, bm_ref, p_ref):
        u = _gelu(z_ref[:, :W])
        _, _, vn = _sgu_norm(z_ref[:, W:], g_ref, b_ref)
        vn = vn.astype(BF16)
        for n in range(nch):
            rows = slice(n * CHUNK, (n + 1) * CHUNK)
            for gi in range(SGU_GROUPS):
                cols = slice(gi * LANES, (gi + 1) * LANES)
                s = _dot_nn(wm_ref[gi], vn[rows, cols]) + bm_ref[:, cols]
                p_ref[rows, cols] = (u[rows, cols] * s).astype(BF16)

    const2 = lambda r, c: pl.BlockSpec((r, c), lambda i: (0, 0))
    return pl.pallas_call(
        body, name=name, grid=(T // tt,),
        in_specs=[pl.BlockSpec((tt, 2 * W), lambda i: (i, 0)), const2(1, W), const2(1, W),
                  pl.BlockSpec((SGU_GROUPS, CHUNK, CHUNK), lambda i: (0, 0, 0)), const2(CHUNK, W)],
        out_specs=pl.BlockSpec((tt, W), lambda i: (i, 0)),
        out_shape=jax.ShapeDtypeStruct((T, W), BF16),
        compiler_params=_cparams(("parallel",)),
    )(zc, ln_g, ln_b, wm, bmap)


def _sgu_bwd(zc, dp, ln_g, ln_b, wm, bmap, *, name, ride=None):
    T = zc.shape[0]
    W = SGU_WIDTH
    tt = ROW_TILE
    nch = tt // CHUNK
    nsteps = T // tt

    def body(z_ref, dp_ref, g_ref, b_ref, wm_ref, bm_ref, dz_ref, dg_ref, db_ref, dwm_ref, dbm_ref,
             s_scr, dvn_scr):
        step = pl.program_id(0)
        zu = z_ref[:, :W]
        zv = z_ref[:, W:]
        u, dgu = _gelu_and_grad(zu)
        xhat, rstd, vn = _sgu_norm(zv, g_ref, b_ref)
        vnb = vn.astype(BF16)
        dpf = dp_ref[...].astype(F32)
        ds = dpf * u

        @pl.when(step == 0)
        def _():
            dwm_ref[...] = jnp.zeros_like(dwm_ref)
            dbm_ref[...] = jnp.zeros_like(dbm_ref)

        for n in range(nch):
            rows = slice(n * CHUNK, (n + 1) * CHUNK)
            for gi in range(SGU_GROUPS):
                cols = slice(gi * LANES, (gi + 1) * LANES)
                s_scr[rows, cols] = _dot_nn(wm_ref[gi], vnb[rows, cols]) + bm_ref[:, cols]
                dsb = ds[rows, cols]
                dvn_scr[rows, cols] = _dot_tn(wm_ref[gi], dsb)
                dwm_ref[gi] += _dot_nt(dsb, vnb[rows, cols])
                dbm_ref[:, cols] += dsb
        dz_ref[:, :W] = (dpf * s_scr[...] * dgu).astype(BF16)
        dvn = dvn_scr[...]
        dxhat = dvn * g_ref[...]
        dv = rstd * (dxhat - jnp.mean(dxhat, axis=-1, keepdims=True)
                     - xhat * jnp.mean(dxhat * xhat, axis=-1, keepdims=True))
        _, dgv = _gelu_and_grad(zv)
        dz_ref[:, W:] = (dv * dgv).astype(BF16)
        dg_part = jnp.sum(dvn * xhat, axis=0, keepdims=True)
        db_part = jnp.sum(dvn, axis=0, keepdims=True)

        @pl.when(step == 0)
        def _():
            dg_ref[...] = dg_part
            db_ref[...] = db_part

        @pl.when(step > 0)
        def _():
            dg_ref[...] += dg_part
            db_ref[...] += db_part

        @pl.when(step == nsteps - 1)
        def _():
            for gi in range(SGU_GROUPS):
                cols = slice(gi * LANES, (gi + 1) * LANES)
                tot = jnp.sum(dbm_ref[:, cols], axis=1, keepdims=True)
                dbm_ref[:, cols] = jnp.broadcast_to(tot, (CHUNK, LANES))

    const2 = lambda r, c: pl.BlockSpec((r, c), lambda i: (0, 0))
    wspec = pl.BlockSpec((SGU_GROUPS, CHUNK, CHUNK), lambda i: (0, 0, 0))
    return _pcall(
        body, name=name, grid=(nsteps,),
        in_specs=[pl.BlockSpec((tt, 2 * W), lambda i: (i, 0)), pl.BlockSpec((tt, W), lambda i: (i, 0)),
                  const2(1, W), const2(1, W), wspec, const2(CHUNK, W)],
        out_specs=[pl.BlockSpec((tt, 2 * W), lambda i: (i, 0)), const2(1, W), const2(1, W), wspec, const2(CHUNK, W)],
        out_shape=[jax.ShapeDtypeStruct((T, 2 * W), BF16), jax.ShapeDtypeStruct((1, W), F32),
                   jax.ShapeDtypeStruct((1, W), F32), jax.ShapeDtypeStruct((SGU_GROUPS, CHUNK, CHUNK), F32),
                   jax.ShapeDtypeStruct((CHUNK, W), F32)],
        args=(zc, dp, ln_g, ln_b, wm, bmap), scratch=[pltpu.VMEM((tt, W), F32), pltpu.VMEM((tt, W), F32)], ride=ride)


def _rope_tables(positions):
    half = QK_ROPE // 2
    inv_freq = jnp.exp(-math.log(ROPE_BASE) * jnp.arange(half, dtype=F32) / half)
    ang = positions.reshape(-1).astype(F32)[:, None] * inv_freq
    cos = jnp.cos(ang)
    sin = jnp.sin(ang)
    n = ang.shape[0]
    tail = LANES - QK_NOPE - QK_ROPE
    cos_t = jnp.concatenate([jnp.ones((n, QK_NOPE), F32), cos, cos, jnp.ones((n, tail), F32)], axis=1)
    sin_t = jnp.concatenate([jnp.zeros((n, QK_NOPE), F32), -sin, sin, jnp.zeros((n, tail), F32)], axis=1)
    return cos_t, sin_t


SGU_GROUP_DIM = SGU_WIDTH // SGU_GROUPS
_O1, _O2, _O3, _O4 = Q_LORA, Q_LORA + KV_LORA, Q_LORA + KV_LORA + QK_ROPE, Q_LORA + KV_LORA + QK_ROPE + LRU_WIDTH
_A0, _A1, _A2 = 2 * LRU_WIDTH, 2 * LRU_WIDTH + Q_LORA, 2 * LRU_WIDTH + Q_LORA + KV_LORA
_A3 = _A2 + QK_NOPE
Z_Q_BLOCK, Z_KV_BLOCK, Z_KPE_BLOCK = _A0 // Q_LORA, _A1 // KV_LORA, _A2 // LANES


def _perm_w_in(w_in):
    zeros = lambda n: jnp.zeros((w_in.shape[0], n), w_in.dtype)
    return jnp.concatenate([w_in[:, _O3:_O4], w_in[:, _O4:], w_in[:, :_O1], w_in[:, _O1:_O2], zeros(QK_NOPE),
                            w_in[:, _O2:_O3], zeros(LANES - QK_NOPE - QK_ROPE)], axis=1)


def _unperm_w_in(w):
    return jnp.concatenate([w[:, _A0:_A1], w[:, _A1:_A2], w[:, _A3:_A3 + QK_ROPE], w[:, :LRU_WIDTH],
                            w[:, LRU_WIDTH:_A0]], axis=1)


def _head_blocks(w, d):
    r = w.shape[0]
    return jnp.pad(w.reshape(r, MLA_HEADS, d), ((0, 0), (0, 0), (0, LANES - d))).reshape(r, MLA_HEADS * LANES)


def _from_head_blocks(w, d):
    r = w.shape[0]
    return w.reshape(r, MLA_HEADS, LANES)[:, :, :d].reshape(r, MLA_HEADS * d)


def _split_kv(w_kv):
    r = w_kv.shape[0]
    w3 = w_kv.reshape(r, MLA_HEADS, QK_NOPE + V_HEAD)
    return _head_blocks(w3[:, :, :QK_NOPE].reshape(r, -1), QK_NOPE), w3[:, :, QK_NOPE:].reshape(r, -1)


def _join_kv(w_k, w_v):
    r = w_k.shape[0]
    return jnp.concatenate([_from_head_blocks(w_k, QK_NOPE).reshape(r, MLA_HEADS, QK_NOPE),
                            w_v.reshape(r, MLA_HEADS, V_HEAD)], axis=2).reshape(r, -1)


def _prep_small(w):
    p = {n: w[n] for n in w if n not in BIG}
    eye = jnp.eye(LRU_HEADS, dtype=F32)
    dense = lambda wg: (wg[:, :, None, :] * eye[:, None, :, None]).reshape(LRU_WIDTH, LRU_WIDTH).astype(BF16)
    p["wa_d"] = dense(w["ab_w_rg_a"][0])
    p["wx_d"] = dense(w["ab_w_rg_x"][0])
    causal = jnp.tril(jnp.ones((CHUNK, CHUNK), F32))
    p["wm"] = (w["c_w_s"][0] * causal).astype(BF16)
    p["bmap"] = jnp.repeat(w["c_b_s"][0].T, SGU_GROUP_DIM, axis=1)
    return p


def _prep_big(ab_w_in, ab_w_q_b, ab_w_kv_b):
    return {"w_in_p": _perm_w_in(ab_w_in).astype(BF16),
            "w_q_p": _head_blocks(ab_w_q_b, QK_NOPE + QK_ROPE).astype(BF16),
            "w_kv_p": jnp.concatenate(_split_kv(ab_w_kv_b), axis=1).astype(BF16)}


def _ffn_fwd(h, l, p, S, rides):
    hn = _rms_fwd(h, p["ffn_norm"][l], name=f"ffn{l}_norm")
    g = _mm(hn, p["ffn_gate_t"][l], tb=True, out_dtype=BF16, name=f"ffn{l}_gate", ride=rides.get(f"ffn{l}_gate"))
    u = _mm(hn, p["ffn_up_t"][l], tb=True, out_dtype=BF16, name=f"ffn{l}_up", ride=rides.get(f"ffn{l}_up"))
    out, act = _ffn_act_down(g, u, p["ffn_conv_w"][l], p["ffn_conv_b"][l][None], p["ffn_down"][l], h, S=S,
                             name=f"ffn{l}_down", ride=rides.get(f"ffn{l}_down"))
    return out, (hn, g, u, act)


def _ffn_bwd(dh, h_in, l, p, saved, S, rides, grads_ready, also_ready=None):
    hn, g, u, act = saved
    dw_down = _mm(act, dh, ta=True, out_dtype=BF16, name=f"ffn{l}_dwdown")
    dg, du, dcw, dcb = _ffn_act_bwd(g, u, dh, p["ffn_down"][l], p["ffn_conv_w"][l], p["ffn_conv_b"][l][None], S=S,
                                    name=f"ffn{l}_dactbwd", ride=rides.get(f"ffn{l}_dactbwd"))
    dhn = _mm(dg, p["ffn_gate_t"][l], name=f"ffn{l}_dhn_g")
    dhn = _mm(du, p["ffn_up_t"][l], res=dhn, out_dtype=BF16, name=f"ffn{l}_dhn_u")
    dw_gate_t = _mm(dg, hn, ta=True, out_dtype=BF16, name=f"ffn{l}_dwgate")
    dw_up_t = _mm(du, hn, ta=True, out_dtype=BF16, name=f"ffn{l}_dwup")
    grads_ready(l, {**(also_ready or {}), "ffn_gate_t": dw_gate_t, "ffn_up_t": dw_up_t, "ffn_down": dw_down})
    dh_in, dnorm = _rms_bwd(h_in, p["ffn_norm"][l], dhn, res=dh, name=f"ffn{l}_dnorm", ride=rides.get(f"ffn{l}_dnorm"))
    grads = dict(ffn_norm=dnorm[0], ffn_gate_t=dw_gate_t, ffn_up_t=dw_up_t, ffn_conv_w=dcw[:FFN_CONV],
                 ffn_conv_b=dcb[0], ffn_down=dw_down)
    return dh_in, grads


def _local_step(x, positions, target, p, rides=None, grads_ready=None):
    rides = {} if rides is None else rides
    grads_ready = grads_ready or (lambda layer, ready: None)
    B, S, D = x.shape
    T = B * S
    H = MLA_HEADS
    xf = x.reshape(T, D)
    tgt = target.reshape(T, D)
    cos, sin = _rope_tables(positions)

    hn0 = _rms_fwd(xf, p["ab_norm"][0], name="ab_norm", ride=rides.get("ab_norm"))
    z = _mm(hn0, p["w_in_p"], name="ab_in")
    cqn = _rms_fwd(z, p["ab_q_norm"][0], cb=Z_Q_BLOCK, name="q_norm")
    ckvn = _rms_fwd(z, p["ab_kv_norm"][0], cb=Z_KV_BLOCK, name="kv_norm")
    q = _mm(cqn, p["w_q_p"], name="q_up")
    kv = _mm(ckvn, p["w_kv_p"], out_dtype=BF16, name="kv_up")
    qs = _rope_q(q, cos, sin, name="q_rope")
    kk = _key_blocks(kv, z, cos, sin, kpe_block=Z_KPE_BLOCK, name="k_rope")
    att = dict(B=B, S=S, v_block0=H)
    o, lse = _attn_fwd(qs, kk, kv, name="attn_fwd", ride=rides.get("attn_fwd"), **att)
    lru_par = (p["ab_conv_w"][0], p["ab_conv_b"], p["wa_d"], p["ab_b_rg_a"], p["wx_d"], p["ab_b_rg_x"], p["ab_lambda"])
    y_lru, hs = _lru_fwd(z, *lru_par, S=S, name="lru_fwd", ride=rides.get("lru_fwd"))
    n_att = H * V_HEAD
    w_out_a, w_out_b = p["ab_w_out"][:n_att], p["ab_w_out"][n_att:]
    h1 = _mm(y_lru, w_out_b, res=_mm(o, w_out_a, res=xf, name="ab_out_a"), name="ab_out_b")
    h2, ffn0 = _ffn_fwd(h1, 0, p, S, rides)

    hn2 = _rms_fwd(h2, p["c_norm"][0], name="c_norm")
    zc = _mm(hn2, p["c_w_in_t"], tb=True, name="c_in")
    pg = _sgu_fwd(zc, p["c_ln_g"], p["c_ln_b"], p["wm"], p["bmap"], name="sgu_fwd")
    h3 = _mm(pg, p["c_w_out"], res=h2, name="c_out")
    h4, ffn1 = _ffn_fwd(h3, 1, p, S, rides)

    loss_row, dh4, dfinal = _final_fwd_bwd(h4, p["final_norm"], tgt, name="final")

    dh3, g_ffn1 = _ffn_bwd(dh4, h3, 1, p, ffn1, S, rides, grads_ready)
    dpg = _mm(dh3, p["c_w_out"], tb=True, out_dtype=BF16, name="c_dp")
    dw_c_out = _mm(pg, dh3, ta=True, out_dtype=BF16, name="c_dwout")
    dzc, dlng, dlnb, dwm, dbm = _sgu_bwd(zc, dpg, p["c_ln_g"], p["c_ln_b"], p["wm"], p["bmap"], name="sgu_bwd",
                                         ride=rides.get("sgu_bwd"))
    dhn2 = _mm(dzc, p["c_w_in_t"], out_dtype=BF16, name="c_dhn")
    dw_c_in_t = _mm(dzc, hn2, ta=True, out_dtype=BF16, name="c_dwin")
    dh2, dcnorm = _rms_bwd(h2, p["c_norm"][0], dhn2, res=dh3, name="c_dnorm")
    dh1, g_ffn0 = _ffn_bwd(dh2, h1, 0, p, ffn0, S, rides, grads_ready, {"c_w_in_t": dw_c_in_t, "c_w_out": dw_c_out})

    do = _mm(dh1, w_out_a, tb=True, name="ab_do")
    dy_lru = _mm(dh1, w_out_b, tb=True, out_dtype=BF16, name="ab_dylru")
    dw_out = jnp.concatenate([_mm(o, dh1, ta=True, out_dtype=BF16, name="ab_dwout_a"),
                              _mm(y_lru, dh1, ta=True, out_dtype=BF16, name="ab_dwout_b")], axis=0)
    delta = _attn_delta(o, do, B=B, S=S, name="attn_delta")
    nq = S // min(ATT_BLOCK, S)
    rows = lambda a: a.reshape(B, H, nq, S // nq)
    dq, dk, dv = _attn_bwd(qs, kk, kv, rows(lse), rows(delta), do, name="attn_bwd", ride=rides.get("attn_bwd"), **att)
    dq_full = _rope_q_bwd(dq, cos, sin, name="q_rope_bwd")
    dkr = _key_rope_bwd(dk, cos, sin, name="k_rope_bwd")
    n_key = H * LANES
    w_k_p, w_v_p = p["w_kv_p"][:, :n_key], p["w_kv_p"][:, n_key:]
    dcqn = _mm(dq_full, p["w_q_p"], tb=True, name="q_dlat")
    dw_q_p = _mm(cqn, dq_full, ta=True, out_dtype=BF16, name="q_dw")
    dckvn = _mm(dv, w_v_p, tb=True, res=_mm(dk, w_k_p, tb=True, name="k_dlat"), name="v_dlat")
    dw_k_p = _mm(ckvn, dk, ta=True, out_dtype=BF16, name="k_dw")
    dw_v_p = _mm(ckvn, dv, ta=True, out_dtype=BF16, name="v_dw")
    dcq, dqnorm = _rms_bwd(z, p["ab_q_norm"][0], dcqn, cb=Z_Q_BLOCK, out_dtype=BF16, name="q_dnorm")
    dckv, dkvnorm = _rms_bwd(z, p["ab_kv_norm"][0], dckvn, cb=Z_KV_BLOCK, out_dtype=BF16, name="kv_dnorm")
    dxl, dgate, dcw, dcb, dwa, dba, dwx, dbx, dlam = _lru_bwd(z, hs, dy_lru, *lru_par, S=S, name="lru_bwd")
    dz = jnp.concatenate([dxl, dgate, dcq, dckv, dkr], axis=1)
    dhn0 = _mm(dz, p["w_in_p"], tb=True, out_dtype=BF16, name="ab_dhn")
    dw_in_p = _mm(hn0, dz, ta=True, out_dtype=BF16, name="ab_dwin")
    dx, dabnorm = _rms_bwd(xf, p["ab_norm"][0], dhn0, res=dh1, name="ab_dnorm")

    blocks = lambda dd: jnp.stack([dd[i * LRU_BLOCK:(i + 1) * LRU_BLOCK, i * LRU_BLOCK:(i + 1) * LRU_BLOCK]
                                   for i in range(LRU_HEADS)])
    causal = jnp.tril(jnp.ones((CHUNK, CHUNK), F32))
    grads = {
        "ab_norm": dabnorm, "w_in_p": dw_in_p, "ab_q_norm": dqnorm, "w_q_p": dw_q_p,
        "ab_kv_norm": dkvnorm, "w_k_p": dw_k_p, "w_v_p": dw_v_p, "ab_conv_w": dcw[:LRU_CONV][None], "ab_conv_b": dcb,
        "ab_w_rg_a": blocks(dwa)[None], "ab_b_rg_a": dba, "ab_w_rg_x": blocks(dwx)[None], "ab_b_rg_x": dbx,
        "ab_lambda": dlam, "ab_w_out": dw_out,
        "c_norm": dcnorm, "c_w_in_t": dw_c_in_t, "c_ln_g": dlng, "c_ln_b": dlnb,
        "c_w_s": (dwm * causal)[None], "c_b_s": dbm[:, ::SGU_GROUP_DIM].T[None], "c_w_out": dw_c_out,
        "final_norm": dfinal[0],
    }
    for name in ("ffn_norm", "ffn_conv_w", "ffn_conv_b"):
        grads[name] = jnp.stack([g_ffn0[name], g_ffn1[name]])
    for name in ("ffn_gate_t", "ffn_up_t", "ffn_down"):
        grads[name] = [g_ffn0[name], g_ffn1[name]]
    return loss_row, dx.reshape(B, S, D), grads


ANY = pl.BlockSpec(memory_space=pl.ANY)


def _place():
    x, y, c = lax.axis_index("x"), lax.axis_index("y"), lax.axis_index("c")
    chips = [(1 - x, y), (x, 1 - y), (1 - x, 1 - y)]
    return x, y, c, 2 * x + y, (x, y, 1 - c), chips


def _remote(src, dst, send_sems, recv_sems, k, to):
    return pltpu.make_async_remote_copy(src_ref=src, dst_ref=dst, send_sem=send_sems.at[k], recv_sem=recv_sems.at[k],
                                        device_id=to, device_id_type=MESH)


class _Exchange:
    def __init__(self, arrs, out_shapes, n_sems, start, finish):
        self.arrs, self.out_shapes, self.n_sems, self.start, self.finish = list(arrs), out_shapes, n_sems, start, finish

    @property
    def in_specs(self):
        return [ANY] * len(self.arrs)

    @property
    def out_specs(self):
        return [ANY] * len(self.out_shapes)

    @property
    def scratch(self):
        return [pltpu.SemaphoreType.DMA((self.n_sems,)), pltpu.SemaphoreType.DMA((self.n_sems,))]

    def split(self, refs):
        n = len(self.arrs)
        return refs[:n], refs[n:n + len(self.out_shapes)], refs[-2], refs[-1]

    def run(self, name):
        def body(*refs):
            parts = self.split(refs)
            self.start(*parts)
            self.finish(*parts)

        return pl.pallas_call(body, name=name, in_specs=self.in_specs, out_specs=self.out_specs,
                              out_shape=self.out_shapes, scratch_shapes=self.scratch)(*self.arrs)


def _put(buf, piece, idx, axis):
    return lax.dynamic_update_slice_in_dim(buf, jnp.expand_dims(piece, axis).astype(buf.dtype), idx, axis)


def _all_gather(arrs):
    n = len(arrs)

    def start(ins, outs, send_sems, recv_sems):
        x, y, c, j, sib, chips = _place()
        for i in range(n):
            for k, (cx, cy) in enumerate(chips):
                _remote(ins[i].at[:, c], outs[i].at[:, j, c], send_sems, recv_sems, 6 * i + k, (cx, cy, c)).start()

    def finish(ins, outs, send_sems, recv_sems):
        x, y, c, j, sib, chips = _place()
        passed = []
        for i in range(n):
            for k, (cx, cy) in enumerate(chips):
                got = outs[i].at[:, 2 * cx + cy, c]
                _remote(got, got, send_sems, recv_sems, 6 * i + k, (cx, cy, c)).wait_recv()
                cp = _remote(got, got, send_sems, recv_sems, 6 * i + 3 + k, sib)
                cp.start()
                passed.append(cp)
        for i in range(n):
            for k, (cx, cy) in enumerate(chips):
                got = outs[i].at[:, 2 * cx + cy, 1 - c]
                _remote(got, got, send_sems, recv_sems, 6 * i + 3 + k, sib).wait_recv()
                _remote(ins[i].at[:, c], ins[i].at[:, c], send_sems, recv_sems, 6 * i + k, sib).wait_send()
        for cp in passed:
            cp.wait_send()

    shapes = [jax.ShapeDtypeStruct((a.shape[0], N_CHIPS) + a.shape[1:], a.dtype) for a in arrs]
    return _Exchange(arrs, shapes, 6 * n, start, finish)


class _Offset:
    def __init__(self, sems, k0):
        self.sems, self.k0 = sems, k0

    @property
    def at(self):
        return self

    def __getitem__(self, k):
        return self.sems.at[self.k0 + k]


def _merge(a, b):
    n_in, n_out = len(a.arrs), len(a.out_shapes)

    def both(fa, fb):
        def f(ins, outs, send_sems, recv_sems):
            fa(ins[:n_in], outs[:n_out], send_sems, recv_sems)
            fb(ins[n_in:], outs[n_out:], _Offset(send_sems, a.n_sems), _Offset(recv_sems, a.n_sems))
        return f

    return _Exchange(a.arrs + b.arrs, a.out_shapes + b.out_shapes, a.n_sems + b.n_sems,
                     both(a.start, b.start), both(a.finish, b.finish))


def _pair_swap(arrs):
    n = len(arrs)

    def start(ins, outs, send_sems, recv_sems):
        x, y, c, j, sib, chips = _place()
        for i in range(n):
            _remote(ins[i].at[:, 1 - c], outs[i], send_sems, recv_sems, i, sib).start()

    def finish(ins, outs, send_sems, recv_sems):
        x, y, c, j, sib, chips = _place()
        for i in range(n):
            _remote(ins[i].at[:, 1 - c], outs[i], send_sems, recv_sems, i, sib).wait()

    shapes = [jax.ShapeDtypeStruct((a.shape[0],) + a.shape[2:], a.dtype) for a in arrs]
    return _Exchange(arrs, shapes, n, start, finish)


def _pair_send(arrs):
    n = len(arrs)

    def start(ins, outs, send_sems, recv_sems):
        x, y, c, j, sib, chips = _place()
        for i in range(n):
            _remote(ins[i], outs[i], send_sems, recv_sems, i, sib).start()

    def finish(ins, outs, send_sems, recv_sems):
        x, y, c, j, sib, chips = _place()
        for i in range(n):
            _remote(ins[i], outs[i], send_sems, recv_sems, i, sib).wait()

    shapes = [jax.ShapeDtypeStruct(a.shape, a.dtype) for a in arrs]
    return _Exchange(arrs, shapes, n, start, finish)


def _chip_exchange(arrs, *, scatter):
    n = len(arrs)

    def copies(ins, outs, send_sems, recv_sems):
        x, y, c, j, sib, chips = _place()
        return [(_remote(ins[i].at[2 * cx + cy] if scatter else ins[i], outs[i].at[j], send_sems, recv_sems,
                         3 * i + k, (cx, cy, c)),
                 _remote(outs[i].at[2 * cx + cy], outs[i].at[2 * cx + cy], send_sems, recv_sems, 3 * i + k, (cx, cy, c)))
                for i in range(n) for k, (cx, cy) in enumerate(chips)]

    def start(*refs):
        for out, _ in copies(*refs):
            out.start()

    def finish(*refs):
        for out, back in copies(*refs):
            back.wait_recv()
            out.wait_send()

    shapes = [jax.ShapeDtypeStruct((N_CHIPS,) + a.shape[-2:], a.dtype) for a in arrs]
    return _Exchange(arrs, shapes, 3 * n, start, finish)


FLAT_ROWS = 512


def _add2(a, b, *, out_dtype, name):
    n, R, L = a.shape
    tr = _tile(R, FLAT_ROWS, 16)

    def body(a_ref, b_ref, o_ref):
        o_ref[...] = (a_ref[...].astype(F32) + b_ref[...].astype(F32)).astype(out_dtype)

    spec = pl.BlockSpec((n, tr, L), lambda i: (0, i, 0))
    return pl.pallas_call(
        body, name=name, grid=(R // tr,), in_specs=[spec, spec], out_specs=spec,
        out_shape=jax.ShapeDtypeStruct(a.shape, out_dtype), compiler_params=_cparams(("parallel",)),
    )(a, b)


def _sum_slots(buf, *, name):
    n, R, L = buf.shape
    tr = _tile(R, FLAT_ROWS, 16)

    def body(b_ref, o_ref):
        acc = b_ref[0].astype(F32)
        for k in range(1, n):
            acc = acc + b_ref[k].astype(F32)
        o_ref[...] = acc

    return pl.pallas_call(
        body, name=name, grid=(R // tr,), in_specs=[pl.BlockSpec((n, tr, L), lambda i: (0, i, 0))],
        out_specs=pl.BlockSpec((tr, L), lambda i: (i, 0)),
        out_shape=jax.ShapeDtypeStruct((R, L), F32), compiler_params=_cparams(("parallel",)),
    )(buf)


def _adamw_update(w, g, m, v):
    c1 = 1.0 - ADAM_B1 ** ADAM_STEP
    c2 = 1.0 - ADAM_B2 ** ADAM_STEP
    m = ADAM_B1 * m + (1.0 - ADAM_B1) * g
    v = ADAM_B2 * v + (1.0 - ADAM_B2) * (g * g)
    return -ADAM_LR * ((m / c1) / (jnp.sqrt(v / c2) + ADAM_EPS) + ADAM_WD * w), m, v


def _adamw_halves(w, m, v, own, other, *, name):
    NL, R, L = w.shape
    h = R // 2
    tr = _tile(h, FLAT_ROWS, 16)
    nt = h // tr

    def body(*refs):
        w_ref, m_ref, v_ref = refs[:3]
        own_refs, other_refs = refs[3:3 + NL], refs[3 + NL:3 + 2 * NL]
        d_ref, nm_ref, nv_ref, g_ref = refs[3 + 2 * NL:]
        layer, half = pl.program_id(0), pl.program_id(1)
        mine = half == lax.axis_index("c")
        g = jnp.where(mine, own_refs[0][...], other_refs[0][...])
        for l in range(1, NL):
            g = jnp.where(layer == l, jnp.where(mine, own_refs[l][...], other_refs[l][...]), g)
        d, mm, vv = _adamw_update(w_ref[0], g, m_ref[0], v_ref[0])
        d_ref[0], nm_ref[0], nv_ref[0], g_ref[0] = d, mm, vv, g

    spec = pl.BlockSpec((1, tr, L), lambda l, hh, i: (l, hh * nt + i, 0))
    part = pl.BlockSpec((tr, L), lambda l, hh, i: (i, 0))
    sh = jax.ShapeDtypeStruct((NL, R, L), F32)
    return pl.pallas_call(
        body, name=name, grid=(NL, 2, nt), in_specs=[spec] * 3 + [part] * (2 * NL), out_specs=[spec] * 4,
        out_shape=[sh] * 4, compiler_params=_cparams(("parallel", "parallel", "parallel")),
    )(w, m, v, *own, *other)


def _adamw(w, g, m, v, *, name):
    NL, R, L = w.shape
    tr = _tile(R, FLAT_ROWS, 16)

    def body(w_ref, g_ref, m_ref, v_ref, d_ref, nm_ref, nv_ref):
        d_ref[...], nm_ref[...], nv_ref[...] = _adamw_update(w_ref[...], g_ref[...], m_ref[...], v_ref[...])

    spec = pl.BlockSpec((1, tr, L), lambda l, i: (l, i, 0))
    sh = jax.ShapeDtypeStruct((NL, R, L), F32)
    return pl.pallas_call(
        body, name=name, grid=(NL, R // tr), in_specs=[spec] * 4, out_specs=[spec] * 3, out_shape=[sh] * 3,
        compiler_params=_cparams(("parallel", "parallel")),
    )(w, g, m, v)


WEIGHT_NAMES = ["ab_norm", "ab_w_in", "ab_q_norm", "ab_w_q_b", "ab_kv_norm", "ab_w_kv_b", "ab_conv_w", "ab_conv_b",
                "ab_w_rg_a", "ab_b_rg_a", "ab_w_rg_x", "ab_b_rg_x", "ab_lambda", "ab_w_out", "c_norm", "c_w_in",
                "c_ln_g", "c_ln_b", "c_w_s", "c_b_s", "c_w_out", "ffn_norm", "ffn_w_gate", "ffn_w_up", "ffn_conv_w",
                "ffn_conv_b", "ffn_w_down", "final_norm"]
BIG = {"ab_w_in": 2, "ab_w_q_b": 2, "ab_w_kv_b": 2, "ab_w_out": 1, "c_w_in": 2, "c_w_out": 1,
       "ffn_w_gate": 2, "ffn_w_up": 2, "ffn_w_down": 1}
SMALL_SHARDED = {"ab_conv_w": 2, "c_norm": 1, "c_ln_g": 1, "c_ln_b": 1, "ffn_conv_w": 2}
SMALL_REPLICATED = [n for n in WEIGHT_NAMES if n not in BIG and n not in SMALL_SHARDED]


def _rows(n_elems, mult):
    r = -(-n_elems // LANES)
    return -(-r // mult) * mult


def _flat(parts, rows):
    flat = jnp.concatenate([a.reshape(-1) for a in parts])
    return jnp.pad(flat, (0, rows * LANES - flat.shape[0])).reshape(rows, LANES)


def _unflat(flat, shapes):
    flat = flat.reshape(-1)
    out, off = [], 0
    for s in shapes:
        n = math.prod(s)
        out.append(flat[off:off + n].reshape(s))
        off += n
    return out


def _join_shards(a, axis):
    a = jnp.moveaxis(a, 0, axis)
    return a.reshape(a.shape[:axis] + (a.shape[axis] * a.shape[axis + 1],) + a.shape[axis + 2:])


def kernel(x, positions, ab_norm, ab_w_in, ab_q_norm, ab_w_q_b, ab_kv_norm, ab_w_kv_b, ab_conv_w, ab_conv_b, ab_w_rg_a, ab_b_rg_a, ab_w_rg_x, ab_b_rg_x, ab_lambda, ab_w_out, c_norm, c_w_in, c_ln_g, c_ln_b, c_w_s, c_b_s, c_w_out, ffn_norm, ffn_w_gate, ffn_w_up, ffn_conv_w, ffn_conv_b, ffn_w_down, final_norm, loss_target, m_ab_norm, m_ab_w_in, m_ab_q_norm, m_ab_w_q_b, m_ab_kv_norm, m_ab_w_kv_b, m_ab_conv_w, m_ab_conv_b, m_ab_w_rg_a, m_ab_b_rg_a, m_ab_w_rg_x, m_ab_b_rg_x, m_ab_lambda, m_ab_w_out, m_c_norm, m_c_w_in, m_c_ln_g, m_c_ln_b, m_c_w_s, m_c_b_s, m_c_w_out, m_ffn_norm, m_ffn_w_gate, m_ffn_w_up, m_ffn_conv_w, m_ffn_conv_b, m_ffn_w_down, m_final_norm, v_ab_norm, v_ab_w_in, v_ab_q_norm, v_ab_w_q_b, v_ab_kv_norm, v_ab_w_kv_b, v_ab_conv_w, v_ab_conv_b, v_ab_w_rg_a, v_ab_b_rg_a, v_ab_w_rg_x, v_ab_b_rg_x, v_ab_lambda, v_ab_w_out, v_c_norm, v_c_w_in, v_c_ln_g, v_c_ln_b, v_c_w_s, v_c_b_s, v_c_w_out, v_ffn_norm, v_ffn_w_gate, v_ffn_w_up, v_ffn_conv_w, v_ffn_conv_b, v_ffn_w_down, v_final_norm):
    given = dict(locals())
    w = {n: given[n] for n in WEIGHT_NAMES}
    m = {n: given["m_" + n] for n in WEIGHT_NAMES}
    v = {n: given["v_" + n] for n in WEIGHT_NAMES}
    c = lax.axis_index("c")
    chip = 2 * lax.axis_index("x") + lax.axis_index("y")

    halves = lambda a: a.reshape(a.shape[0], 2, a.shape[1] // 2, a.shape[2])
    tr = lambda a: jnp.swapaxes(a, 1, 2)
    send = {"ab_w_in": w["ab_w_in"], "ab_w_q_b": w["ab_w_q_b"], "ab_w_kv_b": w["ab_w_kv_b"], "ab_w_out": w["ab_w_out"],
            "c_w_in": tr(w["c_w_in"]), "c_w_out": w["c_w_out"], "ffn_w_gate": tr(w["ffn_w_gate"]),
            "ffn_w_up": tr(w["ffn_w_up"]), "ffn_w_down": w["ffn_w_down"]}
    small_rows = _rows(sum(w[n].size for n in SMALL_SHARDED), 16)
    small_sh = _flat([w[n] for n in SMALL_SHARDED], small_rows).reshape(1, 2, small_rows // 2, LANES)
    first_names = ["ab_w_in", "ab_w_q_b", "ab_w_kv_b", "ab_w_out"]
    mine = {n: halves(send[n].astype(BF16)) for n in BIG}

    def put_own(own, arrived):
        a = _put(arrived, own, chip, 1)
        return a.reshape(a.shape[0], -1, a.shape[-1])

    p = {"ab_norm": w["ab_norm"], "ffn_gate_t": {}, "ffn_up_t": {}, "ffn_down": {}}
    first = [mine[n] for n in first_names] + [small_sh]

    def first_arrived(got):
        full = {n: put_own(o, a) for n, o, a in zip(first_names + ["small"], first, got)}
        unshard = lambda a: jnp.swapaxes(a.reshape(N_CHIPS, -1, a.shape[-1]), 0, 1).reshape(-1, N_CHIPS * a.shape[-1])
        p.update(_prep_big(unshard(full["ab_w_in"][0]), unshard(full["ab_w_q_b"][0]), unshard(full["ab_w_kv_b"][0])))
        p["ab_w_out"] = full["ab_w_out"][0]
        small_full = dict(w)
        off = 0
        small_got = full["small"].reshape(N_CHIPS, -1)
        for n, ax in SMALL_SHARDED.items():
            seg = small_got[:, off:off + w[n].size].reshape((N_CHIPS,) + w[n].shape)
            small_full[n] = _join_shards(seg, ax)
            off += w[n].size
        p.update(_prep_small(small_full))

    def weights_ride(parts):
        def sink(arrived):
            for (own, setter), a in zip(parts, arrived):
                setter(put_own(own, a)[0])
        return _all_gather([own for own, _ in parts]), sink

    ffn_keys = {"ffn_gate_t": "ffn_w_gate", "ffn_up_t": "ffn_w_up", "ffn_down": "ffn_w_down"}
    ffn_part = lambda key, l: (mine[ffn_keys[key]][l:l + 1], functools.partial(p[key].__setitem__, l))
    rides = {
        "ab_norm": (_all_gather(first), first_arrived),
        "attn_fwd": weights_ride([ffn_part("ffn_gate_t", 0), ffn_part("ffn_up_t", 0)]),
        "lru_fwd": weights_ride([ffn_part("ffn_down", 0)]),
        "ffn0_gate": weights_ride([ffn_part("ffn_gate_t", 1)]),
        "ffn0_up": weights_ride([ffn_part("ffn_up_t", 1)]),
        "ffn0_down": weights_ride([ffn_part("ffn_down", 1),
                                   (mine["c_w_in"], functools.partial(p.__setitem__, "c_w_in_t")),
                                   (mine["c_w_out"], functools.partial(p.__setitem__, "c_w_out"))]),
    }

    def chip_sums(pair, arrived, tag):
        own = [lax.dynamic_index_in_dim(a, chip, axis=0, keepdims=False) for a in pair]
        return [_sum_slots(_put(a, o, chip, 0), name=f"grad_chip_sum_{tag}{i}") for i, (a, o) in enumerate(zip(arrived, own))]

    half_of = {}

    def grads_ready(layer, ready):
        if layer == 1:
            named = {"gate1": ready["ffn_gate_t"], "up1": ready["ffn_up_t"], "down1": ready["ffn_down"]}
            hosts = {"sgu_bwd": ["down1"], "ffn0_dactbwd": ["gate1", "up1"]}
        else:
            named = {"c_in": ready["c_w_in_t"], "c_out": ready["c_w_out"], "gate0": ready["ffn_gate_t"],
                     "up0": ready["ffn_up_t"], "down0": ready["ffn_down"]}
            hosts = {"attn_bwd": ["c_in", "c_out", "down0", "gate0", "up0"]}
        tag = f"f{layer}"
        sharded = [a.reshape(N_CHIPS, 2, -1, a.shape[-1]) for a in named.values()]

        def paired(from_sib):
            own = [lax.dynamic_index_in_dim(a, c, axis=1, keepdims=False) for a in sharded]
            pair = {k: _add2(a, b, out_dtype=BF16, name=f"grad_pair_add_{tag}{i}")
                    for i, (k, a, b) in enumerate(zip(named, own, from_sib))}
            for kernel_name, keys in hosts.items():
                def sink(arrived, keys=keys, kernel_name=kernel_name):
                    half_of.update(zip(keys, chip_sums([pair[k] for k in keys], arrived, f"{tag}_{kernel_name}")))
                rides[kernel_name] = (_chip_exchange([pair[k] for k in keys], scatter=True), sink)

        rides[f"ffn{layer}_dnorm"] = (_pair_swap(sharded), paired)

    loss_row, grad_x, g = _local_step(x, positions, loss_target, p, rides, grads_ready)

    cols = lambda a, n: jnp.swapaxes(a.reshape(a.shape[0], N_CHIPS, n), 0, 1)
    n_in, n_q, n_kv = w["ab_w_in"].shape[2], w["ab_w_q_b"].shape[2], w["ab_w_kv_b"].shape[2]
    small_names = SMALL_REPLICATED + list(SMALL_SHARDED)
    rs = _rows(sum(g[n].size for n in small_names) + LANES, FLAT_ROWS)
    small = _flat([loss_row] + [g[n] for n in small_names], rs)
    slot = (jnp.arange(2) == c)[:, None, None]
    last = [cols(_unperm_w_in(g["w_in_p"]), n_in), cols(_from_head_blocks(g["w_q_p"], QK_NOPE + QK_ROPE), n_q),
            cols(_join_kv(g["w_k_p"], g["w_v_p"]), n_kv), g["ab_w_out"]]
    last = [a.reshape(N_CHIPS, 2, -1, a.shape[-1]) for a in last]
    *from_sib, small_sib = _merge(_pair_swap(last), _pair_send([small])).run("tail_pair")
    own = [lax.dynamic_index_in_dim(a, c, axis=1, keepdims=False) for a in last]
    pair = [_add2(a, b, out_dtype=BF16, name=f"grad_pair_add_b{i}") for i, (a, b) in enumerate(zip(own, from_sib))]
    pair_small = _sum_slots(jnp.where(slot, small[None], small_sib[None]), name="small_pair_sum")
    my_small = lax.dynamic_index_in_dim(pair_small.reshape(2, rs // 2, LANES), c, axis=0, keepdims=False)
    *arrived, all_small = _merge(_chip_exchange(pair, scatter=True), _chip_exchange([my_small], scatter=False)).run("tail_chip")
    half_of.update(zip(["in", "q", "kv", "out"], chip_sums(pair, arrived, "b")))
    half_of["small"] = _sum_slots(_put(all_small, my_small, chip, 0), name="small_chip_sum")
    keys = ("in", "q", "kv", "out", "c_in", "c_out", "gate0", "gate1", "up0", "up1", "down0", "down1", "small")
    other_half = dict(zip(keys, _pair_send([half_of[k] for k in keys]).run("grad_pair_share")))
    small_sum = jnp.where(slot, half_of["small"][None], other_half["small"][None]).reshape(rs, LANES)
    whole = lambda k: jnp.where(slot, half_of[k][None], other_half[k][None]).reshape(-1, half_of[k].shape[-1])
    grads_t = {"ab_w_in": whole("in").T[None], "ab_w_q_b": whole("q").T[None]}
    grads = {"ab_w_kv_b": whole("kv")[None], "c_w_in": whole("c_in").T[None], **{n: tr(a) for n, a in grads_t.items()}}
    by_halves = {"ab_w_out": (("out",), False), "c_w_out": (("c_out",), False), "ffn_w_down": (("down0", "down1"), False),
                 "ffn_w_gate": (("gate0", "gate1"), True), "ffn_w_up": (("up0", "up1"), True)}

    small_parts = _unflat(small_sum, [(1, LANES)] + [g[n].shape for n in small_names])
    loss = small_parts[0][0, 0]
    for n, a in zip(small_names, small_parts[1:]):
        if n in SMALL_SHARDED:
            ax = SMALL_SHARDED[n]
            a = lax.dynamic_slice_in_dim(a, chip * w[n].shape[ax], w[n].shape[ax], axis=ax)
        grads[n] = a.reshape(w[n].shape)

    delta, new_m, new_v = {}, {}, {}
    for n in BIG:
        if n in by_halves:
            ks, transposed = by_halves[n]
            view = tr if transposed else (lambda a: a)
            out = _adamw_halves(view(w[n]), view(m[n]), view(v[n]), [half_of[k] for k in ks], [other_half[k] for k in ks],
                                name=f"adamw_{n}")
            delta[n], new_m[n], new_v[n], grads[n] = (view(a) for a in out)
        elif n in grads_t:
            out = _adamw(tr(w[n]), grads_t[n], tr(m[n]), tr(v[n]), name=f"adamw_{n}")
            delta[n], new_m[n], new_v[n] = (tr(a) for a in out)
        else:
            delta[n], new_m[n], new_v[n] = _adamw(w[n], grads[n], m[n], v[n], name=f"adamw_{n}")
    small_all = [n for n in WEIGHT_NAMES if n not in BIG]
    ra = _rows(sum(w[n].size for n in small_all), FLAT_ROWS)
    pack = lambda d: _flat([d[n] for n in small_all], ra)[None]
    out = _adamw(pack(w), pack(grads), pack(m), pack(v), name="adamw_small")
    shapes = [w[n].shape for n in small_all]
    for d, flat in zip((delta, new_m, new_v), out):
        d.update(zip(small_all, _unflat(flat, shapes)))
    return (loss, grad_x, *[grads[n] for n in WEIGHT_NAMES], *[delta[n] for n in WEIGHT_NAMES],
            *[new_m[n] for n in WEIGHT_NAMES], *[new_v[n] for n in WEIGHT_NAMES])
```

```python
import functools
import math

import jax
import jax.numpy as jnp
from jax import lax
from jax.experimental import pallas as pl
from jax.experimental.pallas import tpu as pltpu

F32 = jnp.float32
BF16 = jnp.bfloat16
MESH = pl.DeviceIdType.MESH

D_MODEL = 1024
MLA_HEADS = 8
Q_LORA = 256
KV_LORA = 128
QK_NOPE = 64
QK_ROPE = 32
V_HEAD = 64
LRU_WIDTH = 512
LRU_HEADS = 8
LRU_BLOCK = 64
LRU_CONV = 4
LRU_C = 8.0
CHUNK = 128
SGU_GROUPS = 8
SGU_WIDTH = 1024
D_FF = 2816
FFN_CONV = 3
NORM_EPS = 1e-6
ROPE_BASE = 10000.0
AB_IN_PAD = 1536
ADAM_LR = 0.001
ADAM_B1 = 0.9
ADAM_B2 = 0.999
ADAM_EPS = 1e-08
ADAM_WD = 0.01
ADAM_STEP = 10

N_CHIPS = 4
LANES = 128
VMEM_LIMIT = 56 * 1024 * 1024
ROW_TILE = 256
NORM_TILE = 1024
MM_TM, MM_TN, MM_TK = 1024, 1536, 2816
MM_TM_T, MM_TK_T = 1408, 1024
GELU_C = math.sqrt(2.0 / math.pi)


def _cparams(sem):
    return pltpu.CompilerParams(dimension_semantics=sem, vmem_limit_bytes=VMEM_LIMIT)


def _tile(n, target, mult=LANES):
    t = (min(n, target) // mult) * mult
    while t >= mult:
        if n % t == 0:
            return t
        t -= mult
    return n


GELU_K = GELU_C * 0.044715


def _gelu(x):
    t = jnp.tanh(x * (GELU_C + GELU_K * (x * x)))
    hx = 0.5 * x
    return hx + hx * t


def _gelu_and_grad(x):
    x2 = x * x
    t = jnp.tanh(x * (GELU_C + GELU_K * x2))
    hx = 0.5 * x
    dg = (0.5 + 0.5 * t) + (hx * (1.0 - t * t)) * (GELU_C + (3.0 * GELU_K) * x2)
    return hx + hx * t, dg


def _sigmoid(x):
    return 1.0 / (1.0 + jnp.exp(-x))


def _shift_rows(x, d, fill_rows):
    ext = jnp.concatenate([fill_rows, x], axis=0)
    return pltpu.roll(ext, d, 0)[8:]


def _shift_rows_up(x, d, fill_rows):
    n = x.shape[0]
    ext = jnp.concatenate([x, fill_rows], axis=0)
    return pltpu.roll(ext, n + 8 - d, 0)[:n]


def _dot(a, b, dims):
    return lax.dot_general(a.astype(BF16), b.astype(BF16), (dims, ((), ())), preferred_element_type=F32)


def _dot_nn(a, b):
    return _dot(a, b, ((1,), (0,)))


def _dot_nt(a, b):
    return _dot(a, b, ((1,), (1,)))


def _dot_tn(a, b):
    return _dot(a, b, ((0,), (0,)))


def _mm(a, b, *, name, ta=False, tb=False, res=None, out_dtype=F32, ride=None):
    if ta:
        K, M = a.shape
    else:
        M, K = a.shape
    N = b.shape[0] if tb else b.shape[1]
    tm = _tile(M, MM_TM_T if ta else (MM_TM if K <= MM_TM else MM_TM // 2), LANES if ta else 8)
    tn = _tile(N, MM_TN, LANES)
    tk = _tile(K, MM_TK_T if ta else MM_TK, LANES)
    nk = K // tk
    a_spec = pl.BlockSpec((tk, tm), lambda j, i, k: (k, i)) if ta else pl.BlockSpec((tm, tk), lambda j, i, k: (i, k))
    b_spec = pl.BlockSpec((tn, tk), lambda j, i, k: (j, k)) if tb else pl.BlockSpec((tk, tn), lambda j, i, k: (k, j))
    o_spec = pl.BlockSpec((tm, tn), lambda j, i, k: (i, j))
    dims = ((0,) if ta else (1,), (1,) if tb else (0,))
    has_res = res is not None

    def body(*refs):
        a_ref, b_ref = refs[:2]
        r_ref = refs[2] if has_res else None
        o_ref = refs[3] if has_res else refs[2]
        p = _dot(a_ref[...], b_ref[...], dims)

        def finish(r):
            if has_res:
                r = r + r_ref[...].astype(F32)
            o_ref[...] = r.astype(out_dtype)

        if nk == 1:
            finish(p)
            return
        acc_ref = refs[-1]
        k = pl.program_id(2)

        @pl.when(k == 0)
        def _():
            acc_ref[...] = p

        @pl.when(jnp.logical_and(k > 0, k < nk - 1))
        def _():
            acc_ref[...] += p

        @pl.when(k == nk - 1)
        def _():
            finish(acc_ref[...] + p)

    in_specs = [a_spec, b_spec] + ([o_spec] if has_res else [])
    args = (a, b) + ((res,) if has_res else ())
    return _pcall(
        body, name=name, grid=(N // tn, M // tm, nk), in_specs=in_specs, out_specs=[o_spec],
        out_shape=[jax.ShapeDtypeStruct((M, N), out_dtype)], args=args,
        scratch=[pltpu.VMEM((tm, tn), F32)] if nk > 1 else [], sem=("parallel", "parallel", "arbitrary"), ride=ride)[0]


def _rms_fwd(x, g, *, name, cb=0, out_dtype=BF16, ride=None):
    T = x.shape[0]
    W = g.shape[-1]
    g = g.reshape(1, W)
    tt = _tile(T, NORM_TILE, 16)

    def body(x_ref, g_ref, o_ref):
        xf = x_ref[...].astype(F32)
        rstd = lax.rsqrt(jnp.mean(xf * xf, axis=-1, keepdims=True) + NORM_EPS)
        o_ref[...] = (xf * rstd * g_ref[...]).astype(out_dtype)

    return _pcall(
        body, name=name, grid=(T // tt,),
        in_specs=[pl.BlockSpec((tt, W), lambda i: (i, cb)), pl.BlockSpec((1, W), lambda i: (0, 0))],
        out_specs=[pl.BlockSpec((tt, W), lambda i: (i, 0))], out_shape=[jax.ShapeDtypeStruct((T, W), out_dtype)],
        args=(x, g), sem=("parallel",), ride=ride)[0]


def _rms_bwd(x, g, dy, *, name, cb=0, res=None, out_dtype=F32, ride=None):
    T = x.shape[0]
    W = g.shape[-1]
    g = g.reshape(1, W)
    tt = _tile(T, NORM_TILE // 2, 16)
    has_res = res is not None

    def body(*refs):
        if has_res:
            x_ref, g_ref, dy_ref, r_ref, dx_ref, dg_ref = refs
        else:
            x_ref, g_ref, dy_ref, dx_ref, dg_ref = refs
        xf = x_ref[...].astype(F32)
        dyf = dy_ref[...].astype(F32)
        rstd = lax.rsqrt(jnp.mean(xf * xf, axis=-1, keepdims=True) + NORM_EPS)
        xhat = xf * rstd
        dxhat = dyf * g_ref[...]
        dx = rstd * (dxhat - xhat * jnp.mean(dxhat * xhat, axis=-1, keepdims=True))
        if has_res:
            dx = dx + r_ref[...].astype(F32)
        dx_ref[...] = dx.astype(out_dtype)
        part = jnp.sum(dyf * xhat, axis=0, keepdims=True)

        @pl.when(pl.program_id(0) == 0)
        def _():
            dg_ref[...] = part

        @pl.when(pl.program_id(0) > 0)
        def _():
            dg_ref[...] += part

    row = pl.BlockSpec((tt, W), lambda i: (i, 0))
    in_specs = [pl.BlockSpec((tt, W), lambda i: (i, cb)), pl.BlockSpec((1, W), lambda i: (0, 0)), row]
    args = (x, g, dy)
    if has_res:
        in_specs.append(row)
        args = args + (res,)
    return _pcall(
        body, name=name, grid=(T // tt,), in_specs=in_specs,
        out_specs=[row, pl.BlockSpec((1, W), lambda i: (0, 0))],
        out_shape=[jax.ShapeDtypeStruct((T, W), out_dtype), jax.ShapeDtypeStruct((1, W), F32)], args=args, ride=ride)


def _final_fwd_bwd(h, g, target, *, name):
    T, W = h.shape
    g = g.reshape(1, W)
    tt = _tile(T, NORM_TILE, 16)

    def body(x_ref, g_ref, t_ref, loss_ref, dx_ref, dg_ref):
        xf = x_ref[...]
        rstd = lax.rsqrt(jnp.mean(xf * xf, axis=-1, keepdims=True) + NORM_EPS)
        xhat = xf * rstd
        err = xhat * g_ref[...] - t_ref[...]
        lpart = jnp.zeros((1, LANES), F32) + (0.5 / W) * jnp.sum(err * err)
        dyf = err * (1.0 / W)
        dxhat = dyf * g_ref[...]
        dx_ref[...] = rstd * (dxhat - xhat * jnp.mean(dxhat * xhat, axis=-1, keepdims=True))
        part = jnp.sum(dyf * xhat, axis=0, keepdims=True)

        @pl.when(pl.program_id(0) == 0)
        def _():
            dg_ref[...] = part
            loss_ref[...] = lpart

        @pl.when(pl.program_id(0) > 0)
        def _():
            dg_ref[...] += part
            loss_ref[...] += lpart

    row = pl.BlockSpec((tt, W), lambda i: (i, 0))
    return pl.pallas_call(
        body, name=name, grid=(T // tt,),
        in_specs=[row, pl.BlockSpec((1, W), lambda i: (0, 0)), row],
        out_specs=[pl.BlockSpec((1, LANES), lambda i: (0, 0)), row, pl.BlockSpec((1, W), lambda i: (0, 0))],
        out_shape=[jax.ShapeDtypeStruct((1, LANES), F32), jax.ShapeDtypeStruct((T, W), F32),
                   jax.ShapeDtypeStruct((1, W), F32)],
        compiler_params=_cparams(("arbitrary",)),
    )(h, g, target)


def _swap16(x):
    lane = lax.broadcasted_iota(jnp.int32, x.shape, 1)
    return jnp.where((lane % 32) < 16, pltpu.roll(x, LANES - 16, 1), pltpu.roll(x, 16, 1))


def _rope(x, c, s):
    return x * c + _swap16(x) * s


def _rope_t(d, c, s):
    return d * c + _swap16(d * s)


def _head_block_map(fn, x, cos, sin, *, name):
    T, W = x.shape
    tt = _tile(T, NORM_TILE, 16)

    def body(x_ref, c_ref, s_ref, o_ref):
        c, s = c_ref[...], s_ref[...]
        for h in range(W // LANES):
            lanes = slice(h * LANES, (h + 1) * LANES)
            o_ref[:, lanes] = fn(x_ref[:, lanes], c, s).astype(BF16)

    tab = pl.BlockSpec((tt, LANES), lambda i: (i, 0))
    blk = pl.BlockSpec((tt, W), lambda i: (i, 0))
    return pl.pallas_call(
        body, name=name, grid=(T // tt,), in_specs=[blk, tab, tab], out_specs=blk,
        out_shape=jax.ShapeDtypeStruct((T, W), BF16), compiler_params=_cparams(("parallel",)),
    )(x, cos, sin)


def _rope_q(q, cos, sin, *, name):
    scale = _attn_scale()
    return _head_block_map(lambda x, c, s: _rope(x, c, s) * scale, q, cos, sin, name=name)


def _rope_q_bwd(dq, cos, sin, *, name):
    return _head_block_map(_rope_t, dq, cos, sin, name=name)


def _key_blocks(kv, z, cos, sin, *, kpe_block, name):
    T = kv.shape[0]
    tt = _tile(T, NORM_TILE, 16)
    W = MLA_HEADS * LANES

    def body(kv_ref, z_ref, c_ref, s_ref, o_ref):
        kr = _rope(z_ref[...], c_ref[...], s_ref[...])
        for h in range(MLA_HEADS):
            lanes = slice(h * LANES, (h + 1) * LANES)
            o_ref[:, lanes] = (kv_ref[:, lanes].astype(F32) + kr).astype(BF16)

    tab = pl.BlockSpec((tt, LANES), lambda i: (i, 0))
    blk = pl.BlockSpec((tt, W), lambda i: (i, 0))
    return pl.pallas_call(
        body, name=name, grid=(T // tt,),
        in_specs=[blk, pl.BlockSpec((tt, LANES), lambda i: (i, kpe_block)), tab, tab], out_specs=blk,
        out_shape=jax.ShapeDtypeStruct((T, W), BF16), compiler_params=_cparams(("parallel",)),
    )(kv, z, cos, sin)


def _key_rope_bwd(dk, cos, sin, *, name):
    T = dk.shape[0]
    tt = _tile(T, NORM_TILE, 16)

    def body(d_ref, c_ref, s_ref, o_ref):
        d = d_ref[:, :LANES]
        for h in range(1, MLA_HEADS):
            d = d + d_ref[:, h * LANES:(h + 1) * LANES]
        lane = lax.broadcasted_iota(jnp.int32, d.shape, 1)
        d = jnp.where(jnp.logical_and(lane >= QK_NOPE, lane < QK_NOPE + QK_ROPE), d, 0.0)
        o_ref[...] = _rope_t(d, c_ref[...], s_ref[...]).astype(BF16)

    tab = pl.BlockSpec((tt, LANES), lambda i: (i, 0))
    return pl.pallas_call(
        body, name=name, grid=(T // tt,),
        in_specs=[pl.BlockSpec((tt, MLA_HEADS * LANES), lambda i: (i, 0)), tab, tab], out_specs=tab,
        out_shape=jax.ShapeDtypeStruct((T, LANES), BF16), compiler_params=_cparams(("parallel",)),
    )(dk, cos, sin)


ATT_BLOCK = 512


def _attn_scale():
    return float((QK_NOPE + QK_ROPE) ** -0.5)


def _causal_mask(qi, kj, tq, tk):
    row = qi * tq + lax.broadcasted_iota(jnp.int32, (tq, tk), 0)
    col = kj * tk + lax.broadcasted_iota(jnp.int32, (tq, tk), 1)
    return col <= row


def _pcall(body, *, name, grid, in_specs, out_specs, out_shape, args, scratch=(), sem=None, ride=None):
    n_in, n_out, n_scr = len(args), len(out_shape), len(scratch)
    if ride is None:
        return pl.pallas_call(
            body, name=name, grid=grid, in_specs=list(in_specs), out_specs=list(out_specs), out_shape=list(out_shape),
            scratch_shapes=list(scratch), compiler_params=_cparams(sem or ("arbitrary",) * len(grid)))(*args)
    ex, sink = ride
    o0 = n_in + len(ex.arrs)
    s0 = o0 + n_out + len(ex.out_shapes)

    def hosted(*refs):
        parts = (refs[n_in:o0], refs[o0 + n_out:s0], refs[-2], refs[-1])
        ids = [pl.program_id(i) for i in range(len(grid))]
        pl.when(functools.reduce(jnp.logical_and, [i == 0 for i in ids]))(lambda: ex.start(*parts))
        body(*refs[:n_in], *refs[o0:o0 + n_out], *refs[s0:s0 + n_scr])
        pl.when(functools.reduce(jnp.logical_and, [i == n - 1 for i, n in zip(ids, grid)]))(lambda: ex.finish(*parts))

    outs = pl.pallas_call(
        hosted, name=name, grid=grid, in_specs=list(in_specs) + ex.in_specs, out_specs=list(out_specs) + ex.out_specs,
        out_shape=list(out_shape) + ex.out_shapes, scratch_shapes=list(scratch) + ex.scratch,
        compiler_params=_cparams(("arbitrary",) * len(grid)))(*args, *ex.arrs)
    sink(outs[n_out:])
    return outs[:n_out]


PAIRS = MLA_HEADS // 2


def _own_lanes(x, first):
    lane = lax.broadcasted_iota(jnp.int32, x.shape, 1)
    return jnp.where((lane < V_HEAD) if first else (lane >= V_HEAD), x, 0.0)


def _lane_sums_as_row(x):
    hi = x.astype(BF16)
    lo = (x - hi.astype(F32)).astype(BF16)
    ones = jnp.ones((8, LANES), BF16)
    return (_dot_nt(ones, hi) + _dot_nt(ones, lo))[0:1, :]


def _attn_fwd(q, k, kv, *, B, S, v_block0, name, ride=None):
    tq = tk = min(ATT_BLOCK, S)
    nq = S // tq
    T = B * S

    def body(q_ref, k_ref, v_ref, o_ref, lse_ref):
        qi = pl.program_id(2)
        qs = (q_ref[:, :LANES], q_ref[:, LANES:])

        def step(masked):
            def f(j, carry):
                rows = pl.ds(pl.multiple_of(j * tk, tk), tk)
                vb = v_ref[rows, :]
                out = []
                for h in range(2):
                    m, l, acc = carry[h]
                    s = _dot_nt(qs[h], k_ref[rows, h * LANES:(h + 1) * LANES])
                    if masked:
                        s = jnp.where(_causal_mask(qi, j, tq, tk), s, -jnp.inf)
                    m_new = jnp.maximum(m, jnp.max(s, axis=-1, keepdims=True))
                    alpha = jnp.exp(m - m_new)
                    p = jnp.exp(s - m_new)
                    out.append((m_new, alpha * l + jnp.sum(p, axis=-1, keepdims=True), alpha * acc + _dot_nn(p, vb)))
                return tuple(out)
            return f

        one = (jnp.full((tq, 1), -1e30, F32), jnp.zeros((tq, 1), F32), jnp.zeros((tq, LANES), F32))
        (ma, la, acca), (mb, lb, accb) = step(True)(qi, lax.fori_loop(0, qi, step(False), (one, one)))
        o_ref[...] = _own_lanes(acca / la, True) + _own_lanes(accb / lb, False)
        for h, lse in enumerate((ma + jnp.log(la), mb + jnp.log(lb))):
            lse_ref[0, h, pl.ds(qi, 1), :] = _lane_sums_as_row(jnp.broadcast_to(lse * (1.0 / LANES), (tq, LANES)))

    return _pcall(
        body, name=name, grid=(B, PAIRS, nq),
        in_specs=[pl.BlockSpec((tq, 2 * LANES), lambda b, g, i: (b * nq + i, g)),
                  pl.BlockSpec((S, 2 * LANES), lambda b, g, i: (b, g)),
                  pl.BlockSpec((S, LANES), lambda b, g, i: (b, v_block0 + g))],
        out_specs=[pl.BlockSpec((tq, LANES), lambda b, g, i: (b * nq + i, g)),
                   pl.BlockSpec((1, 2, nq, tq), lambda b, g, i: (b, g, 0, 0))],
        out_shape=[jax.ShapeDtypeStruct((T, PAIRS * LANES), F32), jax.ShapeDtypeStruct((B, MLA_HEADS, nq, tq), F32)],
        args=(q, k, kv), ride=ride)


def _attn_bwd(q, k, kv, o, lse_rows, do, *, B, S, v_block0, name, ride=None):
    tq = tk = min(ATT_BLOCK, S)
    nq = S // tq
    T = B * S
    scale = _attn_scale()

    def body(q_ref, k_ref, v_ref, o_ref, lse_ref, do_ref, dk_ref, dv_ref, dq_ref, delta_ref):
        kj = pl.program_id(2)
        ks = (k_ref[:, :LANES], k_ref[:, LANES:])
        vb = v_ref[...]

        @pl.when(kj == 0)
        def _():
            dq_ref[...] = jnp.zeros_like(dq_ref)
            for i in range(nq):
                prod = do_ref[i * tq:(i + 1) * tq, :] * o_ref[i * tq:(i + 1) * tq, :]
                for h in range(2):
                    delta_ref[h, i:i + 1, :] = _lane_sums_as_row(_own_lanes(prod, h == 0))

        def step(masked):
            def f(i, carry):
                rows = pl.ds(pl.multiple_of(i * tq, tq), tq)
                do_b = do_ref[rows, :]
                dks, dv = list(carry[:2]), carry[2]
                for h in range(2):
                    qb = q_ref[rows, h * LANES:(h + 1) * LANES]
                    doh = _own_lanes(do_b, h == 0)
                    pt = jnp.exp(_dot_nt(ks[h], qb) - lse_ref[0, h, pl.ds(i, 1), :])
                    if masked:
                        krow = kj * tk + lax.broadcasted_iota(jnp.int32, (tk, tq), 0)
                        qcol = i * tq + lax.broadcasted_iota(jnp.int32, (tk, tq), 1)
                        pt = jnp.where(krow <= qcol, pt, 0.0)
                    dst = pt * (_dot_nt(vb, doh) - delta_ref[h, pl.ds(i, 1), :])
                    dks[h] = dks[h] + _dot_nn(dst, qb)
                    dv = dv + _dot_nn(pt, doh)
                    dq_ref[rows, h * LANES:(h + 1) * LANES] += _dot_tn(dst, ks[h]) * scale
                return dks[0], dks[1], dv
            return f

        zero = jnp.zeros((tk, LANES), F32)
        dka, dkb, dv = lax.fori_loop(kj + 1, nq, step(False), step(True)(kj, (zero, zero, zero)))
        dk_ref[:, :LANES] = dka
        dk_ref[:, LANES:] = dkb
        dv_ref[...] = dv

    krow = lambda w, c0: pl.BlockSpec((tk, w), lambda b, g, j: (b * nq + j, c0 + g))
    seq = lambda w: pl.BlockSpec((S, w), lambda b, g, j: (b, g))
    stat = pl.BlockSpec((1, 2, nq, tq), lambda b, g, j: (b, g, 0, 0))
    dk, dv, dq = _pcall(
        body, name=name, grid=(B, PAIRS, nq),
        in_specs=[seq(2 * LANES), krow(2 * LANES, 0), krow(LANES, v_block0), seq(LANES), stat, seq(LANES)],
        out_specs=[krow(2 * LANES, 0), krow(LANES, 0), seq(2 * LANES)],
        out_shape=[jax.ShapeDtypeStruct((T, MLA_HEADS * LANES), F32), jax.ShapeDtypeStruct((T, PAIRS * LANES), F32),
                   jax.ShapeDtypeStruct((T, MLA_HEADS * LANES), F32)],
        args=(q, k, kv, o, lse_rows, do), scratch=[pltpu.VMEM((2, nq, tq), F32)], ride=ride)
    return dq, dk, dv


def _lru_gates(xl, halo, cw_ref, cb_ref, wa_ref, ba_ref, wx_ref, bx_ref, lam_ref):
    xc = cb_ref[...] + cw_ref[3:4, :] * xl
    for kk in range(LRU_CONV - 1):
        xc = xc + cw_ref[kk:kk + 1, :] * _shift_rows(xl, LRU_CONV - 1 - kk, halo)
    r = _sigmoid(_dot_nn(xc, wa_ref[...]) + ba_ref[...])
    i = _sigmoid(_dot_nn(xc, wx_ref[...]) + bx_ref[...])
    lam = lam_ref[...]
    sp = jnp.maximum(-lam, 0.0) + jnp.log(1.0 + jnp.exp(-jnp.abs(lam)))
    a = jnp.exp(-LRU_C * r * sp)
    mult = jnp.sqrt(1.0 - a * a)
    return xc, r, i, sp, a, mult


def _lru_specs(tt, nt, S):
    def make(rev):
        tmap = (lambda t: nt - 1 - t) if rev else (lambda t: t)
        tile = lambda cb: pl.BlockSpec((tt, LRU_WIDTH), lambda b, t: (b * nt + tmap(t), cb))
        prev8 = lambda cb: pl.BlockSpec(
            (8, LRU_WIDTH), lambda b, t: (jnp.maximum((b * nt + tmap(t)) * (tt // 8) - 1, 0), cb))
        return tile, prev8, tmap
    return make


def _lru_fwd(z, cw, cb, wa, ba, wx, bx, lam, *, S, name, ride=None):
    T = z.shape[0]
    tt = min(ROW_TILE, S)
    nt = S // tt
    tile, prev8, _ = _lru_specs(tt, nt, S)(False)
    vec = lambda r: pl.BlockSpec((r, LRU_WIDTH), lambda b, t: (0, 0))
    mat = pl.BlockSpec((LRU_WIDTH, LRU_WIDTH), lambda b, t: (0, 0))

    def body(xl_ref, halo_ref, gate_ref, cw_ref, cb_ref, wa_ref, ba_ref, wx_ref, bx_ref, lam_ref,
             y_ref, h_ref, carry_ref):
        t = pl.program_id(1)
        first = t == 0
        halo = jnp.where(first, 0.0, halo_ref[...])
        xl_t = xl_ref[...]
        xc, r, i, sp, a, mult = _lru_gates(xl_t, halo, cw_ref, cb_ref, wa_ref, ba_ref, wx_ref, bx_ref, lam_ref)
        bv = mult * (i * xc)
        ones = jnp.ones((8, LRU_WIDTH), F32)
        zeros = jnp.zeros((8, LRU_WIDTH), F32)
        row = lax.broadcasted_iota(jnp.int32, (tt, LRU_WIDTH), 0)
        A = a
        d = 1
        while d < tt:
            if d < 8:
                a_sh = _shift_rows(A, d, ones)
                b_sh = _shift_rows(bv, d, zeros)
            else:
                a_sh = jnp.where(row < d, 1.0, pltpu.roll(A, d, 0))
                b_sh = jnp.where(row < d, 0.0, pltpu.roll(bv, d, 0))
            bv = A * b_sh + bv
            A = A * a_sh
            d *= 2
        h0 = jnp.where(first, 0.0, carry_ref[0:1, :])
        h = A * h0 + bv
        carry_ref[...] = jnp.broadcast_to(h[tt - 1:tt, :], (8, LRU_WIDTH))
        h_ref[...] = h
        y_ref[...] = (h * _gelu(gate_ref[...])).astype(BF16)

    return _pcall(
        body, name=name, grid=(T // S, nt),
        in_specs=[tile(0), prev8(0), tile(1), vec(LRU_CONV), vec(1), mat, vec(1), mat, vec(1), vec(1)],
        out_specs=[tile(0), tile(0)],
        out_shape=[jax.ShapeDtypeStruct((T, LRU_WIDTH), BF16), jax.ShapeDtypeStruct((T, LRU_WIDTH), F32)],
        args=(z, z, z, cw, cb, wa, ba, wx, bx, lam), scratch=[pltpu.VMEM((8, LRU_WIDTH), F32)], ride=ride)


def _lru_bwd(z, h, dy, cw, cb, wa, ba, wx, bx, lam, *, S, name):
    T = z.shape[0]
    tt = min(ROW_TILE, S)
    nt = S // tt
    tile, prev8, tmap = _lru_specs(tt, nt, S)(True)
    vec = lambda r: pl.BlockSpec((r, LRU_WIDTH), lambda b, t: (0, 0))
    mat = pl.BlockSpec((LRU_WIDTH, LRU_WIDTH), lambda b, t: (0, 0))

    def body(xl_ref, halo_ref, gate_ref, h_ref, hprev_ref, dy_ref, cw_ref, cb_ref, wa_ref, ba_ref, wx_ref,
             bx_ref, lam_ref, dxl_ref, dgate_ref, dcw_ref, dcb_ref, dwa_ref, dba_ref, dwx_ref, dbx_ref,
             dlam_ref, lamc_ref, ac_ref, dxc_ref):
        b = pl.program_id(0)
        t = pl.program_id(1)
        tr = nt - 1 - t
        seq_first = tr == 0
        seq_last = t == 0
        halo = jnp.where(seq_first, 0.0, halo_ref[...])
        xl_t = xl_ref[...]
        xc, r, i, sp, a, mult = _lru_gates(xl_t, halo, cw_ref, cb_ref, wa_ref, ba_ref, wx_ref, bx_ref, lam_ref)
        hh = h_ref[...]
        dyf = dy_ref[...].astype(F32)
        gl, dgl = _gelu_and_grad(gate_ref[...])
        dgate_ref[...] = (dyf * hh * dgl).astype(BF16)
        dh = dyf * gl

        a_first_later = jnp.where(seq_last, 0.0, ac_ref[...])
        lam_later = jnp.where(seq_last, 0.0, lamc_ref[...])
        row = lax.broadcasted_iota(jnp.int32, (tt, LRU_WIDTH), 0)
        A = _shift_rows_up(a, 1, a_first_later)
        lm = dh
        ones = jnp.ones((8, LRU_WIDTH), F32)
        zeros = jnp.zeros((8, LRU_WIDTH), F32)
        d = 1
        while d < tt:
            if d < 8:
                a_sh = _shift_rows_up(A, d, ones)
                l_sh = _shift_rows_up(lm, d, zeros)
            else:
                a_sh = jnp.where(row >= tt - d, 1.0, pltpu.roll(A, tt - d, 0))
                l_sh = jnp.where(row >= tt - d, 0.0, pltpu.roll(lm, tt - d, 0))
            lm = lm + A * l_sh
            A = A * a_sh
            d *= 2
        lm = lm + A * lam_later[0:1, :]
        lamc_ref[...] = jnp.broadcast_to(lm[0:1, :], (8, LRU_WIDTH))
        ac_ref[...] = jnp.broadcast_to(a[0:1, :], (8, LRU_WIDTH))

        hprev_halo = jnp.where(seq_first, 0.0, hprev_ref[...])
        h_prev = _shift_rows(hh, 1, hprev_halo)
        da = lm * h_prev
        ixc = i * xc
        dmult = lm * ixc
        di = lm * mult * xc
        dxc = lm * mult * i
        da = da - dmult * a / mult
        dlog = da * a
        dr = dlog * (-LRU_C) * sp
        dsp_part = jnp.sum(dlog * (-LRU_C) * r, axis=0, keepdims=True)
        dpa = dr * r * (1.0 - r)
        dpx = di * i * (1.0 - i)
        dxc = dxc + _dot_nt(dpa, wa_ref[...]) + _dot_nt(dpx, wx_ref[...])
        dwa_part = _dot_tn(xc, dpa)
        dwx_part = _dot_tn(xc, dpx)

        later = jnp.where(seq_last, 0.0, dxc_ref[...])
        dxl = cw_ref[3:4, :] * dxc
        for kk in range(LRU_CONV - 1):
            dxl = dxl + cw_ref[kk:kk + 1, :] * _shift_rows_up(dxc, LRU_CONV - 1 - kk, later)
        dxl_ref[...] = dxl.astype(BF16)
        dxc_ref[...] = dxc[0:8, :]
        dcw_rows = [jnp.sum(dxc * _shift_rows(xl_t, LRU_CONV - 1 - kk, halo), axis=0, keepdims=True)
                    for kk in range(LRU_CONV - 1)]
        dcw_rows.append(jnp.sum(dxc * xl_t, axis=0, keepdims=True))
        dcw_part = jnp.concatenate(dcw_rows + [jnp.zeros((8 - LRU_CONV, LRU_WIDTH), F32)], axis=0)
        lamv = lam_ref[...]
        dlam_part = dsp_part * (-_sigmoid(-lamv))
        parts = ((dcw_ref, dcw_part), (dcb_ref, jnp.sum(dxc, axis=0, keepdims=True)),
                 (dwa_ref, dwa_part), (dba_ref, jnp.sum(dpa, axis=0, keepdims=True)),
                 (dwx_ref, dwx_part), (dbx_ref, jnp.sum(dpx, axis=0, keepdims=True)),
                 (dlam_ref, dlam_part))
        start = jnp.logical_and(b == 0, t == 0)

        @pl.when(start)
        def _():
            for ref, val in parts:
                ref[...] = val

        @pl.when(jnp.logical_not(start))
        def _():
            for ref, val in parts:
                ref[...] += val

    acc = lambda r: pl.BlockSpec((r, LRU_WIDTH), lambda b, t: (0, 0))
    return pl.pallas_call(
        body, name=name, grid=(T // S, nt),
        in_specs=[tile(0), prev8(0), tile(1), tile(0), prev8(0), tile(0),
                  vec(LRU_CONV), vec(1), mat, vec(1), mat, vec(1), vec(1)],
        out_specs=[tile(0), tile(0), acc(8), acc(1), mat, acc(1), mat, acc(1), acc(1)],
        out_shape=[jax.ShapeDtypeStruct((T, LRU_WIDTH), BF16), jax.ShapeDtypeStruct((T, LRU_WIDTH), BF16),
                   jax.ShapeDtypeStruct((8, LRU_WIDTH), F32), jax.ShapeDtypeStruct((1, LRU_WIDTH), F32),
                   jax.ShapeDtypeStruct((LRU_WIDTH, LRU_WIDTH), F32), jax.ShapeDtypeStruct((1, LRU_WIDTH), F32),
                   jax.ShapeDtypeStruct((LRU_WIDTH, LRU_WIDTH), F32), jax.ShapeDtypeStruct((1, LRU_WIDTH), F32),
                   jax.ShapeDtypeStruct((1, LRU_WIDTH), F32)],
        scratch_shapes=[pltpu.VMEM((8, LRU_WIDTH), F32), pltpu.VMEM((8, LRU_WIDTH), F32),
                        pltpu.VMEM((8, LRU_WIDTH), F32)],
        compiler_params=_cparams(("arbitrary", "arbitrary")),
    )(z, z, z, h, h, dy, cw, cb, wa, ba, wx, bx, lam)


FFN_CT = 1408
FFN_TILE = 512


def _ffn_conv(g, halo, cw, cb):
    gc = cb + cw[2:3, :] * g
    for kk in range(FFN_CONV - 1):
        gc = gc + cw[kk:kk + 1, :] * _shift_rows(g, FFN_CONV - 1 - kk, halo)
    return gc


def _row_chunks(rows, chunk):
    return [slice(r0, min(r0 + chunk, rows)) for r0 in range(0, rows, chunk)]


FFN_CHUNK = 128
HALO = 16


def _ffn_act_down(g, u, cw, cb, w_down, res, *, S, name, ride=None):
    T, F = g.shape
    D = w_down.shape[1]
    tt = min(FFN_TILE, S)
    nt = S // tt
    tc = _tile(F, FFN_CT)
    nj = F // tc

    def body(g_ref, halo_ref, u_ref, cw_ref, cb_ref, w_ref, r_ref, o_ref, act_ref):
        j = pl.program_id(1)
        first = (pl.program_id(0) % nt) == 0
        cw, cb = cw_ref[...], cb_ref[...]

        @pl.when(j == 0)
        def _():
            o_ref[...] = r_ref[...]

        for r in _row_chunks(tt, FFN_CHUNK):
            before = halo_ref[...] if r.start == 0 else g_ref[r.start - HALO:r.start, :]
            halo = before.astype(F32)[HALO - 8:]
            if r.start == 0:
                halo = jnp.where(first, 0.0, halo)
            gc = _ffn_conv(g_ref[r, :].astype(F32), halo, cw, cb)
            act = (_gelu(gc) * u_ref[r, :].astype(F32)).astype(BF16)
            act_ref[r, :] = act
            o_ref[r, :] += _dot_nn(act, w_ref[...])

    tile = pl.BlockSpec((tt, tc), lambda i, j: (i, j))
    prev = pl.BlockSpec((HALO, tc), lambda i, j: (jnp.maximum(i * (tt // HALO) - 1, 0), j))
    rows = pl.BlockSpec((tt, D), lambda i, j: (i, 0))
    return _pcall(
        body, name=name, grid=(T // tt, nj),
        in_specs=[tile, prev, tile, pl.BlockSpec((FFN_CONV, tc), lambda i, j: (0, j)),
                  pl.BlockSpec((1, tc), lambda i, j: (0, j)), pl.BlockSpec((tc, D), lambda i, j: (j, 0)), rows],
        out_specs=[rows, tile], out_shape=[jax.ShapeDtypeStruct((T, D), F32), jax.ShapeDtypeStruct((T, F), BF16)],
        args=(g, g, u, cw, cb, w_down, res), sem=("parallel", "arbitrary"), ride=ride)


def _ffn_act_bwd(g, u, dh, w_down, cw, cb, *, S, name, ride=None):
    T, F = g.shape
    D = w_down.shape[1]
    tt = min(FFN_TILE, S)
    nt = S // tt
    ntt = T // tt
    tc = _tile(F, FFN_CT)

    def body(g_ref, halo_ref, u_ref, dh_ref, w_ref, cw_ref, cb_ref, dg_ref, du_ref, dcw_ref, dcb_ref, later_ref):
        step = pl.program_id(1)
        ti = (ntt - 1 - step) % nt
        cw, cb = cw_ref[...], cb_ref[...]

        @pl.when(step == 0)
        def _():
            dcw_ref[...] = jnp.zeros_like(dcw_ref)
            dcb_ref[...] = jnp.zeros_like(dcb_ref)

        halo = jnp.where(ti == 0, 0.0, halo_ref[...].astype(F32)[HALO - 8:])
        gt = g_ref[...].astype(F32)
        gl, dgl = _gelu_and_grad(_ffn_conv(gt, halo, cw, cb))
        da = _dot_nt(dh_ref[...], w_ref[...])
        du_ref[...] = (da * gl).astype(BF16)
        dgc = da * u_ref[...].astype(F32) * dgl
        later = jnp.where(ti == nt - 1, 0.0, later_ref[...])
        dg = cw[2:3, :] * dgc
        for kk in range(FFN_CONV - 1):
            dg = dg + cw[kk:kk + 1, :] * _shift_rows_up(dgc, FFN_CONV - 1 - kk, later)
        dg_ref[...] = dg.astype(BF16)
        later_ref[...] = dgc[0:8, :]
        rows = [jnp.sum(dgc * _shift_rows(gt, FFN_CONV - 1 - kk, halo), axis=0, keepdims=True)
                for kk in range(FFN_CONV - 1)]
        rows.append(jnp.sum(dgc * gt, axis=0, keepdims=True))
        dcw_ref[...] += jnp.concatenate(rows + [jnp.zeros((8 - FFN_CONV, tc), F32)], axis=0)
        dcb_ref[...] += jnp.sum(dgc, axis=0, keepdims=True)

    tile = pl.BlockSpec((tt, tc), lambda j, s: (ntt - 1 - s, j))
    prev = pl.BlockSpec((HALO, tc), lambda j, s: (jnp.maximum((ntt - 1 - s) * (tt // HALO) - 1, 0), j))
    return _pcall(
        body, name=name, grid=(F // tc, ntt),
        in_specs=[tile, prev, tile, pl.BlockSpec((tt, D), lambda j, s: (ntt - 1 - s, 0)),
                  pl.BlockSpec((tc, D), lambda j, s: (j, 0)), pl.BlockSpec((FFN_CONV, tc), lambda j, s: (0, j)),
                  pl.BlockSpec((1, tc), lambda j, s: (0, j))],
        out_specs=[tile, tile, pl.BlockSpec((8, tc), lambda j, s: (0, j)), pl.BlockSpec((1, tc), lambda j, s: (0, j))],
        out_shape=[jax.ShapeDtypeStruct((T, F), BF16), jax.ShapeDtypeStruct((T, F), BF16),
                   jax.ShapeDtypeStruct((8, F), F32), jax.ShapeDtypeStruct((1, F), F32)],
        args=(g, g, u, dh, w_down, cw, cb), scratch=[pltpu.VMEM((8, tc), F32)], ride=ride)


def _sgu_norm(zv, g_ref, b_ref):
    v = _gelu(zv)
    mu = jnp.mean(v, axis=-1, keepdims=True)
    xc = v - mu
    rstd = lax.rsqrt(jnp.mean(xc * xc, axis=-1, keepdims=True) + NORM_EPS)
    xhat = xc * rstd
    return xhat, rstd, xhat * g_ref[...] + b_ref[...]


def _sgu_fwd(zc, ln_g, ln_b, wm, bmap, *, name):
    T = zc.shape[0]
    W = SGU_WIDTH
    tt = ROW_TILE
    nch = tt // CHUNK

    def body(z_ref, g_ref, b_ref, wm_ref, bm_ref, p_ref):
        u = _gelu(z_ref[:, :W])
        _, _, vn = _sgu_norm(z_ref[:, W:], g_ref, b_ref)
        vn = vn.astype(BF16)
        for n in range(nch):
            rows = slice(n * CHUNK, (n + 1) * CHUNK)
            for gi in range(SGU_GROUPS):
                cols = slice(gi * LANES, (gi + 1) * LANES)
                s = _dot_nn(wm_ref[gi], vn[rows, cols]) + bm_ref[:, cols]
                p_ref[rows, cols] = (u[rows, cols] * s).astype(BF16)

    const2 = lambda r, c: pl.BlockSpec((r, c), lambda i: (0, 0))
    return pl.pallas_call(
        body, name=name, grid=(T // tt,),
        in_specs=[pl.BlockSpec((tt, 2 * W), lambda i: (i, 0)), const2(1, W), const2(1, W),
                  pl.BlockSpec((SGU_GROUPS, CHUNK, CHUNK), lambda i: (0, 0, 0)), const2(CHUNK, W)],
        out_specs=pl.BlockSpec((tt, W), lambda i: (i, 0)),
        out_shape=jax.ShapeDtypeStruct((T, W), BF16),
        compiler_params=_cparams(("parallel",)),
    )(zc, ln_g, ln_b, wm, bmap)


def _sgu_bwd(zc, dp, ln_g, ln_b, wm, bmap, *, name, ride=None):
    T = zc.shape[0]
    W = SGU_WIDTH
    tt = ROW_TILE
    nch = tt // CHUNK
    nsteps = T // tt

    def body(z_ref, dp_ref, g_ref, b_ref, wm_ref, bm_ref, dz_ref, dg_ref, db_ref, dwm_ref, dbm_ref,
             s_scr, dvn_scr):
        step = pl.program_id(0)
        zu = z_ref[:, :W]
        zv = z_ref[:, W:]
        u, dgu = _gelu_and_grad(zu)
        xhat, rstd, vn = _sgu_norm(zv, g_ref, b_ref)
        vnb = vn.astype(BF16)
        dpf = dp_ref[...].astype(F32)
        ds = dpf * u

        @pl.when(step == 0)
        def _():
            dwm_ref[...] = jnp.zeros_like(dwm_ref)
            dbm_ref[...] = jnp.zeros_like(dbm_ref)

        for n in range(nch):
            rows = slice(n * CHUNK, (n + 1) * CHUNK)
            for gi in range(SGU_GROUPS):
                cols = slice(gi * LANES, (gi + 1) * LANES)
                s_scr[rows, cols] = _dot_nn(wm_ref[gi], vnb[rows, cols]) + bm_ref[:, cols]
                dsb = ds[rows, cols]
                dvn_scr[rows, cols] = _dot_tn(wm_ref[gi], dsb)
                dwm_ref[gi] += _dot_nt(dsb, vnb[rows, cols])
                dbm_ref[:, cols] += dsb
        dz_ref[:, :W] = (dpf * s_scr[...] * dgu).astype(BF16)
        dvn = dvn_scr[...]
        dxhat = dvn * g_ref[...]
        dv = rstd * (dxhat - jnp.mean(dxhat, axis=-1, keepdims=True)
                     - xhat * jnp.mean(dxhat * xhat, axis=-1, keepdims=True))
        _, dgv = _gelu_and_grad(zv)
        dz_ref[:, W:] = (dv * dgv).astype(BF16)
        dg_part = jnp.sum(dvn * xhat, axis=0, keepdims=True)
        db_part = jnp.sum(dvn, axis=0, keepdims=True)

        @pl.when(step == 0)
        def _():
            dg_ref[...] = dg_part
            db_ref[...] = db_part

        @pl.when(step > 0)
        def _():
            dg_ref[...] += dg_part
            db_ref[...] += db_part

        @pl.when(step == nsteps - 1)
        def _():
            for gi in range(SGU_GROUPS):
                cols = slice(gi * LANES, (gi + 1) * LANES)
                tot = jnp.sum(dbm_ref[:, cols], axis=1, keepdims=True)
                dbm_ref[:, cols] = jnp.broadcast_to(tot, (CHUNK, LANES))

    const2 = lambda r, c: pl.BlockSpec((r, c), lambda i: (0, 0))
    wspec = pl.BlockSpec((SGU_GROUPS, CHUNK, CHUNK), lambda i: (0, 0, 0))
    return _pcall(
        body, name=name, grid=(nsteps,),
        in_specs=[pl.BlockSpec((tt, 2 * W), lambda i: (i, 0)), pl.BlockSpec((tt, W), lambda i: (i, 0)),
                  const2(1, W), const2(1, W), wspec, const2(CHUNK, W)],
        out_specs=[pl.BlockSpec((tt, 2 * W), lambda i: (i, 0)), const2(1, W), const2(1, W), wspec, const2(CHUNK, W)],
        out_shape=[jax.ShapeDtypeStruct((T, 2 * W), BF16), jax.ShapeDtypeStruct((1, W), F32),
                   jax.ShapeDtypeStruct((1, W), F32), jax.ShapeDtypeStruct((SGU_GROUPS, CHUNK, CHUNK), F32),
                   jax.ShapeDtypeStruct((CHUNK, W), F32)],
        args=(zc, dp, ln_g, ln_b, wm, bmap), scratch=[pltpu.VMEM((tt, W), F32), pltpu.VMEM((tt, W), F32)], ride=ride)


def _rope_tables(positions):
    half = QK_ROPE // 2
    inv_freq = jnp.exp(-math.log(ROPE_BASE) * jnp.arange(half, dtype=F32) / half)
    ang = positions.reshape(-1).astype(F32)[:, None] * inv_freq
    cos = jnp.cos(ang)
    sin = jnp.sin(ang)
    n = ang.shape[0]
    tail = LANES - QK_NOPE - QK_ROPE
    cos_t = jnp.concatenate([jnp.ones((n, QK_NOPE), F32), cos, cos, jnp.ones((n, tail), F32)], axis=1)
    sin_t = jnp.concatenate([jnp.zeros((n, QK_NOPE), F32), -sin, sin, jnp.zeros((n, tail), F32)], axis=1)
    return cos_t, sin_t


SGU_GROUP_DIM = SGU_WIDTH // SGU_GROUPS
_O1, _O2, _O3, _O4 = Q_LORA, Q_LORA + KV_LORA, Q_LORA + KV_LORA + QK_ROPE, Q_LORA + KV_LORA + QK_ROPE + LRU_WIDTH
_A0, _A1, _A2 = 2 * LRU_WIDTH, 2 * LRU_WIDTH + Q_LORA, 2 * LRU_WIDTH + Q_LORA + KV_LORA
_A3 = _A2 + QK_NOPE
Z_Q_BLOCK, Z_KV_BLOCK, Z_KPE_BLOCK = _A0 // Q_LORA, _A1 // KV_LORA, _A2 // LANES


def _perm_w_in(w_in):
    zeros = lambda n: jnp.zeros((w_in.shape[0], n), w_in.dtype)
    return jnp.concatenate([w_in[:, _O3:_O4], w_in[:, _O4:], w_in[:, :_O1], w_in[:, _O1:_O2], zeros(QK_NOPE),
                            w_in[:, _O2:_O3], zeros(LANES - QK_NOPE - QK_ROPE)], axis=1)


def _unperm_w_in(w):
    return jnp.concatenate([w[:, _A0:_A1], w[:, _A1:_A2], w[:, _A3:_A3 + QK_ROPE], w[:, :LRU_WIDTH],
                            w[:, LRU_WIDTH:_A0]], axis=1)


def _head_blocks(w, d):
    r = w.shape[0]
    return jnp.pad(w.reshape(r, MLA_HEADS, d), ((0, 0), (0, 0), (0, LANES - d))).reshape(r, MLA_HEADS * LANES)


def _from_head_blocks(w, d):
    r = w.shape[0]
    return w.reshape(r, MLA_HEADS, LANES)[:, :, :d].reshape(r, MLA_HEADS * d)


def _split_kv(w_kv):
    r = w_kv.shape[0]
    w3 = w_kv.reshape(r, MLA_HEADS, QK_NOPE + V_HEAD)
    return _head_blocks(w3[:, :, :QK_NOPE].reshape(r, -1), QK_NOPE), w3[:, :, QK_NOPE:].reshape(r, -1)


def _join_kv(w_k, w_v):
    r = w_k.shape[0]
    return jnp.concatenate([_from_head_blocks(w_k, QK_NOPE).reshape(r, MLA_HEADS, QK_NOPE),
                            w_v.reshape(r, MLA_HEADS, V_HEAD)], axis=2).reshape(r, -1)


def _prep_small(w):
    p = {n: w[n] for n in w if n not in BIG}
    eye = jnp.eye(LRU_HEADS, dtype=F32)
    dense = lambda wg: (wg[:, :, None, :] * eye[:, None, :, None]).reshape(LRU_WIDTH, LRU_WIDTH).astype(BF16)
    p["wa_d"] = dense(w["ab_w_rg_a"][0])
    p["wx_d"] = dense(w["ab_w_rg_x"][0])
    causal = jnp.tril(jnp.ones((CHUNK, CHUNK), F32))
    p["wm"] = (w["c_w_s"][0] * causal).astype(BF16)
    p["bmap"] = jnp.repeat(w["c_b_s"][0].T, SGU_GROUP_DIM, axis=1)
    return p


def _prep_big(ab_w_in, ab_w_q_b, ab_w_kv_b):
    return {"w_in_p": _perm_w_in(ab_w_in).astype(BF16),
            "w_q_p": _head_blocks(ab_w_q_b, QK_NOPE + QK_ROPE).astype(BF16),
            "w_kv_p": jnp.concatenate(_split_kv(ab_w_kv_b), axis=1).astype(BF16)}


def _ffn_fwd(h, l, p, S, rides):
    hn = _rms_fwd(h, p["ffn_norm"][l], name=f"ffn{l}_norm")
    g = _mm(hn, p["ffn_gate_t"][l], tb=True, out_dtype=BF16, name=f"ffn{l}_gate", ride=rides.get(f"ffn{l}_gate"))
    u = _mm(hn, p["ffn_up_t"][l], tb=True, out_dtype=BF16, name=f"ffn{l}_up", ride=rides.get(f"ffn{l}_up"))
    out, act = _ffn_act_down(g, u, p["ffn_conv_w"][l], p["ffn_conv_b"][l][None], p["ffn_down"][l], h, S=S,
                             name=f"ffn{l}_down", ride=rides.get(f"ffn{l}_down"))
    return out, (hn, g, u, act)


def _ffn_bwd(dh, h_in, l, p, saved, S, rides, grads_ready, also_ready=None):
    hn, g, u, act = saved
    dw_down = _mm(act, dh, ta=True, out_dtype=BF16, name=f"ffn{l}_dwdown")
    dg, du, dcw, dcb = _ffn_act_bwd(g, u, dh, p["ffn_down"][l], p["ffn_conv_w"][l], p["ffn_conv_b"][l][None], S=S,
                                    name=f"ffn{l}_dactbwd", ride=rides.get(f"ffn{l}_dactbwd"))
    dhn = _mm(dg, p["ffn_gate_t"][l], name=f"ffn{l}_dhn_g")
    dhn = _mm(du, p["ffn_up_t"][l], res=dhn, out_dtype=BF16, name=f"ffn{l}_dhn_u")
    dw_gate_t = _mm(dg, hn, ta=True, out_dtype=BF16, name=f"ffn{l}_dwgate")
    dw_up_t = _mm(du, hn, ta=True, out_dtype=BF16, name=f"ffn{l}_dwup")
    grads_ready(l, {**(also_ready or {}), "ffn_gate_t": dw_gate_t, "ffn_up_t": dw_up_t, "ffn_down": dw_down})
    dh_in, dnorm = _rms_bwd(h_in, p["ffn_norm"][l], dhn, res=dh, name=f"ffn{l}_dnorm", ride=rides.get(f"ffn{l}_dnorm"))
    grads = dict(ffn_norm=dnorm[0], ffn_gate_t=dw_gate_t, ffn_up_t=dw_up_t, ffn_conv_w=dcw[:FFN_CONV],
                 ffn_conv_b=dcb[0], ffn_down=dw_down)
    return dh_in, grads


def _local_step(x, positions, target, p, rides=None, grads_ready=None):
    rides = {} if rides is None else rides
    grads_ready = grads_ready or (lambda layer, ready: None)
    B, S, D = x.shape
    T = B * S
    H = MLA_HEADS
    xf = x.reshape(T, D)
    tgt = target.reshape(T, D)
    cos, sin = _rope_tables(positions)

    hn0 = _rms_fwd(xf, p["ab_norm"][0], name="ab_norm", ride=rides.get("ab_norm"))
    z = _mm(hn0, p["w_in_p"], name="ab_in")
    cqn = _rms_fwd(z, p["ab_q_norm"][0], cb=Z_Q_BLOCK, name="q_norm")
    ckvn = _rms_fwd(z, p["ab_kv_norm"][0], cb=Z_KV_BLOCK, name="kv_norm")
    q = _mm(cqn, p["w_q_p"], name="q_up")
    kv = _mm(ckvn, p["w_kv_p"], out_dtype=BF16, name="kv_up")
    qs = _rope_q(q, cos, sin, name="q_rope")
    kk = _key_blocks(kv, z, cos, sin, kpe_block=Z_KPE_BLOCK, name="k_rope")
    att = dict(B=B, S=S, v_block0=H)
    o, lse = _attn_fwd(qs, kk, kv, name="attn_fwd", ride=rides.get("attn_fwd"), **att)
    lru_par = (p["ab_conv_w"][0], p["ab_conv_b"], p["wa_d"], p["ab_b_rg_a"], p["wx_d"], p["ab_b_rg_x"], p["ab_lambda"])
    y_lru, hs = _lru_fwd(z, *lru_par, S=S, name="lru_fwd", ride=rides.get("lru_fwd"))
    n_att = H * V_HEAD
    w_out_a, w_out_b = p["ab_w_out"][:n_att], p["ab_w_out"][n_att:]
    h1 = _mm(y_lru, w_out_b, res=_mm(o, w_out_a, res=xf, name="ab_out_a"), name="ab_out_b")
    h2, ffn0 = _ffn_fwd(h1, 0, p, S, rides)

    hn2 = _rms_fwd(h2, p["c_norm"][0], name="c_norm")
    zc = _mm(hn2, p["c_w_in_t"], tb=True, name="c_in")
    pg = _sgu_fwd(zc, p["c_ln_g"], p["c_ln_b"], p["wm"], p["bmap"], name="sgu_fwd")
    h3 = _mm(pg, p["c_w_out"], res=h2, name="c_out")
    h4, ffn1 = _ffn_fwd(h3, 1, p, S, rides)

    loss_row, dh4, dfinal = _final_fwd_bwd(h4, p["final_norm"], tgt, name="final")

    dh3, g_ffn1 = _ffn_bwd(dh4, h3, 1, p, ffn1, S, rides, grads_ready)
    dpg = _mm(dh3, p["c_w_out"], tb=True, out_dtype=BF16, name="c_dp")
    dw_c_out = _mm(pg, dh3, ta=True, out_dtype=BF16, name="c_dwout")
    dzc, dlng, dlnb, dwm, dbm = _sgu_bwd(zc, dpg, p["c_ln_g"], p["c_ln_b"], p["wm"], p["bmap"], name="sgu_bwd",
                                         ride=rides.get("sgu_bwd"))
    dhn2 = _mm(dzc, p["c_w_in_t"], out_dtype=BF16, name="c_dhn")
    dw_c_in_t = _mm(dzc, hn2, ta=True, out_dtype=BF16, name="c_dwin")
    dh2, dcnorm = _rms_bwd(h2, p["c_norm"][0], dhn2, res=dh3, name="c_dnorm")
    dh1, g_ffn0 = _ffn_bwd(dh2, h1, 0, p, ffn0, S, rides, grads_ready, {"c_w_in_t": dw_c_in_t, "c_w_out": dw_c_out})

    do = _mm(dh1, w_out_a, tb=True, name="ab_do")
    dy_lru = _mm(dh1, w_out_b, tb=True, out_dtype=BF16, name="ab_dylru")
    dw_out = jnp.concatenate([_mm(o, dh1, ta=True, out_dtype=BF16, name="ab_dwout_a"),
                              _mm(y_lru, dh1, ta=True, out_dtype=BF16, name="ab_dwout_b")], axis=0)
    dq, dk, dv = _attn_bwd(qs, kk, kv, o, lse, do, name="attn_bwd", ride=rides.get("attn_bwd"), **att)
    dq_full = _rope_q_bwd(dq, cos, sin, name="q_rope_bwd")
    dkr = _key_rope_bwd(dk, cos, sin, name="k_rope_bwd")
    n_key = H * LANES
    w_k_p, w_v_p = p["w_kv_p"][:, :n_key], p["w_kv_p"][:, n_key:]
    dcqn = _mm(dq_full, p["w_q_p"], tb=True, name="q_dlat")
    dw_q_p = _mm(cqn, dq_full, ta=True, out_dtype=BF16, name="q_dw")
    dckvn = _mm(dv, w_v_p, tb=True, res=_mm(dk, w_k_p, tb=True, name="k_dlat"), name="v_dlat")
    dw_k_p = _mm(ckvn, dk, ta=True, out_dtype=BF16, name="k_dw")
    dw_v_p = _mm(ckvn, dv, ta=True, out_dtype=BF16, name="v_dw")
    dcq, dqnorm = _rms_bwd(z, p["ab_q_norm"][0], dcqn, cb=Z_Q_BLOCK, out_dtype=BF16, name="q_dnorm")
    dckv, dkvnorm = _rms_bwd(z, p["ab_kv_norm"][0], dckvn, cb=Z_KV_BLOCK, out_dtype=BF16, name="kv_dnorm")
    dxl, dgate, dcw, dcb, dwa, dba, dwx, dbx, dlam = _lru_bwd(z, hs, dy_lru, *lru_par, S=S, name="lru_bwd")
    dz = jnp.concatenate([dxl, dgate, dcq, dckv, dkr], axis=1)
    dhn0 = _mm(dz, p["w_in_p"], tb=True, out_dtype=BF16, name="ab_dhn")
    dw_in_p = _mm(hn0, dz, ta=True, out_dtype=BF16, name="ab_dwin")
    dx, dabnorm = _rms_bwd(xf, p["ab_norm"][0], dhn0, res=dh1, name="ab_dnorm")

    blocks = lambda dd: jnp.stack([dd[i * LRU_BLOCK:(i + 1) * LRU_BLOCK, i * LRU_BLOCK:(i + 1) * LRU_BLOCK]
                                   for i in range(LRU_HEADS)])
    causal = jnp.tril(jnp.ones((CHUNK, CHUNK), F32))
    grads = {
        "ab_norm": dabnorm, "w_in_p": dw_in_p, "ab_q_norm": dqnorm, "w_q_p": dw_q_p,
        "ab_kv_norm": dkvnorm, "w_k_p": dw_k_p, "w_v_p": dw_v_p, "ab_conv_w": dcw[:LRU_CONV][None], "ab_conv_b": dcb,
        "ab_w_rg_a": blocks(dwa)[None], "ab_b_rg_a": dba, "ab_w_rg_x": blocks(dwx)[None], "ab_b_rg_x": dbx,
        "ab_lambda": dlam, "ab_w_out": dw_out,
        "c_norm": dcnorm, "c_w_in_t": dw_c_in_t, "c_ln_g": dlng, "c_ln_b": dlnb,
        "c_w_s": (dwm * causal)[None], "c_b_s": dbm[:, ::SGU_GROUP_DIM].T[None], "c_w_out": dw_c_out,
        "final_norm": dfinal[0],
    }
    for name in ("ffn_norm", "ffn_conv_w", "ffn_conv_b"):
        grads[name] = jnp.stack([g_ffn0[name], g_ffn1[name]])
    for name in ("ffn_gate_t", "ffn_up_t", "ffn_down"):
        grads[name] = [g_ffn0[name], g_ffn1[name]]
    return loss_row, dx.reshape(B, S, D), grads


ANY = pl.BlockSpec(memory_space=pl.ANY)


def _place():
    x, y, c = lax.axis_index("x"), lax.axis_index("y"), lax.axis_index("c")
    chips = [(1 - x, y), (x, 1 - y), (1 - x, 1 - y)]
    return x, y, c, 2 * x + y, (x, y, 1 - c), chips


def _remote(src, dst, send_sems, recv_sems, k, to):
    return pltpu.make_async_remote_copy(src_ref=src, dst_ref=dst, send_sem=send_sems.at[k], recv_sem=recv_sems.at[k],
                                        device_id=to, device_id_type=MESH)


class _Exchange:
    def __init__(self, arrs, out_shapes, n_sems, start, finish):
        self.arrs, self.out_shapes, self.n_sems, self.start, self.finish = list(arrs), out_shapes, n_sems, start, finish

    @property
    def in_specs(self):
        return [ANY] * len(self.arrs)

    @property
    def out_specs(self):
        return [ANY] * len(self.out_shapes)

    @property
    def scratch(self):
        return [pltpu.SemaphoreType.DMA((self.n_sems,)), pltpu.SemaphoreType.DMA((self.n_sems,))]

    def split(self, refs):
        n = len(self.arrs)
        return refs[:n], refs[n:n + len(self.out_shapes)], refs[-2], refs[-1]

    def run(self, name):
        def body(*refs):
            parts = self.split(refs)
            self.start(*parts)
            self.finish(*parts)

        return pl.pallas_call(body, name=name, in_specs=self.in_specs, out_specs=self.out_specs,
                              out_shape=self.out_shapes, scratch_shapes=self.scratch)(*self.arrs)


def _put(buf, piece, idx, axis):
    return lax.dynamic_update_slice_in_dim(buf, jnp.expand_dims(piece, axis).astype(buf.dtype), idx, axis)


def _all_gather(arrs):
    n = len(arrs)

    def start(ins, outs, send_sems, recv_sems):
        x, y, c, j, sib, chips = _place()
        for i in range(n):
            for k, (cx, cy) in enumerate(chips):
                _remote(ins[i].at[:, c], outs[i].at[:, j, c], send_sems, recv_sems, 6 * i + k, (cx, cy, c)).start()

    def finish(ins, outs, send_sems, recv_sems):
        x, y, c, j, sib, chips = _place()
        passed = []
        for i in range(n):
            for k, (cx, cy) in enumerate(chips):
                got = outs[i].at[:, 2 * cx + cy, c]
                _remote(got, got, send_sems, recv_sems, 6 * i + k, (cx, cy, c)).wait_recv()
                cp = _remote(got, got, send_sems, recv_sems, 6 * i + 3 + k, sib)
                cp.start()
                passed.append(cp)
        for i in range(n):
            for k, (cx, cy) in enumerate(chips):
                got = outs[i].at[:, 2 * cx + cy, 1 - c]
                _remote(got, got, send_sems, recv_sems, 6 * i + 3 + k, sib).wait_recv()
                _remote(ins[i].at[:, c], ins[i].at[:, c], send_sems, recv_sems, 6 * i + k, sib).wait_send()
        for cp in passed:
            cp.wait_send()

    shapes = [jax.ShapeDtypeStruct((a.shape[0], N_CHIPS) + a.shape[1:], a.dtype) for a in arrs]
    return _Exchange(arrs, shapes, 6 * n, start, finish)


class _Offset:
    def __init__(self, sems, k0):
        self.sems, self.k0 = sems, k0

    @property
    def at(self):
        return self

    def __getitem__(self, k):
        return self.sems.at[self.k0 + k]


def _merge(a, b):
    n_in, n_out = len(a.arrs), len(a.out_shapes)

    def both(fa, fb):
        def f(ins, outs, send_sems, recv_sems):
            fa(ins[:n_in], outs[:n_out], send_sems, recv_sems)
            fb(ins[n_in:], outs[n_out:], _Offset(send_sems, a.n_sems), _Offset(recv_sems, a.n_sems))
        return f

    return _Exchange(a.arrs + b.arrs, a.out_shapes + b.out_shapes, a.n_sems + b.n_sems,
                     both(a.start, b.start), both(a.finish, b.finish))


def _pair_swap(arrs):
    n = len(arrs)

    def start(ins, outs, send_sems, recv_sems):
        x, y, c, j, sib, chips = _place()
        for i in range(n):
            _remote(ins[i].at[:, 1 - c], outs[i], send_sems, recv_sems, i, sib).start()

    def finish(ins, outs, send_sems, recv_sems):
        x, y, c, j, sib, chips = _place()
        for i in range(n):
            _remote(ins[i].at[:, 1 - c], outs[i], send_sems, recv_sems, i, sib).wait()

    shapes = [jax.ShapeDtypeStruct((a.shape[0],) + a.shape[2:], a.dtype) for a in arrs]
    return _Exchange(arrs, shapes, n, start, finish)


def _pair_send(arrs):
    n = len(arrs)

    def start(ins, outs, send_sems, recv_sems):
        x, y, c, j, sib, chips = _place()
        for i in range(n):
            _remote(ins[i], outs[i], send_sems, recv_sems, i, sib).start()

    def finish(ins, outs, send_sems, recv_sems):
        x, y, c, j, sib, chips = _place()
        for i in range(n):
            _remote(ins[i], outs[i], send_sems, recv_sems, i, sib).wait()

    shapes = [jax.ShapeDtypeStruct(a.shape, a.dtype) for a in arrs]
    return _Exchange(arrs, shapes, n, start, finish)


def _chip_exchange(arrs, *, scatter):
    n = len(arrs)

    def copies(ins, outs, send_sems, recv_sems):
        x, y, c, j, sib, chips = _place()
        return [(_remote(ins[i].at[2 * cx + cy] if scatter else ins[i], outs[i].at[j], send_sems, recv_sems,
                         3 * i + k, (cx, cy, c)),
                 _remote(outs[i].at[2 * cx + cy], outs[i].at[2 * cx + cy], send_sems, recv_sems, 3 * i + k, (cx, cy, c)))
                for i in range(n) for k, (cx, cy) in enumerate(chips)]

    def start(*refs):
        for out, _ in copies(*refs):
            out.start()

    def finish(*refs):
        for out, back in copies(*refs):
            back.wait_recv()
            out.wait_send()

    shapes = [jax.ShapeDtypeStruct((N_CHIPS,) + a.shape[-2:], a.dtype) for a in arrs]
    return _Exchange(arrs, shapes, 3 * n, start, finish)


FLAT_ROWS = 512


def _add2(a, b, *, out_dtype, name):
    n, R, L = a.shape
    tr = _tile(R, FLAT_ROWS, 16)

    def body(a_ref, b_ref, o_ref):
        o_ref[...] = (a_ref[...].astype(F32) + b_ref[...].astype(F32)).astype(out_dtype)

    spec = pl.BlockSpec((n, tr, L), lambda i: (0, i, 0))
    return pl.pallas_call(
        body, name=name, grid=(R // tr,), in_specs=[spec, spec], out_specs=spec,
        out_shape=jax.ShapeDtypeStruct(a.shape, out_dtype), compiler_params=_cparams(("parallel",)),
    )(a, b)


def _sum_slots(buf, *, name):
    n, R, L = buf.shape
    tr = _tile(R, FLAT_ROWS, 16)

    def body(b_ref, o_ref):
        acc = b_ref[0].astype(F32)
        for k in range(1, n):
            acc = acc + b_ref[k].astype(F32)
        o_ref[...] = acc

    return pl.pallas_call(
        body, name=name, grid=(R // tr,), in_specs=[pl.BlockSpec((n, tr, L), lambda i: (0, i, 0))],
        out_specs=pl.BlockSpec((tr, L), lambda i: (i, 0)),
        out_shape=jax.ShapeDtypeStruct((R, L), F32), compiler_params=_cparams(("parallel",)),
    )(buf)


def _adamw_update(w, g, m, v):
    c1 = 1.0 - ADAM_B1 ** ADAM_STEP
    c2 = 1.0 - ADAM_B2 ** ADAM_STEP
    m = ADAM_B1 * m + (1.0 - ADAM_B1) * g
    v = ADAM_B2 * v + (1.0 - ADAM_B2) * (g * g)
    return -ADAM_LR * ((m / c1) / (jnp.sqrt(v / c2) + ADAM_EPS) + ADAM_WD * w), m, v


def _adamw_halves(w, m, v, own, other, *, name):
    NL, R, L = w.shape
    h = R // 2
    tr = _tile(h, FLAT_ROWS, 16)
    nt = h // tr

    def body(*refs):
        w_ref, m_ref, v_ref = refs[:3]
        own_refs, other_refs = refs[3:3 + NL], refs[3 + NL:3 + 2 * NL]
        d_ref, nm_ref, nv_ref, g_ref = refs[3 + 2 * NL:]
        layer, half = pl.program_id(0), pl.program_id(1)
        mine = half == lax.axis_index("c")
        g = jnp.where(mine, own_refs[0][...], other_refs[0][...])
        for l in range(1, NL):
            g = jnp.where(layer == l, jnp.where(mine, own_refs[l][...], other_refs[l][...]), g)
        d, mm, vv = _adamw_update(w_ref[0], g, m_ref[0], v_ref[0])
        d_ref[0], nm_ref[0], nv_ref[0], g_ref[0] = d, mm, vv, g

    spec = pl.BlockSpec((1, tr, L), lambda l, hh, i: (l, hh * nt + i, 0))
    part = pl.BlockSpec((tr, L), lambda l, hh, i: (i, 0))
    sh = jax.ShapeDtypeStruct((NL, R, L), F32)
    return pl.pallas_call(
        body, name=name, grid=(NL, 2, nt), in_specs=[spec] * 3 + [part] * (2 * NL), out_specs=[spec] * 4,
        out_shape=[sh] * 4, compiler_params=_cparams(("parallel", "parallel", "parallel")),
    )(w, m, v, *own, *other)


def _adamw(w, g, m, v, *, name):
    NL, R, L = w.shape
    tr = _tile(R, FLAT_ROWS, 16)

    def body(w_ref, g_ref, m_ref, v_ref, d_ref, nm_ref, nv_ref):
        d_ref[...], nm_ref[...], nv_ref[...] = _adamw_update(w_ref[...], g_ref[...], m_ref[...], v_ref[...])

    spec = pl.BlockSpec((1, tr, L), lambda l, i: (l, i, 0))
    sh = jax.ShapeDtypeStruct((NL, R, L), F32)
    return pl.pallas_call(
        body, name=name, grid=(NL, R // tr), in_specs=[spec] * 4, out_specs=[spec] * 3, out_shape=[sh] * 3,
        compiler_params=_cparams(("parallel", "parallel")),
    )(w, g, m, v)


WEIGHT_NAMES = ["ab_norm", "ab_w_in", "ab_q_norm", "ab_w_q_b", "ab_kv_norm", "ab_w_kv_b", "ab_conv_w", "ab_conv_b",
                "ab_w_rg_a", "ab_b_rg_a", "ab_w_rg_x", "ab_b_rg_x", "ab_lambda", "ab_w_out", "c_norm", "c_w_in",
                "c_ln_g", "c_ln_b", "c_w_s", "c_b_s", "c_w_out", "ffn_norm", "ffn_w_gate", "ffn_w_up", "ffn_conv_w",
                "ffn_conv_b", "ffn_w_down", "final_norm"]
BIG = {"ab_w_in": 2, "ab_w_q_b": 2, "ab_w_kv_b": 2, "ab_w_out": 1, "c_w_in": 2, "c_w_out": 1,
       "ffn_w_gate": 2, "ffn_w_up": 2, "ffn_w_down": 1}
SMALL_SHARDED = {"ab_conv_w": 2, "c_norm": 1, "c_ln_g": 1, "c_ln_b": 1, "ffn_conv_w": 2}
SMALL_REPLICATED = [n for n in WEIGHT_NAMES if n not in BIG and n not in SMALL_SHARDED]


def _rows(n_elems, mult):
    r = -(-n_elems // LANES)
    return -(-r // mult) * mult


def _flat(parts, rows):
    flat = jnp.concatenate([a.reshape(-1) for a in parts])
    return jnp.pad(flat, (0, rows * LANES - flat.shape[0])).reshape(rows, LANES)


def _unflat(flat, shapes):
    flat = flat.reshape(-1)
    out, off = [], 0
    for s in shapes:
        n = math.prod(s)
        out.append(flat[off:off + n].reshape(s))
        off += n
    return out


def _join_shards(a, axis):
    a = jnp.moveaxis(a, 0, axis)
    return a.reshape(a.shape[:axis] + (a.shape[axis] * a.shape[axis + 1],) + a.shape[axis + 2:])


def kernel(x, positions, ab_norm, ab_w_in, ab_q_norm, ab_w_q_b, ab_kv_norm, ab_w_kv_b, ab_conv_w, ab_conv_b, ab_w_rg_a, ab_b_rg_a, ab_w_rg_x, ab_b_rg_x, ab_lambda, ab_w_out, c_norm, c_w_in, c_ln_g, c_ln_b, c_w_s, c_b_s, c_w_out, ffn_norm, ffn_w_gate, ffn_w_up, ffn_conv_w, ffn_conv_b, ffn_w_down, final_norm, loss_target, m_ab_norm, m_ab_w_in, m_ab_q_norm, m_ab_w_q_b, m_ab_kv_norm, m_ab_w_kv_b, m_ab_conv_w, m_ab_conv_b, m_ab_w_rg_a, m_ab_b_rg_a, m_ab_w_rg_x, m_ab_b_rg_x, m_ab_lambda, m_ab_w_out, m_c_norm, m_c_w_in, m_c_ln_g, m_c_ln_b, m_c_w_s, m_c_b_s, m_c_w_out, m_ffn_norm, m_ffn_w_gate, m_ffn_w_up, m_ffn_conv_w, m_ffn_conv_b, m_ffn_w_down, m_final_norm, v_ab_norm, v_ab_w_in, v_ab_q_norm, v_ab_w_q_b, v_ab_kv_norm, v_ab_w_kv_b, v_ab_conv_w, v_ab_conv_b, v_ab_w_rg_a, v_ab_b_rg_a, v_ab_w_rg_x, v_ab_b_rg_x, v_ab_lambda, v_ab_w_out, v_c_norm, v_c_w_in, v_c_ln_g, v_c_ln_b, v_c_w_s, v_c_b_s, v_c_w_out, v_ffn_norm, v_ffn_w_gate, v_ffn_w_up, v_ffn_conv_w, v_ffn_conv_b, v_ffn_w_down, v_final_norm):
    given = dict(locals())
    w = {n: given[n] for n in WEIGHT_NAMES}
    m = {n: given["m_" + n] for n in WEIGHT_NAMES}
    v = {n: given["v_" + n] for n in WEIGHT_NAMES}
    c = lax.axis_index("c")
    chip = 2 * lax.axis_index("x") + lax.axis_index("y")

    halves = lambda a: a.reshape(a.shape[0], 2, a.shape[1] // 2, a.shape[2])
    tr = lambda a: jnp.swapaxes(a, 1, 2)
    send = {"ab_w_in": w["ab_w_in"], "ab_w_q_b": w["ab_w_q_b"], "ab_w_kv_b": w["ab_w_kv_b"], "ab_w_out": w["ab_w_out"],
            "c_w_in": tr(w["c_w_in"]), "c_w_out": w["c_w_out"], "ffn_w_gate": tr(w["ffn_w_gate"]),
            "ffn_w_up": tr(w["ffn_w_up"]), "ffn_w_down": w["ffn_w_down"]}
    small_rows = _rows(sum(w[n].size for n in SMALL_SHARDED), 16)
    small_sh = _flat([w[n] for n in SMALL_SHARDED], small_rows).reshape(1, 2, small_rows // 2, LANES)
    first_names = ["ab_w_in", "ab_w_q_b", "ab_w_kv_b", "ab_w_out"]
    mine = {n: halves(send[n].astype(BF16)) for n in BIG}

    def put_own(own, arrived):
        a = _put(arrived, own, chip, 1)
        return a.reshape(a.shape[0], -1, a.shape[-1])

    p = {"ab_norm": w["ab_norm"], "ffn_gate_t": {}, "ffn_up_t": {}, "ffn_down": {}}
    first = [mine[n] for n in first_names] + [small_sh]

    def first_arrived(got):
        full = {n: put_own(o, a) for n, o, a in zip(first_names + ["small"], first, got)}
        unshard = lambda a: jnp.swapaxes(a.reshape(N_CHIPS, -1, a.shape[-1]), 0, 1).reshape(-1, N_CHIPS * a.shape[-1])
        p.update(_prep_big(unshard(full["ab_w_in"][0]), unshard(full["ab_w_q_b"][0]), unshard(full["ab_w_kv_b"][0])))
        p["ab_w_out"] = full["ab_w_out"][0]
        small_full = dict(w)
        off = 0
        small_got = full["small"].reshape(N_CHIPS, -1)
        for n, ax in SMALL_SHARDED.items():
            seg = small_got[:, off:off + w[n].size].reshape((N_CHIPS,) + w[n].shape)
            small_full[n] = _join_shards(seg, ax)
            off += w[n].size
        p.update(_prep_small(small_full))

    def weights_ride(parts):
        def sink(arrived):
            for (own, setter), a in zip(parts, arrived):
                setter(put_own(own, a)[0])
        return _all_gather([own for own, _ in parts]), sink

    ffn_keys = {"ffn_gate_t": "ffn_w_gate", "ffn_up_t": "ffn_w_up", "ffn_down": "ffn_w_down"}
    ffn_part = lambda key, l: (mine[ffn_keys[key]][l:l + 1], functools.partial(p[key].__setitem__, l))
    rides = {
        "ab_norm": (_all_gather(first), first_arrived),
        "attn_fwd": weights_ride([ffn_part("ffn_gate_t", 0), ffn_part("ffn_up_t", 0)]),
        "lru_fwd": weights_ride([ffn_part("ffn_down", 0)]),
        "ffn0_gate": weights_ride([ffn_part("ffn_gate_t", 1)]),
        "ffn0_up": weights_ride([ffn_part("ffn_up_t", 1)]),
        "ffn0_down": weights_ride([ffn_part("ffn_down", 1),
                                   (mine["c_w_in"], functools.partial(p.__setitem__, "c_w_in_t")),
                                   (mine["c_w_out"], functools.partial(p.__setitem__, "c_w_out"))]),
    }

    def chip_sums(pair, arrived, tag):
        own = [lax.dynamic_index_in_dim(a, chip, axis=0, keepdims=False) for a in pair]
        return [_sum_slots(_put(a, o, chip, 0), name=f"grad_chip_sum_{tag}{i}") for i, (a, o) in enumerate(zip(arrived, own))]

    half_of = {}

    def grads_ready(layer, ready):
        if layer == 1:
            named = {"gate1": ready["ffn_gate_t"], "up1": ready["ffn_up_t"], "down1": ready["ffn_down"]}
            hosts = {"sgu_bwd": ["down1"], "ffn0_dactbwd": ["gate1", "up1"]}
        else:
            named = {"c_in": ready["c_w_in_t"], "c_out": ready["c_w_out"], "gate0": ready["ffn_gate_t"],
                     "up0": ready["ffn_up_t"], "down0": ready["ffn_down"]}
            hosts = {"attn_bwd": ["c_in", "c_out", "down0", "gate0", "up0"]}
        tag = f"f{layer}"
        sharded = [a.reshape(N_CHIPS, 2, -1, a.shape[-1]) for a in named.values()]

        def paired(from_sib):
            own = [lax.dynamic_index_in_dim(a, c, axis=1, keepdims=False) for a in sharded]
            pair = {k: _add2(a, b, out_dtype=BF16, name=f"grad_pair_add_{tag}{i}")
                    for i, (k, a, b) in enumerate(zip(named, own, from_sib))}
            for kernel_name, keys in hosts.items():
                def sink(arrived, keys=keys, kernel_name=kernel_name):
                    half_of.update(zip(keys, chip_sums([pair[k] for k in keys], arrived, f"{tag}_{kernel_name}")))
                rides[kernel_name] = (_chip_exchange([pair[k] for k in keys], scatter=True), sink)

        rides[f"ffn{layer}_dnorm"] = (_pair_swap(sharded), paired)

    loss_row, grad_x, g = _local_step(x, positions, loss_target, p, rides, grads_ready)

    cols = lambda a, n: jnp.swapaxes(a.reshape(a.shape[0], N_CHIPS, n), 0, 1)
    n_in, n_q, n_kv = w["ab_w_in"].shape[2], w["ab_w_q_b"].shape[2], w["ab_w_kv_b"].shape[2]
    small_names = SMALL_REPLICATED + list(SMALL_SHARDED)
    rs = _rows(sum(g[n].size for n in small_names) + LANES, FLAT_ROWS)
    small = _flat([loss_row] + [g[n] for n in small_names], rs)
    slot = (jnp.arange(2) == c)[:, None, None]
    last = [cols(_unperm_w_in(g["w_in_p"]), n_in), cols(_from_head_blocks(g["w_q_p"], QK_NOPE + QK_ROPE), n_q),
            cols(_join_kv(g["w_k_p"], g["w_v_p"]), n_kv), g["ab_w_out"]]
    last = [a.reshape(N_CHIPS, 2, -1, a.shape[-1]) for a in last]
    *from_sib, small_sib = _merge(_pair_swap(last), _pair_send([small])).run("tail_pair")
    own = [lax.dynamic_index_in_dim(a, c, axis=1, keepdims=False) for a in last]
    pair = [_add2(a, b, out_dtype=BF16, name=f"grad_pair_add_b{i}") for i, (a, b) in enumerate(zip(own, from_sib))]
    pair_small = _sum_slots(jnp.where(slot, small[None], small_sib[None]), name="small_pair_sum")
    my_small = lax.dynamic_index_in_dim(pair_small.reshape(2, rs // 2, LANES), c, axis=0, keepdims=False)
    *arrived, all_small = _merge(_chip_exchange(pair, scatter=True), _chip_exchange([my_small], scatter=False)).run("tail_chip")
    half_of.update(zip(["in", "q", "kv", "out"], chip_sums(pair, arrived, "b")))
    half_of["small"] = _sum_slots(_put(all_small, my_small, chip, 0), name="small_chip_sum")
    keys = ("in", "q", "kv", "out", "c_in", "c_out", "gate0", "gate1", "up0", "up1", "down0", "down1", "small")
    other_half = dict(zip(keys, _pair_send([half_of[k] for k in keys]).run("grad_pair_share")))
    small_sum = jnp.where(slot, half_of["small"][None], other_half["small"][None]).reshape(rs, LANES)
    whole = lambda k: jnp.where(slot, half_of[k][None], other_half[k][None]).reshape(-1, half_of[k].shape[-1])
    grads_t = {"ab_w_in": whole("in").T[None], "ab_w_q_b": whole("q").T[None]}
    grads = {"ab_w_kv_b": whole("kv")[None], "c_w_in": whole("c_in").T[None], **{n: tr(a) for n, a in grads_t.items()}}
    by_halves = {"ab_w_out": (("out",), False), "c_w_out": (("c_out",), False), "ffn_w_down": (("down0", "down1"), False),
                 "ffn_w_gate": (("gate0", "gate1"), True), "ffn_w_up": (("up0", "up1"), True)}

    small_parts = _unflat(small_sum, [(1, LANES)] + [g[n].shape for n in small_names])
    loss = small_parts[0][0, 0]
    for n, a in zip(small_names, small_parts[1:]):
        if n in SMALL_SHARDED:
            ax = SMALL_SHARDED[n]
            a = lax.dynamic_slice_in_dim(a, chip * w[n].shape[ax], w[n].shape[ax], axis=ax)
        grads[n] = a.reshape(w[n].shape)

    delta, new_m, new_v = {}, {}, {}
    for n in BIG:
        if n in by_halves:
            ks, transposed = by_halves[n]
            view = tr if transposed else (lambda a: a)
            out = _adamw_halves(view(w[n]), view(m[n]), view(v[n]), [half_of[k] for k in ks], [other_half[k] for k in ks],
                                name=f"adamw_{n}")
            delta[n], new_m[n], new_v[n], grads[n] = (view(a) for a in out)
        elif n in grads_t:
            out = _adamw(tr(w[n]), grads_t[n], tr(m[n]), tr(v[n]), name=f"adamw_{n}")
            delta[n], new_m[n], new_v[n] = (tr(a) for a in out)
        else:
            delta[n], new_m[n], new_v[n] = _adamw(w[n], grads[n], m[n], v[n], name=f"adamw_{n}")
    small_all = [n for n in WEIGHT_NAMES if n not in BIG]
    ra = _rows(sum(w[n].size for n in small_all), FLAT_ROWS)
    pack = lambda d: _flat([d[n] for n in small_all], ra)[None]
    out = _adamw(pack(w), pack(grads), pack(m), pack(v), name="adamw_small")
    shapes = [w[n].shape for n in small_all]
    for d, flat in zip((delta, new_m, new_v), out):
        d.update(zip(small_all, _unflat(flat, shapes)))
    return (loss, grad_x, *[grads[n] for n in WEIGHT_NAMES], *[delta[n] for n in WEIGHT_NAMES],
            *[new_m[n] for n in WEIGHT_NAMES], *[new_v[n] for n in WEIGHT_NAMES])
```

```python
import functools
import math

import jax
import jax.numpy as jnp
from jax import lax
from jax.experimental import pallas as pl
from jax.experimental.pallas import tpu as pltpu

F32 = jnp.float32
BF16 = jnp.bfloat16
MESH = pl.DeviceIdType.MESH

D_MODEL = 1024
MLA_HEADS = 8
Q_LORA = 256
KV_LORA = 128
QK_NOPE = 64
QK_ROPE = 32
V_HEAD = 64
LRU_WIDTH = 512
LRU_HEADS = 8
LRU_BLOCK = 64
LRU_CONV = 4
LRU_C = 8.0
CHUNK = 128
SGU_GROUPS = 8
SGU_WIDTH = 1024
D_FF = 2816
FFN_CONV = 3
NORM_EPS = 1e-6
ROPE_BASE = 10000.0
AB_IN_PAD = 1536
ADAM_LR = 0.001
ADAM_B1 = 0.9
ADAM_B2 = 0.999
ADAM_EPS = 1e-08
ADAM_WD = 0.01
ADAM_STEP = 10

N_CHIPS = 4
LANES = 128
VMEM_LIMIT = 56 * 1024 * 1024
ROW_TILE = 256
NORM_TILE = 1024
MM_TM, MM_TN, MM_TK = 1024, 1536, 2816
MM_TM_T, MM_TK_T = 1408, 1024
GELU_C = math.sqrt(2.0 / math.pi)


def _cparams(sem):
    return pltpu.CompilerParams(dimension_semantics=sem, vmem_limit_bytes=VMEM_LIMIT)


def _tile(n, target, mult=LANES):
    t = (min(n, target) // mult) * mult
    while t >= mult:
        if n % t == 0:
            return t
        t -= mult
    return n


GELU_K = GELU_C * 0.044715


def _gelu(x):
    t = jnp.tanh(x * (GELU_C + GELU_K * (x * x)))
    hx = 0.5 * x
    return hx + hx * t


def _gelu_and_grad(x):
    x2 = x * x
    t = jnp.tanh(x * (GELU_C + GELU_K * x2))
    hx = 0.5 * x
    dg = (0.5 + 0.5 * t) + (hx * (1.0 - t * t)) * (GELU_C + (3.0 * GELU_K) * x2)
    return hx + hx * t, dg


def _sigmoid(x):
    return 1.0 / (1.0 + jnp.exp(-x))


def _shift_rows(x, d, fill_rows):
    ext = jnp.concatenate([fill_rows, x], axis=0)
    return pltpu.roll(ext, d, 0)[8:]


def _shift_rows_up(x, d, fill_rows):
    n = x.shape[0]
    ext = jnp.concatenate([x, fill_rows], axis=0)
    return pltpu.roll(ext, n + 8 - d, 0)[:n]


def _dot(a, b, dims):
    return lax.dot_general(a.astype(BF16), b.astype(BF16), (dims, ((), ())), preferred_element_type=F32)


def _dot_nn(a, b):
    return _dot(a, b, ((1,), (0,)))


def _dot_nt(a, b):
    return _dot(a, b, ((1,), (1,)))


def _dot_tn(a, b):
    return _dot(a, b, ((0,), (0,)))


def _mm(a, b, *, name, ta=False, tb=False, res=None, out_dtype=F32, ride=None):
    if ta:
        K, M = a.shape
    else:
        M, K = a.shape
    N = b.shape[0] if tb else b.shape[1]
    tm = _tile(M, MM_TM_T if ta else (MM_TM if K <= MM_TM else MM_TM // 2), LANES if ta else 8)
    tn = _tile(N, MM_TN, LANES)
    tk = _tile(K, MM_TK_T if ta else MM_TK, LANES)
    nk = K // tk
    a_spec = pl.BlockSpec((tk, tm), lambda j, i, k: (k, i)) if ta else pl.BlockSpec((tm, tk), lambda j, i, k: (i, k))
    b_spec = pl.BlockSpec((tn, tk), lambda j, i, k: (j, k)) if tb else pl.BlockSpec((tk, tn), lambda j, i, k: (k, j))
    o_spec = pl.BlockSpec((tm, tn), lambda j, i, k: (i, j))
    dims = ((0,) if ta else (1,), (1,) if tb else (0,))
    has_res = res is not None

    def body(*refs):
        a_ref, b_ref = refs[:2]
        r_ref = refs[2] if has_res else None
        o_ref = refs[3] if has_res else refs[2]
        p = _dot(a_ref[...], b_ref[...], dims)

        def finish(r):
            if has_res:
                r = r + r_ref[...].astype(F32)
            o_ref[...] = r.astype(out_dtype)

        if nk == 1:
            finish(p)
            return
        acc_ref = refs[-1]
        k = pl.program_id(2)

        @pl.when(k == 0)
        def _():
            acc_ref[...] = p

        @pl.when(jnp.logical_and(k > 0, k < nk - 1))
        def _():
            acc_ref[...] += p

        @pl.when(k == nk - 1)
        def _():
            finish(acc_ref[...] + p)

    in_specs = [a_spec, b_spec] + ([o_spec] if has_res else [])
    args = (a, b) + ((res,) if has_res else ())
    return _pcall(
        body, name=name, grid=(N // tn, M // tm, nk), in_specs=in_specs, out_specs=[o_spec],
        out_shape=[jax.ShapeDtypeStruct((M, N), out_dtype)], args=args,
        scratch=[pltpu.VMEM((tm, tn), F32)] if nk > 1 else [], sem=("parallel", "parallel", "arbitrary"), ride=ride)[0]


def _rms_fwd(x, g, *, name, cb=0, out_dtype=BF16, ride=None):
    T = x.shape[0]
    W = g.shape[-1]
    g = g.reshape(1, W)
    tt = _tile(T, NORM_TILE, 16)

    def body(x_ref, g_ref, o_ref):
        xf = x_ref[...].astype(F32)
        rstd = lax.rsqrt(jnp.mean(xf * xf, axis=-1, keepdims=True) + NORM_EPS)
        o_ref[...] = (xf * rstd * g_ref[...]).astype(out_dtype)

    return _pcall(
        body, name=name, grid=(T // tt,),
        in_specs=[pl.BlockSpec((tt, W), lambda i: (i, cb)), pl.BlockSpec((1, W), lambda i: (0, 0))],
        out_specs=[pl.BlockSpec((tt, W), lambda i: (i, 0))], out_shape=[jax.ShapeDtypeStruct((T, W), out_dtype)],
        args=(x, g), sem=("parallel",), ride=ride)[0]


def _rms_bwd(x, g, dy, *, name, cb=0, res=None, out_dtype=F32, ride=None):
    T = x.shape[0]
    W = g.shape[-1]
    g = g.reshape(1, W)
    tt = _tile(T, NORM_TILE // 2, 16)
    has_res = res is not None

    def body(*refs):
        if has_res:
            x_ref, g_ref, dy_ref, r_ref, dx_ref, dg_ref = refs
        else:
            x_ref, g_ref, dy_ref, dx_ref, dg_ref = refs
        xf = x_ref[...].astype(F32)
        dyf = dy_ref[...].astype(F32)
        rstd = lax.rsqrt(jnp.mean(xf * xf, axis=-1, keepdims=True) + NORM_EPS)
        xhat = xf * rstd
        dxhat = dyf * g_ref[...]
        dx = rstd * (dxhat - xhat * jnp.mean(dxhat * xhat, axis=-1, keepdims=True))
        if has_res:
            dx = dx + r_ref[...].astype(F32)
        dx_ref[...] = dx.astype(out_dtype)
        part = jnp.sum(dyf * xhat, axis=0, keepdims=True)

        @pl.when(pl.program_id(0) == 0)
        def _():
            dg_ref[...] = part

        @pl.when(pl.program_id(0) > 0)
        def _():
            dg_ref[...] += part

    row = pl.BlockSpec((tt, W), lambda i: (i, 0))
    in_specs = [pl.BlockSpec((tt, W), lambda i: (i, cb)), pl.BlockSpec((1, W), lambda i: (0, 0)), row]
    args = (x, g, dy)
    if has_res:
        in_specs.append(row)
        args = args + (res,)
    return _pcall(
        body, name=name, grid=(T // tt,), in_specs=in_specs,
        out_specs=[row, pl.BlockSpec((1, W), lambda i: (0, 0))],
        out_shape=[jax.ShapeDtypeStruct((T, W), out_dtype), jax.ShapeDtypeStruct((1, W), F32)], args=args, ride=ride)


def _final_fwd_bwd(h, g, target, *, name):
    T, W = h.shape
    g = g.reshape(1, W)
    tt = _tile(T, NORM_TILE, 16)

    def body(x_ref, g_ref, t_ref, loss_ref, dx_ref, dg_ref):
        xf = x_ref[...]
        rstd = lax.rsqrt(jnp.mean(xf * xf, axis=-1, keepdims=True) + NORM_EPS)
        xhat = xf * rstd
        err = xhat * g_ref[...] - t_ref[...]
        lpart = jnp.zeros((1, LANES), F32) + (0.5 / W) * jnp.sum(err * err)
        dyf = err * (1.0 / W)
        dxhat = dyf * g_ref[...]
        dx_ref[...] = rstd * (dxhat - xhat * jnp.mean(dxhat * xhat, axis=-1, keepdims=True))
        part = jnp.sum(dyf * xhat, axis=0, keepdims=True)

        @pl.when(pl.program_id(0) == 0)
        def _():
            dg_ref[...] = part
            loss_ref[...] = lpart

        @pl.when(pl.program_id(0) > 0)
        def _():
            dg_ref[...] += part
            loss_ref[...] += lpart

    row = pl.BlockSpec((tt, W), lambda i: (i, 0))
    return pl.pallas_call(
        body, name=name, grid=(T // tt,),
        in_specs=[row, pl.BlockSpec((1, W), lambda i: (0, 0)), row],
        out_specs=[pl.BlockSpec((1, LANES), lambda i: (0, 0)), row, pl.BlockSpec((1, W), lambda i: (0, 0))],
        out_shape=[jax.ShapeDtypeStruct((1, LANES), F32), jax.ShapeDtypeStruct((T, W), F32),
                   jax.ShapeDtypeStruct((1, W), F32)],
        compiler_params=_cparams(("arbitrary",)),
    )(h, g, target)


def _swap16(x):
    lane = lax.broadcasted_iota(jnp.int32, x.shape, 1)
    return jnp.where((lane % 32) < 16, pltpu.roll(x, LANES - 16, 1), pltpu.roll(x, 16, 1))


def _rope(x, c, s):
    return x * c + _swap16(x) * s


def _rope_t(d, c, s):
    return d * c + _swap16(d * s)


def _head_block_map(fn, x, cos, sin, *, name):
    T, W = x.shape
    tt = _tile(T, NORM_TILE, 16)

    def body(x_ref, c_ref, s_ref, o_ref):
        c, s = c_ref[...], s_ref[...]
        for h in range(W // LANES):
            lanes = slice(h * LANES, (h + 1) * LANES)
            o_ref[:, lanes] = fn(x_ref[:, lanes], c, s).astype(BF16)

    tab = pl.BlockSpec((tt, LANES), lambda i: (i, 0))
    blk = pl.BlockSpec((tt, W), lambda i: (i, 0))
    return pl.pallas_call(
        body, name=name, grid=(T // tt,), in_specs=[blk, tab, tab], out_specs=blk,
        out_shape=jax.ShapeDtypeStruct((T, W), BF16), compiler_params=_cparams(("parallel",)),
    )(x, cos, sin)


def _rope_q(q, cos, sin, *, name):
    scale = _attn_scale()
    return _head_block_map(lambda x, c, s: _rope(x, c, s) * scale, q, cos, sin, name=name)


def _rope_q_bwd(dq, cos, sin, *, name):
    return _head_block_map(_rope_t, dq, cos, sin, name=name)


def _key_blocks(kv, z, cos, sin, *, kpe_block, name):
    T = kv.shape[0]
    tt = _tile(T, NORM_TILE, 16)
    W = MLA_HEADS * LANES

    def body(kv_ref, z_ref, c_ref, s_ref, o_ref):
        kr = _rope(z_ref[...], c_ref[...], s_ref[...])
        for h in range(MLA_HEADS):
            lanes = slice(h * LANES, (h + 1) * LANES)
            o_ref[:, lanes] = (kv_ref[:, lanes].astype(F32) + kr).astype(BF16)

    tab = pl.BlockSpec((tt, LANES), lambda i: (i, 0))
    blk = pl.BlockSpec((tt, W), lambda i: (i, 0))
    return pl.pallas_call(
        body, name=name, grid=(T // tt,),
        in_specs=[blk, pl.BlockSpec((tt, LANES), lambda i: (i, kpe_block)), tab, tab], out_specs=blk,
        out_shape=jax.ShapeDtypeStruct((T, W), BF16), compiler_params=_cparams(("parallel",)),
    )(kv, z, cos, sin)


def _key_rope_bwd(dk, cos, sin, *, name):
    T = dk.shape[0]
    tt = _tile(T, NORM_TILE, 16)

    def body(d_ref, c_ref, s_ref, o_ref):
        d = d_ref[:, :LANES]
        for h in range(1, MLA_HEADS):
            d = d + d_ref[:, h * LANES:(h + 1) * LANES]
        lane = lax.broadcasted_iota(jnp.int32, d.shape, 1)
        d = jnp.where(jnp.logical_and(lane >= QK_NOPE, lane < QK_NOPE + QK_ROPE), d, 0.0)
        o_ref[...] = _rope_t(d, c_ref[...], s_ref[...]).astype(BF16)

    tab = pl.BlockSpec((tt, LANES), lambda i: (i, 0))
    return pl.pallas_call(
        body, name=name, grid=(T // tt,),
        in_specs=[pl.BlockSpec((tt, MLA_HEADS * LANES), lambda i: (i, 0)), tab, tab], out_specs=tab,
        out_shape=jax.ShapeDtypeStruct((T, LANES), BF16), compiler_params=_cparams(("parallel",)),
    )(dk, cos, sin)


ATT_BLOCK = 512


def _attn_scale():
    return float((QK_NOPE + QK_ROPE) ** -0.5)


def _causal_mask(qi, kj, tq, tk):
    row = qi * tq + lax.broadcasted_iota(jnp.int32, (tq, tk), 0)
    col = kj * tk + lax.broadcasted_iota(jnp.int32, (tq, tk), 1)
    return col <= row


def _pcall(body, *, name, grid, in_specs, out_specs, out_shape, args, scratch=(), sem=None, ride=None):
    n_in, n_out, n_scr = len(args), len(out_shape), len(scratch)
    if ride is None:
        return pl.pallas_call(
            body, name=name, grid=grid, in_specs=list(in_specs), out_specs=list(out_specs), out_shape=list(out_shape),
            scratch_shapes=list(scratch), compiler_params=_cparams(sem or ("arbitrary",) * len(grid)))(*args)
    ex, sink = ride
    o0 = n_in + len(ex.arrs)
    s0 = o0 + n_out + len(ex.out_shapes)

    def hosted(*refs):
        parts = (refs[n_in:o0], refs[o0 + n_out:s0], refs[-2], refs[-1])
        ids = [pl.program_id(i) for i in range(len(grid))]
        pl.when(functools.reduce(jnp.logical_and, [i == 0 for i in ids]))(lambda: ex.start(*parts))
        body(*refs[:n_in], *refs[o0:o0 + n_out], *refs[s0:s0 + n_scr])
        pl.when(functools.reduce(jnp.logical_and, [i == n - 1 for i, n in zip(ids, grid)]))(lambda: ex.finish(*parts))

    outs = pl.pallas_call(
        hosted, name=name, grid=grid, in_specs=list(in_specs) + ex.in_specs, out_specs=list(out_specs) + ex.out_specs,
        out_shape=list(out_shape) + ex.out_shapes, scratch_shapes=list(scratch) + ex.scratch,
        compiler_params=_cparams(("arbitrary",) * len(grid)))(*args, *ex.arrs)
    sink(outs[n_out:])
    return outs[:n_out]


PAIRS = MLA_HEADS // 2


def _own_lanes(x, first):
    lane = lax.broadcasted_iota(jnp.int32, x.shape, 1)
    return jnp.where((lane < V_HEAD) if first else (lane >= V_HEAD), x, 0.0)


def _lane_sums_as_row(x):
    hi = x.astype(BF16)
    lo = (x - hi.astype(F32)).astype(BF16)
    ones = jnp.ones((8, LANES), BF16)
    return (_dot_nt(ones, hi) + _dot_nt(ones, lo))[0:1, :]


def _attn_fwd(q, k, kv, *, B, S, v_block0, name, ride=None):
    tq = tk = min(ATT_BLOCK, S)
    nq = S // tq
    T = B * S

    def body(q_ref, k_ref, v_ref, o_ref, lse_ref):
        qi = pl.program_id(2)
        qs = (q_ref[:, :LANES], q_ref[:, LANES:])

        def step(masked):
            def f(j, carry):
                rows = pl.ds(pl.multiple_of(j * tk, tk), tk)
                vb = v_ref[rows, :]
                out = []
                for h in range(2):
                    m, l, acc = carry[h]
                    s = _dot_nt(qs[h], k_ref[rows, h * LANES:(h + 1) * LANES])
                    if masked:
                        s = jnp.where(_causal_mask(qi, j, tq, tk), s, -jnp.inf)
                    m_new = jnp.maximum(m, jnp.max(s, axis=-1, keepdims=True))
                    alpha = jnp.exp(m - m_new)
                    p = jnp.exp(s - m_new)
                    out.append((m_new, alpha * l + jnp.sum(p, axis=-1, keepdims=True), alpha * acc + _dot_nn(p, vb)))
                return tuple(out)
            return f

        one = (jnp.full((tq, 1), -1e30, F32), jnp.zeros((tq, 1), F32), jnp.zeros((tq, LANES), F32))
        (ma, la, acca), (mb, lb, accb) = step(True)(qi, lax.fori_loop(0, qi, step(False), (one, one)))
        o_ref[...] = _own_lanes(acca / la, True) + _own_lanes(accb / lb, False)
        for h, lse in enumerate((ma + jnp.log(la), mb + jnp.log(lb))):
            lse_ref[0, h, pl.ds(qi, 1), :] = _lane_sums_as_row(jnp.broadcast_to(lse * (1.0 / LANES), (tq, LANES)))

    return _pcall(
        body, name=name, grid=(B, PAIRS, nq),
        in_specs=[pl.BlockSpec((tq, 2 * LANES), lambda b, g, i: (b * nq + i, g)),
                  pl.BlockSpec((S, 2 * LANES), lambda b, g, i: (b, g)),
                  pl.BlockSpec((S, LANES), lambda b, g, i: (b, v_block0 + g))],
        out_specs=[pl.BlockSpec((tq, LANES), lambda b, g, i: (b * nq + i, g)),
                   pl.BlockSpec((1, 2, nq, tq), lambda b, g, i: (b, g, 0, 0))],
        out_shape=[jax.ShapeDtypeStruct((T, PAIRS * LANES), F32), jax.ShapeDtypeStruct((B, MLA_HEADS, nq, tq), F32)],
        args=(q, k, kv), ride=ride)


def _attn_bwd(q, k, kv, o, lse_rows, do, *, B, S, v_block0, name, ride=None):
    tq = tk = min(ATT_BLOCK, S)
    nq = S // tq
    T = B * S
    scale = _attn_scale()

    def body(q_ref, k_ref, v_ref, o_ref, lse_ref, do_ref, dk_ref, dv_ref, dq_ref, delta_ref):
        kj = pl.program_id(2)
        ks = (k_ref[:, :LANES], k_ref[:, LANES:])
        vb = v_ref[...]

        @pl.when(kj == 0)
        def _():
            dq_ref[...] = jnp.zeros_like(dq_ref)
            for i in range(nq):
                prod = do_ref[i * tq:(i + 1) * tq, :] * o_ref[i * tq:(i + 1) * tq, :]
                for h in range(2):
                    delta_ref[h, i:i + 1, :] = _lane_sums_as_row(_own_lanes(prod, h == 0))

        def step(masked):
            def f(i, carry):
                rows = pl.ds(pl.multiple_of(i * tq, tq), tq)
                do_b = do_ref[rows, :]
                dks, dv = list(carry[:2]), carry[2]
                for h in range(2):
                    qb = q_ref[rows, h * LANES:(h + 1) * LANES]
                    doh = _own_lanes(do_b, h == 0)
                    pt = jnp.exp(_dot_nt(ks[h], qb) - lse_ref[0, h, pl.ds(i, 1), :])
                    if masked:
                        krow = kj * tk + lax.broadcasted_iota(jnp.int32, (tk, tq), 0)
                        qcol = i * tq + lax.broadcasted_iota(jnp.int32, (tk, tq), 1)
                        pt = jnp.where(krow <= qcol, pt, 0.0)
                    dst = pt * (_dot_nt(vb, doh) - delta_ref[h, pl.ds(i, 1), :])
                    dks[h] = dks[h] + _dot_nn(dst, qb)
                    dv = dv + _dot_nn(pt, doh)
                    dq_ref[rows, h * LANES:(h + 1) * LANES] += _dot_tn(dst, ks[h]) * scale
                return dks[0], dks[1], dv
            return f

        zero = jnp.zeros((tk, LANES), F32)
        dka, dkb, dv = lax.fori_loop(kj + 1, nq, step(False), step(True)(kj, (zero, zero, zero)))
        dk_ref[:, :LANES] = dka
        dk_ref[:, LANES:] = dkb
        dv_ref[...] = dv

    krow = lambda w, c0: pl.BlockSpec((tk, w), lambda b, g, j: (b * nq + j, c0 + g))
    seq = lambda w: pl.BlockSpec((S, w), lambda b, g, j: (b, g))
    stat = pl.BlockSpec((1, 2, nq, tq), lambda b, g, j: (b, g, 0, 0))
    dk, dv, dq = _pcall(
        body, name=name, grid=(B, PAIRS, nq),
        in_specs=[seq(2 * LANES), krow(2 * LANES, 0), krow(LANES, v_block0), seq(LANES), stat, seq(LANES)],
        out_specs=[krow(2 * LANES, 0), krow(LANES, 0), seq(2 * LANES)],
        out_shape=[jax.ShapeDtypeStruct((T, MLA_HEADS * LANES), F32), jax.ShapeDtypeStruct((T, PAIRS * LANES), F32),
                   jax.ShapeDtypeStruct((T, MLA_HEADS * LANES), F32)],
        args=(q, k, kv, o, lse_rows, do), scratch=[pltpu.VMEM((2, nq, tq), F32)], ride=ride)
    return dq, dk, dv


def _lru_gates(xl, halo, cw_ref, cb_ref, wa_ref, ba_ref, wx_ref, bx_ref, lam_ref):
    xc = cb_ref[...] + cw_ref[3:4, :] * xl
    for kk in range(LRU_CONV - 1):
        xc = xc + cw_ref[kk:kk + 1, :] * _shift_rows(xl, LRU_CONV - 1 - kk, halo)
    r = _sigmoid(_dot_nn(xc, wa_ref[...]) + ba_ref[...])
    i = _sigmoid(_dot_nn(xc, wx_ref[...]) + bx_ref[...])
    lam = lam_ref[...]
    sp = jnp.maximum(-lam, 0.0) + jnp.log(1.0 + jnp.exp(-jnp.abs(lam)))
    a = jnp.exp(-LRU_C * r * sp)
    mult = jnp.sqrt(1.0 - a * a)
    return xc, r, i, sp, a, mult


def _lru_specs(tt, nt, S):
    def make(rev):
        tmap = (lambda t: nt - 1 - t) if rev else (lambda t: t)
        tile = lambda cb: pl.BlockSpec((tt, LRU_WIDTH), lambda b, t: (b * nt + tmap(t), cb))
        prev8 = lambda cb: pl.BlockSpec(
            (8, LRU_WIDTH), lambda b, t: (jnp.maximum((b * nt + tmap(t)) * (tt // 8) - 1, 0), cb))
        return tile, prev8, tmap
    return make


def _lru_fwd(z, cw, cb, wa, ba, wx, bx, lam, *, S, name, ride=None):
    T = z.shape[0]
    tt = min(ROW_TILE, S)
    nt = S // tt
    tile, prev8, _ = _lru_specs(tt, nt, S)(False)
    vec = lambda r: pl.BlockSpec((r, LRU_WIDTH), lambda b, t: (0, 0))
    mat = pl.BlockSpec((LRU_WIDTH, LRU_WIDTH), lambda b, t: (0, 0))

    def body(xl_ref, halo_ref, gate_ref, cw_ref, cb_ref, wa_ref, ba_ref, wx_ref, bx_ref, lam_ref,
             y_ref, h_ref, carry_ref):
        t = pl.program_id(1)
        first = t == 0
        halo = jnp.where(first, 0.0, halo_ref[...])
        xl_t = xl_ref[...]
        xc, r, i, sp, a, mult = _lru_gates(xl_t, halo, cw_ref, cb_ref, wa_ref, ba_ref, wx_ref, bx_ref, lam_ref)
        bv = mult * (i * xc)
        ones = jnp.ones((8, LRU_WIDTH), F32)
        zeros = jnp.zeros((8, LRU_WIDTH), F32)
        row = lax.broadcasted_iota(jnp.int32, (tt, LRU_WIDTH), 0)
        A = a
        d = 1
        while d < tt:
            if d < 8:
                a_sh = _shift_rows(A, d, ones)
                b_sh = _shift_rows(bv, d, zeros)
            else:
                a_sh = jnp.where(row < d, 1.0, pltpu.roll(A, d, 0))
                b_sh = jnp.where(row < d, 0.0, pltpu.roll(bv, d, 0))
            bv = A * b_sh + bv
            A = A * a_sh
            d *= 2
        h0 = jnp.where(first, 0.0, carry_ref[0:1, :])
        h = A * h0 + bv
        carry_ref[...] = jnp.broadcast_to(h[tt - 1:tt, :], (8, LRU_WIDTH))
        h_ref[...] = h
        y_ref[...] = (h * _gelu(gate_ref[...])).astype(BF16)

    return _pcall(
        body, name=name, grid=(T // S, nt),
        in_specs=[tile(0), prev8(0), tile(1), vec(LRU_CONV), vec(1), mat, vec(1), mat, vec(1), vec(1)],
        out_specs=[tile(0), tile(0)],
        out_shape=[jax.ShapeDtypeStruct((T, LRU_WIDTH), BF16), jax.ShapeDtypeStruct((T, LRU_WIDTH), F32)],
        args=(z, z, z, cw, cb, wa, ba, wx, bx, lam), scratch=[pltpu.VMEM((8, LRU_WIDTH), F32)], ride=ride)


def _lru_bwd(z, h, dy, cw, cb, wa, ba, wx, bx, lam, *, S, name):
    T = z.shape[0]
    tt = min(ROW_TILE, S)
    nt = S // tt
    tile, prev8, tmap = _lru_specs(tt, nt, S)(True)
    vec = lambda r: pl.BlockSpec((r, LRU_WIDTH), lambda b, t: (0, 0))
    mat = pl.BlockSpec((LRU_WIDTH, LRU_WIDTH), lambda b, t: (0, 0))

    def body(xl_ref, halo_ref, gate_ref, h_ref, hprev_ref, dy_ref, cw_ref, cb_ref, wa_ref, ba_ref, wx_ref,
             bx_ref, lam_ref, dxl_ref, dgate_ref, dcw_ref, dcb_ref, dwa_ref, dba_ref, dwx_ref, dbx_ref,
             dlam_ref, lamc_ref, ac_ref, dxc_ref):
        b = pl.program_id(0)
        t = pl.program_id(1)
        tr = nt - 1 - t
        seq_first = tr == 0
        seq_last = t == 0
        halo = jnp.where(seq_first, 0.0, halo_ref[...])
        xl_t = xl_ref[...]
        xc, r, i, sp, a, mult = _lru_gates(xl_t, halo, cw_ref, cb_ref, wa_ref, ba_ref, wx_ref, bx_ref, lam_ref)
        hh = h_ref[...]
        dyf = dy_ref[...].astype(F32)
        gl, dgl = _gelu_and_grad(gate_ref[...])
        dgate_ref[...] = (dyf * hh * dgl).astype(BF16)
        dh = dyf * gl

        a_first_later = jnp.where(seq_last, 0.0, ac_ref[...])
        lam_later = jnp.where(seq_last, 0.0, lamc_ref[...])
        row = lax.broadcasted_iota(jnp.int32, (tt, LRU_WIDTH), 0)
        A = _shift_rows_up(a, 1, a_first_later)
        lm = dh
        ones = jnp.ones((8, LRU_WIDTH), F32)
        zeros = jnp.zeros((8, LRU_WIDTH), F32)
        d = 1
        while d < tt:
            if d < 8:
                a_sh = _shift_rows_up(A, d, ones)
                l_sh = _shift_rows_up(lm, d, zeros)
            else:
                a_sh = jnp.where(row >= tt - d, 1.0, pltpu.roll(A, tt - d, 0))
                l_sh = jnp.where(row >= tt - d, 0.0, pltpu.roll(lm, tt - d, 0))
            lm = lm + A * l_sh
            A = A * a_sh
            d *= 2
        lm = lm + A * lam_later[0:1, :]
        lamc_ref[...] = jnp.broadcast_to(lm[0:1, :], (8, LRU_WIDTH))
        ac_ref[...] = jnp.broadcast_to(a[0:1, :], (8, LRU_WIDTH))

        hprev_halo = jnp.where(seq_first, 0.0, hprev_ref[...])
        h_prev = _shift_rows(hh, 1, hprev_halo)
        da = lm * h_prev
        ixc = i * xc
        dmult = lm * ixc
        di = lm * mult * xc
        dxc = lm * mult * i
        da = da - dmult * a / mult
        dlog = da * a
        dr = dlog * (-LRU_C) * sp
        dsp_part = jnp.sum(dlog * (-LRU_C) * r, axis=0, keepdims=True)
        dpa = dr * r * (1.0 - r)
        dpx = di * i * (1.0 - i)
        dxc = dxc + _dot_nt(dpa, wa_ref[...]) + _dot_nt(dpx, wx_ref[...])
        dwa_part = _dot_tn(xc, dpa)
        dwx_part = _dot_tn(xc, dpx)

        later = jnp.where(seq_last, 0.0, dxc_ref[...])
        dxl = cw_ref[3:4, :] * dxc
        for kk in range(LRU_CONV - 1):
            dxl = dxl + cw_ref[kk:kk + 1, :] * _shift_rows_up(dxc, LRU_CONV - 1 - kk, later)
        dxl_ref[...] = dxl.astype(BF16)
        dxc_ref[...] = dxc[0:8, :]
        dcw_rows = [jnp.sum(dxc * _shift_rows(xl_t, LRU_CONV - 1 - kk, halo), axis=0, keepdims=True)
                    for kk in range(LRU_CONV - 1)]
        dcw_rows.append(jnp.sum(dxc * xl_t, axis=0, keepdims=True))
        dcw_part = jnp.concatenate(dcw_rows + [jnp.zeros((8 - LRU_CONV, LRU_WIDTH), F32)], axis=0)
        lamv = lam_ref[...]
        dlam_part = dsp_part * (-_sigmoid(-lamv))
        parts = ((dcw_ref, dcw_part), (dcb_ref, jnp.sum(dxc, axis=0, keepdims=True)),
                 (dwa_ref, dwa_part), (dba_ref, jnp.sum(dpa, axis=0, keepdims=True)),
                 (dwx_ref, dwx_part), (dbx_ref, jnp.sum(dpx, axis=0, keepdims=True)),
                 (dlam_ref, dlam_part))
        start = jnp.logical_and(b == 0, t == 0)

        @pl.when(start)
        def _():
            for ref, val in parts:
                ref[...] = val

        @pl.when(jnp.logical_not(start))
        def _():
            for ref, val in parts:
                ref[...] += val

    acc = lambda r: pl.BlockSpec((r, LRU_WIDTH), lambda b, t: (0, 0))
    return pl.pallas_call(
        body, name=name, grid=(T // S, nt),
        in_specs=[tile(0), prev8(0), tile(1), tile(0), prev8(0), tile(0),
                  vec(LRU_CONV), vec(1), mat, vec(1), mat, vec(1), vec(1)],
        out_specs=[tile(0), tile(0), acc(8), acc(1), mat, acc(1), mat, acc(1), acc(1)],
        out_shape=[jax.ShapeDtypeStruct((T, LRU_WIDTH), BF16), jax.ShapeDtypeStruct((T, LRU_WIDTH), BF16),
                   jax.ShapeDtypeStruct((8, LRU_WIDTH), F32), jax.ShapeDtypeStruct((1, LRU_WIDTH), F32),
                   jax.ShapeDtypeStruct((LRU_WIDTH, LRU_WIDTH), F32), jax.ShapeDtypeStruct((1, LRU_WIDTH), F32),
                   jax.ShapeDtypeStruct((LRU_WIDTH, LRU_WIDTH), F32), jax.ShapeDtypeStruct((1, LRU_WIDTH), F32),
                   jax.ShapeDtypeStruct((1, LRU_WIDTH), F32)],
        scratch_shapes=[pltpu.VMEM((8, LRU_WIDTH), F32), pltpu.VMEM((8, LRU_WIDTH), F32),
                        pltpu.VMEM((8, LRU_WIDTH), F32)],
        compiler_params=_cparams(("arbitrary", "arbitrary")),
    )(z, z, z, h, h, dy, cw, cb, wa, ba, wx, bx, lam)


FFN_CT = 1408
FFN_TILE = 512


def _ffn_conv(g, halo, cw, cb):
    gc = cb + cw[2:3, :] * g
    for kk in range(FFN_CONV - 1):
        gc = gc + cw[kk:kk + 1, :] * _shift_rows(g, FFN_CONV - 1 - kk, halo)
    return gc


def _row_chunks(rows, chunk):
    return [slice(r0, min(r0 + chunk, rows)) for r0 in range(0, rows, chunk)]


FFN_CHUNK = 128
HALO = 16


def _ffn_act_down(g, u, cw, cb, w_down, res, *, S, name, ride=None):
    T, F = g.shape
    D = w_down.shape[1]
    tt = min(FFN_TILE, S)
    nt = S // tt
    tc = _tile(F, FFN_CT)
    nj = F // tc

    def body(g_ref, halo_ref, u_ref, cw_ref, cb_ref, w_ref, r_ref, o_ref, act_ref):
        j = pl.program_id(1)
        first = (pl.program_id(0) % nt) == 0
        cw, cb = cw_ref[...], cb_ref[...]

        @pl.when(j == 0)
        def _():
            o_ref[...] = r_ref[...]

        for r in _row_chunks(tt, FFN_CHUNK):
            before = halo_ref[...] if r.start == 0 else g_ref[r.start - HALO:r.start, :]
            halo = before.astype(F32)[HALO - 8:]
            if r.start == 0:
                halo = jnp.where(first, 0.0, halo)
            gc = _ffn_conv(g_ref[r, :].astype(F32), halo, cw, cb)
            act = (_gelu(gc) * u_ref[r, :].astype(F32)).astype(BF16)
            act_ref[r, :] = act
            o_ref[r, :] += _dot_nn(act, w_ref[...])

    tile = pl.BlockSpec((tt, tc), lambda i, j: (i, j))
    prev = pl.BlockSpec((HALO, tc), lambda i, j: (jnp.maximum(i * (tt // HALO) - 1, 0), j))
    rows = pl.BlockSpec((tt, D), lambda i, j: (i, 0))
    return _pcall(
        body, name=name, grid=(T // tt, nj),
        in_specs=[tile, prev, tile, pl.BlockSpec((FFN_CONV, tc), lambda i, j: (0, j)),
                  pl.BlockSpec((1, tc), lambda i, j: (0, j)), pl.BlockSpec((tc, D), lambda i, j: (j, 0)), rows],
        out_specs=[rows, tile], out_shape=[jax.ShapeDtypeStruct((T, D), F32), jax.ShapeDtypeStruct((T, F), BF16)],
        args=(g, g, u, cw, cb, w_down, res), sem=("parallel", "arbitrary"), ride=ride)


def _ffn_act_bwd(g, u, dh, w_down, cw, cb, *, S, name, ride=None):
    T, F = g.shape
    D = w_down.shape[1]
    tt = min(FFN_TILE, S)
    nt = S // tt
    ntt = T // tt
    tc = _tile(F, FFN_CT)

    def body(g_ref, halo_ref, u_ref, dh_ref, w_ref, cw_ref, cb_ref, dg_ref, du_ref, dcw_ref, dcb_ref, later_ref):
        step = pl.program_id(1)
        ti = (ntt - 1 - step) % nt
        cw, cb = cw_ref[...], cb_ref[...]

        @pl.when(step == 0)
        def _():
            dcw_ref[...] = jnp.zeros_like(dcw_ref)
            dcb_ref[...] = jnp.zeros_like(dcb_ref)

        halo = jnp.where(ti == 0, 0.0, halo_ref[...].astype(F32)[HALO - 8:])
        gt = g_ref[...].astype(F32)
        gl, dgl = _gelu_and_grad(_ffn_conv(gt, halo, cw, cb))
        da = _dot_nt(dh_ref[...], w_ref[...])
        du_ref[...] = (da * gl).astype(BF16)
        dgc = da * u_ref[...].astype(F32) * dgl
        later = jnp.where(ti == nt - 1, 0.0, later_ref[...])
        dg = cw[2:3, :] * dgc
        for kk in range(FFN_CONV - 1):
            dg = dg + cw[kk:kk + 1, :] * _shift_rows_up(dgc, FFN_CONV - 1 - kk, later)
        dg_ref[...] = dg.astype(BF16)
        later_ref[...] = dgc[0:8, :]
        rows = [jnp.sum(dgc * _shift_rows(gt, FFN_CONV - 1 - kk, halo), axis=0, keepdims=True)
                for kk in range(FFN_CONV - 1)]
        rows.append(jnp.sum(dgc * gt, axis=0, keepdims=True))
        dcw_ref[...] += jnp.concatenate(rows + [jnp.zeros((8 - FFN_CONV, tc), F32)], axis=0)
        dcb_ref[...] += jnp.sum(dgc, axis=0, keepdims=True)

    tile = pl.BlockSpec((tt, tc), lambda j, s: (ntt - 1 - s, j))
    prev = pl.BlockSpec((HALO, tc), lambda j, s: (jnp.maximum((ntt - 1 - s) * (tt // HALO) - 1, 0), j))
    return _pcall(
        body, name=name, grid=(F // tc, ntt),
        in_specs=[tile, prev, tile, pl.BlockSpec((tt, D), lambda j, s: (ntt - 1 - s, 0)),
                  pl.BlockSpec((tc, D), lambda j, s: (j, 0)), pl.BlockSpec((FFN_CONV, tc), lambda j, s: (0, j)),
                  pl.BlockSpec((1, tc), lambda j, s: (0, j))],
        out_specs=[tile, tile, pl.BlockSpec((8, tc), lambda j, s: (0, j)), pl.BlockSpec((1, tc), lambda j, s: (0, j))],
        out_shape=[jax.ShapeDtypeStruct((T, F), BF16), jax.ShapeDtypeStruct((T, F), BF16),
                   jax.ShapeDtypeStruct((8, F), F32), jax.ShapeDtypeStruct((1, F), F32)],
        args=(g, g, u, dh, w_down, cw, cb), scratch=[pltpu.VMEM((8, tc), F32)], ride=ride)


def _sgu_norm(zv, g_ref, b_ref):
    v = _gelu(zv)
    mu = jnp.mean(v, axis=-1, keepdims=True)
    xc = v - mu
    rstd = lax.rsqrt(jnp.mean(xc * xc, axis=-1, keepdims=True) + NORM_EPS)
    xhat = xc * rstd
    return xhat, rstd, xhat * g_ref[...] + b_ref[...]


def _sgu_fwd(zc, ln_g, ln_b, wm, bmap, *, name):
    T = zc.shape[0]
    W = SGU_WIDTH
    tt = ROW_TILE
    nch = tt // CHUNK

    def body(z_ref, g_ref, b_ref, wm_ref, bm_ref, p_ref):
        u = _gelu(z_ref[:, :W])
        _, _, vn = _sgu_norm(z_ref[:, W:], g_ref, b_ref)
        vn = vn.astype(BF16)
        for n in range(nch):
            rows = slice(n * CHUNK, (n + 1) * CHUNK)
            for gi in range(SGU_GROUPS):
                cols = slice(gi * LANES, (gi + 1) * LANES)
                s = _dot_nn(wm_ref[gi], vn[rows, cols]) + bm_ref[:, cols]
                p_ref[rows, cols] = (u[rows, cols] * s).astype(BF16)

    const2 = lambda r, c: pl.BlockSpec((r, c), lambda i: (0, 0))
    return pl.pallas_call(
        body, name=name, grid=(T // tt,),
        in_specs=[pl.BlockSpec((tt, 2 * W), lambda i: (i, 0)), const2(1, W), const2(1, W),
                  pl.BlockSpec((SGU_GROUPS, CHUNK, CHUNK), lambda i: (0, 0, 0)), const2(CHUNK, W)],
        out_specs=pl.BlockSpec((tt, W), lambda i: (i, 0)),
        out_shape=jax.ShapeDtypeStruct((T, W), BF16),
        compiler_params=_cparams(("parallel",)),
    )(zc, ln_g, ln_b, wm, bmap)


def _sgu_bwd(zc, dp, ln_g, ln_b, wm, bmap, *, name, ride=None):
    T = zc.shape[0]
    W = SGU_WIDTH
    tt = ROW_TILE
    nch = tt // CHUNK
    nsteps = T // tt

    def body(z_ref, dp_ref, g_ref, b_ref, wm_ref, bm_ref, dz_ref, dg_ref, db_ref, dwm_ref, dbm_ref,
             s_scr, dvn_scr):
        step = pl.program_id(0)
        zu = z_ref[:, :W]
        zv = z_ref[:, W:]
        u, dgu = _gelu_and_grad(zu)
        xhat, rstd, vn = _sgu_norm(zv, g_ref, b_ref)
        vnb = vn.astype(BF16)
        dpf = dp_ref[...].astype(F32)
        ds = dpf * u

        @pl.when(step == 0)
        def _():
            dwm_ref[...] = jnp.zeros_like(dwm_ref)
            dbm_ref[...] = jnp.zeros_like(dbm_ref)

        for n in range(nch):
            rows = slice(n * CHUNK, (n + 1) * CHUNK)
            for gi in range(SGU_GROUPS):
                cols = slice(gi * LANES, (gi + 1) * LANES)
                s_scr[rows, cols] = _dot_nn(wm_ref[gi], vnb[rows, cols]) + bm_ref[:, cols]
                dsb = ds[rows, cols]
                dvn_scr[rows, cols] = _dot_tn(wm_ref[gi], dsb)
                dwm_ref[gi] += _dot_nt(dsb, vnb[rows, cols])
                dbm_ref[:, cols] += dsb
        dz_ref[:, :W] = (dpf * s_scr[...] * dgu).astype(BF16)
        dvn = dvn_scr[...]
        dxhat = dvn * g_ref[...]
        dv = rstd * (dxhat - jnp.mean(dxhat, axis=-1, keepdims=True)
                     - xhat * jnp.mean(dxhat * xhat, axis=-1, keepdims=True))
        _, dgv = _gelu_and_grad(zv)
        dz_ref[:, W:] = (dv * dgv).astype(BF16)
        dg_part = jnp.sum(dvn * xhat, axis=0, keepdims=True)
        db_part = jnp.sum(dvn, axis=0, keepdims=True)

        @pl.when(step == 0)
        def _():
            dg_ref[...] = dg_part
            db_ref[...] = db_part

        @pl.when(step > 0)
        def _():
            dg_ref[...] += dg_part
            db_ref[...] += db_part

        @pl.when(step == nsteps - 1)
        def _():
            for gi in range(SGU_GROUPS):
                cols = slice(gi * LANES, (gi + 1) * LANES)
                tot = jnp.sum(dbm_ref[:, cols], axis=1, keepdims=True)
                dbm_ref[:, cols] = jnp.broadcast_to(tot, (CHUNK, LANES))

    const2 = lambda r, c: pl.BlockSpec((r, c), lambda i: (0, 0))
    wspec = pl.BlockSpec((SGU_GROUPS, CHUNK, CHUNK), lambda i: (0, 0, 0))
    return _pcall(
        body, name=name, grid=(nsteps,),
        in_specs=[pl.BlockSpec((tt, 2 * W), lambda i: (i, 0)), pl.BlockSpec((tt, W), lambda i: (i, 0)),
                  const2(1, W), const2(1, W), wspec, const2(CHUNK, W)],
        out_specs=[pl.BlockSpec((tt, 2 * W), lambda i: (i, 0)), const2(1, W), const2(1, W), wspec, const2(CHUNK, W)],
        out_shape=[jax.ShapeDtypeStruct((T, 2 * W), BF16), jax.ShapeDtypeStruct((1, W), F32),
                   jax.ShapeDtypeStruct((1, W), F32), jax.ShapeDtypeStruct((SGU_GROUPS, CHUNK, CHUNK), F32),
                   jax.ShapeDtypeStruct((CHUNK, W), F32)],
        args=(zc, dp, ln_g, ln_b, wm, bmap), scratch=[pltpu.VMEM((tt, W), F32), pltpu.VMEM((tt, W), F32)], ride=ride)


def _rope_tables(positions):
    half = QK_ROPE // 2
    inv_freq = jnp.exp(-math.log(ROPE_BASE) * jnp.arange(half, dtype=F32) / half)
    ang = positions.reshape(-1).astype(F32)[:, None] * inv_freq
    cos = jnp.cos(ang)
    sin = jnp.sin(ang)
    n = ang.shape[0]
    tail = LANES - QK_NOPE - QK_ROPE
    cos_t = jnp.concatenate([jnp.ones((n, QK_NOPE), F32), cos, cos, jnp.ones((n, tail), F32)], axis=1)
    sin_t = jnp.concatenate([jnp.zeros((n, QK_NOPE), F32), -sin, sin, jnp.zeros((n, tail), F32)], axis=1)
    return cos_t, sin_t


SGU_GROUP_DIM = SGU_WIDTH // SGU_GROUPS
_O1, _O2, _O3, _O4 = Q_LORA, Q_LORA + KV_LORA, Q_LORA + KV_LORA + QK_ROPE, Q_LORA + KV_LORA + QK_ROPE + LRU_WIDTH
_A0, _A1, _A2 = 2 * LRU_WIDTH, 2 * LRU_WIDTH + Q_LORA, 2 * LRU_WIDTH + Q_LORA + KV_LORA
_A3 = _A2 + QK_NOPE
Z_Q_BLOCK, Z_KV_BLOCK, Z_KPE_BLOCK = _A0 // Q_LORA, _A1 // KV_LORA, _A2 // LANES


def _perm_w_in(w_in):
    zeros = lambda n: jnp.zeros((w_in.shape[0], n), w_in.dtype)
    return jnp.concatenate([w_in[:, _O3:_O4], w_in[:, _O4:], w_in[:, :_O1], w_in[:, _O1:_O2], zeros(QK_NOPE),
                            w_in[:, _O2:_O3], zeros(LANES - QK_NOPE - QK_ROPE)], axis=1)


def _unperm_w_in(w):
    return jnp.concatenate([w[:, _A0:_A1], w[:, _A1:_A2], w[:, _A3:_A3 + QK_ROPE], w[:, :LRU_WIDTH],
                            w[:, LRU_WIDTH:_A0]], axis=1)


def _head_blocks(w, d):
    r = w.shape[0]
    return jnp.pad(w.reshape(r, MLA_HEADS, d), ((0, 0), (0, 0), (0, LANES - d))).reshape(r, MLA_HEADS * LANES)


def _from_head_blocks(w, d):
    r = w.shape[0]
    return w.reshape(r, MLA_HEADS, LANES)[:, :, :d].reshape(r, MLA_HEADS * d)


def _split_kv(w_kv):
    r = w_kv.shape[0]
    w3 = w_kv.reshape(r, MLA_HEADS, QK_NOPE + V_HEAD)
    return _head_blocks(w3[:, :, :QK_NOPE].reshape(r, -1), QK_NOPE), w3[:, :, QK_NOPE:].reshape(r, -1)


def _join_kv(w_k, w_v):
    r = w_k.shape[0]
    return jnp.concatenate([_from_head_blocks(w_k, QK_NOPE).reshape(r, MLA_HEADS, QK_NOPE),
                            w_v.reshape(r, MLA_HEADS, V_HEAD)], axis=2).reshape(r, -1)


def _prep_small(w):
    p = {n: w[n] for n in w if n not in BIG}
    eye = jnp.eye(LRU_HEADS, dtype=F32)
    dense = lambda wg: (wg[:, :, None, :] * eye[:, None, :, None]).reshape(LRU_WIDTH, LRU_WIDTH).astype(BF16)
    p["wa_d"] = dense(w["ab_w_rg_a"][0])
    p["wx_d"] = dense(w["ab_w_rg_x"][0])
    causal = jnp.tril(jnp.ones((CHUNK, CHUNK), F32))
    p["wm"] = (w["c_w_s"][0] * causal).astype(BF16)
    p["bmap"] = jnp.repeat(w["c_b_s"][0].T, SGU_GROUP_DIM, axis=1)
    return p


def _prep_big(ab_w_in, ab_w_q_b, ab_w_kv_b):
    return {"w_in_p": _perm_w_in(ab_w_in).astype(BF16),
            "w_q_p": _head_blocks(ab_w_q_b, QK_NOPE + QK_ROPE).astype(BF16),
            "w_kv_p": jnp.concatenate(_split_kv(ab_w_kv_b), axis=1).astype(BF16)}


def _ffn_fwd(h, l, p, S, rides):
    hn = _rms_fwd(h, p["ffn_norm"][l], name=f"ffn{l}_norm")
    g = _mm(hn, p["ffn_gate_t"][l], tb=True, out_dtype=BF16, name=f"ffn{l}_gate", ride=rides.get(f"ffn{l}_gate"))
    u = _mm(hn, p["ffn_up_t"][l], tb=True, out_dtype=BF16, name=f"ffn{l}_up", ride=rides.get(f"ffn{l}_up"))
    out, act = _ffn_act_down(g, u, p["ffn_conv_w"][l], p["ffn_conv_b"][l][None], p["ffn_down"][l], h, S=S,
                             name=f"ffn{l}_down", ride=rides.get(f"ffn{l}_down"))
    return out, (hn, g, u, act)


def _ffn_bwd(dh, h_in, l, p, saved, S, rides, grads_ready, also_ready=None):
    hn, g, u, act = saved
    dw_down = _mm(act, dh, ta=True, out_dtype=BF16, name=f"ffn{l}_dwdown")
    dg, du, dcw, dcb = _ffn_act_bwd(g, u, dh, p["ffn_down"][l], p["ffn_conv_w"][l], p["ffn_conv_b"][l][None], S=S,
                                    name=f"ffn{l}_dactbwd", ride=rides.get(f"ffn{l}_dactbwd"))
    dhn = _mm(dg, p["ffn_gate_t"][l], name=f"ffn{l}_dhn_g")
    dhn = _mm(du, p["ffn_up_t"][l], res=dhn, out_dtype=BF16, name=f"ffn{l}_dhn_u")
    dw_gate_t = _mm(dg, hn, ta=True, out_dtype=BF16, name=f"ffn{l}_dwgate")
    dw_up_t = _mm(du, hn, ta=True, out_dtype=BF16, name=f"ffn{l}_dwup")
    grads_ready(l, {**(also_ready or {}), "ffn_gate_t": dw_gate_t, "ffn_up_t": dw_up_t, "ffn_down": dw_down})
    dh_in, dnorm = _rms_bwd(h_in, p["ffn_norm"][l], dhn, res=dh, name=f"ffn{l}_dnorm", ride=rides.get(f"ffn{l}_dnorm"))
    grads = dict(ffn_norm=dnorm[0], ffn_gate_t=dw_gate_t, ffn_up_t=dw_up_t, ffn_conv_w=dcw[:FFN_CONV],
                 ffn_conv_b=dcb[0], ffn_down=dw_down)
    return dh_in, grads


def _local_step(x, positions, target, p, rides=None, grads_ready=None):
    rides = {} if rides is None else rides
    grads_ready = grads_ready or (lambda layer, ready: None)
    B, S, D = x.shape
    T = B * S
    H = MLA_HEADS
    xf = x.reshape(T, D)
    tgt = target.reshape(T, D)
    cos, sin = _rope_tables(positions)

    hn0 = _rms_fwd(xf, p["ab_norm"][0], name="ab_norm", ride=rides.get("ab_norm"))
    z = _mm(hn0, p["w_in_p"], name="ab_in")
    cqn = _rms_fwd(z, p["ab_q_norm"][0], cb=Z_Q_BLOCK, name="q_norm")
    ckvn = _rms_fwd(z, p["ab_kv_norm"][0], cb=Z_KV_BLOCK, name="kv_norm")
    q = _mm(cqn, p["w_q_p"], name="q_up")
    kv = _mm(ckvn, p["w_kv_p"], out_dtype=BF16, name="kv_up")
    qs = _rope_q(q, cos, sin, name="q_rope")
    kk = _key_blocks(kv, z, cos, sin, kpe_block=Z_KPE_BLOCK, name="k_rope")
    att = dict(B=B, S=S, v_block0=H)
    o, lse = _attn_fwd(qs, kk, kv, name="attn_fwd", ride=rides.get("attn_fwd"), **att)
    lru_par = (p["ab_conv_w"][0], p["ab_conv_b"], p["wa_d"], p["ab_b_rg_a"], p["wx_d"], p["ab_b_rg_x"], p["ab_lambda"])
    y_lru, hs = _lru_fwd(z, *lru_par, S=S, name="lru_fwd", ride=rides.get("lru_fwd"))
    n_att = H * V_HEAD
    w_out_a, w_out_b = p["ab_w_out"][:n_att], p["ab_w_out"][n_att:]
    h1 = _mm(y_lru, w_out_b, res=_mm(o, w_out_a, res=xf, name="ab_out_a"), name="ab_out_b")
    h2, ffn0 = _ffn_fwd(h1, 0, p, S, rides)

    hn2 = _rms_fwd(h2, p["c_norm"][0], name="c_norm")
    zc = _mm(hn2, p["c_w_in_t"], tb=True, name="c_in")
    pg = _sgu_fwd(zc, p["c_ln_g"], p["c_ln_b"], p["wm"], p["bmap"], name="sgu_fwd")
    h3 = _mm(pg, p["c_w_out"], res=h2, name="c_out")
    h4, ffn1 = _ffn_fwd(h3, 1, p, S, rides)

    loss_row, dh4, dfinal = _final_fwd_bwd(h4, p["final_norm"], tgt, name="final")

    dh3, g_ffn1 = _ffn_bwd(dh4, h3, 1, p, ffn1, S, rides, grads_ready)
    dpg = _mm(dh3, p["c_w_out"], tb=True, out_dtype=BF16, name="c_dp")
    dw_c_out = _mm(pg, dh3, ta=True, out_dtype=BF16, name="c_dwout")
    dzc, dlng, dlnb, dwm, dbm = _sgu_bwd(zc, dpg, p["c_ln_g"], p["c_ln_b"], p["wm"], p["bmap"], name="sgu_bwd",
                                         ride=rides.get("sgu_bwd"))
    dhn2 = _mm(dzc, p["c_w_in_t"], out_dtype=BF16, name="c_dhn")
    dw_c_in_t = _mm(dzc, hn2, ta=True, out_dtype=BF16, name="c_dwin")
    dh2, dcnorm = _rms_bwd(h2, p["c_norm"][0], dhn2, res=dh3, name="c_dnorm")
    dh1, g_ffn0 = _ffn_bwd(dh2, h1, 0, p, ffn0, S, rides, grads_ready, {"c_w_in_t": dw_c_in_t, "c_w_out": dw_c_out})

    do = _mm(dh1, w_out_a, tb=True, name="ab_do")
    dy_lru = _mm(dh1, w_out_b, tb=True, out_dtype=BF16, name="ab_dylru")
    dw_out = jnp.concatenate([_mm(o, dh1, ta=True, out_dtype=BF16, name="ab_dwout_a"),
                              _mm(y_lru, dh1, ta=True, out_dtype=BF16, name="ab_dwout_b")], axis=0)
    dq, dk, dv = _attn_bwd(qs, kk, kv, o, lse, do, name="attn_bwd", ride=rides.get("attn_bwd"), **att)
    dq_full = _rope_q_bwd(dq, cos, sin, name="q_rope_bwd")
    dkr = _key_rope_bwd(dk, cos, sin, name="k_rope_bwd")
    n_key = H * LANES
    w_k_p, w_v_p = p["w_kv_p"][:, :n_key], p["w_kv_p"][:, n_key:]
    dcqn = _mm(dq_full, p["w_q_p"], tb=True, name="q_dlat")
    dw_q_p = _mm(cqn, dq_full, ta=True, out_dtype=BF16, name="q_dw")
    dckvn = _mm(dv, w_v_p, tb=True, res=_mm(dk, w_k_p, tb=True, name="k_dlat"), name="v_dlat")
    dw_k_p = _mm(ckvn, dk, ta=True, out_dtype=BF16, name="k_dw")
    dw_v_p = _mm(ckvn, dv, ta=True, out_dtype=BF16, name="v_dw")
    dcq, dqnorm = _rms_bwd(z, p["ab_q_norm"][0], dcqn, cb=Z_Q_BLOCK, out_dtype=BF16, name="q_dnorm")
    dckv, dkvnorm = _rms_bwd(z, p["ab_kv_norm"][0], dckvn, cb=Z_KV_BLOCK, out_dtype=BF16, name="kv_dnorm")
    dxl, dgate, dcw, dcb, dwa, dba, dwx, dbx, dlam = _lru_bwd(z, hs, dy_lru, *lru_par, S=S, name="lru_bwd")
    dz = jnp.concatenate([dxl, dgate, dcq, dckv, dkr], axis=1)
    dhn0 = _mm(dz, p["w_in_p"], tb=True, out_dtype=BF16, name="ab_dhn")
    dw_in_p = _mm(hn0, dz, ta=True, out_dtype=BF16, name="ab_dwin")
    dx, dabnorm = _rms_bwd(xf, p["ab_norm"][0], dhn0, res=dh1, name="ab_dnorm")

    blocks = lambda dd: jnp.stack([dd[i * LRU_BLOCK:(i + 1) * LRU_BLOCK, i * LRU_BLOCK:(i + 1) * LRU_BLOCK]
                                   for i in range(LRU_HEADS)])
    causal = jnp.tril(jnp.ones((CHUNK, CHUNK), F32))
    grads = {
        "ab_norm": dabnorm, "w_in_p": dw_in_p, "ab_q_norm": dqnorm, "w_q_p": dw_q_p,
        "ab_kv_norm": dkvnorm, "w_k_p": dw_k_p, "w_v_p": dw_v_p, "ab_conv_w": dcw[:LRU_CONV][None], "ab_conv_b": dcb,
        "ab_w_rg_a": blocks(dwa)[None], "ab_b_rg_a": dba, "ab_w_rg_x": blocks(dwx)[None], "ab_b_rg_x": dbx,
        "ab_lambda": dlam, "ab_w_out": dw_out,
        "c_norm": dcnorm, "c_w_in_t": dw_c_in_t, "c_ln_g": dlng, "c_ln_b": dlnb,
        "c_w_s": (dwm * causal)[None], "c_b_s": dbm[:, ::SGU_GROUP_DIM].T[None], "c_w_out": dw_c_out,
        "final_norm": dfinal[0],
    }
    for name in ("ffn_norm", "ffn_conv_w", "ffn_conv_b"):
        grads[name] = jnp.stack([g_ffn0[name], g_ffn1[name]])
    for name in ("ffn_gate_t", "ffn_up_t", "ffn_down"):
        grads[name] = [g_ffn0[name], g_ffn1[name]]
    return loss_row, dx.reshape(B, S, D), grads


ANY = pl.BlockSpec(memory_space=pl.ANY)


def _place():
    x, y, c = lax.axis_index("x"), lax.axis_index("y"), lax.axis_index("c")
    chips = [(1 - x, y), (x, 1 - y), (1 - x, 1 - y)]
    return x, y, c, 2 * x + y, (x, y, 1 - c), chips


def _remote(src, dst, send_sems, recv_sems, k, to):
    return pltpu.make_async_remote_copy(src_ref=src, dst_ref=dst, send_sem=send_sems.at[k], recv_sem=recv_sems.at[k],
                                        device_id=to, device_id_type=MESH)


class _Exchange:
    def __init__(self, arrs, out_shapes, n_sems, start, finish):
        self.arrs, self.out_shapes, self.n_sems, self.start, self.finish = list(arrs), out_shapes, n_sems, start, finish

    @property
    def in_specs(self):
        return [ANY] * len(self.arrs)

    @property
    def out_specs(self):
        return [ANY] * len(self.out_shapes)

    @property
    def scratch(self):
        return [pltpu.SemaphoreType.DMA((self.n_sems,)), pltpu.SemaphoreType.DMA((self.n_sems,))]

    def split(self, refs):
        n = len(self.arrs)
        return refs[:n], refs[n:n + len(self.out_shapes)], refs[-2], refs[-1]

    def run(self, name):
        def body(*refs):
            parts = self.split(refs)
            self.start(*parts)
            self.finish(*parts)

        return pl.pallas_call(body, name=name, in_specs=self.in_specs, out_specs=self.out_specs,
                              out_shape=self.out_shapes, scratch_shapes=self.scratch)(*self.arrs)


def _put(buf, piece, idx, axis):
    return lax.dynamic_update_slice_in_dim(buf, jnp.expand_dims(piece, axis).astype(buf.dtype), idx, axis)


def _all_gather(arrs):
    n = len(arrs)

    def start(ins, outs, send_sems, recv_sems):
        x, y, c, j, sib, chips = _place()
        for i in range(n):
            for k, (cx, cy) in enumerate(chips):
                _remote(ins[i].at[:, c], outs[i].at[:, j, c], send_sems, recv_sems, 6 * i + k, (cx, cy, c)).start()

    def finish(ins, outs, send_sems, recv_sems):
        x, y, c, j, sib, chips = _place()
        passed = []
        for i in range(n):
            for k, (cx, cy) in enumerate(chips):
                got = outs[i].at[:, 2 * cx + cy, c]
                _remote(got, got, send_sems, recv_sems, 6 * i + k, (cx, cy, c)).wait_recv()
                cp = _remote(got, got, send_sems, recv_sems, 6 * i + 3 + k, sib)
                cp.start()
                passed.append(cp)
        for i in range(n):
            for k, (cx, cy) in enumerate(chips):
                got = outs[i].at[:, 2 * cx + cy, 1 - c]
                _remote(got, got, send_sems, recv_sems, 6 * i + 3 + k, sib).wait_recv()
                _remote(ins[i].at[:, c], ins[i].at[:, c], send_sems, recv_sems, 6 * i + k, sib).wait_send()
        for cp in passed:
            cp.wait_send()

    shapes = [jax.ShapeDtypeStruct((a.shape[0], N_CHIPS) + a.shape[1:], a.dtype) for a in arrs]
    return _Exchange(arrs, shapes, 6 * n, start, finish)


class _Offset:
    def __init__(self, sems, k0):
        self.sems, self.k0 = sems, k0

    @property
    def at(self):
        return self

    def __getitem__(self, k):
        return self.sems.at[self.k0 + k]


def _merge(a, b):
    n_in, n_out = len(a.arrs), len(a.out_shapes)

    def both(fa, fb):
        def f(ins, outs, send_sems, recv_sems):
            fa(ins[:n_in], outs[:n_out], send_sems, recv_sems)
            fb(ins[n_in:], outs[n_out:], _Offset(send_sems, a.n_sems), _Offset(recv_sems, a.n_sems))
        return f

    return _Exchange(a.arrs + b.arrs, a.out_shapes + b.out_shapes, a.n_sems + b.n_sems,
                     both(a.start, b.start), both(a.finish, b.finish))


def _pair_swap(arrs):
    n = len(arrs)

    def start(ins, outs, send_sems, recv_sems):
        x, y, c, j, sib, chips = _place()
        for i in range(n):
            _remote(ins[i].at[:, 1 - c], outs[i], send_sems, recv_sems, i, sib).start()

    def finish(ins, outs, send_sems, recv_sems):
        x, y, c, j, sib, chips = _place()
        for i in range(n):
            _remote(ins[i].at[:, 1 - c], outs[i], send_sems, recv_sems, i, sib).wait()

    shapes = [jax.ShapeDtypeStruct((a.shape[0],) + a.shape[2:], a.dtype) for a in arrs]
    return _Exchange(arrs, shapes, n, start, finish)


def _pair_send(arrs):
    n = len(arrs)

    def start(ins, outs, send_sems, recv_sems):
        x, y, c, j, sib, chips = _place()
        for i in range(n):
            _remote(ins[i], outs[i], send_sems, recv_sems, i, sib).start()

    def finish(ins, outs, send_sems, recv_sems):
        x, y, c, j, sib, chips = _place()
        for i in range(n):
            _remote(ins[i], outs[i], send_sems, recv_sems, i, sib).wait()

    shapes = [jax.ShapeDtypeStruct(a.shape, a.dtype) for a in arrs]
    return _Exchange(arrs, shapes, n, start, finish)


def _chip_exchange(arrs, *, scatter):
    n = len(arrs)

    def copies(ins, outs, send_sems, recv_sems):
        x, y, c, j, sib, chips = _place()
        return [(_remote(ins[i].at[2 * cx + cy] if scatter else ins[i], outs[i].at[j], send_sems, recv_sems,
                         3 * i + k, (cx, cy, c)),
                 _remote(outs[i].at[2 * cx + cy], outs[i].at[2 * cx + cy], send_sems, recv_sems, 3 * i + k, (cx, cy, c)))
                for i in range(n) for k, (cx, cy) in enumerate(chips)]

    def start(*refs):
        for out, _ in copies(*refs):
            out.start()

    def finish(*refs):
        for out, back in copies(*refs):
            back.wait_recv()
            out.wait_send()

    shapes = [jax.ShapeDtypeStruct((N_CHIPS,) + a.shape[-2:], a.dtype) for a in arrs]
    return _Exchange(arrs, shapes, 3 * n, start, finish)


FLAT_ROWS = 512


def _pair_add(sharded, from_sib, *, name):
    n, _, R, L = sharded.shape
    tr = _tile(R, FLAT_ROWS, 16)

    def body(s_ref, b_ref, o_ref):
        own = jnp.where(lax.axis_index("c") == 0, s_ref[:, 0], s_ref[:, 1])
        o_ref[...] = (own.astype(F32) + b_ref[...].astype(F32)).astype(BF16)

    spec = pl.BlockSpec((n, tr, L), lambda i: (0, i, 0))
    return pl.pallas_call(
        body, name=name, grid=(R // tr,), in_specs=[pl.BlockSpec((n, 2, tr, L), lambda i: (0, 0, i, 0)), spec],
        out_specs=spec, out_shape=jax.ShapeDtypeStruct((n, R, L), BF16), compiler_params=_cparams(("parallel",)),
    )(sharded, from_sib)


def _chip_sum(arrived, pair, *, name):
    n, R, L = arrived.shape
    tr = _tile(R, FLAT_ROWS, 16)

    def body(a_ref, p_ref, o_ref):
        me = 2 * lax.axis_index("x") + lax.axis_index("y")
        acc = None
        for k in range(n):
            term = jnp.where(me == k, p_ref[k], a_ref[k]).astype(F32)
            acc = term if acc is None else acc + term
        o_ref[...] = acc

    spec = pl.BlockSpec((n, tr, L), lambda i: (0, i, 0))
    return pl.pallas_call(
        body, name=name, grid=(R // tr,), in_specs=[spec, spec], out_specs=pl.BlockSpec((tr, L), lambda i: (i, 0)),
        out_shape=jax.ShapeDtypeStruct((R, L), F32), compiler_params=_cparams(("parallel",)),
    )(arrived, pair)


def _sum_slots(buf, *, name):
    n, R, L = buf.shape
    tr = _tile(R, FLAT_ROWS, 16)

    def body(b_ref, o_ref):
        acc = b_ref[0].astype(F32)
        for k in range(1, n):
            acc = acc + b_ref[k].astype(F32)
        o_ref[...] = acc

    return pl.pallas_call(
        body, name=name, grid=(R // tr,), in_specs=[pl.BlockSpec((n, tr, L), lambda i: (0, i, 0))],
        out_specs=pl.BlockSpec((tr, L), lambda i: (i, 0)),
        out_shape=jax.ShapeDtypeStruct((R, L), F32), compiler_params=_cparams(("parallel",)),
    )(buf)


def _adamw_update(w, g, m, v):
    c1 = 1.0 - ADAM_B1 ** ADAM_STEP
    c2 = 1.0 - ADAM_B2 ** ADAM_STEP
    m = ADAM_B1 * m + (1.0 - ADAM_B1) * g
    v = ADAM_B2 * v + (1.0 - ADAM_B2) * (g * g)
    return -ADAM_LR * ((m / c1) / (jnp.sqrt(v / c2) + ADAM_EPS) + ADAM_WD * w), m, v


def _adamw_halves(w, m, v, own, other, *, name):
    NL, R, L = w.shape
    h = R // 2
    tr = _tile(h, FLAT_ROWS, 16)
    nt = h // tr

    def body(*refs):
        w_ref, m_ref, v_ref = refs[:3]
        own_refs, other_refs = refs[3:3 + NL], refs[3 + NL:3 + 2 * NL]
        d_ref, nm_ref, nv_ref, g_ref = refs[3 + 2 * NL:]
        layer, half = pl.program_id(0), pl.program_id(1)
        mine = half == lax.axis_index("c")
        g = jnp.where(mine, own_refs[0][...], other_refs[0][...])
        for l in range(1, NL):
            g = jnp.where(layer == l, jnp.where(mine, own_refs[l][...], other_refs[l][...]), g)
        d, mm, vv = _adamw_update(w_ref[0], g, m_ref[0], v_ref[0])
        d_ref[0], nm_ref[0], nv_ref[0], g_ref[0] = d, mm, vv, g

    spec = pl.BlockSpec((1, tr, L), lambda l, hh, i: (l, hh * nt + i, 0))
    part = pl.BlockSpec((tr, L), lambda l, hh, i: (i, 0))
    sh = jax.ShapeDtypeStruct((NL, R, L), F32)
    return pl.pallas_call(
        body, name=name, grid=(NL, 2, nt), in_specs=[spec] * 3 + [part] * (2 * NL), out_specs=[spec] * 4,
        out_shape=[sh] * 4, compiler_params=_cparams(("parallel", "parallel", "parallel")),
    )(w, m, v, *own, *other)


def _adamw(w, g, m, v, *, name):
    NL, R, L = w.shape
    tr = _tile(R, FLAT_ROWS, 16)

    def body(w_ref, g_ref, m_ref, v_ref, d_ref, nm_ref, nv_ref):
        d_ref[...], nm_ref[...], nv_ref[...] = _adamw_update(w_ref[...], g_ref[...], m_ref[...], v_ref[...])

    spec = pl.BlockSpec((1, tr, L), lambda l, i: (l, i, 0))
    sh = jax.ShapeDtypeStruct((NL, R, L), F32)
    return pl.pallas_call(
        body, name=name, grid=(NL, R // tr), in_specs=[spec] * 4, out_specs=[spec] * 3, out_shape=[sh] * 3,
        compiler_params=_cparams(("parallel", "parallel")),
    )(w, g, m, v)


WEIGHT_NAMES = ["ab_norm", "ab_w_in", "ab_q_norm", "ab_w_q_b", "ab_kv_norm", "ab_w_kv_b", "ab_conv_w", "ab_conv_b",
                "ab_w_rg_a", "ab_b_rg_a", "ab_w_rg_x", "ab_b_rg_x", "ab_lambda", "ab_w_out", "c_norm", "c_w_in",
                "c_ln_g", "c_ln_b", "c_w_s", "c_b_s", "c_w_out", "ffn_norm", "ffn_w_gate", "ffn_w_up", "ffn_conv_w",
                "ffn_conv_b", "ffn_w_down", "final_norm"]
BIG = {"ab_w_in": 2, "ab_w_q_b": 2, "ab_w_kv_b": 2, "ab_w_out": 1, "c_w_in": 2, "c_w_out": 1,
       "ffn_w_gate": 2, "ffn_w_up": 2, "ffn_w_down": 1}
SMALL_SHARDED = {"ab_conv_w": 2, "c_norm": 1, "c_ln_g": 1, "c_ln_b": 1, "ffn_conv_w": 2}
SMALL_REPLICATED = [n for n in WEIGHT_NAMES if n not in BIG and n not in SMALL_SHARDED]


def _rows(n_elems, mult):
    r = -(-n_elems // LANES)
    return -(-r // mult) * mult


def _flat(parts, rows):
    flat = jnp.concatenate([a.reshape(-1) for a in parts])
    return jnp.pad(flat, (0, rows * LANES - flat.shape[0])).reshape(rows, LANES)


def _unflat(flat, shapes):
    flat = flat.reshape(-1)
    out, off = [], 0
    for s in shapes:
        n = math.prod(s)
        out.append(flat[off:off + n].reshape(s))
        off += n
    return out


def _join_shards(a, axis):
    a = jnp.moveaxis(a, 0, axis)
    return a.reshape(a.shape[:axis] + (a.shape[axis] * a.shape[axis + 1],) + a.shape[axis + 2:])


def kernel(x, positions, ab_norm, ab_w_in, ab_q_norm, ab_w_q_b, ab_kv_norm, ab_w_kv_b, ab_conv_w, ab_conv_b, ab_w_rg_a, ab_b_rg_a, ab_w_rg_x, ab_b_rg_x, ab_lambda, ab_w_out, c_norm, c_w_in, c_ln_g, c_ln_b, c_w_s, c_b_s, c_w_out, ffn_norm, ffn_w_gate, ffn_w_up, ffn_conv_w, ffn_conv_b, ffn_w_down, final_norm, loss_target, m_ab_norm, m_ab_w_in, m_ab_q_norm, m_ab_w_q_b, m_ab_kv_norm, m_ab_w_kv_b, m_ab_conv_w, m_ab_conv_b, m_ab_w_rg_a, m_ab_b_rg_a, m_ab_w_rg_x, m_ab_b_rg_x, m_ab_lambda, m_ab_w_out, m_c_norm, m_c_w_in, m_c_ln_g, m_c_ln_b, m_c_w_s, m_c_b_s, m_c_w_out, m_ffn_norm, m_ffn_w_gate, m_ffn_w_up, m_ffn_conv_w, m_ffn_conv_b, m_ffn_w_down, m_final_norm, v_ab_norm, v_ab_w_in, v_ab_q_norm, v_ab_w_q_b, v_ab_kv_norm, v_ab_w_kv_b, v_ab_conv_w, v_ab_conv_b, v_ab_w_rg_a, v_ab_b_rg_a, v_ab_w_rg_x, v_ab_b_rg_x, v_ab_lambda, v_ab_w_out, v_c_norm, v_c_w_in, v_c_ln_g, v_c_ln_b, v_c_w_s, v_c_b_s, v_c_w_out, v_ffn_norm, v_ffn_w_gate, v_ffn_w_up, v_ffn_conv_w, v_ffn_conv_b, v_ffn_w_down, v_final_norm):
    given = dict(locals())
    w = {n: given[n] for n in WEIGHT_NAMES}
    m = {n: given["m_" + n] for n in WEIGHT_NAMES}
    v = {n: given["v_" + n] for n in WEIGHT_NAMES}
    c = lax.axis_index("c")
    chip = 2 * lax.axis_index("x") + lax.axis_index("y")

    halves = lambda a: a.reshape(a.shape[0], 2, a.shape[1] // 2, a.shape[2])
    tr = lambda a: jnp.swapaxes(a, 1, 2)
    send = {"ab_w_in": w["ab_w_in"], "ab_w_q_b": w["ab_w_q_b"], "ab_w_kv_b": w["ab_w_kv_b"], "ab_w_out": w["ab_w_out"],
            "c_w_in": tr(w["c_w_in"]), "c_w_out": w["c_w_out"], "ffn_w_gate": tr(w["ffn_w_gate"]),
            "ffn_w_up": tr(w["ffn_w_up"]), "ffn_w_down": w["ffn_w_down"]}
    small_rows = _rows(sum(w[n].size for n in SMALL_SHARDED), 16)
    small_sh = _flat([w[n] for n in SMALL_SHARDED], small_rows).reshape(1, 2, small_rows // 2, LANES)
    first_names = ["ab_w_in", "ab_w_q_b", "ab_w_kv_b", "ab_w_out"]
    mine = {n: halves(send[n].astype(BF16)) for n in BIG}

    def put_own(own, arrived):
        a = _put(arrived, own, chip, 1)
        return a.reshape(a.shape[0], -1, a.shape[-1])

    p = {"ab_norm": w["ab_norm"], "ffn_gate_t": {}, "ffn_up_t": {}, "ffn_down": {}}
    first = [mine[n] for n in first_names] + [small_sh]

    def first_arrived(got):
        full = {n: put_own(o, a) for n, o, a in zip(first_names + ["small"], first, got)}
        unshard = lambda a: jnp.swapaxes(a.reshape(N_CHIPS, -1, a.shape[-1]), 0, 1).reshape(-1, N_CHIPS * a.shape[-1])
        p.update(_prep_big(unshard(full["ab_w_in"][0]), unshard(full["ab_w_q_b"][0]), unshard(full["ab_w_kv_b"][0])))
        p["ab_w_out"] = full["ab_w_out"][0]
        small_full = dict(w)
        off = 0
        small_got = full["small"].reshape(N_CHIPS, -1)
        for n, ax in SMALL_SHARDED.items():
            seg = small_got[:, off:off + w[n].size].reshape((N_CHIPS,) + w[n].shape)
            small_full[n] = _join_shards(seg, ax)
            off += w[n].size
        p.update(_prep_small(small_full))

    def weights_ride(parts):
        def sink(arrived):
            for (own, setter), a in zip(parts, arrived):
                setter(put_own(own, a)[0])
        return _all_gather([own for own, _ in parts]), sink

    ffn_keys = {"ffn_gate_t": "ffn_w_gate", "ffn_up_t": "ffn_w_up", "ffn_down": "ffn_w_down"}
    ffn_part = lambda key, l: (mine[ffn_keys[key]][l:l + 1], functools.partial(p[key].__setitem__, l))
    rides = {
        "ab_norm": (_all_gather(first), first_arrived),
        "attn_fwd": weights_ride([ffn_part("ffn_gate_t", 0), ffn_part("ffn_up_t", 0)]),
        "lru_fwd": weights_ride([ffn_part("ffn_down", 0)]),
        "ffn0_gate": weights_ride([ffn_part("ffn_gate_t", 1)]),
        "ffn0_up": weights_ride([ffn_part("ffn_up_t", 1)]),
        "ffn0_down": weights_ride([ffn_part("ffn_down", 1),
                                   (mine["c_w_in"], functools.partial(p.__setitem__, "c_w_in_t")),
                                   (mine["c_w_out"], functools.partial(p.__setitem__, "c_w_out"))]),
    }

    def chip_sums(pair, arrived, tag):
        return [_chip_sum(a, b, name=f"grad_chip_sum_{tag}{i}") for i, (a, b) in enumerate(zip(arrived, pair))]

    half_of = {}

    def grads_ready(layer, ready):
        if layer == 1:
            named = {"gate1": ready["ffn_gate_t"], "up1": ready["ffn_up_t"], "down1": ready["ffn_down"]}
            hosts = {"sgu_bwd": ["down1"], "ffn0_dactbwd": ["gate1", "up1"]}
        else:
            named = {"c_in": ready["c_w_in_t"], "c_out": ready["c_w_out"], "gate0": ready["ffn_gate_t"],
                     "up0": ready["ffn_up_t"], "down0": ready["ffn_down"]}
            hosts = {"attn_bwd": ["c_in", "c_out", "down0", "gate0", "up0"]}
        tag = f"f{layer}"
        sharded = [a.reshape(N_CHIPS, 2, -1, a.shape[-1]) for a in named.values()]

        def paired(from_sib):
            pair = {k: _pair_add(a, b, name=f"grad_pair_add_{tag}{i}")
                    for i, (k, a, b) in enumerate(zip(named, sharded, from_sib))}
            for kernel_name, keys in hosts.items():
                def sink(arrived, keys=keys, kernel_name=kernel_name):
                    half_of.update(zip(keys, chip_sums([pair[k] for k in keys], arrived, f"{tag}_{kernel_name}")))
                rides[kernel_name] = (_chip_exchange([pair[k] for k in keys], scatter=True), sink)

        rides[f"ffn{layer}_dnorm"] = (_pair_swap(sharded), paired)

    loss_row, grad_x, g = _local_step(x, positions, loss_target, p, rides, grads_ready)

    cols = lambda a, n: jnp.swapaxes(a.reshape(a.shape[0], N_CHIPS, n), 0, 1)
    n_in, n_q, n_kv = w["ab_w_in"].shape[2], w["ab_w_q_b"].shape[2], w["ab_w_kv_b"].shape[2]
    small_names = SMALL_REPLICATED + list(SMALL_SHARDED)
    rs = _rows(sum(g[n].size for n in small_names) + LANES, FLAT_ROWS)
    small = _flat([loss_row] + [g[n] for n in small_names], rs)
    slot = (jnp.arange(2) == c)[:, None, None]
    last = [cols(_unperm_w_in(g["w_in_p"]), n_in), cols(_from_head_blocks(g["w_q_p"], QK_NOPE + QK_ROPE), n_q),
            cols(_join_kv(g["w_k_p"], g["w_v_p"]), n_kv), g["ab_w_out"]]
    last = [a.reshape(N_CHIPS, 2, -1, a.shape[-1]) for a in last]
    *from_sib, small_sib = _merge(_pair_swap(last), _pair_send([small])).run("tail_pair")
    pair = [_pair_add(a, b, name=f"grad_pair_add_b{i}") for i, (a, b) in enumerate(zip(last, from_sib))]
    pair_small = _sum_slots(jnp.where(slot, small[None], small_sib[None]), name="small_pair_sum")
    my_small = lax.dynamic_index_in_dim(pair_small.reshape(2, rs // 2, LANES), c, axis=0, keepdims=False)
    *arrived, all_small = _merge(_chip_exchange(pair, scatter=True), _chip_exchange([my_small], scatter=False)).run("tail_chip")
    half_of.update(zip(["in", "q", "kv", "out"], chip_sums(pair, arrived, "b")))
    half_of["small"] = _sum_slots(_put(all_small, my_small, chip, 0), name="small_chip_sum")
    keys = ("in", "q", "kv", "out", "c_in", "c_out", "gate0", "gate1", "up0", "up1", "down0", "down1", "small")
    other_half = dict(zip(keys, _pair_send([half_of[k] for k in keys]).run("grad_pair_share")))
    small_sum = jnp.where(slot, half_of["small"][None], other_half["small"][None]).reshape(rs, LANES)
    whole = lambda k: jnp.where(slot, half_of[k][None], other_half[k][None]).reshape(-1, half_of[k].shape[-1])
    grads_t = {"ab_w_in": whole("in").T[None], "ab_w_q_b": whole("q").T[None]}
    grads = {"ab_w_kv_b": whole("kv")[None], "c_w_in": whole("c_in").T[None], **{n: tr(a) for n, a in grads_t.items()}}
    by_halves = {"ab_w_out": (("out",), False), "c_w_out": (("c_out",), False), "ffn_w_down": (("down0", "down1"), False),
                 "ffn_w_gate": (("gate0", "gate1"), True), "ffn_w_up": (("up0", "up1"), True)}

    small_parts = _unflat(small_sum, [(1, LANES)] + [g[n].shape for n in small_names])
    loss = small_parts[0][0, 0]
    for n, a in zip(small_names, small_parts[1:]):
        if n in SMALL_SHARDED:
            ax = SMALL_SHARDED[n]
            a = lax.dynamic_slice_in_dim(a, chip * w[n].shape[ax], w[n].shape[ax], axis=ax)
        grads[n] = a.reshape(w[n].shape)

    delta, new_m, new_v = {}, {}, {}
    for n in BIG:
        if n in by_halves:
            ks, transposed = by_halves[n]
            view = tr if transposed else (lambda a: a)
            out = _adamw_halves(view(w[n]), view(m[n]), view(v[n]), [half_of[k] for k in ks], [other_half[k] for k in ks],
                                name=f"adamw_{n}")
            delta[n], new_m[n], new_v[n], grads[n] = (view(a) for a in out)
        elif n in grads_t:
            out = _adamw(tr(w[n]), grads_t[n], tr(m[n]), tr(v[n]), name=f"adamw_{n}")
            delta[n], new_m[n], new_v[n] = (tr(a) for a in out)
        else:
            delta[n], new_m[n], new_v[n] = _adamw(w[n], grads[n], m[n], v[n], name=f"adamw_{n}")
    small_all = [n for n in WEIGHT_NAMES if n not in BIG]
    ra = _rows(sum(w[n].size for n in small_all), FLAT_ROWS)
    pack = lambda d: _flat([d[n] for n in small_all], ra)[None]
    out = _adamw(pack(w), pack(grads), pack(m), pack(v), name="adamw_small")
    shapes = [w[n].shape for n in small_all]
    for d, flat in zip((delta, new_m, new_v), out):
        d.update(zip(small_all, _unflat(flat, shapes)))
    return (loss, grad_x, *[grads[n] for n in WEIGHT_NAMES], *[delta[n] for n in WEIGHT_NAMES],
            *[new_m[n] for n in WEIGHT_NAMES], *[new_v[n] for n in WEIGHT_NAMES])
```

```python
import functools
import math

import jax
import jax.numpy as jnp
from jax import lax
from jax.experimental import pallas as pl
from jax.experimental.pallas import tpu as pltpu

F32 = jnp.float32
BF16 = jnp.bfloat16
MESH = pl.DeviceIdType.MESH

D_MODEL = 1024
MLA_HEADS = 8
Q_LORA = 256
KV_LORA = 128
QK_NOPE = 64
QK_ROPE = 32
V_HEAD = 64
LRU_WIDTH = 512
LRU_HEADS = 8
LRU_BLOCK = 64
LRU_CONV = 4
LRU_C = 8.0
CHUNK = 128
SGU_GROUPS = 8
SGU_WIDTH = 1024
D_FF = 2816
FFN_CONV = 3
NORM_EPS = 1e-6
ROPE_BASE = 10000.0
AB_IN_PAD = 1536
ADAM_LR = 0.001
ADAM_B1 = 0.9
ADAM_B2 = 0.999
ADAM_EPS = 1e-08
ADAM_WD = 0.01
ADAM_STEP = 10

N_CHIPS = 4
LANES = 128
VMEM_LIMIT = 56 * 1024 * 1024
ROW_TILE = 256
NORM_TILE = 1024
MM_TM, MM_TN, MM_TK = 1024, 1536, 2816
MM_TM_T, MM_TK_T = 1408, 1024
GELU_C = math.sqrt(2.0 / math.pi)


def _cparams(sem):
    return pltpu.CompilerParams(dimension_semantics=sem, vmem_limit_bytes=VMEM_LIMIT)


def _tile(n, target, mult=LANES):
    t = (min(n, target) // mult) * mult
    while t >= mult:
        if n % t == 0:
            return t
        t -= mult
    return n


GELU_K = GELU_C * 0.044715


def _gelu(x):
    t = jnp.tanh(x * (GELU_C + GELU_K * (x * x)))
    hx = 0.5 * x
    return hx + hx * t


def _gelu_and_grad(x):
    x2 = x * x
    t = jnp.tanh(x * (GELU_C + GELU_K * x2))
    hx = 0.5 * x
    dg = (0.5 + 0.5 * t) + (hx * (1.0 - t * t)) * (GELU_C + (3.0 * GELU_K) * x2)
    return hx + hx * t, dg


def _sigmoid(x):
    return 1.0 / (1.0 + jnp.exp(-x))


def _shift_rows(x, d, fill_rows):
    ext = jnp.concatenate([fill_rows, x], axis=0)
    return pltpu.roll(ext, d, 0)[8:]


def _shift_rows_up(x, d, fill_rows):
    n = x.shape[0]
    ext = jnp.concatenate([x, fill_rows], axis=0)
    return pltpu.roll(ext, n + 8 - d, 0)[:n]


def _dot(a, b, dims):
    return lax.dot_general(a.astype(BF16), b.astype(BF16), (dims, ((), ())), preferred_element_type=F32)


def _dot_nn(a, b):
    return _dot(a, b, ((1,), (0,)))


def _dot_nt(a, b):
    return _dot(a, b, ((1,), (1,)))


def _dot_tn(a, b):
    return _dot(a, b, ((0,), (0,)))


def _mm(a, b, *, name, ta=False, tb=False, res=None, out_dtype=F32, ride=None):
    if ta:
        K, M = a.shape
    else:
        M, K = a.shape
    N = b.shape[0] if tb else b.shape[1]
    tm = _tile(M, MM_TM_T if ta else (MM_TM if K <= MM_TM else MM_TM // 2), LANES if ta else 8)
    tn = _tile(N, MM_TN, LANES)
    tk = _tile(K, MM_TK_T if ta else MM_TK, LANES)
    nk = K // tk
    a_spec = pl.BlockSpec((tk, tm), lambda j, i, k: (k, i)) if ta else pl.BlockSpec((tm, tk), lambda j, i, k: (i, k))
    b_spec = pl.BlockSpec((tn, tk), lambda j, i, k: (j, k)) if tb else pl.BlockSpec((tk, tn), lambda j, i, k: (k, j))
    o_spec = pl.BlockSpec((tm, tn), lambda j, i, k: (i, j))
    dims = ((0,) if ta else (1,), (1,) if tb else (0,))
    has_res = res is not None

    def body(*refs):
        a_ref, b_ref = refs[:2]
        r_ref = refs[2] if has_res else None
        o_ref = refs[3] if has_res else refs[2]
        p = _dot(a_ref[...], b_ref[...], dims)

        def finish(r):
            if has_res:
                r = r + r_ref[...].astype(F32)
            o_ref[...] = r.astype(out_dtype)

        if nk == 1:
            finish(p)
            return
        acc_ref = refs[-1]
        k = pl.program_id(2)

        @pl.when(k == 0)
        def _():
            acc_ref[...] = p

        @pl.when(jnp.logical_and(k > 0, k < nk - 1))
        def _():
            acc_ref[...] += p

        @pl.when(k == nk - 1)
        def _():
            finish(acc_ref[...] + p)

    in_specs = [a_spec, b_spec] + ([o_spec] if has_res else [])
    args = (a, b) + ((res,) if has_res else ())
    return _pcall(
        body, name=name, grid=(N // tn, M // tm, nk), in_specs=in_specs, out_specs=[o_spec],
        out_shape=[jax.ShapeDtypeStruct((M, N), out_dtype)], args=args,
        scratch=[pltpu.VMEM((tm, tn), F32)] if nk > 1 else [], sem=("parallel", "parallel", "arbitrary"), ride=ride)[0]


def _rms_fwd(x, g, *, name, cb=0, out_dtype=BF16, ride=None):
    T = x.shape[0]
    W = g.shape[-1]
    g = g.reshape(1, W)
    tt = _tile(T, NORM_TILE, 16)

    def body(x_ref, g_ref, o_ref):
        xf = x_ref[...].astype(F32)
        rstd = lax.rsqrt(jnp.mean(xf * xf, axis=-1, keepdims=True) + NORM_EPS)
        o_ref[...] = (xf * rstd * g_ref[...]).astype(out_dtype)

    return _pcall(
        body, name=name, grid=(T // tt,),
        in_specs=[pl.BlockSpec((tt, W), lambda i: (i, cb)), pl.BlockSpec((1, W), lambda i: (0, 0))],
        out_specs=[pl.BlockSpec((tt, W), lambda i: (i, 0))], out_shape=[jax.ShapeDtypeStruct((T, W), out_dtype)],
        args=(x, g), sem=("parallel",), ride=ride)[0]


def _rms_bwd(x, g, dy, *, name, cb=0, res=None, out_dtype=F32, ride=None):
    T = x.shape[0]
    W = g.shape[-1]
    g = g.reshape(1, W)
    tt = _tile(T, NORM_TILE // 2, 16)
    has_res = res is not None

    def body(*refs):
        if has_res:
            x_ref, g_ref, dy_ref, r_ref, dx_ref, dg_ref = refs
        else:
            x_ref, g_ref, dy_ref, dx_ref, dg_ref = refs
        xf = x_ref[...].astype(F32)
        dyf = dy_ref[...].astype(F32)
        rstd = lax.rsqrt(jnp.mean(xf * xf, axis=-1, keepdims=True) + NORM_EPS)
        xhat = xf * rstd
        dxhat = dyf * g_ref[...]
        dx = rstd * (dxhat - xhat * jnp.mean(dxhat * xhat, axis=-1, keepdims=True))
        if has_res:
            dx = dx + r_ref[...].astype(F32)
        dx_ref[...] = dx.astype(out_dtype)
        part = jnp.sum(dyf * xhat, axis=0, keepdims=True)

        @pl.when(pl.program_id(0) == 0)
        def _():
            dg_ref[...] = part

        @pl.when(pl.program_id(0) > 0)
        def _():
            dg_ref[...] += part

    row = pl.BlockSpec((tt, W), lambda i: (i, 0))
    in_specs = [pl.BlockSpec((tt, W), lambda i: (i, cb)), pl.BlockSpec((1, W), lambda i: (0, 0)), row]
    args = (x, g, dy)
    if has_res:
        in_specs.append(row)
        args = args + (res,)
    return _pcall(
        body, name=name, grid=(T // tt,), in_specs=in_specs,
        out_specs=[row, pl.BlockSpec((1, W), lambda i: (0, 0))],
        out_shape=[jax.ShapeDtypeStruct((T, W), out_dtype), jax.ShapeDtypeStruct((1, W), F32)], args=args, ride=ride)


def _final_fwd_bwd(h, g, target, *, name):
    T, W = h.shape
    g = g.reshape(1, W)
    tt = _tile(T, NORM_TILE, 16)

    def body(x_ref, g_ref, t_ref, loss_ref, dx_ref, dg_ref):
        xf = x_ref[...]
        rstd = lax.rsqrt(jnp.mean(xf * xf, axis=-1, keepdims=True) + NORM_EPS)
        xhat = xf * rstd
        err = xhat * g_ref[...] - t_ref[...]
        lpart = jnp.zeros((1, LANES), F32) + (0.5 / W) * jnp.sum(err * err)
        dyf = err * (1.0 / W)
        dxhat = dyf * g_ref[...]
        dx_ref[...] = rstd * (dxhat - xhat * jnp.mean(dxhat * xhat, axis=-1, keepdims=True))
        part = jnp.sum(dyf * xhat, axis=0, keepdims=True)

        @pl.when(pl.program_id(0) == 0)
        def _():
            dg_ref[...] = part
            loss_ref[...] = lpart

        @pl.when(pl.program_id(0) > 0)
        def _():
            dg_ref[...] += part
            loss_ref[...] += lpart

    row = pl.BlockSpec((tt, W), lambda i: (i, 0))
    return pl.pallas_call(
        body, name=name, grid=(T // tt,),
        in_specs=[row, pl.BlockSpec((1, W), lambda i: (0, 0)), row],
        out_specs=[pl.BlockSpec((1, LANES), lambda i: (0, 0)), row, pl.BlockSpec((1, W), lambda i: (0, 0))],
        out_shape=[jax.ShapeDtypeStruct((1, LANES), F32), jax.ShapeDtypeStruct((T, W), F32),
                   jax.ShapeDtypeStruct((1, W), F32)],
        compiler_params=_cparams(("arbitrary",)),
    )(h, g, target)


def _swap16(x):
    lane = lax.broadcasted_iota(jnp.int32, x.shape, 1)
    return jnp.where((lane % 32) < 16, pltpu.roll(x, LANES - 16, 1), pltpu.roll(x, 16, 1))


def _rope(x, c, s):
    return x * c + _swap16(x) * s


def _rope_t(d, c, s):
    return d * c + _swap16(d * s)


def _head_block_map(fn, x, cos, sin, *, name):
    T, W = x.shape
    tt = _tile(T, NORM_TILE, 16)

    def body(x_ref, c_ref, s_ref, o_ref):
        c, s = c_ref[...], s_ref[...]
        for h in range(W // LANES):
            lanes = slice(h * LANES, (h + 1) * LANES)
            o_ref[:, lanes] = fn(x_ref[:, lanes], c, s).astype(BF16)

    tab = pl.BlockSpec((tt, LANES), lambda i: (i, 0))
    blk = pl.BlockSpec((tt, W), lambda i: (i, 0))
    return pl.pallas_call(
        body, name=name, grid=(T // tt,), in_specs=[blk, tab, tab], out_specs=blk,
        out_shape=jax.ShapeDtypeStruct((T, W), BF16), compiler_params=_cparams(("parallel",)),
    )(x, cos, sin)


def _rope_q(q, cos, sin, *, name):
    scale = _attn_scale()
    return _head_block_map(lambda x, c, s: _rope(x, c, s) * scale, q, cos, sin, name=name)


def _rope_q_bwd(dq, cos, sin, *, name):
    return _head_block_map(_rope_t, dq, cos, sin, name=name)


def _key_blocks(kv, z, cos, sin, *, kpe_block, name):
    T = kv.shape[0]
    tt = _tile(T, NORM_TILE, 16)
    W = MLA_HEADS * LANES

    def body(kv_ref, z_ref, c_ref, s_ref, o_ref):
        kr = _rope(z_ref[...], c_ref[...], s_ref[...])
        for h in range(MLA_HEADS):
            lanes = slice(h * LANES, (h + 1) * LANES)
            o_ref[:, lanes] = (kv_ref[:, lanes].astype(F32) + kr).astype(BF16)

    tab = pl.BlockSpec((tt, LANES), lambda i: (i, 0))
    blk = pl.BlockSpec((tt, W), lambda i: (i, 0))
    return pl.pallas_call(
        body, name=name, grid=(T // tt,),
        in_specs=[blk, pl.BlockSpec((tt, LANES), lambda i: (i, kpe_block)), tab, tab], out_specs=blk,
        out_shape=jax.ShapeDtypeStruct((T, W), BF16), compiler_params=_cparams(("parallel",)),
    )(kv, z, cos, sin)


def _key_rope_bwd(dk, cos, sin, *, name):
    T = dk.shape[0]
    tt = _tile(T, NORM_TILE, 16)

    def body(d_ref, c_ref, s_ref, o_ref):
        d = d_ref[:, :LANES]
        for h in range(1, MLA_HEADS):
            d = d + d_ref[:, h * LANES:(h + 1) * LANES]
        lane = lax.broadcasted_iota(jnp.int32, d.shape, 1)
        d = jnp.where(jnp.logical_and(lane >= QK_NOPE, lane < QK_NOPE + QK_ROPE), d, 0.0)
        o_ref[...] = _rope_t(d, c_ref[...], s_ref[...]).astype(BF16)

    tab = pl.BlockSpec((tt, LANES), lambda i: (i, 0))
    return pl.pallas_call(
        body, name=name, grid=(T // tt,),
        in_specs=[pl.BlockSpec((tt, MLA_HEADS * LANES), lambda i: (i, 0)), tab, tab], out_specs=tab,
        out_shape=jax.ShapeDtypeStruct((T, LANES), BF16), compiler_params=_cparams(("parallel",)),
    )(dk, cos, sin)


ATT_BLOCK = 512


def _attn_scale():
    return float((QK_NOPE + QK_ROPE) ** -0.5)


def _causal_mask(qi, kj, tq, tk):
    row = qi * tq + lax.broadcasted_iota(jnp.int32, (tq, tk), 0)
    col = kj * tk + lax.broadcasted_iota(jnp.int32, (tq, tk), 1)
    return col <= row


def _pcall(body, *, name, grid, in_specs, out_specs, out_shape, args, scratch=(), sem=None, ride=None):
    n_in, n_out, n_scr = len(args), len(out_shape), len(scratch)
    if ride is None:
        return pl.pallas_call(
            body, name=name, grid=grid, in_specs=list(in_specs), out_specs=list(out_specs), out_shape=list(out_shape),
            scratch_shapes=list(scratch), compiler_params=_cparams(sem or ("arbitrary",) * len(grid)))(*args)
    ex, sink = ride
    o0 = n_in + len(ex.arrs)
    s0 = o0 + n_out + len(ex.out_shapes)

    def hosted(*refs):
        parts = (refs[n_in:o0], refs[o0 + n_out:s0], refs[-2], refs[-1])
        ids = [pl.program_id(i) for i in range(len(grid))]
        pl.when(functools.reduce(jnp.logical_and, [i == 0 for i in ids]))(lambda: ex.start(*parts))
        body(*refs[:n_in], *refs[o0:o0 + n_out], *refs[s0:s0 + n_scr])
        pl.when(functools.reduce(jnp.logical_and, [i == n - 1 for i, n in zip(ids, grid)]))(lambda: ex.finish(*parts))

    outs = pl.pallas_call(
        hosted, name=name, grid=grid, in_specs=list(in_specs) + ex.in_specs, out_specs=list(out_specs) + ex.out_specs,
        out_shape=list(out_shape) + ex.out_shapes, scratch_shapes=list(scratch) + ex.scratch,
        compiler_params=_cparams(("arbitrary",) * len(grid)))(*args, *ex.arrs)
    sink(outs[n_out:])
    return outs[:n_out]


PAIRS = MLA_HEADS // 2


def _own_lanes(x, first):
    lane = lax.broadcasted_iota(jnp.int32, x.shape, 1)
    return jnp.where((lane < V_HEAD) if first else (lane >= V_HEAD), x, 0.0)


def _lane_sums_as_row(x):
    hi = x.astype(BF16)
    lo = (x - hi.astype(F32)).astype(BF16)
    ones = jnp.ones((8, LANES), BF16)
    return (_dot_nt(ones, hi) + _dot_nt(ones, lo))[0:1, :]


def _attn_fwd(q, k, kv, *, B, S, v_block0, name, ride=None):
    tq = tk = min(ATT_BLOCK, S)
    nq = S // tq
    T = B * S

    def body(q_ref, k_ref, v_ref, o_ref, lse_ref):
        qi = pl.program_id(2)
        qs = (q_ref[:, :LANES], q_ref[:, LANES:])

        def step(masked):
            def f(j, carry):
                rows = pl.ds(pl.multiple_of(j * tk, tk), tk)
                vb = v_ref[rows, :]
                out = []
                for h in range(2):
                    m, l, acc = carry[h]
                    s = _dot_nt(qs[h], k_ref[rows, h * LANES:(h + 1) * LANES])
                    if masked:
                        s = jnp.where(_causal_mask(qi, j, tq, tk), s, -jnp.inf)
                    m_new = jnp.maximum(m, jnp.max(s, axis=-1, keepdims=True))
                    alpha = jnp.exp(m - m_new)
                    p = jnp.exp(s - m_new)
                    out.append((m_new, alpha * l + jnp.sum(p, axis=-1, keepdims=True), alpha * acc + _dot_nn(p, vb)))
                return tuple(out)
            return f

        one = (jnp.full((tq, 1), -1e30, F32), jnp.zeros((tq, 1), F32), jnp.zeros((tq, LANES), F32))
        (ma, la, acca), (mb, lb, accb) = step(True)(qi, lax.fori_loop(0, qi, step(False), (one, one)))
        o_ref[...] = _own_lanes(acca / la, True) + _own_lanes(accb / lb, False)
        for h, lse in enumerate((ma + jnp.log(la), mb + jnp.log(lb))):
            lse_ref[0, h, pl.ds(qi, 1), :] = _lane_sums_as_row(jnp.broadcast_to(lse * (1.0 / LANES), (tq, LANES)))

    return _pcall(
        body, name=name, grid=(B, PAIRS, nq),
        in_specs=[pl.BlockSpec((tq, 2 * LANES), lambda b, g, i: (b * nq + i, g)),
                  pl.BlockSpec((S, 2 * LANES), lambda b, g, i: (b, g)),
                  pl.BlockSpec((S, LANES), lambda b, g, i: (b, v_block0 + g))],
        out_specs=[pl.BlockSpec((tq, LANES), lambda b, g, i: (b * nq + i, g)),
                   pl.BlockSpec((1, 2, nq, tq), lambda b, g, i: (b, g, 0, 0))],
        out_shape=[jax.ShapeDtypeStruct((T, PAIRS * LANES), F32), jax.ShapeDtypeStruct((B, MLA_HEADS, nq, tq), F32)],
        args=(q, k, kv), ride=ride)


def _attn_bwd(q, k, kv, o, lse_rows, do, *, B, S, v_block0, name, ride=None):
    tq = tk = min(ATT_BLOCK, S)
    nq = S // tq
    T = B * S
    scale = _attn_scale()

    def body(q_ref, k_ref, v_ref, o_ref, lse_ref, do_ref, dk_ref, dv_ref, dq_ref, delta_ref):
        kj = pl.program_id(2)
        ks = (k_ref[:, :LANES], k_ref[:, LANES:])
        vb = v_ref[...]

        @pl.when(kj == 0)
        def _():
            dq_ref[...] = jnp.zeros_like(dq_ref)
            for i in range(nq):
                prod = do_ref[i * tq:(i + 1) * tq, :] * o_ref[i * tq:(i + 1) * tq, :]
                for h in range(2):
                    delta_ref[h, i:i + 1, :] = _lane_sums_as_row(_own_lanes(prod, h == 0))

        def step(masked):
            def f(i, carry):
                rows = pl.ds(pl.multiple_of(i * tq, tq), tq)
                do_b = do_ref[rows, :]
                dks, dv = list(carry[:2]), carry[2]
                for h in range(2):
                    qb = q_ref[rows, h * LANES:(h + 1) * LANES]
                    doh = _own_lanes(do_b, h == 0)
                    pt = jnp.exp(_dot_nt(ks[h], qb) - lse_ref[0, h, pl.ds(i, 1), :])
                    if masked:
                        krow = kj * tk + lax.broadcasted_iota(jnp.int32, (tk, tq), 0)
                        qcol = i * tq + lax.broadcasted_iota(jnp.int32, (tk, tq), 1)
                        pt = jnp.where(krow <= qcol, pt, 0.0)
                    dst = pt * (_dot_nt(vb, doh) - delta_ref[h, pl.ds(i, 1), :])
                    dks[h] = dks[h] + _dot_nn(dst, qb)
                    dv = dv + _dot_nn(pt, doh)
                    dq_ref[rows, h * LANES:(h + 1) * LANES] += _dot_tn(dst, ks[h]) * scale
                return dks[0], dks[1], dv
            return f

        zero = jnp.zeros((tk, LANES), F32)
        dka, dkb, dv = lax.fori_loop(kj + 1, nq, step(False), step(True)(kj, (zero, zero, zero)))
        dk_ref[:, :LANES] = dka
        dk_ref[:, LANES:] = dkb
        dv_ref[...] = dv

    krow = lambda w, c0: pl.BlockSpec((tk, w), lambda b, g, j: (b * nq + j, c0 + g))
    seq = lambda w: pl.BlockSpec((S, w), lambda b, g, j: (b, g))
    stat = pl.BlockSpec((1, 2, nq, tq), lambda b, g, j: (b, g, 0, 0))
    dk, dv, dq = _pcall(
        body, name=name, grid=(B, PAIRS, nq),
        in_specs=[seq(2 * LANES), krow(2 * LANES, 0), krow(LANES, v_block0), seq(LANES), stat, seq(LANES)],
        out_specs=[krow(2 * LANES, 0), krow(LANES, 0), seq(2 * LANES)],
        out_shape=[jax.ShapeDtypeStruct((T, MLA_HEADS * LANES), F32), jax.ShapeDtypeStruct((T, PAIRS * LANES), F32),
                   jax.ShapeDtypeStruct((T, MLA_HEADS * LANES), F32)],
        args=(q, k, kv, o, lse_rows, do), scratch=[pltpu.VMEM((2, nq, tq), F32)], ride=ride)
    return dq, dk, dv


def _lru_gates(xl, halo, cw_ref, cb_ref, wa_ref, ba_ref, wx_ref, bx_ref, lam_ref):
    xc = cb_ref[...] + cw_ref[3:4, :] * xl
    for kk in range(LRU_CONV - 1):
        xc = xc + cw_ref[kk:kk + 1, :] * _shift_rows(xl, LRU_CONV - 1 - kk, halo)
    r = _sigmoid(_dot_nn(xc, wa_ref[...]) + ba_ref[...])
    i = _sigmoid(_dot_nn(xc, wx_ref[...]) + bx_ref[...])
    lam = lam_ref[...]
    sp = jnp.maximum(-lam, 0.0) + jnp.log(1.0 + jnp.exp(-jnp.abs(lam)))
    a = jnp.exp(-LRU_C * r * sp)
    mult = jnp.sqrt(1.0 - a * a)
    return xc, r, i, sp, a, mult


def _lru_specs(tt, nt, S):
    def make(rev):
        tmap = (lambda t: nt - 1 - t) if rev else (lambda t: t)
        tile = lambda cb: pl.BlockSpec((tt, LRU_WIDTH), lambda b, t: (b * nt + tmap(t), cb))
        prev8 = lambda cb: pl.BlockSpec(
            (8, LRU_WIDTH), lambda b, t: (jnp.maximum((b * nt + tmap(t)) * (tt // 8) - 1, 0), cb))
        return tile, prev8, tmap
    return make


def _lru_fwd(z, cw, cb, wa, ba, wx, bx, lam, *, S, name, ride=None):
    T = z.shape[0]
    tt = min(ROW_TILE, S)
    nt = S // tt
    tile, prev8, _ = _lru_specs(tt, nt, S)(False)
    vec = lambda r: pl.BlockSpec((r, LRU_WIDTH), lambda b, t: (0, 0))
    mat = pl.BlockSpec((LRU_WIDTH, LRU_WIDTH), lambda b, t: (0, 0))

    def body(xl_ref, halo_ref, gate_ref, cw_ref, cb_ref, wa_ref, ba_ref, wx_ref, bx_ref, lam_ref,
             y_ref, h_ref, carry_ref):
        t = pl.program_id(1)
        first = t == 0
        halo = jnp.where(first, 0.0, halo_ref[...])
        xl_t = xl_ref[...]
        xc, r, i, sp, a, mult = _lru_gates(xl_t, halo, cw_ref, cb_ref, wa_ref, ba_ref, wx_ref, bx_ref, lam_ref)
        bv = mult * (i * xc)
        ones = jnp.ones((8, LRU_WIDTH), F32)
        zeros = jnp.zeros((8, LRU_WIDTH), F32)
        row = lax.broadcasted_iota(jnp.int32, (tt, LRU_WIDTH), 0)
        A = a
        d = 1
        while d < tt:
            if d < 8:
                a_sh = _shift_rows(A, d, ones)
                b_sh = _shift_rows(bv, d, zeros)
            else:
                a_sh = jnp.where(row < d, 1.0, pltpu.roll(A, d, 0))
                b_sh = jnp.where(row < d, 0.0, pltpu.roll(bv, d, 0))
            bv = A * b_sh + bv
            A = A * a_sh
            d *= 2
        h0 = jnp.where(first, 0.0, carry_ref[0:1, :])
        h = A * h0 + bv
        carry_ref[...] = jnp.broadcast_to(h[tt - 1:tt, :], (8, LRU_WIDTH))
        h_ref[...] = h
        y_ref[...] = (h * _gelu(gate_ref[...])).astype(BF16)

    return _pcall(
        body, name=name, grid=(T // S, nt),
        in_specs=[tile(0), prev8(0), tile(1), vec(LRU_CONV), vec(1), mat, vec(1), mat, vec(1), vec(1)],
        out_specs=[tile(0), tile(0)],
        out_shape=[jax.ShapeDtypeStruct((T, LRU_WIDTH), BF16), jax.ShapeDtypeStruct((T, LRU_WIDTH), F32)],
        args=(z, z, z, cw, cb, wa, ba, wx, bx, lam), scratch=[pltpu.VMEM((8, LRU_WIDTH), F32)], ride=ride)


def _lru_bwd(z, h, dy, cw, cb, wa, ba, wx, bx, lam, *, S, name):
    T = z.shape[0]
    tt = min(ROW_TILE, S)
    nt = S // tt
    tile, prev8, tmap = _lru_specs(tt, nt, S)(True)
    vec = lambda r: pl.BlockSpec((r, LRU_WIDTH), lambda b, t: (0, 0))
    mat = pl.BlockSpec((LRU_WIDTH, LRU_WIDTH), lambda b, t: (0, 0))

    def body(xl_ref, halo_ref, gate_ref, h_ref, hprev_ref, dy_ref, cw_ref, cb_ref, wa_ref, ba_ref, wx_ref,
             bx_ref, lam_ref, dxl_ref, dgate_ref, dcw_ref, dcb_ref, dwa_ref, dba_ref, dwx_ref, dbx_ref,
             dlam_ref, lamc_ref, ac_ref, dxc_ref):
        b = pl.program_id(0)
        t = pl.program_id(1)
        tr = nt - 1 - t
        seq_first = tr == 0
        seq_last = t == 0
        halo = jnp.where(seq_first, 0.0, halo_ref[...])
        xl_t = xl_ref[...]
        xc, r, i, sp, a, mult = _lru_gates(xl_t, halo, cw_ref, cb_ref, wa_ref, ba_ref, wx_ref, bx_ref, lam_ref)
        hh = h_ref[...]
        dyf = dy_ref[...].astype(F32)
        gl, dgl = _gelu_and_grad(gate_ref[...])
        dgate_ref[...] = (dyf * hh * dgl).astype(BF16)
        dh = dyf * gl

        a_first_later = jnp.where(seq_last, 0.0, ac_ref[...])
        lam_later = jnp.where(seq_last, 0.0, lamc_ref[...])
        row = lax.broadcasted_iota(jnp.int32, (tt, LRU_WIDTH), 0)
        A = _shift_rows_up(a, 1, a_first_later)
        lm = dh
        ones = jnp.ones((8, LRU_WIDTH), F32)
        zeros = jnp.zeros((8, LRU_WIDTH), F32)
        d = 1
        while d < tt:
            if d < 8:
                a_sh = _shift_rows_up(A, d, ones)
                l_sh = _shift_rows_up(lm, d, zeros)
            else:
                a_sh = jnp.where(row >= tt - d, 1.0, pltpu.roll(A, tt - d, 0))
                l_sh = jnp.where(row >= tt - d, 0.0, pltpu.roll(lm, tt - d, 0))
            lm = lm + A * l_sh
            A = A * a_sh
            d *= 2
        lm = lm + A * lam_later[0:1, :]
        lamc_ref[...] = jnp.broadcast_to(lm[0:1, :], (8, LRU_WIDTH))
        ac_ref[...] = jnp.broadcast_to(a[0:1, :], (8, LRU_WIDTH))

        hprev_halo = jnp.where(seq_first, 0.0, hprev_ref[...])
        h_prev = _shift_rows(hh, 1, hprev_halo)
        da = lm * h_prev
        ixc = i * xc
        dmult = lm * ixc
        di = lm * mult * xc
        dxc = lm * mult * i
        da = da - dmult * a / mult
        dlog = da * a
        dr = dlog * (-LRU_C) * sp
        dsp_part = jnp.sum(dlog * (-LRU_C) * r, axis=0, keepdims=True)
        dpa = dr * r * (1.0 - r)
        dpx = di * i * (1.0 - i)
        dxc = dxc + _dot_nt(dpa, wa_ref[...]) + _dot_nt(dpx, wx_ref[...])
        dwa_part = _dot_tn(xc, dpa)
        dwx_part = _dot_tn(xc, dpx)

        later = jnp.where(seq_last, 0.0, dxc_ref[...])
        dxl = cw_ref[3:4, :] * dxc
        for kk in range(LRU_CONV - 1):
            dxl = dxl + cw_ref[kk:kk + 1, :] * _shift_rows_up(dxc, LRU_CONV - 1 - kk, later)
        dxl_ref[...] = dxl.astype(BF16)
        dxc_ref[...] = dxc[0:8, :]
        dcw_rows = [jnp.sum(dxc * _shift_rows(xl_t, LRU_CONV - 1 - kk, halo), axis=0, keepdims=True)
                    for kk in range(LRU_CONV - 1)]
        dcw_rows.append(jnp.sum(dxc * xl_t, axis=0, keepdims=True))
        dcw_part = jnp.concatenate(dcw_rows + [jnp.zeros((8 - LRU_CONV, LRU_WIDTH), F32)], axis=0)
        lamv = lam_ref[...]
        dlam_part = dsp_part * (-_sigmoid(-lamv))
        parts = ((dcw_ref, dcw_part), (dcb_ref, jnp.sum(dxc, axis=0, keepdims=True)),
                 (dwa_ref, dwa_part), (dba_ref, jnp.sum(dpa, axis=0, keepdims=True)),
                 (dwx_ref, dwx_part), (dbx_ref, jnp.sum(dpx, axis=0, keepdims=True)),
                 (dlam_ref, dlam_part))
        start = jnp.logical_and(b == 0, t == 0)

        @pl.when(start)
        def _():
            for ref, val in parts:
                ref[...] = val

        @pl.when(jnp.logical_not(start))
        def _():
            for ref, val in parts:
                ref[...] += val

    acc = lambda r: pl.BlockSpec((r, LRU_WIDTH), lambda b, t: (0, 0))
    return pl.pallas_call(
        body, name=name, grid=(T // S, nt),
        in_specs=[tile(0), prev8(0), tile(1), tile(0), prev8(0), tile(0),
                  vec(LRU_CONV), vec(1), mat, vec(1), mat, vec(1), vec(1)],
        out_specs=[tile(0), tile(0), acc(8), acc(1), mat, acc(1), mat, acc(1), acc(1)],
        out_shape=[jax.ShapeDtypeStruct((T, LRU_WIDTH), BF16), jax.ShapeDtypeStruct((T, LRU_WIDTH), BF16),
                   jax.ShapeDtypeStruct((8, LRU_WIDTH), F32), jax.ShapeDtypeStruct((1, LRU_WIDTH), F32),
                   jax.ShapeDtypeStruct((LRU_WIDTH, LRU_WIDTH), F32), jax.ShapeDtypeStruct((1, LRU_WIDTH), F32),
                   jax.ShapeDtypeStruct((LRU_WIDTH, LRU_WIDTH), F32), jax.ShapeDtypeStruct((1, LRU_WIDTH), F32),
                   jax.ShapeDtypeStruct((1, LRU_WIDTH), F32)],
        scratch_shapes=[pltpu.VMEM((8, LRU_WIDTH), F32), pltpu.VMEM((8, LRU_WIDTH), F32),
                        pltpu.VMEM((8, LRU_WIDTH), F32)],
        compiler_params=_cparams(("arbitrary", "arbitrary")),
    )(z, z, z, h, h, dy, cw, cb, wa, ba, wx, bx, lam)


FFN_CT = 1408
FFN_TILE = 512


def _ffn_conv(g, halo, cw, cb):
    gc = cb + cw[2:3, :] * g
    for kk in range(FFN_CONV - 1):
        gc = gc + cw[kk:kk + 1, :] * _shift_rows(g, FFN_CONV - 1 - kk, halo)
    return gc


def _row_chunks(rows, chunk):
    return [slice(r0, min(r0 + chunk, rows)) for r0 in range(0, rows, chunk)]


FFN_CHUNK = 128
HALO = 16


def _ffn_act_down(g, u, cw, cb, w_down, res, *, S, name, ride=None):
    T, F = g.shape
    D = w_down.shape[1]
    tt = min(FFN_TILE, S)
    nt = S // tt
    tc = _tile(F, FFN_CT)
    nj = F // tc

    def body(g_ref, halo_ref, u_ref, cw_ref, cb_ref, w_ref, r_ref, o_ref, act_ref):
        j = pl.program_id(1)
        first = (pl.program_id(0) % nt) == 0
        cw, cb = cw_ref[...], cb_ref[...]

        @pl.when(j == 0)
        def _():
            o_ref[...] = r_ref[...]

        for r in _row_chunks(tt, FFN_CHUNK):
            before = halo_ref[...] if r.start == 0 else g_ref[r.start - HALO:r.start, :]
            halo = before.astype(F32)[HALO - 8:]
            if r.start == 0:
                halo = jnp.where(first, 0.0, halo)
            gc = _ffn_conv(g_ref[r, :].astype(F32), halo, cw, cb)
            act = (_gelu(gc) * u_ref[r, :].astype(F32)).astype(BF16)
            act_ref[r, :] = act
            o_ref[r, :] += _dot_nn(act, w_ref[...])

    tile = pl.BlockSpec((tt, tc), lambda i, j: (i, j))
    prev = pl.BlockSpec((HALO, tc), lambda i, j: (jnp.maximum(i * (tt // HALO) - 1, 0), j))
    rows = pl.BlockSpec((tt, D), lambda i, j: (i, 0))
    return _pcall(
        body, name=name, grid=(T // tt, nj),
        in_specs=[tile, prev, tile, pl.BlockSpec((FFN_CONV, tc), lambda i, j: (0, j)),
                  pl.BlockSpec((1, tc), lambda i, j: (0, j)), pl.BlockSpec((tc, D), lambda i, j: (j, 0)), rows],
        out_specs=[rows, tile], out_shape=[jax.ShapeDtypeStruct((T, D), F32), jax.ShapeDtypeStruct((T, F), BF16)],
        args=(g, g, u, cw, cb, w_down, res), sem=("parallel", "arbitrary"), ride=ride)


def _ffn_act_bwd(g, u, dh, w_down, cw, cb, *, S, name, ride=None):
    T, F = g.shape
    D = w_down.shape[1]
    tt = min(FFN_TILE, S)
    nt = S // tt
    ntt = T // tt
    tc = _tile(F, FFN_CT)

    def body(g_ref, halo_ref, u_ref, dh_ref, w_ref, cw_ref, cb_ref, dg_ref, du_ref, dcw_ref, dcb_ref, later_ref):
        step = pl.program_id(1)
        ti = (ntt - 1 - step) % nt
        cw, cb = cw_ref[...], cb_ref[...]

        @pl.when(step == 0)
        def _():
            dcw_ref[...] = jnp.zeros_like(dcw_ref)
            dcb_ref[...] = jnp.zeros_like(dcb_ref)

        halo = jnp.where(ti == 0, 0.0, halo_ref[...].astype(F32)[HALO - 8:])
        gt = g_ref[...].astype(F32)
        gl, dgl = _gelu_and_grad(_ffn_conv(gt, halo, cw, cb))
        da = _dot_nt(dh_ref[...], w_ref[...])
        du_ref[...] = (da * gl).astype(BF16)
        dgc = da * u_ref[...].astype(F32) * dgl
        later = jnp.where(ti == nt - 1, 0.0, later_ref[...])
        dg = cw[2:3, :] * dgc
        for kk in range(FFN_CONV - 1):
            dg = dg + cw[kk:kk + 1, :] * _shift_rows_up(dgc, FFN_CONV - 1 - kk, later)
        dg_ref[...] = dg.astype(BF16)
        later_ref[...] = dgc[0:8, :]
        rows = [jnp.sum(dgc * _shift_rows(gt, FFN_CONV - 1 - kk, halo), axis=0, keepdims=True)
                for kk in range(FFN_CONV - 1)]
        rows.append(jnp.sum(dgc * gt, axis=0, keepdims=True))
        dcw_ref[...] += jnp.concatenate(rows + [jnp.zeros((8 - FFN_CONV, tc), F32)], axis=0)
        dcb_ref[...] += jnp.sum(dgc, axis=0, keepdims=True)

    tile = pl.BlockSpec((tt, tc), lambda j, s: (ntt - 1 - s, j))
    prev = pl.BlockSpec((HALO, tc), lambda j, s: (jnp.maximum((ntt - 1 - s) * (tt // HALO) - 1, 0), j))
    return _pcall(
        body, name=name, grid=(F // tc, ntt),
        in_specs=[tile, prev, tile, pl.BlockSpec((tt, D), lambda j, s: (ntt - 1 - s, 0)),
                  pl.BlockSpec((tc, D), lambda j, s: (j, 0)), pl.BlockSpec((FFN_CONV, tc), lambda j, s: (0, j)),
                  pl.BlockSpec((1, tc), lambda j, s: (0, j))],
        out_specs=[tile, tile, pl.BlockSpec((8, tc), lambda j, s: (0, j)), pl.BlockSpec((1, tc), lambda j, s: (0, j))],
        out_shape=[jax.ShapeDtypeStruct((T, F), BF16), jax.ShapeDtypeStruct((T, F), BF16),
                   jax.ShapeDtypeStruct((8, F), F32), jax.ShapeDtypeStruct((1, F), F32)],
        args=(g, g, u, dh, w_down, cw, cb), scratch=[pltpu.VMEM((8, tc), F32)], ride=ride)


def _sgu_norm(zv, g_ref, b_ref):
    v = _gelu(zv)
    mu = jnp.mean(v, axis=-1, keepdims=True)
    xc = v - mu
    rstd = lax.rsqrt(jnp.mean(xc * xc, axis=-1, keepdims=True) + NORM_EPS)
    xhat = xc * rstd
    return xhat, rstd, xhat * g_ref[...] + b_ref[...]


def _sgu_fwd(zc, ln_g, ln_b, wm, bmap, *, name):
    T = zc.shape[0]
    W = SGU_WIDTH
    tt = ROW_TILE
    nch = tt // CHUNK

    def body(z_ref, g_ref, b_ref, wm_ref, bm_ref, p_ref):
        u = _gelu(z_ref[:, :W])
        _, _, vn = _sgu_norm(z_ref[:, W:], g_ref, b_ref)
        vn = vn.astype(BF16)
        for n in range(nch):
            rows = slice(n * CHUNK, (n + 1) * CHUNK)
            for gi in range(SGU_GROUPS):
                cols = slice(gi * LANES, (gi + 1) * LANES)
                s = _dot_nn(wm_ref[gi], vn[rows, cols]) + bm_ref[:, cols]
                p_ref[rows, cols] = (u[rows, cols] * s).astype(BF16)

    const2 = lambda r, c: pl.BlockSpec((r, c), lambda i: (0, 0))
    return pl.pallas_call(
        body, name=name, grid=(T // tt,),
        in_specs=[pl.BlockSpec((tt, 2 * W), lambda i: (i, 0)), const2(1, W), const2(1, W),
                  pl.BlockSpec((SGU_GROUPS, CHUNK, CHUNK), lambda i: (0, 0, 0)), const2(CHUNK, W)],
        out_specs=pl.BlockSpec((tt, W), lambda i: (i, 0)),
        out_shape=jax.ShapeDtypeStruct((T, W), BF16),
        compiler_params=_cparams(("parallel",)),
    )(zc, ln_g, ln_b, wm, bmap)


def _sgu_bwd(zc, dp, ln_g, ln_b, wm, bmap, *, name, ride=None):
    T = zc.shape[0]
    W = SGU_WIDTH
    tt = ROW_TILE
    nch = tt // CHUNK
    nsteps = T // tt

    def body(z_ref, dp_ref, g_ref, b_ref, wm_ref, bm_ref, dz_ref, dg_ref, db_ref, dwm_ref, dbm_ref,
             s_scr, dvn_scr):
        step = pl.program_id(0)
        zu = z_ref[:, :W]
        zv = z_ref[:, W:]
        u, dgu = _gelu_and_grad(zu)
        xhat, rstd, vn = _sgu_norm(zv, g_ref, b_ref)
        vnb = vn.astype(BF16)
        dpf = dp_ref[...].astype(F32)
        ds = dpf * u

        @pl.when(step == 0)
        def _():
            dwm_ref[...] = jnp.zeros_like(dwm_ref)
            dbm_ref[...] = jnp.zeros_like(dbm_ref)

        for n in range(nch):
            rows = slice(n * CHUNK, (n + 1) * CHUNK)
            for gi in range(SGU_GROUPS):
                cols = slice(gi * LANES, (gi + 1) * LANES)
                s_scr[rows, cols] = _dot_nn(wm_ref[gi], vnb[rows, cols]) + bm_ref[:, cols]
                dsb = ds[rows, cols]
                dvn_scr[rows, cols] = _dot_tn(wm_ref[gi], dsb)
                dwm_ref[gi] += _dot_nt(dsb, vnb[rows, cols])
                dbm_ref[:, cols] += dsb
        dz_ref[:, :W] = (dpf * s_scr[...] * dgu).astype(BF16)
        dvn = dvn_scr[...]
        dxhat = dvn * g_ref[...]
        dv = rstd * (dxhat - jnp.mean(dxhat, axis=-1, keepdims=True)
                     - xhat * jnp.mean(dxhat * xhat, axis=-1, keepdims=True))
        _, dgv = _gelu_and_grad(zv)
        dz_ref[:, W:] = (dv * dgv).astype(BF16)
        dg_part = jnp.sum(dvn * xhat, axis=0, keepdims=True)
        db_part = jnp.sum(dvn, axis=0, keepdims=True)

        @pl.when(step == 0)
        def _():
            dg_ref[...] = dg_part
            db_ref[...] = db_part

        @pl.when(step > 0)
        def _():
            dg_ref[...] += dg_part
            db_ref[...] += db_part

        @pl.when(step == nsteps - 1)
        def _():
            for gi in range(SGU_GROUPS):
                cols = slice(gi * LANES, (gi + 1) * LANES)
                tot = jnp.sum(dbm_ref[:, cols], axis=1, keepdims=True)
                dbm_ref[:, cols] = jnp.broadcast_to(tot, (CHUNK, LANES))

    const2 = lambda r, c: pl.BlockSpec((r, c), lambda i: (0, 0))
    wspec = pl.BlockSpec((SGU_GROUPS, CHUNK, CHUNK), lambda i: (0, 0, 0))
    return _pcall(
        body, name=name, grid=(nsteps,),
        in_specs=[pl.BlockSpec((tt, 2 * W), lambda i: (i, 0)), pl.BlockSpec((tt, W), lambda i: (i, 0)),
                  const2(1, W), const2(1, W), wspec, const2(CHUNK, W)],
        out_specs=[pl.BlockSpec((tt, 2 * W), lambda i: (i, 0)), const2(1, W), const2(1, W), wspec, const2(CHUNK, W)],
        out_shape=[jax.ShapeDtypeStruct((T, 2 * W), BF16), jax.ShapeDtypeStruct((1, W), F32),
                   jax.ShapeDtypeStruct((1, W), F32), jax.ShapeDtypeStruct((SGU_GROUPS, CHUNK, CHUNK), F32),
                   jax.ShapeDtypeStruct((CHUNK, W), F32)],
        args=(zc, dp, ln_g, ln_b, wm, bmap), scratch=[pltpu.VMEM((tt, W), F32), pltpu.VMEM((tt, W), F32)], ride=ride)


def _rope_tables(positions):
    half = QK_ROPE // 2
    inv_freq = jnp.exp(-math.log(ROPE_BASE) * jnp.arange(half, dtype=F32) / half)
    ang = positions.reshape(-1).astype(F32)[:, None] * inv_freq
    cos = jnp.cos(ang)
    sin = jnp.sin(ang)
    n = ang.shape[0]
    tail = LANES - QK_NOPE - QK_ROPE
    cos_t = jnp.concatenate([jnp.ones((n, QK_NOPE), F32), cos, cos, jnp.ones((n, tail), F32)], axis=1)
    sin_t = jnp.concatenate([jnp.zeros((n, QK_NOPE), F32), -sin, sin, jnp.zeros((n, tail), F32)], axis=1)
    return cos_t, sin_t


SGU_GROUP_DIM = SGU_WIDTH // SGU_GROUPS
_O1, _O2, _O3, _O4 = Q_LORA, Q_LORA + KV_LORA, Q_LORA + KV_LORA + QK_ROPE, Q_LORA + KV_LORA + QK_ROPE + LRU_WIDTH
_A0, _A1, _A2 = 2 * LRU_WIDTH, 2 * LRU_WIDTH + Q_LORA, 2 * LRU_WIDTH + Q_LORA + KV_LORA
_A3 = _A2 + QK_NOPE
Z_Q_BLOCK, Z_KV_BLOCK, Z_KPE_BLOCK = _A0 // Q_LORA, _A1 // KV_LORA, _A2 // LANES


def _perm_w_in(w_in):
    zeros = lambda n: jnp.zeros((w_in.shape[0], n), w_in.dtype)
    return jnp.concatenate([w_in[:, _O3:_O4], w_in[:, _O4:], w_in[:, :_O1], w_in[:, _O1:_O2], zeros(QK_NOPE),
                            w_in[:, _O2:_O3], zeros(LANES - QK_NOPE - QK_ROPE)], axis=1)


def _unperm_w_in(w):
    return jnp.concatenate([w[:, _A0:_A1], w[:, _A1:_A2], w[:, _A3:_A3 + QK_ROPE], w[:, :LRU_WIDTH],
                            w[:, LRU_WIDTH:_A0]], axis=1)


def _head_blocks(w, d):
    r = w.shape[0]
    return jnp.pad(w.reshape(r, MLA_HEADS, d), ((0, 0), (0, 0), (0, LANES - d))).reshape(r, MLA_HEADS * LANES)


def _from_head_blocks(w, d):
    r = w.shape[0]
    return w.reshape(r, MLA_HEADS, LANES)[:, :, :d].reshape(r, MLA_HEADS * d)


def _split_kv(w_kv):
    r = w_kv.shape[0]
    w3 = w_kv.reshape(r, MLA_HEADS, QK_NOPE + V_HEAD)
    return _head_blocks(w3[:, :, :QK_NOPE].reshape(r, -1), QK_NOPE), w3[:, :, QK_NOPE:].reshape(r, -1)


def _join_kv(w_k, w_v):
    r = w_k.shape[0]
    return jnp.concatenate([_from_head_blocks(w_k, QK_NOPE).reshape(r, MLA_HEADS, QK_NOPE),
                            w_v.reshape(r, MLA_HEADS, V_HEAD)], axis=2).reshape(r, -1)


def _prep_small(w):
    p = {n: w[n] for n in w if n not in BIG}
    eye = jnp.eye(LRU_HEADS, dtype=F32)
    dense = lambda wg: (wg[:, :, None, :] * eye[:, None, :, None]).reshape(LRU_WIDTH, LRU_WIDTH).astype(BF16)
    p["wa_d"] = dense(w["ab_w_rg_a"][0])
    p["wx_d"] = dense(w["ab_w_rg_x"][0])
    causal = jnp.tril(jnp.ones((CHUNK, CHUNK), F32))
    p["wm"] = (w["c_w_s"][0] * causal).astype(BF16)
    p["bmap"] = jnp.repeat(w["c_b_s"][0].T, SGU_GROUP_DIM, axis=1)
    return p


def _prep_big(ab_w_in, ab_w_q_b, ab_w_kv_b):
    return {"w_in_p": _perm_w_in(ab_w_in).astype(BF16),
            "w_q_p": _head_blocks(ab_w_q_b, QK_NOPE + QK_ROPE).astype(BF16),
            "w_kv_p": jnp.concatenate(_split_kv(ab_w_kv_b), axis=1).astype(BF16)}


def _ffn_fwd(h, l, p, S, rides):
    hn = _rms_fwd(h, p["ffn_norm"][l], name=f"ffn{l}_norm")
    g = _mm(hn, p["ffn_gate_t"][l], tb=True, out_dtype=BF16, name=f"ffn{l}_gate", ride=rides.get(f"ffn{l}_gate"))
    u = _mm(hn, p["ffn_up_t"][l], tb=True, out_dtype=BF16, name=f"ffn{l}_up", ride=rides.get(f"ffn{l}_up"))
    out, act = _ffn_act_down(g, u, p["ffn_conv_w"][l], p["ffn_conv_b"][l][None], p["ffn_down"][l], h, S=S,
                             name=f"ffn{l}_down", ride=rides.get(f"ffn{l}_down"))
    return out, (hn, g, u, act)


def _ffn_bwd(dh, h_in, l, p, saved, S, rides, grads_ready, also_ready=None):
    hn, g, u, act = saved
    dw_down = _mm(act, dh, ta=True, out_dtype=BF16, name=f"ffn{l}_dwdown")
    dg, du, dcw, dcb = _ffn_act_bwd(g, u, dh, p["ffn_down"][l], p["ffn_conv_w"][l], p["ffn_conv_b"][l][None], S=S,
                                    name=f"ffn{l}_dactbwd", ride=rides.get(f"ffn{l}_dactbwd"))
    dhn = _mm(dg, p["ffn_gate_t"][l], name=f"ffn{l}_dhn_g")
    dhn = _mm(du, p["ffn_up_t"][l], res=dhn, out_dtype=BF16, name=f"ffn{l}_dhn_u")
    dw_gate_t = _mm(dg, hn, ta=True, out_dtype=BF16, name=f"ffn{l}_dwgate")
    dw_up_t = _mm(du, hn, ta=True, out_dtype=BF16, name=f"ffn{l}_dwup")
    grads_ready(l, {**(also_ready or {}), "ffn_gate_t": dw_gate_t, "ffn_up_t": dw_up_t, "ffn_down": dw_down})
    dh_in, dnorm = _rms_bwd(h_in, p["ffn_norm"][l], dhn, res=dh, name=f"ffn{l}_dnorm", ride=rides.get(f"ffn{l}_dnorm"))
    grads = dict(ffn_norm=dnorm[0], ffn_gate_t=dw_gate_t, ffn_up_t=dw_up_t, ffn_conv_w=dcw[:FFN_CONV],
                 ffn_conv_b=dcb[0], ffn_down=dw_down)
    return dh_in, grads


def _local_step(x, positions, target, p, rides=None, grads_ready=None):
    rides = {} if rides is None else rides
    grads_ready = grads_ready or (lambda layer, ready: None)
    B, S, D = x.shape
    T = B * S
    H = MLA_HEADS
    xf = x.reshape(T, D)
    tgt = target.reshape(T, D)
    cos, sin = _rope_tables(positions)

    hn0 = _rms_fwd(xf, p["ab_norm"][0], name="ab_norm", ride=rides.get("ab_norm"))
    z = _mm(hn0, p["w_in_p"], name="ab_in")
    cqn = _rms_fwd(z, p["ab_q_norm"][0], cb=Z_Q_BLOCK, name="q_norm")
    ckvn = _rms_fwd(z, p["ab_kv_norm"][0], cb=Z_KV_BLOCK, name="kv_norm")
    q = _mm(cqn, p["w_q_p"], name="q_up")
    kv = _mm(ckvn, p["w_kv_p"], out_dtype=BF16, name="kv_up")
    qs = _rope_q(q, cos, sin, name="q_rope")
    kk = _key_blocks(kv, z, cos, sin, kpe_block=Z_KPE_BLOCK, name="k_rope")
    att = dict(B=B, S=S, v_block0=H)
    o, lse = _attn_fwd(qs, kk, kv, name="attn_fwd", ride=rides.get("attn_fwd"), **att)
    lru_par = (p["ab_conv_w"][0], p["ab_conv_b"], p["wa_d"], p["ab_b_rg_a"], p["wx_d"], p["ab_b_rg_x"], p["ab_lambda"])
    y_lru, hs = _lru_fwd(z, *lru_par, S=S, name="lru_fwd", ride=rides.get("lru_fwd"))
    n_att = H * V_HEAD
    w_out_a, w_out_b = p["ab_w_out"][:n_att], p["ab_w_out"][n_att:]
    h1 = _mm(y_lru, w_out_b, res=_mm(o, w_out_a, res=xf, name="ab_out_a"), name="ab_out_b")
    h2, ffn0 = _ffn_fwd(h1, 0, p, S, rides)

    hn2 = _rms_fwd(h2, p["c_norm"][0], name="c_norm")
    zc = _mm(hn2, p["c_w_in_t"], tb=True, name="c_in")
    pg = _sgu_fwd(zc, p["c_ln_g"], p["c_ln_b"], p["wm"], p["bmap"], name="sgu_fwd")
    h3 = _mm(pg, p["c_w_out"], res=h2, name="c_out")
    h4, ffn1 = _ffn_fwd(h3, 1, p, S, rides)

    loss_row, dh4, dfinal = _final_fwd_bwd(h4, p["final_norm"], tgt, name="final")

    dh3, g_ffn1 = _ffn_bwd(dh4, h3, 1, p, ffn1, S, rides, grads_ready)
    dpg = _mm(dh3, p["c_w_out"], tb=True, out_dtype=BF16, name="c_dp")
    dw_c_out = _mm(pg, dh3, ta=True, out_dtype=BF16, name="c_dwout")
    dzc, dlng, dlnb, dwm, dbm = _sgu_bwd(zc, dpg, p["c_ln_g"], p["c_ln_b"], p["wm"], p["bmap"], name="sgu_bwd",
                                         ride=rides.get("sgu_bwd"))
    dhn2 = _mm(dzc, p["c_w_in_t"], out_dtype=BF16, name="c_dhn")
    dw_c_in_t = _mm(dzc, hn2, ta=True, out_dtype=BF16, name="c_dwin")
    dh2, dcnorm = _rms_bwd(h2, p["c_norm"][0], dhn2, res=dh3, name="c_dnorm")
    dh1, g_ffn0 = _ffn_bwd(dh2, h1, 0, p, ffn0, S, rides, grads_ready, {"c_w_in_t": dw_c_in_t, "c_w_out": dw_c_out})

    do = _mm(dh1, w_out_a, tb=True, name="ab_do")
    dy_lru = _mm(dh1, w_out_b, tb=True, out_dtype=BF16, name="ab_dylru")
    dw_out = jnp.concatenate([_mm(o, dh1, ta=True, out_dtype=BF16, name="ab_dwout_a"),
                              _mm(y_lru, dh1, ta=True, out_dtype=BF16, name="ab_dwout_b")], axis=0)
    dq, dk, dv = _attn_bwd(qs, kk, kv, o, lse, do, name="attn_bwd", ride=rides.get("attn_bwd"), **att)
    dq_full = _rope_q_bwd(dq, cos, sin, name="q_rope_bwd")
    dkr = _key_rope_bwd(dk, cos, sin, name="k_rope_bwd")
    n_key = H * LANES
    w_k_p, w_v_p = p["w_kv_p"][:, :n_key], p["w_kv_p"][:, n_key:]
    dcqn = _mm(dq_full, p["w_q_p"], tb=True, name="q_dlat")
    dw_q_p = _mm(cqn, dq_full, ta=True, out_dtype=BF16, name="q_dw")
    dckvn = _mm(dv, w_v_p, tb=True, res=_mm(dk, w_k_p, tb=True, name="k_dlat"), name="v_dlat")
    dw_k_p = _mm(ckvn, dk, ta=True, out_dtype=BF16, name="k_dw")
    dw_v_p = _mm(ckvn, dv, ta=True, out_dtype=BF16, name="v_dw")
    dcq, dqnorm = _rms_bwd(z, p["ab_q_norm"][0], dcqn, cb=Z_Q_BLOCK, out_dtype=BF16, name="q_dnorm")
    dckv, dkvnorm = _rms_bwd(z, p["ab_kv_norm"][0], dckvn, cb=Z_KV_BLOCK, out_dtype=BF16, name="kv_dnorm")
    dxl, dgate, dcw, dcb, dwa, dba, dwx, dbx, dlam = _lru_bwd(z, hs, dy_lru, *lru_par, S=S, name="lru_bwd")
    dz = jnp.concatenate([dxl, dgate, dcq, dckv, dkr], axis=1)
    dhn0 = _mm(dz, p["w_in_p"], tb=True, out_dtype=BF16, name="ab_dhn")
    dw_in_p = _mm(hn0, dz, ta=True, out_dtype=BF16, name="ab_dwin")
    dx, dabnorm = _rms_bwd(xf, p["ab_norm"][0], dhn0, res=dh1, name="ab_dnorm")

    blocks = lambda dd: jnp.stack([dd[i * LRU_BLOCK:(i + 1) * LRU_BLOCK, i * LRU_BLOCK:(i + 1) * LRU_BLOCK]
                                   for i in range(LRU_HEADS)])
    causal = jnp.tril(jnp.ones((CHUNK, CHUNK), F32))
    grads = {
        "ab_norm": dabnorm, "w_in_p": dw_in_p, "ab_q_norm": dqnorm, "w_q_p": dw_q_p,
        "ab_kv_norm": dkvnorm, "w_k_p": dw_k_p, "w_v_p": dw_v_p, "ab_conv_w": dcw[:LRU_CONV][None], "ab_conv_b": dcb,
        "ab_w_rg_a": blocks(dwa)[None], "ab_b_rg_a": dba, "ab_w_rg_x": blocks(dwx)[None], "ab_b_rg_x": dbx,
        "ab_lambda": dlam, "ab_w_out": dw_out,
        "c_norm": dcnorm, "c_w_in_t": dw_c_in_t, "c_ln_g": dlng, "c_ln_b": dlnb,
        "c_w_s": (dwm * causal)[None], "c_b_s": dbm[:, ::SGU_GROUP_DIM].T[None], "c_w_out": dw_c_out,
        "final_norm": dfinal[0],
    }
    for name in ("ffn_norm", "ffn_conv_w", "ffn_conv_b"):
        grads[name] = jnp.stack([g_ffn0[name], g_ffn1[name]])
    for name in ("ffn_gate_t", "ffn_up_t", "ffn_down"):
        grads[name] = [g_ffn0[name], g_ffn1[name]]
    return loss_row, dx.reshape(B, S, D), grads


ANY = pl.BlockSpec(memory_space=pl.ANY)


def _place():
    x, y, c = lax.axis_index("x"), lax.axis_index("y"), lax.axis_index("c")
    chips = [(1 - x, y), (x, 1 - y), (1 - x, 1 - y)]
    return x, y, c, 2 * x + y, (x, y, 1 - c), chips


def _remote(src, dst, send_sems, recv_sems, k, to):
    return pltpu.make_async_remote_copy(src_ref=src, dst_ref=dst, send_sem=send_sems.at[k], recv_sem=recv_sems.at[k],
                                        device_id=to, device_id_type=MESH)


class _Exchange:
    def __init__(self, arrs, out_shapes, n_sems, start, finish):
        self.arrs, self.out_shapes, self.n_sems, self.start, self.finish = list(arrs), out_shapes, n_sems, start, finish

    @property
    def in_specs(self):
        return [ANY] * len(self.arrs)

    @property
    def out_specs(self):
        return [ANY] * len(self.out_shapes)

    @property
    def scratch(self):
        return [pltpu.SemaphoreType.DMA((self.n_sems,)), pltpu.SemaphoreType.DMA((self.n_sems,))]

    def split(self, refs):
        n = len(self.arrs)
        return refs[:n], refs[n:n + len(self.out_shapes)], refs[-2], refs[-1]

    def run(self, name):
        def body(*refs):
            parts = self.split(refs)
            self.start(*parts)
            self.finish(*parts)

        return pl.pallas_call(body, name=name, in_specs=self.in_specs, out_specs=self.out_specs,
                              out_shape=self.out_shapes, scratch_shapes=self.scratch)(*self.arrs)


def _put(buf, piece, idx, axis):
    return lax.dynamic_update_slice_in_dim(buf, jnp.expand_dims(piece, axis).astype(buf.dtype), idx, axis)


def _all_gather(arrs):
    n = len(arrs)
    per = 7

    def start(ins, outs, send_sems, recv_sems):
        x, y, c, j, sib, chips = _place()
        for i in range(n):
            for k, (cx, cy) in enumerate(chips):
                _remote(ins[i].at[:, c], outs[i].at[:, j, c], send_sems, recv_sems, per * i + k, (cx, cy, c)).start()
            _remote(ins[i], outs[i].at[:, j], send_sems, recv_sems, per * i + 6, sib).start()

    def finish(ins, outs, send_sems, recv_sems):
        x, y, c, j, sib, chips = _place()
        passed = []
        for i in range(n):
            for k, (cx, cy) in enumerate(chips):
                got = outs[i].at[:, 2 * cx + cy, c]
                _remote(got, got, send_sems, recv_sems, per * i + k, (cx, cy, c)).wait_recv()
                cp = _remote(got, got, send_sems, recv_sems, per * i + 3 + k, sib)
                cp.start()
                passed.append(cp)
        for i in range(n):
            for k, (cx, cy) in enumerate(chips):
                got = outs[i].at[:, 2 * cx + cy, 1 - c]
                _remote(got, got, send_sems, recv_sems, per * i + 3 + k, sib).wait_recv()
                _remote(ins[i].at[:, c], ins[i].at[:, c], send_sems, recv_sems, per * i + k, sib).wait_send()
            _remote(ins[i], outs[i].at[:, j], send_sems, recv_sems, per * i + 6, sib).wait()
        for cp in passed:
            cp.wait_send()

    shapes = [jax.ShapeDtypeStruct((a.shape[0], N_CHIPS) + a.shape[1:], a.dtype) for a in arrs]
    return _Exchange(arrs, shapes, per * n, start, finish)


class _Offset:
    def __init__(self, sems, k0):
        self.sems, self.k0 = sems, k0

    @property
    def at(self):
        return self

    def __getitem__(self, k):
        return self.sems.at[self.k0 + k]


def _merge(a, b):
    n_in, n_out = len(a.arrs), len(a.out_shapes)

    def both(fa, fb):
        def f(ins, outs, send_sems, recv_sems):
            fa(ins[:n_in], outs[:n_out], send_sems, recv_sems)
            fb(ins[n_in:], outs[n_out:], _Offset(send_sems, a.n_sems), _Offset(recv_sems, a.n_sems))
        return f

    return _Exchange(a.arrs + b.arrs, a.out_shapes + b.out_shapes, a.n_sems + b.n_sems,
                     both(a.start, b.start), both(a.finish, b.finish))


def _pair_swap(arrs):
    n = len(arrs)

    def start(ins, outs, send_sems, recv_sems):
        x, y, c, j, sib, chips = _place()
        for i in range(n):
            _remote(ins[i].at[:, 1 - c], outs[i], send_sems, recv_sems, i, sib).start()

    def finish(ins, outs, send_sems, recv_sems):
        x, y, c, j, sib, chips = _place()
        for i in range(n):
            _remote(ins[i].at[:, 1 - c], outs[i], send_sems, recv_sems, i, sib).wait()

    shapes = [jax.ShapeDtypeStruct((a.shape[0],) + a.shape[2:], a.dtype) for a in arrs]
    return _Exchange(arrs, shapes, n, start, finish)


def _pair_send(arrs):
    n = len(arrs)

    def start(ins, outs, send_sems, recv_sems):
        x, y, c, j, sib, chips = _place()
        for i in range(n):
            _remote(ins[i], outs[i], send_sems, recv_sems, i, sib).start()

    def finish(ins, outs, send_sems, recv_sems):
        x, y, c, j, sib, chips = _place()
        for i in range(n):
            _remote(ins[i], outs[i], send_sems, recv_sems, i, sib).wait()

    shapes = [jax.ShapeDtypeStruct(a.shape, a.dtype) for a in arrs]
    return _Exchange(arrs, shapes, n, start, finish)


def _chip_exchange(arrs, *, scatter):
    n = len(arrs)

    def copies(ins, outs, send_sems, recv_sems):
        x, y, c, j, sib, chips = _place()
        return [(_remote(ins[i].at[2 * cx + cy] if scatter else ins[i], outs[i].at[j], send_sems, recv_sems,
                         3 * i + k, (cx, cy, c)),
                 _remote(outs[i].at[2 * cx + cy], outs[i].at[2 * cx + cy], send_sems, recv_sems, 3 * i + k, (cx, cy, c)))
                for i in range(n) for k, (cx, cy) in enumerate(chips)]

    def start(*refs):
        for out, _ in copies(*refs):
            out.start()

    def finish(*refs):
        for out, back in copies(*refs):
            back.wait_recv()
            out.wait_send()

    shapes = [jax.ShapeDtypeStruct((N_CHIPS,) + a.shape[-2:], a.dtype) for a in arrs]
    return _Exchange(arrs, shapes, 3 * n, start, finish)


FLAT_ROWS = 512


def _pair_add(sharded, from_sib, *, name):
    n, _, R, L = sharded.shape
    tr = _tile(R, FLAT_ROWS, 16)

    def body(s_ref, b_ref, o_ref):
        own = jnp.where(lax.axis_index("c") == 0, s_ref[:, 0], s_ref[:, 1])
        o_ref[...] = (own.astype(F32) + b_ref[...].astype(F32)).astype(BF16)

    spec = pl.BlockSpec((n, tr, L), lambda i: (0, i, 0))
    return pl.pallas_call(
        body, name=name, grid=(R // tr,), in_specs=[pl.BlockSpec((n, 2, tr, L), lambda i: (0, 0, i, 0)), spec],
        out_specs=spec, out_shape=jax.ShapeDtypeStruct((n, R, L), BF16), compiler_params=_cparams(("parallel",)),
    )(sharded, from_sib)


def _chip_sum(arrived, pair, *, name):
    n, R, L = arrived.shape
    tr = _tile(R, FLAT_ROWS, 16)

    def body(a_ref, p_ref, o_ref):
        me = 2 * lax.axis_index("x") + lax.axis_index("y")
        acc = None
        for k in range(n):
            term = jnp.where(me == k, p_ref[k], a_ref[k]).astype(F32)
            acc = term if acc is None else acc + term
        o_ref[...] = acc

    spec = pl.BlockSpec((n, tr, L), lambda i: (0, i, 0))
    return pl.pallas_call(
        body, name=name, grid=(R // tr,), in_specs=[spec, spec], out_specs=pl.BlockSpec((tr, L), lambda i: (i, 0)),
        out_shape=jax.ShapeDtypeStruct((R, L), F32), compiler_params=_cparams(("parallel",)),
    )(arrived, pair)


def _sum_slots(buf, *, name):
    n, R, L = buf.shape
    tr = _tile(R, FLAT_ROWS, 16)

    def body(b_ref, o_ref):
        acc = b_ref[0].astype(F32)
        for k in range(1, n):
            acc = acc + b_ref[k].astype(F32)
        o_ref[...] = acc

    return pl.pallas_call(
        body, name=name, grid=(R // tr,), in_specs=[pl.BlockSpec((n, tr, L), lambda i: (0, i, 0))],
        out_specs=pl.BlockSpec((tr, L), lambda i: (i, 0)),
        out_shape=jax.ShapeDtypeStruct((R, L), F32), compiler_params=_cparams(("parallel",)),
    )(buf)


def _adamw_update(w, g, m, v):
    c1 = 1.0 - ADAM_B1 ** ADAM_STEP
    c2 = 1.0 - ADAM_B2 ** ADAM_STEP
    m = ADAM_B1 * m + (1.0 - ADAM_B1) * g
    v = ADAM_B2 * v + (1.0 - ADAM_B2) * (g * g)
    return -ADAM_LR * ((m / c1) / (jnp.sqrt(v / c2) + ADAM_EPS) + ADAM_WD * w), m, v


def _adamw_halves(w, m, v, own, other, *, name):
    NL, R, L = w.shape
    h = R // 2
    tr = _tile(h, FLAT_ROWS, 16)
    nt = h // tr

    def body(*refs):
        w_ref, m_ref, v_ref = refs[:3]
        own_refs, other_refs = refs[3:3 + NL], refs[3 + NL:3 + 2 * NL]
        d_ref, nm_ref, nv_ref, g_ref = refs[3 + 2 * NL:]
        layer, half = pl.program_id(0), pl.program_id(1)
        mine = half == lax.axis_index("c")
        g = jnp.where(mine, own_refs[0][...], other_refs[0][...])
        for l in range(1, NL):
            g = jnp.where(layer == l, jnp.where(mine, own_refs[l][...], other_refs[l][...]), g)
        d, mm, vv = _adamw_update(w_ref[0], g, m_ref[0], v_ref[0])
        d_ref[0], nm_ref[0], nv_ref[0], g_ref[0] = d, mm, vv, g

    spec = pl.BlockSpec((1, tr, L), lambda l, hh, i: (l, hh * nt + i, 0))
    part = pl.BlockSpec((tr, L), lambda l, hh, i: (i, 0))
    sh = jax.ShapeDtypeStruct((NL, R, L), F32)
    return pl.pallas_call(
        body, name=name, grid=(NL, 2, nt), in_specs=[spec] * 3 + [part] * (2 * NL), out_specs=[spec] * 4,
        out_shape=[sh] * 4, compiler_params=_cparams(("parallel", "parallel", "parallel")),
    )(w, m, v, *own, *other)


def _adamw(w, g, m, v, *, name):
    NL, R, L = w.shape
    tr = _tile(R, FLAT_ROWS, 16)

    def body(w_ref, g_ref, m_ref, v_ref, d_ref, nm_ref, nv_ref):
        d_ref[...], nm_ref[...], nv_ref[...] = _adamw_update(w_ref[...], g_ref[...], m_ref[...], v_ref[...])

    spec = pl.BlockSpec((1, tr, L), lambda l, i: (l, i, 0))
    sh = jax.ShapeDtypeStruct((NL, R, L), F32)
    return pl.pallas_call(
        body, name=name, grid=(NL, R // tr), in_specs=[spec] * 4, out_specs=[spec] * 3, out_shape=[sh] * 3,
        compiler_params=_cparams(("parallel", "parallel")),
    )(w, g, m, v)


WEIGHT_NAMES = ["ab_norm", "ab_w_in", "ab_q_norm", "ab_w_q_b", "ab_kv_norm", "ab_w_kv_b", "ab_conv_w", "ab_conv_b",
                "ab_w_rg_a", "ab_b_rg_a", "ab_w_rg_x", "ab_b_rg_x", "ab_lambda", "ab_w_out", "c_norm", "c_w_in",
                "c_ln_g", "c_ln_b", "c_w_s", "c_b_s", "c_w_out", "ffn_norm", "ffn_w_gate", "ffn_w_up", "ffn_conv_w",
                "ffn_conv_b", "ffn_w_down", "final_norm"]
BIG = {"ab_w_in": 2, "ab_w_q_b": 2, "ab_w_kv_b": 2, "ab_w_out": 1, "c_w_in": 2, "c_w_out": 1,
       "ffn_w_gate": 2, "ffn_w_up": 2, "ffn_w_down": 1}
SMALL_SHARDED = {"ab_conv_w": 2, "c_norm": 1, "c_ln_g": 1, "c_ln_b": 1, "ffn_conv_w": 2}
SMALL_REPLICATED = [n for n in WEIGHT_NAMES if n not in BIG and n not in SMALL_SHARDED]


def _rows(n_elems, mult):
    r = -(-n_elems // LANES)
    return -(-r // mult) * mult


def _flat(parts, rows):
    flat = jnp.concatenate([a.reshape(-1) for a in parts])
    return jnp.pad(flat, (0, rows * LANES - flat.shape[0])).reshape(rows, LANES)


def _unflat(flat, shapes):
    flat = flat.reshape(-1)
    out, off = [], 0
    for s in shapes:
        n = math.prod(s)
        out.append(flat[off:off + n].reshape(s))
        off += n
    return out


def _join_shards(a, axis):
    a = jnp.moveaxis(a, 0, axis)
    return a.reshape(a.shape[:axis] + (a.shape[axis] * a.shape[axis + 1],) + a.shape[axis + 2:])


def kernel(x, positions, ab_norm, ab_w_in, ab_q_norm, ab_w_q_b, ab_kv_norm, ab_w_kv_b, ab_conv_w, ab_conv_b, ab_w_rg_a, ab_b_rg_a, ab_w_rg_x, ab_b_rg_x, ab_lambda, ab_w_out, c_norm, c_w_in, c_ln_g, c_ln_b, c_w_s, c_b_s, c_w_out, ffn_norm, ffn_w_gate, ffn_w_up, ffn_conv_w, ffn_conv_b, ffn_w_down, final_norm, loss_target, m_ab_norm, m_ab_w_in, m_ab_q_norm, m_ab_w_q_b, m_ab_kv_norm, m_ab_w_kv_b, m_ab_conv_w, m_ab_conv_b, m_ab_w_rg_a, m_ab_b_rg_a, m_ab_w_rg_x, m_ab_b_rg_x, m_ab_lambda, m_ab_w_out, m_c_norm, m_c_w_in, m_c_ln_g, m_c_ln_b, m_c_w_s, m_c_b_s, m_c_w_out, m_ffn_norm, m_ffn_w_gate, m_ffn_w_up, m_ffn_conv_w, m_ffn_conv_b, m_ffn_w_down, m_final_norm, v_ab_norm, v_ab_w_in, v_ab_q_norm, v_ab_w_q_b, v_ab_kv_norm, v_ab_w_kv_b, v_ab_conv_w, v_ab_conv_b, v_ab_w_rg_a, v_ab_b_rg_a, v_ab_w_rg_x, v_ab_b_rg_x, v_ab_lambda, v_ab_w_out, v_c_norm, v_c_w_in, v_c_ln_g, v_c_ln_b, v_c_w_s, v_c_b_s, v_c_w_out, v_ffn_norm, v_ffn_w_gate, v_ffn_w_up, v_ffn_conv_w, v_ffn_conv_b, v_ffn_w_down, v_final_norm):
    given = dict(locals())
    w = {n: given[n] for n in WEIGHT_NAMES}
    m = {n: given["m_" + n] for n in WEIGHT_NAMES}
    v = {n: given["v_" + n] for n in WEIGHT_NAMES}
    c = lax.axis_index("c")
    chip = 2 * lax.axis_index("x") + lax.axis_index("y")

    halves = lambda a: a.reshape(a.shape[0], 2, a.shape[1] // 2, a.shape[2])
    tr = lambda a: jnp.swapaxes(a, 1, 2)
    send = {"ab_w_in": w["ab_w_in"], "ab_w_q_b": w["ab_w_q_b"], "ab_w_kv_b": w["ab_w_kv_b"], "ab_w_out": w["ab_w_out"],
            "c_w_in": tr(w["c_w_in"]), "c_w_out": w["c_w_out"], "ffn_w_gate": tr(w["ffn_w_gate"]),
            "ffn_w_up": tr(w["ffn_w_up"]), "ffn_w_down": w["ffn_w_down"]}
    small_rows = _rows(sum(w[n].size for n in SMALL_SHARDED), 16)
    small_sh = _flat([w[n] for n in SMALL_SHARDED], small_rows).reshape(1, 2, small_rows // 2, LANES)
    first_names = ["ab_w_in", "ab_w_q_b", "ab_w_kv_b", "ab_w_out"]
    mine = {n: halves(send[n].astype(BF16)) for n in BIG}

    def put_own(own, arrived):
        return arrived.reshape(arrived.shape[0], -1, arrived.shape[-1])

    p = {"ab_norm": w["ab_norm"], "ffn_gate_t": {}, "ffn_up_t": {}, "ffn_down": {}}
    first = [mine[n] for n in first_names] + [small_sh]

    def first_arrived(got):
        full = {n: put_own(o, a) for n, o, a in zip(first_names + ["small"], first, got)}
        unshard = lambda a: jnp.swapaxes(a.reshape(N_CHIPS, -1, a.shape[-1]), 0, 1).reshape(-1, N_CHIPS * a.shape[-1])
        p.update(_prep_big(unshard(full["ab_w_in"][0]), unshard(full["ab_w_q_b"][0]), unshard(full["ab_w_kv_b"][0])))
        p["ab_w_out"] = full["ab_w_out"][0]
        small_full = dict(w)
        off = 0
        small_got = full["small"].reshape(N_CHIPS, -1)
        for n, ax in SMALL_SHARDED.items():
            seg = small_got[:, off:off + w[n].size].reshape((N_CHIPS,) + w[n].shape)
            small_full[n] = _join_shards(seg, ax)
            off += w[n].size
        p.update(_prep_small(small_full))

    def weights_ride(parts):
        def sink(arrived):
            for (own, setter), a in zip(parts, arrived):
                setter(put_own(own, a)[0])
        return _all_gather([own for own, _ in parts]), sink

    ffn_keys = {"ffn_gate_t": "ffn_w_gate", "ffn_up_t": "ffn_w_up", "ffn_down": "ffn_w_down"}
    ffn_part = lambda key, l: (mine[ffn_keys[key]][l:l + 1], functools.partial(p[key].__setitem__, l))
    rides = {
        "ab_norm": (_all_gather(first), first_arrived),
        "attn_fwd": weights_ride([ffn_part("ffn_gate_t", 0), ffn_part("ffn_up_t", 0)]),
        "lru_fwd": weights_ride([ffn_part("ffn_down", 0)]),
        "ffn0_gate": weights_ride([ffn_part("ffn_gate_t", 1)]),
        "ffn0_up": weights_ride([ffn_part("ffn_up_t", 1)]),
        "ffn0_down": weights_ride([ffn_part("ffn_down", 1),
                                   (mine["c_w_in"], functools.partial(p.__setitem__, "c_w_in_t")),
                                   (mine["c_w_out"], functools.partial(p.__setitem__, "c_w_out"))]),
    }

    def chip_sums(pair, arrived, tag):
        return [_chip_sum(a, b, name=f"grad_chip_sum_{tag}{i}") for i, (a, b) in enumerate(zip(arrived, pair))]

    half_of = {}

    def grads_ready(layer, ready):
        if layer == 1:
            named = {"gate1": ready["ffn_gate_t"], "up1": ready["ffn_up_t"], "down1": ready["ffn_down"]}
            hosts = {"sgu_bwd": ["down1"], "ffn0_dactbwd": ["gate1", "up1"]}
        else:
            named = {"c_in": ready["c_w_in_t"], "c_out": ready["c_w_out"], "gate0": ready["ffn_gate_t"],
                     "up0": ready["ffn_up_t"], "down0": ready["ffn_down"]}
            hosts = {"attn_bwd": ["c_in", "c_out", "down0", "gate0", "up0"]}
        tag = f"f{layer}"
        sharded = [a.reshape(N_CHIPS, 2, -1, a.shape[-1]) for a in named.values()]

        def paired(from_sib):
            pair = {k: _pair_add(a, b, name=f"grad_pair_add_{tag}{i}")
                    for i, (k, a, b) in enumerate(zip(named, sharded, from_sib))}
            for kernel_name, keys in hosts.items():
                def sink(arrived, keys=keys, kernel_name=kernel_name):
                    half_of.update(zip(keys, chip_sums([pair[k] for k in keys], arrived, f"{tag}_{kernel_name}")))
                rides[kernel_name] = (_chip_exchange([pair[k] for k in keys], scatter=True), sink)

        rides[f"ffn{layer}_dnorm"] = (_pair_swap(sharded), paired)

    loss_row, grad_x, g = _local_step(x, positions, loss_target, p, rides, grads_ready)

    cols = lambda a, n: jnp.swapaxes(a.reshape(a.shape[0], N_CHIPS, n), 0, 1)
    n_in, n_q, n_kv = w["ab_w_in"].shape[2], w["ab_w_q_b"].shape[2], w["ab_w_kv_b"].shape[2]
    small_names = SMALL_REPLICATED + list(SMALL_SHARDED)
    rs = _rows(sum(g[n].size for n in small_names) + LANES, FLAT_ROWS)
    small = _flat([loss_row] + [g[n] for n in small_names], rs)
    slot = (jnp.arange(2) == c)[:, None, None]
    last = [cols(_unperm_w_in(g["w_in_p"]), n_in), cols(_from_head_blocks(g["w_q_p"], QK_NOPE + QK_ROPE), n_q),
            cols(_join_kv(g["w_k_p"], g["w_v_p"]), n_kv), g["ab_w_out"]]
    last = [a.reshape(N_CHIPS, 2, -1, a.shape[-1]) for a in last]
    *from_sib, small_sib = _merge(_pair_swap(last), _pair_send([small])).run("tail_pair")
    pair = [_pair_add(a, b, name=f"grad_pair_add_b{i}") for i, (a, b) in enumerate(zip(last, from_sib))]
    pair_small = _sum_slots(jnp.where(slot, small[None], small_sib[None]), name="small_pair_sum")
    my_small = lax.dynamic_index_in_dim(pair_small.reshape(2, rs // 2, LANES), c, axis=0, keepdims=False)
    *arrived, all_small = _merge(_chip_exchange(pair, scatter=True), _chip_exchange([my_small], scatter=False)).run("tail_chip")
    half_of.update(zip(["in", "q", "kv", "out"], chip_sums(pair, arrived, "b")))
    half_of["small"] = _sum_slots(_put(all_small, my_small, chip, 0), name="small_chip_sum")
    keys = ("in", "q", "kv", "out", "c_in", "c_out", "gate0", "gate1", "up0", "up1", "down0", "down1", "small")
    other_half = dict(zip(keys, _pair_send([half_of[k] for k in keys]).run("grad_pair_share")))
    small_sum = jnp.where(slot, half_of["small"][None], other_half["small"][None]).reshape(rs, LANES)
    whole = lambda k: jnp.where(slot, half_of[k][None], other_half[k][None]).reshape(-1, half_of[k].shape[-1])
    grads_t = {"ab_w_in": whole("in").T[None], "ab_w_q_b": whole("q").T[None]}
    grads = {"ab_w_kv_b": whole("kv")[None], "c_w_in": whole("c_in").T[None], **{n: tr(a) for n, a in grads_t.items()}}
    by_halves = {"ab_w_out": (("out",), False), "c_w_out": (("c_out",), False), "ffn_w_down": (("down0", "down1"), False),
                 "ffn_w_gate": (("gate0", "gate1"), True), "ffn_w_up": (("up0", "up1"), True)}

    small_parts = _unflat(small_sum, [(1, LANES)] + [g[n].shape for n in small_names])
    loss = small_parts[0][0, 0]
    for n, a in zip(small_names, small_parts[1:]):
        if n in SMALL_SHARDED:
            ax = SMALL_SHARDED[n]
            a = lax.dynamic_slice_in_dim(a, chip * w[n].shape[ax], w[n].shape[ax], axis=ax)
        grads[n] = a.reshape(w[n].shape)

    delta, new_m, new_v = {}, {}, {}
    for n in BIG:
        if n in by_halves:
            ks, transposed = by_halves[n]
            view = tr if transposed else (lambda a: a)
            out = _adamw_halves(view(w[n]), view(m[n]), view(v[n]), [half_of[k] for k in ks], [other_half[k] for k in ks],
                                name=f"adamw_{n}")
            delta[n], new_m[n], new_v[n], grads[n] = (view(a) for a in out)
        elif n in grads_t:
            out = _adamw(tr(w[n]), grads_t[n], tr(m[n]), tr(v[n]), name=f"adamw_{n}")
            delta[n], new_m[n], new_v[n] = (tr(a) for a in out)
        else:
            delta[n], new_m[n], new_v[n] = _adamw(w[n], grads[n], m[n], v[n], name=f"adamw_{n}")
    small_all = [n for n in WEIGHT_NAMES if n not in BIG]
    ra = _rows(sum(w[n].size for n in small_all), FLAT_ROWS)
    pack = lambda d: _flat([d[n] for n in small_all], ra)[None]
    out = _adamw(pack(w), pack(grads), pack(m), pack(v), name="adamw_small")
    shapes = [w[n].shape for n in small_all]
    for d, flat in zip((delta, new_m, new_v), out):
        d.update(zip(small_all, _unflat(flat, shapes)))
    return (loss, grad_x, *[grads[n] for n in WEIGHT_NAMES], *[delta[n] for n in WEIGHT_NAMES],
            *[new_m[n] for n in WEIGHT_NAMES], *[new_v[n] for n in WEIGHT_NAMES])
```

```python
import functools
import math

import jax
import jax.numpy as jnp
from jax import lax
from jax.experimental import pallas as pl
from jax.experimental.pallas import tpu as pltpu

F32 = jnp.float32
BF16 = jnp.bfloat16
MESH = pl.DeviceIdType.MESH

D_MODEL = 1024
MLA_HEADS = 8
Q_LORA = 256
KV_LORA = 128
QK_NOPE = 64
QK_ROPE = 32
V_HEAD = 64
LRU_WIDTH = 512
LRU_HEADS = 8
LRU_BLOCK = 64
LRU_CONV = 4
LRU_C = 8.0
CHUNK = 128
SGU_GROUPS = 8
SGU_WIDTH = 1024
D_FF = 2816
FFN_CONV = 3
NORM_EPS = 1e-6
ROPE_BASE = 10000.0
AB_IN_PAD = 1536
ADAM_LR = 0.001
ADAM_B1 = 0.9
ADAM_B2 = 0.999
ADAM_EPS = 1e-08
ADAM_WD = 0.01
ADAM_STEP = 10

N_CHIPS = 4
LANES = 128
VMEM_LIMIT = 56 * 1024 * 1024
ROW_TILE = 256
SGU_TILE = 512
NORM_TILE = 1024
MM_TM, MM_TN, MM_TK = 1024, 1536, 2816
MM_TM_T, MM_TK_T = 1408, 1024
GELU_C = math.sqrt(2.0 / math.pi)


def _cparams(sem):
    return pltpu.CompilerParams(dimension_semantics=sem, vmem_limit_bytes=VMEM_LIMIT)


def _tile(n, target, mult=LANES):
    t = (min(n, target) // mult) * mult
    while t >= mult:
        if n % t == 0:
            return t
        t -= mult
    return n


GELU_K = GELU_C * 0.044715


def _gelu(x):
    t = jnp.tanh(x * (GELU_C + GELU_K * (x * x)))
    hx = 0.5 * x
    return hx + hx * t


def _gelu_and_grad(x):
    x2 = x * x
    t = jnp.tanh(x * (GELU_C + GELU_K * x2))
    hx = 0.5 * x
    dg = (0.5 + 0.5 * t) + (hx * (1.0 - t * t)) * (GELU_C + (3.0 * GELU_K) * x2)
    return hx + hx * t, dg


def _sigmoid(x):
    return 1.0 / (1.0 + jnp.exp(-x))


def _shift_rows(x, d, fill_rows):
    ext = jnp.concatenate([fill_rows, x], axis=0)
    return pltpu.roll(ext, d, 0)[8:]


def _shift_rows_up(x, d, fill_rows):
    n = x.shape[0]
    ext = jnp.concatenate([x, fill_rows], axis=0)
    return pltpu.roll(ext, n + 8 - d, 0)[:n]


def _dot(a, b, dims):
    return lax.dot_general(a.astype(BF16), b.astype(BF16), (dims, ((), ())), preferred_element_type=F32)


def _dot_nn(a, b):
    return _dot(a, b, ((1,), (0,)))


def _dot_nt(a, b):
    return _dot(a, b, ((1,), (1,)))


def _dot_tn(a, b):
    return _dot(a, b, ((0,), (0,)))


def _mm(a, b, *, name, ta=False, tb=False, res=None, out_dtype=F32, ride=None):
    if ta:
        K, M = a.shape
    else:
        M, K = a.shape
    N = b.shape[0] if tb else b.shape[1]
    tm = _tile(M, MM_TM_T if ta else (MM_TM if K <= MM_TM else MM_TM // 2), LANES if ta else 8)
    tn = _tile(N, MM_TN, LANES)
    tk = _tile(K, MM_TK_T if ta else MM_TK, LANES)
    nk = K // tk
    a_spec = pl.BlockSpec((tk, tm), lambda j, i, k: (k, i)) if ta else pl.BlockSpec((tm, tk), lambda j, i, k: (i, k))
    b_spec = pl.BlockSpec((tn, tk), lambda j, i, k: (j, k)) if tb else pl.BlockSpec((tk, tn), lambda j, i, k: (k, j))
    o_spec = pl.BlockSpec((tm, tn), lambda j, i, k: (i, j))
    dims = ((0,) if ta else (1,), (1,) if tb else (0,))
    has_res = res is not None

    def body(*refs):
        a_ref, b_ref = refs[:2]
        r_ref = refs[2] if has_res else None
        o_ref = refs[3] if has_res else refs[2]
        p = _dot(a_ref[...], b_ref[...], dims)

        def finish(r):
            if has_res:
                r = r + r_ref[...].astype(F32)
            o_ref[...] = r.astype(out_dtype)

        if nk == 1:
            finish(p)
            return
        acc_ref = refs[-1]
        k = pl.program_id(2)

        @pl.when(k == 0)
        def _():
            acc_ref[...] = p

        @pl.when(jnp.logical_and(k > 0, k < nk - 1))
        def _():
            acc_ref[...] += p

        @pl.when(k == nk - 1)
        def _():
            finish(acc_ref[...] + p)

    in_specs = [a_spec, b_spec] + ([o_spec] if has_res else [])
    args = (a, b) + ((res,) if has_res else ())
    return _pcall(
        body, name=name, grid=(N // tn, M // tm, nk), in_specs=in_specs, out_specs=[o_spec],
        out_shape=[jax.ShapeDtypeStruct((M, N), out_dtype)], args=args,
        scratch=[pltpu.VMEM((tm, tn), F32)] if nk > 1 else [], sem=("parallel", "parallel", "arbitrary"), ride=ride)[0]


def _rms_fwd(x, g, *, name, cb=0, out_dtype=BF16, ride=None):
    T = x.shape[0]
    W = g.shape[-1]
    g = g.reshape(1, W)
    tt = _tile(T, NORM_TILE, 16)

    def body(x_ref, g_ref, o_ref):
        xf = x_ref[...].astype(F32)
        rstd = lax.rsqrt(jnp.mean(xf * xf, axis=-1, keepdims=True) + NORM_EPS)
        o_ref[...] = (xf * rstd * g_ref[...]).astype(out_dtype)

    return _pcall(
        body, name=name, grid=(T // tt,),
        in_specs=[pl.BlockSpec((tt, W), lambda i: (i, cb)), pl.BlockSpec((1, W), lambda i: (0, 0))],
        out_specs=[pl.BlockSpec((tt, W), lambda i: (i, 0))], out_shape=[jax.ShapeDtypeStruct((T, W), out_dtype)],
        args=(x, g), sem=("parallel",), ride=ride)[0]


def _rms_bwd(x, g, dy, *, name, cb=0, res=None, out_dtype=F32, ride=None):
    T = x.shape[0]
    W = g.shape[-1]
    g = g.reshape(1, W)
    tt = _tile(T, NORM_TILE // 2, 16)
    has_res = res is not None

    def body(*refs):
        if has_res:
            x_ref, g_ref, dy_ref, r_ref, dx_ref, dg_ref = refs
        else:
            x_ref, g_ref, dy_ref, dx_ref, dg_ref = refs
        xf = x_ref[...].astype(F32)
        dyf = dy_ref[...].astype(F32)
        rstd = lax.rsqrt(jnp.mean(xf * xf, axis=-1, keepdims=True) + NORM_EPS)
        xhat = xf * rstd
        dxhat = dyf * g_ref[...]
        dx = rstd * (dxhat - xhat * jnp.mean(dxhat * xhat, axis=-1, keepdims=True))
        if has_res:
            dx = dx + r_ref[...].astype(F32)
        dx_ref[...] = dx.astype(out_dtype)
        part = jnp.sum(dyf * xhat, axis=0, keepdims=True)

        @pl.when(pl.program_id(0) == 0)
        def _():
            dg_ref[...] = part

        @pl.when(pl.program_id(0) > 0)
        def _():
            dg_ref[...] += part

    row = pl.BlockSpec((tt, W), lambda i: (i, 0))
    in_specs = [pl.BlockSpec((tt, W), lambda i: (i, cb)), pl.BlockSpec((1, W), lambda i: (0, 0)), row]
    args = (x, g, dy)
    if has_res:
        in_specs.append(row)
        args = args + (res,)
    return _pcall(
        body, name=name, grid=(T // tt,), in_specs=in_specs,
        out_specs=[row, pl.BlockSpec((1, W), lambda i: (0, 0))],
        out_shape=[jax.ShapeDtypeStruct((T, W), out_dtype), jax.ShapeDtypeStruct((1, W), F32)], args=args, ride=ride)


def _final_fwd_bwd(h, g, target, *, name):
    T, W = h.shape
    g = g.reshape(1, W)
    tt = _tile(T, NORM_TILE, 16)

    def body(x_ref, g_ref, t_ref, loss_ref, dx_ref, dg_ref):
        xf = x_ref[...]
        rstd = lax.rsqrt(jnp.mean(xf * xf, axis=-1, keepdims=True) + NORM_EPS)
        xhat = xf * rstd
        err = xhat * g_ref[...] - t_ref[...]
        lpart = jnp.zeros((1, LANES), F32) + (0.5 / W) * jnp.sum(err * err)
        dyf = err * (1.0 / W)
        dxhat = dyf * g_ref[...]
        dx_ref[...] = rstd * (dxhat - xhat * jnp.mean(dxhat * xhat, axis=-1, keepdims=True))
        part = jnp.sum(dyf * xhat, axis=0, keepdims=True)

        @pl.when(pl.program_id(0) == 0)
        def _():
            dg_ref[...] = part
            loss_ref[...] = lpart

        @pl.when(pl.program_id(0) > 0)
        def _():
            dg_ref[...] += part
            loss_ref[...] += lpart

    row = pl.BlockSpec((tt, W), lambda i: (i, 0))
    return pl.pallas_call(
        body, name=name, grid=(T // tt,),
        in_specs=[row, pl.BlockSpec((1, W), lambda i: (0, 0)), row],
        out_specs=[pl.BlockSpec((1, LANES), lambda i: (0, 0)), row, pl.BlockSpec((1, W), lambda i: (0, 0))],
        out_shape=[jax.ShapeDtypeStruct((1, LANES), F32), jax.ShapeDtypeStruct((T, W), F32),
                   jax.ShapeDtypeStruct((1, W), F32)],
        compiler_params=_cparams(("arbitrary",)),
    )(h, g, target)


def _swap16(x):
    lane = lax.broadcasted_iota(jnp.int32, x.shape, 1)
    return jnp.where((lane % 32) < 16, pltpu.roll(x, LANES - 16, 1), pltpu.roll(x, 16, 1))


def _rope(x, c, s):
    return x * c + _swap16(x) * s


def _rope_t(d, c, s):
    return d * c + _swap16(d * s)


def _head_block_map(fn, x, cos, sin, *, name):
    T, W = x.shape
    tt = _tile(T, NORM_TILE, 16)

    def body(x_ref, c_ref, s_ref, o_ref):
        c, s = c_ref[...], s_ref[...]
        for h in range(W // LANES):
            lanes = slice(h * LANES, (h + 1) * LANES)
            o_ref[:, lanes] = fn(x_ref[:, lanes], c, s).astype(BF16)

    tab = pl.BlockSpec((tt, LANES), lambda i: (i, 0))
    blk = pl.BlockSpec((tt, W), lambda i: (i, 0))
    return pl.pallas_call(
        body, name=name, grid=(T // tt,), in_specs=[blk, tab, tab], out_specs=blk,
        out_shape=jax.ShapeDtypeStruct((T, W), BF16), compiler_params=_cparams(("parallel",)),
    )(x, cos, sin)


def _rope_q(q, cos, sin, *, name):
    scale = _attn_scale()
    return _head_block_map(lambda x, c, s: _rope(x, c, s) * scale, q, cos, sin, name=name)


def _rope_q_bwd(dq, cos, sin, *, name):
    return _head_block_map(_rope_t, dq, cos, sin, name=name)


def _key_blocks(kv, z, cos, sin, *, kpe_block, name):
    T = kv.shape[0]
    tt = _tile(T, NORM_TILE, 16)
    W = MLA_HEADS * LANES

    def body(kv_ref, z_ref, c_ref, s_ref, o_ref):
        kr = _rope(z_ref[...], c_ref[...], s_ref[...])
        for h in range(MLA_HEADS):
            lanes = slice(h * LANES, (h + 1) * LANES)
            o_ref[:, lanes] = (kv_ref[:, lanes].astype(F32) + kr).astype(BF16)

    tab = pl.BlockSpec((tt, LANES), lambda i: (i, 0))
    blk = pl.BlockSpec((tt, W), lambda i: (i, 0))
    return pl.pallas_call(
        body, name=name, grid=(T // tt,),
        in_specs=[blk, pl.BlockSpec((tt, LANES), lambda i: (i, kpe_block)), tab, tab], out_specs=blk,
        out_shape=jax.ShapeDtypeStruct((T, W), BF16), compiler_params=_cparams(("parallel",)),
    )(kv, z, cos, sin)


def _key_rope_bwd(dk, cos, sin, *, name):
    T = dk.shape[0]
    tt = _tile(T, NORM_TILE, 16)

    def body(d_ref, c_ref, s_ref, o_ref):
        d = d_ref[:, :LANES]
        for h in range(1, MLA_HEADS):
            d = d + d_ref[:, h * LANES:(h + 1) * LANES]
        lane = lax.broadcasted_iota(jnp.int32, d.shape, 1)
        d = jnp.where(jnp.logical_and(lane >= QK_NOPE, lane < QK_NOPE + QK_ROPE), d, 0.0)
        o_ref[...] = _rope_t(d, c_ref[...], s_ref[...]).astype(BF16)

    tab = pl.BlockSpec((tt, LANES), lambda i: (i, 0))
    return pl.pallas_call(
        body, name=name, grid=(T // tt,),
        in_specs=[pl.BlockSpec((tt, MLA_HEADS * LANES), lambda i: (i, 0)), tab, tab], out_specs=tab,
        out_shape=jax.ShapeDtypeStruct((T, LANES), BF16), compiler_params=_cparams(("parallel",)),
    )(dk, cos, sin)


ATT_BLOCK = 512


def _attn_scale():
    return float((QK_NOPE + QK_ROPE) ** -0.5)


def _causal_mask(qi, kj, tq, tk):
    row = qi * tq + lax.broadcasted_iota(jnp.int32, (tq, tk), 0)
    col = kj * tk + lax.broadcasted_iota(jnp.int32, (tq, tk), 1)
    return col <= row


def _pcall(body, *, name, grid, in_specs, out_specs, out_shape, args, scratch=(), sem=None, ride=None):
    n_in, n_out, n_scr = len(args), len(out_shape), len(scratch)
    if ride is None:
        return pl.pallas_call(
            body, name=name, grid=grid, in_specs=list(in_specs), out_specs=list(out_specs), out_shape=list(out_shape),
            scratch_shapes=list(scratch), compiler_params=_cparams(sem or ("arbitrary",) * len(grid)))(*args)
    ex, sink = ride
    o0 = n_in + len(ex.arrs)
    s0 = o0 + n_out + len(ex.out_shapes)

    def hosted(*refs):
        parts = (refs[n_in:o0], refs[o0 + n_out:s0], refs[-2], refs[-1])
        ids = [pl.program_id(i) for i in range(len(grid))]
        pl.when(functools.reduce(jnp.logical_and, [i == 0 for i in ids]))(lambda: ex.start(*parts))
        body(*refs[:n_in], *refs[o0:o0 + n_out], *refs[s0:s0 + n_scr])
        pl.when(functools.reduce(jnp.logical_and, [i == n - 1 for i, n in zip(ids, grid)]))(lambda: ex.finish(*parts))

    outs = pl.pallas_call(
        hosted, name=name, grid=grid, in_specs=list(in_specs) + ex.in_specs, out_specs=list(out_specs) + ex.out_specs,
        out_shape=list(out_shape) + ex.out_shapes, scratch_shapes=list(scratch) + ex.scratch,
        compiler_params=_cparams(("arbitrary",) * len(grid)))(*args, *ex.arrs)
    sink(outs[n_out:])
    return outs[:n_out]


PAIRS = MLA_HEADS // 2


def _own_lanes(x, first):
    lane = lax.broadcasted_iota(jnp.int32, x.shape, 1)
    return jnp.where((lane < V_HEAD) if first else (lane >= V_HEAD), x, 0.0)


def _lane_sums_as_row(x):
    hi = x.astype(BF16)
    lo = (x - hi.astype(F32)).astype(BF16)
    ones = jnp.ones((8, LANES), BF16)
    return (_dot_nt(ones, hi) + _dot_nt(ones, lo))[0:1, :]


def _attn_fwd(q, k, kv, *, B, S, v_block0, name, ride=None):
    tq = tk = min(ATT_BLOCK, S)
    nq = S // tq
    T = B * S

    def body(q_ref, k_ref, v_ref, o_ref, lse_ref):
        qi = pl.program_id(2)
        qs = (q_ref[:, :LANES], q_ref[:, LANES:])

        def step(masked):
            def f(j, carry):
                rows = pl.ds(pl.multiple_of(j * tk, tk), tk)
                vb = v_ref[rows, :]
                out = []
                for h in range(2):
                    m, l, acc = carry[h]
                    s = _dot_nt(qs[h], k_ref[rows, h * LANES:(h + 1) * LANES])
                    if masked:
                        s = jnp.where(_causal_mask(qi, j, tq, tk), s, -jnp.inf)
                    m_new = jnp.maximum(m, jnp.max(s, axis=-1, keepdims=True))
                    alpha = jnp.exp(m - m_new)
                    p = jnp.exp(s - m_new)
                    out.append((m_new, alpha * l + jnp.sum(p, axis=-1, keepdims=True), alpha * acc + _dot_nn(p, vb)))
                return tuple(out)
            return f

        one = (jnp.full((tq, 1), -1e30, F32), jnp.zeros((tq, 1), F32), jnp.zeros((tq, LANES), F32))
        (ma, la, acca), (mb, lb, accb) = step(True)(qi, lax.fori_loop(0, qi, step(False), (one, one)))
        o_ref[...] = _own_lanes(acca / la, True) + _own_lanes(accb / lb, False)
        for h, lse in enumerate((ma + jnp.log(la), mb + jnp.log(lb))):
            lse_ref[0, h, pl.ds(qi, 1), :] = _lane_sums_as_row(jnp.broadcast_to(lse * (1.0 / LANES), (tq, LANES)))

    return _pcall(
        body, name=name, grid=(B, PAIRS, nq),
        in_specs=[pl.BlockSpec((tq, 2 * LANES), lambda b, g, i: (b * nq + i, g)),
                  pl.BlockSpec((S, 2 * LANES), lambda b, g, i: (b, g)),
                  pl.BlockSpec((S, LANES), lambda b, g, i: (b, v_block0 + g))],
        out_specs=[pl.BlockSpec((tq, LANES), lambda b, g, i: (b * nq + i, g)),
                   pl.BlockSpec((1, 2, nq, tq), lambda b, g, i: (b, g, 0, 0))],
        out_shape=[jax.ShapeDtypeStruct((T, PAIRS * LANES), F32), jax.ShapeDtypeStruct((B, MLA_HEADS, nq, tq), F32)],
        args=(q, k, kv), ride=ride)


def _attn_bwd(q, k, kv, o, lse_rows, do, *, B, S, v_block0, name, ride=None):
    tq = tk = min(ATT_BLOCK, S)
    nq = S // tq
    T = B * S
    scale = _attn_scale()

    def body(q_ref, k_ref, v_ref, o_ref, lse_ref, do_ref, dk_ref, dv_ref, dq_ref, delta_ref):
        kj = pl.program_id(2)
        ks = (k_ref[:, :LANES], k_ref[:, LANES:])
        vb = v_ref[...]

        @pl.when(kj == 0)
        def _():
            dq_ref[...] = jnp.zeros_like(dq_ref)
            for i in range(nq):
                prod = do_ref[i * tq:(i + 1) * tq, :] * o_ref[i * tq:(i + 1) * tq, :]
                for h in range(2):
                    delta_ref[h, i:i + 1, :] = _lane_sums_as_row(_own_lanes(prod, h == 0))

        def step(masked):
            def f(i, carry):
                rows = pl.ds(pl.multiple_of(i * tq, tq), tq)
                do_b = do_ref[rows, :]
                dks, dv = list(carry[:2]), carry[2]
                for h in range(2):
                    qb = q_ref[rows, h * LANES:(h + 1) * LANES]
                    doh = _own_lanes(do_b, h == 0)
                    pt = jnp.exp(_dot_nt(ks[h], qb) - lse_ref[0, h, pl.ds(i, 1), :])
                    if masked:
                        krow = kj * tk + lax.broadcasted_iota(jnp.int32, (tk, tq), 0)
                        qcol = i * tq + lax.broadcasted_iota(jnp.int32, (tk, tq), 1)
                        pt = jnp.where(krow <= qcol, pt, 0.0)
                    dst = pt * (_dot_nt(vb, doh) - delta_ref[h, pl.ds(i, 1), :])
                    dks[h] = dks[h] + _dot_nn(dst, qb)
                    dv = dv + _dot_nn(pt, doh)
                    dq_ref[rows, h * LANES:(h + 1) * LANES] += _dot_tn(dst, ks[h]) * scale
                return dks[0], dks[1], dv
            return f

        zero = jnp.zeros((tk, LANES), F32)
        dka, dkb, dv = lax.fori_loop(kj + 1, nq, step(False), step(True)(kj, (zero, zero, zero)))
        dk_ref[:, :LANES] = dka
        dk_ref[:, LANES:] = dkb
        dv_ref[...] = dv

    krow = lambda w, c0: pl.BlockSpec((tk, w), lambda b, g, j: (b * nq + j, c0 + g))
    seq = lambda w: pl.BlockSpec((S, w), lambda b, g, j: (b, g))
    stat = pl.BlockSpec((1, 2, nq, tq), lambda b, g, j: (b, g, 0, 0))
    dk, dv, dq = _pcall(
        body, name=name, grid=(B, PAIRS, nq),
        in_specs=[seq(2 * LANES), krow(2 * LANES, 0), krow(LANES, v_block0), seq(LANES), stat, seq(LANES)],
        out_specs=[krow(2 * LANES, 0), krow(LANES, 0), seq(2 * LANES)],
        out_shape=[jax.ShapeDtypeStruct((T, MLA_HEADS * LANES), F32), jax.ShapeDtypeStruct((T, PAIRS * LANES), F32),
                   jax.ShapeDtypeStruct((T, MLA_HEADS * LANES), F32)],
        args=(q, k, kv, o, lse_rows, do), scratch=[pltpu.VMEM((2, nq, tq), F32)], ride=ride)
    return dq, dk, dv


def _lru_gates(xl, halo, cw_ref, cb_ref, wa_ref, ba_ref, wx_ref, bx_ref, lam_ref):
    xc = cb_ref[...] + cw_ref[3:4, :] * xl
    for kk in range(LRU_CONV - 1):
        xc = xc + cw_ref[kk:kk + 1, :] * _shift_rows(xl, LRU_CONV - 1 - kk, halo)
    r = _sigmoid(_dot_nn(xc, wa_ref[...]) + ba_ref[...])
    i = _sigmoid(_dot_nn(xc, wx_ref[...]) + bx_ref[...])
    lam = lam_ref[...]
    sp = jnp.maximum(-lam, 0.0) + jnp.log(1.0 + jnp.exp(-jnp.abs(lam)))
    a = jnp.exp(-LRU_C * r * sp)
    mult = jnp.sqrt(1.0 - a * a)
    return xc, r, i, sp, a, mult


def _lru_specs(tt, nt, S):
    def make(rev):
        tmap = (lambda t: nt - 1 - t) if rev else (lambda t: t)
        tile = lambda cb: pl.BlockSpec((tt, LRU_WIDTH), lambda b, t: (b * nt + tmap(t), cb))
        prev8 = lambda cb: pl.BlockSpec(
            (8, LRU_WIDTH), lambda b, t: (jnp.maximum((b * nt + tmap(t)) * (tt // 8) - 1, 0), cb))
        return tile, prev8, tmap
    return make


def _lru_fwd(z, cw, cb, wa, ba, wx, bx, lam, *, S, name, ride=None):
    T = z.shape[0]
    tt = min(ROW_TILE, S)
    nt = S // tt
    tile, prev8, _ = _lru_specs(tt, nt, S)(False)
    vec = lambda r: pl.BlockSpec((r, LRU_WIDTH), lambda b, t: (0, 0))
    mat = pl.BlockSpec((LRU_WIDTH, LRU_WIDTH), lambda b, t: (0, 0))

    def body(xl_ref, halo_ref, gate_ref, cw_ref, cb_ref, wa_ref, ba_ref, wx_ref, bx_ref, lam_ref,
             y_ref, h_ref, carry_ref):
        t = pl.program_id(1)
        first = t == 0
        halo = jnp.where(first, 0.0, halo_ref[...])
        xl_t = xl_ref[...]
        xc, r, i, sp, a, mult = _lru_gates(xl_t, halo, cw_ref, cb_ref, wa_ref, ba_ref, wx_ref, bx_ref, lam_ref)
        bv = mult * (i * xc)
        ones = jnp.ones((8, LRU_WIDTH), F32)
        zeros = jnp.zeros((8, LRU_WIDTH), F32)
        row = lax.broadcasted_iota(jnp.int32, (tt, LRU_WIDTH), 0)
        A = a
        d = 1
        while d < tt:
            if d < 8:
                a_sh = _shift_rows(A, d, ones)
                b_sh = _shift_rows(bv, d, zeros)
            else:
                a_sh = jnp.where(row < d, 1.0, pltpu.roll(A, d, 0))
                b_sh = jnp.where(row < d, 0.0, pltpu.roll(bv, d, 0))
            bv = A * b_sh + bv
            A = A * a_sh
            d *= 2
        h0 = jnp.where(first, 0.0, carry_ref[0:1, :])
        h = A * h0 + bv
        carry_ref[...] = jnp.broadcast_to(h[tt - 1:tt, :], (8, LRU_WIDTH))
        h_ref[...] = h
        y_ref[...] = (h * _gelu(gate_ref[...])).astype(BF16)

    return _pcall(
        body, name=name, grid=(T // S, nt),
        in_specs=[tile(0), prev8(0), tile(1), vec(LRU_CONV), vec(1), mat, vec(1), mat, vec(1), vec(1)],
        out_specs=[tile(0), tile(0)],
        out_shape=[jax.ShapeDtypeStruct((T, LRU_WIDTH), BF16), jax.ShapeDtypeStruct((T, LRU_WIDTH), F32)],
        args=(z, z, z, cw, cb, wa, ba, wx, bx, lam), scratch=[pltpu.VMEM((8, LRU_WIDTH), F32)], ride=ride)


def _lru_bwd(z, h, dy, cw, cb, wa, ba, wx, bx, lam, *, S, name):
    T = z.shape[0]
    tt = min(ROW_TILE, S)
    nt = S // tt
    tile, prev8, tmap = _lru_specs(tt, nt, S)(True)
    vec = lambda r: pl.BlockSpec((r, LRU_WIDTH), lambda b, t: (0, 0))
    mat = pl.BlockSpec((LRU_WIDTH, LRU_WIDTH), lambda b, t: (0, 0))

    def body(xl_ref, halo_ref, gate_ref, h_ref, hprev_ref, dy_ref, cw_ref, cb_ref, wa_ref, ba_ref, wx_ref,
             bx_ref, lam_ref, dxl_ref, dgate_ref, dcw_ref, dcb_ref, dwa_ref, dba_ref, dwx_ref, dbx_ref,
             dlam_ref, lamc_ref, ac_ref, dxc_ref):
        b = pl.program_id(0)
        t = pl.program_id(1)
        tr = nt - 1 - t
        seq_first = tr == 0
        seq_last = t == 0
        halo = jnp.where(seq_first, 0.0, halo_ref[...])
        xl_t = xl_ref[...]
        xc, r, i, sp, a, mult = _lru_gates(xl_t, halo, cw_ref, cb_ref, wa_ref, ba_ref, wx_ref, bx_ref, lam_ref)
        hh = h_ref[...]
        dyf = dy_ref[...].astype(F32)
        gl, dgl = _gelu_and_grad(gate_ref[...])
        dgate_ref[...] = (dyf * hh * dgl).astype(BF16)
        dh = dyf * gl

        a_first_later = jnp.where(seq_last, 0.0, ac_ref[...])
        lam_later = jnp.where(seq_last, 0.0, lamc_ref[...])
        row = lax.broadcasted_iota(jnp.int32, (tt, LRU_WIDTH), 0)
        A = _shift_rows_up(a, 1, a_first_later)
        lm = dh
        ones = jnp.ones((8, LRU_WIDTH), F32)
        zeros = jnp.zeros((8, LRU_WIDTH), F32)
        d = 1
        while d < tt:
            if d < 8:
                a_sh = _shift_rows_up(A, d, ones)
                l_sh = _shift_rows_up(lm, d, zeros)
            else:
                a_sh = jnp.where(row >= tt - d, 1.0, pltpu.roll(A, tt - d, 0))
                l_sh = jnp.where(row >= tt - d, 0.0, pltpu.roll(lm, tt - d, 0))
            lm = lm + A * l_sh
            A = A * a_sh
            d *= 2
        lm = lm + A * lam_later[0:1, :]
        lamc_ref[...] = jnp.broadcast_to(lm[0:1, :], (8, LRU_WIDTH))
        ac_ref[...] = jnp.broadcast_to(a[0:1, :], (8, LRU_WIDTH))

        hprev_halo = jnp.where(seq_first, 0.0, hprev_ref[...])
        h_prev = _shift_rows(hh, 1, hprev_halo)
        da = lm * h_prev
        ixc = i * xc
        dmult = lm * ixc
        di = lm * mult * xc
        dxc = lm * mult * i
        da = da - dmult * a / mult
        dlog = da * a
        dr = dlog * (-LRU_C) * sp
        dsp_part = jnp.sum(dlog * (-LRU_C) * r, axis=0, keepdims=True)
        dpa = dr * r * (1.0 - r)
        dpx = di * i * (1.0 - i)
        dxc = dxc + _dot_nt(dpa, wa_ref[...]) + _dot_nt(dpx, wx_ref[...])
        dwa_part = _dot_tn(xc, dpa)
        dwx_part = _dot_tn(xc, dpx)

        later = jnp.where(seq_last, 0.0, dxc_ref[...])
        dxl = cw_ref[3:4, :] * dxc
        for kk in range(LRU_CONV - 1):
            dxl = dxl + cw_ref[kk:kk + 1, :] * _shift_rows_up(dxc, LRU_CONV - 1 - kk, later)
        dxl_ref[...] = dxl.astype(BF16)
        dxc_ref[...] = dxc[0:8, :]
        dcw_rows = [jnp.sum(dxc * _shift_rows(xl_t, LRU_CONV - 1 - kk, halo), axis=0, keepdims=True)
                    for kk in range(LRU_CONV - 1)]
        dcw_rows.append(jnp.sum(dxc * xl_t, axis=0, keepdims=True))
        dcw_part = jnp.concatenate(dcw_rows + [jnp.zeros((8 - LRU_CONV, LRU_WIDTH), F32)], axis=0)
        lamv = lam_ref[...]
        dlam_part = dsp_part * (-_sigmoid(-lamv))
        parts = ((dcw_ref, dcw_part), (dcb_ref, jnp.sum(dxc, axis=0, keepdims=True)),
                 (dwa_ref, dwa_part), (dba_ref, jnp.sum(dpa, axis=0, keepdims=True)),
                 (dwx_ref, dwx_part), (dbx_ref, jnp.sum(dpx, axis=0, keepdims=True)),
                 (dlam_ref, dlam_part))
        start = jnp.logical_and(b == 0, t == 0)

        @pl.when(start)
        def _():
            for ref, val in parts:
                ref[...] = val

        @pl.when(jnp.logical_not(start))
        def _():
            for ref, val in parts:
                ref[...] += val

    acc = lambda r: pl.BlockSpec((r, LRU_WIDTH), lambda b, t: (0, 0))
    return pl.pallas_call(
        body, name=name, grid=(T // S, nt),
        in_specs=[tile(0), prev8(0), tile(1), tile(0), prev8(0), tile(0),
                  vec(LRU_CONV), vec(1), mat, vec(1), mat, vec(1), vec(1)],
        out_specs=[tile(0), tile(0), acc(8), acc(1), mat, acc(1), mat, acc(1), acc(1)],
        out_shape=[jax.ShapeDtypeStruct((T, LRU_WIDTH), BF16), jax.ShapeDtypeStruct((T, LRU_WIDTH), BF16),
                   jax.ShapeDtypeStruct((8, LRU_WIDTH), F32), jax.ShapeDtypeStruct((1, LRU_WIDTH), F32),
                   jax.ShapeDtypeStruct((LRU_WIDTH, LRU_WIDTH), F32), jax.ShapeDtypeStruct((1, LRU_WIDTH), F32),
                   jax.ShapeDtypeStruct((LRU_WIDTH, LRU_WIDTH), F32), jax.ShapeDtypeStruct((1, LRU_WIDTH), F32),
                   jax.ShapeDtypeStruct((1, LRU_WIDTH), F32)],
        scratch_shapes=[pltpu.VMEM((8, LRU_WIDTH), F32), pltpu.VMEM((8, LRU_WIDTH), F32),
                        pltpu.VMEM((8, LRU_WIDTH), F32)],
        compiler_params=_cparams(("arbitrary", "arbitrary")),
    )(z, z, z, h, h, dy, cw, cb, wa, ba, wx, bx, lam)


FFN_CT = 1408
FFN_TILE = 512


def _ffn_conv(g, halo, cw, cb):
    gc = cb + cw[2:3, :] * g
    for kk in range(FFN_CONV - 1):
        gc = gc + cw[kk:kk + 1, :] * _shift_rows(g, FFN_CONV - 1 - kk, halo)
    return gc


def _row_chunks(rows, chunk):
    return [slice(r0, min(r0 + chunk, rows)) for r0 in range(0, rows, chunk)]


FFN_CHUNK = 128
HALO = 16


def _ffn_act_down(g, u, cw, cb, w_down, res, *, S, name, ride=None):
    T, F = g.shape
    D = w_down.shape[1]
    tt = min(FFN_TILE, S)
    nt = S // tt
    tc = _tile(F, FFN_CT)
    nj = F // tc

    def body(g_ref, halo_ref, u_ref, cw_ref, cb_ref, w_ref, r_ref, o_ref, act_ref):
        j = pl.program_id(1)
        first = (pl.program_id(0) % nt) == 0
        cw, cb = cw_ref[...], cb_ref[...]

        @pl.when(j == 0)
        def _():
            o_ref[...] = r_ref[...]

        for r in _row_chunks(tt, FFN_CHUNK):
            before = halo_ref[...] if r.start == 0 else g_ref[r.start - HALO:r.start, :]
            halo = before.astype(F32)[HALO - 8:]
            if r.start == 0:
                halo = jnp.where(first, 0.0, halo)
            gc = _ffn_conv(g_ref[r, :].astype(F32), halo, cw, cb)
            act = (_gelu(gc) * u_ref[r, :].astype(F32)).astype(BF16)
            act_ref[r, :] = act
            o_ref[r, :] += _dot_nn(act, w_ref[...])

    tile = pl.BlockSpec((tt, tc), lambda i, j: (i, j))
    prev = pl.BlockSpec((HALO, tc), lambda i, j: (jnp.maximum(i * (tt // HALO) - 1, 0), j))
    rows = pl.BlockSpec((tt, D), lambda i, j: (i, 0))
    return _pcall(
        body, name=name, grid=(T // tt, nj),
        in_specs=[tile, prev, tile, pl.BlockSpec((FFN_CONV, tc), lambda i, j: (0, j)),
                  pl.BlockSpec((1, tc), lambda i, j: (0, j)), pl.BlockSpec((tc, D), lambda i, j: (j, 0)), rows],
        out_specs=[rows, tile], out_shape=[jax.ShapeDtypeStruct((T, D), F32), jax.ShapeDtypeStruct((T, F), BF16)],
        args=(g, g, u, cw, cb, w_down, res), sem=("parallel", "arbitrary"), ride=ride)


def _ffn_act_bwd(g, u, dh, w_down, cw, cb, *, S, name, ride=None):
    T, F = g.shape
    D = w_down.shape[1]
    tt = min(FFN_TILE, S)
    nt = S // tt
    ntt = T // tt
    tc = _tile(F, FFN_CT)

    def body(g_ref, halo_ref, u_ref, dh_ref, w_ref, cw_ref, cb_ref, dg_ref, du_ref, dcw_ref, dcb_ref, later_ref):
        step = pl.program_id(1)
        ti = (ntt - 1 - step) % nt
        cw, cb = cw_ref[...], cb_ref[...]

        @pl.when(step == 0)
        def _():
            dcw_ref[...] = jnp.zeros_like(dcw_ref)
            dcb_ref[...] = jnp.zeros_like(dcb_ref)

        halo = jnp.where(ti == 0, 0.0, halo_ref[...].astype(F32)[HALO - 8:])
        gt = g_ref[...].astype(F32)
        gl, dgl = _gelu_and_grad(_ffn_conv(gt, halo, cw, cb))
        da = _dot_nt(dh_ref[...], w_ref[...])
        du_ref[...] = (da * gl).astype(BF16)
        dgc = da * u_ref[...].astype(F32) * dgl
        later = jnp.where(ti == nt - 1, 0.0, later_ref[...])
        dg = cw[2:3, :] * dgc
        for kk in range(FFN_CONV - 1):
            dg = dg + cw[kk:kk + 1, :] * _shift_rows_up(dgc, FFN_CONV - 1 - kk, later)
        dg_ref[...] = dg.astype(BF16)
        later_ref[...] = dgc[0:8, :]
        rows = [jnp.sum(dgc * _shift_rows(gt, FFN_CONV - 1 - kk, halo), axis=0, keepdims=True)
                for kk in range(FFN_CONV - 1)]
        rows.append(jnp.sum(dgc * gt, axis=0, keepdims=True))
        dcw_ref[...] += jnp.concatenate(rows + [jnp.zeros((8 - FFN_CONV, tc), F32)], axis=0)
        dcb_ref[...] += jnp.sum(dgc, axis=0, keepdims=True)

    tile = pl.BlockSpec((tt, tc), lambda j, s: (ntt - 1 - s, j))
    prev = pl.BlockSpec((HALO, tc), lambda j, s: (jnp.maximum((ntt - 1 - s) * (tt // HALO) - 1, 0), j))
    return _pcall(
        body, name=name, grid=(F // tc, ntt),
        in_specs=[tile, prev, tile, pl.BlockSpec((tt, D), lambda j, s: (ntt - 1 - s, 0)),
                  pl.BlockSpec((tc, D), lambda j, s: (j, 0)), pl.BlockSpec((FFN_CONV, tc), lambda j, s: (0, j)),
                  pl.BlockSpec((1, tc), lambda j, s: (0, j))],
        out_specs=[tile, tile, pl.BlockSpec((8, tc), lambda j, s: (0, j)), pl.BlockSpec((1, tc), lambda j, s: (0, j))],
        out_shape=[jax.ShapeDtypeStruct((T, F), BF16), jax.ShapeDtypeStruct((T, F), BF16),
                   jax.ShapeDtypeStruct((8, F), F32), jax.ShapeDtypeStruct((1, F), F32)],
        args=(g, g, u, dh, w_down, cw, cb), scratch=[pltpu.VMEM((8, tc), F32)], ride=ride)


def _sgu_norm(zv, g_ref, b_ref):
    v = _gelu(zv)
    mu = jnp.mean(v, axis=-1, keepdims=True)
    xc = v - mu
    rstd = lax.rsqrt(jnp.mean(xc * xc, axis=-1, keepdims=True) + NORM_EPS)
    xhat = xc * rstd
    return xhat, rstd, xhat * g_ref[...] + b_ref[...]


def _sgu_fwd(zc, ln_g, ln_b, wm, bmap, *, name):
    T = zc.shape[0]
    W = SGU_WIDTH
    tt = _tile(T, SGU_TILE, CHUNK)
    nch = tt // CHUNK

    def body(z_ref, g_ref, b_ref, wm_ref, bm_ref, p_ref):
        u = _gelu(z_ref[:, :W])
        _, _, vn = _sgu_norm(z_ref[:, W:], g_ref, b_ref)
        vn = vn.astype(BF16)
        for n in range(nch):
            rows = slice(n * CHUNK, (n + 1) * CHUNK)
            for gi in range(SGU_GROUPS):
                cols = slice(gi * LANES, (gi + 1) * LANES)
                s = _dot_nn(wm_ref[gi], vn[rows, cols]) + bm_ref[:, cols]
                p_ref[rows, cols] = (u[rows, cols] * s).astype(BF16)

    const2 = lambda r, c: pl.BlockSpec((r, c), lambda i: (0, 0))
    return pl.pallas_call(
        body, name=name, grid=(T // tt,),
        in_specs=[pl.BlockSpec((tt, 2 * W), lambda i: (i, 0)), const2(1, W), const2(1, W),
                  pl.BlockSpec((SGU_GROUPS, CHUNK, CHUNK), lambda i: (0, 0, 0)), const2(CHUNK, W)],
        out_specs=pl.BlockSpec((tt, W), lambda i: (i, 0)),
        out_shape=jax.ShapeDtypeStruct((T, W), BF16),
        compiler_params=_cparams(("parallel",)),
    )(zc, ln_g, ln_b, wm, bmap)


def _sgu_bwd(zc, dp, ln_g, ln_b, wm, bmap, *, name, ride=None):
    T = zc.shape[0]
    W = SGU_WIDTH
    tt = _tile(T, SGU_TILE, CHUNK)
    nch = tt // CHUNK
    nsteps = T // tt

    def body(z_ref, dp_ref, g_ref, b_ref, wm_ref, bm_ref, dz_ref, dg_ref, db_ref, dwm_ref, dbm_ref,
             s_scr, dvn_scr):
        step = pl.program_id(0)
        zu = z_ref[:, :W]
        zv = z_ref[:, W:]
        u, dgu = _gelu_and_grad(zu)
        xhat, rstd, vn = _sgu_norm(zv, g_ref, b_ref)
        vnb = vn.astype(BF16)
        dpf = dp_ref[...].astype(F32)
        ds = dpf * u

        @pl.when(step == 0)
        def _():
            dwm_ref[...] = jnp.zeros_like(dwm_ref)
            dbm_ref[...] = jnp.zeros_like(dbm_ref)

        for n in range(nch):
            rows = slice(n * CHUNK, (n + 1) * CHUNK)
            for gi in range(SGU_GROUPS):
                cols = slice(gi * LANES, (gi + 1) * LANES)
                s_scr[rows, cols] = _dot_nn(wm_ref[gi], vnb[rows, cols]) + bm_ref[:, cols]
                dsb = ds[rows, cols]
                dvn_scr[rows, cols] = _dot_tn(wm_ref[gi], dsb)
                dwm_ref[gi] += _dot_nt(dsb, vnb[rows, cols])
                dbm_ref[:, cols] += dsb
        dz_ref[:, :W] = (dpf * s_scr[...] * dgu).astype(BF16)
        dvn = dvn_scr[...]
        dxhat = dvn * g_ref[...]
        dv = rstd * (dxhat - jnp.mean(dxhat, axis=-1, keepdims=True)
                     - xhat * jnp.mean(dxhat * xhat, axis=-1, keepdims=True))
        _, dgv = _gelu_and_grad(zv)
        dz_ref[:, W:] = (dv * dgv).astype(BF16)
        dg_part = jnp.sum(dvn * xhat, axis=0, keepdims=True)
        db_part = jnp.sum(dvn, axis=0, keepdims=True)

        @pl.when(step == 0)
        def _():
            dg_ref[...] = dg_part
            db_ref[...] = db_part

        @pl.when(step > 0)
        def _():
            dg_ref[...] += dg_part
            db_ref[...] += db_part

        @pl.when(step == nsteps - 1)
        def _():
            for gi in range(SGU_GROUPS):
                cols = slice(gi * LANES, (gi + 1) * LANES)
                tot = jnp.sum(dbm_ref[:, cols], axis=1, keepdims=True)
                dbm_ref[:, cols] = jnp.broadcast_to(tot, (CHUNK, LANES))

    const2 = lambda r, c: pl.BlockSpec((r, c), lambda i: (0, 0))
    wspec = pl.BlockSpec((SGU_GROUPS, CHUNK, CHUNK), lambda i: (0, 0, 0))
    return _pcall(
        body, name=name, grid=(nsteps,),
        in_specs=[pl.BlockSpec((tt, 2 * W), lambda i: (i, 0)), pl.BlockSpec((tt, W), lambda i: (i, 0)),
                  const2(1, W), const2(1, W), wspec, const2(CHUNK, W)],
        out_specs=[pl.BlockSpec((tt, 2 * W), lambda i: (i, 0)), const2(1, W), const2(1, W), wspec, const2(CHUNK, W)],
        out_shape=[jax.ShapeDtypeStruct((T, 2 * W), BF16), jax.ShapeDtypeStruct((1, W), F32),
                   jax.ShapeDtypeStruct((1, W), F32), jax.ShapeDtypeStruct((SGU_GROUPS, CHUNK, CHUNK), F32),
                   jax.ShapeDtypeStruct((CHUNK, W), F32)],
        args=(zc, dp, ln_g, ln_b, wm, bmap), scratch=[pltpu.VMEM((tt, W), F32), pltpu.VMEM((tt, W), F32)], ride=ride)


def _rope_tables(positions):
    half = QK_ROPE // 2
    inv_freq = jnp.exp(-math.log(ROPE_BASE) * jnp.arange(half, dtype=F32) / half)
    ang = positions.reshape(-1).astype(F32)[:, None] * inv_freq
    cos = jnp.cos(ang)
    sin = jnp.sin(ang)
    n = ang.shape[0]
    tail = LANES - QK_NOPE - QK_ROPE
    cos_t = jnp.concatenate([jnp.ones((n, QK_NOPE), F32), cos, cos, jnp.ones((n, tail), F32)], axis=1)
    sin_t = jnp.concatenate([jnp.zeros((n, QK_NOPE), F32), -sin, sin, jnp.zeros((n, tail), F32)], axis=1)
    return cos_t, sin_t


SGU_GROUP_DIM = SGU_WIDTH // SGU_GROUPS
_O1, _O2, _O3, _O4 = Q_LORA, Q_LORA + KV_LORA, Q_LORA + KV_LORA + QK_ROPE, Q_LORA + KV_LORA + QK_ROPE + LRU_WIDTH
_A0, _A1, _A2 = 2 * LRU_WIDTH, 2 * LRU_WIDTH + Q_LORA, 2 * LRU_WIDTH + Q_LORA + KV_LORA
_A3 = _A2 + QK_NOPE
Z_Q_BLOCK, Z_KV_BLOCK, Z_KPE_BLOCK = _A0 // Q_LORA, _A1 // KV_LORA, _A2 // LANES


def _perm_w_in(w_in):
    zeros = lambda n: jnp.zeros((w_in.shape[0], n), w_in.dtype)
    return jnp.concatenate([w_in[:, _O3:_O4], w_in[:, _O4:], w_in[:, :_O1], w_in[:, _O1:_O2], zeros(QK_NOPE),
                            w_in[:, _O2:_O3], zeros(LANES - QK_NOPE - QK_ROPE)], axis=1)


def _unperm_w_in(w):
    return jnp.concatenate([w[:, _A0:_A1], w[:, _A1:_A2], w[:, _A3:_A3 + QK_ROPE], w[:, :LRU_WIDTH],
                            w[:, LRU_WIDTH:_A0]], axis=1)


def _head_blocks(w, d):
    r = w.shape[0]
    return jnp.pad(w.reshape(r, MLA_HEADS, d), ((0, 0), (0, 0), (0, LANES - d))).reshape(r, MLA_HEADS * LANES)


def _from_head_blocks(w, d):
    r = w.shape[0]
    return w.reshape(r, MLA_HEADS, LANES)[:, :, :d].reshape(r, MLA_HEADS * d)


def _split_kv(w_kv):
    r = w_kv.shape[0]
    w3 = w_kv.reshape(r, MLA_HEADS, QK_NOPE + V_HEAD)
    return _head_blocks(w3[:, :, :QK_NOPE].reshape(r, -1), QK_NOPE), w3[:, :, QK_NOPE:].reshape(r, -1)


def _join_kv(w_k, w_v):
    r = w_k.shape[0]
    return jnp.concatenate([_from_head_blocks(w_k, QK_NOPE).reshape(r, MLA_HEADS, QK_NOPE),
                            w_v.reshape(r, MLA_HEADS, V_HEAD)], axis=2).reshape(r, -1)


def _prep_small(w):
    p = {n: w[n] for n in w if n not in BIG}
    eye = jnp.eye(LRU_HEADS, dtype=F32)
    dense = lambda wg: (wg[:, :, None, :] * eye[:, None, :, None]).reshape(LRU_WIDTH, LRU_WIDTH).astype(BF16)
    p["wa_d"] = dense(w["ab_w_rg_a"][0])
    p["wx_d"] = dense(w["ab_w_rg_x"][0])
    causal = jnp.tril(jnp.ones((CHUNK, CHUNK), F32))
    p["wm"] = (w["c_w_s"][0] * causal).astype(BF16)
    p["bmap"] = jnp.repeat(w["c_b_s"][0].T, SGU_GROUP_DIM, axis=1)
    return p


def _prep_big(ab_w_in, ab_w_q_b, ab_w_kv_b):
    return {"w_in_p": _perm_w_in(ab_w_in).astype(BF16),
            "w_q_p": _head_blocks(ab_w_q_b, QK_NOPE + QK_ROPE).astype(BF16),
            "w_kv_p": jnp.concatenate(_split_kv(ab_w_kv_b), axis=1).astype(BF16)}


def _ffn_fwd(h, l, p, S, rides):
    hn = _rms_fwd(h, p["ffn_norm"][l], name=f"ffn{l}_norm")
    g = _mm(hn, p["ffn_gate_t"][l], tb=True, out_dtype=BF16, name=f"ffn{l}_gate", ride=rides.get(f"ffn{l}_gate"))
    u = _mm(hn, p["ffn_up_t"][l], tb=True, out_dtype=BF16, name=f"ffn{l}_up", ride=rides.get(f"ffn{l}_up"))
    out, act = _ffn_act_down(g, u, p["ffn_conv_w"][l], p["ffn_conv_b"][l][None], p["ffn_down"][l], h, S=S,
                             name=f"ffn{l}_down", ride=rides.get(f"ffn{l}_down"))
    return out, (hn, g, u, act)


def _ffn_bwd(dh, h_in, l, p, saved, S, rides, grads_ready, also_ready=None):
    hn, g, u, act = saved
    dw_down = _mm(act, dh, ta=True, out_dtype=BF16, name=f"ffn{l}_dwdown")
    dg, du, dcw, dcb = _ffn_act_bwd(g, u, dh, p["ffn_down"][l], p["ffn_conv_w"][l], p["ffn_conv_b"][l][None], S=S,
                                    name=f"ffn{l}_dactbwd", ride=rides.get(f"ffn{l}_dactbwd"))
    dhn = _mm(dg, p["ffn_gate_t"][l], name=f"ffn{l}_dhn_g")
    dhn = _mm(du, p["ffn_up_t"][l], res=dhn, out_dtype=BF16, name=f"ffn{l}_dhn_u")
    dw_gate_t = _mm(dg, hn, ta=True, out_dtype=BF16, name=f"ffn{l}_dwgate")
    dw_up_t = _mm(du, hn, ta=True, out_dtype=BF16, name=f"ffn{l}_dwup")
    grads_ready(l, {**(also_ready or {}), "ffn_gate_t": dw_gate_t, "ffn_up_t": dw_up_t, "ffn_down": dw_down})
    dh_in, dnorm = _rms_bwd(h_in, p["ffn_norm"][l], dhn, res=dh, name=f"ffn{l}_dnorm", ride=rides.get(f"ffn{l}_dnorm"))
    grads = dict(ffn_norm=dnorm[0], ffn_gate_t=dw_gate_t, ffn_up_t=dw_up_t, ffn_conv_w=dcw[:FFN_CONV],
                 ffn_conv_b=dcb[0], ffn_down=dw_down)
    return dh_in, grads


def _local_step(x, positions, target, p, rides=None, grads_ready=None):
    rides = {} if rides is None else rides
    grads_ready = grads_ready or (lambda layer, ready: None)
    B, S, D = x.shape
    T = B * S
    H = MLA_HEADS
    xf = x.reshape(T, D)
    tgt = target.reshape(T, D)
    cos, sin = _rope_tables(positions)

    hn0 = _rms_fwd(xf, p["ab_norm"][0], name="ab_norm", ride=rides.get("ab_norm"))
    z = _mm(hn0, p["w_in_p"], name="ab_in")
    cqn = _rms_fwd(z, p["ab_q_norm"][0], cb=Z_Q_BLOCK, name="q_norm")
    ckvn = _rms_fwd(z, p["ab_kv_norm"][0], cb=Z_KV_BLOCK, name="kv_norm")
    q = _mm(cqn, p["w_q_p"], name="q_up")
    kv = _mm(ckvn, p["w_kv_p"], out_dtype=BF16, name="kv_up")
    qs = _rope_q(q, cos, sin, name="q_rope")
    kk = _key_blocks(kv, z, cos, sin, kpe_block=Z_KPE_BLOCK, name="k_rope")
    att = dict(B=B, S=S, v_block0=H)
    o, lse = _attn_fwd(qs, kk, kv, name="attn_fwd", ride=rides.get("attn_fwd"), **att)
    lru_par = (p["ab_conv_w"][0], p["ab_conv_b"], p["wa_d"], p["ab_b_rg_a"], p["wx_d"], p["ab_b_rg_x"], p["ab_lambda"])
    y_lru, hs = _lru_fwd(z, *lru_par, S=S, name="lru_fwd", ride=rides.get("lru_fwd"))
    n_att = H * V_HEAD
    w_out_a, w_out_b = p["ab_w_out"][:n_att], p["ab_w_out"][n_att:]
    h1 = _mm(y_lru, w_out_b, res=_mm(o, w_out_a, res=xf, name="ab_out_a"), name="ab_out_b")
    h2, ffn0 = _ffn_fwd(h1, 0, p, S, rides)

    hn2 = _rms_fwd(h2, p["c_norm"][0], name="c_norm")
    zc = _mm(hn2, p["c_w_in_t"], tb=True, name="c_in")
    pg = _sgu_fwd(zc, p["c_ln_g"], p["c_ln_b"], p["wm"], p["bmap"], name="sgu_fwd")
    h3 = _mm(pg, p["c_w_out"], res=h2, name="c_out")
    h4, ffn1 = _ffn_fwd(h3, 1, p, S, rides)

    loss_row, dh4, dfinal = _final_fwd_bwd(h4, p["final_norm"], tgt, name="final")

    dh3, g_ffn1 = _ffn_bwd(dh4, h3, 1, p, ffn1, S, rides, grads_ready)
    dpg = _mm(dh3, p["c_w_out"], tb=True, out_dtype=BF16, name="c_dp")
    dw_c_out = _mm(pg, dh3, ta=True, out_dtype=BF16, name="c_dwout")
    dzc, dlng, dlnb, dwm, dbm = _sgu_bwd(zc, dpg, p["c_ln_g"], p["c_ln_b"], p["wm"], p["bmap"], name="sgu_bwd",
                                         ride=rides.get("sgu_bwd"))
    dhn2 = _mm(dzc, p["c_w_in_t"], out_dtype=BF16, name="c_dhn")
    dw_c_in_t = _mm(dzc, hn2, ta=True, out_dtype=BF16, name="c_dwin")
    dh2, dcnorm = _rms_bwd(h2, p["c_norm"][0], dhn2, res=dh3, name="c_dnorm")
    dh1, g_ffn0 = _ffn_bwd(dh2, h1, 0, p, ffn0, S, rides, grads_ready, {"c_w_in_t": dw_c_in_t, "c_w_out": dw_c_out})

    do = _mm(dh1, w_out_a, tb=True, name="ab_do")
    dy_lru = _mm(dh1, w_out_b, tb=True, out_dtype=BF16, name="ab_dylru")
    dw_out = jnp.concatenate([_mm(o, dh1, ta=True, out_dtype=BF16, name="ab_dwout_a"),
                              _mm(y_lru, dh1, ta=True, out_dtype=BF16, name="ab_dwout_b")], axis=0)
    dq, dk, dv = _attn_bwd(qs, kk, kv, o, lse, do, name="attn_bwd", ride=rides.get("attn_bwd"), **att)
    dq_full = _rope_q_bwd(dq, cos, sin, name="q_rope_bwd")
    dkr = _key_rope_bwd(dk, cos, sin, name="k_rope_bwd")
    n_key = H * LANES
    w_k_p, w_v_p = p["w_kv_p"][:, :n_key], p["w_kv_p"][:, n_key:]
    dcqn = _mm(dq_full, p["w_q_p"], tb=True, name="q_dlat")
    dw_q_p = _mm(cqn, dq_full, ta=True, out_dtype=BF16, name="q_dw")
    dckvn = _mm(dv, w_v_p, tb=True, res=_mm(dk, w_k_p, tb=True, name="k_dlat"), name="v_dlat")
    dw_k_p = _mm(ckvn, dk, ta=True, out_dtype=BF16, name="k_dw")
    dw_v_p = _mm(ckvn, dv, ta=True, out_dtype=BF16, name="v_dw")
    dcq, dqnorm = _rms_bwd(z, p["ab_q_norm"][0], dcqn, cb=Z_Q_BLOCK, out_dtype=BF16, name="q_dnorm")
    dckv, dkvnorm = _rms_bwd(z, p["ab_kv_norm"][0], dckvn, cb=Z_KV_BLOCK, out_dtype=BF16, name="kv_dnorm")
    dxl, dgate, dcw, dcb, dwa, dba, dwx, dbx, dlam = _lru_bwd(z, hs, dy_lru, *lru_par, S=S, name="lru_bwd")
    dz = jnp.concatenate([dxl, dgate, dcq, dckv, dkr], axis=1)
    dhn0 = _mm(dz, p["w_in_p"], tb=True, out_dtype=BF16, name="ab_dhn")
    dw_in_p = _mm(hn0, dz, ta=True, out_dtype=BF16, name="ab_dwin")
    dx, dabnorm = _rms_bwd(xf, p["ab_norm"][0], dhn0, res=dh1, name="ab_dnorm")

    blocks = lambda dd: jnp.stack([dd[i * LRU_BLOCK:(i + 1) * LRU_BLOCK, i * LRU_BLOCK:(i + 1) * LRU_BLOCK]
                                   for i in range(LRU_HEADS)])
    causal = jnp.tril(jnp.ones((CHUNK, CHUNK), F32))
    grads = {
        "ab_norm": dabnorm, "w_in_p": dw_in_p, "ab_q_norm": dqnorm, "w_q_p": dw_q_p,
        "ab_kv_norm": dkvnorm, "w_k_p": dw_k_p, "w_v_p": dw_v_p, "ab_conv_w": dcw[:LRU_CONV][None], "ab_conv_b": dcb,
        "ab_w_rg_a": blocks(dwa)[None], "ab_b_rg_a": dba, "ab_w_rg_x": blocks(dwx)[None], "ab_b_rg_x": dbx,
        "ab_lambda": dlam, "ab_w_out": dw_out,
        "c_norm": dcnorm, "c_w_in_t": dw_c_in_t, "c_ln_g": dlng, "c_ln_b": dlnb,
        "c_w_s": (dwm * causal)[None], "c_b_s": dbm[:, ::SGU_GROUP_DIM].T[None], "c_w_out": dw_c_out,
        "final_norm": dfinal[0],
    }
    for name in ("ffn_norm", "ffn_conv_w", "ffn_conv_b"):
        grads[name] = jnp.stack([g_ffn0[name], g_ffn1[name]])
    for name in ("ffn_gate_t", "ffn_up_t", "ffn_down"):
        grads[name] = [g_ffn0[name], g_ffn1[name]]
    return loss_row, dx.reshape(B, S, D), grads


ANY = pl.BlockSpec(memory_space=pl.ANY)


def _place():
    x, y, c = lax.axis_index("x"), lax.axis_index("y"), lax.axis_index("c")
    chips = [(1 - x, y), (x, 1 - y), (1 - x, 1 - y)]
    return x, y, c, 2 * x + y, (x, y, 1 - c), chips


def _remote(src, dst, send_sems, recv_sems, k, to):
    return pltpu.make_async_remote_copy(src_ref=src, dst_ref=dst, send_sem=send_sems.at[k], recv_sem=recv_sems.at[k],
                                        device_id=to, device_id_type=MESH)


class _Exchange:
    def __init__(self, arrs, out_shapes, n_sems, start, finish):
        self.arrs, self.out_shapes, self.n_sems, self.start, self.finish = list(arrs), out_shapes, n_sems, start, finish

    @property
    def in_specs(self):
        return [ANY] * len(self.arrs)

    @property
    def out_specs(self):
        return [ANY] * len(self.out_shapes)

    @property
    def scratch(self):
        return [pltpu.SemaphoreType.DMA((self.n_sems,)), pltpu.SemaphoreType.DMA((self.n_sems,))]

    def split(self, refs):
        n = len(self.arrs)
        return refs[:n], refs[n:n + len(self.out_shapes)], refs[-2], refs[-1]

    def run(self, name):
        def body(*refs):
            parts = self.split(refs)
            self.start(*parts)
            self.finish(*parts)

        return pl.pallas_call(body, name=name, in_specs=self.in_specs, out_specs=self.out_specs,
                              out_shape=self.out_shapes, scratch_shapes=self.scratch)(*self.arrs)


def _put(buf, piece, idx, axis):
    return lax.dynamic_update_slice_in_dim(buf, jnp.expand_dims(piece, axis).astype(buf.dtype), idx, axis)


def _all_gather(arrs):
    n = len(arrs)
    per = 7

    def start(ins, outs, send_sems, recv_sems):
        x, y, c, j, sib, chips = _place()
        for i in range(n):
            for k, (cx, cy) in enumerate(chips):
                _remote(ins[i].at[:, c], outs[i].at[:, j, c], send_sems, recv_sems, per * i + k, (cx, cy, c)).start()
            _remote(ins[i], outs[i].at[:, j], send_sems, recv_sems, per * i + 6, sib).start()

    def finish(ins, outs, send_sems, recv_sems):
        x, y, c, j, sib, chips = _place()
        passed = []
        for i in range(n):
            for k, (cx, cy) in enumerate(chips):
                got = outs[i].at[:, 2 * cx + cy, c]
                _remote(got, got, send_sems, recv_sems, per * i + k, (cx, cy, c)).wait_recv()
                cp = _remote(got, got, send_sems, recv_sems, per * i + 3 + k, sib)
                cp.start()
                passed.append(cp)
        for i in range(n):
            for k, (cx, cy) in enumerate(chips):
                got = outs[i].at[:, 2 * cx + cy, 1 - c]
                _remote(got, got, send_sems, recv_sems, per * i + 3 + k, sib).wait_recv()
                _remote(ins[i].at[:, c], ins[i].at[:, c], send_sems, recv_sems, per * i + k, sib).wait_send()
            _remote(ins[i], outs[i].at[:, j], send_sems, recv_sems, per * i + 6, sib).wait()
        for cp in passed:
            cp.wait_send()

    shapes = [jax.ShapeDtypeStruct((a.shape[0], N_CHIPS) + a.shape[1:], a.dtype) for a in arrs]
    return _Exchange(arrs, shapes, per * n, start, finish)


class _Offset:
    def __init__(self, sems, k0):
        self.sems, self.k0 = sems, k0

    @property
    def at(self):
        return self

    def __getitem__(self, k):
        return self.sems.at[self.k0 + k]


def _merge(a, b):
    n_in, n_out = len(a.arrs), len(a.out_shapes)

    def both(fa, fb):
        def f(ins, outs, send_sems, recv_sems):
            fa(ins[:n_in], outs[:n_out], send_sems, recv_sems)
            fb(ins[n_in:], outs[n_out:], _Offset(send_sems, a.n_sems), _Offset(recv_sems, a.n_sems))
        return f

    return _Exchange(a.arrs + b.arrs, a.out_shapes + b.out_shapes, a.n_sems + b.n_sems,
                     both(a.start, b.start), both(a.finish, b.finish))


def _pair_swap(arrs):
    n = len(arrs)

    def start(ins, outs, send_sems, recv_sems):
        x, y, c, j, sib, chips = _place()
        for i in range(n):
            _remote(ins[i].at[:, 1 - c], outs[i], send_sems, recv_sems, i, sib).start()

    def finish(ins, outs, send_sems, recv_sems):
        x, y, c, j, sib, chips = _place()
        for i in range(n):
            _remote(ins[i].at[:, 1 - c], outs[i], send_sems, recv_sems, i, sib).wait()

    shapes = [jax.ShapeDtypeStruct((a.shape[0],) + a.shape[2:], a.dtype) for a in arrs]
    return _Exchange(arrs, shapes, n, start, finish)


def _pair_send(arrs):
    n = len(arrs)

    def start(ins, outs, send_sems, recv_sems):
        x, y, c, j, sib, chips = _place()
        for i in range(n):
            _remote(ins[i], outs[i], send_sems, recv_sems, i, sib).start()

    def finish(ins, outs, send_sems, recv_sems):
        x, y, c, j, sib, chips = _place()
        for i in range(n):
            _remote(ins[i], outs[i], send_sems, recv_sems, i, sib).wait()

    shapes = [jax.ShapeDtypeStruct(a.shape, a.dtype) for a in arrs]
    return _Exchange(arrs, shapes, n, start, finish)


def _chip_exchange(arrs, *, scatter):
    n = len(arrs)

    def copies(ins, outs, send_sems, recv_sems):
        x, y, c, j, sib, chips = _place()
        return [(_remote(ins[i].at[2 * cx + cy] if scatter else ins[i], outs[i].at[j], send_sems, recv_sems,
                         3 * i + k, (cx, cy, c)),
                 _remote(outs[i].at[2 * cx + cy], outs[i].at[2 * cx + cy], send_sems, recv_sems, 3 * i + k, (cx, cy, c)))
                for i in range(n) for k, (cx, cy) in enumerate(chips)]

    def start(*refs):
        for out, _ in copies(*refs):
            out.start()

    def finish(*refs):
        for out, back in copies(*refs):
            back.wait_recv()
            out.wait_send()

    shapes = [jax.ShapeDtypeStruct((N_CHIPS,) + a.shape[-2:], a.dtype) for a in arrs]
    return _Exchange(arrs, shapes, 3 * n, start, finish)


FLAT_ROWS = 512


def _pair_add(sharded, from_sib, *, name):
    n, _, R, L = sharded.shape
    tr = _tile(R, FLAT_ROWS, 16)

    def body(s_ref, b_ref, o_ref):
        own = jnp.where(lax.axis_index("c") == 0, s_ref[:, 0], s_ref[:, 1])
        o_ref[...] = (own.astype(F32) + b_ref[...].astype(F32)).astype(BF16)

    spec = pl.BlockSpec((n, tr, L), lambda i: (0, i, 0))
    return pl.pallas_call(
        body, name=name, grid=(R // tr,), in_specs=[pl.BlockSpec((n, 2, tr, L), lambda i: (0, 0, i, 0)), spec],
        out_specs=spec, out_shape=jax.ShapeDtypeStruct((n, R, L), BF16), compiler_params=_cparams(("parallel",)),
    )(sharded, from_sib)


def _chip_sum(arrived, pair, *, name):
    n, R, L = arrived.shape
    tr = _tile(R, FLAT_ROWS, 16)

    def body(a_ref, p_ref, o_ref):
        me = 2 * lax.axis_index("x") + lax.axis_index("y")
        acc = None
        for k in range(n):
            term = jnp.where(me == k, p_ref[k], a_ref[k]).astype(F32)
            acc = term if acc is None else acc + term
        o_ref[...] = acc

    spec = pl.BlockSpec((n, tr, L), lambda i: (0, i, 0))
    return pl.pallas_call(
        body, name=name, grid=(R // tr,), in_specs=[spec, spec], out_specs=pl.BlockSpec((tr, L), lambda i: (i, 0)),
        out_shape=jax.ShapeDtypeStruct((R, L), F32), compiler_params=_cparams(("parallel",)),
    )(arrived, pair)


def _sum_slots(buf, *, name):
    n, R, L = buf.shape
    tr = _tile(R, FLAT_ROWS, 16)

    def body(b_ref, o_ref):
        acc = b_ref[0].astype(F32)
        for k in range(1, n):
            acc = acc + b_ref[k].astype(F32)
        o_ref[...] = acc

    return pl.pallas_call(
        body, name=name, grid=(R // tr,), in_specs=[pl.BlockSpec((n, tr, L), lambda i: (0, i, 0))],
        out_specs=pl.BlockSpec((tr, L), lambda i: (i, 0)),
        out_shape=jax.ShapeDtypeStruct((R, L), F32), compiler_params=_cparams(("parallel",)),
    )(buf)


def _adamw_update(w, g, m, v):
    c1 = 1.0 - ADAM_B1 ** ADAM_STEP
    c2 = 1.0 - ADAM_B2 ** ADAM_STEP
    m = ADAM_B1 * m + (1.0 - ADAM_B1) * g
    v = ADAM_B2 * v + (1.0 - ADAM_B2) * (g * g)
    return -ADAM_LR * ((m / c1) / (jnp.sqrt(v / c2) + ADAM_EPS) + ADAM_WD * w), m, v


def _adamw_halves(w, m, v, own, other, *, name):
    NL, R, L = w.shape
    h = R // 2
    tr = _tile(h, FLAT_ROWS, 16)
    nt = h // tr

    def body(*refs):
        w_ref, m_ref, v_ref = refs[:3]
        own_refs, other_refs = refs[3:3 + NL], refs[3 + NL:3 + 2 * NL]
        d_ref, nm_ref, nv_ref, g_ref = refs[3 + 2 * NL:]
        layer, half = pl.program_id(0), pl.program_id(1)
        mine = half == lax.axis_index("c")
        g = jnp.where(mine, own_refs[0][...], other_refs[0][...])
        for l in range(1, NL):
            g = jnp.where(layer == l, jnp.where(mine, own_refs[l][...], other_refs[l][...]), g)
        d, mm, vv = _adamw_update(w_ref[0], g, m_ref[0], v_ref[0])
        d_ref[0], nm_ref[0], nv_ref[0], g_ref[0] = d, mm, vv, g

    spec = pl.BlockSpec((1, tr, L), lambda l, hh, i: (l, hh * nt + i, 0))
    part = pl.BlockSpec((tr, L), lambda l, hh, i: (i, 0))
    sh = jax.ShapeDtypeStruct((NL, R, L), F32)
    return pl.pallas_call(
        body, name=name, grid=(NL, 2, nt), in_specs=[spec] * 3 + [part] * (2 * NL), out_specs=[spec] * 4,
        out_shape=[sh] * 4, compiler_params=_cparams(("parallel", "parallel", "parallel")),
    )(w, m, v, *own, *other)


def _adamw(w, g, m, v, *, name):
    NL, R, L = w.shape
    tr = _tile(R, FLAT_ROWS, 16)

    def body(w_ref, g_ref, m_ref, v_ref, d_ref, nm_ref, nv_ref):
        d_ref[...], nm_ref[...], nv_ref[...] = _adamw_update(w_ref[...], g_ref[...], m_ref[...], v_ref[...])

    spec = pl.BlockSpec((1, tr, L), lambda l, i: (l, i, 0))
    sh = jax.ShapeDtypeStruct((NL, R, L), F32)
    return pl.pallas_call(
        body, name=name, grid=(NL, R // tr), in_specs=[spec] * 4, out_specs=[spec] * 3, out_shape=[sh] * 3,
        compiler_params=_cparams(("parallel", "parallel")),
    )(w, g, m, v)


WEIGHT_NAMES = ["ab_norm", "ab_w_in", "ab_q_norm", "ab_w_q_b", "ab_kv_norm", "ab_w_kv_b", "ab_conv_w", "ab_conv_b",
                "ab_w_rg_a", "ab_b_rg_a", "ab_w_rg_x", "ab_b_rg_x", "ab_lambda", "ab_w_out", "c_norm", "c_w_in",
                "c_ln_g", "c_ln_b", "c_w_s", "c_b_s", "c_w_out", "ffn_norm", "ffn_w_gate", "ffn_w_up", "ffn_conv_w",
                "ffn_conv_b", "ffn_w_down", "final_norm"]
BIG = {"ab_w_in": 2, "ab_w_q_b": 2, "ab_w_kv_b": 2, "ab_w_out": 1, "c_w_in": 2, "c_w_out": 1,
       "ffn_w_gate": 2, "ffn_w_up": 2, "ffn_w_down": 1}
SMALL_SHARDED = {"ab_conv_w": 2, "c_norm": 1, "c_ln_g": 1, "c_ln_b": 1, "ffn_conv_w": 2}
SMALL_REPLICATED = [n for n in WEIGHT_NAMES if n not in BIG and n not in SMALL_SHARDED]


def _rows(n_elems, mult):
    r = -(-n_elems // LANES)
    return -(-r // mult) * mult


def _flat(parts, rows):
    flat = jnp.concatenate([a.reshape(-1) for a in parts])
    return jnp.pad(flat, (0, rows * LANES - flat.shape[0])).reshape(rows, LANES)


def _unflat(flat, shapes):
    flat = flat.reshape(-1)
    out, off = [], 0
    for s in shapes:
        n = math.prod(s)
        out.append(flat[off:off + n].reshape(s))
        off += n
    return out


def _join_shards(a, axis):
    a = jnp.moveaxis(a, 0, axis)
    return a.reshape(a.shape[:axis] + (a.shape[axis] * a.shape[axis + 1],) + a.shape[axis + 2:])


def kernel(x, positions, ab_norm, ab_w_in, ab_q_norm, ab_w_q_b, ab_kv_norm, ab_w_kv_b, ab_conv_w, ab_conv_b, ab_w_rg_a, ab_b_rg_a, ab_w_rg_x, ab_b_rg_x, ab_lambda, ab_w_out, c_norm, c_w_in, c_ln_g, c_ln_b, c_w_s, c_b_s, c_w_out, ffn_norm, ffn_w_gate, ffn_w_up, ffn_conv_w, ffn_conv_b, ffn_w_down, final_norm, loss_target, m_ab_norm, m_ab_w_in, m_ab_q_norm, m_ab_w_q_b, m_ab_kv_norm, m_ab_w_kv_b, m_ab_conv_w, m_ab_conv_b, m_ab_w_rg_a, m_ab_b_rg_a, m_ab_w_rg_x, m_ab_b_rg_x, m_ab_lambda, m_ab_w_out, m_c_norm, m_c_w_in, m_c_ln_g, m_c_ln_b, m_c_w_s, m_c_b_s, m_c_w_out, m_ffn_norm, m_ffn_w_gate, m_ffn_w_up, m_ffn_conv_w, m_ffn_conv_b, m_ffn_w_down, m_final_norm, v_ab_norm, v_ab_w_in, v_ab_q_norm, v_ab_w_q_b, v_ab_kv_norm, v_ab_w_kv_b, v_ab_conv_w, v_ab_conv_b, v_ab_w_rg_a, v_ab_b_rg_a, v_ab_w_rg_x, v_ab_b_rg_x, v_ab_lambda, v_ab_w_out, v_c_norm, v_c_w_in, v_c_ln_g, v_c_ln_b, v_c_w_s, v_c_b_s, v_c_w_out, v_ffn_norm, v_ffn_w_gate, v_ffn_w_up, v_ffn_conv_w, v_ffn_conv_b, v_ffn_w_down, v_final_norm):
    given = dict(locals())
    w = {n: given[n] for n in WEIGHT_NAMES}
    m = {n: given["m_" + n] for n in WEIGHT_NAMES}
    v = {n: given["v_" + n] for n in WEIGHT_NAMES}
    c = lax.axis_index("c")
    chip = 2 * lax.axis_index("x") + lax.axis_index("y")

    halves = lambda a: a.reshape(a.shape[0], 2, a.shape[1] // 2, a.shape[2])
    tr = lambda a: jnp.swapaxes(a, 1, 2)
    send = {"ab_w_in": w["ab_w_in"], "ab_w_q_b": w["ab_w_q_b"], "ab_w_kv_b": w["ab_w_kv_b"], "ab_w_out": w["ab_w_out"],
            "c_w_in": tr(w["c_w_in"]), "c_w_out": w["c_w_out"], "ffn_w_gate": tr(w["ffn_w_gate"]),
            "ffn_w_up": tr(w["ffn_w_up"]), "ffn_w_down": w["ffn_w_down"]}
    small_rows = _rows(sum(w[n].size for n in SMALL_SHARDED), 16)
    small_sh = _flat([w[n] for n in SMALL_SHARDED], small_rows).reshape(1, 2, small_rows // 2, LANES)
    first_names = ["ab_w_in", "ab_w_q_b", "ab_w_kv_b", "ab_w_out"]
    mine = {n: halves(send[n].astype(BF16)) for n in BIG}

    def put_own(own, arrived):
        return arrived.reshape(arrived.shape[0], -1, arrived.shape[-1])

    p = {"ab_norm": w["ab_norm"], "ffn_gate_t": {}, "ffn_up_t": {}, "ffn_down": {}}
    first = [mine[n] for n in first_names] + [small_sh]

    def first_arrived(got):
        full = {n: put_own(o, a) for n, o, a in zip(first_names + ["small"], first, got)}
        unshard = lambda a: jnp.swapaxes(a.reshape(N_CHIPS, -1, a.shape[-1]), 0, 1).reshape(-1, N_CHIPS * a.shape[-1])
        p.update(_prep_big(unshard(full["ab_w_in"][0]), unshard(full["ab_w_q_b"][0]), unshard(full["ab_w_kv_b"][0])))
        p["ab_w_out"] = full["ab_w_out"][0]
        small_full = dict(w)
        off = 0
        small_got = full["small"].reshape(N_CHIPS, -1)
        for n, ax in SMALL_SHARDED.items():
            seg = small_got[:, off:off + w[n].size].reshape((N_CHIPS,) + w[n].shape)
            small_full[n] = _join_shards(seg, ax)
            off += w[n].size
        p.update(_prep_small(small_full))

    def weights_ride(parts):
        def sink(arrived):
            for (own, setter), a in zip(parts, arrived):
                setter(put_own(own, a)[0])
        return _all_gather([own for own, _ in parts]), sink

    ffn_keys = {"ffn_gate_t": "ffn_w_gate", "ffn_up_t": "ffn_w_up", "ffn_down": "ffn_w_down"}
    ffn_part = lambda key, l: (mine[ffn_keys[key]][l:l + 1], functools.partial(p[key].__setitem__, l))
    rides = {
        "ab_norm": (_all_gather(first), first_arrived),
        "attn_fwd": weights_ride([ffn_part(key, 0) for key in ffn_keys]),
        "ffn0_gate": weights_ride([ffn_part("ffn_gate_t", 1)]),
        "ffn0_up": weights_ride([ffn_part("ffn_up_t", 1)]),
        "ffn0_down": weights_ride([ffn_part("ffn_down", 1),
                                   (mine["c_w_in"], functools.partial(p.__setitem__, "c_w_in_t")),
                                   (mine["c_w_out"], functools.partial(p.__setitem__, "c_w_out"))]),
    }

    def chip_sums(pair, arrived, tag):
        return [_chip_sum(a, b, name=f"grad_chip_sum_{tag}{i}") for i, (a, b) in enumerate(zip(arrived, pair))]

    half_of = {}

    def grads_ready(layer, ready):
        if layer == 1:
            named = {"gate1": ready["ffn_gate_t"], "up1": ready["ffn_up_t"], "down1": ready["ffn_down"]}
            hosts = {"sgu_bwd": ["down1"], "ffn0_dactbwd": ["gate1", "up1"]}
        else:
            named = {"c_in": ready["c_w_in_t"], "c_out": ready["c_w_out"], "gate0": ready["ffn_gate_t"],
                     "up0": ready["ffn_up_t"], "down0": ready["ffn_down"]}
            hosts = {"attn_bwd": ["c_in", "c_out", "down0", "gate0", "up0"]}
        tag = f"f{layer}"
        sharded = [a.reshape(N_CHIPS, 2, -1, a.shape[-1]) for a in named.values()]

        def paired(from_sib):
            pair = {k: _pair_add(a, b, name=f"grad_pair_add_{tag}{i}")
                    for i, (k, a, b) in enumerate(zip(named, sharded, from_sib))}
            for kernel_name, keys in hosts.items():
                def sink(arrived, keys=keys, kernel_name=kernel_name):
                    half_of.update(zip(keys, chip_sums([pair[k] for k in keys], arrived, f"{tag}_{kernel_name}")))
                rides[kernel_name] = (_chip_exchange([pair[k] for k in keys], scatter=True), sink)

        rides[f"ffn{layer}_dnorm"] = (_pair_swap(sharded), paired)

    loss_row, grad_x, g = _local_step(x, positions, loss_target, p, rides, grads_ready)

    cols = lambda a, n: jnp.swapaxes(a.reshape(a.shape[0], N_CHIPS, n), 0, 1)
    n_in, n_q, n_kv = w["ab_w_in"].shape[2], w["ab_w_q_b"].shape[2], w["ab_w_kv_b"].shape[2]
    small_names = SMALL_REPLICATED + list(SMALL_SHARDED)
    rs = _rows(sum(g[n].size for n in small_names) + LANES, FLAT_ROWS)
    small = _flat([loss_row] + [g[n] for n in small_names], rs)
    slot = (jnp.arange(2) == c)[:, None, None]
    last = [cols(_unperm_w_in(g["w_in_p"]), n_in), cols(_from_head_blocks(g["w_q_p"], QK_NOPE + QK_ROPE), n_q),
            cols(_join_kv(g["w_k_p"], g["w_v_p"]), n_kv), g["ab_w_out"]]
    last = [a.reshape(N_CHIPS, 2, -1, a.shape[-1]) for a in last]
    *from_sib, small_sib = _merge(_pair_swap(last), _pair_send([small])).run("tail_pair")
    pair = [_pair_add(a, b, name=f"grad_pair_add_b{i}") for i, (a, b) in enumerate(zip(last, from_sib))]
    pair_small = _sum_slots(jnp.where(slot, small[None], small_sib[None]), name="small_pair_sum")
    my_small = lax.dynamic_index_in_dim(pair_small.reshape(2, rs // 2, LANES), c, axis=0, keepdims=False)
    *arrived, all_small = _merge(_chip_exchange(pair, scatter=True), _chip_exchange([my_small], scatter=False)).run("tail_chip")
    half_of.update(zip(["in", "q", "kv", "out"], chip_sums(pair, arrived, "b")))
    half_of["small"] = _sum_slots(_put(all_small, my_small, chip, 0), name="small_chip_sum")
    keys = ("in", "q", "kv", "out", "c_in", "c_out", "gate0", "gate1", "up0", "up1", "down0", "down1", "small")
    other_half = dict(zip(keys, _pair_send([half_of[k] for k in keys]).run("grad_pair_share")))
    small_sum = jnp.where(slot, half_of["small"][None], other_half["small"][None]).reshape(rs, LANES)
    whole = lambda k: jnp.where(slot, half_of[k][None], other_half[k][None]).reshape(-1, half_of[k].shape[-1])
    grads_t = {"ab_w_in": whole("in").T[None], "ab_w_q_b": whole("q").T[None]}
    grads = {"ab_w_kv_b": whole("kv")[None], "c_w_in": whole("c_in").T[None], **{n: tr(a) for n, a in grads_t.items()}}
    by_halves = {"ab_w_out": (("out",), False), "c_w_out": (("c_out",), False), "ffn_w_down": (("down0", "down1"), False),
                 "ffn_w_gate": (("gate0", "gate1"), True), "ffn_w_up": (("up0", "up1"), True)}

    small_parts = _unflat(small_sum, [(1, LANES)] + [g[n].shape for n in small_names])
    loss = small_parts[0][0, 0]
    for n, a in zip(small_names, small_parts[1:]):
        if n in SMALL_SHARDED:
            ax = SMALL_SHARDED[n]
            a = lax.dynamic_slice_in_dim(a, chip * w[n].shape[ax], w[n].shape[ax], axis=ax)
        grads[n] = a.reshape(w[n].shape)

    delta, new_m, new_v = {}, {}, {}
    for n in BIG:
        if n in by_halves:
            ks, transposed = by_halves[n]
            view = tr if transposed else (lambda a: a)
            out = _adamw_halves(view(w[n]), view(m[n]), view(v[n]), [half_of[k] for k in ks], [other_half[k] for k in ks],
                                name=f"adamw_{n}")
            delta[n], new_m[n], new_v[n], grads[n] = (view(a) for a in out)
        elif n in grads_t:
            out = _adamw(tr(w[n]), grads_t[n], tr(m[n]), tr(v[n]), name=f"adamw_{n}")
            delta[n], new_m[n], new_v[n] = (tr(a) for a in out)
        else:
            delta[n], new_m[n], new_v[n] = _adamw(w[n], grads[n], m[n], v[n], name=f"adamw_{n}")
    small_all = [n for n in WEIGHT_NAMES if n not in BIG]
    ra = _rows(sum(w[n].size for n in small_all), FLAT_ROWS)
    pack = lambda d: _flat([d[n] for n in small_all], ra)[None]
    out = _adamw(pack(w), pack(grads), pack(m), pack(v), name="adamw_small")
    shapes = [w[n].shape for n in small_all]
    for d, flat in zip((delta, new_m, new_v), out):
        d.update(zip(small_all, _unflat(flat, shapes)))
    return (loss, grad_x, *[grads[n] for n in WEIGHT_NAMES], *[delta[n] for n in WEIGHT_NAMES],
            *[new_m[n] for n in WEIGHT_NAMES], *[new_v[n] for n in WEIGHT_NAMES])
```

```python
import functools
import math

import jax
import jax.numpy as jnp
from jax import lax
from jax.experimental import pallas as pl
from jax.experimental.pallas import tpu as pltpu

F32 = jnp.float32
BF16 = jnp.bfloat16
MESH = pl.DeviceIdType.MESH

D_MODEL = 1024
MLA_HEADS = 8
Q_LORA = 256
KV_LORA = 128
QK_NOPE = 64
QK_ROPE = 32
V_HEAD = 64
LRU_WIDTH = 512
LRU_HEADS = 8
LRU_BLOCK = 64
LRU_CONV = 4
LRU_C = 8.0
CHUNK = 128
SGU_GROUPS = 8
SGU_WIDTH = 1024
D_FF = 2816
FFN_CONV = 3
NORM_EPS = 1e-6
ROPE_BASE = 10000.0
AB_IN_PAD = 1536
ADAM_LR = 0.001
ADAM_B1 = 0.9
ADAM_B2 = 0.999
ADAM_EPS = 1e-08
ADAM_WD = 0.01
ADAM_STEP = 10

N_CHIPS = 4
LANES = 128
VMEM_LIMIT = 56 * 1024 * 1024
ROW_TILE = 256
SGU_TILE = 512
NORM_TILE = 1024
MM_TM, MM_TN, MM_TK = 1024, 1536, 2816
MM_TM_T, MM_TK_T = 1408, 1024
GELU_C = math.sqrt(2.0 / math.pi)


def _cparams(sem):
    return pltpu.CompilerParams(dimension_semantics=sem, vmem_limit_bytes=VMEM_LIMIT)


def _tile(n, target, mult=LANES):
    t = (min(n, target) // mult) * mult
    while t >= mult:
        if n % t == 0:
            return t
        t -= mult
    return n


GELU_K = GELU_C * 0.044715


def _gelu(x):
    t = jnp.tanh(x * (GELU_C + GELU_K * (x * x)))
    hx = 0.5 * x
    return hx + hx * t


def _gelu_and_grad(x):
    x2 = x * x
    t = jnp.tanh(x * (GELU_C + GELU_K * x2))
    hx = 0.5 * x
    dg = (0.5 + 0.5 * t) + (hx * (1.0 - t * t)) * (GELU_C + (3.0 * GELU_K) * x2)
    return hx + hx * t, dg


def _sigmoid(x):
    return 1.0 / (1.0 + jnp.exp(-x))


def _shift_rows(x, d, fill_rows):
    ext = jnp.concatenate([fill_rows, x], axis=0)
    return pltpu.roll(ext, d, 0)[8:]


def _shift_rows_up(x, d, fill_rows):
    n = x.shape[0]
    ext = jnp.concatenate([x, fill_rows], axis=0)
    return pltpu.roll(ext, n + 8 - d, 0)[:n]


def _dot(a, b, dims):
    return lax.dot_general(a.astype(BF16), b.astype(BF16), (dims, ((), ())), preferred_element_type=F32)


def _dot_nn(a, b):
    return _dot(a, b, ((1,), (0,)))


def _dot_nt(a, b):
    return _dot(a, b, ((1,), (1,)))


def _dot_tn(a, b):
    return _dot(a, b, ((0,), (0,)))


def _mm(a, b, *, name, ta=False, tb=False, res=None, out_dtype=F32, ride=None, also=None):
    if ta:
        K, M = a.shape
    else:
        M, K = a.shape
    N = b.shape[0] if tb else b.shape[1]
    tm = _tile(M, MM_TM_T if ta else (MM_TM if K <= MM_TM else MM_TM // 2), LANES if ta else 8)
    tn = _tile(N, MM_TN, LANES)
    tk = _tile(K, MM_TK_T if ta else MM_TK, LANES)
    nk = K // tk
    a_spec = pl.BlockSpec((tk, tm), lambda j, i, k: (k, i)) if ta else pl.BlockSpec((tm, tk), lambda j, i, k: (i, k))
    b_spec = pl.BlockSpec((tn, tk), lambda j, i, k: (j, k)) if tb else pl.BlockSpec((tk, tn), lambda j, i, k: (k, j))
    o_spec = pl.BlockSpec((tm, tn), lambda j, i, k: (i, j))
    dims = ((0,) if ta else (1,), (1,) if tb else (0,))
    has_res = res is not None
    pairs = [(a, b)] + ([also] if also is not None else [])
    n_ab = 2 * len(pairs)

    def body(*refs):
        r_ref = refs[n_ab] if has_res else None
        o_ref = refs[n_ab + 1] if has_res else refs[n_ab]
        p = _dot(refs[0][...], refs[1][...], dims)
        if also is not None:
            p = p + _dot(refs[2][...], refs[3][...], dims)

        def finish(r):
            if has_res:
                r = r + r_ref[...].astype(F32)
            o_ref[...] = r.astype(out_dtype)

        if nk == 1:
            finish(p)
            return
        acc_ref = refs[-1]
        k = pl.program_id(2)

        @pl.when(k == 0)
        def _():
            acc_ref[...] = p

        @pl.when(jnp.logical_and(k > 0, k < nk - 1))
        def _():
            acc_ref[...] += p

        @pl.when(k == nk - 1)
        def _():
            finish(acc_ref[...] + p)

    in_specs = [a_spec, b_spec] * len(pairs) + ([o_spec] if has_res else [])
    args = tuple(x for pair in pairs for x in pair) + ((res,) if has_res else ())
    return _pcall(
        body, name=name, grid=(N // tn, M // tm, nk), in_specs=in_specs, out_specs=[o_spec],
        out_shape=[jax.ShapeDtypeStruct((M, N), out_dtype)], args=args,
        scratch=[pltpu.VMEM((tm, tn), F32)] if nk > 1 else [], sem=("parallel", "parallel", "arbitrary"), ride=ride)[0]


def _rms_fwd(x, g, *, name, cb=0, out_dtype=BF16, ride=None):
    T = x.shape[0]
    W = g.shape[-1]
    g = g.reshape(1, W)
    tt = _tile(T, NORM_TILE, 16)

    def body(x_ref, g_ref, o_ref):
        xf = x_ref[...].astype(F32)
        rstd = lax.rsqrt(jnp.mean(xf * xf, axis=-1, keepdims=True) + NORM_EPS)
        o_ref[...] = (xf * rstd * g_ref[...]).astype(out_dtype)

    return _pcall(
        body, name=name, grid=(T // tt,),
        in_specs=[pl.BlockSpec((tt, W), lambda i: (i, cb)), pl.BlockSpec((1, W), lambda i: (0, 0))],
        out_specs=[pl.BlockSpec((tt, W), lambda i: (i, 0))], out_shape=[jax.ShapeDtypeStruct((T, W), out_dtype)],
        args=(x, g), sem=("parallel",), ride=ride)[0]


def _rms_bwd(x, g, dy, *, name, cb=0, res=None, out_dtype=F32, ride=None):
    T = x.shape[0]
    W = g.shape[-1]
    g = g.reshape(1, W)
    tt = _tile(T, NORM_TILE // 2, 16)
    has_res = res is not None

    def body(*refs):
        if has_res:
            x_ref, g_ref, dy_ref, r_ref, dx_ref, dg_ref = refs
        else:
            x_ref, g_ref, dy_ref, dx_ref, dg_ref = refs
        xf = x_ref[...].astype(F32)
        dyf = dy_ref[...].astype(F32)
        rstd = lax.rsqrt(jnp.mean(xf * xf, axis=-1, keepdims=True) + NORM_EPS)
        xhat = xf * rstd
        dxhat = dyf * g_ref[...]
        dx = rstd * (dxhat - xhat * jnp.mean(dxhat * xhat, axis=-1, keepdims=True))
        if has_res:
            dx = dx + r_ref[...].astype(F32)
        dx_ref[...] = dx.astype(out_dtype)
        part = jnp.sum(dyf * xhat, axis=0, keepdims=True)

        @pl.when(pl.program_id(0) == 0)
        def _():
            dg_ref[...] = part

        @pl.when(pl.program_id(0) > 0)
        def _():
            dg_ref[...] += part

    row = pl.BlockSpec((tt, W), lambda i: (i, 0))
    in_specs = [pl.BlockSpec((tt, W), lambda i: (i, cb)), pl.BlockSpec((1, W), lambda i: (0, 0)), row]
    args = (x, g, dy)
    if has_res:
        in_specs.append(row)
        args = args + (res,)
    return _pcall(
        body, name=name, grid=(T // tt,), in_specs=in_specs,
        out_specs=[row, pl.BlockSpec((1, W), lambda i: (0, 0))],
        out_shape=[jax.ShapeDtypeStruct((T, W), out_dtype), jax.ShapeDtypeStruct((1, W), F32)], args=args, ride=ride)


def _final_fwd_bwd(h, g, target, *, name):
    T, W = h.shape
    g = g.reshape(1, W)
    tt = _tile(T, NORM_TILE, 16)

    def body(x_ref, g_ref, t_ref, loss_ref, dx_ref, dg_ref):
        xf = x_ref[...]
        rstd = lax.rsqrt(jnp.mean(xf * xf, axis=-1, keepdims=True) + NORM_EPS)
        xhat = xf * rstd
        err = xhat * g_ref[...] - t_ref[...]
        lpart = jnp.zeros((1, LANES), F32) + (0.5 / W) * jnp.sum(err * err)
        dyf = err * (1.0 / W)
        dxhat = dyf * g_ref[...]
        dx_ref[...] = rstd * (dxhat - xhat * jnp.mean(dxhat * xhat, axis=-1, keepdims=True))
        part = jnp.sum(dyf * xhat, axis=0, keepdims=True)

        @pl.when(pl.program_id(0) == 0)
        def _():
            dg_ref[...] = part
            loss_ref[...] = lpart

        @pl.when(pl.program_id(0) > 0)
        def _():
            dg_ref[...] += part
            loss_ref[...] += lpart

    row = pl.BlockSpec((tt, W), lambda i: (i, 0))
    return pl.pallas_call(
        body, name=name, grid=(T // tt,),
        in_specs=[row, pl.BlockSpec((1, W), lambda i: (0, 0)), row],
        out_specs=[pl.BlockSpec((1, LANES), lambda i: (0, 0)), row, pl.BlockSpec((1, W), lambda i: (0, 0))],
        out_shape=[jax.ShapeDtypeStruct((1, LANES), F32), jax.ShapeDtypeStruct((T, W), F32),
                   jax.ShapeDtypeStruct((1, W), F32)],
        compiler_params=_cparams(("arbitrary",)),
    )(h, g, target)


def _swap16(x):
    lane = lax.broadcasted_iota(jnp.int32, x.shape, 1)
    return jnp.where((lane % 32) < 16, pltpu.roll(x, LANES - 16, 1), pltpu.roll(x, 16, 1))


def _rope(x, c, s):
    return x * c + _swap16(x) * s


def _rope_t(d, c, s):
    return d * c + _swap16(d * s)


def _head_block_map(fn, x, cos, sin, *, name):
    T, W = x.shape
    tt = _tile(T, NORM_TILE, 16)

    def body(x_ref, c_ref, s_ref, o_ref):
        c, s = c_ref[...], s_ref[...]
        for h in range(W // LANES):
            lanes = slice(h * LANES, (h + 1) * LANES)
            o_ref[:, lanes] = fn(x_ref[:, lanes], c, s).astype(BF16)

    tab = pl.BlockSpec((tt, LANES), lambda i: (i, 0))
    blk = pl.BlockSpec((tt, W), lambda i: (i, 0))
    return pl.pallas_call(
        body, name=name, grid=(T // tt,), in_specs=[blk, tab, tab], out_specs=blk,
        out_shape=jax.ShapeDtypeStruct((T, W), BF16), compiler_params=_cparams(("parallel",)),
    )(x, cos, sin)


def _rope_q(q, cos, sin, *, name):
    scale = _attn_scale()
    return _head_block_map(lambda x, c, s: _rope(x, c, s) * scale, q, cos, sin, name=name)


def _rope_q_bwd(dq, cos, sin, *, name):
    return _head_block_map(_rope_t, dq, cos, sin, name=name)


def _key_blocks(kv, z, cos, sin, *, kpe_block, name):
    T = kv.shape[0]
    tt = _tile(T, NORM_TILE, 16)
    W = MLA_HEADS * LANES

    def body(kv_ref, z_ref, c_ref, s_ref, o_ref):
        kr = _rope(z_ref[...], c_ref[...], s_ref[...])
        for h in range(MLA_HEADS):
            lanes = slice(h * LANES, (h + 1) * LANES)
            o_ref[:, lanes] = (kv_ref[:, lanes].astype(F32) + kr).astype(BF16)

    tab = pl.BlockSpec((tt, LANES), lambda i: (i, 0))
    blk = pl.BlockSpec((tt, W), lambda i: (i, 0))
    return pl.pallas_call(
        body, name=name, grid=(T // tt,),
        in_specs=[blk, pl.BlockSpec((tt, LANES), lambda i: (i, kpe_block)), tab, tab], out_specs=blk,
        out_shape=jax.ShapeDtypeStruct((T, W), BF16), compiler_params=_cparams(("parallel",)),
    )(kv, z, cos, sin)


def _key_rope_bwd(dk, cos, sin, *, name):
    T = dk.shape[0]
    tt = _tile(T, NORM_TILE, 16)

    def body(d_ref, c_ref, s_ref, o_ref):
        d = d_ref[:, :LANES]
        for h in range(1, MLA_HEADS):
            d = d + d_ref[:, h * LANES:(h + 1) * LANES]
        lane = lax.broadcasted_iota(jnp.int32, d.shape, 1)
        d = jnp.where(jnp.logical_and(lane >= QK_NOPE, lane < QK_NOPE + QK_ROPE), d, 0.0)
        o_ref[...] = _rope_t(d, c_ref[...], s_ref[...]).astype(BF16)

    tab = pl.BlockSpec((tt, LANES), lambda i: (i, 0))
    return pl.pallas_call(
        body, name=name, grid=(T // tt,),
        in_specs=[pl.BlockSpec((tt, MLA_HEADS * LANES), lambda i: (i, 0)), tab, tab], out_specs=tab,
        out_shape=jax.ShapeDtypeStruct((T, LANES), BF16), compiler_params=_cparams(("parallel",)),
    )(dk, cos, sin)


ATT_BLOCK = 512


def _attn_scale():
    return float((QK_NOPE + QK_ROPE) ** -0.5)


def _causal_mask(qi, kj, tq, tk):
    row = qi * tq + lax.broadcasted_iota(jnp.int32, (tq, tk), 0)
    col = kj * tk + lax.broadcasted_iota(jnp.int32, (tq, tk), 1)
    return col <= row


def _pcall(body, *, name, grid, in_specs, out_specs, out_shape, args, scratch=(), sem=None, ride=None):
    n_in, n_out, n_scr = len(args), len(out_shape), len(scratch)
    if ride is None:
        return pl.pallas_call(
            body, name=name, grid=grid, in_specs=list(in_specs), out_specs=list(out_specs), out_shape=list(out_shape),
            scratch_shapes=list(scratch), compiler_params=_cparams(sem or ("arbitrary",) * len(grid)))(*args)
    ex, sink = ride
    o0 = n_in + len(ex.arrs)
    s0 = o0 + n_out + len(ex.out_shapes)

    def hosted(*refs):
        parts = (refs[n_in:o0], refs[o0 + n_out:s0], refs[-2], refs[-1])
        ids = [pl.program_id(i) for i in range(len(grid))]
        pl.when(functools.reduce(jnp.logical_and, [i == 0 for i in ids]))(lambda: ex.start(*parts))
        body(*refs[:n_in], *refs[o0:o0 + n_out], *refs[s0:s0 + n_scr])
        pl.when(functools.reduce(jnp.logical_and, [i == n - 1 for i, n in zip(ids, grid)]))(lambda: ex.finish(*parts))

    outs = pl.pallas_call(
        hosted, name=name, grid=grid, in_specs=list(in_specs) + ex.in_specs, out_specs=list(out_specs) + ex.out_specs,
        out_shape=list(out_shape) + ex.out_shapes, scratch_shapes=list(scratch) + ex.scratch,
        compiler_params=_cparams(("arbitrary",) * len(grid)))(*args, *ex.arrs)
    sink(outs[n_out:])
    return outs[:n_out]


PAIRS = MLA_HEADS // 2


def _own_lanes(x, first):
    lane = lax.broadcasted_iota(jnp.int32, x.shape, 1)
    return jnp.where((lane < V_HEAD) if first else (lane >= V_HEAD), x, 0.0)


def _lane_sums_as_row(x):
    hi = x.astype(BF16)
    lo = (x - hi.astype(F32)).astype(BF16)
    ones = jnp.ones((8, LANES), BF16)
    return (_dot_nt(ones, hi) + _dot_nt(ones, lo))[0:1, :]


def _attn_fwd(q, k, kv, *, B, S, v_block0, name, ride=None):
    tq = tk = min(ATT_BLOCK, S)
    nq = S // tq
    T = B * S

    def body(q_ref, k_ref, v_ref, o_ref, lse_ref):
        qi = pl.program_id(2)
        qs = (q_ref[:, :LANES], q_ref[:, LANES:])

        def step(masked):
            def f(j, carry):
                rows = pl.ds(pl.multiple_of(j * tk, tk), tk)
                vb = v_ref[rows, :]
                out = []
                for h in range(2):
                    m, l, acc = carry[h]
                    s = _dot_nt(qs[h], k_ref[rows, h * LANES:(h + 1) * LANES])
                    if masked:
                        s = jnp.where(_causal_mask(qi, j, tq, tk), s, -jnp.inf)
                    m_new = jnp.maximum(m, jnp.max(s, axis=-1, keepdims=True))
                    alpha = jnp.exp(m - m_new)
                    p = jnp.exp(s - m_new)
                    out.append((m_new, alpha * l + jnp.sum(p, axis=-1, keepdims=True), alpha * acc + _dot_nn(p, vb)))
                return tuple(out)
            return f

        one = (jnp.full((tq, 1), -1e30, F32), jnp.zeros((tq, 1), F32), jnp.zeros((tq, LANES), F32))
        (ma, la, acca), (mb, lb, accb) = step(True)(qi, lax.fori_loop(0, qi, step(False), (one, one)))
        o_ref[...] = _own_lanes(acca / la, True) + _own_lanes(accb / lb, False)
        for h, lse in enumerate((ma + jnp.log(la), mb + jnp.log(lb))):
            lse_ref[0, h, pl.ds(qi, 1), :] = _lane_sums_as_row(jnp.broadcast_to(lse * (1.0 / LANES), (tq, LANES)))

    return _pcall(
        body, name=name, grid=(B, PAIRS, nq),
        in_specs=[pl.BlockSpec((tq, 2 * LANES), lambda b, g, i: (b * nq + i, g)),
                  pl.BlockSpec((S, 2 * LANES), lambda b, g, i: (b, g)),
                  pl.BlockSpec((S, LANES), lambda b, g, i: (b, v_block0 + g))],
        out_specs=[pl.BlockSpec((tq, LANES), lambda b, g, i: (b * nq + i, g)),
                   pl.BlockSpec((1, 2, nq, tq), lambda b, g, i: (b, g, 0, 0))],
        out_shape=[jax.ShapeDtypeStruct((T, PAIRS * LANES), F32), jax.ShapeDtypeStruct((B, MLA_HEADS, nq, tq), F32)],
        args=(q, k, kv), ride=ride)


def _attn_bwd(q, k, kv, o, lse_rows, do, *, B, S, v_block0, name, ride=None):
    tq = tk = min(ATT_BLOCK, S)
    nq = S // tq
    T = B * S
    scale = _attn_scale()

    def body(q_ref, k_ref, v_ref, o_ref, lse_ref, do_ref, dk_ref, dv_ref, dq_ref, delta_ref):
        kj = pl.program_id(2)
        ks = (k_ref[:, :LANES], k_ref[:, LANES:])
        vb = v_ref[...]

        @pl.when(kj == 0)
        def _():
            dq_ref[...] = jnp.zeros_like(dq_ref)
            for i in range(nq):
                prod = do_ref[i * tq:(i + 1) * tq, :] * o_ref[i * tq:(i + 1) * tq, :]
                for h in range(2):
                    delta_ref[h, i:i + 1, :] = _lane_sums_as_row(_own_lanes(prod, h == 0))

        def step(masked):
            def f(i, carry):
                rows = pl.ds(pl.multiple_of(i * tq, tq), tq)
                do_b = do_ref[rows, :]
                dks, dv = list(carry[:2]), carry[2]
                for h in range(2):
                    qb = q_ref[rows, h * LANES:(h + 1) * LANES]
                    doh = _own_lanes(do_b, h == 0)
                    pt = jnp.exp(_dot_nt(ks[h], qb) - lse_ref[0, h, pl.ds(i, 1), :])
                    if masked:
                        krow = kj * tk + lax.broadcasted_iota(jnp.int32, (tk, tq), 0)
                        qcol = i * tq + lax.broadcasted_iota(jnp.int32, (tk, tq), 1)
                        pt = jnp.where(krow <= qcol, pt, 0.0)
                    dst = pt * (_dot_nt(vb, doh) - delta_ref[h, pl.ds(i, 1), :])
                    dks[h] = dks[h] + _dot_nn(dst, qb)
                    dv = dv + _dot_nn(pt, doh)
                    dq_ref[rows, h * LANES:(h + 1) * LANES] += _dot_tn(dst, ks[h]) * scale
                return dks[0], dks[1], dv
            return f

        zero = jnp.zeros((tk, LANES), F32)
        dka, dkb, dv = lax.fori_loop(kj + 1, nq, step(False), step(True)(kj, (zero, zero, zero)))
        dk_ref[:, :LANES] = dka
        dk_ref[:, LANES:] = dkb
        dv_ref[...] = dv

    krow = lambda w, c0: pl.BlockSpec((tk, w), lambda b, g, j: (b * nq + j, c0 + g))
    seq = lambda w: pl.BlockSpec((S, w), lambda b, g, j: (b, g))
    stat = pl.BlockSpec((1, 2, nq, tq), lambda b, g, j: (b, g, 0, 0))
    dk, dv, dq = _pcall(
        body, name=name, grid=(B, PAIRS, nq),
        in_specs=[seq(2 * LANES), krow(2 * LANES, 0), krow(LANES, v_block0), seq(LANES), stat, seq(LANES)],
        out_specs=[krow(2 * LANES, 0), krow(LANES, 0), seq(2 * LANES)],
        out_shape=[jax.ShapeDtypeStruct((T, MLA_HEADS * LANES), F32), jax.ShapeDtypeStruct((T, PAIRS * LANES), F32),
                   jax.ShapeDtypeStruct((T, MLA_HEADS * LANES), F32)],
        args=(q, k, kv, o, lse_rows, do), scratch=[pltpu.VMEM((2, nq, tq), F32)], ride=ride)
    return dq, dk, dv


def _lru_gates(xl, halo, cw_ref, cb_ref, wa_ref, ba_ref, wx_ref, bx_ref, lam_ref):
    xc = cb_ref[...] + cw_ref[3:4, :] * xl
    for kk in range(LRU_CONV - 1):
        xc = xc + cw_ref[kk:kk + 1, :] * _shift_rows(xl, LRU_CONV - 1 - kk, halo)
    r = _sigmoid(_dot_nn(xc, wa_ref[...]) + ba_ref[...])
    i = _sigmoid(_dot_nn(xc, wx_ref[...]) + bx_ref[...])
    lam = lam_ref[...]
    sp = jnp.maximum(-lam, 0.0) + jnp.log(1.0 + jnp.exp(-jnp.abs(lam)))
    a = jnp.exp(-LRU_C * r * sp)
    mult = jnp.sqrt(1.0 - a * a)
    return xc, r, i, sp, a, mult


def _lru_specs(tt, nt, S):
    def make(rev):
        tmap = (lambda t: nt - 1 - t) if rev else (lambda t: t)
        tile = lambda cb: pl.BlockSpec((tt, LRU_WIDTH), lambda b, t: (b * nt + tmap(t), cb))
        prev8 = lambda cb: pl.BlockSpec(
            (8, LRU_WIDTH), lambda b, t: (jnp.maximum((b * nt + tmap(t)) * (tt // 8) - 1, 0), cb))
        return tile, prev8, tmap
    return make


def _lru_fwd(z, cw, cb, wa, ba, wx, bx, lam, *, S, name, ride=None):
    T = z.shape[0]
    tt = min(ROW_TILE, S)
    nt = S // tt
    tile, prev8, _ = _lru_specs(tt, nt, S)(False)
    vec = lambda r: pl.BlockSpec((r, LRU_WIDTH), lambda b, t: (0, 0))
    mat = pl.BlockSpec((LRU_WIDTH, LRU_WIDTH), lambda b, t: (0, 0))

    def body(xl_ref, halo_ref, gate_ref, cw_ref, cb_ref, wa_ref, ba_ref, wx_ref, bx_ref, lam_ref,
             y_ref, h_ref, carry_ref):
        t = pl.program_id(1)
        first = t == 0
        halo = jnp.where(first, 0.0, halo_ref[...])
        xl_t = xl_ref[...]
        xc, r, i, sp, a, mult = _lru_gates(xl_t, halo, cw_ref, cb_ref, wa_ref, ba_ref, wx_ref, bx_ref, lam_ref)
        bv = mult * (i * xc)
        ones = jnp.ones((8, LRU_WIDTH), F32)
        zeros = jnp.zeros((8, LRU_WIDTH), F32)
        row = lax.broadcasted_iota(jnp.int32, (tt, LRU_WIDTH), 0)
        A = a
        d = 1
        while d < tt:
            if d < 8:
                a_sh = _shift_rows(A, d, ones)
                b_sh = _shift_rows(bv, d, zeros)
            else:
                a_sh = jnp.where(row < d, 1.0, pltpu.roll(A, d, 0))
                b_sh = jnp.where(row < d, 0.0, pltpu.roll(bv, d, 0))
            bv = A * b_sh + bv
            A = A * a_sh
            d *= 2
        h0 = jnp.where(first, 0.0, carry_ref[0:1, :])
        h = A * h0 + bv
        carry_ref[...] = jnp.broadcast_to(h[tt - 1:tt, :], (8, LRU_WIDTH))
        h_ref[...] = h
        y_ref[...] = (h * _gelu(gate_ref[...])).astype(BF16)

    return _pcall(
        body, name=name, grid=(T // S, nt),
        in_specs=[tile(0), prev8(0), tile(1), vec(LRU_CONV), vec(1), mat, vec(1), mat, vec(1), vec(1)],
        out_specs=[tile(0), tile(0)],
        out_shape=[jax.ShapeDtypeStruct((T, LRU_WIDTH), BF16), jax.ShapeDtypeStruct((T, LRU_WIDTH), F32)],
        args=(z, z, z, cw, cb, wa, ba, wx, bx, lam), scratch=[pltpu.VMEM((8, LRU_WIDTH), F32)], ride=ride)


def _lru_bwd(z, h, dy, cw, cb, wa, ba, wx, bx, lam, *, S, name):
    T = z.shape[0]
    tt = min(ROW_TILE, S)
    nt = S // tt
    tile, prev8, tmap = _lru_specs(tt, nt, S)(True)
    vec = lambda r: pl.BlockSpec((r, LRU_WIDTH), lambda b, t: (0, 0))
    mat = pl.BlockSpec((LRU_WIDTH, LRU_WIDTH), lambda b, t: (0, 0))

    def body(xl_ref, halo_ref, gate_ref, h_ref, hprev_ref, dy_ref, cw_ref, cb_ref, wa_ref, ba_ref, wx_ref,
             bx_ref, lam_ref, dxl_ref, dgate_ref, dcw_ref, dcb_ref, dwa_ref, dba_ref, dwx_ref, dbx_ref,
             dlam_ref, lamc_ref, ac_ref, dxc_ref):
        b = pl.program_id(0)
        t = pl.program_id(1)
        tr = nt - 1 - t
        seq_first = tr == 0
        seq_last = t == 0
        halo = jnp.where(seq_first, 0.0, halo_ref[...])
        xl_t = xl_ref[...]
        xc, r, i, sp, a, mult = _lru_gates(xl_t, halo, cw_ref, cb_ref, wa_ref, ba_ref, wx_ref, bx_ref, lam_ref)
        hh = h_ref[...]
        dyf = dy_ref[...].astype(F32)
        gl, dgl = _gelu_and_grad(gate_ref[...])
        dgate_ref[...] = (dyf * hh * dgl).astype(BF16)
        dh = dyf * gl

        a_first_later = jnp.where(seq_last, 0.0, ac_ref[...])
        lam_later = jnp.where(seq_last, 0.0, lamc_ref[...])
        row = lax.broadcasted_iota(jnp.int32, (tt, LRU_WIDTH), 0)
        A = _shift_rows_up(a, 1, a_first_later)
        lm = dh
        ones = jnp.ones((8, LRU_WIDTH), F32)
        zeros = jnp.zeros((8, LRU_WIDTH), F32)
        d = 1
        while d < tt:
            if d < 8:
                a_sh = _shift_rows_up(A, d, ones)
                l_sh = _shift_rows_up(lm, d, zeros)
            else:
                a_sh = jnp.where(row >= tt - d, 1.0, pltpu.roll(A, tt - d, 0))
                l_sh = jnp.where(row >= tt - d, 0.0, pltpu.roll(lm, tt - d, 0))
            lm = lm + A * l_sh
            A = A * a_sh
            d *= 2
        lm = lm + A * lam_later[0:1, :]
        lamc_ref[...] = jnp.broadcast_to(lm[0:1, :], (8, LRU_WIDTH))
        ac_ref[...] = jnp.broadcast_to(a[0:1, :], (8, LRU_WIDTH))

        hprev_halo = jnp.where(seq_first, 0.0, hprev_ref[...])
        h_prev = _shift_rows(hh, 1, hprev_halo)
        da = lm * h_prev
        ixc = i * xc
        dmult = lm * ixc
        di = lm * mult * xc
        dxc = lm * mult * i
        da = da - dmult * a / mult
        dlog = da * a
        dr = dlog * (-LRU_C) * sp
        dsp_part = jnp.sum(dlog * (-LRU_C) * r, axis=0, keepdims=True)
        dpa = dr * r * (1.0 - r)
        dpx = di * i * (1.0 - i)
        dxc = dxc + _dot_nt(dpa, wa_ref[...]) + _dot_nt(dpx, wx_ref[...])
        dwa_part = _dot_tn(xc, dpa)
        dwx_part = _dot_tn(xc, dpx)

        later = jnp.where(seq_last, 0.0, dxc_ref[...])
        dxl = cw_ref[3:4, :] * dxc
        for kk in range(LRU_CONV - 1):
            dxl = dxl + cw_ref[kk:kk + 1, :] * _shift_rows_up(dxc, LRU_CONV - 1 - kk, later)
        dxl_ref[...] = dxl.astype(BF16)
        dxc_ref[...] = dxc[0:8, :]
        dcw_rows = [jnp.sum(dxc * _shift_rows(xl_t, LRU_CONV - 1 - kk, halo), axis=0, keepdims=True)
                    for kk in range(LRU_CONV - 1)]
        dcw_rows.append(jnp.sum(dxc * xl_t, axis=0, keepdims=True))
        dcw_part = jnp.concatenate(dcw_rows + [jnp.zeros((8 - LRU_CONV, LRU_WIDTH), F32)], axis=0)
        lamv = lam_ref[...]
        dlam_part = dsp_part * (-_sigmoid(-lamv))
        parts = ((dcw_ref, dcw_part), (dcb_ref, jnp.sum(dxc, axis=0, keepdims=True)),
                 (dwa_ref, dwa_part), (dba_ref, jnp.sum(dpa, axis=0, keepdims=True)),
                 (dwx_ref, dwx_part), (dbx_ref, jnp.sum(dpx, axis=0, keepdims=True)),
                 (dlam_ref, dlam_part))
        start = jnp.logical_and(b == 0, t == 0)

        @pl.when(start)
        def _():
            for ref, val in parts:
                ref[...] = val

        @pl.when(jnp.logical_not(start))
        def _():
            for ref, val in parts:
                ref[...] += val

    acc = lambda r: pl.BlockSpec((r, LRU_WIDTH), lambda b, t: (0, 0))
    return pl.pallas_call(
        body, name=name, grid=(T // S, nt),
        in_specs=[tile(0), prev8(0), tile(1), tile(0), prev8(0), tile(0),
                  vec(LRU_CONV), vec(1), mat, vec(1), mat, vec(1), vec(1)],
        out_specs=[tile(0), tile(0), acc(8), acc(1), mat, acc(1), mat, acc(1), acc(1)],
        out_shape=[jax.ShapeDtypeStruct((T, LRU_WIDTH), BF16), jax.ShapeDtypeStruct((T, LRU_WIDTH), BF16),
                   jax.ShapeDtypeStruct((8, LRU_WIDTH), F32), jax.ShapeDtypeStruct((1, LRU_WIDTH), F32),
                   jax.ShapeDtypeStruct((LRU_WIDTH, LRU_WIDTH), F32), jax.ShapeDtypeStruct((1, LRU_WIDTH), F32),
                   jax.ShapeDtypeStruct((LRU_WIDTH, LRU_WIDTH), F32), jax.ShapeDtypeStruct((1, LRU_WIDTH), F32),
                   jax.ShapeDtypeStruct((1, LRU_WIDTH), F32)],
        scratch_shapes=[pltpu.VMEM((8, LRU_WIDTH), F32), pltpu.VMEM((8, LRU_WIDTH), F32),
                        pltpu.VMEM((8, LRU_WIDTH), F32)],
        compiler_params=_cparams(("arbitrary", "arbitrary")),
    )(z, z, z, h, h, dy, cw, cb, wa, ba, wx, bx, lam)


FFN_CT = 1408
FFN_TILE = 512


def _ffn_conv(g, halo, cw, cb):
    gc = cb + cw[2:3, :] * g
    for kk in range(FFN_CONV - 1):
        gc = gc + cw[kk:kk + 1, :] * _shift_rows(g, FFN_CONV - 1 - kk, halo)
    return gc


def _row_chunks(rows, chunk):
    return [slice(r0, min(r0 + chunk, rows)) for r0 in range(0, rows, chunk)]


FFN_CHUNK = 128
HALO = 16


def _ffn_act_down(g, u, cw, cb, w_down, res, *, S, name, ride=None):
    T, F = g.shape
    D = w_down.shape[1]
    tt = min(FFN_TILE, S)
    nt = S // tt
    tc = _tile(F, FFN_CT)
    nj = F // tc

    def body(g_ref, halo_ref, u_ref, cw_ref, cb_ref, w_ref, r_ref, o_ref, act_ref):
        j = pl.program_id(1)
        first = (pl.program_id(0) % nt) == 0
        cw, cb = cw_ref[...], cb_ref[...]

        @pl.when(j == 0)
        def _():
            o_ref[...] = r_ref[...]

        for r in _row_chunks(tt, FFN_CHUNK):
            before = halo_ref[...] if r.start == 0 else g_ref[r.start - HALO:r.start, :]
            halo = before.astype(F32)[HALO - 8:]
            if r.start == 0:
                halo = jnp.where(first, 0.0, halo)
            gc = _ffn_conv(g_ref[r, :].astype(F32), halo, cw, cb)
            act = (_gelu(gc) * u_ref[r, :].astype(F32)).astype(BF16)
            act_ref[r, :] = act
            o_ref[r, :] += _dot_nn(act, w_ref[...])

    tile = pl.BlockSpec((tt, tc), lambda i, j: (i, j))
    prev = pl.BlockSpec((HALO, tc), lambda i, j: (jnp.maximum(i * (tt // HALO) - 1, 0), j))
    rows = pl.BlockSpec((tt, D), lambda i, j: (i, 0))
    return _pcall(
        body, name=name, grid=(T // tt, nj),
        in_specs=[tile, prev, tile, pl.BlockSpec((FFN_CONV, tc), lambda i, j: (0, j)),
                  pl.BlockSpec((1, tc), lambda i, j: (0, j)), pl.BlockSpec((tc, D), lambda i, j: (j, 0)), rows],
        out_specs=[rows, tile], out_shape=[jax.ShapeDtypeStruct((T, D), F32), jax.ShapeDtypeStruct((T, F), BF16)],
        args=(g, g, u, cw, cb, w_down, res), sem=("parallel", "arbitrary"), ride=ride)


def _ffn_act_bwd(g, u, dh, w_down, cw, cb, *, S, name, ride=None):
    T, F = g.shape
    D = w_down.shape[1]
    tt = min(FFN_TILE, S)
    nt = S // tt
    ntt = T // tt
    tc = _tile(F, FFN_CT)

    def body(g_ref, halo_ref, u_ref, dh_ref, w_ref, cw_ref, cb_ref, dg_ref, du_ref, dcw_ref, dcb_ref, later_ref):
        step = pl.program_id(1)
        ti = (ntt - 1 - step) % nt
        cw, cb = cw_ref[...], cb_ref[...]

        @pl.when(step == 0)
        def _():
            dcw_ref[...] = jnp.zeros_like(dcw_ref)
            dcb_ref[...] = jnp.zeros_like(dcb_ref)

        halo = jnp.where(ti == 0, 0.0, halo_ref[...].astype(F32)[HALO - 8:])
        gt = g_ref[...].astype(F32)
        gl, dgl = _gelu_and_grad(_ffn_conv(gt, halo, cw, cb))
        da = _dot_nt(dh_ref[...], w_ref[...])
        du_ref[...] = (da * gl).astype(BF16)
        dgc = da * u_ref[...].astype(F32) * dgl
        later = jnp.where(ti == nt - 1, 0.0, later_ref[...])
        dg = cw[2:3, :] * dgc
        for kk in range(FFN_CONV - 1):
            dg = dg + cw[kk:kk + 1, :] * _shift_rows_up(dgc, FFN_CONV - 1 - kk, later)
        dg_ref[...] = dg.astype(BF16)
        later_ref[...] = dgc[0:8, :]
        rows = [jnp.sum(dgc * _shift_rows(gt, FFN_CONV - 1 - kk, halo), axis=0, keepdims=True)
                for kk in range(FFN_CONV - 1)]
        rows.append(jnp.sum(dgc * gt, axis=0, keepdims=True))
        dcw_ref[...] += jnp.concatenate(rows + [jnp.zeros((8 - FFN_CONV, tc), F32)], axis=0)
        dcb_ref[...] += jnp.sum(dgc, axis=0, keepdims=True)

    tile = pl.BlockSpec((tt, tc), lambda j, s: (ntt - 1 - s, j))
    prev = pl.BlockSpec((HALO, tc), lambda j, s: (jnp.maximum((ntt - 1 - s) * (tt // HALO) - 1, 0), j))
    return _pcall(
        body, name=name, grid=(F // tc, ntt),
        in_specs=[tile, prev, tile, pl.BlockSpec((tt, D), lambda j, s: (ntt - 1 - s, 0)),
                  pl.BlockSpec((tc, D), lambda j, s: (j, 0)), pl.BlockSpec((FFN_CONV, tc), lambda j, s: (0, j)),
                  pl.BlockSpec((1, tc), lambda j, s: (0, j))],
        out_specs=[tile, tile, pl.BlockSpec((8, tc), lambda j, s: (0, j)), pl.BlockSpec((1, tc), lambda j, s: (0, j))],
        out_shape=[jax.ShapeDtypeStruct((T, F), BF16), jax.ShapeDtypeStruct((T, F), BF16),
                   jax.ShapeDtypeStruct((8, F), F32), jax.ShapeDtypeStruct((1, F), F32)],
        args=(g, g, u, dh, w_down, cw, cb), scratch=[pltpu.VMEM((8, tc), F32)], ride=ride)


def _sgu_norm(zv, g_ref, b_ref):
    v = _gelu(zv)
    mu = jnp.mean(v, axis=-1, keepdims=True)
    xc = v - mu
    rstd = lax.rsqrt(jnp.mean(xc * xc, axis=-1, keepdims=True) + NORM_EPS)
    xhat = xc * rstd
    return xhat, rstd, xhat * g_ref[...] + b_ref[...]


def _sgu_fwd(zc, ln_g, ln_b, wm, bmap, *, name):
    T = zc.shape[0]
    W = SGU_WIDTH
    tt = _tile(T, SGU_TILE, CHUNK)
    nch = tt // CHUNK

    def body(z_ref, g_ref, b_ref, wm_ref, bm_ref, p_ref):
        u = _gelu(z_ref[:, :W])
        _, _, vn = _sgu_norm(z_ref[:, W:], g_ref, b_ref)
        vn = vn.astype(BF16)
        for n in range(nch):
            rows = slice(n * CHUNK, (n + 1) * CHUNK)
            for gi in range(SGU_GROUPS):
                cols = slice(gi * LANES, (gi + 1) * LANES)
                s = _dot_nn(wm_ref[gi], vn[rows, cols]) + bm_ref[:, cols]
                p_ref[rows, cols] = (u[rows, cols] * s).astype(BF16)

    const2 = lambda r, c: pl.BlockSpec((r, c), lambda i: (0, 0))
    return pl.pallas_call(
        body, name=name, grid=(T // tt,),
        in_specs=[pl.BlockSpec((tt, 2 * W), lambda i: (i, 0)), const2(1, W), const2(1, W),
                  pl.BlockSpec((SGU_GROUPS, CHUNK, CHUNK), lambda i: (0, 0, 0)), const2(CHUNK, W)],
        out_specs=pl.BlockSpec((tt, W), lambda i: (i, 0)),
        out_shape=jax.ShapeDtypeStruct((T, W), BF16),
        compiler_params=_cparams(("parallel",)),
    )(zc, ln_g, ln_b, wm, bmap)


def _sgu_bwd(zc, dp, ln_g, ln_b, wm, bmap, *, name, ride=None):
    T = zc.shape[0]
    W = SGU_WIDTH
    tt = _tile(T, SGU_TILE, CHUNK)
    nch = tt // CHUNK
    nsteps = T // tt

    def body(z_ref, dp_ref, g_ref, b_ref, wm_ref, bm_ref, dz_ref, dg_ref, db_ref, dwm_ref, dbm_ref,
             s_scr, dvn_scr):
        step = pl.program_id(0)
        zu = z_ref[:, :W]
        zv = z_ref[:, W:]
        u, dgu = _gelu_and_grad(zu)
        xhat, rstd, vn = _sgu_norm(zv, g_ref, b_ref)
        vnb = vn.astype(BF16)
        dpf = dp_ref[...].astype(F32)
        ds = dpf * u

        @pl.when(step == 0)
        def _():
            dwm_ref[...] = jnp.zeros_like(dwm_ref)
            dbm_ref[...] = jnp.zeros_like(dbm_ref)

        for n in range(nch):
            rows = slice(n * CHUNK, (n + 1) * CHUNK)
            for gi in range(SGU_GROUPS):
                cols = slice(gi * LANES, (gi + 1) * LANES)
                s_scr[rows, cols] = _dot_nn(wm_ref[gi], vnb[rows, cols]) + bm_ref[:, cols]
                dsb = ds[rows, cols]
                dvn_scr[rows, cols] = _dot_tn(wm_ref[gi], dsb)
                dwm_ref[gi] += _dot_nt(dsb, vnb[rows, cols])
                dbm_ref[:, cols] += dsb
        dz_ref[:, :W] = (dpf * s_scr[...] * dgu).astype(BF16)
        dvn = dvn_scr[...]
        dxhat = dvn * g_ref[...]
        dv = rstd * (dxhat - jnp.mean(dxhat, axis=-1, keepdims=True)
                     - xhat * jnp.mean(dxhat * xhat, axis=-1, keepdims=True))
        _, dgv = _gelu_and_grad(zv)
        dz_ref[:, W:] = (dv * dgv).astype(BF16)
        dg_part = jnp.sum(dvn * xhat, axis=0, keepdims=True)
        db_part = jnp.sum(dvn, axis=0, keepdims=True)

        @pl.when(step == 0)
        def _():
            dg_ref[...] = dg_part
            db_ref[...] = db_part

        @pl.when(step > 0)
        def _():
            dg_ref[...] += dg_part
            db_ref[...] += db_part

        @pl.when(step == nsteps - 1)
        def _():
            for gi in range(SGU_GROUPS):
                cols = slice(gi * LANES, (gi + 1) * LANES)
                tot = jnp.sum(dbm_ref[:, cols], axis=1, keepdims=True)
                dbm_ref[:, cols] = jnp.broadcast_to(tot, (CHUNK, LANES))

    const2 = lambda r, c: pl.BlockSpec((r, c), lambda i: (0, 0))
    wspec = pl.BlockSpec((SGU_GROUPS, CHUNK, CHUNK), lambda i: (0, 0, 0))
    return _pcall(
        body, name=name, grid=(nsteps,),
        in_specs=[pl.BlockSpec((tt, 2 * W), lambda i: (i, 0)), pl.BlockSpec((tt, W), lambda i: (i, 0)),
                  const2(1, W), const2(1, W), wspec, const2(CHUNK, W)],
        out_specs=[pl.BlockSpec((tt, 2 * W), lambda i: (i, 0)), const2(1, W), const2(1, W), wspec, const2(CHUNK, W)],
        out_shape=[jax.ShapeDtypeStruct((T, 2 * W), BF16), jax.ShapeDtypeStruct((1, W), F32),
                   jax.ShapeDtypeStruct((1, W), F32), jax.ShapeDtypeStruct((SGU_GROUPS, CHUNK, CHUNK), F32),
                   jax.ShapeDtypeStruct((CHUNK, W), F32)],
        args=(zc, dp, ln_g, ln_b, wm, bmap), scratch=[pltpu.VMEM((tt, W), F32), pltpu.VMEM((tt, W), F32)], ride=ride)


def _rope_tables(positions):
    half = QK_ROPE // 2
    inv_freq = jnp.exp(-math.log(ROPE_BASE) * jnp.arange(half, dtype=F32) / half)
    ang = positions.reshape(-1).astype(F32)[:, None] * inv_freq
    cos = jnp.cos(ang)
    sin = jnp.sin(ang)
    n = ang.shape[0]
    tail = LANES - QK_NOPE - QK_ROPE
    cos_t = jnp.concatenate([jnp.ones((n, QK_NOPE), F32), cos, cos, jnp.ones((n, tail), F32)], axis=1)
    sin_t = jnp.concatenate([jnp.zeros((n, QK_NOPE), F32), -sin, sin, jnp.zeros((n, tail), F32)], axis=1)
    return cos_t, sin_t


SGU_GROUP_DIM = SGU_WIDTH // SGU_GROUPS
_O1, _O2, _O3, _O4 = Q_LORA, Q_LORA + KV_LORA, Q_LORA + KV_LORA + QK_ROPE, Q_LORA + KV_LORA + QK_ROPE + LRU_WIDTH
_A0, _A1, _A2 = 2 * LRU_WIDTH, 2 * LRU_WIDTH + Q_LORA, 2 * LRU_WIDTH + Q_LORA + KV_LORA
_A3 = _A2 + QK_NOPE
Z_Q_BLOCK, Z_KV_BLOCK, Z_KPE_BLOCK = _A0 // Q_LORA, _A1 // KV_LORA, _A2 // LANES


def _perm_w_in(w_in):
    zeros = lambda n: jnp.zeros((w_in.shape[0], n), w_in.dtype)
    return jnp.concatenate([w_in[:, _O3:_O4], w_in[:, _O4:], w_in[:, :_O1], w_in[:, _O1:_O2], zeros(QK_NOPE),
                            w_in[:, _O2:_O3], zeros(LANES - QK_NOPE - QK_ROPE)], axis=1)


def _unperm_w_in(w):
    return jnp.concatenate([w[:, _A0:_A1], w[:, _A1:_A2], w[:, _A3:_A3 + QK_ROPE], w[:, :LRU_WIDTH],
                            w[:, LRU_WIDTH:_A0]], axis=1)


def _head_blocks(w, d):
    r = w.shape[0]
    return jnp.pad(w.reshape(r, MLA_HEADS, d), ((0, 0), (0, 0), (0, LANES - d))).reshape(r, MLA_HEADS * LANES)


def _from_head_blocks(w, d):
    r = w.shape[0]
    return w.reshape(r, MLA_HEADS, LANES)[:, :, :d].reshape(r, MLA_HEADS * d)


def _split_kv(w_kv):
    r = w_kv.shape[0]
    w3 = w_kv.reshape(r, MLA_HEADS, QK_NOPE + V_HEAD)
    return _head_blocks(w3[:, :, :QK_NOPE].reshape(r, -1), QK_NOPE), w3[:, :, QK_NOPE:].reshape(r, -1)


def _join_kv(w_k, w_v):
    r = w_k.shape[0]
    return jnp.concatenate([_from_head_blocks(w_k, QK_NOPE).reshape(r, MLA_HEADS, QK_NOPE),
                            w_v.reshape(r, MLA_HEADS, V_HEAD)], axis=2).reshape(r, -1)


def _prep_small(w):
    p = {n: w[n] for n in w if n not in BIG}
    eye = jnp.eye(LRU_HEADS, dtype=F32)
    dense = lambda wg: (wg[:, :, None, :] * eye[:, None, :, None]).reshape(LRU_WIDTH, LRU_WIDTH).astype(BF16)
    p["wa_d"] = dense(w["ab_w_rg_a"][0])
    p["wx_d"] = dense(w["ab_w_rg_x"][0])
    causal = jnp.tril(jnp.ones((CHUNK, CHUNK), F32))
    p["wm"] = (w["c_w_s"][0] * causal).astype(BF16)
    p["bmap"] = jnp.repeat(w["c_b_s"][0].T, SGU_GROUP_DIM, axis=1)
    return p


def _prep_big(ab_w_in, ab_w_q_b, ab_w_kv_b):
    return {"w_in_p": _perm_w_in(ab_w_in).astype(BF16),
            "w_q_p": _head_blocks(ab_w_q_b, QK_NOPE + QK_ROPE).astype(BF16),
            "w_kv_p": jnp.concatenate(_split_kv(ab_w_kv_b), axis=1).astype(BF16)}


def _ffn_fwd(h, l, p, S, rides):
    hn = _rms_fwd(h, p["ffn_norm"][l], name=f"ffn{l}_norm")
    g = _mm(hn, p["ffn_gate_t"][l], tb=True, out_dtype=BF16, name=f"ffn{l}_gate", ride=rides.get(f"ffn{l}_gate"))
    u = _mm(hn, p["ffn_up_t"][l], tb=True, out_dtype=BF16, name=f"ffn{l}_up", ride=rides.get(f"ffn{l}_up"))
    out, act = _ffn_act_down(g, u, p["ffn_conv_w"][l], p["ffn_conv_b"][l][None], p["ffn_down"][l], h, S=S,
                             name=f"ffn{l}_down", ride=rides.get(f"ffn{l}_down"))
    return out, (hn, g, u, act)


def _ffn_bwd(dh, h_in, l, p, saved, S, rides, grads_ready, also_ready=None):
    hn, g, u, act = saved
    dw_down = _mm(act, dh, ta=True, out_dtype=BF16, name=f"ffn{l}_dwdown")
    dg, du, dcw, dcb = _ffn_act_bwd(g, u, dh, p["ffn_down"][l], p["ffn_conv_w"][l], p["ffn_conv_b"][l][None], S=S,
                                    name=f"ffn{l}_dactbwd", ride=rides.get(f"ffn{l}_dactbwd"))
    dhn = _mm(dg, p["ffn_gate_t"][l], also=(du, p["ffn_up_t"][l]), out_dtype=BF16, name=f"ffn{l}_dhn")
    dw_gate_t = _mm(dg, hn, ta=True, out_dtype=BF16, name=f"ffn{l}_dwgate")
    dw_up_t = _mm(du, hn, ta=True, out_dtype=BF16, name=f"ffn{l}_dwup")
    grads_ready(l, {**(also_ready or {}), "ffn_gate_t": dw_gate_t, "ffn_up_t": dw_up_t, "ffn_down": dw_down})
    dh_in, dnorm = _rms_bwd(h_in, p["ffn_norm"][l], dhn, res=dh, name=f"ffn{l}_dnorm", ride=rides.get(f"ffn{l}_dnorm"))
    grads = dict(ffn_norm=dnorm[0], ffn_gate_t=dw_gate_t, ffn_up_t=dw_up_t, ffn_conv_w=dcw[:FFN_CONV],
                 ffn_conv_b=dcb[0], ffn_down=dw_down)
    return dh_in, grads


def _local_step(x, positions, target, p, rides=None, grads_ready=None):
    rides = {} if rides is None else rides
    grads_ready = grads_ready or (lambda layer, ready: None)
    B, S, D = x.shape
    T = B * S
    H = MLA_HEADS
    xf = x.reshape(T, D)
    tgt = target.reshape(T, D)
    cos, sin = _rope_tables(positions)

    hn0 = _rms_fwd(xf, p["ab_norm"][0], name="ab_norm", ride=rides.get("ab_norm"))
    z = _mm(hn0, p["w_in_p"], name="ab_in")
    cqn = _rms_fwd(z, p["ab_q_norm"][0], cb=Z_Q_BLOCK, name="q_norm")
    ckvn = _rms_fwd(z, p["ab_kv_norm"][0], cb=Z_KV_BLOCK, name="kv_norm")
    q = _mm(cqn, p["w_q_p"], name="q_up")
    kv = _mm(ckvn, p["w_kv_p"], out_dtype=BF16, name="kv_up")
    qs = _rope_q(q, cos, sin, name="q_rope")
    kk = _key_blocks(kv, z, cos, sin, kpe_block=Z_KPE_BLOCK, name="k_rope")
    att = dict(B=B, S=S, v_block0=H)
    o, lse = _attn_fwd(qs, kk, kv, name="attn_fwd", ride=rides.get("attn_fwd"), **att)
    lru_par = (p["ab_conv_w"][0], p["ab_conv_b"], p["wa_d"], p["ab_b_rg_a"], p["wx_d"], p["ab_b_rg_x"], p["ab_lambda"])
    y_lru, hs = _lru_fwd(z, *lru_par, S=S, name="lru_fwd", ride=rides.get("lru_fwd"))
    n_att = H * V_HEAD
    w_out_a, w_out_b = p["ab_w_out"][:n_att], p["ab_w_out"][n_att:]
    h1 = _mm(o, w_out_a, also=(y_lru, w_out_b), res=xf, name="ab_out")
    h2, ffn0 = _ffn_fwd(h1, 0, p, S, rides)

    hn2 = _rms_fwd(h2, p["c_norm"][0], name="c_norm")
    zc = _mm(hn2, p["c_w_in_t"], tb=True, name="c_in")
    pg = _sgu_fwd(zc, p["c_ln_g"], p["c_ln_b"], p["wm"], p["bmap"], name="sgu_fwd")
    h3 = _mm(pg, p["c_w_out"], res=h2, name="c_out")
    h4, ffn1 = _ffn_fwd(h3, 1, p, S, rides)

    loss_row, dh4, dfinal = _final_fwd_bwd(h4, p["final_norm"], tgt, name="final")

    dh3, g_ffn1 = _ffn_bwd(dh4, h3, 1, p, ffn1, S, rides, grads_ready)
    dpg = _mm(dh3, p["c_w_out"], tb=True, out_dtype=BF16, name="c_dp")
    dw_c_out = _mm(pg, dh3, ta=True, out_dtype=BF16, name="c_dwout")
    dzc, dlng, dlnb, dwm, dbm = _sgu_bwd(zc, dpg, p["c_ln_g"], p["c_ln_b"], p["wm"], p["bmap"], name="sgu_bwd",
                                         ride=rides.get("sgu_bwd"))
    dhn2 = _mm(dzc, p["c_w_in_t"], out_dtype=BF16, name="c_dhn")
    dw_c_in_t = _mm(dzc, hn2, ta=True, out_dtype=BF16, name="c_dwin")
    dh2, dcnorm = _rms_bwd(h2, p["c_norm"][0], dhn2, res=dh3, name="c_dnorm")
    dh1, g_ffn0 = _ffn_bwd(dh2, h1, 0, p, ffn0, S, rides, grads_ready, {"c_w_in_t": dw_c_in_t, "c_w_out": dw_c_out})

    do = _mm(dh1, w_out_a, tb=True, name="ab_do")
    dy_lru = _mm(dh1, w_out_b, tb=True, out_dtype=BF16, name="ab_dylru")
    dw_out = jnp.concatenate([_mm(o, dh1, ta=True, out_dtype=BF16, name="ab_dwout_a"),
                              _mm(y_lru, dh1, ta=True, out_dtype=BF16, name="ab_dwout_b")], axis=0)
    dq, dk, dv = _attn_bwd(qs, kk, kv, o, lse, do, name="attn_bwd", ride=rides.get("attn_bwd"), **att)
    dq_full = _rope_q_bwd(dq, cos, sin, name="q_rope_bwd")
    dkr = _key_rope_bwd(dk, cos, sin, name="k_rope_bwd")
    n_key = H * LANES
    w_k_p, w_v_p = p["w_kv_p"][:, :n_key], p["w_kv_p"][:, n_key:]
    dcqn = _mm(dq_full, p["w_q_p"], tb=True, name="q_dlat")
    dw_q_p = _mm(cqn, dq_full, ta=True, out_dtype=BF16, name="q_dw")
    dckvn = _mm(dv, w_v_p, tb=True, res=_mm(dk, w_k_p, tb=True, name="k_dlat"), name="v_dlat")
    dw_k_p = _mm(ckvn, dk, ta=True, out_dtype=BF16, name="k_dw")
    dw_v_p = _mm(ckvn, dv, ta=True, out_dtype=BF16, name="v_dw")
    dcq, dqnorm = _rms_bwd(z, p["ab_q_norm"][0], dcqn, cb=Z_Q_BLOCK, out_dtype=BF16, name="q_dnorm")
    dckv, dkvnorm = _rms_bwd(z, p["ab_kv_norm"][0], dckvn, cb=Z_KV_BLOCK, out_dtype=BF16, name="kv_dnorm")
    dxl, dgate, dcw, dcb, dwa, dba, dwx, dbx, dlam = _lru_bwd(z, hs, dy_lru, *lru_par, S=S, name="lru_bwd")
    dz = jnp.concatenate([dxl, dgate, dcq, dckv, dkr], axis=1)
    dhn0 = _mm(dz, p["w_in_p"], tb=True, out_dtype=BF16, name="ab_dhn")
    dw_in_p = _mm(hn0, dz, ta=True, out_dtype=BF16, name="ab_dwin")
    dx, dabnorm = _rms_bwd(xf, p["ab_norm"][0], dhn0, res=dh1, name="ab_dnorm")

    blocks = lambda dd: jnp.stack([dd[i * LRU_BLOCK:(i + 1) * LRU_BLOCK, i * LRU_BLOCK:(i + 1) * LRU_BLOCK]
                                   for i in range(LRU_HEADS)])
    causal = jnp.tril(jnp.ones((CHUNK, CHUNK), F32))
    grads = {
        "ab_norm": dabnorm, "w_in_p": dw_in_p, "ab_q_norm": dqnorm, "w_q_p": dw_q_p,
        "ab_kv_norm": dkvnorm, "w_k_p": dw_k_p, "w_v_p": dw_v_p, "ab_conv_w": dcw[:LRU_CONV][None], "ab_conv_b": dcb,
        "ab_w_rg_a": blocks(dwa)[None], "ab_b_rg_a": dba, "ab_w_rg_x": blocks(dwx)[None], "ab_b_rg_x": dbx,
        "ab_lambda": dlam, "ab_w_out": dw_out,
        "c_norm": dcnorm, "c_w_in_t": dw_c_in_t, "c_ln_g": dlng, "c_ln_b": dlnb,
        "c_w_s": (dwm * causal)[None], "c_b_s": dbm[:, ::SGU_GROUP_DIM].T[None], "c_w_out": dw_c_out,
        "final_norm": dfinal[0],
    }
    for name in ("ffn_norm", "ffn_conv_w", "ffn_conv_b"):
        grads[name] = jnp.stack([g_ffn0[name], g_ffn1[name]])
    for name in ("ffn_gate_t", "ffn_up_t", "ffn_down"):
        grads[name] = [g_ffn0[name], g_ffn1[name]]
    return loss_row, dx.reshape(B, S, D), grads


ANY = pl.BlockSpec(memory_space=pl.ANY)


def _place():
    x, y, c = lax.axis_index("x"), lax.axis_index("y"), lax.axis_index("c")
    chips = [(1 - x, y), (x, 1 - y), (1 - x, 1 - y)]
    return x, y, c, 2 * x + y, (x, y, 1 - c), chips


def _remote(src, dst, send_sems, recv_sems, k, to):
    return pltpu.make_async_remote_copy(src_ref=src, dst_ref=dst, send_sem=send_sems.at[k], recv_sem=recv_sems.at[k],
                                        device_id=to, device_id_type=MESH)


class _Exchange:
    def __init__(self, arrs, out_shapes, n_sems, start, finish):
        self.arrs, self.out_shapes, self.n_sems, self.start, self.finish = list(arrs), out_shapes, n_sems, start, finish

    @property
    def in_specs(self):
        return [ANY] * len(self.arrs)

    @property
    def out_specs(self):
        return [ANY] * len(self.out_shapes)

    @property
    def scratch(self):
        return [pltpu.SemaphoreType.DMA((self.n_sems,)), pltpu.SemaphoreType.DMA((self.n_sems,))]

    def split(self, refs):
        n = len(self.arrs)
        return refs[:n], refs[n:n + len(self.out_shapes)], refs[-2], refs[-1]

    def run(self, name):
        def body(*refs):
            parts = self.split(refs)
            self.start(*parts)
            self.finish(*parts)

        return pl.pallas_call(body, name=name, in_specs=self.in_specs, out_specs=self.out_specs,
                              out_shape=self.out_shapes, scratch_shapes=self.scratch)(*self.arrs)


def _put(buf, piece, idx, axis):
    return lax.dynamic_update_slice_in_dim(buf, jnp.expand_dims(piece, axis).astype(buf.dtype), idx, axis)


def _all_gather(arrs):
    n = len(arrs)
    per = 7

    def start(ins, outs, send_sems, recv_sems):
        x, y, c, j, sib, chips = _place()
        for i in range(n):
            for k, (cx, cy) in enumerate(chips):
                _remote(ins[i].at[:, c], outs[i].at[:, j, c], send_sems, recv_sems, per * i + k, (cx, cy, c)).start()
            _remote(ins[i], outs[i].at[:, j], send_sems, recv_sems, per * i + 6, sib).start()

    def finish(ins, outs, send_sems, recv_sems):
        x, y, c, j, sib, chips = _place()
        passed = []
        for i in range(n):
            for k, (cx, cy) in enumerate(chips):
                got = outs[i].at[:, 2 * cx + cy, c]
                _remote(got, got, send_sems, recv_sems, per * i + k, (cx, cy, c)).wait_recv()
                cp = _remote(got, got, send_sems, recv_sems, per * i + 3 + k, sib)
                cp.start()
                passed.append(cp)
        for i in range(n):
            for k, (cx, cy) in enumerate(chips):
                got = outs[i].at[:, 2 * cx + cy, 1 - c]
                _remote(got, got, send_sems, recv_sems, per * i + 3 + k, sib).wait_recv()
                _remote(ins[i].at[:, c], ins[i].at[:, c], send_sems, recv_sems, per * i + k, sib).wait_send()
            _remote(ins[i], outs[i].at[:, j], send_sems, recv_sems, per * i + 6, sib).wait()
        for cp in passed:
            cp.wait_send()

    shapes = [jax.ShapeDtypeStruct((a.shape[0], N_CHIPS) + a.shape[1:], a.dtype) for a in arrs]
    return _Exchange(arrs, shapes, per * n, start, finish)


class _Offset:
    def __init__(self, sems, k0):
        self.sems, self.k0 = sems, k0

    @property
    def at(self):
        return self

    def __getitem__(self, k):
        return self.sems.at[self.k0 + k]


def _merge(a, b):
    n_in, n_out = len(a.arrs), len(a.out_shapes)

    def both(fa, fb):
        def f(ins, outs, send_sems, recv_sems):
            fa(ins[:n_in], outs[:n_out], send_sems, recv_sems)
            fb(ins[n_in:], outs[n_out:], _Offset(send_sems, a.n_sems), _Offset(recv_sems, a.n_sems))
        return f

    return _Exchange(a.arrs + b.arrs, a.out_shapes + b.out_shapes, a.n_sems + b.n_sems,
                     both(a.start, b.start), both(a.finish, b.finish))


def _pair_swap(arrs):
    n = len(arrs)

    def start(ins, outs, send_sems, recv_sems):
        x, y, c, j, sib, chips = _place()
        for i in range(n):
            _remote(ins[i].at[:, 1 - c], outs[i], send_sems, recv_sems, i, sib).start()

    def finish(ins, outs, send_sems, recv_sems):
        x, y, c, j, sib, chips = _place()
        for i in range(n):
            _remote(ins[i].at[:, 1 - c], outs[i], send_sems, recv_sems, i, sib).wait()

    shapes = [jax.ShapeDtypeStruct((a.shape[0],) + a.shape[2:], a.dtype) for a in arrs]
    return _Exchange(arrs, shapes, n, start, finish)


def _pair_send(arrs):
    n = len(arrs)

    def start(ins, outs, send_sems, recv_sems):
        x, y, c, j, sib, chips = _place()
        for i in range(n):
            _remote(ins[i], outs[i], send_sems, recv_sems, i, sib).start()

    def finish(ins, outs, send_sems, recv_sems):
        x, y, c, j, sib, chips = _place()
        for i in range(n):
            _remote(ins[i], outs[i], send_sems, recv_sems, i, sib).wait()

    shapes = [jax.ShapeDtypeStruct(a.shape, a.dtype) for a in arrs]
    return _Exchange(arrs, shapes, n, start, finish)


def _chip_exchange(arrs, *, scatter):
    n = len(arrs)

    def copies(ins, outs, send_sems, recv_sems):
        x, y, c, j, sib, chips = _place()
        return [(_remote(ins[i].at[2 * cx + cy] if scatter else ins[i], outs[i].at[j], send_sems, recv_sems,
                         3 * i + k, (cx, cy, c)),
                 _remote(outs[i].at[2 * cx + cy], outs[i].at[2 * cx + cy], send_sems, recv_sems, 3 * i + k, (cx, cy, c)))
                for i in range(n) for k, (cx, cy) in enumerate(chips)]

    def start(*refs):
        for out, _ in copies(*refs):
            out.start()

    def finish(*refs):
        for out, back in copies(*refs):
            back.wait_recv()
            out.wait_send()

    shapes = [jax.ShapeDtypeStruct((N_CHIPS,) + a.shape[-2:], a.dtype) for a in arrs]
    return _Exchange(arrs, shapes, 3 * n, start, finish)


FLAT_ROWS = 512


def _pair_add(sharded, from_sib, *, name):
    n, _, R, L = sharded.shape
    tr = _tile(R, FLAT_ROWS, 16)

    def body(s_ref, b_ref, o_ref):
        own = jnp.where(lax.axis_index("c") == 0, s_ref[:, 0], s_ref[:, 1])
        o_ref[...] = (own.astype(F32) + b_ref[...].astype(F32)).astype(BF16)

    spec = pl.BlockSpec((n, tr, L), lambda i: (0, i, 0))
    return pl.pallas_call(
        body, name=name, grid=(R // tr,), in_specs=[pl.BlockSpec((n, 2, tr, L), lambda i: (0, 0, i, 0)), spec],
        out_specs=spec, out_shape=jax.ShapeDtypeStruct((n, R, L), BF16), compiler_params=_cparams(("parallel",)),
    )(sharded, from_sib)


def _chip_sum(arrived, pair, *, name):
    n, R, L = arrived.shape
    tr = _tile(R, FLAT_ROWS, 16)

    def body(a_ref, p_ref, o_ref):
        me = 2 * lax.axis_index("x") + lax.axis_index("y")
        acc = None
        for k in range(n):
            term = jnp.where(me == k, p_ref[k], a_ref[k]).astype(F32)
            acc = term if acc is None else acc + term
        o_ref[...] = acc

    spec = pl.BlockSpec((n, tr, L), lambda i: (0, i, 0))
    return pl.pallas_call(
        body, name=name, grid=(R // tr,), in_specs=[spec, spec], out_specs=pl.BlockSpec((tr, L), lambda i: (i, 0)),
        out_shape=jax.ShapeDtypeStruct((R, L), F32), compiler_params=_cparams(("parallel",)),
    )(arrived, pair)


def _sum_slots(buf, *, name):
    n, R, L = buf.shape
    tr = _tile(R, FLAT_ROWS, 16)

    def body(b_ref, o_ref):
        acc = b_ref[0].astype(F32)
        for k in range(1, n):
            acc = acc + b_ref[k].astype(F32)
        o_ref[...] = acc

    return pl.pallas_call(
        body, name=name, grid=(R // tr,), in_specs=[pl.BlockSpec((n, tr, L), lambda i: (0, i, 0))],
        out_specs=pl.BlockSpec((tr, L), lambda i: (i, 0)),
        out_shape=jax.ShapeDtypeStruct((R, L), F32), compiler_params=_cparams(("parallel",)),
    )(buf)


def _adamw_update(w, g, m, v):
    c1 = 1.0 - ADAM_B1 ** ADAM_STEP
    c2 = 1.0 - ADAM_B2 ** ADAM_STEP
    m = ADAM_B1 * m + (1.0 - ADAM_B1) * g
    v = ADAM_B2 * v + (1.0 - ADAM_B2) * (g * g)
    return -ADAM_LR * ((m / c1) / (jnp.sqrt(v / c2) + ADAM_EPS) + ADAM_WD * w), m, v


def _adamw_halves(w, m, v, own, other, *, name):
    NL, R, L = w.shape
    h = R // 2
    tr = _tile(h, FLAT_ROWS, 16)
    nt = h // tr

    def body(*refs):
        w_ref, m_ref, v_ref = refs[:3]
        own_refs, other_refs = refs[3:3 + NL], refs[3 + NL:3 + 2 * NL]
        d_ref, nm_ref, nv_ref, g_ref = refs[3 + 2 * NL:]
        layer, half = pl.program_id(0), pl.program_id(1)
        mine = half == lax.axis_index("c")
        g = jnp.where(mine, own_refs[0][...], other_refs[0][...])
        for l in range(1, NL):
            g = jnp.where(layer == l, jnp.where(mine, own_refs[l][...], other_refs[l][...]), g)
        d, mm, vv = _adamw_update(w_ref[0], g, m_ref[0], v_ref[0])
        d_ref[0], nm_ref[0], nv_ref[0], g_ref[0] = d, mm, vv, g

    spec = pl.BlockSpec((1, tr, L), lambda l, hh, i: (l, hh * nt + i, 0))
    part = pl.BlockSpec((tr, L), lambda l, hh, i: (i, 0))
    sh = jax.ShapeDtypeStruct((NL, R, L), F32)
    return pl.pallas_call(
        body, name=name, grid=(NL, 2, nt), in_specs=[spec] * 3 + [part] * (2 * NL), out_specs=[spec] * 4,
        out_shape=[sh] * 4, compiler_params=_cparams(("parallel", "parallel", "parallel")),
    )(w, m, v, *own, *other)


def _adamw(w, g, m, v, *, name):
    NL, R, L = w.shape
    tr = _tile(R, FLAT_ROWS, 16)

    def body(w_ref, g_ref, m_ref, v_ref, d_ref, nm_ref, nv_ref):
        d_ref[...], nm_ref[...], nv_ref[...] = _adamw_update(w_ref[...], g_ref[...], m_ref[...], v_ref[...])

    spec = pl.BlockSpec((1, tr, L), lambda l, i: (l, i, 0))
    sh = jax.ShapeDtypeStruct((NL, R, L), F32)
    return pl.pallas_call(
        body, name=name, grid=(NL, R // tr), in_specs=[spec] * 4, out_specs=[spec] * 3, out_shape=[sh] * 3,
        compiler_params=_cparams(("parallel", "parallel")),
    )(w, g, m, v)


WEIGHT_NAMES = ["ab_norm", "ab_w_in", "ab_q_norm", "ab_w_q_b", "ab_kv_norm", "ab_w_kv_b", "ab_conv_w", "ab_conv_b",
                "ab_w_rg_a", "ab_b_rg_a", "ab_w_rg_x", "ab_b_rg_x", "ab_lambda", "ab_w_out", "c_norm", "c_w_in",
                "c_ln_g", "c_ln_b", "c_w_s", "c_b_s", "c_w_out", "ffn_norm", "ffn_w_gate", "ffn_w_up", "ffn_conv_w",
                "ffn_conv_b", "ffn_w_down", "final_norm"]
BIG = {"ab_w_in": 2, "ab_w_q_b": 2, "ab_w_kv_b": 2, "ab_w_out": 1, "c_w_in": 2, "c_w_out": 1,
       "ffn_w_gate": 2, "ffn_w_up": 2, "ffn_w_down": 1}
SMALL_SHARDED = {"ab_conv_w": 2, "c_norm": 1, "c_ln_g": 1, "c_ln_b": 1, "ffn_conv_w": 2}
SMALL_REPLICATED = [n for n in WEIGHT_NAMES if n not in BIG and n not in SMALL_SHARDED]


def _rows(n_elems, mult):
    r = -(-n_elems // LANES)
    return -(-r // mult) * mult


def _flat(parts, rows):
    flat = jnp.concatenate([a.reshape(-1) for a in parts])
    return jnp.pad(flat, (0, rows * LANES - flat.shape[0])).reshape(rows, LANES)


def _unflat(flat, shapes):
    flat = flat.reshape(-1)
    out, off = [], 0
    for s in shapes:
        n = math.prod(s)
        out.append(flat[off:off + n].reshape(s))
        off += n
    return out


def _join_shards(a, axis):
    a = jnp.moveaxis(a, 0, axis)
    return a.reshape(a.shape[:axis] + (a.shape[axis] * a.shape[axis + 1],) + a.shape[axis + 2:])


def kernel(x, positions, ab_norm, ab_w_in, ab_q_norm, ab_w_q_b, ab_kv_norm, ab_w_kv_b, ab_conv_w, ab_conv_b, ab_w_rg_a, ab_b_rg_a, ab_w_rg_x, ab_b_rg_x, ab_lambda, ab_w_out, c_norm, c_w_in, c_ln_g, c_ln_b, c_w_s, c_b_s, c_w_out, ffn_norm, ffn_w_gate, ffn_w_up, ffn_conv_w, ffn_conv_b, ffn_w_down, final_norm, loss_target, m_ab_norm, m_ab_w_in, m_ab_q_norm, m_ab_w_q_b, m_ab_kv_norm, m_ab_w_kv_b, m_ab_conv_w, m_ab_conv_b, m_ab_w_rg_a, m_ab_b_rg_a, m_ab_w_rg_x, m_ab_b_rg_x, m_ab_lambda, m_ab_w_out, m_c_norm, m_c_w_in, m_c_ln_g, m_c_ln_b, m_c_w_s, m_c_b_s, m_c_w_out, m_ffn_norm, m_ffn_w_gate, m_ffn_w_up, m_ffn_conv_w, m_ffn_conv_b, m_ffn_w_down, m_final_norm, v_ab_norm, v_ab_w_in, v_ab_q_norm, v_ab_w_q_b, v_ab_kv_norm, v_ab_w_kv_b, v_ab_conv_w, v_ab_conv_b, v_ab_w_rg_a, v_ab_b_rg_a, v_ab_w_rg_x, v_ab_b_rg_x, v_ab_lambda, v_ab_w_out, v_c_norm, v_c_w_in, v_c_ln_g, v_c_ln_b, v_c_w_s, v_c_b_s, v_c_w_out, v_ffn_norm, v_ffn_w_gate, v_ffn_w_up, v_ffn_conv_w, v_ffn_conv_b, v_ffn_w_down, v_final_norm):
    given = dict(locals())
    w = {n: given[n] for n in WEIGHT_NAMES}
    m = {n: given["m_" + n] for n in WEIGHT_NAMES}
    v = {n: given["v_" + n] for n in WEIGHT_NAMES}
    c = lax.axis_index("c")
    chip = 2 * lax.axis_index("x") + lax.axis_index("y")

    halves = lambda a: a.reshape(a.shape[0], 2, a.shape[1] // 2, a.shape[2])
    tr = lambda a: jnp.swapaxes(a, 1, 2)
    send = {"ab_w_in": w["ab_w_in"], "ab_w_q_b": w["ab_w_q_b"], "ab_w_kv_b": w["ab_w_kv_b"], "ab_w_out": w["ab_w_out"],
            "c_w_in": tr(w["c_w_in"]), "c_w_out": w["c_w_out"], "ffn_w_gate": tr(w["ffn_w_gate"]),
            "ffn_w_up": tr(w["ffn_w_up"]), "ffn_w_down": w["ffn_w_down"]}
    small_rows = _rows(sum(w[n].size for n in SMALL_SHARDED), 16)
    small_sh = _flat([w[n] for n in SMALL_SHARDED], small_rows).reshape(1, 2, small_rows // 2, LANES)
    first_names = ["ab_w_in", "ab_w_q_b", "ab_w_kv_b", "ab_w_out"]
    mine = {n: halves(send[n].astype(BF16)) for n in BIG}

    def put_own(own, arrived):
        return arrived.reshape(arrived.shape[0], -1, arrived.shape[-1])

    p = {"ab_norm": w["ab_norm"], "ffn_gate_t": {}, "ffn_up_t": {}, "ffn_down": {}}
    first = [mine[n] for n in first_names] + [small_sh]

    def first_arrived(got):
        full = {n: put_own(o, a) for n, o, a in zip(first_names + ["small"], first, got)}
        unshard = lambda a: jnp.swapaxes(a.reshape(N_CHIPS, -1, a.shape[-1]), 0, 1).reshape(-1, N_CHIPS * a.shape[-1])
        p.update(_prep_big(unshard(full["ab_w_in"][0]), unshard(full["ab_w_q_b"][0]), unshard(full["ab_w_kv_b"][0])))
        p["ab_w_out"] = full["ab_w_out"][0]
        small_full = dict(w)
        off = 0
        small_got = full["small"].reshape(N_CHIPS, -1)
        for n, ax in SMALL_SHARDED.items():
            seg = small_got[:, off:off + w[n].size].reshape((N_CHIPS,) + w[n].shape)
            small_full[n] = _join_shards(seg, ax)
            off += w[n].size
        p.update(_prep_small(small_full))

    def weights_ride(parts):
        def sink(arrived):
            for (own, setter), a in zip(parts, arrived):
                setter(put_own(own, a)[0])
        return _all_gather([own for own, _ in parts]), sink

    ffn_keys = {"ffn_gate_t": "ffn_w_gate", "ffn_up_t": "ffn_w_up", "ffn_down": "ffn_w_down"}
    ffn_part = lambda key, l: (mine[ffn_keys[key]][l:l + 1], functools.partial(p[key].__setitem__, l))
    rides = {
        "ab_norm": (_all_gather(first), first_arrived),
        "attn_fwd": weights_ride([ffn_part(key, 0) for key in ffn_keys]),
        "ffn0_gate": weights_ride([ffn_part("ffn_gate_t", 1)]),
        "ffn0_up": weights_ride([ffn_part("ffn_up_t", 1)]),
        "lru_fwd": weights_ride([(mine["c_w_in"], functools.partial(p.__setitem__, "c_w_in_t")),
                                 (mine["c_w_out"], functools.partial(p.__setitem__, "c_w_out"))]),
        "ffn0_down": weights_ride([ffn_part("ffn_down", 1)]),
    }

    def chip_sums(pair, arrived, tag):
        return [_chip_sum(a, b, name=f"grad_chip_sum_{tag}{i}") for i, (a, b) in enumerate(zip(arrived, pair))]

    half_of = {}

    def grads_ready(layer, ready):
        if layer == 1:
            named = {"gate1": ready["ffn_gate_t"], "up1": ready["ffn_up_t"], "down1": ready["ffn_down"]}
            hosts = {"sgu_bwd": ["down1"], "ffn0_dactbwd": ["gate1", "up1"]}
        else:
            named = {"c_in": ready["c_w_in_t"], "c_out": ready["c_w_out"], "gate0": ready["ffn_gate_t"],
                     "up0": ready["ffn_up_t"], "down0": ready["ffn_down"]}
            hosts = {"attn_bwd": ["c_in", "c_out", "down0", "gate0", "up0"]}
        tag = f"f{layer}"
        sharded = [a.reshape(N_CHIPS, 2, -1, a.shape[-1]) for a in named.values()]

        def paired(from_sib):
            pair = {k: _pair_add(a, b, name=f"grad_pair_add_{tag}{i}")
                    for i, (k, a, b) in enumerate(zip(named, sharded, from_sib))}
            for kernel_name, keys in hosts.items():
                def sink(arrived, keys=keys, kernel_name=kernel_name):
                    half_of.update(zip(keys, chip_sums([pair[k] for k in keys], arrived, f"{tag}_{kernel_name}")))
                rides[kernel_name] = (_chip_exchange([pair[k] for k in keys], scatter=True), sink)

        rides[f"ffn{layer}_dnorm"] = (_pair_swap(sharded), paired)

    loss_row, grad_x, g = _local_step(x, positions, loss_target, p, rides, grads_ready)

    cols = lambda a, n: jnp.swapaxes(a.reshape(a.shape[0], N_CHIPS, n), 0, 1)
    n_in, n_q, n_kv = w["ab_w_in"].shape[2], w["ab_w_q_b"].shape[2], w["ab_w_kv_b"].shape[2]
    small_names = SMALL_REPLICATED + list(SMALL_SHARDED)
    rs = _rows(sum(g[n].size for n in small_names) + LANES, FLAT_ROWS)
    small = _flat([loss_row] + [g[n] for n in small_names], rs)
    slot = (jnp.arange(2) == c)[:, None, None]
    last = [cols(_unperm_w_in(g["w_in_p"]), n_in), cols(_from_head_blocks(g["w_q_p"], QK_NOPE + QK_ROPE), n_q),
            cols(_join_kv(g["w_k_p"], g["w_v_p"]), n_kv), g["ab_w_out"]]
    last = [a.reshape(N_CHIPS, 2, -1, a.shape[-1]) for a in last]
    *from_sib, small_sib = _merge(_pair_swap(last), _pair_send([small])).run("tail_pair")
    pair = [_pair_add(a, b, name=f"grad_pair_add_b{i}") for i, (a, b) in enumerate(zip(last, from_sib))]
    pair_small = _sum_slots(jnp.where(slot, small[None], small_sib[None]), name="small_pair_sum")
    my_small = lax.dynamic_index_in_dim(pair_small.reshape(2, rs // 2, LANES), c, axis=0, keepdims=False)
    *arrived, all_small = _merge(_chip_exchange(pair, scatter=True), _chip_exchange([my_small], scatter=False)).run("tail_chip")
    half_of.update(zip(["in", "q", "kv", "out"], chip_sums(pair, arrived, "b")))
    half_of["small"] = _sum_slots(_put(all_small, my_small, chip, 0), name="small_chip_sum")
    keys = ("in", "q", "kv", "out", "c_in", "c_out", "gate0", "gate1", "up0", "up1", "down0", "down1", "small")
    other_half = dict(zip(keys, _pair_send([half_of[k] for k in keys]).run("grad_pair_share")))
    small_sum = jnp.where(slot, half_of["small"][None], other_half["small"][None]).reshape(rs, LANES)
    whole = lambda k: jnp.where(slot, half_of[k][None], other_half[k][None]).reshape(-1, half_of[k].shape[-1])
    grads_t = {"ab_w_in": whole("in").T[None], "ab_w_q_b": whole("q").T[None]}
    grads = {"ab_w_kv_b": whole("kv")[None], "c_w_in": whole("c_in").T[None], **{n: tr(a) for n, a in grads_t.items()}}
    by_halves = {"ab_w_out": (("out",), False), "c_w_out": (("c_out",), False), "ffn_w_down": (("down0", "down1"), False),
                 "ffn_w_gate": (("gate0", "gate1"), True), "ffn_w_up": (("up0", "up1"), True)}

    small_parts = _unflat(small_sum, [(1, LANES)] + [g[n].shape for n in small_names])
    loss = small_parts[0][0, 0]
    for n, a in zip(small_names, small_parts[1:]):
        if n in SMALL_SHARDED:
            ax = SMALL_SHARDED[n]
            a = lax.dynamic_slice_in_dim(a, chip * w[n].shape[ax], w[n].shape[ax], axis=ax)
        grads[n] = a.reshape(w[n].shape)

    delta, new_m, new_v = {}, {}, {}
    for n in BIG:
        if n in by_halves:
            ks, transposed = by_halves[n]
            view = tr if transposed else (lambda a: a)
            out = _adamw_halves(view(w[n]), view(m[n]), view(v[n]), [half_of[k] for k in ks], [other_half[k] for k in ks],
                                name=f"adamw_{n}")
            delta[n], new_m[n], new_v[n], grads[n] = (view(a) for a in out)
        elif n in grads_t:
            out = _adamw(tr(w[n]), grads_t[n], tr(m[n]), tr(v[n]), name=f"adamw_{n}")
            delta[n], new_m[n], new_v[n] = (tr(a) for a in out)
        else:
            delta[n], new_m[n], new_v[n] = _adamw(w[n], grads[n], m[n], v[n], name=f"adamw_{n}")
    small_all = [n for n in WEIGHT_NAMES if n not in BIG]
    ra = _rows(sum(w[n].size for n in small_all), FLAT_ROWS)
    pack = lambda d: _flat([d[n] for n in small_all], ra)[None]
    out = _adamw(pack(w), pack(grads), pack(m), pack(v), name="adamw_small")
    shapes = [w[n].shape for n in small_all]
    for d, flat in zip((delta, new_m, new_v), out):
        d.update(zip(small_all, _unflat(flat, shapes)))
    return (loss, grad_x, *[grads[n] for n in WEIGHT_NAMES], *[delta[n] for n in WEIGHT_NAMES],
            *[new_m[n] for n in WEIGHT_NAMES], *[new_v[n] for n in WEIGHT_NAMES])
```

```python
import functools
import math

import jax
import jax.numpy as jnp
from jax import lax
from jax.experimental import pallas as pl
from jax.experimental.pallas import tpu as pltpu

F32 = jnp.float32
BF16 = jnp.bfloat16
MESH = pl.DeviceIdType.MESH

D_MODEL = 1024
MLA_HEADS = 8
Q_LORA = 256
KV_LORA = 128
QK_NOPE = 64
QK_ROPE = 32
V_HEAD = 64
LRU_WIDTH = 512
LRU_HEADS = 8
LRU_BLOCK = 64
LRU_CONV = 4
LRU_C = 8.0
CHUNK = 128
SGU_GROUPS = 8
SGU_WIDTH = 1024
D_FF = 2816
FFN_CONV = 3
NORM_EPS = 1e-6
ROPE_BASE = 10000.0
ADAM_LR = 0.001
ADAM_B1 = 0.9
ADAM_B2 = 0.999
ADAM_EPS = 1e-08
ADAM_WD = 0.01
ADAM_STEP = 10

N_CHIPS = 4
LANES = 128
VMEM_LIMIT = 56 * 1024 * 1024
ROW_TILE = 256
SGU_TILE = 512
NORM_TILE = 1024
MM_TM, MM_TN, MM_TK = 1024, 1536, 2816
MM_TM_T, MM_TK_T = 1408, 2048
GELU_C = math.sqrt(2.0 / math.pi)


def _cparams(sem):
    return pltpu.CompilerParams(dimension_semantics=sem, vmem_limit_bytes=VMEM_LIMIT)


def _tile(n, target, mult=LANES):
    t = (min(n, target) // mult) * mult
    while t >= mult:
        if n % t == 0:
            return t
        t -= mult
    return n


GELU_K = GELU_C * 0.044715


def _gelu(x):
    t = jnp.tanh(x * (GELU_C + GELU_K * (x * x)))
    hx = 0.5 * x
    return hx + hx * t


def _gelu_and_grad(x):
    x2 = x * x
    t = jnp.tanh(x * (GELU_C + GELU_K * x2))
    hx = 0.5 * x
    dg = (0.5 + 0.5 * t) + (hx * (1.0 - t * t)) * (GELU_C + (3.0 * GELU_K) * x2)
    return hx + hx * t, dg


def _sigmoid(x):
    return 1.0 / (1.0 + jnp.exp(-x))


def _shift_rows(x, d, fill_rows):
    ext = jnp.concatenate([fill_rows, x], axis=0)
    return pltpu.roll(ext, d, 0)[8:]


def _shift_rows_up(x, d, fill_rows):
    n = x.shape[0]
    ext = jnp.concatenate([x, fill_rows], axis=0)
    return pltpu.roll(ext, n + 8 - d, 0)[:n]


def _dot(a, b, dims):
    return lax.dot_general(a.astype(BF16), b.astype(BF16), (dims, ((), ())), preferred_element_type=F32)


def _dot_nn(a, b):
    return _dot(a, b, ((1,), (0,)))


def _dot_nt(a, b):
    return _dot(a, b, ((1,), (1,)))


def _dot_tn(a, b):
    return _dot(a, b, ((0,), (0,)))


def _mm(a, b, *, name, ta=False, tb=False, res=None, out_dtype=F32, ride=None, also=None):
    if ta:
        K, M = a.shape
    else:
        M, K = a.shape
    N = b.shape[0] if tb else b.shape[1]
    tm = _tile(M, MM_TM_T if ta else (MM_TM if K <= MM_TM else MM_TM // 2), LANES if ta else 8)
    tn = _tile(N, MM_TN, LANES)
    tk = _tile(K, MM_TK_T if ta else MM_TK, LANES)
    nk = K // tk
    a_spec = pl.BlockSpec((tk, tm), lambda j, i, k: (k, i)) if ta else pl.BlockSpec((tm, tk), lambda j, i, k: (i, k))
    b_spec = pl.BlockSpec((tn, tk), lambda j, i, k: (j, k)) if tb else pl.BlockSpec((tk, tn), lambda j, i, k: (k, j))
    o_spec = pl.BlockSpec((tm, tn), lambda j, i, k: (i, j))
    dims = ((0,) if ta else (1,), (1,) if tb else (0,))
    has_res = res is not None
    pairs = [(a, b)] + ([also] if also is not None else [])
    n_ab = 2 * len(pairs)

    def body(*refs):
        r_ref = refs[n_ab] if has_res else None
        o_ref = refs[n_ab + 1] if has_res else refs[n_ab]
        p = _dot(refs[0][...], refs[1][...], dims)
        if also is not None:
            p = p + _dot(refs[2][...], refs[3][...], dims)

        def finish(r):
            if has_res:
                r = r + r_ref[...].astype(F32)
            o_ref[...] = r.astype(out_dtype)

        if nk == 1:
            finish(p)
            return
        acc_ref = refs[-1]
        k = pl.program_id(2)

        @pl.when(k == 0)
        def _():
            acc_ref[...] = p

        @pl.when(jnp.logical_and(k > 0, k < nk - 1))
        def _():
            acc_ref[...] += p

        @pl.when(k == nk - 1)
        def _():
            finish(acc_ref[...] + p)

    in_specs = [a_spec, b_spec] * len(pairs) + ([o_spec] if has_res else [])
    args = tuple(x for pair in pairs for x in pair) + ((res,) if has_res else ())
    return _pcall(
        body, name=name, grid=(N // tn, M // tm, nk), in_specs=in_specs, out_specs=[o_spec],
        out_shape=[jax.ShapeDtypeStruct((M, N), out_dtype)], args=args,
        scratch=[pltpu.VMEM((tm, tn), F32)] if nk > 1 else [], sem=("parallel", "parallel", "arbitrary"), ride=ride)[0]


def _rms_fwd(x, g, *, name, cb=0, out_dtype=BF16, ride=None):
    T = x.shape[0]
    W = g.shape[-1]
    g = g.reshape(1, W)
    tt = _tile(T, NORM_TILE, 16)

    def body(x_ref, g_ref, o_ref):
        xf = x_ref[...].astype(F32)
        rstd = lax.rsqrt(jnp.mean(xf * xf, axis=-1, keepdims=True) + NORM_EPS)
        o_ref[...] = (xf * rstd * g_ref[...]).astype(out_dtype)

    return _pcall(
        body, name=name, grid=(T // tt,),
        in_specs=[pl.BlockSpec((tt, W), lambda i: (i, cb)), pl.BlockSpec((1, W), lambda i: (0, 0))],
        out_specs=[pl.BlockSpec((tt, W), lambda i: (i, 0))], out_shape=[jax.ShapeDtypeStruct((T, W), out_dtype)],
        args=(x, g), sem=("parallel",), ride=ride)[0]


def _rms_bwd(x, g, dy, *, name, cb=0, res=None, out_dtype=F32, ride=None):
    T = x.shape[0]
    W = g.shape[-1]
    g = g.reshape(1, W)
    tt = _tile(T, NORM_TILE // 2, 16)
    has_res = res is not None

    def body(*refs):
        if has_res:
            x_ref, g_ref, dy_ref, r_ref, dx_ref, dg_ref = refs
        else:
            x_ref, g_ref, dy_ref, dx_ref, dg_ref = refs
        xf = x_ref[...].astype(F32)
        dyf = dy_ref[...].astype(F32)
        rstd = lax.rsqrt(jnp.mean(xf * xf, axis=-1, keepdims=True) + NORM_EPS)
        xhat = xf * rstd
        dxhat = dyf * g_ref[...]
        dx = rstd * (dxhat - xhat * jnp.mean(dxhat * xhat, axis=-1, keepdims=True))
        if has_res:
            dx = dx + r_ref[...].astype(F32)
        dx_ref[...] = dx.astype(out_dtype)
        part = jnp.sum(dyf * xhat, axis=0, keepdims=True)

        @pl.when(pl.program_id(0) == 0)
        def _():
            dg_ref[...] = part

        @pl.when(pl.program_id(0) > 0)
        def _():
            dg_ref[...] += part

    row = pl.BlockSpec((tt, W), lambda i: (i, 0))
    in_specs = [pl.BlockSpec((tt, W), lambda i: (i, cb)), pl.BlockSpec((1, W), lambda i: (0, 0)), row]
    args = (x, g, dy)
    if has_res:
        in_specs.append(row)
        args = args + (res,)
    return _pcall(
        body, name=name, grid=(T // tt,), in_specs=in_specs,
        out_specs=[row, pl.BlockSpec((1, W), lambda i: (0, 0))],
        out_shape=[jax.ShapeDtypeStruct((T, W), out_dtype), jax.ShapeDtypeStruct((1, W), F32)], args=args, ride=ride)


def _final_fwd_bwd(h, g, target, *, name):
    T, W = h.shape
    g = g.reshape(1, W)
    tt = _tile(T, NORM_TILE, 16)

    def body(x_ref, g_ref, t_ref, loss_ref, dx_ref, dg_ref):
        xf = x_ref[...]
        rstd = lax.rsqrt(jnp.mean(xf * xf, axis=-1, keepdims=True) + NORM_EPS)
        xhat = xf * rstd
        err = xhat * g_ref[...] - t_ref[...]
        lpart = jnp.zeros((1, LANES), F32) + (0.5 / W) * jnp.sum(err * err)
        dyf = err * (1.0 / W)
        dxhat = dyf * g_ref[...]
        dx_ref[...] = rstd * (dxhat - xhat * jnp.mean(dxhat * xhat, axis=-1, keepdims=True))
        part = jnp.sum(dyf * xhat, axis=0, keepdims=True)

        @pl.when(pl.program_id(0) == 0)
        def _():
            dg_ref[...] = part
            loss_ref[...] = lpart

        @pl.when(pl.program_id(0) > 0)
        def _():
            dg_ref[...] += part
            loss_ref[...] += lpart

    row = pl.BlockSpec((tt, W), lambda i: (i, 0))
    return pl.pallas_call(
        body, name=name, grid=(T // tt,),
        in_specs=[row, pl.BlockSpec((1, W), lambda i: (0, 0)), row],
        out_specs=[pl.BlockSpec((1, LANES), lambda i: (0, 0)), row, pl.BlockSpec((1, W), lambda i: (0, 0))],
        out_shape=[jax.ShapeDtypeStruct((1, LANES), F32), jax.ShapeDtypeStruct((T, W), F32),
                   jax.ShapeDtypeStruct((1, W), F32)],
        compiler_params=_cparams(("arbitrary",)),
    )(h, g, target)


def _swap16(x):
    lane = lax.broadcasted_iota(jnp.int32, x.shape, 1)
    return jnp.where((lane % 32) < 16, pltpu.roll(x, LANES - 16, 1), pltpu.roll(x, 16, 1))


def _rope(x, c, s):
    return x * c + _swap16(x) * s


def _rope_t(d, c, s):
    return d * c + _swap16(d * s)


def _head_block_map(fn, x, cos, sin, *, name):
    T, W = x.shape
    tt = _tile(T, NORM_TILE, 16)

    def body(x_ref, c_ref, s_ref, o_ref):
        c, s = c_ref[...], s_ref[...]
        for h in range(W // LANES):
            lanes = slice(h * LANES, (h + 1) * LANES)
            o_ref[:, lanes] = fn(x_ref[:, lanes], c, s).astype(BF16)

    tab = pl.BlockSpec((tt, LANES), lambda i: (i, 0))
    blk = pl.BlockSpec((tt, W), lambda i: (i, 0))
    return pl.pallas_call(
        body, name=name, grid=(T // tt,), in_specs=[blk, tab, tab], out_specs=blk,
        out_shape=jax.ShapeDtypeStruct((T, W), BF16), compiler_params=_cparams(("parallel",)),
    )(x, cos, sin)


def _rope_q(q, cos, sin, *, name):
    scale = _attn_scale()
    return _head_block_map(lambda x, c, s: _rope(x, c, s) * scale, q, cos, sin, name=name)


def _rope_q_bwd(dq, cos, sin, *, name):
    return _head_block_map(_rope_t, dq, cos, sin, name=name)


def _key_blocks(kv, z, cos, sin, *, kpe_block, name):
    T = kv.shape[0]
    tt = _tile(T, NORM_TILE, 16)
    W = MLA_HEADS * LANES

    def body(kv_ref, z_ref, c_ref, s_ref, o_ref):
        kr = _rope(z_ref[...], c_ref[...], s_ref[...])
        for h in range(MLA_HEADS):
            lanes = slice(h * LANES, (h + 1) * LANES)
            o_ref[:, lanes] = (kv_ref[:, lanes].astype(F32) + kr).astype(BF16)

    tab = pl.BlockSpec((tt, LANES), lambda i: (i, 0))
    blk = pl.BlockSpec((tt, W), lambda i: (i, 0))
    return pl.pallas_call(
        body, name=name, grid=(T // tt,),
        in_specs=[blk, pl.BlockSpec((tt, LANES), lambda i: (i, kpe_block)), tab, tab], out_specs=blk,
        out_shape=jax.ShapeDtypeStruct((T, W), BF16), compiler_params=_cparams(("parallel",)),
    )(kv, z, cos, sin)


def _key_rope_bwd(dk, cos, sin, *, name):
    T = dk.shape[0]
    tt = _tile(T, NORM_TILE, 16)

    def body(d_ref, c_ref, s_ref, o_ref):
        d = d_ref[:, :LANES]
        for h in range(1, MLA_HEADS):
            d = d + d_ref[:, h * LANES:(h + 1) * LANES]
        lane = lax.broadcasted_iota(jnp.int32, d.shape, 1)
        d = jnp.where(jnp.logical_and(lane >= QK_NOPE, lane < QK_NOPE + QK_ROPE), d, 0.0)
        o_ref[...] = _rope_t(d, c_ref[...], s_ref[...]).astype(BF16)

    tab = pl.BlockSpec((tt, LANES), lambda i: (i, 0))
    return pl.pallas_call(
        body, name=name, grid=(T // tt,),
        in_specs=[pl.BlockSpec((tt, MLA_HEADS * LANES), lambda i: (i, 0)), tab, tab], out_specs=tab,
        out_shape=jax.ShapeDtypeStruct((T, LANES), BF16), compiler_params=_cparams(("parallel",)),
    )(dk, cos, sin)


ATT_BLOCK = 512


def _attn_scale():
    return float((QK_NOPE + QK_ROPE) ** -0.5)


def _causal_mask(qi, kj, tq, tk):
    row = qi * tq + lax.broadcasted_iota(jnp.int32, (tq, tk), 0)
    col = kj * tk + lax.broadcasted_iota(jnp.int32, (tq, tk), 1)
    return col <= row


def _pcall(body, *, name, grid, in_specs, out_specs, out_shape, args, scratch=(), sem=None, ride=None):
    n_in, n_out, n_scr = len(args), len(out_shape), len(scratch)
    if ride is None:
        return pl.pallas_call(
            body, name=name, grid=grid, in_specs=list(in_specs), out_specs=list(out_specs), out_shape=list(out_shape),
            scratch_shapes=list(scratch), compiler_params=_cparams(sem or ("arbitrary",) * len(grid)))(*args)
    ex, sink = ride
    o0 = n_in + len(ex.arrs)
    s0 = o0 + n_out + len(ex.out_shapes)

    def hosted(*refs):
        parts = (refs[n_in:o0], refs[o0 + n_out:s0], refs[-2], refs[-1])
        ids = [pl.program_id(i) for i in range(len(grid))]
        pl.when(functools.reduce(jnp.logical_and, [i == 0 for i in ids]))(lambda: ex.start(*parts))
        body(*refs[:n_in], *refs[o0:o0 + n_out], *refs[s0:s0 + n_scr])
        pl.when(functools.reduce(jnp.logical_and, [i == n - 1 for i, n in zip(ids, grid)]))(lambda: ex.finish(*parts))

    outs = pl.pallas_call(
        hosted, name=name, grid=grid, in_specs=list(in_specs) + ex.in_specs, out_specs=list(out_specs) + ex.out_specs,
        out_shape=list(out_shape) + ex.out_shapes, scratch_shapes=list(scratch) + ex.scratch,
        compiler_params=_cparams(("arbitrary",) * len(grid)))(*args, *ex.arrs)
    sink(outs[n_out:])
    return outs[:n_out]


PAIRS = MLA_HEADS // 2


def _own_lanes(x, first):
    lane = lax.broadcasted_iota(jnp.int32, x.shape, 1)
    return jnp.where((lane < V_HEAD) if first else (lane >= V_HEAD), x, 0.0)


def _lane_sums_as_row(x):
    hi = x.astype(BF16)
    lo = (x - hi.astype(F32)).astype(BF16)
    ones = jnp.ones((8, LANES), BF16)
    return (_dot_nt(ones, hi) + _dot_nt(ones, lo))[0:1, :]


def _attn_fwd(q, k, kv, *, B, S, v_block0, name, ride=None):
    tq = tk = min(ATT_BLOCK, S)
    nq = S // tq
    T = B * S

    def body(q_ref, k_ref, v_ref, o_ref, lse_ref):
        qi = pl.program_id(2)
        qs = (q_ref[:, :LANES], q_ref[:, LANES:])

        def step(masked):
            def f(j, carry):
                rows = pl.ds(pl.multiple_of(j * tk, tk), tk)
                vb = v_ref[rows, :]
                out = []
                for h in range(2):
                    m, l, acc = carry[h]
                    s = _dot_nt(qs[h], k_ref[rows, h * LANES:(h + 1) * LANES])
                    if masked:
                        s = jnp.where(_causal_mask(qi, j, tq, tk), s, -jnp.inf)
                    m_new = jnp.maximum(m, jnp.max(s, axis=-1, keepdims=True))
                    alpha = jnp.exp(m - m_new)
                    p = jnp.exp(s - m_new)
                    out.append((m_new, alpha * l + jnp.sum(p, axis=-1, keepdims=True), alpha * acc + _dot_nn(p, vb)))
                return tuple(out)
            return f

        one = (jnp.full((tq, 1), -1e30, F32), jnp.zeros((tq, 1), F32), jnp.zeros((tq, LANES), F32))
        (ma, la, acca), (mb, lb, accb) = step(True)(qi, lax.fori_loop(0, qi, step(False), (one, one)))
        o_ref[...] = _own_lanes(acca / la, True) + _own_lanes(accb / lb, False)
        for h, lse in enumerate((ma + jnp.log(la), mb + jnp.log(lb))):
            lse_ref[0, h, pl.ds(qi, 1), :] = _lane_sums_as_row(jnp.broadcast_to(lse * (1.0 / LANES), (tq, LANES)))

    return _pcall(
        body, name=name, grid=(B, PAIRS, nq),
        in_specs=[pl.BlockSpec((tq, 2 * LANES), lambda b, g, i: (b * nq + i, g)),
                  pl.BlockSpec((S, 2 * LANES), lambda b, g, i: (b, g)),
                  pl.BlockSpec((S, LANES), lambda b, g, i: (b, v_block0 + g))],
        out_specs=[pl.BlockSpec((tq, LANES), lambda b, g, i: (b * nq + i, g)),
                   pl.BlockSpec((1, 2, nq, tq), lambda b, g, i: (b, g, 0, 0))],
        out_shape=[jax.ShapeDtypeStruct((T, PAIRS * LANES), F32), jax.ShapeDtypeStruct((B, MLA_HEADS, nq, tq), F32)],
        args=(q, k, kv), ride=ride)


def _attn_bwd(q, k, kv, o, lse_rows, do, *, B, S, v_block0, name, ride=None):
    tq = tk = min(ATT_BLOCK, S)
    nq = S // tq
    T = B * S
    scale = _attn_scale()

    def body(q_ref, k_ref, v_ref, o_ref, lse_ref, do_ref, dk_ref, dv_ref, dq_ref, delta_ref):
        kj = pl.program_id(2)
        ks = (k_ref[:, :LANES], k_ref[:, LANES:])
        vb = v_ref[...]

        @pl.when(kj == 0)
        def _():
            dq_ref[...] = jnp.zeros_like(dq_ref)
            for i in range(nq):
                prod = do_ref[i * tq:(i + 1) * tq, :] * o_ref[i * tq:(i + 1) * tq, :]
                for h in range(2):
                    delta_ref[h, i:i + 1, :] = _lane_sums_as_row(_own_lanes(prod, h == 0))

        def step(masked):
            def f(i, carry):
                rows = pl.ds(pl.multiple_of(i * tq, tq), tq)
                do_b = do_ref[rows, :]
                dks, dv = list(carry[:2]), carry[2]
                for h in range(2):
                    qb = q_ref[rows, h * LANES:(h + 1) * LANES]
                    doh = _own_lanes(do_b, h == 0)
                    pt = jnp.exp(_dot_nt(ks[h], qb) - lse_ref[0, h, pl.ds(i, 1), :])
                    if masked:
                        krow = kj * tk + lax.broadcasted_iota(jnp.int32, (tk, tq), 0)
                        qcol = i * tq + lax.broadcasted_iota(jnp.int32, (tk, tq), 1)
                        pt = jnp.where(krow <= qcol, pt, 0.0)
                    dst = pt * (_dot_nt(vb, doh) - delta_ref[h, pl.ds(i, 1), :])
                    dks[h] = dks[h] + _dot_nn(dst, qb)
                    dv = dv + _dot_nn(pt, doh)
                    dq_ref[rows, h * LANES:(h + 1) * LANES] += _dot_tn(dst, ks[h]) * scale
                return dks[0], dks[1], dv
            return f

        zero = jnp.zeros((tk, LANES), F32)
        dka, dkb, dv = lax.fori_loop(kj + 1, nq, step(False), step(True)(kj, (zero, zero, zero)))
        dk_ref[:, :LANES] = dka
        dk_ref[:, LANES:] = dkb
        dv_ref[...] = dv

    krow = lambda w, c0: pl.BlockSpec((tk, w), lambda b, g, j: (b * nq + j, c0 + g))
    seq = lambda w: pl.BlockSpec((S, w), lambda b, g, j: (b, g))
    stat = pl.BlockSpec((1, 2, nq, tq), lambda b, g, j: (b, g, 0, 0))
    dk, dv, dq = _pcall(
        body, name=name, grid=(B, PAIRS, nq),
        in_specs=[seq(2 * LANES), krow(2 * LANES, 0), krow(LANES, v_block0), seq(LANES), stat, seq(LANES)],
        out_specs=[krow(2 * LANES, 0), krow(LANES, 0), seq(2 * LANES)],
        out_shape=[jax.ShapeDtypeStruct((T, MLA_HEADS * LANES), F32), jax.ShapeDtypeStruct((T, PAIRS * LANES), F32),
                   jax.ShapeDtypeStruct((T, MLA_HEADS * LANES), F32)],
        args=(q, k, kv, o, lse_rows, do), scratch=[pltpu.VMEM((2, nq, tq), F32)], ride=ride)
    return dq, dk, dv


def _lru_gates(xl, halo, cw_ref, cb_ref, wa_ref, ba_ref, wx_ref, bx_ref, lam_ref):
    xc = cb_ref[...] + cw_ref[3:4, :] * xl
    for kk in range(LRU_CONV - 1):
        xc = xc + cw_ref[kk:kk + 1, :] * _shift_rows(xl, LRU_CONV - 1 - kk, halo)
    r = _sigmoid(_dot_nn(xc, wa_ref[...]) + ba_ref[...])
    i = _sigmoid(_dot_nn(xc, wx_ref[...]) + bx_ref[...])
    lam = lam_ref[...]
    sp = jnp.maximum(-lam, 0.0) + jnp.log(1.0 + jnp.exp(-jnp.abs(lam)))
    a = jnp.exp(-LRU_C * r * sp)
    mult = jnp.sqrt(1.0 - a * a)
    return xc, r, i, sp, a, mult


def _lru_specs(tt, nt, S):
    def make(rev):
        tmap = (lambda t: nt - 1 - t) if rev else (lambda t: t)
        tile = lambda cb: pl.BlockSpec((tt, LRU_WIDTH), lambda b, t: (b * nt + tmap(t), cb))
        prev8 = lambda cb: pl.BlockSpec(
            (8, LRU_WIDTH), lambda b, t: (jnp.maximum((b * nt + tmap(t)) * (tt // 8) - 1, 0), cb))
        return tile, prev8, tmap
    return make


def _lru_fwd(z, cw, cb, wa, ba, wx, bx, lam, *, S, name, ride=None):
    T = z.shape[0]
    tt = min(ROW_TILE, S)
    nt = S // tt
    tile, prev8, _ = _lru_specs(tt, nt, S)(False)
    vec = lambda r: pl.BlockSpec((r, LRU_WIDTH), lambda b, t: (0, 0))
    mat = pl.BlockSpec((LRU_WIDTH, LRU_WIDTH), lambda b, t: (0, 0))

    def body(xl_ref, halo_ref, gate_ref, cw_ref, cb_ref, wa_ref, ba_ref, wx_ref, bx_ref, lam_ref,
             y_ref, h_ref, carry_ref):
        t = pl.program_id(1)
        first = t == 0
        halo = jnp.where(first, 0.0, halo_ref[...])
        xl_t = xl_ref[...]
        xc, r, i, sp, a, mult = _lru_gates(xl_t, halo, cw_ref, cb_ref, wa_ref, ba_ref, wx_ref, bx_ref, lam_ref)
        bv = mult * (i * xc)
        ones = jnp.ones((8, LRU_WIDTH), F32)
        zeros = jnp.zeros((8, LRU_WIDTH), F32)
        row = lax.broadcasted_iota(jnp.int32, (tt, LRU_WIDTH), 0)
        A = a
        d = 1
        while d < tt:
            if d < 8:
                a_sh = _shift_rows(A, d, ones)
                b_sh = _shift_rows(bv, d, zeros)
            else:
                a_sh = jnp.where(row < d, 1.0, pltpu.roll(A, d, 0))
                b_sh = jnp.where(row < d, 0.0, pltpu.roll(bv, d, 0))
            bv = A * b_sh + bv
            A = A * a_sh
            d *= 2
        h0 = jnp.where(first, 0.0, carry_ref[0:1, :])
        h = A * h0 + bv
        carry_ref[...] = jnp.broadcast_to(h[tt - 1:tt, :], (8, LRU_WIDTH))
        h_ref[...] = h
        y_ref[...] = (h * _gelu(gate_ref[...])).astype(BF16)

    return _pcall(
        body, name=name, grid=(T // S, nt),
        in_specs=[tile(0), prev8(0), tile(1), vec(LRU_CONV), vec(1), mat, vec(1), mat, vec(1), vec(1)],
        out_specs=[tile(0), tile(0)],
        out_shape=[jax.ShapeDtypeStruct((T, LRU_WIDTH), BF16), jax.ShapeDtypeStruct((T, LRU_WIDTH), F32)],
        args=(z, z, z, cw, cb, wa, ba, wx, bx, lam), scratch=[pltpu.VMEM((8, LRU_WIDTH), F32)], ride=ride)


def _lru_bwd(z, h, dy, cw, cb, wa, ba, wx, bx, lam, *, S, name):
    T = z.shape[0]
    tt = min(ROW_TILE, S)
    nt = S // tt
    tile, prev8, tmap = _lru_specs(tt, nt, S)(True)
    vec = lambda r: pl.BlockSpec((r, LRU_WIDTH), lambda b, t: (0, 0))
    mat = pl.BlockSpec((LRU_WIDTH, LRU_WIDTH), lambda b, t: (0, 0))

    def body(xl_ref, halo_ref, gate_ref, h_ref, hprev_ref, dy_ref, cw_ref, cb_ref, wa_ref, ba_ref, wx_ref,
             bx_ref, lam_ref, dxl_ref, dgate_ref, dcw_ref, dcb_ref, dwa_ref, dba_ref, dwx_ref, dbx_ref,
             dlam_ref, lamc_ref, ac_ref, dxc_ref):
        b = pl.program_id(0)
        t = pl.program_id(1)
        tr = nt - 1 - t
        seq_first = tr == 0
        seq_last = t == 0
        halo = jnp.where(seq_first, 0.0, halo_ref[...])
        xl_t = xl_ref[...]
        xc, r, i, sp, a, mult = _lru_gates(xl_t, halo, cw_ref, cb_ref, wa_ref, ba_ref, wx_ref, bx_ref, lam_ref)
        hh = h_ref[...]
        dyf = dy_ref[...].astype(F32)
        gl, dgl = _gelu_and_grad(gate_ref[...])
        dgate_ref[...] = (dyf * hh * dgl).astype(BF16)
        dh = dyf * gl

        a_first_later = jnp.where(seq_last, 0.0, ac_ref[...])
        lam_later = jnp.where(seq_last, 0.0, lamc_ref[...])
        row = lax.broadcasted_iota(jnp.int32, (tt, LRU_WIDTH), 0)
        A = _shift_rows_up(a, 1, a_first_later)
        lm = dh
        ones = jnp.ones((8, LRU_WIDTH), F32)
        zeros = jnp.zeros((8, LRU_WIDTH), F32)
        d = 1
        while d < tt:
            if d < 8:
                a_sh = _shift_rows_up(A, d, ones)
                l_sh = _shift_rows_up(lm, d, zeros)
            else:
                a_sh = jnp.where(row >= tt - d, 1.0, pltpu.roll(A, tt - d, 0))
                l_sh = jnp.where(row >= tt - d, 0.0, pltpu.roll(lm, tt - d, 0))
            lm = lm + A * l_sh
            A = A * a_sh
            d *= 2
        lm = lm + A * lam_later[0:1, :]
        lamc_ref[...] = jnp.broadcast_to(lm[0:1, :], (8, LRU_WIDTH))
        ac_ref[...] = jnp.broadcast_to(a[0:1, :], (8, LRU_WIDTH))

        hprev_halo = jnp.where(seq_first, 0.0, hprev_ref[...])
        h_prev = _shift_rows(hh, 1, hprev_halo)
        da = lm * h_prev
        ixc = i * xc
        dmult = lm * ixc
        di = lm * mult * xc
        dxc = lm * mult * i
        da = da - dmult * a / mult
        dlog = da * a
        dr = dlog * (-LRU_C) * sp
        dsp_part = jnp.sum(dlog * (-LRU_C) * r, axis=0, keepdims=True)
        dpa = dr * r * (1.0 - r)
        dpx = di * i * (1.0 - i)
        dxc = dxc + _dot_nt(dpa, wa_ref[...]) + _dot_nt(dpx, wx_ref[...])
        dwa_part = _dot_tn(xc, dpa)
        dwx_part = _dot_tn(xc, dpx)

        later = jnp.where(seq_last, 0.0, dxc_ref[...])
        dxl = cw_ref[3:4, :] * dxc
        for kk in range(LRU_CONV - 1):
            dxl = dxl + cw_ref[kk:kk + 1, :] * _shift_rows_up(dxc, LRU_CONV - 1 - kk, later)
        dxl_ref[...] = dxl.astype(BF16)
        dxc_ref[...] = dxc[0:8, :]
        dcw_rows = [jnp.sum(dxc * _shift_rows(xl_t, LRU_CONV - 1 - kk, halo), axis=0, keepdims=True)
                    for kk in range(LRU_CONV - 1)]
        dcw_rows.append(jnp.sum(dxc * xl_t, axis=0, keepdims=True))
        dcw_part = jnp.concatenate(dcw_rows + [jnp.zeros((8 - LRU_CONV, LRU_WIDTH), F32)], axis=0)
        lamv = lam_ref[...]
        dlam_part = dsp_part * (-_sigmoid(-lamv))
        parts = ((dcw_ref, dcw_part), (dcb_ref, jnp.sum(dxc, axis=0, keepdims=True)),
                 (dwa_ref, dwa_part), (dba_ref, jnp.sum(dpa, axis=0, keepdims=True)),
                 (dwx_ref, dwx_part), (dbx_ref, jnp.sum(dpx, axis=0, keepdims=True)),
                 (dlam_ref, dlam_part))
        start = jnp.logical_and(b == 0, t == 0)

        @pl.when(start)
        def _():
            for ref, val in parts:
                ref[...] = val

        @pl.when(jnp.logical_not(start))
        def _():
            for ref, val in parts:
                ref[...] += val

    acc = lambda r: pl.BlockSpec((r, LRU_WIDTH), lambda b, t: (0, 0))
    return pl.pallas_call(
        body, name=name, grid=(T // S, nt),
        in_specs=[tile(0), prev8(0), tile(1), tile(0), prev8(0), tile(0),
                  vec(LRU_CONV), vec(1), mat, vec(1), mat, vec(1), vec(1)],
        out_specs=[tile(0), tile(0), acc(8), acc(1), mat, acc(1), mat, acc(1), acc(1)],
        out_shape=[jax.ShapeDtypeStruct((T, LRU_WIDTH), BF16), jax.ShapeDtypeStruct((T, LRU_WIDTH), BF16),
                   jax.ShapeDtypeStruct((8, LRU_WIDTH), F32), jax.ShapeDtypeStruct((1, LRU_WIDTH), F32),
                   jax.ShapeDtypeStruct((LRU_WIDTH, LRU_WIDTH), F32), jax.ShapeDtypeStruct((1, LRU_WIDTH), F32),
                   jax.ShapeDtypeStruct((LRU_WIDTH, LRU_WIDTH), F32), jax.ShapeDtypeStruct((1, LRU_WIDTH), F32),
                   jax.ShapeDtypeStruct((1, LRU_WIDTH), F32)],
        scratch_shapes=[pltpu.VMEM((8, LRU_WIDTH), F32), pltpu.VMEM((8, LRU_WIDTH), F32),
                        pltpu.VMEM((8, LRU_WIDTH), F32)],
        compiler_params=_cparams(("arbitrary", "arbitrary")),
    )(z, z, z, h, h, dy, cw, cb, wa, ba, wx, bx, lam)


FFN_CT = 1408
FFN_TILE = 512


def _ffn_conv(g, halo, cw, cb):
    gc = cb + cw[2:3, :] * g
    for kk in range(FFN_CONV - 1):
        gc = gc + cw[kk:kk + 1, :] * _shift_rows(g, FFN_CONV - 1 - kk, halo)
    return gc


def _row_chunks(rows, chunk):
    return [slice(r0, min(r0 + chunk, rows)) for r0 in range(0, rows, chunk)]


FFN_CHUNK = 128
HALO = 16


def _ffn_act_down(g, u, cw, cb, w_down, res, *, S, name, ride=None):
    T, F = g.shape
    D = w_down.shape[1]
    tt = min(FFN_TILE, S)
    nt = S // tt
    tc = _tile(F, FFN_CT)
    nj = F // tc

    def body(g_ref, halo_ref, u_ref, cw_ref, cb_ref, w_ref, r_ref, o_ref, act_ref):
        j = pl.program_id(1)
        first = (pl.program_id(0) % nt) == 0
        cw, cb = cw_ref[...], cb_ref[...]

        @pl.when(j == 0)
        def _():
            o_ref[...] = r_ref[...]

        for r in _row_chunks(tt, FFN_CHUNK):
            before = halo_ref[...] if r.start == 0 else g_ref[r.start - HALO:r.start, :]
            halo = before.astype(F32)[HALO - 8:]
            if r.start == 0:
                halo = jnp.where(first, 0.0, halo)
            gc = _ffn_conv(g_ref[r, :].astype(F32), halo, cw, cb)
            act = (_gelu(gc) * u_ref[r, :].astype(F32)).astype(BF16)
            act_ref[r, :] = act
            o_ref[r, :] += _dot_nn(act, w_ref[...])

    tile = pl.BlockSpec((tt, tc), lambda i, j: (i, j))
    prev = pl.BlockSpec((HALO, tc), lambda i, j: (jnp.maximum(i * (tt // HALO) - 1, 0), j))
    rows = pl.BlockSpec((tt, D), lambda i, j: (i, 0))
    return _pcall(
        body, name=name, grid=(T // tt, nj),
        in_specs=[tile, prev, tile, pl.BlockSpec((FFN_CONV, tc), lambda i, j: (0, j)),
                  pl.BlockSpec((1, tc), lambda i, j: (0, j)), pl.BlockSpec((tc, D), lambda i, j: (j, 0)), rows],
        out_specs=[rows, tile], out_shape=[jax.ShapeDtypeStruct((T, D), F32), jax.ShapeDtypeStruct((T, F), BF16)],
        args=(g, g, u, cw, cb, w_down, res), sem=("parallel", "arbitrary"), ride=ride)


def _ffn_act_bwd(g, u, dh, w_down, cw, cb, *, S, name, ride=None):
    T, F = g.shape
    D = w_down.shape[1]
    tt = min(FFN_TILE, S)
    nt = S // tt
    ntt = T // tt
    tc = _tile(F, FFN_CT)

    def body(g_ref, halo_ref, u_ref, dh_ref, w_ref, cw_ref, cb_ref, dg_ref, du_ref, dcw_ref, dcb_ref, later_ref):
        step = pl.program_id(1)
        ti = (ntt - 1 - step) % nt
        cw, cb = cw_ref[...], cb_ref[...]

        @pl.when(step == 0)
        def _():
            dcw_ref[...] = jnp.zeros_like(dcw_ref)
            dcb_ref[...] = jnp.zeros_like(dcb_ref)

        halo = jnp.where(ti == 0, 0.0, halo_ref[...].astype(F32)[HALO - 8:])
        gt = g_ref[...].astype(F32)
        gl, dgl = _gelu_and_grad(_ffn_conv(gt, halo, cw, cb))
        da = _dot_nt(dh_ref[...], w_ref[...])
        du_ref[...] = (da * gl).astype(BF16)
        dgc = da * u_ref[...].astype(F32) * dgl
        later = jnp.where(ti == nt - 1, 0.0, later_ref[...])
        dg = cw[2:3, :] * dgc
        for kk in range(FFN_CONV - 1):
            dg = dg + cw[kk:kk + 1, :] * _shift_rows_up(dgc, FFN_CONV - 1 - kk, later)
        dg_ref[...] = dg.astype(BF16)
        later_ref[...] = dgc[0:8, :]
        rows = [jnp.sum(dgc * _shift_rows(gt, FFN_CONV - 1 - kk, halo), axis=0, keepdims=True)
                for kk in range(FFN_CONV - 1)]
        rows.append(jnp.sum(dgc * gt, axis=0, keepdims=True))
        dcw_ref[...] += jnp.concatenate(rows + [jnp.zeros((8 - FFN_CONV, tc), F32)], axis=0)
        dcb_ref[...] += jnp.sum(dgc, axis=0, keepdims=True)

    tile = pl.BlockSpec((tt, tc), lambda j, s: (ntt - 1 - s, j))
    prev = pl.BlockSpec((HALO, tc), lambda j, s: (jnp.maximum((ntt - 1 - s) * (tt // HALO) - 1, 0), j))
    return _pcall(
        body, name=name, grid=(F // tc, ntt),
        in_specs=[tile, prev, tile, pl.BlockSpec((tt, D), lambda j, s: (ntt - 1 - s, 0)),
                  pl.BlockSpec((tc, D), lambda j, s: (j, 0)), pl.BlockSpec((FFN_CONV, tc), lambda j, s: (0, j)),
                  pl.BlockSpec((1, tc), lambda j, s: (0, j))],
        out_specs=[tile, tile, pl.BlockSpec((8, tc), lambda j, s: (0, j)), pl.BlockSpec((1, tc), lambda j, s: (0, j))],
        out_shape=[jax.ShapeDtypeStruct((T, F), BF16), jax.ShapeDtypeStruct((T, F), BF16),
                   jax.ShapeDtypeStruct((8, F), F32), jax.ShapeDtypeStruct((1, F), F32)],
        args=(g, g, u, dh, w_down, cw, cb), scratch=[pltpu.VMEM((8, tc), F32)], ride=ride)


def _sgu_norm(zv, g_ref, b_ref):
    v = _gelu(zv)
    mu = jnp.mean(v, axis=-1, keepdims=True)
    xc = v - mu
    rstd = lax.rsqrt(jnp.mean(xc * xc, axis=-1, keepdims=True) + NORM_EPS)
    xhat = xc * rstd
    return xhat, rstd, xhat * g_ref[...] + b_ref[...]


def _sgu_fwd(zc, ln_g, ln_b, wm, bmap, *, name):
    T = zc.shape[0]
    W = SGU_WIDTH
    tt = _tile(T, SGU_TILE, CHUNK)
    nch = tt // CHUNK

    def body(z_ref, g_ref, b_ref, wm_ref, bm_ref, p_ref):
        u = _gelu(z_ref[:, :W])
        _, _, vn = _sgu_norm(z_ref[:, W:], g_ref, b_ref)
        vn = vn.astype(BF16)
        for n in range(nch):
            rows = slice(n * CHUNK, (n + 1) * CHUNK)
            for gi in range(SGU_GROUPS):
                cols = slice(gi * LANES, (gi + 1) * LANES)
                s = _dot_nn(wm_ref[gi], vn[rows, cols]) + bm_ref[:, cols]
                p_ref[rows, cols] = (u[rows, cols] * s).astype(BF16)

    const2 = lambda r, c: pl.BlockSpec((r, c), lambda i: (0, 0))
    return pl.pallas_call(
        body, name=name, grid=(T // tt,),
        in_specs=[pl.BlockSpec((tt, 2 * W), lambda i: (i, 0)), const2(1, W), const2(1, W),
                  pl.BlockSpec((SGU_GROUPS, CHUNK, CHUNK), lambda i: (0, 0, 0)), const2(CHUNK, W)],
        out_specs=pl.BlockSpec((tt, W), lambda i: (i, 0)),
        out_shape=jax.ShapeDtypeStruct((T, W), BF16),
        compiler_params=_cparams(("parallel",)),
    )(zc, ln_g, ln_b, wm, bmap)


def _sgu_bwd(zc, dp, ln_g, ln_b, wm, bmap, *, name, ride=None):
    T = zc.shape[0]
    W = SGU_WIDTH
    tt = _tile(T, SGU_TILE, CHUNK)
    nch = tt // CHUNK
    nsteps = T // tt

    def body(z_ref, dp_ref, g_ref, b_ref, wm_ref, bm_ref, dz_ref, dg_ref, db_ref, dwm_ref, dbm_ref,
             s_scr, dvn_scr):
        step = pl.program_id(0)
        zu = z_ref[:, :W]
        zv = z_ref[:, W:]
        u, dgu = _gelu_and_grad(zu)
        xhat, rstd, vn = _sgu_norm(zv, g_ref, b_ref)
        vnb = vn.astype(BF16)
        dpf = dp_ref[...].astype(F32)
        ds = dpf * u

        @pl.when(step == 0)
        def _():
            dwm_ref[...] = jnp.zeros_like(dwm_ref)
            dbm_ref[...] = jnp.zeros_like(dbm_ref)

        for n in range(nch):
            rows = slice(n * CHUNK, (n + 1) * CHUNK)
            for gi in range(SGU_GROUPS):
                cols = slice(gi * LANES, (gi + 1) * LANES)
                s_scr[rows, cols] = _dot_nn(wm_ref[gi], vnb[rows, cols]) + bm_ref[:, cols]
                dsb = ds[rows, cols]
                dvn_scr[rows, cols] = _dot_tn(wm_ref[gi], dsb)
                dwm_ref[gi] += _dot_nt(dsb, vnb[rows, cols])
                dbm_ref[:, cols] += dsb
        dz_ref[:, :W] = (dpf * s_scr[...] * dgu).astype(BF16)
        dvn = dvn_scr[...]
        dxhat = dvn * g_ref[...]
        dv = rstd * (dxhat - jnp.mean(dxhat, axis=-1, keepdims=True)
                     - xhat * jnp.mean(dxhat * xhat, axis=-1, keepdims=True))
        _, dgv = _gelu_and_grad(zv)
        dz_ref[:, W:] = (dv * dgv).astype(BF16)
        dg_part = jnp.sum(dvn * xhat, axis=0, keepdims=True)
        db_part = jnp.sum(dvn, axis=0, keepdims=True)

        @pl.when(step == 0)
        def _():
            dg_ref[...] = dg_part
            db_ref[...] = db_part

        @pl.when(step > 0)
        def _():
            dg_ref[...] += dg_part
            db_ref[...] += db_part

        @pl.when(step == nsteps - 1)
        def _():
            for gi in range(SGU_GROUPS):
                cols = slice(gi * LANES, (gi + 1) * LANES)
                tot = jnp.sum(dbm_ref[:, cols], axis=1, keepdims=True)
                dbm_ref[:, cols] = jnp.broadcast_to(tot, (CHUNK, LANES))

    const2 = lambda r, c: pl.BlockSpec((r, c), lambda i: (0, 0))
    wspec = pl.BlockSpec((SGU_GROUPS, CHUNK, CHUNK), lambda i: (0, 0, 0))
    return _pcall(
        body, name=name, grid=(nsteps,),
        in_specs=[pl.BlockSpec((tt, 2 * W), lambda i: (i, 0)), pl.BlockSpec((tt, W), lambda i: (i, 0)),
                  const2(1, W), const2(1, W), wspec, const2(CHUNK, W)],
        out_specs=[pl.BlockSpec((tt, 2 * W), lambda i: (i, 0)), const2(1, W), const2(1, W), wspec, const2(CHUNK, W)],
        out_shape=[jax.ShapeDtypeStruct((T, 2 * W), BF16), jax.ShapeDtypeStruct((1, W), F32),
                   jax.ShapeDtypeStruct((1, W), F32), jax.ShapeDtypeStruct((SGU_GROUPS, CHUNK, CHUNK), F32),
                   jax.ShapeDtypeStruct((CHUNK, W), F32)],
        args=(zc, dp, ln_g, ln_b, wm, bmap), scratch=[pltpu.VMEM((tt, W), F32), pltpu.VMEM((tt, W), F32)], ride=ride)


def _rope_tables(positions):
    half = QK_ROPE // 2
    inv_freq = jnp.exp(-math.log(ROPE_BASE) * jnp.arange(half, dtype=F32) / half)
    ang = positions.reshape(-1).astype(F32)[:, None] * inv_freq
    cos = jnp.cos(ang)
    sin = jnp.sin(ang)
    n = ang.shape[0]
    tail = LANES - QK_NOPE - QK_ROPE
    cos_t = jnp.concatenate([jnp.ones((n, QK_NOPE), F32), cos, cos, jnp.ones((n, tail), F32)], axis=1)
    sin_t = jnp.concatenate([jnp.zeros((n, QK_NOPE), F32), -sin, sin, jnp.zeros((n, tail), F32)], axis=1)
    return cos_t, sin_t


SGU_GROUP_DIM = SGU_WIDTH // SGU_GROUPS
_O1, _O2, _O3, _O4 = Q_LORA, Q_LORA + KV_LORA, Q_LORA + KV_LORA + QK_ROPE, Q_LORA + KV_LORA + QK_ROPE + LRU_WIDTH
_A0, _A1, _A2 = 2 * LRU_WIDTH, 2 * LRU_WIDTH + Q_LORA, 2 * LRU_WIDTH + Q_LORA + KV_LORA
_A3 = _A2 + QK_NOPE
Z_Q_BLOCK, Z_KV_BLOCK, Z_KPE_BLOCK = _A0 // Q_LORA, _A1 // KV_LORA, _A2 // LANES


def _perm_w_in(w_in):
    zeros = lambda n: jnp.zeros((w_in.shape[0], n), w_in.dtype)
    return jnp.concatenate([w_in[:, _O3:_O4], w_in[:, _O4:], w_in[:, :_O1], w_in[:, _O1:_O2], zeros(QK_NOPE),
                            w_in[:, _O2:_O3], zeros(LANES - QK_NOPE - QK_ROPE)], axis=1)


def _unperm_w_in(w):
    return jnp.concatenate([w[:, _A0:_A1], w[:, _A1:_A2], w[:, _A3:_A3 + QK_ROPE], w[:, :LRU_WIDTH],
                            w[:, LRU_WIDTH:_A0]], axis=1)


def _head_blocks(w, d):
    r = w.shape[0]
    return jnp.pad(w.reshape(r, MLA_HEADS, d), ((0, 0), (0, 0), (0, LANES - d))).reshape(r, MLA_HEADS * LANES)


def _from_head_blocks(w, d):
    r = w.shape[0]
    return w.reshape(r, MLA_HEADS, LANES)[:, :, :d].reshape(r, MLA_HEADS * d)


def _split_kv(w_kv):
    r = w_kv.shape[0]
    w3 = w_kv.reshape(r, MLA_HEADS, QK_NOPE + V_HEAD)
    return _head_blocks(w3[:, :, :QK_NOPE].reshape(r, -1), QK_NOPE), w3[:, :, QK_NOPE:].reshape(r, -1)


def _join_kv(w_k, w_v):
    r = w_k.shape[0]
    return jnp.concatenate([_from_head_blocks(w_k, QK_NOPE).reshape(r, MLA_HEADS, QK_NOPE),
                            w_v.reshape(r, MLA_HEADS, V_HEAD)], axis=2).reshape(r, -1)


def _prep_small(w):
    p = {n: w[n] for n in w if n not in BIG}
    eye = jnp.eye(LRU_HEADS, dtype=F32)
    dense = lambda wg: (wg[:, :, None, :] * eye[:, None, :, None]).reshape(LRU_WIDTH, LRU_WIDTH).astype(BF16)
    p["wa_d"] = dense(w["ab_w_rg_a"][0])
    p["wx_d"] = dense(w["ab_w_rg_x"][0])
    causal = jnp.tril(jnp.ones((CHUNK, CHUNK), F32))
    p["wm"] = (w["c_w_s"][0] * causal).astype(BF16)
    p["bmap"] = jnp.repeat(w["c_b_s"][0].T, SGU_GROUP_DIM, axis=1)
    return p


def _prep_big(ab_w_in, ab_w_q_b, ab_w_kv_b):
    return {"w_in_p": _perm_w_in(ab_w_in).astype(BF16),
            "w_q_p": _head_blocks(ab_w_q_b, QK_NOPE + QK_ROPE).astype(BF16),
            "w_kv_p": jnp.concatenate(_split_kv(ab_w_kv_b), axis=1).astype(BF16)}


def _ffn_fwd(h, l, p, S, rides):
    hn = _rms_fwd(h, p["ffn_norm"][l], name=f"ffn{l}_norm")
    g = _mm(hn, p["ffn_gate_t"][l], tb=True, out_dtype=BF16, name=f"ffn{l}_gate", ride=rides.get(f"ffn{l}_gate"))
    u = _mm(hn, p["ffn_up_t"][l], tb=True, out_dtype=BF16, name=f"ffn{l}_up", ride=rides.get(f"ffn{l}_up"))
    out, act = _ffn_act_down(g, u, p["ffn_conv_w"][l], p["ffn_conv_b"][l][None], p["ffn_down"][l], h, S=S,
                             name=f"ffn{l}_down", ride=rides.get(f"ffn{l}_down"))
    return out, (hn, g, u, act)


def _ffn_bwd(dh, h_in, l, p, saved, S, rides, grads_ready, also_ready=None):
    hn, g, u, act = saved
    dw_down = _mm(act, dh, ta=True, out_dtype=BF16, name=f"ffn{l}_dwdown")
    dg, du, dcw, dcb = _ffn_act_bwd(g, u, dh, p["ffn_down"][l], p["ffn_conv_w"][l], p["ffn_conv_b"][l][None], S=S,
                                    name=f"ffn{l}_dactbwd", ride=rides.get(f"ffn{l}_dactbwd"))
    dhn = _mm(dg, p["ffn_gate_t"][l], also=(du, p["ffn_up_t"][l]), out_dtype=BF16, name=f"ffn{l}_dhn")
    dw_gate_t = _mm(dg, hn, ta=True, out_dtype=BF16, name=f"ffn{l}_dwgate")
    dw_up_t = _mm(du, hn, ta=True, out_dtype=BF16, name=f"ffn{l}_dwup")
    grads_ready(l, {**(also_ready or {}), "ffn_gate_t": dw_gate_t, "ffn_up_t": dw_up_t, "ffn_down": dw_down})
    dh_in, dnorm = _rms_bwd(h_in, p["ffn_norm"][l], dhn, res=dh, name=f"ffn{l}_dnorm", ride=rides.get(f"ffn{l}_dnorm"))
    grads = dict(ffn_norm=dnorm[0], ffn_gate_t=dw_gate_t, ffn_up_t=dw_up_t, ffn_conv_w=dcw[:FFN_CONV],
                 ffn_conv_b=dcb[0], ffn_down=dw_down)
    return dh_in, grads


def _local_step(x, positions, target, p, rides=None, grads_ready=None):
    rides = {} if rides is None else rides
    grads_ready = grads_ready or (lambda layer, ready: None)
    B, S, D = x.shape
    T = B * S
    H = MLA_HEADS
    xf = x.reshape(T, D)
    tgt = target.reshape(T, D)
    cos, sin = _rope_tables(positions)

    hn0 = _rms_fwd(xf, p["ab_norm"][0], name="ab_norm", ride=rides.get("ab_norm"))
    z = _mm(hn0, p["w_in_p"], name="ab_in")
    cqn = _rms_fwd(z, p["ab_q_norm"][0], cb=Z_Q_BLOCK, name="q_norm")
    ckvn = _rms_fwd(z, p["ab_kv_norm"][0], cb=Z_KV_BLOCK, name="kv_norm")
    q = _mm(cqn, p["w_q_p"], name="q_up")
    kv = _mm(ckvn, p["w_kv_p"], out_dtype=BF16, name="kv_up")
    qs = _rope_q(q, cos, sin, name="q_rope")
    kk = _key_blocks(kv, z, cos, sin, kpe_block=Z_KPE_BLOCK, name="k_rope")
    att = dict(B=B, S=S, v_block0=H)
    o, lse = _attn_fwd(qs, kk, kv, name="attn_fwd", ride=rides.get("attn_fwd"), **att)
    lru_par = (p["ab_conv_w"][0], p["ab_conv_b"], p["wa_d"], p["ab_b_rg_a"], p["wx_d"], p["ab_b_rg_x"], p["ab_lambda"])
    y_lru, hs = _lru_fwd(z, *lru_par, S=S, name="lru_fwd", ride=rides.get("lru_fwd"))
    n_att = H * V_HEAD
    w_out_a, w_out_b = p["ab_w_out"][:n_att], p["ab_w_out"][n_att:]
    h1 = _mm(o, w_out_a, also=(y_lru, w_out_b), res=xf, name="ab_out")
    h2, ffn0 = _ffn_fwd(h1, 0, p, S, rides)

    hn2 = _rms_fwd(h2, p["c_norm"][0], name="c_norm")
    zc = _mm(hn2, p["c_w_in_t"], tb=True, name="c_in")
    pg = _sgu_fwd(zc, p["c_ln_g"], p["c_ln_b"], p["wm"], p["bmap"], name="sgu_fwd")
    h3 = _mm(pg, p["c_w_out"], res=h2, name="c_out")
    h4, ffn1 = _ffn_fwd(h3, 1, p, S, rides)

    loss_row, dh4, dfinal = _final_fwd_bwd(h4, p["final_norm"], tgt, name="final")

    dh3, g_ffn1 = _ffn_bwd(dh4, h3, 1, p, ffn1, S, rides, grads_ready)
    dpg = _mm(dh3, p["c_w_out"], tb=True, out_dtype=BF16, name="c_dp")
    dw_c_out = _mm(pg, dh3, ta=True, out_dtype=BF16, name="c_dwout")
    dzc, dlng, dlnb, dwm, dbm = _sgu_bwd(zc, dpg, p["c_ln_g"], p["c_ln_b"], p["wm"], p["bmap"], name="sgu_bwd",
                                         ride=rides.get("sgu_bwd"))
    dhn2 = _mm(dzc, p["c_w_in_t"], out_dtype=BF16, name="c_dhn")
    dw_c_in_t = _mm(dzc, hn2, ta=True, out_dtype=BF16, name="c_dwin")
    dh2, dcnorm = _rms_bwd(h2, p["c_norm"][0], dhn2, res=dh3, name="c_dnorm")
    dh1, g_ffn0 = _ffn_bwd(dh2, h1, 0, p, ffn0, S, rides, grads_ready, {"c_w_in_t": dw_c_in_t, "c_w_out": dw_c_out})

    do = _mm(dh1, w_out_a, tb=True, name="ab_do")
    dy_lru = _mm(dh1, w_out_b, tb=True, out_dtype=BF16, name="ab_dylru")
    dw_out = jnp.concatenate([_mm(o, dh1, ta=True, out_dtype=BF16, name="ab_dwout_a"),
                              _mm(y_lru, dh1, ta=True, out_dtype=BF16, name="ab_dwout_b")], axis=0)
    dq, dk, dv = _attn_bwd(qs, kk, kv, o, lse, do, name="attn_bwd", ride=rides.get("attn_bwd"), **att)
    dq_full = _rope_q_bwd(dq, cos, sin, name="q_rope_bwd")
    dkr = _key_rope_bwd(dk, cos, sin, name="k_rope_bwd")
    n_key = H * LANES
    w_k_p, w_v_p = p["w_kv_p"][:, :n_key], p["w_kv_p"][:, n_key:]
    dcqn = _mm(dq_full, p["w_q_p"], tb=True, name="q_dlat")
    dw_q_p = _mm(cqn, dq_full, ta=True, out_dtype=BF16, name="q_dw")
    dckvn = _mm(dv, w_v_p, tb=True, res=_mm(dk, w_k_p, tb=True, name="k_dlat"), name="v_dlat")
    dw_k_p = _mm(ckvn, dk, ta=True, out_dtype=BF16, name="k_dw")
    dw_v_p = _mm(ckvn, dv, ta=True, out_dtype=BF16, name="v_dw")
    dcq, dqnorm = _rms_bwd(z, p["ab_q_norm"][0], dcqn, cb=Z_Q_BLOCK, out_dtype=BF16, name="q_dnorm")
    dckv, dkvnorm = _rms_bwd(z, p["ab_kv_norm"][0], dckvn, cb=Z_KV_BLOCK, out_dtype=BF16, name="kv_dnorm")
    dxl, dgate, dcw, dcb, dwa, dba, dwx, dbx, dlam = _lru_bwd(z, hs, dy_lru, *lru_par, S=S, name="lru_bwd")
    dz = jnp.concatenate([dxl, dgate, dcq, dckv, dkr], axis=1)
    dhn0 = _mm(dz, p["w_in_p"], tb=True, out_dtype=BF16, name="ab_dhn")
    dw_in_p = _mm(hn0, dz, ta=True, out_dtype=BF16, name="ab_dwin")
    dx, dabnorm = _rms_bwd(xf, p["ab_norm"][0], dhn0, res=dh1, name="ab_dnorm")

    blocks = lambda dd: jnp.stack([dd[i * LRU_BLOCK:(i + 1) * LRU_BLOCK, i * LRU_BLOCK:(i + 1) * LRU_BLOCK]
                                   for i in range(LRU_HEADS)])
    causal = jnp.tril(jnp.ones((CHUNK, CHUNK), F32))
    grads = {
        "ab_norm": dabnorm, "w_in_p": dw_in_p, "ab_q_norm": dqnorm, "w_q_p": dw_q_p,
        "ab_kv_norm": dkvnorm, "w_k_p": dw_k_p, "w_v_p": dw_v_p, "ab_conv_w": dcw[:LRU_CONV][None], "ab_conv_b": dcb,
        "ab_w_rg_a": blocks(dwa)[None], "ab_b_rg_a": dba, "ab_w_rg_x": blocks(dwx)[None], "ab_b_rg_x": dbx,
        "ab_lambda": dlam, "ab_w_out": dw_out,
        "c_norm": dcnorm, "c_w_in_t": dw_c_in_t, "c_ln_g": dlng, "c_ln_b": dlnb,
        "c_w_s": (dwm * causal)[None], "c_b_s": dbm[:, ::SGU_GROUP_DIM].T[None], "c_w_out": dw_c_out,
        "final_norm": dfinal[0],
    }
    for name in ("ffn_norm", "ffn_conv_w", "ffn_conv_b"):
        grads[name] = jnp.stack([g_ffn0[name], g_ffn1[name]])
    for name in ("ffn_gate_t", "ffn_up_t", "ffn_down"):
        grads[name] = [g_ffn0[name], g_ffn1[name]]
    return loss_row, dx.reshape(B, S, D), grads


ANY = pl.BlockSpec(memory_space=pl.ANY)


def _place():
    x, y, c = lax.axis_index("x"), lax.axis_index("y"), lax.axis_index("c")
    chips = [(1 - x, y), (x, 1 - y), (1 - x, 1 - y)]
    return x, y, c, 2 * x + y, (x, y, 1 - c), chips


def _remote(src, dst, send_sems, recv_sems, k, to):
    return pltpu.make_async_remote_copy(src_ref=src, dst_ref=dst, send_sem=send_sems.at[k], recv_sem=recv_sems.at[k],
                                        device_id=to, device_id_type=MESH)


class _Exchange:
    def __init__(self, arrs, out_shapes, n_sems, start, finish):
        self.arrs, self.out_shapes, self.n_sems, self.start, self.finish = list(arrs), out_shapes, n_sems, start, finish

    @property
    def in_specs(self):
        return [ANY] * len(self.arrs)

    @property
    def out_specs(self):
        return [ANY] * len(self.out_shapes)

    @property
    def scratch(self):
        return [pltpu.SemaphoreType.DMA((self.n_sems,)), pltpu.SemaphoreType.DMA((self.n_sems,))]

    def split(self, refs):
        n = len(self.arrs)
        return refs[:n], refs[n:n + len(self.out_shapes)], refs[-2], refs[-1]

    def run(self, name):
        def body(*refs):
            parts = self.split(refs)
            self.start(*parts)
            self.finish(*parts)

        return pl.pallas_call(body, name=name, in_specs=self.in_specs, out_specs=self.out_specs,
                              out_shape=self.out_shapes, scratch_shapes=self.scratch)(*self.arrs)


def _put(buf, piece, idx, axis):
    return lax.dynamic_update_slice_in_dim(buf, jnp.expand_dims(piece, axis).astype(buf.dtype), idx, axis)


def _all_gather(arrs):
    n = len(arrs)
    per = 7

    def start(ins, outs, send_sems, recv_sems):
        x, y, c, j, sib, chips = _place()
        for i in range(n):
            for k, (cx, cy) in enumerate(chips):
                _remote(ins[i].at[:, c], outs[i].at[:, j, c], send_sems, recv_sems, per * i + k, (cx, cy, c)).start()
            _remote(ins[i], outs[i].at[:, j], send_sems, recv_sems, per * i + 6, sib).start()

    def finish(ins, outs, send_sems, recv_sems):
        x, y, c, j, sib, chips = _place()
        passed = []
        for i in range(n):
            for k, (cx, cy) in enumerate(chips):
                got = outs[i].at[:, 2 * cx + cy, c]
                _remote(got, got, send_sems, recv_sems, per * i + k, (cx, cy, c)).wait_recv()
                cp = _remote(got, got, send_sems, recv_sems, per * i + 3 + k, sib)
                cp.start()
                passed.append(cp)
        for i in range(n):
            for k, (cx, cy) in enumerate(chips):
                got = outs[i].at[:, 2 * cx + cy, 1 - c]
                _remote(got, got, send_sems, recv_sems, per * i + 3 + k, sib).wait_recv()
                _remote(ins[i].at[:, c], ins[i].at[:, c], send_sems, recv_sems, per * i + k, sib).wait_send()
            _remote(ins[i], outs[i].at[:, j], send_sems, recv_sems, per * i + 6, sib).wait()
        for cp in passed:
            cp.wait_send()

    shapes = [jax.ShapeDtypeStruct((a.shape[0], N_CHIPS) + a.shape[1:], a.dtype) for a in arrs]
    return _Exchange(arrs, shapes, per * n, start, finish)


class _Offset:
    def __init__(self, sems, k0):
        self.sems, self.k0 = sems, k0

    @property
    def at(self):
        return self

    def __getitem__(self, k):
        return self.sems.at[self.k0 + k]


def _merge(a, b):
    n_in, n_out = len(a.arrs), len(a.out_shapes)

    def both(fa, fb):
        def f(ins, outs, send_sems, recv_sems):
            fa(ins[:n_in], outs[:n_out], send_sems, recv_sems)
            fb(ins[n_in:], outs[n_out:], _Offset(send_sems, a.n_sems), _Offset(recv_sems, a.n_sems))
        return f

    return _Exchange(a.arrs + b.arrs, a.out_shapes + b.out_shapes, a.n_sems + b.n_sems,
                     both(a.start, b.start), both(a.finish, b.finish))


def _pair_swap(arrs):
    n = len(arrs)

    def start(ins, outs, send_sems, recv_sems):
        x, y, c, j, sib, chips = _place()
        for i in range(n):
            _remote(ins[i].at[:, 1 - c], outs[i], send_sems, recv_sems, i, sib).start()

    def finish(ins, outs, send_sems, recv_sems):
        x, y, c, j, sib, chips = _place()
        for i in range(n):
            _remote(ins[i].at[:, 1 - c], outs[i], send_sems, recv_sems, i, sib).wait()

    shapes = [jax.ShapeDtypeStruct((a.shape[0],) + a.shape[2:], a.dtype) for a in arrs]
    return _Exchange(arrs, shapes, n, start, finish)


def _pair_send(arrs):
    n = len(arrs)

    def start(ins, outs, send_sems, recv_sems):
        x, y, c, j, sib, chips = _place()
        for i in range(n):
            _remote(ins[i], outs[i], send_sems, recv_sems, i, sib).start()

    def finish(ins, outs, send_sems, recv_sems):
        x, y, c, j, sib, chips = _place()
        for i in range(n):
            _remote(ins[i], outs[i], send_sems, recv_sems, i, sib).wait()

    shapes = [jax.ShapeDtypeStruct(a.shape, a.dtype) for a in arrs]
    return _Exchange(arrs, shapes, n, start, finish)


def _chip_exchange(arrs, *, scatter):
    n = len(arrs)

    def copies(ins, outs, send_sems, recv_sems):
        x, y, c, j, sib, chips = _place()
        return [(_remote(ins[i].at[2 * cx + cy] if scatter else ins[i], outs[i].at[j], send_sems, recv_sems,
                         3 * i + k, (cx, cy, c)),
                 _remote(outs[i].at[2 * cx + cy], outs[i].at[2 * cx + cy], send_sems, recv_sems, 3 * i + k, (cx, cy, c)))
                for i in range(n) for k, (cx, cy) in enumerate(chips)]

    def start(*refs):
        for out, _ in copies(*refs):
            out.start()

    def finish(*refs):
        for out, back in copies(*refs):
            back.wait_recv()
            out.wait_send()

    shapes = [jax.ShapeDtypeStruct((N_CHIPS,) + a.shape[-2:], a.dtype) for a in arrs]
    return _Exchange(arrs, shapes, 3 * n, start, finish)


FLAT_ROWS = 512


def _pair_add(sharded, from_sib, *, name):
    n, _, R, L = sharded.shape
    tr = _tile(R, FLAT_ROWS, 16)

    def body(s_ref, b_ref, o_ref):
        own = jnp.where(lax.axis_index("c") == 0, s_ref[:, 0], s_ref[:, 1])
        o_ref[...] = (own.astype(F32) + b_ref[...].astype(F32)).astype(BF16)

    spec = pl.BlockSpec((n, tr, L), lambda i: (0, i, 0))
    return pl.pallas_call(
        body, name=name, grid=(R // tr,), in_specs=[pl.BlockSpec((n, 2, tr, L), lambda i: (0, 0, i, 0)), spec],
        out_specs=spec, out_shape=jax.ShapeDtypeStruct((n, R, L), BF16), compiler_params=_cparams(("parallel",)),
    )(sharded, from_sib)


def _chip_sum(arrived, pair, *, name):
    n, R, L = arrived.shape
    tr = _tile(R, FLAT_ROWS, 16)

    def body(a_ref, p_ref, o_ref):
        me = 2 * lax.axis_index("x") + lax.axis_index("y")
        acc = None
        for k in range(n):
            term = jnp.where(me == k, p_ref[k], a_ref[k]).astype(F32)
            acc = term if acc is None else acc + term
        o_ref[...] = acc

    spec = pl.BlockSpec((n, tr, L), lambda i: (0, i, 0))
    return pl.pallas_call(
        body, name=name, grid=(R // tr,), in_specs=[spec, spec], out_specs=pl.BlockSpec((tr, L), lambda i: (i, 0)),
        out_shape=jax.ShapeDtypeStruct((R, L), F32), compiler_params=_cparams(("parallel",)),
    )(arrived, pair)


def _sum_slots(buf, *, name):
    n, R, L = buf.shape
    tr = _tile(R, FLAT_ROWS, 16)

    def body(b_ref, o_ref):
        acc = b_ref[0].astype(F32)
        for k in range(1, n):
            acc = acc + b_ref[k].astype(F32)
        o_ref[...] = acc

    return pl.pallas_call(
        body, name=name, grid=(R // tr,), in_specs=[pl.BlockSpec((n, tr, L), lambda i: (0, i, 0))],
        out_specs=pl.BlockSpec((tr, L), lambda i: (i, 0)),
        out_shape=jax.ShapeDtypeStruct((R, L), F32), compiler_params=_cparams(("parallel",)),
    )(buf)


def _adamw_update(w, g, m, v):
    c1 = 1.0 - ADAM_B1 ** ADAM_STEP
    c2 = 1.0 - ADAM_B2 ** ADAM_STEP
    m = ADAM_B1 * m + (1.0 - ADAM_B1) * g
    v = ADAM_B2 * v + (1.0 - ADAM_B2) * (g * g)
    return -ADAM_LR * ((m / c1) / (jnp.sqrt(v / c2) + ADAM_EPS) + ADAM_WD * w), m, v


def _adamw_halves(w, m, v, own, other, *, name):
    NL, R, L = w.shape
    h = R // 2
    tr = _tile(h, FLAT_ROWS, 16)
    nt = h // tr

    def body(*refs):
        w_ref, m_ref, v_ref = refs[:3]
        own_refs, other_refs = refs[3:3 + NL], refs[3 + NL:3 + 2 * NL]
        d_ref, nm_ref, nv_ref, g_ref = refs[3 + 2 * NL:]
        layer, half = pl.program_id(0), pl.program_id(1)
        mine = half == lax.axis_index("c")
        g = jnp.where(mine, own_refs[0][...], other_refs[0][...])
        for l in range(1, NL):
            g = jnp.where(layer == l, jnp.where(mine, own_refs[l][...], other_refs[l][...]), g)
        d, mm, vv = _adamw_update(w_ref[0], g, m_ref[0], v_ref[0])
        d_ref[0], nm_ref[0], nv_ref[0], g_ref[0] = d, mm, vv, g

    spec = pl.BlockSpec((1, tr, L), lambda l, hh, i: (l, hh * nt + i, 0))
    part = pl.BlockSpec((tr, L), lambda l, hh, i: (i, 0))
    sh = jax.ShapeDtypeStruct((NL, R, L), F32)
    return pl.pallas_call(
        body, name=name, grid=(NL, 2, nt), in_specs=[spec] * 3 + [part] * (2 * NL), out_specs=[spec] * 4,
        out_shape=[sh] * 4, compiler_params=_cparams(("parallel", "parallel", "parallel")),
    )(w, m, v, *own, *other)


def _adamw(w, g, m, v, *, name):
    NL, R, L = w.shape
    tr = _tile(R, FLAT_ROWS, 16)

    def body(w_ref, g_ref, m_ref, v_ref, d_ref, nm_ref, nv_ref):
        d_ref[...], nm_ref[...], nv_ref[...] = _adamw_update(w_ref[...], g_ref[...], m_ref[...], v_ref[...])

    spec = pl.BlockSpec((1, tr, L), lambda l, i: (l, i, 0))
    sh = jax.ShapeDtypeStruct((NL, R, L), F32)
    return pl.pallas_call(
        body, name=name, grid=(NL, R // tr), in_specs=[spec] * 4, out_specs=[spec] * 3, out_shape=[sh] * 3,
        compiler_params=_cparams(("parallel", "parallel")),
    )(w, g, m, v)


WEIGHT_NAMES = ["ab_norm", "ab_w_in", "ab_q_norm", "ab_w_q_b", "ab_kv_norm", "ab_w_kv_b", "ab_conv_w", "ab_conv_b",
                "ab_w_rg_a", "ab_b_rg_a", "ab_w_rg_x", "ab_b_rg_x", "ab_lambda", "ab_w_out", "c_norm", "c_w_in",
                "c_ln_g", "c_ln_b", "c_w_s", "c_b_s", "c_w_out", "ffn_norm", "ffn_w_gate", "ffn_w_up", "ffn_conv_w",
                "ffn_conv_b", "ffn_w_down", "final_norm"]
BIG = {"ab_w_in": 2, "ab_w_q_b": 2, "ab_w_kv_b": 2, "ab_w_out": 1, "c_w_in": 2, "c_w_out": 1,
       "ffn_w_gate": 2, "ffn_w_up": 2, "ffn_w_down": 1}
SMALL_SHARDED = {"ab_conv_w": 2, "c_norm": 1, "c_ln_g": 1, "c_ln_b": 1, "ffn_conv_w": 2}
SMALL_REPLICATED = [n for n in WEIGHT_NAMES if n not in BIG and n not in SMALL_SHARDED]


def _rows(n_elems, mult):
    r = -(-n_elems // LANES)
    return -(-r // mult) * mult


def _flat(parts, rows):
    flat = jnp.concatenate([a.reshape(-1) for a in parts])
    return jnp.pad(flat, (0, rows * LANES - flat.shape[0])).reshape(rows, LANES)


def _unflat(flat, shapes):
    flat = flat.reshape(-1)
    out, off = [], 0
    for s in shapes:
        n = math.prod(s)
        out.append(flat[off:off + n].reshape(s))
        off += n
    return out


def _join_shards(a, axis):
    a = jnp.moveaxis(a, 0, axis)
    return a.reshape(a.shape[:axis] + (a.shape[axis] * a.shape[axis + 1],) + a.shape[axis + 2:])


def kernel(x, positions, ab_norm, ab_w_in, ab_q_norm, ab_w_q_b, ab_kv_norm, ab_w_kv_b, ab_conv_w, ab_conv_b, ab_w_rg_a, ab_b_rg_a, ab_w_rg_x, ab_b_rg_x, ab_lambda, ab_w_out, c_norm, c_w_in, c_ln_g, c_ln_b, c_w_s, c_b_s, c_w_out, ffn_norm, ffn_w_gate, ffn_w_up, ffn_conv_w, ffn_conv_b, ffn_w_down, final_norm, loss_target, m_ab_norm, m_ab_w_in, m_ab_q_norm, m_ab_w_q_b, m_ab_kv_norm, m_ab_w_kv_b, m_ab_conv_w, m_ab_conv_b, m_ab_w_rg_a, m_ab_b_rg_a, m_ab_w_rg_x, m_ab_b_rg_x, m_ab_lambda, m_ab_w_out, m_c_norm, m_c_w_in, m_c_ln_g, m_c_ln_b, m_c_w_s, m_c_b_s, m_c_w_out, m_ffn_norm, m_ffn_w_gate, m_ffn_w_up, m_ffn_conv_w, m_ffn_conv_b, m_ffn_w_down, m_final_norm, v_ab_norm, v_ab_w_in, v_ab_q_norm, v_ab_w_q_b, v_ab_kv_norm, v_ab_w_kv_b, v_ab_conv_w, v_ab_conv_b, v_ab_w_rg_a, v_ab_b_rg_a, v_ab_w_rg_x, v_ab_b_rg_x, v_ab_lambda, v_ab_w_out, v_c_norm, v_c_w_in, v_c_ln_g, v_c_ln_b, v_c_w_s, v_c_b_s, v_c_w_out, v_ffn_norm, v_ffn_w_gate, v_ffn_w_up, v_ffn_conv_w, v_ffn_conv_b, v_ffn_w_down, v_final_norm):
    given = dict(locals())
    w = {n: given[n] for n in WEIGHT_NAMES}
    m = {n: given["m_" + n] for n in WEIGHT_NAMES}
    v = {n: given["v_" + n] for n in WEIGHT_NAMES}
    c = lax.axis_index("c")
    chip = 2 * lax.axis_index("x") + lax.axis_index("y")

    halves = lambda a: a.reshape(a.shape[0], 2, a.shape[1] // 2, a.shape[2])
    tr = lambda a: jnp.swapaxes(a, 1, 2)
    send = {"ab_w_in": w["ab_w_in"], "ab_w_q_b": w["ab_w_q_b"], "ab_w_kv_b": w["ab_w_kv_b"], "ab_w_out": w["ab_w_out"],
            "c_w_in": tr(w["c_w_in"]), "c_w_out": w["c_w_out"], "ffn_w_gate": tr(w["ffn_w_gate"]),
            "ffn_w_up": tr(w["ffn_w_up"]), "ffn_w_down": w["ffn_w_down"]}
    small_rows = _rows(sum(w[n].size for n in SMALL_SHARDED), 16)
    small_sh = _flat([w[n] for n in SMALL_SHARDED], small_rows).reshape(1, 2, small_rows // 2, LANES)
    first_names = ["ab_w_in", "ab_w_q_b", "ab_w_kv_b", "ab_w_out"]
    mine = {n: halves(send[n].astype(BF16)) for n in BIG}

    def put_own(own, arrived):
        return arrived.reshape(arrived.shape[0], -1, arrived.shape[-1])

    p = {"ab_norm": w["ab_norm"], "ffn_gate_t": {}, "ffn_up_t": {}, "ffn_down": {}}
    first = [mine[n] for n in first_names] + [small_sh]

    def first_arrived(got):
        full = {n: put_own(o, a) for n, o, a in zip(first_names + ["small"], first, got)}
        unshard = lambda a: jnp.swapaxes(a.reshape(N_CHIPS, -1, a.shape[-1]), 0, 1).reshape(-1, N_CHIPS * a.shape[-1])
        p.update(_prep_big(unshard(full["ab_w_in"][0]), unshard(full["ab_w_q_b"][0]), unshard(full["ab_w_kv_b"][0])))
        p["ab_w_out"] = full["ab_w_out"][0]
        small_full = dict(w)
        off = 0
        small_got = full["small"].reshape(N_CHIPS, -1)
        for n, ax in SMALL_SHARDED.items():
            seg = small_got[:, off:off + w[n].size].reshape((N_CHIPS,) + w[n].shape)
            small_full[n] = _join_shards(seg, ax)
            off += w[n].size
        p.update(_prep_small(small_full))

    def weights_ride(parts):
        def sink(arrived):
            for (own, setter), a in zip(parts, arrived):
                setter(put_own(own, a)[0])
        return _all_gather([own for own, _ in parts]), sink

    ffn_keys = {"ffn_gate_t": "ffn_w_gate", "ffn_up_t": "ffn_w_up", "ffn_down": "ffn_w_down"}
    ffn_part = lambda key, l: (mine[ffn_keys[key]][l:l + 1], functools.partial(p[key].__setitem__, l))
    rides = {
        "ab_norm": (_all_gather(first), first_arrived),
        "attn_fwd": weights_ride([ffn_part(key, 0) for key in ffn_keys]),
        "ffn0_gate": weights_ride([ffn_part("ffn_gate_t", 1)]),
        "ffn0_up": weights_ride([ffn_part("ffn_up_t", 1)]),
        "lru_fwd": weights_ride([(mine["c_w_in"], functools.partial(p.__setitem__, "c_w_in_t")),
                                 (mine["c_w_out"], functools.partial(p.__setitem__, "c_w_out"))]),
        "ffn0_down": weights_ride([ffn_part("ffn_down", 1)]),
    }

    def chip_sums(pair, arrived, tag):
        return [_chip_sum(a, b, name=f"grad_chip_sum_{tag}{i}") for i, (a, b) in enumerate(zip(arrived, pair))]

    half_of = {}

    def grads_ready(layer, ready):
        if layer == 1:
            named = {"gate1": ready["ffn_gate_t"], "up1": ready["ffn_up_t"], "down1": ready["ffn_down"]}
            hosts = {"sgu_bwd": ["down1"], "ffn0_dactbwd": ["gate1", "up1"]}
        else:
            named = {"c_in": ready["c_w_in_t"], "c_out": ready["c_w_out"], "gate0": ready["ffn_gate_t"],
                     "up0": ready["ffn_up_t"], "down0": ready["ffn_down"]}
            hosts = {"attn_bwd": ["c_in", "c_out", "down0", "gate0", "up0"]}
        tag = f"f{layer}"
        sharded = [a.reshape(N_CHIPS, 2, -1, a.shape[-1]) for a in named.values()]

        def paired(from_sib):
            pair = {k: _pair_add(a, b, name=f"grad_pair_add_{tag}{i}")
                    for i, (k, a, b) in enumerate(zip(named, sharded, from_sib))}
            for kernel_name, keys in hosts.items():
                def sink(arrived, keys=keys, kernel_name=kernel_name):
                    half_of.update(zip(keys, chip_sums([pair[k] for k in keys], arrived, f"{tag}_{kernel_name}")))
                rides[kernel_name] = (_chip_exchange([pair[k] for k in keys], scatter=True), sink)

        rides[f"ffn{layer}_dnorm"] = (_pair_swap(sharded), paired)

    loss_row, grad_x, g = _local_step(x, positions, loss_target, p, rides, grads_ready)

    cols = lambda a, n: jnp.swapaxes(a.reshape(a.shape[0], N_CHIPS, n), 0, 1)
    n_in, n_q, n_kv = w["ab_w_in"].shape[2], w["ab_w_q_b"].shape[2], w["ab_w_kv_b"].shape[2]
    small_names = SMALL_REPLICATED + list(SMALL_SHARDED)
    rs = _rows(sum(g[n].size for n in small_names) + LANES, FLAT_ROWS)
    small = _flat([loss_row] + [g[n] for n in small_names], rs)
    slot = (jnp.arange(2) == c)[:, None, None]
    last = [cols(_unperm_w_in(g["w_in_p"]), n_in), cols(_from_head_blocks(g["w_q_p"], QK_NOPE + QK_ROPE), n_q),
            cols(_join_kv(g["w_k_p"], g["w_v_p"]), n_kv), g["ab_w_out"]]
    last = [a.reshape(N_CHIPS, 2, -1, a.shape[-1]) for a in last]
    *from_sib, small_sib = _merge(_pair_swap(last), _pair_send([small])).run("tail_pair")
    pair = [_pair_add(a, b, name=f"grad_pair_add_b{i}") for i, (a, b) in enumerate(zip(last, from_sib))]
    pair_small = _sum_slots(jnp.where(slot, small[None], small_sib[None]), name="small_pair_sum")
    my_small = lax.dynamic_index_in_dim(pair_small.reshape(2, rs // 2, LANES), c, axis=0, keepdims=False)
    *arrived, all_small = _merge(_chip_exchange(pair, scatter=True), _chip_exchange([my_small], scatter=False)).run("tail_chip")
    half_of.update(zip(["in", "q", "kv", "out"], chip_sums(pair, arrived, "b")))
    half_of["small"] = _sum_slots(_put(all_small, my_small, chip, 0), name="small_chip_sum")
    keys = ("in", "q", "kv", "out", "c_in", "c_out", "gate0", "gate1", "up0", "up1", "down0", "down1", "small")
    other_half = dict(zip(keys, _pair_send([half_of[k] for k in keys]).run("grad_pair_share")))
    small_sum = jnp.where(slot, half_of["small"][None], other_half["small"][None]).reshape(rs, LANES)
    whole = lambda k: jnp.where(slot, half_of[k][None], other_half[k][None]).reshape(-1, half_of[k].shape[-1])
    grads_t = {"ab_w_in": whole("in").T[None], "ab_w_q_b": whole("q").T[None]}
    grads = {"ab_w_kv_b": whole("kv")[None], "c_w_in": whole("c_in").T[None], **{n: tr(a) for n, a in grads_t.items()}}
    by_halves = {"ab_w_out": (("out",), False), "c_w_out": (("c_out",), False), "ffn_w_down": (("down0", "down1"), False),
                 "ffn_w_gate": (("gate0", "gate1"), True), "ffn_w_up": (("up0", "up1"), True)}

    small_parts = _unflat(small_sum, [(1, LANES)] + [g[n].shape for n in small_names])
    loss = small_parts[0][0, 0]
    for n, a in zip(small_names, small_parts[1:]):
        if n in SMALL_SHARDED:
            ax = SMALL_SHARDED[n]
            a = lax.dynamic_slice_in_dim(a, chip * w[n].shape[ax], w[n].shape[ax], axis=ax)
        grads[n] = a.reshape(w[n].shape)

    delta, new_m, new_v = {}, {}, {}
    for n in BIG:
        if n in by_halves:
            ks, transposed = by_halves[n]
            view = tr if transposed else (lambda a: a)
            out = _adamw_halves(view(w[n]), view(m[n]), view(v[n]), [half_of[k] for k in ks], [other_half[k] for k in ks],
                                name=f"adamw_{n}")
            delta[n], new_m[n], new_v[n], grads[n] = (view(a) for a in out)
        elif n in grads_t:
            out = _adamw(tr(w[n]), grads_t[n], tr(m[n]), tr(v[n]), name=f"adamw_{n}")
            delta[n], new_m[n], new_v[n] = (tr(a) for a in out)
        else:
            delta[n], new_m[n], new_v[n] = _adamw(w[n], grads[n], m[n], v[n], name=f"adamw_{n}")
    small_all = [n for n in WEIGHT_NAMES if n not in BIG]
    ra = _rows(sum(w[n].size for n in small_all), FLAT_ROWS)
    pack = lambda d: _flat([d[n] for n in small_all], ra)[None]
    out = _adamw(pack(w), pack(grads), pack(m), pack(v), name="adamw_small")
    shapes = [w[n].shape for n in small_all]
    for d, flat in zip((delta, new_m, new_v), out):
        d.update(zip(small_all, _unflat(flat, shapes)))
    return (loss, grad_x, *[grads[n] for n in WEIGHT_NAMES], *[delta[n] for n in WEIGHT_NAMES],
            *[new_m[n] for n in WEIGHT_NAMES], *[new_v[n] for n in WEIGHT_NAMES])
```

```python
import functools
import math

import jax
import jax.numpy as jnp
from jax import lax
from jax.experimental import pallas as pl
from jax.experimental.pallas import tpu as pltpu

F32 = jnp.float32
BF16 = jnp.bfloat16
MESH = pl.DeviceIdType.MESH

D_MODEL = 1024
MLA_HEADS = 8
Q_LORA = 256
KV_LORA = 128
QK_NOPE = 64
QK_ROPE = 32
V_HEAD = 64
LRU_WIDTH = 512
LRU_HEADS = 8
LRU_BLOCK = 64
LRU_CONV = 4
LRU_C = 8.0
CHUNK = 128
SGU_GROUPS = 8
SGU_WIDTH = 1024
D_FF = 2816
FFN_CONV = 3
NORM_EPS = 1e-6
ROPE_BASE = 10000.0
ADAM_LR = 0.001
ADAM_B1 = 0.9
ADAM_B2 = 0.999
ADAM_EPS = 1e-08
ADAM_WD = 0.01
ADAM_STEP = 10

N_CHIPS = 4
LANES = 128
VMEM_LIMIT = 56 * 1024 * 1024
ROW_TILE = 256
SGU_TILE = 512
NORM_TILE = 1024
MM_TM, MM_TN, MM_TK = 1024, 1536, 2816
MM_TM_T, MM_TK_T = 1408, 2048
GELU_C = math.sqrt(2.0 / math.pi)


def _cparams(sem):
    return pltpu.CompilerParams(dimension_semantics=sem, vmem_limit_bytes=VMEM_LIMIT)


def _tile(n, target, mult=LANES):
    t = (min(n, target) // mult) * mult
    while t >= mult:
        if n % t == 0:
            return t
        t -= mult
    return n


GELU_K = GELU_C * 0.044715


def _gelu(x):
    t = jnp.tanh(x * (GELU_C + GELU_K * (x * x)))
    hx = 0.5 * x
    return hx + hx * t


def _gelu_and_grad(x):
    x2 = x * x
    t = jnp.tanh(x * (GELU_C + GELU_K * x2))
    hx = 0.5 * x
    dg = (0.5 + 0.5 * t) + (hx * (1.0 - t * t)) * (GELU_C + (3.0 * GELU_K) * x2)
    return hx + hx * t, dg


def _sigmoid(x):
    return 1.0 / (1.0 + jnp.exp(-x))


def _shift_rows(x, d, fill_rows):
    ext = jnp.concatenate([fill_rows, x], axis=0)
    return pltpu.roll(ext, d, 0)[8:]


def _shift_rows_up(x, d, fill_rows):
    n = x.shape[0]
    ext = jnp.concatenate([x, fill_rows], axis=0)
    return pltpu.roll(ext, n + 8 - d, 0)[:n]


def _dot(a, b, dims):
    return lax.dot_general(a.astype(BF16), b.astype(BF16), (dims, ((), ())), preferred_element_type=F32)


def _dot_nn(a, b):
    return _dot(a, b, ((1,), (0,)))


def _dot_nt(a, b):
    return _dot(a, b, ((1,), (1,)))


def _dot_tn(a, b):
    return _dot(a, b, ((0,), (0,)))


def _mm(a, b, *, name, ta=False, tb=False, res=None, out_dtype=F32, ride=None, also=None):
    if ta:
        K, M = a.shape
    else:
        M, K = a.shape
    N = b.shape[0] if tb else b.shape[1]
    tm = _tile(M, MM_TM_T if ta else (MM_TM if K <= MM_TM else MM_TM // 2), LANES if ta else 8)
    tn = _tile(N, MM_TN, LANES)
    tk = _tile(K, MM_TK_T if ta else MM_TK, LANES)
    nk = K // tk
    a_spec = pl.BlockSpec((tk, tm), lambda j, i, k: (k, i)) if ta else pl.BlockSpec((tm, tk), lambda j, i, k: (i, k))
    b_spec = pl.BlockSpec((tn, tk), lambda j, i, k: (j, k)) if tb else pl.BlockSpec((tk, tn), lambda j, i, k: (k, j))
    o_spec = pl.BlockSpec((tm, tn), lambda j, i, k: (i, j))
    dims = ((0,) if ta else (1,), (1,) if tb else (0,))
    has_res = res is not None
    pairs = [(a, b)] + ([also] if also is not None else [])
    n_ab = 2 * len(pairs)

    def body(*refs):
        r_ref = refs[n_ab] if has_res else None
        o_ref = refs[n_ab + 1] if has_res else refs[n_ab]
        p = _dot(refs[0][...], refs[1][...], dims)
        if also is not None:
            p = p + _dot(refs[2][...], refs[3][...], dims)

        def finish(r):
            if has_res:
                r = r + r_ref[...].astype(F32)
            o_ref[...] = r.astype(out_dtype)

        if nk == 1:
            finish(p)
            return
        acc_ref = refs[-1]
        k = pl.program_id(2)

        @pl.when(k == 0)
        def _():
            acc_ref[...] = p

        @pl.when(jnp.logical_and(k > 0, k < nk - 1))
        def _():
            acc_ref[...] += p

        @pl.when(k == nk - 1)
        def _():
            finish(acc_ref[...] + p)

    in_specs = [a_spec, b_spec] * len(pairs) + ([o_spec] if has_res else [])
    args = tuple(x for pair in pairs for x in pair) + ((res,) if has_res else ())
    return _pcall(
        body, name=name, grid=(N // tn, M // tm, nk), in_specs=in_specs, out_specs=[o_spec],
        out_shape=[jax.ShapeDtypeStruct((M, N), out_dtype)], args=args,
        scratch=[pltpu.VMEM((tm, tn), F32)] if nk > 1 else [], sem=("parallel", "parallel", "arbitrary"), ride=ride)[0]


def _rms_fwd(x, g, *, name, cb=0, out_dtype=BF16, ride=None):
    T = x.shape[0]
    W = g.shape[-1]
    g = g.reshape(1, W)
    tt = _tile(T, NORM_TILE, 16)

    def body(x_ref, g_ref, o_ref):
        xf = x_ref[...].astype(F32)
        rstd = lax.rsqrt(jnp.mean(xf * xf, axis=-1, keepdims=True) + NORM_EPS)
        o_ref[...] = (xf * rstd * g_ref[...]).astype(out_dtype)

    return _pcall(
        body, name=name, grid=(T // tt,),
        in_specs=[pl.BlockSpec((tt, W), lambda i: (i, cb)), pl.BlockSpec((1, W), lambda i: (0, 0))],
        out_specs=[pl.BlockSpec((tt, W), lambda i: (i, 0))], out_shape=[jax.ShapeDtypeStruct((T, W), out_dtype)],
        args=(x, g), sem=("parallel",), ride=ride)[0]


def _rms_bwd(x, g, dy, *, name, cb=0, res=None, out_dtype=F32, ride=None):
    T = x.shape[0]
    W = g.shape[-1]
    g = g.reshape(1, W)
    tt = _tile(T, NORM_TILE // 2, 16)
    has_res = res is not None

    def body(*refs):
        if has_res:
            x_ref, g_ref, dy_ref, r_ref, dx_ref, dg_ref = refs
        else:
            x_ref, g_ref, dy_ref, dx_ref, dg_ref = refs
        xf = x_ref[...].astype(F32)
        dyf = dy_ref[...].astype(F32)
        rstd = lax.rsqrt(jnp.mean(xf * xf, axis=-1, keepdims=True) + NORM_EPS)
        xhat = xf * rstd
        dxhat = dyf * g_ref[...]
        dx = rstd * (dxhat - xhat * jnp.mean(dxhat * xhat, axis=-1, keepdims=True))
        if has_res:
            dx = dx + r_ref[...].astype(F32)
        dx_ref[...] = dx.astype(out_dtype)
        part = jnp.sum(dyf * xhat, axis=0, keepdims=True)

        @pl.when(pl.program_id(0) == 0)
        def _():
            dg_ref[...] = part

        @pl.when(pl.program_id(0) > 0)
        def _():
            dg_ref[...] += part

    row = pl.BlockSpec((tt, W), lambda i: (i, 0))
    in_specs = [pl.BlockSpec((tt, W), lambda i: (i, cb)), pl.BlockSpec((1, W), lambda i: (0, 0)), row]
    args = (x, g, dy)
    if has_res:
        in_specs.append(row)
        args = args + (res,)
    return _pcall(
        body, name=name, grid=(T // tt,), in_specs=in_specs,
        out_specs=[row, pl.BlockSpec((1, W), lambda i: (0, 0))],
        out_shape=[jax.ShapeDtypeStruct((T, W), out_dtype), jax.ShapeDtypeStruct((1, W), F32)], args=args, ride=ride)


def _final_fwd_bwd(h, g, target, *, name):
    T, W = h.shape
    g = g.reshape(1, W)
    tt = _tile(T, NORM_TILE, 16)

    def body(x_ref, g_ref, t_ref, loss_ref, dx_ref, dg_ref):
        xf = x_ref[...]
        rstd = lax.rsqrt(jnp.mean(xf * xf, axis=-1, keepdims=True) + NORM_EPS)
        xhat = xf * rstd
        err = xhat * g_ref[...] - t_ref[...]
        lpart = jnp.zeros((1, LANES), F32) + (0.5 / W) * jnp.sum(err * err)
        dyf = err * (1.0 / W)
        dxhat = dyf * g_ref[...]
        dx_ref[...] = rstd * (dxhat - xhat * jnp.mean(dxhat * xhat, axis=-1, keepdims=True))
        part = jnp.sum(dyf * xhat, axis=0, keepdims=True)

        @pl.when(pl.program_id(0) == 0)
        def _():
            dg_ref[...] = part
            loss_ref[...] = lpart

        @pl.when(pl.program_id(0) > 0)
        def _():
            dg_ref[...] += part
            loss_ref[...] += lpart

    row = pl.BlockSpec((tt, W), lambda i: (i, 0))
    return pl.pallas_call(
        body, name=name, grid=(T // tt,),
        in_specs=[row, pl.BlockSpec((1, W), lambda i: (0, 0)), row],
        out_specs=[pl.BlockSpec((1, LANES), lambda i: (0, 0)), row, pl.BlockSpec((1, W), lambda i: (0, 0))],
        out_shape=[jax.ShapeDtypeStruct((1, LANES), F32), jax.ShapeDtypeStruct((T, W), F32),
                   jax.ShapeDtypeStruct((1, W), F32)],
        compiler_params=_cparams(("arbitrary",)),
    )(h, g, target)


def _swap16(x):
    lane = lax.broadcasted_iota(jnp.int32, x.shape, 1)
    return jnp.where((lane % 32) < 16, pltpu.roll(x, LANES - 16, 1), pltpu.roll(x, 16, 1))


def _rope(x, c, s):
    return x * c + _swap16(x) * s


def _rope_t(d, c, s):
    return d * c + _swap16(d * s)


def _head_block_map(fn, x, cos, sin, *, name):
    T, W = x.shape
    tt = _tile(T, NORM_TILE, 16)

    def body(x_ref, c_ref, s_ref, o_ref):
        c, s = c_ref[...], s_ref[...]
        for h in range(W // LANES):
            lanes = slice(h * LANES, (h + 1) * LANES)
            o_ref[:, lanes] = fn(x_ref[:, lanes], c, s).astype(BF16)

    tab = pl.BlockSpec((tt, LANES), lambda i: (i, 0))
    blk = pl.BlockSpec((tt, W), lambda i: (i, 0))
    return pl.pallas_call(
        body, name=name, grid=(T // tt,), in_specs=[blk, tab, tab], out_specs=blk,
        out_shape=jax.ShapeDtypeStruct((T, W), BF16), compiler_params=_cparams(("parallel",)),
    )(x, cos, sin)


def _rope_q(q, cos, sin, *, name):
    scale = _attn_scale()
    return _head_block_map(lambda x, c, s: _rope(x, c, s) * scale, q, cos, sin, name=name)


def _rope_q_bwd(dq, cos, sin, *, name):
    return _head_block_map(_rope_t, dq, cos, sin, name=name)


def _key_blocks(kv, z, cos, sin, *, kpe_block, name):
    T = kv.shape[0]
    tt = _tile(T, NORM_TILE, 16)
    W = MLA_HEADS * LANES

    def body(kv_ref, z_ref, c_ref, s_ref, o_ref):
        kr = _rope(z_ref[...], c_ref[...], s_ref[...])
        for h in range(MLA_HEADS):
            lanes = slice(h * LANES, (h + 1) * LANES)
            o_ref[:, lanes] = (kv_ref[:, lanes].astype(F32) + kr).astype(BF16)

    tab = pl.BlockSpec((tt, LANES), lambda i: (i, 0))
    blk = pl.BlockSpec((tt, W), lambda i: (i, 0))
    return pl.pallas_call(
        body, name=name, grid=(T // tt,),
        in_specs=[blk, pl.BlockSpec((tt, LANES), lambda i: (i, kpe_block)), tab, tab], out_specs=blk,
        out_shape=jax.ShapeDtypeStruct((T, W), BF16), compiler_params=_cparams(("parallel",)),
    )(kv, z, cos, sin)


def _key_rope_bwd(dk, cos, sin, *, name):
    T = dk.shape[0]
    tt = _tile(T, NORM_TILE, 16)

    def body(d_ref, c_ref, s_ref, o_ref):
        d = d_ref[:, :LANES]
        for h in range(1, MLA_HEADS):
            d = d + d_ref[:, h * LANES:(h + 1) * LANES]
        lane = lax.broadcasted_iota(jnp.int32, d.shape, 1)
        d = jnp.where(jnp.logical_and(lane >= QK_NOPE, lane < QK_NOPE + QK_ROPE), d, 0.0)
        o_ref[...] = _rope_t(d, c_ref[...], s_ref[...]).astype(BF16)

    tab = pl.BlockSpec((tt, LANES), lambda i: (i, 0))
    return pl.pallas_call(
        body, name=name, grid=(T // tt,),
        in_specs=[pl.BlockSpec((tt, MLA_HEADS * LANES), lambda i: (i, 0)), tab, tab], out_specs=tab,
        out_shape=jax.ShapeDtypeStruct((T, LANES), BF16), compiler_params=_cparams(("parallel",)),
    )(dk, cos, sin)


ATT_BLOCK = 512


def _attn_scale():
    return float((QK_NOPE + QK_ROPE) ** -0.5)


def _causal_mask(qi, kj, tq, tk):
    row = qi * tq + lax.broadcasted_iota(jnp.int32, (tq, tk), 0)
    col = kj * tk + lax.broadcasted_iota(jnp.int32, (tq, tk), 1)
    return col <= row


def _pcall(body, *, name, grid, in_specs, out_specs, out_shape, args, scratch=(), sem=None, ride=None):
    n_in, n_out, n_scr = len(args), len(out_shape), len(scratch)
    if ride is None:
        return pl.pallas_call(
            body, name=name, grid=grid, in_specs=list(in_specs), out_specs=list(out_specs), out_shape=list(out_shape),
            scratch_shapes=list(scratch), compiler_params=_cparams(sem or ("arbitrary",) * len(grid)))(*args)
    ex, sink = ride
    o0 = n_in + len(ex.arrs)
    s0 = o0 + n_out + len(ex.out_shapes)

    def hosted(*refs):
        parts = (refs[n_in:o0], refs[o0 + n_out:s0], refs[-2], refs[-1])
        ids = [pl.program_id(i) for i in range(len(grid))]
        pl.when(functools.reduce(jnp.logical_and, [i == 0 for i in ids]))(lambda: ex.start(*parts))
        body(*refs[:n_in], *refs[o0:o0 + n_out], *refs[s0:s0 + n_scr])
        pl.when(functools.reduce(jnp.logical_and, [i == n - 1 for i, n in zip(ids, grid)]))(lambda: ex.finish(*parts))

    outs = pl.pallas_call(
        hosted, name=name, grid=grid, in_specs=list(in_specs) + ex.in_specs, out_specs=list(out_specs) + ex.out_specs,
        out_shape=list(out_shape) + ex.out_shapes, scratch_shapes=list(scratch) + ex.scratch,
        compiler_params=_cparams(("arbitrary",) * len(grid)))(*args, *ex.arrs)
    sink(outs[n_out:])
    return outs[:n_out]


PAIRS = MLA_HEADS // 2


def _own_lanes(x, first):
    lane = lax.broadcasted_iota(jnp.int32, x.shape, 1)
    return jnp.where((lane < V_HEAD) if first else (lane >= V_HEAD), x, 0.0)


def _lane_sums_as_row(x):
    hi = x.astype(BF16)
    lo = (x - hi.astype(F32)).astype(BF16)
    ones = jnp.ones((8, LANES), BF16)
    return (_dot_nt(ones, hi) + _dot_nt(ones, lo))[0:1, :]


def _attn_fwd(q, k, kv, *, B, S, v_block0, name, ride=None):
    tq = tk = min(ATT_BLOCK, S)
    nq = S // tq
    T = B * S

    def body(q_ref, k_ref, v_ref, o_ref, lse_ref):
        qi = pl.program_id(2)
        qs = (q_ref[:, :LANES], q_ref[:, LANES:])

        def step(masked):
            def f(j, carry):
                rows = pl.ds(pl.multiple_of(j * tk, tk), tk)
                vb = v_ref[rows, :]
                out = []
                for h in range(2):
                    m, l, acc = carry[h]
                    s = _dot_nt(qs[h], k_ref[rows, h * LANES:(h + 1) * LANES])
                    if masked:
                        s = jnp.where(_causal_mask(qi, j, tq, tk), s, -jnp.inf)
                    m_new = jnp.maximum(m, jnp.max(s, axis=-1, keepdims=True))
                    alpha = jnp.exp(m - m_new)
                    p = jnp.exp(s - m_new)
                    out.append((m_new, alpha * l + jnp.sum(p, axis=-1, keepdims=True), alpha * acc + _dot_nn(p, vb)))
                return tuple(out)
            return f

        one = (jnp.full((tq, 1), -1e30, F32), jnp.zeros((tq, 1), F32), jnp.zeros((tq, LANES), F32))
        (ma, la, acca), (mb, lb, accb) = step(True)(qi, lax.fori_loop(0, qi, step(False), (one, one)))
        o_ref[...] = _own_lanes(acca / la, True) + _own_lanes(accb / lb, False)
        for h, lse in enumerate((ma + jnp.log(la), mb + jnp.log(lb))):
            lse_ref[0, h, pl.ds(qi, 1), :] = _lane_sums_as_row(jnp.broadcast_to(lse * (1.0 / LANES), (tq, LANES)))

    return _pcall(
        body, name=name, grid=(B, PAIRS, nq),
        in_specs=[pl.BlockSpec((tq, 2 * LANES), lambda b, g, i: (b * nq + i, g)),
                  pl.BlockSpec((S, 2 * LANES), lambda b, g, i: (b, g)),
                  pl.BlockSpec((S, LANES), lambda b, g, i: (b, v_block0 + g))],
        out_specs=[pl.BlockSpec((tq, LANES), lambda b, g, i: (b * nq + i, g)),
                   pl.BlockSpec((1, 2, nq, tq), lambda b, g, i: (b, g, 0, 0))],
        out_shape=[jax.ShapeDtypeStruct((T, PAIRS * LANES), F32), jax.ShapeDtypeStruct((B, MLA_HEADS, nq, tq), F32)],
        args=(q, k, kv), ride=ride)


def _attn_bwd(q, k, kv, o, lse_rows, do, *, B, S, v_block0, name, ride=None):
    tq = tk = min(ATT_BLOCK, S)
    nq = S // tq
    T = B * S
    scale = _attn_scale()

    def body(q_ref, k_ref, v_ref, o_ref, lse_ref, do_ref, dk_ref, dv_ref, dq_ref, delta_ref):
        kj = pl.program_id(2)
        ks = (k_ref[:, :LANES], k_ref[:, LANES:])
        vb = v_ref[...]

        @pl.when(kj == 0)
        def _():
            dq_ref[...] = jnp.zeros_like(dq_ref)
            for i in range(nq):
                prod = do_ref[i * tq:(i + 1) * tq, :] * o_ref[i * tq:(i + 1) * tq, :]
                for h in range(2):
                    delta_ref[h, i:i + 1, :] = _lane_sums_as_row(_own_lanes(prod, h == 0))

        def step(masked):
            def f(i, carry):
                rows = pl.ds(pl.multiple_of(i * tq, tq), tq)
                do_b = do_ref[rows, :]
                dks, dv = list(carry[:2]), carry[2]
                for h in range(2):
                    qb = q_ref[rows, h * LANES:(h + 1) * LANES]
                    doh = _own_lanes(do_b, h == 0)
                    pt = jnp.exp(_dot_nt(ks[h], qb) - lse_ref[0, h, pl.ds(i, 1), :])
                    if masked:
                        krow = kj * tk + lax.broadcasted_iota(jnp.int32, (tk, tq), 0)
                        qcol = i * tq + lax.broadcasted_iota(jnp.int32, (tk, tq), 1)
                        pt = jnp.where(krow <= qcol, pt, 0.0)
                    dst = pt * (_dot_nt(vb, doh) - delta_ref[h, pl.ds(i, 1), :])
                    dks[h] = dks[h] + _dot_nn(dst, qb)
                    dv = dv + _dot_nn(pt, doh)
                    dq_ref[rows, h * LANES:(h + 1) * LANES] += _dot_tn(dst, ks[h]) * scale
                return dks[0], dks[1], dv
            return f

        zero = jnp.zeros((tk, LANES), F32)
        dka, dkb, dv = lax.fori_loop(kj + 1, nq, step(False), step(True)(kj, (zero, zero, zero)))
        dk_ref[:, :LANES] = dka
        dk_ref[:, LANES:] = dkb
        dv_ref[...] = dv

    krow = lambda w, c0: pl.BlockSpec((tk, w), lambda b, g, j: (b * nq + j, c0 + g))
    seq = lambda w: pl.BlockSpec((S, w), lambda b, g, j: (b, g))
    stat = pl.BlockSpec((1, 2, nq, tq), lambda b, g, j: (b, g, 0, 0))
    dk, dv, dq = _pcall(
        body, name=name, grid=(B, PAIRS, nq),
        in_specs=[seq(2 * LANES), krow(2 * LANES, 0), krow(LANES, v_block0), seq(LANES), stat, seq(LANES)],
        out_specs=[krow(2 * LANES, 0), krow(LANES, 0), seq(2 * LANES)],
        out_shape=[jax.ShapeDtypeStruct((T, MLA_HEADS * LANES), F32), jax.ShapeDtypeStruct((T, PAIRS * LANES), F32),
                   jax.ShapeDtypeStruct((T, MLA_HEADS * LANES), F32)],
        args=(q, k, kv, o, lse_rows, do), scratch=[pltpu.VMEM((2, nq, tq), F32)], ride=ride)
    return dq, dk, dv


def _lru_gates(xl, halo, cw_ref, cb_ref, wa_ref, ba_ref, wx_ref, bx_ref, lam_ref):
    xc = cb_ref[...] + cw_ref[3:4, :] * xl
    for kk in range(LRU_CONV - 1):
        xc = xc + cw_ref[kk:kk + 1, :] * _shift_rows(xl, LRU_CONV - 1 - kk, halo)
    r = _sigmoid(_dot_nn(xc, wa_ref[...]) + ba_ref[...])
    i = _sigmoid(_dot_nn(xc, wx_ref[...]) + bx_ref[...])
    lam = lam_ref[...]
    sp = jnp.maximum(-lam, 0.0) + jnp.log(1.0 + jnp.exp(-jnp.abs(lam)))
    a = jnp.exp(-LRU_C * r * sp)
    mult = jnp.sqrt(1.0 - a * a)
    return xc, r, i, sp, a, mult


def _lru_specs(tt, nt, S):
    def make(rev):
        tmap = (lambda t: nt - 1 - t) if rev else (lambda t: t)
        tile = lambda cb: pl.BlockSpec((tt, LRU_WIDTH), lambda b, t: (b * nt + tmap(t), cb))
        prev8 = lambda cb: pl.BlockSpec(
            (8, LRU_WIDTH), lambda b, t: (jnp.maximum((b * nt + tmap(t)) * (tt // 8) - 1, 0), cb))
        return tile, prev8, tmap
    return make


def _lru_fwd(z, cw, cb, wa, ba, wx, bx, lam, *, S, name, ride=None):
    T = z.shape[0]
    tt = min(ROW_TILE, S)
    nt = S // tt
    tile, prev8, _ = _lru_specs(tt, nt, S)(False)
    vec = lambda r: pl.BlockSpec((r, LRU_WIDTH), lambda b, t: (0, 0))
    mat = pl.BlockSpec((LRU_WIDTH, LRU_WIDTH), lambda b, t: (0, 0))

    def body(xl_ref, halo_ref, gate_ref, cw_ref, cb_ref, wa_ref, ba_ref, wx_ref, bx_ref, lam_ref,
             y_ref, h_ref, carry_ref):
        t = pl.program_id(1)
        first = t == 0
        halo = jnp.where(first, 0.0, halo_ref[...])
        xl_t = xl_ref[...]
        xc, r, i, sp, a, mult = _lru_gates(xl_t, halo, cw_ref, cb_ref, wa_ref, ba_ref, wx_ref, bx_ref, lam_ref)
        bv = mult * (i * xc)
        ones = jnp.ones((8, LRU_WIDTH), F32)
        zeros = jnp.zeros((8, LRU_WIDTH), F32)
        row = lax.broadcasted_iota(jnp.int32, (tt, LRU_WIDTH), 0)
        A = a
        d = 1
        while d < tt:
            if d < 8:
                a_sh = _shift_rows(A, d, ones)
                b_sh = _shift_rows(bv, d, zeros)
            else:
                a_sh = jnp.where(row < d, 1.0, pltpu.roll(A, d, 0))
                b_sh = jnp.where(row < d, 0.0, pltpu.roll(bv, d, 0))
            bv = A * b_sh + bv
            A = A * a_sh
            d *= 2
        h0 = jnp.where(first, 0.0, carry_ref[0:1, :])
        h = A * h0 + bv
        carry_ref[...] = jnp.broadcast_to(h[tt - 1:tt, :], (8, LRU_WIDTH))
        h_ref[...] = h
        y_ref[...] = (h * _gelu(gate_ref[...])).astype(BF16)

    return _pcall(
        body, name=name, grid=(T // S, nt),
        in_specs=[tile(0), prev8(0), tile(1), vec(LRU_CONV), vec(1), mat, vec(1), mat, vec(1), vec(1)],
        out_specs=[tile(0), tile(0)],
        out_shape=[jax.ShapeDtypeStruct((T, LRU_WIDTH), BF16), jax.ShapeDtypeStruct((T, LRU_WIDTH), F32)],
        args=(z, z, z, cw, cb, wa, ba, wx, bx, lam), scratch=[pltpu.VMEM((8, LRU_WIDTH), F32)], ride=ride)


def _lru_bwd(z, h, dy, cw, cb, wa, ba, wx, bx, lam, *, S, name):
    T = z.shape[0]
    tt = min(ROW_TILE, S)
    nt = S // tt
    tile, prev8, tmap = _lru_specs(tt, nt, S)(True)
    vec = lambda r: pl.BlockSpec((r, LRU_WIDTH), lambda b, t: (0, 0))
    mat = pl.BlockSpec((LRU_WIDTH, LRU_WIDTH), lambda b, t: (0, 0))

    def body(xl_ref, halo_ref, gate_ref, h_ref, hprev_ref, dy_ref, cw_ref, cb_ref, wa_ref, ba_ref, wx_ref,
             bx_ref, lam_ref, dxl_ref, dgate_ref, dcw_ref, dcb_ref, dwa_ref, dba_ref, dwx_ref, dbx_ref,
             dlam_ref, lamc_ref, ac_ref, dxc_ref):
        b = pl.program_id(0)
        t = pl.program_id(1)
        tr = nt - 1 - t
        seq_first = tr == 0
        seq_last = t == 0
        halo = jnp.where(seq_first, 0.0, halo_ref[...])
        xl_t = xl_ref[...]
        xc, r, i, sp, a, mult = _lru_gates(xl_t, halo, cw_ref, cb_ref, wa_ref, ba_ref, wx_ref, bx_ref, lam_ref)
        hh = h_ref[...]
        dyf = dy_ref[...].astype(F32)
        gl, dgl = _gelu_and_grad(gate_ref[...])
        dgate_ref[...] = (dyf * hh * dgl).astype(BF16)
        dh = dyf * gl

        a_first_later = jnp.where(seq_last, 0.0, ac_ref[...])
        lam_later = jnp.where(seq_last, 0.0, lamc_ref[...])
        row = lax.broadcasted_iota(jnp.int32, (tt, LRU_WIDTH), 0)
        A = _shift_rows_up(a, 1, a_first_later)
        lm = dh
        ones = jnp.ones((8, LRU_WIDTH), F32)
        zeros = jnp.zeros((8, LRU_WIDTH), F32)
        d = 1
        while d < tt:
            if d < 8:
                a_sh = _shift_rows_up(A, d, ones)
                l_sh = _shift_rows_up(lm, d, zeros)
            else:
                a_sh = jnp.where(row >= tt - d, 1.0, pltpu.roll(A, tt - d, 0))
                l_sh = jnp.where(row >= tt - d, 0.0, pltpu.roll(lm, tt - d, 0))
            lm = lm + A * l_sh
            A = A * a_sh
            d *= 2
        lm = lm + A * lam_later[0:1, :]
        lamc_ref[...] = jnp.broadcast_to(lm[0:1, :], (8, LRU_WIDTH))
        ac_ref[...] = jnp.broadcast_to(a[0:1, :], (8, LRU_WIDTH))

        hprev_halo = jnp.where(seq_first, 0.0, hprev_ref[...])
        h_prev = _shift_rows(hh, 1, hprev_halo)
        da = lm * h_prev
        ixc = i * xc
        dmult = lm * ixc
        di = lm * mult * xc
        dxc = lm * mult * i
        da = da - dmult * a / mult
        dlog = da * a
        dr = dlog * (-LRU_C) * sp
        dsp_part = jnp.sum(dlog * (-LRU_C) * r, axis=0, keepdims=True)
        dpa = dr * r * (1.0 - r)
        dpx = di * i * (1.0 - i)
        dxc = dxc + _dot_nt(dpa, wa_ref[...]) + _dot_nt(dpx, wx_ref[...])
        dwa_part = _dot_tn(xc, dpa)
        dwx_part = _dot_tn(xc, dpx)

        later = jnp.where(seq_last, 0.0, dxc_ref[...])
        dxl = cw_ref[3:4, :] * dxc
        for kk in range(LRU_CONV - 1):
            dxl = dxl + cw_ref[kk:kk + 1, :] * _shift_rows_up(dxc, LRU_CONV - 1 - kk, later)
        dxl_ref[...] = dxl.astype(BF16)
        dxc_ref[...] = dxc[0:8, :]
        dcw_rows = [jnp.sum(dxc * _shift_rows(xl_t, LRU_CONV - 1 - kk, halo), axis=0, keepdims=True)
                    for kk in range(LRU_CONV - 1)]
        dcw_rows.append(jnp.sum(dxc * xl_t, axis=0, keepdims=True))
        dcw_part = jnp.concatenate(dcw_rows + [jnp.zeros((8 - LRU_CONV, LRU_WIDTH), F32)], axis=0)
        lamv = lam_ref[...]
        dlam_part = dsp_part * (-_sigmoid(-lamv))
        parts = ((dcw_ref, dcw_part), (dcb_ref, jnp.sum(dxc, axis=0, keepdims=True)),
                 (dwa_ref, dwa_part), (dba_ref, jnp.sum(dpa, axis=0, keepdims=True)),
                 (dwx_ref, dwx_part), (dbx_ref, jnp.sum(dpx, axis=0, keepdims=True)),
                 (dlam_ref, dlam_part))
        start = jnp.logical_and(b == 0, t == 0)

        @pl.when(start)
        def _():
            for ref, val in parts:
                ref[...] = val

        @pl.when(jnp.logical_not(start))
        def _():
            for ref, val in parts:
                ref[...] += val

    acc = lambda r: pl.BlockSpec((r, LRU_WIDTH), lambda b, t: (0, 0))
    return pl.pallas_call(
        body, name=name, grid=(T // S, nt),
        in_specs=[tile(0), prev8(0), tile(1), tile(0), prev8(0), tile(0),
                  vec(LRU_CONV), vec(1), mat, vec(1), mat, vec(1), vec(1)],
        out_specs=[tile(0), tile(0), acc(8), acc(1), mat, acc(1), mat, acc(1), acc(1)],
        out_shape=[jax.ShapeDtypeStruct((T, LRU_WIDTH), BF16), jax.ShapeDtypeStruct((T, LRU_WIDTH), BF16),
                   jax.ShapeDtypeStruct((8, LRU_WIDTH), F32), jax.ShapeDtypeStruct((1, LRU_WIDTH), F32),
                   jax.ShapeDtypeStruct((LRU_WIDTH, LRU_WIDTH), F32), jax.ShapeDtypeStruct((1, LRU_WIDTH), F32),
                   jax.ShapeDtypeStruct((LRU_WIDTH, LRU_WIDTH), F32), jax.ShapeDtypeStruct((1, LRU_WIDTH), F32),
                   jax.ShapeDtypeStruct((1, LRU_WIDTH), F32)],
        scratch_shapes=[pltpu.VMEM((8, LRU_WIDTH), F32), pltpu.VMEM((8, LRU_WIDTH), F32),
                        pltpu.VMEM((8, LRU_WIDTH), F32)],
        compiler_params=_cparams(("arbitrary", "arbitrary")),
    )(z, z, z, h, h, dy, cw, cb, wa, ba, wx, bx, lam)


FFN_CT = 1408
FFN_TILE = 512


def _ffn_conv(g, halo, cw, cb):
    gc = cb + cw[2:3, :] * g
    for kk in range(FFN_CONV - 1):
        gc = gc + cw[kk:kk + 1, :] * _shift_rows(g, FFN_CONV - 1 - kk, halo)
    return gc


def _row_chunks(rows, chunk):
    return [slice(r0, min(r0 + chunk, rows)) for r0 in range(0, rows, chunk)]


FFN_CHUNK = 128
HALO = 16


def _ffn_act_down(g, u, cw, cb, w_down, res, *, S, name, ride=None):
    T, F = g.shape
    D = w_down.shape[1]
    tt = min(2 * FFN_TILE, S)
    nt = S // tt
    tc = _tile(F, FFN_CT)
    nj = F // tc

    def body(g_ref, halo_ref, u_ref, cw_ref, cb_ref, w_ref, r_ref, o_ref, act_ref):
        j = pl.program_id(1)
        first = (pl.program_id(0) % nt) == 0
        cw, cb = cw_ref[...], cb_ref[...]

        @pl.when(j == 0)
        def _():
            o_ref[...] = r_ref[...]

        for r in _row_chunks(tt, FFN_CHUNK):
            before = halo_ref[...] if r.start == 0 else g_ref[r.start - HALO:r.start, :]
            halo = before.astype(F32)[HALO - 8:]
            if r.start == 0:
                halo = jnp.where(first, 0.0, halo)
            gc = _ffn_conv(g_ref[r, :].astype(F32), halo, cw, cb)
            act = (_gelu(gc) * u_ref[r, :].astype(F32)).astype(BF16)
            act_ref[r, :] = act
            o_ref[r, :] += _dot_nn(act, w_ref[...])

    tile = pl.BlockSpec((tt, tc), lambda i, j: (i, j))
    prev = pl.BlockSpec((HALO, tc), lambda i, j: (jnp.maximum(i * (tt // HALO) - 1, 0), j))
    rows = pl.BlockSpec((tt, D), lambda i, j: (i, 0))
    return _pcall(
        body, name=name, grid=(T // tt, nj),
        in_specs=[tile, prev, tile, pl.BlockSpec((FFN_CONV, tc), lambda i, j: (0, j)),
                  pl.BlockSpec((1, tc), lambda i, j: (0, j)), pl.BlockSpec((tc, D), lambda i, j: (j, 0)), rows],
        out_specs=[rows, tile], out_shape=[jax.ShapeDtypeStruct((T, D), F32), jax.ShapeDtypeStruct((T, F), BF16)],
        args=(g, g, u, cw, cb, w_down, res), sem=("parallel", "arbitrary"), ride=ride)


def _ffn_act_bwd(g, u, dh, w_down, cw, cb, *, S, name, ride=None):
    T, F = g.shape
    D = w_down.shape[1]
    tt = min(FFN_TILE, S)
    nt = S // tt
    ntt = T // tt
    tc = _tile(F, FFN_CT)

    def body(g_ref, halo_ref, u_ref, dh_ref, w_ref, cw_ref, cb_ref, dg_ref, du_ref, dcw_ref, dcb_ref, later_ref):
        step = pl.program_id(1)
        ti = (ntt - 1 - step) % nt
        cw, cb = cw_ref[...], cb_ref[...]

        @pl.when(step == 0)
        def _():
            dcw_ref[...] = jnp.zeros_like(dcw_ref)
            dcb_ref[...] = jnp.zeros_like(dcb_ref)

        halo = jnp.where(ti == 0, 0.0, halo_ref[...].astype(F32)[HALO - 8:])
        gt = g_ref[...].astype(F32)
        gl, dgl = _gelu_and_grad(_ffn_conv(gt, halo, cw, cb))
        da = _dot_nt(dh_ref[...], w_ref[...])
        du_ref[...] = (da * gl).astype(BF16)
        dgc = da * u_ref[...].astype(F32) * dgl
        later = jnp.where(ti == nt - 1, 0.0, later_ref[...])
        dg = cw[2:3, :] * dgc
        for kk in range(FFN_CONV - 1):
            dg = dg + cw[kk:kk + 1, :] * _shift_rows_up(dgc, FFN_CONV - 1 - kk, later)
        dg_ref[...] = dg.astype(BF16)
        later_ref[...] = dgc[0:8, :]
        rows = [jnp.sum(dgc * _shift_rows(gt, FFN_CONV - 1 - kk, halo), axis=0, keepdims=True)
                for kk in range(FFN_CONV - 1)]
        rows.append(jnp.sum(dgc * gt, axis=0, keepdims=True))
        dcw_ref[...] += jnp.concatenate(rows + [jnp.zeros((8 - FFN_CONV, tc), F32)], axis=0)
        dcb_ref[...] += jnp.sum(dgc, axis=0, keepdims=True)

    tile = pl.BlockSpec((tt, tc), lambda j, s: (ntt - 1 - s, j))
    prev = pl.BlockSpec((HALO, tc), lambda j, s: (jnp.maximum((ntt - 1 - s) * (tt // HALO) - 1, 0), j))
    return _pcall(
        body, name=name, grid=(F // tc, ntt),
        in_specs=[tile, prev, tile, pl.BlockSpec((tt, D), lambda j, s: (ntt - 1 - s, 0)),
                  pl.BlockSpec((tc, D), lambda j, s: (j, 0)), pl.BlockSpec((FFN_CONV, tc), lambda j, s: (0, j)),
                  pl.BlockSpec((1, tc), lambda j, s: (0, j))],
        out_specs=[tile, tile, pl.BlockSpec((8, tc), lambda j, s: (0, j)), pl.BlockSpec((1, tc), lambda j, s: (0, j))],
        out_shape=[jax.ShapeDtypeStruct((T, F), BF16), jax.ShapeDtypeStruct((T, F), BF16),
                   jax.ShapeDtypeStruct((8, F), F32), jax.ShapeDtypeStruct((1, F), F32)],
        args=(g, g, u, dh, w_down, cw, cb), scratch=[pltpu.VMEM((8, tc), F32)], ride=ride)


def _sgu_norm(zv, g_ref, b_ref):
    v = _gelu(zv)
    mu = jnp.mean(v, axis=-1, keepdims=True)
    xc = v - mu
    rstd = lax.rsqrt(jnp.mean(xc * xc, axis=-1, keepdims=True) + NORM_EPS)
    xhat = xc * rstd
    return xhat, rstd, xhat * g_ref[...] + b_ref[...]


def _sgu_fwd(zc, ln_g, ln_b, wm, bmap, *, name):
    T = zc.shape[0]
    W = SGU_WIDTH
    tt = _tile(T, SGU_TILE, CHUNK)
    nch = tt // CHUNK

    def body(z_ref, g_ref, b_ref, wm_ref, bm_ref, p_ref):
        u = _gelu(z_ref[:, :W])
        _, _, vn = _sgu_norm(z_ref[:, W:], g_ref, b_ref)
        vn = vn.astype(BF16)
        for n in range(nch):
            rows = slice(n * CHUNK, (n + 1) * CHUNK)
            for gi in range(SGU_GROUPS):
                cols = slice(gi * LANES, (gi + 1) * LANES)
                s = _dot_nn(wm_ref[gi], vn[rows, cols]) + bm_ref[:, cols]
                p_ref[rows, cols] = (u[rows, cols] * s).astype(BF16)

    const2 = lambda r, c: pl.BlockSpec((r, c), lambda i: (0, 0))
    return pl.pallas_call(
        body, name=name, grid=(T // tt,),
        in_specs=[pl.BlockSpec((tt, 2 * W), lambda i: (i, 0)), const2(1, W), const2(1, W),
                  pl.BlockSpec((SGU_GROUPS, CHUNK, CHUNK), lambda i: (0, 0, 0)), const2(CHUNK, W)],
        out_specs=pl.BlockSpec((tt, W), lambda i: (i, 0)),
        out_shape=jax.ShapeDtypeStruct((T, W), BF16),
        compiler_params=_cparams(("parallel",)),
    )(zc, ln_g, ln_b, wm, bmap)


def _sgu_bwd(zc, dp, ln_g, ln_b, wm, bmap, *, name, ride=None):
    T = zc.shape[0]
    W = SGU_WIDTH
    tt = _tile(T, SGU_TILE, CHUNK)
    nch = tt // CHUNK
    nsteps = T // tt

    def body(z_ref, dp_ref, g_ref, b_ref, wm_ref, bm_ref, dz_ref, dg_ref, db_ref, dwm_ref, dbm_ref,
             s_scr, dvn_scr):
        step = pl.program_id(0)
        zu = z_ref[:, :W]
        zv = z_ref[:, W:]
        u, dgu = _gelu_and_grad(zu)
        xhat, rstd, vn = _sgu_norm(zv, g_ref, b_ref)
        vnb = vn.astype(BF16)
        dpf = dp_ref[...].astype(F32)
        ds = dpf * u

        @pl.when(step == 0)
        def _():
            dwm_ref[...] = jnp.zeros_like(dwm_ref)
            dbm_ref[...] = jnp.zeros_like(dbm_ref)

        for n in range(nch):
            rows = slice(n * CHUNK, (n + 1) * CHUNK)
            for gi in range(SGU_GROUPS):
                cols = slice(gi * LANES, (gi + 1) * LANES)
                s_scr[rows, cols] = _dot_nn(wm_ref[gi], vnb[rows, cols]) + bm_ref[:, cols]
                dsb = ds[rows, cols]
                dvn_scr[rows, cols] = _dot_tn(wm_ref[gi], dsb)
                dwm_ref[gi] += _dot_nt(dsb, vnb[rows, cols])
                dbm_ref[:, cols] += dsb
        dz_ref[:, :W] = (dpf * s_scr[...] * dgu).astype(BF16)
        dvn = dvn_scr[...]
        dxhat = dvn * g_ref[...]
        dv = rstd * (dxhat - jnp.mean(dxhat, axis=-1, keepdims=True)
                     - xhat * jnp.mean(dxhat * xhat, axis=-1, keepdims=True))
        _, dgv = _gelu_and_grad(zv)
        dz_ref[:, W:] = (dv * dgv).astype(BF16)
        dg_part = jnp.sum(dvn * xhat, axis=0, keepdims=True)
        db_part = jnp.sum(dvn, axis=0, keepdims=True)

        @pl.when(step == 0)
        def _():
            dg_ref[...] = dg_part
            db_ref[...] = db_part

        @pl.when(step > 0)
        def _():
            dg_ref[...] += dg_part
            db_ref[...] += db_part

        @pl.when(step == nsteps - 1)
        def _():
            for gi in range(SGU_GROUPS):
                cols = slice(gi * LANES, (gi + 1) * LANES)
                tot = jnp.sum(dbm_ref[:, cols], axis=1, keepdims=True)
                dbm_ref[:, cols] = jnp.broadcast_to(tot, (CHUNK, LANES))

    const2 = lambda r, c: pl.BlockSpec((r, c), lambda i: (0, 0))
    wspec = pl.BlockSpec((SGU_GROUPS, CHUNK, CHUNK), lambda i: (0, 0, 0))
    return _pcall(
        body, name=name, grid=(nsteps,),
        in_specs=[pl.BlockSpec((tt, 2 * W), lambda i: (i, 0)), pl.BlockSpec((tt, W), lambda i: (i, 0)),
                  const2(1, W), const2(1, W), wspec, const2(CHUNK, W)],
        out_specs=[pl.BlockSpec((tt, 2 * W), lambda i: (i, 0)), const2(1, W), const2(1, W), wspec, const2(CHUNK, W)],
        out_shape=[jax.ShapeDtypeStruct((T, 2 * W), BF16), jax.ShapeDtypeStruct((1, W), F32),
                   jax.ShapeDtypeStruct((1, W), F32), jax.ShapeDtypeStruct((SGU_GROUPS, CHUNK, CHUNK), F32),
                   jax.ShapeDtypeStruct((CHUNK, W), F32)],
        args=(zc, dp, ln_g, ln_b, wm, bmap), scratch=[pltpu.VMEM((tt, W), F32), pltpu.VMEM((tt, W), F32)], ride=ride)


def _rope_tables(positions):
    half = QK_ROPE // 2
    inv_freq = jnp.exp(-math.log(ROPE_BASE) * jnp.arange(half, dtype=F32) / half)
    ang = positions.reshape(-1).astype(F32)[:, None] * inv_freq
    cos = jnp.cos(ang)
    sin = jnp.sin(ang)
    n = ang.shape[0]
    tail = LANES - QK_NOPE - QK_ROPE
    cos_t = jnp.concatenate([jnp.ones((n, QK_NOPE), F32), cos, cos, jnp.ones((n, tail), F32)], axis=1)
    sin_t = jnp.concatenate([jnp.zeros((n, QK_NOPE), F32), -sin, sin, jnp.zeros((n, tail), F32)], axis=1)
    return cos_t, sin_t


SGU_GROUP_DIM = SGU_WIDTH // SGU_GROUPS
_O1, _O2, _O3, _O4 = Q_LORA, Q_LORA + KV_LORA, Q_LORA + KV_LORA + QK_ROPE, Q_LORA + KV_LORA + QK_ROPE + LRU_WIDTH
_A0, _A1, _A2 = 2 * LRU_WIDTH, 2 * LRU_WIDTH + Q_LORA, 2 * LRU_WIDTH + Q_LORA + KV_LORA
_A3 = _A2 + QK_NOPE
Z_Q_BLOCK, Z_KV_BLOCK, Z_KPE_BLOCK = _A0 // Q_LORA, _A1 // KV_LORA, _A2 // LANES


def _perm_w_in(w_in):
    zeros = lambda n: jnp.zeros((w_in.shape[0], n), w_in.dtype)
    return jnp.concatenate([w_in[:, _O3:_O4], w_in[:, _O4:], w_in[:, :_O1], w_in[:, _O1:_O2], zeros(QK_NOPE),
                            w_in[:, _O2:_O3], zeros(LANES - QK_NOPE - QK_ROPE)], axis=1)


def _unperm_w_in(w):
    return jnp.concatenate([w[:, _A0:_A1], w[:, _A1:_A2], w[:, _A3:_A3 + QK_ROPE], w[:, :LRU_WIDTH],
                            w[:, LRU_WIDTH:_A0]], axis=1)


def _head_blocks(w, d):
    r = w.shape[0]
    return jnp.pad(w.reshape(r, MLA_HEADS, d), ((0, 0), (0, 0), (0, LANES - d))).reshape(r, MLA_HEADS * LANES)


def _from_head_blocks(w, d):
    r = w.shape[0]
    return w.reshape(r, MLA_HEADS, LANES)[:, :, :d].reshape(r, MLA_HEADS * d)


def _split_kv(w_kv):
    r = w_kv.shape[0]
    w3 = w_kv.reshape(r, MLA_HEADS, QK_NOPE + V_HEAD)
    return _head_blocks(w3[:, :, :QK_NOPE].reshape(r, -1), QK_NOPE), w3[:, :, QK_NOPE:].reshape(r, -1)


def _join_kv(w_k, w_v):
    r = w_k.shape[0]
    return jnp.concatenate([_from_head_blocks(w_k, QK_NOPE).reshape(r, MLA_HEADS, QK_NOPE),
                            w_v.reshape(r, MLA_HEADS, V_HEAD)], axis=2).reshape(r, -1)


def _prep_small(w):
    p = {n: w[n] for n in w if n not in BIG}
    eye = jnp.eye(LRU_HEADS, dtype=F32)
    dense = lambda wg: (wg[:, :, None, :] * eye[:, None, :, None]).reshape(LRU_WIDTH, LRU_WIDTH).astype(BF16)
    p["wa_d"] = dense(w["ab_w_rg_a"][0])
    p["wx_d"] = dense(w["ab_w_rg_x"][0])
    causal = jnp.tril(jnp.ones((CHUNK, CHUNK), F32))
    p["wm"] = (w["c_w_s"][0] * causal).astype(BF16)
    p["bmap"] = jnp.repeat(w["c_b_s"][0].T, SGU_GROUP_DIM, axis=1)
    return p


def _prep_big(ab_w_in, ab_w_q_b, ab_w_kv_b):
    return {"w_in_p": _perm_w_in(ab_w_in).astype(BF16),
            "w_q_p": _head_blocks(ab_w_q_b, QK_NOPE + QK_ROPE).astype(BF16),
            "w_kv_p": jnp.concatenate(_split_kv(ab_w_kv_b), axis=1).astype(BF16)}


def _ffn_fwd(h, l, p, S, rides):
    hn = _rms_fwd(h, p["ffn_norm"][l], name=f"ffn{l}_norm")
    g = _mm(hn, p["ffn_gate_t"][l], tb=True, out_dtype=BF16, name=f"ffn{l}_gate", ride=rides.get(f"ffn{l}_gate"))
    u = _mm(hn, p["ffn_up_t"][l], tb=True, out_dtype=BF16, name=f"ffn{l}_up", ride=rides.get(f"ffn{l}_up"))
    out, act = _ffn_act_down(g, u, p["ffn_conv_w"][l], p["ffn_conv_b"][l][None], p["ffn_down"][l], h, S=S,
                             name=f"ffn{l}_down", ride=rides.get(f"ffn{l}_down"))
    return out, (hn, g, u, act)


def _ffn_bwd(dh, h_in, l, p, saved, S, rides, grads_ready, also_ready=None):
    hn, g, u, act = saved
    dw_down = _mm(act, dh, ta=True, out_dtype=BF16, name=f"ffn{l}_dwdown")
    dg, du, dcw, dcb = _ffn_act_bwd(g, u, dh, p["ffn_down"][l], p["ffn_conv_w"][l], p["ffn_conv_b"][l][None], S=S,
                                    name=f"ffn{l}_dactbwd", ride=rides.get(f"ffn{l}_dactbwd"))
    dhn = _mm(dg, p["ffn_gate_t"][l], also=(du, p["ffn_up_t"][l]), out_dtype=BF16, name=f"ffn{l}_dhn")
    dw_gate_t = _mm(dg, hn, ta=True, out_dtype=BF16, name=f"ffn{l}_dwgate")
    dw_up_t = _mm(du, hn, ta=True, out_dtype=BF16, name=f"ffn{l}_dwup")
    grads_ready(l, {**(also_ready or {}), "ffn_gate_t": dw_gate_t, "ffn_up_t": dw_up_t, "ffn_down": dw_down})
    dh_in, dnorm = _rms_bwd(h_in, p["ffn_norm"][l], dhn, res=dh, name=f"ffn{l}_dnorm", ride=rides.get(f"ffn{l}_dnorm"))
    grads = dict(ffn_norm=dnorm[0], ffn_gate_t=dw_gate_t, ffn_up_t=dw_up_t, ffn_conv_w=dcw[:FFN_CONV],
                 ffn_conv_b=dcb[0], ffn_down=dw_down)
    return dh_in, grads


def _local_step(x, positions, target, p, rides=None, grads_ready=None):
    rides = {} if rides is None else rides
    grads_ready = grads_ready or (lambda layer, ready: None)
    B, S, D = x.shape
    T = B * S
    H = MLA_HEADS
    xf = x.reshape(T, D)
    tgt = target.reshape(T, D)
    cos, sin = _rope_tables(positions)

    hn0 = _rms_fwd(xf, p["ab_norm"][0], name="ab_norm", ride=rides.get("ab_norm"))
    z = _mm(hn0, p["w_in_p"], name="ab_in")
    cqn = _rms_fwd(z, p["ab_q_norm"][0], cb=Z_Q_BLOCK, name="q_norm")
    ckvn = _rms_fwd(z, p["ab_kv_norm"][0], cb=Z_KV_BLOCK, name="kv_norm")
    q = _mm(cqn, p["w_q_p"], name="q_up")
    kv = _mm(ckvn, p["w_kv_p"], out_dtype=BF16, name="kv_up")
    qs = _rope_q(q, cos, sin, name="q_rope")
    kk = _key_blocks(kv, z, cos, sin, kpe_block=Z_KPE_BLOCK, name="k_rope")
    att = dict(B=B, S=S, v_block0=H)
    o, lse = _attn_fwd(qs, kk, kv, name="attn_fwd", ride=rides.get("attn_fwd"), **att)
    lru_par = (p["ab_conv_w"][0], p["ab_conv_b"], p["wa_d"], p["ab_b_rg_a"], p["wx_d"], p["ab_b_rg_x"], p["ab_lambda"])
    y_lru, hs = _lru_fwd(z, *lru_par, S=S, name="lru_fwd", ride=rides.get("lru_fwd"))
    n_att = H * V_HEAD
    w_out_a, w_out_b = p["ab_w_out"][:n_att], p["ab_w_out"][n_att:]
    h1 = _mm(o, w_out_a, also=(y_lru, w_out_b), res=xf, name="ab_out")
    h2, ffn0 = _ffn_fwd(h1, 0, p, S, rides)

    hn2 = _rms_fwd(h2, p["c_norm"][0], name="c_norm")
    zc = _mm(hn2, p["c_w_in_t"], tb=True, name="c_in")
    pg = _sgu_fwd(zc, p["c_ln_g"], p["c_ln_b"], p["wm"], p["bmap"], name="sgu_fwd")
    h3 = _mm(pg, p["c_w_out"], res=h2, name="c_out")
    h4, ffn1 = _ffn_fwd(h3, 1, p, S, rides)

    loss_row, dh4, dfinal = _final_fwd_bwd(h4, p["final_norm"], tgt, name="final")

    dh3, g_ffn1 = _ffn_bwd(dh4, h3, 1, p, ffn1, S, rides, grads_ready)
    dpg = _mm(dh3, p["c_w_out"], tb=True, out_dtype=BF16, name="c_dp")
    dw_c_out = _mm(pg, dh3, ta=True, out_dtype=BF16, name="c_dwout")
    dzc, dlng, dlnb, dwm, dbm = _sgu_bwd(zc, dpg, p["c_ln_g"], p["c_ln_b"], p["wm"], p["bmap"], name="sgu_bwd",
                                         ride=rides.get("sgu_bwd"))
    dhn2 = _mm(dzc, p["c_w_in_t"], out_dtype=BF16, name="c_dhn")
    dw_c_in_t = _mm(dzc, hn2, ta=True, out_dtype=BF16, name="c_dwin")
    dh2, dcnorm = _rms_bwd(h2, p["c_norm"][0], dhn2, res=dh3, name="c_dnorm")
    dh1, g_ffn0 = _ffn_bwd(dh2, h1, 0, p, ffn0, S, rides, grads_ready, {"c_w_in_t": dw_c_in_t, "c_w_out": dw_c_out})

    do = _mm(dh1, w_out_a, tb=True, name="ab_do")
    dy_lru = _mm(dh1, w_out_b, tb=True, out_dtype=BF16, name="ab_dylru")
    dw_out = jnp.concatenate([_mm(o, dh1, ta=True, out_dtype=BF16, name="ab_dwout_a"),
                              _mm(y_lru, dh1, ta=True, out_dtype=BF16, name="ab_dwout_b")], axis=0)
    dq, dk, dv = _attn_bwd(qs, kk, kv, o, lse, do, name="attn_bwd", ride=rides.get("attn_bwd"), **att)
    dq_full = _rope_q_bwd(dq, cos, sin, name="q_rope_bwd")
    dkr = _key_rope_bwd(dk, cos, sin, name="k_rope_bwd")
    n_key = H * LANES
    w_k_p, w_v_p = p["w_kv_p"][:, :n_key], p["w_kv_p"][:, n_key:]
    dcqn = _mm(dq_full, p["w_q_p"], tb=True, name="q_dlat")
    dw_q_p = _mm(cqn, dq_full, ta=True, out_dtype=BF16, name="q_dw")
    dckvn = _mm(dv, w_v_p, tb=True, res=_mm(dk, w_k_p, tb=True, name="k_dlat"), name="v_dlat")
    dw_k_p = _mm(ckvn, dk, ta=True, out_dtype=BF16, name="k_dw")
    dw_v_p = _mm(ckvn, dv, ta=True, out_dtype=BF16, name="v_dw")
    dcq, dqnorm = _rms_bwd(z, p["ab_q_norm"][0], dcqn, cb=Z_Q_BLOCK, out_dtype=BF16, name="q_dnorm")
    dckv, dkvnorm = _rms_bwd(z, p["ab_kv_norm"][0], dckvn, cb=Z_KV_BLOCK, out_dtype=BF16, name="kv_dnorm")
    dxl, dgate, dcw, dcb, dwa, dba, dwx, dbx, dlam = _lru_bwd(z, hs, dy_lru, *lru_par, S=S, name="lru_bwd")
    dz = jnp.concatenate([dxl, dgate, dcq, dckv, dkr], axis=1)
    dhn0 = _mm(dz, p["w_in_p"], tb=True, out_dtype=BF16, name="ab_dhn")
    dw_in_p = _mm(hn0, dz, ta=True, out_dtype=BF16, name="ab_dwin")
    dx, dabnorm = _rms_bwd(xf, p["ab_norm"][0], dhn0, res=dh1, name="ab_dnorm")

    blocks = lambda dd: jnp.stack([dd[i * LRU_BLOCK:(i + 1) * LRU_BLOCK, i * LRU_BLOCK:(i + 1) * LRU_BLOCK]
                                   for i in range(LRU_HEADS)])
    causal = jnp.tril(jnp.ones((CHUNK, CHUNK), F32))
    grads = {
        "ab_norm": dabnorm, "w_in_p": dw_in_p, "ab_q_norm": dqnorm, "w_q_p": dw_q_p,
        "ab_kv_norm": dkvnorm, "w_k_p": dw_k_p, "w_v_p": dw_v_p, "ab_conv_w": dcw[:LRU_CONV][None], "ab_conv_b": dcb,
        "ab_w_rg_a": blocks(dwa)[None], "ab_b_rg_a": dba, "ab_w_rg_x": blocks(dwx)[None], "ab_b_rg_x": dbx,
        "ab_lambda": dlam, "ab_w_out": dw_out,
        "c_norm": dcnorm, "c_w_in_t": dw_c_in_t, "c_ln_g": dlng, "c_ln_b": dlnb,
        "c_w_s": (dwm * causal)[None], "c_b_s": dbm[:, ::SGU_GROUP_DIM].T[None], "c_w_out": dw_c_out,
        "final_norm": dfinal[0],
    }
    for name in ("ffn_norm", "ffn_conv_w", "ffn_conv_b"):
        grads[name] = jnp.stack([g_ffn0[name], g_ffn1[name]])
    for name in ("ffn_gate_t", "ffn_up_t", "ffn_down"):
        grads[name] = [g_ffn0[name], g_ffn1[name]]
    return loss_row, dx.reshape(B, S, D), grads


ANY = pl.BlockSpec(memory_space=pl.ANY)


def _place():
    x, y, c = lax.axis_index("x"), lax.axis_index("y"), lax.axis_index("c")
    chips = [(1 - x, y), (x, 1 - y), (1 - x, 1 - y)]
    return x, y, c, 2 * x + y, (x, y, 1 - c), chips


def _remote(src, dst, send_sems, recv_sems, k, to):
    return pltpu.make_async_remote_copy(src_ref=src, dst_ref=dst, send_sem=send_sems.at[k], recv_sem=recv_sems.at[k],
                                        device_id=to, device_id_type=MESH)


class _Exchange:
    def __init__(self, arrs, out_shapes, n_sems, start, finish):
        self.arrs, self.out_shapes, self.n_sems, self.start, self.finish = list(arrs), out_shapes, n_sems, start, finish

    @property
    def in_specs(self):
        return [ANY] * len(self.arrs)

    @property
    def out_specs(self):
        return [ANY] * len(self.out_shapes)

    @property
    def scratch(self):
        return [pltpu.SemaphoreType.DMA((self.n_sems,)), pltpu.SemaphoreType.DMA((self.n_sems,))]

    def split(self, refs):
        n = len(self.arrs)
        return refs[:n], refs[n:n + len(self.out_shapes)], refs[-2], refs[-1]

    def run(self, name):
        def body(*refs):
            parts = self.split(refs)
            self.start(*parts)
            self.finish(*parts)

        return pl.pallas_call(body, name=name, in_specs=self.in_specs, out_specs=self.out_specs,
                              out_shape=self.out_shapes, scratch_shapes=self.scratch)(*self.arrs)


def _put(buf, piece, idx, axis):
    return lax.dynamic_update_slice_in_dim(buf, jnp.expand_dims(piece, axis).astype(buf.dtype), idx, axis)


def _all_gather(arrs):
    n = len(arrs)
    per = 7

    def start(ins, outs, send_sems, recv_sems):
        x, y, c, j, sib, chips = _place()
        for i in range(n):
            for k, (cx, cy) in enumerate(chips):
                _remote(ins[i].at[:, c], outs[i].at[:, j, c], send_sems, recv_sems, per * i + k, (cx, cy, c)).start()
            _remote(ins[i], outs[i].at[:, j], send_sems, recv_sems, per * i + 6, sib).start()

    def finish(ins, outs, send_sems, recv_sems):
        x, y, c, j, sib, chips = _place()
        passed = []
        for i in range(n):
            for k, (cx, cy) in enumerate(chips):
                got = outs[i].at[:, 2 * cx + cy, c]
                _remote(got, got, send_sems, recv_sems, per * i + k, (cx, cy, c)).wait_recv()
                cp = _remote(got, got, send_sems, recv_sems, per * i + 3 + k, sib)
                cp.start()
                passed.append(cp)
        for i in range(n):
            for k, (cx, cy) in enumerate(chips):
                got = outs[i].at[:, 2 * cx + cy, 1 - c]
                _remote(got, got, send_sems, recv_sems, per * i + 3 + k, sib).wait_recv()
                _remote(ins[i].at[:, c], ins[i].at[:, c], send_sems, recv_sems, per * i + k, sib).wait_send()
            _remote(ins[i], outs[i].at[:, j], send_sems, recv_sems, per * i + 6, sib).wait()
        for cp in passed:
            cp.wait_send()

    shapes = [jax.ShapeDtypeStruct((a.shape[0], N_CHIPS) + a.shape[1:], a.dtype) for a in arrs]
    return _Exchange(arrs, shapes, per * n, start, finish)


class _Offset:
    def __init__(self, sems, k0):
        self.sems, self.k0 = sems, k0

    @property
    def at(self):
        return self

    def __getitem__(self, k):
        return self.sems.at[self.k0 + k]


def _merge(a, b):
    n_in, n_out = len(a.arrs), len(a.out_shapes)

    def both(fa, fb):
        def f(ins, outs, send_sems, recv_sems):
            fa(ins[:n_in], outs[:n_out], send_sems, recv_sems)
            fb(ins[n_in:], outs[n_out:], _Offset(send_sems, a.n_sems), _Offset(recv_sems, a.n_sems))
        return f

    return _Exchange(a.arrs + b.arrs, a.out_shapes + b.out_shapes, a.n_sems + b.n_sems,
                     both(a.start, b.start), both(a.finish, b.finish))


def _pair_swap(arrs):
    n = len(arrs)

    def start(ins, outs, send_sems, recv_sems):
        x, y, c, j, sib, chips = _place()
        for i in range(n):
            _remote(ins[i].at[:, 1 - c], outs[i], send_sems, recv_sems, i, sib).start()

    def finish(ins, outs, send_sems, recv_sems):
        x, y, c, j, sib, chips = _place()
        for i in range(n):
            _remote(ins[i].at[:, 1 - c], outs[i], send_sems, recv_sems, i, sib).wait()

    shapes = [jax.ShapeDtypeStruct((a.shape[0],) + a.shape[2:], a.dtype) for a in arrs]
    return _Exchange(arrs, shapes, n, start, finish)


def _pair_send(arrs):
    n = len(arrs)

    def start(ins, outs, send_sems, recv_sems):
        x, y, c, j, sib, chips = _place()
        for i in range(n):
            _remote(ins[i], outs[i], send_sems, recv_sems, i, sib).start()

    def finish(ins, outs, send_sems, recv_sems):
        x, y, c, j, sib, chips = _place()
        for i in range(n):
            _remote(ins[i], outs[i], send_sems, recv_sems, i, sib).wait()

    shapes = [jax.ShapeDtypeStruct(a.shape, a.dtype) for a in arrs]
    return _Exchange(arrs, shapes, n, start, finish)


def _chip_exchange(arrs, *, scatter):
    n = len(arrs)

    def copies(ins, outs, send_sems, recv_sems):
        x, y, c, j, sib, chips = _place()
        return [(_remote(ins[i].at[2 * cx + cy] if scatter else ins[i], outs[i].at[j], send_sems, recv_sems,
                         3 * i + k, (cx, cy, c)),
                 _remote(outs[i].at[2 * cx + cy], outs[i].at[2 * cx + cy], send_sems, recv_sems, 3 * i + k, (cx, cy, c)))
                for i in range(n) for k, (cx, cy) in enumerate(chips)]

    def start(*refs):
        for out, _ in copies(*refs):
            out.start()

    def finish(*refs):
        for out, back in copies(*refs):
            back.wait_recv()
            out.wait_send()

    shapes = [jax.ShapeDtypeStruct((N_CHIPS,) + a.shape[-2:], a.dtype) for a in arrs]
    return _Exchange(arrs, shapes, 3 * n, start, finish)


FLAT_ROWS = 512


def _pair_add(sharded, from_sib, *, name):
    n, _, R, L = sharded.shape
    tr = _tile(R, FLAT_ROWS, 16)

    def body(s_ref, b_ref, o_ref):
        own = jnp.where(lax.axis_index("c") == 0, s_ref[:, 0], s_ref[:, 1])
        o_ref[...] = (own.astype(F32) + b_ref[...].astype(F32)).astype(BF16)

    spec = pl.BlockSpec((n, tr, L), lambda i: (0, i, 0))
    return pl.pallas_call(
        body, name=name, grid=(R // tr,), in_specs=[pl.BlockSpec((n, 2, tr, L), lambda i: (0, 0, i, 0)), spec],
        out_specs=spec, out_shape=jax.ShapeDtypeStruct((n, R, L), BF16), compiler_params=_cparams(("parallel",)),
    )(sharded, from_sib)


def _chip_sum(arrived, pair, *, name):
    n, R, L = arrived.shape
    tr = _tile(R, FLAT_ROWS, 16)

    def body(a_ref, p_ref, o_ref):
        me = 2 * lax.axis_index("x") + lax.axis_index("y")
        acc = None
        for k in range(n):
            term = jnp.where(me == k, p_ref[k], a_ref[k]).astype(F32)
            acc = term if acc is None else acc + term
        o_ref[...] = acc

    spec = pl.BlockSpec((n, tr, L), lambda i: (0, i, 0))
    return pl.pallas_call(
        body, name=name, grid=(R // tr,), in_specs=[spec, spec], out_specs=pl.BlockSpec((tr, L), lambda i: (i, 0)),
        out_shape=jax.ShapeDtypeStruct((R, L), F32), compiler_params=_cparams(("parallel",)),
    )(arrived, pair)


def _sum_slots(buf, *, name):
    n, R, L = buf.shape
    tr = _tile(R, FLAT_ROWS, 16)

    def body(b_ref, o_ref):
        acc = b_ref[0].astype(F32)
        for k in range(1, n):
            acc = acc + b_ref[k].astype(F32)
        o_ref[...] = acc

    return pl.pallas_call(
        body, name=name, grid=(R // tr,), in_specs=[pl.BlockSpec((n, tr, L), lambda i: (0, i, 0))],
        out_specs=pl.BlockSpec((tr, L), lambda i: (i, 0)),
        out_shape=jax.ShapeDtypeStruct((R, L), F32), compiler_params=_cparams(("parallel",)),
    )(buf)


def _adamw_update(w, g, m, v):
    c1 = 1.0 - ADAM_B1 ** ADAM_STEP
    c2 = 1.0 - ADAM_B2 ** ADAM_STEP
    m = ADAM_B1 * m + (1.0 - ADAM_B1) * g
    v = ADAM_B2 * v + (1.0 - ADAM_B2) * (g * g)
    return -ADAM_LR * ((m / c1) / (jnp.sqrt(v / c2) + ADAM_EPS) + ADAM_WD * w), m, v


def _adamw_halves(w, m, v, own, other, *, name):
    NL, R, L = w.shape
    h = R // 2
    tr = _tile(h, FLAT_ROWS, 16)
    nt = h // tr

    def body(*refs):
        w_ref, m_ref, v_ref = refs[:3]
        own_refs, other_refs = refs[3:3 + NL], refs[3 + NL:3 + 2 * NL]
        d_ref, nm_ref, nv_ref, g_ref = refs[3 + 2 * NL:]
        layer, half = pl.program_id(0), pl.program_id(1)
        mine = half == lax.axis_index("c")
        g = jnp.where(mine, own_refs[0][...], other_refs[0][...])
        for l in range(1, NL):
            g = jnp.where(layer == l, jnp.where(mine, own_refs[l][...], other_refs[l][...]), g)
        d, mm, vv = _adamw_update(w_ref[0], g, m_ref[0], v_ref[0])
        d_ref[0], nm_ref[0], nv_ref[0], g_ref[0] = d, mm, vv, g

    spec = pl.BlockSpec((1, tr, L), lambda l, hh, i: (l, hh * nt + i, 0))
    part = pl.BlockSpec((tr, L), lambda l, hh, i: (i, 0))
    sh = jax.ShapeDtypeStruct((NL, R, L), F32)
    return pl.pallas_call(
        body, name=name, grid=(NL, 2, nt), in_specs=[spec] * 3 + [part] * (2 * NL), out_specs=[spec] * 4,
        out_shape=[sh] * 4, compiler_params=_cparams(("parallel", "parallel", "parallel")),
    )(w, m, v, *own, *other)


def _adamw(w, g, m, v, *, name):
    NL, R, L = w.shape
    tr = _tile(R, FLAT_ROWS, 16)

    def body(w_ref, g_ref, m_ref, v_ref, d_ref, nm_ref, nv_ref):
        d_ref[...], nm_ref[...], nv_ref[...] = _adamw_update(w_ref[...], g_ref[...], m_ref[...], v_ref[...])

    spec = pl.BlockSpec((1, tr, L), lambda l, i: (l, i, 0))
    sh = jax.ShapeDtypeStruct((NL, R, L), F32)
    return pl.pallas_call(
        body, name=name, grid=(NL, R // tr), in_specs=[spec] * 4, out_specs=[spec] * 3, out_shape=[sh] * 3,
        compiler_params=_cparams(("parallel", "parallel")),
    )(w, g, m, v)


WEIGHT_NAMES = ["ab_norm", "ab_w_in", "ab_q_norm", "ab_w_q_b", "ab_kv_norm", "ab_w_kv_b", "ab_conv_w", "ab_conv_b",
                "ab_w_rg_a", "ab_b_rg_a", "ab_w_rg_x", "ab_b_rg_x", "ab_lambda", "ab_w_out", "c_norm", "c_w_in",
                "c_ln_g", "c_ln_b", "c_w_s", "c_b_s", "c_w_out", "ffn_norm", "ffn_w_gate", "ffn_w_up", "ffn_conv_w",
                "ffn_conv_b", "ffn_w_down", "final_norm"]
BIG = {"ab_w_in": 2, "ab_w_q_b": 2, "ab_w_kv_b": 2, "ab_w_out": 1, "c_w_in": 2, "c_w_out": 1,
       "ffn_w_gate": 2, "ffn_w_up": 2, "ffn_w_down": 1}
SMALL_SHARDED = {"ab_conv_w": 2, "c_norm": 1, "c_ln_g": 1, "c_ln_b": 1, "ffn_conv_w": 2}
SMALL_REPLICATED = [n for n in WEIGHT_NAMES if n not in BIG and n not in SMALL_SHARDED]


def _rows(n_elems, mult):
    r = -(-n_elems // LANES)
    return -(-r // mult) * mult


def _flat(parts, rows):
    flat = jnp.concatenate([a.reshape(-1) for a in parts])
    return jnp.pad(flat, (0, rows * LANES - flat.shape[0])).reshape(rows, LANES)


def _unflat(flat, shapes):
    flat = flat.reshape(-1)
    out, off = [], 0
    for s in shapes:
        n = math.prod(s)
        out.append(flat[off:off + n].reshape(s))
        off += n
    return out


def _join_shards(a, axis):
    a = jnp.moveaxis(a, 0, axis)
    return a.reshape(a.shape[:axis] + (a.shape[axis] * a.shape[axis + 1],) + a.shape[axis + 2:])


def kernel(x, positions, ab_norm, ab_w_in, ab_q_norm, ab_w_q_b, ab_kv_norm, ab_w_kv_b, ab_conv_w, ab_conv_b, ab_w_rg_a, ab_b_rg_a, ab_w_rg_x, ab_b_rg_x, ab_lambda, ab_w_out, c_norm, c_w_in, c_ln_g, c_ln_b, c_w_s, c_b_s, c_w_out, ffn_norm, ffn_w_gate, ffn_w_up, ffn_conv_w, ffn_conv_b, ffn_w_down, final_norm, loss_target, m_ab_norm, m_ab_w_in, m_ab_q_norm, m_ab_w_q_b, m_ab_kv_norm, m_ab_w_kv_b, m_ab_conv_w, m_ab_conv_b, m_ab_w_rg_a, m_ab_b_rg_a, m_ab_w_rg_x, m_ab_b_rg_x, m_ab_lambda, m_ab_w_out, m_c_norm, m_c_w_in, m_c_ln_g, m_c_ln_b, m_c_w_s, m_c_b_s, m_c_w_out, m_ffn_norm, m_ffn_w_gate, m_ffn_w_up, m_ffn_conv_w, m_ffn_conv_b, m_ffn_w_down, m_final_norm, v_ab_norm, v_ab_w_in, v_ab_q_norm, v_ab_w_q_b, v_ab_kv_norm, v_ab_w_kv_b, v_ab_conv_w, v_ab_conv_b, v_ab_w_rg_a, v_ab_b_rg_a, v_ab_w_rg_x, v_ab_b_rg_x, v_ab_lambda, v_ab_w_out, v_c_norm, v_c_w_in, v_c_ln_g, v_c_ln_b, v_c_w_s, v_c_b_s, v_c_w_out, v_ffn_norm, v_ffn_w_gate, v_ffn_w_up, v_ffn_conv_w, v_ffn_conv_b, v_ffn_w_down, v_final_norm):
    given = dict(locals())
    w = {n: given[n] for n in WEIGHT_NAMES}
    m = {n: given["m_" + n] for n in WEIGHT_NAMES}
    v = {n: given["v_" + n] for n in WEIGHT_NAMES}
    c = lax.axis_index("c")
    chip = 2 * lax.axis_index("x") + lax.axis_index("y")

    halves = lambda a: a.reshape(a.shape[0], 2, a.shape[1] // 2, a.shape[2])
    tr = lambda a: jnp.swapaxes(a, 1, 2)
    send = {"ab_w_in": w["ab_w_in"], "ab_w_q_b": w["ab_w_q_b"], "ab_w_kv_b": w["ab_w_kv_b"], "ab_w_out": w["ab_w_out"],
            "c_w_in": tr(w["c_w_in"]), "c_w_out": w["c_w_out"], "ffn_w_gate": tr(w["ffn_w_gate"]),
            "ffn_w_up": tr(w["ffn_w_up"]), "ffn_w_down": w["ffn_w_down"]}
    small_rows = _rows(sum(w[n].size for n in SMALL_SHARDED), 16)
    small_sh = _flat([w[n] for n in SMALL_SHARDED], small_rows).reshape(1, 2, small_rows // 2, LANES)
    first_names = ["ab_w_in", "ab_w_q_b", "ab_w_kv_b", "ab_w_out"]
    mine = {n: halves(send[n].astype(BF16)) for n in BIG}

    def put_own(own, arrived):
        return arrived.reshape(arrived.shape[0], -1, arrived.shape[-1])

    p = {"ab_norm": w["ab_norm"], "ffn_gate_t": {}, "ffn_up_t": {}, "ffn_down": {}}
    first = [mine[n] for n in first_names] + [small_sh]

    def first_arrived(got):
        full = {n: put_own(o, a) for n, o, a in zip(first_names + ["small"], first, got)}
        unshard = lambda a: jnp.swapaxes(a.reshape(N_CHIPS, -1, a.shape[-1]), 0, 1).reshape(-1, N_CHIPS * a.shape[-1])
        p.update(_prep_big(unshard(full["ab_w_in"][0]), unshard(full["ab_w_q_b"][0]), unshard(full["ab_w_kv_b"][0])))
        p["ab_w_out"] = full["ab_w_out"][0]
        small_full = dict(w)
        off = 0
        small_got = full["small"].reshape(N_CHIPS, -1)
        for n, ax in SMALL_SHARDED.items():
            seg = small_got[:, off:off + w[n].size].reshape((N_CHIPS,) + w[n].shape)
            small_full[n] = _join_shards(seg, ax)
            off += w[n].size
        p.update(_prep_small(small_full))

    def weights_ride(parts):
        def sink(arrived):
            for (own, setter), a in zip(parts, arrived):
                setter(put_own(own, a)[0])
        return _all_gather([own for own, _ in parts]), sink

    ffn_keys = {"ffn_gate_t": "ffn_w_gate", "ffn_up_t": "ffn_w_up", "ffn_down": "ffn_w_down"}
    ffn_part = lambda key, l: (mine[ffn_keys[key]][l:l + 1], functools.partial(p[key].__setitem__, l))
    rides = {
        "ab_norm": (_all_gather(first), first_arrived),
        "attn_fwd": weights_ride([ffn_part(key, 0) for key in ffn_keys]),
        "ffn0_gate": weights_ride([ffn_part("ffn_gate_t", 1)]),
        "ffn0_up": weights_ride([ffn_part("ffn_up_t", 1)]),
        "lru_fwd": weights_ride([(mine["c_w_in"], functools.partial(p.__setitem__, "c_w_in_t")),
                                 (mine["c_w_out"], functools.partial(p.__setitem__, "c_w_out"))]),
        "ffn0_down": weights_ride([ffn_part("ffn_down", 1)]),
    }

    def chip_sums(pair, arrived, tag):
        return [_chip_sum(a, b, name=f"grad_chip_sum_{tag}{i}") for i, (a, b) in enumerate(zip(arrived, pair))]

    half_of = {}

    def grads_ready(layer, ready):
        if layer == 1:
            named = {"gate1": ready["ffn_gate_t"], "up1": ready["ffn_up_t"], "down1": ready["ffn_down"]}
            hosts = {"sgu_bwd": ["down1"], "ffn0_dactbwd": ["gate1", "up1"]}
        else:
            named = {"c_in": ready["c_w_in_t"], "c_out": ready["c_w_out"], "gate0": ready["ffn_gate_t"],
                     "up0": ready["ffn_up_t"], "down0": ready["ffn_down"]}
            hosts = {"attn_bwd": ["c_in", "c_out", "down0", "gate0", "up0"]}
        tag = f"f{layer}"
        sharded = [a.reshape(N_CHIPS, 2, -1, a.shape[-1]) for a in named.values()]

        def paired(from_sib):
            pair = {k: _pair_add(a, b, name=f"grad_pair_add_{tag}{i}")
                    for i, (k, a, b) in enumerate(zip(named, sharded, from_sib))}
            for kernel_name, keys in hosts.items():
                def sink(arrived, keys=keys, kernel_name=kernel_name):
                    half_of.update(zip(keys, chip_sums([pair[k] for k in keys], arrived, f"{tag}_{kernel_name}")))
                rides[kernel_name] = (_chip_exchange([pair[k] for k in keys], scatter=True), sink)

        rides[f"ffn{layer}_dnorm"] = (_pair_swap(sharded), paired)

    loss_row, grad_x, g = _local_step(x, positions, loss_target, p, rides, grads_ready)

    cols = lambda a, n: jnp.swapaxes(a.reshape(a.shape[0], N_CHIPS, n), 0, 1)
    n_in, n_q, n_kv = w["ab_w_in"].shape[2], w["ab_w_q_b"].shape[2], w["ab_w_kv_b"].shape[2]
    small_names = SMALL_REPLICATED + list(SMALL_SHARDED)
    rs = _rows(sum(g[n].size for n in small_names) + LANES, FLAT_ROWS)
    small = _flat([loss_row] + [g[n] for n in small_names], rs)
    slot = (jnp.arange(2) == c)[:, None, None]
    last = [cols(_unperm_w_in(g["w_in_p"]), n_in), cols(_from_head_blocks(g["w_q_p"], QK_NOPE + QK_ROPE), n_q),
            cols(_join_kv(g["w_k_p"], g["w_v_p"]), n_kv), g["ab_w_out"]]
    last = [a.reshape(N_CHIPS, 2, -1, a.shape[-1]) for a in last]
    *from_sib, small_sib = _merge(_pair_swap(last), _pair_send([small])).run("tail_pair")
    pair = [_pair_add(a, b, name=f"grad_pair_add_b{i}") for i, (a, b) in enumerate(zip(last, from_sib))]
    pair_small = _sum_slots(jnp.where(slot, small[None], small_sib[None]), name="small_pair_sum")
    my_small = lax.dynamic_index_in_dim(pair_small.reshape(2, rs // 2, LANES), c, axis=0, keepdims=False)
    *arrived, all_small = _merge(_chip_exchange(pair, scatter=True), _chip_exchange([my_small], scatter=False)).run("tail_chip")
    half_of.update(zip(["in", "q", "kv", "out"], chip_sums(pair, arrived, "b")))
    half_of["small"] = _sum_slots(_put(all_small, my_small, chip, 0), name="small_chip_sum")
    keys = ("in", "q", "kv", "out", "c_in", "c_out", "gate0", "gate1", "up0", "up1", "down0", "down1", "small")
    other_half = dict(zip(keys, _pair_send([half_of[k] for k in keys]).run("grad_pair_share")))
    small_sum = jnp.where(slot, half_of["small"][None], other_half["small"][None]).reshape(rs, LANES)
    whole = lambda k: jnp.where(slot, half_of[k][None], other_half[k][None]).reshape(-1, half_of[k].shape[-1])
    grads_t = {"ab_w_in": whole("in").T[None], "ab_w_q_b": whole("q").T[None]}
    grads = {"ab_w_kv_b": whole("kv")[None], "c_w_in": whole("c_in").T[None], **{n: tr(a) for n, a in grads_t.items()}}
    by_halves = {"ab_w_out": (("out",), False), "c_w_out": (("c_out",), False), "ffn_w_down": (("down0", "down1"), False),
                 "ffn_w_gate": (("gate0", "gate1"), True), "ffn_w_up": (("up0", "up1"), True)}

    small_parts = _unflat(small_sum, [(1, LANES)] + [g[n].shape for n in small_names])
    loss = small_parts[0][0, 0]
    for n, a in zip(small_names, small_parts[1:]):
        if n in SMALL_SHARDED:
            ax = SMALL_SHARDED[n]
            a = lax.dynamic_slice_in_dim(a, chip * w[n].shape[ax], w[n].shape[ax], axis=ax)
        grads[n] = a.reshape(w[n].shape)

    delta, new_m, new_v = {}, {}, {}
    for n in BIG:
        if n in by_halves:
            ks, transposed = by_halves[n]
            view = tr if transposed else (lambda a: a)
            out = _adamw_halves(view(w[n]), view(m[n]), view(v[n]), [half_of[k] for k in ks], [other_half[k] for k in ks],
                                name=f"adamw_{n}")
            delta[n], new_m[n], new_v[n], grads[n] = (view(a) for a in out)
        elif n in grads_t:
            out = _adamw(tr(w[n]), grads_t[n], tr(m[n]), tr(v[n]), name=f"adamw_{n}")
            delta[n], new_m[n], new_v[n] = (tr(a) for a in out)
        else:
            delta[n], new_m[n], new_v[n] = _adamw(w[n], grads[n], m[n], v[n], name=f"adamw_{n}")
    small_all = [n for n in WEIGHT_NAMES if n not in BIG]
    ra = _rows(sum(w[n].size for n in small_all), FLAT_ROWS)
    pack = lambda d: _flat([d[n] for n in small_all], ra)[None]
    out = _adamw(pack(w), pack(grads), pack(m), pack(v), name="adamw_small")
    shapes = [w[n].shape for n in small_all]
    for d, flat in zip((delta, new_m, new_v), out):
        d.update(zip(small_all, _unflat(flat, shapes)))
    return (loss, grad_x, *[grads[n] for n in WEIGHT_NAMES], *[delta[n] for n in WEIGHT_NAMES],
            *[new_m[n] for n in WEIGHT_NAMES], *[new_v[n] for n in WEIGHT_NAMES])
```

```python
import functools
import math

import jax
import jax.numpy as jnp
from jax import lax
from jax.experimental import pallas as pl
from jax.experimental.pallas import tpu as pltpu

F32 = jnp.float32
BF16 = jnp.bfloat16
MESH = pl.DeviceIdType.MESH

D_MODEL = 1024
MLA_HEADS = 8
Q_LORA = 256
KV_LORA = 128
QK_NOPE = 64
QK_ROPE = 32
V_HEAD = 64
LRU_WIDTH = 512
LRU_HEADS = 8
LRU_BLOCK = 64
LRU_CONV = 4
LRU_C = 8.0
CHUNK = 128
SGU_GROUPS = 8
SGU_WIDTH = 1024
D_FF = 2816
FFN_CONV = 3
NORM_EPS = 1e-6
ROPE_BASE = 10000.0
ADAM_LR = 0.001
ADAM_B1 = 0.9
ADAM_B2 = 0.999
ADAM_EPS = 1e-08
ADAM_WD = 0.01
ADAM_STEP = 10

N_CHIPS = 4
LANES = 128
VMEM_LIMIT = 56 * 1024 * 1024
ROW_TILE = 256
SGU_TILE = 512
NORM_TILE = 1024
MM_TM, MM_TN, MM_TK = 1024, 1536, 2816
MM_TM_T, MM_TK_T = 1408, 2048
GELU_C = math.sqrt(2.0 / math.pi)


def _cparams(sem):
    return pltpu.CompilerParams(dimension_semantics=sem, vmem_limit_bytes=VMEM_LIMIT)


def _tile(n, target, mult=LANES):
    t = (min(n, target) // mult) * mult
    while t >= mult:
        if n % t == 0:
            return t
        t -= mult
    return n


GELU_K = GELU_C * 0.044715


def _gelu(x):
    t = jnp.tanh(x * (GELU_C + GELU_K * (x * x)))
    hx = 0.5 * x
    return hx + hx * t


def _gelu_and_grad(x):
    x2 = x * x
    t = jnp.tanh(x * (GELU_C + GELU_K * x2))
    hx = 0.5 * x
    dg = (0.5 + 0.5 * t) + (hx * (1.0 - t * t)) * (GELU_C + (3.0 * GELU_K) * x2)
    return hx + hx * t, dg


def _sigmoid(x):
    return 1.0 / (1.0 + jnp.exp(-x))


def _shift_rows(x, d, fill_rows):
    ext = jnp.concatenate([fill_rows, x], axis=0)
    return pltpu.roll(ext, d, 0)[8:]


def _shift_rows_up(x, d, fill_rows):
    n = x.shape[0]
    ext = jnp.concatenate([x, fill_rows], axis=0)
    return pltpu.roll(ext, n + 8 - d, 0)[:n]


def _dot(a, b, dims):
    return lax.dot_general(a.astype(BF16), b.astype(BF16), (dims, ((), ())), preferred_element_type=F32)


def _dot_nn(a, b):
    return _dot(a, b, ((1,), (0,)))


def _dot_nt(a, b):
    return _dot(a, b, ((1,), (1,)))


def _dot_tn(a, b):
    return _dot(a, b, ((0,), (0,)))


def _mm(a, b, *, name, ta=False, tb=False, res=None, out_dtype=F32, ride=None, also=None):
    if ta:
        K, M = a.shape
    else:
        M, K = a.shape
    N = b.shape[0] if tb else b.shape[1]
    tm = _tile(M, MM_TM_T if ta else (MM_TM if K <= MM_TM else MM_TM // 2), LANES if ta else 8)
    tn = _tile(N, MM_TN, LANES)
    tk = _tile(K, MM_TK_T if ta else MM_TK, LANES)
    nk = K // tk
    a_spec = pl.BlockSpec((tk, tm), lambda j, i, k: (k, i)) if ta else pl.BlockSpec((tm, tk), lambda j, i, k: (i, k))
    b_spec = pl.BlockSpec((tn, tk), lambda j, i, k: (j, k)) if tb else pl.BlockSpec((tk, tn), lambda j, i, k: (k, j))
    o_spec = pl.BlockSpec((tm, tn), lambda j, i, k: (i, j))
    dims = ((0,) if ta else (1,), (1,) if tb else (0,))
    has_res = res is not None
    pairs = [(a, b)] + ([also] if also is not None else [])
    n_ab = 2 * len(pairs)

    def body(*refs):
        r_ref = refs[n_ab] if has_res else None
        o_ref = refs[n_ab + 1] if has_res else refs[n_ab]
        p = _dot(refs[0][...], refs[1][...], dims)
        if also is not None:
            p = p + _dot(refs[2][...], refs[3][...], dims)

        def finish(r):
            if has_res:
                r = r + r_ref[...].astype(F32)
            o_ref[...] = r.astype(out_dtype)

        if nk == 1:
            finish(p)
            return
        acc_ref = refs[-1]
        k = pl.program_id(2)

        @pl.when(k == 0)
        def _():
            acc_ref[...] = p

        @pl.when(jnp.logical_and(k > 0, k < nk - 1))
        def _():
            acc_ref[...] += p

        @pl.when(k == nk - 1)
        def _():
            finish(acc_ref[...] + p)

    in_specs = [a_spec, b_spec] * len(pairs) + ([o_spec] if has_res else [])
    args = tuple(x for pair in pairs for x in pair) + ((res,) if has_res else ())
    return _pcall(
        body, name=name, grid=(N // tn, M // tm, nk), in_specs=in_specs, out_specs=[o_spec],
        out_shape=[jax.ShapeDtypeStruct((M, N), out_dtype)], args=args,
        scratch=[pltpu.VMEM((tm, tn), F32)] if nk > 1 else [], sem=("parallel", "parallel", "arbitrary"), ride=ride)[0]


def _rms_fwd(x, g, *, name, cb=0, out_dtype=BF16, ride=None):
    T = x.shape[0]
    W = g.shape[-1]
    g = g.reshape(1, W)
    tt = _tile(T, NORM_TILE, 16)

    def body(x_ref, g_ref, o_ref):
        xf = x_ref[...].astype(F32)
        rstd = lax.rsqrt(jnp.mean(xf * xf, axis=-1, keepdims=True) + NORM_EPS)
        o_ref[...] = (xf * rstd * g_ref[...]).astype(out_dtype)

    return _pcall(
        body, name=name, grid=(T // tt,),
        in_specs=[pl.BlockSpec((tt, W), lambda i: (i, cb)), pl.BlockSpec((1, W), lambda i: (0, 0))],
        out_specs=[pl.BlockSpec((tt, W), lambda i: (i, 0))], out_shape=[jax.ShapeDtypeStruct((T, W), out_dtype)],
        args=(x, g), sem=("parallel",), ride=ride)[0]


def _rms_bwd(x, g, dy, *, name, cb=0, res=None, out_dtype=F32, ride=None):
    T = x.shape[0]
    W = g.shape[-1]
    g = g.reshape(1, W)
    tt = _tile(T, NORM_TILE // 2, 16)
    has_res = res is not None

    def body(*refs):
        if has_res:
            x_ref, g_ref, dy_ref, r_ref, dx_ref, dg_ref = refs
        else:
            x_ref, g_ref, dy_ref, dx_ref, dg_ref = refs
        xf = x_ref[...].astype(F32)
        dyf = dy_ref[...].astype(F32)
        rstd = lax.rsqrt(jnp.mean(xf * xf, axis=-1, keepdims=True) + NORM_EPS)
        xhat = xf * rstd
        dxhat = dyf * g_ref[...]
        dx = rstd * (dxhat - xhat * jnp.mean(dxhat * xhat, axis=-1, keepdims=True))
        if has_res:
            dx = dx + r_ref[...].astype(F32)
        dx_ref[...] = dx.astype(out_dtype)
        part = jnp.sum(dyf * xhat, axis=0, keepdims=True)

        @pl.when(pl.program_id(0) == 0)
        def _():
            dg_ref[...] = part

        @pl.when(pl.program_id(0) > 0)
        def _():
            dg_ref[...] += part

    row = pl.BlockSpec((tt, W), lambda i: (i, 0))
    in_specs = [pl.BlockSpec((tt, W), lambda i: (i, cb)), pl.BlockSpec((1, W), lambda i: (0, 0)), row]
    args = (x, g, dy)
    if has_res:
        in_specs.append(row)
        args = args + (res,)
    return _pcall(
        body, name=name, grid=(T // tt,), in_specs=in_specs,
        out_specs=[row, pl.BlockSpec((1, W), lambda i: (0, 0))],
        out_shape=[jax.ShapeDtypeStruct((T, W), out_dtype), jax.ShapeDtypeStruct((1, W), F32)], args=args, ride=ride)


def _final_fwd_bwd(h, g, target, *, name):
    T, W = h.shape
    g = g.reshape(1, W)
    tt = _tile(T, NORM_TILE, 16)

    def body(x_ref, g_ref, t_ref, loss_ref, dx_ref, dg_ref):
        xf = x_ref[...]
        rstd = lax.rsqrt(jnp.mean(xf * xf, axis=-1, keepdims=True) + NORM_EPS)
        xhat = xf * rstd
        err = xhat * g_ref[...] - t_ref[...]
        lpart = jnp.zeros((1, LANES), F32) + (0.5 / W) * jnp.sum(err * err)
        dyf = err * (1.0 / W)
        dxhat = dyf * g_ref[...]
        dx_ref[...] = rstd * (dxhat - xhat * jnp.mean(dxhat * xhat, axis=-1, keepdims=True))
        part = jnp.sum(dyf * xhat, axis=0, keepdims=True)

        @pl.when(pl.program_id(0) == 0)
        def _():
            dg_ref[...] = part
            loss_ref[...] = lpart

        @pl.when(pl.program_id(0) > 0)
        def _():
            dg_ref[...] += part
            loss_ref[...] += lpart

    row = pl.BlockSpec((tt, W), lambda i: (i, 0))
    return pl.pallas_call(
        body, name=name, grid=(T // tt,),
        in_specs=[row, pl.BlockSpec((1, W), lambda i: (0, 0)), row],
        out_specs=[pl.BlockSpec((1, LANES), lambda i: (0, 0)), row, pl.BlockSpec((1, W), lambda i: (0, 0))],
        out_shape=[jax.ShapeDtypeStruct((1, LANES), F32), jax.ShapeDtypeStruct((T, W), F32),
                   jax.ShapeDtypeStruct((1, W), F32)],
        compiler_params=_cparams(("arbitrary",)),
    )(h, g, target)


def _swap16(x):
    lane = lax.broadcasted_iota(jnp.int32, x.shape, 1)
    return jnp.where((lane % 32) < 16, pltpu.roll(x, LANES - 16, 1), pltpu.roll(x, 16, 1))


def _rope(x, c, s):
    return x * c + _swap16(x) * s


def _rope_t(d, c, s):
    return d * c + _swap16(d * s)


def _head_block_map(fn, x, cos, sin, *, name):
    T, W = x.shape
    tt = _tile(T, NORM_TILE, 16)

    def body(x_ref, c_ref, s_ref, o_ref):
        c, s = c_ref[...], s_ref[...]
        for h in range(W // LANES):
            lanes = slice(h * LANES, (h + 1) * LANES)
            o_ref[:, lanes] = fn(x_ref[:, lanes], c, s).astype(BF16)

    tab = pl.BlockSpec((tt, LANES), lambda i: (i, 0))
    blk = pl.BlockSpec((tt, W), lambda i: (i, 0))
    return pl.pallas_call(
        body, name=name, grid=(T // tt,), in_specs=[blk, tab, tab], out_specs=blk,
        out_shape=jax.ShapeDtypeStruct((T, W), BF16), compiler_params=_cparams(("parallel",)),
    )(x, cos, sin)


def _rope_q(q, cos, sin, *, name):
    scale = _attn_scale()
    return _head_block_map(lambda x, c, s: _rope(x, c, s) * scale, q, cos, sin, name=name)


def _rope_q_bwd(dq, cos, sin, *, name):
    return _head_block_map(_rope_t, dq, cos, sin, name=name)


def _key_blocks(kv, z, cos, sin, *, kpe_block, name):
    T = kv.shape[0]
    tt = _tile(T, NORM_TILE, 16)
    W = MLA_HEADS * LANES

    def body(kv_ref, z_ref, c_ref, s_ref, o_ref):
        kr = _rope(z_ref[...], c_ref[...], s_ref[...])
        for h in range(MLA_HEADS):
            lanes = slice(h * LANES, (h + 1) * LANES)
            o_ref[:, lanes] = (kv_ref[:, lanes].astype(F32) + kr).astype(BF16)

    tab = pl.BlockSpec((tt, LANES), lambda i: (i, 0))
    blk = pl.BlockSpec((tt, W), lambda i: (i, 0))
    return pl.pallas_call(
        body, name=name, grid=(T // tt,),
        in_specs=[blk, pl.BlockSpec((tt, LANES), lambda i: (i, kpe_block)), tab, tab], out_specs=blk,
        out_shape=jax.ShapeDtypeStruct((T, W), BF16), compiler_params=_cparams(("parallel",)),
    )(kv, z, cos, sin)


def _key_rope_bwd(dk, cos, sin, *, name):
    T = dk.shape[0]
    tt = _tile(T, NORM_TILE, 16)

    def body(d_ref, c_ref, s_ref, o_ref):
        d = d_ref[:, :LANES]
        for h in range(1, MLA_HEADS):
            d = d + d_ref[:, h * LANES:(h + 1) * LANES]
        lane = lax.broadcasted_iota(jnp.int32, d.shape, 1)
        d = jnp.where(jnp.logical_and(lane >= QK_NOPE, lane < QK_NOPE + QK_ROPE), d, 0.0)
        o_ref[...] = _rope_t(d, c_ref[...], s_ref[...]).astype(BF16)

    tab = pl.BlockSpec((tt, LANES), lambda i: (i, 0))
    return pl.pallas_call(
        body, name=name, grid=(T // tt,),
        in_specs=[pl.BlockSpec((tt, MLA_HEADS * LANES), lambda i: (i, 0)), tab, tab], out_specs=tab,
        out_shape=jax.ShapeDtypeStruct((T, LANES), BF16), compiler_params=_cparams(("parallel",)),
    )(dk, cos, sin)


ATT_BLOCK = 512


def _attn_scale():
    return float((QK_NOPE + QK_ROPE) ** -0.5)


def _causal_mask(qi, kj, tq, tk):
    row = qi * tq + lax.broadcasted_iota(jnp.int32, (tq, tk), 0)
    col = kj * tk + lax.broadcasted_iota(jnp.int32, (tq, tk), 1)
    return col <= row


def _pcall(body, *, name, grid, in_specs, out_specs, out_shape, args, scratch=(), sem=None, ride=None):
    n_in, n_out, n_scr = len(args), len(out_shape), len(scratch)
    if ride is None:
        return pl.pallas_call(
            body, name=name, grid=grid, in_specs=list(in_specs), out_specs=list(out_specs), out_shape=list(out_shape),
            scratch_shapes=list(scratch), compiler_params=_cparams(sem or ("arbitrary",) * len(grid)))(*args)
    ex, sink = ride
    o0 = n_in + len(ex.arrs)
    s0 = o0 + n_out + len(ex.out_shapes)

    def hosted(*refs):
        parts = (refs[n_in:o0], refs[o0 + n_out:s0], refs[-2], refs[-1])
        ids = [pl.program_id(i) for i in range(len(grid))]
        pl.when(functools.reduce(jnp.logical_and, [i == 0 for i in ids]))(lambda: ex.start(*parts))
        body(*refs[:n_in], *refs[o0:o0 + n_out], *refs[s0:s0 + n_scr])
        pl.when(functools.reduce(jnp.logical_and, [i == n - 1 for i, n in zip(ids, grid)]))(lambda: ex.finish(*parts))

    outs = pl.pallas_call(
        hosted, name=name, grid=grid, in_specs=list(in_specs) + ex.in_specs, out_specs=list(out_specs) + ex.out_specs,
        out_shape=list(out_shape) + ex.out_shapes, scratch_shapes=list(scratch) + ex.scratch,
        compiler_params=_cparams(("arbitrary",) * len(grid)))(*args, *ex.arrs)
    sink(outs[n_out:])
    return outs[:n_out]


PAIRS = MLA_HEADS // 2


def _own_lanes(x, first):
    lane = lax.broadcasted_iota(jnp.int32, x.shape, 1)
    return jnp.where((lane < V_HEAD) if first else (lane >= V_HEAD), x, 0.0)


def _lane_sums_as_row(x):
    hi = x.astype(BF16)
    lo = (x - hi.astype(F32)).astype(BF16)
    ones = jnp.ones((8, LANES), BF16)
    return (_dot_nt(ones, hi) + _dot_nt(ones, lo))[0:1, :]


def _attn_fwd(q, k, kv, *, B, S, v_block0, name, ride=None):
    tq = tk = min(ATT_BLOCK, S)
    nq = S // tq
    T = B * S

    def body(q_ref, k_ref, v_ref, o_ref, lse_ref):
        qi = pl.program_id(2)
        qs = (q_ref[:, :LANES], q_ref[:, LANES:])

        def step(masked):
            def f(j, carry):
                rows = pl.ds(pl.multiple_of(j * tk, tk), tk)
                vb = v_ref[rows, :]
                out = []
                for h in range(2):
                    m, l, acc = carry[h]
                    s = _dot_nt(qs[h], k_ref[rows, h * LANES:(h + 1) * LANES])
                    if masked:
                        s = jnp.where(_causal_mask(qi, j, tq, tk), s, -jnp.inf)
                    m_new = jnp.maximum(m, jnp.max(s, axis=-1, keepdims=True))
                    alpha = jnp.exp(m - m_new)
                    p = jnp.exp(s - m_new)
                    out.append((m_new, alpha * l + jnp.sum(p, axis=-1, keepdims=True), alpha * acc + _dot_nn(p, vb)))
                return tuple(out)
            return f

        one = (jnp.full((tq, 1), -1e30, F32), jnp.zeros((tq, 1), F32), jnp.zeros((tq, LANES), F32))
        (ma, la, acca), (mb, lb, accb) = step(True)(qi, lax.fori_loop(0, qi, step(False), (one, one)))
        o_ref[...] = _own_lanes(acca / la, True) + _own_lanes(accb / lb, False)
        for h, lse in enumerate((ma + jnp.log(la), mb + jnp.log(lb))):
            lse_ref[0, h, pl.ds(qi, 1), :] = _lane_sums_as_row(jnp.broadcast_to(lse * (1.0 / LANES), (tq, LANES)))

    return _pcall(
        body, name=name, grid=(B, PAIRS, nq),
        in_specs=[pl.BlockSpec((tq, 2 * LANES), lambda b, g, i: (b * nq + i, g)),
                  pl.BlockSpec((S, 2 * LANES), lambda b, g, i: (b, g)),
                  pl.BlockSpec((S, LANES), lambda b, g, i: (b, v_block0 + g))],
        out_specs=[pl.BlockSpec((tq, LANES), lambda b, g, i: (b * nq + i, g)),
                   pl.BlockSpec((1, 2, nq, tq), lambda b, g, i: (b, g, 0, 0))],
        out_shape=[jax.ShapeDtypeStruct((T, PAIRS * LANES), F32), jax.ShapeDtypeStruct((B, MLA_HEADS, nq, tq), F32)],
        args=(q, k, kv), ride=ride)


def _attn_bwd(q, k, kv, o, lse_rows, do, *, B, S, v_block0, name, ride=None):
    tq = tk = min(ATT_BLOCK, S)
    nq = S // tq
    T = B * S
    scale = _attn_scale()

    def body(q_ref, k_ref, v_ref, o_ref, lse_ref, do_ref, dk_ref, dv_ref, dq_ref, delta_ref):
        kj = pl.program_id(2)
        ks = (k_ref[:, :LANES], k_ref[:, LANES:])
        vb = v_ref[...]

        @pl.when(kj == 0)
        def _():
            dq_ref[...] = jnp.zeros_like(dq_ref)
            for i in range(nq):
                prod = do_ref[i * tq:(i + 1) * tq, :] * o_ref[i * tq:(i + 1) * tq, :]
                for h in range(2):
                    delta_ref[h, i:i + 1, :] = _lane_sums_as_row(_own_lanes(prod, h == 0))

        def step(masked):
            def f(i, carry):
                rows = pl.ds(pl.multiple_of(i * tq, tq), tq)
                do_b = do_ref[rows, :]
                dks, dv = list(carry[:2]), carry[2]
                for h in range(2):
                    qb = q_ref[rows, h * LANES:(h + 1) * LANES]
                    doh = _own_lanes(do_b, h == 0)
                    pt = jnp.exp(_dot_nt(ks[h], qb) - lse_ref[0, h, pl.ds(i, 1), :])
                    if masked:
                        krow = kj * tk + lax.broadcasted_iota(jnp.int32, (tk, tq), 0)
                        qcol = i * tq + lax.broadcasted_iota(jnp.int32, (tk, tq), 1)
                        pt = jnp.where(krow <= qcol, pt, 0.0)
                    dst = pt * (_dot_nt(vb, doh) - delta_ref[h, pl.ds(i, 1), :])
                    dks[h] = dks[h] + _dot_nn(dst, qb)
                    dv = dv + _dot_nn(pt, doh)
                    dq_ref[rows, h * LANES:(h + 1) * LANES] += _dot_tn(dst, ks[h]) * scale
                return dks[0], dks[1], dv
            return f

        zero = jnp.zeros((tk, LANES), F32)
        dka, dkb, dv = lax.fori_loop(kj + 1, nq, step(False), step(True)(kj, (zero, zero, zero)))
        dk_ref[:, :LANES] = dka
        dk_ref[:, LANES:] = dkb
        dv_ref[...] = dv

    krow = lambda w, c0: pl.BlockSpec((tk, w), lambda b, g, j: (b * nq + j, c0 + g))
    seq = lambda w: pl.BlockSpec((S, w), lambda b, g, j: (b, g))
    stat = pl.BlockSpec((1, 2, nq, tq), lambda b, g, j: (b, g, 0, 0))
    dk, dv, dq = _pcall(
        body, name=name, grid=(B, PAIRS, nq),
        in_specs=[seq(2 * LANES), krow(2 * LANES, 0), krow(LANES, v_block0), seq(LANES), stat, seq(LANES)],
        out_specs=[krow(2 * LANES, 0), krow(LANES, 0), seq(2 * LANES)],
        out_shape=[jax.ShapeDtypeStruct((T, MLA_HEADS * LANES), F32), jax.ShapeDtypeStruct((T, PAIRS * LANES), F32),
                   jax.ShapeDtypeStruct((T, MLA_HEADS * LANES), F32)],
        args=(q, k, kv, o, lse_rows, do), scratch=[pltpu.VMEM((2, nq, tq), F32)], ride=ride)
    return dq, dk, dv


def _lru_gates(xl, halo, cw_ref, cb_ref, wa_ref, ba_ref, wx_ref, bx_ref, lam_ref):
    xc = cb_ref[...] + cw_ref[3:4, :] * xl
    for kk in range(LRU_CONV - 1):
        xc = xc + cw_ref[kk:kk + 1, :] * _shift_rows(xl, LRU_CONV - 1 - kk, halo)
    r = _sigmoid(_dot_nn(xc, wa_ref[...]) + ba_ref[...])
    i = _sigmoid(_dot_nn(xc, wx_ref[...]) + bx_ref[...])
    lam = lam_ref[...]
    sp = jnp.maximum(-lam, 0.0) + jnp.log(1.0 + jnp.exp(-jnp.abs(lam)))
    a = jnp.exp(-LRU_C * r * sp)
    mult = jnp.sqrt(1.0 - a * a)
    return xc, r, i, sp, a, mult


def _lru_specs(tt, nt, S):
    def make(rev):
        tmap = (lambda t: nt - 1 - t) if rev else (lambda t: t)
        tile = lambda cb: pl.BlockSpec((tt, LRU_WIDTH), lambda b, t: (b * nt + tmap(t), cb))
        prev8 = lambda cb: pl.BlockSpec(
            (8, LRU_WIDTH), lambda b, t: (jnp.maximum((b * nt + tmap(t)) * (tt // 8) - 1, 0), cb))
        return tile, prev8, tmap
    return make


def _lru_fwd(z, cw, cb, wa, ba, wx, bx, lam, *, S, name, ride=None):
    T = z.shape[0]
    tt = min(ROW_TILE, S)
    nt = S // tt
    tile, prev8, _ = _lru_specs(tt, nt, S)(False)
    vec = lambda r: pl.BlockSpec((r, LRU_WIDTH), lambda b, t: (0, 0))
    mat = pl.BlockSpec((LRU_WIDTH, LRU_WIDTH), lambda b, t: (0, 0))

    def body(xl_ref, halo_ref, gate_ref, cw_ref, cb_ref, wa_ref, ba_ref, wx_ref, bx_ref, lam_ref,
             y_ref, h_ref, carry_ref):
        t = pl.program_id(1)
        first = t == 0
        halo = jnp.where(first, 0.0, halo_ref[...])
        xl_t = xl_ref[...]
        xc, r, i, sp, a, mult = _lru_gates(xl_t, halo, cw_ref, cb_ref, wa_ref, ba_ref, wx_ref, bx_ref, lam_ref)
        bv = mult * (i * xc)
        ones = jnp.ones((8, LRU_WIDTH), F32)
        zeros = jnp.zeros((8, LRU_WIDTH), F32)
        row = lax.broadcasted_iota(jnp.int32, (tt, LRU_WIDTH), 0)
        A = a
        d = 1
        while d < tt:
            if d < 8:
                a_sh = _shift_rows(A, d, ones)
                b_sh = _shift_rows(bv, d, zeros)
            else:
                a_sh = jnp.where(row < d, 1.0, pltpu.roll(A, d, 0))
                b_sh = jnp.where(row < d, 0.0, pltpu.roll(bv, d, 0))
            bv = A * b_sh + bv
            A = A * a_sh
            d *= 2
        h0 = jnp.where(first, 0.0, carry_ref[0:1, :])
        h = A * h0 + bv
        carry_ref[...] = jnp.broadcast_to(h[tt - 1:tt, :], (8, LRU_WIDTH))
        h_ref[...] = h
        y_ref[...] = (h * _gelu(gate_ref[...])).astype(BF16)

    return _pcall(
        body, name=name, grid=(T // S, nt),
        in_specs=[tile(0), prev8(0), tile(1), vec(LRU_CONV), vec(1), mat, vec(1), mat, vec(1), vec(1)],
        out_specs=[tile(0), tile(0)],
        out_shape=[jax.ShapeDtypeStruct((T, LRU_WIDTH), BF16), jax.ShapeDtypeStruct((T, LRU_WIDTH), F32)],
        args=(z, z, z, cw, cb, wa, ba, wx, bx, lam), scratch=[pltpu.VMEM((8, LRU_WIDTH), F32)], ride=ride)


def _lru_bwd(z, h, dy, cw, cb, wa, ba, wx, bx, lam, *, S, name):
    T = z.shape[0]
    tt = min(ROW_TILE, S)
    nt = S // tt
    tile, prev8, tmap = _lru_specs(tt, nt, S)(True)
    vec = lambda r: pl.BlockSpec((r, LRU_WIDTH), lambda b, t: (0, 0))
    mat = pl.BlockSpec((LRU_WIDTH, LRU_WIDTH), lambda b, t: (0, 0))

    def body(xl_ref, halo_ref, gate_ref, h_ref, hprev_ref, dy_ref, cw_ref, cb_ref, wa_ref, ba_ref, wx_ref,
             bx_ref, lam_ref, dxl_ref, dgate_ref, dcw_ref, dcb_ref, dwa_ref, dba_ref, dwx_ref, dbx_ref,
             dlam_ref, lamc_ref, ac_ref, dxc_ref):
        b = pl.program_id(0)
        t = pl.program_id(1)
        tr = nt - 1 - t
        seq_first = tr == 0
        seq_last = t == 0
        halo = jnp.where(seq_first, 0.0, halo_ref[...])
        xl_t = xl_ref[...]
        xc, r, i, sp, a, mult = _lru_gates(xl_t, halo, cw_ref, cb_ref, wa_ref, ba_ref, wx_ref, bx_ref, lam_ref)
        hh = h_ref[...]
        dyf = dy_ref[...].astype(F32)
        gl, dgl = _gelu_and_grad(gate_ref[...])
        dgate_ref[...] = (dyf * hh * dgl).astype(BF16)
        dh = dyf * gl

        a_first_later = jnp.where(seq_last, 0.0, ac_ref[...])
        lam_later = jnp.where(seq_last, 0.0, lamc_ref[...])
        row = lax.broadcasted_iota(jnp.int32, (tt, LRU_WIDTH), 0)
        A = _shift_rows_up(a, 1, a_first_later)
        lm = dh
        ones = jnp.ones((8, LRU_WIDTH), F32)
        zeros = jnp.zeros((8, LRU_WIDTH), F32)
        d = 1
        while d < tt:
            if d < 8:
                a_sh = _shift_rows_up(A, d, ones)
                l_sh = _shift_rows_up(lm, d, zeros)
            else:
                a_sh = jnp.where(row >= tt - d, 1.0, pltpu.roll(A, tt - d, 0))
                l_sh = jnp.where(row >= tt - d, 0.0, pltpu.roll(lm, tt - d, 0))
            lm = lm + A * l_sh
            A = A * a_sh
            d *= 2
        lm = lm + A * lam_later[0:1, :]
        lamc_ref[...] = jnp.broadcast_to(lm[0:1, :], (8, LRU_WIDTH))
        ac_ref[...] = jnp.broadcast_to(a[0:1, :], (8, LRU_WIDTH))

        hprev_halo = jnp.where(seq_first, 0.0, hprev_ref[...])
        h_prev = _shift_rows(hh, 1, hprev_halo)
        da = lm * h_prev
        ixc = i * xc
        dmult = lm * ixc
        di = lm * mult * xc
        dxc = lm * mult * i
        da = da - dmult * a / mult
        dlog = da * a
        dr = dlog * (-LRU_C) * sp
        dsp_part = jnp.sum(dlog * (-LRU_C) * r, axis=0, keepdims=True)
        dpa = dr * r * (1.0 - r)
        dpx = di * i * (1.0 - i)
        dxc = dxc + _dot_nt(dpa, wa_ref[...]) + _dot_nt(dpx, wx_ref[...])
        dwa_part = _dot_tn(xc, dpa)
        dwx_part = _dot_tn(xc, dpx)

        later = jnp.where(seq_last, 0.0, dxc_ref[...])
        dxl = cw_ref[3:4, :] * dxc
        for kk in range(LRU_CONV - 1):
            dxl = dxl + cw_ref[kk:kk + 1, :] * _shift_rows_up(dxc, LRU_CONV - 1 - kk, later)
        dxl_ref[...] = dxl.astype(BF16)
        dxc_ref[...] = dxc[0:8, :]
        dcw_rows = [jnp.sum(dxc * _shift_rows(xl_t, LRU_CONV - 1 - kk, halo), axis=0, keepdims=True)
                    for kk in range(LRU_CONV - 1)]
        dcw_rows.append(jnp.sum(dxc * xl_t, axis=0, keepdims=True))
        dcw_part = jnp.concatenate(dcw_rows + [jnp.zeros((8 - LRU_CONV, LRU_WIDTH), F32)], axis=0)
        lamv = lam_ref[...]
        dlam_part = dsp_part * (-_sigmoid(-lamv))
        parts = ((dcw_ref, dcw_part), (dcb_ref, jnp.sum(dxc, axis=0, keepdims=True)),
                 (dwa_ref, dwa_part), (dba_ref, jnp.sum(dpa, axis=0, keepdims=True)),
                 (dwx_ref, dwx_part), (dbx_ref, jnp.sum(dpx, axis=0, keepdims=True)),
                 (dlam_ref, dlam_part))
        start = jnp.logical_and(b == 0, t == 0)

        @pl.when(start)
        def _():
            for ref, val in parts:
                ref[...] = val

        @pl.when(jnp.logical_not(start))
        def _():
            for ref, val in parts:
                ref[...] += val

    acc = lambda r: pl.BlockSpec((r, LRU_WIDTH), lambda b, t: (0, 0))
    return pl.pallas_call(
        body, name=name, grid=(T // S, nt),
        in_specs=[tile(0), prev8(0), tile(1), tile(0), prev8(0), tile(0),
                  vec(LRU_CONV), vec(1), mat, vec(1), mat, vec(1), vec(1)],
        out_specs=[tile(0), tile(0), acc(8), acc(1), mat, acc(1), mat, acc(1), acc(1)],
        out_shape=[jax.ShapeDtypeStruct((T, LRU_WIDTH), BF16), jax.ShapeDtypeStruct((T, LRU_WIDTH), BF16),
                   jax.ShapeDtypeStruct((8, LRU_WIDTH), F32), jax.ShapeDtypeStruct((1, LRU_WIDTH), F32),
                   jax.ShapeDtypeStruct((LRU_WIDTH, LRU_WIDTH), F32), jax.ShapeDtypeStruct((1, LRU_WIDTH), F32),
                   jax.ShapeDtypeStruct((LRU_WIDTH, LRU_WIDTH), F32), jax.ShapeDtypeStruct((1, LRU_WIDTH), F32),
                   jax.ShapeDtypeStruct((1, LRU_WIDTH), F32)],
        scratch_shapes=[pltpu.VMEM((8, LRU_WIDTH), F32), pltpu.VMEM((8, LRU_WIDTH), F32),
                        pltpu.VMEM((8, LRU_WIDTH), F32)],
        compiler_params=_cparams(("arbitrary", "arbitrary")),
    )(z, z, z, h, h, dy, cw, cb, wa, ba, wx, bx, lam)


FFN_CT = 1408
FFN_TILE = 512


def _ffn_conv(g, halo, cw, cb):
    gc = cb + cw[2:3, :] * g
    for kk in range(FFN_CONV - 1):
        gc = gc + cw[kk:kk + 1, :] * _shift_rows(g, FFN_CONV - 1 - kk, halo)
    return gc


def _row_chunks(rows, chunk):
    return [slice(r0, min(r0 + chunk, rows)) for r0 in range(0, rows, chunk)]


FFN_CHUNK = 128
HALO = 16


def _ffn_act_down(g, u, cw, cb, w_down, res, *, S, name, ride=None):
    T, F = g.shape
    D = w_down.shape[1]
    tt = min(2 * FFN_TILE, S)
    nt = S // tt
    tc = _tile(F, FFN_CT)
    nj = F // tc

    def body(g_ref, halo_ref, u_ref, cw_ref, cb_ref, w_ref, r_ref, o_ref, act_ref):
        j = pl.program_id(1)
        first = (pl.program_id(0) % nt) == 0
        cw, cb = cw_ref[...], cb_ref[...]

        @pl.when(j == 0)
        def _():
            o_ref[...] = r_ref[...]

        for r in _row_chunks(tt, FFN_CHUNK):
            before = halo_ref[...] if r.start == 0 else g_ref[r.start - HALO:r.start, :]
            halo = before.astype(F32)[HALO - 8:]
            if r.start == 0:
                halo = jnp.where(first, 0.0, halo)
            gc = _ffn_conv(g_ref[r, :].astype(F32), halo, cw, cb)
            act = (_gelu(gc) * u_ref[r, :].astype(F32)).astype(BF16)
            act_ref[r, :] = act
            o_ref[r, :] += _dot_nn(act, w_ref[...])

    tile = pl.BlockSpec((tt, tc), lambda i, j: (i, j))
    prev = pl.BlockSpec((HALO, tc), lambda i, j: (jnp.maximum(i * (tt // HALO) - 1, 0), j))
    rows = pl.BlockSpec((tt, D), lambda i, j: (i, 0))
    return _pcall(
        body, name=name, grid=(T // tt, nj),
        in_specs=[tile, prev, tile, pl.BlockSpec((FFN_CONV, tc), lambda i, j: (0, j)),
                  pl.BlockSpec((1, tc), lambda i, j: (0, j)), pl.BlockSpec((tc, D), lambda i, j: (j, 0)), rows],
        out_specs=[rows, tile], out_shape=[jax.ShapeDtypeStruct((T, D), F32), jax.ShapeDtypeStruct((T, F), BF16)],
        args=(g, g, u, cw, cb, w_down, res), sem=("parallel", "arbitrary"), ride=ride)


def _ffn_act_bwd(g, u, dh, w_down, cw, cb, *, S, name, ride=None):
    T, F = g.shape
    D = w_down.shape[1]
    tt = min(FFN_TILE, S)
    nt = S // tt
    ntt = T // tt
    tc = _tile(F, FFN_CT)

    def body(g_ref, halo_ref, u_ref, dh_ref, w_ref, cw_ref, cb_ref, dg_ref, du_ref, dcw_ref, dcb_ref, later_ref):
        step = pl.program_id(1)
        ti = (ntt - 1 - step) % nt
        cw, cb = cw_ref[...], cb_ref[...]

        @pl.when(step == 0)
        def _():
            dcw_ref[...] = jnp.zeros_like(dcw_ref)
            dcb_ref[...] = jnp.zeros_like(dcb_ref)

        halo = jnp.where(ti == 0, 0.0, halo_ref[...].astype(F32)[HALO - 8:])
        gt = g_ref[...].astype(F32)
        gl, dgl = _gelu_and_grad(_ffn_conv(gt, halo, cw, cb))
        da = _dot_nt(dh_ref[...], w_ref[...])
        du_ref[...] = (da * gl).astype(BF16)
        dgc = da * u_ref[...].astype(F32) * dgl
        later = jnp.where(ti == nt - 1, 0.0, later_ref[...])
        dg = cw[2:3, :] * dgc
        for kk in range(FFN_CONV - 1):
            dg = dg + cw[kk:kk + 1, :] * _shift_rows_up(dgc, FFN_CONV - 1 - kk, later)
        dg_ref[...] = dg.astype(BF16)
        later_ref[...] = dgc[0:8, :]
        rows = [jnp.sum(dgc * _shift_rows(gt, FFN_CONV - 1 - kk, halo), axis=0, keepdims=True)
                for kk in range(FFN_CONV - 1)]
        rows.append(jnp.sum(dgc * gt, axis=0, keepdims=True))
        dcw_ref[...] += jnp.concatenate(rows + [jnp.zeros((8 - FFN_CONV, tc), F32)], axis=0)
        dcb_ref[...] += jnp.sum(dgc, axis=0, keepdims=True)

    tile = pl.BlockSpec((tt, tc), lambda j, s: (ntt - 1 - s, j))
    prev = pl.BlockSpec((HALO, tc), lambda j, s: (jnp.maximum((ntt - 1 - s) * (tt // HALO) - 1, 0), j))
    return _pcall(
        body, name=name, grid=(F // tc, ntt),
        in_specs=[tile, prev, tile, pl.BlockSpec((tt, D), lambda j, s: (ntt - 1 - s, 0)),
                  pl.BlockSpec((tc, D), lambda j, s: (j, 0)), pl.BlockSpec((FFN_CONV, tc), lambda j, s: (0, j)),
                  pl.BlockSpec((1, tc), lambda j, s: (0, j))],
        out_specs=[tile, tile, pl.BlockSpec((8, tc), lambda j, s: (0, j)), pl.BlockSpec((1, tc), lambda j, s: (0, j))],
        out_shape=[jax.ShapeDtypeStruct((T, F), BF16), jax.ShapeDtypeStruct((T, F), BF16),
                   jax.ShapeDtypeStruct((8, F), F32), jax.ShapeDtypeStruct((1, F), F32)],
        args=(g, g, u, dh, w_down, cw, cb), scratch=[pltpu.VMEM((8, tc), F32)], ride=ride)


def _sgu_norm(zv, g_ref, b_ref):
    v = _gelu(zv)
    mu = jnp.mean(v, axis=-1, keepdims=True)
    xc = v - mu
    rstd = lax.rsqrt(jnp.mean(xc * xc, axis=-1, keepdims=True) + NORM_EPS)
    xhat = xc * rstd
    return xhat, rstd, xhat * g_ref[...] + b_ref[...]


def _sgu_fwd(zc, ln_g, ln_b, wm, bmap, *, name):
    T = zc.shape[0]
    W = SGU_WIDTH
    tt = _tile(T, SGU_TILE, CHUNK)
    nch = tt // CHUNK

    def body(z_ref, g_ref, b_ref, wm_ref, bm_ref, p_ref):
        u = _gelu(z_ref[:, :W])
        _, _, vn = _sgu_norm(z_ref[:, W:], g_ref, b_ref)
        vn = vn.astype(BF16)
        for n in range(nch):
            rows = slice(n * CHUNK, (n + 1) * CHUNK)
            for gi in range(SGU_GROUPS):
                cols = slice(gi * LANES, (gi + 1) * LANES)
                s = _dot_nn(wm_ref[gi], vn[rows, cols]) + bm_ref[:, cols]
                p_ref[rows, cols] = (u[rows, cols] * s).astype(BF16)

    const2 = lambda r, c: pl.BlockSpec((r, c), lambda i: (0, 0))
    return pl.pallas_call(
        body, name=name, grid=(T // tt,),
        in_specs=[pl.BlockSpec((tt, 2 * W), lambda i: (i, 0)), const2(1, W), const2(1, W),
                  pl.BlockSpec((SGU_GROUPS, CHUNK, CHUNK), lambda i: (0, 0, 0)), const2(CHUNK, W)],
        out_specs=pl.BlockSpec((tt, W), lambda i: (i, 0)),
        out_shape=jax.ShapeDtypeStruct((T, W), BF16),
        compiler_params=_cparams(("parallel",)),
    )(zc, ln_g, ln_b, wm, bmap)


def _sgu_bwd(zc, dp, ln_g, ln_b, wm, bmap, *, name, ride=None):
    T = zc.shape[0]
    W = SGU_WIDTH
    tt = _tile(T, SGU_TILE, CHUNK)
    nch = tt // CHUNK
    nsteps = T // tt

    def body(z_ref, dp_ref, g_ref, b_ref, wm_ref, bm_ref, dz_ref, dg_ref, db_ref, dwm_ref, dbm_ref,
             s_scr, dvn_scr):
        step = pl.program_id(0)
        zu = z_ref[:, :W]
        zv = z_ref[:, W:]
        u, dgu = _gelu_and_grad(zu)
        xhat, rstd, vn = _sgu_norm(zv, g_ref, b_ref)
        vnb = vn.astype(BF16)
        dpf = dp_ref[...].astype(F32)
        ds = dpf * u

        @pl.when(step == 0)
        def _():
            dwm_ref[...] = jnp.zeros_like(dwm_ref)
            dbm_ref[...] = jnp.zeros_like(dbm_ref)

        for n in range(nch):
            rows = slice(n * CHUNK, (n + 1) * CHUNK)
            for gi in range(SGU_GROUPS):
                cols = slice(gi * LANES, (gi + 1) * LANES)
                s_scr[rows, cols] = _dot_nn(wm_ref[gi], vnb[rows, cols]) + bm_ref[:, cols]
                dsb = ds[rows, cols]
                dvn_scr[rows, cols] = _dot_tn(wm_ref[gi], dsb)
                dwm_ref[gi] += _dot_nt(dsb, vnb[rows, cols])
                dbm_ref[:, cols] += dsb
        dz_ref[:, :W] = (dpf * s_scr[...] * dgu).astype(BF16)
        dvn = dvn_scr[...]
        dxhat = dvn * g_ref[...]
        dv = rstd * (dxhat - jnp.mean(dxhat, axis=-1, keepdims=True)
                     - xhat * jnp.mean(dxhat * xhat, axis=-1, keepdims=True))
        _, dgv = _gelu_and_grad(zv)
        dz_ref[:, W:] = (dv * dgv).astype(BF16)
        dg_part = jnp.sum(dvn * xhat, axis=0, keepdims=True)
        db_part = jnp.sum(dvn, axis=0, keepdims=True)

        @pl.when(step == 0)
        def _():
            dg_ref[...] = dg_part
            db_ref[...] = db_part

        @pl.when(step > 0)
        def _():
            dg_ref[...] += dg_part
            db_ref[...] += db_part

        @pl.when(step == nsteps - 1)
        def _():
            for gi in range(SGU_GROUPS):
                cols = slice(gi * LANES, (gi + 1) * LANES)
                tot = jnp.sum(dbm_ref[:, cols], axis=1, keepdims=True)
                dbm_ref[:, cols] = jnp.broadcast_to(tot, (CHUNK, LANES))

    const2 = lambda r, c: pl.BlockSpec((r, c), lambda i: (0, 0))
    wspec = pl.BlockSpec((SGU_GROUPS, CHUNK, CHUNK), lambda i: (0, 0, 0))
    return _pcall(
        body, name=name, grid=(nsteps,),
        in_specs=[pl.BlockSpec((tt, 2 * W), lambda i: (i, 0)), pl.BlockSpec((tt, W), lambda i: (i, 0)),
                  const2(1, W), const2(1, W), wspec, const2(CHUNK, W)],
        out_specs=[pl.BlockSpec((tt, 2 * W), lambda i: (i, 0)), const2(1, W), const2(1, W), wspec, const2(CHUNK, W)],
        out_shape=[jax.ShapeDtypeStruct((T, 2 * W), BF16), jax.ShapeDtypeStruct((1, W), F32),
                   jax.ShapeDtypeStruct((1, W), F32), jax.ShapeDtypeStruct((SGU_GROUPS, CHUNK, CHUNK), F32),
                   jax.ShapeDtypeStruct((CHUNK, W), F32)],
        args=(zc, dp, ln_g, ln_b, wm, bmap), scratch=[pltpu.VMEM((tt, W), F32), pltpu.VMEM((tt, W), F32)], ride=ride)


def _rope_tables(positions):
    half = QK_ROPE // 2
    inv_freq = jnp.exp(-math.log(ROPE_BASE) * jnp.arange(half, dtype=F32) / half)
    ang = positions.reshape(-1).astype(F32)[:, None] * inv_freq
    cos = jnp.cos(ang)
    sin = jnp.sin(ang)
    n = ang.shape[0]
    tail = LANES - QK_NOPE - QK_ROPE
    cos_t = jnp.concatenate([jnp.ones((n, QK_NOPE), F32), cos, cos, jnp.ones((n, tail), F32)], axis=1)
    sin_t = jnp.concatenate([jnp.zeros((n, QK_NOPE), F32), -sin, sin, jnp.zeros((n, tail), F32)], axis=1)
    return cos_t, sin_t


SGU_GROUP_DIM = SGU_WIDTH // SGU_GROUPS
_O1, _O2, _O3, _O4 = Q_LORA, Q_LORA + KV_LORA, Q_LORA + KV_LORA + QK_ROPE, Q_LORA + KV_LORA + QK_ROPE + LRU_WIDTH
_A0, _A1, _A2 = 2 * LRU_WIDTH, 2 * LRU_WIDTH + Q_LORA, 2 * LRU_WIDTH + Q_LORA + KV_LORA
_A3 = _A2 + QK_NOPE
Z_Q_BLOCK, Z_KV_BLOCK, Z_KPE_BLOCK = _A0 // Q_LORA, _A1 // KV_LORA, _A2 // LANES


def _perm_w_in(w_in):
    zeros = lambda n: jnp.zeros((w_in.shape[0], n), w_in.dtype)
    return jnp.concatenate([w_in[:, _O3:_O4], w_in[:, _O4:], w_in[:, :_O1], w_in[:, _O1:_O2], zeros(QK_NOPE),
                            w_in[:, _O2:_O3], zeros(LANES - QK_NOPE - QK_ROPE)], axis=1)


def _unperm_w_in(w):
    return jnp.concatenate([w[:, _A0:_A1], w[:, _A1:_A2], w[:, _A3:_A3 + QK_ROPE], w[:, :LRU_WIDTH],
                            w[:, LRU_WIDTH:_A0]], axis=1)


def _head_blocks(w, d):
    r = w.shape[0]
    return jnp.pad(w.reshape(r, MLA_HEADS, d), ((0, 0), (0, 0), (0, LANES - d))).reshape(r, MLA_HEADS * LANES)


def _from_head_blocks(w, d):
    r = w.shape[0]
    return w.reshape(r, MLA_HEADS, LANES)[:, :, :d].reshape(r, MLA_HEADS * d)


def _split_kv(w_kv):
    r = w_kv.shape[0]
    w3 = w_kv.reshape(r, MLA_HEADS, QK_NOPE + V_HEAD)
    return _head_blocks(w3[:, :, :QK_NOPE].reshape(r, -1), QK_NOPE), w3[:, :, QK_NOPE:].reshape(r, -1)


def _join_kv(w_k, w_v):
    r = w_k.shape[0]
    return jnp.concatenate([_from_head_blocks(w_k, QK_NOPE).reshape(r, MLA_HEADS, QK_NOPE),
                            w_v.reshape(r, MLA_HEADS, V_HEAD)], axis=2).reshape(r, -1)


def _prep_small(w):
    p = {n: w[n] for n in w if n not in BIG}
    eye = jnp.eye(LRU_HEADS, dtype=F32)
    dense = lambda wg: (wg[:, :, None, :] * eye[:, None, :, None]).reshape(LRU_WIDTH, LRU_WIDTH).astype(BF16)
    p["wa_d"] = dense(w["ab_w_rg_a"][0])
    p["wx_d"] = dense(w["ab_w_rg_x"][0])
    causal = jnp.tril(jnp.ones((CHUNK, CHUNK), F32))
    p["wm"] = (w["c_w_s"][0] * causal).astype(BF16)
    p["bmap"] = jnp.repeat(w["c_b_s"][0].T, SGU_GROUP_DIM, axis=1)
    return p


def _prep_big(ab_w_in, ab_w_q_b, ab_w_kv_b):
    return {"w_in_p": _perm_w_in(ab_w_in).astype(BF16),
            "w_q_p": _head_blocks(ab_w_q_b, QK_NOPE + QK_ROPE).astype(BF16),
            "w_kv_p": jnp.concatenate(_split_kv(ab_w_kv_b), axis=1).astype(BF16)}


def _ffn_fwd(h, l, p, S, rides):
    hn = _rms_fwd(h, p["ffn_norm"][l], name=f"ffn{l}_norm")
    g = _mm(hn, p["ffn_gate_t"][l], tb=True, out_dtype=BF16, name=f"ffn{l}_gate", ride=rides.get(f"ffn{l}_gate"))
    u = _mm(hn, p["ffn_up_t"][l], tb=True, out_dtype=BF16, name=f"ffn{l}_up", ride=rides.get(f"ffn{l}_up"))
    out, act = _ffn_act_down(g, u, p["ffn_conv_w"][l], p["ffn_conv_b"][l][None], p["ffn_down"][l], h, S=S,
                             name=f"ffn{l}_down", ride=rides.get(f"ffn{l}_down"))
    return out, (hn, g, u, act)


def _ffn_bwd(dh, h_in, l, p, saved, S, rides, grads_ready, also_ready=None):
    hn, g, u, act = saved
    dw_down = _mm(act, dh, ta=True, out_dtype=BF16, name=f"ffn{l}_dwdown")
    dg, du, dcw, dcb = _ffn_act_bwd(g, u, dh, p["ffn_down"][l], p["ffn_conv_w"][l], p["ffn_conv_b"][l][None], S=S,
                                    name=f"ffn{l}_dactbwd", ride=rides.get(f"ffn{l}_dactbwd"))
    dhn = _mm(dg, p["ffn_gate_t"][l], also=(du, p["ffn_up_t"][l]), out_dtype=BF16, name=f"ffn{l}_dhn")
    dw_gate_t = _mm(dg, hn, ta=True, out_dtype=BF16, name=f"ffn{l}_dwgate")
    dw_up_t = _mm(du, hn, ta=True, out_dtype=BF16, name=f"ffn{l}_dwup")
    grads_ready(l, {**(also_ready or {}), "ffn_gate_t": dw_gate_t, "ffn_up_t": dw_up_t, "ffn_down": dw_down})
    dh_in, dnorm = _rms_bwd(h_in, p["ffn_norm"][l], dhn, res=dh, name=f"ffn{l}_dnorm", ride=rides.get(f"ffn{l}_dnorm"))
    grads = dict(ffn_norm=dnorm[0], ffn_gate_t=dw_gate_t, ffn_up_t=dw_up_t, ffn_conv_w=dcw[:FFN_CONV],
                 ffn_conv_b=dcb[0], ffn_down=dw_down)
    return dh_in, grads


def _local_step(x, positions, target, p, rides=None, grads_ready=None):
    rides = {} if rides is None else rides
    grads_ready = grads_ready or (lambda layer, ready: None)
    B, S, D = x.shape
    T = B * S
    H = MLA_HEADS
    xf = x.reshape(T, D)
    tgt = target.reshape(T, D)
    cos, sin = _rope_tables(positions)

    hn0 = _rms_fwd(xf, p["ab_norm"][0], name="ab_norm", ride=rides.get("ab_norm"))
    z = _mm(hn0, p["w_in_p"], name="ab_in")
    cqn = _rms_fwd(z, p["ab_q_norm"][0], cb=Z_Q_BLOCK, name="q_norm")
    ckvn = _rms_fwd(z, p["ab_kv_norm"][0], cb=Z_KV_BLOCK, name="kv_norm")
    q = _mm(cqn, p["w_q_p"], name="q_up")
    kv = _mm(ckvn, p["w_kv_p"], out_dtype=BF16, name="kv_up")
    qs = _rope_q(q, cos, sin, name="q_rope")
    kk = _key_blocks(kv, z, cos, sin, kpe_block=Z_KPE_BLOCK, name="k_rope")
    att = dict(B=B, S=S, v_block0=H)
    o, lse = _attn_fwd(qs, kk, kv, name="attn_fwd", ride=rides.get("attn_fwd"), **att)
    lru_par = (p["ab_conv_w"][0], p["ab_conv_b"], p["wa_d"], p["ab_b_rg_a"], p["wx_d"], p["ab_b_rg_x"], p["ab_lambda"])
    y_lru, hs = _lru_fwd(z, *lru_par, S=S, name="lru_fwd", ride=rides.get("lru_fwd"))
    n_att = H * V_HEAD
    w_out_a, w_out_b = p["ab_w_out"][:n_att], p["ab_w_out"][n_att:]
    h1 = _mm(o, w_out_a, also=(y_lru, w_out_b), res=xf, name="ab_out")
    h2, ffn0 = _ffn_fwd(h1, 0, p, S, rides)

    hn2 = _rms_fwd(h2, p["c_norm"][0], name="c_norm")
    zc = _mm(hn2, p["c_w_in_t"], tb=True, name="c_in")
    pg = _sgu_fwd(zc, p["c_ln_g"], p["c_ln_b"], p["wm"], p["bmap"], name="sgu_fwd")
    h3 = _mm(pg, p["c_w_out"], res=h2, name="c_out")
    h4, ffn1 = _ffn_fwd(h3, 1, p, S, rides)

    loss_row, dh4, dfinal = _final_fwd_bwd(h4, p["final_norm"], tgt, name="final")

    dh3, g_ffn1 = _ffn_bwd(dh4, h3, 1, p, ffn1, S, rides, grads_ready)
    dpg = _mm(dh3, p["c_w_out"], tb=True, out_dtype=BF16, name="c_dp")
    dw_c_out = _mm(pg, dh3, ta=True, out_dtype=BF16, name="c_dwout")
    dzc, dlng, dlnb, dwm, dbm = _sgu_bwd(zc, dpg, p["c_ln_g"], p["c_ln_b"], p["wm"], p["bmap"], name="sgu_bwd",
                                         ride=rides.get("sgu_bwd"))
    dhn2 = _mm(dzc, p["c_w_in_t"], out_dtype=BF16, name="c_dhn")
    dw_c_in_t = _mm(dzc, hn2, ta=True, out_dtype=BF16, name="c_dwin")
    dh2, dcnorm = _rms_bwd(h2, p["c_norm"][0], dhn2, res=dh3, name="c_dnorm")
    dh1, g_ffn0 = _ffn_bwd(dh2, h1, 0, p, ffn0, S, rides, grads_ready, {"c_w_in_t": dw_c_in_t, "c_w_out": dw_c_out})

    do = _mm(dh1, w_out_a, tb=True, name="ab_do")
    dy_lru = _mm(dh1, w_out_b, tb=True, out_dtype=BF16, name="ab_dylru")
    dw_out = jnp.concatenate([_mm(o, dh1, ta=True, out_dtype=BF16, name="ab_dwout_a"),
                              _mm(y_lru, dh1, ta=True, out_dtype=BF16, name="ab_dwout_b")], axis=0)
    dq, dk, dv = _attn_bwd(qs, kk, kv, o, lse, do, name="attn_bwd", ride=rides.get("attn_bwd"), **att)
    dq_full = _rope_q_bwd(dq, cos, sin, name="q_rope_bwd")
    dkr = _key_rope_bwd(dk, cos, sin, name="k_rope_bwd")
    n_key = H * LANES
    w_k_p, w_v_p = p["w_kv_p"][:, :n_key], p["w_kv_p"][:, n_key:]
    dcqn = _mm(dq_full, p["w_q_p"], tb=True, name="q_dlat")
    dw_q_p = _mm(cqn, dq_full, ta=True, out_dtype=BF16, name="q_dw")
    dckvn = _mm(dv, w_v_p, tb=True, res=_mm(dk, w_k_p, tb=True, name="k_dlat"), name="v_dlat")
    dw_k_p = _mm(ckvn, dk, ta=True, out_dtype=BF16, name="k_dw")
    dw_v_p = _mm(ckvn, dv, ta=True, out_dtype=BF16, name="v_dw")
    dcq, dqnorm = _rms_bwd(z, p["ab_q_norm"][0], dcqn, cb=Z_Q_BLOCK, out_dtype=BF16, name="q_dnorm")
    dckv, dkvnorm = _rms_bwd(z, p["ab_kv_norm"][0], dckvn, cb=Z_KV_BLOCK, out_dtype=BF16, name="kv_dnorm")
    dxl, dgate, dcw, dcb, dwa, dba, dwx, dbx, dlam = _lru_bwd(z, hs, dy_lru, *lru_par, S=S, name="lru_bwd")
    dz = jnp.concatenate([dxl, dgate, dcq, dckv, dkr], axis=1)
    dhn0 = _mm(dz, p["w_in_p"], tb=True, out_dtype=BF16, name="ab_dhn")
    dw_in_p = _mm(hn0, dz, ta=True, out_dtype=BF16, name="ab_dwin")
    dx, dabnorm = _rms_bwd(xf, p["ab_norm"][0], dhn0, res=dh1, name="ab_dnorm")

    blocks = lambda dd: jnp.stack([dd[i * LRU_BLOCK:(i + 1) * LRU_BLOCK, i * LRU_BLOCK:(i + 1) * LRU_BLOCK]
                                   for i in range(LRU_HEADS)])
    causal = jnp.tril(jnp.ones((CHUNK, CHUNK), F32))
    grads = {
        "ab_norm": dabnorm, "w_in_p": dw_in_p, "ab_q_norm": dqnorm, "w_q_p": dw_q_p,
        "ab_kv_norm": dkvnorm, "w_k_p": dw_k_p, "w_v_p": dw_v_p, "ab_conv_w": dcw[:LRU_CONV][None], "ab_conv_b": dcb,
        "ab_w_rg_a": blocks(dwa)[None], "ab_b_rg_a": dba, "ab_w_rg_x": blocks(dwx)[None], "ab_b_rg_x": dbx,
        "ab_lambda": dlam, "ab_w_out": dw_out,
        "c_norm": dcnorm, "c_w_in_t": dw_c_in_t, "c_ln_g": dlng, "c_ln_b": dlnb,
        "c_w_s": (dwm * causal)[None], "c_b_s": dbm[:, ::SGU_GROUP_DIM].T[None], "c_w_out": dw_c_out,
        "final_norm": dfinal[0],
    }
    for name in ("ffn_norm", "ffn_conv_w", "ffn_conv_b"):
        grads[name] = jnp.stack([g_ffn0[name], g_ffn1[name]])
    for name in ("ffn_gate_t", "ffn_up_t", "ffn_down"):
        grads[name] = [g_ffn0[name], g_ffn1[name]]
    return loss_row, dx.reshape(B, S, D), grads


ANY = pl.BlockSpec(memory_space=pl.ANY)


def _place():
    x, y, c = lax.axis_index("x"), lax.axis_index("y"), lax.axis_index("c")
    chips = [(1 - x, y), (x, 1 - y), (1 - x, 1 - y)]
    return x, y, c, 2 * x + y, (x, y, 1 - c), chips


def _remote(src, dst, send_sems, recv_sems, k, to):
    return pltpu.make_async_remote_copy(src_ref=src, dst_ref=dst, send_sem=send_sems.at[k], recv_sem=recv_sems.at[k],
                                        device_id=to, device_id_type=MESH)


class _Exchange:
    def __init__(self, arrs, out_shapes, n_sems, start, finish):
        self.arrs, self.out_shapes, self.n_sems, self.start, self.finish = list(arrs), out_shapes, n_sems, start, finish

    @property
    def in_specs(self):
        return [ANY] * len(self.arrs)

    @property
    def out_specs(self):
        return [ANY] * len(self.out_shapes)

    @property
    def scratch(self):
        return [pltpu.SemaphoreType.DMA((self.n_sems,)), pltpu.SemaphoreType.DMA((self.n_sems,))]

    def split(self, refs):
        n = len(self.arrs)
        return refs[:n], refs[n:n + len(self.out_shapes)], refs[-2], refs[-1]

    def run(self, name):
        def body(*refs):
            parts = self.split(refs)
            self.start(*parts)
            self.finish(*parts)

        return pl.pallas_call(body, name=name, in_specs=self.in_specs, out_specs=self.out_specs,
                              out_shape=self.out_shapes, scratch_shapes=self.scratch)(*self.arrs)


def _put(buf, piece, idx, axis):
    return lax.dynamic_update_slice_in_dim(buf, jnp.expand_dims(piece, axis).astype(buf.dtype), idx, axis)


def _all_gather(arrs):
    n = len(arrs)
    per = 7

    def start(ins, outs, send_sems, recv_sems):
        x, y, c, j, sib, chips = _place()
        for i in range(n):
            for k, (cx, cy) in enumerate(chips):
                _remote(ins[i].at[:, c], outs[i].at[:, j, c], send_sems, recv_sems, per * i + k, (cx, cy, c)).start()
            _remote(ins[i], outs[i].at[:, j], send_sems, recv_sems, per * i + 6, sib).start()

    def finish(ins, outs, send_sems, recv_sems):
        x, y, c, j, sib, chips = _place()
        passed = []
        for i in range(n):
            for k, (cx, cy) in enumerate(chips):
                got = outs[i].at[:, 2 * cx + cy, c]
                _remote(got, got, send_sems, recv_sems, per * i + k, (cx, cy, c)).wait_recv()
                cp = _remote(got, got, send_sems, recv_sems, per * i + 3 + k, sib)
                cp.start()
                passed.append(cp)
        for i in range(n):
            for k, (cx, cy) in enumerate(chips):
                got = outs[i].at[:, 2 * cx + cy, 1 - c]
                _remote(got, got, send_sems, recv_sems, per * i + 3 + k, sib).wait_recv()
                _remote(ins[i].at[:, c], ins[i].at[:, c], send_sems, recv_sems, per * i + k, sib).wait_send()
            _remote(ins[i], outs[i].at[:, j], send_sems, recv_sems, per * i + 6, sib).wait()
        for cp in passed:
            cp.wait_send()

    shapes = [jax.ShapeDtypeStruct((a.shape[0], N_CHIPS) + a.shape[1:], a.dtype) for a in arrs]
    return _Exchange(arrs, shapes, per * n, start, finish)


class _Offset:
    def __init__(self, sems, k0):
        self.sems, self.k0 = sems, k0

    @property
    def at(self):
        return self

    def __getitem__(self, k):
        return self.sems.at[self.k0 + k]


def _merge(a, b):
    n_in, n_out = len(a.arrs), len(a.out_shapes)

    def both(fa, fb):
        def f(ins, outs, send_sems, recv_sems):
            fa(ins[:n_in], outs[:n_out], send_sems, recv_sems)
            fb(ins[n_in:], outs[n_out:], _Offset(send_sems, a.n_sems), _Offset(recv_sems, a.n_sems))
        return f

    return _Exchange(a.arrs + b.arrs, a.out_shapes + b.out_shapes, a.n_sems + b.n_sems,
                     both(a.start, b.start), both(a.finish, b.finish))


def _pair_swap(arrs):
    n = len(arrs)

    def start(ins, outs, send_sems, recv_sems):
        x, y, c, j, sib, chips = _place()
        for i in range(n):
            _remote(ins[i].at[:, 1 - c], outs[i], send_sems, recv_sems, i, sib).start()

    def finish(ins, outs, send_sems, recv_sems):
        x, y, c, j, sib, chips = _place()
        for i in range(n):
            _remote(ins[i].at[:, 1 - c], outs[i], send_sems, recv_sems, i, sib).wait()

    shapes = [jax.ShapeDtypeStruct((a.shape[0],) + a.shape[2:], a.dtype) for a in arrs]
    return _Exchange(arrs, shapes, n, start, finish)


def _pair_send(arrs):
    n = len(arrs)

    def start(ins, outs, send_sems, recv_sems):
        x, y, c, j, sib, chips = _place()
        for i in range(n):
            _remote(ins[i], outs[i], send_sems, recv_sems, i, sib).start()

    def finish(ins, outs, send_sems, recv_sems):
        x, y, c, j, sib, chips = _place()
        for i in range(n):
            _remote(ins[i], outs[i], send_sems, recv_sems, i, sib).wait()

    shapes = [jax.ShapeDtypeStruct(a.shape, a.dtype) for a in arrs]
    return _Exchange(arrs, shapes, n, start, finish)


def _chip_exchange(arrs, *, scatter):
    n = len(arrs)

    def copies(ins, outs, send_sems, recv_sems):
        x, y, c, j, sib, chips = _place()
        return [(_remote(ins[i].at[2 * cx + cy] if scatter else ins[i], outs[i].at[j], send_sems, recv_sems,
                         3 * i + k, (cx, cy, c)),
                 _remote(outs[i].at[2 * cx + cy], outs[i].at[2 * cx + cy], send_sems, recv_sems, 3 * i + k, (cx, cy, c)))
                for i in range(n) for k, (cx, cy) in enumerate(chips)]

    def start(*refs):
        for out, _ in copies(*refs):
            out.start()

    def finish(*refs):
        for out, back in copies(*refs):
            back.wait_recv()
            out.wait_send()

    shapes = [jax.ShapeDtypeStruct((N_CHIPS,) + a.shape[-2:], a.dtype) for a in arrs]
    return _Exchange(arrs, shapes, 3 * n, start, finish)


FLAT_ROWS = 512


def _pair_add(sharded, from_sib, *, name):
    n, _, R, L = sharded.shape
    tr = _tile(R, FLAT_ROWS, 16)

    def body(s_ref, b_ref, o_ref):
        own = jnp.where(lax.axis_index("c") == 0, s_ref[:, 0], s_ref[:, 1])
        o_ref[...] = (own.astype(F32) + b_ref[...].astype(F32)).astype(BF16)

    spec = pl.BlockSpec((n, tr, L), lambda i: (0, i, 0))
    return pl.pallas_call(
        body, name=name, grid=(R // tr,), in_specs=[pl.BlockSpec((n, 2, tr, L), lambda i: (0, 0, i, 0)), spec],
        out_specs=spec, out_shape=jax.ShapeDtypeStruct((n, R, L), BF16), compiler_params=_cparams(("parallel",)),
    )(sharded, from_sib)


def _chip_sum(arrived, pair, *, name):
    n, R, L = arrived.shape
    tr = _tile(R, FLAT_ROWS, 16)

    def body(a_ref, p_ref, o_ref):
        me = 2 * lax.axis_index("x") + lax.axis_index("y")
        acc = None
        for k in range(n):
            term = jnp.where(me == k, p_ref[k], a_ref[k]).astype(F32)
            acc = term if acc is None else acc + term
        o_ref[...] = acc

    spec = pl.BlockSpec((n, tr, L), lambda i: (0, i, 0))
    return pl.pallas_call(
        body, name=name, grid=(R // tr,), in_specs=[spec, spec], out_specs=pl.BlockSpec((tr, L), lambda i: (i, 0)),
        out_shape=jax.ShapeDtypeStruct((R, L), F32), compiler_params=_cparams(("parallel",)),
    )(arrived, pair)


def _sum_slots(buf, *, name):
    n, R, L = buf.shape
    tr = _tile(R, FLAT_ROWS, 16)

    def body(b_ref, o_ref):
        acc = b_ref[0].astype(F32)
        for k in range(1, n):
            acc = acc + b_ref[k].astype(F32)
        o_ref[...] = acc

    return pl.pallas_call(
        body, name=name, grid=(R // tr,), in_specs=[pl.BlockSpec((n, tr, L), lambda i: (0, i, 0))],
        out_specs=pl.BlockSpec((tr, L), lambda i: (i, 0)),
        out_shape=jax.ShapeDtypeStruct((R, L), F32), compiler_params=_cparams(("parallel",)),
    )(buf)


def _adamw_update(w, g, m, v):
    c1 = 1.0 - ADAM_B1 ** ADAM_STEP
    c2 = 1.0 - ADAM_B2 ** ADAM_STEP
    m = ADAM_B1 * m + (1.0 - ADAM_B1) * g
    v = ADAM_B2 * v + (1.0 - ADAM_B2) * (g * g)
    return -ADAM_LR * ((m / c1) / (jnp.sqrt(v / c2) + ADAM_EPS) + ADAM_WD * w), m, v


def _adamw_halves(w, m, v, own, other, *, name):
    NL, R, L = w.shape
    h = R // 2
    tr = _tile(h, FLAT_ROWS, 16)
    nt = h // tr

    def body(*refs):
        w_ref, m_ref, v_ref = refs[:3]
        own_refs, other_refs = refs[3:3 + NL], refs[3 + NL:3 + 2 * NL]
        d_ref, nm_ref, nv_ref, g_ref = refs[3 + 2 * NL:]
        layer, half = pl.program_id(0), pl.program_id(1)
        mine = half == lax.axis_index("c")
        g = jnp.where(mine, own_refs[0][...], other_refs[0][...])
        for l in range(1, NL):
            g = jnp.where(layer == l, jnp.where(mine, own_refs[l][...], other_refs[l][...]), g)
        d, mm, vv = _adamw_update(w_ref[0], g, m_ref[0], v_ref[0])
        d_ref[0], nm_ref[0], nv_ref[0], g_ref[0] = d, mm, vv, g

    spec = pl.BlockSpec((1, tr, L), lambda l, hh, i: (l, hh * nt + i, 0))
    part = pl.BlockSpec((tr, L), lambda l, hh, i: (i, 0))
    sh = jax.ShapeDtypeStruct((NL, R, L), F32)
    return pl.pallas_call(
        body, name=name, grid=(NL, 2, nt), in_specs=[spec] * 3 + [part] * (2 * NL), out_specs=[spec] * 4,
        out_shape=[sh] * 4, compiler_params=_cparams(("parallel", "parallel", "parallel")),
    )(w, m, v, *own, *other)


def _adamw(w, g, m, v, *, name):
    NL, R, L = w.shape
    tr = _tile(R, FLAT_ROWS, 16)

    def body(w_ref, g_ref, m_ref, v_ref, d_ref, nm_ref, nv_ref):
        d_ref[...], nm_ref[...], nv_ref[...] = _adamw_update(w_ref[...], g_ref[...], m_ref[...], v_ref[...])

    spec = pl.BlockSpec((1, tr, L), lambda l, i: (l, i, 0))
    sh = jax.ShapeDtypeStruct((NL, R, L), F32)
    return pl.pallas_call(
        body, name=name, grid=(NL, R // tr), in_specs=[spec] * 4, out_specs=[spec] * 3, out_shape=[sh] * 3,
        compiler_params=_cparams(("parallel", "parallel")),
    )(w, g, m, v)


WEIGHT_NAMES = ["ab_norm", "ab_w_in", "ab_q_norm", "ab_w_q_b", "ab_kv_norm", "ab_w_kv_b", "ab_conv_w", "ab_conv_b",
                "ab_w_rg_a", "ab_b_rg_a", "ab_w_rg_x", "ab_b_rg_x", "ab_lambda", "ab_w_out", "c_norm", "c_w_in",
                "c_ln_g", "c_ln_b", "c_w_s", "c_b_s", "c_w_out", "ffn_norm", "ffn_w_gate", "ffn_w_up", "ffn_conv_w",
                "ffn_conv_b", "ffn_w_down", "final_norm"]
BIG = {"ab_w_in": 2, "ab_w_q_b": 2, "ab_w_kv_b": 2, "ab_w_out": 1, "c_w_in": 2, "c_w_out": 1,
       "ffn_w_gate": 2, "ffn_w_up": 2, "ffn_w_down": 1}
SMALL_SHARDED = {"ab_conv_w": 2, "c_norm": 1, "c_ln_g": 1, "c_ln_b": 1, "ffn_conv_w": 2}
SMALL_REPLICATED = [n for n in WEIGHT_NAMES if n not in BIG and n not in SMALL_SHARDED]


def _rows(n_elems, mult):
    r = -(-n_elems // LANES)
    return -(-r // mult) * mult


def _flat(parts, rows):
    flat = jnp.concatenate([a.reshape(-1) for a in parts])
    return jnp.pad(flat, (0, rows * LANES - flat.shape[0])).reshape(rows, LANES)


def _unflat(flat, shapes):
    flat = flat.reshape(-1)
    out, off = [], 0
    for s in shapes:
        n = math.prod(s)
        out.append(flat[off:off + n].reshape(s))
        off += n
    return out


def _join_shards(a, axis):
    a = jnp.moveaxis(a, 0, axis)
    return a.reshape(a.shape[:axis] + (a.shape[axis] * a.shape[axis + 1],) + a.shape[axis + 2:])


def kernel(x, positions, ab_norm, ab_w_in, ab_q_norm, ab_w_q_b, ab_kv_norm, ab_w_kv_b, ab_conv_w, ab_conv_b, ab_w_rg_a, ab_b_rg_a, ab_w_rg_x, ab_b_rg_x, ab_lambda, ab_w_out, c_norm, c_w_in, c_ln_g, c_ln_b, c_w_s, c_b_s, c_w_out, ffn_norm, ffn_w_gate, ffn_w_up, ffn_conv_w, ffn_conv_b, ffn_w_down, final_norm, loss_target, m_ab_norm, m_ab_w_in, m_ab_q_norm, m_ab_w_q_b, m_ab_kv_norm, m_ab_w_kv_b, m_ab_conv_w, m_ab_conv_b, m_ab_w_rg_a, m_ab_b_rg_a, m_ab_w_rg_x, m_ab_b_rg_x, m_ab_lambda, m_ab_w_out, m_c_norm, m_c_w_in, m_c_ln_g, m_c_ln_b, m_c_w_s, m_c_b_s, m_c_w_out, m_ffn_norm, m_ffn_w_gate, m_ffn_w_up, m_ffn_conv_w, m_ffn_conv_b, m_ffn_w_down, m_final_norm, v_ab_norm, v_ab_w_in, v_ab_q_norm, v_ab_w_q_b, v_ab_kv_norm, v_ab_w_kv_b, v_ab_conv_w, v_ab_conv_b, v_ab_w_rg_a, v_ab_b_rg_a, v_ab_w_rg_x, v_ab_b_rg_x, v_ab_lambda, v_ab_w_out, v_c_norm, v_c_w_in, v_c_ln_g, v_c_ln_b, v_c_w_s, v_c_b_s, v_c_w_out, v_ffn_norm, v_ffn_w_gate, v_ffn_w_up, v_ffn_conv_w, v_ffn_conv_b, v_ffn_w_down, v_final_norm):
    given = dict(locals())
    w = {n: given[n] for n in WEIGHT_NAMES}
    m = {n: given["m_" + n] for n in WEIGHT_NAMES}
    v = {n: given["v_" + n] for n in WEIGHT_NAMES}
    c = lax.axis_index("c")
    chip = 2 * lax.axis_index("x") + lax.axis_index("y")

    halves = lambda a: a.reshape(a.shape[0], 2, a.shape[1] // 2, a.shape[2])
    tr = lambda a: jnp.swapaxes(a, 1, 2)
    send = {"ab_w_in": w["ab_w_in"], "ab_w_q_b": w["ab_w_q_b"], "ab_w_kv_b": w["ab_w_kv_b"], "ab_w_out": w["ab_w_out"],
            "c_w_in": tr(w["c_w_in"]), "c_w_out": w["c_w_out"], "ffn_w_gate": tr(w["ffn_w_gate"]),
            "ffn_w_up": tr(w["ffn_w_up"]), "ffn_w_down": w["ffn_w_down"]}
    small_rows = _rows(sum(w[n].size for n in SMALL_SHARDED), 16)
    small_sh = _flat([w[n] for n in SMALL_SHARDED], small_rows).reshape(1, 2, small_rows // 2, LANES)
    first_names = ["ab_w_in", "ab_w_q_b", "ab_w_kv_b"]
    mine = {n: halves(send[n].astype(BF16)) for n in BIG}

    def put_own(own, arrived):
        return arrived.reshape(arrived.shape[0], -1, arrived.shape[-1])

    p = {"ab_norm": w["ab_norm"], "ffn_gate_t": {}, "ffn_up_t": {}, "ffn_down": {}}
    first = [mine[n] for n in first_names] + [small_sh]

    def first_arrived(got):
        full = {n: put_own(o, a) for n, o, a in zip(first_names + ["small"], first, got)}
        unshard = lambda a: jnp.swapaxes(a.reshape(N_CHIPS, -1, a.shape[-1]), 0, 1).reshape(-1, N_CHIPS * a.shape[-1])
        p.update(_prep_big(unshard(full["ab_w_in"][0]), unshard(full["ab_w_q_b"][0]), unshard(full["ab_w_kv_b"][0])))
        small_full = dict(w)
        off = 0
        small_got = full["small"].reshape(N_CHIPS, -1)
        for n, ax in SMALL_SHARDED.items():
            seg = small_got[:, off:off + w[n].size].reshape((N_CHIPS,) + w[n].shape)
            small_full[n] = _join_shards(seg, ax)
            off += w[n].size
        p.update(_prep_small(small_full))

    def weights_ride(parts):
        def sink(arrived):
            for (own, setter), a in zip(parts, arrived):
                setter(put_own(own, a)[0])
        return _all_gather([own for own, _ in parts]), sink

    ffn_keys = {"ffn_gate_t": "ffn_w_gate", "ffn_up_t": "ffn_w_up", "ffn_down": "ffn_w_down"}
    ffn_part = lambda key, l: (mine[ffn_keys[key]][l:l + 1], functools.partial(p[key].__setitem__, l))
    rides = {
        "ab_norm": (_all_gather(first), first_arrived),
        "attn_fwd": weights_ride([(mine["ab_w_out"], functools.partial(p.__setitem__, "ab_w_out"))]
                                 + [ffn_part(key, 0) for key in ffn_keys]),
        "ffn0_gate": weights_ride([ffn_part("ffn_gate_t", 1)]),
        "ffn0_up": weights_ride([ffn_part("ffn_up_t", 1)]),
        "lru_fwd": weights_ride([(mine["c_w_in"], functools.partial(p.__setitem__, "c_w_in_t")),
                                 (mine["c_w_out"], functools.partial(p.__setitem__, "c_w_out"))]),
        "ffn0_down": weights_ride([ffn_part("ffn_down", 1)]),
    }

    def chip_sums(pair, arrived, tag):
        return [_chip_sum(a, b, name=f"grad_chip_sum_{tag}{i}") for i, (a, b) in enumerate(zip(arrived, pair))]

    half_of = {}

    def grads_ready(layer, ready):
        if layer == 1:
            named = {"gate1": ready["ffn_gate_t"], "up1": ready["ffn_up_t"], "down1": ready["ffn_down"]}
            hosts = {"sgu_bwd": ["down1"], "ffn0_dactbwd": ["gate1", "up1"]}
        else:
            named = {"c_in": ready["c_w_in_t"], "c_out": ready["c_w_out"], "gate0": ready["ffn_gate_t"],
                     "up0": ready["ffn_up_t"], "down0": ready["ffn_down"]}
            hosts = {"attn_bwd": ["c_in", "c_out", "down0", "gate0", "up0"]}
        tag = f"f{layer}"
        sharded = [a.reshape(N_CHIPS, 2, -1, a.shape[-1]) for a in named.values()]

        def paired(from_sib):
            pair = {k: _pair_add(a, b, name=f"grad_pair_add_{tag}{i}")
                    for i, (k, a, b) in enumerate(zip(named, sharded, from_sib))}
            for kernel_name, keys in hosts.items():
                def sink(arrived, keys=keys, kernel_name=kernel_name):
                    half_of.update(zip(keys, chip_sums([pair[k] for k in keys], arrived, f"{tag}_{kernel_name}")))
                rides[kernel_name] = (_chip_exchange([pair[k] for k in keys], scatter=True), sink)

        rides[f"ffn{layer}_dnorm"] = (_pair_swap(sharded), paired)

    loss_row, grad_x, g = _local_step(x, positions, loss_target, p, rides, grads_ready)

    cols = lambda a, n: jnp.swapaxes(a.reshape(a.shape[0], N_CHIPS, n), 0, 1)
    n_in, n_q, n_kv = w["ab_w_in"].shape[2], w["ab_w_q_b"].shape[2], w["ab_w_kv_b"].shape[2]
    small_names = SMALL_REPLICATED + list(SMALL_SHARDED)
    rs = _rows(sum(g[n].size for n in small_names) + LANES, FLAT_ROWS)
    small = _flat([loss_row] + [g[n] for n in small_names], rs)
    slot = (jnp.arange(2) == c)[:, None, None]
    last = [cols(_unperm_w_in(g["w_in_p"]), n_in), cols(_from_head_blocks(g["w_q_p"], QK_NOPE + QK_ROPE), n_q),
            cols(_join_kv(g["w_k_p"], g["w_v_p"]), n_kv), g["ab_w_out"]]
    last = [a.reshape(N_CHIPS, 2, -1, a.shape[-1]) for a in last]
    *from_sib, small_sib = _merge(_pair_swap(last), _pair_send([small])).run("tail_pair")
    pair = [_pair_add(a, b, name=f"grad_pair_add_b{i}") for i, (a, b) in enumerate(zip(last, from_sib))]
    pair_small = _sum_slots(jnp.where(slot, small[None], small_sib[None]), name="small_pair_sum")
    my_small = lax.dynamic_index_in_dim(pair_small.reshape(2, rs // 2, LANES), c, axis=0, keepdims=False)
    *arrived, all_small = _merge(_chip_exchange(pair, scatter=True), _chip_exchange([my_small], scatter=False)).run("tail_chip")
    half_of.update(zip(["in", "q", "kv", "out"], chip_sums(pair, arrived, "b")))
    half_of["small"] = _sum_slots(_put(all_small, my_small, chip, 0), name="small_chip_sum")
    keys = ("in", "q", "kv", "out", "c_in", "c_out", "gate0", "gate1", "up0", "up1", "down0", "down1", "small")
    other_half = dict(zip(keys, _pair_send([half_of[k] for k in keys]).run("grad_pair_share")))
    small_sum = jnp.where(slot, half_of["small"][None], other_half["small"][None]).reshape(rs, LANES)
    whole = lambda k: jnp.where(slot, half_of[k][None], other_half[k][None]).reshape(-1, half_of[k].shape[-1])
    grads_t = {"ab_w_in": whole("in").T[None], "ab_w_q_b": whole("q").T[None]}
    grads = {"ab_w_kv_b": whole("kv")[None], "c_w_in": whole("c_in").T[None], **{n: tr(a) for n, a in grads_t.items()}}
    by_halves = {"ab_w_out": (("out",), False), "c_w_out": (("c_out",), False), "ffn_w_down": (("down0", "down1"), False),
                 "ffn_w_gate": (("gate0", "gate1"), True), "ffn_w_up": (("up0", "up1"), True)}

    small_parts = _unflat(small_sum, [(1, LANES)] + [g[n].shape for n in small_names])
    loss = small_parts[0][0, 0]
    for n, a in zip(small_names, small_parts[1:]):
        if n in SMALL_SHARDED:
            ax = SMALL_SHARDED[n]
            a = lax.dynamic_slice_in_dim(a, chip * w[n].shape[ax], w[n].shape[ax], axis=ax)
        grads[n] = a.reshape(w[n].shape)

    delta, new_m, new_v = {}, {}, {}
    for n in BIG:
        if n in by_halves:
            ks, transposed = by_halves[n]
            view = tr if transposed else (lambda a: a)
            out = _adamw_halves(view(w[n]), view(m[n]), view(v[n]), [half_of[k] for k in ks], [other_half[k] for k in ks],
                                name=f"adamw_{n}")
            delta[n], new_m[n], new_v[n], grads[n] = (view(a) for a in out)
        elif n in grads_t:
            out = _adamw(tr(w[n]), grads_t[n], tr(m[n]), tr(v[n]), name=f"adamw_{n}")
            delta[n], new_m[n], new_v[n] = (tr(a) for a in out)
        else:
            delta[n], new_m[n], new_v[n] = _adamw(w[n], grads[n], m[n], v[n], name=f"adamw_{n}")
    small_all = [n for n in WEIGHT_NAMES if n not in BIG]
    ra = _rows(sum(w[n].size for n in small_all), FLAT_ROWS)
    pack = lambda d: _flat([d[n] for n in small_all], ra)[None]
    out = _adamw(pack(w), pack(grads), pack(m), pack(v), name="adamw_small")
    shapes = [w[n].shape for n in small_all]
    for d, flat in zip((delta, new_m, new_v), out):
        d.update(zip(small_all, _unflat(flat, shapes)))
    return (loss, grad_x, *[grads[n] for n in WEIGHT_NAMES], *[delta[n] for n in WEIGHT_NAMES],
            *[new_m[n] for n in WEIGHT_NAMES], *[new_v[n] for n in WEIGHT_NAMES])
```

```python
import functools
import math

import jax
import jax.numpy as jnp
from jax import lax
from jax.experimental import pallas as pl
from jax.experimental.pallas import tpu as pltpu

F32 = jnp.float32
BF16 = jnp.bfloat16
MESH = pl.DeviceIdType.MESH

D_MODEL = 1024
MLA_HEADS = 8
Q_LORA = 256
KV_LORA = 128
QK_NOPE = 64
QK_ROPE = 32
V_HEAD = 64
LRU_WIDTH = 512
LRU_HEADS = 8
LRU_BLOCK = 64
LRU_CONV = 4
LRU_C = 8.0
CHUNK = 128
SGU_GROUPS = 8
SGU_WIDTH = 1024
D_FF = 2816
FFN_CONV = 3
NORM_EPS = 1e-6
ROPE_BASE = 10000.0
ADAM_LR = 0.001
ADAM_B1 = 0.9
ADAM_B2 = 0.999
ADAM_EPS = 1e-08
ADAM_WD = 0.01
ADAM_STEP = 10

N_CHIPS = 4
LANES = 128
VMEM_LIMIT = 56 * 1024 * 1024
ROW_TILE = 256
SGU_TILE = 512
NORM_TILE = 1024
MM_TM, MM_TN, MM_TK = 1024, 1536, 2816
MM_TM_T, MM_TK_T = 1408, 2048
GELU_C = math.sqrt(2.0 / math.pi)


def _cparams(sem):
    return pltpu.CompilerParams(dimension_semantics=sem, vmem_limit_bytes=VMEM_LIMIT)


def _tile(n, target, mult=LANES):
    t = (min(n, target) // mult) * mult
    while t >= mult:
        if n % t == 0:
            return t
        t -= mult
    return n


GELU_K = GELU_C * 0.044715


def _gelu(x):
    t = jnp.tanh(x * (GELU_C + GELU_K * (x * x)))
    hx = 0.5 * x
    return hx + hx * t


def _gelu_and_grad(x):
    x2 = x * x
    t = jnp.tanh(x * (GELU_C + GELU_K * x2))
    hx = 0.5 * x
    dg = (0.5 + 0.5 * t) + (hx * (1.0 - t * t)) * (GELU_C + (3.0 * GELU_K) * x2)
    return hx + hx * t, dg


def _sigmoid(x):
    return 1.0 / (1.0 + jnp.exp(-x))


def _shift_rows(x, d, fill_rows):
    ext = jnp.concatenate([fill_rows, x], axis=0)
    return pltpu.roll(ext, d, 0)[8:]


def _shift_rows_up(x, d, fill_rows):
    n = x.shape[0]
    ext = jnp.concatenate([x, fill_rows], axis=0)
    return pltpu.roll(ext, n + 8 - d, 0)[:n]


def _dot(a, b, dims):
    return lax.dot_general(a.astype(BF16), b.astype(BF16), (dims, ((), ())), preferred_element_type=F32)


def _dot_nn(a, b):
    return _dot(a, b, ((1,), (0,)))


def _dot_nt(a, b):
    return _dot(a, b, ((1,), (1,)))


def _dot_tn(a, b):
    return _dot(a, b, ((0,), (0,)))


def _mm(a, b, *, name, ta=False, tb=False, res=None, out_dtype=F32, ride=None, also=None):
    if ta:
        K, M = a.shape
    else:
        M, K = a.shape
    N = b.shape[0] if tb else b.shape[1]
    tm = _tile(M, MM_TM_T if ta else (MM_TM if K <= MM_TM else MM_TM // 2), LANES if ta else 8)
    tn = _tile(N, MM_TN, LANES)
    tk = _tile(K, MM_TK_T if ta else MM_TK, LANES)
    nk = K // tk
    a_spec = pl.BlockSpec((tk, tm), lambda j, i, k: (k, i)) if ta else pl.BlockSpec((tm, tk), lambda j, i, k: (i, k))
    b_spec = pl.BlockSpec((tn, tk), lambda j, i, k: (j, k)) if tb else pl.BlockSpec((tk, tn), lambda j, i, k: (k, j))
    o_spec = pl.BlockSpec((tm, tn), lambda j, i, k: (i, j))
    dims = ((0,) if ta else (1,), (1,) if tb else (0,))
    has_res = res is not None
    pairs = [(a, b)] + ([also] if also is not None else [])
    n_ab = 2 * len(pairs)

    def body(*refs):
        r_ref = refs[n_ab] if has_res else None
        o_ref = refs[n_ab + 1] if has_res else refs[n_ab]
        p = _dot(refs[0][...], refs[1][...], dims)
        if also is not None:
            p = p + _dot(refs[2][...], refs[3][...], dims)

        def finish(r):
            if has_res:
                r = r + r_ref[...].astype(F32)
            o_ref[...] = r.astype(out_dtype)

        if nk == 1:
            finish(p)
            return
        acc_ref = refs[-1]
        k = pl.program_id(2)

        @pl.when(k == 0)
        def _():
            acc_ref[...] = p

        @pl.when(jnp.logical_and(k > 0, k < nk - 1))
        def _():
            acc_ref[...] += p

        @pl.when(k == nk - 1)
        def _():
            finish(acc_ref[...] + p)

    in_specs = [a_spec, b_spec] * len(pairs) + ([o_spec] if has_res else [])
    args = tuple(x for pair in pairs for x in pair) + ((res,) if has_res else ())
    return _pcall(
        body, name=name, grid=(N // tn, M // tm, nk), in_specs=in_specs, out_specs=[o_spec],
        out_shape=[jax.ShapeDtypeStruct((M, N), out_dtype)], args=args,
        scratch=[pltpu.VMEM((tm, tn), F32)] if nk > 1 else [], sem=("parallel", "parallel", "arbitrary"), ride=ride)[0]


def _rms_fwd(x, g, *, name, cb=0, out_dtype=BF16, ride=None):
    T = x.shape[0]
    W = g.shape[-1]
    g = g.reshape(1, W)
    tt = _tile(T, NORM_TILE, 16)

    def body(x_ref, g_ref, o_ref):
        xf = x_ref[...].astype(F32)
        rstd = lax.rsqrt(jnp.mean(xf * xf, axis=-1, keepdims=True) + NORM_EPS)
        o_ref[...] = (xf * rstd * g_ref[...]).astype(out_dtype)

    return _pcall(
        body, name=name, grid=(T // tt,),
        in_specs=[pl.BlockSpec((tt, W), lambda i: (i, cb)), pl.BlockSpec((1, W), lambda i: (0, 0))],
        out_specs=[pl.BlockSpec((tt, W), lambda i: (i, 0))], out_shape=[jax.ShapeDtypeStruct((T, W), out_dtype)],
        args=(x, g), sem=("parallel",), ride=ride)[0]


def _rms_bwd(x, g, dy, *, name, cb=0, res=None, out_dtype=F32, ride=None):
    T = x.shape[0]
    W = g.shape[-1]
    g = g.reshape(1, W)
    tt = _tile(T, NORM_TILE // 2, 16)
    has_res = res is not None

    def body(*refs):
        if has_res:
            x_ref, g_ref, dy_ref, r_ref, dx_ref, dg_ref = refs
        else:
            x_ref, g_ref, dy_ref, dx_ref, dg_ref = refs
        xf = x_ref[...].astype(F32)
        dyf = dy_ref[...].astype(F32)
        rstd = lax.rsqrt(jnp.mean(xf * xf, axis=-1, keepdims=True) + NORM_EPS)
        xhat = xf * rstd
        dxhat = dyf * g_ref[...]
        dx = rstd * (dxhat - xhat * jnp.mean(dxhat * xhat, axis=-1, keepdims=True))
        if has_res:
            dx = dx + r_ref[...].astype(F32)
        dx_ref[...] = dx.astype(out_dtype)
        part = jnp.sum(dyf * xhat, axis=0, keepdims=True)

        @pl.when(pl.program_id(0) == 0)
        def _():
            dg_ref[...] = part

        @pl.when(pl.program_id(0) > 0)
        def _():
            dg_ref[...] += part

    row = pl.BlockSpec((tt, W), lambda i: (i, 0))
    in_specs = [pl.BlockSpec((tt, W), lambda i: (i, cb)), pl.BlockSpec((1, W), lambda i: (0, 0)), row]
    args = (x, g, dy)
    if has_res:
        in_specs.append(row)
        args = args + (res,)
    return _pcall(
        body, name=name, grid=(T // tt,), in_specs=in_specs,
        out_specs=[row, pl.BlockSpec((1, W), lambda i: (0, 0))],
        out_shape=[jax.ShapeDtypeStruct((T, W), out_dtype), jax.ShapeDtypeStruct((1, W), F32)], args=args, ride=ride)


def _final_fwd_bwd(h, g, target, *, name):
    T, W = h.shape
    g = g.reshape(1, W)
    tt = _tile(T, NORM_TILE, 16)

    def body(x_ref, g_ref, t_ref, loss_ref, dx_ref, dg_ref):
        xf = x_ref[...]
        rstd = lax.rsqrt(jnp.mean(xf * xf, axis=-1, keepdims=True) + NORM_EPS)
        xhat = xf * rstd
        err = xhat * g_ref[...] - t_ref[...]
        lpart = jnp.zeros((1, LANES), F32) + (0.5 / W) * jnp.sum(err * err)
        dyf = err * (1.0 / W)
        dxhat = dyf * g_ref[...]
        dx_ref[...] = rstd * (dxhat - xhat * jnp.mean(dxhat * xhat, axis=-1, keepdims=True))
        part = jnp.sum(dyf * xhat, axis=0, keepdims=True)

        @pl.when(pl.program_id(0) == 0)
        def _():
            dg_ref[...] = part
            loss_ref[...] = lpart

        @pl.when(pl.program_id(0) > 0)
        def _():
            dg_ref[...] += part
            loss_ref[...] += lpart

    row = pl.BlockSpec((tt, W), lambda i: (i, 0))
    return pl.pallas_call(
        body, name=name, grid=(T // tt,),
        in_specs=[row, pl.BlockSpec((1, W), lambda i: (0, 0)), row],
        out_specs=[pl.BlockSpec((1, LANES), lambda i: (0, 0)), row, pl.BlockSpec((1, W), lambda i: (0, 0))],
        out_shape=[jax.ShapeDtypeStruct((1, LANES), F32), jax.ShapeDtypeStruct((T, W), F32),
                   jax.ShapeDtypeStruct((1, W), F32)],
        compiler_params=_cparams(("arbitrary",)),
    )(h, g, target)


def _swap16(x):
    lane = lax.broadcasted_iota(jnp.int32, x.shape, 1)
    return jnp.where((lane % 32) < 16, pltpu.roll(x, LANES - 16, 1), pltpu.roll(x, 16, 1))


def _rope(x, c, s):
    return x * c + _swap16(x) * s


def _rope_t(d, c, s):
    return d * c + _swap16(d * s)


def _head_block_map(fn, x, cos, sin, *, name):
    T, W = x.shape
    tt = _tile(T, NORM_TILE, 16)

    def body(x_ref, c_ref, s_ref, o_ref):
        c, s = c_ref[...], s_ref[...]
        for h in range(W // LANES):
            lanes = slice(h * LANES, (h + 1) * LANES)
            o_ref[:, lanes] = fn(x_ref[:, lanes], c, s).astype(BF16)

    tab = pl.BlockSpec((tt, LANES), lambda i: (i, 0))
    blk = pl.BlockSpec((tt, W), lambda i: (i, 0))
    return pl.pallas_call(
        body, name=name, grid=(T // tt,), in_specs=[blk, tab, tab], out_specs=blk,
        out_shape=jax.ShapeDtypeStruct((T, W), BF16), compiler_params=_cparams(("parallel",)),
    )(x, cos, sin)


def _rope_q(q, cos, sin, *, name):
    scale = _attn_scale()
    return _head_block_map(lambda x, c, s: _rope(x, c, s) * scale, q, cos, sin, name=name)


def _rope_q_bwd(dq, cos, sin, *, name):
    return _head_block_map(_rope_t, dq, cos, sin, name=name)


def _key_blocks(kv, z, cos, sin, *, kpe_block, name):
    T = kv.shape[0]
    tt = _tile(T, NORM_TILE, 16)
    W = MLA_HEADS * LANES

    def body(kv_ref, z_ref, c_ref, s_ref, o_ref):
        kr = _rope(z_ref[...], c_ref[...], s_ref[...])
        for h in range(MLA_HEADS):
            lanes = slice(h * LANES, (h + 1) * LANES)
            o_ref[:, lanes] = (kv_ref[:, lanes].astype(F32) + kr).astype(BF16)

    tab = pl.BlockSpec((tt, LANES), lambda i: (i, 0))
    blk = pl.BlockSpec((tt, W), lambda i: (i, 0))
    return pl.pallas_call(
        body, name=name, grid=(T // tt,),
        in_specs=[blk, pl.BlockSpec((tt, LANES), lambda i: (i, kpe_block)), tab, tab], out_specs=blk,
        out_shape=jax.ShapeDtypeStruct((T, W), BF16), compiler_params=_cparams(("parallel",)),
    )(kv, z, cos, sin)


def _key_rope_bwd(dk, cos, sin, *, name):
    T = dk.shape[0]
    tt = _tile(T, NORM_TILE, 16)

    def body(d_ref, c_ref, s_ref, o_ref):
        d = d_ref[:, :LANES]
        for h in range(1, MLA_HEADS):
            d = d + d_ref[:, h * LANES:(h + 1) * LANES]
        lane = lax.broadcasted_iota(jnp.int32, d.shape, 1)
        d = jnp.where(jnp.logical_and(lane >= QK_NOPE, lane < QK_NOPE + QK_ROPE), d, 0.0)
        o_ref[...] = _rope_t(d, c_ref[...], s_ref[...]).astype(BF16)

    tab = pl.BlockSpec((tt, LANES), lambda i: (i, 0))
    return pl.pallas_call(
        body, name=name, grid=(T // tt,),
        in_specs=[pl.BlockSpec((tt, MLA_HEADS * LANES), lambda i: (i, 0)), tab, tab], out_specs=tab,
        out_shape=jax.ShapeDtypeStruct((T, LANES), BF16), compiler_params=_cparams(("parallel",)),
    )(dk, cos, sin)


ATT_BLOCK = 512


def _attn_scale():
    return float((QK_NOPE + QK_ROPE) ** -0.5)


def _causal_mask(qi, kj, tq, tk):
    row = qi * tq + lax.broadcasted_iota(jnp.int32, (tq, tk), 0)
    col = kj * tk + lax.broadcasted_iota(jnp.int32, (tq, tk), 1)
    return col <= row


def _pcall(body, *, name, grid, in_specs, out_specs, out_shape, args, scratch=(), sem=None, ride=None):
    n_in, n_out, n_scr = len(args), len(out_shape), len(scratch)
    if ride is None:
        return pl.pallas_call(
            body, name=name, grid=grid, in_specs=list(in_specs), out_specs=list(out_specs), out_shape=list(out_shape),
            scratch_shapes=list(scratch), compiler_params=_cparams(sem or ("arbitrary",) * len(grid)))(*args)
    ex, sink = ride
    o0 = n_in + len(ex.arrs)
    s0 = o0 + n_out + len(ex.out_shapes)

    def hosted(*refs):
        parts = (refs[n_in:o0], refs[o0 + n_out:s0], refs[-2], refs[-1])
        ids = [pl.program_id(i) for i in range(len(grid))]
        pl.when(functools.reduce(jnp.logical_and, [i == 0 for i in ids]))(lambda: ex.start(*parts))
        body(*refs[:n_in], *refs[o0:o0 + n_out], *refs[s0:s0 + n_scr])
        pl.when(functools.reduce(jnp.logical_and, [i == n - 1 for i, n in zip(ids, grid)]))(lambda: ex.finish(*parts))

    outs = pl.pallas_call(
        hosted, name=name, grid=grid, in_specs=list(in_specs) + ex.in_specs, out_specs=list(out_specs) + ex.out_specs,
        out_shape=list(out_shape) + ex.out_shapes, scratch_shapes=list(scratch) + ex.scratch,
        compiler_params=_cparams(("arbitrary",) * len(grid)))(*args, *ex.arrs)
    sink(outs[n_out:])
    return outs[:n_out]


PAIRS = MLA_HEADS // 2


def _own_lanes(x, first):
    lane = lax.broadcasted_iota(jnp.int32, x.shape, 1)
    return jnp.where((lane < V_HEAD) if first else (lane >= V_HEAD), x, 0.0)


def _lane_sums_as_row(x):
    hi = x.astype(BF16)
    lo = (x - hi.astype(F32)).astype(BF16)
    ones = jnp.ones((8, LANES), BF16)
    return (_dot_nt(ones, hi) + _dot_nt(ones, lo))[0:1, :]


def _attn_fwd(q, k, kv, *, B, S, v_block0, name, ride=None):
    tq = tk = min(ATT_BLOCK, S)
    nq = S // tq
    T = B * S

    def body(q_ref, k_ref, v_ref, o_ref, lse_ref):
        qi = pl.program_id(2)
        qs = (q_ref[:, :LANES], q_ref[:, LANES:])

        def step(masked):
            def f(j, carry):
                rows = pl.ds(pl.multiple_of(j * tk, tk), tk)
                vb = v_ref[rows, :]
                out = []
                for h in range(2):
                    m, l, acc = carry[h]
                    s = _dot_nt(qs[h], k_ref[rows, h * LANES:(h + 1) * LANES])
                    if masked:
                        s = jnp.where(_causal_mask(qi, j, tq, tk), s, -jnp.inf)
                    m_new = jnp.maximum(m, jnp.max(s, axis=-1, keepdims=True))
                    alpha = jnp.exp(m - m_new)
                    p = jnp.exp(s - m_new)
                    out.append((m_new, alpha * l + jnp.sum(p, axis=-1, keepdims=True), alpha * acc + _dot_nn(p, vb)))
                return tuple(out)
            return f

        one = (jnp.full((tq, 1), -1e30, F32), jnp.zeros((tq, 1), F32), jnp.zeros((tq, LANES), F32))
        (ma, la, acca), (mb, lb, accb) = step(True)(qi, lax.fori_loop(0, qi, step(False), (one, one)))
        o_ref[...] = _own_lanes(acca / la, True) + _own_lanes(accb / lb, False)
        for h, lse in enumerate((ma + jnp.log(la), mb + jnp.log(lb))):
            lse_ref[0, h, pl.ds(qi, 1), :] = _lane_sums_as_row(jnp.broadcast_to(lse * (1.0 / LANES), (tq, LANES)))

    return _pcall(
        body, name=name, grid=(B, PAIRS, nq),
        in_specs=[pl.BlockSpec((tq, 2 * LANES), lambda b, g, i: (b * nq + i, g)),
                  pl.BlockSpec((S, 2 * LANES), lambda b, g, i: (b, g)),
                  pl.BlockSpec((S, LANES), lambda b, g, i: (b, v_block0 + g))],
        out_specs=[pl.BlockSpec((tq, LANES), lambda b, g, i: (b * nq + i, g)),
                   pl.BlockSpec((1, 2, nq, tq), lambda b, g, i: (b, g, 0, 0))],
        out_shape=[jax.ShapeDtypeStruct((T, PAIRS * LANES), F32), jax.ShapeDtypeStruct((B, MLA_HEADS, nq, tq), F32)],
        args=(q, k, kv), ride=ride)


def _attn_bwd(q, k, kv, o, lse_rows, do, *, B, S, v_block0, name, ride=None):
    tq = tk = min(ATT_BLOCK, S)
    nq = S // tq
    T = B * S
    scale = _attn_scale()

    def body(q_ref, k_ref, v_ref, o_ref, lse_ref, do_ref, dk_ref, dv_ref, dq_ref, delta_ref):
        kj = pl.program_id(2)
        ks = (k_ref[:, :LANES], k_ref[:, LANES:])
        vb = v_ref[...]

        @pl.when(kj == 0)
        def _():
            dq_ref[...] = jnp.zeros_like(dq_ref)
            for i in range(nq):
                prod = do_ref[i * tq:(i + 1) * tq, :] * o_ref[i * tq:(i + 1) * tq, :]
                for h in range(2):
                    delta_ref[h, i:i + 1, :] = _lane_sums_as_row(_own_lanes(prod, h == 0))

        def step(masked):
            def f(i, carry):
                rows = pl.ds(pl.multiple_of(i * tq, tq), tq)
                do_b = do_ref[rows, :]
                dks, dv = list(carry[:2]), carry[2]
                for h in range(2):
                    qb = q_ref[rows, h * LANES:(h + 1) * LANES]
                    doh = _own_lanes(do_b, h == 0)
                    pt = jnp.exp(_dot_nt(ks[h], qb) - lse_ref[0, h, pl.ds(i, 1), :])
                    if masked:
                        krow = kj * tk + lax.broadcasted_iota(jnp.int32, (tk, tq), 0)
                        qcol = i * tq + lax.broadcasted_iota(jnp.int32, (tk, tq), 1)
                        pt = jnp.where(krow <= qcol, pt, 0.0)
                    dst = pt * (_dot_nt(vb, doh) - delta_ref[h, pl.ds(i, 1), :])
                    dks[h] = dks[h] + _dot_nn(dst, qb)
                    dv = dv + _dot_nn(pt, doh)
                    dq_ref[rows, h * LANES:(h + 1) * LANES] += _dot_tn(dst, ks[h]) * scale
                return dks[0], dks[1], dv
            return f

        zero = jnp.zeros((tk, LANES), F32)
        dka, dkb, dv = lax.fori_loop(kj + 1, nq, step(False), step(True)(kj, (zero, zero, zero)))
        dk_ref[:, :LANES] = dka
        dk_ref[:, LANES:] = dkb
        dv_ref[...] = dv

    krow = lambda w, c0: pl.BlockSpec((tk, w), lambda b, g, j: (b * nq + j, c0 + g))
    seq = lambda w: pl.BlockSpec((S, w), lambda b, g, j: (b, g))
    stat = pl.BlockSpec((1, 2, nq, tq), lambda b, g, j: (b, g, 0, 0))
    dk, dv, dq = _pcall(
        body, name=name, grid=(B, PAIRS, nq),
        in_specs=[seq(2 * LANES), krow(2 * LANES, 0), krow(LANES, v_block0), seq(LANES), stat, seq(LANES)],
        out_specs=[krow(2 * LANES, 0), krow(LANES, 0), seq(2 * LANES)],
        out_shape=[jax.ShapeDtypeStruct((T, MLA_HEADS * LANES), F32), jax.ShapeDtypeStruct((T, PAIRS * LANES), F32),
                   jax.ShapeDtypeStruct((T, MLA_HEADS * LANES), F32)],
        args=(q, k, kv, o, lse_rows, do), scratch=[pltpu.VMEM((2, nq, tq), F32)], ride=ride)
    return dq, dk, dv


def _lru_gates(xl, halo, cw_ref, cb_ref, wa_ref, ba_ref, wx_ref, bx_ref, lam_ref):
    xc = cb_ref[...] + cw_ref[3:4, :] * xl
    for kk in range(LRU_CONV - 1):
        xc = xc + cw_ref[kk:kk + 1, :] * _shift_rows(xl, LRU_CONV - 1 - kk, halo)
    r = _sigmoid(_dot_nn(xc, wa_ref[...]) + ba_ref[...])
    i = _sigmoid(_dot_nn(xc, wx_ref[...]) + bx_ref[...])
    lam = lam_ref[...]
    sp = jnp.maximum(-lam, 0.0) + jnp.log(1.0 + jnp.exp(-jnp.abs(lam)))
    a = jnp.exp(-LRU_C * r * sp)
    mult = jnp.sqrt(1.0 - a * a)
    return xc, r, i, sp, a, mult


def _lru_specs(tt, nt, S):
    def make(rev):
        tmap = (lambda t: nt - 1 - t) if rev else (lambda t: t)
        tile = lambda cb: pl.BlockSpec((tt, LRU_WIDTH), lambda b, t: (b * nt + tmap(t), cb))
        prev8 = lambda cb: pl.BlockSpec(
            (8, LRU_WIDTH), lambda b, t: (jnp.maximum((b * nt + tmap(t)) * (tt // 8) - 1, 0), cb))
        return tile, prev8, tmap
    return make


def _lru_fwd(z, cw, cb, wa, ba, wx, bx, lam, *, S, name, ride=None):
    T = z.shape[0]
    tt = min(ROW_TILE, S)
    nt = S // tt
    tile, prev8, _ = _lru_specs(tt, nt, S)(False)
    vec = lambda r: pl.BlockSpec((r, LRU_WIDTH), lambda b, t: (0, 0))
    mat = pl.BlockSpec((LRU_WIDTH, LRU_WIDTH), lambda b, t: (0, 0))

    def body(xl_ref, halo_ref, gate_ref, cw_ref, cb_ref, wa_ref, ba_ref, wx_ref, bx_ref, lam_ref,
             y_ref, h_ref, carry_ref):
        t = pl.program_id(1)
        first = t == 0
        halo = jnp.where(first, 0.0, halo_ref[...])
        xl_t = xl_ref[...]
        xc, r, i, sp, a, mult = _lru_gates(xl_t, halo, cw_ref, cb_ref, wa_ref, ba_ref, wx_ref, bx_ref, lam_ref)
        bv = mult * (i * xc)
        ones = jnp.ones((8, LRU_WIDTH), F32)
        zeros = jnp.zeros((8, LRU_WIDTH), F32)
        row = lax.broadcasted_iota(jnp.int32, (tt, LRU_WIDTH), 0)
        A = a
        d = 1
        while d < tt:
            if d < 8:
                a_sh = _shift_rows(A, d, ones)
                b_sh = _shift_rows(bv, d, zeros)
            else:
                a_sh = jnp.where(row < d, 1.0, pltpu.roll(A, d, 0))
                b_sh = jnp.where(row < d, 0.0, pltpu.roll(bv, d, 0))
            bv = A * b_sh + bv
            A = A * a_sh
            d *= 2
        h0 = jnp.where(first, 0.0, carry_ref[0:1, :])
        h = A * h0 + bv
        carry_ref[...] = jnp.broadcast_to(h[tt - 1:tt, :], (8, LRU_WIDTH))
        h_ref[...] = h
        y_ref[...] = (h * _gelu(gate_ref[...])).astype(BF16)

    return _pcall(
        body, name=name, grid=(T // S, nt),
        in_specs=[tile(0), prev8(0), tile(1), vec(LRU_CONV), vec(1), mat, vec(1), mat, vec(1), vec(1)],
        out_specs=[tile(0), tile(0)],
        out_shape=[jax.ShapeDtypeStruct((T, LRU_WIDTH), BF16), jax.ShapeDtypeStruct((T, LRU_WIDTH), F32)],
        args=(z, z, z, cw, cb, wa, ba, wx, bx, lam), scratch=[pltpu.VMEM((8, LRU_WIDTH), F32)], ride=ride)


def _lru_bwd(z, h, dy, cw, cb, wa, ba, wx, bx, lam, *, S, name):
    T = z.shape[0]
    tt = min(ROW_TILE, S)
    nt = S // tt
    tile, prev8, tmap = _lru_specs(tt, nt, S)(True)
    vec = lambda r: pl.BlockSpec((r, LRU_WIDTH), lambda b, t: (0, 0))
    mat = pl.BlockSpec((LRU_WIDTH, LRU_WIDTH), lambda b, t: (0, 0))

    def body(xl_ref, halo_ref, gate_ref, h_ref, hprev_ref, dy_ref, cw_ref, cb_ref, wa_ref, ba_ref, wx_ref,
             bx_ref, lam_ref, dxl_ref, dgate_ref, dcw_ref, dcb_ref, dwa_ref, dba_ref, dwx_ref, dbx_ref,
             dlam_ref, lamc_ref, ac_ref, dxc_ref):
        b = pl.program_id(0)
        t = pl.program_id(1)
        tr = nt - 1 - t
        seq_first = tr == 0
        seq_last = t == 0
        halo = jnp.where(seq_first, 0.0, halo_ref[...])
        xl_t = xl_ref[...]
        xc, r, i, sp, a, mult = _lru_gates(xl_t, halo, cw_ref, cb_ref, wa_ref, ba_ref, wx_ref, bx_ref, lam_ref)
        hh = h_ref[...]
        dyf = dy_ref[...].astype(F32)
        gl, dgl = _gelu_and_grad(gate_ref[...])
        dgate_ref[...] = (dyf * hh * dgl).astype(BF16)
        dh = dyf * gl

        a_first_later = jnp.where(seq_last, 0.0, ac_ref[...])
        lam_later = jnp.where(seq_last, 0.0, lamc_ref[...])
        row = lax.broadcasted_iota(jnp.int32, (tt, LRU_WIDTH), 0)
        A = _shift_rows_up(a, 1, a_first_later)
        lm = dh
        ones = jnp.ones((8, LRU_WIDTH), F32)
        zeros = jnp.zeros((8, LRU_WIDTH), F32)
        d = 1
        while d < tt:
            if d < 8:
                a_sh = _shift_rows_up(A, d, ones)
                l_sh = _shift_rows_up(lm, d, zeros)
            else:
                a_sh = jnp.where(row >= tt - d, 1.0, pltpu.roll(A, tt - d, 0))
                l_sh = jnp.where(row >= tt - d, 0.0, pltpu.roll(lm, tt - d, 0))
            lm = lm + A * l_sh
            A = A * a_sh
            d *= 2
        lm = lm + A * lam_later[0:1, :]
        lamc_ref[...] = jnp.broadcast_to(lm[0:1, :], (8, LRU_WIDTH))
        ac_ref[...] = jnp.broadcast_to(a[0:1, :], (8, LRU_WIDTH))

        hprev_halo = jnp.where(seq_first, 0.0, hprev_ref[...])
        h_prev = _shift_rows(hh, 1, hprev_halo)
        da = lm * h_prev
        ixc = i * xc
        dmult = lm * ixc
        di = lm * mult * xc
        dxc = lm * mult * i
        da = da - dmult * a / mult
        dlog = da * a
        dr = dlog * (-LRU_C) * sp
        dsp_part = jnp.sum(dlog * (-LRU_C) * r, axis=0, keepdims=True)
        dpa = dr * r * (1.0 - r)
        dpx = di * i * (1.0 - i)
        dxc = dxc + _dot_nt(dpa, wa_ref[...]) + _dot_nt(dpx, wx_ref[...])
        dwa_part = _dot_tn(xc, dpa)
        dwx_part = _dot_tn(xc, dpx)

        later = jnp.where(seq_last, 0.0, dxc_ref[...])
        dxl = cw_ref[3:4, :] * dxc
        for kk in range(LRU_CONV - 1):
            dxl = dxl + cw_ref[kk:kk + 1, :] * _shift_rows_up(dxc, LRU_CONV - 1 - kk, later)
        dxl_ref[...] = dxl.astype(BF16)
        dxc_ref[...] = dxc[0:8, :]
        dcw_rows = [jnp.sum(dxc * _shift_rows(xl_t, LRU_CONV - 1 - kk, halo), axis=0, keepdims=True)
                    for kk in range(LRU_CONV - 1)]
        dcw_rows.append(jnp.sum(dxc * xl_t, axis=0, keepdims=True))
        dcw_part = jnp.concatenate(dcw_rows + [jnp.zeros((8 - LRU_CONV, LRU_WIDTH), F32)], axis=0)
        lamv = lam_ref[...]
        dlam_part = dsp_part * (-_sigmoid(-lamv))
        parts = ((dcw_ref, dcw_part), (dcb_ref, jnp.sum(dxc, axis=0, keepdims=True)),
                 (dwa_ref, dwa_part), (dba_ref, jnp.sum(dpa, axis=0, keepdims=True)),
                 (dwx_ref, dwx_part), (dbx_ref, jnp.sum(dpx, axis=0, keepdims=True)),
                 (dlam_ref, dlam_part))
        start = jnp.logical_and(b == 0, t == 0)

        @pl.when(start)
        def _():
            for ref, val in parts:
                ref[...] = val

        @pl.when(jnp.logical_not(start))
        def _():
            for ref, val in parts:
                ref[...] += val

    acc = lambda r: pl.BlockSpec((r, LRU_WIDTH), lambda b, t: (0, 0))
    return pl.pallas_call(
        body, name=name, grid=(T // S, nt),
        in_specs=[tile(0), prev8(0), tile(1), tile(0), prev8(0), tile(0),
                  vec(LRU_CONV), vec(1), mat, vec(1), mat, vec(1), vec(1)],
        out_specs=[tile(0), tile(0), acc(8), acc(1), mat, acc(1), mat, acc(1), acc(1)],
        out_shape=[jax.ShapeDtypeStruct((T, LRU_WIDTH), BF16), jax.ShapeDtypeStruct((T, LRU_WIDTH), BF16),
                   jax.ShapeDtypeStruct((8, LRU_WIDTH), F32), jax.ShapeDtypeStruct((1, LRU_WIDTH), F32),
                   jax.ShapeDtypeStruct((LRU_WIDTH, LRU_WIDTH), F32), jax.ShapeDtypeStruct((1, LRU_WIDTH), F32),
                   jax.ShapeDtypeStruct((LRU_WIDTH, LRU_WIDTH), F32), jax.ShapeDtypeStruct((1, LRU_WIDTH), F32),
                   jax.ShapeDtypeStruct((1, LRU_WIDTH), F32)],
        scratch_shapes=[pltpu.VMEM((8, LRU_WIDTH), F32), pltpu.VMEM((8, LRU_WIDTH), F32),
                        pltpu.VMEM((8, LRU_WIDTH), F32)],
        compiler_params=_cparams(("arbitrary", "arbitrary")),
    )(z, z, z, h, h, dy, cw, cb, wa, ba, wx, bx, lam)


FFN_CT = 1408
FFN_TILE = 512


def _ffn_conv(g, halo, cw, cb):
    gc = cb + cw[2:3, :] * g
    for kk in range(FFN_CONV - 1):
        gc = gc + cw[kk:kk + 1, :] * _shift_rows(g, FFN_CONV - 1 - kk, halo)
    return gc


def _row_chunks(rows, chunk):
    return [slice(r0, min(r0 + chunk, rows)) for r0 in range(0, rows, chunk)]


FFN_CHUNK = 128
HALO = 16


def _ffn_act_down(g, u, cw, cb, w_down, res, *, S, name, ride=None):
    T, F = g.shape
    D = w_down.shape[1]
    tt = min(2 * FFN_TILE, S)
    nt = S // tt
    tc = _tile(F, FFN_CT)
    nj = F // tc

    def body(g_ref, halo_ref, u_ref, cw_ref, cb_ref, w_ref, r_ref, o_ref, act_ref):
        j = pl.program_id(1)
        first = (pl.program_id(0) % nt) == 0
        cw, cb = cw_ref[...], cb_ref[...]

        @pl.when(j == 0)
        def _():
            o_ref[...] = r_ref[...]

        for r in _row_chunks(tt, FFN_CHUNK):
            before = halo_ref[...] if r.start == 0 else g_ref[r.start - HALO:r.start, :]
            halo = before.astype(F32)[HALO - 8:]
            if r.start == 0:
                halo = jnp.where(first, 0.0, halo)
            gc = _ffn_conv(g_ref[r, :].astype(F32), halo, cw, cb)
            act = (_gelu(gc) * u_ref[r, :].astype(F32)).astype(BF16)
            act_ref[r, :] = act
            o_ref[r, :] += _dot_nn(act, w_ref[...])

    tile = pl.BlockSpec((tt, tc), lambda i, j: (i, j))
    prev = pl.BlockSpec((HALO, tc), lambda i, j: (jnp.maximum(i * (tt // HALO) - 1, 0), j))
    rows = pl.BlockSpec((tt, D), lambda i, j: (i, 0))
    return _pcall(
        body, name=name, grid=(T // tt, nj),
        in_specs=[tile, prev, tile, pl.BlockSpec((FFN_CONV, tc), lambda i, j: (0, j)),
                  pl.BlockSpec((1, tc), lambda i, j: (0, j)), pl.BlockSpec((tc, D), lambda i, j: (j, 0)), rows],
        out_specs=[rows, tile], out_shape=[jax.ShapeDtypeStruct((T, D), F32), jax.ShapeDtypeStruct((T, F), BF16)],
        args=(g, g, u, cw, cb, w_down, res), sem=("parallel", "arbitrary"), ride=ride)


def _ffn_act_bwd(g, u, dh, w_down, cw, cb, *, S, name, ride=None):
    T, F = g.shape
    D = w_down.shape[1]
    tt = min(FFN_TILE, S)
    nt = S // tt
    ntt = T // tt
    tc = _tile(F, FFN_CT)

    def body(g_ref, halo_ref, u_ref, dh_ref, w_ref, cw_ref, cb_ref, dg_ref, du_ref, dcw_ref, dcb_ref, later_ref):
        step = pl.program_id(1)
        ti = (ntt - 1 - step) % nt
        cw, cb = cw_ref[...], cb_ref[...]

        @pl.when(step == 0)
        def _():
            dcw_ref[...] = jnp.zeros_like(dcw_ref)
            dcb_ref[...] = jnp.zeros_like(dcb_ref)

        halo = jnp.where(ti == 0, 0.0, halo_ref[...].astype(F32)[HALO - 8:])
        gt = g_ref[...].astype(F32)
        gl, dgl = _gelu_and_grad(_ffn_conv(gt, halo, cw, cb))
        da = _dot_nt(dh_ref[...], w_ref[...])
        du_ref[...] = (da * gl).astype(BF16)
        dgc = da * u_ref[...].astype(F32) * dgl
        later = jnp.where(ti == nt - 1, 0.0, later_ref[...])
        dg = cw[2:3, :] * dgc
        for kk in range(FFN_CONV - 1):
            dg = dg + cw[kk:kk + 1, :] * _shift_rows_up(dgc, FFN_CONV - 1 - kk, later)
        dg_ref[...] = dg.astype(BF16)
        later_ref[...] = dgc[0:8, :]
        rows = [jnp.sum(dgc * _shift_rows(gt, FFN_CONV - 1 - kk, halo), axis=0, keepdims=True)
                for kk in range(FFN_CONV - 1)]
        rows.append(jnp.sum(dgc * gt, axis=0, keepdims=True))
        dcw_ref[...] += jnp.concatenate(rows + [jnp.zeros((8 - FFN_CONV, tc), F32)], axis=0)
        dcb_ref[...] += jnp.sum(dgc, axis=0, keepdims=True)

    tile = pl.BlockSpec((tt, tc), lambda j, s: (ntt - 1 - s, j))
    prev = pl.BlockSpec((HALO, tc), lambda j, s: (jnp.maximum((ntt - 1 - s) * (tt // HALO) - 1, 0), j))
    return _pcall(
        body, name=name, grid=(F // tc, ntt),
        in_specs=[tile, prev, tile, pl.BlockSpec((tt, D), lambda j, s: (ntt - 1 - s, 0)),
                  pl.BlockSpec((tc, D), lambda j, s: (j, 0)), pl.BlockSpec((FFN_CONV, tc), lambda j, s: (0, j)),
                  pl.BlockSpec((1, tc), lambda j, s: (0, j))],
        out_specs=[tile, tile, pl.BlockSpec((8, tc), lambda j, s: (0, j)), pl.BlockSpec((1, tc), lambda j, s: (0, j))],
        out_shape=[jax.ShapeDtypeStruct((T, F), BF16), jax.ShapeDtypeStruct((T, F), BF16),
                   jax.ShapeDtypeStruct((8, F), F32), jax.ShapeDtypeStruct((1, F), F32)],
        args=(g, g, u, dh, w_down, cw, cb), scratch=[pltpu.VMEM((8, tc), F32)], ride=ride)


def _sgu_norm(zv, g_ref, b_ref):
    v = _gelu(zv)
    mu = jnp.mean(v, axis=-1, keepdims=True)
    xc = v - mu
    rstd = lax.rsqrt(jnp.mean(xc * xc, axis=-1, keepdims=True) + NORM_EPS)
    xhat = xc * rstd
    return xhat, rstd, xhat * g_ref[...] + b_ref[...]


def _sgu_fwd(zc, ln_g, ln_b, wm, bmap, *, name):
    T = zc.shape[0]
    W = SGU_WIDTH
    tt = _tile(T, SGU_TILE, CHUNK)
    nch = tt // CHUNK

    def body(z_ref, g_ref, b_ref, wm_ref, bm_ref, p_ref):
        u = _gelu(z_ref[:, :W])
        _, _, vn = _sgu_norm(z_ref[:, W:], g_ref, b_ref)
        vn = vn.astype(BF16)
        for n in range(nch):
            rows = slice(n * CHUNK, (n + 1) * CHUNK)
            for gi in range(SGU_GROUPS):
                cols = slice(gi * LANES, (gi + 1) * LANES)
                s = _dot_nn(wm_ref[gi], vn[rows, cols]) + bm_ref[:, cols]
                p_ref[rows, cols] = (u[rows, cols] * s).astype(BF16)

    const2 = lambda r, c: pl.BlockSpec((r, c), lambda i: (0, 0))
    return pl.pallas_call(
        body, name=name, grid=(T // tt,),
        in_specs=[pl.BlockSpec((tt, 2 * W), lambda i: (i, 0)), const2(1, W), const2(1, W),
                  pl.BlockSpec((SGU_GROUPS, CHUNK, CHUNK), lambda i: (0, 0, 0)), const2(CHUNK, W)],
        out_specs=pl.BlockSpec((tt, W), lambda i: (i, 0)),
        out_shape=jax.ShapeDtypeStruct((T, W), BF16),
        compiler_params=_cparams(("parallel",)),
    )(zc, ln_g, ln_b, wm, bmap)


def _sgu_bwd(zc, dp, ln_g, ln_b, wm, bmap, *, name, ride=None):
    T = zc.shape[0]
    W = SGU_WIDTH
    tt = _tile(T, SGU_TILE, CHUNK)
    nch = tt // CHUNK
    nsteps = T // tt

    def body(z_ref, dp_ref, g_ref, b_ref, wm_ref, bm_ref, dz_ref, dg_ref, db_ref, dwm_ref, dbm_ref,
             s_scr, dvn_scr):
        step = pl.program_id(0)
        zu = z_ref[:, :W]
        zv = z_ref[:, W:]
        u, dgu = _gelu_and_grad(zu)
        xhat, rstd, vn = _sgu_norm(zv, g_ref, b_ref)
        vnb = vn.astype(BF16)
        dpf = dp_ref[...].astype(F32)
        ds = dpf * u

        @pl.when(step == 0)
        def _():
            dwm_ref[...] = jnp.zeros_like(dwm_ref)
            dbm_ref[...] = jnp.zeros_like(dbm_ref)

        for n in range(nch):
            rows = slice(n * CHUNK, (n + 1) * CHUNK)
            for gi in range(SGU_GROUPS):
                cols = slice(gi * LANES, (gi + 1) * LANES)
                s_scr[rows, cols] = _dot_nn(wm_ref[gi], vnb[rows, cols]) + bm_ref[:, cols]
                dsb = ds[rows, cols]
                dvn_scr[rows, cols] = _dot_tn(wm_ref[gi], dsb)
                dwm_ref[gi] += _dot_nt(dsb, vnb[rows, cols])
                dbm_ref[:, cols] += dsb
        dz_ref[:, :W] = (dpf * s_scr[...] * dgu).astype(BF16)
        dvn = dvn_scr[...]
        dxhat = dvn * g_ref[...]
        dv = rstd * (dxhat - jnp.mean(dxhat, axis=-1, keepdims=True)
                     - xhat * jnp.mean(dxhat * xhat, axis=-1, keepdims=True))
        _, dgv = _gelu_and_grad(zv)
        dz_ref[:, W:] = (dv * dgv).astype(BF16)
        dg_part = jnp.sum(dvn * xhat, axis=0, keepdims=True)
        db_part = jnp.sum(dvn, axis=0, keepdims=True)

        @pl.when(step == 0)
        def _():
            dg_ref[...] = dg_part
            db_ref[...] = db_part

        @pl.when(step > 0)
        def _():
            dg_ref[...] += dg_part
            db_ref[...] += db_part

        @pl.when(step == nsteps - 1)
        def _():
            for gi in range(SGU_GROUPS):
                cols = slice(gi * LANES, (gi + 1) * LANES)
                tot = jnp.sum(dbm_ref[:, cols], axis=1, keepdims=True)
                dbm_ref[:, cols] = jnp.broadcast_to(tot, (CHUNK, LANES))

    const2 = lambda r, c: pl.BlockSpec((r, c), lambda i: (0, 0))
    wspec = pl.BlockSpec((SGU_GROUPS, CHUNK, CHUNK), lambda i: (0, 0, 0))
    return _pcall(
        body, name=name, grid=(nsteps,),
        in_specs=[pl.BlockSpec((tt, 2 * W), lambda i: (i, 0)), pl.BlockSpec((tt, W), lambda i: (i, 0)),
                  const2(1, W), const2(1, W), wspec, const2(CHUNK, W)],
        out_specs=[pl.BlockSpec((tt, 2 * W), lambda i: (i, 0)), const2(1, W), const2(1, W), wspec, const2(CHUNK, W)],
        out_shape=[jax.ShapeDtypeStruct((T, 2 * W), BF16), jax.ShapeDtypeStruct((1, W), F32),
                   jax.ShapeDtypeStruct((1, W), F32), jax.ShapeDtypeStruct((SGU_GROUPS, CHUNK, CHUNK), F32),
                   jax.ShapeDtypeStruct((CHUNK, W), F32)],
        args=(zc, dp, ln_g, ln_b, wm, bmap), scratch=[pltpu.VMEM((tt, W), F32), pltpu.VMEM((tt, W), F32)], ride=ride)


def _rope_tables(positions):
    half = QK_ROPE // 2
    inv_freq = jnp.exp(-math.log(ROPE_BASE) * jnp.arange(half, dtype=F32) / half)
    ang = positions.reshape(-1).astype(F32)[:, None] * inv_freq
    cos = jnp.cos(ang)
    sin = jnp.sin(ang)
    n = ang.shape[0]
    tail = LANES - QK_NOPE - QK_ROPE
    cos_t = jnp.concatenate([jnp.ones((n, QK_NOPE), F32), cos, cos, jnp.ones((n, tail), F32)], axis=1)
    sin_t = jnp.concatenate([jnp.zeros((n, QK_NOPE), F32), -sin, sin, jnp.zeros((n, tail), F32)], axis=1)
    return cos_t, sin_t


SGU_GROUP_DIM = SGU_WIDTH // SGU_GROUPS
_O1, _O2, _O3, _O4 = Q_LORA, Q_LORA + KV_LORA, Q_LORA + KV_LORA + QK_ROPE, Q_LORA + KV_LORA + QK_ROPE + LRU_WIDTH
_A0, _A1, _A2 = 2 * LRU_WIDTH, 2 * LRU_WIDTH + Q_LORA, 2 * LRU_WIDTH + Q_LORA + KV_LORA
_A3 = _A2 + QK_NOPE
Z_Q_BLOCK, Z_KV_BLOCK, Z_KPE_BLOCK = _A0 // Q_LORA, _A1 // KV_LORA, _A2 // LANES


def _perm_w_in(w_in):
    zeros = lambda n: jnp.zeros((w_in.shape[0], n), w_in.dtype)
    return jnp.concatenate([w_in[:, _O3:_O4], w_in[:, _O4:], w_in[:, :_O1], w_in[:, _O1:_O2], zeros(QK_NOPE),
                            w_in[:, _O2:_O3], zeros(LANES - QK_NOPE - QK_ROPE)], axis=1)


def _unperm_w_in(w):
    return jnp.concatenate([w[:, _A0:_A1], w[:, _A1:_A2], w[:, _A3:_A3 + QK_ROPE], w[:, :LRU_WIDTH],
                            w[:, LRU_WIDTH:_A0]], axis=1)


def _head_blocks(w, d):
    r = w.shape[0]
    return jnp.pad(w.reshape(r, MLA_HEADS, d), ((0, 0), (0, 0), (0, LANES - d))).reshape(r, MLA_HEADS * LANES)


def _from_head_blocks(w, d):
    r = w.shape[0]
    return w.reshape(r, MLA_HEADS, LANES)[:, :, :d].reshape(r, MLA_HEADS * d)


def _split_kv(w_kv):
    r = w_kv.shape[0]
    w3 = w_kv.reshape(r, MLA_HEADS, QK_NOPE + V_HEAD)
    return _head_blocks(w3[:, :, :QK_NOPE].reshape(r, -1), QK_NOPE), w3[:, :, QK_NOPE:].reshape(r, -1)


def _join_kv(w_k, w_v):
    r = w_k.shape[0]
    return jnp.concatenate([_from_head_blocks(w_k, QK_NOPE).reshape(r, MLA_HEADS, QK_NOPE),
                            w_v.reshape(r, MLA_HEADS, V_HEAD)], axis=2).reshape(r, -1)


def _prep_small(w):
    p = {n: w[n] for n in w if n not in BIG}
    eye = jnp.eye(LRU_HEADS, dtype=F32)
    dense = lambda wg: (wg[:, :, None, :] * eye[:, None, :, None]).reshape(LRU_WIDTH, LRU_WIDTH).astype(BF16)
    p["wa_d"] = dense(w["ab_w_rg_a"][0])
    p["wx_d"] = dense(w["ab_w_rg_x"][0])
    causal = jnp.tril(jnp.ones((CHUNK, CHUNK), F32))
    p["wm"] = (w["c_w_s"][0] * causal).astype(BF16)
    p["bmap"] = jnp.repeat(w["c_b_s"][0].T, SGU_GROUP_DIM, axis=1)
    return p


def _prep_big(ab_w_in, ab_w_q_b, ab_w_kv_b):
    return {"w_in_p": _perm_w_in(ab_w_in).astype(BF16),
            "w_q_p": _head_blocks(ab_w_q_b, QK_NOPE + QK_ROPE).astype(BF16),
            "w_kv_p": jnp.concatenate(_split_kv(ab_w_kv_b), axis=1).astype(BF16)}


def _ffn_fwd(h, l, p, S, rides):
    hn = _rms_fwd(h, p["ffn_norm"][l], name=f"ffn{l}_norm")
    g = _mm(hn, p["ffn_gate_t"][l], tb=True, out_dtype=BF16, name=f"ffn{l}_gate", ride=rides.get(f"ffn{l}_gate"))
    u = _mm(hn, p["ffn_up_t"][l], tb=True, out_dtype=BF16, name=f"ffn{l}_up", ride=rides.get(f"ffn{l}_up"))
    out, act = _ffn_act_down(g, u, p["ffn_conv_w"][l], p["ffn_conv_b"][l][None], p["ffn_down"][l], h, S=S,
                             name=f"ffn{l}_down", ride=rides.get(f"ffn{l}_down"))
    return out, (hn, g, u, act)


def _ffn_bwd(dh, h_in, l, p, saved, S, rides, grads_ready, also_ready=None):
    hn, g, u, act = saved
    dw_down = _mm(act, dh, ta=True, out_dtype=BF16, name=f"ffn{l}_dwdown")
    dg, du, dcw, dcb = _ffn_act_bwd(g, u, dh, p["ffn_down"][l], p["ffn_conv_w"][l], p["ffn_conv_b"][l][None], S=S,
                                    name=f"ffn{l}_dactbwd", ride=rides.get(f"ffn{l}_dactbwd"))
    dhn = _mm(dg, p["ffn_gate_t"][l], also=(du, p["ffn_up_t"][l]), out_dtype=BF16, name=f"ffn{l}_dhn")
    dw_gate_t = _mm(dg, hn, ta=True, out_dtype=BF16, name=f"ffn{l}_dwgate")
    dw_up_t = _mm(du, hn, ta=True, out_dtype=BF16, name=f"ffn{l}_dwup")
    grads_ready(l, {**(also_ready or {}), "ffn_gate_t": dw_gate_t, "ffn_up_t": dw_up_t, "ffn_down": dw_down})
    dh_in, dnorm = _rms_bwd(h_in, p["ffn_norm"][l], dhn, res=dh, name=f"ffn{l}_dnorm", ride=rides.get(f"ffn{l}_dnorm"))
    grads = dict(ffn_norm=dnorm[0], ffn_gate_t=dw_gate_t, ffn_up_t=dw_up_t, ffn_conv_w=dcw[:FFN_CONV],
                 ffn_conv_b=dcb[0], ffn_down=dw_down)
    return dh_in, grads


def _local_step(x, positions, target, p, rides=None, grads_ready=None):
    rides = {} if rides is None else rides
    grads_ready = grads_ready or (lambda layer, ready: None)
    B, S, D = x.shape
    T = B * S
    H = MLA_HEADS
    xf = x.reshape(T, D)
    tgt = target.reshape(T, D)
    cos, sin = _rope_tables(positions)

    hn0 = _rms_fwd(xf, p["ab_norm"][0], name="ab_norm", ride=rides.get("ab_norm"))
    z = _mm(hn0, p["w_in_p"], name="ab_in")
    cqn = _rms_fwd(z, p["ab_q_norm"][0], cb=Z_Q_BLOCK, name="q_norm")
    ckvn = _rms_fwd(z, p["ab_kv_norm"][0], cb=Z_KV_BLOCK, name="kv_norm")
    q = _mm(cqn, p["w_q_p"], name="q_up")
    kv = _mm(ckvn, p["w_kv_p"], out_dtype=BF16, name="kv_up")
    qs = _rope_q(q, cos, sin, name="q_rope")
    kk = _key_blocks(kv, z, cos, sin, kpe_block=Z_KPE_BLOCK, name="k_rope")
    att = dict(B=B, S=S, v_block0=H)
    o, lse = _attn_fwd(qs, kk, kv, name="attn_fwd", ride=rides.get("attn_fwd"), **att)
    lru_par = (p["ab_conv_w"][0], p["ab_conv_b"], p["wa_d"], p["ab_b_rg_a"], p["wx_d"], p["ab_b_rg_x"], p["ab_lambda"])
    y_lru, hs = _lru_fwd(z, *lru_par, S=S, name="lru_fwd", ride=rides.get("lru_fwd"))
    n_att = H * V_HEAD
    w_out_a, w_out_b = p["ab_w_out"][:n_att], p["ab_w_out"][n_att:]
    h1 = _mm(o, w_out_a, also=(y_lru, w_out_b), res=xf, name="ab_out")
    h2, ffn0 = _ffn_fwd(h1, 0, p, S, rides)

    hn2 = _rms_fwd(h2, p["c_norm"][0], name="c_norm")
    zc = _mm(hn2, p["c_w_in_t"], tb=True, name="c_in")
    pg = _sgu_fwd(zc, p["c_ln_g"], p["c_ln_b"], p["wm"], p["bmap"], name="sgu_fwd")
    h3 = _mm(pg, p["c_w_out"], res=h2, name="c_out")
    h4, ffn1 = _ffn_fwd(h3, 1, p, S, rides)

    loss_row, dh4, dfinal = _final_fwd_bwd(h4, p["final_norm"], tgt, name="final")

    dh3, g_ffn1 = _ffn_bwd(dh4, h3, 1, p, ffn1, S, rides, grads_ready)
    dpg = _mm(dh3, p["c_w_out"], tb=True, out_dtype=BF16, name="c_dp")
    dw_c_out = _mm(pg, dh3, ta=True, out_dtype=BF16, name="c_dwout")
    dzc, dlng, dlnb, dwm, dbm = _sgu_bwd(zc, dpg, p["c_ln_g"], p["c_ln_b"], p["wm"], p["bmap"], name="sgu_bwd",
                                         ride=rides.get("sgu_bwd"))
    dhn2 = _mm(dzc, p["c_w_in_t"], out_dtype=BF16, name="c_dhn")
    dw_c_in_t = _mm(dzc, hn2, ta=True, out_dtype=BF16, name="c_dwin")
    dh2, dcnorm = _rms_bwd(h2, p["c_norm"][0], dhn2, res=dh3, name="c_dnorm")
    dh1, g_ffn0 = _ffn_bwd(dh2, h1, 0, p, ffn0, S, rides, grads_ready, {"c_w_in_t": dw_c_in_t, "c_w_out": dw_c_out})

    do = _mm(dh1, w_out_a, tb=True, name="ab_do")
    dy_lru = _mm(dh1, w_out_b, tb=True, out_dtype=BF16, name="ab_dylru")
    dw_out = jnp.concatenate([_mm(o, dh1, ta=True, out_dtype=BF16, name="ab_dwout_a"),
                              _mm(y_lru, dh1, ta=True, out_dtype=BF16, name="ab_dwout_b")], axis=0)
    dq, dk, dv = _attn_bwd(qs, kk, kv, o, lse, do, name="attn_bwd", ride=rides.get("attn_bwd"), **att)
    dq_full = _rope_q_bwd(dq, cos, sin, name="q_rope_bwd")
    dkr = _key_rope_bwd(dk, cos, sin, name="k_rope_bwd")
    n_key = H * LANES
    w_k_p, w_v_p = p["w_kv_p"][:, :n_key], p["w_kv_p"][:, n_key:]
    dcqn = _mm(dq_full, p["w_q_p"], tb=True, name="q_dlat")
    dw_q_p = _mm(cqn, dq_full, ta=True, out_dtype=BF16, name="q_dw")
    dckvn = _mm(dv, w_v_p, tb=True, res=_mm(dk, w_k_p, tb=True, name="k_dlat"), name="v_dlat")
    dw_k_p = _mm(ckvn, dk, ta=True, out_dtype=BF16, name="k_dw")
    dw_v_p = _mm(ckvn, dv, ta=True, out_dtype=BF16, name="v_dw")
    dcq, dqnorm = _rms_bwd(z, p["ab_q_norm"][0], dcqn, cb=Z_Q_BLOCK, out_dtype=BF16, name="q_dnorm")
    dckv, dkvnorm = _rms_bwd(z, p["ab_kv_norm"][0], dckvn, cb=Z_KV_BLOCK, out_dtype=BF16, name="kv_dnorm")
    dxl, dgate, dcw, dcb, dwa, dba, dwx, dbx, dlam = _lru_bwd(z, hs, dy_lru, *lru_par, S=S, name="lru_bwd")
    dz = jnp.concatenate([dxl, dgate, dcq, dckv, dkr], axis=1)
    dhn0 = _mm(dz, p["w_in_p"], tb=True, out_dtype=BF16, name="ab_dhn")
    dw_in_p = _mm(hn0, dz, ta=True, out_dtype=BF16, name="ab_dwin")
    dx, dabnorm = _rms_bwd(xf, p["ab_norm"][0], dhn0, res=dh1, name="ab_dnorm")

    blocks = lambda dd: jnp.stack([dd[i * LRU_BLOCK:(i + 1) * LRU_BLOCK, i * LRU_BLOCK:(i + 1) * LRU_BLOCK]
                                   for i in range(LRU_HEADS)])
    causal = jnp.tril(jnp.ones((CHUNK, CHUNK), F32))
    grads = {
        "ab_norm": dabnorm, "w_in_p": dw_in_p, "ab_q_norm": dqnorm, "w_q_p": dw_q_p,
        "ab_kv_norm": dkvnorm, "w_k_p": dw_k_p, "w_v_p": dw_v_p, "ab_conv_w": dcw[:LRU_CONV][None], "ab_conv_b": dcb,
        "ab_w_rg_a": blocks(dwa)[None], "ab_b_rg_a": dba, "ab_w_rg_x": blocks(dwx)[None], "ab_b_rg_x": dbx,
        "ab_lambda": dlam, "ab_w_out": dw_out,
        "c_norm": dcnorm, "c_w_in_t": dw_c_in_t, "c_ln_g": dlng, "c_ln_b": dlnb,
        "c_w_s": (dwm * causal)[None], "c_b_s": dbm[:, ::SGU_GROUP_DIM].T[None], "c_w_out": dw_c_out,
        "final_norm": dfinal[0],
    }
    for name in ("ffn_norm", "ffn_conv_w", "ffn_conv_b"):
        grads[name] = jnp.stack([g_ffn0[name], g_ffn1[name]])
    for name in ("ffn_gate_t", "ffn_up_t", "ffn_down"):
        grads[name] = [g_ffn0[name], g_ffn1[name]]
    return loss_row, dx.reshape(B, S, D), grads


ANY = pl.BlockSpec(memory_space=pl.ANY)


def _place():
    x, y, c = lax.axis_index("x"), lax.axis_index("y"), lax.axis_index("c")
    chips = [(1 - x, y), (x, 1 - y), (1 - x, 1 - y)]
    return x, y, c, 2 * x + y, (x, y, 1 - c), chips


def _remote(src, dst, send_sems, recv_sems, k, to):
    return pltpu.make_async_remote_copy(src_ref=src, dst_ref=dst, send_sem=send_sems.at[k], recv_sem=recv_sems.at[k],
                                        device_id=to, device_id_type=MESH)


class _Exchange:
    def __init__(self, arrs, out_shapes, n_sems, start, finish):
        self.arrs, self.out_shapes, self.n_sems, self.start, self.finish = list(arrs), out_shapes, n_sems, start, finish

    @property
    def in_specs(self):
        return [ANY] * len(self.arrs)

    @property
    def out_specs(self):
        return [ANY] * len(self.out_shapes)

    @property
    def scratch(self):
        return [pltpu.SemaphoreType.DMA((self.n_sems,)), pltpu.SemaphoreType.DMA((self.n_sems,))]

    def split(self, refs):
        n = len(self.arrs)
        return refs[:n], refs[n:n + len(self.out_shapes)], refs[-2], refs[-1]

    def run(self, name):
        def body(*refs):
            parts = self.split(refs)
            self.start(*parts)
            self.finish(*parts)

        return pl.pallas_call(body, name=name, in_specs=self.in_specs, out_specs=self.out_specs,
                              out_shape=self.out_shapes, scratch_shapes=self.scratch)(*self.arrs)


def _put(buf, piece, idx, axis):
    return lax.dynamic_update_slice_in_dim(buf, jnp.expand_dims(piece, axis).astype(buf.dtype), idx, axis)


def _all_gather(arrs):
    n = len(arrs)
    per = 7

    def start(ins, outs, send_sems, recv_sems):
        x, y, c, j, sib, chips = _place()
        for i in range(n):
            for k, (cx, cy) in enumerate(chips):
                _remote(ins[i].at[:, c], outs[i].at[:, j, c], send_sems, recv_sems, per * i + k, (cx, cy, c)).start()
            _remote(ins[i], outs[i].at[:, j], send_sems, recv_sems, per * i + 6, sib).start()

    def finish(ins, outs, send_sems, recv_sems):
        x, y, c, j, sib, chips = _place()
        passed = []
        for i in range(n):
            for k, (cx, cy) in enumerate(chips):
                got = outs[i].at[:, 2 * cx + cy, c]
                _remote(got, got, send_sems, recv_sems, per * i + k, (cx, cy, c)).wait_recv()
                cp = _remote(got, got, send_sems, recv_sems, per * i + 3 + k, sib)
                cp.start()
                passed.append(cp)
        for i in range(n):
            for k, (cx, cy) in enumerate(chips):
                got = outs[i].at[:, 2 * cx + cy, 1 - c]
                _remote(got, got, send_sems, recv_sems, per * i + 3 + k, sib).wait_recv()
                _remote(ins[i].at[:, c], ins[i].at[:, c], send_sems, recv_sems, per * i + k, sib).wait_send()
            _remote(ins[i], outs[i].at[:, j], send_sems, recv_sems, per * i + 6, sib).wait()
        for cp in passed:
            cp.wait_send()

    shapes = [jax.ShapeDtypeStruct((a.shape[0], N_CHIPS) + a.shape[1:], a.dtype) for a in arrs]
    return _Exchange(arrs, shapes, per * n, start, finish)


class _Offset:
    def __init__(self, sems, k0):
        self.sems, self.k0 = sems, k0

    @property
    def at(self):
        return self

    def __getitem__(self, k):
        return self.sems.at[self.k0 + k]


def _merge(a, b):
    n_in, n_out = len(a.arrs), len(a.out_shapes)

    def both(fa, fb):
        def f(ins, outs, send_sems, recv_sems):
            fa(ins[:n_in], outs[:n_out], send_sems, recv_sems)
            fb(ins[n_in:], outs[n_out:], _Offset(send_sems, a.n_sems), _Offset(recv_sems, a.n_sems))
        return f

    return _Exchange(a.arrs + b.arrs, a.out_shapes + b.out_shapes, a.n_sems + b.n_sems,
                     both(a.start, b.start), both(a.finish, b.finish))


def _pair_swap(arrs):
    n = len(arrs)

    def start(ins, outs, send_sems, recv_sems):
        x, y, c, j, sib, chips = _place()
        for i in range(n):
            _remote(ins[i].at[:, 1 - c], outs[i], send_sems, recv_sems, i, sib).start()

    def finish(ins, outs, send_sems, recv_sems):
        x, y, c, j, sib, chips = _place()
        for i in range(n):
            _remote(ins[i].at[:, 1 - c], outs[i], send_sems, recv_sems, i, sib).wait()

    shapes = [jax.ShapeDtypeStruct((a.shape[0],) + a.shape[2:], a.dtype) for a in arrs]
    return _Exchange(arrs, shapes, n, start, finish)


def _pair_send(arrs):
    n = len(arrs)

    def start(ins, outs, send_sems, recv_sems):
        x, y, c, j, sib, chips = _place()
        for i in range(n):
            _remote(ins[i], outs[i], send_sems, recv_sems, i, sib).start()

    def finish(ins, outs, send_sems, recv_sems):
        x, y, c, j, sib, chips = _place()
        for i in range(n):
            _remote(ins[i], outs[i], send_sems, recv_sems, i, sib).wait()

    shapes = [jax.ShapeDtypeStruct(a.shape, a.dtype) for a in arrs]
    return _Exchange(arrs, shapes, n, start, finish)


def _chip_exchange(arrs, *, scatter):
    n = len(arrs)

    def copies(ins, outs, send_sems, recv_sems):
        x, y, c, j, sib, chips = _place()
        return [(_remote(ins[i].at[2 * cx + cy] if scatter else ins[i], outs[i].at[j], send_sems, recv_sems,
                         3 * i + k, (cx, cy, c)),
                 _remote(outs[i].at[2 * cx + cy], outs[i].at[2 * cx + cy], send_sems, recv_sems, 3 * i + k, (cx, cy, c)))
                for i in range(n) for k, (cx, cy) in enumerate(chips)]

    def start(*refs):
        for out, _ in copies(*refs):
            out.start()

    def finish(*refs):
        for out, back in copies(*refs):
            back.wait_recv()
            out.wait_send()

    shapes = [jax.ShapeDtypeStruct((N_CHIPS,) + a.shape[-2:], a.dtype) for a in arrs]
    return _Exchange(arrs, shapes, 3 * n, start, finish)


FLAT_ROWS = 512


def _pair_add(sharded, from_sib, *, name):
    n, _, R, L = sharded.shape
    tr = _tile(R, FLAT_ROWS, 16)

    def body(s_ref, b_ref, o_ref):
        own = jnp.where(lax.axis_index("c") == 0, s_ref[:, 0], s_ref[:, 1])
        o_ref[...] = (own.astype(F32) + b_ref[...].astype(F32)).astype(BF16)

    spec = pl.BlockSpec((n, tr, L), lambda i: (0, i, 0))
    return pl.pallas_call(
        body, name=name, grid=(R // tr,), in_specs=[pl.BlockSpec((n, 2, tr, L), lambda i: (0, 0, i, 0)), spec],
        out_specs=spec, out_shape=jax.ShapeDtypeStruct((n, R, L), BF16), compiler_params=_cparams(("parallel",)),
    )(sharded, from_sib)


def _chip_sum(arrived, pair, *, name):
    n, R, L = arrived.shape
    tr = _tile(R, FLAT_ROWS, 16)

    def body(a_ref, p_ref, o_ref):
        me = 2 * lax.axis_index("x") + lax.axis_index("y")
        acc = None
        for k in range(n):
            term = jnp.where(me == k, p_ref[k], a_ref[k]).astype(F32)
            acc = term if acc is None else acc + term
        o_ref[...] = acc

    spec = pl.BlockSpec((n, tr, L), lambda i: (0, i, 0))
    return pl.pallas_call(
        body, name=name, grid=(R // tr,), in_specs=[spec, spec], out_specs=pl.BlockSpec((tr, L), lambda i: (i, 0)),
        out_shape=jax.ShapeDtypeStruct((R, L), F32), compiler_params=_cparams(("parallel",)),
    )(arrived, pair)


def _sum_slots(buf, *, name):
    n, R, L = buf.shape
    tr = _tile(R, FLAT_ROWS, 16)

    def body(b_ref, o_ref):
        acc = b_ref[0].astype(F32)
        for k in range(1, n):
            acc = acc + b_ref[k].astype(F32)
        o_ref[...] = acc

    return pl.pallas_call(
        body, name=name, grid=(R // tr,), in_specs=[pl.BlockSpec((n, tr, L), lambda i: (0, i, 0))],
        out_specs=pl.BlockSpec((tr, L), lambda i: (i, 0)),
        out_shape=jax.ShapeDtypeStruct((R, L), F32), compiler_params=_cparams(("parallel",)),
    )(buf)


def _adamw_update(w, g, m, v):
    c1 = 1.0 - ADAM_B1 ** ADAM_STEP
    c2 = 1.0 - ADAM_B2 ** ADAM_STEP
    m = ADAM_B1 * m + (1.0 - ADAM_B1) * g
    v = ADAM_B2 * v + (1.0 - ADAM_B2) * (g * g)
    return -ADAM_LR * ((m / c1) / (jnp.sqrt(v / c2) + ADAM_EPS) + ADAM_WD * w), m, v


def _adamw_halves(w, m, v, own, other, *, name):
    NL, R, L = w.shape
    h = R // 2
    tr = _tile(h, FLAT_ROWS, 16)
    nt = h // tr

    def body(*refs):
        w_ref, m_ref, v_ref = refs[:3]
        own_refs, other_refs = refs[3:3 + NL], refs[3 + NL:3 + 2 * NL]
        d_ref, nm_ref, nv_ref, g_ref = refs[3 + 2 * NL:]
        layer, half = pl.program_id(0), pl.program_id(1)
        mine = half == lax.axis_index("c")
        g = jnp.where(mine, own_refs[0][...], other_refs[0][...])
        for l in range(1, NL):
            g = jnp.where(layer == l, jnp.where(mine, own_refs[l][...], other_refs[l][...]), g)
        d, mm, vv = _adamw_update(w_ref[0], g, m_ref[0], v_ref[0])
        d_ref[0], nm_ref[0], nv_ref[0], g_ref[0] = d, mm, vv, g

    spec = pl.BlockSpec((1, tr, L), lambda l, hh, i: (l, hh * nt + i, 0))
    part = pl.BlockSpec((tr, L), lambda l, hh, i: (i, 0))
    sh = jax.ShapeDtypeStruct((NL, R, L), F32)
    return pl.pallas_call(
        body, name=name, grid=(NL, 2, nt), in_specs=[spec] * 3 + [part] * (2 * NL), out_specs=[spec] * 4,
        out_shape=[sh] * 4, compiler_params=_cparams(("parallel", "parallel", "parallel")),
    )(w, m, v, *own, *other)


def _adamw(w, g, m, v, *, name):
    NL, R, L = w.shape
    tr = _tile(R, FLAT_ROWS, 16)

    def body(w_ref, g_ref, m_ref, v_ref, d_ref, nm_ref, nv_ref):
        d_ref[...], nm_ref[...], nv_ref[...] = _adamw_update(w_ref[...], g_ref[...], m_ref[...], v_ref[...])

    spec = pl.BlockSpec((1, tr, L), lambda l, i: (l, i, 0))
    sh = jax.ShapeDtypeStruct((NL, R, L), F32)
    return pl.pallas_call(
        body, name=name, grid=(NL, R // tr), in_specs=[spec] * 4, out_specs=[spec] * 3, out_shape=[sh] * 3,
        compiler_params=_cparams(("parallel", "parallel")),
    )(w, g, m, v)


WEIGHT_NAMES = ["ab_norm", "ab_w_in", "ab_q_norm", "ab_w_q_b", "ab_kv_norm", "ab_w_kv_b", "ab_conv_w", "ab_conv_b",
                "ab_w_rg_a", "ab_b_rg_a", "ab_w_rg_x", "ab_b_rg_x", "ab_lambda", "ab_w_out", "c_norm", "c_w_in",
                "c_ln_g", "c_ln_b", "c_w_s", "c_b_s", "c_w_out", "ffn_norm", "ffn_w_gate", "ffn_w_up", "ffn_conv_w",
                "ffn_conv_b", "ffn_w_down", "final_norm"]
BIG = {"ab_w_in": 2, "ab_w_q_b": 2, "ab_w_kv_b": 2, "ab_w_out": 1, "c_w_in": 2, "c_w_out": 1,
       "ffn_w_gate": 2, "ffn_w_up": 2, "ffn_w_down": 1}
SMALL_SHARDED = {"ab_conv_w": 2, "c_norm": 1, "c_ln_g": 1, "c_ln_b": 1, "ffn_conv_w": 2}
SMALL_REPLICATED = [n for n in WEIGHT_NAMES if n not in BIG and n not in SMALL_SHARDED]


def _rows(n_elems, mult):
    r = -(-n_elems // LANES)
    return -(-r // mult) * mult


def _flat(parts, rows):
    flat = jnp.concatenate([a.reshape(-1) for a in parts])
    return jnp.pad(flat, (0, rows * LANES - flat.shape[0])).reshape(rows, LANES)


def _unflat(flat, shapes):
    flat = flat.reshape(-1)
    out, off = [], 0
    for s in shapes:
        n = math.prod(s)
        out.append(flat[off:off + n].reshape(s))
        off += n
    return out


def _join_shards(a, axis):
    a = jnp.moveaxis(a, 0, axis)
    return a.reshape(a.shape[:axis] + (a.shape[axis] * a.shape[axis + 1],) + a.shape[axis + 2:])


def kernel(x, positions, ab_norm, ab_w_in, ab_q_norm, ab_w_q_b, ab_kv_norm, ab_w_kv_b, ab_conv_w, ab_conv_b, ab_w_rg_a, ab_b_rg_a, ab_w_rg_x, ab_b_rg_x, ab_lambda, ab_w_out, c_norm, c_w_in, c_ln_g, c_ln_b, c_w_s, c_b_s, c_w_out, ffn_norm, ffn_w_gate, ffn_w_up, ffn_conv_w, ffn_conv_b, ffn_w_down, final_norm, loss_target, m_ab_norm, m_ab_w_in, m_ab_q_norm, m_ab_w_q_b, m_ab_kv_norm, m_ab_w_kv_b, m_ab_conv_w, m_ab_conv_b, m_ab_w_rg_a, m_ab_b_rg_a, m_ab_w_rg_x, m_ab_b_rg_x, m_ab_lambda, m_ab_w_out, m_c_norm, m_c_w_in, m_c_ln_g, m_c_ln_b, m_c_w_s, m_c_b_s, m_c_w_out, m_ffn_norm, m_ffn_w_gate, m_ffn_w_up, m_ffn_conv_w, m_ffn_conv_b, m_ffn_w_down, m_final_norm, v_ab_norm, v_ab_w_in, v_ab_q_norm, v_ab_w_q_b, v_ab_kv_norm, v_ab_w_kv_b, v_ab_conv_w, v_ab_conv_b, v_ab_w_rg_a, v_ab_b_rg_a, v_ab_w_rg_x, v_ab_b_rg_x, v_ab_lambda, v_ab_w_out, v_c_norm, v_c_w_in, v_c_ln_g, v_c_ln_b, v_c_w_s, v_c_b_s, v_c_w_out, v_ffn_norm, v_ffn_w_gate, v_ffn_w_up, v_ffn_conv_w, v_ffn_conv_b, v_ffn_w_down, v_final_norm):
    given = dict(locals())
    w = {n: given[n] for n in WEIGHT_NAMES}
    m = {n: given["m_" + n] for n in WEIGHT_NAMES}
    v = {n: given["v_" + n] for n in WEIGHT_NAMES}
    c = lax.axis_index("c")
    chip = 2 * lax.axis_index("x") + lax.axis_index("y")

    halves = lambda a: a.reshape(a.shape[0], 2, a.shape[1] // 2, a.shape[2])
    tr = lambda a: jnp.swapaxes(a, 1, 2)
    send = {"ab_w_in": w["ab_w_in"], "ab_w_q_b": w["ab_w_q_b"], "ab_w_kv_b": w["ab_w_kv_b"], "ab_w_out": w["ab_w_out"],
            "c_w_in": tr(w["c_w_in"]), "c_w_out": w["c_w_out"], "ffn_w_gate": tr(w["ffn_w_gate"]),
            "ffn_w_up": tr(w["ffn_w_up"]), "ffn_w_down": w["ffn_w_down"]}
    small_rows = _rows(sum(w[n].size for n in SMALL_SHARDED), 16)
    small_sh = _flat([w[n] for n in SMALL_SHARDED], small_rows).reshape(1, 2, small_rows // 2, LANES)
    first_names = ["ab_w_in", "ab_w_q_b", "ab_w_kv_b"]
    mine = {n: halves(send[n].astype(BF16)) for n in BIG}

    def put_own(own, arrived):
        return arrived.reshape(arrived.shape[0], -1, arrived.shape[-1])

    p = {"ab_norm": w["ab_norm"], "ffn_gate_t": {}, "ffn_up_t": {}, "ffn_down": {}}
    first = [mine[n] for n in first_names] + [small_sh]

    def first_arrived(got):
        full = {n: put_own(o, a) for n, o, a in zip(first_names + ["small"], first, got)}
        unshard = lambda a: jnp.swapaxes(a.reshape(N_CHIPS, -1, a.shape[-1]), 0, 1).reshape(-1, N_CHIPS * a.shape[-1])
        p.update(_prep_big(unshard(full["ab_w_in"][0]), unshard(full["ab_w_q_b"][0]), unshard(full["ab_w_kv_b"][0])))
        small_full = dict(w)
        off = 0
        small_got = full["small"].reshape(N_CHIPS, -1)
        for n, ax in SMALL_SHARDED.items():
            seg = small_got[:, off:off + w[n].size].reshape((N_CHIPS,) + w[n].shape)
            small_full[n] = _join_shards(seg, ax)
            off += w[n].size
        p.update(_prep_small(small_full))

    def weights_ride(parts):
        def sink(arrived):
            for (own, setter), a in zip(parts, arrived):
                setter(put_own(own, a)[0])
        return _all_gather([own for own, _ in parts]), sink

    ffn_keys = {"ffn_gate_t": "ffn_w_gate", "ffn_up_t": "ffn_w_up", "ffn_down": "ffn_w_down"}
    ffn_part = lambda key, l: (mine[ffn_keys[key]][l:l + 1], functools.partial(p[key].__setitem__, l))
    rides = {
        "ab_norm": (_all_gather(first), first_arrived),
        "attn_fwd": weights_ride([(mine["ab_w_out"], functools.partial(p.__setitem__, "ab_w_out"))]
                                 + [ffn_part(key, 0) for key in ffn_keys]),
        "ffn0_gate": weights_ride([ffn_part("ffn_gate_t", 1)]),
        "ffn0_up": weights_ride([ffn_part("ffn_up_t", 1)]),
        "ffn0_down": weights_ride([ffn_part("ffn_down", 1),
                                   (mine["c_w_in"], functools.partial(p.__setitem__, "c_w_in_t")),
                                   (mine["c_w_out"], functools.partial(p.__setitem__, "c_w_out"))]),
    }

    def chip_sums(pair, arrived, tag):
        return [_chip_sum(a, b, name=f"grad_chip_sum_{tag}{i}") for i, (a, b) in enumerate(zip(arrived, pair))]

    half_of = {}

    def grads_ready(layer, ready):
        if layer == 1:
            named = {"gate1": ready["ffn_gate_t"], "up1": ready["ffn_up_t"], "down1": ready["ffn_down"]}
            hosts = {"sgu_bwd": ["down1"], "ffn0_dactbwd": ["gate1", "up1"]}
        else:
            named = {"c_in": ready["c_w_in_t"], "c_out": ready["c_w_out"], "gate0": ready["ffn_gate_t"],
                     "up0": ready["ffn_up_t"], "down0": ready["ffn_down"]}
            hosts = {"attn_bwd": ["c_in", "c_out", "down0", "gate0", "up0"]}
        tag = f"f{layer}"
        sharded = [a.reshape(N_CHIPS, 2, -1, a.shape[-1]) for a in named.values()]

        def paired(from_sib):
            pair = {k: _pair_add(a, b, name=f"grad_pair_add_{tag}{i}")
                    for i, (k, a, b) in enumerate(zip(named, sharded, from_sib))}
            for kernel_name, keys in hosts.items():
                def sink(arrived, keys=keys, kernel_name=kernel_name):
                    half_of.update(zip(keys, chip_sums([pair[k] for k in keys], arrived, f"{tag}_{kernel_name}")))
                rides[kernel_name] = (_chip_exchange([pair[k] for k in keys], scatter=True), sink)

        rides[f"ffn{layer}_dnorm"] = (_pair_swap(sharded), paired)

    loss_row, grad_x, g = _local_step(x, positions, loss_target, p, rides, grads_ready)

    cols = lambda a, n: jnp.swapaxes(a.reshape(a.shape[0], N_CHIPS, n), 0, 1)
    n_in, n_q, n_kv = w["ab_w_in"].shape[2], w["ab_w_q_b"].shape[2], w["ab_w_kv_b"].shape[2]
    small_names = SMALL_REPLICATED + list(SMALL_SHARDED)
    rs = _rows(sum(g[n].size for n in small_names) + LANES, FLAT_ROWS)
    small = _flat([loss_row] + [g[n] for n in small_names], rs)
    slot = (jnp.arange(2) == c)[:, None, None]
    last = [cols(_unperm_w_in(g["w_in_p"]), n_in), cols(_from_head_blocks(g["w_q_p"], QK_NOPE + QK_ROPE), n_q),
            cols(_join_kv(g["w_k_p"], g["w_v_p"]), n_kv), g["ab_w_out"]]
    last = [a.reshape(N_CHIPS, 2, -1, a.shape[-1]) for a in last]
    *from_sib, small_sib = _merge(_pair_swap(last), _pair_send([small])).run("tail_pair")
    pair = [_pair_add(a, b, name=f"grad_pair_add_b{i}") for i, (a, b) in enumerate(zip(last, from_sib))]
    pair_small = _sum_slots(jnp.where(slot, small[None], small_sib[None]), name="small_pair_sum")
    my_small = lax.dynamic_index_in_dim(pair_small.reshape(2, rs // 2, LANES), c, axis=0, keepdims=False)
    *arrived, all_small = _merge(_chip_exchange(pair, scatter=True), _chip_exchange([my_small], scatter=False)).run("tail_chip")
    half_of.update(zip(["in", "q", "kv", "out"], chip_sums(pair, arrived, "b")))
    half_of["small"] = _sum_slots(_put(all_small, my_small, chip, 0), name="small_chip_sum")
    keys = ("in", "q", "kv", "out", "c_in", "c_out", "gate0", "gate1", "up0", "up1", "down0", "down1", "small")
    other_half = dict(zip(keys, _pair_send([half_of[k] for k in keys]).run("grad_pair_share")))
    small_sum = jnp.where(slot, half_of["small"][None], other_half["small"][None]).reshape(rs, LANES)
    whole = lambda k: jnp.where(slot, half_of[k][None], other_half[k][None]).reshape(-1, half_of[k].shape[-1])
    grads_t = {"ab_w_in": whole("in").T[None], "ab_w_q_b": whole("q").T[None]}
    grads = {"ab_w_kv_b": whole("kv")[None], "c_w_in": whole("c_in").T[None], **{n: tr(a) for n, a in grads_t.items()}}
    by_halves = {"ab_w_out": (("out",), False), "c_w_out": (("c_out",), False), "ffn_w_down": (("down0", "down1"), False),
                 "ffn_w_gate": (("gate0", "gate1"), True), "ffn_w_up": (("up0", "up1"), True)}

    small_parts = _unflat(small_sum, [(1, LANES)] + [g[n].shape for n in small_names])
    loss = small_parts[0][0, 0]
    for n, a in zip(small_names, small_parts[1:]):
        if n in SMALL_SHARDED:
            ax = SMALL_SHARDED[n]
            a = lax.dynamic_slice_in_dim(a, chip * w[n].shape[ax], w[n].shape[ax], axis=ax)
        grads[n] = a.reshape(w[n].shape)

    delta, new_m, new_v = {}, {}, {}
    for n in BIG:
        if n in by_halves:
            ks, transposed = by_halves[n]
            view = tr if transposed else (lambda a: a)
            out = _adamw_halves(view(w[n]), view(m[n]), view(v[n]), [half_of[k] for k in ks], [other_half[k] for k in ks],
                                name=f"adamw_{n}")
            delta[n], new_m[n], new_v[n], grads[n] = (view(a) for a in out)
        elif n in grads_t:
            out = _adamw(tr(w[n]), grads_t[n], tr(m[n]), tr(v[n]), name=f"adamw_{n}")
            delta[n], new_m[n], new_v[n] = (tr(a) for a in out)
        else:
            delta[n], new_m[n], new_v[n] = _adamw(w[n], grads[n], m[n], v[n], name=f"adamw_{n}")
    small_all = [n for n in WEIGHT_NAMES if n not in BIG]
    ra = _rows(sum(w[n].size for n in small_all), FLAT_ROWS)
    pack = lambda d: _flat([d[n] for n in small_all], ra)[None]
    out = _adamw(pack(w), pack(grads), pack(m), pack(v), name="adamw_small")
    shapes = [w[n].shape for n in small_all]
    for d, flat in zip((delta, new_m, new_v), out):
        d.update(zip(small_all, _unflat(flat, shapes)))
    return (loss, grad_x, *[grads[n] for n in WEIGHT_NAMES], *[delta[n] for n in WEIGHT_NAMES],
            *[new_m[n] for n in WEIGHT_NAMES], *[new_v[n] for n in WEIGHT_NAMES])
```
